```python
import jax, jax.numpy as jnp
from jax import lax
import numpy as np

D_MODEL = 1024
BATCH = 8
SEQ = 4096
DEPTH = 1

CHUNK = 64
EPS = 1e-6
D_FF = 2816
S5_WIDTH = D_MODEL
S5_GROUP = 16
S5_GROUPS = S5_WIDTH // S5_GROUP
S5_STATE = 64
D_INNER = 2 * D_MODEL
SSD_HEADDIM = 64
SSD_HEADS = D_INNER // SSD_HEADDIM
SSD_GROUPS = 8
SSD_HPG = SSD_HEADS // SSD_GROUPS
SSD_STATE = 128
CONV_K = 4
CONV_DIM = D_INNER + 2 * SSD_GROUPS * SSD_STATE
N_BRANCH = 2
IN_SPLITS = (S5_WIDTH, D_INNER, CONV_DIM, SSD_HEADS, N_BRANCH * D_MODEL)
D_IN = S5_WIDTH + D_INNER + CONV_DIM + SSD_HEADS + N_BRANCH * D_MODEL

kernel_name = "hybrid_s5_ssd_gated_macaron"


def rmsnorm(x, g):
    xf = x.astype(jnp.float32)
    y = xf * lax.rsqrt(jnp.mean(xf * xf, axis=-1, keepdims=True) + EPS)
    return (y * g.astype(jnp.float32)).astype(x.dtype)


def swiglu(h, w_gate, w_up, w_down):
    return (jax.nn.silu(h @ w_gate) * (h @ w_up)) @ w_down


def s5_branch(u, A_re, A_im, log_dt, B_re, B_im, C_re, C_im, d_skip, w_glu, b_glu):
    f32 = jnp.float32
    bsz, L, _ = u.shape
    uf = u.astype(f32).reshape(bsz, L, S5_GROUPS, S5_GROUP)
    dt = jnp.exp(log_dt.astype(f32))[:, None]
    lr, li = A_re.astype(f32), A_im.astype(f32)
    mag = jnp.exp(lr * dt)
    ar, ai = mag * jnp.cos(li * dt), mag * jnp.sin(li * dt)
    den = lr * lr + li * li
    cr = ((ar - 1.0) * lr + ai * li) / den
    ci = (ai * lr - (ar - 1.0) * li) / den
    Br, Bi = B_re.astype(f32), B_im.astype(f32)
    bbr = cr[..., None] * Br - ci[..., None] * Bi
    bbi = cr[..., None] * Bi + ci[..., None] * Br
    bu_r = jnp.einsum('blgm,gnm->blgn', uf, bbr)
    bu_i = jnp.einsum('blgm,gnm->blgn', uf, bbi)
    a_r = jnp.broadcast_to(ar, bu_r.shape)
    a_i = jnp.broadcast_to(ai, bu_i.shape)

    def combine(e1, e2):
        a1r, a1i, b1r, b1i = e1
        a2r, a2i, b2r, b2i = e2
        return (a2r * a1r - a2i * a1i,
                a2r * a1i + a2i * a1r,
                a2r * b1r - a2i * b1i + b2r,
                a2r * b1i + a2i * b1r + b2i)

    _, _, s_r, s_i = lax.associative_scan(combine, (a_r, a_i, bu_r, bu_i), axis=1)
    y = (jnp.einsum('blgn,gmn->blgm', s_r, C_re.astype(f32))
         - jnp.einsum('blgn,gmn->blgm', s_i, C_im.astype(f32)))
    y = y.reshape(bsz, L, S5_WIDTH) + d_skip.astype(f32) * u.astype(f32)
    g = jax.nn.gelu(y)
    out = g * jax.nn.sigmoid(g @ w_glu.astype(f32) + b_glu.astype(f32))
    return out.astype(u.dtype)


def causal_depthwise_conv(x, w, b):
    y = lax.conv_general_dilated(
        x, w[:, None, :].astype(x.dtype), window_strides=(1,),
        padding=[(CONV_K - 1, 0)], dimension_numbers=('NWC', 'WIO', 'NWC'),
        feature_group_count=CONV_DIM)
    return y + b.astype(x.dtype)


def ssd_branch(z, xbc, dt_raw, conv_w, conv_b, A_log, dt_bias, d_head, norm_w):
    f32 = jnp.float32
    bsz, L, _ = z.shape
    nc = L // CHUNK
    xbc = jax.nn.silu(causal_depthwise_conv(xbc, conv_w, conv_b))
    xs, Bm, Cm = jnp.split(xbc, [D_INNER, D_INNER + SSD_GROUPS * SSD_STATE], axis=-1)
    x = xs.reshape(bsz, nc, CHUNK, SSD_GROUPS, SSD_HPG, SSD_HEADDIM).astype(f32)
    Bm = Bm.reshape(bsz, nc, CHUNK, SSD_GROUPS, SSD_STATE).astype(f32)
    Cm = Cm.reshape(bsz, nc, CHUNK, SSD_GROUPS, SSD_STATE).astype(f32)
    dt = jax.nn.softplus(dt_raw.astype(f32) + dt_bias.astype(f32))
    dt = dt.reshape(bsz, nc, CHUNK, SSD_GROUPS, SSD_HPG)
    A = -jnp.exp(A_log.astype(f32)).reshape(SSD_GROUPS, SSD_HPG)
    a_cum = jnp.cumsum(dt * A, axis=2)
    seg = a_cum[:, :, :, None] - a_cum[:, :, None]
    mask = jnp.tril(jnp.ones((CHUNK, CHUNK), dtype=bool))[:, :, None, None]
    Lmat = jnp.exp(jnp.where(mask, seg, -jnp.inf))
    cb = jnp.einsum('bcign,bcjgn->bcijg', Cm, Bm)
    w = cb[..., None] * Lmat * dt[:, :, None]
    y_diag = jnp.einsum('bcijgk,bcjgkp->bcigkp', w, x)
    decay_to_end = jnp.exp(a_cum[:, :, -1:] - a_cum)
    xw = x * (decay_to_end * dt)[..., None]
    states = jnp.einsum('bclgn,bclgkp->bcgkpn', Bm, xw)
    chunk_decay = jnp.exp(a_cum[:, :, -1])

    def step(h, inp):
        dec, s = inp
        return dec[..., None, None] * h + s, h

    h0 = jnp.zeros((bsz, SSD_GROUPS, SSD_HPG, SSD_HEADDIM, SSD_STATE), f32)
    _, prev = lax.scan(step, h0, (jnp.moveaxis(chunk_decay, 1, 0), jnp.moveaxis(states, 1, 0)))
    prev = jnp.moveaxis(prev, 0, 1)
    y_off = jnp.einsum('bcign,bcgkpn->bcigkp', Cm, prev) * jnp.exp(a_cum)[..., None]
    y = y_diag + y_off + d_head.astype(f32).reshape(SSD_GROUPS, SSD_HPG)[:, :, None] * x
    y = y.reshape(bsz, L, D_INNER).astype(z.dtype)
    return rmsnorm(y * jax.nn.silu(z), norm_w)


def _fwd_setup_inputs(seed: int = 0) -> dict:
    key = jax.random.key(seed)
    ks = iter(jax.random.split(key, 40))
    f32 = jnp.float32

    def nrm(shape, scale):
        return jax.random.normal(next(ks), shape, f32) * scale

    def gain(shape):
        return 1.0 + nrm(shape, 0.02)

    Dp = DEPTH
    x = jax.random.normal(next(ks), (BATCH, SEQ, D_MODEL), f32)
    inp = {"x": x}
    inp["ffn1_norm"] = gain((Dp, D_MODEL))
    inp["ffn1_w_gate"] = nrm((Dp, D_MODEL, D_FF), D_MODEL ** -0.5)
    inp["ffn1_w_up"] = nrm((Dp, D_MODEL, D_FF), D_MODEL ** -0.5)
    inp["ffn1_w_down"] = nrm((Dp, D_FF, D_MODEL), D_FF ** -0.5)
    inp["mix_norm"] = gain((Dp, D_MODEL))
    inp["w_in"] = nrm((Dp, D_MODEL, D_IN), D_MODEL ** -0.5)
    inp["conv_w"] = nrm((Dp, CONV_K, CONV_DIM), CONV_K ** -0.5)
    inp["conv_b"] = nrm((Dp, CONV_DIM), 0.02)
    inp["s5_A_re"] = -0.5 + nrm((Dp, S5_GROUPS, S5_STATE), 0.01)
    inp["s5_A_im"] = jnp.pi * jnp.arange(S5_STATE, dtype=f32) + nrm((Dp, S5_GROUPS, S5_STATE), 0.01)
    inp["s5_log_dt"] = jax.random.uniform(next(ks), (Dp, S5_GROUPS), f32,
                                          np.log(0.001), np.log(0.1))
    inp["s5_B_re"] = nrm((Dp, S5_GROUPS, S5_STATE, S5_GROUP), (2 * S5_GROUP) ** -0.5)
    inp["s5_B_im"] = nrm((Dp, S5_GROUPS, S5_STATE, S5_GROUP), (2 * S5_GROUP) ** -0.5)
    inp["s5_C_re"] = nrm((Dp, S5_GROUPS, S5_GROUP, S5_STATE), S5_STATE ** -0.5)
    inp["s5_C_im"] = nrm((Dp, S5_GROUPS, S5_GROUP, S5_STATE), S5_STATE ** -0.5)
    inp["s5_D"] = nrm((Dp, S5_WIDTH), 1.0)
    inp["s5_w_glu"] = nrm((Dp, S5_WIDTH, S5_WIDTH), S5_WIDTH ** -0.5)
    inp["s5_b_glu"] = nrm((Dp, S5_WIDTH), 0.02)
    inp["ssd_A_log"] = jnp.log(jax.random.uniform(next(ks), (Dp, SSD_HEADS), f32, 1.0, 16.0))
    dt0 = jnp.exp(jax.random.uniform(next(ks), (Dp, SSD_HEADS), f32, np.log(0.001), np.log(0.1)))
    inp["ssd_dt_bias"] = dt0 + jnp.log(-jnp.expm1(-dt0))
    inp["ssd_D"] = gain((Dp, SSD_HEADS))
    inp["ssd_norm"] = gain((Dp, D_INNER))
    inp["w_proj_s5"] = nrm((Dp, S5_WIDTH, D_MODEL), S5_WIDTH ** -0.5)
    inp["w_proj_ssd"] = nrm((Dp, D_INNER, D_MODEL), D_INNER ** -0.5)
    inp["b_gate"] = nrm((Dp, N_BRANCH * D_MODEL), 0.02)
    inp["w_out"] = nrm((Dp, D_MODEL, D_MODEL), D_MODEL ** -0.5)
    inp["ffn2_norm"] = gain((Dp, D_MODEL))
    inp["ffn2_w_gate"] = nrm((Dp, D_MODEL, D_FF), D_MODEL ** -0.5)
    inp["ffn2_w_up"] = nrm((Dp, D_MODEL, D_FF), D_MODEL ** -0.5)
    inp["ffn2_w_down"] = nrm((Dp, D_FF, D_MODEL), D_FF ** -0.5)
    inp["final_norm"] = gain((D_MODEL,))
    return inp


def _fwd_reference(x, ffn1_norm, ffn1_w_gate, ffn1_w_up, ffn1_w_down, mix_norm, w_in,
              conv_w, conv_b, s5_A_re, s5_A_im, s5_log_dt, s5_B_re, s5_B_im,
              s5_C_re, s5_C_im, s5_D, s5_w_glu, s5_b_glu, ssd_A_log, ssd_dt_bias,
              ssd_D, ssd_norm, w_proj_s5, w_proj_ssd, b_gate, w_out,
              ffn2_norm, ffn2_w_gate, ffn2_w_up, ffn2_w_down, final_norm):
    split_at = [sum(IN_SPLITS[:i + 1]) for i in range(len(IN_SPLITS) - 1)]
    for i in range(DEPTH):
        x = x + 0.5 * swiglu(rmsnorm(x, ffn1_norm[i]), ffn1_w_gate[i], ffn1_w_up[i], ffn1_w_down[i])
        h = rmsnorm(x, mix_norm[i])
        proj = h @ w_in[i]
        u_s5, z, xbc, dt_raw, gate_logits = jnp.split(proj, split_at, axis=-1)
        y_s5 = s5_branch(u_s5, s5_A_re[i], s5_A_im[i], s5_log_dt[i], s5_B_re[i], s5_B_im[i],
                         s5_C_re[i], s5_C_im[i], s5_D[i], s5_w_glu[i], s5_b_glu[i])
        y_ssd = ssd_branch(z, xbc, dt_raw, conv_w[i], conv_b[i], ssd_A_log[i], ssd_dt_bias[i],
                           ssd_D[i], ssd_norm[i])
        gates = jax.nn.sigmoid(gate_logits + b_gate[i])
        g_s5, g_ssd = jnp.split(gates, 2, axis=-1)
        merged = g_s5 * (y_s5 @ w_proj_s5[i]) + g_ssd * (y_ssd @ w_proj_ssd[i])
        x = x + merged @ w_out[i]
        x = x + 0.5 * swiglu(rmsnorm(x, ffn2_norm[i]), ffn2_w_gate[i], ffn2_w_up[i], ffn2_w_down[i])
    return rmsnorm(x, final_norm)


import jax as _jax
import jax.numpy as _jnp

TWIN_FORMAT = 'train_step'
FWD_PARAMS = ['x', 'ffn1_norm', 'ffn1_w_gate', 'ffn1_w_up', 'ffn1_w_down', 'mix_norm', 'w_in', 'conv_w', 'conv_b', 's5_A_re', 's5_A_im', 's5_log_dt', 's5_B_re', 's5_B_im', 's5_C_re', 's5_C_im', 's5_D', 's5_w_glu', 's5_b_glu', 'ssd_A_log', 'ssd_dt_bias', 'ssd_D', 'ssd_norm', 'w_proj_s5', 'w_proj_ssd', 'b_gate', 'w_out', 'ffn2_norm', 'ffn2_w_gate', 'ffn2_w_up', 'ffn2_w_down', 'final_norm']
TWIN_WEIGHTS = ['ffn1_norm', 'ffn1_w_gate', 'ffn1_w_up', 'ffn1_w_down', 'mix_norm', 'w_in', 'conv_w', 'conv_b', 's5_A_re', 's5_A_im', 's5_log_dt', 's5_B_re', 's5_B_im', 's5_C_re', 's5_C_im', 's5_D', 's5_w_glu', 's5_b_glu', 'ssd_A_log', 'ssd_dt_bias', 'ssd_D', 'ssd_norm', 'w_proj_s5', 'w_proj_ssd', 'b_gate', 'w_out', 'ffn2_norm', 'ffn2_w_gate', 'ffn2_w_up', 'ffn2_w_down', 'final_norm']
TWIN_DIFF_INPUT = 'x'
TWIN_INPUTS = ['x', 'ffn1_norm', 'ffn1_w_gate', 'ffn1_w_up', 'ffn1_w_down', 'mix_norm', 'w_in', 'conv_w', 'conv_b', 's5_A_re', 's5_A_im', 's5_log_dt', 's5_B_re', 's5_B_im', 's5_C_re', 's5_C_im', 's5_D', 's5_w_glu', 's5_b_glu', 'ssd_A_log', 'ssd_dt_bias', 'ssd_D', 'ssd_norm', 'w_proj_s5', 'w_proj_ssd', 'b_gate', 'w_out', 'ffn2_norm', 'ffn2_w_gate', 'ffn2_w_up', 'ffn2_w_down', 'final_norm', 'loss_target', 'm_ffn1_norm', 'm_ffn1_w_gate', 'm_ffn1_w_up', 'm_ffn1_w_down', 'm_mix_norm', 'm_w_in', 'm_conv_w', 'm_conv_b', 'm_s5_A_re', 'm_s5_A_im', 'm_s5_log_dt', 'm_s5_B_re', 'm_s5_B_im', 'm_s5_C_re', 'm_s5_C_im', 'm_s5_D', 'm_s5_w_glu', 'm_s5_b_glu', 'm_ssd_A_log', 'm_ssd_dt_bias', 'm_ssd_D', 'm_ssd_norm', 'm_w_proj_s5', 'm_w_proj_ssd', 'm_b_gate', 'm_w_out', 'm_ffn2_norm', 'm_ffn2_w_gate', 'm_ffn2_w_up', 'm_ffn2_w_down', 'm_final_norm', 'v_ffn1_norm', 'v_ffn1_w_gate', 'v_ffn1_w_up', 'v_ffn1_w_down', 'v_mix_norm', 'v_w_in', 'v_conv_w', 'v_conv_b', 'v_s5_A_re', 'v_s5_A_im', 'v_s5_log_dt', 'v_s5_B_re', 'v_s5_B_im', 'v_s5_C_re', 'v_s5_C_im', 'v_s5_D', 'v_s5_w_glu', 'v_s5_b_glu', 'v_ssd_A_log', 'v_ssd_dt_bias', 'v_ssd_D', 'v_ssd_norm', 'v_w_proj_s5', 'v_w_proj_ssd', 'v_b_gate', 'v_w_out', 'v_ffn2_norm', 'v_ffn2_w_gate', 'v_ffn2_w_up', 'v_ffn2_w_down', 'v_final_norm']
TWIN_OUTPUTS = ['loss', 'grad_x', 'grad_ffn1_norm', 'grad_ffn1_w_gate', 'grad_ffn1_w_up', 'grad_ffn1_w_down', 'grad_mix_norm', 'grad_w_in', 'grad_conv_w', 'grad_conv_b', 'grad_s5_A_re', 'grad_s5_A_im', 'grad_s5_log_dt', 'grad_s5_B_re', 'grad_s5_B_im', 'grad_s5_C_re', 'grad_s5_C_im', 'grad_s5_D', 'grad_s5_w_glu', 'grad_s5_b_glu', 'grad_ssd_A_log', 'grad_ssd_dt_bias', 'grad_ssd_D', 'grad_ssd_norm', 'grad_w_proj_s5', 'grad_w_proj_ssd', 'grad_b_gate', 'grad_w_out', 'grad_ffn2_norm', 'grad_ffn2_w_gate', 'grad_ffn2_w_up', 'grad_ffn2_w_down', 'grad_final_norm', 'delta_ffn1_norm', 'delta_ffn1_w_gate', 'delta_ffn1_w_up', 'delta_ffn1_w_down', 'delta_mix_norm', 'delta_w_in', 'delta_conv_w', 'delta_conv_b', 'delta_s5_A_re', 'delta_s5_A_im', 'delta_s5_log_dt', 'delta_s5_B_re', 'delta_s5_B_im', 'delta_s5_C_re', 'delta_s5_C_im', 'delta_s5_D', 'delta_s5_w_glu', 'delta_s5_b_glu', 'delta_ssd_A_log', 'delta_ssd_dt_bias', 'delta_ssd_D', 'delta_ssd_norm', 'delta_w_proj_s5', 'delta_w_proj_ssd', 'delta_b_gate', 'delta_w_out', 'delta_ffn2_norm', 'delta_ffn2_w_gate', 'delta_ffn2_w_up', 'delta_ffn2_w_down', 'delta_final_norm', 'new_m_ffn1_norm', 'new_m_ffn1_w_gate', 'new_m_ffn1_w_up', 'new_m_ffn1_w_down', 'new_m_mix_norm', 'new_m_w_in', 'new_m_conv_w', 'new_m_conv_b', 'new_m_s5_A_re', 'new_m_s5_A_im', 'new_m_s5_log_dt', 'new_m_s5_B_re', 'new_m_s5_B_im', 'new_m_s5_C_re', 'new_m_s5_C_im', 'new_m_s5_D', 'new_m_s5_w_glu', 'new_m_s5_b_glu', 'new_m_ssd_A_log', 'new_m_ssd_dt_bias', 'new_m_ssd_D', 'new_m_ssd_norm', 'new_m_w_proj_s5', 'new_m_w_proj_ssd', 'new_m_b_gate', 'new_m_w_out', 'new_m_ffn2_norm', 'new_m_ffn2_w_gate', 'new_m_ffn2_w_up', 'new_m_ffn2_w_down', 'new_m_final_norm', 'new_v_ffn1_norm', 'new_v_ffn1_w_gate', 'new_v_ffn1_w_up', 'new_v_ffn1_w_down', 'new_v_mix_norm', 'new_v_w_in', 'new_v_conv_w', 'new_v_conv_b', 'new_v_s5_A_re', 'new_v_s5_A_im', 'new_v_s5_log_dt', 'new_v_s5_B_re', 'new_v_s5_B_im', 'new_v_s5_C_re', 'new_v_s5_C_im', 'new_v_s5_D', 'new_v_s5_w_glu', 'new_v_s5_b_glu', 'new_v_ssd_A_log', 'new_v_ssd_dt_bias', 'new_v_ssd_D', 'new_v_ssd_norm', 'new_v_w_proj_s5', 'new_v_w_proj_ssd', 'new_v_b_gate', 'new_v_w_out', 'new_v_ffn2_norm', 'new_v_ffn2_w_gate', 'new_v_ffn2_w_up', 'new_v_ffn2_w_down', 'new_v_final_norm']
TWIN_LEAF_KINDS = {'loss': 'loss', 'grad_x': 'grad_x', 'grad_ffn1_norm': 'grad_w', 'grad_ffn1_w_gate': 'grad_w', 'grad_ffn1_w_up': 'grad_w', 'grad_ffn1_w_down': 'grad_w', 'grad_mix_norm': 'grad_w', 'grad_w_in': 'grad_w', 'grad_conv_w': 'grad_w', 'grad_conv_b': 'grad_w', 'grad_s5_A_re': 'grad_w', 'grad_s5_A_im': 'grad_w', 'grad_s5_log_dt': 'grad_w', 'grad_s5_B_re': 'grad_w', 'grad_s5_B_im': 'grad_w', 'grad_s5_C_re': 'grad_w', 'grad_s5_C_im': 'grad_w', 'grad_s5_D': 'grad_w', 'grad_s5_w_glu': 'grad_w', 'grad_s5_b_glu': 'grad_w', 'grad_ssd_A_log': 'grad_w', 'grad_ssd_dt_bias': 'grad_w', 'grad_ssd_D': 'grad_w', 'grad_ssd_norm': 'grad_w', 'grad_w_proj_s5': 'grad_w', 'grad_w_proj_ssd': 'grad_w', 'grad_b_gate': 'grad_w', 'grad_w_out': 'grad_w', 'grad_ffn2_norm': 'grad_w', 'grad_ffn2_w_gate': 'grad_w', 'grad_ffn2_w_up': 'grad_w', 'grad_ffn2_w_down': 'grad_w', 'grad_final_norm': 'grad_w', 'delta_ffn1_norm': 'delta_w', 'delta_ffn1_w_gate': 'delta_w', 'delta_ffn1_w_up': 'delta_w', 'delta_ffn1_w_down': 'delta_w', 'delta_mix_norm': 'delta_w', 'delta_w_in': 'delta_w', 'delta_conv_w': 'delta_w', 'delta_conv_b': 'delta_w', 'delta_s5_A_re': 'delta_w', 'delta_s5_A_im': 'delta_w', 'delta_s5_log_dt': 'delta_w', 'delta_s5_B_re': 'delta_w', 'delta_s5_B_im': 'delta_w', 'delta_s5_C_re': 'delta_w', 'delta_s5_C_im': 'delta_w', 'delta_s5_D': 'delta_w', 'delta_s5_w_glu': 'delta_w', 'delta_s5_b_glu': 'delta_w', 'delta_ssd_A_log': 'delta_w', 'delta_ssd_dt_bias': 'delta_w', 'delta_ssd_D': 'delta_w', 'delta_ssd_norm': 'delta_w', 'delta_w_proj_s5': 'delta_w', 'delta_w_proj_ssd': 'delta_w', 'delta_b_gate': 'delta_w', 'delta_w_out': 'delta_w', 'delta_ffn2_norm': 'delta_w', 'delta_ffn2_w_gate': 'delta_w', 'delta_ffn2_w_up': 'delta_w', 'delta_ffn2_w_down': 'delta_w', 'delta_final_norm': 'delta_w', 'new_m_ffn1_norm': 'new_m', 'new_m_ffn1_w_gate': 'new_m', 'new_m_ffn1_w_up': 'new_m', 'new_m_ffn1_w_down': 'new_m', 'new_m_mix_norm': 'new_m', 'new_m_w_in': 'new_m', 'new_m_conv_w': 'new_m', 'new_m_conv_b': 'new_m', 'new_m_s5_A_re': 'new_m', 'new_m_s5_A_im': 'new_m', 'new_m_s5_log_dt': 'new_m', 'new_m_s5_B_re': 'new_m', 'new_m_s5_B_im': 'new_m', 'new_m_s5_C_re': 'new_m', 'new_m_s5_C_im': 'new_m', 'new_m_s5_D': 'new_m', 'new_m_s5_w_glu': 'new_m', 'new_m_s5_b_glu': 'new_m', 'new_m_ssd_A_log': 'new_m', 'new_m_ssd_dt_bias': 'new_m', 'new_m_ssd_D': 'new_m', 'new_m_ssd_norm': 'new_m', 'new_m_w_proj_s5': 'new_m', 'new_m_w_proj_ssd': 'new_m', 'new_m_b_gate': 'new_m', 'new_m_w_out': 'new_m', 'new_m_ffn2_norm': 'new_m', 'new_m_ffn2_w_gate': 'new_m', 'new_m_ffn2_w_up': 'new_m', 'new_m_ffn2_w_down': 'new_m', 'new_m_final_norm': 'new_m', 'new_v_ffn1_norm': 'new_v', 'new_v_ffn1_w_gate': 'new_v', 'new_v_ffn1_w_up': 'new_v', 'new_v_ffn1_w_down': 'new_v', 'new_v_mix_norm': 'new_v', 'new_v_w_in': 'new_v', 'new_v_conv_w': 'new_v', 'new_v_conv_b': 'new_v', 'new_v_s5_A_re': 'new_v', 'new_v_s5_A_im': 'new_v', 'new_v_s5_log_dt': 'new_v', 'new_v_s5_B_re': 'new_v', 'new_v_s5_B_im': 'new_v', 'new_v_s5_C_re': 'new_v', 'new_v_s5_C_im': 'new_v', 'new_v_s5_D': 'new_v', 'new_v_s5_w_glu': 'new_v', 'new_v_s5_b_glu': 'new_v', 'new_v_ssd_A_log': 'new_v', 'new_v_ssd_dt_bias': 'new_v', 'new_v_ssd_D': 'new_v', 'new_v_ssd_norm': 'new_v', 'new_v_w_proj_s5': 'new_v', 'new_v_w_proj_ssd': 'new_v', 'new_v_b_gate': 'new_v', 'new_v_w_out': 'new_v', 'new_v_ffn2_norm': 'new_v', 'new_v_ffn2_w_gate': 'new_v', 'new_v_ffn2_w_up': 'new_v', 'new_v_ffn2_w_down': 'new_v', 'new_v_final_norm': 'new_v'}


def _forward(args):
    return _fwd_reference(*[args[k] for k in FWD_PARAMS])


def _output_shape():
    def fwd():
        inp = _fwd_setup_inputs(0)
        return _fwd_reference(*[inp[k] for k in FWD_PARAMS])
    out = _jax.eval_shape(fwd)
    return out.shape, out.dtype

N_MICROBATCH = 1
ADAM_LR = 0.001
ADAM_B1 = 0.9
ADAM_B2 = 0.999
ADAM_EPS = 1e-08
ADAM_WD = 0.01
ADAM_STEP = 10
PER_EXAMPLE_BATCH_AXIS = {'x': 0, 'loss_target': 0}
SHARED_INPUTS = []
_WEIGHT_DTYPES = {'ffn1_norm': _jnp.float32, 'ffn1_w_gate': _jnp.float32, 'ffn1_w_up': _jnp.float32, 'ffn1_w_down': _jnp.float32, 'mix_norm': _jnp.float32, 'w_in': _jnp.float32, 'conv_w': _jnp.float32, 'conv_b': _jnp.float32, 's5_A_re': _jnp.float32, 's5_A_im': _jnp.float32, 's5_log_dt': _jnp.float32, 's5_B_re': _jnp.float32, 's5_B_im': _jnp.float32, 's5_C_re': _jnp.float32, 's5_C_im': _jnp.float32, 's5_D': _jnp.float32, 's5_w_glu': _jnp.float32, 's5_b_glu': _jnp.float32, 'ssd_A_log': _jnp.float32, 'ssd_dt_bias': _jnp.float32, 'ssd_D': _jnp.float32, 'ssd_norm': _jnp.float32, 'w_proj_s5': _jnp.float32, 'w_proj_ssd': _jnp.float32, 'b_gate': _jnp.float32, 'w_out': _jnp.float32, 'ffn2_norm': _jnp.float32, 'ffn2_w_gate': _jnp.float32, 'ffn2_w_up': _jnp.float32, 'ffn2_w_down': _jnp.float32, 'final_norm': _jnp.float32}
MOMENT_SCALE = {'ffn1_norm': 9.335074e-02, 'ffn1_w_gate': 3.713882e-02, 'ffn1_w_up': 3.598759e-02, 'ffn1_w_down': 5.967293e-02, 'mix_norm': 1.352651e-01, 'w_in': 4.470862e-02, 'conv_w': 4.488300e-02, 'conv_b': 5.914754e-02, 's5_A_re': 2.087879e-03, 's5_A_im': 2.159538e-03, 's5_log_dt': 1.817381e+00, 's5_B_re': 1.374863e-03, 's5_B_im': 1.375568e-03, 's5_C_re': 1.924918e-03, 's5_C_im': 2.005260e-03, 's5_D': 3.481238e-02, 's5_w_glu': 8.718241e-03, 's5_b_glu': 1.398298e-02, 'ssd_A_log': 2.029028e-01, 'ssd_dt_bias': 2.437683e-01, 'ssd_D': 3.140191e-01, 'ssd_norm': 6.210588e-02, 'w_proj_s5': 2.954326e-02, 'w_proj_ssd': 8.342748e-02, 'b_gate': 2.481877e-02, 'w_out': 8.864872e-02, 'ffn2_norm': 6.079060e-02, 'ffn2_w_gate': 2.678153e-02, 'ffn2_w_up': 2.593236e-02, 'ffn2_w_down': 4.318641e-02, 'final_norm': 3.199971e+01}


def _to_microbatches(a, axis):
    t = _jnp.moveaxis(a, axis, 0)
    t = t.reshape((N_MICROBATCH, t.shape[0] // N_MICROBATCH) + t.shape[1:])
    return _jnp.moveaxis(t, 1, axis + 1)


def setup_inputs(seed: int = 0) -> dict:
    inp = _fwd_setup_inputs(seed)
    key = _jax.random.fold_in(_jax.random.key(seed), 7919)
    shape, _ = _output_shape()
    out = dict(inp)
    out["loss_target"] = _jax.random.normal(_jax.random.fold_in(key, 0), shape, _jnp.float32)
    for i, name in enumerate(TWIN_WEIGHTS):
        w = inp[name].astype(_jnp.float32)
        if MOMENT_SCALE is None:
            s = _jnp.sqrt(_jnp.mean(_jnp.square(w)) + 1e-30)
        else:
            s = MOMENT_SCALE[name]
        km, kv = _jax.random.split(_jax.random.fold_in(key, i + 1))
        out[name] = w
        out["m_" + name] = s * _jax.random.normal(km, w.shape, _jnp.float32)
        out["v_" + name] = (s * s) * _jax.random.uniform(kv, w.shape, _jnp.float32, 0.5, 1.5)
    if N_MICROBATCH > 1:
        for name, axis in PER_EXAMPLE_BATCH_AXIS.items():
            out[name] = _to_microbatches(out[name], axis)
    return {'x': out['x'], 'ffn1_norm': out['ffn1_norm'], 'ffn1_w_gate': out['ffn1_w_gate'], 'ffn1_w_up': out['ffn1_w_up'], 'ffn1_w_down': out['ffn1_w_down'], 'mix_norm': out['mix_norm'], 'w_in': out['w_in'], 'conv_w': out['conv_w'], 'conv_b': out['conv_b'], 's5_A_re': out['s5_A_re'], 's5_A_im': out['s5_A_im'], 's5_log_dt': out['s5_log_dt'], 's5_B_re': out['s5_B_re'], 's5_B_im': out['s5_B_im'], 's5_C_re': out['s5_C_re'], 's5_C_im': out['s5_C_im'], 's5_D': out['s5_D'], 's5_w_glu': out['s5_w_glu'], 's5_b_glu': out['s5_b_glu'], 'ssd_A_log': out['ssd_A_log'], 'ssd_dt_bias': out['ssd_dt_bias'], 'ssd_D': out['ssd_D'], 'ssd_norm': out['ssd_norm'], 'w_proj_s5': out['w_proj_s5'], 'w_proj_ssd': out['w_proj_ssd'], 'b_gate': out['b_gate'], 'w_out': out['w_out'], 'ffn2_norm': out['ffn2_norm'], 'ffn2_w_gate': out['ffn2_w_gate'], 'ffn2_w_up': out['ffn2_w_up'], 'ffn2_w_down': out['ffn2_w_down'], 'final_norm': out['final_norm'], 'loss_target': out['loss_target'], 'm_ffn1_norm': out['m_ffn1_norm'], 'm_ffn1_w_gate': out['m_ffn1_w_gate'], 'm_ffn1_w_up': out['m_ffn1_w_up'], 'm_ffn1_w_down': out['m_ffn1_w_down'], 'm_mix_norm': out['m_mix_norm'], 'm_w_in': out['m_w_in'], 'm_conv_w': out['m_conv_w'], 'm_conv_b': out['m_conv_b'], 'm_s5_A_re': out['m_s5_A_re'], 'm_s5_A_im': out['m_s5_A_im'], 'm_s5_log_dt': out['m_s5_log_dt'], 'm_s5_B_re': out['m_s5_B_re'], 'm_s5_B_im': out['m_s5_B_im'], 'm_s5_C_re': out['m_s5_C_re'], 'm_s5_C_im': out['m_s5_C_im'], 'm_s5_D': out['m_s5_D'], 'm_s5_w_glu': out['m_s5_w_glu'], 'm_s5_b_glu': out['m_s5_b_glu'], 'm_ssd_A_log': out['m_ssd_A_log'], 'm_ssd_dt_bias': out['m_ssd_dt_bias'], 'm_ssd_D': out['m_ssd_D'], 'm_ssd_norm': out['m_ssd_norm'], 'm_w_proj_s5': out['m_w_proj_s5'], 'm_w_proj_ssd': out['m_w_proj_ssd'], 'm_b_gate': out['m_b_gate'], 'm_w_out': out['m_w_out'], 'm_ffn2_norm': out['m_ffn2_norm'], 'm_ffn2_w_gate': out['m_ffn2_w_gate'], 'm_ffn2_w_up': out['m_ffn2_w_up'], 'm_ffn2_w_down': out['m_ffn2_w_down'], 'm_final_norm': out['m_final_norm'], 'v_ffn1_norm': out['v_ffn1_norm'], 'v_ffn1_w_gate': out['v_ffn1_w_gate'], 'v_ffn1_w_up': out['v_ffn1_w_up'], 'v_ffn1_w_down': out['v_ffn1_w_down'], 'v_mix_norm': out['v_mix_norm'], 'v_w_in': out['v_w_in'], 'v_conv_w': out['v_conv_w'], 'v_conv_b': out['v_conv_b'], 'v_s5_A_re': out['v_s5_A_re'], 'v_s5_A_im': out['v_s5_A_im'], 'v_s5_log_dt': out['v_s5_log_dt'], 'v_s5_B_re': out['v_s5_B_re'], 'v_s5_B_im': out['v_s5_B_im'], 'v_s5_C_re': out['v_s5_C_re'], 'v_s5_C_im': out['v_s5_C_im'], 'v_s5_D': out['v_s5_D'], 'v_s5_w_glu': out['v_s5_w_glu'], 'v_s5_b_glu': out['v_s5_b_glu'], 'v_ssd_A_log': out['v_ssd_A_log'], 'v_ssd_dt_bias': out['v_ssd_dt_bias'], 'v_ssd_D': out['v_ssd_D'], 'v_ssd_norm': out['v_ssd_norm'], 'v_w_proj_s5': out['v_w_proj_s5'], 'v_w_proj_ssd': out['v_w_proj_ssd'], 'v_b_gate': out['v_b_gate'], 'v_w_out': out['v_w_out'], 'v_ffn2_norm': out['v_ffn2_norm'], 'v_ffn2_w_gate': out['v_ffn2_w_gate'], 'v_ffn2_w_up': out['v_ffn2_w_up'], 'v_ffn2_w_down': out['v_ffn2_w_down'], 'v_final_norm': out['v_final_norm']}


def _loss(weights, diff, rest, loss_target):
    with _jax.named_scope("forward"):
        args = {**rest, TWIN_DIFF_INPUT: diff, **{k: w.astype(_WEIGHT_DTYPES[k]) for k, w in weights.items()}}
        y = _forward(args)
    with _jax.named_scope("loss_head"):
        err = _jnp.square(y.astype(_jnp.float32) - loss_target)
        return 0.5 * _jnp.sum(_jnp.mean(err, axis=-1)) if err.ndim else 0.5 * err


def _adamw(w, g, m, v):
    m = ADAM_B1 * m + (1.0 - ADAM_B1) * g
    v = ADAM_B2 * v + (1.0 - ADAM_B2) * _jnp.square(g)
    m_hat = m / (1.0 - ADAM_B1 ** ADAM_STEP)
    v_hat = v / (1.0 - ADAM_B2 ** ADAM_STEP)
    delta = -ADAM_LR * (m_hat / (_jnp.sqrt(v_hat) + ADAM_EPS) + ADAM_WD * w)
    return delta, m, v


def reference(x, ffn1_norm, ffn1_w_gate, ffn1_w_up, ffn1_w_down, mix_norm, w_in, conv_w, conv_b, s5_A_re, s5_A_im, s5_log_dt, s5_B_re, s5_B_im, s5_C_re, s5_C_im, s5_D, s5_w_glu, s5_b_glu, ssd_A_log, ssd_dt_bias, ssd_D, ssd_norm, w_proj_s5, w_proj_ssd, b_gate, w_out, ffn2_norm, ffn2_w_gate, ffn2_w_up, ffn2_w_down, final_norm, loss_target, m_ffn1_norm, m_ffn1_w_gate, m_ffn1_w_up, m_ffn1_w_down, m_mix_norm, m_w_in, m_conv_w, m_conv_b, m_s5_A_re, m_s5_A_im, m_s5_log_dt, m_s5_B_re, m_s5_B_im, m_s5_C_re, m_s5_C_im, m_s5_D, m_s5_w_glu, m_s5_b_glu, m_ssd_A_log, m_ssd_dt_bias, m_ssd_D, m_ssd_norm, m_w_proj_s5, m_w_proj_ssd, m_b_gate, m_w_out, m_ffn2_norm, m_ffn2_w_gate, m_ffn2_w_up, m_ffn2_w_down, m_final_norm, v_ffn1_norm, v_ffn1_w_gate, v_ffn1_w_up, v_ffn1_w_down, v_mix_norm, v_w_in, v_conv_w, v_conv_b, v_s5_A_re, v_s5_A_im, v_s5_log_dt, v_s5_B_re, v_s5_B_im, v_s5_C_re, v_s5_C_im, v_s5_D, v_s5_w_glu, v_s5_b_glu, v_ssd_A_log, v_ssd_dt_bias, v_ssd_D, v_ssd_norm, v_w_proj_s5, v_w_proj_ssd, v_b_gate, v_w_out, v_ffn2_norm, v_ffn2_w_gate, v_ffn2_w_up, v_ffn2_w_down, v_final_norm):
    given = dict(x=x, ffn1_norm=ffn1_norm, ffn1_w_gate=ffn1_w_gate, ffn1_w_up=ffn1_w_up, ffn1_w_down=ffn1_w_down, mix_norm=mix_norm, w_in=w_in, conv_w=conv_w, conv_b=conv_b, s5_A_re=s5_A_re, s5_A_im=s5_A_im, s5_log_dt=s5_log_dt, s5_B_re=s5_B_re, s5_B_im=s5_B_im, s5_C_re=s5_C_re, s5_C_im=s5_C_im, s5_D=s5_D, s5_w_glu=s5_w_glu, s5_b_glu=s5_b_glu, ssd_A_log=ssd_A_log, ssd_dt_bias=ssd_dt_bias, ssd_D=ssd_D, ssd_norm=ssd_norm, w_proj_s5=w_proj_s5, w_proj_ssd=w_proj_ssd, b_gate=b_gate, w_out=w_out, ffn2_norm=ffn2_norm, ffn2_w_gate=ffn2_w_gate, ffn2_w_up=ffn2_w_up, ffn2_w_down=ffn2_w_down, final_norm=final_norm, loss_target=loss_target, m_ffn1_norm=m_ffn1_norm, m_ffn1_w_gate=m_ffn1_w_gate, m_ffn1_w_up=m_ffn1_w_up, m_ffn1_w_down=m_ffn1_w_down, m_mix_norm=m_mix_norm, m_w_in=m_w_in, m_conv_w=m_conv_w, m_conv_b=m_conv_b, m_s5_A_re=m_s5_A_re, m_s5_A_im=m_s5_A_im, m_s5_log_dt=m_s5_log_dt, m_s5_B_re=m_s5_B_re, m_s5_B_im=m_s5_B_im, m_s5_C_re=m_s5_C_re, m_s5_C_im=m_s5_C_im, m_s5_D=m_s5_D, m_s5_w_glu=m_s5_w_glu, m_s5_b_glu=m_s5_b_glu, m_ssd_A_log=m_ssd_A_log, m_ssd_dt_bias=m_ssd_dt_bias, m_ssd_D=m_ssd_D, m_ssd_norm=m_ssd_norm, m_w_proj_s5=m_w_proj_s5, m_w_proj_ssd=m_w_proj_ssd, m_b_gate=m_b_gate, m_w_out=m_w_out, m_ffn2_norm=m_ffn2_norm, m_ffn2_w_gate=m_ffn2_w_gate, m_ffn2_w_up=m_ffn2_w_up, m_ffn2_w_down=m_ffn2_w_down, m_final_norm=m_final_norm, v_ffn1_norm=v_ffn1_norm, v_ffn1_w_gate=v_ffn1_w_gate, v_ffn1_w_up=v_ffn1_w_up, v_ffn1_w_down=v_ffn1_w_down, v_mix_norm=v_mix_norm, v_w_in=v_w_in, v_conv_w=v_conv_w, v_conv_b=v_conv_b, v_s5_A_re=v_s5_A_re, v_s5_A_im=v_s5_A_im, v_s5_log_dt=v_s5_log_dt, v_s5_B_re=v_s5_B_re, v_s5_B_im=v_s5_B_im, v_s5_C_re=v_s5_C_re, v_s5_C_im=v_s5_C_im, v_s5_D=v_s5_D, v_s5_w_glu=v_s5_w_glu, v_s5_b_glu=v_s5_b_glu, v_ssd_A_log=v_ssd_A_log, v_ssd_dt_bias=v_ssd_dt_bias, v_ssd_D=v_ssd_D, v_ssd_norm=v_ssd_norm, v_w_proj_s5=v_w_proj_s5, v_w_proj_ssd=v_w_proj_ssd, v_b_gate=v_b_gate, v_w_out=v_w_out, v_ffn2_norm=v_ffn2_norm, v_ffn2_w_gate=v_ffn2_w_gate, v_ffn2_w_up=v_ffn2_w_up, v_ffn2_w_down=v_ffn2_w_down, v_final_norm=v_final_norm)
    weights = {n: given[n] for n in TWIN_WEIGHTS}
    shared = {n: given[n] for n in SHARED_INPUTS}
    per_example = {n: given[n] for n in ['x']}
    grad_fn = _jax.value_and_grad(_loss, argnums=(0, 1))

    def one_microbatch(ex, loss_target):
        ex = dict(ex)
        diff = ex.pop(TWIN_DIFF_INPUT)
        return grad_fn(weights, diff, {**shared, **ex}, loss_target)

    if N_MICROBATCH == 1:
        loss, (grad_w, grad_x) = one_microbatch(per_example, given["loss_target"])
    else:
        def body(carry, xs):
            loss_sum, grad_sum = carry
            l_k, (gw_k, gx_k) = one_microbatch(xs[0], xs[1])
            with _jax.named_scope("update"):
                return (loss_sum + l_k, _jax.tree.map(_jnp.add, grad_sum, gw_k)), gx_k

        init = (_jnp.zeros((), _jnp.float32), _jax.tree.map(_jnp.zeros_like, weights))
        (loss, grad_w), grad_x = _jax.lax.scan(body, init, (per_example, given["loss_target"]))
    with _jax.named_scope("update"):
        delta_w, new_m, new_v = {}, {}, {}
        for n in TWIN_WEIGHTS:
            delta_w[n], new_m[n], new_v[n] = _adamw(weights[n], grad_w[n], given["m_" + n], given["v_" + n])
    return (loss, grad_x, *[grad_w[n] for n in TWIN_WEIGHTS], *[delta_w[n] for n in TWIN_WEIGHTS],
            *[new_m[n] for n in TWIN_WEIGHTS], *[new_v[n] for n in TWIN_WEIGHTS])
```

```python
import math

import jax
import jax.numpy as jnp
from jax import lax
from jax.experimental import pallas as pl
from jax.experimental.pallas import tpu as pltpu

f32 = jnp.float32
bf16 = jnp.bfloat16
_S = jax.ShapeDtypeStruct

EPS = 1e-6
S5_GROUP = 16
S5_STATE = 64
HEADDIM = 64
SSD_STATE = 128
CHUNK = 64
CONV_K = 4
NSEG = 8
S5_GPB = 8
N_DEV = 8
LANES = 128
TILE_ELEMS = 8 * LANES

ADAM_LR = 0.001
ADAM_B1 = 0.9
ADAM_B2 = 0.999
ADAM_EPS = 1e-08
ADAM_WD = 0.01
ADAM_STEP = 10

VMEM_LIMIT = 56 * 1024 * 1024
MM_FULL_K = 3072
MESH = pl.DeviceIdType.MESH


def _cparams(sem=None):
    return pltpu.CompilerParams(dimension_semantics=sem, vmem_limit_bytes=VMEM_LIMIT)


def _pick(dim, pref, align=LANES):
    best = None
    t = align
    while t <= min(dim, pref):
        if dim % t == 0:
            best = t
        t += align
    return best or dim


def _mm(a, b, *, name, ta=False, tb=False, out_dtype=f32, scale=1.0, add=None):
    M, K = (a.shape[1], a.shape[0]) if ta else a.shape
    N = b.shape[0] if tb else b.shape[1]
    assert (b.shape[1] if tb else b.shape[0]) == K, (a.shape, b.shape, ta, tb)
    tm, tn = _pick(M, 512), _pick(N, 1024)
    tk = K if K <= MM_FULL_K else _pick(K, 1024)
    nk = K // tk
    a_spec = pl.BlockSpec((tk, tm), lambda i, j, k: (k, i)) if ta else pl.BlockSpec((tm, tk), lambda i, j, k: (i, k))
    b_spec = pl.BlockSpec((tn, tk), lambda i, j, k: (j, k)) if tb else pl.BlockSpec((tk, tn), lambda i, j, k: (k, j))
    o_spec = pl.BlockSpec((tm, tn), lambda i, j, k: (i, j))
    dims = (((0 if ta else 1,), (1 if tb else 0,)), ((), ()))
    has_add = add is not None

    def body(*refs):
        a_ref, b_ref = refs[0], refs[1]
        add_ref = refs[2] if has_add else None
        o_ref, acc_ref = refs[-2], refs[-1]
        k = pl.program_id(2)

        @pl.when(k == 0)
        def _():
            acc_ref[...] = jnp.zeros_like(acc_ref)

        acc_ref[...] += lax.dot_general(a_ref[...].astype(bf16), b_ref[...].astype(bf16), dims, preferred_element_type=f32)

        @pl.when(k == nk - 1)
        def _():
            r = acc_ref[...] * scale
            if has_add:
                r = r + add_ref[...].astype(f32)
            o_ref[...] = r.astype(out_dtype)

    ins = [a, b] + ([add] if has_add else [])
    in_specs = [a_spec, b_spec] + ([o_spec] if has_add else [])
    return pl.pallas_call(
        body, name=name, grid=(M // tm, N // tn, nk), in_specs=in_specs, out_specs=o_spec,
        out_shape=_S((M, N), out_dtype), scratch_shapes=[pltpu.VMEM((tm, tn), f32)],
        compiler_params=_cparams(("parallel", "parallel", "arbitrary")),
    )(*ins)


def _row_tile(T, widths):
    budget = 6 * 1024 * 1024
    tb = max(8, budget // (4 * sum(widths)))
    return _pick(T, tb, align=8)


def _rows(fn, rows, params, outs, *, name):
    T = rows[0].shape[0]
    nr, npar = len(rows), len(params)
    tb = _row_tile(T, [r.shape[1] for r in rows] + [w for w, _ in outs])

    def body(*refs):
        ins = [r[...].astype(f32) for r in refs[: nr + npar]]
        res = fn(*ins)
        for o_ref, r in zip(refs[nr + npar:], res):
            o_ref[...] = r.astype(o_ref.dtype)

    in_specs = [pl.BlockSpec((tb, r.shape[1]), lambda i: (i, 0)) for r in rows]
    in_specs += [pl.BlockSpec(p.shape, lambda i: (0, 0)) for p in params]
    out_specs = [pl.BlockSpec((tb, w), lambda i: (i, 0)) for w, _ in outs]
    res = pl.pallas_call(
        body, name=name, grid=(T // tb,), in_specs=in_specs, out_specs=out_specs,
        out_shape=[_S((T, w), d) for w, d in outs], compiler_params=_cparams(("parallel",)),
    )(*rows, *params)
    return tuple(res)


def _rows_bwd(fn, rows, params, cots, *, name, want_rows, row_dtypes=None, adds=None):
    T = rows[0].shape[0]
    nr, npar, nc = len(rows), len(params), len(cots)
    adds = adds or {}
    add_idx = sorted(adds)
    row_dtypes = row_dtypes or {}
    widths = [r.shape[1] for r in rows] + [c.shape[1] for c in cots] + [rows[i].shape[1] for i in want_rows]
    tb = _row_tile(T, widths)

    def body(*refs):
        ins = [r[...].astype(f32) for r in refs[: nr + npar]]
        cot = tuple(r[...].astype(f32) for r in refs[nr + npar: nr + npar + nc])
        add_refs = refs[nr + npar + nc: nr + npar + nc + len(add_idx)]
        out_refs = refs[nr + npar + nc + len(add_idx):]
        _, vjp = jax.vjp(lambda *a: tuple(fn(*a)), *ins)
        g = vjp(cot)
        for o_ref, i in zip(out_refs[: len(want_rows)], want_rows):
            r = g[i]
            if i in adds:
                r = r + add_refs[add_idx.index(i)][...].astype(f32)
            o_ref[...] = r.astype(o_ref.dtype)
        first = pl.program_id(0) == 0
        for o_ref, gp in zip(out_refs[len(want_rows):], g[nr:]):
            @pl.when(first)
            def _(o_ref=o_ref):
                o_ref[...] = jnp.zeros_like(o_ref)

            o_ref[...] += gp

    in_specs = [pl.BlockSpec((tb, r.shape[1]), lambda i: (i, 0)) for r in rows]
    in_specs += [pl.BlockSpec(p.shape, lambda i: (0, 0)) for p in params]
    in_specs += [pl.BlockSpec((tb, c.shape[1]), lambda i: (i, 0)) for c in cots]
    in_specs += [pl.BlockSpec((tb, adds[i].shape[1]), lambda i_: (i_, 0)) for i in add_idx]
    out_specs = [pl.BlockSpec((tb, rows[i].shape[1]), lambda i_: (i_, 0)) for i in want_rows]
    out_specs += [pl.BlockSpec(p.shape, lambda i: (0, 0)) for p in params]
    out_shape = [_S(rows[i].shape, row_dtypes.get(i, f32)) for i in want_rows] + [_S(p.shape, f32) for p in params]
    res = pl.pallas_call(
        body, name=name, grid=(T // tb,), in_specs=in_specs, out_specs=out_specs, out_shape=out_shape,
        compiler_params=_cparams(("arbitrary",)),
    )(*rows, *params, *cots, *[adds[i] for i in add_idx])
    return list(res[: len(want_rows)]), list(res[len(want_rows):])


def _f_rmsnorm(x, g):
    return (x * lax.rsqrt(jnp.mean(x * x, axis=-1, keepdims=True) + EPS) * g,)


def _f_swiglu(ab):
    F = ab.shape[1] // 2
    return (jax.nn.silu(ab[:, :F]) * ab[:, F:],)


def _f_s5_post(y, u, d):
    return (jax.nn.gelu(y + d * u),)


def _f_glu(g, v, b):
    return (g * jax.nn.sigmoid(v + b),)


def _f_gated_norm(y, z, w):
    return _f_rmsnorm(y * jax.nn.silu(z), w)


def _f_merge(gl, p5, pssd, b):
    D = p5.shape[1]
    gates = jax.nn.sigmoid(gl + b)
    return (gates[:, :D] * p5 + gates[:, D:] * pssd,)


def _f_dt(dtr, bias, a_log):
    dt = jax.nn.softplus(dtr + bias)
    return dt, dt * (-jnp.exp(a_log))


def _loss_stage(x, tgt, g, *, name):
    T, D = x.shape
    tb = _row_tile(T, [D, D, D])

    def f(xb, gb, tb_):
        y = _f_rmsnorm(xb, gb)[0]
        return 0.5 * jnp.sum(jnp.mean(jnp.square(y - tb_), axis=-1, keepdims=True), axis=0, keepdims=True)

    def body(x_ref, t_ref, g_ref, l_ref, dx_ref, dg_ref):
        tv = t_ref[...]
        val, vjp = jax.vjp(lambda a, b: f(a, b, tv), x_ref[...], g_ref[...])
        dx, dg = vjp(jnp.ones((1, 1), f32))
        dx_ref[...] = dx

        @pl.when(pl.program_id(0) == 0)
        def _():
            l_ref[...] = jnp.zeros_like(l_ref)
            dg_ref[...] = jnp.zeros_like(dg_ref)

        l_ref[...] += jnp.broadcast_to(val, l_ref.shape)
        dg_ref[...] += dg

    row = pl.BlockSpec((tb, D), lambda i: (i, 0))
    par = pl.BlockSpec((1, D), lambda i: (0, 0))
    return pl.pallas_call(
        body, name=name, grid=(T // tb,), in_specs=[row, row, par],
        out_specs=[pl.BlockSpec((1, LANES), lambda i: (0, 0)), row, par],
        out_shape=[_S((1, LANES), f32), _S((T, D), f32), _S((1, D), f32)], compiler_params=_cparams(("arbitrary",)),
    )(x, tgt, g)


def _shift_down(x, s):
    if s == 0:
        return x
    t = lax.broadcasted_iota(jnp.int32, x.shape, 0)
    return jnp.where(t >= s, pltpu.roll(x, s, axis=0), 0.0)


def _shift_up(x, s):
    if s == 0:
        return x
    T = x.shape[0]
    t = lax.broadcasted_iota(jnp.int32, x.shape, 0)
    return jnp.where(t < T - s, pltpu.roll(x, T - s, axis=0), 0.0)


def _conv_pre(x, w, b):
    pre = b
    for k in range(CONV_K):
        pre = pre + w[k:k + 1, :] * _shift_down(x, CONV_K - 1 - k)
    return pre


def _conv_fwd(x, w, b, *, name):
    T, C = x.shape
    cb = _pick(C, 256)

    def body(x_ref, w_ref, b_ref, o_ref):
        o_ref[...] = jax.nn.silu(_conv_pre(x_ref[...], w_ref[...], b_ref[...]))

    col = pl.BlockSpec((T, cb), lambda j: (0, j))
    return pl.pallas_call(
        body, name=name, grid=(C // cb,), in_specs=[col, pl.BlockSpec((CONV_K, cb), lambda j: (0, j)), pl.BlockSpec((1, cb), lambda j: (0, j))],
        out_specs=col, out_shape=_S((T, C), f32), compiler_params=_cparams(("parallel",)),
    )(x, w, b)


def _conv_bwd(x, w, b, dy, *, name):
    T, C = x.shape
    cb = _pick(C, 256)

    def body(x_ref, w_ref, b_ref, dy_ref, dx_ref, dw_ref, db_ref):
        xv, wv = x_ref[...], w_ref[...]
        pre = _conv_pre(xv, wv, b_ref[...])
        sg = jax.nn.sigmoid(pre)
        dpre = dy_ref[...] * sg * (1.0 + pre * (1.0 - sg))
        dx = jnp.zeros_like(xv)
        for k in range(CONV_K):
            s = CONV_K - 1 - k
            dx = dx + wv[k:k + 1, :] * _shift_up(dpre, s)
            dw_ref[k:k + 1, :] = jnp.sum(dpre * _shift_down(xv, s), axis=0, keepdims=True)
        dx_ref[...] = dx
        db_ref[...] = jnp.sum(dpre, axis=0, keepdims=True)

    col = pl.BlockSpec((T, cb), lambda j: (0, j))
    wsp = pl.BlockSpec((CONV_K, cb), lambda j: (0, j))
    bsp = pl.BlockSpec((1, cb), lambda j: (0, j))
    return pl.pallas_call(
        body, name=name, grid=(C // cb,), in_specs=[col, wsp, bsp, col], out_specs=[col, wsp, bsp],
        out_shape=[_S((T, C), f32), _S((CONV_K, C), f32), _S((1, C), f32)], compiler_params=_cparams(("parallel",)),
    )(x, w, b, dy)


def _f_s5_prep(lr, li, ldt, lrb, lib, ldtb, brt, bit):
    def disc(lr_, li_, ldt_):
        dt = jnp.exp(ldt_)
        mag = jnp.exp(lr_ * dt)
        ar, ai = mag * jnp.cos(li_ * dt), mag * jnp.sin(li_ * dt)
        den = lr_ * lr_ + li_ * li_
        cr = ((ar - 1.0) * lr_ + ai * li_) / den
        ci = (ai * lr_ - (ar - 1.0) * li_) / den
        return ar, ai, cr, ci

    ar, ai, _, _ = disc(lr, li, ldt)
    _, _, cr, ci = disc(lrb, lib, ldtb)
    return ar, ai, cr * brt - ci * bit, cr * bit + ci * brt


def _s5_prep(args, *, name):
    G, N = args[0].shape
    GM = args[3].shape[0]

    def body(*refs):
        res = _f_s5_prep(*[r[...] for r in refs[:8]])
        for o, r in zip(refs[8:], res):
            o[...] = r

    return pl.pallas_call(body, name=name, out_shape=[_S((G, N), f32)] * 2 + [_S((GM, N), f32)] * 2)(*args)


def _s5_prep_bwd(args, cots, rsum, *, name):
    G, N = args[0].shape
    GM = args[3].shape[0]

    def body(*refs):
        ins = [r[...] for r in refs[:8]]
        cot = tuple(r[...] for r in refs[8:12])
        rs = refs[12][...]
        _, vjp = jax.vjp(_f_s5_prep, *ins)
        g = vjp(cot)
        fold = lambda v: jnp.dot(rs, v, preferred_element_type=f32, precision=lax.Precision.HIGHEST)
        o = refs[13:]
        o[0][...] = g[0] + fold(g[3])
        o[1][...] = g[1] + fold(g[4])
        o[2][...] = g[2] + fold(jnp.broadcast_to(g[5], (GM, LANES)))[:, 0:1]
        o[3][...] = g[6]
        o[4][...] = g[7]

    return pl.pallas_call(
        body, name=name, out_shape=[_S((G, N), f32), _S((G, N), f32), _S((G, 1), f32), _S((GM, N), f32), _S((GM, N), f32)],
    )(*args, *cots, rsum)


S5_TC = 512


def _s5_local_scan(src, w_r, w_i, a_r, a_i, *, reverse, name):
    T, C = src.shape
    nblk, cb, sb = w_r.shape
    NS = nblk * sb
    tc = min(S5_TC, T)
    nT, nt = T // tc, tc // NSEG
    tmap = (lambda i: nT - 1 - i) if reverse else (lambda i: i)

    def body(u_ref, wr_ref, wi_ref, ar_ref, ai_ref, sr_ref, si_ref, pr_ref, pi_ref, st_r, st_i, pw_r, pw_i):
        @pl.when(pl.program_id(1) == 0)
        def _():
            st_r[...] = jnp.zeros_like(st_r)
            st_i[...] = jnp.zeros_like(st_i)
            pw_r[...] = jnp.ones_like(pw_r)
            pw_i[...] = jnp.zeros_like(pw_i)

        u = u_ref[...].astype(bf16)
        sr_ref[...] = jnp.dot(u, wr_ref[...], preferred_element_type=f32)
        si_ref[...] = jnp.dot(u, wi_ref[...], preferred_element_type=f32)
        ar = jnp.broadcast_to(ar_ref[...], (NSEG, sb))
        ai = jnp.broadcast_to(ai_ref[...], (NSEG, sb))

        def step(k, c):
            cr, ci, qr, qi = c
            kk = (nt - 1 - k) if reverse else k
            rows = pl.ds(pl.multiple_of(kk * NSEG, NSEG), NSEG)
            nr = ar * cr - ai * ci + sr_ref[rows, :]
            ni = ar * ci + ai * cr + si_ref[rows, :]
            sr_ref[rows, :] = nr
            si_ref[rows, :] = ni
            return nr, ni, ar * qr - ai * qi, ar * qi + ai * qr

        cr, ci, qr, qi = lax.fori_loop(0, nt, step, (st_r[...], st_i[...], pw_r[...], pw_i[...]), unroll=8)
        st_r[...], st_i[...], pw_r[...], pw_i[...] = cr, ci, qr, qi
        pr_ref[...] = qr
        pi_ref[...] = qi

    blk = pl.BlockSpec((tc, sb), lambda j, i: (tmap(i), j))
    wsp = pl.BlockSpec((None, cb, sb), lambda j, i: (j, 0, 0))
    asp = pl.BlockSpec((1, sb), lambda j, i: (0, j))
    psp = pl.BlockSpec((NSEG, sb), lambda j, i: (0, j))
    return pl.pallas_call(
        body, name=name, grid=(nblk, nT), in_specs=[pl.BlockSpec((tc, cb), lambda j, i: (tmap(i), j)), wsp, wsp, asp, asp],
        out_specs=[blk, blk, psp, psp], out_shape=[_S((T, NS), f32)] * 2 + [_S((NSEG, NS), f32)] * 2,
        scratch_shapes=[pltpu.VMEM((NSEG, sb), f32)] * 4, compiler_params=_cparams(("parallel", "arbitrary")),
    )(src, w_r, w_i, a_r, a_i)


def _s5_carry(e_r, e_i, p_r, p_i, *, reverse, name):
    NS = e_r.shape[1]

    def body(er_ref, ei_ref, pr_ref, pi_ref, cr_ref, ci_ref):
        ar, ai = pr_ref[0:1, :], pi_ref[0:1, :]
        cr = jnp.zeros((1, NS), f32)
        ci = jnp.zeros((1, NS), f32)
        order = list(range(NSEG - 1, -1, -1)) if reverse else list(range(NSEG))
        cr_ref[order[0]:order[0] + 1, :] = cr
        ci_ref[order[0]:order[0] + 1, :] = ci
        for prev, q in zip(order[:-1], order[1:]):
            er, ei = er_ref[prev:prev + 1, :], ei_ref[prev:prev + 1, :]
            cr, ci = er + ar * cr - ai * ci, ei + ar * ci + ai * cr
            cr_ref[q:q + 1, :] = cr
            ci_ref[q:q + 1, :] = ci

    return pl.pallas_call(body, name=name, out_shape=[_S((NSEG, NS), f32)] * 2)(e_r, e_i, p_r, p_i)


def _s5_fix_out(sl_r, sl_i, a_r, a_i, c_r, c_i, wc_r, wc_i, *, name):
    T, NS = sl_r.shape
    nblk, sb, cb = wc_r.shape
    tc = min(S5_TC, T)
    nT, nt = T // tc, tc // NSEG

    def body(lr_ref, li_ref, ar_ref, ai_ref, cr_ref, ci_ref, wr_ref, wi_ref, sr_ref, si_ref, y_ref, pw_r, pw_i):
        @pl.when(pl.program_id(1) == 0)
        def _():
            pw_r[...] = jnp.ones_like(pw_r)
            pw_i[...] = jnp.zeros_like(pw_i)

        ar = jnp.broadcast_to(ar_ref[...], (NSEG, sb))
        ai = jnp.broadcast_to(ai_ref[...], (NSEG, sb))
        cr, ci = cr_ref[...], ci_ref[...]

        def step(k, c):
            qr, qi = c
            qr, qi = ar * qr - ai * qi, ar * qi + ai * qr
            rows = pl.ds(pl.multiple_of(k * NSEG, NSEG), NSEG)
            sr_ref[rows, :] = lr_ref[rows, :] + qr * cr - qi * ci
            si_ref[rows, :] = li_ref[rows, :] + qr * ci + qi * cr
            return qr, qi

        qr, qi = lax.fori_loop(0, nt, step, (pw_r[...], pw_i[...]), unroll=8)
        pw_r[...], pw_i[...] = qr, qi
        y_ref[...] = (jnp.dot(sr_ref[...].astype(bf16), wr_ref[...], preferred_element_type=f32)
                      - jnp.dot(si_ref[...].astype(bf16), wi_ref[...], preferred_element_type=f32))

    blk = pl.BlockSpec((tc, sb), lambda j, i: (i, j))
    asp = pl.BlockSpec((1, sb), lambda j, i: (0, j))
    csp = pl.BlockSpec((NSEG, sb), lambda j, i: (0, j))
    wsp = pl.BlockSpec((None, sb, cb), lambda j, i: (j, 0, 0))
    return pl.pallas_call(
        body, name=name, grid=(nblk, nT), in_specs=[blk, blk, asp, asp, csp, csp, wsp, wsp],
        out_specs=[blk, blk, pl.BlockSpec((tc, cb), lambda j, i: (i, j))],
        out_shape=[_S((T, NS), f32)] * 2 + [_S((T, nblk * cb), f32)],
        scratch_shapes=[pltpu.VMEM((NSEG, sb), f32)] * 2, compiler_params=_cparams(("parallel", "arbitrary")),
    )(sl_r, sl_i, a_r, a_i, c_r, c_i, wc_r, wc_i)


def _s5_fix_bwd(ql_r, ql_i, ab_r, ab_i, c_r, c_i, s_r, s_i, sb_r, sb_i, u, dy, du_add, w_r, w_i, *, name):
    T, NS = ql_r.shape
    nblk, cb, sb = w_r.shape
    tc = min(S5_TC, T)
    nT, nt = T // tc, tc // NSEG
    tmap = lambda i: nT - 1 - i

    def body(lr_ref, li_ref, ar_ref, ai_ref, cr_ref, ci_ref, sr_ref, si_ref, br_ref, bi_ref, u_ref, dy_ref, dua_ref, wr_ref, wi_ref,
             du_ref, dwr_ref, dwi_ref, dcr_ref, dci_ref, dar_ref, dai_ref, pw_r, pw_i, ac_r, ac_i, q_r, q_i):
        first = pl.program_id(1) == 0

        @pl.when(first)
        def _():
            pw_r[...] = jnp.ones_like(pw_r)
            pw_i[...] = jnp.zeros_like(pw_i)
            ac_r[...] = jnp.zeros_like(ac_r)
            ac_i[...] = jnp.zeros_like(ac_i)
            dwr_ref[...] = jnp.zeros_like(dwr_ref)
            dwi_ref[...] = jnp.zeros_like(dwi_ref)
            dcr_ref[...] = jnp.zeros_like(dcr_ref)
            dci_ref[...] = jnp.zeros_like(dci_ref)

        ar = jnp.broadcast_to(ar_ref[...], (NSEG, sb))
        ai = jnp.broadcast_to(ai_ref[...], (NSEG, sb))
        cr, ci = cr_ref[...], ci_ref[...]

        def fix(rows, qr, qi, spr, spi, accr, acci):
            qr, qi = ar * qr - ai * qi, ar * qi + ai * qr
            xr = lr_ref[rows, :] + qr * cr - qi * ci
            xi = li_ref[rows, :] + qr * ci + qi * cr
            q_r[rows, :] = xr
            q_i[rows, :] = xi
            return qr, qi, accr + xr * spr + xi * spi, acci + xi * spr - xr * spi

        def step(k, c):
            qr, qi, accr, acci = c
            kk = nt - 1 - k
            rows = pl.ds(pl.multiple_of(kk * NSEG, NSEG), NSEG)
            prev = pl.ds(pl.multiple_of((kk - 1) * NSEG, NSEG), NSEG)
            return fix(rows, qr, qi, sr_ref[prev, :], si_ref[prev, :], accr, acci)

        c = lax.fori_loop(0, nt - 1, step, (pw_r[...], pw_i[...], ac_r[...], ac_i[...]), unroll=7)
        qr, qi, accr, acci = fix(pl.ds(0, NSEG), *c[:2], br_ref[...], bi_ref[...], *c[2:])
        pw_r[...], pw_i[...], ac_r[...], ac_i[...] = qr, qi, accr, acci

        qrb, qib = q_r[...].astype(bf16), q_i[...].astype(bf16)
        nt_dims = (((1,), (1,)), ((), ()))
        tn_dims = (((0,), (0,)), ((), ()))
        du_ref[...] = (dua_ref[...] + lax.dot_general(qrb, wr_ref[...], nt_dims, preferred_element_type=f32)
                       + lax.dot_general(qib, wi_ref[...], nt_dims, preferred_element_type=f32))
        ub = u_ref[...].astype(bf16)
        dwr_ref[...] += lax.dot_general(ub, qrb, tn_dims, preferred_element_type=f32)
        dwi_ref[...] += lax.dot_general(ub, qib, tn_dims, preferred_element_type=f32)
        dyb = dy_ref[...].astype(bf16)
        dcr_ref[...] += lax.dot_general(sr_ref[...].astype(bf16), dyb, tn_dims, preferred_element_type=f32)
        dci_ref[...] -= lax.dot_general(si_ref[...].astype(bf16), dyb, tn_dims, preferred_element_type=f32)

        @pl.when(pl.program_id(1) == nT - 1)
        def _():
            dar_ref[...] = jnp.sum(accr, axis=0, keepdims=True)
            dai_ref[...] = jnp.sum(acci, axis=0, keepdims=True)

    blk = pl.BlockSpec((tc, sb), lambda j, i: (tmap(i), j))
    asp = pl.BlockSpec((1, sb), lambda j, i: (0, j))
    csp = pl.BlockSpec((NSEG, sb), lambda j, i: (0, j))
    bsp = pl.BlockSpec((None, NSEG, sb), lambda j, i: (tmap(i), 0, j))
    chn = pl.BlockSpec((tc, cb), lambda j, i: (tmap(i), j))
    wsp = pl.BlockSpec((None, cb, sb), lambda j, i: (j, 0, 0))
    wcs = pl.BlockSpec((None, sb, cb), lambda j, i: (j, 0, 0))
    return pl.pallas_call(
        body, name=name, grid=(nblk, nT), in_specs=[blk, blk, asp, asp, csp, csp, blk, blk, bsp, bsp, chn, chn, chn, wsp, wsp],
        out_specs=[chn, wsp, wsp, wcs, wcs, asp, asp],
        out_shape=[_S((T, nblk * cb), f32), _S((nblk, cb, sb), f32), _S((nblk, cb, sb), f32), _S((nblk, sb, cb), f32),
                   _S((nblk, sb, cb), f32), _S((1, NS), f32), _S((1, NS), f32)],
        scratch_shapes=[pltpu.VMEM((NSEG, sb), f32)] * 4 + [pltpu.VMEM((tc, sb), f32)] * 2,
        compiler_params=_cparams(("parallel", "arbitrary")),
    )(ql_r, ql_i, ab_r, ab_i, c_r, c_i, s_r, s_i, sb_r, sb_i, u, dy, du_add, w_r, w_i)


SSD_TB = 256


def _dotf(a, b, dims):
    return lax.dot_general(a.astype(bf16), b.astype(bf16), dims, preferred_element_type=f32)


_NN = (((1,), (0,)), ((), ()))
_NT = (((1,), (1,)), ((), ()))
_TN = (((0,), (0,)), ((), ()))


def _f_ssd_block(x, bm, cm, dtc, dac, dtr, dar, dh, h):
    tb = x.shape[0]
    hpg = dtc.shape[1]
    ii = lax.broadcasted_iota(jnp.int32, (CHUNK, CHUNK), 0)
    jj = lax.broadcasted_iota(jnp.int32, (CHUNK, CHUNK), 1)
    tril = (ii >= jj)
    trilf = tril.astype(f32)
    triuf = (ii <= jj).astype(f32)
    hi = lax.Precision.HIGHEST
    ys = []
    for c in range(tb // CHUNK):
        r0 = c * CHUNK
        xc, bc, cc = x[r0:r0 + CHUNK], bm[r0:r0 + CHUNK], cm[r0:r0 + CHUNK]
        acc = jnp.dot(trilf, dac[r0:r0 + CHUNK], preferred_element_type=f32, precision=hi)
        acr = jnp.dot(dar[c], triuf, preferred_element_type=f32, precision=hi)
        cb = _dotf(cc, bc, _NT)
        yk, hk = [], []
        for k in range(hpg):
            xk = xc[:, k * HEADDIM:(k + 1) * HEADDIM]
            hp = h[k * HEADDIM:(k + 1) * HEADDIM]
            a_col = acc[:, k:k + 1]
            seg = a_col - acr[k:k + 1, :]
            lmat = jnp.where(tril, jnp.exp(jnp.where(tril, seg, 0.0)), 0.0)
            w = cb * lmat * dtr[c][k:k + 1, :]
            y = _dotf(w, xk, _NN) + _dotf(cc, hp, _NT) * jnp.exp(a_col) + dh[:, k:k + 1] * xk
            a_last = acc[CHUNK - 1:CHUNK, k:k + 1]
            xw = xk * (jnp.exp(a_last - a_col) * dtc[r0:r0 + CHUNK, k:k + 1])
            hk.append(jnp.exp(a_last) * hp + _dotf(xw, bc, _TN))
            yk.append(y)
        ys.append(jnp.concatenate(yk, axis=1))
        h = jnp.concatenate(hk, axis=0)
    return jnp.concatenate(ys, axis=0), h


def _ssd_specs(G, hpg, tb, tmap, b_off=0, c_off=0):
    gw = hpg * HEADDIM
    ncb = tb // CHUNK
    xsp = pl.BlockSpec((tb, gw), lambda g, i: (tmap(i), g))
    bsp = pl.BlockSpec((tb, SSD_STATE), lambda g, i: (tmap(i), b_off + g))
    osp = pl.BlockSpec((tb, SSD_STATE), lambda g, i: (tmap(i), c_off + g))
    csp = pl.BlockSpec((None, tb, hpg), lambda g, i: (g, tmap(i), 0))
    rsp = pl.BlockSpec((None, ncb, hpg, CHUNK), lambda g, i: (g, tmap(i), 0, 0))
    dsp = pl.BlockSpec((None, 1, hpg), lambda g, i: (g, 0, 0))
    hsp = pl.BlockSpec((None, None, gw, SSD_STATE), lambda g, i: (g, tmap(i), 0, 0))
    return xsp, bsp, osp, csp, rsp, dsp, hsp


def _ssd_fwd(xc, dtc, dac, dtr, dar, dh, *, d_inner, name):
    T = xc.shape[0]
    G, _, hpg = dtc.shape
    gw = hpg * HEADDIM
    tb = min(SSD_TB, T)
    nb = T // tb
    b_off = d_inner // SSD_STATE
    xsp, bsp, osp, csp, rsp, dsp, hsp = _ssd_specs(G, hpg, tb, lambda i: i, b_off, b_off + G)

    def body(x_ref, b_ref, c_ref, dtc_ref, dac_ref, dtr_ref, dar_ref, dh_ref, y_ref, hs_ref, h_scr):
        @pl.when(pl.program_id(1) == 0)
        def _():
            h_scr[...] = jnp.zeros_like(h_scr)

        h = h_scr[...]
        hs_ref[...] = h
        y, hn = _f_ssd_block(x_ref[...], b_ref[...], c_ref[...], dtc_ref[...], dac_ref[...], dtr_ref[...], dar_ref[...], dh_ref[...], h)
        y_ref[...] = y
        h_scr[...] = hn

    return pl.pallas_call(
        body, name=name, grid=(G, nb), in_specs=[xsp, bsp, osp, csp, csp, rsp, rsp, dsp], out_specs=[xsp, hsp],
        out_shape=[_S((T, G * gw), f32), _S((G, nb, gw, SSD_STATE), f32)], scratch_shapes=[pltpu.VMEM((gw, SSD_STATE), f32)],
        compiler_params=_cparams(("parallel", "arbitrary")),
    )(xc, xc, xc, dtc, dac, dtr, dar, dh)


def _ssd_bwd(xc, dtc, dac, dtr, dar, dh, hs, dy, *, d_inner, name):
    T = xc.shape[0]
    G, _, hpg = dtc.shape
    gw = hpg * HEADDIM
    tb = min(SSD_TB, T)
    nb = T // tb
    b_off = d_inner // SSD_STATE
    xsp, bsp, osp, csp, rsp, dsp, hsp = _ssd_specs(G, hpg, tb, lambda i: nb - 1 - i, b_off, b_off + G)
    _, gsp, _, _, _, _, _ = _ssd_specs(G, hpg, tb, lambda i: nb - 1 - i)

    def body(x_ref, b_ref, c_ref, dtc_ref, dac_ref, dtr_ref, dar_ref, dh_ref, hs_ref, dy_ref,
             dx_ref, db_ref, dc_ref, ddtc_ref, ddac_ref, ddtr_ref, ddar_ref, ddh_ref, g_scr):
        first = pl.program_id(1) == 0

        @pl.when(first)
        def _():
            g_scr[...] = jnp.zeros_like(g_scr)
            ddh_ref[...] = jnp.zeros_like(ddh_ref)

        ins = [r[...] for r in (x_ref, b_ref, c_ref, dtc_ref, dac_ref, dtr_ref, dar_ref, dh_ref, hs_ref)]
        _, vjp = jax.vjp(_f_ssd_block, *ins)
        g = vjp((dy_ref[...], g_scr[...]))
        dx_ref[...], db_ref[...], dc_ref[...] = g[0], g[1], g[2]
        ddtc_ref[...], ddac_ref[...], ddtr_ref[...], ddar_ref[...] = g[3], g[4], g[5], g[6]
        ddh_ref[...] += g[7]
        g_scr[...] = g[8]

    return pl.pallas_call(
        body, name=name, grid=(G, nb), in_specs=[xsp, bsp, osp, csp, csp, rsp, rsp, dsp, hsp, xsp],
        out_specs=[xsp, gsp, gsp, csp, csp, rsp, rsp, dsp],
        out_shape=[_S((T, G * gw), f32), _S((T, G * SSD_STATE), f32), _S((T, G * SSD_STATE), f32), _S(dtc.shape, f32), _S(dtc.shape, f32),
                   _S(dtr.shape, f32), _S(dtr.shape, f32), _S((G, 1, hpg), f32)],
        scratch_shapes=[pltpu.VMEM((gw, SSD_STATE), f32)], compiler_params=_cparams(("parallel", "arbitrary")),
    )(xc, xc, xc, dtc, dac, dtr, dar, dh, hs, dy)


def _peers():
    x, y, c = lax.axis_index("x"), lax.axis_index("y"), lax.axis_index("c")
    return x, y, c


def _all_gather(shard, *, name):
    R, W = shard.shape

    def body(x_ref, out_ref, send_sems, recv_sems, local_sem):
        x, y, c = _peers()
        me, sibling = (x, y, c), (x, y, 1 - c)
        chips = [(1 - x, y), (x, 1 - y), (1 - x, 1 - y)]

        def slot(px, py, pc):
            return out_ref.at[4 * px + 2 * py + pc]

        def copy(k, block, to, src=None):
            return pltpu.make_async_remote_copy(
                src_ref=slot(*block) if src is None else src, dst_ref=slot(*block), send_sem=send_sems.at[k],
                recv_sem=recv_sems.at[k], device_id=to, device_id_type=MESH)

        mine = pltpu.make_async_copy(x_ref, slot(*me), local_sem)
        mine.start()
        first = [copy(0, me, sibling, src=x_ref)]
        first += [copy(1 + j, me, (*chip, c), src=x_ref) for j, chip in enumerate(chips)]
        for cp in first:
            cp.start()
        passed = [copy(4 + j, (*chip, c), sibling) for j, chip in enumerate(chips)]
        for j, chip in enumerate(chips):
            copy(1 + j, (*chip, c), me).wait_recv()
            passed[j].start()
        copy(0, sibling, me).wait_recv()
        for j, chip in enumerate(chips):
            copy(4 + j, (*chip, 1 - c), me).wait_recv()
        for cp in first + passed:
            cp.wait_send()
        mine.wait()

    anyspec = pl.BlockSpec(memory_space=pl.ANY)
    return pl.pallas_call(
        body, name=name, out_shape=_S((N_DEV, R, W), shard.dtype), in_specs=[anyspec], out_specs=anyspec,
        scratch_shapes=[pltpu.SemaphoreType.DMA((7,)), pltpu.SemaphoreType.DMA((7,)), pltpu.SemaphoreType.DMA],
    )(shard)


def _all_to_all(parts, *, name):
    _, R, W = parts.shape

    def body(p_ref, out_ref, send_sems, recv_sems, local_sem):
        x, y, c = _peers()
        me = 4 * x + 2 * y + c
        mine = pltpu.make_async_copy(p_ref.at[me], out_ref.at[me], local_sem)
        mine.start()
        copies = []
        for r in range(1, N_DEV):
            px, py, pc = x ^ (r >> 2), y ^ ((r >> 1) & 1), c ^ (r & 1)
            peer = 4 * px + 2 * py + pc
            cp = pltpu.make_async_remote_copy(
                src_ref=p_ref.at[peer], dst_ref=out_ref.at[me], send_sem=send_sems.at[r - 1], recv_sem=recv_sems.at[r - 1],
                device_id=(px, py, pc), device_id_type=MESH)
            cp.start()
            copies.append(cp)
        for cp in copies:
            cp.wait_recv()
        for cp in copies:
            cp.wait_send()
        mine.wait()

    anyspec = pl.BlockSpec(memory_space=pl.ANY)
    return pl.pallas_call(
        body, name=name, out_shape=_S(parts.shape, parts.dtype), in_specs=[anyspec], out_specs=anyspec,
        scratch_shapes=[pltpu.SemaphoreType.DMA((7,)), pltpu.SemaphoreType.DMA((7,)), pltpu.SemaphoreType.DMA],
    )(parts)


def _sum_slots(stack, *, name):
    n, R, W = stack.shape
    tr = _pick(R, 1024, align=8)

    def body(s_ref, o_ref):
        acc = s_ref[0]
        for k in range(1, n):
            acc = acc + s_ref[k]
        o_ref[...] = acc

    return pl.pallas_call(
        body, name=name, grid=(R // tr,), in_specs=[pl.BlockSpec((n, tr, W), lambda i: (0, i, 0))],
        out_specs=pl.BlockSpec((tr, W), lambda i: (i, 0)), out_shape=_S((R, W), f32), compiler_params=_cparams(("parallel",)),
    )(stack)


def _adamw(g, w, m, v, *, name):
    R, W = w.shape
    tr = _pick(R, max(8, (1 << 20) // (4 * W)), align=8)
    c1 = 1.0 / (1.0 - ADAM_B1 ** ADAM_STEP)
    c2 = 1.0 / (1.0 - ADAM_B2 ** ADAM_STEP)

    def body(g_ref, w_ref, m_ref, v_ref, d_ref, nm_ref, nv_ref):
        gv = g_ref[...]
        nm = ADAM_B1 * m_ref[...] + (1.0 - ADAM_B1) * gv
        nv = ADAM_B2 * v_ref[...] + (1.0 - ADAM_B2) * jnp.square(gv)
        d_ref[...] = -ADAM_LR * ((nm * c1) / (jnp.sqrt(nv * c2) + ADAM_EPS) + ADAM_WD * w_ref[...])
        nm_ref[...] = nm
        nv_ref[...] = nv

    sp = pl.BlockSpec((tr, W), lambda i: (i, 0))
    return pl.pallas_call(
        body, name=name, grid=(R // tr,), in_specs=[sp] * 4, out_specs=[sp] * 3, out_shape=[_S((R, W), f32)] * 3,
        compiler_params=_cparams(("parallel",)),
    )(g, w, m, v)


def _seg_interleave(a):
    T, W = a.shape
    return a.reshape(NSEG, T // NSEG, W).transpose(1, 0, 2).reshape(T, W)


def _seg_deinterleave(a):
    T, W = a.shape
    return a.reshape(T // NSEG, NSEG, W).transpose(1, 0, 2).reshape(T, W)


def _pad_flat(a, mult):
    a = a.reshape(-1)
    n = -(-a.shape[0] // mult) * mult
    return a if n == a.shape[0] else jnp.pad(a, (0, n - a.shape[0]))


def _pad_cols(a, mult):
    n = -(-a.shape[1] // mult) * mult
    return a if n == a.shape[1] else jnp.pad(a, ((0, 0), (0, n - a.shape[1])))


def _block_diag(t):
    nblk, g, P, Q = t.shape
    eye = jnp.eye(g, dtype=t.dtype)
    return (t[:, :, :, None, :] * eye[None, :, None, :, None]).reshape(nblk, g * P, g * Q)


def _block_diag_t(w, P, Q):
    nblk = w.shape[0]
    g = w.shape[1] // P
    eye = jnp.eye(g, dtype=w.dtype)
    return (w.reshape(nblk, g, P, g, Q) * eye[None, :, None, :, None]).sum(axis=3)


_BIG = (("ffn1_w_gate", "col"), ("ffn1_w_up", "col"), ("ffn1_w_down", "row"), ("w_in", "col"), ("s5_w_glu", "row"),
        ("w_proj_s5", "row"), ("w_proj_ssd", "row"), ("w_out", "row"), ("ffn2_w_gate", "col"), ("ffn2_w_up", "col"),
        ("ffn2_w_down", "row"))
_SMALL = ("ffn1_norm", "mix_norm", "conv_b", "s5_A_re", "s5_A_im", "s5_log_dt", "s5_B_re", "s5_B_im", "s5_C_re", "s5_C_im",
          "s5_D", "s5_b_glu", "ssd_A_log", "ssd_dt_bias", "ssd_D", "ssd_norm", "b_gate", "ffn2_norm", "final_norm")
_WEIGHTS = ("ffn1_norm", "ffn1_w_gate", "ffn1_w_up", "ffn1_w_down", "mix_norm", "w_in", "conv_w", "conv_b", "s5_A_re", "s5_A_im",
            "s5_log_dt", "s5_B_re", "s5_B_im", "s5_C_re", "s5_C_im", "s5_D", "s5_w_glu", "s5_b_glu", "ssd_A_log", "ssd_dt_bias",
            "ssd_D", "ssd_norm", "w_proj_s5", "w_proj_ssd", "b_gate", "w_out", "ffn2_norm", "ffn2_w_gate", "ffn2_w_up",
            "ffn2_w_down", "final_norm")
BF16_TILE = 16 * LANES


def _full_from_gathered(g, shape, kind):
    r, c = shape
    g = g[:, : r * c].reshape(N_DEV, r, c)
    return g.reshape(N_DEV * r, c) if kind == "row" else g.transpose(1, 0, 2).reshape(r, N_DEV * c)


def _slots_from_full(w, kind):
    if kind == "row":
        return w.reshape(N_DEV, -1)
    r = w.shape[0]
    return w.reshape(r, N_DEV, -1).transpose(1, 0, 2).reshape(N_DEV, -1)


def _ffn_fwd(x, n, wgu, wd, tag):
    D = x.shape[1]
    F = wd.shape[0]
    h = _rows(_f_rmsnorm, [x], [n], [(D, bf16)], name=tag + "_norm")[0]
    ab = _mm(h, wgu, name=tag + "_gate_up")
    c = _rows(_f_swiglu, [ab], [], [(F, bf16)], name=tag + "_act")[0]
    y = _mm(c, wd, scale=0.5, add=x, name=tag + "_down")
    return y, (x, n, h, ab, c)


def _ffn_bwd(saved, wgu, wd, dy, tag):
    x, n, h, ab, c = saved
    dc = _mm(dy, wd, tb=True, scale=0.5, name=tag + "_d_act")
    dwd = _mm(c, dy, ta=True, scale=0.5, name=tag + "_d_wdown")
    (dab,), _ = _rows_bwd(_f_swiglu, [ab], [], [dc], name=tag + "_act_bwd", want_rows=[0], row_dtypes={0: bf16})
    dh = _mm(dab, wgu, tb=True, name=tag + "_d_h")
    dwgu = _mm(h, dab, ta=True, name=tag + "_d_wgu")
    (dx,), (dn,) = _rows_bwd(_f_rmsnorm, [x], [n], [dh], name=tag + "_norm_bwd", want_rows=[0], adds={0: dy})
    return dx, dn, dwgu, dwd


def kernel(x, ffn1_norm, ffn1_w_gate, ffn1_w_up, ffn1_w_down, mix_norm, w_in, conv_w, conv_b, s5_A_re, s5_A_im, s5_log_dt, s5_B_re, s5_B_im, s5_C_re, s5_C_im, s5_D, s5_w_glu, s5_b_glu, ssd_A_log, ssd_dt_bias, ssd_D, ssd_norm, w_proj_s5, w_proj_ssd, b_gate, w_out, ffn2_norm, ffn2_w_gate, ffn2_w_up, ffn2_w_down, final_norm, loss_target, m_ffn1_norm, m_ffn1_w_gate, m_ffn1_w_up, m_ffn1_w_down, m_mix_norm, m_w_in, m_conv_w, m_conv_b, m_s5_A_re, m_s5_A_im, m_s5_log_dt, m_s5_B_re, m_s5_B_im, m_s5_C_re, m_s5_C_im, m_s5_D, m_s5_w_glu, m_s5_b_glu, m_ssd_A_log, m_ssd_dt_bias, m_ssd_D, m_ssd_norm, m_w_proj_s5, m_w_proj_ssd, m_b_gate, m_w_out, m_ffn2_norm, m_ffn2_w_gate, m_ffn2_w_up, m_ffn2_w_down, m_final_norm, v_ffn1_norm, v_ffn1_w_gate, v_ffn1_w_up, v_ffn1_w_down, v_mix_norm, v_w_in, v_conv_w, v_conv_b, v_s5_A_re, v_s5_A_im, v_s5_log_dt, v_s5_B_re, v_s5_B_im, v_s5_C_re, v_s5_C_im, v_s5_D, v_s5_w_glu, v_s5_b_glu, v_ssd_A_log, v_ssd_dt_bias, v_ssd_D, v_ssd_norm, v_w_proj_s5, v_w_proj_ssd, v_b_gate, v_w_out, v_ffn2_norm, v_ffn2_w_gate, v_ffn2_w_up, v_ffn2_w_down, v_final_norm):
    P = dict(locals())
    T, D = x.shape[1], x.shape[2]
    x0, tgt = x[0], loss_target[0]
    sh = {k: P[k][0] for k, _ in _BIG}
    kinds = dict(_BIG)

    sizes = {k: -(-sh[k].size // BF16_TILE) * BF16_TILE for k, _ in _BIG}
    flat = jnp.concatenate([_pad_flat(sh[k].astype(bf16), BF16_TILE) for k, _ in _BIG])
    gathered = _all_gather(flat.reshape(-1, LANES), name="gather_weights").reshape(N_DEV, -1)
    W, off = {}, 0
    for k, kind in _BIG:
        W[k] = _full_from_gathered(gathered[:, off:off + sizes[k]], sh[k].shape, kind)
        off += sizes[k]
    cw = conv_w[0]
    cw_g = _all_gather(_pad_flat(cw, TILE_ELEMS).reshape(-1, LANES), name="gather_conv_w").reshape(N_DEV, -1)
    conv_w_full = _full_from_gathered(cw_g, cw.shape, "col")

    F = W["ffn1_w_down"].shape[0]
    d_inner = W["w_proj_ssd"].shape[0]
    conv_dim = conv_w_full.shape[1]
    H = ssd_A_log.shape[1]
    G = (conv_dim - d_inner) // (2 * SSD_STATE)
    hpg = H // G
    nc = T // CHUNK
    Gs = D // S5_GROUP
    nblk = Gs // S5_GPB
    NS = Gs * S5_STATE
    wgu1 = jnp.concatenate([W["ffn1_w_gate"], W["ffn1_w_up"]], axis=1)
    wgu2 = jnp.concatenate([W["ffn2_w_gate"], W["ffn2_w_up"]], axis=1)
    win = W["w_in"]
    o1, o2, o3, o4 = D, D + d_inner, D + d_inner + conv_dim, D + d_inner + conv_dim + H
    w_u, w_z, w_xbc, w_gl = win[:, :o1], win[:, o1:o2], win[:, o2:o3], win[:, o4:]
    w_dt = _pad_cols(win[:, o3:o4], LANES)

    x1, sv1 = _ffn_fwd(x0, ffn1_norm, wgu1, W["ffn1_w_down"], "ffn1")
    h2 = _rows(_f_rmsnorm, [x1], [mix_norm], [(D, bf16)], name="mix_norm")[0]
    u = _mm(h2, w_u, name="in_u")
    z = _mm(h2, w_z, name="in_z")
    xbc = _mm(h2, w_xbc, name="in_xbc")
    gl = _mm(h2, w_gl, name="in_gate")
    dtr = _mm(h2, w_dt, name="in_dt")

    u_p = _seg_interleave(u)
    rep = lambda a: jnp.repeat(a, S5_GROUP, axis=0)
    lr, li, ldt = s5_A_re[0], s5_A_im[0], s5_log_dt[0].reshape(Gs, 1)
    brt = s5_B_re[0].transpose(0, 2, 1).reshape(Gs * S5_GROUP, S5_STATE)
    bit = s5_B_im[0].transpose(0, 2, 1).reshape(Gs * S5_GROUP, S5_STATE)
    prep_args = (lr, li, ldt, rep(lr), rep(li), rep(ldt), brt, bit)
    ar, ai, bbrt, bbit = _s5_prep(prep_args, name="s5_prep")
    a_r, a_i = ar.reshape(1, NS), ai.reshape(1, NS)
    wb_r = _block_diag(bbrt.reshape(nblk, S5_GPB, S5_GROUP, S5_STATE)).astype(bf16)
    wb_i = _block_diag(bbit.reshape(nblk, S5_GPB, S5_GROUP, S5_STATE)).astype(bf16)
    c4r = s5_C_re[0].reshape(nblk, S5_GPB, S5_GROUP, S5_STATE).transpose(0, 1, 3, 2)
    c4i = s5_C_im[0].reshape(nblk, S5_GPB, S5_GROUP, S5_STATE).transpose(0, 1, 3, 2)
    wc_r, wc_i = _block_diag(c4r).astype(bf16), _block_diag(c4i).astype(bf16)
    sl_r, sl_i, p_r, p_i = _s5_local_scan(u_p, wb_r, wb_i, a_r, a_i, reverse=False, name="s5_scan")
    c_r, c_i = _s5_carry(sl_r[T - NSEG:], sl_i[T - NSEG:], p_r, p_i, reverse=False, name="s5_carry")
    s_r, s_i, ylin = _s5_fix_out(sl_r, sl_i, a_r, a_i, c_r, c_i, wc_r, wc_i, name="s5_fix_out")
    g5 = _rows(_f_s5_post, [ylin, u_p], [s5_D], [(D, f32)], name="s5_gelu")[0]
    v5 = _mm(g5, W["s5_w_glu"], name="s5_glu_mm")
    o5 = _rows(_f_glu, [g5, v5], [s5_b_glu], [(D, bf16)], name="s5_glu")[0]
    p5 = _seg_deinterleave(_mm(o5, W["w_proj_s5"], name="proj_s5"))

    xc = _conv_fwd(xbc, conv_w_full, conv_b, name="conv")
    bias_p, alog_p = _pad_cols(ssd_dt_bias, LANES), _pad_cols(ssd_A_log, LANES)
    dt_p, da_p = _rows(_f_dt, [dtr], [bias_p, alog_p], [(LANES, f32), (LANES, f32)], name="ssd_dt")
    col_l = lambda a: a[:, :H].reshape(T, G, hpg).transpose(1, 0, 2)
    row_l = lambda a: a.reshape(G, nc, CHUNK, hpg).transpose(0, 1, 3, 2)
    dtc, dac = col_l(dt_p), col_l(da_p)
    dtw, daw = row_l(dtc), row_l(dac)
    dh = ssd_D.reshape(G, 1, hpg)
    ssd_in = (xc, dtc, dac, dtw, daw, dh)
    y_ssd, hs = _ssd_fwd(*ssd_in, d_inner=d_inner, name="ssd")
    yn = _rows(_f_gated_norm, [y_ssd, z], [ssd_norm], [(d_inner, bf16)], name="ssd_gated_norm")[0]
    pssd = _mm(yn, W["w_proj_ssd"], name="proj_ssd")

    merged = _rows(_f_merge, [gl, p5, pssd], [b_gate], [(D, bf16)], name="merge")[0]
    x2 = _mm(merged, W["w_out"], add=x1, name="out_proj")
    x3, sv2 = _ffn_fwd(x2, ffn2_norm, wgu2, W["ffn2_w_down"], "ffn2")
    lossv, dx3, d_final = _loss_stage(x3, tgt, final_norm.reshape(1, D), name="loss")

    gw = {}
    gs = {"final_norm": d_final}
    dx2, gs["ffn2_norm"], dwgu2, gw["ffn2_w_down"] = _ffn_bwd(sv2, wgu2, W["ffn2_w_down"], dx3, "ffn2")
    gw["ffn2_w_gate"], gw["ffn2_w_up"] = dwgu2[:, :F], dwgu2[:, F:]

    dmerged = _mm(dx2, W["w_out"], tb=True, name="d_merged")
    gw["w_out"] = _mm(merged, dx2, ta=True, name="d_w_out")
    (dgl, dp5, dpssd), (gs["b_gate"],) = _rows_bwd(_f_merge, [gl, p5, pssd], [b_gate], [dmerged], name="merge_bwd", want_rows=[0, 1, 2])

    dyn = _mm(dpssd, W["w_proj_ssd"], tb=True, name="d_yn")
    gw["w_proj_ssd"] = _mm(yn, dpssd, ta=True, name="d_w_proj_ssd")
    (dyssd, dz), (gs["ssd_norm"],) = _rows_bwd(_f_gated_norm, [y_ssd, z], [ssd_norm], [dyn], name="ssd_gated_norm_bwd", want_rows=[0, 1])
    dxs, dbm, dcm, ddtc, ddac, ddtw, ddaw, ddh = _ssd_bwd(*ssd_in, hs, dyssd, d_inner=d_inner, name="ssd_bwd")
    fold = lambda dc_, dw_: _pad_cols((dc_ + dw_.transpose(0, 1, 3, 2).reshape(G, T, hpg)).transpose(1, 0, 2).reshape(T, H), LANES)
    (ddtr,), (dbias_p, dalog_p) = _rows_bwd(_f_dt, [dtr], [bias_p, alog_p], [fold(ddtc, ddtw), fold(ddac, ddaw)], name="ssd_dt_bwd", want_rows=[0])
    gs["ssd_dt_bias"], gs["ssd_A_log"], gs["ssd_D"] = dbias_p[:, :H], dalog_p[:, :H], ddh.reshape(1, H)
    dxbc, gw["conv_w"], gs["conv_b"] = _conv_bwd(xbc, conv_w_full, conv_b, jnp.concatenate([dxs, dbm, dcm], axis=1), name="conv_bwd")

    dp5p = _seg_interleave(dp5)
    do5 = _mm(dp5p, W["w_proj_s5"], tb=True, name="d_o5")
    gw["w_proj_s5"] = _mm(o5, dp5p, ta=True, name="d_w_proj_s5")
    (dg5a, dv5), (gs["s5_b_glu"],) = _rows_bwd(_f_glu, [g5, v5], [s5_b_glu], [do5], name="s5_glu_bwd", want_rows=[0, 1])
    dg5 = _mm(dv5, W["s5_w_glu"], tb=True, add=dg5a, name="d_g5")
    gw["s5_w_glu"] = _mm(g5, dv5, ta=True, name="d_w_glu")
    (dylin, du_a), (gs["s5_D"],) = _rows_bwd(_f_s5_post, [ylin, u_p], [s5_D], [dg5], name="s5_gelu_bwd", want_rows=[0, 1])
    wct_r, wct_i = wc_r.transpose(0, 2, 1), -wc_i.transpose(0, 2, 1)
    ql_r, ql_i, pb_r, pb_i = _s5_local_scan(dylin, wct_r, wct_i, a_r, -a_i, reverse=True, name="s5_scan_bwd")
    cb_r, cb_i = _s5_carry(ql_r[:NSEG], ql_i[:NSEG], pb_r, pb_i, reverse=True, name="s5_carry_bwd")
    tc = min(S5_TC, T)

    def before_blocks(s):
        last = s.reshape(T // tc, tc, NS)[:, tc - NSEG:, :]
        wrap = jnp.concatenate([jnp.zeros((1, 1, NS), f32), last[-1:, : NSEG - 1, :]], axis=1)
        return jnp.concatenate([wrap, last[:-1]], axis=0)

    du_p, dwb_r, dwb_i, dwc_r, dwc_i, d_ar, d_ai = _s5_fix_bwd(
        ql_r, ql_i, a_r, -a_i, cb_r, cb_i, s_r, s_i, before_blocks(s_r), before_blocks(s_i), u_p, dylin, du_a, wb_r, wb_i, name="s5_fix_bwd")
    du = _seg_deinterleave(du_p)
    unblk = lambda w: _block_diag_t(w, S5_GROUP, S5_STATE).reshape(Gs * S5_GROUP, S5_STATE)
    rsum = jnp.repeat(jnp.eye(Gs, dtype=f32), S5_GROUP, axis=1)
    d_lr, d_li, d_ldt, d_brt, d_bit = _s5_prep_bwd(
        prep_args, (d_ar.reshape(Gs, S5_STATE), d_ai.reshape(Gs, S5_STATE), unblk(dwb_r), unblk(dwb_i)), rsum, name="s5_prep_bwd")
    gs["s5_A_re"], gs["s5_A_im"], gs["s5_log_dt"] = d_lr, d_li, d_ldt.reshape(1, Gs)
    gs["s5_B_re"] = d_brt.reshape(Gs, S5_GROUP, S5_STATE).transpose(0, 2, 1)
    gs["s5_B_im"] = d_bit.reshape(Gs, S5_GROUP, S5_STATE).transpose(0, 2, 1)
    gs["s5_C_re"] = _block_diag_t(dwc_r, S5_STATE, S5_GROUP).transpose(0, 1, 3, 2).reshape(Gs, S5_GROUP, S5_STATE)
    gs["s5_C_im"] = _block_diag_t(dwc_i, S5_STATE, S5_GROUP).transpose(0, 1, 3, 2).reshape(Gs, S5_GROUP, S5_STATE)

    dh2 = _mm(du, w_u, tb=True, name="d_h2_u")
    dh2 = _mm(dz, w_z, tb=True, add=dh2, name="d_h2_z")
    dh2 = _mm(dxbc, w_xbc, tb=True, add=dh2, name="d_h2_xbc")
    dh2 = _mm(dgl, w_gl, tb=True, add=dh2, name="d_h2_gate")
    dh2 = _mm(ddtr, w_dt, tb=True, add=dh2, name="d_h2_dt")
    gw["w_in"] = jnp.concatenate([
        _mm(h2, du, ta=True, name="d_w_u"), _mm(h2, dz, ta=True, name="d_w_z"), _mm(h2, dxbc, ta=True, name="d_w_xbc"),
        _mm(h2, ddtr, ta=True, name="d_w_dt")[:, :H], _mm(h2, dgl, ta=True, name="d_w_gate")], axis=1)
    (dx1,), (gs["mix_norm"],) = _rows_bwd(_f_rmsnorm, [x1], [mix_norm], [dh2], name="mix_norm_bwd", want_rows=[0], adds={0: dx2})
    dx0, gs["ffn1_norm"], dwgu1, gw["ffn1_w_down"] = _ffn_bwd(sv1, wgu1, W["ffn1_w_down"], dx1, "ffn1")
    gw["ffn1_w_gate"], gw["ffn1_w_up"] = dwgu1[:, :F], dwgu1[:, F:]

    big = list(_BIG) + [("conv_w", "col")]
    sh["conv_w"] = cw
    gsz = {k: -(-sh[k].size // TILE_ELEMS) * TILE_ELEMS for k, _ in big}
    slots = jnp.concatenate([_pad_cols(_slots_from_full(gw[k], kind), TILE_ELEMS) for k, kind in big], axis=1)
    arrived = _all_to_all(slots.reshape(N_DEV, -1, LANES), name="exchange_grads")
    gsum = _sum_slots(arrived, name="sum_grads").reshape(-1)
    grads, off = {}, 0
    for k, _ in big:
        grads[k] = gsum[off:off + sh[k].size].reshape(sh[k].shape)
        off += gsz[k]
    small_shapes = {k: (P[k][0].shape if P[k].ndim > 1 else P[k].shape) for k in _SMALL}
    pack = lambda d: jnp.concatenate([_pad_flat(d[k], TILE_ELEMS) for k in _SMALL]).reshape(-1, LANES)
    gsmall = _sum_slots(_all_gather(pack(gs), name="gather_small_grads"), name="sum_small_grads")
    snum = {k: math.prod(small_shapes[k]) for k in _SMALL}
    ssz = {k: -(-snum[k] // TILE_ELEMS) * TILE_ELEMS for k in _SMALL}

    delta, new_m, new_v = {}, {}, {}
    for k, _ in big:
        two_d = lambda a: a[0].reshape(sh[k].shape)
        d_, m_, v_ = _adamw(grads[k], two_d(P[k]), two_d(P["m_" + k]), two_d(P["v_" + k]), name="adamw_" + k)
        delta[k], new_m[k], new_v[k] = d_, m_, v_
    d_s, m_s, v_s = _adamw(gsmall, pack({k: P[k] for k in _SMALL}), pack({k: P["m_" + k] for k in _SMALL}),
                           pack({k: P["v_" + k] for k in _SMALL}), name="adamw_small")
    off = 0
    gflat, dflat, mflat, vflat = gsmall.reshape(-1), d_s.reshape(-1), m_s.reshape(-1), v_s.reshape(-1)
    for k in _SMALL:
        n = snum[k]
        grads[k], delta[k], new_m[k], new_v[k] = (a[off:off + n] for a in (gflat, dflat, mflat, vflat))
        off += ssz[k]

    loss = lax.psum(lossv[0, 0], ("x", "y", "c"))
    out = [loss, dx0.reshape(x.shape)]
    for d in (grads, delta, new_m, new_v):
        out += [d[k].reshape(P[k].shape) for k in _WEIGHTS]
    return tuple(out)
```

```python
import math

import jax
import jax.numpy as jnp
from jax import lax
from jax.experimental import pallas as pl
from jax.experimental.pallas import tpu as pltpu

f32 = jnp.float32
bf16 = jnp.bfloat16
_S = jax.ShapeDtypeStruct

EPS = 1e-6
S5_GROUP = 16
S5_STATE = 64
HEADDIM = 64
SSD_STATE = 128
CHUNK = 64
CONV_K = 4
NSEG = 8
S5_GPB = 8
N_DEV = 8
LANES = 128
TILE_ELEMS = 8 * LANES

ADAM_LR = 0.001
ADAM_B1 = 0.9
ADAM_B2 = 0.999
ADAM_EPS = 1e-08
ADAM_WD = 0.01
ADAM_STEP = 10

VMEM_LIMIT = 56 * 1024 * 1024
MM_FULL_K = 3072
MESH = pl.DeviceIdType.MESH


def _cparams(sem=None):
    return pltpu.CompilerParams(dimension_semantics=sem, vmem_limit_bytes=VMEM_LIMIT)


def _pick(dim, pref, align=LANES):
    best = None
    t = align
    while t <= min(dim, pref):
        if dim % t == 0:
            best = t
        t += align
    return best or dim


def _slot_of(k):
    return (k & 1) * (N_DEV // 2) + (k >> 1)


def _mm(a, b, *, name, ta=False, tb=False, a_blk=None, b_blk=None, o_blk=None, o_slots=False, a_seg=False, b_seg=False,
        o_seg=False, tm=None, out_dtype=f32, scale=1.0, add=None):
    a2, b2 = a.shape[-2:], b.shape[-2:]
    Ma, Ka = (a2[1], a2[0]) if ta else a2
    Kb, Nb = (b2[1], b2[0]) if tb else b2
    M = Ma * (a.shape[0] if a_blk == "m" else 1)
    K = Ka * (a.shape[0] if a_blk == "k" else 1)
    N = Nb * (b.shape[0] if b_blk == "n" else 1)
    assert K == Kb * (b.shape[0] if b_blk == "k" else 1), (a.shape, b.shape, ta, tb, a_blk, b_blk)
    assert (a.ndim == 3) == (a_blk is not None) and (b.ndim == 3) == (b_blk is not None)
    tm = Ma if a_blk == "m" else (tm or _pick(M, 512))
    tn = Nb if b_blk == "n" else _pick(N, 1024)
    if a_blk == "k" or b_blk == "k":
        tk = Ka if a_blk == "k" else Kb
        assert tk == (Kb if b_blk == "k" else tk)
    else:
        tk = K if K <= MM_FULL_K else _pick(K, 1024)
    if (a_seg and not ta) or o_seg:
        tm = M // NSEG
    if (a_seg and ta) or b_seg:
        tk = K // NSEG
    gm, gn, nk = M // tm, N // tn, K // tk
    assert not (add is not None and (o_seg or o_blk)) and not (o_blk and o_seg)

    if a_seg:
        assert a.ndim == 2
        a = a.reshape(a.shape[0] // NSEG, NSEG * a.shape[1])
        if ta:
            a_spec = pl.BlockSpec((tk, tm), lambda i, j, k: (0, k * (Ma // tm) + i))
        else:
            a_spec = pl.BlockSpec((tm, tk), lambda i, j, k: (0, i * (Ka // tk) + k))
    elif a.ndim == 3:
        lead = (lambda i, k: i) if a_blk == "m" else (lambda i, k: k)
        if ta:
            a_spec = pl.BlockSpec((None, tk, tm), lambda i, j, k: (lead(i, k), 0 if a_blk == "k" else k, 0 if a_blk == "m" else i))
        else:
            a_spec = pl.BlockSpec((None, tm, tk), lambda i, j, k: (lead(i, k), 0 if a_blk == "m" else i, 0 if a_blk == "k" else k))
    else:
        a_spec = pl.BlockSpec((tk, tm), lambda i, j, k: (k, i)) if ta else pl.BlockSpec((tm, tk), lambda i, j, k: (i, k))
    if b_seg:
        assert b.ndim == 2 and not tb
        b = b.reshape(b.shape[0] // NSEG, NSEG * b.shape[1])
        b_spec = pl.BlockSpec((tk, tn), lambda i, j, k: (0, k * (Nb // tn) + j))
    elif b.ndim == 3:
        lead = (lambda j, k: j) if b_blk == "n" else (lambda j, k: k)
        if tb:
            b_spec = pl.BlockSpec((None, tn, tk), lambda i, j, k: (lead(j, k), 0 if b_blk == "n" else j, 0 if b_blk == "k" else k))
        else:
            b_spec = pl.BlockSpec((None, tk, tn), lambda i, j, k: (lead(j, k), 0 if b_blk == "k" else k, 0 if b_blk == "n" else j))
    else:
        b_spec = pl.BlockSpec((tn, tk), lambda i, j, k: (j, k)) if tb else pl.BlockSpec((tk, tn), lambda i, j, k: (k, j))
    slot = _slot_of if o_slots else (lambda k: k)
    if o_blk == "n":
        assert gn == N_DEV or not o_slots
        o_shape, o_spec = (gn, M, tn), pl.BlockSpec((None, tm, tn), lambda i, j, k: (slot(j), i, 0))
    elif o_blk == "m":
        assert gm == N_DEV or not o_slots
        o_shape, o_spec = (gm, tm, N), pl.BlockSpec((None, tm, tn), lambda i, j, k: (slot(i), 0, j))
    elif o_seg:
        o_shape, o_spec = (tm, NSEG * N), pl.BlockSpec((tm, tn), lambda i, j, k: (0, i * (N // tn) + j))
    else:
        o_shape, o_spec = (M, N), pl.BlockSpec((tm, tn), lambda i, j, k: (i, j))
    dims = (((0 if ta else 1,), (1 if tb else 0,)), ((), ()))
    has_add = add is not None

    def body(*refs):
        a_ref, b_ref = refs[0], refs[1]
        add_ref = refs[2] if has_add else None
        o_ref, acc_ref = refs[-2], refs[-1]
        k = pl.program_id(2)

        @pl.when(k == 0)
        def _():
            acc_ref[...] = jnp.zeros_like(acc_ref)

        acc_ref[...] += lax.dot_general(a_ref[...].astype(bf16), b_ref[...].astype(bf16), dims, preferred_element_type=f32)

        @pl.when(k == nk - 1)
        def _():
            r = acc_ref[...] * scale
            if has_add:
                r = r + add_ref[...].astype(f32)
            o_ref[...] = r.astype(out_dtype)

    ins = [a, b] + ([add] if has_add else [])
    in_specs = [a_spec, b_spec] + ([o_spec] if has_add else [])
    out = pl.pallas_call(
        body, name=name, grid=(gm, gn, nk), in_specs=in_specs, out_specs=o_spec,
        out_shape=_S(o_shape, out_dtype), scratch_shapes=[pltpu.VMEM((tm, tn), f32)],
        compiler_params=_cparams(("parallel", "parallel", "arbitrary")),
    )(*ins)
    return out.reshape(M, N) if o_seg else out


def _row_tile(T, widths):
    budget = 6 * 1024 * 1024
    tb = max(16, budget // (4 * sum(widths)))
    return _pick(T, tb, align=16)


def _rows(fn, rows, params, outs, *, name):
    T = rows[0].shape[0]
    nr, npar = len(rows), len(params)
    tb = _row_tile(T, [r.shape[1] for r in rows] + [w for w, _ in outs])

    def body(*refs):
        ins = [r[...].astype(f32) for r in refs[: nr + npar]]
        res = fn(*ins)
        for o_ref, r in zip(refs[nr + npar:], res):
            o_ref[...] = r.astype(o_ref.dtype)

    in_specs = [pl.BlockSpec((tb, r.shape[1]), lambda i: (i, 0)) for r in rows]
    in_specs += [pl.BlockSpec(p.shape, lambda i: (0, 0)) for p in params]
    out_specs = [pl.BlockSpec((tb, w), lambda i: (i, 0)) for w, _ in outs]
    res = pl.pallas_call(
        body, name=name, grid=(T // tb,), in_specs=in_specs, out_specs=out_specs,
        out_shape=[_S((T, w), d) for w, d in outs], compiler_params=_cparams(("parallel",)),
    )(*rows, *params)
    return tuple(res)


def _rows_bwd(fn, rows, params, cots, *, name, want_rows, row_dtypes=None, adds=None):
    T = rows[0].shape[0]
    nr, npar, nc = len(rows), len(params), len(cots)
    adds = adds or {}
    add_idx = sorted(adds)
    row_dtypes = row_dtypes or {}
    widths = [r.shape[1] for r in rows] + [c.shape[1] for c in cots] + [rows[i].shape[1] for i in want_rows]
    tb = _row_tile(T, widths)

    def body(*refs):
        ins = [r[...].astype(f32) for r in refs[: nr + npar]]
        cot = tuple(r[...].astype(f32) for r in refs[nr + npar: nr + npar + nc])
        add_refs = refs[nr + npar + nc: nr + npar + nc + len(add_idx)]
        out_refs = refs[nr + npar + nc + len(add_idx):]
        _, vjp = jax.vjp(lambda *a: tuple(fn(*a)), *ins)
        g = vjp(cot)
        for o_ref, i in zip(out_refs[: len(want_rows)], want_rows):
            r = g[i]
            if i in adds:
                r = r + add_refs[add_idx.index(i)][...].astype(f32)
            o_ref[...] = r.astype(o_ref.dtype)
        first = pl.program_id(0) == 0
        for o_ref, gp in zip(out_refs[len(want_rows):], g[nr:]):
            @pl.when(first)
            def _(o_ref=o_ref):
                o_ref[...] = jnp.zeros_like(o_ref)

            o_ref[...] += gp

    in_specs = [pl.BlockSpec((tb, r.shape[1]), lambda i: (i, 0)) for r in rows]
    in_specs += [pl.BlockSpec(p.shape, lambda i: (0, 0)) for p in params]
    in_specs += [pl.BlockSpec((tb, c.shape[1]), lambda i: (i, 0)) for c in cots]
    in_specs += [pl.BlockSpec((tb, adds[i].shape[1]), lambda i_: (i_, 0)) for i in add_idx]
    out_specs = [pl.BlockSpec((tb, rows[i].shape[1]), lambda i_: (i_, 0)) for i in want_rows]
    out_specs += [pl.BlockSpec(p.shape, lambda i: (0, 0)) for p in params]
    out_shape = [_S(rows[i].shape, row_dtypes.get(i, f32)) for i in want_rows] + [_S(p.shape, f32) for p in params]
    res = pl.pallas_call(
        body, name=name, grid=(T // tb,), in_specs=in_specs, out_specs=out_specs, out_shape=out_shape,
        compiler_params=_cparams(("arbitrary",)),
    )(*rows, *params, *cots, *[adds[i] for i in add_idx])
    return list(res[: len(want_rows)]), list(res[len(want_rows):])


def _f_rmsnorm(x, g):
    return (x * lax.rsqrt(jnp.mean(x * x, axis=-1, keepdims=True) + EPS) * g,)


def _f_swiglu(ab):
    F = ab.shape[1] // 2
    return (jax.nn.silu(ab[:, :F]) * ab[:, F:],)


def _f_s5_post(y, u, d):
    return (jax.nn.gelu(y + d * u),)


def _f_glu(g, v, b):
    return (g * jax.nn.sigmoid(v + b),)


def _f_gated_norm(y, z, w):
    return _f_rmsnorm(y * jax.nn.silu(z), w)


def _f_merge(gl, p5, pssd, b):
    D = p5.shape[1]
    gates = jax.nn.sigmoid(gl + b)
    return (gates[:, :D] * p5 + gates[:, D:] * pssd,)


def _f_dt(dtr, bias, a_log):
    dt = jax.nn.softplus(dtr + bias)
    return dt, dt * (-jnp.exp(a_log))


def _loss_stage(x, tgt, g, *, name):
    T, D = x.shape
    tb = _row_tile(T, [D, D, D])

    def f(xb, gb, tb_):
        y = _f_rmsnorm(xb, gb)[0]
        return 0.5 * jnp.sum(jnp.mean(jnp.square(y - tb_), axis=-1, keepdims=True), axis=0, keepdims=True)

    def body(x_ref, t_ref, g_ref, l_ref, dx_ref, dg_ref):
        tv = t_ref[...]
        val, vjp = jax.vjp(lambda a, b: f(a, b, tv), x_ref[...], g_ref[...])
        dx, dg = vjp(jnp.ones((1, 1), f32))
        dx_ref[...] = dx

        @pl.when(pl.program_id(0) == 0)
        def _():
            l_ref[...] = jnp.zeros_like(l_ref)
            dg_ref[...] = jnp.zeros_like(dg_ref)

        l_ref[...] += jnp.broadcast_to(val, l_ref.shape)
        dg_ref[...] += dg

    row = pl.BlockSpec((tb, D), lambda i: (i, 0))
    par = pl.BlockSpec((1, D), lambda i: (0, 0))
    return pl.pallas_call(
        body, name=name, grid=(T // tb,), in_specs=[row, row, par],
        out_specs=[pl.BlockSpec((1, LANES), lambda i: (0, 0)), row, par],
        out_shape=[_S((1, LANES), f32), _S((T, D), f32), _S((1, D), f32)], compiler_params=_cparams(("arbitrary",)),
    )(x, tgt, g)


def _shift_down(x, s):
    if s == 0:
        return x
    t = lax.broadcasted_iota(jnp.int32, x.shape, 0)
    return jnp.where(t >= s, pltpu.roll(x, s, axis=0), 0.0)


def _shift_up(x, s):
    if s == 0:
        return x
    T = x.shape[0]
    t = lax.broadcasted_iota(jnp.int32, x.shape, 0)
    return jnp.where(t < T - s, pltpu.roll(x, T - s, axis=0), 0.0)


def _conv_pre(x, w, b):
    pre = b
    for k in range(CONV_K):
        pre = pre + w[k:k + 1, :] * _shift_down(x, CONV_K - 1 - k)
    return pre


def _conv_fwd(x, w, b, *, name):
    T, C = x.shape
    cb = _pick(C, 256)

    def body(x_ref, w_ref, b_ref, o_ref):
        o_ref[...] = jax.nn.silu(_conv_pre(x_ref[...], w_ref[...], b_ref[...]))

    col = pl.BlockSpec((T, cb), lambda j: (0, j))
    return pl.pallas_call(
        body, name=name, grid=(C // cb,), in_specs=[col, pl.BlockSpec((CONV_K, cb), lambda j: (0, j)), pl.BlockSpec((1, cb), lambda j: (0, j))],
        out_specs=col, out_shape=_S((T, C), f32), compiler_params=_cparams(("parallel",)),
    )(x, w, b)


def _conv_bwd(x, w, b, dy, *, name):
    T, C = x.shape
    cb = _pick(C, 256)

    def body(x_ref, w_ref, b_ref, dy_ref, dx_ref, dw_ref, db_ref):
        xv, wv = x_ref[...], w_ref[...]
        pre = _conv_pre(xv, wv, b_ref[...])
        sg = jax.nn.sigmoid(pre)
        dpre = dy_ref[...] * sg * (1.0 + pre * (1.0 - sg))
        dx = jnp.zeros_like(xv)
        for k in range(CONV_K):
            s = CONV_K - 1 - k
            dx = dx + wv[k:k + 1, :] * _shift_up(dpre, s)
            dw_ref[k:k + 1, :] = jnp.sum(dpre * _shift_down(xv, s), axis=0, keepdims=True)
        dx_ref[...] = dx
        db_ref[...] = jnp.sum(dpre, axis=0, keepdims=True)

    col = pl.BlockSpec((T, cb), lambda j: (0, j))
    wsp = pl.BlockSpec((CONV_K, cb), lambda j: (0, j))
    bsp = pl.BlockSpec((1, cb), lambda j: (0, j))
    return pl.pallas_call(
        body, name=name, grid=(C // cb,), in_specs=[col, wsp, bsp, col], out_specs=[col, wsp, bsp],
        out_shape=[_S((T, C), f32), _S((CONV_K, C), f32), _S((1, C), f32)], compiler_params=_cparams(("parallel",)),
    )(x, w, b, dy)


def _f_s5_prep(lr, li, ldt, lrb, lib, ldtb, brt, bit):
    def disc(lr_, li_, ldt_):
        dt = jnp.exp(ldt_)
        mag = jnp.exp(lr_ * dt)
        ar, ai = mag * jnp.cos(li_ * dt), mag * jnp.sin(li_ * dt)
        den = lr_ * lr_ + li_ * li_
        cr = ((ar - 1.0) * lr_ + ai * li_) / den
        ci = (ai * lr_ - (ar - 1.0) * li_) / den
        return ar, ai, cr, ci

    ar, ai, _, _ = disc(lr, li, ldt)
    _, _, cr, ci = disc(lrb, lib, ldtb)
    return ar, ai, cr * brt - ci * bit, cr * bit + ci * brt


def _s5_prep(args, *, name):
    G, N = args[0].shape
    GM = args[3].shape[0]

    def body(*refs):
        res = _f_s5_prep(*[r[...] for r in refs[:8]])
        for o, r in zip(refs[8:], res):
            o[...] = r

    return pl.pallas_call(body, name=name, out_shape=[_S((G, N), f32)] * 2 + [_S((GM, N), f32)] * 2)(*args)


def _s5_prep_bwd(args, cots, rsum, *, name):
    G, N = args[0].shape
    GM = args[3].shape[0]

    def body(*refs):
        ins = [r[...] for r in refs[:8]]
        cot = tuple(r[...] for r in refs[8:12])
        rs = refs[12][...]
        _, vjp = jax.vjp(_f_s5_prep, *ins)
        g = vjp(cot)
        fold = lambda v: jnp.dot(rs, v, preferred_element_type=f32, precision=lax.Precision.HIGHEST)
        o = refs[13:]
        o[0][...] = g[0] + fold(g[3])
        o[1][...] = g[1] + fold(g[4])
        o[2][...] = g[2] + fold(jnp.broadcast_to(g[5], (GM, LANES)))[:, 0:1]
        o[3][...] = g[6]
        o[4][...] = g[7]

    return pl.pallas_call(
        body, name=name, out_shape=[_S((G, N), f32), _S((G, N), f32), _S((G, 1), f32), _S((GM, N), f32), _S((GM, N), f32)],
    )(*args, *cots, rsum)


S5_TC = 512


def _s5_local_scan(src, w_r, w_i, a_r, a_i, *, reverse, name):
    T, C = src.shape
    nblk, cb, sb = w_r.shape
    NS = nblk * sb
    tc = min(S5_TC, T)
    nT, nt = T // tc, tc // NSEG
    tmap = (lambda i: nT - 1 - i) if reverse else (lambda i: i)

    def body(u_ref, wr_ref, wi_ref, ar_ref, ai_ref, sr_ref, si_ref, pr_ref, pi_ref, st_r, st_i, pw_r, pw_i):
        @pl.when(pl.program_id(1) == 0)
        def _():
            st_r[...] = jnp.zeros_like(st_r)
            st_i[...] = jnp.zeros_like(st_i)
            pw_r[...] = jnp.ones_like(pw_r)
            pw_i[...] = jnp.zeros_like(pw_i)

        u = u_ref[...].astype(bf16)
        sr_ref[...] = jnp.dot(u, wr_ref[...], preferred_element_type=f32)
        si_ref[...] = jnp.dot(u, wi_ref[...], preferred_element_type=f32)
        ar = jnp.broadcast_to(ar_ref[...], (NSEG, sb))
        ai = jnp.broadcast_to(ai_ref[...], (NSEG, sb))

        def step(k, c):
            cr, ci, qr, qi = c
            kk = (nt - 1 - k) if reverse else k
            rows = pl.ds(pl.multiple_of(kk * NSEG, NSEG), NSEG)
            nr = ar * cr - ai * ci + sr_ref[rows, :]
            ni = ar * ci + ai * cr + si_ref[rows, :]
            sr_ref[rows, :] = nr
            si_ref[rows, :] = ni
            return nr, ni, ar * qr - ai * qi, ar * qi + ai * qr

        cr, ci, qr, qi = lax.fori_loop(0, nt, step, (st_r[...], st_i[...], pw_r[...], pw_i[...]), unroll=8)
        st_r[...], st_i[...], pw_r[...], pw_i[...] = cr, ci, qr, qi
        pr_ref[...] = qr
        pi_ref[...] = qi

    blk = pl.BlockSpec((tc, sb), lambda j, i: (tmap(i), j))
    wsp = pl.BlockSpec((None, cb, sb), lambda j, i: (j, 0, 0))
    asp = pl.BlockSpec((1, sb), lambda j, i: (0, j))
    psp = pl.BlockSpec((NSEG, sb), lambda j, i: (0, j))
    return pl.pallas_call(
        body, name=name, grid=(nblk, nT), in_specs=[pl.BlockSpec((tc, cb), lambda j, i: (tmap(i), j)), wsp, wsp, asp, asp],
        out_specs=[blk, blk, psp, psp], out_shape=[_S((T, NS), f32)] * 2 + [_S((NSEG, NS), f32)] * 2,
        scratch_shapes=[pltpu.VMEM((NSEG, sb), f32)] * 4, compiler_params=_cparams(("parallel", "arbitrary")),
    )(src, w_r, w_i, a_r, a_i)


def _s5_carry(e_r, e_i, p_r, p_i, *, reverse, name):
    NS = e_r.shape[1]

    def body(er_ref, ei_ref, pr_ref, pi_ref, cr_ref, ci_ref):
        ar, ai = pr_ref[0:1, :], pi_ref[0:1, :]
        cr = jnp.zeros((1, NS), f32)
        ci = jnp.zeros((1, NS), f32)
        order = list(range(NSEG - 1, -1, -1)) if reverse else list(range(NSEG))
        cr_ref[order[0]:order[0] + 1, :] = cr
        ci_ref[order[0]:order[0] + 1, :] = ci
        for prev, q in zip(order[:-1], order[1:]):
            er, ei = er_ref[prev:prev + 1, :], ei_ref[prev:prev + 1, :]
            cr, ci = er + ar * cr - ai * ci, ei + ar * ci + ai * cr
            cr_ref[q:q + 1, :] = cr
            ci_ref[q:q + 1, :] = ci

    return pl.pallas_call(body, name=name, out_shape=[_S((NSEG, NS), f32)] * 2)(e_r, e_i, p_r, p_i)


def _s5_fix_out(sl_r, sl_i, a_r, a_i, c_r, c_i, wc_r, wc_i, *, name):
    T, NS = sl_r.shape
    nblk, sb, cb = wc_r.shape
    tc = min(S5_TC, T)
    nT, nt = T // tc, tc // NSEG

    def body(lr_ref, li_ref, ar_ref, ai_ref, cr_ref, ci_ref, wr_ref, wi_ref, sr_ref, si_ref, y_ref, pw_r, pw_i):
        @pl.when(pl.program_id(1) == 0)
        def _():
            pw_r[...] = jnp.ones_like(pw_r)
            pw_i[...] = jnp.zeros_like(pw_i)

        ar = jnp.broadcast_to(ar_ref[...], (NSEG, sb))
        ai = jnp.broadcast_to(ai_ref[...], (NSEG, sb))
        cr, ci = cr_ref[...], ci_ref[...]

        def step(k, c):
            qr, qi = c
            qr, qi = ar * qr - ai * qi, ar * qi + ai * qr
            rows = pl.ds(pl.multiple_of(k * NSEG, NSEG), NSEG)
            sr_ref[rows, :] = lr_ref[rows, :] + qr * cr - qi * ci
            si_ref[rows, :] = li_ref[rows, :] + qr * ci + qi * cr
            return qr, qi

        qr, qi = lax.fori_loop(0, nt, step, (pw_r[...], pw_i[...]), unroll=8)
        pw_r[...], pw_i[...] = qr, qi
        y_ref[...] = (jnp.dot(sr_ref[...].astype(bf16), wr_ref[...], preferred_element_type=f32)
                      - jnp.dot(si_ref[...].astype(bf16), wi_ref[...], preferred_element_type=f32))

    blk = pl.BlockSpec((tc, sb), lambda j, i: (i, j))
    asp = pl.BlockSpec((1, sb), lambda j, i: (0, j))
    csp = pl.BlockSpec((NSEG, sb), lambda j, i: (0, j))
    wsp = pl.BlockSpec((None, sb, cb), lambda j, i: (j, 0, 0))
    return pl.pallas_call(
        body, name=name, grid=(nblk, nT), in_specs=[blk, blk, asp, asp, csp, csp, wsp, wsp],
        out_specs=[blk, blk, pl.BlockSpec((tc, cb), lambda j, i: (i, j))],
        out_shape=[_S((T, NS), f32)] * 2 + [_S((T, nblk * cb), f32)],
        scratch_shapes=[pltpu.VMEM((NSEG, sb), f32)] * 2, compiler_params=_cparams(("parallel", "arbitrary")),
    )(sl_r, sl_i, a_r, a_i, c_r, c_i, wc_r, wc_i)


def _s5_fix_bwd(ql_r, ql_i, ab_r, ab_i, c_r, c_i, s_r, s_i, sb_r, sb_i, u, dy, du_add, w_r, w_i, *, name):
    T, NS = ql_r.shape
    nblk, cb, sb = w_r.shape
    tc = min(S5_TC, T)
    nT, nt = T // tc, tc // NSEG
    tmap = lambda i: nT - 1 - i

    def body(lr_ref, li_ref, ar_ref, ai_ref, cr_ref, ci_ref, sr_ref, si_ref, br_ref, bi_ref, u_ref, dy_ref, dua_ref, wr_ref, wi_ref,
             du_ref, dwr_ref, dwi_ref, dcr_ref, dci_ref, dar_ref, dai_ref, pw_r, pw_i, ac_r, ac_i, q_r, q_i):
        first = pl.program_id(1) == 0

        @pl.when(first)
        def _():
            pw_r[...] = jnp.ones_like(pw_r)
            pw_i[...] = jnp.zeros_like(pw_i)
            ac_r[...] = jnp.zeros_like(ac_r)
            ac_i[...] = jnp.zeros_like(ac_i)
            dwr_ref[...] = jnp.zeros_like(dwr_ref)
            dwi_ref[...] = jnp.zeros_like(dwi_ref)
            dcr_ref[...] = jnp.zeros_like(dcr_ref)
            dci_ref[...] = jnp.zeros_like(dci_ref)

        ar = jnp.broadcast_to(ar_ref[...], (NSEG, sb))
        ai = jnp.broadcast_to(ai_ref[...], (NSEG, sb))
        cr, ci = cr_ref[...], ci_ref[...]

        def fix(rows, qr, qi, spr, spi, accr, acci):
            qr, qi = ar * qr - ai * qi, ar * qi + ai * qr
            xr = lr_ref[rows, :] + qr * cr - qi * ci
            xi = li_ref[rows, :] + qr * ci + qi * cr
            q_r[rows, :] = xr
            q_i[rows, :] = xi
            return qr, qi, accr + xr * spr + xi * spi, acci + xi * spr - xr * spi

        def step(k, c):
            qr, qi, accr, acci = c
            kk = nt - 1 - k
            rows = pl.ds(pl.multiple_of(kk * NSEG, NSEG), NSEG)
            prev = pl.ds(pl.multiple_of((kk - 1) * NSEG, NSEG), NSEG)
            return fix(rows, qr, qi, sr_ref[prev, :], si_ref[prev, :], accr, acci)

        c = lax.fori_loop(0, nt - 1, step, (pw_r[...], pw_i[...], ac_r[...], ac_i[...]), unroll=7)
        qr, qi, accr, acci = fix(pl.ds(0, NSEG), *c[:2], br_ref[...], bi_ref[...], *c[2:])
        pw_r[...], pw_i[...], ac_r[...], ac_i[...] = qr, qi, accr, acci

        qrb, qib = q_r[...].astype(bf16), q_i[...].astype(bf16)
        nt_dims = (((1,), (1,)), ((), ()))
        tn_dims = (((0,), (0,)), ((), ()))
        du_ref[...] = (dua_ref[...] + lax.dot_general(qrb, wr_ref[...], nt_dims, preferred_element_type=f32)
                       + lax.dot_general(qib, wi_ref[...], nt_dims, preferred_element_type=f32))
        ub = u_ref[...].astype(bf16)
        dwr_ref[...] += lax.dot_general(ub, qrb, tn_dims, preferred_element_type=f32)
        dwi_ref[...] += lax.dot_general(ub, qib, tn_dims, preferred_element_type=f32)
        dyb = dy_ref[...].astype(bf16)
        dcr_ref[...] += lax.dot_general(sr_ref[...].astype(bf16), dyb, tn_dims, preferred_element_type=f32)
        dci_ref[...] -= lax.dot_general(si_ref[...].astype(bf16), dyb, tn_dims, preferred_element_type=f32)

        @pl.when(pl.program_id(1) == nT - 1)
        def _():
            dar_ref[...] = jnp.sum(accr, axis=0, keepdims=True)
            dai_ref[...] = jnp.sum(acci, axis=0, keepdims=True)

    blk = pl.BlockSpec((tc, sb), lambda j, i: (tmap(i), j))
    asp = pl.BlockSpec((1, sb), lambda j, i: (0, j))
    csp = pl.BlockSpec((NSEG, sb), lambda j, i: (0, j))
    bsp = pl.BlockSpec((None, NSEG, sb), lambda j, i: (tmap(i), 0, j))
    chn = pl.BlockSpec((tc, cb), lambda j, i: (tmap(i), j))
    wsp = pl.BlockSpec((None, cb, sb), lambda j, i: (j, 0, 0))
    wcs = pl.BlockSpec((None, sb, cb), lambda j, i: (j, 0, 0))
    return pl.pallas_call(
        body, name=name, grid=(nblk, nT), in_specs=[blk, blk, asp, asp, csp, csp, blk, blk, bsp, bsp, chn, chn, chn, wsp, wsp],
        out_specs=[chn, wsp, wsp, wcs, wcs, asp, asp],
        out_shape=[_S((T, nblk * cb), f32), _S((nblk, cb, sb), f32), _S((nblk, cb, sb), f32), _S((nblk, sb, cb), f32),
                   _S((nblk, sb, cb), f32), _S((1, NS), f32), _S((1, NS), f32)],
        scratch_shapes=[pltpu.VMEM((NSEG, sb), f32)] * 4 + [pltpu.VMEM((tc, sb), f32)] * 2,
        compiler_params=_cparams(("parallel", "arbitrary")),
    )(ql_r, ql_i, ab_r, ab_i, c_r, c_i, s_r, s_i, sb_r, sb_i, u, dy, du_add, w_r, w_i)


SSD_TB = 256


def _dotf(a, b, dims):
    return lax.dot_general(a.astype(bf16), b.astype(bf16), dims, preferred_element_type=f32)


_NN = (((1,), (0,)), ((), ()))
_NT = (((1,), (1,)), ((), ()))
_TN = (((0,), (0,)), ((), ()))


def _f_ssd_block(x, bm, cm, dtc, dac, dtr, dar, dh, h):
    tb = x.shape[0]
    hpg = dtc.shape[1]
    ii = lax.broadcasted_iota(jnp.int32, (CHUNK, CHUNK), 0)
    jj = lax.broadcasted_iota(jnp.int32, (CHUNK, CHUNK), 1)
    tril = (ii >= jj)
    trilf = tril.astype(f32)
    triuf = (ii <= jj).astype(f32)
    hi = lax.Precision.HIGHEST
    ys = []
    for c in range(tb // CHUNK):
        r0 = c * CHUNK
        xc, bc, cc = x[r0:r0 + CHUNK], bm[r0:r0 + CHUNK], cm[r0:r0 + CHUNK]
        acc = jnp.dot(trilf, dac[r0:r0 + CHUNK], preferred_element_type=f32, precision=hi)
        acr = jnp.dot(dar[c], triuf, preferred_element_type=f32, precision=hi)
        cb = _dotf(cc, bc, _NT)
        yk, hk = [], []
        for k in range(hpg):
            xk = xc[:, k * HEADDIM:(k + 1) * HEADDIM]
            hp = h[k * HEADDIM:(k + 1) * HEADDIM]
            a_col = acc[:, k:k + 1]
            seg = a_col - acr[k:k + 1, :]
            lmat = jnp.where(tril, jnp.exp(jnp.where(tril, seg, 0.0)), 0.0)
            w = cb * lmat * dtr[c][k:k + 1, :]
            y = _dotf(w, xk, _NN) + _dotf(cc, hp, _NT) * jnp.exp(a_col) + dh[:, k:k + 1] * xk
            a_last = acc[CHUNK - 1:CHUNK, k:k + 1]
            xw = xk * (jnp.exp(a_last - a_col) * dtc[r0:r0 + CHUNK, k:k + 1])
            hk.append(jnp.exp(a_last) * hp + _dotf(xw, bc, _TN))
            yk.append(y)
        ys.append(jnp.concatenate(yk, axis=1))
        h = jnp.concatenate(hk, axis=0)
    return jnp.concatenate(ys, axis=0), h


def _ssd_specs(G, hpg, tb, tmap, b_off=0, c_off=0):
    gw = hpg * HEADDIM
    ncb = tb // CHUNK
    xsp = pl.BlockSpec((tb, gw), lambda g, i: (tmap(i), g))
    bsp = pl.BlockSpec((tb, SSD_STATE), lambda g, i: (tmap(i), b_off + g))
    osp = pl.BlockSpec((tb, SSD_STATE), lambda g, i: (tmap(i), c_off + g))
    csp = pl.BlockSpec((None, tb, hpg), lambda g, i: (g, tmap(i), 0))
    rsp = pl.BlockSpec((None, ncb, hpg, CHUNK), lambda g, i: (g, tmap(i), 0, 0))
    dsp = pl.BlockSpec((None, 1, hpg), lambda g, i: (g, 0, 0))
    hsp = pl.BlockSpec((None, None, gw, SSD_STATE), lambda g, i: (g, tmap(i), 0, 0))
    return xsp, bsp, osp, csp, rsp, dsp, hsp


def _ssd_fwd(xc, dtc, dac, dtr, dar, dh, *, d_inner, name):
    T = xc.shape[0]
    G, _, hpg = dtc.shape
    gw = hpg * HEADDIM
    tb = min(SSD_TB, T)
    nb = T // tb
    b_off = d_inner // SSD_STATE
    xsp, bsp, osp, csp, rsp, dsp, hsp = _ssd_specs(G, hpg, tb, lambda i: i, b_off, b_off + G)

    def body(x_ref, b_ref, c_ref, dtc_ref, dac_ref, dtr_ref, dar_ref, dh_ref, y_ref, hs_ref, h_scr):
        @pl.when(pl.program_id(1) == 0)
        def _():
            h_scr[...] = jnp.zeros_like(h_scr)

        h = h_scr[...]
        hs_ref[...] = h
        y, hn = _f_ssd_block(x_ref[...], b_ref[...], c_ref[...], dtc_ref[...], dac_ref[...], dtr_ref[...], dar_ref[...], dh_ref[...], h)
        y_ref[...] = y
        h_scr[...] = hn

    return pl.pallas_call(
        body, name=name, grid=(G, nb), in_specs=[xsp, bsp, osp, csp, csp, rsp, rsp, dsp], out_specs=[xsp, hsp],
        out_shape=[_S((T, G * gw), f32), _S((G, nb, gw, SSD_STATE), f32)], scratch_shapes=[pltpu.VMEM((gw, SSD_STATE), f32)],
        compiler_params=_cparams(("parallel", "arbitrary")),
    )(xc, xc, xc, dtc, dac, dtr, dar, dh)


def _ssd_bwd(xc, dtc, dac, dtr, dar, dh, hs, dy, *, d_inner, name):
    T = xc.shape[0]
    G, _, hpg = dtc.shape
    gw = hpg * HEADDIM
    tb = min(SSD_TB, T)
    nb = T // tb
    b_off = d_inner // SSD_STATE
    xsp, bsp, osp, csp, rsp, dsp, hsp = _ssd_specs(G, hpg, tb, lambda i: nb - 1 - i, b_off, b_off + G)
    _, gsp, _, _, _, _, _ = _ssd_specs(G, hpg, tb, lambda i: nb - 1 - i)

    def body(x_ref, b_ref, c_ref, dtc_ref, dac_ref, dtr_ref, dar_ref, dh_ref, hs_ref, dy_ref,
             dx_ref, db_ref, dc_ref, ddtc_ref, ddac_ref, ddtr_ref, ddar_ref, ddh_ref, g_scr):
        first = pl.program_id(1) == 0

        @pl.when(first)
        def _():
            g_scr[...] = jnp.zeros_like(g_scr)
            ddh_ref[...] = jnp.zeros_like(ddh_ref)

        ins = [r[...] for r in (x_ref, b_ref, c_ref, dtc_ref, dac_ref, dtr_ref, dar_ref, dh_ref, hs_ref)]
        _, vjp = jax.vjp(_f_ssd_block, *ins)
        g = vjp((dy_ref[...], g_scr[...]))
        dx_ref[...], db_ref[...], dc_ref[...] = g[0], g[1], g[2]
        ddtc_ref[...], ddac_ref[...], ddtr_ref[...], ddar_ref[...] = g[3], g[4], g[5], g[6]
        ddh_ref[...] += g[7]
        g_scr[...] = g[8]

    return pl.pallas_call(
        body, name=name, grid=(G, nb), in_specs=[xsp, bsp, osp, csp, csp, rsp, rsp, dsp, hsp, xsp],
        out_specs=[xsp, gsp, gsp, csp, csp, rsp, rsp, dsp],
        out_shape=[_S((T, G * gw), f32), _S((T, G * SSD_STATE), f32), _S((T, G * SSD_STATE), f32), _S(dtc.shape, f32), _S(dtc.shape, f32),
                   _S(dtr.shape, f32), _S(dtr.shape, f32), _S((G, 1, hpg), f32)],
        scratch_shapes=[pltpu.VMEM((gw, SSD_STATE), f32)], compiler_params=_cparams(("parallel", "arbitrary")),
    )(xc, xc, xc, dtc, dac, dtr, dar, dh, hs, dy)


def _peers():
    x, y, c = lax.axis_index("x"), lax.axis_index("y"), lax.axis_index("c")
    return x, y, c


_ANY = pl.BlockSpec(memory_space=pl.ANY)
N_CHIP = N_DEV // 2


def _all_gather(shards, *, name):
    n = len(shards)

    def body(*refs):
        x_refs, out_refs = refs[:n], refs[n:2 * n]
        send_sems, recv_sems, local_sems = refs[2 * n:]
        x, y, c = _peers()
        me, sibling = (x, y, c), (x, y, 1 - c)
        chips = [(1 - x, y), (x, 1 - y), (1 - x, 1 - y)]

        def copy(a, r, block, to, src=None):
            px, py, pc = block
            slot = out_refs[a].at[4 * px + 2 * py + pc]
            return pltpu.make_async_remote_copy(
                src_ref=slot if src is None else src, dst_ref=slot, send_sem=send_sems.at[7 * a + r],
                recv_sem=recv_sems.at[7 * a + r], device_id=to, device_id_type=MESH)

        mine = [pltpu.make_async_copy(x_refs[a], out_refs[a].at[4 * x + 2 * y + c], local_sems.at[a]) for a in range(n)]
        for cp in mine:
            cp.start()
        first = []
        for a in range(n):
            first.append(copy(a, 0, me, sibling, src=x_refs[a]))
            first += [copy(a, 1 + j, me, (*chip, c), src=x_refs[a]) for j, chip in enumerate(chips)]
        for cp in first:
            cp.start()
        passed = []
        for j, chip in enumerate(chips):
            for a in range(n):
                copy(a, 1 + j, (*chip, c), me).wait_recv()
                fwd = copy(a, 4 + j, (*chip, c), sibling)
                fwd.start()
                passed.append(fwd)
        for a in range(n):
            copy(a, 0, sibling, me).wait_recv()
        for j, chip in enumerate(chips):
            for a in range(n):
                copy(a, 4 + j, (*chip, 1 - c), me).wait_recv()
        for cp in first + passed:
            cp.wait_send()
        for cp in mine:
            cp.wait()

    return pl.pallas_call(
        body, name=name, out_shape=[_S((N_DEV,) + s.shape, s.dtype) for s in shards], in_specs=[_ANY] * n, out_specs=[_ANY] * n,
        scratch_shapes=[pltpu.SemaphoreType.DMA((7 * n,)), pltpu.SemaphoreType.DMA((7 * n,)), pltpu.SemaphoreType.DMA((n,))],
    )(*shards)


def _exchange_sibling(slots, *, name):
    n = len(slots)

    def body(*refs):
        x_refs, own_refs, sib_refs = refs[:n], refs[n:2 * n], refs[2 * n:3 * n]
        send_sems, recv_sems, local_sems = refs[3 * n:]
        x, y, c = _peers()
        keep = [pltpu.make_async_copy(x_refs[a].at[pl.ds(N_CHIP * c, N_CHIP)], own_refs[a], local_sems.at[a]) for a in range(n)]
        give = [pltpu.make_async_remote_copy(
            src_ref=x_refs[a].at[pl.ds(N_CHIP * (1 - c), N_CHIP)], dst_ref=sib_refs[a], send_sem=send_sems.at[a],
            recv_sem=recv_sems.at[a], device_id=(x, y, 1 - c), device_id_type=MESH) for a in range(n)]
        for cp in keep + give:
            cp.start()
        for cp in give:
            cp.wait_recv()
        for cp in give:
            cp.wait_send()
        for cp in keep:
            cp.wait()

    half = [_S((N_CHIP,) + s.shape[1:], s.dtype) for s in slots]
    res = pl.pallas_call(
        body, name=name, out_shape=half + half, in_specs=[_ANY] * n, out_specs=[_ANY] * (2 * n),
        scratch_shapes=[pltpu.SemaphoreType.DMA((n,)), pltpu.SemaphoreType.DMA((n,)), pltpu.SemaphoreType.DMA((n,))],
    )(*slots)
    return list(res[:n]), list(res[n:])


def _exchange_chips(parts, *, name):
    n = len(parts)

    def body(*refs):
        p_refs, out_refs = refs[:n], refs[n:2 * n]
        send_sems, recv_sems, local_sems = refs[2 * n:]
        x, y, c = _peers()
        mychip = 2 * x + y
        mine = [pltpu.make_async_copy(p_refs[a].at[mychip], out_refs[a].at[mychip], local_sems.at[a]) for a in range(n)]
        copies = []
        for j in range(1, N_CHIP):
            tx, ty = x ^ (j >> 1), y ^ (j & 1)
            for a in range(n):
                copies.append(pltpu.make_async_remote_copy(
                    src_ref=p_refs[a].at[2 * tx + ty], dst_ref=out_refs[a].at[mychip], send_sem=send_sems.at[3 * a + j - 1],
                    recv_sem=recv_sems.at[3 * a + j - 1], device_id=(tx, ty, c), device_id_type=MESH))
        for cp in mine + copies:
            cp.start()
        for cp in copies:
            cp.wait_recv()
        for cp in copies:
            cp.wait_send()
        for cp in mine:
            cp.wait()

    return list(pl.pallas_call(
        body, name=name, out_shape=[_S(p.shape, p.dtype) for p in parts], in_specs=[_ANY] * n, out_specs=[_ANY] * n,
        scratch_shapes=[pltpu.SemaphoreType.DMA((3 * n,)), pltpu.SemaphoreType.DMA((3 * n,)), pltpu.SemaphoreType.DMA((n,))],
    )(*parts))


def _sum_slots(stack, *, name):
    n, R, W = stack.shape
    tr = _pick(R, 1024, align=8)

    def body(s_ref, o_ref):
        acc = s_ref[0]
        for k in range(1, n):
            acc = acc + s_ref[k]
        o_ref[...] = acc

    return pl.pallas_call(
        body, name=name, grid=(R // tr,), in_specs=[pl.BlockSpec((n, tr, W), lambda i: (0, i, 0))],
        out_specs=pl.BlockSpec((tr, W), lambda i: (i, 0)), out_shape=_S((R, W), f32), compiler_params=_cparams(("parallel",)),
    )(stack)


def _adamw_math(gv, wv, mv, vv):
    c1 = 1.0 / (1.0 - ADAM_B1 ** ADAM_STEP)
    c2 = 1.0 / (1.0 - ADAM_B2 ** ADAM_STEP)
    nm = ADAM_B1 * mv + (1.0 - ADAM_B1) * gv
    nv = ADAM_B2 * vv + (1.0 - ADAM_B2) * jnp.square(gv)
    return -ADAM_LR * ((nm * c1) / (jnp.sqrt(nv * c2) + ADAM_EPS) + ADAM_WD * wv), nm, nv


def _adamw(g, w, m, v, *, name):
    R, W = w.shape
    tr = _pick(R, max(8, (1 << 20) // (4 * W)), align=8)

    def body(g_ref, w_ref, m_ref, v_ref, d_ref, nm_ref, nv_ref):
        d_ref[...], nm_ref[...], nv_ref[...] = _adamw_math(g_ref[...], w_ref[...], m_ref[...], v_ref[...])

    sp = pl.BlockSpec((tr, W), lambda i: (i, 0))
    return pl.pallas_call(
        body, name=name, grid=(R // tr,), in_specs=[sp] * 4, out_specs=[sp] * 3, out_shape=[_S((R, W), f32)] * 3,
        compiler_params=_cparams(("parallel",)),
    )(g, w, m, v)


def _reduce_adamw(parts, w, m, v, *, name):
    n, R, W = parts.shape
    tr = _pick(R, max(8, (1 << 20) // (4 * W)), align=16)

    def body(p_ref, w_ref, m_ref, v_ref, g_ref, d_ref, nm_ref, nv_ref):
        gv = p_ref[0].astype(f32)
        for k in range(1, n):
            gv = gv + p_ref[k].astype(f32)
        g_ref[...] = gv
        d_ref[...], nm_ref[...], nv_ref[...] = _adamw_math(gv, w_ref[...], m_ref[...], v_ref[...])

    sp = pl.BlockSpec((tr, W), lambda i: (i, 0))
    return pl.pallas_call(
        body, name=name, grid=(R // tr,), in_specs=[pl.BlockSpec((n, tr, W), lambda i: (0, i, 0))] + [sp] * 3, out_specs=[sp] * 4,
        out_shape=[_S((R, W), f32)] * 4, compiler_params=_cparams(("parallel",)),
    )(parts, w, m, v)


def _pieces(seg_start, seg_len, shard_w):
    out, col = [], seg_start
    while col < seg_start + seg_len:
        k, a = divmod(col, shard_w)
        n = min(shard_w - a, seg_start + seg_len - col)
        out.append((k, a, col - seg_start, n))
        col += n
    return out


def _unshard_w_in(g, seg_lens, *, name):
    _, D, w = g.shape
    starts = [sum(seg_lens[:i]) for i in range(len(seg_lens))]
    widths = [max(n, LANES) for n in seg_lens]
    tm = _pick(D, 256, align=16)

    def body(g_ref, *o_refs):
        for o_ref, s0, n in zip(o_refs, starts, seg_lens):
            if n < o_ref.shape[1]:
                o_ref[...] = jnp.zeros_like(o_ref)
            for k, a, off, m in _pieces(s0, n, w):
                o_ref[:, off:off + m] = g_ref[k, :, a:a + m]

    return pl.pallas_call(
        body, name=name, grid=(D // tm,), in_specs=[pl.BlockSpec((N_DEV, tm, w), lambda i: (0, i, 0))],
        out_specs=[pl.BlockSpec((tm, wd), lambda i: (i, 0)) for wd in widths], out_shape=[_S((D, wd), g.dtype) for wd in widths],
        compiler_params=_cparams(("parallel",)),
    )(g)


def _reshard_w_in(grads, seg_lens, w, *, name):
    D = grads[0].shape[0]
    starts = [sum(seg_lens[:i]) for i in range(len(seg_lens))]
    tm = _pick(D, 128, align=16)

    def body(*refs):
        o_ref = refs[-1]
        for g_ref, s0, n in zip(refs[:-1], starts, seg_lens):
            for k, a, off, m in _pieces(s0, n, w):
                o_ref[_slot_of(k), :, a:a + m] = g_ref[:, off:off + m].astype(o_ref.dtype)

    return pl.pallas_call(
        body, name=name, grid=(D // tm,), in_specs=[pl.BlockSpec((tm, g.shape[1]), lambda i: (i, 0)) for g in grads],
        out_specs=pl.BlockSpec((N_DEV, tm, w), lambda i: (0, i, 0)), out_shape=_S((N_DEV, D, w), bf16),
        compiler_params=_cparams(("parallel",)),
    )(*grads)


def _pad_flat(a, mult):
    a = a.reshape(-1)
    n = -(-a.shape[0] // mult) * mult
    return a if n == a.shape[0] else jnp.pad(a, (0, n - a.shape[0]))


def _pad_cols(a, mult):
    n = -(-a.shape[1] // mult) * mult
    return a if n == a.shape[1] else jnp.pad(a, ((0, 0), (0, n - a.shape[1])))


def _block_diag(t):
    nblk, g, P, Q = t.shape
    eye = jnp.eye(g, dtype=t.dtype)
    return (t[:, :, :, None, :] * eye[None, :, None, :, None]).reshape(nblk, g * P, g * Q)


def _block_diag_t(w, P, Q):
    nblk = w.shape[0]
    g = w.shape[1] // P
    eye = jnp.eye(g, dtype=w.dtype)
    return (w.reshape(nblk, g, P, g, Q) * eye[None, :, None, :, None]).sum(axis=3)


_COLS = ("ffn1_w_gate", "ffn1_w_up", "ffn2_w_gate", "ffn2_w_up")
_ROWS = ("ffn1_w_down", "ffn2_w_down", "s5_w_glu", "w_proj_s5", "w_out", "w_proj_ssd")
_BIG = _COLS + _ROWS + ("w_in", "conv_w")
_SMALL = ("ffn1_norm", "mix_norm", "conv_b", "s5_A_re", "s5_A_im", "s5_log_dt", "s5_B_re", "s5_B_im", "s5_C_re", "s5_C_im",
          "s5_D", "s5_b_glu", "ssd_A_log", "ssd_dt_bias", "ssd_D", "ssd_norm", "b_gate", "ffn2_norm", "final_norm")
_WEIGHTS = ("ffn1_norm", "ffn1_w_gate", "ffn1_w_up", "ffn1_w_down", "mix_norm", "w_in", "conv_w", "conv_b", "s5_A_re", "s5_A_im",
            "s5_log_dt", "s5_B_re", "s5_B_im", "s5_C_re", "s5_C_im", "s5_D", "s5_w_glu", "s5_b_glu", "ssd_A_log", "ssd_dt_bias",
            "ssd_D", "ssd_norm", "w_proj_s5", "w_proj_ssd", "b_gate", "w_out", "ffn2_norm", "ffn2_w_gate", "ffn2_w_up",
            "ffn2_w_down", "final_norm")
def _f_swiglu2(a, b):
    return (jax.nn.silu(a) * b,)


def _ffn_fwd(x, n, wg, wu, wd, tag):
    T, D = x.shape
    nf = wg.shape[2]
    h = _rows(_f_rmsnorm, [x], [n], [(D, bf16)], name=tag + "_norm")[0]
    a = _mm(h, wg, b_blk="n", o_blk="n", name=tag + "_gate").reshape(N_DEV * T, nf)
    b = _mm(h, wu, b_blk="n", o_blk="n", name=tag + "_up").reshape(N_DEV * T, nf)
    c = _rows(_f_swiglu2, [a, b], [], [(nf, bf16)], name=tag + "_act")[0].reshape(N_DEV, T, nf)
    y = _mm(c, wd, a_blk="k", b_blk="k", scale=0.5, add=x, name=tag + "_down")
    return y, (x, n, h, a, b, c)


def _ffn_bwd(saved, wg, wu, wd, dy, tag):
    x, n, h, a, b, c = saved
    T = x.shape[0]
    nf = wg.shape[2]
    dc = _mm(dy, wd, tb=True, b_blk="n", o_blk="n", scale=0.5, name=tag + "_d_act").reshape(N_DEV * T, nf)
    dwd = _mm(c, dy, ta=True, a_blk="m", o_blk="m", o_slots=True, out_dtype=bf16, scale=0.5, name=tag + "_d_wdown")
    (da, db), _ = _rows_bwd(_f_swiglu2, [a, b], [], [dc], name=tag + "_act_bwd", want_rows=[0, 1], row_dtypes={0: bf16, 1: bf16})
    da, db = da.reshape(N_DEV, T, nf), db.reshape(N_DEV, T, nf)
    dh = _mm(da, wg, tb=True, a_blk="k", b_blk="k", name=tag + "_d_h_gate")
    dh = _mm(db, wu, tb=True, a_blk="k", b_blk="k", add=dh, name=tag + "_d_h_up")
    dwg = _mm(h, da, ta=True, b_blk="n", o_blk="n", o_slots=True, out_dtype=bf16, name=tag + "_d_wgate")
    dwu = _mm(h, db, ta=True, b_blk="n", o_blk="n", o_slots=True, out_dtype=bf16, name=tag + "_d_wup")
    (dx,), (dn,) = _rows_bwd(_f_rmsnorm, [x], [n], [dh], name=tag + "_norm_bwd", want_rows=[0], adds={0: dy})
    return dx, dn, dwg, dwu, dwd


def kernel(x, ffn1_norm, ffn1_w_gate, ffn1_w_up, ffn1_w_down, mix_norm, w_in, conv_w, conv_b, s5_A_re, s5_A_im, s5_log_dt, s5_B_re, s5_B_im, s5_C_re, s5_C_im, s5_D, s5_w_glu, s5_b_glu, ssd_A_log, ssd_dt_bias, ssd_D, ssd_norm, w_proj_s5, w_proj_ssd, b_gate, w_out, ffn2_norm, ffn2_w_gate, ffn2_w_up, ffn2_w_down, final_norm, loss_target, m_ffn1_norm, m_ffn1_w_gate, m_ffn1_w_up, m_ffn1_w_down, m_mix_norm, m_w_in, m_conv_w, m_conv_b, m_s5_A_re, m_s5_A_im, m_s5_log_dt, m_s5_B_re, m_s5_B_im, m_s5_C_re, m_s5_C_im, m_s5_D, m_s5_w_glu, m_s5_b_glu, m_ssd_A_log, m_ssd_dt_bias, m_ssd_D, m_ssd_norm, m_w_proj_s5, m_w_proj_ssd, m_b_gate, m_w_out, m_ffn2_norm, m_ffn2_w_gate, m_ffn2_w_up, m_ffn2_w_down, m_final_norm, v_ffn1_norm, v_ffn1_w_gate, v_ffn1_w_up, v_ffn1_w_down, v_mix_norm, v_w_in, v_conv_w, v_conv_b, v_s5_A_re, v_s5_A_im, v_s5_log_dt, v_s5_B_re, v_s5_B_im, v_s5_C_re, v_s5_C_im, v_s5_D, v_s5_w_glu, v_s5_b_glu, v_ssd_A_log, v_ssd_dt_bias, v_ssd_D, v_ssd_norm, v_w_proj_s5, v_w_proj_ssd, v_b_gate, v_w_out, v_ffn2_norm, v_ffn2_w_gate, v_ffn2_w_up, v_ffn2_w_down, v_final_norm):
    P = dict(locals())
    T, D = x.shape[1], x.shape[2]
    x0, tgt = x[0], loss_target[0]
    sh = {k: P[k][0] for k in _BIG}

    gathered = _all_gather([sh[k] if k == "conv_w" else sh[k].astype(bf16) for k in _BIG], name="gather_weights")
    W = dict(zip(_BIG, gathered))
    whole = lambda k: W[k].reshape(-1, D)
    g_win = W["w_in"]
    conv_w_full = W["conv_w"].transpose(1, 0, 2).reshape(CONV_K, -1)

    d_inner = N_DEV * sh["w_proj_ssd"].shape[0]
    conv_dim = conv_w_full.shape[1]
    H = ssd_A_log.shape[1]
    G = (conv_dim - d_inner) // (2 * SSD_STATE)
    hpg = H // G
    nc = T // CHUNK
    Gs = D // S5_GROUP
    nblk = Gs // S5_GPB
    NS = Gs * S5_STATE
    seg_lens = (D, d_inner, conv_dim, H, 2 * D)
    w_u, w_z, w_xbc, w_dt, w_gl = _unshard_w_in(g_win, seg_lens, name="unshard_w_in")
    w_glu, w_p5, w_pssd, w_o = whole("s5_w_glu"), whole("w_proj_s5"), whole("w_proj_ssd"), whole("w_out")

    ffn1_w = (W["ffn1_w_gate"], W["ffn1_w_up"], W["ffn1_w_down"])
    ffn2_w = (W["ffn2_w_gate"], W["ffn2_w_up"], W["ffn2_w_down"])
    x1, sv1 = _ffn_fwd(x0, ffn1_norm, *ffn1_w, "ffn1")
    h2 = _rows(_f_rmsnorm, [x1], [mix_norm], [(D, bf16)], name="mix_norm")[0]
    u_p = _mm(h2, w_u, o_seg=True, name="in_u")
    z = _mm(h2, w_z, name="in_z")
    xbc = _mm(h2, w_xbc, name="in_xbc")
    gl = _mm(h2, w_gl, name="in_gate")
    dtr = _mm(h2, w_dt, name="in_dt")

    rep = lambda a: jnp.repeat(a, S5_GROUP, axis=0)
    lr, li, ldt = s5_A_re[0], s5_A_im[0], s5_log_dt[0].reshape(Gs, 1)
    brt = s5_B_re[0].transpose(0, 2, 1).reshape(Gs * S5_GROUP, S5_STATE)
    bit = s5_B_im[0].transpose(0, 2, 1).reshape(Gs * S5_GROUP, S5_STATE)
    prep_args = (lr, li, ldt, rep(lr), rep(li), rep(ldt), brt, bit)
    ar, ai, bbrt, bbit = _s5_prep(prep_args, name="s5_prep")
    a_r, a_i = ar.reshape(1, NS), ai.reshape(1, NS)
    wb_r = _block_diag(bbrt.reshape(nblk, S5_GPB, S5_GROUP, S5_STATE)).astype(bf16)
    wb_i = _block_diag(bbit.reshape(nblk, S5_GPB, S5_GROUP, S5_STATE)).astype(bf16)
    c4r = s5_C_re[0].reshape(nblk, S5_GPB, S5_GROUP, S5_STATE).transpose(0, 1, 3, 2)
    c4i = s5_C_im[0].reshape(nblk, S5_GPB, S5_GROUP, S5_STATE).transpose(0, 1, 3, 2)
    wc_r, wc_i = _block_diag(c4r).astype(bf16), _block_diag(c4i).astype(bf16)
    sl_r, sl_i, p_r, p_i = _s5_local_scan(u_p, wb_r, wb_i, a_r, a_i, reverse=False, name="s5_scan")
    c_r, c_i = _s5_carry(sl_r[T - NSEG:], sl_i[T - NSEG:], p_r, p_i, reverse=False, name="s5_carry")
    s_r, s_i, ylin = _s5_fix_out(sl_r, sl_i, a_r, a_i, c_r, c_i, wc_r, wc_i, name="s5_fix_out")
    g5 = _rows(_f_s5_post, [ylin, u_p], [s5_D], [(D, f32)], name="s5_gelu")[0]
    v5 = _mm(g5, w_glu, name="s5_glu_mm")
    o5 = _rows(_f_glu, [g5, v5], [s5_b_glu], [(D, bf16)], name="s5_glu")[0]
    p5 = _mm(o5, w_p5, a_seg=True, name="proj_s5")

    xc = _conv_fwd(xbc, conv_w_full, conv_b, name="conv")
    bias_p, alog_p = _pad_cols(ssd_dt_bias, LANES), _pad_cols(ssd_A_log, LANES)
    dt_p, da_p = _rows(_f_dt, [dtr], [bias_p, alog_p], [(LANES, f32), (LANES, f32)], name="ssd_dt")
    col_l = lambda a: a[:, :H].reshape(T, G, hpg).transpose(1, 0, 2)
    row_l = lambda a: a.reshape(G, nc, CHUNK, hpg).transpose(0, 1, 3, 2)
    dtc, dac = col_l(dt_p), col_l(da_p)
    dtw, daw = row_l(dtc), row_l(dac)
    dh = ssd_D.reshape(G, 1, hpg)
    ssd_in = (xc, dtc, dac, dtw, daw, dh)
    y_ssd, hs = _ssd_fwd(*ssd_in, d_inner=d_inner, name="ssd")
    yn = _rows(_f_gated_norm, [y_ssd, z], [ssd_norm], [(d_inner, bf16)], name="ssd_gated_norm")[0]
    pssd = _mm(yn, w_pssd, name="proj_ssd")

    merged = _rows(_f_merge, [gl, p5, pssd], [b_gate], [(D, bf16)], name="merge")[0]
    x2 = _mm(merged, w_o, add=x1, name="out_proj")
    x3, sv2 = _ffn_fwd(x2, ffn2_norm, *ffn2_w, "ffn2")
    lossv, dx3, d_final = _loss_stage(x3, tgt, final_norm.reshape(1, D), name="loss")

    gw = {}
    gs = {"final_norm": d_final}
    slot_mm = lambda a_, b_, name, **kw: _mm(a_, b_, ta=True, o_blk="m", o_slots=True, out_dtype=bf16, name=name, **kw)
    dx2, gs["ffn2_norm"], gw["ffn2_w_gate"], gw["ffn2_w_up"], gw["ffn2_w_down"] = _ffn_bwd(sv2, *ffn2_w, dx3, "ffn2")

    dmerged = _mm(dx2, w_o, tb=True, name="d_merged")
    gw["w_out"] = slot_mm(merged, dx2, "d_w_out", tm=D // N_DEV)
    (dgl, dp5, dpssd), (gs["b_gate"],) = _rows_bwd(_f_merge, [gl, p5, pssd], [b_gate], [dmerged], name="merge_bwd", want_rows=[0, 1, 2])

    dyn = _mm(dpssd, w_pssd, tb=True, name="d_yn")
    gw["w_proj_ssd"] = slot_mm(yn, dpssd, "d_w_proj_ssd", tm=d_inner // N_DEV)
    (dyssd, dz), (gs["ssd_norm"],) = _rows_bwd(_f_gated_norm, [y_ssd, z], [ssd_norm], [dyn], name="ssd_gated_norm_bwd", want_rows=[0, 1])
    dxs, dbm, dcm, ddtc, ddac, ddtw, ddaw, ddh = _ssd_bwd(*ssd_in, hs, dyssd, d_inner=d_inner, name="ssd_bwd")
    fold = lambda dc_, dw_: _pad_cols((dc_ + dw_.transpose(0, 1, 3, 2).reshape(G, T, hpg)).transpose(1, 0, 2).reshape(T, H), LANES)
    (ddtr,), (dbias_p, dalog_p) = _rows_bwd(_f_dt, [dtr], [bias_p, alog_p], [fold(ddtc, ddtw), fold(ddac, ddaw)], name="ssd_dt_bwd", want_rows=[0])
    gs["ssd_dt_bias"], gs["ssd_A_log"], gs["ssd_D"] = dbias_p[:, :H], dalog_p[:, :H], ddh.reshape(1, H)
    dxbc, d_conv_w, gs["conv_b"] = _conv_bwd(xbc, conv_w_full, conv_b, jnp.concatenate([dxs, dbm, dcm], axis=1), name="conv_bwd")
    cwk = sh["conv_w"].shape[1]
    gw["conv_w"] = d_conv_w.reshape(CONV_K, N_CHIP, 2, cwk).transpose(2, 1, 0, 3).reshape(N_DEV, CONV_K, cwk)

    do5 = _mm(dp5, w_p5, tb=True, o_seg=True, name="d_o5")
    gw["w_proj_s5"] = slot_mm(o5, dp5, "d_w_proj_s5", a_seg=True, tm=D // N_DEV)
    (dg5a, dv5), (gs["s5_b_glu"],) = _rows_bwd(_f_glu, [g5, v5], [s5_b_glu], [do5], name="s5_glu_bwd", want_rows=[0, 1])
    dg5 = _mm(dv5, w_glu, tb=True, add=dg5a, name="d_g5")
    gw["s5_w_glu"] = slot_mm(g5, dv5, "d_w_glu", tm=D // N_DEV)
    (dylin, du_a), (gs["s5_D"],) = _rows_bwd(_f_s5_post, [ylin, u_p], [s5_D], [dg5], name="s5_gelu_bwd", want_rows=[0, 1])
    wct_r, wct_i = wc_r.transpose(0, 2, 1), -wc_i.transpose(0, 2, 1)
    ql_r, ql_i, pb_r, pb_i = _s5_local_scan(dylin, wct_r, wct_i, a_r, -a_i, reverse=True, name="s5_scan_bwd")
    cb_r, cb_i = _s5_carry(ql_r[:NSEG], ql_i[:NSEG], pb_r, pb_i, reverse=True, name="s5_carry_bwd")
    tc = min(S5_TC, T)

    def before_blocks(s):
        last = s.reshape(T // tc, tc, NS)[:, tc - NSEG:, :]
        wrap = jnp.concatenate([jnp.zeros((1, 1, NS), f32), last[-1:, : NSEG - 1, :]], axis=1)
        return jnp.concatenate([wrap, last[:-1]], axis=0)

    du_p, dwb_r, dwb_i, dwc_r, dwc_i, d_ar, d_ai = _s5_fix_bwd(
        ql_r, ql_i, a_r, -a_i, cb_r, cb_i, s_r, s_i, before_blocks(s_r), before_blocks(s_i), u_p, dylin, du_a, wb_r, wb_i, name="s5_fix_bwd")
    unblk = lambda w: _block_diag_t(w, S5_GROUP, S5_STATE).reshape(Gs * S5_GROUP, S5_STATE)
    rsum = jnp.repeat(jnp.eye(Gs, dtype=f32), S5_GROUP, axis=1)
    d_lr, d_li, d_ldt, d_brt, d_bit = _s5_prep_bwd(
        prep_args, (d_ar.reshape(Gs, S5_STATE), d_ai.reshape(Gs, S5_STATE), unblk(dwb_r), unblk(dwb_i)), rsum, name="s5_prep_bwd")
    gs["s5_A_re"], gs["s5_A_im"], gs["s5_log_dt"] = d_lr, d_li, d_ldt.reshape(1, Gs)
    gs["s5_B_re"] = d_brt.reshape(Gs, S5_GROUP, S5_STATE).transpose(0, 2, 1)
    gs["s5_B_im"] = d_bit.reshape(Gs, S5_GROUP, S5_STATE).transpose(0, 2, 1)
    gs["s5_C_re"] = _block_diag_t(dwc_r, S5_STATE, S5_GROUP).transpose(0, 1, 3, 2).reshape(Gs, S5_GROUP, S5_STATE)
    gs["s5_C_im"] = _block_diag_t(dwc_i, S5_STATE, S5_GROUP).transpose(0, 1, 3, 2).reshape(Gs, S5_GROUP, S5_STATE)

    dh2 = _mm(du_p, w_u, tb=True, a_seg=True, name="d_h2_u")
    dh2 = _mm(dz, w_z, tb=True, add=dh2, name="d_h2_z")
    dh2 = _mm(dxbc, w_xbc, tb=True, add=dh2, name="d_h2_xbc")
    dh2 = _mm(dgl, w_gl, tb=True, add=dh2, name="d_h2_gate")
    dh2 = _mm(ddtr, w_dt, tb=True, add=dh2, name="d_h2_dt")
    d_w_in = [_mm(h2, du_p, ta=True, b_seg=True, name="d_w_u"), _mm(h2, dz, ta=True, name="d_w_z"), _mm(h2, dxbc, ta=True, name="d_w_xbc"),
              _mm(h2, ddtr, ta=True, name="d_w_dt"), _mm(h2, dgl, ta=True, name="d_w_gate")]
    gw["w_in"] = _reshard_w_in(d_w_in, seg_lens, sh["w_in"].shape[1], name="reshard_d_w_in")
    (dx1,), (gs["mix_norm"],) = _rows_bwd(_f_rmsnorm, [x1], [mix_norm], [dh2], name="mix_norm_bwd", want_rows=[0], adds={0: dx2})
    dx0, gs["ffn1_norm"], gw["ffn1_w_gate"], gw["ffn1_w_up"], gw["ffn1_w_down"] = _ffn_bwd(sv1, *ffn1_w, dx1, "ffn1")

    own, sib = _exchange_sibling([gw[k] for k in _BIG], name="exchange_grads_sibling")
    chip_sums = []
    for k, o_, s_ in zip(_BIG, own, sib):
        w2 = o_.shape[-1]
        p_ = _rows(lambda a_, b_: (a_ + b_,), [o_.reshape(-1, w2), s_.reshape(-1, w2)], [], [(w2, o_.dtype)], name="chip_sum_" + k)[0]
        chip_sums.append(p_.reshape(o_.shape))
    arrived = dict(zip(_BIG, _exchange_chips(chip_sums, name="exchange_grads_chips")))
    small_shapes = {k: (P[k][0].shape if P[k].ndim > 1 else P[k].shape) for k in _SMALL}
    pack = lambda d: jnp.concatenate([_pad_flat(d[k], TILE_ELEMS) for k in _SMALL]).reshape(-1, LANES)
    gsmall = _sum_slots(_all_gather([pack(gs)], name="gather_small_grads")[0], name="sum_small_grads")
    snum = {k: math.prod(small_shapes[k]) for k in _SMALL}
    ssz = {k: -(-snum[k] // TILE_ELEMS) * TILE_ELEMS for k in _SMALL}

    grads, delta, new_m, new_v = {}, {}, {}, {}
    for k in _BIG:
        grads[k], delta[k], new_m[k], new_v[k] = _reduce_adamw(arrived[k], P[k][0], P["m_" + k][0], P["v_" + k][0], name="adamw_" + k)
    d_s, m_s, v_s = _adamw(gsmall, pack({k: P[k] for k in _SMALL}), pack({k: P["m_" + k] for k in _SMALL}),
                           pack({k: P["v_" + k] for k in _SMALL}), name="adamw_small")
    off = 0
    gflat, dflat, mflat, vflat = gsmall.reshape(-1), d_s.reshape(-1), m_s.reshape(-1), v_s.reshape(-1)
    for k in _SMALL:
        n = snum[k]
        grads[k], delta[k], new_m[k], new_v[k] = (a[off:off + n] for a in (gflat, dflat, mflat, vflat))
        off += ssz[k]

    loss = lax.psum(lossv[0, 0], ("x", "y", "c"))
    out = [loss, dx0.reshape(x.shape)]
    for d in (grads, delta, new_m, new_v):
        out += [d[k].reshape(P[k].shape) for k in _WEIGHTS]
    return tuple(out)
```

```python
import math

import jax
import jax.numpy as jnp
from jax import lax
from jax.experimental import pallas as pl
from jax.experimental.pallas import tpu as pltpu

f32 = jnp.float32
bf16 = jnp.bfloat16
_S = jax.ShapeDtypeStruct

EPS = 1e-6
S5_GROUP = 16
S5_STATE = 64
HEADDIM = 64
SSD_STATE = 128
CHUNK = 64
CONV_K = 4
NSEG = 8
S5_GPB = 8
N_DEV = 8
LANES = 128
TILE_ELEMS = 8 * LANES

ADAM_LR = 0.001
ADAM_B1 = 0.9
ADAM_B2 = 0.999
ADAM_EPS = 1e-08
ADAM_WD = 0.01
ADAM_STEP = 10

VMEM_LIMIT = 56 * 1024 * 1024
MM_FULL_K = 3072
MESH = pl.DeviceIdType.MESH


def _cparams(sem=None):
    return pltpu.CompilerParams(dimension_semantics=sem, vmem_limit_bytes=VMEM_LIMIT)


def _pick(dim, pref, align=LANES):
    best = None
    t = align
    while t <= min(dim, pref):
        if dim % t == 0:
            best = t
        t += align
    return best or dim


def _slot_of(k):
    return (k & 1) * (N_DEV // 2) + (k >> 1)


def _mm(a, b, *, name, ta=False, tb=False, a_blk=None, b_blk=None, o_blk=None, o_slots=False, a_seg=False, b_seg=False,
        o_seg=False, tm=None, out_dtype=f32, scale=1.0, add=None):
    a2, b2 = a.shape[-2:], b.shape[-2:]
    Ma, Ka = (a2[1], a2[0]) if ta else a2
    Kb, Nb = (b2[1], b2[0]) if tb else b2
    M = Ma * (a.shape[0] if a_blk == "m" else 1)
    K = Ka * (a.shape[0] if a_blk == "k" else 1)
    N = Nb * (b.shape[0] if b_blk == "n" else 1)
    assert K == Kb * (b.shape[0] if b_blk == "k" else 1), (a.shape, b.shape, ta, tb, a_blk, b_blk)
    assert (a.ndim == 3) == (a_blk is not None) and (b.ndim == 3) == (b_blk is not None)
    tm = Ma if a_blk == "m" else (tm or _pick(M, 512))
    tn = Nb if b_blk == "n" else _pick(N, 1024)
    if a_blk == "k" or b_blk == "k":
        tk = Ka if a_blk == "k" else Kb
        assert tk == (Kb if b_blk == "k" else tk)
    else:
        tk = K if K <= MM_FULL_K else _pick(K, 1024)
    if (a_seg and not ta) or o_seg:
        tm = M // NSEG
    if (a_seg and ta) or b_seg:
        tk = K // NSEG
    gm, gn, nk = M // tm, N // tn, K // tk
    assert not (add is not None and (o_seg or o_blk)) and not (o_blk and o_seg)

    if a_seg:
        assert a.ndim == 2
        a = a.reshape(a.shape[0] // NSEG, NSEG * a.shape[1])
        if ta:
            a_spec = pl.BlockSpec((tk, tm), lambda i, j, k: (0, k * (Ma // tm) + i))
        else:
            a_spec = pl.BlockSpec((tm, tk), lambda i, j, k: (0, i * (Ka // tk) + k))
    elif a.ndim == 3:
        lead = (lambda i, k: i) if a_blk == "m" else (lambda i, k: k)
        if ta:
            a_spec = pl.BlockSpec((None, tk, tm), lambda i, j, k: (lead(i, k), 0 if a_blk == "k" else k, 0 if a_blk == "m" else i))
        else:
            a_spec = pl.BlockSpec((None, tm, tk), lambda i, j, k: (lead(i, k), 0 if a_blk == "m" else i, 0 if a_blk == "k" else k))
    else:
        a_spec = pl.BlockSpec((tk, tm), lambda i, j, k: (k, i)) if ta else pl.BlockSpec((tm, tk), lambda i, j, k: (i, k))
    if b_seg:
        assert b.ndim == 2 and not tb
        b = b.reshape(b.shape[0] // NSEG, NSEG * b.shape[1])
        b_spec = pl.BlockSpec((tk, tn), lambda i, j, k: (0, k * (Nb // tn) + j))
    elif b.ndim == 3:
        lead = (lambda j, k: j) if b_blk == "n" else (lambda j, k: k)
        if tb:
            b_spec = pl.BlockSpec((None, tn, tk), lambda i, j, k: (lead(j, k), 0 if b_blk == "n" else j, 0 if b_blk == "k" else k))
        else:
            b_spec = pl.BlockSpec((None, tk, tn), lambda i, j, k: (lead(j, k), 0 if b_blk == "k" else k, 0 if b_blk == "n" else j))
    else:
        b_spec = pl.BlockSpec((tn, tk), lambda i, j, k: (j, k)) if tb else pl.BlockSpec((tk, tn), lambda i, j, k: (k, j))
    slot = _slot_of if o_slots else (lambda k: k)
    if o_blk == "n":
        assert gn == N_DEV or not o_slots
        o_shape, o_spec = (gn, M, tn), pl.BlockSpec((None, tm, tn), lambda i, j, k: (slot(j), i, 0))
    elif o_blk == "m" and o_slots and gm < N_DEV:
        rs = M // N_DEV
        per_tile = tm // rs
        assert per_tile % 2 == 0 and tm % rs == 0
        o_shape = (2, N_CHIP, rs, N)
        o_spec = pl.BlockSpec((2, per_tile // 2, rs, tn), lambda i, j, k: (0, i, 0, j))
    elif o_blk == "m":
        assert gm == N_DEV or not o_slots
        o_shape, o_spec = (gm, tm, N), pl.BlockSpec((None, tm, tn), lambda i, j, k: (slot(i), 0, j))
    elif o_seg:
        o_shape, o_spec = (tm, NSEG * N), pl.BlockSpec((tm, tn), lambda i, j, k: (0, i * (N // tn) + j))
    else:
        o_shape, o_spec = (M, N), pl.BlockSpec((tm, tn), lambda i, j, k: (i, j))
    dims = (((0 if ta else 1,), (1 if tb else 0,)), ((), ()))
    has_add = add is not None

    def body(*refs):
        a_ref, b_ref = refs[0], refs[1]
        add_ref = refs[2] if has_add else None
        o_ref, acc_ref = refs[-2], refs[-1]
        k = pl.program_id(2)

        @pl.when(k == 0)
        def _():
            acc_ref[...] = jnp.zeros_like(acc_ref)

        acc_ref[...] += lax.dot_general(a_ref[...].astype(bf16), b_ref[...].astype(bf16), dims, preferred_element_type=f32)

        @pl.when(k == nk - 1)
        def _():
            r = acc_ref[...] * scale
            if has_add:
                r = r + add_ref[...].astype(f32)
            if len(o_shape) == 4:
                rs = o_shape[2]
                for chip_l in range(o_ref.shape[1]):
                    for core in range(2):
                        dev = 2 * chip_l + core
                        o_ref[core, chip_l] = r[dev * rs:(dev + 1) * rs].astype(out_dtype)
            else:
                o_ref[...] = r.astype(out_dtype)

    ins = [a, b] + ([add] if has_add else [])
    in_specs = [a_spec, b_spec] + ([o_spec] if has_add else [])
    out = pl.pallas_call(
        body, name=name, grid=(gm, gn, nk), in_specs=in_specs, out_specs=o_spec,
        out_shape=_S(o_shape, out_dtype), scratch_shapes=[pltpu.VMEM((tm, tn), f32)],
        compiler_params=_cparams(("parallel", "parallel", "arbitrary")),
    )(*ins)
    if len(o_shape) == 4:
        return out.reshape(N_DEV, o_shape[2], N)
    return out.reshape(M, N) if o_seg else out


def _row_tile(T, widths):
    budget = 6 * 1024 * 1024
    tb = max(16, budget // (4 * sum(widths)))
    return _pick(T, tb, align=16)


def _rows(fn, rows, params, outs, *, name):
    T = rows[0].shape[0]
    nr, npar = len(rows), len(params)
    tb = _row_tile(T, [r.shape[1] for r in rows] + [w for w, _ in outs])

    def body(*refs):
        ins = [r[...].astype(f32) for r in refs[: nr + npar]]
        res = fn(*ins)
        for o_ref, r in zip(refs[nr + npar:], res):
            o_ref[...] = r.astype(o_ref.dtype)

    in_specs = [pl.BlockSpec((tb, r.shape[1]), lambda i: (i, 0)) for r in rows]
    in_specs += [pl.BlockSpec(p.shape, lambda i: (0, 0)) for p in params]
    out_specs = [pl.BlockSpec((tb, w), lambda i: (i, 0)) for w, _ in outs]
    res = pl.pallas_call(
        body, name=name, grid=(T // tb,), in_specs=in_specs, out_specs=out_specs,
        out_shape=[_S((T, w), d) for w, d in outs], compiler_params=_cparams(("parallel",)),
    )(*rows, *params)
    return tuple(res)


def _rows_bwd(fn, rows, params, cots, *, name, want_rows, row_dtypes=None, adds=None):
    T = rows[0].shape[0]
    nr, npar, nc = len(rows), len(params), len(cots)
    adds = adds or {}
    add_idx = sorted(adds)
    row_dtypes = row_dtypes or {}
    widths = [r.shape[1] for r in rows] + [c.shape[1] for c in cots] + [rows[i].shape[1] for i in want_rows]
    tb = _row_tile(T, widths)

    def body(*refs):
        ins = [r[...].astype(f32) for r in refs[: nr + npar]]
        cot = tuple(r[...].astype(f32) for r in refs[nr + npar: nr + npar + nc])
        add_refs = refs[nr + npar + nc: nr + npar + nc + len(add_idx)]
        out_refs = refs[nr + npar + nc + len(add_idx):]
        _, vjp = jax.vjp(lambda *a: tuple(fn(*a)), *ins)
        g = vjp(cot)
        for o_ref, i in zip(out_refs[: len(want_rows)], want_rows):
            r = g[i]
            if i in adds:
                r = r + add_refs[add_idx.index(i)][...].astype(f32)
            o_ref[...] = r.astype(o_ref.dtype)
        first = pl.program_id(0) == 0
        for o_ref, gp in zip(out_refs[len(want_rows):], g[nr:]):
            @pl.when(first)
            def _(o_ref=o_ref):
                o_ref[...] = jnp.zeros_like(o_ref)

            o_ref[...] += gp

    in_specs = [pl.BlockSpec((tb, r.shape[1]), lambda i: (i, 0)) for r in rows]
    in_specs += [pl.BlockSpec(p.shape, lambda i: (0, 0)) for p in params]
    in_specs += [pl.BlockSpec((tb, c.shape[1]), lambda i: (i, 0)) for c in cots]
    in_specs += [pl.BlockSpec((tb, adds[i].shape[1]), lambda i_: (i_, 0)) for i in add_idx]
    out_specs = [pl.BlockSpec((tb, rows[i].shape[1]), lambda i_: (i_, 0)) for i in want_rows]
    out_specs += [pl.BlockSpec(p.shape, lambda i: (0, 0)) for p in params]
    out_shape = [_S(rows[i].shape, row_dtypes.get(i, f32)) for i in want_rows] + [_S(p.shape, f32) for p in params]
    res = pl.pallas_call(
        body, name=name, grid=(T // tb,), in_specs=in_specs, out_specs=out_specs, out_shape=out_shape,
        compiler_params=_cparams(("arbitrary",)),
    )(*rows, *params, *cots, *[adds[i] for i in add_idx])
    return list(res[: len(want_rows)]), list(res[len(want_rows):])


def _f_rmsnorm(x, g):
    return (x * lax.rsqrt(jnp.mean(x * x, axis=-1, keepdims=True) + EPS) * g,)


def _f_swiglu(ab):
    F = ab.shape[1] // 2
    return (jax.nn.silu(ab[:, :F]) * ab[:, F:],)


def _f_s5_post(y, u, d):
    return (jax.nn.gelu(y + d * u),)


def _f_glu(g, v, b):
    return (g * jax.nn.sigmoid(v + b),)


def _f_gated_norm(y, z, w):
    return _f_rmsnorm(y * jax.nn.silu(z), w)


def _f_merge(gl, p5, pssd, b):
    D = p5.shape[1]
    gates = jax.nn.sigmoid(gl + b)
    return (gates[:, :D] * p5 + gates[:, D:] * pssd,)


def _f_dt(dtr, bias, a_log):
    dt = jax.nn.softplus(dtr + bias)
    return dt, dt * (-jnp.exp(a_log))


def _loss_stage(x, tgt, g, *, name):
    T, D = x.shape
    tb = _row_tile(T, [D, D, D])

    def f(xb, gb, tb_):
        y = _f_rmsnorm(xb, gb)[0]
        return 0.5 * jnp.sum(jnp.mean(jnp.square(y - tb_), axis=-1, keepdims=True), axis=0, keepdims=True)

    def body(x_ref, t_ref, g_ref, l_ref, dx_ref, dg_ref):
        tv = t_ref[...]
        val, vjp = jax.vjp(lambda a, b: f(a, b, tv), x_ref[...], g_ref[...])
        dx, dg = vjp(jnp.ones((1, 1), f32))
        dx_ref[...] = dx

        @pl.when(pl.program_id(0) == 0)
        def _():
            l_ref[...] = jnp.zeros_like(l_ref)
            dg_ref[...] = jnp.zeros_like(dg_ref)

        l_ref[...] += jnp.broadcast_to(val, l_ref.shape)
        dg_ref[...] += dg

    row = pl.BlockSpec((tb, D), lambda i: (i, 0))
    par = pl.BlockSpec((1, D), lambda i: (0, 0))
    return pl.pallas_call(
        body, name=name, grid=(T // tb,), in_specs=[row, row, par],
        out_specs=[pl.BlockSpec((1, LANES), lambda i: (0, 0)), row, par],
        out_shape=[_S((1, LANES), f32), _S((T, D), f32), _S((1, D), f32)], compiler_params=_cparams(("arbitrary",)),
    )(x, tgt, g)


def _shift_down(x, s):
    if s == 0:
        return x
    t = lax.broadcasted_iota(jnp.int32, x.shape, 0)
    return jnp.where(t >= s, pltpu.roll(x, s, axis=0), 0.0)


def _shift_up(x, s):
    if s == 0:
        return x
    T = x.shape[0]
    t = lax.broadcasted_iota(jnp.int32, x.shape, 0)
    return jnp.where(t < T - s, pltpu.roll(x, T - s, axis=0), 0.0)


def _conv_pre(x, w, b):
    pre = b
    for k in range(CONV_K):
        pre = pre + w[k:k + 1, :] * _shift_down(x, CONV_K - 1 - k)
    return pre


def _conv_fwd(x, w, b, *, name):
    T, C = x.shape
    cb = _pick(C, 256)

    def body(x_ref, w_ref, b_ref, o_ref):
        o_ref[...] = jax.nn.silu(_conv_pre(x_ref[...], w_ref[...], b_ref[...]))

    col = pl.BlockSpec((T, cb), lambda j: (0, j))
    return pl.pallas_call(
        body, name=name, grid=(C // cb,), in_specs=[col, pl.BlockSpec((CONV_K, cb), lambda j: (0, j)), pl.BlockSpec((1, cb), lambda j: (0, j))],
        out_specs=col, out_shape=_S((T, C), f32), compiler_params=_cparams(("parallel",)),
    )(x, w, b)


def _conv_bwd(x, w, b, dy, *, name):
    T, C = x.shape
    cb = _pick(C, 256)

    def body(x_ref, w_ref, b_ref, dy_ref, dx_ref, dw_ref, db_ref):
        xv, wv = x_ref[...], w_ref[...]
        pre = _conv_pre(xv, wv, b_ref[...])
        sg = jax.nn.sigmoid(pre)
        dpre = dy_ref[...] * sg * (1.0 + pre * (1.0 - sg))
        dx = jnp.zeros_like(xv)
        for k in range(CONV_K):
            s = CONV_K - 1 - k
            dx = dx + wv[k:k + 1, :] * _shift_up(dpre, s)
            dw_ref[k:k + 1, :] = jnp.sum(dpre * _shift_down(xv, s), axis=0, keepdims=True)
        dx_ref[...] = dx
        db_ref[...] = jnp.sum(dpre, axis=0, keepdims=True)

    col = pl.BlockSpec((T, cb), lambda j: (0, j))
    wsp = pl.BlockSpec((CONV_K, cb), lambda j: (0, j))
    bsp = pl.BlockSpec((1, cb), lambda j: (0, j))
    return pl.pallas_call(
        body, name=name, grid=(C // cb,), in_specs=[col, wsp, bsp, col], out_specs=[col, wsp, bsp],
        out_shape=[_S((T, C), f32), _S((CONV_K, C), f32), _S((1, C), f32)], compiler_params=_cparams(("parallel",)),
    )(x, w, b, dy)


def _f_s5_prep(lr, li, ldt, lrb, lib, ldtb, brt, bit):
    def disc(lr_, li_, ldt_):
        dt = jnp.exp(ldt_)
        mag = jnp.exp(lr_ * dt)
        ar, ai = mag * jnp.cos(li_ * dt), mag * jnp.sin(li_ * dt)
        den = lr_ * lr_ + li_ * li_
        cr = ((ar - 1.0) * lr_ + ai * li_) / den
        ci = (ai * lr_ - (ar - 1.0) * li_) / den
        return ar, ai, cr, ci

    ar, ai, _, _ = disc(lr, li, ldt)
    _, _, cr, ci = disc(lrb, lib, ldtb)
    return ar, ai, cr * brt - ci * bit, cr * bit + ci * brt


def _s5_prep(args, *, name):
    G, N = args[0].shape
    GM = args[3].shape[0]

    def body(*refs):
        res = _f_s5_prep(*[r[...] for r in refs[:8]])
        for o, r in zip(refs[8:], res):
            o[...] = r

    return pl.pallas_call(body, name=name, out_shape=[_S((G, N), f32)] * 2 + [_S((GM, N), f32)] * 2)(*args)


def _s5_prep_bwd(args, cots, rsum, *, name):
    G, N = args[0].shape
    GM = args[3].shape[0]

    def body(*refs):
        ins = [r[...] for r in refs[:8]]
        cot = tuple(r[...] for r in refs[8:12])
        rs = refs[12][...]
        _, vjp = jax.vjp(_f_s5_prep, *ins)
        g = vjp(cot)
        fold = lambda v: jnp.dot(rs, v, preferred_element_type=f32, precision=lax.Precision.HIGHEST)
        o = refs[13:]
        o[0][...] = g[0] + fold(g[3])
        o[1][...] = g[1] + fold(g[4])
        o[2][...] = g[2] + fold(jnp.broadcast_to(g[5], (GM, LANES)))[:, 0:1]
        o[3][...] = g[6]
        o[4][...] = g[7]

    return pl.pallas_call(
        body, name=name, out_shape=[_S((G, N), f32), _S((G, N), f32), _S((G, 1), f32), _S((GM, N), f32), _S((GM, N), f32)],
    )(*args, *cots, rsum)


S5_TC = 512


def _s5_local_scan(src, w_r, w_i, a_r, a_i, *, reverse, name):
    T, C = src.shape
    nblk, cb, sb = w_r.shape
    NS = nblk * sb
    tc = min(S5_TC, T)
    nT, nt = T // tc, tc // NSEG
    tmap = (lambda i: nT - 1 - i) if reverse else (lambda i: i)

    def body(u_ref, wr_ref, wi_ref, ar_ref, ai_ref, sr_ref, si_ref, pr_ref, pi_ref, st_r, st_i, pw_r, pw_i):
        @pl.when(pl.program_id(1) == 0)
        def _():
            st_r[...] = jnp.zeros_like(st_r)
            st_i[...] = jnp.zeros_like(st_i)
            pw_r[...] = jnp.ones_like(pw_r)
            pw_i[...] = jnp.zeros_like(pw_i)

        u = u_ref[...].astype(bf16)
        sr_ref[...] = jnp.dot(u, wr_ref[...], preferred_element_type=f32)
        si_ref[...] = jnp.dot(u, wi_ref[...], preferred_element_type=f32)
        ar = jnp.broadcast_to(ar_ref[...], (NSEG, sb))
        ai = jnp.broadcast_to(ai_ref[...], (NSEG, sb))

        def step(k, c):
            cr, ci, qr, qi = c
            kk = (nt - 1 - k) if reverse else k
            rows = pl.ds(pl.multiple_of(kk * NSEG, NSEG), NSEG)
            nr = ar * cr - ai * ci + sr_ref[rows, :]
            ni = ar * ci + ai * cr + si_ref[rows, :]
            sr_ref[rows, :] = nr
            si_ref[rows, :] = ni
            return nr, ni, ar * qr - ai * qi, ar * qi + ai * qr

        cr, ci, qr, qi = lax.fori_loop(0, nt, step, (st_r[...], st_i[...], pw_r[...], pw_i[...]), unroll=8)
        st_r[...], st_i[...], pw_r[...], pw_i[...] = cr, ci, qr, qi
        pr_ref[...] = qr
        pi_ref[...] = qi

    blk = pl.BlockSpec((tc, sb), lambda j, i: (tmap(i), j))
    wsp = pl.BlockSpec((None, cb, sb), lambda j, i: (j, 0, 0))
    asp = pl.BlockSpec((1, sb), lambda j, i: (0, j))
    psp = pl.BlockSpec((NSEG, sb), lambda j, i: (0, j))
    return pl.pallas_call(
        body, name=name, grid=(nblk, nT), in_specs=[pl.BlockSpec((tc, cb), lambda j, i: (tmap(i), j)), wsp, wsp, asp, asp],
        out_specs=[blk, blk, psp, psp], out_shape=[_S((T, NS), f32)] * 2 + [_S((NSEG, NS), f32)] * 2,
        scratch_shapes=[pltpu.VMEM((NSEG, sb), f32)] * 4, compiler_params=_cparams(("parallel", "arbitrary")),
    )(src, w_r, w_i, a_r, a_i)


def _s5_carry(e_r, e_i, p_r, p_i, *, reverse, name):
    NS = e_r.shape[1]

    def body(er_ref, ei_ref, pr_ref, pi_ref, cr_ref, ci_ref):
        ar, ai = pr_ref[0:1, :], pi_ref[0:1, :]
        cr = jnp.zeros((1, NS), f32)
        ci = jnp.zeros((1, NS), f32)
        order = list(range(NSEG - 1, -1, -1)) if reverse else list(range(NSEG))
        cr_ref[order[0]:order[0] + 1, :] = cr
        ci_ref[order[0]:order[0] + 1, :] = ci
        for prev, q in zip(order[:-1], order[1:]):
            er, ei = er_ref[prev:prev + 1, :], ei_ref[prev:prev + 1, :]
            cr, ci = er + ar * cr - ai * ci, ei + ar * ci + ai * cr
            cr_ref[q:q + 1, :] = cr
            ci_ref[q:q + 1, :] = ci

    return pl.pallas_call(body, name=name, out_shape=[_S((NSEG, NS), f32)] * 2)(e_r, e_i, p_r, p_i)


def _s5_fix_out(sl_r, sl_i, a_r, a_i, c_r, c_i, wc_r, wc_i, *, name):
    T, NS = sl_r.shape
    nblk, sb, cb = wc_r.shape
    tc = min(S5_TC, T)
    nT, nt = T // tc, tc // NSEG

    def body(lr_ref, li_ref, ar_ref, ai_ref, cr_ref, ci_ref, wr_ref, wi_ref, sr_ref, si_ref, y_ref, pw_r, pw_i):
        @pl.when(pl.program_id(1) == 0)
        def _():
            pw_r[...] = jnp.ones_like(pw_r)
            pw_i[...] = jnp.zeros_like(pw_i)

        ar = jnp.broadcast_to(ar_ref[...], (NSEG, sb))
        ai = jnp.broadcast_to(ai_ref[...], (NSEG, sb))
        cr, ci = cr_ref[...], ci_ref[...]

        def step(k, c):
            qr, qi = c
            qr, qi = ar * qr - ai * qi, ar * qi + ai * qr
            rows = pl.ds(pl.multiple_of(k * NSEG, NSEG), NSEG)
            sr_ref[rows, :] = lr_ref[rows, :] + qr * cr - qi * ci
            si_ref[rows, :] = li_ref[rows, :] + qr * ci + qi * cr
            return qr, qi

        qr, qi = lax.fori_loop(0, nt, step, (pw_r[...], pw_i[...]), unroll=8)
        pw_r[...], pw_i[...] = qr, qi
        y_ref[...] = (jnp.dot(sr_ref[...].astype(bf16), wr_ref[...], preferred_element_type=f32)
                      - jnp.dot(si_ref[...].astype(bf16), wi_ref[...], preferred_element_type=f32))

    blk = pl.BlockSpec((tc, sb), lambda j, i: (i, j))
    asp = pl.BlockSpec((1, sb), lambda j, i: (0, j))
    csp = pl.BlockSpec((NSEG, sb), lambda j, i: (0, j))
    wsp = pl.BlockSpec((None, sb, cb), lambda j, i: (j, 0, 0))
    return pl.pallas_call(
        body, name=name, grid=(nblk, nT), in_specs=[blk, blk, asp, asp, csp, csp, wsp, wsp],
        out_specs=[blk, blk, pl.BlockSpec((tc, cb), lambda j, i: (i, j))],
        out_shape=[_S((T, NS), f32)] * 2 + [_S((T, nblk * cb), f32)],
        scratch_shapes=[pltpu.VMEM((NSEG, sb), f32)] * 2, compiler_params=_cparams(("parallel", "arbitrary")),
    )(sl_r, sl_i, a_r, a_i, c_r, c_i, wc_r, wc_i)


def _s5_fix_bwd(ql_r, ql_i, ab_r, ab_i, c_r, c_i, s_r, s_i, sb_r, sb_i, u, dy, du_add, w_r, w_i, *, name):
    T, NS = ql_r.shape
    nblk, cb, sb = w_r.shape
    tc = min(S5_TC, T)
    nT, nt = T // tc, tc // NSEG
    tmap = lambda i: nT - 1 - i

    def body(lr_ref, li_ref, ar_ref, ai_ref, cr_ref, ci_ref, sr_ref, si_ref, br_ref, bi_ref, u_ref, dy_ref, dua_ref, wr_ref, wi_ref,
             du_ref, dwr_ref, dwi_ref, dcr_ref, dci_ref, dar_ref, dai_ref, pw_r, pw_i, ac_r, ac_i, q_r, q_i):
        first = pl.program_id(1) == 0

        @pl.when(first)
        def _():
            pw_r[...] = jnp.ones_like(pw_r)
            pw_i[...] = jnp.zeros_like(pw_i)
            ac_r[...] = jnp.zeros_like(ac_r)
            ac_i[...] = jnp.zeros_like(ac_i)
            dwr_ref[...] = jnp.zeros_like(dwr_ref)
            dwi_ref[...] = jnp.zeros_like(dwi_ref)
            dcr_ref[...] = jnp.zeros_like(dcr_ref)
            dci_ref[...] = jnp.zeros_like(dci_ref)

        ar = jnp.broadcast_to(ar_ref[...], (NSEG, sb))
        ai = jnp.broadcast_to(ai_ref[...], (NSEG, sb))
        cr, ci = cr_ref[...], ci_ref[...]

        def fix(rows, qr, qi, spr, spi, accr, acci):
            qr, qi = ar * qr - ai * qi, ar * qi + ai * qr
            xr = lr_ref[rows, :] + qr * cr - qi * ci
            xi = li_ref[rows, :] + qr * ci + qi * cr
            q_r[rows, :] = xr
            q_i[rows, :] = xi
            return qr, qi, accr + xr * spr + xi * spi, acci + xi * spr - xr * spi

        def step(k, c):
            qr, qi, accr, acci = c
            kk = nt - 1 - k
            rows = pl.ds(pl.multiple_of(kk * NSEG, NSEG), NSEG)
            prev = pl.ds(pl.multiple_of((kk - 1) * NSEG, NSEG), NSEG)
            return fix(rows, qr, qi, sr_ref[prev, :], si_ref[prev, :], accr, acci)

        c = lax.fori_loop(0, nt - 1, step, (pw_r[...], pw_i[...], ac_r[...], ac_i[...]), unroll=7)
        qr, qi, accr, acci = fix(pl.ds(0, NSEG), *c[:2], br_ref[...], bi_ref[...], *c[2:])
        pw_r[...], pw_i[...], ac_r[...], ac_i[...] = qr, qi, accr, acci

        qrb, qib = q_r[...].astype(bf16), q_i[...].astype(bf16)
        nt_dims = (((1,), (1,)), ((), ()))
        tn_dims = (((0,), (0,)), ((), ()))
        du_ref[...] = (dua_ref[...] + lax.dot_general(qrb, wr_ref[...], nt_dims, preferred_element_type=f32)
                       + lax.dot_general(qib, wi_ref[...], nt_dims, preferred_element_type=f32))
        ub = u_ref[...].astype(bf16)
        dwr_ref[...] += lax.dot_general(ub, qrb, tn_dims, preferred_element_type=f32)
        dwi_ref[...] += lax.dot_general(ub, qib, tn_dims, preferred_element_type=f32)
        dyb = dy_ref[...].astype(bf16)
        dcr_ref[...] += lax.dot_general(sr_ref[...].astype(bf16), dyb, tn_dims, preferred_element_type=f32)
        dci_ref[...] -= lax.dot_general(si_ref[...].astype(bf16), dyb, tn_dims, preferred_element_type=f32)

        @pl.when(pl.program_id(1) == nT - 1)
        def _():
            dar_ref[...] = jnp.sum(accr, axis=0, keepdims=True)
            dai_ref[...] = jnp.sum(acci, axis=0, keepdims=True)

    blk = pl.BlockSpec((tc, sb), lambda j, i: (tmap(i), j))
    asp = pl.BlockSpec((1, sb), lambda j, i: (0, j))
    csp = pl.BlockSpec((NSEG, sb), lambda j, i: (0, j))
    bsp = pl.BlockSpec((None, NSEG, sb), lambda j, i: (tmap(i), 0, j))
    chn = pl.BlockSpec((tc, cb), lambda j, i: (tmap(i), j))
    wsp = pl.BlockSpec((None, cb, sb), lambda j, i: (j, 0, 0))
    wcs = pl.BlockSpec((None, sb, cb), lambda j, i: (j, 0, 0))
    return pl.pallas_call(
        body, name=name, grid=(nblk, nT), in_specs=[blk, blk, asp, asp, csp, csp, blk, blk, bsp, bsp, chn, chn, chn, wsp, wsp],
        out_specs=[chn, wsp, wsp, wcs, wcs, asp, asp],
        out_shape=[_S((T, nblk * cb), f32), _S((nblk, cb, sb), f32), _S((nblk, cb, sb), f32), _S((nblk, sb, cb), f32),
                   _S((nblk, sb, cb), f32), _S((1, NS), f32), _S((1, NS), f32)],
        scratch_shapes=[pltpu.VMEM((NSEG, sb), f32)] * 4 + [pltpu.VMEM((tc, sb), f32)] * 2,
        compiler_params=_cparams(("parallel", "arbitrary")),
    )(ql_r, ql_i, ab_r, ab_i, c_r, c_i, s_r, s_i, sb_r, sb_i, u, dy, du_add, w_r, w_i)


SSD_TB = 256


def _dotf(a, b, dims):
    return lax.dot_general(a.astype(bf16), b.astype(bf16), dims, preferred_element_type=f32)


_NN = (((1,), (0,)), ((), ()))
_NT = (((1,), (1,)), ((), ()))
_TN = (((0,), (0,)), ((), ()))


def _f_ssd_block(x, bm, cm, dtc, dac, dtr, dar, dh, h):
    tb = x.shape[0]
    hpg = dtc.shape[1]
    ii = lax.broadcasted_iota(jnp.int32, (CHUNK, CHUNK), 0)
    jj = lax.broadcasted_iota(jnp.int32, (CHUNK, CHUNK), 1)
    tril = (ii >= jj)
    trilf = tril.astype(f32)
    triuf = (ii <= jj).astype(f32)
    hi = lax.Precision.HIGHEST
    ys = []
    for c in range(tb // CHUNK):
        r0 = c * CHUNK
        xc, bc, cc = x[r0:r0 + CHUNK], bm[r0:r0 + CHUNK], cm[r0:r0 + CHUNK]
        acc = jnp.dot(trilf, dac[r0:r0 + CHUNK], preferred_element_type=f32, precision=hi)
        acr = jnp.dot(dar[c], triuf, preferred_element_type=f32, precision=hi)
        cb = _dotf(cc, bc, _NT)
        yk, hk = [], []
        for k in range(hpg):
            xk = xc[:, k * HEADDIM:(k + 1) * HEADDIM]
            hp = h[k * HEADDIM:(k + 1) * HEADDIM]
            a_col = acc[:, k:k + 1]
            seg = a_col - acr[k:k + 1, :]
            lmat = jnp.where(tril, jnp.exp(jnp.where(tril, seg, 0.0)), 0.0)
            w = cb * lmat * dtr[c][k:k + 1, :]
            y = _dotf(w, xk, _NN) + _dotf(cc, hp, _NT) * jnp.exp(a_col) + dh[:, k:k + 1] * xk
            a_last = acc[CHUNK - 1:CHUNK, k:k + 1]
            xw = xk * (jnp.exp(a_last - a_col) * dtc[r0:r0 + CHUNK, k:k + 1])
            hk.append(jnp.exp(a_last) * hp + _dotf(xw, bc, _TN))
            yk.append(y)
        ys.append(jnp.concatenate(yk, axis=1))
        h = jnp.concatenate(hk, axis=0)
    return jnp.concatenate(ys, axis=0), h


def _ssd_specs(G, hpg, tb, tmap, b_off=0, c_off=0):
    gw = hpg * HEADDIM
    ncb = tb // CHUNK
    xsp = pl.BlockSpec((tb, gw), lambda g, i: (tmap(i), g))
    bsp = pl.BlockSpec((tb, SSD_STATE), lambda g, i: (tmap(i), b_off + g))
    osp = pl.BlockSpec((tb, SSD_STATE), lambda g, i: (tmap(i), c_off + g))
    csp = pl.BlockSpec((None, tb, hpg), lambda g, i: (g, tmap(i), 0))
    rsp = pl.BlockSpec((None, ncb, hpg, CHUNK), lambda g, i: (g, tmap(i), 0, 0))
    dsp = pl.BlockSpec((None, 1, hpg), lambda g, i: (g, 0, 0))
    hsp = pl.BlockSpec((None, None, gw, SSD_STATE), lambda g, i: (g, tmap(i), 0, 0))
    return xsp, bsp, osp, csp, rsp, dsp, hsp


def _ssd_fwd(xc, dtc, dac, dtr, dar, dh, *, d_inner, name):
    T = xc.shape[0]
    G, _, hpg = dtc.shape
    gw = hpg * HEADDIM
    tb = min(SSD_TB, T)
    nb = T // tb
    b_off = d_inner // SSD_STATE
    xsp, bsp, osp, csp, rsp, dsp, hsp = _ssd_specs(G, hpg, tb, lambda i: i, b_off, b_off + G)

    def body(x_ref, b_ref, c_ref, dtc_ref, dac_ref, dtr_ref, dar_ref, dh_ref, y_ref, hs_ref, h_scr):
        @pl.when(pl.program_id(1) == 0)
        def _():
            h_scr[...] = jnp.zeros_like(h_scr)

        h = h_scr[...]
        hs_ref[...] = h
        y, hn = _f_ssd_block(x_ref[...], b_ref[...], c_ref[...], dtc_ref[...], dac_ref[...], dtr_ref[...], dar_ref[...], dh_ref[...], h)
        y_ref[...] = y
        h_scr[...] = hn

    return pl.pallas_call(
        body, name=name, grid=(G, nb), in_specs=[xsp, bsp, osp, csp, csp, rsp, rsp, dsp], out_specs=[xsp, hsp],
        out_shape=[_S((T, G * gw), f32), _S((G, nb, gw, SSD_STATE), f32)], scratch_shapes=[pltpu.VMEM((gw, SSD_STATE), f32)],
        compiler_params=_cparams(("parallel", "arbitrary")),
    )(xc, xc, xc, dtc, dac, dtr, dar, dh)


def _ssd_bwd(xc, dtc, dac, dtr, dar, dh, hs, dy, *, d_inner, name):
    T = xc.shape[0]
    G, _, hpg = dtc.shape
    gw = hpg * HEADDIM
    tb = min(SSD_TB, T)
    nb = T // tb
    b_off = d_inner // SSD_STATE
    xsp, bsp, osp, csp, rsp, dsp, hsp = _ssd_specs(G, hpg, tb, lambda i: nb - 1 - i, b_off, b_off + G)
    _, gsp, _, _, _, _, _ = _ssd_specs(G, hpg, tb, lambda i: nb - 1 - i)

    def body(x_ref, b_ref, c_ref, dtc_ref, dac_ref, dtr_ref, dar_ref, dh_ref, hs_ref, dy_ref,
             dx_ref, db_ref, dc_ref, ddtc_ref, ddac_ref, ddtr_ref, ddar_ref, ddh_ref, g_scr):
        first = pl.program_id(1) == 0

        @pl.when(first)
        def _():
            g_scr[...] = jnp.zeros_like(g_scr)
            ddh_ref[...] = jnp.zeros_like(ddh_ref)

        ins = [r[...] for r in (x_ref, b_ref, c_ref, dtc_ref, dac_ref, dtr_ref, dar_ref, dh_ref, hs_ref)]
        _, vjp = jax.vjp(_f_ssd_block, *ins)
        g = vjp((dy_ref[...], g_scr[...]))
        dx_ref[...], db_ref[...], dc_ref[...] = g[0], g[1], g[2]
        ddtc_ref[...], ddac_ref[...], ddtr_ref[...], ddar_ref[...] = g[3], g[4], g[5], g[6]
        ddh_ref[...] += g[7]
        g_scr[...] = g[8]

    return pl.pallas_call(
        body, name=name, grid=(G, nb), in_specs=[xsp, bsp, osp, csp, csp, rsp, rsp, dsp, hsp, xsp],
        out_specs=[xsp, gsp, gsp, csp, csp, rsp, rsp, dsp],
        out_shape=[_S((T, G * gw), f32), _S((T, G * SSD_STATE), f32), _S((T, G * SSD_STATE), f32), _S(dtc.shape, f32), _S(dtc.shape, f32),
                   _S(dtr.shape, f32), _S(dtr.shape, f32), _S((G, 1, hpg), f32)],
        scratch_shapes=[pltpu.VMEM((gw, SSD_STATE), f32)], compiler_params=_cparams(("parallel", "arbitrary")),
    )(xc, xc, xc, dtc, dac, dtr, dar, dh, hs, dy)


def _peers():
    x, y, c = lax.axis_index("x"), lax.axis_index("y"), lax.axis_index("c")
    return x, y, c


_ANY = pl.BlockSpec(memory_space=pl.ANY)
N_CHIP = N_DEV // 2


def _all_gather(shards, *, name):
    n = len(shards)

    def body(*refs):
        x_refs, out_refs = refs[:n], refs[n:2 * n]
        send_sems, recv_sems, local_sems = refs[2 * n:]
        x, y, c = _peers()
        me, sibling = (x, y, c), (x, y, 1 - c)
        chips = [(1 - x, y), (x, 1 - y), (1 - x, 1 - y)]

        def copy(a, r, block, to, src=None):
            px, py, pc = block
            slot = out_refs[a].at[4 * px + 2 * py + pc]
            return pltpu.make_async_remote_copy(
                src_ref=slot if src is None else src, dst_ref=slot, send_sem=send_sems.at[7 * a + r],
                recv_sem=recv_sems.at[7 * a + r], device_id=to, device_id_type=MESH)

        mine = [pltpu.make_async_copy(x_refs[a], out_refs[a].at[4 * x + 2 * y + c], local_sems.at[a]) for a in range(n)]
        for cp in mine:
            cp.start()
        first = []
        for a in range(n):
            first.append(copy(a, 0, me, sibling, src=x_refs[a]))
            first += [copy(a, 1 + j, me, (*chip, c), src=x_refs[a]) for j, chip in enumerate(chips)]
        for cp in first:
            cp.start()
        passed = []
        for j, chip in enumerate(chips):
            for a in range(n):
                copy(a, 1 + j, (*chip, c), me).wait_recv()
                fwd = copy(a, 4 + j, (*chip, c), sibling)
                fwd.start()
                passed.append(fwd)
        for a in range(n):
            copy(a, 0, sibling, me).wait_recv()
        for j, chip in enumerate(chips):
            for a in range(n):
                copy(a, 4 + j, (*chip, 1 - c), me).wait_recv()
        for cp in first + passed:
            cp.wait_send()
        for cp in mine:
            cp.wait()

    return pl.pallas_call(
        body, name=name, out_shape=[_S((N_DEV,) + s.shape, s.dtype) for s in shards], in_specs=[_ANY] * n, out_specs=[_ANY] * n,
        scratch_shapes=[pltpu.SemaphoreType.DMA((7 * n,)), pltpu.SemaphoreType.DMA((7 * n,)), pltpu.SemaphoreType.DMA((n,))],
    )(*shards)


def _exchange_sibling(slots, *, name):
    n = len(slots)

    def body(*refs):
        x_refs, sib_refs = refs[:n], refs[n:2 * n]
        send_sems, recv_sems = refs[2 * n:]
        x, y, c = _peers()
        give = [pltpu.make_async_remote_copy(
            src_ref=x_refs[a].at[pl.ds(N_CHIP * (1 - c), N_CHIP)], dst_ref=sib_refs[a], send_sem=send_sems.at[a],
            recv_sem=recv_sems.at[a], device_id=(x, y, 1 - c), device_id_type=MESH) for a in range(n)]
        for cp in give:
            cp.start()
        for cp in give:
            cp.wait_recv()
        for cp in give:
            cp.wait_send()

    return list(pl.pallas_call(
        body, name=name, out_shape=[_S((N_CHIP,) + s.shape[1:], s.dtype) for s in slots], in_specs=[_ANY] * n, out_specs=[_ANY] * n,
        scratch_shapes=[pltpu.SemaphoreType.DMA((n,)), pltpu.SemaphoreType.DMA((n,))],
    )(*slots))


def _chip_sum(slots, sib, core, *, name):
    _, R, W = slots.shape
    tr = _pick(R, max(16, (1 << 20) // (4 * W)), align=16)

    def body(core_ref, x_ref, s_ref, o_ref):
        o_ref[...] = (x_ref[...].astype(f32) + s_ref[...].astype(f32)).astype(o_ref.dtype)

    blk = pl.BlockSpec((None, tr, W), lambda t, i, core_ref: (t, i, 0))
    return pl.pallas_call(
        body, name=name, out_shape=_S(sib.shape, slots.dtype),
        grid_spec=pltpu.PrefetchScalarGridSpec(
            num_scalar_prefetch=1, grid=(N_CHIP, R // tr),
            in_specs=[pl.BlockSpec((None, tr, W), lambda t, i, core_ref: (N_CHIP * core_ref[0] + t, i, 0)), blk], out_specs=blk),
        compiler_params=_cparams(("parallel", "parallel")),
    )(core, slots, sib)


def _exchange_chips(parts, *, name):
    n = len(parts)

    def body(*refs):
        p_refs, out_refs = refs[:n], refs[n:2 * n]
        send_sems, recv_sems = refs[2 * n:]
        x, y, c = _peers()
        copies = []
        for j in range(1, N_CHIP):
            tx, ty = x ^ (j >> 1), y ^ (j & 1)
            for a in range(n):
                copies.append(pltpu.make_async_remote_copy(
                    src_ref=p_refs[a].at[2 * tx + ty], dst_ref=out_refs[a].at[j - 1], send_sem=send_sems.at[3 * a + j - 1],
                    recv_sem=recv_sems.at[3 * a + j - 1], device_id=(tx, ty, c), device_id_type=MESH))
        for cp in copies:
            cp.start()
        for cp in copies:
            cp.wait_recv()
        for cp in copies:
            cp.wait_send()

    return list(pl.pallas_call(
        body, name=name, out_shape=[_S((N_CHIP - 1,) + p.shape[1:], p.dtype) for p in parts], in_specs=[_ANY] * n, out_specs=[_ANY] * n,
        scratch_shapes=[pltpu.SemaphoreType.DMA((3 * n,)), pltpu.SemaphoreType.DMA((3 * n,))],
    )(*parts))


def _sum_slots(stack, *, name):
    n, R, W = stack.shape
    tr = _pick(R, 1024, align=8)

    def body(s_ref, o_ref):
        acc = s_ref[0]
        for k in range(1, n):
            acc = acc + s_ref[k]
        o_ref[...] = acc

    return pl.pallas_call(
        body, name=name, grid=(R // tr,), in_specs=[pl.BlockSpec((n, tr, W), lambda i: (0, i, 0))],
        out_specs=pl.BlockSpec((tr, W), lambda i: (i, 0)), out_shape=_S((R, W), f32), compiler_params=_cparams(("parallel",)),
    )(stack)


def _adamw_math(gv, wv, mv, vv):
    c1 = 1.0 / (1.0 - ADAM_B1 ** ADAM_STEP)
    c2 = 1.0 / (1.0 - ADAM_B2 ** ADAM_STEP)
    nm = ADAM_B1 * mv + (1.0 - ADAM_B1) * gv
    nv = ADAM_B2 * vv + (1.0 - ADAM_B2) * jnp.square(gv)
    return -ADAM_LR * ((nm * c1) / (jnp.sqrt(nv * c2) + ADAM_EPS) + ADAM_WD * wv), nm, nv


def _adamw(g, w, m, v, *, name):
    R, W = w.shape
    tr = _pick(R, max(8, (1 << 20) // (4 * W)), align=8)

    def body(g_ref, w_ref, m_ref, v_ref, d_ref, nm_ref, nv_ref):
        d_ref[...], nm_ref[...], nv_ref[...] = _adamw_math(g_ref[...], w_ref[...], m_ref[...], v_ref[...])

    sp = pl.BlockSpec((tr, W), lambda i: (i, 0))
    return pl.pallas_call(
        body, name=name, grid=(R // tr,), in_specs=[sp] * 4, out_specs=[sp] * 3, out_shape=[_S((R, W), f32)] * 3,
        compiler_params=_cparams(("parallel",)),
    )(g, w, m, v)


def _reduce_adamw(own, arrived, chip, w, m, v, *, name):
    n, R, W = arrived.shape
    tr = _pick(R, max(16, (1 << 20) // (4 * W)), align=16)

    def body(chip_ref, o_ref, p_ref, w_ref, m_ref, v_ref, g_ref, d_ref, nm_ref, nv_ref):
        gv = o_ref[...].astype(f32)
        for k in range(n):
            gv = gv + p_ref[k].astype(f32)
        g_ref[...] = gv
        d_ref[...], nm_ref[...], nv_ref[...] = _adamw_math(gv, w_ref[...], m_ref[...], v_ref[...])

    sp = pl.BlockSpec((tr, W), lambda i, chip_ref: (i, 0))
    return pl.pallas_call(
        body, name=name, out_shape=[_S((R, W), f32)] * 4,
        grid_spec=pltpu.PrefetchScalarGridSpec(
            num_scalar_prefetch=1, grid=(R // tr,),
            in_specs=[pl.BlockSpec((None, tr, W), lambda i, chip_ref: (chip_ref[0], i, 0)),
                      pl.BlockSpec((n, tr, W), lambda i, chip_ref: (0, i, 0))] + [sp] * 3, out_specs=[sp] * 4),
        compiler_params=_cparams(("parallel",)),
    )(chip, own, arrived, w, m, v)


def _pieces(seg_start, seg_len, shard_w):
    out, col = [], seg_start
    while col < seg_start + seg_len:
        k, a = divmod(col, shard_w)
        n = min(shard_w - a, seg_start + seg_len - col)
        out.append((k, a, col - seg_start, n))
        col += n
    return out


def _unshard_w_in(g, seg_lens, *, name):
    _, D, w = g.shape
    starts = [sum(seg_lens[:i]) for i in range(len(seg_lens))]
    widths = [max(n, LANES) for n in seg_lens]
    tm = _pick(D, 256, align=16)

    def body(g_ref, *o_refs):
        for o_ref, s0, n in zip(o_refs, starts, seg_lens):
            if n < o_ref.shape[1]:
                o_ref[...] = jnp.zeros_like(o_ref)
            for k, a, off, m in _pieces(s0, n, w):
                o_ref[:, off:off + m] = g_ref[k, :, a:a + m]

    return pl.pallas_call(
        body, name=name, grid=(D // tm,), in_specs=[pl.BlockSpec((N_DEV, tm, w), lambda i: (0, i, 0))],
        out_specs=[pl.BlockSpec((tm, wd), lambda i: (i, 0)) for wd in widths], out_shape=[_S((D, wd), g.dtype) for wd in widths],
        compiler_params=_cparams(("parallel",)),
    )(g)


def _reshard_w_in(grads, seg_lens, w, *, name):
    D = grads[0].shape[0]
    starts = [sum(seg_lens[:i]) for i in range(len(seg_lens))]
    tm = _pick(D, 128, align=16)

    def body(*refs):
        o_ref = refs[-1]
        for g_ref, s0, n in zip(refs[:-1], starts, seg_lens):
            for k, a, off, m in _pieces(s0, n, w):
                o_ref[_slot_of(k), :, a:a + m] = g_ref[:, off:off + m].astype(o_ref.dtype)

    return pl.pallas_call(
        body, name=name, grid=(D // tm,), in_specs=[pl.BlockSpec((tm, g.shape[1]), lambda i: (i, 0)) for g in grads],
        out_specs=pl.BlockSpec((N_DEV, tm, w), lambda i: (0, i, 0)), out_shape=_S((N_DEV, D, w), bf16),
        compiler_params=_cparams(("parallel",)),
    )(*grads)


def _pad_flat(a, mult):
    a = a.reshape(-1)
    n = -(-a.shape[0] // mult) * mult
    return a if n == a.shape[0] else jnp.pad(a, (0, n - a.shape[0]))


def _pad_cols(a, mult):
    n = -(-a.shape[1] // mult) * mult
    return a if n == a.shape[1] else jnp.pad(a, ((0, 0), (0, n - a.shape[1])))


def _block_diag(t):
    nblk, g, P, Q = t.shape
    eye = jnp.eye(g, dtype=t.dtype)
    return (t[:, :, :, None, :] * eye[None, :, None, :, None]).reshape(nblk, g * P, g * Q)


def _block_diag_t(w, P, Q):
    nblk = w.shape[0]
    g = w.shape[1] // P
    eye = jnp.eye(g, dtype=w.dtype)
    return (w.reshape(nblk, g, P, g, Q) * eye[None, :, None, :, None]).sum(axis=3)


_COLS = ("ffn1_w_gate", "ffn1_w_up", "ffn2_w_gate", "ffn2_w_up")
_ROWS = ("ffn1_w_down", "ffn2_w_down", "s5_w_glu", "w_proj_s5", "w_out", "w_proj_ssd")
_BIG = _COLS + _ROWS + ("w_in", "conv_w")
_SMALL = ("ffn1_norm", "mix_norm", "conv_b", "s5_A_re", "s5_A_im", "s5_log_dt", "s5_B_re", "s5_B_im", "s5_C_re", "s5_C_im",
          "s5_D", "s5_b_glu", "ssd_A_log", "ssd_dt_bias", "ssd_D", "ssd_norm", "b_gate", "ffn2_norm", "final_norm")
_WEIGHTS = ("ffn1_norm", "ffn1_w_gate", "ffn1_w_up", "ffn1_w_down", "mix_norm", "w_in", "conv_w", "conv_b", "s5_A_re", "s5_A_im",
            "s5_log_dt", "s5_B_re", "s5_B_im", "s5_C_re", "s5_C_im", "s5_D", "s5_w_glu", "s5_b_glu", "ssd_A_log", "ssd_dt_bias",
            "ssd_D", "ssd_norm", "w_proj_s5", "w_proj_ssd", "b_gate", "w_out", "ffn2_norm", "ffn2_w_gate", "ffn2_w_up",
            "ffn2_w_down", "final_norm")
def _f_swiglu2(a, b):
    return (jax.nn.silu(a) * b,)


def _ffn_fwd(x, n, wg, wu, wd, tag):
    T, D = x.shape
    nf = wg.shape[2]
    h = _rows(_f_rmsnorm, [x], [n], [(D, bf16)], name=tag + "_norm")[0]
    a = _mm(h, wg, b_blk="n", o_blk="n", name=tag + "_gate").reshape(N_DEV * T, nf)
    b = _mm(h, wu, b_blk="n", o_blk="n", name=tag + "_up").reshape(N_DEV * T, nf)
    c = _rows(_f_swiglu2, [a, b], [], [(nf, bf16)], name=tag + "_act")[0].reshape(N_DEV, T, nf)
    y = _mm(c, wd, a_blk="k", b_blk="k", scale=0.5, add=x, name=tag + "_down")
    return y, (x, n, h, a, b, c)


def _ffn_bwd(saved, wg, wu, wd, dy, tag):
    x, n, h, a, b, c = saved
    T = x.shape[0]
    nf = wg.shape[2]
    dc = _mm(dy, wd, tb=True, b_blk="n", o_blk="n", scale=0.5, name=tag + "_d_act").reshape(N_DEV * T, nf)
    dwd = _mm(c, dy, ta=True, a_blk="m", o_blk="m", o_slots=True, out_dtype=bf16, scale=0.5, name=tag + "_d_wdown")
    (da, db), _ = _rows_bwd(_f_swiglu2, [a, b], [], [dc], name=tag + "_act_bwd", want_rows=[0, 1], row_dtypes={0: bf16, 1: bf16})
    da, db = da.reshape(N_DEV, T, nf), db.reshape(N_DEV, T, nf)
    dh = _mm(da, wg, tb=True, a_blk="k", b_blk="k", name=tag + "_d_h_gate")
    dh = _mm(db, wu, tb=True, a_blk="k", b_blk="k", add=dh, name=tag + "_d_h_up")
    dwg = _mm(h, da, ta=True, b_blk="n", o_blk="n", o_slots=True, out_dtype=bf16, name=tag + "_d_wgate")
    dwu = _mm(h, db, ta=True, b_blk="n", o_blk="n", o_slots=True, out_dtype=bf16, name=tag + "_d_wup")
    (dx,), (dn,) = _rows_bwd(_f_rmsnorm, [x], [n], [dh], name=tag + "_norm_bwd", want_rows=[0], adds={0: dy})
    return dx, dn, dwg, dwu, dwd


def kernel(x, ffn1_norm, ffn1_w_gate, ffn1_w_up, ffn1_w_down, mix_norm, w_in, conv_w, conv_b, s5_A_re, s5_A_im, s5_log_dt, s5_B_re, s5_B_im, s5_C_re, s5_C_im, s5_D, s5_w_glu, s5_b_glu, ssd_A_log, ssd_dt_bias, ssd_D, ssd_norm, w_proj_s5, w_proj_ssd, b_gate, w_out, ffn2_norm, ffn2_w_gate, ffn2_w_up, ffn2_w_down, final_norm, loss_target, m_ffn1_norm, m_ffn1_w_gate, m_ffn1_w_up, m_ffn1_w_down, m_mix_norm, m_w_in, m_conv_w, m_conv_b, m_s5_A_re, m_s5_A_im, m_s5_log_dt, m_s5_B_re, m_s5_B_im, m_s5_C_re, m_s5_C_im, m_s5_D, m_s5_w_glu, m_s5_b_glu, m_ssd_A_log, m_ssd_dt_bias, m_ssd_D, m_ssd_norm, m_w_proj_s5, m_w_proj_ssd, m_b_gate, m_w_out, m_ffn2_norm, m_ffn2_w_gate, m_ffn2_w_up, m_ffn2_w_down, m_final_norm, v_ffn1_norm, v_ffn1_w_gate, v_ffn1_w_up, v_ffn1_w_down, v_mix_norm, v_w_in, v_conv_w, v_conv_b, v_s5_A_re, v_s5_A_im, v_s5_log_dt, v_s5_B_re, v_s5_B_im, v_s5_C_re, v_s5_C_im, v_s5_D, v_s5_w_glu, v_s5_b_glu, v_ssd_A_log, v_ssd_dt_bias, v_ssd_D, v_ssd_norm, v_w_proj_s5, v_w_proj_ssd, v_b_gate, v_w_out, v_ffn2_norm, v_ffn2_w_gate, v_ffn2_w_up, v_ffn2_w_down, v_final_norm):
    P = dict(locals())
    T, D = x.shape[1], x.shape[2]
    x0, tgt = x[0], loss_target[0]
    sh = {k: P[k][0] for k in _BIG}

    gathered = _all_gather([sh[k] if k == "conv_w" else sh[k].astype(bf16) for k in _BIG], name="gather_weights")
    W = dict(zip(_BIG, gathered))
    whole = lambda k: W[k].reshape(-1, D)
    g_win = W["w_in"]
    conv_w_full = W["conv_w"].transpose(1, 0, 2).reshape(CONV_K, -1)

    d_inner = N_DEV * sh["w_proj_ssd"].shape[0]
    conv_dim = conv_w_full.shape[1]
    H = ssd_A_log.shape[1]
    G = (conv_dim - d_inner) // (2 * SSD_STATE)
    hpg = H // G
    nc = T // CHUNK
    Gs = D // S5_GROUP
    nblk = Gs // S5_GPB
    NS = Gs * S5_STATE
    seg_lens = (D, d_inner, conv_dim, H, 2 * D)
    w_u, w_z, w_xbc, w_dt, w_gl = _unshard_w_in(g_win, seg_lens, name="unshard_w_in")
    w_glu, w_p5, w_pssd, w_o = whole("s5_w_glu"), whole("w_proj_s5"), whole("w_proj_ssd"), whole("w_out")

    ffn1_w = (W["ffn1_w_gate"], W["ffn1_w_up"], W["ffn1_w_down"])
    ffn2_w = (W["ffn2_w_gate"], W["ffn2_w_up"], W["ffn2_w_down"])
    x1, sv1 = _ffn_fwd(x0, ffn1_norm, *ffn1_w, "ffn1")
    h2 = _rows(_f_rmsnorm, [x1], [mix_norm], [(D, bf16)], name="mix_norm")[0]
    u_p = _mm(h2, w_u, o_seg=True, name="in_u")
    z = _mm(h2, w_z, name="in_z")
    xbc = _mm(h2, w_xbc, name="in_xbc")
    gl = _mm(h2, w_gl, name="in_gate")
    dtr = _mm(h2, w_dt, name="in_dt")

    rep = lambda a: jnp.repeat(a, S5_GROUP, axis=0)
    lr, li, ldt = s5_A_re[0], s5_A_im[0], s5_log_dt[0].reshape(Gs, 1)
    brt = s5_B_re[0].transpose(0, 2, 1).reshape(Gs * S5_GROUP, S5_STATE)
    bit = s5_B_im[0].transpose(0, 2, 1).reshape(Gs * S5_GROUP, S5_STATE)
    prep_args = (lr, li, ldt, rep(lr), rep(li), rep(ldt), brt, bit)
    ar, ai, bbrt, bbit = _s5_prep(prep_args, name="s5_prep")
    a_r, a_i = ar.reshape(1, NS), ai.reshape(1, NS)
    wb_r = _block_diag(bbrt.reshape(nblk, S5_GPB, S5_GROUP, S5_STATE)).astype(bf16)
    wb_i = _block_diag(bbit.reshape(nblk, S5_GPB, S5_GROUP, S5_STATE)).astype(bf16)
    c4r = s5_C_re[0].reshape(nblk, S5_GPB, S5_GROUP, S5_STATE).transpose(0, 1, 3, 2)
    c4i = s5_C_im[0].reshape(nblk, S5_GPB, S5_GROUP, S5_STATE).transpose(0, 1, 3, 2)
    wc_r, wc_i = _block_diag(c4r).astype(bf16), _block_diag(c4i).astype(bf16)
    sl_r, sl_i, p_r, p_i = _s5_local_scan(u_p, wb_r, wb_i, a_r, a_i, reverse=False, name="s5_scan")
    c_r, c_i = _s5_carry(sl_r[T - NSEG:], sl_i[T - NSEG:], p_r, p_i, reverse=False, name="s5_carry")
    s_r, s_i, ylin = _s5_fix_out(sl_r, sl_i, a_r, a_i, c_r, c_i, wc_r, wc_i, name="s5_fix_out")
    g5 = _rows(_f_s5_post, [ylin, u_p], [s5_D], [(D, f32)], name="s5_gelu")[0]
    v5 = _mm(g5, w_glu, name="s5_glu_mm")
    o5 = _rows(_f_glu, [g5, v5], [s5_b_glu], [(D, bf16)], name="s5_glu")[0]
    p5 = _mm(o5, w_p5, a_seg=True, name="proj_s5")

    xc = _conv_fwd(xbc, conv_w_full, conv_b, name="conv")
    bias_p, alog_p = _pad_cols(ssd_dt_bias, LANES), _pad_cols(ssd_A_log, LANES)
    dt_p, da_p = _rows(_f_dt, [dtr], [bias_p, alog_p], [(LANES, f32), (LANES, f32)], name="ssd_dt")
    col_l = lambda a: a[:, :H].reshape(T, G, hpg).transpose(1, 0, 2)
    row_l = lambda a: a.reshape(G, nc, CHUNK, hpg).transpose(0, 1, 3, 2)
    dtc, dac = col_l(dt_p), col_l(da_p)
    dtw, daw = row_l(dtc), row_l(dac)
    dh = ssd_D.reshape(G, 1, hpg)
    ssd_in = (xc, dtc, dac, dtw, daw, dh)
    y_ssd, hs = _ssd_fwd(*ssd_in, d_inner=d_inner, name="ssd")
    yn = _rows(_f_gated_norm, [y_ssd, z], [ssd_norm], [(d_inner, bf16)], name="ssd_gated_norm")[0]
    pssd = _mm(yn, w_pssd, name="proj_ssd")

    merged = _rows(_f_merge, [gl, p5, pssd], [b_gate], [(D, bf16)], name="merge")[0]
    x2 = _mm(merged, w_o, add=x1, name="out_proj")
    x3, sv2 = _ffn_fwd(x2, ffn2_norm, *ffn2_w, "ffn2")
    lossv, dx3, d_final = _loss_stage(x3, tgt, final_norm.reshape(1, D), name="loss")

    gw = {}
    gs = {"final_norm": d_final}
    slot_mm = lambda a_, b_, name, **kw: _mm(a_, b_, ta=True, o_blk="m", o_slots=True, out_dtype=bf16, name=name, **kw)
    dx2, gs["ffn2_norm"], gw["ffn2_w_gate"], gw["ffn2_w_up"], gw["ffn2_w_down"] = _ffn_bwd(sv2, *ffn2_w, dx3, "ffn2")

    dmerged = _mm(dx2, w_o, tb=True, name="d_merged")
    gw["w_out"] = slot_mm(merged, dx2, "d_w_out")
    (dgl, dp5, dpssd), (gs["b_gate"],) = _rows_bwd(_f_merge, [gl, p5, pssd], [b_gate], [dmerged], name="merge_bwd", want_rows=[0, 1, 2])

    dyn = _mm(dpssd, w_pssd, tb=True, name="d_yn")
    gw["w_proj_ssd"] = slot_mm(yn, dpssd, "d_w_proj_ssd")
    (dyssd, dz), (gs["ssd_norm"],) = _rows_bwd(_f_gated_norm, [y_ssd, z], [ssd_norm], [dyn], name="ssd_gated_norm_bwd", want_rows=[0, 1])
    dxs, dbm, dcm, ddtc, ddac, ddtw, ddaw, ddh = _ssd_bwd(*ssd_in, hs, dyssd, d_inner=d_inner, name="ssd_bwd")
    fold = lambda dc_, dw_: _pad_cols((dc_ + dw_.transpose(0, 1, 3, 2).reshape(G, T, hpg)).transpose(1, 0, 2).reshape(T, H), LANES)
    (ddtr,), (dbias_p, dalog_p) = _rows_bwd(_f_dt, [dtr], [bias_p, alog_p], [fold(ddtc, ddtw), fold(ddac, ddaw)], name="ssd_dt_bwd", want_rows=[0])
    gs["ssd_dt_bias"], gs["ssd_A_log"], gs["ssd_D"] = dbias_p[:, :H], dalog_p[:, :H], ddh.reshape(1, H)
    dxbc, d_conv_w, gs["conv_b"] = _conv_bwd(xbc, conv_w_full, conv_b, jnp.concatenate([dxs, dbm, dcm], axis=1), name="conv_bwd")
    cwk = sh["conv_w"].shape[1]
    gw["conv_w"] = d_conv_w.reshape(CONV_K, N_CHIP, 2, cwk).transpose(2, 1, 0, 3).reshape(N_DEV, CONV_K, cwk)

    do5 = _mm(dp5, w_p5, tb=True, o_seg=True, name="d_o5")
    gw["w_proj_s5"] = slot_mm(o5, dp5, "d_w_proj_s5", a_seg=True)
    (dg5a, dv5), (gs["s5_b_glu"],) = _rows_bwd(_f_glu, [g5, v5], [s5_b_glu], [do5], name="s5_glu_bwd", want_rows=[0, 1])
    dg5 = _mm(dv5, w_glu, tb=True, add=dg5a, name="d_g5")
    gw["s5_w_glu"] = slot_mm(g5, dv5, "d_w_glu")
    (dylin, du_a), (gs["s5_D"],) = _rows_bwd(_f_s5_post, [ylin, u_p], [s5_D], [dg5], name="s5_gelu_bwd", want_rows=[0, 1])
    wct_r, wct_i = wc_r.transpose(0, 2, 1), -wc_i.transpose(0, 2, 1)
    ql_r, ql_i, pb_r, pb_i = _s5_local_scan(dylin, wct_r, wct_i, a_r, -a_i, reverse=True, name="s5_scan_bwd")
    cb_r, cb_i = _s5_carry(ql_r[:NSEG], ql_i[:NSEG], pb_r, pb_i, reverse=True, name="s5_carry_bwd")
    tc = min(S5_TC, T)

    def before_blocks(s):
        last = s.reshape(T // tc, tc, NS)[:, tc - NSEG:, :]
        wrap = jnp.concatenate([jnp.zeros((1, 1, NS), f32), last[-1:, : NSEG - 1, :]], axis=1)
        return jnp.concatenate([wrap, last[:-1]], axis=0)

    du_p, dwb_r, dwb_i, dwc_r, dwc_i, d_ar, d_ai = _s5_fix_bwd(
        ql_r, ql_i, a_r, -a_i, cb_r, cb_i, s_r, s_i, before_blocks(s_r), before_blocks(s_i), u_p, dylin, du_a, wb_r, wb_i, name="s5_fix_bwd")
    unblk = lambda w: _block_diag_t(w, S5_GROUP, S5_STATE).reshape(Gs * S5_GROUP, S5_STATE)
    rsum = jnp.repeat(jnp.eye(Gs, dtype=f32), S5_GROUP, axis=1)
    d_lr, d_li, d_ldt, d_brt, d_bit = _s5_prep_bwd(
        prep_args, (d_ar.reshape(Gs, S5_STATE), d_ai.reshape(Gs, S5_STATE), unblk(dwb_r), unblk(dwb_i)), rsum, name="s5_prep_bwd")
    gs["s5_A_re"], gs["s5_A_im"], gs["s5_log_dt"] = d_lr, d_li, d_ldt.reshape(1, Gs)
    gs["s5_B_re"] = d_brt.reshape(Gs, S5_GROUP, S5_STATE).transpose(0, 2, 1)
    gs["s5_B_im"] = d_bit.reshape(Gs, S5_GROUP, S5_STATE).transpose(0, 2, 1)
    gs["s5_C_re"] = _block_diag_t(dwc_r, S5_STATE, S5_GROUP).transpose(0, 1, 3, 2).reshape(Gs, S5_GROUP, S5_STATE)
    gs["s5_C_im"] = _block_diag_t(dwc_i, S5_STATE, S5_GROUP).transpose(0, 1, 3, 2).reshape(Gs, S5_GROUP, S5_STATE)

    dh2 = _mm(du_p, w_u, tb=True, a_seg=True, name="d_h2_u")
    dh2 = _mm(dz, w_z, tb=True, add=dh2, name="d_h2_z")
    dh2 = _mm(dxbc, w_xbc, tb=True, add=dh2, name="d_h2_xbc")
    dh2 = _mm(dgl, w_gl, tb=True, add=dh2, name="d_h2_gate")
    dh2 = _mm(ddtr, w_dt, tb=True, add=dh2, name="d_h2_dt")
    d_w_in = [_mm(h2, du_p, ta=True, b_seg=True, name="d_w_u"), _mm(h2, dz, ta=True, name="d_w_z"), _mm(h2, dxbc, ta=True, name="d_w_xbc"),
              _mm(h2, ddtr, ta=True, name="d_w_dt"), _mm(h2, dgl, ta=True, name="d_w_gate")]
    gw["w_in"] = _reshard_w_in(d_w_in, seg_lens, sh["w_in"].shape[1], name="reshard_d_w_in")
    (dx1,), (gs["mix_norm"],) = _rows_bwd(_f_rmsnorm, [x1], [mix_norm], [dh2], name="mix_norm_bwd", want_rows=[0], adds={0: dx2})
    dx0, gs["ffn1_norm"], gw["ffn1_w_gate"], gw["ffn1_w_up"], gw["ffn1_w_down"] = _ffn_bwd(sv1, *ffn1_w, dx1, "ffn1")

    core = lax.axis_index("c").astype(jnp.int32).reshape(1)
    chip = (2 * lax.axis_index("x") + lax.axis_index("y")).astype(jnp.int32).reshape(1)
    sib = _exchange_sibling([gw[k] for k in _BIG], name="exchange_grads_sibling")
    chip_sums = {k: _chip_sum(gw[k], s_, core, name="chip_sum_" + k) for k, s_ in zip(_BIG, sib)}
    arrived = dict(zip(_BIG, _exchange_chips([chip_sums[k] for k in _BIG], name="exchange_grads_chips")))
    small_shapes = {k: (P[k][0].shape if P[k].ndim > 1 else P[k].shape) for k in _SMALL}
    pack = lambda d: jnp.concatenate([_pad_flat(d[k], TILE_ELEMS) for k in _SMALL]).reshape(-1, LANES)
    gsmall = _sum_slots(_all_gather([pack(gs)], name="gather_small_grads")[0], name="sum_small_grads")
    snum = {k: math.prod(small_shapes[k]) for k in _SMALL}
    ssz = {k: -(-snum[k] // TILE_ELEMS) * TILE_ELEMS for k in _SMALL}

    grads, delta, new_m, new_v = {}, {}, {}, {}
    for k in _BIG:
        grads[k], delta[k], new_m[k], new_v[k] = _reduce_adamw(
            chip_sums[k], arrived[k], chip, P[k][0], P["m_" + k][0], P["v_" + k][0], name="adamw_" + k)
    d_s, m_s, v_s = _adamw(gsmall, pack({k: P[k] for k in _SMALL}), pack({k: P["m_" + k] for k in _SMALL}),
                           pack({k: P["v_" + k] for k in _SMALL}), name="adamw_small")
    off = 0
    gflat, dflat, mflat, vflat = gsmall.reshape(-1), d_s.reshape(-1), m_s.reshape(-1), v_s.reshape(-1)
    for k in _SMALL:
        n = snum[k]
        grads[k], delta[k], new_m[k], new_v[k] = (a[off:off + n] for a in (gflat, dflat, mflat, vflat))
        off += ssz[k]

    loss = lax.psum(lossv[0, 0], ("x", "y", "c"))
    out = [loss, dx0.reshape(x.shape)]
    for d in (grads, delta, new_m, new_v):
        out += [d[k].reshape(P[k].shape) for k in _WEIGHTS]
    return tuple(out)
```

```python
import math

import jax
import jax.numpy as jnp
from jax import lax
from jax.experimental import pallas as pl
from jax.experimental.pallas import tpu as pltpu

f32 = jnp.float32
bf16 = jnp.bfloat16
_S = jax.ShapeDtypeStruct

EPS = 1e-6
S5_GROUP = 16
S5_STATE = 64
HEADDIM = 64
SSD_STATE = 128
CHUNK = 64
CONV_K = 4
NSEG = 8
S5_GPB = 8
N_DEV = 8
LANES = 128
TILE_ELEMS = 8 * LANES

ADAM_LR = 0.001
ADAM_B1 = 0.9
ADAM_B2 = 0.999
ADAM_EPS = 1e-08
ADAM_WD = 0.01
ADAM_STEP = 10

VMEM_LIMIT = 56 * 1024 * 1024
MM_FULL_K = 3072
MM_MAX_TN = 3072
MESH = pl.DeviceIdType.MESH


def _cparams(sem=None):
    return pltpu.CompilerParams(dimension_semantics=sem, vmem_limit_bytes=VMEM_LIMIT)


def _pick(dim, pref, align=LANES):
    best = None
    t = align
    while t <= min(dim, pref):
        if dim % t == 0:
            best = t
        t += align
    return best or dim


def _slot_of(k):
    return (k & 1) * (N_DEV // 2) + (k >> 1)


def _mm(a, b, *, name, ta=False, tb=False, a_blk=None, b_blk=None, o_blk=None, o_slots=False, a_seg=False, b_seg=False,
        o_seg=False, tm=None, out_dtype=f32, scale=1.0, add=None):
    a2, b2 = a.shape[-2:], b.shape[-2:]
    Ma, Ka = (a2[1], a2[0]) if ta else a2
    Kb, Nb = (b2[1], b2[0]) if tb else b2
    M = Ma * (a.shape[0] if a_blk == "m" else 1)
    K = Ka * (a.shape[0] if a_blk == "k" else 1)
    N = Nb * (b.shape[0] if b_blk == "n" else 1)
    assert K == Kb * (b.shape[0] if b_blk == "k" else 1), (a.shape, b.shape, ta, tb, a_blk, b_blk)
    assert (a.ndim == 3) == (a_blk is not None) and (b.ndim == 3) == (b_blk is not None)
    tm = Ma if a_blk == "m" else (tm or _pick(M, 512))
    tn = Nb if b_blk == "n" else _pick(N, MM_MAX_TN)
    if a_blk == "k" or b_blk == "k":
        tk = Ka if a_blk == "k" else Kb
        assert tk == (Kb if b_blk == "k" else tk)
    else:
        tk = K if K <= MM_FULL_K else _pick(K, 1024)
    if (a_seg and not ta) or o_seg:
        tm = M // NSEG
    if (a_seg and ta) or b_seg:
        tk = K // NSEG
    gm, gn, nk = M // tm, N // tn, K // tk
    assert not (add is not None and (o_seg or o_blk)) and not (o_blk and o_seg)

    if a_seg:
        assert a.ndim == 2
        a = a.reshape(a.shape[0] // NSEG, NSEG * a.shape[1])
        if ta:
            a_spec = pl.BlockSpec((tk, tm), lambda i, j, k: (0, k * (Ma // tm) + i))
        else:
            a_spec = pl.BlockSpec((tm, tk), lambda i, j, k: (0, i * (Ka // tk) + k))
    elif a.ndim == 3:
        lead = (lambda i, k: i) if a_blk == "m" else (lambda i, k: k)
        if ta:
            a_spec = pl.BlockSpec((None, tk, tm), lambda i, j, k: (lead(i, k), 0 if a_blk == "k" else k, 0 if a_blk == "m" else i))
        else:
            a_spec = pl.BlockSpec((None, tm, tk), lambda i, j, k: (lead(i, k), 0 if a_blk == "m" else i, 0 if a_blk == "k" else k))
    else:
        a_spec = pl.BlockSpec((tk, tm), lambda i, j, k: (k, i)) if ta else pl.BlockSpec((tm, tk), lambda i, j, k: (i, k))
    if b_seg:
        assert b.ndim == 2 and not tb
        b = b.reshape(b.shape[0] // NSEG, NSEG * b.shape[1])
        b_spec = pl.BlockSpec((tk, tn), lambda i, j, k: (0, k * (Nb // tn) + j))
    elif b.ndim == 3:
        lead = (lambda j, k: j) if b_blk == "n" else (lambda j, k: k)
        if tb:
            b_spec = pl.BlockSpec((None, tn, tk), lambda i, j, k: (lead(j, k), 0 if b_blk == "n" else j, 0 if b_blk == "k" else k))
        else:
            b_spec = pl.BlockSpec((None, tk, tn), lambda i, j, k: (lead(j, k), 0 if b_blk == "k" else k, 0 if b_blk == "n" else j))
    else:
        b_spec = pl.BlockSpec((tn, tk), lambda i, j, k: (j, k)) if tb else pl.BlockSpec((tk, tn), lambda i, j, k: (k, j))
    slot = _slot_of if o_slots else (lambda k: k)
    if o_blk == "n":
        assert gn == N_DEV or not o_slots
        o_shape, o_spec = (gn, M, tn), pl.BlockSpec((None, tm, tn), lambda i, j, k: (slot(j), i, 0))
    elif o_blk == "m" and o_slots and gm < N_DEV:
        rs = M // N_DEV
        per_tile = tm // rs
        assert per_tile % 2 == 0 and tm % rs == 0
        o_shape = (2, N_CHIP, rs, N)
        o_spec = pl.BlockSpec((2, per_tile // 2, rs, tn), lambda i, j, k: (0, i, 0, j))
    elif o_blk == "m":
        assert gm == N_DEV or not o_slots
        o_shape, o_spec = (gm, tm, N), pl.BlockSpec((None, tm, tn), lambda i, j, k: (slot(i), 0, j))
    elif o_seg:
        o_shape, o_spec = (tm, NSEG * N), pl.BlockSpec((tm, tn), lambda i, j, k: (0, i * (N // tn) + j))
    else:
        o_shape, o_spec = (M, N), pl.BlockSpec((tm, tn), lambda i, j, k: (i, j))
    dims = (((0 if ta else 1,), (1 if tb else 0,)), ((), ()))
    has_add = add is not None

    def body(*refs):
        a_ref, b_ref = refs[0], refs[1]
        add_ref = refs[2] if has_add else None
        o_ref, acc_ref = refs[-2], refs[-1]
        k = pl.program_id(2)

        @pl.when(k == 0)
        def _():
            acc_ref[...] = jnp.zeros_like(acc_ref)

        acc_ref[...] += lax.dot_general(a_ref[...].astype(bf16), b_ref[...].astype(bf16), dims, preferred_element_type=f32)

        @pl.when(k == nk - 1)
        def _():
            r = acc_ref[...] * scale
            if has_add:
                r = r + add_ref[...].astype(f32)
            if len(o_shape) == 4:
                rs = o_shape[2]
                for chip_l in range(o_ref.shape[1]):
                    for core in range(2):
                        dev = 2 * chip_l + core
                        o_ref[core, chip_l] = r[dev * rs:(dev + 1) * rs].astype(out_dtype)
            else:
                o_ref[...] = r.astype(out_dtype)

    ins = [a, b] + ([add] if has_add else [])
    in_specs = [a_spec, b_spec] + ([o_spec] if has_add else [])
    out = pl.pallas_call(
        body, name=name, grid=(gm, gn, nk), in_specs=in_specs, out_specs=o_spec,
        out_shape=_S(o_shape, out_dtype), scratch_shapes=[pltpu.VMEM((tm, tn), f32)],
        compiler_params=_cparams(("parallel", "parallel", "arbitrary")),
    )(*ins)
    if len(o_shape) == 4:
        return out.reshape(N_DEV, o_shape[2], N)
    return out.reshape(M, N) if o_seg else out


def _row_tile(T, widths):
    budget = 6 * 1024 * 1024
    tb = max(16, budget // (4 * sum(widths)))
    return _pick(T, tb, align=16)


def _rows(fn, rows, params, outs, *, name):
    T = rows[0].shape[0]
    nr, npar = len(rows), len(params)
    tb = _row_tile(T, [r.shape[1] for r in rows] + [w for w, _ in outs])

    def body(*refs):
        ins = [r[...].astype(f32) for r in refs[: nr + npar]]
        res = fn(*ins)
        for o_ref, r in zip(refs[nr + npar:], res):
            o_ref[...] = r.astype(o_ref.dtype)

    in_specs = [pl.BlockSpec((tb, r.shape[1]), lambda i: (i, 0)) for r in rows]
    in_specs += [pl.BlockSpec(p.shape, lambda i: (0, 0)) for p in params]
    out_specs = [pl.BlockSpec((tb, w), lambda i: (i, 0)) for w, _ in outs]
    res = pl.pallas_call(
        body, name=name, grid=(T // tb,), in_specs=in_specs, out_specs=out_specs,
        out_shape=[_S((T, w), d) for w, d in outs], compiler_params=_cparams(("parallel",)),
    )(*rows, *params)
    return tuple(res)


def _rows_bwd(fn, rows, params, cots, *, name, want_rows, row_dtypes=None, adds=None):
    T = rows[0].shape[0]
    nr, npar, nc = len(rows), len(params), len(cots)
    adds = adds or {}
    add_idx = sorted(adds)
    row_dtypes = row_dtypes or {}
    widths = [r.shape[1] for r in rows] + [c.shape[1] for c in cots] + [rows[i].shape[1] for i in want_rows]
    tb = _row_tile(T, widths)

    def body(*refs):
        ins = [r[...].astype(f32) for r in refs[: nr + npar]]
        cot = tuple(r[...].astype(f32) for r in refs[nr + npar: nr + npar + nc])
        add_refs = refs[nr + npar + nc: nr + npar + nc + len(add_idx)]
        out_refs = refs[nr + npar + nc + len(add_idx):]
        _, vjp = jax.vjp(lambda *a: tuple(fn(*a)), *ins)
        g = vjp(cot)
        for o_ref, i in zip(out_refs[: len(want_rows)], want_rows):
            r = g[i]
            if i in adds:
                r = r + add_refs[add_idx.index(i)][...].astype(f32)
            o_ref[...] = r.astype(o_ref.dtype)
        first = pl.program_id(0) == 0
        for o_ref, gp in zip(out_refs[len(want_rows):], g[nr:]):
            @pl.when(first)
            def _(o_ref=o_ref):
                o_ref[...] = jnp.zeros_like(o_ref)

            o_ref[...] += gp

    in_specs = [pl.BlockSpec((tb, r.shape[1]), lambda i: (i, 0)) for r in rows]
    in_specs += [pl.BlockSpec(p.shape, lambda i: (0, 0)) for p in params]
    in_specs += [pl.BlockSpec((tb, c.shape[1]), lambda i: (i, 0)) for c in cots]
    in_specs += [pl.BlockSpec((tb, adds[i].shape[1]), lambda i_: (i_, 0)) for i in add_idx]
    out_specs = [pl.BlockSpec((tb, rows[i].shape[1]), lambda i_: (i_, 0)) for i in want_rows]
    out_specs += [pl.BlockSpec(p.shape, lambda i: (0, 0)) for p in params]
    out_shape = [_S(rows[i].shape, row_dtypes.get(i, f32)) for i in want_rows] + [_S(p.shape, f32) for p in params]
    res = pl.pallas_call(
        body, name=name, grid=(T // tb,), in_specs=in_specs, out_specs=out_specs, out_shape=out_shape,
        compiler_params=_cparams(("arbitrary",)),
    )(*rows, *params, *cots, *[adds[i] for i in add_idx])
    return list(res[: len(want_rows)]), list(res[len(want_rows):])


def _f_rmsnorm(x, g):
    return (x * lax.rsqrt(jnp.mean(x * x, axis=-1, keepdims=True) + EPS) * g,)


def _f_swiglu(ab):
    F = ab.shape[1] // 2
    return (jax.nn.silu(ab[:, :F]) * ab[:, F:],)


def _f_s5_post(y, u, d):
    return (jax.nn.gelu(y + d * u),)


def _f_glu(g, v, b):
    return (g * jax.nn.sigmoid(v + b),)


def _f_gated_norm(y, z, w):
    return _f_rmsnorm(y * jax.nn.silu(z), w)


def _f_merge(gl, p5, pssd, b):
    D = p5.shape[1]
    gates = jax.nn.sigmoid(gl + b)
    return (gates[:, :D] * p5 + gates[:, D:] * pssd,)


def _f_dt(dtr, bias, a_log):
    dt = jax.nn.softplus(dtr + bias)
    return dt, dt * (-jnp.exp(a_log))


def _loss_stage(x, tgt, g, *, name):
    T, D = x.shape
    tb = _row_tile(T, [D, D, D])

    def f(xb, gb, tb_):
        y = _f_rmsnorm(xb, gb)[0]
        return 0.5 * jnp.sum(jnp.mean(jnp.square(y - tb_), axis=-1, keepdims=True), axis=0, keepdims=True)

    def body(x_ref, t_ref, g_ref, l_ref, dx_ref, dg_ref):
        tv = t_ref[...]
        val, vjp = jax.vjp(lambda a, b: f(a, b, tv), x_ref[...], g_ref[...])
        dx, dg = vjp(jnp.ones((1, 1), f32))
        dx_ref[...] = dx

        @pl.when(pl.program_id(0) == 0)
        def _():
            l_ref[...] = jnp.zeros_like(l_ref)
            dg_ref[...] = jnp.zeros_like(dg_ref)

        l_ref[...] += jnp.broadcast_to(val, l_ref.shape)
        dg_ref[...] += dg

    row = pl.BlockSpec((tb, D), lambda i: (i, 0))
    par = pl.BlockSpec((1, D), lambda i: (0, 0))
    return pl.pallas_call(
        body, name=name, grid=(T // tb,), in_specs=[row, row, par],
        out_specs=[pl.BlockSpec((1, LANES), lambda i: (0, 0)), row, par],
        out_shape=[_S((1, LANES), f32), _S((T, D), f32), _S((1, D), f32)], compiler_params=_cparams(("arbitrary",)),
    )(x, tgt, g)


def _shift_down(x, s):
    if s == 0:
        return x
    t = lax.broadcasted_iota(jnp.int32, x.shape, 0)
    return jnp.where(t >= s, pltpu.roll(x, s, axis=0), 0.0)


def _shift_up(x, s):
    if s == 0:
        return x
    T = x.shape[0]
    t = lax.broadcasted_iota(jnp.int32, x.shape, 0)
    return jnp.where(t < T - s, pltpu.roll(x, T - s, axis=0), 0.0)


def _conv_pre(x, w, b):
    pre = b
    for k in range(CONV_K):
        pre = pre + w[k:k + 1, :] * _shift_down(x, CONV_K - 1 - k)
    return pre


def _conv_fwd(x, w, b, *, name):
    T, C = x.shape
    cb = _pick(C, 256)

    def body(x_ref, w_ref, b_ref, o_ref):
        o_ref[...] = jax.nn.silu(_conv_pre(x_ref[...], w_ref[...], b_ref[...]))

    col = pl.BlockSpec((T, cb), lambda j: (0, j))
    return pl.pallas_call(
        body, name=name, grid=(C // cb,), in_specs=[col, pl.BlockSpec((CONV_K, cb), lambda j: (0, j)), pl.BlockSpec((1, cb), lambda j: (0, j))],
        out_specs=col, out_shape=_S((T, C), f32), compiler_params=_cparams(("parallel",)),
    )(x, w, b)


def _conv_bwd(x, w, b, dy, *, name):
    T, C = x.shape
    cb = _pick(C, 256)

    def body(x_ref, w_ref, b_ref, dy_ref, dx_ref, dw_ref, db_ref):
        xv, wv = x_ref[...], w_ref[...]
        pre = _conv_pre(xv, wv, b_ref[...])
        sg = jax.nn.sigmoid(pre)
        dpre = dy_ref[...] * sg * (1.0 + pre * (1.0 - sg))
        dx = jnp.zeros_like(xv)
        for k in range(CONV_K):
            s = CONV_K - 1 - k
            dx = dx + wv[k:k + 1, :] * _shift_up(dpre, s)
            dw_ref[k:k + 1, :] = jnp.sum(dpre * _shift_down(xv, s), axis=0, keepdims=True)
        dx_ref[...] = dx
        db_ref[...] = jnp.sum(dpre, axis=0, keepdims=True)

    col = pl.BlockSpec((T, cb), lambda j: (0, j))
    wsp = pl.BlockSpec((CONV_K, cb), lambda j: (0, j))
    bsp = pl.BlockSpec((1, cb), lambda j: (0, j))
    return pl.pallas_call(
        body, name=name, grid=(C // cb,), in_specs=[col, wsp, bsp, col], out_specs=[col, wsp, bsp],
        out_shape=[_S((T, C), f32), _S((CONV_K, C), f32), _S((1, C), f32)], compiler_params=_cparams(("parallel",)),
    )(x, w, b, dy)


def _f_s5_prep(lr, li, ldt, lrb, lib, ldtb, brt, bit):
    def disc(lr_, li_, ldt_):
        dt = jnp.exp(ldt_)
        mag = jnp.exp(lr_ * dt)
        ar, ai = mag * jnp.cos(li_ * dt), mag * jnp.sin(li_ * dt)
        den = lr_ * lr_ + li_ * li_
        cr = ((ar - 1.0) * lr_ + ai * li_) / den
        ci = (ai * lr_ - (ar - 1.0) * li_) / den
        return ar, ai, cr, ci

    ar, ai, _, _ = disc(lr, li, ldt)
    _, _, cr, ci = disc(lrb, lib, ldtb)
    return ar, ai, cr * brt - ci * bit, cr * bit + ci * brt


def _s5_prep(args, *, name):
    G, N = args[0].shape
    GM = args[3].shape[0]

    def body(*refs):
        res = _f_s5_prep(*[r[...] for r in refs[:8]])
        for o, r in zip(refs[8:], res):
            o[...] = r

    return pl.pallas_call(body, name=name, out_shape=[_S((G, N), f32)] * 2 + [_S((GM, N), f32)] * 2)(*args)


def _s5_prep_bwd(args, cots, rsum, *, name):
    G, N = args[0].shape
    GM = args[3].shape[0]

    def body(*refs):
        ins = [r[...] for r in refs[:8]]
        cot = tuple(r[...] for r in refs[8:12])
        rs = refs[12][...]
        _, vjp = jax.vjp(_f_s5_prep, *ins)
        g = vjp(cot)
        fold = lambda v: jnp.dot(rs, v, preferred_element_type=f32, precision=lax.Precision.HIGHEST)
        o = refs[13:]
        o[0][...] = g[0] + fold(g[3])
        o[1][...] = g[1] + fold(g[4])
        o[2][...] = g[2] + fold(jnp.broadcast_to(g[5], (GM, LANES)))[:, 0:1]
        o[3][...] = g[6]
        o[4][...] = g[7]

    return pl.pallas_call(
        body, name=name, out_shape=[_S((G, N), f32), _S((G, N), f32), _S((G, 1), f32), _S((GM, N), f32), _S((GM, N), f32)],
    )(*args, *cots, rsum)


S5_TC = 512


def _s5_local_scan(src, w_r, w_i, a_r, a_i, *, reverse, name):
    T, C = src.shape
    nblk, cb, sb = w_r.shape
    NS = nblk * sb
    tc = min(S5_TC, T)
    nT, nt = T // tc, tc // NSEG
    tmap = (lambda i: nT - 1 - i) if reverse else (lambda i: i)

    def body(u_ref, wr_ref, wi_ref, ar_ref, ai_ref, sr_ref, si_ref, pr_ref, pi_ref, st_r, st_i, pw_r, pw_i):
        @pl.when(pl.program_id(1) == 0)
        def _():
            st_r[...] = jnp.zeros_like(st_r)
            st_i[...] = jnp.zeros_like(st_i)
            pw_r[...] = jnp.ones_like(pw_r)
            pw_i[...] = jnp.zeros_like(pw_i)

        u = u_ref[...].astype(bf16)
        sr_ref[...] = jnp.dot(u, wr_ref[...], preferred_element_type=f32)
        si_ref[...] = jnp.dot(u, wi_ref[...], preferred_element_type=f32)
        ar = jnp.broadcast_to(ar_ref[...], (NSEG, sb))
        ai = jnp.broadcast_to(ai_ref[...], (NSEG, sb))

        def step(k, c):
            cr, ci, qr, qi = c
            kk = (nt - 1 - k) if reverse else k
            rows = pl.ds(pl.multiple_of(kk * NSEG, NSEG), NSEG)
            nr = ar * cr - ai * ci + sr_ref[rows, :]
            ni = ar * ci + ai * cr + si_ref[rows, :]
            sr_ref[rows, :] = nr
            si_ref[rows, :] = ni
            return nr, ni, ar * qr - ai * qi, ar * qi + ai * qr

        cr, ci, qr, qi = lax.fori_loop(0, nt, step, (st_r[...], st_i[...], pw_r[...], pw_i[...]), unroll=8)
        st_r[...], st_i[...], pw_r[...], pw_i[...] = cr, ci, qr, qi
        pr_ref[...] = qr
        pi_ref[...] = qi

    blk = pl.BlockSpec((tc, sb), lambda j, i: (tmap(i), j))
    wsp = pl.BlockSpec((None, cb, sb), lambda j, i: (j, 0, 0))
    asp = pl.BlockSpec((1, sb), lambda j, i: (0, j))
    psp = pl.BlockSpec((NSEG, sb), lambda j, i: (0, j))
    return pl.pallas_call(
        body, name=name, grid=(nblk, nT), in_specs=[pl.BlockSpec((tc, cb), lambda j, i: (tmap(i), j)), wsp, wsp, asp, asp],
        out_specs=[blk, blk, psp, psp], out_shape=[_S((T, NS), f32)] * 2 + [_S((NSEG, NS), f32)] * 2,
        scratch_shapes=[pltpu.VMEM((NSEG, sb), f32)] * 4, compiler_params=_cparams(("parallel", "arbitrary")),
    )(src, w_r, w_i, a_r, a_i)


def _s5_carry(e_r, e_i, p_r, p_i, *, reverse, name):
    NS = e_r.shape[1]

    def body(er_ref, ei_ref, pr_ref, pi_ref, cr_ref, ci_ref):
        ar, ai = pr_ref[0:1, :], pi_ref[0:1, :]
        cr = jnp.zeros((1, NS), f32)
        ci = jnp.zeros((1, NS), f32)
        order = list(range(NSEG - 1, -1, -1)) if reverse else list(range(NSEG))
        cr_ref[order[0]:order[0] + 1, :] = cr
        ci_ref[order[0]:order[0] + 1, :] = ci
        for prev, q in zip(order[:-1], order[1:]):
            er, ei = er_ref[prev:prev + 1, :], ei_ref[prev:prev + 1, :]
            cr, ci = er + ar * cr - ai * ci, ei + ar * ci + ai * cr
            cr_ref[q:q + 1, :] = cr
            ci_ref[q:q + 1, :] = ci

    return pl.pallas_call(body, name=name, out_shape=[_S((NSEG, NS), f32)] * 2)(e_r, e_i, p_r, p_i)


def _s5_fix_out(sl_r, sl_i, a_r, a_i, c_r, c_i, wc_r, wc_i, *, name):
    T, NS = sl_r.shape
    nblk, sb, cb = wc_r.shape
    tc = min(S5_TC, T)
    nT, nt = T // tc, tc // NSEG

    def body(lr_ref, li_ref, ar_ref, ai_ref, cr_ref, ci_ref, wr_ref, wi_ref, sr_ref, si_ref, y_ref, pw_r, pw_i):
        @pl.when(pl.program_id(1) == 0)
        def _():
            pw_r[...] = jnp.ones_like(pw_r)
            pw_i[...] = jnp.zeros_like(pw_i)

        ar = jnp.broadcast_to(ar_ref[...], (NSEG, sb))
        ai = jnp.broadcast_to(ai_ref[...], (NSEG, sb))
        cr, ci = cr_ref[...], ci_ref[...]

        def step(k, c):
            qr, qi = c
            qr, qi = ar * qr - ai * qi, ar * qi + ai * qr
            rows = pl.ds(pl.multiple_of(k * NSEG, NSEG), NSEG)
            sr_ref[rows, :] = lr_ref[rows, :] + qr * cr - qi * ci
            si_ref[rows, :] = li_ref[rows, :] + qr * ci + qi * cr
            return qr, qi

        qr, qi = lax.fori_loop(0, nt, step, (pw_r[...], pw_i[...]), unroll=8)
        pw_r[...], pw_i[...] = qr, qi
        y_ref[...] = (jnp.dot(sr_ref[...].astype(bf16), wr_ref[...], preferred_element_type=f32)
                      - jnp.dot(si_ref[...].astype(bf16), wi_ref[...], preferred_element_type=f32))

    blk = pl.BlockSpec((tc, sb), lambda j, i: (i, j))
    asp = pl.BlockSpec((1, sb), lambda j, i: (0, j))
    csp = pl.BlockSpec((NSEG, sb), lambda j, i: (0, j))
    wsp = pl.BlockSpec((None, sb, cb), lambda j, i: (j, 0, 0))
    return pl.pallas_call(
        body, name=name, grid=(nblk, nT), in_specs=[blk, blk, asp, asp, csp, csp, wsp, wsp],
        out_specs=[blk, blk, pl.BlockSpec((tc, cb), lambda j, i: (i, j))],
        out_shape=[_S((T, NS), f32)] * 2 + [_S((T, nblk * cb), f32)],
        scratch_shapes=[pltpu.VMEM((NSEG, sb), f32)] * 2, compiler_params=_cparams(("parallel", "arbitrary")),
    )(sl_r, sl_i, a_r, a_i, c_r, c_i, wc_r, wc_i)


def _s5_fix_bwd(ql_r, ql_i, ab_r, ab_i, c_r, c_i, s_r, s_i, sb_r, sb_i, u, dy, du_add, w_r, w_i, *, name):
    T, NS = ql_r.shape
    nblk, cb, sb = w_r.shape
    tc = min(S5_TC, T)
    nT, nt = T // tc, tc // NSEG
    tmap = lambda i: nT - 1 - i

    def body(lr_ref, li_ref, ar_ref, ai_ref, cr_ref, ci_ref, sr_ref, si_ref, br_ref, bi_ref, u_ref, dy_ref, dua_ref, wr_ref, wi_ref,
             du_ref, dwr_ref, dwi_ref, dcr_ref, dci_ref, dar_ref, dai_ref, pw_r, pw_i, ac_r, ac_i, q_r, q_i):
        first = pl.program_id(1) == 0

        @pl.when(first)
        def _():
            pw_r[...] = jnp.ones_like(pw_r)
            pw_i[...] = jnp.zeros_like(pw_i)
            ac_r[...] = jnp.zeros_like(ac_r)
            ac_i[...] = jnp.zeros_like(ac_i)
            dwr_ref[...] = jnp.zeros_like(dwr_ref)
            dwi_ref[...] = jnp.zeros_like(dwi_ref)
            dcr_ref[...] = jnp.zeros_like(dcr_ref)
            dci_ref[...] = jnp.zeros_like(dci_ref)

        ar = jnp.broadcast_to(ar_ref[...], (NSEG, sb))
        ai = jnp.broadcast_to(ai_ref[...], (NSEG, sb))
        cr, ci = cr_ref[...], ci_ref[...]

        def fix(rows, qr, qi, spr, spi, accr, acci):
            qr, qi = ar * qr - ai * qi, ar * qi + ai * qr
            xr = lr_ref[rows, :] + qr * cr - qi * ci
            xi = li_ref[rows, :] + qr * ci + qi * cr
            q_r[rows, :] = xr
            q_i[rows, :] = xi
            return qr, qi, accr + xr * spr + xi * spi, acci + xi * spr - xr * spi

        def step(k, c):
            qr, qi, accr, acci = c
            kk = nt - 1 - k
            rows = pl.ds(pl.multiple_of(kk * NSEG, NSEG), NSEG)
            prev = pl.ds(pl.multiple_of((kk - 1) * NSEG, NSEG), NSEG)
            return fix(rows, qr, qi, sr_ref[prev, :], si_ref[prev, :], accr, acci)

        c = lax.fori_loop(0, nt - 1, step, (pw_r[...], pw_i[...], ac_r[...], ac_i[...]), unroll=7)
        qr, qi, accr, acci = fix(pl.ds(0, NSEG), *c[:2], br_ref[...], bi_ref[...], *c[2:])
        pw_r[...], pw_i[...], ac_r[...], ac_i[...] = qr, qi, accr, acci

        qrb, qib = q_r[...].astype(bf16), q_i[...].astype(bf16)
        nt_dims = (((1,), (1,)), ((), ()))
        tn_dims = (((0,), (0,)), ((), ()))
        du_ref[...] = (dua_ref[...] + lax.dot_general(qrb, wr_ref[...], nt_dims, preferred_element_type=f32)
                       + lax.dot_general(qib, wi_ref[...], nt_dims, preferred_element_type=f32))
        ub = u_ref[...].astype(bf16)
        dwr_ref[...] += lax.dot_general(ub, qrb, tn_dims, preferred_element_type=f32)
        dwi_ref[...] += lax.dot_general(ub, qib, tn_dims, preferred_element_type=f32)
        dyb = dy_ref[...].astype(bf16)
        dcr_ref[...] += lax.dot_general(sr_ref[...].astype(bf16), dyb, tn_dims, preferred_element_type=f32)
        dci_ref[...] -= lax.dot_general(si_ref[...].astype(bf16), dyb, tn_dims, preferred_element_type=f32)

        @pl.when(pl.program_id(1) == nT - 1)
        def _():
            dar_ref[...] = jnp.sum(accr, axis=0, keepdims=True)
            dai_ref[...] = jnp.sum(acci, axis=0, keepdims=True)

    blk = pl.BlockSpec((tc, sb), lambda j, i: (tmap(i), j))
    asp = pl.BlockSpec((1, sb), lambda j, i: (0, j))
    csp = pl.BlockSpec((NSEG, sb), lambda j, i: (0, j))
    bsp = pl.BlockSpec((None, NSEG, sb), lambda j, i: (tmap(i), 0, j))
    chn = pl.BlockSpec((tc, cb), lambda j, i: (tmap(i), j))
    wsp = pl.BlockSpec((None, cb, sb), lambda j, i: (j, 0, 0))
    wcs = pl.BlockSpec((None, sb, cb), lambda j, i: (j, 0, 0))
    return pl.pallas_call(
        body, name=name, grid=(nblk, nT), in_specs=[blk, blk, asp, asp, csp, csp, blk, blk, bsp, bsp, chn, chn, chn, wsp, wsp],
        out_specs=[chn, wsp, wsp, wcs, wcs, asp, asp],
        out_shape=[_S((T, nblk * cb), f32), _S((nblk, cb, sb), f32), _S((nblk, cb, sb), f32), _S((nblk, sb, cb), f32),
                   _S((nblk, sb, cb), f32), _S((1, NS), f32), _S((1, NS), f32)],
        scratch_shapes=[pltpu.VMEM((NSEG, sb), f32)] * 4 + [pltpu.VMEM((tc, sb), f32)] * 2,
        compiler_params=_cparams(("parallel", "arbitrary")),
    )(ql_r, ql_i, ab_r, ab_i, c_r, c_i, s_r, s_i, sb_r, sb_i, u, dy, du_add, w_r, w_i)


SSD2_TB = 512
_HI = lax.Precision.HIGHEST
_NN = (((1,), (0,)), ((), ()))
_NT = (((1,), (1,)), ((), ()))
_TN = (((0,), (0,)), ((), ()))


def _dotf(a, b, dims):
    return lax.dot_general(a.astype(bf16), b.astype(bf16), dims, preferred_element_type=f32)


def _doth(a, b, dims=_NN):
    return lax.dot_general(a, b, dims, preferred_element_type=f32, precision=_HI)


def _ssd_consts(hpg):
    W = hpg * CHUNK
    i = lax.broadcasted_iota(jnp.int32, (CHUNK, CHUNK), 0)
    j = lax.broadcasted_iota(jnp.int32, (CHUNK, CHUNK), 1)
    tril = (i >= j).astype(f32)
    r = lax.broadcasted_iota(jnp.int32, (W, W), 0)
    c = lax.broadcasted_iota(jnp.int32, (W, W), 1)
    bd = (r // CHUNK == c // CHUNK).astype(f32)
    triu_bd = bd * (r <= c).astype(f32)
    e_r = lax.broadcasted_iota(jnp.int32, (W, LANES), 0)
    e_c = lax.broadcasted_iota(jnp.int32, (W, LANES), 1)
    ered = (e_r // HEADDIM == e_c).astype(f32)
    return tril, jnp.tile(tril, (1, hpg)), bd, triu_bd, ered


def _ssd2_specs(G, hpg, tb, tmap, b_off, c_off):
    W = hpg * HEADDIM
    ncb = tb // CHUNK
    xsp = pl.BlockSpec((tb, W), lambda g, i: (tmap(i), g))
    bsp = pl.BlockSpec((tb, SSD_STATE), lambda g, i: (tmap(i), b_off + g))
    csp = pl.BlockSpec((tb, SSD_STATE), lambda g, i: (tmap(i), c_off + g))
    rsp = pl.BlockSpec((None, ncb, W), lambda g, i: (g, tmap(i), 0))
    dsp = pl.BlockSpec((1, W), lambda g, i: (0, g))
    hsp = pl.BlockSpec((None, ncb, SSD_STATE, W), lambda g, i: (g, tmap(i), 0, 0))
    const = lambda a: pl.BlockSpec(a.shape, lambda g, i: (0, 0))
    return xsp, bsp, csp, rsp, dsp, hsp, const


def _tile_rows(a, n):
    return jnp.concatenate([a] * n, axis=0)


def _ssd2_fwd(xc, dt4, a4, dtw, aw, d4, consts, *, d_inner, name):
    T = xc.shape[0]
    G, nc, W = dtw.shape
    hpg = W // CHUNK
    tb = min(SSD2_TB, T)
    nb, ncb = T // tb, tb // CHUNK
    b_off = d_inner // SSD_STATE
    xsp, bsp, csp, rsp, dsp, hsp, const = _ssd2_specs(G, hpg, tb, lambda i: i, b_off, b_off + G)
    tril, mask4, bd, triu_bd, _ = consts

    def body(x_ref, b_ref, c_ref, dt_ref, a_ref, dtw_ref, aw_ref, d_ref, tril_ref, mask_ref, bd_ref, tbd_ref, y_ref, hs_ref, h_scr):
        @pl.when(pl.program_id(1) == 0)
        def _():
            h_scr[...] = jnp.zeros_like(h_scr)

        acs_rows = _doth(aw_ref[...], tbd_ref[...])
        ht = h_scr[...]
        for c in range(ncb):
            rows = slice(c * CHUNK, (c + 1) * CHUNK)
            x, bm, cm = x_ref[rows, :], b_ref[rows, :], c_ref[rows, :]
            acs = _doth(tril_ref[...], a_ref[rows, :])
            lmat = jnp.where(mask_ref[...] > 0, jnp.exp(jnp.minimum(acs - acs_rows[c:c + 1, :], 0.0)), 0.0)
            m4 = _dotf(cm, _tile_rows(bm, hpg), _NT) * lmat * dtw_ref[c:c + 1, :]
            xbd = _tile_rows(x, hpg) * bd_ref[...]
            hs_ref[c] = ht
            y_ref[rows, :] = _dotf(m4, xbd, _NN) + _dotf(cm, ht, _NN) * jnp.exp(acs) + d_ref[...] * x
            a_last = acs[CHUNK - 1:CHUNK, :]
            xw = x * (jnp.exp(a_last - acs) * dt_ref[rows, :])
            ht = ht * jnp.exp(a_last) + _dotf(bm, xw, _TN)
        h_scr[...] = ht

    return pl.pallas_call(
        body, name=name, grid=(G, nb),
        in_specs=[xsp, bsp, csp, xsp, xsp, rsp, rsp, dsp, const(tril), const(mask4), const(bd), const(triu_bd)],
        out_specs=[xsp, hsp], out_shape=[_S((T, G * W), f32), _S((G, nc, SSD_STATE, W), f32)],
        scratch_shapes=[pltpu.VMEM((SSD_STATE, W), f32)], compiler_params=_cparams(("parallel", "arbitrary")),
    )(xc, xc, xc, dt4, a4, dtw, aw, d4, tril, mask4, bd, triu_bd)


def _ssd2_bwd(xc, dt4, a4, dtw, aw, d4, consts, hs, dy, *, d_inner, name):
    T = xc.shape[0]
    G, nc, W = dtw.shape
    hpg = W // CHUNK
    tb = min(SSD2_TB, T)
    nb, ncb = T // tb, tb // CHUNK
    b_off = d_inner // SSD_STATE
    tmap = lambda i: nb - 1 - i
    xsp, bsp, csp, rsp, dsp, hsp, const = _ssd2_specs(G, hpg, tb, tmap, b_off, b_off + G)
    gsp = pl.BlockSpec((tb, SSD_STATE), lambda g, i: (tmap(i), g))
    ddsp = pl.BlockSpec((None, 1, LANES), lambda g, i: (g, 0, 0))
    tril, mask4, bd, triu_bd, ered = consts

    def body(x_ref, b_ref, c_ref, dt_ref, a_ref, dtw_ref, aw_ref, d_ref, tril_ref, mask_ref, bd_ref, tbd_ref, er_ref, hs_ref, dy_ref,
             dx_ref, db_ref, dc_ref, ddtc_ref, dac_ref, ddtw_ref, daw_ref, dd_ref, g_scr, dd_scr, rw_scr, tl_scr):
        first = pl.program_id(1) == 0

        @pl.when(first)
        def _():
            g_scr[...] = jnp.zeros_like(g_scr)
            dd_scr[...] = jnp.zeros_like(dd_scr)

        mask = mask_ref[...] > 0
        lane_in_block = lax.broadcasted_iota(jnp.int32, mask.shape, 1) & (CHUNK - 1)
        maskt = lax.broadcasted_iota(jnp.int32, mask.shape, 0) <= lane_in_block
        acs_rows = _doth(aw_ref[...], tbd_ref[...])
        dht = g_scr[...]
        dd = dd_scr[...]
        for c in range(ncb - 1, -1, -1):
            rows = slice(c * CHUNK, (c + 1) * CHUNK)
            x, bm, cm, dyc = x_ref[rows, :], b_ref[rows, :], c_ref[rows, :], dy_ref[rows, :]
            dtc, dtr = dt_ref[rows, :], dtw_ref[c:c + 1, :]
            ht = hs_ref[c]
            acs = _doth(tril_ref[...], a_ref[rows, :])
            seg = acs - acs_rows[c:c + 1, :]
            lmat = jnp.where(mask, jnp.exp(jnp.minimum(seg, 0.0)), 0.0)
            lmat_t = jnp.where(maskt, jnp.exp(jnp.minimum(-seg, 0.0)), 0.0)
            btile, ctile = _tile_rows(bm, hpg), _tile_rows(cm, hpg)
            g4 = _dotf(cm, btile, _NT)
            gt4 = _dotf(bm, ctile, _NT)
            m4 = g4 * lmat * dtr
            mt4 = gt4 * lmat_t * dtc
            xbd = _tile_rows(x, hpg) * bd_ref[...]
            dybd = _tile_rows(dyc, hpg) * bd_ref[...]
            dm4 = _dotf(dyc, xbd, _NT)
            dmt4 = _dotf(x, dybd, _NT)
            dx = d_ref[...] * dyc + _dotf(mt4, dybd, _NN)
            dd = dd + jnp.sum(dyc * x, axis=0, keepdims=True)
            e4 = dm4 * m4
            dc = _dotf(dm4 * lmat * dtr, btile, _NN)
            db = _dotf(dmt4 * lmat_t * dtc, ctile, _NN)
            decay = jnp.exp(acs)
            yoff = _dotf(cm, ht, _NN) * decay
            dz = dyc * decay
            dc = dc + _dotf(dz, ht, _NT)
            dht_prev = _dotf(cm, dz, _TN)
            a_last = acs[CHUNK - 1:CHUNK, :]
            ea_last = jnp.exp(a_last)
            erel = jnp.exp(a_last - acs)
            dte = erel * dtc
            dxw = _dotf(bm, dht, _NN)
            db = db + _dotf(x * dte, dht, _NT)
            dx = dx + dxw * dte
            q4 = dxw * x
            dacs = e4 + dyc * yoff - q4 * dte
            col = jnp.concatenate([q4 * erel, _doth(tril_ref[...], dacs, _TN)], axis=0)
            col = _doth(col, er_ref[...])
            ddtc_ref[rows, :] = col[:CHUNK]
            dac_ref[rows, :] = col[CHUNK:]
            ddtw_ref[c:c + 1, :] = jnp.sum(dm4 * g4 * lmat, axis=0, keepdims=True)
            rw_scr[c:c + 1, :] = -jnp.sum(e4, axis=0, keepdims=True)
            tl_scr[c:c + 1, :] = jnp.sum(q4 * dte, axis=0, keepdims=True) + ea_last * jnp.sum(dht * ht, axis=0, keepdims=True)
            dx_ref[rows, :] = dx
            db_ref[rows, :] = db
            dc_ref[rows, :] = dc
            dht = dht_prev + dht * ea_last
        daw_ref[...] = _doth(rw_scr[...], tbd_ref[...], _NT) + _doth(tl_scr[...], bd_ref[...])
        g_scr[...] = dht
        dd_scr[...] = dd

        @pl.when(pl.program_id(1) == nb - 1)
        def _():
            dd_ref[...] = _doth(dd, er_ref[...])

    return pl.pallas_call(
        body, name=name, grid=(G, nb),
        in_specs=[xsp, bsp, csp, xsp, xsp, rsp, rsp, dsp, const(tril), const(mask4), const(bd), const(triu_bd), const(ered), hsp, xsp],
        out_specs=[xsp, gsp, gsp, gsp, gsp, rsp, rsp, ddsp],
        out_shape=[_S((T, G * W), f32), _S((T, G * SSD_STATE), f32), _S((T, G * SSD_STATE), f32), _S((T, G * LANES), f32),
                   _S((T, G * LANES), f32), _S(dtw.shape, f32), _S(dtw.shape, f32), _S((G, 1, LANES), f32)],
        scratch_shapes=[pltpu.VMEM((SSD_STATE, W), f32), pltpu.VMEM((1, W), f32), pltpu.VMEM((ncb, W), f32), pltpu.VMEM((ncb, W), f32)],
        compiler_params=_cparams(("parallel", "arbitrary")),
    )(xc, xc, xc, dt4, a4, dtw, aw, d4, tril, mask4, bd, triu_bd, ered, hs, dy)


def _peers():
    x, y, c = lax.axis_index("x"), lax.axis_index("y"), lax.axis_index("c")
    return x, y, c


_ANY = pl.BlockSpec(memory_space=pl.ANY)
N_CHIP = N_DEV // 2


def _all_gather(shards, *, name):
    n = len(shards)

    def body(*refs):
        x_refs, out_refs = refs[:n], refs[n:2 * n]
        send_sems, recv_sems, local_sems = refs[2 * n:]
        x, y, c = _peers()
        me, sibling = (x, y, c), (x, y, 1 - c)
        chips = [(1 - x, y), (x, 1 - y), (1 - x, 1 - y)]

        def copy(a, r, block, to, src=None):
            px, py, pc = block
            slot = out_refs[a].at[4 * px + 2 * py + pc]
            return pltpu.make_async_remote_copy(
                src_ref=slot if src is None else src, dst_ref=slot, send_sem=send_sems.at[7 * a + r],
                recv_sem=recv_sems.at[7 * a + r], device_id=to, device_id_type=MESH)

        mine = [pltpu.make_async_copy(x_refs[a], out_refs[a].at[4 * x + 2 * y + c], local_sems.at[a]) for a in range(n)]
        for cp in mine:
            cp.start()
        first = []
        for a in range(n):
            first.append(copy(a, 0, me, sibling, src=x_refs[a]))
            first += [copy(a, 1 + j, me, (*chip, c), src=x_refs[a]) for j, chip in enumerate(chips)]
        for cp in first:
            cp.start()
        passed = []
        for j, chip in enumerate(chips):
            for a in range(n):
                copy(a, 1 + j, (*chip, c), me).wait_recv()
                fwd = copy(a, 4 + j, (*chip, c), sibling)
                fwd.start()
                passed.append(fwd)
        for a in range(n):
            copy(a, 0, sibling, me).wait_recv()
        for j, chip in enumerate(chips):
            for a in range(n):
                copy(a, 4 + j, (*chip, 1 - c), me).wait_recv()
        for cp in first + passed:
            cp.wait_send()
        for cp in mine:
            cp.wait()

    return pl.pallas_call(
        body, name=name, out_shape=[_S((N_DEV,) + s.shape, s.dtype) for s in shards], in_specs=[_ANY] * n, out_specs=[_ANY] * n,
        scratch_shapes=[pltpu.SemaphoreType.DMA((7 * n,)), pltpu.SemaphoreType.DMA((7 * n,)), pltpu.SemaphoreType.DMA((n,))],
    )(*shards)


def _exchange_sibling(slots, *, name):
    n = len(slots)

    def body(*refs):
        x_refs, sib_refs = refs[:n], refs[n:2 * n]
        send_sems, recv_sems = refs[2 * n:]
        x, y, c = _peers()
        give = [pltpu.make_async_remote_copy(
            src_ref=x_refs[a].at[pl.ds(N_CHIP * (1 - c), N_CHIP)], dst_ref=sib_refs[a], send_sem=send_sems.at[a],
            recv_sem=recv_sems.at[a], device_id=(x, y, 1 - c), device_id_type=MESH) for a in range(n)]
        for cp in give:
            cp.start()
        for cp in give:
            cp.wait_recv()
        for cp in give:
            cp.wait_send()

    return list(pl.pallas_call(
        body, name=name, out_shape=[_S((N_CHIP,) + s.shape[1:], s.dtype) for s in slots], in_specs=[_ANY] * n, out_specs=[_ANY] * n,
        scratch_shapes=[pltpu.SemaphoreType.DMA((n,)), pltpu.SemaphoreType.DMA((n,))],
    )(*slots))


def _chip_sum(slots, sib, core, *, name):
    _, R, W = slots.shape
    tr = _pick(R, max(16, (1 << 20) // (4 * W)), align=16)

    def body(core_ref, x_ref, s_ref, o_ref):
        o_ref[...] = (x_ref[...].astype(f32) + s_ref[...].astype(f32)).astype(o_ref.dtype)

    blk = pl.BlockSpec((None, tr, W), lambda t, i, core_ref: (t, i, 0))
    return pl.pallas_call(
        body, name=name, out_shape=_S(sib.shape, slots.dtype),
        grid_spec=pltpu.PrefetchScalarGridSpec(
            num_scalar_prefetch=1, grid=(N_CHIP, R // tr),
            in_specs=[pl.BlockSpec((None, tr, W), lambda t, i, core_ref: (N_CHIP * core_ref[0] + t, i, 0)), blk], out_specs=blk),
        compiler_params=_cparams(("parallel", "parallel")),
    )(core, slots, sib)


def _exchange_chips(parts, *, name):
    n = len(parts)

    def body(*refs):
        p_refs, out_refs = refs[:n], refs[n:2 * n]
        send_sems, recv_sems = refs[2 * n:]
        x, y, c = _peers()
        copies = []
        for j in range(1, N_CHIP):
            tx, ty = x ^ (j >> 1), y ^ (j & 1)
            for a in range(n):
                copies.append(pltpu.make_async_remote_copy(
                    src_ref=p_refs[a].at[2 * tx + ty], dst_ref=out_refs[a].at[j - 1], send_sem=send_sems.at[3 * a + j - 1],
                    recv_sem=recv_sems.at[3 * a + j - 1], device_id=(tx, ty, c), device_id_type=MESH))
        for cp in copies:
            cp.start()
        for cp in copies:
            cp.wait_recv()
        for cp in copies:
            cp.wait_send()

    return list(pl.pallas_call(
        body, name=name, out_shape=[_S((N_CHIP - 1,) + p.shape[1:], p.dtype) for p in parts], in_specs=[_ANY] * n, out_specs=[_ANY] * n,
        scratch_shapes=[pltpu.SemaphoreType.DMA((3 * n,)), pltpu.SemaphoreType.DMA((3 * n,))],
    )(*parts))


def _sum_slots(stack, *, name):
    n, R, W = stack.shape
    tr = _pick(R, 1024, align=8)

    def body(s_ref, o_ref):
        acc = s_ref[0]
        for k in range(1, n):
            acc = acc + s_ref[k]
        o_ref[...] = acc

    return pl.pallas_call(
        body, name=name, grid=(R // tr,), in_specs=[pl.BlockSpec((n, tr, W), lambda i: (0, i, 0))],
        out_specs=pl.BlockSpec((tr, W), lambda i: (i, 0)), out_shape=_S((R, W), f32), compiler_params=_cparams(("parallel",)),
    )(stack)


def _adamw_math(gv, wv, mv, vv):
    c1 = 1.0 / (1.0 - ADAM_B1 ** ADAM_STEP)
    c2 = 1.0 / (1.0 - ADAM_B2 ** ADAM_STEP)
    nm = ADAM_B1 * mv + (1.0 - ADAM_B1) * gv
    nv = ADAM_B2 * vv + (1.0 - ADAM_B2) * jnp.square(gv)
    return -ADAM_LR * ((nm * c1) / (jnp.sqrt(nv * c2) + ADAM_EPS) + ADAM_WD * wv), nm, nv


def _adamw(g, w, m, v, *, name):
    R, W = w.shape
    tr = _pick(R, max(8, (1 << 20) // (4 * W)), align=8)

    def body(g_ref, w_ref, m_ref, v_ref, d_ref, nm_ref, nv_ref):
        d_ref[...], nm_ref[...], nv_ref[...] = _adamw_math(g_ref[...], w_ref[...], m_ref[...], v_ref[...])

    sp = pl.BlockSpec((tr, W), lambda i: (i, 0))
    return pl.pallas_call(
        body, name=name, grid=(R // tr,), in_specs=[sp] * 4, out_specs=[sp] * 3, out_shape=[_S((R, W), f32)] * 3,
        compiler_params=_cparams(("parallel",)),
    )(g, w, m, v)


def _reduce_adamw(own, arrived, chip, w, m, v, *, name):
    n, R, W = arrived.shape
    tr = _pick(R, max(16, (1 << 20) // (4 * W)), align=16)

    def body(chip_ref, o_ref, p_ref, w_ref, m_ref, v_ref, g_ref, d_ref, nm_ref, nv_ref):
        gv = o_ref[...].astype(f32)
        for k in range(n):
            gv = gv + p_ref[k].astype(f32)
        g_ref[...] = gv
        d_ref[...], nm_ref[...], nv_ref[...] = _adamw_math(gv, w_ref[...], m_ref[...], v_ref[...])

    sp = pl.BlockSpec((tr, W), lambda i, chip_ref: (i, 0))
    return pl.pallas_call(
        body, name=name, out_shape=[_S((R, W), f32)] * 4,
        grid_spec=pltpu.PrefetchScalarGridSpec(
            num_scalar_prefetch=1, grid=(R // tr,),
            in_specs=[pl.BlockSpec((None, tr, W), lambda i, chip_ref: (chip_ref[0], i, 0)),
                      pl.BlockSpec((n, tr, W), lambda i, chip_ref: (0, i, 0))] + [sp] * 3, out_specs=[sp] * 4),
        compiler_params=_cparams(("parallel",)),
    )(chip, own, arrived, w, m, v)


def _pieces(seg_start, seg_len, shard_w):
    out, col = [], seg_start
    while col < seg_start + seg_len:
        k, a = divmod(col, shard_w)
        n = min(shard_w - a, seg_start + seg_len - col)
        out.append((k, a, col - seg_start, n))
        col += n
    return out


def _unshard_w_in(g, seg_lens, *, name):
    _, D, w = g.shape
    starts = [sum(seg_lens[:i]) for i in range(len(seg_lens))]
    widths = [max(n, LANES) for n in seg_lens]
    tm = _pick(D, 256, align=16)

    def body(g_ref, *o_refs):
        for o_ref, s0, n in zip(o_refs, starts, seg_lens):
            if n < o_ref.shape[1]:
                o_ref[...] = jnp.zeros_like(o_ref)
            for k, a, off, m in _pieces(s0, n, w):
                o_ref[:, off:off + m] = g_ref[k, :, a:a + m]

    return pl.pallas_call(
        body, name=name, grid=(D // tm,), in_specs=[pl.BlockSpec((N_DEV, tm, w), lambda i: (0, i, 0))],
        out_specs=[pl.BlockSpec((tm, wd), lambda i: (i, 0)) for wd in widths], out_shape=[_S((D, wd), g.dtype) for wd in widths],
        compiler_params=_cparams(("parallel",)),
    )(g)


def _unshard_pair(g1, g2, *, name):
    _, D, w = g1.shape
    tm = _pick(D, 256, align=16)

    def body(a_ref, b_ref, o_ref):
        for i, g_ref in enumerate((a_ref, b_ref)):
            for k in range(N_DEV):
                off = (i * N_DEV + k) * w
                o_ref[:, off:off + w] = g_ref[k]

    blk = pl.BlockSpec((N_DEV, tm, w), lambda i: (0, i, 0))
    return pl.pallas_call(
        body, name=name, grid=(D // tm,), in_specs=[blk, blk], out_specs=pl.BlockSpec((tm, 2 * N_DEV * w), lambda i: (i, 0)),
        out_shape=_S((D, 2 * N_DEV * w), g1.dtype), compiler_params=_cparams(("parallel",)),
    )(g1, g2)


def _reshard_pair(dw, *, name):
    D, w = dw.shape[0], dw.shape[1] // (2 * N_DEV)
    tm = _pick(D, 128, align=16)

    def body(g_ref, a_ref, b_ref):
        for i, o_ref in enumerate((a_ref, b_ref)):
            for k in range(N_DEV):
                off = (i * N_DEV + k) * w
                o_ref[_slot_of(k)] = g_ref[:, off:off + w].astype(o_ref.dtype)

    blk = pl.BlockSpec((N_DEV, tm, w), lambda i: (0, i, 0))
    return pl.pallas_call(
        body, name=name, grid=(D // tm,), in_specs=[pl.BlockSpec((tm, dw.shape[1]), lambda i: (i, 0))], out_specs=[blk, blk],
        out_shape=[_S((N_DEV, D, w), bf16)] * 2, compiler_params=_cparams(("parallel",)),
    )(dw)


def _reshard_w_in(grads, seg_lens, w, *, name):
    D = grads[0].shape[0]
    starts = [sum(seg_lens[:i]) for i in range(len(seg_lens))]
    tm = _pick(D, 128, align=16)

    def body(*refs):
        o_ref = refs[-1]
        for g_ref, s0, n in zip(refs[:-1], starts, seg_lens):
            for k, a, off, m in _pieces(s0, n, w):
                o_ref[_slot_of(k), :, a:a + m] = g_ref[:, off:off + m].astype(o_ref.dtype)

    return pl.pallas_call(
        body, name=name, grid=(D // tm,), in_specs=[pl.BlockSpec((tm, g.shape[1]), lambda i: (i, 0)) for g in grads],
        out_specs=pl.BlockSpec((N_DEV, tm, w), lambda i: (0, i, 0)), out_shape=_S((N_DEV, D, w), bf16),
        compiler_params=_cparams(("parallel",)),
    )(*grads)


def _pad_flat(a, mult):
    a = a.reshape(-1)
    n = -(-a.shape[0] // mult) * mult
    return a if n == a.shape[0] else jnp.pad(a, (0, n - a.shape[0]))


def _pad_cols(a, mult):
    n = -(-a.shape[1] // mult) * mult
    return a if n == a.shape[1] else jnp.pad(a, ((0, 0), (0, n - a.shape[1])))


def _block_diag(t):
    nblk, g, P, Q = t.shape
    eye = jnp.eye(g, dtype=t.dtype)
    return (t[:, :, :, None, :] * eye[None, :, None, :, None]).reshape(nblk, g * P, g * Q)


def _block_diag_t(w, P, Q):
    nblk = w.shape[0]
    g = w.shape[1] // P
    eye = jnp.eye(g, dtype=w.dtype)
    return (w.reshape(nblk, g, P, g, Q) * eye[None, :, None, :, None]).sum(axis=3)


_COLS = ("ffn1_w_gate", "ffn1_w_up", "ffn2_w_gate", "ffn2_w_up")
_ROWS = ("ffn1_w_down", "ffn2_w_down", "s5_w_glu", "w_proj_s5", "w_out", "w_proj_ssd")
_BIG = _COLS + _ROWS + ("w_in", "conv_w")
_SMALL = ("ffn1_norm", "mix_norm", "conv_b", "s5_A_re", "s5_A_im", "s5_log_dt", "s5_B_re", "s5_B_im", "s5_C_re", "s5_C_im",
          "s5_D", "s5_b_glu", "ssd_A_log", "ssd_dt_bias", "ssd_D", "ssd_norm", "b_gate", "ffn2_norm", "final_norm")
_WEIGHTS = ("ffn1_norm", "ffn1_w_gate", "ffn1_w_up", "ffn1_w_down", "mix_norm", "w_in", "conv_w", "conv_b", "s5_A_re", "s5_A_im",
            "s5_log_dt", "s5_B_re", "s5_B_im", "s5_C_re", "s5_C_im", "s5_D", "s5_w_glu", "s5_b_glu", "ssd_A_log", "ssd_dt_bias",
            "ssd_D", "ssd_norm", "w_proj_s5", "w_proj_ssd", "b_gate", "w_out", "ffn2_norm", "ffn2_w_gate", "ffn2_w_up",
            "ffn2_w_down", "final_norm")
def _ffn_fwd(x, n, wgu, wd, tag):
    D = x.shape[1]
    F = wd.shape[0]
    h = _rows(_f_rmsnorm, [x], [n], [(D, bf16)], name=tag + "_norm")[0]
    ab = _mm(h, wgu, name=tag + "_gate_up")
    c = _rows(_f_swiglu, [ab], [], [(F, bf16)], name=tag + "_act")[0]
    y = _mm(c, wd, scale=0.5, add=x, name=tag + "_down")
    return y, (x, n, h, ab, c)


def _ffn_bwd(saved, wgu, wd, dy, tag):
    x, n, h, ab, c = saved
    F = wd.shape[0]
    dc = _mm(dy, wd, tb=True, scale=0.5, name=tag + "_d_act")
    dwd = _mm(c, dy, ta=True, o_blk="m", o_slots=True, tm=F // 2, out_dtype=bf16, scale=0.5, name=tag + "_d_wdown")
    (dab,), _ = _rows_bwd(_f_swiglu, [ab], [], [dc], name=tag + "_act_bwd", want_rows=[0], row_dtypes={0: bf16})
    dh = _mm(dab, wgu, tb=True, name=tag + "_d_h")
    dwg, dwu = _reshard_pair(_mm(h, dab, ta=True, name=tag + "_d_wgu"), name=tag + "_reshard_d_wgu")
    (dx,), (dn,) = _rows_bwd(_f_rmsnorm, [x], [n], [dh], name=tag + "_norm_bwd", want_rows=[0], adds={0: dy})
    return dx, dn, dwg, dwu, dwd


def kernel(x, ffn1_norm, ffn1_w_gate, ffn1_w_up, ffn1_w_down, mix_norm, w_in, conv_w, conv_b, s5_A_re, s5_A_im, s5_log_dt, s5_B_re, s5_B_im, s5_C_re, s5_C_im, s5_D, s5_w_glu, s5_b_glu, ssd_A_log, ssd_dt_bias, ssd_D, ssd_norm, w_proj_s5, w_proj_ssd, b_gate, w_out, ffn2_norm, ffn2_w_gate, ffn2_w_up, ffn2_w_down, final_norm, loss_target, m_ffn1_norm, m_ffn1_w_gate, m_ffn1_w_up, m_ffn1_w_down, m_mix_norm, m_w_in, m_conv_w, m_conv_b, m_s5_A_re, m_s5_A_im, m_s5_log_dt, m_s5_B_re, m_s5_B_im, m_s5_C_re, m_s5_C_im, m_s5_D, m_s5_w_glu, m_s5_b_glu, m_ssd_A_log, m_ssd_dt_bias, m_ssd_D, m_ssd_norm, m_w_proj_s5, m_w_proj_ssd, m_b_gate, m_w_out, m_ffn2_norm, m_ffn2_w_gate, m_ffn2_w_up, m_ffn2_w_down, m_final_norm, v_ffn1_norm, v_ffn1_w_gate, v_ffn1_w_up, v_ffn1_w_down, v_mix_norm, v_w_in, v_conv_w, v_conv_b, v_s5_A_re, v_s5_A_im, v_s5_log_dt, v_s5_B_re, v_s5_B_im, v_s5_C_re, v_s5_C_im, v_s5_D, v_s5_w_glu, v_s5_b_glu, v_ssd_A_log, v_ssd_dt_bias, v_ssd_D, v_ssd_norm, v_w_proj_s5, v_w_proj_ssd, v_b_gate, v_w_out, v_ffn2_norm, v_ffn2_w_gate, v_ffn2_w_up, v_ffn2_w_down, v_final_norm):
    P = dict(locals())
    T, D = x.shape[1], x.shape[2]
    x0, tgt = x[0], loss_target[0]
    sh = {k: P[k][0] for k in _BIG}

    gathered = _all_gather([sh[k] if k == "conv_w" else sh[k].astype(bf16) for k in _BIG], name="gather_weights")
    W = dict(zip(_BIG, gathered))
    whole = lambda k: W[k].reshape(-1, D)
    g_win = W["w_in"]
    conv_w_full = W["conv_w"].transpose(1, 0, 2).reshape(CONV_K, -1)

    d_inner = N_DEV * sh["w_proj_ssd"].shape[0]
    conv_dim = conv_w_full.shape[1]
    H = ssd_A_log.shape[1]
    G = (conv_dim - d_inner) // (2 * SSD_STATE)
    hpg = H // G
    nc = T // CHUNK
    Gs = D // S5_GROUP
    nblk = Gs // S5_GPB
    NS = Gs * S5_STATE
    seg_lens = (D, d_inner, conv_dim, H, 2 * D)
    w_u, w_z, w_xbc, w_dt, w_gl = _unshard_w_in(g_win, seg_lens, name="unshard_w_in")
    w_glu, w_p5, w_pssd, w_o = whole("s5_w_glu"), whole("w_proj_s5"), whole("w_proj_ssd"), whole("w_out")

    ffn1_w = (_unshard_pair(W["ffn1_w_gate"], W["ffn1_w_up"], name="unshard_ffn1_gate_up"), whole("ffn1_w_down"))
    ffn2_w = (_unshard_pair(W["ffn2_w_gate"], W["ffn2_w_up"], name="unshard_ffn2_gate_up"), whole("ffn2_w_down"))
    x1, sv1 = _ffn_fwd(x0, ffn1_norm, *ffn1_w, "ffn1")
    h2 = _rows(_f_rmsnorm, [x1], [mix_norm], [(D, bf16)], name="mix_norm")[0]
    u_p = _mm(h2, w_u, o_seg=True, name="in_u")
    z = _mm(h2, w_z, name="in_z")
    xbc = _mm(h2, w_xbc, name="in_xbc")
    gl = _mm(h2, w_gl, name="in_gate")
    dtr = _mm(h2, w_dt, name="in_dt")

    rep = lambda a: jnp.repeat(a, S5_GROUP, axis=0)
    lr, li, ldt = s5_A_re[0], s5_A_im[0], s5_log_dt[0].reshape(Gs, 1)
    brt = s5_B_re[0].transpose(0, 2, 1).reshape(Gs * S5_GROUP, S5_STATE)
    bit = s5_B_im[0].transpose(0, 2, 1).reshape(Gs * S5_GROUP, S5_STATE)
    prep_args = (lr, li, ldt, rep(lr), rep(li), rep(ldt), brt, bit)
    ar, ai, bbrt, bbit = _s5_prep(prep_args, name="s5_prep")
    a_r, a_i = ar.reshape(1, NS), ai.reshape(1, NS)
    wb_r = _block_diag(bbrt.reshape(nblk, S5_GPB, S5_GROUP, S5_STATE)).astype(bf16)
    wb_i = _block_diag(bbit.reshape(nblk, S5_GPB, S5_GROUP, S5_STATE)).astype(bf16)
    c4r = s5_C_re[0].reshape(nblk, S5_GPB, S5_GROUP, S5_STATE).transpose(0, 1, 3, 2)
    c4i = s5_C_im[0].reshape(nblk, S5_GPB, S5_GROUP, S5_STATE).transpose(0, 1, 3, 2)
    wc_r, wc_i = _block_diag(c4r).astype(bf16), _block_diag(c4i).astype(bf16)
    sl_r, sl_i, p_r, p_i = _s5_local_scan(u_p, wb_r, wb_i, a_r, a_i, reverse=False, name="s5_scan")
    c_r, c_i = _s5_carry(sl_r[T - NSEG:], sl_i[T - NSEG:], p_r, p_i, reverse=False, name="s5_carry")
    s_r, s_i, ylin = _s5_fix_out(sl_r, sl_i, a_r, a_i, c_r, c_i, wc_r, wc_i, name="s5_fix_out")
    g5 = _rows(_f_s5_post, [ylin, u_p], [s5_D], [(D, f32)], name="s5_gelu")[0]
    v5 = _mm(g5, w_glu, name="s5_glu_mm")
    o5 = _rows(_f_glu, [g5, v5], [s5_b_glu], [(D, bf16)], name="s5_glu")[0]
    p5 = _mm(o5, w_p5, a_seg=True, name="proj_s5")

    xc = _conv_fwd(xbc, conv_w_full, conv_b, name="conv")
    bias_p, alog_p = _pad_cols(ssd_dt_bias, LANES), _pad_cols(ssd_A_log, LANES)
    dt_p, da_p = _rows(_f_dt, [dtr], [bias_p, alog_p], [(LANES, f32), (LANES, f32)], name="ssd_dt")
    col_l = lambda a: jnp.repeat(a[:, :H], HEADDIM, axis=1)
    row_l = lambda a: a[:, :H].reshape(nc, CHUNK, G, hpg).transpose(2, 0, 3, 1).reshape(G, nc, hpg * CHUNK)
    ssd_in = (xc, col_l(dt_p), col_l(da_p), row_l(dt_p), row_l(da_p), jnp.repeat(ssd_D, HEADDIM, axis=1), _ssd_consts(hpg))
    y_ssd, hs = _ssd2_fwd(*ssd_in, d_inner=d_inner, name="ssd")
    yn = _rows(_f_gated_norm, [y_ssd, z], [ssd_norm], [(d_inner, bf16)], name="ssd_gated_norm")[0]
    pssd = _mm(yn, w_pssd, name="proj_ssd")

    merged = _rows(_f_merge, [gl, p5, pssd], [b_gate], [(D, bf16)], name="merge")[0]
    x2 = _mm(merged, w_o, add=x1, name="out_proj")
    x3, sv2 = _ffn_fwd(x2, ffn2_norm, *ffn2_w, "ffn2")
    lossv, dx3, d_final = _loss_stage(x3, tgt, final_norm.reshape(1, D), name="loss")

    gw = {}
    gs = {"final_norm": d_final}
    slot_mm = lambda a_, b_, name, **kw: _mm(a_, b_, ta=True, o_blk="m", o_slots=True, out_dtype=bf16, name=name, **kw)
    dx2, gs["ffn2_norm"], gw["ffn2_w_gate"], gw["ffn2_w_up"], gw["ffn2_w_down"] = _ffn_bwd(sv2, *ffn2_w, dx3, "ffn2")

    dmerged = _mm(dx2, w_o, tb=True, name="d_merged")
    gw["w_out"] = slot_mm(merged, dx2, "d_w_out")
    (dgl, dp5, dpssd), (gs["b_gate"],) = _rows_bwd(_f_merge, [gl, p5, pssd], [b_gate], [dmerged], name="merge_bwd", want_rows=[0, 1, 2])

    dyn = _mm(dpssd, w_pssd, tb=True, name="d_yn")
    gw["w_proj_ssd"] = slot_mm(yn, dpssd, "d_w_proj_ssd")
    (dyssd, dz), (gs["ssd_norm"],) = _rows_bwd(_f_gated_norm, [y_ssd, z], [ssd_norm], [dyn], name="ssd_gated_norm_bwd", want_rows=[0, 1])
    dxs, dbm, dcm, ddtc, ddac, ddtw, ddaw, ddh = _ssd2_bwd(*ssd_in, hs, dyssd, d_inner=d_inner, name="ssd_bwd")

    def fold(col, row):
        col = col.reshape(T, G, LANES)[:, :, :hpg].reshape(T, H)
        row = row.reshape(G, nc, hpg, CHUNK).transpose(1, 3, 0, 2).reshape(T, H)
        return _pad_cols(col + row, LANES)

    (ddtr,), (dbias_p, dalog_p) = _rows_bwd(_f_dt, [dtr], [bias_p, alog_p], [fold(ddtc, ddtw), fold(ddac, ddaw)], name="ssd_dt_bwd", want_rows=[0])
    gs["ssd_dt_bias"], gs["ssd_A_log"], gs["ssd_D"] = dbias_p[:, :H], dalog_p[:, :H], ddh[:, 0, :hpg].reshape(1, H)
    dxbc, d_conv_w, gs["conv_b"] = _conv_bwd(xbc, conv_w_full, conv_b, jnp.concatenate([dxs, dbm, dcm], axis=1), name="conv_bwd")
    cwk = sh["conv_w"].shape[1]
    gw["conv_w"] = d_conv_w.reshape(CONV_K, N_CHIP, 2, cwk).transpose(2, 1, 0, 3).reshape(N_DEV, CONV_K, cwk)

    do5 = _mm(dp5, w_p5, tb=True, o_seg=True, name="d_o5")
    gw["w_proj_s5"] = slot_mm(o5, dp5, "d_w_proj_s5", a_seg=True)
    (dg5a, dv5), (gs["s5_b_glu"],) = _rows_bwd(_f_glu, [g5, v5], [s5_b_glu], [do5], name="s5_glu_bwd", want_rows=[0, 1])
    dg5 = _mm(dv5, w_glu, tb=True, add=dg5a, name="d_g5")
    gw["s5_w_glu"] = slot_mm(g5, dv5, "d_w_glu")
    (dylin, du_a), (gs["s5_D"],) = _rows_bwd(_f_s5_post, [ylin, u_p], [s5_D], [dg5], name="s5_gelu_bwd", want_rows=[0, 1])
    wct_r, wct_i = wc_r.transpose(0, 2, 1), -wc_i.transpose(0, 2, 1)
    ql_r, ql_i, pb_r, pb_i = _s5_local_scan(dylin, wct_r, wct_i, a_r, -a_i, reverse=True, name="s5_scan_bwd")
    cb_r, cb_i = _s5_carry(ql_r[:NSEG], ql_i[:NSEG], pb_r, pb_i, reverse=True, name="s5_carry_bwd")
    tc = min(S5_TC, T)

    def before_blocks(s):
        last = s.reshape(T // tc, tc, NS)[:, tc - NSEG:, :]
        wrap = jnp.concatenate([jnp.zeros((1, 1, NS), f32), last[-1:, : NSEG - 1, :]], axis=1)
        return jnp.concatenate([wrap, last[:-1]], axis=0)

    du_p, dwb_r, dwb_i, dwc_r, dwc_i, d_ar, d_ai = _s5_fix_bwd(
        ql_r, ql_i, a_r, -a_i, cb_r, cb_i, s_r, s_i, before_blocks(s_r), before_blocks(s_i), u_p, dylin, du_a, wb_r, wb_i, name="s5_fix_bwd")
    unblk = lambda w: _block_diag_t(w, S5_GROUP, S5_STATE).reshape(Gs * S5_GROUP, S5_STATE)
    rsum = jnp.repeat(jnp.eye(Gs, dtype=f32), S5_GROUP, axis=1)
    d_lr, d_li, d_ldt, d_brt, d_bit = _s5_prep_bwd(
        prep_args, (d_ar.reshape(Gs, S5_STATE), d_ai.reshape(Gs, S5_STATE), unblk(dwb_r), unblk(dwb_i)), rsum, name="s5_prep_bwd")
    gs["s5_A_re"], gs["s5_A_im"], gs["s5_log_dt"] = d_lr, d_li, d_ldt.reshape(1, Gs)
    gs["s5_B_re"] = d_brt.reshape(Gs, S5_GROUP, S5_STATE).transpose(0, 2, 1)
    gs["s5_B_im"] = d_bit.reshape(Gs, S5_GROUP, S5_STATE).transpose(0, 2, 1)
    gs["s5_C_re"] = _block_diag_t(dwc_r, S5_STATE, S5_GROUP).transpose(0, 1, 3, 2).reshape(Gs, S5_GROUP, S5_STATE)
    gs["s5_C_im"] = _block_diag_t(dwc_i, S5_STATE, S5_GROUP).transpose(0, 1, 3, 2).reshape(Gs, S5_GROUP, S5_STATE)

    dh2 = _mm(du_p, w_u, tb=True, a_seg=True, name="d_h2_u")
    dh2 = _mm(dz, w_z, tb=True, add=dh2, name="d_h2_z")
    dh2 = _mm(dxbc, w_xbc, tb=True, add=dh2, name="d_h2_xbc")
    dh2 = _mm(dgl, w_gl, tb=True, add=dh2, name="d_h2_gate")
    dh2 = _mm(ddtr, w_dt, tb=True, add=dh2, name="d_h2_dt")
    d_w_in = [_mm(h2, du_p, ta=True, b_seg=True, name="d_w_u"), _mm(h2, dz, ta=True, name="d_w_z"), _mm(h2, dxbc, ta=True, name="d_w_xbc"),
              _mm(h2, ddtr, ta=True, name="d_w_dt"), _mm(h2, dgl, ta=True, name="d_w_gate")]
    gw["w_in"] = _reshard_w_in(d_w_in, seg_lens, sh["w_in"].shape[1], name="reshard_d_w_in")
    (dx1,), (gs["mix_norm"],) = _rows_bwd(_f_rmsnorm, [x1], [mix_norm], [dh2], name="mix_norm_bwd", want_rows=[0], adds={0: dx2})
    dx0, gs["ffn1_norm"], gw["ffn1_w_gate"], gw["ffn1_w_up"], gw["ffn1_w_down"] = _ffn_bwd(sv1, *ffn1_w, dx1, "ffn1")

    core = lax.axis_index("c").astype(jnp.int32).reshape(1)
    chip = (2 * lax.axis_index("x") + lax.axis_index("y")).astype(jnp.int32).reshape(1)
    sib = _exchange_sibling([gw[k] for k in _BIG], name="exchange_grads_sibling")
    chip_sums = {k: _chip_sum(gw[k], s_, core, name="chip_sum_" + k) for k, s_ in zip(_BIG, sib)}
    arrived = dict(zip(_BIG, _exchange_chips([chip_sums[k] for k in _BIG], name="exchange_grads_chips")))
    small_shapes = {k: (P[k][0].shape if P[k].ndim > 1 else P[k].shape) for k in _SMALL}
    pack = lambda d: jnp.concatenate([_pad_flat(d[k], TILE_ELEMS) for k in _SMALL]).reshape(-1, LANES)
    gsmall = _sum_slots(_all_gather([pack(gs)], name="gather_small_grads")[0], name="sum_small_grads")
    snum = {k: math.prod(small_shapes[k]) for k in _SMALL}
    ssz = {k: -(-snum[k] // TILE_ELEMS) * TILE_ELEMS for k in _SMALL}

    grads, delta, new_m, new_v = {}, {}, {}, {}
    for k in _BIG:
        grads[k], delta[k], new_m[k], new_v[k] = _reduce_adamw(
            chip_sums[k], arrived[k], chip, P[k][0], P["m_" + k][0], P["v_" + k][0], name="adamw_" + k)
    d_s, m_s, v_s = _adamw(gsmall, pack({k: P[k] for k in _SMALL}), pack({k: P["m_" + k] for k in _SMALL}),
                           pack({k: P["v_" + k] for k in _SMALL}), name="adamw_small")
    off = 0
    gflat, dflat, mflat, vflat = gsmall.reshape(-1), d_s.reshape(-1), m_s.reshape(-1), v_s.reshape(-1)
    for k in _SMALL:
        n = snum[k]
        grads[k], delta[k], new_m[k], new_v[k] = (a[off:off + n] for a in (gflat, dflat, mflat, vflat))
        off += ssz[k]

    loss = lax.psum(lossv[0, 0], ("x", "y", "c"))
    out = [loss, dx0.reshape(x.shape)]
    for d in (grads, delta, new_m, new_v):
        out += [d[k].reshape(P[k].shape) for k in _WEIGHTS]
    return tuple(out)
```

```python
import math

import jax
import jax.numpy as jnp
from jax import lax
from jax.experimental import pallas as pl
from jax.experimental.pallas import tpu as pltpu

f32 = jnp.float32
bf16 = jnp.bfloat16
_S = jax.ShapeDtypeStruct

EPS = 1e-6
S5_GROUP = 16
S5_STATE = 64
HEADDIM = 64
SSD_STATE = 128
CHUNK = 64
CONV_K = 4
NSEG = 8
S5_GPB = 16
N_DEV = 8
LANES = 128
TILE_ELEMS = 8 * LANES

ADAM_LR = 0.001
ADAM_B1 = 0.9
ADAM_B2 = 0.999
ADAM_EPS = 1e-08
ADAM_WD = 0.01
ADAM_STEP = 10

VMEM_LIMIT = 56 * 1024 * 1024
MM_FULL_K = 3072
MM_MAX_TN = 3072
MESH = pl.DeviceIdType.MESH


def _cparams(sem=None):
    return pltpu.CompilerParams(dimension_semantics=sem, vmem_limit_bytes=VMEM_LIMIT)


def _pick(dim, pref, align=LANES):
    best = None
    t = align
    while t <= min(dim, pref):
        if dim % t == 0:
            best = t
        t += align
    return best or dim


def _slot_of(k):
    return (k & 1) * (N_DEV // 2) + (k >> 1)


def _mm(a, b, *, name, ta=False, tb=False, a_blk=None, b_blk=None, o_blk=None, o_slots=False, a_seg=False, b_seg=False,
        o_seg=False, tm=None, out_dtype=f32, scale=1.0, add=None):
    a2, b2 = a.shape[-2:], b.shape[-2:]
    Ma, Ka = (a2[1], a2[0]) if ta else a2
    Kb, Nb = (b2[1], b2[0]) if tb else b2
    M = Ma * (a.shape[0] if a_blk == "m" else 1)
    K = Ka * (a.shape[0] if a_blk == "k" else 1)
    N = Nb * (b.shape[0] if b_blk == "n" else 1)
    assert K == Kb * (b.shape[0] if b_blk == "k" else 1), (a.shape, b.shape, ta, tb, a_blk, b_blk)
    assert (a.ndim == 3) == (a_blk is not None) and (b.ndim == 3) == (b_blk is not None)
    tm = Ma if a_blk == "m" else (tm or _pick(M, 512))
    tn = Nb if b_blk == "n" else _pick(N, MM_MAX_TN)
    if a_blk == "k" or b_blk == "k":
        tk = Ka if a_blk == "k" else Kb
        assert tk == (Kb if b_blk == "k" else tk)
    else:
        tk = K if K <= MM_FULL_K else _pick(K, 1024 if ta else MM_FULL_K)
    if (a_seg and not ta) or o_seg:
        tm = M // NSEG
    if (a_seg and ta) or b_seg:
        tk = K // NSEG
    gm, gn, nk = M // tm, N // tn, K // tk
    assert not (add is not None and (o_seg or o_blk)) and not (o_blk and o_seg)

    if a_seg:
        assert a.ndim == 2
        a = a.reshape(a.shape[0] // NSEG, NSEG * a.shape[1])
        if ta:
            a_spec = pl.BlockSpec((tk, tm), lambda i, j, k: (0, k * (Ma // tm) + i))
        else:
            a_spec = pl.BlockSpec((tm, tk), lambda i, j, k: (0, i * (Ka // tk) + k))
    elif a.ndim == 3:
        lead = (lambda i, k: i) if a_blk == "m" else (lambda i, k: k)
        if ta:
            a_spec = pl.BlockSpec((None, tk, tm), lambda i, j, k: (lead(i, k), 0 if a_blk == "k" else k, 0 if a_blk == "m" else i))
        else:
            a_spec = pl.BlockSpec((None, tm, tk), lambda i, j, k: (lead(i, k), 0 if a_blk == "m" else i, 0 if a_blk == "k" else k))
    else:
        a_spec = pl.BlockSpec((tk, tm), lambda i, j, k: (k, i)) if ta else pl.BlockSpec((tm, tk), lambda i, j, k: (i, k))
    if b_seg:
        assert b.ndim == 2 and not tb
        b = b.reshape(b.shape[0] // NSEG, NSEG * b.shape[1])
        b_spec = pl.BlockSpec((tk, tn), lambda i, j, k: (0, k * (Nb // tn) + j))
    elif b.ndim == 3:
        lead = (lambda j, k: j) if b_blk == "n" else (lambda j, k: k)
        if tb:
            b_spec = pl.BlockSpec((None, tn, tk), lambda i, j, k: (lead(j, k), 0 if b_blk == "n" else j, 0 if b_blk == "k" else k))
        else:
            b_spec = pl.BlockSpec((None, tk, tn), lambda i, j, k: (lead(j, k), 0 if b_blk == "k" else k, 0 if b_blk == "n" else j))
    else:
        b_spec = pl.BlockSpec((tn, tk), lambda i, j, k: (j, k)) if tb else pl.BlockSpec((tk, tn), lambda i, j, k: (k, j))
    slot = _slot_of if o_slots else (lambda k: k)
    if o_blk == "n":
        assert gn == N_DEV or not o_slots
        o_shape, o_spec = (gn, M, tn), pl.BlockSpec((None, tm, tn), lambda i, j, k: (slot(j), i, 0))
    elif o_blk == "m" and o_slots and gm < N_DEV:
        rs = M // N_DEV
        per_tile = tm // rs
        assert per_tile % 2 == 0 and tm % rs == 0
        o_shape = (2, N_CHIP, rs, N)
        o_spec = pl.BlockSpec((2, per_tile // 2, rs, tn), lambda i, j, k: (0, i, 0, j))
    elif o_blk == "m":
        assert gm == N_DEV or not o_slots
        o_shape, o_spec = (gm, tm, N), pl.BlockSpec((None, tm, tn), lambda i, j, k: (slot(i), 0, j))
    elif o_seg:
        o_shape, o_spec = (tm, NSEG * N), pl.BlockSpec((tm, tn), lambda i, j, k: (0, i * (N // tn) + j))
    else:
        o_shape, o_spec = (M, N), pl.BlockSpec((tm, tn), lambda i, j, k: (i, j))
    dims = (((0 if ta else 1,), (1 if tb else 0,)), ((), ()))
    has_add = add is not None

    def body(*refs):
        a_ref, b_ref = refs[0], refs[1]
        add_ref = refs[2] if has_add else None
        o_ref, acc_ref = refs[-2], refs[-1]
        k = pl.program_id(2)

        @pl.when(k == 0)
        def _():
            acc_ref[...] = jnp.zeros_like(acc_ref)

        acc_ref[...] += lax.dot_general(a_ref[...].astype(bf16), b_ref[...].astype(bf16), dims, preferred_element_type=f32)

        @pl.when(k == nk - 1)
        def _():
            r = acc_ref[...] * scale
            if has_add:
                r = r + add_ref[...].astype(f32)
            if len(o_shape) == 4:
                rs = o_shape[2]
                for chip_l in range(o_ref.shape[1]):
                    for core in range(2):
                        dev = 2 * chip_l + core
                        o_ref[core, chip_l] = r[dev * rs:(dev + 1) * rs].astype(out_dtype)
            else:
                o_ref[...] = r.astype(out_dtype)

    ins = [a, b] + ([add] if has_add else [])
    in_specs = [a_spec, b_spec] + ([o_spec] if has_add else [])
    out = pl.pallas_call(
        body, name=name, grid=(gm, gn, nk), in_specs=in_specs, out_specs=o_spec,
        out_shape=_S(o_shape, out_dtype), scratch_shapes=[pltpu.VMEM((tm, tn), f32)],
        compiler_params=_cparams(("parallel", "parallel", "arbitrary")),
    )(*ins)
    if len(o_shape) == 4:
        return out.reshape(N_DEV, o_shape[2], N)
    return out.reshape(M, N) if o_seg else out


def _row_tile(T, widths):
    budget = 6 * 1024 * 1024
    tb = max(16, budget // (4 * sum(widths)))
    return _pick(T, tb, align=16)


def _rows(fn, rows, params, outs, *, name):
    T = rows[0].shape[0]
    nr, npar = len(rows), len(params)
    tb = _row_tile(T, [r.shape[1] for r in rows] + [w for w, _ in outs])

    def body(*refs):
        ins = [r[...].astype(f32) for r in refs[: nr + npar]]
        res = fn(*ins)
        for o_ref, r in zip(refs[nr + npar:], res):
            o_ref[...] = r.astype(o_ref.dtype)

    in_specs = [pl.BlockSpec((tb, r.shape[1]), lambda i: (i, 0)) for r in rows]
    in_specs += [pl.BlockSpec(p.shape, lambda i: (0, 0)) for p in params]
    out_specs = [pl.BlockSpec((tb, w), lambda i: (i, 0)) for w, _ in outs]
    res = pl.pallas_call(
        body, name=name, grid=(T // tb,), in_specs=in_specs, out_specs=out_specs,
        out_shape=[_S((T, w), d) for w, d in outs], compiler_params=_cparams(("parallel",)),
    )(*rows, *params)
    return tuple(res)


def _rows_bwd(fn, rows, params, cots, *, name, want_rows, row_dtypes=None, adds=None):
    T = rows[0].shape[0]
    nr, npar, nc = len(rows), len(params), len(cots)
    adds = adds or {}
    add_idx = sorted(adds)
    row_dtypes = row_dtypes or {}
    widths = [r.shape[1] for r in rows] + [c.shape[1] for c in cots] + [rows[i].shape[1] for i in want_rows]
    tb = _row_tile(T, widths)

    def body(*refs):
        ins = [r[...].astype(f32) for r in refs[: nr + npar]]
        cot = tuple(r[...].astype(f32) for r in refs[nr + npar: nr + npar + nc])
        add_refs = refs[nr + npar + nc: nr + npar + nc + len(add_idx)]
        out_refs = refs[nr + npar + nc + len(add_idx):]
        _, vjp = jax.vjp(lambda *a: tuple(fn(*a)), *ins)
        g = vjp(cot)
        for o_ref, i in zip(out_refs[: len(want_rows)], want_rows):
            r = g[i]
            if i in adds:
                r = r + add_refs[add_idx.index(i)][...].astype(f32)
            o_ref[...] = r.astype(o_ref.dtype)
        first = pl.program_id(0) == 0
        for o_ref, gp in zip(out_refs[len(want_rows):], g[nr:]):
            @pl.when(first)
            def _(o_ref=o_ref):
                o_ref[...] = jnp.zeros_like(o_ref)

            o_ref[...] += gp

    in_specs = [pl.BlockSpec((tb, r.shape[1]), lambda i: (i, 0)) for r in rows]
    in_specs += [pl.BlockSpec(p.shape, lambda i: (0, 0)) for p in params]
    in_specs += [pl.BlockSpec((tb, c.shape[1]), lambda i: (i, 0)) for c in cots]
    in_specs += [pl.BlockSpec((tb, adds[i].shape[1]), lambda i_: (i_, 0)) for i in add_idx]
    out_specs = [pl.BlockSpec((tb, rows[i].shape[1]), lambda i_: (i_, 0)) for i in want_rows]
    out_specs += [pl.BlockSpec(p.shape, lambda i: (0, 0)) for p in params]
    out_shape = [_S(rows[i].shape, row_dtypes.get(i, f32)) for i in want_rows] + [_S(p.shape, f32) for p in params]
    res = pl.pallas_call(
        body, name=name, grid=(T // tb,), in_specs=in_specs, out_specs=out_specs, out_shape=out_shape,
        compiler_params=_cparams(("arbitrary",)),
    )(*rows, *params, *cots, *[adds[i] for i in add_idx])
    return list(res[: len(want_rows)]), list(res[len(want_rows):])


def _f_rmsnorm(x, g):
    return (x * lax.rsqrt(jnp.mean(x * x, axis=-1, keepdims=True) + EPS) * g,)


def _f_swiglu(ab):
    F = ab.shape[1] // 2
    return (jax.nn.silu(ab[:, :F]) * ab[:, F:],)


def _f_s5_post(y, u, d):
    return (jax.nn.gelu(y + d * u),)


def _f_glu(g, v, b):
    return (g * jax.nn.sigmoid(v + b),)


def _f_gated_norm(y, z, w):
    return _f_rmsnorm(y * jax.nn.silu(z), w)


def _f_merge(gl, p5, pssd, b):
    D = p5.shape[1]
    gates = jax.nn.sigmoid(gl + b)
    return (gates[:, :D] * p5 + gates[:, D:] * pssd,)


def _f_dt(dtr, bias, a_log):
    dt = jax.nn.softplus(dtr + bias)
    return dt, dt * (-jnp.exp(a_log))


def _loss_stage(x, tgt, g, *, name):
    T, D = x.shape
    tb = _row_tile(T, [D, D, D])

    def f(xb, gb, tb_):
        y = _f_rmsnorm(xb, gb)[0]
        return 0.5 * jnp.sum(jnp.mean(jnp.square(y - tb_), axis=-1, keepdims=True), axis=0, keepdims=True)

    def body(x_ref, t_ref, g_ref, l_ref, dx_ref, dg_ref):
        tv = t_ref[...]
        val, vjp = jax.vjp(lambda a, b: f(a, b, tv), x_ref[...], g_ref[...])
        dx, dg = vjp(jnp.ones((1, 1), f32))
        dx_ref[...] = dx

        @pl.when(pl.program_id(0) == 0)
        def _():
            l_ref[...] = jnp.zeros_like(l_ref)
            dg_ref[...] = jnp.zeros_like(dg_ref)

        l_ref[...] += jnp.broadcast_to(val, l_ref.shape)
        dg_ref[...] += dg

    row = pl.BlockSpec((tb, D), lambda i: (i, 0))
    par = pl.BlockSpec((1, D), lambda i: (0, 0))
    return pl.pallas_call(
        body, name=name, grid=(T // tb,), in_specs=[row, row, par],
        out_specs=[pl.BlockSpec((1, LANES), lambda i: (0, 0)), row, par],
        out_shape=[_S((1, LANES), f32), _S((T, D), f32), _S((1, D), f32)], compiler_params=_cparams(("arbitrary",)),
    )(x, tgt, g)


def _shift_down(x, s):
    if s == 0:
        return x
    t = lax.broadcasted_iota(jnp.int32, x.shape, 0)
    return jnp.where(t >= s, pltpu.roll(x, s, axis=0), 0.0)


def _shift_up(x, s):
    if s == 0:
        return x
    T = x.shape[0]
    t = lax.broadcasted_iota(jnp.int32, x.shape, 0)
    return jnp.where(t < T - s, pltpu.roll(x, T - s, axis=0), 0.0)


def _conv_pre(x, w, b):
    pre = b
    for k in range(CONV_K):
        pre = pre + w[k:k + 1, :] * _shift_down(x, CONV_K - 1 - k)
    return pre


def _conv_fwd(x, w, b, *, name):
    T, C = x.shape
    cb = _pick(C, 256)

    def body(x_ref, w_ref, b_ref, o_ref):
        o_ref[...] = jax.nn.silu(_conv_pre(x_ref[...], w_ref[...], b_ref[...]))

    col = pl.BlockSpec((T, cb), lambda j: (0, j))
    return pl.pallas_call(
        body, name=name, grid=(C // cb,), in_specs=[col, pl.BlockSpec((CONV_K, cb), lambda j: (0, j)), pl.BlockSpec((1, cb), lambda j: (0, j))],
        out_specs=col, out_shape=_S((T, C), f32), compiler_params=_cparams(("parallel",)),
    )(x, w, b)


def _conv_bwd(x, w, b, dy, *, name):
    T, C = x.shape
    cb = _pick(C, 256)

    def body(x_ref, w_ref, b_ref, dy_ref, dx_ref, dw_ref, db_ref):
        xv, wv = x_ref[...], w_ref[...]
        pre = _conv_pre(xv, wv, b_ref[...])
        sg = jax.nn.sigmoid(pre)
        dpre = dy_ref[...] * sg * (1.0 + pre * (1.0 - sg))
        dx = jnp.zeros_like(xv)
        for k in range(CONV_K):
            s = CONV_K - 1 - k
            dx = dx + wv[k:k + 1, :] * _shift_up(dpre, s)
            dw_ref[k:k + 1, :] = jnp.sum(dpre * _shift_down(xv, s), axis=0, keepdims=True)
        dx_ref[...] = dx
        db_ref[...] = jnp.sum(dpre, axis=0, keepdims=True)

    col = pl.BlockSpec((T, cb), lambda j: (0, j))
    wsp = pl.BlockSpec((CONV_K, cb), lambda j: (0, j))
    bsp = pl.BlockSpec((1, cb), lambda j: (0, j))
    return pl.pallas_call(
        body, name=name, grid=(C // cb,), in_specs=[col, wsp, bsp, col], out_specs=[col, wsp, bsp],
        out_shape=[_S((T, C), f32), _S((CONV_K, C), f32), _S((1, C), f32)], compiler_params=_cparams(("parallel",)),
    )(x, w, b, dy)


def _f_s5_prep(lr, li, ldt, lrb, lib, ldtb, brt, bit):
    def disc(lr_, li_, ldt_):
        dt = jnp.exp(ldt_)
        mag = jnp.exp(lr_ * dt)
        ar, ai = mag * jnp.cos(li_ * dt), mag * jnp.sin(li_ * dt)
        den = lr_ * lr_ + li_ * li_
        cr = ((ar - 1.0) * lr_ + ai * li_) / den
        ci = (ai * lr_ - (ar - 1.0) * li_) / den
        return ar, ai, cr, ci

    ar, ai, _, _ = disc(lr, li, ldt)
    _, _, cr, ci = disc(lrb, lib, ldtb)
    return ar, ai, cr * brt - ci * bit, cr * bit + ci * brt


def _s5_prep(args, *, name):
    G, N = args[0].shape
    GM = args[3].shape[0]

    def body(*refs):
        res = _f_s5_prep(*[r[...] for r in refs[:8]])
        for o, r in zip(refs[8:], res):
            o[...] = r

    return pl.pallas_call(body, name=name, out_shape=[_S((G, N), f32)] * 2 + [_S((GM, N), f32)] * 2)(*args)


def _s5_prep_bwd(args, cots, rsum, *, name):
    G, N = args[0].shape
    GM = args[3].shape[0]

    def body(*refs):
        ins = [r[...] for r in refs[:8]]
        cot = tuple(r[...] for r in refs[8:12])
        rs = refs[12][...]
        _, vjp = jax.vjp(_f_s5_prep, *ins)
        g = vjp(cot)
        fold = lambda v: jnp.dot(rs, v, preferred_element_type=f32, precision=lax.Precision.HIGHEST)
        o = refs[13:]
        o[0][...] = g[0] + fold(g[3])
        o[1][...] = g[1] + fold(g[4])
        o[2][...] = g[2] + fold(jnp.broadcast_to(g[5], (GM, LANES)))[:, 0:1]
        o[3][...] = g[6]
        o[4][...] = g[7]

    return pl.pallas_call(
        body, name=name, out_shape=[_S((G, N), f32), _S((G, N), f32), _S((G, 1), f32), _S((GM, N), f32), _S((GM, N), f32)],
    )(*args, *cots, rsum)


S5_TC = 512


def _s5_local_scan(src, w_r, w_i, a_r, a_i, *, reverse, name):
    T, C = src.shape
    nblk, cb, sb = w_r.shape
    NS = nblk * sb
    tc = min(S5_TC, T)
    nT, nt = T // tc, tc // NSEG
    tmap = (lambda i: nT - 1 - i) if reverse else (lambda i: i)

    def body(u_ref, wr_ref, wi_ref, ar_ref, ai_ref, sr_ref, si_ref, pr_ref, pi_ref, st_r, st_i, pw_r, pw_i):
        @pl.when(pl.program_id(1) == 0)
        def _():
            st_r[...] = jnp.zeros_like(st_r)
            st_i[...] = jnp.zeros_like(st_i)
            pw_r[...] = jnp.ones_like(pw_r)
            pw_i[...] = jnp.zeros_like(pw_i)

        u = u_ref[...].astype(bf16)
        sr_ref[...] = jnp.dot(u, wr_ref[...], preferred_element_type=f32)
        si_ref[...] = jnp.dot(u, wi_ref[...], preferred_element_type=f32)
        ar = jnp.broadcast_to(ar_ref[...], (NSEG, sb))
        ai = jnp.broadcast_to(ai_ref[...], (NSEG, sb))

        def step(k, c):
            cr, ci, qr, qi = c
            kk = (nt - 1 - k) if reverse else k
            rows = pl.ds(pl.multiple_of(kk * NSEG, NSEG), NSEG)
            nr = ar * cr - ai * ci + sr_ref[rows, :]
            ni = ar * ci + ai * cr + si_ref[rows, :]
            sr_ref[rows, :] = nr
            si_ref[rows, :] = ni
            return nr, ni, ar * qr - ai * qi, ar * qi + ai * qr

        cr, ci, qr, qi = lax.fori_loop(0, nt, step, (st_r[...], st_i[...], pw_r[...], pw_i[...]), unroll=8)
        st_r[...], st_i[...], pw_r[...], pw_i[...] = cr, ci, qr, qi
        pr_ref[...] = qr
        pi_ref[...] = qi

    blk = pl.BlockSpec((tc, sb), lambda j, i: (tmap(i), j))
    wsp = pl.BlockSpec((None, cb, sb), lambda j, i: (j, 0, 0))
    asp = pl.BlockSpec((1, sb), lambda j, i: (0, j))
    psp = pl.BlockSpec((NSEG, sb), lambda j, i: (0, j))
    return pl.pallas_call(
        body, name=name, grid=(nblk, nT), in_specs=[pl.BlockSpec((tc, cb), lambda j, i: (tmap(i), j)), wsp, wsp, asp, asp],
        out_specs=[blk, blk, psp, psp], out_shape=[_S((T, NS), f32)] * 2 + [_S((NSEG, NS), f32)] * 2,
        scratch_shapes=[pltpu.VMEM((NSEG, sb), f32)] * 4, compiler_params=_cparams(("parallel", "arbitrary")),
    )(src, w_r, w_i, a_r, a_i)


def _s5_carry(e_r, e_i, p_r, p_i, *, reverse, name):
    NS = e_r.shape[1]

    def body(er_ref, ei_ref, pr_ref, pi_ref, cr_ref, ci_ref):
        ar, ai = pr_ref[0:1, :], pi_ref[0:1, :]
        cr = jnp.zeros((1, NS), f32)
        ci = jnp.zeros((1, NS), f32)
        order = list(range(NSEG - 1, -1, -1)) if reverse else list(range(NSEG))
        cr_ref[order[0]:order[0] + 1, :] = cr
        ci_ref[order[0]:order[0] + 1, :] = ci
        for prev, q in zip(order[:-1], order[1:]):
            er, ei = er_ref[prev:prev + 1, :], ei_ref[prev:prev + 1, :]
            cr, ci = er + ar * cr - ai * ci, ei + ar * ci + ai * cr
            cr_ref[q:q + 1, :] = cr
            ci_ref[q:q + 1, :] = ci

    return pl.pallas_call(body, name=name, out_shape=[_S((NSEG, NS), f32)] * 2)(e_r, e_i, p_r, p_i)


def _s5_fix_out(sl_r, sl_i, a_r, a_i, c_r, c_i, wc_r, wc_i, *, name):
    T, NS = sl_r.shape
    nblk, sb, cb = wc_r.shape
    tc = min(S5_TC, T)
    nT, nt = T // tc, tc // NSEG

    def body(lr_ref, li_ref, ar_ref, ai_ref, cr_ref, ci_ref, wr_ref, wi_ref, sr_ref, si_ref, y_ref, pw_r, pw_i):
        @pl.when(pl.program_id(1) == 0)
        def _():
            pw_r[...] = jnp.ones_like(pw_r)
            pw_i[...] = jnp.zeros_like(pw_i)

        ar = jnp.broadcast_to(ar_ref[...], (NSEG, sb))
        ai = jnp.broadcast_to(ai_ref[...], (NSEG, sb))
        cr, ci = cr_ref[...], ci_ref[...]

        def step(k, c):
            qr, qi = c
            qr, qi = ar * qr - ai * qi, ar * qi + ai * qr
            rows = pl.ds(pl.multiple_of(k * NSEG, NSEG), NSEG)
            sr_ref[rows, :] = lr_ref[rows, :] + qr * cr - qi * ci
            si_ref[rows, :] = li_ref[rows, :] + qr * ci + qi * cr
            return qr, qi

        qr, qi = lax.fori_loop(0, nt, step, (pw_r[...], pw_i[...]), unroll=8)
        pw_r[...], pw_i[...] = qr, qi
        y_ref[...] = (jnp.dot(sr_ref[...].astype(bf16), wr_ref[...], preferred_element_type=f32)
                      - jnp.dot(si_ref[...].astype(bf16), wi_ref[...], preferred_element_type=f32))

    blk = pl.BlockSpec((tc, sb), lambda j, i: (i, j))
    asp = pl.BlockSpec((1, sb), lambda j, i: (0, j))
    csp = pl.BlockSpec((NSEG, sb), lambda j, i: (0, j))
    wsp = pl.BlockSpec((None, sb, cb), lambda j, i: (j, 0, 0))
    return pl.pallas_call(
        body, name=name, grid=(nblk, nT), in_specs=[blk, blk, asp, asp, csp, csp, wsp, wsp],
        out_specs=[blk, blk, pl.BlockSpec((tc, cb), lambda j, i: (i, j))],
        out_shape=[_S((T, NS), f32)] * 2 + [_S((T, nblk * cb), f32)],
        scratch_shapes=[pltpu.VMEM((NSEG, sb), f32)] * 2, compiler_params=_cparams(("parallel", "arbitrary")),
    )(sl_r, sl_i, a_r, a_i, c_r, c_i, wc_r, wc_i)


def _s5_fix_bwd(ql_r, ql_i, ab_r, ab_i, c_r, c_i, s_r, s_i, sb_r, sb_i, u, dy, du_add, w_r, w_i, *, name):
    T, NS = ql_r.shape
    nblk, cb, sb = w_r.shape
    tc = min(S5_TC, T)
    nT, nt = T // tc, tc // NSEG
    tmap = lambda i: nT - 1 - i

    def body(lr_ref, li_ref, ar_ref, ai_ref, cr_ref, ci_ref, sr_ref, si_ref, br_ref, bi_ref, u_ref, dy_ref, dua_ref, wr_ref, wi_ref,
             du_ref, dwr_ref, dwi_ref, dcr_ref, dci_ref, dar_ref, dai_ref, pw_r, pw_i, ac_r, ac_i, q_r, q_i):
        first = pl.program_id(1) == 0

        @pl.when(first)
        def _():
            pw_r[...] = jnp.ones_like(pw_r)
            pw_i[...] = jnp.zeros_like(pw_i)
            ac_r[...] = jnp.zeros_like(ac_r)
            ac_i[...] = jnp.zeros_like(ac_i)
            dwr_ref[...] = jnp.zeros_like(dwr_ref)
            dwi_ref[...] = jnp.zeros_like(dwi_ref)
            dcr_ref[...] = jnp.zeros_like(dcr_ref)
            dci_ref[...] = jnp.zeros_like(dci_ref)

        ar = jnp.broadcast_to(ar_ref[...], (NSEG, sb))
        ai = jnp.broadcast_to(ai_ref[...], (NSEG, sb))
        cr, ci = cr_ref[...], ci_ref[...]

        def fix(rows, qr, qi, spr, spi, accr, acci):
            qr, qi = ar * qr - ai * qi, ar * qi + ai * qr
            xr = lr_ref[rows, :] + qr * cr - qi * ci
            xi = li_ref[rows, :] + qr * ci + qi * cr
            q_r[rows, :] = xr
            q_i[rows, :] = xi
            return qr, qi, accr + xr * spr + xi * spi, acci + xi * spr - xr * spi

        def step(k, c):
            qr, qi, accr, acci = c
            kk = nt - 1 - k
            rows = pl.ds(pl.multiple_of(kk * NSEG, NSEG), NSEG)
            prev = pl.ds(pl.multiple_of((kk - 1) * NSEG, NSEG), NSEG)
            return fix(rows, qr, qi, sr_ref[prev, :], si_ref[prev, :], accr, acci)

        c = lax.fori_loop(0, nt - 1, step, (pw_r[...], pw_i[...], ac_r[...], ac_i[...]), unroll=7)
        qr, qi, accr, acci = fix(pl.ds(0, NSEG), *c[:2], br_ref[...], bi_ref[...], *c[2:])
        pw_r[...], pw_i[...], ac_r[...], ac_i[...] = qr, qi, accr, acci

        qrb, qib = q_r[...].astype(bf16), q_i[...].astype(bf16)
        nt_dims = (((1,), (1,)), ((), ()))
        tn_dims = (((0,), (0,)), ((), ()))
        du_ref[...] = (dua_ref[...] + lax.dot_general(qrb, wr_ref[...], nt_dims, preferred_element_type=f32)
                       + lax.dot_general(qib, wi_ref[...], nt_dims, preferred_element_type=f32))
        ub = u_ref[...].astype(bf16)
        dwr_ref[...] += lax.dot_general(ub, qrb, tn_dims, preferred_element_type=f32)
        dwi_ref[...] += lax.dot_general(ub, qib, tn_dims, preferred_element_type=f32)
        dyb = dy_ref[...].astype(bf16)
        dcr_ref[...] += lax.dot_general(sr_ref[...].astype(bf16), dyb, tn_dims, preferred_element_type=f32)
        dci_ref[...] -= lax.dot_general(si_ref[...].astype(bf16), dyb, tn_dims, preferred_element_type=f32)

        @pl.when(pl.program_id(1) == nT - 1)
        def _():
            dar_ref[...] = jnp.sum(accr, axis=0, keepdims=True)
            dai_ref[...] = jnp.sum(acci, axis=0, keepdims=True)

    blk = pl.BlockSpec((tc, sb), lambda j, i: (tmap(i), j))
    asp = pl.BlockSpec((1, sb), lambda j, i: (0, j))
    csp = pl.BlockSpec((NSEG, sb), lambda j, i: (0, j))
    bsp = pl.BlockSpec((None, NSEG, sb), lambda j, i: (tmap(i), 0, j))
    chn = pl.BlockSpec((tc, cb), lambda j, i: (tmap(i), j))
    wsp = pl.BlockSpec((None, cb, sb), lambda j, i: (j, 0, 0))
    wcs = pl.BlockSpec((None, sb, cb), lambda j, i: (j, 0, 0))
    return pl.pallas_call(
        body, name=name, grid=(nblk, nT), in_specs=[blk, blk, asp, asp, csp, csp, blk, blk, bsp, bsp, chn, chn, chn, wsp, wsp],
        out_specs=[chn, wsp, wsp, wcs, wcs, asp, asp],
        out_shape=[_S((T, nblk * cb), f32), _S((nblk, cb, sb), f32), _S((nblk, cb, sb), f32), _S((nblk, sb, cb), f32),
                   _S((nblk, sb, cb), f32), _S((1, NS), f32), _S((1, NS), f32)],
        scratch_shapes=[pltpu.VMEM((NSEG, sb), f32)] * 4 + [pltpu.VMEM((tc, sb), f32)] * 2,
        compiler_params=_cparams(("parallel", "arbitrary")),
    )(ql_r, ql_i, ab_r, ab_i, c_r, c_i, s_r, s_i, sb_r, sb_i, u, dy, du_add, w_r, w_i)


SSD2_TB = 512
_HI = lax.Precision.HIGHEST
_NN = (((1,), (0,)), ((), ()))
_NT = (((1,), (1,)), ((), ()))
_TN = (((0,), (0,)), ((), ()))


def _dotf(a, b, dims):
    return lax.dot_general(a.astype(bf16), b.astype(bf16), dims, preferred_element_type=f32)


def _doth(a, b, dims=_NN):
    return lax.dot_general(a, b, dims, preferred_element_type=f32, precision=_HI)


def _ssd_consts(hpg):
    W = hpg * CHUNK
    i = lax.broadcasted_iota(jnp.int32, (CHUNK, CHUNK), 0)
    j = lax.broadcasted_iota(jnp.int32, (CHUNK, CHUNK), 1)
    tril = (i >= j).astype(f32)
    r = lax.broadcasted_iota(jnp.int32, (W, W), 0)
    c = lax.broadcasted_iota(jnp.int32, (W, W), 1)
    bd = (r // CHUNK == c // CHUNK).astype(f32)
    triu_bd = bd * (r <= c).astype(f32)
    e_r = lax.broadcasted_iota(jnp.int32, (W, LANES), 0)
    e_c = lax.broadcasted_iota(jnp.int32, (W, LANES), 1)
    ered = (e_r // HEADDIM == e_c).astype(f32)
    return tril, jnp.tile(tril, (1, hpg)), bd, triu_bd, ered


def _ssd2_specs(G, hpg, tb, tmap, b_off, c_off):
    W = hpg * HEADDIM
    ncb = tb // CHUNK
    xsp = pl.BlockSpec((tb, W), lambda g, i: (tmap(i), g))
    bsp = pl.BlockSpec((tb, SSD_STATE), lambda g, i: (tmap(i), b_off + g))
    csp = pl.BlockSpec((tb, SSD_STATE), lambda g, i: (tmap(i), c_off + g))
    rsp = pl.BlockSpec((None, ncb, W), lambda g, i: (g, tmap(i), 0))
    dsp = pl.BlockSpec((1, W), lambda g, i: (0, g))
    hsp = pl.BlockSpec((None, ncb, SSD_STATE, W), lambda g, i: (g, tmap(i), 0, 0))
    const = lambda a: pl.BlockSpec(a.shape, lambda g, i: (0, 0))
    return xsp, bsp, csp, rsp, dsp, hsp, const


def _tile_rows(a, n):
    return jnp.concatenate([a] * n, axis=0)


def _ssd2_fwd(xc, dt4, a4, dtw, aw, d4, consts, *, d_inner, name):
    T = xc.shape[0]
    G, nc, W = dtw.shape
    hpg = W // CHUNK
    tb = min(SSD2_TB, T)
    nb, ncb = T // tb, tb // CHUNK
    b_off = d_inner // SSD_STATE
    xsp, bsp, csp, rsp, dsp, hsp, const = _ssd2_specs(G, hpg, tb, lambda i: i, b_off, b_off + G)
    tril, mask4, bd, triu_bd, _ = consts

    def body(x_ref, b_ref, c_ref, dt_ref, a_ref, dtw_ref, aw_ref, d_ref, tril_ref, mask_ref, bd_ref, tbd_ref, y_ref, hs_ref, h_scr):
        @pl.when(pl.program_id(1) == 0)
        def _():
            h_scr[...] = jnp.zeros_like(h_scr)

        acs_rows = _doth(aw_ref[...], tbd_ref[...])
        ht = h_scr[...]
        for c in range(ncb):
            rows = slice(c * CHUNK, (c + 1) * CHUNK)
            x, bm, cm = x_ref[rows, :], b_ref[rows, :], c_ref[rows, :]
            acs = _doth(tril_ref[...], a_ref[rows, :])
            lmat = jnp.where(mask_ref[...] > 0, jnp.exp(jnp.minimum(acs - acs_rows[c:c + 1, :], 0.0)), 0.0)
            m4 = _dotf(cm, _tile_rows(bm, hpg), _NT) * lmat * dtw_ref[c:c + 1, :]
            xbd = _tile_rows(x, hpg) * bd_ref[...]
            hs_ref[c] = ht
            y_ref[rows, :] = _dotf(m4, xbd, _NN) + _dotf(cm, ht, _NN) * jnp.exp(acs) + d_ref[...] * x
            a_last = acs[CHUNK - 1:CHUNK, :]
            xw = x * (jnp.exp(a_last - acs) * dt_ref[rows, :])
            ht = ht * jnp.exp(a_last) + _dotf(bm, xw, _TN)
        h_scr[...] = ht

    return pl.pallas_call(
        body, name=name, grid=(G, nb),
        in_specs=[xsp, bsp, csp, xsp, xsp, rsp, rsp, dsp, const(tril), const(mask4), const(bd), const(triu_bd)],
        out_specs=[xsp, hsp], out_shape=[_S((T, G * W), f32), _S((G, nc, SSD_STATE, W), f32)],
        scratch_shapes=[pltpu.VMEM((SSD_STATE, W), f32)], compiler_params=_cparams(("parallel", "arbitrary")),
    )(xc, xc, xc, dt4, a4, dtw, aw, d4, tril, mask4, bd, triu_bd)


def _ssd2_bwd(xc, dt4, a4, dtw, aw, d4, consts, hs, dy, *, d_inner, name):
    T = xc.shape[0]
    G, nc, W = dtw.shape
    hpg = W // CHUNK
    tb = min(SSD2_TB, T)
    nb, ncb = T // tb, tb // CHUNK
    b_off = d_inner // SSD_STATE
    tmap = lambda i: nb - 1 - i
    xsp, bsp, csp, rsp, dsp, hsp, const = _ssd2_specs(G, hpg, tb, tmap, b_off, b_off + G)
    gsp = pl.BlockSpec((tb, SSD_STATE), lambda g, i: (tmap(i), g))
    ddsp = pl.BlockSpec((None, 1, LANES), lambda g, i: (g, 0, 0))
    tril, mask4, bd, triu_bd, ered = consts

    def body(x_ref, b_ref, c_ref, dt_ref, a_ref, dtw_ref, aw_ref, d_ref, tril_ref, mask_ref, bd_ref, tbd_ref, er_ref, hs_ref, dy_ref,
             dx_ref, db_ref, dc_ref, ddtc_ref, dac_ref, ddtw_ref, daw_ref, dd_ref, g_scr, dd_scr, rw_scr, tl_scr):
        first = pl.program_id(1) == 0

        @pl.when(first)
        def _():
            g_scr[...] = jnp.zeros_like(g_scr)
            dd_scr[...] = jnp.zeros_like(dd_scr)

        mask = mask_ref[...] > 0
        lane_in_block = lax.broadcasted_iota(jnp.int32, mask.shape, 1) & (CHUNK - 1)
        maskt = lax.broadcasted_iota(jnp.int32, mask.shape, 0) <= lane_in_block
        acs_rows = _doth(aw_ref[...], tbd_ref[...])
        dht = g_scr[...]
        dd = dd_scr[...]
        for c in range(ncb - 1, -1, -1):
            rows = slice(c * CHUNK, (c + 1) * CHUNK)
            x, bm, cm, dyc = x_ref[rows, :], b_ref[rows, :], c_ref[rows, :], dy_ref[rows, :]
            dtc, dtr = dt_ref[rows, :], dtw_ref[c:c + 1, :]
            ht = hs_ref[c]
            acs = _doth(tril_ref[...], a_ref[rows, :])
            seg = acs - acs_rows[c:c + 1, :]
            lmat = jnp.where(mask, jnp.exp(jnp.minimum(seg, 0.0)), 0.0)
            lmat_t = jnp.where(maskt, jnp.exp(jnp.minimum(-seg, 0.0)), 0.0)
            btile, ctile = _tile_rows(bm, hpg), _tile_rows(cm, hpg)
            g4 = _dotf(cm, btile, _NT)
            gt4 = _dotf(bm, ctile, _NT)
            m4 = g4 * lmat * dtr
            mt4 = gt4 * lmat_t * dtc
            xbd = _tile_rows(x, hpg) * bd_ref[...]
            dybd = _tile_rows(dyc, hpg) * bd_ref[...]
            dm4 = _dotf(dyc, xbd, _NT)
            dmt4 = _dotf(x, dybd, _NT)
            dx = d_ref[...] * dyc + _dotf(mt4, dybd, _NN)
            dd = dd + jnp.sum(dyc * x, axis=0, keepdims=True)
            e4 = dm4 * m4
            dc = _dotf(dm4 * lmat * dtr, btile, _NN)
            db = _dotf(dmt4 * lmat_t * dtc, ctile, _NN)
            decay = jnp.exp(acs)
            yoff = _dotf(cm, ht, _NN) * decay
            dz = dyc * decay
            dc = dc + _dotf(dz, ht, _NT)
            dht_prev = _dotf(cm, dz, _TN)
            a_last = acs[CHUNK - 1:CHUNK, :]
            ea_last = jnp.exp(a_last)
            erel = jnp.exp(a_last - acs)
            dte = erel * dtc
            dxw = _dotf(bm, dht, _NN)
            db = db + _dotf(x * dte, dht, _NT)
            dx = dx + dxw * dte
            q4 = dxw * x
            dacs = e4 + dyc * yoff - q4 * dte
            col = jnp.concatenate([q4 * erel, _doth(tril_ref[...], dacs, _TN)], axis=0)
            col = _doth(col, er_ref[...])
            ddtc_ref[rows, :] = col[:CHUNK]
            dac_ref[rows, :] = col[CHUNK:]
            ddtw_ref[c:c + 1, :] = jnp.sum(dm4 * g4 * lmat, axis=0, keepdims=True)
            rw_scr[c:c + 1, :] = -jnp.sum(e4, axis=0, keepdims=True)
            tl_scr[c:c + 1, :] = jnp.sum(q4 * dte, axis=0, keepdims=True) + ea_last * jnp.sum(dht * ht, axis=0, keepdims=True)
            dx_ref[rows, :] = dx
            db_ref[rows, :] = db
            dc_ref[rows, :] = dc
            dht = dht_prev + dht * ea_last
        daw_ref[...] = _doth(rw_scr[...], tbd_ref[...], _NT) + _doth(tl_scr[...], bd_ref[...])
        g_scr[...] = dht
        dd_scr[...] = dd

        @pl.when(pl.program_id(1) == nb - 1)
        def _():
            dd_ref[...] = _doth(dd, er_ref[...])

    return pl.pallas_call(
        body, name=name, grid=(G, nb),
        in_specs=[xsp, bsp, csp, xsp, xsp, rsp, rsp, dsp, const(tril), const(mask4), const(bd), const(triu_bd), const(ered), hsp, xsp],
        out_specs=[xsp, gsp, gsp, gsp, gsp, rsp, rsp, ddsp],
        out_shape=[_S((T, G * W), f32), _S((T, G * SSD_STATE), f32), _S((T, G * SSD_STATE), f32), _S((T, G * LANES), f32),
                   _S((T, G * LANES), f32), _S(dtw.shape, f32), _S(dtw.shape, f32), _S((G, 1, LANES), f32)],
        scratch_shapes=[pltpu.VMEM((SSD_STATE, W), f32), pltpu.VMEM((1, W), f32), pltpu.VMEM((ncb, W), f32), pltpu.VMEM((ncb, W), f32)],
        compiler_params=_cparams(("parallel", "arbitrary")),
    )(xc, xc, xc, dt4, a4, dtw, aw, d4, tril, mask4, bd, triu_bd, ered, hs, dy)


def _peers():
    x, y, c = lax.axis_index("x"), lax.axis_index("y"), lax.axis_index("c")
    return x, y, c


_ANY = pl.BlockSpec(memory_space=pl.ANY)
N_CHIP = N_DEV // 2


def _all_gather(shards, *, name):
    n = len(shards)

    def body(*refs):
        x_refs, out_refs = refs[:n], refs[n:2 * n]
        send_sems, recv_sems, local_sems = refs[2 * n:]
        x, y, c = _peers()
        me, sibling = (x, y, c), (x, y, 1 - c)
        chips = [(1 - x, y), (x, 1 - y), (1 - x, 1 - y)]

        def copy(a, r, block, to, src=None):
            px, py, pc = block
            slot = out_refs[a].at[4 * px + 2 * py + pc]
            return pltpu.make_async_remote_copy(
                src_ref=slot if src is None else src, dst_ref=slot, send_sem=send_sems.at[7 * a + r],
                recv_sem=recv_sems.at[7 * a + r], device_id=to, device_id_type=MESH)

        mine = [pltpu.make_async_copy(x_refs[a], out_refs[a].at[4 * x + 2 * y + c], local_sems.at[a]) for a in range(n)]
        for cp in mine:
            cp.start()
        first = []
        for a in range(n):
            first.append(copy(a, 0, me, sibling, src=x_refs[a]))
            first += [copy(a, 1 + j, me, (*chip, c), src=x_refs[a]) for j, chip in enumerate(chips)]
        for cp in first:
            cp.start()
        passed = []
        for j, chip in enumerate(chips):
            for a in range(n):
                copy(a, 1 + j, (*chip, c), me).wait_recv()
                fwd = copy(a, 4 + j, (*chip, c), sibling)
                fwd.start()
                passed.append(fwd)
        for a in range(n):
            copy(a, 0, sibling, me).wait_recv()
        for j, chip in enumerate(chips):
            for a in range(n):
                copy(a, 4 + j, (*chip, 1 - c), me).wait_recv()
        for cp in first + passed:
            cp.wait_send()
        for cp in mine:
            cp.wait()

    return pl.pallas_call(
        body, name=name, out_shape=[_S((N_DEV,) + s.shape, s.dtype) for s in shards], in_specs=[_ANY] * n, out_specs=[_ANY] * n,
        scratch_shapes=[pltpu.SemaphoreType.DMA((7 * n,)), pltpu.SemaphoreType.DMA((7 * n,)), pltpu.SemaphoreType.DMA((n,))],
    )(*shards)


def _exchange_sibling(slots, *, name):
    n = len(slots)

    def body(*refs):
        x_refs, sib_refs = refs[:n], refs[n:2 * n]
        send_sems, recv_sems = refs[2 * n:]
        x, y, c = _peers()
        give = [pltpu.make_async_remote_copy(
            src_ref=x_refs[a].at[pl.ds(N_CHIP * (1 - c), N_CHIP)], dst_ref=sib_refs[a], send_sem=send_sems.at[a],
            recv_sem=recv_sems.at[a], device_id=(x, y, 1 - c), device_id_type=MESH) for a in range(n)]
        for cp in give:
            cp.start()
        for cp in give:
            cp.wait_recv()
        for cp in give:
            cp.wait_send()

    return list(pl.pallas_call(
        body, name=name, out_shape=[_S((N_CHIP,) + s.shape[1:], s.dtype) for s in slots], in_specs=[_ANY] * n, out_specs=[_ANY] * n,
        scratch_shapes=[pltpu.SemaphoreType.DMA((n,)), pltpu.SemaphoreType.DMA((n,))],
    )(*slots))


def _chip_sum(slots, sib, core, *, name):
    _, R, W = slots.shape
    tr = _pick(R, max(16, (1 << 20) // (4 * W)), align=16)

    def body(core_ref, x_ref, s_ref, o_ref):
        o_ref[...] = (x_ref[...].astype(f32) + s_ref[...].astype(f32)).astype(o_ref.dtype)

    blk = pl.BlockSpec((None, tr, W), lambda t, i, core_ref: (t, i, 0))
    return pl.pallas_call(
        body, name=name, out_shape=_S(sib.shape, slots.dtype),
        grid_spec=pltpu.PrefetchScalarGridSpec(
            num_scalar_prefetch=1, grid=(N_CHIP, R // tr),
            in_specs=[pl.BlockSpec((None, tr, W), lambda t, i, core_ref: (N_CHIP * core_ref[0] + t, i, 0)), blk], out_specs=blk),
        compiler_params=_cparams(("parallel", "parallel")),
    )(core, slots, sib)


def _exchange_chips(parts, *, name):
    n = len(parts)

    def body(*refs):
        p_refs, out_refs = refs[:n], refs[n:2 * n]
        send_sems, recv_sems = refs[2 * n:]
        x, y, c = _peers()
        copies = []
        for j in range(1, N_CHIP):
            tx, ty = x ^ (j >> 1), y ^ (j & 1)
            for a in range(n):
                copies.append(pltpu.make_async_remote_copy(
                    src_ref=p_refs[a].at[2 * tx + ty], dst_ref=out_refs[a].at[j - 1], send_sem=send_sems.at[3 * a + j - 1],
                    recv_sem=recv_sems.at[3 * a + j - 1], device_id=(tx, ty, c), device_id_type=MESH))
        for cp in copies:
            cp.start()
        for cp in copies:
            cp.wait_recv()
        for cp in copies:
            cp.wait_send()

    return list(pl.pallas_call(
        body, name=name, out_shape=[_S((N_CHIP - 1,) + p.shape[1:], p.dtype) for p in parts], in_specs=[_ANY] * n, out_specs=[_ANY] * n,
        scratch_shapes=[pltpu.SemaphoreType.DMA((3 * n,)), pltpu.SemaphoreType.DMA((3 * n,))],
    )(*parts))


def _sum_slots(stack, *, name):
    n, R, W = stack.shape
    tr = _pick(R, 1024, align=8)

    def body(s_ref, o_ref):
        acc = s_ref[0]
        for k in range(1, n):
            acc = acc + s_ref[k]
        o_ref[...] = acc

    return pl.pallas_call(
        body, name=name, grid=(R // tr,), in_specs=[pl.BlockSpec((n, tr, W), lambda i: (0, i, 0))],
        out_specs=pl.BlockSpec((tr, W), lambda i: (i, 0)), out_shape=_S((R, W), f32), compiler_params=_cparams(("parallel",)),
    )(stack)


def _adamw_math(gv, wv, mv, vv):
    c1 = 1.0 / (1.0 - ADAM_B1 ** ADAM_STEP)
    c2 = 1.0 / (1.0 - ADAM_B2 ** ADAM_STEP)
    nm = ADAM_B1 * mv + (1.0 - ADAM_B1) * gv
    nv = ADAM_B2 * vv + (1.0 - ADAM_B2) * jnp.square(gv)
    return -ADAM_LR * ((nm * c1) / (jnp.sqrt(nv * c2) + ADAM_EPS) + ADAM_WD * wv), nm, nv


def _adamw(g, w, m, v, *, name):
    R, W = w.shape
    tr = _pick(R, max(8, (1 << 20) // (4 * W)), align=8)

    def body(g_ref, w_ref, m_ref, v_ref, d_ref, nm_ref, nv_ref):
        d_ref[...], nm_ref[...], nv_ref[...] = _adamw_math(g_ref[...], w_ref[...], m_ref[...], v_ref[...])

    sp = pl.BlockSpec((tr, W), lambda i: (i, 0))
    return pl.pallas_call(
        body, name=name, grid=(R // tr,), in_specs=[sp] * 4, out_specs=[sp] * 3, out_shape=[_S((R, W), f32)] * 3,
        compiler_params=_cparams(("parallel",)),
    )(g, w, m, v)


def _reduce_adamw(own, arrived, chip, w, m, v, *, name):
    n, R, W = arrived.shape
    tr = _pick(R, max(16, (1 << 20) // (4 * W)), align=16)

    def body(chip_ref, o_ref, p_ref, w_ref, m_ref, v_ref, g_ref, d_ref, nm_ref, nv_ref):
        gv = o_ref[...].astype(f32)
        for k in range(n):
            gv = gv + p_ref[k].astype(f32)
        g_ref[...] = gv
        d_ref[...], nm_ref[...], nv_ref[...] = _adamw_math(gv, w_ref[...], m_ref[...], v_ref[...])

    sp = pl.BlockSpec((tr, W), lambda i, chip_ref: (i, 0))
    return pl.pallas_call(
        body, name=name, out_shape=[_S((R, W), f32)] * 4,
        grid_spec=pltpu.PrefetchScalarGridSpec(
            num_scalar_prefetch=1, grid=(R // tr,),
            in_specs=[pl.BlockSpec((None, tr, W), lambda i, chip_ref: (chip_ref[0], i, 0)),
                      pl.BlockSpec((n, tr, W), lambda i, chip_ref: (0, i, 0))] + [sp] * 3, out_specs=[sp] * 4),
        compiler_params=_cparams(("parallel",)),
    )(chip, own, arrived, w, m, v)


def _pieces(seg_start, seg_len, shard_w):
    out, col = [], seg_start
    while col < seg_start + seg_len:
        k, a = divmod(col, shard_w)
        n = min(shard_w - a, seg_start + seg_len - col)
        out.append((k, a, col - seg_start, n))
        col += n
    return out


def _unshard_w_in(g, seg_lens, *, name):
    _, D, w = g.shape
    starts = [sum(seg_lens[:i]) for i in range(len(seg_lens))]
    widths = [max(n, LANES) for n in seg_lens]
    tm = _pick(D, 256, align=16)

    def body(g_ref, *o_refs):
        for o_ref, s0, n in zip(o_refs, starts, seg_lens):
            if n < o_ref.shape[1]:
                o_ref[...] = jnp.zeros_like(o_ref)
            for k, a, off, m in _pieces(s0, n, w):
                o_ref[:, off:off + m] = g_ref[k, :, a:a + m]

    return pl.pallas_call(
        body, name=name, grid=(D // tm,), in_specs=[pl.BlockSpec((N_DEV, tm, w), lambda i: (0, i, 0))],
        out_specs=[pl.BlockSpec((tm, wd), lambda i: (i, 0)) for wd in widths], out_shape=[_S((D, wd), g.dtype) for wd in widths],
        compiler_params=_cparams(("parallel",)),
    )(g)


def _unshard_pair(g1, g2, *, name):
    _, D, w = g1.shape
    tm = _pick(D, 256, align=16)

    def body(a_ref, b_ref, o_ref):
        for i, g_ref in enumerate((a_ref, b_ref)):
            for k in range(N_DEV):
                off = (i * N_DEV + k) * w
                o_ref[:, off:off + w] = g_ref[k]

    blk = pl.BlockSpec((N_DEV, tm, w), lambda i: (0, i, 0))
    return pl.pallas_call(
        body, name=name, grid=(D // tm,), in_specs=[blk, blk], out_specs=pl.BlockSpec((tm, 2 * N_DEV * w), lambda i: (i, 0)),
        out_shape=_S((D, 2 * N_DEV * w), g1.dtype), compiler_params=_cparams(("parallel",)),
    )(g1, g2)


def _reshard_pair(dw, *, name):
    D, w = dw.shape[0], dw.shape[1] // (2 * N_DEV)
    tm = _pick(D, 128, align=16)

    def body(g_ref, a_ref, b_ref):
        for i, o_ref in enumerate((a_ref, b_ref)):
            for k in range(N_DEV):
                off = (i * N_DEV + k) * w
                o_ref[_slot_of(k)] = g_ref[:, off:off + w].astype(o_ref.dtype)

    blk = pl.BlockSpec((N_DEV, tm, w), lambda i: (0, i, 0))
    return pl.pallas_call(
        body, name=name, grid=(D // tm,), in_specs=[pl.BlockSpec((tm, dw.shape[1]), lambda i: (i, 0))], out_specs=[blk, blk],
        out_shape=[_S((N_DEV, D, w), bf16)] * 2, compiler_params=_cparams(("parallel",)),
    )(dw)


def _reshard_w_in(grads, seg_lens, w, *, name):
    D = grads[0].shape[0]
    starts = [sum(seg_lens[:i]) for i in range(len(seg_lens))]
    tm = _pick(D, 128, align=16)

    def body(*refs):
        o_ref = refs[-1]
        for g_ref, s0, n in zip(refs[:-1], starts, seg_lens):
            for k, a, off, m in _pieces(s0, n, w):
                o_ref[_slot_of(k), :, a:a + m] = g_ref[:, off:off + m].astype(o_ref.dtype)

    return pl.pallas_call(
        body, name=name, grid=(D // tm,), in_specs=[pl.BlockSpec((tm, g.shape[1]), lambda i: (i, 0)) for g in grads],
        out_specs=pl.BlockSpec((N_DEV, tm, w), lambda i: (0, i, 0)), out_shape=_S((N_DEV, D, w), bf16),
        compiler_params=_cparams(("parallel",)),
    )(*grads)


def _pad_flat(a, mult):
    a = a.reshape(-1)
    n = -(-a.shape[0] // mult) * mult
    return a if n == a.shape[0] else jnp.pad(a, (0, n - a.shape[0]))


def _pad_cols(a, mult):
    n = -(-a.shape[1] // mult) * mult
    return a if n == a.shape[1] else jnp.pad(a, ((0, 0), (0, n - a.shape[1])))


def _block_diag(t):
    nblk, g, P, Q = t.shape
    eye = jnp.eye(g, dtype=t.dtype)
    return (t[:, :, :, None, :] * eye[None, :, None, :, None]).reshape(nblk, g * P, g * Q)


def _block_diag_t(w, P, Q):
    nblk = w.shape[0]
    g = w.shape[1] // P
    eye = jnp.eye(g, dtype=w.dtype)
    return (w.reshape(nblk, g, P, g, Q) * eye[None, :, None, :, None]).sum(axis=3)


_COLS = ("ffn1_w_gate", "ffn1_w_up", "ffn2_w_gate", "ffn2_w_up")
_ROWS = ("ffn1_w_down", "ffn2_w_down", "s5_w_glu", "w_proj_s5", "w_out", "w_proj_ssd")
_BIG = _COLS + _ROWS + ("w_in", "conv_w")
_SMALL = ("ffn1_norm", "mix_norm", "conv_b", "s5_A_re", "s5_A_im", "s5_log_dt", "s5_B_re", "s5_B_im", "s5_C_re", "s5_C_im",
          "s5_D", "s5_b_glu", "ssd_A_log", "ssd_dt_bias", "ssd_D", "ssd_norm", "b_gate", "ffn2_norm", "final_norm")
_WEIGHTS = ("ffn1_norm", "ffn1_w_gate", "ffn1_w_up", "ffn1_w_down", "mix_norm", "w_in", "conv_w", "conv_b", "s5_A_re", "s5_A_im",
            "s5_log_dt", "s5_B_re", "s5_B_im", "s5_C_re", "s5_C_im", "s5_D", "s5_w_glu", "s5_b_glu", "ssd_A_log", "ssd_dt_bias",
            "ssd_D", "ssd_norm", "w_proj_s5", "w_proj_ssd", "b_gate", "w_out", "ffn2_norm", "ffn2_w_gate", "ffn2_w_up",
            "ffn2_w_down", "final_norm")
def _ffn_fwd(x, n, wgu, wd, tag):
    D = x.shape[1]
    F = wd.shape[0]
    h = _rows(_f_rmsnorm, [x], [n], [(D, bf16)], name=tag + "_norm")[0]
    ab = _mm(h, wgu, name=tag + "_gate_up")
    c = _rows(_f_swiglu, [ab], [], [(F, bf16)], name=tag + "_act")[0]
    y = _mm(c, wd, scale=0.5, add=x, name=tag + "_down")
    return y, (x, n, h, ab, c)


def _ffn_bwd(saved, wgu, wd, dy, tag):
    x, n, h, ab, c = saved
    F = wd.shape[0]
    dc = _mm(dy, wd, tb=True, scale=0.5, name=tag + "_d_act")
    dwd = _mm(c, dy, ta=True, o_blk="m", o_slots=True, tm=F // 2, out_dtype=bf16, scale=0.5, name=tag + "_d_wdown")
    (dab,), _ = _rows_bwd(_f_swiglu, [ab], [], [dc], name=tag + "_act_bwd", want_rows=[0], row_dtypes={0: bf16})
    dh = _mm(dab, wgu, tb=True, name=tag + "_d_h")
    dwg, dwu = _reshard_pair(_mm(h, dab, ta=True, name=tag + "_d_wgu"), name=tag + "_reshard_d_wgu")
    (dx,), (dn,) = _rows_bwd(_f_rmsnorm, [x], [n], [dh], name=tag + "_norm_bwd", want_rows=[0], adds={0: dy})
    return dx, dn, dwg, dwu, dwd


def kernel(x, ffn1_norm, ffn1_w_gate, ffn1_w_up, ffn1_w_down, mix_norm, w_in, conv_w, conv_b, s5_A_re, s5_A_im, s5_log_dt, s5_B_re, s5_B_im, s5_C_re, s5_C_im, s5_D, s5_w_glu, s5_b_glu, ssd_A_log, ssd_dt_bias, ssd_D, ssd_norm, w_proj_s5, w_proj_ssd, b_gate, w_out, ffn2_norm, ffn2_w_gate, ffn2_w_up, ffn2_w_down, final_norm, loss_target, m_ffn1_norm, m_ffn1_w_gate, m_ffn1_w_up, m_ffn1_w_down, m_mix_norm, m_w_in, m_conv_w, m_conv_b, m_s5_A_re, m_s5_A_im, m_s5_log_dt, m_s5_B_re, m_s5_B_im, m_s5_C_re, m_s5_C_im, m_s5_D, m_s5_w_glu, m_s5_b_glu, m_ssd_A_log, m_ssd_dt_bias, m_ssd_D, m_ssd_norm, m_w_proj_s5, m_w_proj_ssd, m_b_gate, m_w_out, m_ffn2_norm, m_ffn2_w_gate, m_ffn2_w_up, m_ffn2_w_down, m_final_norm, v_ffn1_norm, v_ffn1_w_gate, v_ffn1_w_up, v_ffn1_w_down, v_mix_norm, v_w_in, v_conv_w, v_conv_b, v_s5_A_re, v_s5_A_im, v_s5_log_dt, v_s5_B_re, v_s5_B_im, v_s5_C_re, v_s5_C_im, v_s5_D, v_s5_w_glu, v_s5_b_glu, v_ssd_A_log, v_ssd_dt_bias, v_ssd_D, v_ssd_norm, v_w_proj_s5, v_w_proj_ssd, v_b_gate, v_w_out, v_ffn2_norm, v_ffn2_w_gate, v_ffn2_w_up, v_ffn2_w_down, v_final_norm):
    P = dict(locals())
    T, D = x.shape[1], x.shape[2]
    x0, tgt = x[0], loss_target[0]
    sh = {k: P[k][0] for k in _BIG}

    gathered = _all_gather([sh[k] if k == "conv_w" else sh[k].astype(bf16) for k in _BIG], name="gather_weights")
    W = dict(zip(_BIG, gathered))
    whole = lambda k: W[k].reshape(-1, D)
    g_win = W["w_in"]
    conv_w_full = W["conv_w"].transpose(1, 0, 2).reshape(CONV_K, -1)

    d_inner = N_DEV * sh["w_proj_ssd"].shape[0]
    conv_dim = conv_w_full.shape[1]
    H = ssd_A_log.shape[1]
    G = (conv_dim - d_inner) // (2 * SSD_STATE)
    hpg = H // G
    nc = T // CHUNK
    Gs = D // S5_GROUP
    nblk = Gs // S5_GPB
    NS = Gs * S5_STATE
    seg_lens = (D, d_inner, conv_dim, H, 2 * D)
    w_u, w_z, w_xbc, w_dt, w_gl = _unshard_w_in(g_win, seg_lens, name="unshard_w_in")
    w_glu, w_p5, w_pssd, w_o = whole("s5_w_glu"), whole("w_proj_s5"), whole("w_proj_ssd"), whole("w_out")

    ffn1_w = (_unshard_pair(W["ffn1_w_gate"], W["ffn1_w_up"], name="unshard_ffn1_gate_up"), whole("ffn1_w_down"))
    ffn2_w = (_unshard_pair(W["ffn2_w_gate"], W["ffn2_w_up"], name="unshard_ffn2_gate_up"), whole("ffn2_w_down"))
    x1, sv1 = _ffn_fwd(x0, ffn1_norm, *ffn1_w, "ffn1")
    h2 = _rows(_f_rmsnorm, [x1], [mix_norm], [(D, bf16)], name="mix_norm")[0]
    u_p = _mm(h2, w_u, o_seg=True, name="in_u")
    z = _mm(h2, w_z, name="in_z")
    xbc = _mm(h2, w_xbc, name="in_xbc")
    gl = _mm(h2, w_gl, name="in_gate")
    dtr = _mm(h2, w_dt, name="in_dt")

    rep = lambda a: jnp.repeat(a, S5_GROUP, axis=0)
    lr, li, ldt = s5_A_re[0], s5_A_im[0], s5_log_dt[0].reshape(Gs, 1)
    brt = s5_B_re[0].transpose(0, 2, 1).reshape(Gs * S5_GROUP, S5_STATE)
    bit = s5_B_im[0].transpose(0, 2, 1).reshape(Gs * S5_GROUP, S5_STATE)
    prep_args = (lr, li, ldt, rep(lr), rep(li), rep(ldt), brt, bit)
    ar, ai, bbrt, bbit = _s5_prep(prep_args, name="s5_prep")
    a_r, a_i = ar.reshape(1, NS), ai.reshape(1, NS)
    wb_r = _block_diag(bbrt.reshape(nblk, S5_GPB, S5_GROUP, S5_STATE)).astype(bf16)
    wb_i = _block_diag(bbit.reshape(nblk, S5_GPB, S5_GROUP, S5_STATE)).astype(bf16)
    c4r = s5_C_re[0].reshape(nblk, S5_GPB, S5_GROUP, S5_STATE).transpose(0, 1, 3, 2)
    c4i = s5_C_im[0].reshape(nblk, S5_GPB, S5_GROUP, S5_STATE).transpose(0, 1, 3, 2)
    wc_r, wc_i = _block_diag(c4r).astype(bf16), _block_diag(c4i).astype(bf16)
    sl_r, sl_i, p_r, p_i = _s5_local_scan(u_p, wb_r, wb_i, a_r, a_i, reverse=False, name="s5_scan")
    c_r, c_i = _s5_carry(sl_r[T - NSEG:], sl_i[T - NSEG:], p_r, p_i, reverse=False, name="s5_carry")
    s_r, s_i, ylin = _s5_fix_out(sl_r, sl_i, a_r, a_i, c_r, c_i, wc_r, wc_i, name="s5_fix_out")
    g5 = _rows(_f_s5_post, [ylin, u_p], [s5_D], [(D, f32)], name="s5_gelu")[0]
    v5 = _mm(g5, w_glu, name="s5_glu_mm")
    o5 = _rows(_f_glu, [g5, v5], [s5_b_glu], [(D, bf16)], name="s5_glu")[0]
    p5 = _mm(o5, w_p5, a_seg=True, name="proj_s5")

    xc = _conv_fwd(xbc, conv_w_full, conv_b, name="conv")
    bias_p, alog_p = _pad_cols(ssd_dt_bias, LANES), _pad_cols(ssd_A_log, LANES)
    dt_p, da_p = _rows(_f_dt, [dtr], [bias_p, alog_p], [(LANES, f32), (LANES, f32)], name="ssd_dt")
    col_l = lambda a: jnp.repeat(a[:, :H], HEADDIM, axis=1)
    row_l = lambda a: a[:, :H].reshape(nc, CHUNK, G, hpg).transpose(2, 0, 3, 1).reshape(G, nc, hpg * CHUNK)
    ssd_in = (xc, col_l(dt_p), col_l(da_p), row_l(dt_p), row_l(da_p), jnp.repeat(ssd_D, HEADDIM, axis=1), _ssd_consts(hpg))
    y_ssd, hs = _ssd2_fwd(*ssd_in, d_inner=d_inner, name="ssd")
    yn = _rows(_f_gated_norm, [y_ssd, z], [ssd_norm], [(d_inner, bf16)], name="ssd_gated_norm")[0]
    pssd = _mm(yn, w_pssd, name="proj_ssd")

    merged = _rows(_f_merge, [gl, p5, pssd], [b_gate], [(D, bf16)], name="merge")[0]
    x2 = _mm(merged, w_o, add=x1, name="out_proj")
    x3, sv2 = _ffn_fwd(x2, ffn2_norm, *ffn2_w, "ffn2")
    lossv, dx3, d_final = _loss_stage(x3, tgt, final_norm.reshape(1, D), name="loss")

    gw = {}
    gs = {"final_norm": d_final}
    slot_mm = lambda a_, b_, name, **kw: _mm(a_, b_, ta=True, o_blk="m", o_slots=True, out_dtype=bf16, name=name, **kw)
    dx2, gs["ffn2_norm"], gw["ffn2_w_gate"], gw["ffn2_w_up"], gw["ffn2_w_down"] = _ffn_bwd(sv2, *ffn2_w, dx3, "ffn2")

    dmerged = _mm(dx2, w_o, tb=True, name="d_merged")
    gw["w_out"] = slot_mm(merged, dx2, "d_w_out")
    (dgl, dp5, dpssd), (gs["b_gate"],) = _rows_bwd(_f_merge, [gl, p5, pssd], [b_gate], [dmerged], name="merge_bwd", want_rows=[0, 1, 2])

    dyn = _mm(dpssd, w_pssd, tb=True, name="d_yn")
    gw["w_proj_ssd"] = slot_mm(yn, dpssd, "d_w_proj_ssd")
    (dyssd, dz), (gs["ssd_norm"],) = _rows_bwd(_f_gated_norm, [y_ssd, z], [ssd_norm], [dyn], name="ssd_gated_norm_bwd", want_rows=[0, 1])
    dxs, dbm, dcm, ddtc, ddac, ddtw, ddaw, ddh = _ssd2_bwd(*ssd_in, hs, dyssd, d_inner=d_inner, name="ssd_bwd")

    def fold(col, row):
        col = col.reshape(T, G, LANES)[:, :, :hpg].reshape(T, H)
        row = row.reshape(G, nc, hpg, CHUNK).transpose(1, 3, 0, 2).reshape(T, H)
        return _pad_cols(col + row, LANES)

    (ddtr,), (dbias_p, dalog_p) = _rows_bwd(_f_dt, [dtr], [bias_p, alog_p], [fold(ddtc, ddtw), fold(ddac, ddaw)], name="ssd_dt_bwd", want_rows=[0])
    gs["ssd_dt_bias"], gs["ssd_A_log"], gs["ssd_D"] = dbias_p[:, :H], dalog_p[:, :H], ddh[:, 0, :hpg].reshape(1, H)
    dxbc, d_conv_w, gs["conv_b"] = _conv_bwd(xbc, conv_w_full, conv_b, jnp.concatenate([dxs, dbm, dcm], axis=1), name="conv_bwd")
    cwk = sh["conv_w"].shape[1]
    gw["conv_w"] = d_conv_w.reshape(CONV_K, N_CHIP, 2, cwk).transpose(2, 1, 0, 3).reshape(N_DEV, CONV_K, cwk)

    do5 = _mm(dp5, w_p5, tb=True, o_seg=True, name="d_o5")
    gw["w_proj_s5"] = slot_mm(o5, dp5, "d_w_proj_s5", a_seg=True)
    (dg5a, dv5), (gs["s5_b_glu"],) = _rows_bwd(_f_glu, [g5, v5], [s5_b_glu], [do5], name="s5_glu_bwd", want_rows=[0, 1])
    dg5 = _mm(dv5, w_glu, tb=True, add=dg5a, name="d_g5")
    gw["s5_w_glu"] = slot_mm(g5, dv5, "d_w_glu")
    (dylin, du_a), (gs["s5_D"],) = _rows_bwd(_f_s5_post, [ylin, u_p], [s5_D], [dg5], name="s5_gelu_bwd", want_rows=[0, 1])
    wct_r, wct_i = wc_r.transpose(0, 2, 1), -wc_i.transpose(0, 2, 1)
    ql_r, ql_i, pb_r, pb_i = _s5_local_scan(dylin, wct_r, wct_i, a_r, -a_i, reverse=True, name="s5_scan_bwd")
    cb_r, cb_i = _s5_carry(ql_r[:NSEG], ql_i[:NSEG], pb_r, pb_i, reverse=True, name="s5_carry_bwd")
    tc = min(S5_TC, T)

    def before_blocks(s):
        last = s.reshape(T // tc, tc, NS)[:, tc - NSEG:, :]
        wrap = jnp.concatenate([jnp.zeros((1, 1, NS), f32), last[-1:, : NSEG - 1, :]], axis=1)
        return jnp.concatenate([wrap, last[:-1]], axis=0)

    du_p, dwb_r, dwb_i, dwc_r, dwc_i, d_ar, d_ai = _s5_fix_bwd(
        ql_r, ql_i, a_r, -a_i, cb_r, cb_i, s_r, s_i, before_blocks(s_r), before_blocks(s_i), u_p, dylin, du_a, wb_r, wb_i, name="s5_fix_bwd")
    unblk = lambda w: _block_diag_t(w, S5_GROUP, S5_STATE).reshape(Gs * S5_GROUP, S5_STATE)
    rsum = jnp.repeat(jnp.eye(Gs, dtype=f32), S5_GROUP, axis=1)
    d_lr, d_li, d_ldt, d_brt, d_bit = _s5_prep_bwd(
        prep_args, (d_ar.reshape(Gs, S5_STATE), d_ai.reshape(Gs, S5_STATE), unblk(dwb_r), unblk(dwb_i)), rsum, name="s5_prep_bwd")
    gs["s5_A_re"], gs["s5_A_im"], gs["s5_log_dt"] = d_lr, d_li, d_ldt.reshape(1, Gs)
    gs["s5_B_re"] = d_brt.reshape(Gs, S5_GROUP, S5_STATE).transpose(0, 2, 1)
    gs["s5_B_im"] = d_bit.reshape(Gs, S5_GROUP, S5_STATE).transpose(0, 2, 1)
    gs["s5_C_re"] = _block_diag_t(dwc_r, S5_STATE, S5_GROUP).transpose(0, 1, 3, 2).reshape(Gs, S5_GROUP, S5_STATE)
    gs["s5_C_im"] = _block_diag_t(dwc_i, S5_STATE, S5_GROUP).transpose(0, 1, 3, 2).reshape(Gs, S5_GROUP, S5_STATE)

    dh2 = _mm(du_p, w_u, tb=True, a_seg=True, name="d_h2_u")
    dh2 = _mm(dz, w_z, tb=True, add=dh2, name="d_h2_z")
    dh2 = _mm(dxbc, w_xbc, tb=True, add=dh2, name="d_h2_xbc")
    dh2 = _mm(dgl, w_gl, tb=True, add=dh2, name="d_h2_gate")
    dh2 = _mm(ddtr, w_dt, tb=True, add=dh2, name="d_h2_dt")
    d_w_in = [_mm(h2, du_p, ta=True, b_seg=True, name="d_w_u"), _mm(h2, dz, ta=True, name="d_w_z"), _mm(h2, dxbc, ta=True, name="d_w_xbc"),
              _mm(h2, ddtr, ta=True, name="d_w_dt"), _mm(h2, dgl, ta=True, name="d_w_gate")]
    gw["w_in"] = _reshard_w_in(d_w_in, seg_lens, sh["w_in"].shape[1], name="reshard_d_w_in")
    (dx1,), (gs["mix_norm"],) = _rows_bwd(_f_rmsnorm, [x1], [mix_norm], [dh2], name="mix_norm_bwd", want_rows=[0], adds={0: dx2})
    dx0, gs["ffn1_norm"], gw["ffn1_w_gate"], gw["ffn1_w_up"], gw["ffn1_w_down"] = _ffn_bwd(sv1, *ffn1_w, dx1, "ffn1")

    core = lax.axis_index("c").astype(jnp.int32).reshape(1)
    chip = (2 * lax.axis_index("x") + lax.axis_index("y")).astype(jnp.int32).reshape(1)
    sib = _exchange_sibling([gw[k] for k in _BIG], name="exchange_grads_sibling")
    chip_sums = {k: _chip_sum(gw[k], s_, core, name="chip_sum_" + k) for k, s_ in zip(_BIG, sib)}
    arrived = dict(zip(_BIG, _exchange_chips([chip_sums[k] for k in _BIG], name="exchange_grads_chips")))
    small_shapes = {k: (P[k][0].shape if P[k].ndim > 1 else P[k].shape) for k in _SMALL}
    pack = lambda d: jnp.concatenate([_pad_flat(d[k], TILE_ELEMS) for k in _SMALL]).reshape(-1, LANES)
    gsmall = _sum_slots(_all_gather([pack(gs)], name="gather_small_grads")[0], name="sum_small_grads")
    snum = {k: math.prod(small_shapes[k]) for k in _SMALL}
    ssz = {k: -(-snum[k] // TILE_ELEMS) * TILE_ELEMS for k in _SMALL}

    grads, delta, new_m, new_v = {}, {}, {}, {}
    for k in _BIG:
        grads[k], delta[k], new_m[k], new_v[k] = _reduce_adamw(
            chip_sums[k], arrived[k], chip, P[k][0], P["m_" + k][0], P["v_" + k][0], name="adamw_" + k)
    d_s, m_s, v_s = _adamw(gsmall, pack({k: P[k] for k in _SMALL}), pack({k: P["m_" + k] for k in _SMALL}),
                           pack({k: P["v_" + k] for k in _SMALL}), name="adamw_small")
    off = 0
    gflat, dflat, mflat, vflat = gsmall.reshape(-1), d_s.reshape(-1), m_s.reshape(-1), v_s.reshape(-1)
    for k in _SMALL:
        n = snum[k]
        grads[k], delta[k], new_m[k], new_v[k] = (a[off:off + n] for a in (gflat, dflat, mflat, vflat))
        off += ssz[k]

    loss = lax.psum(lossv[0, 0], ("x", "y", "c"))
    out = [loss, dx0.reshape(x.shape)]
    for d in (grads, delta, new_m, new_v):
        out += [d[k].reshape(P[k].shape) for k in _WEIGHTS]
    return tuple(out)
```

```python
import math

import jax
import jax.numpy as jnp
from jax import lax
from jax.experimental import pallas as pl
from jax.experimental.pallas import tpu as pltpu

f32 = jnp.float32
bf16 = jnp.bfloat16
_S = jax.ShapeDtypeStruct

EPS = 1e-6
S5_GROUP = 16
S5_STATE = 64
HEADDIM = 64
SSD_STATE = 128
CHUNK = 64
CONV_K = 4
NSEG = 8
S5_GPB = 16
N_DEV = 8
LANES = 128
TILE_ELEMS = 8 * LANES

ADAM_LR = 0.001
ADAM_B1 = 0.9
ADAM_B2 = 0.999
ADAM_EPS = 1e-08
ADAM_WD = 0.01
ADAM_STEP = 10

VMEM_LIMIT = 56 * 1024 * 1024
MM_FULL_K = 3072
MM_MAX_TN = 3072
MESH = pl.DeviceIdType.MESH


def _cparams(sem=None):
    return pltpu.CompilerParams(dimension_semantics=sem, vmem_limit_bytes=VMEM_LIMIT)


def _pick(dim, pref, align=LANES):
    best = None
    t = align
    while t <= min(dim, pref):
        if dim % t == 0:
            best = t
        t += align
    return best or dim


def _slot_of(k):
    return (k & 1) * (N_DEV // 2) + (k >> 1)


def _mm(a, b, *, name, ta=False, tb=False, a_blk=None, b_blk=None, o_blk=None, o_slots=False, a_seg=False, b_seg=False,
        o_seg=False, tm=None, out_dtype=f32, scale=1.0, add=None, carry=()):
    a2, b2 = a.shape[-2:], b.shape[-2:]
    Ma, Ka = (a2[1], a2[0]) if ta else a2
    Kb, Nb = (b2[1], b2[0]) if tb else b2
    M = Ma * (a.shape[0] if a_blk == "m" else 1)
    K = Ka * (a.shape[0] if a_blk == "k" else 1)
    N = Nb * (b.shape[0] if b_blk == "n" else 1)
    assert K == Kb * (b.shape[0] if b_blk == "k" else 1), (a.shape, b.shape, ta, tb, a_blk, b_blk)
    assert (a.ndim == 3) == (a_blk is not None) and (b.ndim == 3) == (b_blk is not None)
    tm = Ma if a_blk == "m" else (tm or _pick(M, 512))
    tn = Nb if b_blk == "n" else _pick(N, MM_MAX_TN)
    if a_blk == "k" or b_blk == "k":
        tk = Ka if a_blk == "k" else Kb
        assert tk == (Kb if b_blk == "k" else tk)
    else:
        tk = K if K <= MM_FULL_K else _pick(K, 1024 if ta else MM_FULL_K)
    if (a_seg and not ta) or o_seg:
        tm = M // NSEG
    if (a_seg and ta) or b_seg:
        tk = K // NSEG
    gm, gn, nk = M // tm, N // tn, K // tk
    assert not (add is not None and (o_seg or o_blk)) and not (o_blk and o_seg)

    if a_seg:
        assert a.ndim == 2
        a = a.reshape(a.shape[0] // NSEG, NSEG * a.shape[1])
        if ta:
            a_spec = pl.BlockSpec((tk, tm), lambda i, j, k: (0, k * (Ma // tm) + i))
        else:
            a_spec = pl.BlockSpec((tm, tk), lambda i, j, k: (0, i * (Ka // tk) + k))
    elif a.ndim == 3:
        lead = (lambda i, k: i) if a_blk == "m" else (lambda i, k: k)
        if ta:
            a_spec = pl.BlockSpec((None, tk, tm), lambda i, j, k: (lead(i, k), 0 if a_blk == "k" else k, 0 if a_blk == "m" else i))
        else:
            a_spec = pl.BlockSpec((None, tm, tk), lambda i, j, k: (lead(i, k), 0 if a_blk == "m" else i, 0 if a_blk == "k" else k))
    else:
        a_spec = pl.BlockSpec((tk, tm), lambda i, j, k: (k, i)) if ta else pl.BlockSpec((tm, tk), lambda i, j, k: (i, k))
    if b_seg:
        assert b.ndim == 2 and not tb
        b = b.reshape(b.shape[0] // NSEG, NSEG * b.shape[1])
        b_spec = pl.BlockSpec((tk, tn), lambda i, j, k: (0, k * (Nb // tn) + j))
    elif b.ndim == 3:
        lead = (lambda j, k: j) if b_blk == "n" else (lambda j, k: k)
        if tb:
            b_spec = pl.BlockSpec((None, tn, tk), lambda i, j, k: (lead(j, k), 0 if b_blk == "n" else j, 0 if b_blk == "k" else k))
        else:
            b_spec = pl.BlockSpec((None, tk, tn), lambda i, j, k: (lead(j, k), 0 if b_blk == "k" else k, 0 if b_blk == "n" else j))
    else:
        b_spec = pl.BlockSpec((tn, tk), lambda i, j, k: (j, k)) if tb else pl.BlockSpec((tk, tn), lambda i, j, k: (k, j))
    slot = _slot_of if o_slots else (lambda k: k)
    if o_blk == "n":
        assert gn == N_DEV or not o_slots
        o_shape, o_spec = (gn, M, tn), pl.BlockSpec((None, tm, tn), lambda i, j, k: (slot(j), i, 0))
    elif o_blk == "m" and o_slots and gm < N_DEV:
        rs = M // N_DEV
        per_tile = tm // rs
        assert per_tile % 2 == 0 and tm % rs == 0
        o_shape = (2, N_CHIP, rs, N)
        o_spec = pl.BlockSpec((2, per_tile // 2, rs, tn), lambda i, j, k: (0, i, 0, j))
    elif o_blk == "m":
        assert gm == N_DEV or not o_slots
        o_shape, o_spec = (gm, tm, N), pl.BlockSpec((None, tm, tn), lambda i, j, k: (slot(i), 0, j))
    elif o_seg:
        o_shape, o_spec = (tm, NSEG * N), pl.BlockSpec((tm, tn), lambda i, j, k: (0, i * (N // tn) + j))
    else:
        o_shape, o_spec = (M, N), pl.BlockSpec((tm, tn), lambda i, j, k: (i, j))
    dims = (((0 if ta else 1,), (1 if tb else 0,)), ((), ()))
    has_add = add is not None

    nc_ = len(carry)
    n_in = 2 + has_add

    def body(*refs):
        a_ref, b_ref = refs[0], refs[1]
        add_ref = refs[2] if has_add else None
        p_refs, o_ref, arr_refs = refs[n_in:n_in + nc_], refs[n_in + nc_], refs[n_in + nc_ + 1:n_in + 2 * nc_ + 1]
        acc_ref, sems = refs[n_in + 2 * nc_ + 1], refs[n_in + 2 * nc_ + 2:]
        i, j, k = pl.program_id(0), pl.program_id(1), pl.program_id(2)
        _carry_start(p_refs, arr_refs, sems, (i == 0) & (j == 0) & (k == 0))

        @pl.when(k == 0)
        def _():
            acc_ref[...] = jnp.zeros_like(acc_ref)

        acc_ref[...] += lax.dot_general(a_ref[...].astype(bf16), b_ref[...].astype(bf16), dims, preferred_element_type=f32)

        @pl.when(k == nk - 1)
        def _():
            r = acc_ref[...] * scale
            if has_add:
                r = r + add_ref[...].astype(f32)
            if len(o_shape) == 4:
                rs = o_shape[2]
                for chip_l in range(o_ref.shape[1]):
                    for core in range(2):
                        dev = 2 * chip_l + core
                        o_ref[core, chip_l] = r[dev * rs:(dev + 1) * rs].astype(out_dtype)
            else:
                o_ref[...] = r.astype(out_dtype)

        _carry_wait(p_refs, arr_refs, sems, (i == gm - 1) & (j == gn - 1) & (k == nk - 1))

    ins = [a, b] + ([add] if has_add else []) + list(carry)
    in_specs = [a_spec, b_spec] + ([o_spec] if has_add else []) + [_ANY] * nc_
    res = pl.pallas_call(
        body, name=name, grid=(gm, gn, nk), in_specs=in_specs, out_specs=[o_spec] + [_ANY] * nc_,
        out_shape=[_S(o_shape, out_dtype)] + (_chip_out_shapes(carry) if nc_ else []),
        scratch_shapes=[pltpu.VMEM((tm, tn), f32)] + (_chip_sems(nc_) if nc_ else []),
        compiler_params=_cparams(("arbitrary",) * 3 if nc_ else ("parallel", "parallel", "arbitrary")),
    )(*ins)
    out = res[0]
    if len(o_shape) == 4:
        out = out.reshape(N_DEV, o_shape[2], N)
    elif o_seg:
        out = out.reshape(M, N)
    return (out, list(res[1:])) if nc_ else out


def _row_tile(T, widths):
    budget = 6 * 1024 * 1024
    tb = max(16, budget // (4 * sum(widths)))
    return _pick(T, tb, align=16)


def _rows(fn, rows, params, outs, *, name):
    T = rows[0].shape[0]
    nr, npar = len(rows), len(params)
    tb = _row_tile(T, [r.shape[1] for r in rows] + [w for w, _ in outs])

    def body(*refs):
        ins = [r[...].astype(f32) for r in refs[: nr + npar]]
        res = fn(*ins)
        for o_ref, r in zip(refs[nr + npar:], res):
            o_ref[...] = r.astype(o_ref.dtype)

    in_specs = [pl.BlockSpec((tb, r.shape[1]), lambda i: (i, 0)) for r in rows]
    in_specs += [pl.BlockSpec(p.shape, lambda i: (0, 0)) for p in params]
    out_specs = [pl.BlockSpec((tb, w), lambda i: (i, 0)) for w, _ in outs]
    res = pl.pallas_call(
        body, name=name, grid=(T // tb,), in_specs=in_specs, out_specs=out_specs,
        out_shape=[_S((T, w), d) for w, d in outs], compiler_params=_cparams(("parallel",)),
    )(*rows, *params)
    return tuple(res)


def _rows_bwd(fn, rows, params, cots, *, name, want_rows, row_dtypes=None, adds=None):
    T = rows[0].shape[0]
    nr, npar, nc = len(rows), len(params), len(cots)
    adds = adds or {}
    add_idx = sorted(adds)
    row_dtypes = row_dtypes or {}
    widths = [r.shape[1] for r in rows] + [c.shape[1] for c in cots] + [rows[i].shape[1] for i in want_rows]
    tb = _row_tile(T, widths)

    def body(*refs):
        ins = [r[...].astype(f32) for r in refs[: nr + npar]]
        cot = tuple(r[...].astype(f32) for r in refs[nr + npar: nr + npar + nc])
        add_refs = refs[nr + npar + nc: nr + npar + nc + len(add_idx)]
        out_refs = refs[nr + npar + nc + len(add_idx):]
        _, vjp = jax.vjp(lambda *a: tuple(fn(*a)), *ins)
        g = vjp(cot)
        for o_ref, i in zip(out_refs[: len(want_rows)], want_rows):
            r = g[i]
            if i in adds:
                r = r + add_refs[add_idx.index(i)][...].astype(f32)
            o_ref[...] = r.astype(o_ref.dtype)
        first = pl.program_id(0) == 0
        for o_ref, gp in zip(out_refs[len(want_rows):], g[nr:]):
            @pl.when(first)
            def _(o_ref=o_ref):
                o_ref[...] = jnp.zeros_like(o_ref)

            o_ref[...] += gp

    in_specs = [pl.BlockSpec((tb, r.shape[1]), lambda i: (i, 0)) for r in rows]
    in_specs += [pl.BlockSpec(p.shape, lambda i: (0, 0)) for p in params]
    in_specs += [pl.BlockSpec((tb, c.shape[1]), lambda i: (i, 0)) for c in cots]
    in_specs += [pl.BlockSpec((tb, adds[i].shape[1]), lambda i_: (i_, 0)) for i in add_idx]
    out_specs = [pl.BlockSpec((tb, rows[i].shape[1]), lambda i_: (i_, 0)) for i in want_rows]
    out_specs += [pl.BlockSpec(p.shape, lambda i: (0, 0)) for p in params]
    out_shape = [_S(rows[i].shape, row_dtypes.get(i, f32)) for i in want_rows] + [_S(p.shape, f32) for p in params]
    res = pl.pallas_call(
        body, name=name, grid=(T // tb,), in_specs=in_specs, out_specs=out_specs, out_shape=out_shape,
        compiler_params=_cparams(("arbitrary",)),
    )(*rows, *params, *cots, *[adds[i] for i in add_idx])
    return list(res[: len(want_rows)]), list(res[len(want_rows):])


def _f_rmsnorm(x, g):
    return (x * lax.rsqrt(jnp.mean(x * x, axis=-1, keepdims=True) + EPS) * g,)


def _f_swiglu(ab):
    F = ab.shape[1] // 2
    return (jax.nn.silu(ab[:, :F]) * ab[:, F:],)


def _f_s5_post(y, u, d):
    return (jax.nn.gelu(y + d * u),)


def _f_glu(g, v, b):
    return (g * jax.nn.sigmoid(v + b),)


def _f_gated_norm(y, z, w):
    return _f_rmsnorm(y * jax.nn.silu(z), w)


def _f_merge(gl, p5, pssd, b):
    D = p5.shape[1]
    gates = jax.nn.sigmoid(gl + b)
    return (gates[:, :D] * p5 + gates[:, D:] * pssd,)


def _f_dt(dtr, bias, a_log):
    dt = jax.nn.softplus(dtr + bias)
    return dt, dt * (-jnp.exp(a_log))


def _loss_stage(x, tgt, g, *, name):
    T, D = x.shape
    tb = _row_tile(T, [D, D, D])

    def f(xb, gb, tb_):
        y = _f_rmsnorm(xb, gb)[0]
        return 0.5 * jnp.sum(jnp.mean(jnp.square(y - tb_), axis=-1, keepdims=True), axis=0, keepdims=True)

    def body(x_ref, t_ref, g_ref, l_ref, dx_ref, dg_ref):
        tv = t_ref[...]
        val, vjp = jax.vjp(lambda a, b: f(a, b, tv), x_ref[...], g_ref[...])
        dx, dg = vjp(jnp.ones((1, 1), f32))
        dx_ref[...] = dx

        @pl.when(pl.program_id(0) == 0)
        def _():
            l_ref[...] = jnp.zeros_like(l_ref)
            dg_ref[...] = jnp.zeros_like(dg_ref)

        l_ref[...] += jnp.broadcast_to(val, l_ref.shape)
        dg_ref[...] += dg

    row = pl.BlockSpec((tb, D), lambda i: (i, 0))
    par = pl.BlockSpec((1, D), lambda i: (0, 0))
    return pl.pallas_call(
        body, name=name, grid=(T // tb,), in_specs=[row, row, par],
        out_specs=[pl.BlockSpec((1, LANES), lambda i: (0, 0)), row, par],
        out_shape=[_S((1, LANES), f32), _S((T, D), f32), _S((1, D), f32)], compiler_params=_cparams(("arbitrary",)),
    )(x, tgt, g)


def _shift_down(x, s):
    if s == 0:
        return x
    t = lax.broadcasted_iota(jnp.int32, x.shape, 0)
    return jnp.where(t >= s, pltpu.roll(x, s, axis=0), 0.0)


def _shift_up(x, s):
    if s == 0:
        return x
    T = x.shape[0]
    t = lax.broadcasted_iota(jnp.int32, x.shape, 0)
    return jnp.where(t < T - s, pltpu.roll(x, T - s, axis=0), 0.0)


def _conv_pre(x, w, b):
    pre = b
    for k in range(CONV_K):
        pre = pre + w[k:k + 1, :] * _shift_down(x, CONV_K - 1 - k)
    return pre


def _conv_fwd(x, w, b, *, name):
    T, C = x.shape
    cb = _pick(C, 256)

    def body(x_ref, w_ref, b_ref, o_ref):
        o_ref[...] = jax.nn.silu(_conv_pre(x_ref[...], w_ref[...], b_ref[...]))

    col = pl.BlockSpec((T, cb), lambda j: (0, j))
    return pl.pallas_call(
        body, name=name, grid=(C // cb,), in_specs=[col, pl.BlockSpec((CONV_K, cb), lambda j: (0, j)), pl.BlockSpec((1, cb), lambda j: (0, j))],
        out_specs=col, out_shape=_S((T, C), f32), compiler_params=_cparams(("parallel",)),
    )(x, w, b)


def _conv_bwd(x, w, b, dy, *, name, carry=()):
    T, C = x.shape
    cb = _pick(C, 256)
    nc_ = len(carry)

    def body(*refs):
        x_ref, w_ref, b_ref, dy_ref = refs[:4]
        p_refs, (dx_ref, dw_ref, db_ref) = refs[4:4 + nc_], refs[4 + nc_:7 + nc_]
        arr_refs, sems = refs[7 + nc_:7 + 2 * nc_], refs[7 + 2 * nc_:]
        _carry_start(p_refs, arr_refs, sems, pl.program_id(0) == 0)
        xv, wv = x_ref[...], w_ref[...]
        pre = _conv_pre(xv, wv, b_ref[...])
        sg = jax.nn.sigmoid(pre)
        dpre = dy_ref[...] * sg * (1.0 + pre * (1.0 - sg))
        dx = jnp.zeros_like(xv)
        for k in range(CONV_K):
            s = CONV_K - 1 - k
            dx = dx + wv[k:k + 1, :] * _shift_up(dpre, s)
            dw_ref[k:k + 1, :] = jnp.sum(dpre * _shift_down(xv, s), axis=0, keepdims=True)
        dx_ref[...] = dx
        db_ref[...] = jnp.sum(dpre, axis=0, keepdims=True)
        _carry_wait(p_refs, arr_refs, sems, pl.program_id(0) == C // cb - 1)

    col = pl.BlockSpec((T, cb), lambda j: (0, j))
    wsp = pl.BlockSpec((CONV_K, cb), lambda j: (0, j))
    bsp = pl.BlockSpec((1, cb), lambda j: (0, j))
    res = pl.pallas_call(
        body, name=name, grid=(C // cb,), in_specs=[col, wsp, bsp, col] + [_ANY] * nc_, out_specs=[col, wsp, bsp] + [_ANY] * nc_,
        out_shape=[_S((T, C), f32), _S((CONV_K, C), f32), _S((1, C), f32)] + (_chip_out_shapes(carry) if nc_ else []),
        scratch_shapes=_chip_sems(nc_) if nc_ else [], compiler_params=_cparams(("arbitrary",) if nc_ else ("parallel",)),
    )(x, w, b, dy, *carry)
    return (*res[:3], list(res[3:]))


def _f_s5_prep(lr, li, ldt, lrb, lib, ldtb, brt, bit):
    def disc(lr_, li_, ldt_):
        dt = jnp.exp(ldt_)
        mag = jnp.exp(lr_ * dt)
        ar, ai = mag * jnp.cos(li_ * dt), mag * jnp.sin(li_ * dt)
        den = lr_ * lr_ + li_ * li_
        cr = ((ar - 1.0) * lr_ + ai * li_) / den
        ci = (ai * lr_ - (ar - 1.0) * li_) / den
        return ar, ai, cr, ci

    ar, ai, _, _ = disc(lr, li, ldt)
    _, _, cr, ci = disc(lrb, lib, ldtb)
    return ar, ai, cr * brt - ci * bit, cr * bit + ci * brt


def _s5_prep(args, *, name):
    G, N = args[0].shape
    GM = args[3].shape[0]

    def body(*refs):
        res = _f_s5_prep(*[r[...] for r in refs[:8]])
        for o, r in zip(refs[8:], res):
            o[...] = r

    return pl.pallas_call(body, name=name, out_shape=[_S((G, N), f32)] * 2 + [_S((GM, N), f32)] * 2)(*args)


def _s5_prep_bwd(args, cots, rsum, *, name):
    G, N = args[0].shape
    GM = args[3].shape[0]

    def body(*refs):
        ins = [r[...] for r in refs[:8]]
        cot = tuple(r[...] for r in refs[8:12])
        rs = refs[12][...]
        _, vjp = jax.vjp(_f_s5_prep, *ins)
        g = vjp(cot)
        fold = lambda v: jnp.dot(rs, v, preferred_element_type=f32, precision=lax.Precision.HIGHEST)
        o = refs[13:]
        o[0][...] = g[0] + fold(g[3])
        o[1][...] = g[1] + fold(g[4])
        o[2][...] = g[2] + fold(jnp.broadcast_to(g[5], (GM, LANES)))[:, 0:1]
        o[3][...] = g[6]
        o[4][...] = g[7]

    return pl.pallas_call(
        body, name=name, out_shape=[_S((G, N), f32), _S((G, N), f32), _S((G, 1), f32), _S((GM, N), f32), _S((GM, N), f32)],
    )(*args, *cots, rsum)


S5_TC = 512


def _s5_local_scan(src, w_r, w_i, a_r, a_i, *, reverse, name):
    T, C = src.shape
    nblk, cb, sb = w_r.shape
    NS = nblk * sb
    tc = min(S5_TC, T)
    nT, nt = T // tc, tc // NSEG
    tmap = (lambda i: nT - 1 - i) if reverse else (lambda i: i)

    def body(u_ref, wr_ref, wi_ref, ar_ref, ai_ref, sr_ref, si_ref, pr_ref, pi_ref, st_r, st_i, pw_r, pw_i):
        @pl.when(pl.program_id(1) == 0)
        def _():
            st_r[...] = jnp.zeros_like(st_r)
            st_i[...] = jnp.zeros_like(st_i)
            pw_r[...] = jnp.ones_like(pw_r)
            pw_i[...] = jnp.zeros_like(pw_i)

        u = u_ref[...].astype(bf16)
        sr_ref[...] = jnp.dot(u, wr_ref[...], preferred_element_type=f32)
        si_ref[...] = jnp.dot(u, wi_ref[...], preferred_element_type=f32)
        ar = jnp.broadcast_to(ar_ref[...], (NSEG, sb))
        ai = jnp.broadcast_to(ai_ref[...], (NSEG, sb))

        def step(k, c):
            cr, ci, qr, qi = c
            kk = (nt - 1 - k) if reverse else k
            rows = pl.ds(pl.multiple_of(kk * NSEG, NSEG), NSEG)
            nr = ar * cr - ai * ci + sr_ref[rows, :]
            ni = ar * ci + ai * cr + si_ref[rows, :]
            sr_ref[rows, :] = nr
            si_ref[rows, :] = ni
            return nr, ni, ar * qr - ai * qi, ar * qi + ai * qr

        cr, ci, qr, qi = lax.fori_loop(0, nt, step, (st_r[...], st_i[...], pw_r[...], pw_i[...]), unroll=8)
        st_r[...], st_i[...], pw_r[...], pw_i[...] = cr, ci, qr, qi
        pr_ref[...] = qr
        pi_ref[...] = qi

    blk = pl.BlockSpec((tc, sb), lambda j, i: (tmap(i), j))
    wsp = pl.BlockSpec((None, cb, sb), lambda j, i: (j, 0, 0))
    asp = pl.BlockSpec((1, sb), lambda j, i: (0, j))
    psp = pl.BlockSpec((NSEG, sb), lambda j, i: (0, j))
    return pl.pallas_call(
        body, name=name, grid=(nblk, nT), in_specs=[pl.BlockSpec((tc, cb), lambda j, i: (tmap(i), j)), wsp, wsp, asp, asp],
        out_specs=[blk, blk, psp, psp], out_shape=[_S((T, NS), f32)] * 2 + [_S((NSEG, NS), f32)] * 2,
        scratch_shapes=[pltpu.VMEM((NSEG, sb), f32)] * 4, compiler_params=_cparams(("parallel", "arbitrary")),
    )(src, w_r, w_i, a_r, a_i)


def _s5_carry(e_r, e_i, p_r, p_i, *, reverse, name):
    NS = e_r.shape[1]

    def body(er_ref, ei_ref, pr_ref, pi_ref, cr_ref, ci_ref):
        ar, ai = pr_ref[0:1, :], pi_ref[0:1, :]
        cr = jnp.zeros((1, NS), f32)
        ci = jnp.zeros((1, NS), f32)
        order = list(range(NSEG - 1, -1, -1)) if reverse else list(range(NSEG))
        cr_ref[order[0]:order[0] + 1, :] = cr
        ci_ref[order[0]:order[0] + 1, :] = ci
        for prev, q in zip(order[:-1], order[1:]):
            er, ei = er_ref[prev:prev + 1, :], ei_ref[prev:prev + 1, :]
            cr, ci = er + ar * cr - ai * ci, ei + ar * ci + ai * cr
            cr_ref[q:q + 1, :] = cr
            ci_ref[q:q + 1, :] = ci

    return pl.pallas_call(body, name=name, out_shape=[_S((NSEG, NS), f32)] * 2)(e_r, e_i, p_r, p_i)


def _s5_fix_out(sl_r, sl_i, a_r, a_i, c_r, c_i, wc_r, wc_i, *, name):
    T, NS = sl_r.shape
    nblk, sb, cb = wc_r.shape
    tc = min(S5_TC, T)
    nT, nt = T // tc, tc // NSEG

    def body(lr_ref, li_ref, ar_ref, ai_ref, cr_ref, ci_ref, wr_ref, wi_ref, sr_ref, si_ref, y_ref, pw_r, pw_i):
        @pl.when(pl.program_id(1) == 0)
        def _():
            pw_r[...] = jnp.ones_like(pw_r)
            pw_i[...] = jnp.zeros_like(pw_i)

        ar = jnp.broadcast_to(ar_ref[...], (NSEG, sb))
        ai = jnp.broadcast_to(ai_ref[...], (NSEG, sb))
        cr, ci = cr_ref[...], ci_ref[...]

        def step(k, c):
            qr, qi = c
            qr, qi = ar * qr - ai * qi, ar * qi + ai * qr
            rows = pl.ds(pl.multiple_of(k * NSEG, NSEG), NSEG)
            sr_ref[rows, :] = lr_ref[rows, :] + qr * cr - qi * ci
            si_ref[rows, :] = li_ref[rows, :] + qr * ci + qi * cr
            return qr, qi

        qr, qi = lax.fori_loop(0, nt, step, (pw_r[...], pw_i[...]), unroll=8)
        pw_r[...], pw_i[...] = qr, qi
        y_ref[...] = (jnp.dot(sr_ref[...].astype(bf16), wr_ref[...], preferred_element_type=f32)
                      - jnp.dot(si_ref[...].astype(bf16), wi_ref[...], preferred_element_type=f32))

    blk = pl.BlockSpec((tc, sb), lambda j, i: (i, j))
    asp = pl.BlockSpec((1, sb), lambda j, i: (0, j))
    csp = pl.BlockSpec((NSEG, sb), lambda j, i: (0, j))
    wsp = pl.BlockSpec((None, sb, cb), lambda j, i: (j, 0, 0))
    return pl.pallas_call(
        body, name=name, grid=(nblk, nT), in_specs=[blk, blk, asp, asp, csp, csp, wsp, wsp],
        out_specs=[blk, blk, pl.BlockSpec((tc, cb), lambda j, i: (i, j))],
        out_shape=[_S((T, NS), f32)] * 2 + [_S((T, nblk * cb), f32)],
        scratch_shapes=[pltpu.VMEM((NSEG, sb), f32)] * 2, compiler_params=_cparams(("parallel", "arbitrary")),
    )(sl_r, sl_i, a_r, a_i, c_r, c_i, wc_r, wc_i)


def _s5_fix_bwd(ql_r, ql_i, ab_r, ab_i, c_r, c_i, s_r, s_i, sb_r, sb_i, u, dy, du_add, w_r, w_i, *, name):
    T, NS = ql_r.shape
    nblk, cb, sb = w_r.shape
    tc = min(S5_TC, T)
    nT, nt = T // tc, tc // NSEG
    tmap = lambda i: nT - 1 - i

    def body(lr_ref, li_ref, ar_ref, ai_ref, cr_ref, ci_ref, sr_ref, si_ref, br_ref, bi_ref, u_ref, dy_ref, dua_ref, wr_ref, wi_ref,
             du_ref, dwr_ref, dwi_ref, dcr_ref, dci_ref, dar_ref, dai_ref, pw_r, pw_i, ac_r, ac_i, q_r, q_i):
        first = pl.program_id(1) == 0

        @pl.when(first)
        def _():
            pw_r[...] = jnp.ones_like(pw_r)
            pw_i[...] = jnp.zeros_like(pw_i)
            ac_r[...] = jnp.zeros_like(ac_r)
            ac_i[...] = jnp.zeros_like(ac_i)
            dwr_ref[...] = jnp.zeros_like(dwr_ref)
            dwi_ref[...] = jnp.zeros_like(dwi_ref)
            dcr_ref[...] = jnp.zeros_like(dcr_ref)
            dci_ref[...] = jnp.zeros_like(dci_ref)

        ar = jnp.broadcast_to(ar_ref[...], (NSEG, sb))
        ai = jnp.broadcast_to(ai_ref[...], (NSEG, sb))
        cr, ci = cr_ref[...], ci_ref[...]

        def fix(rows, qr, qi, spr, spi, accr, acci):
            qr, qi = ar * qr - ai * qi, ar * qi + ai * qr
            xr = lr_ref[rows, :] + qr * cr - qi * ci
            xi = li_ref[rows, :] + qr * ci + qi * cr
            q_r[rows, :] = xr
            q_i[rows, :] = xi
            return qr, qi, accr + xr * spr + xi * spi, acci + xi * spr - xr * spi

        def step(k, c):
            qr, qi, accr, acci = c
            kk = nt - 1 - k
            rows = pl.ds(pl.multiple_of(kk * NSEG, NSEG), NSEG)
            prev = pl.ds(pl.multiple_of((kk - 1) * NSEG, NSEG), NSEG)
            return fix(rows, qr, qi, sr_ref[prev, :], si_ref[prev, :], accr, acci)

        c = lax.fori_loop(0, nt - 1, step, (pw_r[...], pw_i[...], ac_r[...], ac_i[...]), unroll=7)
        qr, qi, accr, acci = fix(pl.ds(0, NSEG), *c[:2], br_ref[...], bi_ref[...], *c[2:])
        pw_r[...], pw_i[...], ac_r[...], ac_i[...] = qr, qi, accr, acci

        qrb, qib = q_r[...].astype(bf16), q_i[...].astype(bf16)
        nt_dims = (((1,), (1,)), ((), ()))
        tn_dims = (((0,), (0,)), ((), ()))
        du_ref[...] = (dua_ref[...] + lax.dot_general(qrb, wr_ref[...], nt_dims, preferred_element_type=f32)
                       + lax.dot_general(qib, wi_ref[...], nt_dims, preferred_element_type=f32))
        ub = u_ref[...].astype(bf16)
        dwr_ref[...] += lax.dot_general(ub, qrb, tn_dims, preferred_element_type=f32)
        dwi_ref[...] += lax.dot_general(ub, qib, tn_dims, preferred_element_type=f32)
        dyb = dy_ref[...].astype(bf16)
        dcr_ref[...] += lax.dot_general(sr_ref[...].astype(bf16), dyb, tn_dims, preferred_element_type=f32)
        dci_ref[...] -= lax.dot_general(si_ref[...].astype(bf16), dyb, tn_dims, preferred_element_type=f32)

        @pl.when(pl.program_id(1) == nT - 1)
        def _():
            dar_ref[...] = jnp.sum(accr, axis=0, keepdims=True)
            dai_ref[...] = jnp.sum(acci, axis=0, keepdims=True)

    blk = pl.BlockSpec((tc, sb), lambda j, i: (tmap(i), j))
    asp = pl.BlockSpec((1, sb), lambda j, i: (0, j))
    csp = pl.BlockSpec((NSEG, sb), lambda j, i: (0, j))
    bsp = pl.BlockSpec((None, NSEG, sb), lambda j, i: (tmap(i), 0, j))
    chn = pl.BlockSpec((tc, cb), lambda j, i: (tmap(i), j))
    wsp = pl.BlockSpec((None, cb, sb), lambda j, i: (j, 0, 0))
    wcs = pl.BlockSpec((None, sb, cb), lambda j, i: (j, 0, 0))
    return pl.pallas_call(
        body, name=name, grid=(nblk, nT), in_specs=[blk, blk, asp, asp, csp, csp, blk, blk, bsp, bsp, chn, chn, chn, wsp, wsp],
        out_specs=[chn, wsp, wsp, wcs, wcs, asp, asp],
        out_shape=[_S((T, nblk * cb), f32), _S((nblk, cb, sb), f32), _S((nblk, cb, sb), f32), _S((nblk, sb, cb), f32),
                   _S((nblk, sb, cb), f32), _S((1, NS), f32), _S((1, NS), f32)],
        scratch_shapes=[pltpu.VMEM((NSEG, sb), f32)] * 4 + [pltpu.VMEM((tc, sb), f32)] * 2,
        compiler_params=_cparams(("parallel", "arbitrary")),
    )(ql_r, ql_i, ab_r, ab_i, c_r, c_i, s_r, s_i, sb_r, sb_i, u, dy, du_add, w_r, w_i)


SSD2_TB = 512
_HI = lax.Precision.HIGHEST
_NN = (((1,), (0,)), ((), ()))
_NT = (((1,), (1,)), ((), ()))
_TN = (((0,), (0,)), ((), ()))


def _dotf(a, b, dims):
    return lax.dot_general(a.astype(bf16), b.astype(bf16), dims, preferred_element_type=f32)


def _doth(a, b, dims=_NN):
    return lax.dot_general(a, b, dims, preferred_element_type=f32, precision=_HI)


def _ssd_consts(hpg):
    W = hpg * CHUNK
    i = lax.broadcasted_iota(jnp.int32, (CHUNK, CHUNK), 0)
    j = lax.broadcasted_iota(jnp.int32, (CHUNK, CHUNK), 1)
    tril = (i >= j).astype(f32)
    r = lax.broadcasted_iota(jnp.int32, (W, W), 0)
    c = lax.broadcasted_iota(jnp.int32, (W, W), 1)
    bd = (r // CHUNK == c // CHUNK).astype(f32)
    triu_bd = bd * (r <= c).astype(f32)
    e_r = lax.broadcasted_iota(jnp.int32, (W, LANES), 0)
    e_c = lax.broadcasted_iota(jnp.int32, (W, LANES), 1)
    ered = (e_r // HEADDIM == e_c).astype(f32)
    return tril, jnp.tile(tril, (1, hpg)), bd, triu_bd, ered


def _ssd2_specs(G, hpg, tb, tmap, b_off, c_off):
    W = hpg * HEADDIM
    ncb = tb // CHUNK
    xsp = pl.BlockSpec((tb, W), lambda g, i: (tmap(i), g))
    bsp = pl.BlockSpec((tb, SSD_STATE), lambda g, i: (tmap(i), b_off + g))
    csp = pl.BlockSpec((tb, SSD_STATE), lambda g, i: (tmap(i), c_off + g))
    rsp = pl.BlockSpec((None, ncb, W), lambda g, i: (g, tmap(i), 0))
    dsp = pl.BlockSpec((1, W), lambda g, i: (0, g))
    hsp = pl.BlockSpec((None, ncb, SSD_STATE, W), lambda g, i: (g, tmap(i), 0, 0))
    const = lambda a: pl.BlockSpec(a.shape, lambda g, i: (0, 0))
    return xsp, bsp, csp, rsp, dsp, hsp, const


def _tile_rows(a, n):
    return jnp.concatenate([a] * n, axis=0)


def _ssd2_fwd(xc, dt4, a4, dtw, aw, d4, consts, *, d_inner, name):
    T = xc.shape[0]
    G, nc, W = dtw.shape
    hpg = W // CHUNK
    tb = min(SSD2_TB, T)
    nb, ncb = T // tb, tb // CHUNK
    b_off = d_inner // SSD_STATE
    xsp, bsp, csp, rsp, dsp, hsp, const = _ssd2_specs(G, hpg, tb, lambda i: i, b_off, b_off + G)
    tril, mask4, bd, triu_bd, _ = consts

    def body(x_ref, b_ref, c_ref, dt_ref, a_ref, dtw_ref, aw_ref, d_ref, tril_ref, mask_ref, bd_ref, tbd_ref, y_ref, hs_ref, h_scr):
        @pl.when(pl.program_id(1) == 0)
        def _():
            h_scr[...] = jnp.zeros_like(h_scr)

        acs_rows = _doth(aw_ref[...], tbd_ref[...])
        ht = h_scr[...]
        for c in range(ncb):
            rows = slice(c * CHUNK, (c + 1) * CHUNK)
            x, bm, cm = x_ref[rows, :], b_ref[rows, :], c_ref[rows, :]
            acs = _doth(tril_ref[...], a_ref[rows, :])
            lmat = jnp.where(mask_ref[...] > 0, jnp.exp(jnp.minimum(acs - acs_rows[c:c + 1, :], 0.0)), 0.0)
            m4 = _dotf(cm, _tile_rows(bm, hpg), _NT) * lmat * dtw_ref[c:c + 1, :]
            xbd = _tile_rows(x, hpg) * bd_ref[...]
            hs_ref[c] = ht
            y_ref[rows, :] = _dotf(m4, xbd, _NN) + _dotf(cm, ht, _NN) * jnp.exp(acs) + d_ref[...] * x
            a_last = acs[CHUNK - 1:CHUNK, :]
            xw = x * (jnp.exp(a_last - acs) * dt_ref[rows, :])
            ht = ht * jnp.exp(a_last) + _dotf(bm, xw, _TN)
        h_scr[...] = ht

    return pl.pallas_call(
        body, name=name, grid=(G, nb),
        in_specs=[xsp, bsp, csp, xsp, xsp, rsp, rsp, dsp, const(tril), const(mask4), const(bd), const(triu_bd)],
        out_specs=[xsp, hsp], out_shape=[_S((T, G * W), f32), _S((G, nc, SSD_STATE, W), f32)],
        scratch_shapes=[pltpu.VMEM((SSD_STATE, W), f32)], compiler_params=_cparams(("parallel", "arbitrary")),
    )(xc, xc, xc, dt4, a4, dtw, aw, d4, tril, mask4, bd, triu_bd)


def _ssd2_bwd(xc, dt4, a4, dtw, aw, d4, consts, hs, dy, *, d_inner, name):
    T = xc.shape[0]
    G, nc, W = dtw.shape
    hpg = W // CHUNK
    tb = min(SSD2_TB, T)
    nb, ncb = T // tb, tb // CHUNK
    b_off = d_inner // SSD_STATE
    tmap = lambda i: nb - 1 - i
    xsp, bsp, csp, rsp, dsp, hsp, const = _ssd2_specs(G, hpg, tb, tmap, b_off, b_off + G)
    gsp = pl.BlockSpec((tb, SSD_STATE), lambda g, i: (tmap(i), g))
    ddsp = pl.BlockSpec((None, 1, LANES), lambda g, i: (g, 0, 0))
    tril, mask4, bd, triu_bd, ered = consts

    def body(x_ref, b_ref, c_ref, dt_ref, a_ref, dtw_ref, aw_ref, d_ref, tril_ref, mask_ref, bd_ref, tbd_ref, er_ref, hs_ref, dy_ref,
             dx_ref, db_ref, dc_ref, ddtc_ref, dac_ref, ddtw_ref, daw_ref, dd_ref, g_scr, dd_scr, rw_scr, tl_scr):
        first = pl.program_id(1) == 0

        @pl.when(first)
        def _():
            g_scr[...] = jnp.zeros_like(g_scr)
            dd_scr[...] = jnp.zeros_like(dd_scr)

        mask = mask_ref[...] > 0
        lane_in_block = lax.broadcasted_iota(jnp.int32, mask.shape, 1) & (CHUNK - 1)
        maskt = lax.broadcasted_iota(jnp.int32, mask.shape, 0) <= lane_in_block
        acs_rows = _doth(aw_ref[...], tbd_ref[...])
        dht = g_scr[...]
        dd = dd_scr[...]
        for c in range(ncb - 1, -1, -1):
            rows = slice(c * CHUNK, (c + 1) * CHUNK)
            x, bm, cm, dyc = x_ref[rows, :], b_ref[rows, :], c_ref[rows, :], dy_ref[rows, :]
            dtc, dtr = dt_ref[rows, :], dtw_ref[c:c + 1, :]
            ht = hs_ref[c]
            acs = _doth(tril_ref[...], a_ref[rows, :])
            seg = acs - acs_rows[c:c + 1, :]
            lmat = jnp.where(mask, jnp.exp(jnp.minimum(seg, 0.0)), 0.0)
            lmat_t = jnp.where(maskt, jnp.exp(jnp.minimum(-seg, 0.0)), 0.0)
            btile, ctile = _tile_rows(bm, hpg), _tile_rows(cm, hpg)
            g4 = _dotf(cm, btile, _NT)
            gt4 = _dotf(bm, ctile, _NT)
            m4 = g4 * lmat * dtr
            mt4 = gt4 * lmat_t * dtc
            xbd = _tile_rows(x, hpg) * bd_ref[...]
            dybd = _tile_rows(dyc, hpg) * bd_ref[...]
            dm4 = _dotf(dyc, xbd, _NT)
            dmt4 = _dotf(x, dybd, _NT)
            dx = d_ref[...] * dyc + _dotf(mt4, dybd, _NN)
            dd = dd + jnp.sum(dyc * x, axis=0, keepdims=True)
            e4 = dm4 * m4
            dc = _dotf(dm4 * lmat * dtr, btile, _NN)
            db = _dotf(dmt4 * lmat_t * dtc, ctile, _NN)
            decay = jnp.exp(acs)
            yoff = _dotf(cm, ht, _NN) * decay
            dz = dyc * decay
            dc = dc + _dotf(dz, ht, _NT)
            dht_prev = _dotf(cm, dz, _TN)
            a_last = acs[CHUNK - 1:CHUNK, :]
            ea_last = jnp.exp(a_last)
            erel = jnp.exp(a_last - acs)
            dte = erel * dtc
            dxw = _dotf(bm, dht, _NN)
            db = db + _dotf(x * dte, dht, _NT)
            dx = dx + dxw * dte
            q4 = dxw * x
            dacs = e4 + dyc * yoff - q4 * dte
            col = jnp.concatenate([q4 * erel, _doth(tril_ref[...], dacs, _TN)], axis=0)
            col = _doth(col, er_ref[...])
            ddtc_ref[rows, :] = col[:CHUNK]
            dac_ref[rows, :] = col[CHUNK:]
            ddtw_ref[c:c + 1, :] = jnp.sum(dm4 * g4 * lmat, axis=0, keepdims=True)
            rw_scr[c:c + 1, :] = -jnp.sum(e4, axis=0, keepdims=True)
            tl_scr[c:c + 1, :] = jnp.sum(q4 * dte, axis=0, keepdims=True) + ea_last * jnp.sum(dht * ht, axis=0, keepdims=True)
            dx_ref[rows, :] = dx
            db_ref[rows, :] = db
            dc_ref[rows, :] = dc
            dht = dht_prev + dht * ea_last
        daw_ref[...] = _doth(rw_scr[...], tbd_ref[...], _NT) + _doth(tl_scr[...], bd_ref[...])
        g_scr[...] = dht
        dd_scr[...] = dd

        @pl.when(pl.program_id(1) == nb - 1)
        def _():
            dd_ref[...] = _doth(dd, er_ref[...])

    return pl.pallas_call(
        body, name=name, grid=(G, nb),
        in_specs=[xsp, bsp, csp, xsp, xsp, rsp, rsp, dsp, const(tril), const(mask4), const(bd), const(triu_bd), const(ered), hsp, xsp],
        out_specs=[xsp, gsp, gsp, gsp, gsp, rsp, rsp, ddsp],
        out_shape=[_S((T, G * W), f32), _S((T, G * SSD_STATE), f32), _S((T, G * SSD_STATE), f32), _S((T, G * LANES), f32),
                   _S((T, G * LANES), f32), _S(dtw.shape, f32), _S(dtw.shape, f32), _S((G, 1, LANES), f32)],
        scratch_shapes=[pltpu.VMEM((SSD_STATE, W), f32), pltpu.VMEM((1, W), f32), pltpu.VMEM((ncb, W), f32), pltpu.VMEM((ncb, W), f32)],
        compiler_params=_cparams(("parallel", "arbitrary")),
    )(xc, xc, xc, dt4, a4, dtw, aw, d4, tril, mask4, bd, triu_bd, ered, hs, dy)


def _peers():
    x, y, c = lax.axis_index("x"), lax.axis_index("y"), lax.axis_index("c")
    return x, y, c


_ANY = pl.BlockSpec(memory_space=pl.ANY)
N_CHIP = N_DEV // 2


def _all_gather(shards, *, name):
    n = len(shards)

    def body(*refs):
        x_refs, out_refs = refs[:n], refs[n:2 * n]
        send_sems, recv_sems, local_sems = refs[2 * n:]
        x, y, c = _peers()
        me, sibling = (x, y, c), (x, y, 1 - c)
        chips = [(1 - x, y), (x, 1 - y), (1 - x, 1 - y)]

        def copy(a, r, block, to, src=None):
            px, py, pc = block
            slot = out_refs[a].at[4 * px + 2 * py + pc]
            return pltpu.make_async_remote_copy(
                src_ref=slot if src is None else src, dst_ref=slot, send_sem=send_sems.at[7 * a + r],
                recv_sem=recv_sems.at[7 * a + r], device_id=to, device_id_type=MESH)

        mine = [pltpu.make_async_copy(x_refs[a], out_refs[a].at[4 * x + 2 * y + c], local_sems.at[a]) for a in range(n)]
        for cp in mine:
            cp.start()
        first = []
        for a in range(n):
            first.append(copy(a, 0, me, sibling, src=x_refs[a]))
            first += [copy(a, 1 + j, me, (*chip, c), src=x_refs[a]) for j, chip in enumerate(chips)]
        for cp in first:
            cp.start()
        passed = []
        for j, chip in enumerate(chips):
            for a in range(n):
                copy(a, 1 + j, (*chip, c), me).wait_recv()
                fwd = copy(a, 4 + j, (*chip, c), sibling)
                fwd.start()
                passed.append(fwd)
        for a in range(n):
            copy(a, 0, sibling, me).wait_recv()
        for j, chip in enumerate(chips):
            for a in range(n):
                copy(a, 4 + j, (*chip, 1 - c), me).wait_recv()
        for cp in first + passed:
            cp.wait_send()
        for cp in mine:
            cp.wait()

    return pl.pallas_call(
        body, name=name, out_shape=[_S((N_DEV,) + s.shape, s.dtype) for s in shards], in_specs=[_ANY] * n, out_specs=[_ANY] * n,
        scratch_shapes=[pltpu.SemaphoreType.DMA((7 * n,)), pltpu.SemaphoreType.DMA((7 * n,)), pltpu.SemaphoreType.DMA((n,))],
    )(*shards)


def _exchange_sibling(slots, *, name):
    n = len(slots)

    def body(*refs):
        x_refs, sib_refs = refs[:n], refs[n:2 * n]
        send_sems, recv_sems = refs[2 * n:]
        x, y, c = _peers()
        give = [pltpu.make_async_remote_copy(
            src_ref=x_refs[a].at[pl.ds(N_CHIP * (1 - c), N_CHIP)], dst_ref=sib_refs[a], send_sem=send_sems.at[a],
            recv_sem=recv_sems.at[a], device_id=(x, y, 1 - c), device_id_type=MESH) for a in range(n)]
        for cp in give:
            cp.start()
        for cp in give:
            cp.wait_recv()
        for cp in give:
            cp.wait_send()

    return list(pl.pallas_call(
        body, name=name, out_shape=[_S((N_CHIP,) + s.shape[1:], s.dtype) for s in slots], in_specs=[_ANY] * n, out_specs=[_ANY] * n,
        scratch_shapes=[pltpu.SemaphoreType.DMA((n,)), pltpu.SemaphoreType.DMA((n,))],
    )(*slots))


def _chip_sum(slots, sib, core, *, name):
    _, R, W = slots.shape
    tr = _pick(R, max(16, (1 << 20) // (4 * W)), align=16)

    def body(core_ref, x_ref, s_ref, o_ref):
        o_ref[...] = (x_ref[...].astype(f32) + s_ref[...].astype(f32)).astype(o_ref.dtype)

    blk = pl.BlockSpec((None, tr, W), lambda t, i, core_ref: (t, i, 0))
    return pl.pallas_call(
        body, name=name, out_shape=_S(sib.shape, slots.dtype),
        grid_spec=pltpu.PrefetchScalarGridSpec(
            num_scalar_prefetch=1, grid=(N_CHIP, R // tr),
            in_specs=[pl.BlockSpec((None, tr, W), lambda t, i, core_ref: (N_CHIP * core_ref[0] + t, i, 0)), blk], out_specs=blk),
        compiler_params=_cparams(("parallel", "parallel")),
    )(core, slots, sib)


def _exchange_chips(parts, *, name):
    n = len(parts)

    def body(*refs):
        copies = _chip_copies(refs[:n], refs[n:2 * n], *refs[2 * n:])
        _start_all(copies)
        _wait_all(copies)

    return list(pl.pallas_call(
        body, name=name, out_shape=_chip_out_shapes(parts), in_specs=[_ANY] * n, out_specs=[_ANY] * n,
        scratch_shapes=_chip_sems(n),
    )(*parts))


def _chip_out_shapes(parts):
    return [_S((N_CHIP - 1,) + p.shape[1:], p.dtype) for p in parts]


def _chip_sems(n):
    return [pltpu.SemaphoreType.DMA((3 * n,)), pltpu.SemaphoreType.DMA((3 * n,))]


def _chip_copies(p_refs, out_refs, send_sems, recv_sems):
    x, y, c = _peers()
    copies = []
    for j in range(1, N_CHIP):
        tx, ty = x ^ (j >> 1), y ^ (j & 1)
        for a in range(len(p_refs)):
            copies.append(pltpu.make_async_remote_copy(
                src_ref=p_refs[a].at[2 * tx + ty], dst_ref=out_refs[a].at[j - 1], send_sem=send_sems.at[3 * a + j - 1],
                recv_sem=recv_sems.at[3 * a + j - 1], device_id=(tx, ty, c), device_id_type=MESH))
    return copies


def _start_all(copies):
    for cp in copies:
        cp.start()


def _wait_all(copies):
    for cp in copies:
        cp.wait_recv()
    for cp in copies:
        cp.wait_send()


def _carry_start(p_refs, out_refs, sems, first):
    if p_refs:
        @pl.when(first)
        def _():
            _start_all(_chip_copies(p_refs, out_refs, *sems))


def _carry_wait(p_refs, out_refs, sems, last):
    if p_refs:
        @pl.when(last)
        def _():
            _wait_all(_chip_copies(p_refs, out_refs, *sems))


def _sum_slots(stack, *, name):
    n, R, W = stack.shape
    tr = _pick(R, 1024, align=8)

    def body(s_ref, o_ref):
        acc = s_ref[0]
        for k in range(1, n):
            acc = acc + s_ref[k]
        o_ref[...] = acc

    return pl.pallas_call(
        body, name=name, grid=(R // tr,), in_specs=[pl.BlockSpec((n, tr, W), lambda i: (0, i, 0))],
        out_specs=pl.BlockSpec((tr, W), lambda i: (i, 0)), out_shape=_S((R, W), f32), compiler_params=_cparams(("parallel",)),
    )(stack)


def _adamw_math(gv, wv, mv, vv):
    c1 = 1.0 / (1.0 - ADAM_B1 ** ADAM_STEP)
    c2 = 1.0 / (1.0 - ADAM_B2 ** ADAM_STEP)
    nm = ADAM_B1 * mv + (1.0 - ADAM_B1) * gv
    nv = ADAM_B2 * vv + (1.0 - ADAM_B2) * jnp.square(gv)
    return -ADAM_LR * ((nm * c1) / (jnp.sqrt(nv * c2) + ADAM_EPS) + ADAM_WD * wv), nm, nv


def _adamw(g, w, m, v, *, name):
    R, W = w.shape
    tr = _pick(R, max(8, (1 << 20) // (4 * W)), align=8)

    def body(g_ref, w_ref, m_ref, v_ref, d_ref, nm_ref, nv_ref):
        d_ref[...], nm_ref[...], nv_ref[...] = _adamw_math(g_ref[...], w_ref[...], m_ref[...], v_ref[...])

    sp = pl.BlockSpec((tr, W), lambda i: (i, 0))
    return pl.pallas_call(
        body, name=name, grid=(R // tr,), in_specs=[sp] * 4, out_specs=[sp] * 3, out_shape=[_S((R, W), f32)] * 3,
        compiler_params=_cparams(("parallel",)),
    )(g, w, m, v)


def _reduce_adamw(own, arrived, chip, w, m, v, *, name):
    n, R, W = arrived.shape
    tr = _pick(R, max(16, (1 << 20) // (4 * W)), align=16)

    def body(chip_ref, o_ref, p_ref, w_ref, m_ref, v_ref, g_ref, d_ref, nm_ref, nv_ref):
        gv = o_ref[...].astype(f32)
        for k in range(n):
            gv = gv + p_ref[k].astype(f32)
        g_ref[...] = gv
        d_ref[...], nm_ref[...], nv_ref[...] = _adamw_math(gv, w_ref[...], m_ref[...], v_ref[...])

    sp = pl.BlockSpec((tr, W), lambda i, chip_ref: (i, 0))
    return pl.pallas_call(
        body, name=name, out_shape=[_S((R, W), f32)] * 4,
        grid_spec=pltpu.PrefetchScalarGridSpec(
            num_scalar_prefetch=1, grid=(R // tr,),
            in_specs=[pl.BlockSpec((None, tr, W), lambda i, chip_ref: (chip_ref[0], i, 0)),
                      pl.BlockSpec((n, tr, W), lambda i, chip_ref: (0, i, 0))] + [sp] * 3, out_specs=[sp] * 4),
        compiler_params=_cparams(("parallel",)),
    )(chip, own, arrived, w, m, v)


def _pieces(seg_start, seg_len, shard_w):
    out, col = [], seg_start
    while col < seg_start + seg_len:
        k, a = divmod(col, shard_w)
        n = min(shard_w - a, seg_start + seg_len - col)
        out.append((k, a, col - seg_start, n))
        col += n
    return out


def _unshard_w_in(g, seg_lens, *, name):
    _, D, w = g.shape
    starts = [sum(seg_lens[:i]) for i in range(len(seg_lens))]
    widths = [max(n, LANES) for n in seg_lens]
    tm = _pick(D, 256, align=16)

    def body(g_ref, *o_refs):
        for o_ref, s0, n in zip(o_refs, starts, seg_lens):
            if n < o_ref.shape[1]:
                o_ref[...] = jnp.zeros_like(o_ref)
            for k, a, off, m in _pieces(s0, n, w):
                o_ref[:, off:off + m] = g_ref[k, :, a:a + m]

    return pl.pallas_call(
        body, name=name, grid=(D // tm,), in_specs=[pl.BlockSpec((N_DEV, tm, w), lambda i: (0, i, 0))],
        out_specs=[pl.BlockSpec((tm, wd), lambda i: (i, 0)) for wd in widths], out_shape=[_S((D, wd), g.dtype) for wd in widths],
        compiler_params=_cparams(("parallel",)),
    )(g)


def _unshard_pair(g1, g2, *, name):
    _, D, w = g1.shape
    tm = _pick(D, 256, align=16)

    def body(a_ref, b_ref, o_ref):
        for i, g_ref in enumerate((a_ref, b_ref)):
            for k in range(N_DEV):
                off = (i * N_DEV + k) * w
                o_ref[:, off:off + w] = g_ref[k]

    blk = pl.BlockSpec((N_DEV, tm, w), lambda i: (0, i, 0))
    return pl.pallas_call(
        body, name=name, grid=(D // tm,), in_specs=[blk, blk], out_specs=pl.BlockSpec((tm, 2 * N_DEV * w), lambda i: (i, 0)),
        out_shape=_S((D, 2 * N_DEV * w), g1.dtype), compiler_params=_cparams(("parallel",)),
    )(g1, g2)


def _reshard_pair(dw, *, name):
    D, w = dw.shape[0], dw.shape[1] // (2 * N_DEV)
    tm = _pick(D, 128, align=16)

    def body(g_ref, a_ref, b_ref):
        for i, o_ref in enumerate((a_ref, b_ref)):
            for k in range(N_DEV):
                off = (i * N_DEV + k) * w
                o_ref[_slot_of(k)] = g_ref[:, off:off + w].astype(o_ref.dtype)

    blk = pl.BlockSpec((N_DEV, tm, w), lambda i: (0, i, 0))
    return pl.pallas_call(
        body, name=name, grid=(D // tm,), in_specs=[pl.BlockSpec((tm, dw.shape[1]), lambda i: (i, 0))], out_specs=[blk, blk],
        out_shape=[_S((N_DEV, D, w), bf16)] * 2, compiler_params=_cparams(("parallel",)),
    )(dw)


def _reshard_w_in(grads, seg_lens, w, *, name):
    D = grads[0].shape[0]
    starts = [sum(seg_lens[:i]) for i in range(len(seg_lens))]
    tm = _pick(D, 128, align=16)

    def body(*refs):
        o_ref = refs[-1]
        for g_ref, s0, n in zip(refs[:-1], starts, seg_lens):
            for k, a, off, m in _pieces(s0, n, w):
                o_ref[_slot_of(k), :, a:a + m] = g_ref[:, off:off + m].astype(o_ref.dtype)

    return pl.pallas_call(
        body, name=name, grid=(D // tm,), in_specs=[pl.BlockSpec((tm, g.shape[1]), lambda i: (i, 0)) for g in grads],
        out_specs=pl.BlockSpec((N_DEV, tm, w), lambda i: (0, i, 0)), out_shape=_S((N_DEV, D, w), bf16),
        compiler_params=_cparams(("parallel",)),
    )(*grads)


def _pad_flat(a, mult):
    a = a.reshape(-1)
    n = -(-a.shape[0] // mult) * mult
    return a if n == a.shape[0] else jnp.pad(a, (0, n - a.shape[0]))


def _pad_cols(a, mult):
    n = -(-a.shape[1] // mult) * mult
    return a if n == a.shape[1] else jnp.pad(a, ((0, 0), (0, n - a.shape[1])))


def _block_diag(t):
    nblk, g, P, Q = t.shape
    eye = jnp.eye(g, dtype=t.dtype)
    return (t[:, :, :, None, :] * eye[None, :, None, :, None]).reshape(nblk, g * P, g * Q)


def _block_diag_t(w, P, Q):
    nblk = w.shape[0]
    g = w.shape[1] // P
    eye = jnp.eye(g, dtype=w.dtype)
    return (w.reshape(nblk, g, P, g, Q) * eye[None, :, None, :, None]).sum(axis=3)


_COLS = ("ffn1_w_gate", "ffn1_w_up", "ffn2_w_gate", "ffn2_w_up")
_ROWS = ("ffn1_w_down", "ffn2_w_down", "s5_w_glu", "w_proj_s5", "w_out", "w_proj_ssd")
_BIG = _COLS + _ROWS + ("w_in", "conv_w")
_SMALL = ("ffn1_norm", "mix_norm", "conv_b", "s5_A_re", "s5_A_im", "s5_log_dt", "s5_B_re", "s5_B_im", "s5_C_re", "s5_C_im",
          "s5_D", "s5_b_glu", "ssd_A_log", "ssd_dt_bias", "ssd_D", "ssd_norm", "b_gate", "ffn2_norm", "final_norm")
_WEIGHTS = ("ffn1_norm", "ffn1_w_gate", "ffn1_w_up", "ffn1_w_down", "mix_norm", "w_in", "conv_w", "conv_b", "s5_A_re", "s5_A_im",
            "s5_log_dt", "s5_B_re", "s5_B_im", "s5_C_re", "s5_C_im", "s5_D", "s5_w_glu", "s5_b_glu", "ssd_A_log", "ssd_dt_bias",
            "ssd_D", "ssd_norm", "w_proj_s5", "w_proj_ssd", "b_gate", "w_out", "ffn2_norm", "ffn2_w_gate", "ffn2_w_up",
            "ffn2_w_down", "final_norm")
def _ffn_fwd(x, n, wgu, wd, tag):
    D = x.shape[1]
    F = wd.shape[0]
    h = _rows(_f_rmsnorm, [x], [n], [(D, bf16)], name=tag + "_norm")[0]
    ab = _mm(h, wgu, name=tag + "_gate_up")
    c = _rows(_f_swiglu, [ab], [], [(F, bf16)], name=tag + "_act")[0]
    y = _mm(c, wd, scale=0.5, add=x, name=tag + "_down")
    return y, (x, n, h, ab, c)


def _ffn_bwd(saved, wgu, wd, dy, tag, carry=()):
    x, n, h, ab, c = saved
    F = wd.shape[0]
    dc = _mm(dy, wd, tb=True, scale=0.5, name=tag + "_d_act")
    dwd = _mm(c, dy, ta=True, o_blk="m", o_slots=True, tm=F // 2, out_dtype=bf16, scale=0.5, name=tag + "_d_wdown")
    (dab,), _ = _rows_bwd(_f_swiglu, [ab], [], [dc], name=tag + "_act_bwd", want_rows=[0], row_dtypes={0: bf16})
    dh = _mm(dab, wgu, tb=True, carry=carry, name=tag + "_d_h")
    dh, arrived = dh if carry else (dh, [])
    dwg, dwu = _reshard_pair(_mm(h, dab, ta=True, name=tag + "_d_wgu"), name=tag + "_reshard_d_wgu")
    (dx,), (dn,) = _rows_bwd(_f_rmsnorm, [x], [n], [dh], name=tag + "_norm_bwd", want_rows=[0], adds={0: dy})
    return dx, dn, dwg, dwu, dwd, arrived


def kernel(x, ffn1_norm, ffn1_w_gate, ffn1_w_up, ffn1_w_down, mix_norm, w_in, conv_w, conv_b, s5_A_re, s5_A_im, s5_log_dt, s5_B_re, s5_B_im, s5_C_re, s5_C_im, s5_D, s5_w_glu, s5_b_glu, ssd_A_log, ssd_dt_bias, ssd_D, ssd_norm, w_proj_s5, w_proj_ssd, b_gate, w_out, ffn2_norm, ffn2_w_gate, ffn2_w_up, ffn2_w_down, final_norm, loss_target, m_ffn1_norm, m_ffn1_w_gate, m_ffn1_w_up, m_ffn1_w_down, m_mix_norm, m_w_in, m_conv_w, m_conv_b, m_s5_A_re, m_s5_A_im, m_s5_log_dt, m_s5_B_re, m_s5_B_im, m_s5_C_re, m_s5_C_im, m_s5_D, m_s5_w_glu, m_s5_b_glu, m_ssd_A_log, m_ssd_dt_bias, m_ssd_D, m_ssd_norm, m_w_proj_s5, m_w_proj_ssd, m_b_gate, m_w_out, m_ffn2_norm, m_ffn2_w_gate, m_ffn2_w_up, m_ffn2_w_down, m_final_norm, v_ffn1_norm, v_ffn1_w_gate, v_ffn1_w_up, v_ffn1_w_down, v_mix_norm, v_w_in, v_conv_w, v_conv_b, v_s5_A_re, v_s5_A_im, v_s5_log_dt, v_s5_B_re, v_s5_B_im, v_s5_C_re, v_s5_C_im, v_s5_D, v_s5_w_glu, v_s5_b_glu, v_ssd_A_log, v_ssd_dt_bias, v_ssd_D, v_ssd_norm, v_w_proj_s5, v_w_proj_ssd, v_b_gate, v_w_out, v_ffn2_norm, v_ffn2_w_gate, v_ffn2_w_up, v_ffn2_w_down, v_final_norm):
    P = dict(locals())
    T, D = x.shape[1], x.shape[2]
    x0, tgt = x[0], loss_target[0]
    sh = {k: P[k][0] for k in _BIG}

    gathered = _all_gather([sh[k] if k == "conv_w" else sh[k].astype(bf16) for k in _BIG], name="gather_weights")
    W = dict(zip(_BIG, gathered))
    whole = lambda k: W[k].reshape(-1, D)
    g_win = W["w_in"]
    conv_w_full = W["conv_w"].transpose(1, 0, 2).reshape(CONV_K, -1)

    d_inner = N_DEV * sh["w_proj_ssd"].shape[0]
    conv_dim = conv_w_full.shape[1]
    H = ssd_A_log.shape[1]
    G = (conv_dim - d_inner) // (2 * SSD_STATE)
    hpg = H // G
    nc = T // CHUNK
    Gs = D // S5_GROUP
    nblk = Gs // S5_GPB
    NS = Gs * S5_STATE
    seg_lens = (D, d_inner, conv_dim, H, 2 * D)
    w_u, w_z, w_xbc, w_dt, w_gl = _unshard_w_in(g_win, seg_lens, name="unshard_w_in")
    w_glu, w_p5, w_pssd, w_o = whole("s5_w_glu"), whole("w_proj_s5"), whole("w_proj_ssd"), whole("w_out")

    ffn1_w = (_unshard_pair(W["ffn1_w_gate"], W["ffn1_w_up"], name="unshard_ffn1_gate_up"), whole("ffn1_w_down"))
    ffn2_w = (_unshard_pair(W["ffn2_w_gate"], W["ffn2_w_up"], name="unshard_ffn2_gate_up"), whole("ffn2_w_down"))
    x1, sv1 = _ffn_fwd(x0, ffn1_norm, *ffn1_w, "ffn1")
    h2 = _rows(_f_rmsnorm, [x1], [mix_norm], [(D, bf16)], name="mix_norm")[0]
    u_p = _mm(h2, w_u, o_seg=True, name="in_u")
    z = _mm(h2, w_z, name="in_z")
    xbc = _mm(h2, w_xbc, name="in_xbc")
    gl = _mm(h2, w_gl, name="in_gate")
    dtr = _mm(h2, w_dt, name="in_dt")

    rep = lambda a: jnp.repeat(a, S5_GROUP, axis=0)
    lr, li, ldt = s5_A_re[0], s5_A_im[0], s5_log_dt[0].reshape(Gs, 1)
    brt = s5_B_re[0].transpose(0, 2, 1).reshape(Gs * S5_GROUP, S5_STATE)
    bit = s5_B_im[0].transpose(0, 2, 1).reshape(Gs * S5_GROUP, S5_STATE)
    prep_args = (lr, li, ldt, rep(lr), rep(li), rep(ldt), brt, bit)
    ar, ai, bbrt, bbit = _s5_prep(prep_args, name="s5_prep")
    a_r, a_i = ar.reshape(1, NS), ai.reshape(1, NS)
    wb_r = _block_diag(bbrt.reshape(nblk, S5_GPB, S5_GROUP, S5_STATE)).astype(bf16)
    wb_i = _block_diag(bbit.reshape(nblk, S5_GPB, S5_GROUP, S5_STATE)).astype(bf16)
    c4r = s5_C_re[0].reshape(nblk, S5_GPB, S5_GROUP, S5_STATE).transpose(0, 1, 3, 2)
    c4i = s5_C_im[0].reshape(nblk, S5_GPB, S5_GROUP, S5_STATE).transpose(0, 1, 3, 2)
    wc_r, wc_i = _block_diag(c4r).astype(bf16), _block_diag(c4i).astype(bf16)
    sl_r, sl_i, p_r, p_i = _s5_local_scan(u_p, wb_r, wb_i, a_r, a_i, reverse=False, name="s5_scan")
    c_r, c_i = _s5_carry(sl_r[T - NSEG:], sl_i[T - NSEG:], p_r, p_i, reverse=False, name="s5_carry")
    s_r, s_i, ylin = _s5_fix_out(sl_r, sl_i, a_r, a_i, c_r, c_i, wc_r, wc_i, name="s5_fix_out")
    g5 = _rows(_f_s5_post, [ylin, u_p], [s5_D], [(D, f32)], name="s5_gelu")[0]
    v5 = _mm(g5, w_glu, name="s5_glu_mm")
    o5 = _rows(_f_glu, [g5, v5], [s5_b_glu], [(D, bf16)], name="s5_glu")[0]
    p5 = _mm(o5, w_p5, a_seg=True, name="proj_s5")

    xc = _conv_fwd(xbc, conv_w_full, conv_b, name="conv")
    bias_p, alog_p = _pad_cols(ssd_dt_bias, LANES), _pad_cols(ssd_A_log, LANES)
    dt_p, da_p = _rows(_f_dt, [dtr], [bias_p, alog_p], [(LANES, f32), (LANES, f32)], name="ssd_dt")
    col_l = lambda a: jnp.repeat(a[:, :H], HEADDIM, axis=1)
    row_l = lambda a: a[:, :H].reshape(nc, CHUNK, G, hpg).transpose(2, 0, 3, 1).reshape(G, nc, hpg * CHUNK)
    ssd_in = (xc, col_l(dt_p), col_l(da_p), row_l(dt_p), row_l(da_p), jnp.repeat(ssd_D, HEADDIM, axis=1), _ssd_consts(hpg))
    y_ssd, hs = _ssd2_fwd(*ssd_in, d_inner=d_inner, name="ssd")
    yn = _rows(_f_gated_norm, [y_ssd, z], [ssd_norm], [(d_inner, bf16)], name="ssd_gated_norm")[0]
    pssd = _mm(yn, w_pssd, name="proj_ssd")

    merged = _rows(_f_merge, [gl, p5, pssd], [b_gate], [(D, bf16)], name="merge")[0]
    x2 = _mm(merged, w_o, add=x1, name="out_proj")
    x3, sv2 = _ffn_fwd(x2, ffn2_norm, *ffn2_w, "ffn2")
    lossv, dx3, d_final = _loss_stage(x3, tgt, final_norm.reshape(1, D), name="loss")

    gw = {}
    gs = {"final_norm": d_final}
    slot_mm = lambda a_, b_, name, **kw: _mm(a_, b_, ta=True, o_blk="m", o_slots=True, out_dtype=bf16, name=name, **kw)
    core = lax.axis_index("c").astype(jnp.int32).reshape(1)
    chip = (2 * lax.axis_index("x") + lax.axis_index("y")).astype(jnp.int32).reshape(1)
    chip_sums, arrived = {}, {}

    def level1(keys, tag):
        sib = _exchange_sibling([gw[k] for k in keys], name="exchange_sibling_" + tag)
        for k, s_ in zip(keys, sib):
            chip_sums[k] = _chip_sum(gw[k], s_, core, name="chip_sum_" + k)
        return [chip_sums[k] for k in keys]

    dx2, gs["ffn2_norm"], gw["ffn2_w_gate"], gw["ffn2_w_up"], gw["ffn2_w_down"], _ = _ffn_bwd(sv2, *ffn2_w, dx3, "ffn2")

    dmerged = _mm(dx2, w_o, tb=True, name="d_merged")
    gw["w_out"] = slot_mm(merged, dx2, "d_w_out")
    (dgl, dp5, dpssd), (gs["b_gate"],) = _rows_bwd(_f_merge, [gl, p5, pssd], [b_gate], [dmerged], name="merge_bwd", want_rows=[0, 1, 2])

    dyn = _mm(dpssd, w_pssd, tb=True, name="d_yn")
    gw["w_proj_ssd"] = slot_mm(yn, dpssd, "d_w_proj_ssd")
    group_a = ("ffn2_w_gate", "ffn2_w_up", "ffn2_w_down", "w_out", "w_proj_ssd")
    parts_a = level1(group_a, "a")
    (dyssd, dz), (gs["ssd_norm"],) = _rows_bwd(_f_gated_norm, [y_ssd, z], [ssd_norm], [dyn], name="ssd_gated_norm_bwd", want_rows=[0, 1])
    dxs, dbm, dcm, ddtc, ddac, ddtw, ddaw, ddh = _ssd2_bwd(*ssd_in, hs, dyssd, d_inner=d_inner, name="ssd_bwd")

    def fold(col, row):
        col = col.reshape(T, G, LANES)[:, :, :hpg].reshape(T, H)
        row = row.reshape(G, nc, hpg, CHUNK).transpose(1, 3, 0, 2).reshape(T, H)
        return _pad_cols(col + row, LANES)

    (ddtr,), (dbias_p, dalog_p) = _rows_bwd(_f_dt, [dtr], [bias_p, alog_p], [fold(ddtc, ddtw), fold(ddac, ddaw)], name="ssd_dt_bwd", want_rows=[0])
    gs["ssd_dt_bias"], gs["ssd_A_log"], gs["ssd_D"] = dbias_p[:, :H], dalog_p[:, :H], ddh[:, 0, :hpg].reshape(1, H)
    dxbc, d_conv_w, gs["conv_b"], arr = _conv_bwd(
        xbc, conv_w_full, conv_b, jnp.concatenate([dxs, dbm, dcm], axis=1), carry=parts_a, name="conv_bwd")
    arrived.update(zip(group_a, arr))
    cwk = sh["conv_w"].shape[1]
    gw["conv_w"] = d_conv_w.reshape(CONV_K, N_CHIP, 2, cwk).transpose(2, 1, 0, 3).reshape(N_DEV, CONV_K, cwk)

    do5 = _mm(dp5, w_p5, tb=True, o_seg=True, name="d_o5")
    gw["w_proj_s5"] = slot_mm(o5, dp5, "d_w_proj_s5", a_seg=True)
    (dg5a, dv5), (gs["s5_b_glu"],) = _rows_bwd(_f_glu, [g5, v5], [s5_b_glu], [do5], name="s5_glu_bwd", want_rows=[0, 1])
    dg5 = _mm(dv5, w_glu, tb=True, add=dg5a, name="d_g5")
    gw["s5_w_glu"] = slot_mm(g5, dv5, "d_w_glu")
    (dylin, du_a), (gs["s5_D"],) = _rows_bwd(_f_s5_post, [ylin, u_p], [s5_D], [dg5], name="s5_gelu_bwd", want_rows=[0, 1])
    wct_r, wct_i = wc_r.transpose(0, 2, 1), -wc_i.transpose(0, 2, 1)
    ql_r, ql_i, pb_r, pb_i = _s5_local_scan(dylin, wct_r, wct_i, a_r, -a_i, reverse=True, name="s5_scan_bwd")
    cb_r, cb_i = _s5_carry(ql_r[:NSEG], ql_i[:NSEG], pb_r, pb_i, reverse=True, name="s5_carry_bwd")
    tc = min(S5_TC, T)

    def before_blocks(s):
        last = s.reshape(T // tc, tc, NS)[:, tc - NSEG:, :]
        wrap = jnp.concatenate([jnp.zeros((1, 1, NS), f32), last[-1:, : NSEG - 1, :]], axis=1)
        return jnp.concatenate([wrap, last[:-1]], axis=0)

    du_p, dwb_r, dwb_i, dwc_r, dwc_i, d_ar, d_ai = _s5_fix_bwd(
        ql_r, ql_i, a_r, -a_i, cb_r, cb_i, s_r, s_i, before_blocks(s_r), before_blocks(s_i), u_p, dylin, du_a, wb_r, wb_i, name="s5_fix_bwd")
    unblk = lambda w: _block_diag_t(w, S5_GROUP, S5_STATE).reshape(Gs * S5_GROUP, S5_STATE)
    rsum = jnp.repeat(jnp.eye(Gs, dtype=f32), S5_GROUP, axis=1)
    d_lr, d_li, d_ldt, d_brt, d_bit = _s5_prep_bwd(
        prep_args, (d_ar.reshape(Gs, S5_STATE), d_ai.reshape(Gs, S5_STATE), unblk(dwb_r), unblk(dwb_i)), rsum, name="s5_prep_bwd")
    gs["s5_A_re"], gs["s5_A_im"], gs["s5_log_dt"] = d_lr, d_li, d_ldt.reshape(1, Gs)
    gs["s5_B_re"] = d_brt.reshape(Gs, S5_GROUP, S5_STATE).transpose(0, 2, 1)
    gs["s5_B_im"] = d_bit.reshape(Gs, S5_GROUP, S5_STATE).transpose(0, 2, 1)
    gs["s5_C_re"] = _block_diag_t(dwc_r, S5_STATE, S5_GROUP).transpose(0, 1, 3, 2).reshape(Gs, S5_GROUP, S5_STATE)
    gs["s5_C_im"] = _block_diag_t(dwc_i, S5_STATE, S5_GROUP).transpose(0, 1, 3, 2).reshape(Gs, S5_GROUP, S5_STATE)

    group_c = ("w_proj_s5", "s5_w_glu", "conv_w")
    parts_c = level1(group_c, "c")
    dh2 = _mm(du_p, w_u, tb=True, a_seg=True, name="d_h2_u")
    dh2 = _mm(dz, w_z, tb=True, add=dh2, name="d_h2_z")
    dh2, arr = _mm(dxbc, w_xbc, tb=True, add=dh2, carry=parts_c, name="d_h2_xbc")
    arrived.update(zip(group_c, arr))
    dh2 = _mm(dgl, w_gl, tb=True, add=dh2, name="d_h2_gate")
    dh2 = _mm(ddtr, w_dt, tb=True, add=dh2, name="d_h2_dt")
    d_w_in = [_mm(h2, du_p, ta=True, b_seg=True, name="d_w_u"), _mm(h2, dz, ta=True, name="d_w_z"), _mm(h2, dxbc, ta=True, name="d_w_xbc"),
              _mm(h2, ddtr, ta=True, name="d_w_dt"), _mm(h2, dgl, ta=True, name="d_w_gate")]
    gw["w_in"] = _reshard_w_in(d_w_in, seg_lens, sh["w_in"].shape[1], name="reshard_d_w_in")
    parts_d = level1(("w_in",), "d")
    (dx1,), (gs["mix_norm"],) = _rows_bwd(_f_rmsnorm, [x1], [mix_norm], [dh2], name="mix_norm_bwd", want_rows=[0], adds={0: dx2})
    dx0, gs["ffn1_norm"], gw["ffn1_w_gate"], gw["ffn1_w_up"], gw["ffn1_w_down"], arr = _ffn_bwd(sv1, *ffn1_w, dx1, "ffn1", carry=parts_d)
    arrived["w_in"] = arr[0]
    group_e = ("ffn1_w_gate", "ffn1_w_up", "ffn1_w_down")
    arrived.update(zip(group_e, _exchange_chips(level1(group_e, "e"), name="exchange_chips_e")))

    small_shapes = {k: (P[k][0].shape if P[k].ndim > 1 else P[k].shape) for k in _SMALL}
    pack = lambda d: jnp.concatenate([_pad_flat(d[k], TILE_ELEMS) for k in _SMALL]).reshape(-1, LANES)
    gsmall = _sum_slots(_all_gather([pack(gs)], name="gather_small_grads")[0], name="sum_small_grads")
    snum = {k: math.prod(small_shapes[k]) for k in _SMALL}
    ssz = {k: -(-snum[k] // TILE_ELEMS) * TILE_ELEMS for k in _SMALL}

    grads, delta, new_m, new_v = {}, {}, {}, {}
    for k in _BIG:
        grads[k], delta[k], new_m[k], new_v[k] = _reduce_adamw(
            chip_sums[k], arrived[k], chip, P[k][0], P["m_" + k][0], P["v_" + k][0], name="adamw_" + k)
    d_s, m_s, v_s = _adamw(gsmall, pack({k: P[k] for k in _SMALL}), pack({k: P["m_" + k] for k in _SMALL}),
                           pack({k: P["v_" + k] for k in _SMALL}), name="adamw_small")
    off = 0
    gflat, dflat, mflat, vflat = gsmall.reshape(-1), d_s.reshape(-1), m_s.reshape(-1), v_s.reshape(-1)
    for k in _SMALL:
        n = snum[k]
        grads[k], delta[k], new_m[k], new_v[k] = (a[off:off + n] for a in (gflat, dflat, mflat, vflat))
        off += ssz[k]

    loss = lax.psum(lossv[0, 0], ("x", "y", "c"))
    out = [loss, dx0.reshape(x.shape)]
    for d in (grads, delta, new_m, new_v):
        out += [d[k].reshape(P[k].shape) for k in _WEIGHTS]
    return tuple(out)
```

```python
import math
from typing import Callable, NamedTuple

import jax
import jax.numpy as jnp
from jax import lax
from jax.experimental import pallas as pl
from jax.experimental.pallas import tpu as pltpu

f32 = jnp.float32
bf16 = jnp.bfloat16
_S = jax.ShapeDtypeStruct

EPS = 1e-6
S5_GROUP = 16
S5_STATE = 64
HEADDIM = 64
SSD_STATE = 128
CHUNK = 64
CONV_K = 4
NSEG = 8
S5_GPB = 16
N_DEV = 8
LANES = 128
TILE_ELEMS = 8 * LANES

ADAM_LR = 0.001
ADAM_B1 = 0.9
ADAM_B2 = 0.999
ADAM_EPS = 1e-08
ADAM_WD = 0.01
ADAM_STEP = 10

VMEM_LIMIT = 56 * 1024 * 1024
MM_FULL_K = 3072
MM_MAX_TN = 3072
MESH = pl.DeviceIdType.MESH


def _cparams(sem=None):
    return pltpu.CompilerParams(dimension_semantics=sem, vmem_limit_bytes=VMEM_LIMIT)


def _pick(dim, pref, align=LANES):
    best = None
    t = align
    while t <= min(dim, pref):
        if dim % t == 0:
            best = t
        t += align
    return best or dim


def _slot_of(k):
    return (k & 1) * (N_DEV // 2) + (k >> 1)


def _mm(a, b, *, name, ta=False, tb=False, a_blk=None, b_blk=None, o_blk=None, o_slots=False, a_seg=False, b_seg=False,
        o_seg=False, tm=None, out_dtype=f32, scale=1.0, add=None, carry=None):
    a2, b2 = a.shape[-2:], b.shape[-2:]
    Ma, Ka = (a2[1], a2[0]) if ta else a2
    Kb, Nb = (b2[1], b2[0]) if tb else b2
    M = Ma * (a.shape[0] if a_blk == "m" else 1)
    K = Ka * (a.shape[0] if a_blk == "k" else 1)
    N = Nb * (b.shape[0] if b_blk == "n" else 1)
    assert K == Kb * (b.shape[0] if b_blk == "k" else 1), (a.shape, b.shape, ta, tb, a_blk, b_blk)
    assert (a.ndim == 3) == (a_blk is not None) and (b.ndim == 3) == (b_blk is not None)
    tm = Ma if a_blk == "m" else (tm or _pick(M, 512))
    tn = Nb if b_blk == "n" else _pick(N, MM_MAX_TN)
    if a_blk == "k" or b_blk == "k":
        tk = Ka if a_blk == "k" else Kb
        assert tk == (Kb if b_blk == "k" else tk)
    else:
        tk = K if K <= MM_FULL_K else _pick(K, 1024 if ta else MM_FULL_K)
    if (a_seg and not ta) or o_seg:
        tm = M // NSEG
    if (a_seg and ta) or b_seg:
        tk = K // NSEG
    gm, gn, nk = M // tm, N // tn, K // tk
    assert not (add is not None and (o_seg or o_blk)) and not (o_blk and o_seg)

    if a_seg:
        assert a.ndim == 2
        a = a.reshape(a.shape[0] // NSEG, NSEG * a.shape[1])
        if ta:
            a_spec = pl.BlockSpec((tk, tm), lambda i, j, k: (0, k * (Ma // tm) + i))
        else:
            a_spec = pl.BlockSpec((tm, tk), lambda i, j, k: (0, i * (Ka // tk) + k))
    elif a.ndim == 3:
        lead = (lambda i, k: i) if a_blk == "m" else (lambda i, k: k)
        if ta:
            a_spec = pl.BlockSpec((None, tk, tm), lambda i, j, k: (lead(i, k), 0 if a_blk == "k" else k, 0 if a_blk == "m" else i))
        else:
            a_spec = pl.BlockSpec((None, tm, tk), lambda i, j, k: (lead(i, k), 0 if a_blk == "m" else i, 0 if a_blk == "k" else k))
    else:
        a_spec = pl.BlockSpec((tk, tm), lambda i, j, k: (k, i)) if ta else pl.BlockSpec((tm, tk), lambda i, j, k: (i, k))
    if b_seg:
        assert b.ndim == 2 and not tb
        b = b.reshape(b.shape[0] // NSEG, NSEG * b.shape[1])
        b_spec = pl.BlockSpec((tk, tn), lambda i, j, k: (0, k * (Nb // tn) + j))
    elif b.ndim == 3:
        lead = (lambda j, k: j) if b_blk == "n" else (lambda j, k: k)
        if tb:
            b_spec = pl.BlockSpec((None, tn, tk), lambda i, j, k: (lead(j, k), 0 if b_blk == "n" else j, 0 if b_blk == "k" else k))
        else:
            b_spec = pl.BlockSpec((None, tk, tn), lambda i, j, k: (lead(j, k), 0 if b_blk == "k" else k, 0 if b_blk == "n" else j))
    else:
        b_spec = pl.BlockSpec((tn, tk), lambda i, j, k: (j, k)) if tb else pl.BlockSpec((tk, tn), lambda i, j, k: (k, j))
    slot = _slot_of if o_slots else (lambda k: k)
    if o_blk == "n":
        assert gn == N_DEV or not o_slots
        o_shape, o_spec = (gn, M, tn), pl.BlockSpec((None, tm, tn), lambda i, j, k: (slot(j), i, 0))
    elif o_blk == "m" and o_slots and gm < N_DEV:
        rs = M // N_DEV
        per_tile = tm // rs
        assert per_tile % 2 == 0 and tm % rs == 0
        o_shape = (2, N_CHIP, rs, N)
        o_spec = pl.BlockSpec((2, per_tile // 2, rs, tn), lambda i, j, k: (0, i, 0, j))
    elif o_blk == "m":
        assert gm == N_DEV or not o_slots
        o_shape, o_spec = (gm, tm, N), pl.BlockSpec((None, tm, tn), lambda i, j, k: (slot(i), 0, j))
    elif o_seg:
        o_shape, o_spec = (tm, NSEG * N), pl.BlockSpec((tm, tn), lambda i, j, k: (0, i * (N // tn) + j))
    else:
        o_shape, o_spec = (M, N), pl.BlockSpec((tm, tn), lambda i, j, k: (i, j))
    dims = (((0 if ta else 1,), (1 if tb else 0,)), ((), ()))
    has_add = add is not None

    carry = carry or _NO_CARRY
    n_in = 2 + has_add

    def body(*refs):
        own, c_in, c_out, c_sems = _carry_split(carry, refs, n_in, 1)
        a_ref, b_ref = own[0], own[1]
        add_ref = own[2] if has_add else None
        o_ref, acc_ref = own[-2], own[-1]
        i, j, k = pl.program_id(0), pl.program_id(1), pl.program_id(2)
        _carry_start(carry, c_in, c_out, c_sems, (i == 0) & (j == 0) & (k == 0))

        @pl.when(k == 0)
        def _():
            acc_ref[...] = jnp.zeros_like(acc_ref)

        acc_ref[...] += lax.dot_general(a_ref[...].astype(bf16), b_ref[...].astype(bf16), dims, preferred_element_type=f32)

        @pl.when(k == nk - 1)
        def _():
            r = acc_ref[...] * scale
            if has_add:
                r = r + add_ref[...].astype(f32)
            if len(o_shape) == 4:
                rs = o_shape[2]
                for chip_l in range(o_ref.shape[1]):
                    for core in range(2):
                        dev = 2 * chip_l + core
                        o_ref[core, chip_l] = r[dev * rs:(dev + 1) * rs].astype(out_dtype)
            else:
                o_ref[...] = r.astype(out_dtype)

        _carry_finish(carry, c_in, c_out, c_sems, (i == gm - 1) & (j == gn - 1) & (k == nk - 1))

    ins = [a, b] + ([add] if has_add else []) + list(carry.ins)
    in_specs = [a_spec, b_spec] + ([o_spec] if has_add else []) + [_ANY] * len(carry.ins)
    res = pl.pallas_call(
        body, name=name, grid=(gm, gn, nk), in_specs=in_specs, out_specs=[o_spec] + [_ANY] * len(carry.out_shapes),
        out_shape=[_S(o_shape, out_dtype)] + list(carry.out_shapes),
        scratch_shapes=[pltpu.VMEM((tm, tn), f32)] + list(carry.sems),
        compiler_params=_cparams(("arbitrary",) * 3 if carry.ins else ("parallel", "parallel", "arbitrary")),
    )(*ins)
    out = res[0]
    if len(o_shape) == 4:
        out = out.reshape(N_DEV, o_shape[2], N)
    elif o_seg:
        out = out.reshape(M, N)
    return (out, list(res[1:])) if carry.ins else out


def _row_tile(T, widths):
    budget = 6 * 1024 * 1024
    tb = max(16, budget // (4 * sum(widths)))
    return _pick(T, tb, align=16)


def _rows(fn, rows, params, outs, *, name):
    T = rows[0].shape[0]
    nr, npar = len(rows), len(params)
    tb = _row_tile(T, [r.shape[1] for r in rows] + [w for w, _ in outs])

    def body(*refs):
        ins = [r[...].astype(f32) for r in refs[: nr + npar]]
        res = fn(*ins)
        for o_ref, r in zip(refs[nr + npar:], res):
            o_ref[...] = r.astype(o_ref.dtype)

    in_specs = [pl.BlockSpec((tb, r.shape[1]), lambda i: (i, 0)) for r in rows]
    in_specs += [pl.BlockSpec(p.shape, lambda i: (0, 0)) for p in params]
    out_specs = [pl.BlockSpec((tb, w), lambda i: (i, 0)) for w, _ in outs]
    res = pl.pallas_call(
        body, name=name, grid=(T // tb,), in_specs=in_specs, out_specs=out_specs,
        out_shape=[_S((T, w), d) for w, d in outs], compiler_params=_cparams(("parallel",)),
    )(*rows, *params)
    return tuple(res)


def _rows_bwd(fn, rows, params, cots, *, name, want_rows, row_dtypes=None, adds=None):
    T = rows[0].shape[0]
    nr, npar, nc = len(rows), len(params), len(cots)
    adds = adds or {}
    add_idx = sorted(adds)
    row_dtypes = row_dtypes or {}
    widths = [r.shape[1] for r in rows] + [c.shape[1] for c in cots] + [rows[i].shape[1] for i in want_rows]
    tb = _row_tile(T, widths)

    def body(*refs):
        ins = [r[...].astype(f32) for r in refs[: nr + npar]]
        cot = tuple(r[...].astype(f32) for r in refs[nr + npar: nr + npar + nc])
        add_refs = refs[nr + npar + nc: nr + npar + nc + len(add_idx)]
        out_refs = refs[nr + npar + nc + len(add_idx):]
        _, vjp = jax.vjp(lambda *a: tuple(fn(*a)), *ins)
        g = vjp(cot)
        for o_ref, i in zip(out_refs[: len(want_rows)], want_rows):
            r = g[i]
            if i in adds:
                r = r + add_refs[add_idx.index(i)][...].astype(f32)
            o_ref[...] = r.astype(o_ref.dtype)
        first = pl.program_id(0) == 0
        for o_ref, gp in zip(out_refs[len(want_rows):], g[nr:]):
            @pl.when(first)
            def _(o_ref=o_ref):
                o_ref[...] = jnp.zeros_like(o_ref)

            o_ref[...] += gp

    in_specs = [pl.BlockSpec((tb, r.shape[1]), lambda i: (i, 0)) for r in rows]
    in_specs += [pl.BlockSpec(p.shape, lambda i: (0, 0)) for p in params]
    in_specs += [pl.BlockSpec((tb, c.shape[1]), lambda i: (i, 0)) for c in cots]
    in_specs += [pl.BlockSpec((tb, adds[i].shape[1]), lambda i_: (i_, 0)) for i in add_idx]
    out_specs = [pl.BlockSpec((tb, rows[i].shape[1]), lambda i_: (i_, 0)) for i in want_rows]
    out_specs += [pl.BlockSpec(p.shape, lambda i: (0, 0)) for p in params]
    out_shape = [_S(rows[i].shape, row_dtypes.get(i, f32)) for i in want_rows] + [_S(p.shape, f32) for p in params]
    res = pl.pallas_call(
        body, name=name, grid=(T // tb,), in_specs=in_specs, out_specs=out_specs, out_shape=out_shape,
        compiler_params=_cparams(("arbitrary",)),
    )(*rows, *params, *cots, *[adds[i] for i in add_idx])
    return list(res[: len(want_rows)]), list(res[len(want_rows):])


def _f_rmsnorm(x, g):
    return (x * lax.rsqrt(jnp.mean(x * x, axis=-1, keepdims=True) + EPS) * g,)


def _f_swiglu(ab):
    F = ab.shape[1] // 2
    return (jax.nn.silu(ab[:, :F]) * ab[:, F:],)


def _f_s5_post(y, u, d):
    return (jax.nn.gelu(y + d * u),)


def _f_glu(g, v, b):
    return (g * jax.nn.sigmoid(v + b),)


def _f_gated_norm(y, z, w):
    return _f_rmsnorm(y * jax.nn.silu(z), w)


def _f_merge(gl, p5, pssd, b):
    D = p5.shape[1]
    gates = jax.nn.sigmoid(gl + b)
    return (gates[:, :D] * p5 + gates[:, D:] * pssd,)


def _f_dt(dtr, bias, a_log):
    dt = jax.nn.softplus(dtr + bias)
    return dt, dt * (-jnp.exp(a_log))


def _loss_stage(x, tgt, g, *, name):
    T, D = x.shape
    tb = _row_tile(T, [D, D, D])

    def f(xb, gb, tb_):
        y = _f_rmsnorm(xb, gb)[0]
        return 0.5 * jnp.sum(jnp.mean(jnp.square(y - tb_), axis=-1, keepdims=True), axis=0, keepdims=True)

    def body(x_ref, t_ref, g_ref, l_ref, dx_ref, dg_ref):
        tv = t_ref[...]
        val, vjp = jax.vjp(lambda a, b: f(a, b, tv), x_ref[...], g_ref[...])
        dx, dg = vjp(jnp.ones((1, 1), f32))
        dx_ref[...] = dx

        @pl.when(pl.program_id(0) == 0)
        def _():
            l_ref[...] = jnp.zeros_like(l_ref)
            dg_ref[...] = jnp.zeros_like(dg_ref)

        l_ref[...] += jnp.broadcast_to(val, l_ref.shape)
        dg_ref[...] += dg

    row = pl.BlockSpec((tb, D), lambda i: (i, 0))
    par = pl.BlockSpec((1, D), lambda i: (0, 0))
    return pl.pallas_call(
        body, name=name, grid=(T // tb,), in_specs=[row, row, par],
        out_specs=[pl.BlockSpec((1, LANES), lambda i: (0, 0)), row, par],
        out_shape=[_S((1, LANES), f32), _S((T, D), f32), _S((1, D), f32)], compiler_params=_cparams(("arbitrary",)),
    )(x, tgt, g)


def _shift_down(x, s):
    if s == 0:
        return x
    t = lax.broadcasted_iota(jnp.int32, x.shape, 0)
    return jnp.where(t >= s, pltpu.roll(x, s, axis=0), 0.0)


def _shift_up(x, s):
    if s == 0:
        return x
    T = x.shape[0]
    t = lax.broadcasted_iota(jnp.int32, x.shape, 0)
    return jnp.where(t < T - s, pltpu.roll(x, T - s, axis=0), 0.0)


def _conv_pre(x, w, b):
    pre = b
    for k in range(CONV_K):
        pre = pre + w[k:k + 1, :] * _shift_down(x, CONV_K - 1 - k)
    return pre


def _conv_fwd(x, w, b, *, name):
    T, C = x.shape
    cb = _pick(C, 256)

    def body(x_ref, w_ref, b_ref, o_ref):
        o_ref[...] = jax.nn.silu(_conv_pre(x_ref[...], w_ref[...], b_ref[...]))

    col = pl.BlockSpec((T, cb), lambda j: (0, j))
    return pl.pallas_call(
        body, name=name, grid=(C // cb,), in_specs=[col, pl.BlockSpec((CONV_K, cb), lambda j: (0, j)), pl.BlockSpec((1, cb), lambda j: (0, j))],
        out_specs=col, out_shape=_S((T, C), f32), compiler_params=_cparams(("parallel",)),
    )(x, w, b)


def _conv_bwd(x, w, b, dy, *, name, carry=None):
    T, C = x.shape
    cb = _pick(C, 256)
    carry = carry or _NO_CARRY

    def body(*refs):
        (x_ref, w_ref, b_ref, dy_ref, dx_ref, dw_ref, db_ref), c_in, c_out, c_sems = _carry_split(carry, refs, 4, 3)
        _carry_start(carry, c_in, c_out, c_sems, pl.program_id(0) == 0)
        xv, wv = x_ref[...], w_ref[...]
        pre = _conv_pre(xv, wv, b_ref[...])
        sg = jax.nn.sigmoid(pre)
        dpre = dy_ref[...] * sg * (1.0 + pre * (1.0 - sg))
        dx = jnp.zeros_like(xv)
        for k in range(CONV_K):
            s = CONV_K - 1 - k
            dx = dx + wv[k:k + 1, :] * _shift_up(dpre, s)
            dw_ref[k:k + 1, :] = jnp.sum(dpre * _shift_down(xv, s), axis=0, keepdims=True)
        dx_ref[...] = dx
        db_ref[...] = jnp.sum(dpre, axis=0, keepdims=True)
        _carry_finish(carry, c_in, c_out, c_sems, pl.program_id(0) == C // cb - 1)

    col = pl.BlockSpec((T, cb), lambda j: (0, j))
    wsp = pl.BlockSpec((CONV_K, cb), lambda j: (0, j))
    bsp = pl.BlockSpec((1, cb), lambda j: (0, j))
    res = pl.pallas_call(
        body, name=name, grid=(C // cb,), in_specs=[col, wsp, bsp, col] + [_ANY] * len(carry.ins),
        out_specs=[col, wsp, bsp] + [_ANY] * len(carry.out_shapes),
        out_shape=[_S((T, C), f32), _S((CONV_K, C), f32), _S((1, C), f32)] + list(carry.out_shapes),
        scratch_shapes=list(carry.sems), compiler_params=_cparams(("arbitrary",) if carry.ins else ("parallel",)),
    )(x, w, b, dy, *carry.ins)
    return (*res[:3], list(res[3:]))


def _f_s5_prep(lr, li, ldt, lrb, lib, ldtb, brt, bit):
    def disc(lr_, li_, ldt_):
        dt = jnp.exp(ldt_)
        mag = jnp.exp(lr_ * dt)
        ar, ai = mag * jnp.cos(li_ * dt), mag * jnp.sin(li_ * dt)
        den = lr_ * lr_ + li_ * li_
        cr = ((ar - 1.0) * lr_ + ai * li_) / den
        ci = (ai * lr_ - (ar - 1.0) * li_) / den
        return ar, ai, cr, ci

    ar, ai, _, _ = disc(lr, li, ldt)
    _, _, cr, ci = disc(lrb, lib, ldtb)
    return ar, ai, cr * brt - ci * bit, cr * bit + ci * brt


def _s5_prep(args, *, name):
    G, N = args[0].shape
    GM = args[3].shape[0]

    def body(*refs):
        res = _f_s5_prep(*[r[...] for r in refs[:8]])
        for o, r in zip(refs[8:], res):
            o[...] = r

    return pl.pallas_call(body, name=name, out_shape=[_S((G, N), f32)] * 2 + [_S((GM, N), f32)] * 2)(*args)


def _s5_prep_bwd(args, cots, rsum, *, name):
    G, N = args[0].shape
    GM = args[3].shape[0]

    def body(*refs):
        ins = [r[...] for r in refs[:8]]
        cot = tuple(r[...] for r in refs[8:12])
        rs = refs[12][...]
        _, vjp = jax.vjp(_f_s5_prep, *ins)
        g = vjp(cot)
        fold = lambda v: jnp.dot(rs, v, preferred_element_type=f32, precision=lax.Precision.HIGHEST)
        o = refs[13:]
        o[0][...] = g[0] + fold(g[3])
        o[1][...] = g[1] + fold(g[4])
        o[2][...] = g[2] + fold(jnp.broadcast_to(g[5], (GM, LANES)))[:, 0:1]
        o[3][...] = g[6]
        o[4][...] = g[7]

    return pl.pallas_call(
        body, name=name, out_shape=[_S((G, N), f32), _S((G, N), f32), _S((G, 1), f32), _S((GM, N), f32), _S((GM, N), f32)],
    )(*args, *cots, rsum)


S5_TC = 512


def _s5_local_scan(src, w_r, w_i, a_r, a_i, *, reverse, name, carry=None):
    T, C = src.shape
    nblk, cb, sb = w_r.shape
    NS = nblk * sb
    tc = min(S5_TC, T)
    nT, nt = T // tc, tc // NSEG
    tmap = (lambda i: nT - 1 - i) if reverse else (lambda i: i)

    carry = carry or _NO_CARRY

    def body(*refs):
        own, c_in, c_out, c_sems = _carry_split(carry, refs, 5, 4)
        u_ref, wr_ref, wi_ref, ar_ref, ai_ref, sr_ref, si_ref, pr_ref, pi_ref, st_r, st_i, pw_r, pw_i = own
        _carry_start(carry, c_in, c_out, c_sems, (pl.program_id(0) == 0) & (pl.program_id(1) == 0))

        @pl.when(pl.program_id(1) == 0)
        def _():
            st_r[...] = jnp.zeros_like(st_r)
            st_i[...] = jnp.zeros_like(st_i)
            pw_r[...] = jnp.ones_like(pw_r)
            pw_i[...] = jnp.zeros_like(pw_i)

        u = u_ref[...].astype(bf16)
        sr_ref[...] = jnp.dot(u, wr_ref[...], preferred_element_type=f32)
        si_ref[...] = jnp.dot(u, wi_ref[...], preferred_element_type=f32)
        ar = jnp.broadcast_to(ar_ref[...], (NSEG, sb))
        ai = jnp.broadcast_to(ai_ref[...], (NSEG, sb))

        def step(k, c):
            cr, ci, qr, qi = c
            kk = (nt - 1 - k) if reverse else k
            rows = pl.ds(pl.multiple_of(kk * NSEG, NSEG), NSEG)
            nr = ar * cr - ai * ci + sr_ref[rows, :]
            ni = ar * ci + ai * cr + si_ref[rows, :]
            sr_ref[rows, :] = nr
            si_ref[rows, :] = ni
            return nr, ni, ar * qr - ai * qi, ar * qi + ai * qr

        cr, ci, qr, qi = lax.fori_loop(0, nt, step, (st_r[...], st_i[...], pw_r[...], pw_i[...]), unroll=8)
        st_r[...], st_i[...], pw_r[...], pw_i[...] = cr, ci, qr, qi
        pr_ref[...] = qr
        pi_ref[...] = qi
        _carry_finish(carry, c_in, c_out, c_sems, (pl.program_id(0) == nblk - 1) & (pl.program_id(1) == nT - 1))

    blk = pl.BlockSpec((tc, sb), lambda j, i: (tmap(i), j))
    wsp = pl.BlockSpec((None, cb, sb), lambda j, i: (j, 0, 0))
    asp = pl.BlockSpec((1, sb), lambda j, i: (0, j))
    psp = pl.BlockSpec((NSEG, sb), lambda j, i: (0, j))
    res = pl.pallas_call(
        body, name=name, grid=(nblk, nT),
        in_specs=[pl.BlockSpec((tc, cb), lambda j, i: (tmap(i), j)), wsp, wsp, asp, asp] + [_ANY] * len(carry.ins),
        out_specs=[blk, blk, psp, psp] + [_ANY] * len(carry.out_shapes),
        out_shape=[_S((T, NS), f32)] * 2 + [_S((NSEG, NS), f32)] * 2 + list(carry.out_shapes),
        scratch_shapes=[pltpu.VMEM((NSEG, sb), f32)] * 4 + list(carry.sems),
        compiler_params=_cparams(("arbitrary", "arbitrary") if carry.ins else ("parallel", "arbitrary")),
    )(src, w_r, w_i, a_r, a_i, *carry.ins)
    return (*res[:4], list(res[4:])) if carry.ins else res


def _s5_carry(e_r, e_i, p_r, p_i, *, reverse, name):
    NS = e_r.shape[1]

    def body(er_ref, ei_ref, pr_ref, pi_ref, cr_ref, ci_ref):
        ar, ai = pr_ref[0:1, :], pi_ref[0:1, :]
        cr = jnp.zeros((1, NS), f32)
        ci = jnp.zeros((1, NS), f32)
        order = list(range(NSEG - 1, -1, -1)) if reverse else list(range(NSEG))
        cr_ref[order[0]:order[0] + 1, :] = cr
        ci_ref[order[0]:order[0] + 1, :] = ci
        for prev, q in zip(order[:-1], order[1:]):
            er, ei = er_ref[prev:prev + 1, :], ei_ref[prev:prev + 1, :]
            cr, ci = er + ar * cr - ai * ci, ei + ar * ci + ai * cr
            cr_ref[q:q + 1, :] = cr
            ci_ref[q:q + 1, :] = ci

    return pl.pallas_call(body, name=name, out_shape=[_S((NSEG, NS), f32)] * 2)(e_r, e_i, p_r, p_i)


def _s5_fix_out(sl_r, sl_i, a_r, a_i, c_r, c_i, wc_r, wc_i, *, name):
    T, NS = sl_r.shape
    nblk, sb, cb = wc_r.shape
    tc = min(S5_TC, T)
    nT, nt = T // tc, tc // NSEG

    def body(lr_ref, li_ref, ar_ref, ai_ref, cr_ref, ci_ref, wr_ref, wi_ref, sr_ref, si_ref, y_ref, pw_r, pw_i):
        @pl.when(pl.program_id(1) == 0)
        def _():
            pw_r[...] = jnp.ones_like(pw_r)
            pw_i[...] = jnp.zeros_like(pw_i)

        ar = jnp.broadcast_to(ar_ref[...], (NSEG, sb))
        ai = jnp.broadcast_to(ai_ref[...], (NSEG, sb))
        cr, ci = cr_ref[...], ci_ref[...]

        def step(k, c):
            qr, qi = c
            qr, qi = ar * qr - ai * qi, ar * qi + ai * qr
            rows = pl.ds(pl.multiple_of(k * NSEG, NSEG), NSEG)
            sr_ref[rows, :] = lr_ref[rows, :] + qr * cr - qi * ci
            si_ref[rows, :] = li_ref[rows, :] + qr * ci + qi * cr
            return qr, qi

        qr, qi = lax.fori_loop(0, nt, step, (pw_r[...], pw_i[...]), unroll=8)
        pw_r[...], pw_i[...] = qr, qi
        y_ref[...] = (jnp.dot(sr_ref[...].astype(bf16), wr_ref[...], preferred_element_type=f32)
                      - jnp.dot(si_ref[...].astype(bf16), wi_ref[...], preferred_element_type=f32))

    blk = pl.BlockSpec((tc, sb), lambda j, i: (i, j))
    asp = pl.BlockSpec((1, sb), lambda j, i: (0, j))
    csp = pl.BlockSpec((NSEG, sb), lambda j, i: (0, j))
    wsp = pl.BlockSpec((None, sb, cb), lambda j, i: (j, 0, 0))
    return pl.pallas_call(
        body, name=name, grid=(nblk, nT), in_specs=[blk, blk, asp, asp, csp, csp, wsp, wsp],
        out_specs=[blk, blk, pl.BlockSpec((tc, cb), lambda j, i: (i, j))],
        out_shape=[_S((T, NS), f32)] * 2 + [_S((T, nblk * cb), f32)],
        scratch_shapes=[pltpu.VMEM((NSEG, sb), f32)] * 2, compiler_params=_cparams(("parallel", "arbitrary")),
    )(sl_r, sl_i, a_r, a_i, c_r, c_i, wc_r, wc_i)


def _s5_fix_bwd(ql_r, ql_i, ab_r, ab_i, c_r, c_i, s_r, s_i, sb_r, sb_i, u, dy, du_add, w_r, w_i, *, name):
    T, NS = ql_r.shape
    nblk, cb, sb = w_r.shape
    tc = min(S5_TC, T)
    nT, nt = T // tc, tc // NSEG
    tmap = lambda i: nT - 1 - i

    def body(lr_ref, li_ref, ar_ref, ai_ref, cr_ref, ci_ref, sr_ref, si_ref, br_ref, bi_ref, u_ref, dy_ref, dua_ref, wr_ref, wi_ref,
             du_ref, dwr_ref, dwi_ref, dcr_ref, dci_ref, dar_ref, dai_ref, pw_r, pw_i, ac_r, ac_i, q_r, q_i):
        first = pl.program_id(1) == 0

        @pl.when(first)
        def _():
            pw_r[...] = jnp.ones_like(pw_r)
            pw_i[...] = jnp.zeros_like(pw_i)
            ac_r[...] = jnp.zeros_like(ac_r)
            ac_i[...] = jnp.zeros_like(ac_i)
            dwr_ref[...] = jnp.zeros_like(dwr_ref)
            dwi_ref[...] = jnp.zeros_like(dwi_ref)
            dcr_ref[...] = jnp.zeros_like(dcr_ref)
            dci_ref[...] = jnp.zeros_like(dci_ref)

        ar = jnp.broadcast_to(ar_ref[...], (NSEG, sb))
        ai = jnp.broadcast_to(ai_ref[...], (NSEG, sb))
        cr, ci = cr_ref[...], ci_ref[...]

        def fix(rows, qr, qi, spr, spi, accr, acci):
            qr, qi = ar * qr - ai * qi, ar * qi + ai * qr
            xr = lr_ref[rows, :] + qr * cr - qi * ci
            xi = li_ref[rows, :] + qr * ci + qi * cr
            q_r[rows, :] = xr
            q_i[rows, :] = xi
            return qr, qi, accr + xr * spr + xi * spi, acci + xi * spr - xr * spi

        def step(k, c):
            qr, qi, accr, acci = c
            kk = nt - 1 - k
            rows = pl.ds(pl.multiple_of(kk * NSEG, NSEG), NSEG)
            prev = pl.ds(pl.multiple_of((kk - 1) * NSEG, NSEG), NSEG)
            return fix(rows, qr, qi, sr_ref[prev, :], si_ref[prev, :], accr, acci)

        c = lax.fori_loop(0, nt - 1, step, (pw_r[...], pw_i[...], ac_r[...], ac_i[...]), unroll=7)
        qr, qi, accr, acci = fix(pl.ds(0, NSEG), *c[:2], br_ref[...], bi_ref[...], *c[2:])
        pw_r[...], pw_i[...], ac_r[...], ac_i[...] = qr, qi, accr, acci

        qrb, qib = q_r[...].astype(bf16), q_i[...].astype(bf16)
        nt_dims = (((1,), (1,)), ((), ()))
        tn_dims = (((0,), (0,)), ((), ()))
        du_ref[...] = (dua_ref[...] + lax.dot_general(qrb, wr_ref[...], nt_dims, preferred_element_type=f32)
                       + lax.dot_general(qib, wi_ref[...], nt_dims, preferred_element_type=f32))
        ub = u_ref[...].astype(bf16)
        dwr_ref[...] += lax.dot_general(ub, qrb, tn_dims, preferred_element_type=f32)
        dwi_ref[...] += lax.dot_general(ub, qib, tn_dims, preferred_element_type=f32)
        dyb = dy_ref[...].astype(bf16)
        dcr_ref[...] += lax.dot_general(sr_ref[...].astype(bf16), dyb, tn_dims, preferred_element_type=f32)
        dci_ref[...] -= lax.dot_general(si_ref[...].astype(bf16), dyb, tn_dims, preferred_element_type=f32)

        @pl.when(pl.program_id(1) == nT - 1)
        def _():
            dar_ref[...] = jnp.sum(accr, axis=0, keepdims=True)
            dai_ref[...] = jnp.sum(acci, axis=0, keepdims=True)

    blk = pl.BlockSpec((tc, sb), lambda j, i: (tmap(i), j))
    asp = pl.BlockSpec((1, sb), lambda j, i: (0, j))
    csp = pl.BlockSpec((NSEG, sb), lambda j, i: (0, j))
    bsp = pl.BlockSpec((None, NSEG, sb), lambda j, i: (tmap(i), 0, j))
    chn = pl.BlockSpec((tc, cb), lambda j, i: (tmap(i), j))
    wsp = pl.BlockSpec((None, cb, sb), lambda j, i: (j, 0, 0))
    wcs = pl.BlockSpec((None, sb, cb), lambda j, i: (j, 0, 0))
    return pl.pallas_call(
        body, name=name, grid=(nblk, nT), in_specs=[blk, blk, asp, asp, csp, csp, blk, blk, bsp, bsp, chn, chn, chn, wsp, wsp],
        out_specs=[chn, wsp, wsp, wcs, wcs, asp, asp],
        out_shape=[_S((T, nblk * cb), f32), _S((nblk, cb, sb), f32), _S((nblk, cb, sb), f32), _S((nblk, sb, cb), f32),
                   _S((nblk, sb, cb), f32), _S((1, NS), f32), _S((1, NS), f32)],
        scratch_shapes=[pltpu.VMEM((NSEG, sb), f32)] * 4 + [pltpu.VMEM((tc, sb), f32)] * 2,
        compiler_params=_cparams(("parallel", "arbitrary")),
    )(ql_r, ql_i, ab_r, ab_i, c_r, c_i, s_r, s_i, sb_r, sb_i, u, dy, du_add, w_r, w_i)


SSD2_TB = 512
_HI = lax.Precision.HIGHEST
_NN = (((1,), (0,)), ((), ()))
_NT = (((1,), (1,)), ((), ()))
_TN = (((0,), (0,)), ((), ()))


def _dotf(a, b, dims):
    return lax.dot_general(a.astype(bf16), b.astype(bf16), dims, preferred_element_type=f32)


def _doth(a, b, dims=_NN):
    return lax.dot_general(a, b, dims, preferred_element_type=f32, precision=_HI)


def _ssd_consts(hpg):
    W = hpg * CHUNK
    i = lax.broadcasted_iota(jnp.int32, (CHUNK, CHUNK), 0)
    j = lax.broadcasted_iota(jnp.int32, (CHUNK, CHUNK), 1)
    tril = (i >= j).astype(f32)
    r = lax.broadcasted_iota(jnp.int32, (W, W), 0)
    c = lax.broadcasted_iota(jnp.int32, (W, W), 1)
    bd = (r // CHUNK == c // CHUNK).astype(f32)
    triu_bd = bd * (r <= c).astype(f32)
    e_r = lax.broadcasted_iota(jnp.int32, (W, LANES), 0)
    e_c = lax.broadcasted_iota(jnp.int32, (W, LANES), 1)
    ered = (e_r // HEADDIM == e_c).astype(f32)
    return tril, jnp.tile(tril, (1, hpg)), bd, triu_bd, ered


def _ssd2_specs(G, hpg, tb, tmap, b_off, c_off):
    W = hpg * HEADDIM
    ncb = tb // CHUNK
    xsp = pl.BlockSpec((tb, W), lambda g, i: (tmap(i), g))
    bsp = pl.BlockSpec((tb, SSD_STATE), lambda g, i: (tmap(i), b_off + g))
    csp = pl.BlockSpec((tb, SSD_STATE), lambda g, i: (tmap(i), c_off + g))
    rsp = pl.BlockSpec((None, ncb, W), lambda g, i: (g, tmap(i), 0))
    dsp = pl.BlockSpec((1, W), lambda g, i: (0, g))
    hsp = pl.BlockSpec((None, ncb, SSD_STATE, W), lambda g, i: (g, tmap(i), 0, 0))
    const = lambda a: pl.BlockSpec(a.shape, lambda g, i: (0, 0))
    return xsp, bsp, csp, rsp, dsp, hsp, const


def _tile_rows(a, n):
    return jnp.concatenate([a] * n, axis=0)


def _ssd2_fwd(xc, dt4, a4, dtw, aw, d4, consts, *, d_inner, name, carry=None):
    carry = carry or _NO_CARRY
    T = xc.shape[0]
    G, nc, W = dtw.shape
    hpg = W // CHUNK
    tb = min(SSD2_TB, T)
    nb, ncb = T // tb, tb // CHUNK
    b_off = d_inner // SSD_STATE
    xsp, bsp, csp, rsp, dsp, hsp, const = _ssd2_specs(G, hpg, tb, lambda i: i, b_off, b_off + G)
    tril, mask4, bd, triu_bd, _ = consts

    def body(*refs):
        own, c_in, c_out, c_sems = _carry_split(carry, refs, 12, 2)
        x_ref, b_ref, c_ref, dt_ref, a_ref, dtw_ref, aw_ref, d_ref, tril_ref, mask_ref, bd_ref, tbd_ref, y_ref, hs_ref, h_scr = own
        _carry_start(carry, c_in, c_out, c_sems, (pl.program_id(0) == 0) & (pl.program_id(1) == 0))

        @pl.when(pl.program_id(1) == 0)
        def _():
            h_scr[...] = jnp.zeros_like(h_scr)

        acs_rows = _doth(aw_ref[...], tbd_ref[...])
        ht = h_scr[...]
        for c in range(ncb):
            rows = slice(c * CHUNK, (c + 1) * CHUNK)
            x, bm, cm = x_ref[rows, :], b_ref[rows, :], c_ref[rows, :]
            acs = _doth(tril_ref[...], a_ref[rows, :])
            lmat = jnp.where(mask_ref[...] > 0, jnp.exp(jnp.minimum(acs - acs_rows[c:c + 1, :], 0.0)), 0.0)
            m4 = _dotf(cm, _tile_rows(bm, hpg), _NT) * lmat * dtw_ref[c:c + 1, :]
            xbd = _tile_rows(x, hpg) * bd_ref[...]
            hs_ref[c] = ht
            y_ref[rows, :] = _dotf(m4, xbd, _NN) + _dotf(cm, ht, _NN) * jnp.exp(acs) + d_ref[...] * x
            a_last = acs[CHUNK - 1:CHUNK, :]
            xw = x * (jnp.exp(a_last - acs) * dt_ref[rows, :])
            ht = ht * jnp.exp(a_last) + _dotf(bm, xw, _TN)
        h_scr[...] = ht
        _carry_finish(carry, c_in, c_out, c_sems, (pl.program_id(0) == G - 1) & (pl.program_id(1) == nb - 1))

    res = pl.pallas_call(
        body, name=name, grid=(G, nb),
        in_specs=[xsp, bsp, csp, xsp, xsp, rsp, rsp, dsp, const(tril), const(mask4), const(bd), const(triu_bd)] + [_ANY] * len(carry.ins),
        out_specs=[xsp, hsp] + [_ANY] * len(carry.out_shapes),
        out_shape=[_S((T, G * W), f32), _S((G, nc, SSD_STATE, W), f32)] + list(carry.out_shapes),
        scratch_shapes=[pltpu.VMEM((SSD_STATE, W), f32)] + list(carry.sems),
        compiler_params=_cparams(("arbitrary", "arbitrary") if carry.ins else ("parallel", "arbitrary")),
    )(xc, xc, xc, dt4, a4, dtw, aw, d4, tril, mask4, bd, triu_bd, *carry.ins)
    return res[0], res[1], list(res[2:])


def _ssd2_bwd(xc, dt4, a4, dtw, aw, d4, consts, hs, dy, *, d_inner, name):
    T = xc.shape[0]
    G, nc, W = dtw.shape
    hpg = W // CHUNK
    tb = min(SSD2_TB, T)
    nb, ncb = T // tb, tb // CHUNK
    b_off = d_inner // SSD_STATE
    tmap = lambda i: nb - 1 - i
    xsp, bsp, csp, rsp, dsp, hsp, const = _ssd2_specs(G, hpg, tb, tmap, b_off, b_off + G)
    gsp = pl.BlockSpec((tb, SSD_STATE), lambda g, i: (tmap(i), g))
    ddsp = pl.BlockSpec((None, 1, LANES), lambda g, i: (g, 0, 0))
    tril, mask4, bd, triu_bd, ered = consts

    def body(x_ref, b_ref, c_ref, dt_ref, a_ref, dtw_ref, aw_ref, d_ref, tril_ref, mask_ref, bd_ref, tbd_ref, er_ref, hs_ref, dy_ref,
             dx_ref, db_ref, dc_ref, ddtc_ref, dac_ref, ddtw_ref, daw_ref, dd_ref, g_scr, dd_scr, rw_scr, tl_scr):
        first = pl.program_id(1) == 0

        @pl.when(first)
        def _():
            g_scr[...] = jnp.zeros_like(g_scr)
            dd_scr[...] = jnp.zeros_like(dd_scr)

        mask = mask_ref[...] > 0
        lane_in_block = lax.broadcasted_iota(jnp.int32, mask.shape, 1) & (CHUNK - 1)
        maskt = lax.broadcasted_iota(jnp.int32, mask.shape, 0) <= lane_in_block
        acs_rows = _doth(aw_ref[...], tbd_ref[...])
        dht = g_scr[...]
        dd = dd_scr[...]
        for c in range(ncb - 1, -1, -1):
            rows = slice(c * CHUNK, (c + 1) * CHUNK)
            x, bm, cm, dyc = x_ref[rows, :], b_ref[rows, :], c_ref[rows, :], dy_ref[rows, :]
            dtc, dtr = dt_ref[rows, :], dtw_ref[c:c + 1, :]
            ht = hs_ref[c]
            acs = _doth(tril_ref[...], a_ref[rows, :])
            seg = acs - acs_rows[c:c + 1, :]
            lmat = jnp.where(mask, jnp.exp(jnp.minimum(seg, 0.0)), 0.0)
            lmat_t = jnp.where(maskt, jnp.exp(jnp.minimum(-seg, 0.0)), 0.0)
            btile, ctile = _tile_rows(bm, hpg), _tile_rows(cm, hpg)
            g4 = _dotf(cm, btile, _NT)
            gt4 = _dotf(bm, ctile, _NT)
            m4 = g4 * lmat * dtr
            mt4 = gt4 * lmat_t * dtc
            xbd = _tile_rows(x, hpg) * bd_ref[...]
            dybd = _tile_rows(dyc, hpg) * bd_ref[...]
            dm4 = _dotf(dyc, xbd, _NT)
            dmt4 = _dotf(x, dybd, _NT)
            dx = d_ref[...] * dyc + _dotf(mt4, dybd, _NN)
            dd = dd + jnp.sum(dyc * x, axis=0, keepdims=True)
            e4 = dm4 * m4
            dc = _dotf(dm4 * lmat * dtr, btile, _NN)
            db = _dotf(dmt4 * lmat_t * dtc, ctile, _NN)
            decay = jnp.exp(acs)
            yoff = _dotf(cm, ht, _NN) * decay
            dz = dyc * decay
            dc = dc + _dotf(dz, ht, _NT)
            dht_prev = _dotf(cm, dz, _TN)
            a_last = acs[CHUNK - 1:CHUNK, :]
            ea_last = jnp.exp(a_last)
            erel = jnp.exp(a_last - acs)
            dte = erel * dtc
            dxw = _dotf(bm, dht, _NN)
            db = db + _dotf(x * dte, dht, _NT)
            dx = dx + dxw * dte
            q4 = dxw * x
            dacs = e4 + dyc * yoff - q4 * dte
            col = jnp.concatenate([q4 * erel, _doth(tril_ref[...], dacs, _TN)], axis=0)
            col = _doth(col, er_ref[...])
            ddtc_ref[rows, :] = col[:CHUNK]
            dac_ref[rows, :] = col[CHUNK:]
            ddtw_ref[c:c + 1, :] = jnp.sum(dm4 * g4 * lmat, axis=0, keepdims=True)
            rw_scr[c:c + 1, :] = -jnp.sum(e4, axis=0, keepdims=True)
            tl_scr[c:c + 1, :] = jnp.sum(q4 * dte, axis=0, keepdims=True) + ea_last * jnp.sum(dht * ht, axis=0, keepdims=True)
            dx_ref[rows, :] = dx
            db_ref[rows, :] = db
            dc_ref[rows, :] = dc
            dht = dht_prev + dht * ea_last
        daw_ref[...] = _doth(rw_scr[...], tbd_ref[...], _NT) + _doth(tl_scr[...], bd_ref[...])
        g_scr[...] = dht
        dd_scr[...] = dd

        @pl.when(pl.program_id(1) == nb - 1)
        def _():
            dd_ref[...] = _doth(dd, er_ref[...])

    return pl.pallas_call(
        body, name=name, grid=(G, nb),
        in_specs=[xsp, bsp, csp, xsp, xsp, rsp, rsp, dsp, const(tril), const(mask4), const(bd), const(triu_bd), const(ered), hsp, xsp],
        out_specs=[xsp, gsp, gsp, gsp, gsp, rsp, rsp, ddsp],
        out_shape=[_S((T, G * W), f32), _S((T, G * SSD_STATE), f32), _S((T, G * SSD_STATE), f32), _S((T, G * LANES), f32),
                   _S((T, G * LANES), f32), _S(dtw.shape, f32), _S(dtw.shape, f32), _S((G, 1, LANES), f32)],
        scratch_shapes=[pltpu.VMEM((SSD_STATE, W), f32), pltpu.VMEM((1, W), f32), pltpu.VMEM((ncb, W), f32), pltpu.VMEM((ncb, W), f32)],
        compiler_params=_cparams(("parallel", "arbitrary")),
    )(xc, xc, xc, dt4, a4, dtw, aw, d4, tril, mask4, bd, triu_bd, ered, hs, dy)


def _peers():
    x, y, c = lax.axis_index("x"), lax.axis_index("y"), lax.axis_index("c")
    return x, y, c


_ANY = pl.BlockSpec(memory_space=pl.ANY)
N_CHIP = N_DEV // 2


def _all_gather(shards, *, name):
    n = len(shards)
    carry = _carry_gather(shards)

    def body(*refs):
        x_refs, out_refs, sems = refs[:n], refs[n:2 * n], refs[2 * n:]
        _gather_start(x_refs, out_refs, sems)
        _gather_finish(x_refs, out_refs, sems)

    return pl.pallas_call(
        body, name=name, out_shape=list(carry.out_shapes), in_specs=[_ANY] * n, out_specs=[_ANY] * n, scratch_shapes=list(carry.sems),
    )(*shards)


def _gather_parts(x_refs, out_refs, sems):
    send_sems, recv_sems, local_sems = sems
    x, y, c = _peers()
    me, sibling = (x, y, c), (x, y, 1 - c)
    chips = [(1 - x, y), (x, 1 - y), (1 - x, 1 - y)]
    n = len(x_refs)

    def copy(a, r, block, to, src=None):
        px, py, pc = block
        slot = out_refs[a].at[4 * px + 2 * py + pc]
        return pltpu.make_async_remote_copy(
            src_ref=slot if src is None else src, dst_ref=slot, send_sem=send_sems.at[7 * a + r],
            recv_sem=recv_sems.at[7 * a + r], device_id=to, device_id_type=MESH)

    mine = [pltpu.make_async_copy(x_refs[a], out_refs[a].at[4 * x + 2 * y + c], local_sems.at[a]) for a in range(n)]
    first = []
    for a in range(n):
        first.append(copy(a, 0, me, sibling, src=x_refs[a]))
        first += [copy(a, 1 + j, me, (*chip, c), src=x_refs[a]) for j, chip in enumerate(chips)]
    return copy, mine, first, me, sibling, chips, c, n


def _gather_start(x_refs, out_refs, sems):
    _, mine, first, *_ = _gather_parts(x_refs, out_refs, sems)
    for cp in mine + first:
        cp.start()


def _gather_finish(x_refs, out_refs, sems):
    copy, mine, first, me, sibling, chips, c, n = _gather_parts(x_refs, out_refs, sems)
    passed = []
    for j, chip in enumerate(chips):
        for a in range(n):
            copy(a, 1 + j, (*chip, c), me).wait_recv()
            fwd = copy(a, 4 + j, (*chip, c), sibling)
            fwd.start()
            passed.append(fwd)
    for a in range(n):
        copy(a, 0, sibling, me).wait_recv()
    for j, chip in enumerate(chips):
        for a in range(n):
            copy(a, 4 + j, (*chip, 1 - c), me).wait_recv()
    for cp in first + passed:
        cp.wait_send()
    for cp in mine:
        cp.wait()


def _exchange_sibling(slots, *, name):
    n = len(slots)

    def body(*refs):
        x_refs, sib_refs = refs[:n], refs[n:2 * n]
        send_sems, recv_sems = refs[2 * n:]
        x, y, c = _peers()
        give = [pltpu.make_async_remote_copy(
            src_ref=x_refs[a].at[pl.ds(N_CHIP * (1 - c), N_CHIP)], dst_ref=sib_refs[a], send_sem=send_sems.at[a],
            recv_sem=recv_sems.at[a], device_id=(x, y, 1 - c), device_id_type=MESH) for a in range(n)]
        for cp in give:
            cp.start()
        for cp in give:
            cp.wait_recv()
        for cp in give:
            cp.wait_send()

    return list(pl.pallas_call(
        body, name=name, out_shape=[_S((N_CHIP,) + s.shape[1:], s.dtype) for s in slots], in_specs=[_ANY] * n, out_specs=[_ANY] * n,
        scratch_shapes=[pltpu.SemaphoreType.DMA((n,)), pltpu.SemaphoreType.DMA((n,))],
    )(*slots))


def _chip_sum(slots, sib, core, *, name):
    _, R, W = slots.shape
    tr = _pick(R, max(16, (1 << 20) // (4 * W)), align=16)

    def body(core_ref, x_ref, s_ref, o_ref):
        o_ref[...] = (x_ref[...].astype(f32) + s_ref[...].astype(f32)).astype(o_ref.dtype)

    blk = pl.BlockSpec((None, tr, W), lambda t, i, core_ref: (t, i, 0))
    return pl.pallas_call(
        body, name=name, out_shape=_S(sib.shape, slots.dtype),
        grid_spec=pltpu.PrefetchScalarGridSpec(
            num_scalar_prefetch=1, grid=(N_CHIP, R // tr),
            in_specs=[pl.BlockSpec((None, tr, W), lambda t, i, core_ref: (N_CHIP * core_ref[0] + t, i, 0)), blk], out_specs=blk),
        compiler_params=_cparams(("parallel", "parallel")),
    )(core, slots, sib)


def _exchange_chips(parts, *, name):
    n = len(parts)

    def body(*refs):
        copies = _chip_copies(refs[:n], refs[n:2 * n], *refs[2 * n:])
        _start_all(copies)
        _wait_all(copies)

    return list(pl.pallas_call(
        body, name=name, out_shape=_chip_out_shapes(parts), in_specs=[_ANY] * n, out_specs=[_ANY] * n,
        scratch_shapes=_chip_sems(n),
    )(*parts))


def _chip_out_shapes(parts):
    return [_S((N_CHIP - 1,) + p.shape[1:], p.dtype) for p in parts]


def _chip_sems(n):
    return [pltpu.SemaphoreType.DMA((3 * n,)), pltpu.SemaphoreType.DMA((3 * n,))]


def _chip_copies(p_refs, out_refs, send_sems, recv_sems):
    x, y, c = _peers()
    copies = []
    for j in range(1, N_CHIP):
        tx, ty = x ^ (j >> 1), y ^ (j & 1)
        for a in range(len(p_refs)):
            copies.append(pltpu.make_async_remote_copy(
                src_ref=p_refs[a].at[2 * tx + ty], dst_ref=out_refs[a].at[j - 1], send_sem=send_sems.at[3 * a + j - 1],
                recv_sem=recv_sems.at[3 * a + j - 1], device_id=(tx, ty, c), device_id_type=MESH))
    return copies


def _start_all(copies):
    for cp in copies:
        cp.start()


def _wait_all(copies):
    for cp in copies:
        cp.wait_recv()
    for cp in copies:
        cp.wait_send()


class _Carry(NamedTuple):
    ins: tuple = ()
    out_shapes: tuple = ()
    sems: tuple = ()
    start: Callable = None
    finish: Callable = None


_NO_CARRY = _Carry()


def _carry_chips(parts):
    return _Carry(tuple(parts), tuple(_chip_out_shapes(parts)), tuple(_chip_sems(len(parts))),
                  lambda i, o, s: _start_all(_chip_copies(i, o, *s)), lambda i, o, s: _wait_all(_chip_copies(i, o, *s)))


def _carry_gather(shards):
    n = len(shards)
    sems = (pltpu.SemaphoreType.DMA((7 * n,)), pltpu.SemaphoreType.DMA((7 * n,)), pltpu.SemaphoreType.DMA((n,)))
    return _Carry(tuple(shards), tuple(_S((N_DEV,) + s.shape, s.dtype) for s in shards), sems, _gather_start, _gather_finish)


def _carry_split(carry, refs, n_in, n_out):
    ci, co, cs = len(carry.ins), len(carry.out_shapes), len(carry.sems)
    refs = list(refs)
    own_in, c_in = refs[:n_in], refs[n_in:n_in + ci]
    own_out, c_out = refs[n_in + ci:n_in + ci + n_out], refs[n_in + ci + n_out:n_in + ci + n_out + co]
    rest = refs[n_in + ci + n_out + co:]
    own_scratch, c_sems = rest[:len(rest) - cs], rest[len(rest) - cs:]
    return own_in + own_out + own_scratch, c_in, c_out, c_sems


def _carry_start(carry, c_in, c_out, c_sems, first):
    if carry.ins:
        @pl.when(first)
        def _():
            carry.start(c_in, c_out, c_sems)


def _carry_finish(carry, c_in, c_out, c_sems, last):
    if carry.ins:
        @pl.when(last)
        def _():
            carry.finish(c_in, c_out, c_sems)


def _sum_slots(stack, *, name):
    n, R, W = stack.shape
    tr = _pick(R, 1024, align=8)

    def body(s_ref, o_ref):
        acc = s_ref[0]
        for k in range(1, n):
            acc = acc + s_ref[k]
        o_ref[...] = acc

    return pl.pallas_call(
        body, name=name, grid=(R // tr,), in_specs=[pl.BlockSpec((n, tr, W), lambda i: (0, i, 0))],
        out_specs=pl.BlockSpec((tr, W), lambda i: (i, 0)), out_shape=_S((R, W), f32), compiler_params=_cparams(("parallel",)),
    )(stack)


def _adamw_math(gv, wv, mv, vv):
    c1 = 1.0 / (1.0 - ADAM_B1 ** ADAM_STEP)
    c2 = 1.0 / (1.0 - ADAM_B2 ** ADAM_STEP)
    nm = ADAM_B1 * mv + (1.0 - ADAM_B1) * gv
    nv = ADAM_B2 * vv + (1.0 - ADAM_B2) * jnp.square(gv)
    return -ADAM_LR * ((nm * c1) / (jnp.sqrt(nv * c2) + ADAM_EPS) + ADAM_WD * wv), nm, nv


def _adamw(g, w, m, v, *, name):
    R, W = w.shape
    tr = _pick(R, max(8, (1 << 20) // (4 * W)), align=8)

    def body(g_ref, w_ref, m_ref, v_ref, d_ref, nm_ref, nv_ref):
        d_ref[...], nm_ref[...], nv_ref[...] = _adamw_math(g_ref[...], w_ref[...], m_ref[...], v_ref[...])

    sp = pl.BlockSpec((tr, W), lambda i: (i, 0))
    return pl.pallas_call(
        body, name=name, grid=(R // tr,), in_specs=[sp] * 4, out_specs=[sp] * 3, out_shape=[_S((R, W), f32)] * 3,
        compiler_params=_cparams(("parallel",)),
    )(g, w, m, v)


def _reduce_adamw(own, arrived, chip, w, m, v, *, name):
    n, R, W = arrived.shape
    tr = _pick(R, max(16, (1 << 20) // (4 * W)), align=16)

    def body(chip_ref, o_ref, p_ref, w_ref, m_ref, v_ref, g_ref, d_ref, nm_ref, nv_ref):
        gv = o_ref[...].astype(f32)
        for k in range(n):
            gv = gv + p_ref[k].astype(f32)
        g_ref[...] = gv
        d_ref[...], nm_ref[...], nv_ref[...] = _adamw_math(gv, w_ref[...], m_ref[...], v_ref[...])

    sp = pl.BlockSpec((tr, W), lambda i, chip_ref: (i, 0))
    return pl.pallas_call(
        body, name=name, out_shape=[_S((R, W), f32)] * 4,
        grid_spec=pltpu.PrefetchScalarGridSpec(
            num_scalar_prefetch=1, grid=(R // tr,),
            in_specs=[pl.BlockSpec((None, tr, W), lambda i, chip_ref: (chip_ref[0], i, 0)),
                      pl.BlockSpec((n, tr, W), lambda i, chip_ref: (0, i, 0))] + [sp] * 3, out_specs=[sp] * 4),
        compiler_params=_cparams(("parallel",)),
    )(chip, own, arrived, w, m, v)


def _pieces(seg_start, seg_len, shard_w):
    out, col = [], seg_start
    while col < seg_start + seg_len:
        k, a = divmod(col, shard_w)
        n = min(shard_w - a, seg_start + seg_len - col)
        out.append((k, a, col - seg_start, n))
        col += n
    return out


def _unshard_w_in(g, seg_lens, *, name):
    _, D, w = g.shape
    starts = [sum(seg_lens[:i]) for i in range(len(seg_lens))]
    widths = [max(n, LANES) for n in seg_lens]
    tm = _pick(D, 256, align=16)

    def body(g_ref, *o_refs):
        for o_ref, s0, n in zip(o_refs, starts, seg_lens):
            if n < o_ref.shape[1]:
                o_ref[...] = jnp.zeros_like(o_ref)
            for k, a, off, m in _pieces(s0, n, w):
                o_ref[:, off:off + m] = g_ref[k, :, a:a + m]

    return pl.pallas_call(
        body, name=name, grid=(D // tm,), in_specs=[pl.BlockSpec((N_DEV, tm, w), lambda i: (0, i, 0))],
        out_specs=[pl.BlockSpec((tm, wd), lambda i: (i, 0)) for wd in widths], out_shape=[_S((D, wd), g.dtype) for wd in widths],
        compiler_params=_cparams(("parallel",)),
    )(g)


def _unshard_pair(g1, g2, *, name):
    _, D, w = g1.shape
    tm = _pick(D, 256, align=16)

    def body(a_ref, b_ref, o_ref):
        for i, g_ref in enumerate((a_ref, b_ref)):
            for k in range(N_DEV):
                off = (i * N_DEV + k) * w
                o_ref[:, off:off + w] = g_ref[k]

    blk = pl.BlockSpec((N_DEV, tm, w), lambda i: (0, i, 0))
    return pl.pallas_call(
        body, name=name, grid=(D // tm,), in_specs=[blk, blk], out_specs=pl.BlockSpec((tm, 2 * N_DEV * w), lambda i: (i, 0)),
        out_shape=_S((D, 2 * N_DEV * w), g1.dtype), compiler_params=_cparams(("parallel",)),
    )(g1, g2)


def _reshard_pair(dw, *, name):
    D, w = dw.shape[0], dw.shape[1] // (2 * N_DEV)
    tm = _pick(D, 128, align=16)

    def body(g_ref, a_ref, b_ref):
        for i, o_ref in enumerate((a_ref, b_ref)):
            for k in range(N_DEV):
                off = (i * N_DEV + k) * w
                o_ref[_slot_of(k)] = g_ref[:, off:off + w].astype(o_ref.dtype)

    blk = pl.BlockSpec((N_DEV, tm, w), lambda i: (0, i, 0))
    return pl.pallas_call(
        body, name=name, grid=(D // tm,), in_specs=[pl.BlockSpec((tm, dw.shape[1]), lambda i: (i, 0))], out_specs=[blk, blk],
        out_shape=[_S((N_DEV, D, w), bf16)] * 2, compiler_params=_cparams(("parallel",)),
    )(dw)


def _reshard_w_in(grads, seg_lens, w, *, name):
    D = grads[0].shape[0]
    starts = [sum(seg_lens[:i]) for i in range(len(seg_lens))]
    tm = _pick(D, 128, align=16)

    def body(*refs):
        o_ref = refs[-1]
        for g_ref, s0, n in zip(refs[:-1], starts, seg_lens):
            for k, a, off, m in _pieces(s0, n, w):
                o_ref[_slot_of(k), :, a:a + m] = g_ref[:, off:off + m].astype(o_ref.dtype)

    return pl.pallas_call(
        body, name=name, grid=(D // tm,), in_specs=[pl.BlockSpec((tm, g.shape[1]), lambda i: (i, 0)) for g in grads],
        out_specs=pl.BlockSpec((N_DEV, tm, w), lambda i: (0, i, 0)), out_shape=_S((N_DEV, D, w), bf16),
        compiler_params=_cparams(("parallel",)),
    )(*grads)


def _pad_flat(a, mult):
    a = a.reshape(-1)
    n = -(-a.shape[0] // mult) * mult
    return a if n == a.shape[0] else jnp.pad(a, (0, n - a.shape[0]))


def _pad_cols(a, mult):
    n = -(-a.shape[1] // mult) * mult
    return a if n == a.shape[1] else jnp.pad(a, ((0, 0), (0, n - a.shape[1])))


def _block_diag(t):
    nblk, g, P, Q = t.shape
    eye = jnp.eye(g, dtype=t.dtype)
    return (t[:, :, :, None, :] * eye[None, :, None, :, None]).reshape(nblk, g * P, g * Q)


def _block_diag_t(w, P, Q):
    nblk = w.shape[0]
    g = w.shape[1] // P
    eye = jnp.eye(g, dtype=w.dtype)
    return (w.reshape(nblk, g, P, g, Q) * eye[None, :, None, :, None]).sum(axis=3)


_COLS = ("ffn1_w_gate", "ffn1_w_up", "ffn2_w_gate", "ffn2_w_up")
_ROWS = ("ffn1_w_down", "ffn2_w_down", "s5_w_glu", "w_proj_s5", "w_out", "w_proj_ssd")
_BIG = _COLS + _ROWS + ("w_in", "conv_w")
_SMALL = ("ffn1_norm", "mix_norm", "conv_b", "s5_A_re", "s5_A_im", "s5_log_dt", "s5_B_re", "s5_B_im", "s5_C_re", "s5_C_im",
          "s5_D", "s5_b_glu", "ssd_A_log", "ssd_dt_bias", "ssd_D", "ssd_norm", "b_gate", "ffn2_norm", "final_norm")
_WEIGHTS = ("ffn1_norm", "ffn1_w_gate", "ffn1_w_up", "ffn1_w_down", "mix_norm", "w_in", "conv_w", "conv_b", "s5_A_re", "s5_A_im",
            "s5_log_dt", "s5_B_re", "s5_B_im", "s5_C_re", "s5_C_im", "s5_D", "s5_w_glu", "s5_b_glu", "ssd_A_log", "ssd_dt_bias",
            "ssd_D", "ssd_norm", "w_proj_s5", "w_proj_ssd", "b_gate", "w_out", "ffn2_norm", "ffn2_w_gate", "ffn2_w_up",
            "ffn2_w_down", "final_norm")
def _ffn_fwd(x, n, wgu, wd, tag, carry=None):
    D = x.shape[1]
    F = wd.shape[0]
    h = _rows(_f_rmsnorm, [x], [n], [(D, bf16)], name=tag + "_norm")[0]
    ab = _mm(h, wgu, carry=carry, name=tag + "_gate_up")
    ab, carried = ab if carry else (ab, [])
    c = _rows(_f_swiglu, [ab], [], [(F, bf16)], name=tag + "_act")[0]
    y = _mm(c, wd, scale=0.5, add=x, name=tag + "_down")
    return y, (x, n, h, ab, c), carried


def _ffn_bwd(saved, wgu, wd, dy, tag, carry_dh=None, carry_after_dwd=None):
    x, n, h, ab, c = saved
    F = wd.shape[0]
    dc = _mm(dy, wd, tb=True, scale=0.5, name=tag + "_d_act")
    dwd = _mm(c, dy, ta=True, o_blk="m", o_slots=True, tm=F // 2, out_dtype=bf16, scale=0.5, name=tag + "_d_wdown")
    carry_w = carry_after_dwd(dwd) if carry_after_dwd else None
    (dab,), _ = _rows_bwd(_f_swiglu, [ab], [], [dc], name=tag + "_act_bwd", want_rows=[0], row_dtypes={0: bf16})
    dh = _mm(dab, wgu, tb=True, carry=carry_dh, name=tag + "_d_h")
    dh, arr_dh = dh if carry_dh else (dh, [])
    dwgu = _mm(h, dab, ta=True, carry=carry_w, name=tag + "_d_wgu")
    dwgu, arr_w = dwgu if carry_w else (dwgu, [])
    dwg, dwu = _reshard_pair(dwgu, name=tag + "_reshard_d_wgu")
    (dx,), (dn,) = _rows_bwd(_f_rmsnorm, [x], [n], [dh], name=tag + "_norm_bwd", want_rows=[0], adds={0: dy})
    return dx, dn, dwg, dwu, dwd, arr_dh, arr_w


def kernel(x, ffn1_norm, ffn1_w_gate, ffn1_w_up, ffn1_w_down, mix_norm, w_in, conv_w, conv_b, s5_A_re, s5_A_im, s5_log_dt, s5_B_re, s5_B_im, s5_C_re, s5_C_im, s5_D, s5_w_glu, s5_b_glu, ssd_A_log, ssd_dt_bias, ssd_D, ssd_norm, w_proj_s5, w_proj_ssd, b_gate, w_out, ffn2_norm, ffn2_w_gate, ffn2_w_up, ffn2_w_down, final_norm, loss_target, m_ffn1_norm, m_ffn1_w_gate, m_ffn1_w_up, m_ffn1_w_down, m_mix_norm, m_w_in, m_conv_w, m_conv_b, m_s5_A_re, m_s5_A_im, m_s5_log_dt, m_s5_B_re, m_s5_B_im, m_s5_C_re, m_s5_C_im, m_s5_D, m_s5_w_glu, m_s5_b_glu, m_ssd_A_log, m_ssd_dt_bias, m_ssd_D, m_ssd_norm, m_w_proj_s5, m_w_proj_ssd, m_b_gate, m_w_out, m_ffn2_norm, m_ffn2_w_gate, m_ffn2_w_up, m_ffn2_w_down, m_final_norm, v_ffn1_norm, v_ffn1_w_gate, v_ffn1_w_up, v_ffn1_w_down, v_mix_norm, v_w_in, v_conv_w, v_conv_b, v_s5_A_re, v_s5_A_im, v_s5_log_dt, v_s5_B_re, v_s5_B_im, v_s5_C_re, v_s5_C_im, v_s5_D, v_s5_w_glu, v_s5_b_glu, v_ssd_A_log, v_ssd_dt_bias, v_ssd_D, v_ssd_norm, v_w_proj_s5, v_w_proj_ssd, v_b_gate, v_w_out, v_ffn2_norm, v_ffn2_w_gate, v_ffn2_w_up, v_ffn2_w_down, v_final_norm):
    P = dict(locals())
    T, D = x.shape[1], x.shape[2]
    x0, tgt = x[0], loss_target[0]
    sh = {k: P[k][0] for k in _BIG}

    send = {k: (sh[k] if k == "conv_w" else sh[k].astype(bf16)) for k in _BIG}
    W = {}

    def gather_in(keys):
        return _carry_gather([send[k] for k in keys])

    first_keys = ("ffn1_w_gate", "ffn1_w_up", "ffn1_w_down", "conv_w")
    W.update(zip(first_keys, _all_gather([send[k] for k in first_keys], name="gather_weights_ffn1")))
    whole = lambda k: W[k].reshape(-1, D)
    conv_w_full = W["conv_w"].transpose(1, 0, 2).reshape(CONV_K, -1)

    d_inner = N_DEV * sh["w_proj_ssd"].shape[0]
    conv_dim = conv_w_full.shape[1]
    H = ssd_A_log.shape[1]
    G = (conv_dim - d_inner) // (2 * SSD_STATE)
    hpg = H // G
    nc = T // CHUNK
    Gs = D // S5_GROUP
    nblk = Gs // S5_GPB
    NS = Gs * S5_STATE
    seg_lens = (D, d_inner, conv_dim, H, 2 * D)

    ffn1_w = (_unshard_pair(W["ffn1_w_gate"], W["ffn1_w_up"], name="unshard_ffn1_gate_up"), whole("ffn1_w_down"))
    x1, sv1, (W["w_in"],) = _ffn_fwd(x0, ffn1_norm, *ffn1_w, "ffn1", carry=gather_in(("w_in",)))
    w_u, w_z, w_xbc, w_dt, w_gl = _unshard_w_in(W["w_in"], seg_lens, name="unshard_w_in")
    h2 = _rows(_f_rmsnorm, [x1], [mix_norm], [(D, bf16)], name="mix_norm")[0]
    u_p = _mm(h2, w_u, o_seg=True, name="in_u")
    z = _mm(h2, w_z, name="in_z")
    xbc = _mm(h2, w_xbc, name="in_xbc")
    gl = _mm(h2, w_gl, name="in_gate")
    dtr = _mm(h2, w_dt, name="in_dt")

    rep = lambda a: jnp.repeat(a, S5_GROUP, axis=0)
    lr, li, ldt = s5_A_re[0], s5_A_im[0], s5_log_dt[0].reshape(Gs, 1)
    brt = s5_B_re[0].transpose(0, 2, 1).reshape(Gs * S5_GROUP, S5_STATE)
    bit = s5_B_im[0].transpose(0, 2, 1).reshape(Gs * S5_GROUP, S5_STATE)
    prep_args = (lr, li, ldt, rep(lr), rep(li), rep(ldt), brt, bit)
    ar, ai, bbrt, bbit = _s5_prep(prep_args, name="s5_prep")
    a_r, a_i = ar.reshape(1, NS), ai.reshape(1, NS)
    wb_r = _block_diag(bbrt.reshape(nblk, S5_GPB, S5_GROUP, S5_STATE)).astype(bf16)
    wb_i = _block_diag(bbit.reshape(nblk, S5_GPB, S5_GROUP, S5_STATE)).astype(bf16)
    c4r = s5_C_re[0].reshape(nblk, S5_GPB, S5_GROUP, S5_STATE).transpose(0, 1, 3, 2)
    c4i = s5_C_im[0].reshape(nblk, S5_GPB, S5_GROUP, S5_STATE).transpose(0, 1, 3, 2)
    wc_r, wc_i = _block_diag(c4r).astype(bf16), _block_diag(c4i).astype(bf16)
    mix_keys = ("s5_w_glu", "w_proj_s5", "w_proj_ssd", "w_out")
    sl_r, sl_i, p_r, p_i, got = _s5_local_scan(u_p, wb_r, wb_i, a_r, a_i, reverse=False, carry=gather_in(mix_keys), name="s5_scan")
    W.update(zip(mix_keys, got))
    w_glu, w_p5, w_pssd, w_o = whole("s5_w_glu"), whole("w_proj_s5"), whole("w_proj_ssd"), whole("w_out")
    c_r, c_i = _s5_carry(sl_r[T - NSEG:], sl_i[T - NSEG:], p_r, p_i, reverse=False, name="s5_carry")
    s_r, s_i, ylin = _s5_fix_out(sl_r, sl_i, a_r, a_i, c_r, c_i, wc_r, wc_i, name="s5_fix_out")
    g5 = _rows(_f_s5_post, [ylin, u_p], [s5_D], [(D, f32)], name="s5_gelu")[0]
    v5 = _mm(g5, w_glu, name="s5_glu_mm")
    o5 = _rows(_f_glu, [g5, v5], [s5_b_glu], [(D, bf16)], name="s5_glu")[0]
    p5 = _mm(o5, w_p5, a_seg=True, name="proj_s5")

    xc = _conv_fwd(xbc, conv_w_full, conv_b, name="conv")
    bias_p, alog_p = _pad_cols(ssd_dt_bias, LANES), _pad_cols(ssd_A_log, LANES)
    dt_p, da_p = _rows(_f_dt, [dtr], [bias_p, alog_p], [(LANES, f32), (LANES, f32)], name="ssd_dt")
    col_l = lambda a: jnp.repeat(a[:, :H], HEADDIM, axis=1)
    row_l = lambda a: a[:, :H].reshape(nc, CHUNK, G, hpg).transpose(2, 0, 3, 1).reshape(G, nc, hpg * CHUNK)
    ssd_in = (xc, col_l(dt_p), col_l(da_p), row_l(dt_p), row_l(da_p), jnp.repeat(ssd_D, HEADDIM, axis=1), _ssd_consts(hpg))
    ffn2_keys = ("ffn2_w_gate", "ffn2_w_up", "ffn2_w_down")
    y_ssd, hs, got = _ssd2_fwd(*ssd_in, d_inner=d_inner, carry=gather_in(ffn2_keys), name="ssd")
    W.update(zip(ffn2_keys, got))
    ffn2_w = (_unshard_pair(W["ffn2_w_gate"], W["ffn2_w_up"], name="unshard_ffn2_gate_up"), whole("ffn2_w_down"))
    yn = _rows(_f_gated_norm, [y_ssd, z], [ssd_norm], [(d_inner, bf16)], name="ssd_gated_norm")[0]
    pssd = _mm(yn, w_pssd, name="proj_ssd")

    merged = _rows(_f_merge, [gl, p5, pssd], [b_gate], [(D, bf16)], name="merge")[0]
    x2 = _mm(merged, w_o, add=x1, name="out_proj")
    x3, sv2, _ = _ffn_fwd(x2, ffn2_norm, *ffn2_w, "ffn2")
    lossv, dx3, d_final = _loss_stage(x3, tgt, final_norm.reshape(1, D), name="loss")

    gw = {}
    gs = {"final_norm": d_final}
    slot_mm = lambda a_, b_, name, **kw: _mm(a_, b_, ta=True, o_blk="m", o_slots=True, out_dtype=bf16, name=name, **kw)
    core = lax.axis_index("c").astype(jnp.int32).reshape(1)
    chip = (2 * lax.axis_index("x") + lax.axis_index("y")).astype(jnp.int32).reshape(1)
    chip_sums, arrived = {}, {}

    def level1(keys, tag):
        sib = _exchange_sibling([gw[k] for k in keys], name="exchange_sibling_" + tag)
        for k, s_ in zip(keys, sib):
            chip_sums[k] = _chip_sum(gw[k], s_, core, name="chip_sum_" + k)
        return [chip_sums[k] for k in keys]

    dx2, gs["ffn2_norm"], gw["ffn2_w_gate"], gw["ffn2_w_up"], gw["ffn2_w_down"], _, _ = _ffn_bwd(sv2, *ffn2_w, dx3, "ffn2")

    dmerged = _mm(dx2, w_o, tb=True, name="d_merged")
    gw["w_out"] = slot_mm(merged, dx2, "d_w_out")
    (dgl, dp5, dpssd), (gs["b_gate"],) = _rows_bwd(_f_merge, [gl, p5, pssd], [b_gate], [dmerged], name="merge_bwd", want_rows=[0, 1, 2])

    dyn = _mm(dpssd, w_pssd, tb=True, name="d_yn")
    gw["w_proj_ssd"] = slot_mm(yn, dpssd, "d_w_proj_ssd")
    group_a = ("ffn2_w_gate", "ffn2_w_up", "ffn2_w_down", "w_out", "w_proj_ssd")
    parts_a = level1(group_a, "a")
    (dyssd, dz), (gs["ssd_norm"],) = _rows_bwd(_f_gated_norm, [y_ssd, z], [ssd_norm], [dyn], name="ssd_gated_norm_bwd", want_rows=[0, 1])
    dxs, dbm, dcm, ddtc, ddac, ddtw, ddaw, ddh = _ssd2_bwd(*ssd_in, hs, dyssd, d_inner=d_inner, name="ssd_bwd")

    def fold(col, row):
        col = col.reshape(T, G, LANES)[:, :, :hpg].reshape(T, H)
        row = row.reshape(G, nc, hpg, CHUNK).transpose(1, 3, 0, 2).reshape(T, H)
        return _pad_cols(col + row, LANES)

    (ddtr,), (dbias_p, dalog_p) = _rows_bwd(_f_dt, [dtr], [bias_p, alog_p], [fold(ddtc, ddtw), fold(ddac, ddaw)], name="ssd_dt_bwd", want_rows=[0])
    gs["ssd_dt_bias"], gs["ssd_A_log"], gs["ssd_D"] = dbias_p[:, :H], dalog_p[:, :H], ddh[:, 0, :hpg].reshape(1, H)
    dxbc, d_conv_w, gs["conv_b"], arr = _conv_bwd(
        xbc, conv_w_full, conv_b, jnp.concatenate([dxs, dbm, dcm], axis=1), carry=_carry_chips(parts_a), name="conv_bwd")
    arrived.update(zip(group_a, arr))
    cwk = sh["conv_w"].shape[1]
    gw["conv_w"] = d_conv_w.reshape(CONV_K, N_CHIP, 2, cwk).transpose(2, 1, 0, 3).reshape(N_DEV, CONV_K, cwk)

    do5 = _mm(dp5, w_p5, tb=True, o_seg=True, name="d_o5")
    gw["w_proj_s5"] = slot_mm(o5, dp5, "d_w_proj_s5", a_seg=True)
    (dg5a, dv5), (gs["s5_b_glu"],) = _rows_bwd(_f_glu, [g5, v5], [s5_b_glu], [do5], name="s5_glu_bwd", want_rows=[0, 1])
    dg5 = _mm(dv5, w_glu, tb=True, add=dg5a, name="d_g5")
    gw["s5_w_glu"] = slot_mm(g5, dv5, "d_w_glu")
    (dylin, du_a), (gs["s5_D"],) = _rows_bwd(_f_s5_post, [ylin, u_p], [s5_D], [dg5], name="s5_gelu_bwd", want_rows=[0, 1])
    wct_r, wct_i = wc_r.transpose(0, 2, 1), -wc_i.transpose(0, 2, 1)
    ql_r, ql_i, pb_r, pb_i = _s5_local_scan(dylin, wct_r, wct_i, a_r, -a_i, reverse=True, name="s5_scan_bwd")
    cb_r, cb_i = _s5_carry(ql_r[:NSEG], ql_i[:NSEG], pb_r, pb_i, reverse=True, name="s5_carry_bwd")
    tc = min(S5_TC, T)

    def before_blocks(s):
        last = s.reshape(T // tc, tc, NS)[:, tc - NSEG:, :]
        wrap = jnp.concatenate([jnp.zeros((1, 1, NS), f32), last[-1:, : NSEG - 1, :]], axis=1)
        return jnp.concatenate([wrap, last[:-1]], axis=0)

    du_p, dwb_r, dwb_i, dwc_r, dwc_i, d_ar, d_ai = _s5_fix_bwd(
        ql_r, ql_i, a_r, -a_i, cb_r, cb_i, s_r, s_i, before_blocks(s_r), before_blocks(s_i), u_p, dylin, du_a, wb_r, wb_i, name="s5_fix_bwd")
    unblk = lambda w: _block_diag_t(w, S5_GROUP, S5_STATE).reshape(Gs * S5_GROUP, S5_STATE)
    rsum = jnp.repeat(jnp.eye(Gs, dtype=f32), S5_GROUP, axis=1)
    d_lr, d_li, d_ldt, d_brt, d_bit = _s5_prep_bwd(
        prep_args, (d_ar.reshape(Gs, S5_STATE), d_ai.reshape(Gs, S5_STATE), unblk(dwb_r), unblk(dwb_i)), rsum, name="s5_prep_bwd")
    gs["s5_A_re"], gs["s5_A_im"], gs["s5_log_dt"] = d_lr, d_li, d_ldt.reshape(1, Gs)
    gs["s5_B_re"] = d_brt.reshape(Gs, S5_GROUP, S5_STATE).transpose(0, 2, 1)
    gs["s5_B_im"] = d_bit.reshape(Gs, S5_GROUP, S5_STATE).transpose(0, 2, 1)
    gs["s5_C_re"] = _block_diag_t(dwc_r, S5_STATE, S5_GROUP).transpose(0, 1, 3, 2).reshape(Gs, S5_GROUP, S5_STATE)
    gs["s5_C_im"] = _block_diag_t(dwc_i, S5_STATE, S5_GROUP).transpose(0, 1, 3, 2).reshape(Gs, S5_GROUP, S5_STATE)

    d_w_in = [_mm(h2, du_p, ta=True, b_seg=True, name="d_w_u"), _mm(h2, dz, ta=True, name="d_w_z"), _mm(h2, dxbc, ta=True, name="d_w_xbc"),
              _mm(h2, ddtr, ta=True, name="d_w_dt"), _mm(h2, dgl, ta=True, name="d_w_gate")]
    gw["w_in"] = _reshard_w_in(d_w_in, seg_lens, sh["w_in"].shape[1], name="reshard_d_w_in")
    group_c = ("w_proj_s5", "s5_w_glu", "conv_w")
    parts_c = level1(group_c + ("w_in",), "c")
    half = D // 2
    win_lo, win_hi = parts_c[3][:, :half], parts_c[3][:, half:]
    dh2 = _mm(du_p, w_u, tb=True, a_seg=True, name="d_h2_u")
    dh2, arr = _mm(dz, w_z, tb=True, add=dh2, carry=_carry_chips(parts_c[:3]), name="d_h2_z")
    arrived.update(zip(group_c, arr))
    dh2, (arr_lo,) = _mm(dxbc, w_xbc, tb=True, add=dh2, carry=_carry_chips([win_lo]), name="d_h2_xbc")
    dh2 = _mm(dgl, w_gl, tb=True, add=dh2, name="d_h2_gate")
    dh2 = _mm(ddtr, w_dt, tb=True, add=dh2, name="d_h2_dt")
    (dx1,), (gs["mix_norm"],) = _rows_bwd(_f_rmsnorm, [x1], [mix_norm], [dh2], name="mix_norm_bwd", want_rows=[0], adds={0: dx2})

    def carry_ffn1_down(dwd):
        gw["ffn1_w_down"] = dwd
        return _carry_chips(level1(("ffn1_w_down",), "d"))

    dx0, gs["ffn1_norm"], gw["ffn1_w_gate"], gw["ffn1_w_up"], _, (arr_hi,), (arrived["ffn1_w_down"],) = _ffn_bwd(
        sv1, *ffn1_w, dx1, "ffn1", carry_dh=_carry_chips([win_hi]), carry_after_dwd=carry_ffn1_down)
    arrived["w_in"] = jnp.concatenate([arr_lo, arr_hi], axis=1)
    group_e = ("ffn1_w_gate", "ffn1_w_up")
    arrived.update(zip(group_e, _exchange_chips(level1(group_e, "e"), name="exchange_chips_e")))

    small_shapes = {k: (P[k][0].shape if P[k].ndim > 1 else P[k].shape) for k in _SMALL}
    pack = lambda d: jnp.concatenate([_pad_flat(d[k], TILE_ELEMS) for k in _SMALL]).reshape(-1, LANES)
    gsmall = _sum_slots(_all_gather([pack(gs)], name="gather_small_grads")[0], name="sum_small_grads")
    snum = {k: math.prod(small_shapes[k]) for k in _SMALL}
    ssz = {k: -(-snum[k] // TILE_ELEMS) * TILE_ELEMS for k in _SMALL}

    grads, delta, new_m, new_v = {}, {}, {}, {}
    for k in _BIG:
        grads[k], delta[k], new_m[k], new_v[k] = _reduce_adamw(
            chip_sums[k], arrived[k], chip, P[k][0], P["m_" + k][0], P["v_" + k][0], name="adamw_" + k)
    d_s, m_s, v_s = _adamw(gsmall, pack({k: P[k] for k in _SMALL}), pack({k: P["m_" + k] for k in _SMALL}),
                           pack({k: P["v_" + k] for k in _SMALL}), name="adamw_small")
    off = 0
    gflat, dflat, mflat, vflat = gsmall.reshape(-1), d_s.reshape(-1), m_s.reshape(-1), v_s.reshape(-1)
    for k in _SMALL:
        n = snum[k]
        grads[k], delta[k], new_m[k], new_v[k] = (a[off:off + n] for a in (gflat, dflat, mflat, vflat))
        off += ssz[k]

    loss = lax.psum(lossv[0, 0], ("x", "y", "c"))
    out = [loss, dx0.reshape(x.shape)]
    for d in (grads, delta, new_m, new_v):
        out += [d[k].reshape(P[k].shape) for k in _WEIGHTS]
    return tuple(out)
```

```python
import math
from typing import Callable, NamedTuple

import jax
import jax.numpy as jnp
from jax import lax
from jax.experimental import pallas as pl
from jax.experimental.pallas import tpu as pltpu

f32 = jnp.float32
bf16 = jnp.bfloat16
_S = jax.ShapeDtypeStruct

EPS = 1e-6
S5_GROUP = 16
S5_STATE = 64
HEADDIM = 64
SSD_STATE = 128
CHUNK = 64
CONV_K = 4
NSEG = 8
S5_GPB = 16
N_DEV = 8
LANES = 128
TILE_ELEMS = 8 * LANES

ADAM_LR = 0.001
ADAM_B1 = 0.9
ADAM_B2 = 0.999
ADAM_EPS = 1e-08
ADAM_WD = 0.01
ADAM_STEP = 10

VMEM_LIMIT = 56 * 1024 * 1024
MM_FULL_K = 3072
MM_MAX_TN = 3072
MESH = pl.DeviceIdType.MESH


def _cparams(sem=None):
    return pltpu.CompilerParams(dimension_semantics=sem, vmem_limit_bytes=VMEM_LIMIT)


def _pick(dim, pref, align=LANES):
    best = None
    t = align
    while t <= min(dim, pref):
        if dim % t == 0:
            best = t
        t += align
    return best or dim


def _slot_of(k):
    return (k & 1) * (N_DEV // 2) + (k >> 1)


def _mm(a, b, *, name, ta=False, tb=False, a_blk=None, b_blk=None, o_blk=None, o_slots=False, a_seg=False, b_seg=False,
        o_seg=False, tm=None, out_dtype=f32, scale=1.0, add=None, carry=None):
    a2, b2 = a.shape[-2:], b.shape[-2:]
    Ma, Ka = (a2[1], a2[0]) if ta else a2
    Kb, Nb = (b2[1], b2[0]) if tb else b2
    M = Ma * (a.shape[0] if a_blk == "m" else 1)
    K = Ka * (a.shape[0] if a_blk == "k" else 1)
    N = Nb * (b.shape[0] if b_blk == "n" else 1)
    assert K == Kb * (b.shape[0] if b_blk == "k" else 1), (a.shape, b.shape, ta, tb, a_blk, b_blk)
    assert (a.ndim == 3) == (a_blk is not None) and (b.ndim == 3) == (b_blk is not None)
    tm = Ma if a_blk == "m" else (tm or _pick(M, 512))
    tn = Nb if b_blk == "n" else _pick(N, MM_MAX_TN)
    if a_blk == "k" or b_blk == "k":
        tk = Ka if a_blk == "k" else Kb
        assert tk == (Kb if b_blk == "k" else tk)
    else:
        tk = K if K <= MM_FULL_K else _pick(K, 1024 if ta else MM_FULL_K)
    if (a_seg and not ta) or o_seg:
        tm = M // NSEG
    if (a_seg and ta) or b_seg:
        tk = K // NSEG
    gm, gn, nk = M // tm, N // tn, K // tk
    assert not (add is not None and (o_seg or o_blk)) and not (o_blk and o_seg)

    if a_seg:
        assert a.ndim == 2
        a = a.reshape(a.shape[0] // NSEG, NSEG * a.shape[1])
        if ta:
            a_spec = pl.BlockSpec((tk, tm), lambda i, j, k: (0, k * (Ma // tm) + i))
        else:
            a_spec = pl.BlockSpec((tm, tk), lambda i, j, k: (0, i * (Ka // tk) + k))
    elif a.ndim == 3:
        lead = (lambda i, k: i) if a_blk == "m" else (lambda i, k: k)
        if ta:
            a_spec = pl.BlockSpec((None, tk, tm), lambda i, j, k: (lead(i, k), 0 if a_blk == "k" else k, 0 if a_blk == "m" else i))
        else:
            a_spec = pl.BlockSpec((None, tm, tk), lambda i, j, k: (lead(i, k), 0 if a_blk == "m" else i, 0 if a_blk == "k" else k))
    else:
        a_spec = pl.BlockSpec((tk, tm), lambda i, j, k: (k, i)) if ta else pl.BlockSpec((tm, tk), lambda i, j, k: (i, k))
    if b_seg:
        assert b.ndim == 2 and not tb
        b = b.reshape(b.shape[0] // NSEG, NSEG * b.shape[1])
        b_spec = pl.BlockSpec((tk, tn), lambda i, j, k: (0, k * (Nb // tn) + j))
    elif b.ndim == 3:
        lead = (lambda j, k: j) if b_blk == "n" else (lambda j, k: k)
        if tb:
            b_spec = pl.BlockSpec((None, tn, tk), lambda i, j, k: (lead(j, k), 0 if b_blk == "n" else j, 0 if b_blk == "k" else k))
        else:
            b_spec = pl.BlockSpec((None, tk, tn), lambda i, j, k: (lead(j, k), 0 if b_blk == "k" else k, 0 if b_blk == "n" else j))
    else:
        b_spec = pl.BlockSpec((tn, tk), lambda i, j, k: (j, k)) if tb else pl.BlockSpec((tk, tn), lambda i, j, k: (k, j))
    slot = _slot_of if o_slots else (lambda k: k)
    if o_blk == "n":
        assert gn == N_DEV or not o_slots
        o_shape, o_spec = (gn, M, tn), pl.BlockSpec((None, tm, tn), lambda i, j, k: (slot(j), i, 0))
    elif o_blk == "m" and o_slots and gm < N_DEV:
        rs = M // N_DEV
        per_tile = tm // rs
        assert per_tile % 2 == 0 and tm % rs == 0
        o_shape = (2, N_CHIP, rs, N)
        o_spec = pl.BlockSpec((2, per_tile // 2, rs, tn), lambda i, j, k: (0, i, 0, j))
    elif o_blk == "m":
        assert gm == N_DEV or not o_slots
        o_shape, o_spec = (gm, tm, N), pl.BlockSpec((None, tm, tn), lambda i, j, k: (slot(i), 0, j))
    elif o_seg:
        o_shape, o_spec = (tm, NSEG * N), pl.BlockSpec((tm, tn), lambda i, j, k: (0, i * (N // tn) + j))
    else:
        o_shape, o_spec = (M, N), pl.BlockSpec((tm, tn), lambda i, j, k: (i, j))
    dims = (((0 if ta else 1,), (1 if tb else 0,)), ((), ()))
    has_add = add is not None

    carry = carry or _NO_CARRY
    n_in = 2 + has_add

    def body(*refs):
        own, c_in, c_out, c_sems = _carry_split(carry, refs, n_in, 1)
        a_ref, b_ref = own[0], own[1]
        add_ref = own[2] if has_add else None
        o_ref, acc_ref = own[-2], own[-1]
        i, j, k = pl.program_id(0), pl.program_id(1), pl.program_id(2)
        _carry_start(carry, c_in, c_out, c_sems, (i == 0) & (j == 0) & (k == 0))

        @pl.when(k == 0)
        def _():
            acc_ref[...] = jnp.zeros_like(acc_ref)

        acc_ref[...] += lax.dot_general(a_ref[...].astype(bf16), b_ref[...].astype(bf16), dims, preferred_element_type=f32)

        @pl.when(k == nk - 1)
        def _():
            r = acc_ref[...] * scale
            if has_add:
                r = r + add_ref[...].astype(f32)
            if len(o_shape) == 4:
                rs = o_shape[2]
                for chip_l in range(o_ref.shape[1]):
                    for core in range(2):
                        dev = 2 * chip_l + core
                        o_ref[core, chip_l] = r[dev * rs:(dev + 1) * rs].astype(out_dtype)
            else:
                o_ref[...] = r.astype(out_dtype)

        _carry_finish(carry, c_in, c_out, c_sems, (i == gm - 1) & (j == gn - 1) & (k == nk - 1))

    ins = [a, b] + ([add] if has_add else []) + list(carry.ins)
    in_specs = [a_spec, b_spec] + ([o_spec] if has_add else []) + [_ANY] * len(carry.ins)
    res = pl.pallas_call(
        body, name=name, grid=(gm, gn, nk), in_specs=in_specs, out_specs=[o_spec] + [_ANY] * len(carry.out_shapes),
        out_shape=[_S(o_shape, out_dtype)] + list(carry.out_shapes),
        scratch_shapes=[pltpu.VMEM((tm, tn), f32)] + list(carry.sems),
        compiler_params=_cparams(("arbitrary",) * 3 if carry.ins else ("parallel", "parallel", "arbitrary")),
    )(*ins)
    out = res[0]
    if len(o_shape) == 4:
        out = out.reshape(N_DEV, o_shape[2], N)
    elif o_seg:
        out = out.reshape(M, N)
    return (out, list(res[1:])) if carry.ins else out


def _row_tile(T, widths):
    budget = 6 * 1024 * 1024
    tb = max(16, budget // (4 * sum(widths)))
    return _pick(T, tb, align=16)


def _rows(fn, rows, params, outs, *, name, carry=None):
    T = rows[0].shape[0]
    nr, npar = len(rows), len(params)
    tb = _row_tile(T, [r.shape[1] for r in rows] + [w for w, _ in outs])
    carry = carry or _NO_CARRY

    def body(*refs):
        own, c_in, c_out, c_sems = _carry_split(carry, refs, nr + npar, len(outs))
        _carry_start(carry, c_in, c_out, c_sems, pl.program_id(0) == 0)
        ins = [r[...].astype(f32) for r in own[: nr + npar]]
        res = fn(*ins)
        for o_ref, r in zip(own[nr + npar:], res):
            o_ref[...] = r.astype(o_ref.dtype)
        _carry_finish(carry, c_in, c_out, c_sems, pl.program_id(0) == T // tb - 1)

    in_specs = [pl.BlockSpec((tb, r.shape[1]), lambda i: (i, 0)) for r in rows]
    in_specs += [pl.BlockSpec(p.shape, lambda i: (0, 0)) for p in params]
    out_specs = [pl.BlockSpec((tb, w), lambda i: (i, 0)) for w, _ in outs]
    res = pl.pallas_call(
        body, name=name, grid=(T // tb,), in_specs=in_specs + [_ANY] * len(carry.ins),
        out_specs=out_specs + [_ANY] * len(carry.out_shapes), out_shape=[_S((T, w), d) for w, d in outs] + list(carry.out_shapes),
        scratch_shapes=list(carry.sems), compiler_params=_cparams(("arbitrary",) if carry.ins else ("parallel",)),
    )(*rows, *params, *carry.ins)
    return (tuple(res[:len(outs)]), list(res[len(outs):])) if carry.ins else tuple(res)


def _rows_bwd(fn, rows, params, cots, *, name, want_rows, row_dtypes=None, adds=None):
    T = rows[0].shape[0]
    nr, npar, nc = len(rows), len(params), len(cots)
    adds = adds or {}
    add_idx = sorted(adds)
    row_dtypes = row_dtypes or {}
    widths = [r.shape[1] for r in rows] + [c.shape[1] for c in cots] + [rows[i].shape[1] for i in want_rows]
    tb = _row_tile(T, widths)

    def body(*refs):
        ins = [r[...].astype(f32) for r in refs[: nr + npar]]
        cot = tuple(r[...].astype(f32) for r in refs[nr + npar: nr + npar + nc])
        add_refs = refs[nr + npar + nc: nr + npar + nc + len(add_idx)]
        out_refs = refs[nr + npar + nc + len(add_idx):]
        _, vjp = jax.vjp(lambda *a: tuple(fn(*a)), *ins)
        g = vjp(cot)
        for o_ref, i in zip(out_refs[: len(want_rows)], want_rows):
            r = g[i]
            if i in adds:
                r = r + add_refs[add_idx.index(i)][...].astype(f32)
            o_ref[...] = r.astype(o_ref.dtype)
        first = pl.program_id(0) == 0
        for o_ref, gp in zip(out_refs[len(want_rows):], g[nr:]):
            @pl.when(first)
            def _(o_ref=o_ref):
                o_ref[...] = jnp.zeros_like(o_ref)

            o_ref[...] += gp

    in_specs = [pl.BlockSpec((tb, r.shape[1]), lambda i: (i, 0)) for r in rows]
    in_specs += [pl.BlockSpec(p.shape, lambda i: (0, 0)) for p in params]
    in_specs += [pl.BlockSpec((tb, c.shape[1]), lambda i: (i, 0)) for c in cots]
    in_specs += [pl.BlockSpec((tb, adds[i].shape[1]), lambda i_: (i_, 0)) for i in add_idx]
    out_specs = [pl.BlockSpec((tb, rows[i].shape[1]), lambda i_: (i_, 0)) for i in want_rows]
    out_specs += [pl.BlockSpec(p.shape, lambda i: (0, 0)) for p in params]
    out_shape = [_S(rows[i].shape, row_dtypes.get(i, f32)) for i in want_rows] + [_S(p.shape, f32) for p in params]
    res = pl.pallas_call(
        body, name=name, grid=(T // tb,), in_specs=in_specs, out_specs=out_specs, out_shape=out_shape,
        compiler_params=_cparams(("arbitrary",)),
    )(*rows, *params, *cots, *[adds[i] for i in add_idx])
    return list(res[: len(want_rows)]), list(res[len(want_rows):])


def _f_rmsnorm(x, g):
    return (x * lax.rsqrt(jnp.mean(x * x, axis=-1, keepdims=True) + EPS) * g,)


def _f_swiglu(ab):
    F = ab.shape[1] // 2
    return (jax.nn.silu(ab[:, :F]) * ab[:, F:],)


def _f_s5_post(y, u, d):
    return (jax.nn.gelu(y + d * u),)


def _f_glu(g, v, b):
    return (g * jax.nn.sigmoid(v + b),)


def _f_gated_norm(y, z, w):
    return _f_rmsnorm(y * jax.nn.silu(z), w)


def _f_merge(gl, p5, pssd, b):
    D = p5.shape[1]
    gates = jax.nn.sigmoid(gl + b)
    return (gates[:, :D] * p5 + gates[:, D:] * pssd,)


def _f_dt(dtr, bias, a_log):
    dt = jax.nn.softplus(dtr + bias)
    return dt, dt * (-jnp.exp(a_log))


def _loss_stage(x, tgt, g, *, name):
    T, D = x.shape
    tb = _row_tile(T, [D, D, D])

    def f(xb, gb, tb_):
        y = _f_rmsnorm(xb, gb)[0]
        return 0.5 * jnp.sum(jnp.mean(jnp.square(y - tb_), axis=-1, keepdims=True), axis=0, keepdims=True)

    def body(x_ref, t_ref, g_ref, l_ref, dx_ref, dg_ref):
        tv = t_ref[...]
        val, vjp = jax.vjp(lambda a, b: f(a, b, tv), x_ref[...], g_ref[...])
        dx, dg = vjp(jnp.ones((1, 1), f32))
        dx_ref[...] = dx

        @pl.when(pl.program_id(0) == 0)
        def _():
            l_ref[...] = jnp.zeros_like(l_ref)
            dg_ref[...] = jnp.zeros_like(dg_ref)

        l_ref[...] += jnp.broadcast_to(val, l_ref.shape)
        dg_ref[...] += dg

    row = pl.BlockSpec((tb, D), lambda i: (i, 0))
    par = pl.BlockSpec((1, D), lambda i: (0, 0))
    return pl.pallas_call(
        body, name=name, grid=(T // tb,), in_specs=[row, row, par],
        out_specs=[pl.BlockSpec((1, LANES), lambda i: (0, 0)), row, par],
        out_shape=[_S((1, LANES), f32), _S((T, D), f32), _S((1, D), f32)], compiler_params=_cparams(("arbitrary",)),
    )(x, tgt, g)


def _shift_down(x, s):
    if s == 0:
        return x
    t = lax.broadcasted_iota(jnp.int32, x.shape, 0)
    return jnp.where(t >= s, pltpu.roll(x, s, axis=0), 0.0)


def _shift_up(x, s):
    if s == 0:
        return x
    T = x.shape[0]
    t = lax.broadcasted_iota(jnp.int32, x.shape, 0)
    return jnp.where(t < T - s, pltpu.roll(x, T - s, axis=0), 0.0)


def _conv_pre(x, w, b):
    pre = b
    for k in range(CONV_K):
        pre = pre + w[k:k + 1, :] * _shift_down(x, CONV_K - 1 - k)
    return pre


def _conv_fwd(x, w, b, *, name):
    T, C = x.shape
    cb = _pick(C, 256)

    def body(x_ref, w_ref, b_ref, o_ref):
        o_ref[...] = jax.nn.silu(_conv_pre(x_ref[...], w_ref[...], b_ref[...]))

    col = pl.BlockSpec((T, cb), lambda j: (0, j))
    return pl.pallas_call(
        body, name=name, grid=(C // cb,), in_specs=[col, pl.BlockSpec((CONV_K, cb), lambda j: (0, j)), pl.BlockSpec((1, cb), lambda j: (0, j))],
        out_specs=col, out_shape=_S((T, C), f32), compiler_params=_cparams(("parallel",)),
    )(x, w, b)


def _conv_bwd(x, w, b, dy, *, name, carry=None):
    T, C = x.shape
    cb = _pick(C, 256)
    carry = carry or _NO_CARRY

    def body(*refs):
        (x_ref, w_ref, b_ref, dy_ref, dx_ref, dw_ref, db_ref), c_in, c_out, c_sems = _carry_split(carry, refs, 4, 3)
        _carry_start(carry, c_in, c_out, c_sems, pl.program_id(0) == 0)
        xv, wv = x_ref[...], w_ref[...]
        pre = _conv_pre(xv, wv, b_ref[...])
        sg = jax.nn.sigmoid(pre)
        dpre = dy_ref[...] * sg * (1.0 + pre * (1.0 - sg))
        dx = jnp.zeros_like(xv)
        for k in range(CONV_K):
            s = CONV_K - 1 - k
            dx = dx + wv[k:k + 1, :] * _shift_up(dpre, s)
            dw_ref[k:k + 1, :] = jnp.sum(dpre * _shift_down(xv, s), axis=0, keepdims=True)
        dx_ref[...] = dx
        db_ref[...] = jnp.sum(dpre, axis=0, keepdims=True)
        _carry_finish(carry, c_in, c_out, c_sems, pl.program_id(0) == C // cb - 1)

    col = pl.BlockSpec((T, cb), lambda j: (0, j))
    wsp = pl.BlockSpec((CONV_K, cb), lambda j: (0, j))
    bsp = pl.BlockSpec((1, cb), lambda j: (0, j))
    res = pl.pallas_call(
        body, name=name, grid=(C // cb,), in_specs=[col, wsp, bsp, col] + [_ANY] * len(carry.ins),
        out_specs=[col, wsp, bsp] + [_ANY] * len(carry.out_shapes),
        out_shape=[_S((T, C), f32), _S((CONV_K, C), f32), _S((1, C), f32)] + list(carry.out_shapes),
        scratch_shapes=list(carry.sems), compiler_params=_cparams(("arbitrary",) if carry.ins else ("parallel",)),
    )(x, w, b, dy, *carry.ins)
    return (*res[:3], list(res[3:]))


def _f_s5_prep(lr, li, ldt, lrb, lib, ldtb, brt, bit):
    def disc(lr_, li_, ldt_):
        dt = jnp.exp(ldt_)
        mag = jnp.exp(lr_ * dt)
        ar, ai = mag * jnp.cos(li_ * dt), mag * jnp.sin(li_ * dt)
        den = lr_ * lr_ + li_ * li_
        cr = ((ar - 1.0) * lr_ + ai * li_) / den
        ci = (ai * lr_ - (ar - 1.0) * li_) / den
        return ar, ai, cr, ci

    ar, ai, _, _ = disc(lr, li, ldt)
    _, _, cr, ci = disc(lrb, lib, ldtb)
    return ar, ai, cr * brt - ci * bit, cr * bit + ci * brt


def _s5_prep(args, *, name):
    G, N = args[0].shape
    GM = args[3].shape[0]

    def body(*refs):
        res = _f_s5_prep(*[r[...] for r in refs[:8]])
        for o, r in zip(refs[8:], res):
            o[...] = r

    return pl.pallas_call(body, name=name, out_shape=[_S((G, N), f32)] * 2 + [_S((GM, N), f32)] * 2)(*args)


def _s5_prep_bwd(args, cots, rsum, *, name):
    G, N = args[0].shape
    GM = args[3].shape[0]

    def body(*refs):
        ins = [r[...] for r in refs[:8]]
        cot = tuple(r[...] for r in refs[8:12])
        rs = refs[12][...]
        _, vjp = jax.vjp(_f_s5_prep, *ins)
        g = vjp(cot)
        fold = lambda v: jnp.dot(rs, v, preferred_element_type=f32, precision=lax.Precision.HIGHEST)
        o = refs[13:]
        o[0][...] = g[0] + fold(g[3])
        o[1][...] = g[1] + fold(g[4])
        o[2][...] = g[2] + fold(jnp.broadcast_to(g[5], (GM, LANES)))[:, 0:1]
        o[3][...] = g[6]
        o[4][...] = g[7]

    return pl.pallas_call(
        body, name=name, out_shape=[_S((G, N), f32), _S((G, N), f32), _S((G, 1), f32), _S((GM, N), f32), _S((GM, N), f32)],
    )(*args, *cots, rsum)


S5_TC = 512


def _s5_local_scan(src, w_r, w_i, a_r, a_i, *, reverse, name, carry=None):
    T, C = src.shape
    nblk, cb, sb = w_r.shape
    NS = nblk * sb
    tc = min(S5_TC, T)
    nT, nt = T // tc, tc // NSEG
    tmap = (lambda i: nT - 1 - i) if reverse else (lambda i: i)

    carry = carry or _NO_CARRY

    def body(*refs):
        own, c_in, c_out, c_sems = _carry_split(carry, refs, 5, 4)
        u_ref, wr_ref, wi_ref, ar_ref, ai_ref, sr_ref, si_ref, pr_ref, pi_ref, st_r, st_i, pw_r, pw_i = own
        _carry_start(carry, c_in, c_out, c_sems, (pl.program_id(0) == 0) & (pl.program_id(1) == 0))

        @pl.when(pl.program_id(1) == 0)
        def _():
            st_r[...] = jnp.zeros_like(st_r)
            st_i[...] = jnp.zeros_like(st_i)
            pw_r[...] = jnp.ones_like(pw_r)
            pw_i[...] = jnp.zeros_like(pw_i)

        u = u_ref[...].astype(bf16)
        sr_ref[...] = jnp.dot(u, wr_ref[...], preferred_element_type=f32)
        si_ref[...] = jnp.dot(u, wi_ref[...], preferred_element_type=f32)
        ar = jnp.broadcast_to(ar_ref[...], (NSEG, sb))
        ai = jnp.broadcast_to(ai_ref[...], (NSEG, sb))

        def step(k, c):
            cr, ci, qr, qi = c
            kk = (nt - 1 - k) if reverse else k
            rows = pl.ds(pl.multiple_of(kk * NSEG, NSEG), NSEG)
            nr = ar * cr - ai * ci + sr_ref[rows, :]
            ni = ar * ci + ai * cr + si_ref[rows, :]
            sr_ref[rows, :] = nr
            si_ref[rows, :] = ni
            return nr, ni, ar * qr - ai * qi, ar * qi + ai * qr

        cr, ci, qr, qi = lax.fori_loop(0, nt, step, (st_r[...], st_i[...], pw_r[...], pw_i[...]), unroll=8)
        st_r[...], st_i[...], pw_r[...], pw_i[...] = cr, ci, qr, qi
        pr_ref[...] = qr
        pi_ref[...] = qi
        _carry_finish(carry, c_in, c_out, c_sems, (pl.program_id(0) == nblk - 1) & (pl.program_id(1) == nT - 1))

    blk = pl.BlockSpec((tc, sb), lambda j, i: (tmap(i), j))
    wsp = pl.BlockSpec((None, cb, sb), lambda j, i: (j, 0, 0))
    asp = pl.BlockSpec((1, sb), lambda j, i: (0, j))
    psp = pl.BlockSpec((NSEG, sb), lambda j, i: (0, j))
    res = pl.pallas_call(
        body, name=name, grid=(nblk, nT),
        in_specs=[pl.BlockSpec((tc, cb), lambda j, i: (tmap(i), j)), wsp, wsp, asp, asp] + [_ANY] * len(carry.ins),
        out_specs=[blk, blk, psp, psp] + [_ANY] * len(carry.out_shapes),
        out_shape=[_S((T, NS), f32)] * 2 + [_S((NSEG, NS), f32)] * 2 + list(carry.out_shapes),
        scratch_shapes=[pltpu.VMEM((NSEG, sb), f32)] * 4 + list(carry.sems),
        compiler_params=_cparams(("arbitrary", "arbitrary") if carry.ins else ("parallel", "arbitrary")),
    )(src, w_r, w_i, a_r, a_i, *carry.ins)
    return (*res[:4], list(res[4:])) if carry.ins else res


def _s5_carry(e_r, e_i, p_r, p_i, *, reverse, name):
    NS = e_r.shape[1]

    def body(er_ref, ei_ref, pr_ref, pi_ref, cr_ref, ci_ref):
        ar, ai = pr_ref[0:1, :], pi_ref[0:1, :]
        cr = jnp.zeros((1, NS), f32)
        ci = jnp.zeros((1, NS), f32)
        order = list(range(NSEG - 1, -1, -1)) if reverse else list(range(NSEG))
        cr_ref[order[0]:order[0] + 1, :] = cr
        ci_ref[order[0]:order[0] + 1, :] = ci
        for prev, q in zip(order[:-1], order[1:]):
            er, ei = er_ref[prev:prev + 1, :], ei_ref[prev:prev + 1, :]
            cr, ci = er + ar * cr - ai * ci, ei + ar * ci + ai * cr
            cr_ref[q:q + 1, :] = cr
            ci_ref[q:q + 1, :] = ci

    return pl.pallas_call(body, name=name, out_shape=[_S((NSEG, NS), f32)] * 2)(e_r, e_i, p_r, p_i)


def _s5_fix_out(sl_r, sl_i, a_r, a_i, c_r, c_i, wc_r, wc_i, *, name):
    T, NS = sl_r.shape
    nblk, sb, cb = wc_r.shape
    tc = min(S5_TC, T)
    nT, nt = T // tc, tc // NSEG

    def body(lr_ref, li_ref, ar_ref, ai_ref, cr_ref, ci_ref, wr_ref, wi_ref, sr_ref, si_ref, y_ref, pw_r, pw_i):
        @pl.when(pl.program_id(1) == 0)
        def _():
            pw_r[...] = jnp.ones_like(pw_r)
            pw_i[...] = jnp.zeros_like(pw_i)

        ar = jnp.broadcast_to(ar_ref[...], (NSEG, sb))
        ai = jnp.broadcast_to(ai_ref[...], (NSEG, sb))
        cr, ci = cr_ref[...], ci_ref[...]

        def step(k, c):
            qr, qi = c
            qr, qi = ar * qr - ai * qi, ar * qi + ai * qr
            rows = pl.ds(pl.multiple_of(k * NSEG, NSEG), NSEG)
            sr_ref[rows, :] = lr_ref[rows, :] + qr * cr - qi * ci
            si_ref[rows, :] = li_ref[rows, :] + qr * ci + qi * cr
            return qr, qi

        qr, qi = lax.fori_loop(0, nt, step, (pw_r[...], pw_i[...]), unroll=8)
        pw_r[...], pw_i[...] = qr, qi
        y_ref[...] = (jnp.dot(sr_ref[...].astype(bf16), wr_ref[...], preferred_element_type=f32)
                      - jnp.dot(si_ref[...].astype(bf16), wi_ref[...], preferred_element_type=f32))

    blk = pl.BlockSpec((tc, sb), lambda j, i: (i, j))
    asp = pl.BlockSpec((1, sb), lambda j, i: (0, j))
    csp = pl.BlockSpec((NSEG, sb), lambda j, i: (0, j))
    wsp = pl.BlockSpec((None, sb, cb), lambda j, i: (j, 0, 0))
    return pl.pallas_call(
        body, name=name, grid=(nblk, nT), in_specs=[blk, blk, asp, asp, csp, csp, wsp, wsp],
        out_specs=[blk, blk, pl.BlockSpec((tc, cb), lambda j, i: (i, j))],
        out_shape=[_S((T, NS), f32)] * 2 + [_S((T, nblk * cb), f32)],
        scratch_shapes=[pltpu.VMEM((NSEG, sb), f32)] * 2, compiler_params=_cparams(("parallel", "arbitrary")),
    )(sl_r, sl_i, a_r, a_i, c_r, c_i, wc_r, wc_i)


def _s5_fix_bwd(ql_r, ql_i, ab_r, ab_i, c_r, c_i, s_r, s_i, sb_r, sb_i, u, dy, du_add, w_r, w_i, *, name):
    T, NS = ql_r.shape
    nblk, cb, sb = w_r.shape
    tc = min(S5_TC, T)
    nT, nt = T // tc, tc // NSEG
    tmap = lambda i: nT - 1 - i

    def body(lr_ref, li_ref, ar_ref, ai_ref, cr_ref, ci_ref, sr_ref, si_ref, br_ref, bi_ref, u_ref, dy_ref, dua_ref, wr_ref, wi_ref,
             du_ref, dwr_ref, dwi_ref, dcr_ref, dci_ref, dar_ref, dai_ref, pw_r, pw_i, ac_r, ac_i, q_r, q_i):
        first = pl.program_id(1) == 0

        @pl.when(first)
        def _():
            pw_r[...] = jnp.ones_like(pw_r)
            pw_i[...] = jnp.zeros_like(pw_i)
            ac_r[...] = jnp.zeros_like(ac_r)
            ac_i[...] = jnp.zeros_like(ac_i)
            dwr_ref[...] = jnp.zeros_like(dwr_ref)
            dwi_ref[...] = jnp.zeros_like(dwi_ref)
            dcr_ref[...] = jnp.zeros_like(dcr_ref)
            dci_ref[...] = jnp.zeros_like(dci_ref)

        ar = jnp.broadcast_to(ar_ref[...], (NSEG, sb))
        ai = jnp.broadcast_to(ai_ref[...], (NSEG, sb))
        cr, ci = cr_ref[...], ci_ref[...]

        def fix(rows, qr, qi, spr, spi, accr, acci):
            qr, qi = ar * qr - ai * qi, ar * qi + ai * qr
            xr = lr_ref[rows, :] + qr * cr - qi * ci
            xi = li_ref[rows, :] + qr * ci + qi * cr
            q_r[rows, :] = xr
            q_i[rows, :] = xi
            return qr, qi, accr + xr * spr + xi * spi, acci + xi * spr - xr * spi

        def step(k, c):
            qr, qi, accr, acci = c
            kk = nt - 1 - k
            rows = pl.ds(pl.multiple_of(kk * NSEG, NSEG), NSEG)
            prev = pl.ds(pl.multiple_of((kk - 1) * NSEG, NSEG), NSEG)
            return fix(rows, qr, qi, sr_ref[prev, :], si_ref[prev, :], accr, acci)

        c = lax.fori_loop(0, nt - 1, step, (pw_r[...], pw_i[...], ac_r[...], ac_i[...]), unroll=7)
        qr, qi, accr, acci = fix(pl.ds(0, NSEG), *c[:2], br_ref[...], bi_ref[...], *c[2:])
        pw_r[...], pw_i[...], ac_r[...], ac_i[...] = qr, qi, accr, acci

        qrb, qib = q_r[...].astype(bf16), q_i[...].astype(bf16)
        nt_dims = (((1,), (1,)), ((), ()))
        tn_dims = (((0,), (0,)), ((), ()))
        du_ref[...] = (dua_ref[...] + lax.dot_general(qrb, wr_ref[...], nt_dims, preferred_element_type=f32)
                       + lax.dot_general(qib, wi_ref[...], nt_dims, preferred_element_type=f32))
        ub = u_ref[...].astype(bf16)
        dwr_ref[...] += lax.dot_general(ub, qrb, tn_dims, preferred_element_type=f32)
        dwi_ref[...] += lax.dot_general(ub, qib, tn_dims, preferred_element_type=f32)
        dyb = dy_ref[...].astype(bf16)
        dcr_ref[...] += lax.dot_general(sr_ref[...].astype(bf16), dyb, tn_dims, preferred_element_type=f32)
        dci_ref[...] -= lax.dot_general(si_ref[...].astype(bf16), dyb, tn_dims, preferred_element_type=f32)

        @pl.when(pl.program_id(1) == nT - 1)
        def _():
            dar_ref[...] = jnp.sum(accr, axis=0, keepdims=True)
            dai_ref[...] = jnp.sum(acci, axis=0, keepdims=True)

    blk = pl.BlockSpec((tc, sb), lambda j, i: (tmap(i), j))
    asp = pl.BlockSpec((1, sb), lambda j, i: (0, j))
    csp = pl.BlockSpec((NSEG, sb), lambda j, i: (0, j))
    bsp = pl.BlockSpec((None, NSEG, sb), lambda j, i: (tmap(i), 0, j))
    chn = pl.BlockSpec((tc, cb), lambda j, i: (tmap(i), j))
    wsp = pl.BlockSpec((None, cb, sb), lambda j, i: (j, 0, 0))
    wcs = pl.BlockSpec((None, sb, cb), lambda j, i: (j, 0, 0))
    return pl.pallas_call(
        body, name=name, grid=(nblk, nT), in_specs=[blk, blk, asp, asp, csp, csp, blk, blk, bsp, bsp, chn, chn, chn, wsp, wsp],
        out_specs=[chn, wsp, wsp, wcs, wcs, asp, asp],
        out_shape=[_S((T, nblk * cb), f32), _S((nblk, cb, sb), f32), _S((nblk, cb, sb), f32), _S((nblk, sb, cb), f32),
                   _S((nblk, sb, cb), f32), _S((1, NS), f32), _S((1, NS), f32)],
        scratch_shapes=[pltpu.VMEM((NSEG, sb), f32)] * 4 + [pltpu.VMEM((tc, sb), f32)] * 2,
        compiler_params=_cparams(("parallel", "arbitrary")),
    )(ql_r, ql_i, ab_r, ab_i, c_r, c_i, s_r, s_i, sb_r, sb_i, u, dy, du_add, w_r, w_i)


SSD2_TB = 512
_HI = lax.Precision.HIGHEST
_NN = (((1,), (0,)), ((), ()))
_NT = (((1,), (1,)), ((), ()))
_TN = (((0,), (0,)), ((), ()))


def _dotf(a, b, dims):
    return lax.dot_general(a.astype(bf16), b.astype(bf16), dims, preferred_element_type=f32)


def _doth(a, b, dims=_NN):
    return lax.dot_general(a, b, dims, preferred_element_type=f32, precision=_HI)


def _ssd_consts(hpg):
    W = hpg * CHUNK
    i = lax.broadcasted_iota(jnp.int32, (CHUNK, CHUNK), 0)
    j = lax.broadcasted_iota(jnp.int32, (CHUNK, CHUNK), 1)
    tril = (i >= j).astype(f32)
    r = lax.broadcasted_iota(jnp.int32, (W, W), 0)
    c = lax.broadcasted_iota(jnp.int32, (W, W), 1)
    bd = (r // CHUNK == c // CHUNK).astype(f32)
    triu_bd = bd * (r <= c).astype(f32)
    e_r = lax.broadcasted_iota(jnp.int32, (W, LANES), 0)
    e_c = lax.broadcasted_iota(jnp.int32, (W, LANES), 1)
    ered = (e_r // HEADDIM == e_c).astype(f32)
    return tril, jnp.tile(tril, (1, hpg)), bd, triu_bd, ered


def _ssd2_specs(G, hpg, tb, tmap, b_off, c_off):
    W = hpg * HEADDIM
    ncb = tb // CHUNK
    xsp = pl.BlockSpec((tb, W), lambda g, i: (tmap(i), g))
    bsp = pl.BlockSpec((tb, SSD_STATE), lambda g, i: (tmap(i), b_off + g))
    csp = pl.BlockSpec((tb, SSD_STATE), lambda g, i: (tmap(i), c_off + g))
    rsp = pl.BlockSpec((None, ncb, W), lambda g, i: (g, tmap(i), 0))
    dsp = pl.BlockSpec((1, W), lambda g, i: (0, g))
    hsp = pl.BlockSpec((None, ncb, SSD_STATE, W), lambda g, i: (g, tmap(i), 0, 0))
    const = lambda a: pl.BlockSpec(a.shape, lambda g, i: (0, 0))
    return xsp, bsp, csp, rsp, dsp, hsp, const


def _tile_rows(a, n):
    return jnp.concatenate([a] * n, axis=0)


def _ssd2_fwd(xc, dt4, a4, dtw, aw, d4, consts, *, d_inner, name, carry=None):
    carry = carry or _NO_CARRY
    T = xc.shape[0]
    G, nc, W = dtw.shape
    hpg = W // CHUNK
    tb = min(SSD2_TB, T)
    nb, ncb = T // tb, tb // CHUNK
    b_off = d_inner // SSD_STATE
    xsp, bsp, csp, rsp, dsp, hsp, const = _ssd2_specs(G, hpg, tb, lambda i: i, b_off, b_off + G)
    tril, mask4, bd, triu_bd, _ = consts

    def body(*refs):
        own, c_in, c_out, c_sems = _carry_split(carry, refs, 12, 2)
        x_ref, b_ref, c_ref, dt_ref, a_ref, dtw_ref, aw_ref, d_ref, tril_ref, mask_ref, bd_ref, tbd_ref, y_ref, hs_ref, h_scr = own
        _carry_start(carry, c_in, c_out, c_sems, (pl.program_id(0) == 0) & (pl.program_id(1) == 0))

        @pl.when(pl.program_id(1) == 0)
        def _():
            h_scr[...] = jnp.zeros_like(h_scr)

        acs_rows = _doth(aw_ref[...], tbd_ref[...])
        ht = h_scr[...]
        for c in range(ncb):
            rows = slice(c * CHUNK, (c + 1) * CHUNK)
            x, bm, cm = x_ref[rows, :], b_ref[rows, :], c_ref[rows, :]
            acs = _doth(tril_ref[...], a_ref[rows, :])
            lmat = jnp.where(mask_ref[...] > 0, jnp.exp(jnp.minimum(acs - acs_rows[c:c + 1, :], 0.0)), 0.0)
            m4 = _dotf(cm, _tile_rows(bm, hpg), _NT) * lmat * dtw_ref[c:c + 1, :]
            xbd = _tile_rows(x, hpg) * bd_ref[...]
            hs_ref[c] = ht
            y_ref[rows, :] = _dotf(m4, xbd, _NN) + _dotf(cm, ht, _NN) * jnp.exp(acs) + d_ref[...] * x
            a_last = acs[CHUNK - 1:CHUNK, :]
            xw = x * (jnp.exp(a_last - acs) * dt_ref[rows, :])
            ht = ht * jnp.exp(a_last) + _dotf(bm, xw, _TN)
        h_scr[...] = ht
        _carry_finish(carry, c_in, c_out, c_sems, (pl.program_id(0) == G - 1) & (pl.program_id(1) == nb - 1))

    res = pl.pallas_call(
        body, name=name, grid=(G, nb),
        in_specs=[xsp, bsp, csp, xsp, xsp, rsp, rsp, dsp, const(tril), const(mask4), const(bd), const(triu_bd)] + [_ANY] * len(carry.ins),
        out_specs=[xsp, hsp] + [_ANY] * len(carry.out_shapes),
        out_shape=[_S((T, G * W), f32), _S((G, nc, SSD_STATE, W), f32)] + list(carry.out_shapes),
        scratch_shapes=[pltpu.VMEM((SSD_STATE, W), f32)] + list(carry.sems),
        compiler_params=_cparams(("arbitrary", "arbitrary") if carry.ins else ("parallel", "arbitrary")),
    )(xc, xc, xc, dt4, a4, dtw, aw, d4, tril, mask4, bd, triu_bd, *carry.ins)
    return res[0], res[1], list(res[2:])


def _ssd2_bwd(xc, dt4, a4, dtw, aw, d4, consts, hs, dy, *, d_inner, name):
    T = xc.shape[0]
    G, nc, W = dtw.shape
    hpg = W // CHUNK
    tb = min(SSD2_TB, T)
    nb, ncb = T // tb, tb // CHUNK
    b_off = d_inner // SSD_STATE
    tmap = lambda i: nb - 1 - i
    xsp, bsp, csp, rsp, dsp, hsp, const = _ssd2_specs(G, hpg, tb, tmap, b_off, b_off + G)
    gsp = pl.BlockSpec((tb, SSD_STATE), lambda g, i: (tmap(i), g))
    ddsp = pl.BlockSpec((None, 1, LANES), lambda g, i: (g, 0, 0))
    tril, mask4, bd, triu_bd, ered = consts

    def body(x_ref, b_ref, c_ref, dt_ref, a_ref, dtw_ref, aw_ref, d_ref, tril_ref, mask_ref, bd_ref, tbd_ref, er_ref, hs_ref, dy_ref,
             dx_ref, db_ref, dc_ref, ddtc_ref, dac_ref, ddtw_ref, daw_ref, dd_ref, g_scr, dd_scr, rw_scr, tl_scr):
        first = pl.program_id(1) == 0

        @pl.when(first)
        def _():
            g_scr[...] = jnp.zeros_like(g_scr)
            dd_scr[...] = jnp.zeros_like(dd_scr)

        mask = mask_ref[...] > 0
        lane_in_block = lax.broadcasted_iota(jnp.int32, mask.shape, 1) & (CHUNK - 1)
        maskt = lax.broadcasted_iota(jnp.int32, mask.shape, 0) <= lane_in_block
        acs_rows = _doth(aw_ref[...], tbd_ref[...])
        dht = g_scr[...]
        dd = dd_scr[...]
        for c in range(ncb - 1, -1, -1):
            rows = slice(c * CHUNK, (c + 1) * CHUNK)
            x, bm, cm, dyc = x_ref[rows, :], b_ref[rows, :], c_ref[rows, :], dy_ref[rows, :]
            dtc, dtr = dt_ref[rows, :], dtw_ref[c:c + 1, :]
            ht = hs_ref[c]
            acs = _doth(tril_ref[...], a_ref[rows, :])
            seg = acs - acs_rows[c:c + 1, :]
            lmat = jnp.where(mask, jnp.exp(jnp.minimum(seg, 0.0)), 0.0)
            lmat_t = jnp.where(maskt, jnp.exp(jnp.minimum(-seg, 0.0)), 0.0)
            btile, ctile = _tile_rows(bm, hpg), _tile_rows(cm, hpg)
            g4 = _dotf(cm, btile, _NT)
            gt4 = _dotf(bm, ctile, _NT)
            m4 = g4 * lmat * dtr
            mt4 = gt4 * lmat_t * dtc
            xbd = _tile_rows(x, hpg) * bd_ref[...]
            dybd = _tile_rows(dyc, hpg) * bd_ref[...]
            dm4 = _dotf(dyc, xbd, _NT)
            dmt4 = _dotf(x, dybd, _NT)
            dx = d_ref[...] * dyc + _dotf(mt4, dybd, _NN)
            dd = dd + jnp.sum(dyc * x, axis=0, keepdims=True)
            e4 = dm4 * m4
            dc = _dotf(dm4 * lmat * dtr, btile, _NN)
            db = _dotf(dmt4 * lmat_t * dtc, ctile, _NN)
            decay = jnp.exp(acs)
            yoff = _dotf(cm, ht, _NN) * decay
            dz = dyc * decay
            dc = dc + _dotf(dz, ht, _NT)
            dht_prev = _dotf(cm, dz, _TN)
            a_last = acs[CHUNK - 1:CHUNK, :]
            ea_last = jnp.exp(a_last)
            erel = jnp.exp(a_last - acs)
            dte = erel * dtc
            dxw = _dotf(bm, dht, _NN)
            db = db + _dotf(x * dte, dht, _NT)
            dx = dx + dxw * dte
            q4 = dxw * x
            dacs = e4 + dyc * yoff - q4 * dte
            col = jnp.concatenate([q4 * erel, _doth(tril_ref[...], dacs, _TN)], axis=0)
            col = _doth(col, er_ref[...])
            ddtc_ref[rows, :] = col[:CHUNK]
            dac_ref[rows, :] = col[CHUNK:]
            ddtw_ref[c:c + 1, :] = jnp.sum(dm4 * g4 * lmat, axis=0, keepdims=True)
            rw_scr[c:c + 1, :] = -jnp.sum(e4, axis=0, keepdims=True)
            tl_scr[c:c + 1, :] = jnp.sum(q4 * dte, axis=0, keepdims=True) + ea_last * jnp.sum(dht * ht, axis=0, keepdims=True)
            dx_ref[rows, :] = dx
            db_ref[rows, :] = db
            dc_ref[rows, :] = dc
            dht = dht_prev + dht * ea_last
        daw_ref[...] = _doth(rw_scr[...], tbd_ref[...], _NT) + _doth(tl_scr[...], bd_ref[...])
        g_scr[...] = dht
        dd_scr[...] = dd

        @pl.when(pl.program_id(1) == nb - 1)
        def _():
            dd_ref[...] = _doth(dd, er_ref[...])

    return pl.pallas_call(
        body, name=name, grid=(G, nb),
        in_specs=[xsp, bsp, csp, xsp, xsp, rsp, rsp, dsp, const(tril), const(mask4), const(bd), const(triu_bd), const(ered), hsp, xsp],
        out_specs=[xsp, gsp, gsp, gsp, gsp, rsp, rsp, ddsp],
        out_shape=[_S((T, G * W), f32), _S((T, G * SSD_STATE), f32), _S((T, G * SSD_STATE), f32), _S((T, G * LANES), f32),
                   _S((T, G * LANES), f32), _S(dtw.shape, f32), _S(dtw.shape, f32), _S((G, 1, LANES), f32)],
        scratch_shapes=[pltpu.VMEM((SSD_STATE, W), f32), pltpu.VMEM((1, W), f32), pltpu.VMEM((ncb, W), f32), pltpu.VMEM((ncb, W), f32)],
        compiler_params=_cparams(("parallel", "arbitrary")),
    )(xc, xc, xc, dt4, a4, dtw, aw, d4, tril, mask4, bd, triu_bd, ered, hs, dy)


def _peers():
    x, y, c = lax.axis_index("x"), lax.axis_index("y"), lax.axis_index("c")
    return x, y, c


_ANY = pl.BlockSpec(memory_space=pl.ANY)
N_CHIP = N_DEV // 2


def _all_gather(shards, *, name):
    n = len(shards)
    carry = _carry_gather(shards)

    def body(*refs):
        x_refs, out_refs, sems = refs[:n], refs[n:2 * n], refs[2 * n:]
        _gather_start(x_refs, out_refs, sems)
        _gather_finish(x_refs, out_refs, sems)

    return pl.pallas_call(
        body, name=name, out_shape=list(carry.out_shapes), in_specs=[_ANY] * n, out_specs=[_ANY] * n, scratch_shapes=list(carry.sems),
    )(*shards)


def _gather_parts(x_refs, out_refs, sems):
    send_sems, recv_sems, local_sems = sems
    x, y, c = _peers()
    me, sibling = (x, y, c), (x, y, 1 - c)
    chips = [(1 - x, y), (x, 1 - y), (1 - x, 1 - y)]
    n = len(x_refs)

    def copy(a, r, block, to, src=None):
        px, py, pc = block
        slot = out_refs[a].at[4 * px + 2 * py + pc]
        return pltpu.make_async_remote_copy(
            src_ref=slot if src is None else src, dst_ref=slot, send_sem=send_sems.at[7 * a + r],
            recv_sem=recv_sems.at[7 * a + r], device_id=to, device_id_type=MESH)

    mine = [pltpu.make_async_copy(x_refs[a], out_refs[a].at[4 * x + 2 * y + c], local_sems.at[a]) for a in range(n)]
    first = []
    for a in range(n):
        first.append(copy(a, 0, me, sibling, src=x_refs[a]))
        first += [copy(a, 1 + j, me, (*chip, c), src=x_refs[a]) for j, chip in enumerate(chips)]
    return copy, mine, first, me, sibling, chips, c, n


def _gather_start(x_refs, out_refs, sems):
    _, mine, first, *_ = _gather_parts(x_refs, out_refs, sems)
    for cp in mine + first:
        cp.start()


def _gather_finish(x_refs, out_refs, sems):
    copy, mine, first, me, sibling, chips, c, n = _gather_parts(x_refs, out_refs, sems)
    passed = []
    for j, chip in enumerate(chips):
        for a in range(n):
            copy(a, 1 + j, (*chip, c), me).wait_recv()
            fwd = copy(a, 4 + j, (*chip, c), sibling)
            fwd.start()
            passed.append(fwd)
    for a in range(n):
        copy(a, 0, sibling, me).wait_recv()
    for j, chip in enumerate(chips):
        for a in range(n):
            copy(a, 4 + j, (*chip, 1 - c), me).wait_recv()
    for cp in first + passed:
        cp.wait_send()
    for cp in mine:
        cp.wait()


def _exchange_sibling(slots, *, name):
    n = len(slots)

    def body(*refs):
        x_refs, sib_refs = refs[:n], refs[n:2 * n]
        send_sems, recv_sems = refs[2 * n:]
        x, y, c = _peers()
        give = [pltpu.make_async_remote_copy(
            src_ref=x_refs[a].at[pl.ds(N_CHIP * (1 - c), N_CHIP)], dst_ref=sib_refs[a], send_sem=send_sems.at[a],
            recv_sem=recv_sems.at[a], device_id=(x, y, 1 - c), device_id_type=MESH) for a in range(n)]
        for cp in give:
            cp.start()
        for cp in give:
            cp.wait_recv()
        for cp in give:
            cp.wait_send()

    return list(pl.pallas_call(
        body, name=name, out_shape=[_S((N_CHIP,) + s.shape[1:], s.dtype) for s in slots], in_specs=[_ANY] * n, out_specs=[_ANY] * n,
        scratch_shapes=[pltpu.SemaphoreType.DMA((n,)), pltpu.SemaphoreType.DMA((n,))],
    )(*slots))


def _chip_sum(slots, sib, core, *, name):
    _, R, W = slots.shape
    tr = _pick(R, max(16, (1 << 20) // (4 * W)), align=16)

    def body(core_ref, x_ref, s_ref, o_ref):
        o_ref[...] = (x_ref[...].astype(f32) + s_ref[...].astype(f32)).astype(o_ref.dtype)

    blk = pl.BlockSpec((None, tr, W), lambda t, i, core_ref: (t, i, 0))
    return pl.pallas_call(
        body, name=name, out_shape=_S(sib.shape, slots.dtype),
        grid_spec=pltpu.PrefetchScalarGridSpec(
            num_scalar_prefetch=1, grid=(N_CHIP, R // tr),
            in_specs=[pl.BlockSpec((None, tr, W), lambda t, i, core_ref: (N_CHIP * core_ref[0] + t, i, 0)), blk], out_specs=blk),
        compiler_params=_cparams(("parallel", "parallel")),
    )(core, slots, sib)


def _exchange_chips(parts, *, name):
    n = len(parts)

    def body(*refs):
        copies = _chip_copies(refs[:n], refs[n:2 * n], *refs[2 * n:])
        _start_all(copies)
        _wait_all(copies)

    return list(pl.pallas_call(
        body, name=name, out_shape=_chip_out_shapes(parts), in_specs=[_ANY] * n, out_specs=[_ANY] * n,
        scratch_shapes=_chip_sems(n),
    )(*parts))


def _chip_out_shapes(parts):
    return [_S((N_CHIP - 1,) + p.shape[1:], p.dtype) for p in parts]


def _chip_sems(n):
    return [pltpu.SemaphoreType.DMA((3 * n,)), pltpu.SemaphoreType.DMA((3 * n,))]


def _chip_copies(p_refs, out_refs, send_sems, recv_sems):
    x, y, c = _peers()
    copies = []
    for j in range(1, N_CHIP):
        tx, ty = x ^ (j >> 1), y ^ (j & 1)
        for a in range(len(p_refs)):
            copies.append(pltpu.make_async_remote_copy(
                src_ref=p_refs[a].at[2 * tx + ty], dst_ref=out_refs[a].at[j - 1], send_sem=send_sems.at[3 * a + j - 1],
                recv_sem=recv_sems.at[3 * a + j - 1], device_id=(tx, ty, c), device_id_type=MESH))
    return copies


def _start_all(copies):
    for cp in copies:
        cp.start()


def _wait_all(copies):
    for cp in copies:
        cp.wait_recv()
    for cp in copies:
        cp.wait_send()


class _Carry(NamedTuple):
    ins: tuple = ()
    out_shapes: tuple = ()
    sems: tuple = ()
    start: Callable = None
    finish: Callable = None


_NO_CARRY = _Carry()


def _carry_chips(parts):
    return _Carry(tuple(parts), tuple(_chip_out_shapes(parts)), tuple(_chip_sems(len(parts))),
                  lambda i, o, s: _start_all(_chip_copies(i, o, *s)), lambda i, o, s: _wait_all(_chip_copies(i, o, *s)))


def _carry_gather(shards):
    n = len(shards)
    sems = (pltpu.SemaphoreType.DMA((7 * n,)), pltpu.SemaphoreType.DMA((7 * n,)), pltpu.SemaphoreType.DMA((n,)))
    return _Carry(tuple(shards), tuple(_S((N_DEV,) + s.shape, s.dtype) for s in shards), sems, _gather_start, _gather_finish)


def _carry_split(carry, refs, n_in, n_out):
    ci, co, cs = len(carry.ins), len(carry.out_shapes), len(carry.sems)
    refs = list(refs)
    own_in, c_in = refs[:n_in], refs[n_in:n_in + ci]
    own_out, c_out = refs[n_in + ci:n_in + ci + n_out], refs[n_in + ci + n_out:n_in + ci + n_out + co]
    rest = refs[n_in + ci + n_out + co:]
    own_scratch, c_sems = rest[:len(rest) - cs], rest[len(rest) - cs:]
    return own_in + own_out + own_scratch, c_in, c_out, c_sems


def _carry_start(carry, c_in, c_out, c_sems, first):
    if carry.ins:
        @pl.when(first)
        def _():
            carry.start(c_in, c_out, c_sems)


def _carry_finish(carry, c_in, c_out, c_sems, last):
    if carry.ins:
        @pl.when(last)
        def _():
            carry.finish(c_in, c_out, c_sems)


def _sum_slots(stack, *, name):
    n, R, W = stack.shape
    tr = _pick(R, 1024, align=8)

    def body(s_ref, o_ref):
        acc = s_ref[0]
        for k in range(1, n):
            acc = acc + s_ref[k]
        o_ref[...] = acc

    return pl.pallas_call(
        body, name=name, grid=(R // tr,), in_specs=[pl.BlockSpec((n, tr, W), lambda i: (0, i, 0))],
        out_specs=pl.BlockSpec((tr, W), lambda i: (i, 0)), out_shape=_S((R, W), f32), compiler_params=_cparams(("parallel",)),
    )(stack)


def _adamw_math(gv, wv, mv, vv):
    c1 = 1.0 / (1.0 - ADAM_B1 ** ADAM_STEP)
    c2 = 1.0 / (1.0 - ADAM_B2 ** ADAM_STEP)
    nm = ADAM_B1 * mv + (1.0 - ADAM_B1) * gv
    nv = ADAM_B2 * vv + (1.0 - ADAM_B2) * jnp.square(gv)
    return -ADAM_LR * ((nm * c1) / (jnp.sqrt(nv * c2) + ADAM_EPS) + ADAM_WD * wv), nm, nv


def _adamw(g, w, m, v, *, name):
    R, W = w.shape
    tr = _pick(R, max(8, (1 << 20) // (4 * W)), align=8)

    def body(g_ref, w_ref, m_ref, v_ref, d_ref, nm_ref, nv_ref):
        d_ref[...], nm_ref[...], nv_ref[...] = _adamw_math(g_ref[...], w_ref[...], m_ref[...], v_ref[...])

    sp = pl.BlockSpec((tr, W), lambda i: (i, 0))
    return pl.pallas_call(
        body, name=name, grid=(R // tr,), in_specs=[sp] * 4, out_specs=[sp] * 3, out_shape=[_S((R, W), f32)] * 3,
        compiler_params=_cparams(("parallel",)),
    )(g, w, m, v)


def _reduce_adamw(own, arrived, chip, w, m, v, *, name):
    n, R, W = arrived.shape
    tr = _pick(R, max(16, (1 << 20) // (4 * W)), align=16)

    def body(chip_ref, o_ref, p_ref, w_ref, m_ref, v_ref, g_ref, d_ref, nm_ref, nv_ref):
        gv = o_ref[...].astype(f32)
        for k in range(n):
            gv = gv + p_ref[k].astype(f32)
        g_ref[...] = gv
        d_ref[...], nm_ref[...], nv_ref[...] = _adamw_math(gv, w_ref[...], m_ref[...], v_ref[...])

    sp = pl.BlockSpec((tr, W), lambda i, chip_ref: (i, 0))
    return pl.pallas_call(
        body, name=name, out_shape=[_S((R, W), f32)] * 4,
        grid_spec=pltpu.PrefetchScalarGridSpec(
            num_scalar_prefetch=1, grid=(R // tr,),
            in_specs=[pl.BlockSpec((None, tr, W), lambda i, chip_ref: (chip_ref[0], i, 0)),
                      pl.BlockSpec((n, tr, W), lambda i, chip_ref: (0, i, 0))] + [sp] * 3, out_specs=[sp] * 4),
        compiler_params=_cparams(("parallel",)),
    )(chip, own, arrived, w, m, v)


def _pieces(seg_start, seg_len, shard_w):
    out, col = [], seg_start
    while col < seg_start + seg_len:
        k, a = divmod(col, shard_w)
        n = min(shard_w - a, seg_start + seg_len - col)
        out.append((k, a, col - seg_start, n))
        col += n
    return out


def _unshard_w_in(g, seg_lens, *, name):
    _, D, w = g.shape
    starts = [sum(seg_lens[:i]) for i in range(len(seg_lens))]
    widths = [max(n, LANES) for n in seg_lens]
    tm = _pick(D, 256, align=16)

    def body(g_ref, *o_refs):
        for o_ref, s0, n in zip(o_refs, starts, seg_lens):
            if n < o_ref.shape[1]:
                o_ref[...] = jnp.zeros_like(o_ref)
            for k, a, off, m in _pieces(s0, n, w):
                o_ref[:, off:off + m] = g_ref[k, :, a:a + m]

    return pl.pallas_call(
        body, name=name, grid=(D // tm,), in_specs=[pl.BlockSpec((N_DEV, tm, w), lambda i: (0, i, 0))],
        out_specs=[pl.BlockSpec((tm, wd), lambda i: (i, 0)) for wd in widths], out_shape=[_S((D, wd), g.dtype) for wd in widths],
        compiler_params=_cparams(("parallel",)),
    )(g)


def _unshard_pair(g1, g2, *, name):
    _, D, w = g1.shape
    tm = _pick(D, 256, align=16)

    def body(a_ref, b_ref, o_ref):
        for i, g_ref in enumerate((a_ref, b_ref)):
            for k in range(N_DEV):
                off = (i * N_DEV + k) * w
                o_ref[:, off:off + w] = g_ref[k]

    blk = pl.BlockSpec((N_DEV, tm, w), lambda i: (0, i, 0))
    return pl.pallas_call(
        body, name=name, grid=(D // tm,), in_specs=[blk, blk], out_specs=pl.BlockSpec((tm, 2 * N_DEV * w), lambda i: (i, 0)),
        out_shape=_S((D, 2 * N_DEV * w), g1.dtype), compiler_params=_cparams(("parallel",)),
    )(g1, g2)


def _reshard_pair(dw, *, name):
    D, w = dw.shape[0], dw.shape[1] // (2 * N_DEV)
    tm = _pick(D, 128, align=16)

    def body(g_ref, a_ref, b_ref):
        for i, o_ref in enumerate((a_ref, b_ref)):
            for k in range(N_DEV):
                off = (i * N_DEV + k) * w
                o_ref[_slot_of(k)] = g_ref[:, off:off + w].astype(o_ref.dtype)

    blk = pl.BlockSpec((N_DEV, tm, w), lambda i: (0, i, 0))
    return pl.pallas_call(
        body, name=name, grid=(D // tm,), in_specs=[pl.BlockSpec((tm, dw.shape[1]), lambda i: (i, 0))], out_specs=[blk, blk],
        out_shape=[_S((N_DEV, D, w), bf16)] * 2, compiler_params=_cparams(("parallel",)),
    )(dw)


def _reshard_w_in(grads, seg_lens, w, *, name):
    D = grads[0].shape[0]
    starts = [sum(seg_lens[:i]) for i in range(len(seg_lens))]
    tm = _pick(D, 128, align=16)

    def body(*refs):
        o_ref = refs[-1]
        for g_ref, s0, n in zip(refs[:-1], starts, seg_lens):
            for k, a, off, m in _pieces(s0, n, w):
                o_ref[_slot_of(k), :, a:a + m] = g_ref[:, off:off + m].astype(o_ref.dtype)

    return pl.pallas_call(
        body, name=name, grid=(D // tm,), in_specs=[pl.BlockSpec((tm, g.shape[1]), lambda i: (i, 0)) for g in grads],
        out_specs=pl.BlockSpec((N_DEV, tm, w), lambda i: (0, i, 0)), out_shape=_S((N_DEV, D, w), bf16),
        compiler_params=_cparams(("parallel",)),
    )(*grads)


def _pad_flat(a, mult):
    a = a.reshape(-1)
    n = -(-a.shape[0] // mult) * mult
    return a if n == a.shape[0] else jnp.pad(a, (0, n - a.shape[0]))


def _pad_cols(a, mult):
    n = -(-a.shape[1] // mult) * mult
    return a if n == a.shape[1] else jnp.pad(a, ((0, 0), (0, n - a.shape[1])))


def _block_diag(t):
    nblk, g, P, Q = t.shape
    eye = jnp.eye(g, dtype=t.dtype)
    return (t[:, :, :, None, :] * eye[None, :, None, :, None]).reshape(nblk, g * P, g * Q)


def _block_diag_t(w, P, Q):
    nblk = w.shape[0]
    g = w.shape[1] // P
    eye = jnp.eye(g, dtype=w.dtype)
    return (w.reshape(nblk, g, P, g, Q) * eye[None, :, None, :, None]).sum(axis=3)


_COLS = ("ffn1_w_gate", "ffn1_w_up", "ffn2_w_gate", "ffn2_w_up")
_ROWS = ("ffn1_w_down", "ffn2_w_down", "s5_w_glu", "w_proj_s5", "w_out", "w_proj_ssd")
_BIG = _COLS + _ROWS + ("w_in", "conv_w")
_SMALL = ("ffn1_norm", "mix_norm", "conv_b", "s5_A_re", "s5_A_im", "s5_log_dt", "s5_B_re", "s5_B_im", "s5_C_re", "s5_C_im",
          "s5_D", "s5_b_glu", "ssd_A_log", "ssd_dt_bias", "ssd_D", "ssd_norm", "b_gate", "ffn2_norm", "final_norm")
_WEIGHTS = ("ffn1_norm", "ffn1_w_gate", "ffn1_w_up", "ffn1_w_down", "mix_norm", "w_in", "conv_w", "conv_b", "s5_A_re", "s5_A_im",
            "s5_log_dt", "s5_B_re", "s5_B_im", "s5_C_re", "s5_C_im", "s5_D", "s5_w_glu", "s5_b_glu", "ssd_A_log", "ssd_dt_bias",
            "ssd_D", "ssd_norm", "w_proj_s5", "w_proj_ssd", "b_gate", "w_out", "ffn2_norm", "ffn2_w_gate", "ffn2_w_up",
            "ffn2_w_down", "final_norm")
def _with_carry(res, carry):
    return res if carry else (res, [])


def _ffn_fwd(x, n, wgu, wd_of, tag, carries=(None, None, None)):
    D = x.shape[1]
    h = _rows(_f_rmsnorm, [x], [n], [(D, bf16)], name=tag + "_norm")[0]
    ab, got0 = _with_carry(_mm(h, wgu, carry=carries[0], name=tag + "_gate_up"), carries[0])
    wd = wd_of(got0)
    F = wd.shape[0]
    (c,), got1 = _with_carry(_rows(_f_swiglu, [ab], [], [(F, bf16)], carry=carries[1], name=tag + "_act"), carries[1])
    y, got2 = _with_carry(_mm(c, wd, scale=0.5, add=x, carry=carries[2], name=tag + "_down"), carries[2])
    return y, (x, n, h, ab, c), (got0, got1, got2)


def _ffn_bwd(saved, wgu, wd, dy, tag, carry_after_dwd=None, carry_after_dwgu=None):
    x, n, h, ab, c = saved
    F = wd.shape[0]
    dc = _mm(dy, wd, tb=True, scale=0.5, name=tag + "_d_act")
    dwd = _mm(c, dy, ta=True, o_blk="m", o_slots=True, tm=F // 2, out_dtype=bf16, scale=0.5, name=tag + "_d_wdown")
    carry_w = carry_after_dwd(dwd) if carry_after_dwd else None
    (dab,), _ = _rows_bwd(_f_swiglu, [ab], [], [dc], name=tag + "_act_bwd", want_rows=[0], row_dtypes={0: bf16})
    dwgu, arr_w = _with_carry(_mm(h, dab, ta=True, carry=carry_w, name=tag + "_d_wgu"), carry_w)
    dwg, dwu = _reshard_pair(dwgu, name=tag + "_reshard_d_wgu")
    carry_h = carry_after_dwgu(dwg, dwu) if carry_after_dwgu else None
    dh, arr_h = _with_carry(_mm(dab, wgu, tb=True, carry=carry_h, name=tag + "_d_h"), carry_h)
    (dx,), (dn,) = _rows_bwd(_f_rmsnorm, [x], [n], [dh], name=tag + "_norm_bwd", want_rows=[0], adds={0: dy})
    return dx, dn, dwg, dwu, dwd, arr_w, arr_h


def kernel(x, ffn1_norm, ffn1_w_gate, ffn1_w_up, ffn1_w_down, mix_norm, w_in, conv_w, conv_b, s5_A_re, s5_A_im, s5_log_dt, s5_B_re, s5_B_im, s5_C_re, s5_C_im, s5_D, s5_w_glu, s5_b_glu, ssd_A_log, ssd_dt_bias, ssd_D, ssd_norm, w_proj_s5, w_proj_ssd, b_gate, w_out, ffn2_norm, ffn2_w_gate, ffn2_w_up, ffn2_w_down, final_norm, loss_target, m_ffn1_norm, m_ffn1_w_gate, m_ffn1_w_up, m_ffn1_w_down, m_mix_norm, m_w_in, m_conv_w, m_conv_b, m_s5_A_re, m_s5_A_im, m_s5_log_dt, m_s5_B_re, m_s5_B_im, m_s5_C_re, m_s5_C_im, m_s5_D, m_s5_w_glu, m_s5_b_glu, m_ssd_A_log, m_ssd_dt_bias, m_ssd_D, m_ssd_norm, m_w_proj_s5, m_w_proj_ssd, m_b_gate, m_w_out, m_ffn2_norm, m_ffn2_w_gate, m_ffn2_w_up, m_ffn2_w_down, m_final_norm, v_ffn1_norm, v_ffn1_w_gate, v_ffn1_w_up, v_ffn1_w_down, v_mix_norm, v_w_in, v_conv_w, v_conv_b, v_s5_A_re, v_s5_A_im, v_s5_log_dt, v_s5_B_re, v_s5_B_im, v_s5_C_re, v_s5_C_im, v_s5_D, v_s5_w_glu, v_s5_b_glu, v_ssd_A_log, v_ssd_dt_bias, v_ssd_D, v_ssd_norm, v_w_proj_s5, v_w_proj_ssd, v_b_gate, v_w_out, v_ffn2_norm, v_ffn2_w_gate, v_ffn2_w_up, v_ffn2_w_down, v_final_norm):
    P = dict(locals())
    T, D = x.shape[1], x.shape[2]
    x0, tgt = x[0], loss_target[0]
    sh = {k: P[k][0] for k in _BIG}

    send = {k: (sh[k] if k == "conv_w" else sh[k].astype(bf16)) for k in _BIG}
    W = {}

    def gather_in(keys):
        return _carry_gather([send[k] for k in keys])

    first_keys = ("ffn1_w_gate", "ffn1_w_up", "conv_w")
    W.update(zip(first_keys, _all_gather([send[k] for k in first_keys], name="gather_weights_first")))
    whole = lambda k: W[k].reshape(-1, D)
    conv_w_full = W["conv_w"].transpose(1, 0, 2).reshape(CONV_K, -1)

    d_inner = N_DEV * sh["w_proj_ssd"].shape[0]
    conv_dim = conv_w_full.shape[1]
    H = ssd_A_log.shape[1]
    G = (conv_dim - d_inner) // (2 * SSD_STATE)
    hpg = H // G
    nc = T // CHUNK
    Gs = D // S5_GROUP
    nblk = Gs // S5_GPB
    NS = Gs * S5_STATE
    seg_lens = (D, d_inner, conv_dim, H, 2 * D)

    cuts = [0, D // 3 // 16 * 16, D // 3 // 16 * 16 + 3 * D // 8 // 16 * 16, D]
    win_rows = [send["w_in"][a_:b_] for a_, b_ in zip(cuts[:-1], cuts[1:])]
    wgu1 = _unshard_pair(W["ffn1_w_gate"], W["ffn1_w_up"], name="unshard_ffn1_gate_up")

    def ffn1_down(got):
        W["ffn1_w_down"] = got[0]
        return whole("ffn1_w_down")

    x1, sv1, (got0, got1, got2) = _ffn_fwd(
        x0, ffn1_norm, wgu1, ffn1_down, "ffn1",
        carries=(_carry_gather([send["ffn1_w_down"], win_rows[0]]), _carry_gather([win_rows[1]]), _carry_gather([win_rows[2]])))
    ffn1_w = (wgu1, whole("ffn1_w_down"))
    W["w_in"] = jnp.concatenate([got0[1], got1[0], got2[0]], axis=1)
    w_u, w_z, w_xbc, w_dt, w_gl = _unshard_w_in(W["w_in"], seg_lens, name="unshard_w_in")
    h2 = _rows(_f_rmsnorm, [x1], [mix_norm], [(D, bf16)], name="mix_norm")[0]
    u_p = _mm(h2, w_u, o_seg=True, name="in_u")
    z = _mm(h2, w_z, name="in_z")
    xbc = _mm(h2, w_xbc, name="in_xbc")
    gl = _mm(h2, w_gl, name="in_gate")
    dtr = _mm(h2, w_dt, name="in_dt")

    rep = lambda a: jnp.repeat(a, S5_GROUP, axis=0)
    lr, li, ldt = s5_A_re[0], s5_A_im[0], s5_log_dt[0].reshape(Gs, 1)
    brt = s5_B_re[0].transpose(0, 2, 1).reshape(Gs * S5_GROUP, S5_STATE)
    bit = s5_B_im[0].transpose(0, 2, 1).reshape(Gs * S5_GROUP, S5_STATE)
    prep_args = (lr, li, ldt, rep(lr), rep(li), rep(ldt), brt, bit)
    ar, ai, bbrt, bbit = _s5_prep(prep_args, name="s5_prep")
    a_r, a_i = ar.reshape(1, NS), ai.reshape(1, NS)
    wb_r = _block_diag(bbrt.reshape(nblk, S5_GPB, S5_GROUP, S5_STATE)).astype(bf16)
    wb_i = _block_diag(bbit.reshape(nblk, S5_GPB, S5_GROUP, S5_STATE)).astype(bf16)
    c4r = s5_C_re[0].reshape(nblk, S5_GPB, S5_GROUP, S5_STATE).transpose(0, 1, 3, 2)
    c4i = s5_C_im[0].reshape(nblk, S5_GPB, S5_GROUP, S5_STATE).transpose(0, 1, 3, 2)
    wc_r, wc_i = _block_diag(c4r).astype(bf16), _block_diag(c4i).astype(bf16)
    mix_keys = ("s5_w_glu", "w_proj_s5", "w_proj_ssd", "w_out")
    sl_r, sl_i, p_r, p_i, got = _s5_local_scan(u_p, wb_r, wb_i, a_r, a_i, reverse=False, carry=gather_in(mix_keys), name="s5_scan")
    W.update(zip(mix_keys, got))
    w_glu, w_p5, w_pssd, w_o = whole("s5_w_glu"), whole("w_proj_s5"), whole("w_proj_ssd"), whole("w_out")
    c_r, c_i = _s5_carry(sl_r[T - NSEG:], sl_i[T - NSEG:], p_r, p_i, reverse=False, name="s5_carry")
    s_r, s_i, ylin = _s5_fix_out(sl_r, sl_i, a_r, a_i, c_r, c_i, wc_r, wc_i, name="s5_fix_out")
    g5 = _rows(_f_s5_post, [ylin, u_p], [s5_D], [(D, f32)], name="s5_gelu")[0]
    v5 = _mm(g5, w_glu, name="s5_glu_mm")
    o5 = _rows(_f_glu, [g5, v5], [s5_b_glu], [(D, bf16)], name="s5_glu")[0]
    p5 = _mm(o5, w_p5, a_seg=True, name="proj_s5")

    xc = _conv_fwd(xbc, conv_w_full, conv_b, name="conv")
    bias_p, alog_p = _pad_cols(ssd_dt_bias, LANES), _pad_cols(ssd_A_log, LANES)
    dt_p, da_p = _rows(_f_dt, [dtr], [bias_p, alog_p], [(LANES, f32), (LANES, f32)], name="ssd_dt")
    col_l = lambda a: jnp.repeat(a[:, :H], HEADDIM, axis=1)
    row_l = lambda a: a[:, :H].reshape(nc, CHUNK, G, hpg).transpose(2, 0, 3, 1).reshape(G, nc, hpg * CHUNK)
    ssd_in = (xc, col_l(dt_p), col_l(da_p), row_l(dt_p), row_l(da_p), jnp.repeat(ssd_D, HEADDIM, axis=1), _ssd_consts(hpg))
    ffn2_keys = ("ffn2_w_gate", "ffn2_w_up", "ffn2_w_down")
    y_ssd, hs, got = _ssd2_fwd(*ssd_in, d_inner=d_inner, carry=gather_in(ffn2_keys), name="ssd")
    W.update(zip(ffn2_keys, got))
    ffn2_w = (_unshard_pair(W["ffn2_w_gate"], W["ffn2_w_up"], name="unshard_ffn2_gate_up"), whole("ffn2_w_down"))
    yn = _rows(_f_gated_norm, [y_ssd, z], [ssd_norm], [(d_inner, bf16)], name="ssd_gated_norm")[0]
    pssd = _mm(yn, w_pssd, name="proj_ssd")

    merged = _rows(_f_merge, [gl, p5, pssd], [b_gate], [(D, bf16)], name="merge")[0]
    x2 = _mm(merged, w_o, add=x1, name="out_proj")
    x3, sv2, _ = _ffn_fwd(x2, ffn2_norm, ffn2_w[0], lambda _: ffn2_w[1], "ffn2")
    lossv, dx3, d_final = _loss_stage(x3, tgt, final_norm.reshape(1, D), name="loss")

    gw = {}
    gs = {"final_norm": d_final}
    slot_mm = lambda a_, b_, name, **kw: _mm(a_, b_, ta=True, o_blk="m", o_slots=True, out_dtype=bf16, name=name, **kw)
    core = lax.axis_index("c").astype(jnp.int32).reshape(1)
    chip = (2 * lax.axis_index("x") + lax.axis_index("y")).astype(jnp.int32).reshape(1)
    chip_sums, arrived = {}, {}

    def level1(keys, tag):
        sib = _exchange_sibling([gw[k] for k in keys], name="exchange_sibling_" + tag)
        for k, s_ in zip(keys, sib):
            chip_sums[k] = _chip_sum(gw[k], s_, core, name="chip_sum_" + k)
        return [chip_sums[k] for k in keys]

    dx2, gs["ffn2_norm"], gw["ffn2_w_gate"], gw["ffn2_w_up"], gw["ffn2_w_down"], _, _ = _ffn_bwd(sv2, *ffn2_w, dx3, "ffn2")

    dmerged = _mm(dx2, w_o, tb=True, name="d_merged")
    gw["w_out"] = slot_mm(merged, dx2, "d_w_out")
    (dgl, dp5, dpssd), (gs["b_gate"],) = _rows_bwd(_f_merge, [gl, p5, pssd], [b_gate], [dmerged], name="merge_bwd", want_rows=[0, 1, 2])

    dyn = _mm(dpssd, w_pssd, tb=True, name="d_yn")
    gw["w_proj_ssd"] = slot_mm(yn, dpssd, "d_w_proj_ssd")
    group_a = ("ffn2_w_gate", "ffn2_w_up", "ffn2_w_down", "w_out", "w_proj_ssd")
    parts_a = level1(group_a, "a")
    (dyssd, dz), (gs["ssd_norm"],) = _rows_bwd(_f_gated_norm, [y_ssd, z], [ssd_norm], [dyn], name="ssd_gated_norm_bwd", want_rows=[0, 1])
    dxs, dbm, dcm, ddtc, ddac, ddtw, ddaw, ddh = _ssd2_bwd(*ssd_in, hs, dyssd, d_inner=d_inner, name="ssd_bwd")

    def fold(col, row):
        col = col.reshape(T, G, LANES)[:, :, :hpg].reshape(T, H)
        row = row.reshape(G, nc, hpg, CHUNK).transpose(1, 3, 0, 2).reshape(T, H)
        return _pad_cols(col + row, LANES)

    (ddtr,), (dbias_p, dalog_p) = _rows_bwd(_f_dt, [dtr], [bias_p, alog_p], [fold(ddtc, ddtw), fold(ddac, ddaw)], name="ssd_dt_bwd", want_rows=[0])
    gs["ssd_dt_bias"], gs["ssd_A_log"], gs["ssd_D"] = dbias_p[:, :H], dalog_p[:, :H], ddh[:, 0, :hpg].reshape(1, H)
    dxbc, d_conv_w, gs["conv_b"], arr = _conv_bwd(
        xbc, conv_w_full, conv_b, jnp.concatenate([dxs, dbm, dcm], axis=1), carry=_carry_chips(parts_a), name="conv_bwd")
    arrived.update(zip(group_a, arr))
    cwk = sh["conv_w"].shape[1]
    gw["conv_w"] = d_conv_w.reshape(CONV_K, N_CHIP, 2, cwk).transpose(2, 1, 0, 3).reshape(N_DEV, CONV_K, cwk)

    do5 = _mm(dp5, w_p5, tb=True, o_seg=True, name="d_o5")
    gw["w_proj_s5"] = slot_mm(o5, dp5, "d_w_proj_s5", a_seg=True)
    (dg5a, dv5), (gs["s5_b_glu"],) = _rows_bwd(_f_glu, [g5, v5], [s5_b_glu], [do5], name="s5_glu_bwd", want_rows=[0, 1])
    dg5 = _mm(dv5, w_glu, tb=True, add=dg5a, name="d_g5")
    gw["s5_w_glu"] = slot_mm(g5, dv5, "d_w_glu")
    (dylin, du_a), (gs["s5_D"],) = _rows_bwd(_f_s5_post, [ylin, u_p], [s5_D], [dg5], name="s5_gelu_bwd", want_rows=[0, 1])
    wct_r, wct_i = wc_r.transpose(0, 2, 1), -wc_i.transpose(0, 2, 1)
    ql_r, ql_i, pb_r, pb_i = _s5_local_scan(dylin, wct_r, wct_i, a_r, -a_i, reverse=True, name="s5_scan_bwd")
    cb_r, cb_i = _s5_carry(ql_r[:NSEG], ql_i[:NSEG], pb_r, pb_i, reverse=True, name="s5_carry_bwd")
    tc = min(S5_TC, T)

    def before_blocks(s):
        last = s.reshape(T // tc, tc, NS)[:, tc - NSEG:, :]
        wrap = jnp.concatenate([jnp.zeros((1, 1, NS), f32), last[-1:, : NSEG - 1, :]], axis=1)
        return jnp.concatenate([wrap, last[:-1]], axis=0)

    du_p, dwb_r, dwb_i, dwc_r, dwc_i, d_ar, d_ai = _s5_fix_bwd(
        ql_r, ql_i, a_r, -a_i, cb_r, cb_i, s_r, s_i, before_blocks(s_r), before_blocks(s_i), u_p, dylin, du_a, wb_r, wb_i, name="s5_fix_bwd")
    unblk = lambda w: _block_diag_t(w, S5_GROUP, S5_STATE).reshape(Gs * S5_GROUP, S5_STATE)
    rsum = jnp.repeat(jnp.eye(Gs, dtype=f32), S5_GROUP, axis=1)
    d_lr, d_li, d_ldt, d_brt, d_bit = _s5_prep_bwd(
        prep_args, (d_ar.reshape(Gs, S5_STATE), d_ai.reshape(Gs, S5_STATE), unblk(dwb_r), unblk(dwb_i)), rsum, name="s5_prep_bwd")
    gs["s5_A_re"], gs["s5_A_im"], gs["s5_log_dt"] = d_lr, d_li, d_ldt.reshape(1, Gs)
    gs["s5_B_re"] = d_brt.reshape(Gs, S5_GROUP, S5_STATE).transpose(0, 2, 1)
    gs["s5_B_im"] = d_bit.reshape(Gs, S5_GROUP, S5_STATE).transpose(0, 2, 1)
    gs["s5_C_re"] = _block_diag_t(dwc_r, S5_STATE, S5_GROUP).transpose(0, 1, 3, 2).reshape(Gs, S5_GROUP, S5_STATE)
    gs["s5_C_im"] = _block_diag_t(dwc_i, S5_STATE, S5_GROUP).transpose(0, 1, 3, 2).reshape(Gs, S5_GROUP, S5_STATE)

    d_w_in = [_mm(h2, du_p, ta=True, b_seg=True, name="d_w_u"), _mm(h2, dz, ta=True, name="d_w_z"), _mm(h2, dxbc, ta=True, name="d_w_xbc"),
              _mm(h2, ddtr, ta=True, name="d_w_dt"), _mm(h2, dgl, ta=True, name="d_w_gate")]
    gw["w_in"] = _reshard_w_in(d_w_in, seg_lens, sh["w_in"].shape[1], name="reshard_d_w_in")
    group_c = ("w_proj_s5", "s5_w_glu", "conv_w")
    parts_c = level1(group_c + ("w_in",), "c")
    c1 = int(D * 0.45) // 16 * 16
    c2 = c1 + D // 4 // 16 * 16
    win = [parts_c[3][:, :c1], parts_c[3][:, c1:c2], parts_c[3][:, c2:]]
    dh2 = _mm(du_p, w_u, tb=True, a_seg=True, name="d_h2_u")
    dh2, arr = _mm(dz, w_z, tb=True, add=dh2, carry=_carry_chips(parts_c[:3]), name="d_h2_z")
    arrived.update(zip(group_c, arr))
    dh2, (arr0,) = _mm(dxbc, w_xbc, tb=True, add=dh2, carry=_carry_chips([win[0]]), name="d_h2_xbc")
    dh2, (arr1,) = _mm(dgl, w_gl, tb=True, add=dh2, carry=_carry_chips([win[1]]), name="d_h2_gate")
    dh2 = _mm(ddtr, w_dt, tb=True, add=dh2, name="d_h2_dt")
    (dx1,), (gs["mix_norm"],) = _rows_bwd(_f_rmsnorm, [x1], [mix_norm], [dh2], name="mix_norm_bwd", want_rows=[0], adds={0: dx2})

    def carry_ffn1_down(dwd):
        gw["ffn1_w_down"] = dwd
        return _carry_chips(level1(("ffn1_w_down",), "d") + [win[2]])

    def carry_ffn1_gate_up(dwg, dwu):
        gw["ffn1_w_gate"], gw["ffn1_w_up"] = dwg, dwu
        return _carry_chips(level1(("ffn1_w_gate", "ffn1_w_up"), "e"))

    dx0, gs["ffn1_norm"], _, _, _, arr_w, arr_h = _ffn_bwd(
        sv1, *ffn1_w, dx1, "ffn1", carry_after_dwd=carry_ffn1_down, carry_after_dwgu=carry_ffn1_gate_up)
    arrived["ffn1_w_down"], arr2 = arr_w
    arrived["ffn1_w_gate"], arrived["ffn1_w_up"] = arr_h
    arrived["w_in"] = jnp.concatenate([arr0, arr1, arr2], axis=1)

    small_shapes = {k: (P[k][0].shape if P[k].ndim > 1 else P[k].shape) for k in _SMALL}
    pack = lambda d: jnp.concatenate([_pad_flat(d[k], TILE_ELEMS) for k in _SMALL]).reshape(-1, LANES)
    gsmall = _sum_slots(_all_gather([pack(gs)], name="gather_small_grads")[0], name="sum_small_grads")
    snum = {k: math.prod(small_shapes[k]) for k in _SMALL}
    ssz = {k: -(-snum[k] // TILE_ELEMS) * TILE_ELEMS for k in _SMALL}

    grads, delta, new_m, new_v = {}, {}, {}, {}
    for k in _BIG:
        grads[k], delta[k], new_m[k], new_v[k] = _reduce_adamw(
            chip_sums[k], arrived[k], chip, P[k][0], P["m_" + k][0], P["v_" + k][0], name="adamw_" + k)
    d_s, m_s, v_s = _adamw(gsmall, pack({k: P[k] for k in _SMALL}), pack({k: P["m_" + k] for k in _SMALL}),
                           pack({k: P["v_" + k] for k in _SMALL}), name="adamw_small")
    off = 0
    gflat, dflat, mflat, vflat = gsmall.reshape(-1), d_s.reshape(-1), m_s.reshape(-1), v_s.reshape(-1)
    for k in _SMALL:
        n = snum[k]
        grads[k], delta[k], new_m[k], new_v[k] = (a[off:off + n] for a in (gflat, dflat, mflat, vflat))
        off += ssz[k]

    loss = lax.psum(lossv[0, 0], ("x", "y", "c"))
    out = [loss, dx0.reshape(x.shape)]
    for d in (grads, delta, new_m, new_v):
        out += [d[k].reshape(P[k].shape) for k in _WEIGHTS]
    return tuple(out)
```

```python
import math
from typing import Callable, NamedTuple

import jax
import jax.numpy as jnp
from jax import lax
from jax.experimental import pallas as pl
from jax.experimental.pallas import tpu as pltpu

f32 = jnp.float32
bf16 = jnp.bfloat16
_S = jax.ShapeDtypeStruct

EPS = 1e-6
S5_GROUP = 16
S5_STATE = 64
HEADDIM = 64
SSD_STATE = 128
CHUNK = 64
CONV_K = 4
NSEG = 8
S5_GPB = 16
N_DEV = 8
LANES = 128
TILE_ELEMS = 8 * LANES

ADAM_LR = 0.001
ADAM_B1 = 0.9
ADAM_B2 = 0.999
ADAM_EPS = 1e-08
ADAM_WD = 0.01
ADAM_STEP = 10

VMEM_LIMIT = 56 * 1024 * 1024
MM_FULL_K = 3072
MM_MAX_TN = 3072
MESH = pl.DeviceIdType.MESH


def _cparams(sem=None):
    return pltpu.CompilerParams(dimension_semantics=sem, vmem_limit_bytes=VMEM_LIMIT)


def _pick(dim, pref, align=LANES):
    best = None
    t = align
    while t <= min(dim, pref):
        if dim % t == 0:
            best = t
        t += align
    return best or dim


def _slot_of(k):
    return (k & 1) * (N_DEV // 2) + (k >> 1)


def _mm(a, b, *, name, ta=False, tb=False, a_blk=None, b_blk=None, o_blk=None, o_slots=False, a_seg=False, b_seg=False,
        o_seg=False, tm=None, out_dtype=f32, scale=1.0, add=None, carry=None):
    a2, b2 = a.shape[-2:], b.shape[-2:]
    Ma, Ka = (a2[1], a2[0]) if ta else a2
    Kb, Nb = (b2[1], b2[0]) if tb else b2
    M = Ma * (a.shape[0] if a_blk == "m" else 1)
    K = Ka * (a.shape[0] if a_blk == "k" else 1)
    N = Nb * (b.shape[0] if b_blk == "n" else 1)
    assert K == Kb * (b.shape[0] if b_blk == "k" else 1), (a.shape, b.shape, ta, tb, a_blk, b_blk)
    assert (a.ndim == 3) == (a_blk is not None) and (b.ndim == 3) == (b_blk is not None)
    tm = Ma if a_blk == "m" else (tm or _pick(M, 512))
    tn = Nb if b_blk == "n" else _pick(N, MM_MAX_TN)
    if a_blk == "k" or b_blk == "k":
        tk = Ka if a_blk == "k" else Kb
        assert tk == (Kb if b_blk == "k" else tk)
    else:
        tk = K if K <= MM_FULL_K else _pick(K, 1024 if ta else MM_FULL_K)
    if (a_seg and not ta) or o_seg:
        tm = M // NSEG
    if (a_seg and ta) or b_seg:
        tk = K // NSEG
    gm, gn, nk = M // tm, N // tn, K // tk
    assert not (add is not None and (o_seg or o_blk)) and not (o_blk and o_seg)

    if a_seg:
        assert a.ndim == 2
        a = a.reshape(a.shape[0] // NSEG, NSEG * a.shape[1])
        if ta:
            a_spec = pl.BlockSpec((tk, tm), lambda i, j, k: (0, k * (Ma // tm) + i))
        else:
            a_spec = pl.BlockSpec((tm, tk), lambda i, j, k: (0, i * (Ka // tk) + k))
    elif a.ndim == 3:
        lead = (lambda i, k: i) if a_blk == "m" else (lambda i, k: k)
        if ta:
            a_spec = pl.BlockSpec((None, tk, tm), lambda i, j, k: (lead(i, k), 0 if a_blk == "k" else k, 0 if a_blk == "m" else i))
        else:
            a_spec = pl.BlockSpec((None, tm, tk), lambda i, j, k: (lead(i, k), 0 if a_blk == "m" else i, 0 if a_blk == "k" else k))
    else:
        a_spec = pl.BlockSpec((tk, tm), lambda i, j, k: (k, i)) if ta else pl.BlockSpec((tm, tk), lambda i, j, k: (i, k))
    if b_seg:
        assert b.ndim == 2 and not tb
        b = b.reshape(b.shape[0] // NSEG, NSEG * b.shape[1])
        b_spec = pl.BlockSpec((tk, tn), lambda i, j, k: (0, k * (Nb // tn) + j))
    elif b.ndim == 3:
        lead = (lambda j, k: j) if b_blk == "n" else (lambda j, k: k)
        if tb:
            b_spec = pl.BlockSpec((None, tn, tk), lambda i, j, k: (lead(j, k), 0 if b_blk == "n" else j, 0 if b_blk == "k" else k))
        else:
            b_spec = pl.BlockSpec((None, tk, tn), lambda i, j, k: (lead(j, k), 0 if b_blk == "k" else k, 0 if b_blk == "n" else j))
    else:
        b_spec = pl.BlockSpec((tn, tk), lambda i, j, k: (j, k)) if tb else pl.BlockSpec((tk, tn), lambda i, j, k: (k, j))
    slot = _slot_of if o_slots else (lambda k: k)
    if o_blk == "n":
        assert gn == N_DEV or not o_slots
        o_shape, o_spec = (gn, M, tn), pl.BlockSpec((None, tm, tn), lambda i, j, k: (slot(j), i, 0))
    elif o_blk == "m" and o_slots and gm < N_DEV:
        rs = M // N_DEV
        per_tile = tm // rs
        assert per_tile % 2 == 0 and tm % rs == 0
        o_shape = (2, N_CHIP, rs, N)
        o_spec = pl.BlockSpec((2, per_tile // 2, rs, tn), lambda i, j, k: (0, i, 0, j))
    elif o_blk == "m":
        assert gm == N_DEV or not o_slots
        o_shape, o_spec = (gm, tm, N), pl.BlockSpec((None, tm, tn), lambda i, j, k: (slot(i), 0, j))
    elif o_seg:
        o_shape, o_spec = (tm, NSEG * N), pl.BlockSpec((tm, tn), lambda i, j, k: (0, i * (N // tn) + j))
    else:
        o_shape, o_spec = (M, N), pl.BlockSpec((tm, tn), lambda i, j, k: (i, j))
    dims = (((0 if ta else 1,), (1 if tb else 0,)), ((), ()))
    has_add = add is not None

    carry = carry or _NO_CARRY
    n_in = 2 + has_add

    def body(*refs):
        own, c_in, c_out, c_sems = _carry_split(carry, refs, n_in, 1)
        a_ref, b_ref = own[0], own[1]
        add_ref = own[2] if has_add else None
        o_ref, acc_ref = own[-2], own[-1]
        i, j, k = pl.program_id(0), pl.program_id(1), pl.program_id(2)
        _carry_start(carry, c_in, c_out, c_sems, (i == 0) & (j == 0) & (k == 0))

        @pl.when(k == 0)
        def _():
            acc_ref[...] = jnp.zeros_like(acc_ref)

        acc_ref[...] += lax.dot_general(a_ref[...].astype(bf16), b_ref[...].astype(bf16), dims, preferred_element_type=f32)

        @pl.when(k == nk - 1)
        def _():
            r = acc_ref[...] * scale
            if has_add:
                r = r + add_ref[...].astype(f32)
            if len(o_shape) == 4:
                rs = o_shape[2]
                for chip_l in range(o_ref.shape[1]):
                    for core in range(2):
                        dev = 2 * chip_l + core
                        o_ref[core, chip_l] = r[dev * rs:(dev + 1) * rs].astype(out_dtype)
            else:
                o_ref[...] = r.astype(out_dtype)

        _carry_finish(carry, c_in, c_out, c_sems, (i == gm - 1) & (j == gn - 1) & (k == nk - 1))

    ins = [a, b] + ([add] if has_add else []) + list(carry.ins)
    in_specs = [a_spec, b_spec] + ([o_spec] if has_add else []) + [_ANY] * len(carry.ins)
    res = pl.pallas_call(
        body, name=name, grid=(gm, gn, nk), in_specs=in_specs, out_specs=[o_spec] + [_ANY] * len(carry.out_shapes),
        out_shape=[_S(o_shape, out_dtype)] + list(carry.out_shapes),
        scratch_shapes=[pltpu.VMEM((tm, tn), f32)] + list(carry.sems),
        compiler_params=_cparams(("arbitrary",) * 3 if carry.ins else ("parallel", "parallel", "arbitrary")),
    )(*ins)
    out = res[0]
    if len(o_shape) == 4:
        out = out.reshape(N_DEV, o_shape[2], N)
    elif o_seg:
        out = out.reshape(M, N)
    return (out, list(res[1:])) if carry.ins else out


def _row_tile(T, widths):
    budget = 6 * 1024 * 1024
    tb = max(16, budget // (4 * sum(widths)))
    return _pick(T, tb, align=16)


def _rows(fn, rows, params, outs, *, name, carry=None):
    T = rows[0].shape[0]
    nr, npar = len(rows), len(params)
    tb = _row_tile(T, [r.shape[1] for r in rows] + [w for w, _ in outs])
    carry = carry or _NO_CARRY

    def body(*refs):
        own, c_in, c_out, c_sems = _carry_split(carry, refs, nr + npar, len(outs))
        _carry_start(carry, c_in, c_out, c_sems, pl.program_id(0) == 0)
        ins = [r[...].astype(f32) for r in own[: nr + npar]]
        res = fn(*ins)
        for o_ref, r in zip(own[nr + npar:], res):
            o_ref[...] = r.astype(o_ref.dtype)
        _carry_finish(carry, c_in, c_out, c_sems, pl.program_id(0) == T // tb - 1)

    in_specs = [pl.BlockSpec((tb, r.shape[1]), lambda i: (i, 0)) for r in rows]
    in_specs += [pl.BlockSpec(p.shape, lambda i: (0, 0)) for p in params]
    out_specs = [pl.BlockSpec((tb, w), lambda i: (i, 0)) for w, _ in outs]
    res = pl.pallas_call(
        body, name=name, grid=(T // tb,), in_specs=in_specs + [_ANY] * len(carry.ins),
        out_specs=out_specs + [_ANY] * len(carry.out_shapes), out_shape=[_S((T, w), d) for w, d in outs] + list(carry.out_shapes),
        scratch_shapes=list(carry.sems), compiler_params=_cparams(("arbitrary",) if carry.ins else ("parallel",)),
    )(*rows, *params, *carry.ins)
    return (tuple(res[:len(outs)]), list(res[len(outs):])) if carry.ins else tuple(res)


def _rows_bwd(fn, rows, params, cots, *, name, want_rows, row_dtypes=None, adds=None):
    T = rows[0].shape[0]
    nr, npar, nc = len(rows), len(params), len(cots)
    adds = adds or {}
    add_idx = sorted(adds)
    row_dtypes = row_dtypes or {}
    widths = [r.shape[1] for r in rows] + [c.shape[1] for c in cots] + [rows[i].shape[1] for i in want_rows]
    tb = _row_tile(T, widths)

    def body(*refs):
        ins = [r[...].astype(f32) for r in refs[: nr + npar]]
        cot = tuple(r[...].astype(f32) for r in refs[nr + npar: nr + npar + nc])
        add_refs = refs[nr + npar + nc: nr + npar + nc + len(add_idx)]
        out_refs = refs[nr + npar + nc + len(add_idx):]
        _, vjp = jax.vjp(lambda *a: tuple(fn(*a)), *ins)
        g = vjp(cot)
        for o_ref, i in zip(out_refs[: len(want_rows)], want_rows):
            r = g[i]
            if i in adds:
                r = r + add_refs[add_idx.index(i)][...].astype(f32)
            o_ref[...] = r.astype(o_ref.dtype)
        first = pl.program_id(0) == 0
        for o_ref, gp in zip(out_refs[len(want_rows):], g[nr:]):
            @pl.when(first)
            def _(o_ref=o_ref):
                o_ref[...] = jnp.zeros_like(o_ref)

            o_ref[...] += gp

    in_specs = [pl.BlockSpec((tb, r.shape[1]), lambda i: (i, 0)) for r in rows]
    in_specs += [pl.BlockSpec(p.shape, lambda i: (0, 0)) for p in params]
    in_specs += [pl.BlockSpec((tb, c.shape[1]), lambda i: (i, 0)) for c in cots]
    in_specs += [pl.BlockSpec((tb, adds[i].shape[1]), lambda i_: (i_, 0)) for i in add_idx]
    out_specs = [pl.BlockSpec((tb, rows[i].shape[1]), lambda i_: (i_, 0)) for i in want_rows]
    out_specs += [pl.BlockSpec(p.shape, lambda i: (0, 0)) for p in params]
    out_shape = [_S(rows[i].shape, row_dtypes.get(i, f32)) for i in want_rows] + [_S(p.shape, f32) for p in params]
    res = pl.pallas_call(
        body, name=name, grid=(T // tb,), in_specs=in_specs, out_specs=out_specs, out_shape=out_shape,
        compiler_params=_cparams(("arbitrary",)),
    )(*rows, *params, *cots, *[adds[i] for i in add_idx])
    return list(res[: len(want_rows)]), list(res[len(want_rows):])


def _f_rmsnorm(x, g):
    return (x * lax.rsqrt(jnp.mean(x * x, axis=-1, keepdims=True) + EPS) * g,)


def _f_swiglu(ab):
    F = ab.shape[1] // 2
    return (jax.nn.silu(ab[:, :F]) * ab[:, F:],)


def _f_s5_post(y, u, d):
    return (jax.nn.gelu(y + d * u),)


def _f_glu(g, v, b):
    return (g * jax.nn.sigmoid(v + b),)


def _f_gated_norm(y, z, w):
    return _f_rmsnorm(y * jax.nn.silu(z), w)


def _f_merge(gl, p5, pssd, b):
    D = p5.shape[1]
    gates = jax.nn.sigmoid(gl + b)
    return (gates[:, :D] * p5 + gates[:, D:] * pssd,)


def _f_dt(dtr, bias, a_log):
    dt = jax.nn.softplus(dtr + bias)
    return dt, dt * (-jnp.exp(a_log))


def _f_dt_expand(dtr, bias, a_log, e):
    dt, a = _f_dt(dtr, bias, a_log)
    return dt, a, _doth(dt, e), _doth(a, e)


def _loss_stage(x, tgt, g, *, name):
    T, D = x.shape
    tb = _row_tile(T, [D, D, D])

    def f(xb, gb, tb_):
        y = _f_rmsnorm(xb, gb)[0]
        return 0.5 * jnp.sum(jnp.mean(jnp.square(y - tb_), axis=-1, keepdims=True), axis=0, keepdims=True)

    def body(x_ref, t_ref, g_ref, l_ref, dx_ref, dg_ref):
        tv = t_ref[...]
        val, vjp = jax.vjp(lambda a, b: f(a, b, tv), x_ref[...], g_ref[...])
        dx, dg = vjp(jnp.ones((1, 1), f32))
        dx_ref[...] = dx

        @pl.when(pl.program_id(0) == 0)
        def _():
            l_ref[...] = jnp.zeros_like(l_ref)
            dg_ref[...] = jnp.zeros_like(dg_ref)

        l_ref[...] += jnp.broadcast_to(val, l_ref.shape)
        dg_ref[...] += dg

    row = pl.BlockSpec((tb, D), lambda i: (i, 0))
    par = pl.BlockSpec((1, D), lambda i: (0, 0))
    return pl.pallas_call(
        body, name=name, grid=(T // tb,), in_specs=[row, row, par],
        out_specs=[pl.BlockSpec((1, LANES), lambda i: (0, 0)), row, par],
        out_shape=[_S((1, LANES), f32), _S((T, D), f32), _S((1, D), f32)], compiler_params=_cparams(("arbitrary",)),
    )(x, tgt, g)


def _shift_down(x, s):
    if s == 0:
        return x
    t = lax.broadcasted_iota(jnp.int32, x.shape, 0)
    return jnp.where(t >= s, pltpu.roll(x, s, axis=0), 0.0)


def _shift_up(x, s):
    if s == 0:
        return x
    T = x.shape[0]
    t = lax.broadcasted_iota(jnp.int32, x.shape, 0)
    return jnp.where(t < T - s, pltpu.roll(x, T - s, axis=0), 0.0)


def _conv_pre(x, w, b):
    pre = b
    for k in range(CONV_K):
        pre = pre + w[k:k + 1, :] * _shift_down(x, CONV_K - 1 - k)
    return pre


def _conv_fwd(x, w, b, *, name):
    T, C = x.shape
    cb = _pick(C, 256)

    def body(x_ref, w_ref, b_ref, o_ref):
        o_ref[...] = jax.nn.silu(_conv_pre(x_ref[...], w_ref[...], b_ref[...]))

    col = pl.BlockSpec((T, cb), lambda j: (0, j))
    return pl.pallas_call(
        body, name=name, grid=(C // cb,), in_specs=[col, pl.BlockSpec((CONV_K, cb), lambda j: (0, j)), pl.BlockSpec((1, cb), lambda j: (0, j))],
        out_specs=col, out_shape=_S((T, C), f32), compiler_params=_cparams(("parallel",)),
    )(x, w, b)


def _conv_bwd(x, w, b, dy, *, name, carry=None):
    T, C = x.shape
    cb = _pick(C, 128)
    carry = carry or _NO_CARRY
    ends = []
    for d in dy:
        ends.append((ends[-1] if ends else 0) + d.shape[1] // cb)
    assert ends[-1] == C // cb and all(d.shape[1] % cb == 0 for d in dy)
    npc = len(dy)

    def body(*refs):
        own, c_in, c_out, c_sems = _carry_split(carry, refs, 3 + npc, 3)
        x_ref, w_ref, b_ref = own[:3]
        dy_refs, (dx_ref, dw_ref, db_ref) = own[3:3 + npc], own[3 + npc:]
        _carry_start(carry, c_in, c_out, c_sems, pl.program_id(0) == 0)
        j = pl.program_id(0)
        dyv = dy_refs[-1][...]
        for p in range(npc - 2, -1, -1):
            dyv = jnp.where(j < ends[p], dy_refs[p][...], dyv)
        xv, wv = x_ref[...], w_ref[...]
        pre = _conv_pre(xv, wv, b_ref[...])
        sg = jax.nn.sigmoid(pre)
        dpre = dyv * sg * (1.0 + pre * (1.0 - sg))
        dx = jnp.zeros_like(xv)
        for k in range(CONV_K):
            s = CONV_K - 1 - k
            dx = dx + wv[k:k + 1, :] * _shift_up(dpre, s)
            dw_ref[k:k + 1, :] = jnp.sum(dpre * _shift_down(xv, s), axis=0, keepdims=True)
        dx_ref[...] = dx
        db_ref[...] = jnp.sum(dpre, axis=0, keepdims=True)
        _carry_finish(carry, c_in, c_out, c_sems, pl.program_id(0) == C // cb - 1)

    col = pl.BlockSpec((T, cb), lambda j: (0, j))
    wsp = pl.BlockSpec((CONV_K, cb), lambda j: (0, j))
    bsp = pl.BlockSpec((1, cb), lambda j: (0, j))
    starts = [0] + ends[:-1]
    dy_specs = [pl.BlockSpec((T, cb), lambda j, s=s, e=e: (0, jnp.clip(j, s, e - 1) - s)) for s, e in zip(starts, ends)]
    res = pl.pallas_call(
        body, name=name, grid=(C // cb,), in_specs=[col, wsp, bsp] + dy_specs + [_ANY] * len(carry.ins),
        out_specs=[col, wsp, bsp] + [_ANY] * len(carry.out_shapes),
        out_shape=[_S((T, C), f32), _S((CONV_K, C), f32), _S((1, C), f32)] + list(carry.out_shapes),
        scratch_shapes=list(carry.sems), compiler_params=_cparams(("arbitrary",) if carry.ins else ("parallel",)),
    )(x, w, b, *dy, *carry.ins)
    return (*res[:3], list(res[3:]))


def _f_s5_prep(lr, li, ldt, lrb, lib, ldtb, brt, bit):
    def disc(lr_, li_, ldt_):
        dt = jnp.exp(ldt_)
        mag = jnp.exp(lr_ * dt)
        ar, ai = mag * jnp.cos(li_ * dt), mag * jnp.sin(li_ * dt)
        den = lr_ * lr_ + li_ * li_
        cr = ((ar - 1.0) * lr_ + ai * li_) / den
        ci = (ai * lr_ - (ar - 1.0) * li_) / den
        return ar, ai, cr, ci

    ar, ai, _, _ = disc(lr, li, ldt)
    _, _, cr, ci = disc(lrb, lib, ldtb)
    return ar, ai, cr * brt - ci * bit, cr * bit + ci * brt


def _s5_prep(args, *, name):
    G, N = args[0].shape
    GM = args[3].shape[0]

    def body(*refs):
        res = _f_s5_prep(*[r[...] for r in refs[:8]])
        for o, r in zip(refs[8:], res):
            o[...] = r

    return pl.pallas_call(body, name=name, out_shape=[_S((G, N), f32)] * 2 + [_S((GM, N), f32)] * 2)(*args)


def _s5_prep_bwd(args, cots, rsum, *, name):
    G, N = args[0].shape
    GM = args[3].shape[0]

    def body(*refs):
        ins = [r[...] for r in refs[:8]]
        cot = tuple(r[...] for r in refs[8:12])
        rs = refs[12][...]
        _, vjp = jax.vjp(_f_s5_prep, *ins)
        g = vjp(cot)
        fold = lambda v: jnp.dot(rs, v, preferred_element_type=f32, precision=lax.Precision.HIGHEST)
        o = refs[13:]
        o[0][...] = g[0] + fold(g[3])
        o[1][...] = g[1] + fold(g[4])
        o[2][...] = g[2] + fold(jnp.broadcast_to(g[5], (GM, LANES)))[:, 0:1]
        o[3][...] = g[6]
        o[4][...] = g[7]

    return pl.pallas_call(
        body, name=name, out_shape=[_S((G, N), f32), _S((G, N), f32), _S((G, 1), f32), _S((GM, N), f32), _S((GM, N), f32)],
    )(*args, *cots, rsum)


S5_TC = 512


def _s5_local_scan(src, w_r, w_i, a_r, a_i, *, reverse, name, carry=None):
    T, C = src.shape
    nblk, cb, sb = w_r.shape
    NS = nblk * sb
    tc = min(S5_TC, T)
    nT, nt = T // tc, tc // NSEG
    tmap = (lambda i: nT - 1 - i) if reverse else (lambda i: i)

    carry = carry or _NO_CARRY

    def body(*refs):
        own, c_in, c_out, c_sems = _carry_split(carry, refs, 5, 4)
        u_ref, wr_ref, wi_ref, ar_ref, ai_ref, sr_ref, si_ref, pr_ref, pi_ref, st_r, st_i, pw_r, pw_i = own
        _carry_start(carry, c_in, c_out, c_sems, (pl.program_id(0) == 0) & (pl.program_id(1) == 0))

        @pl.when(pl.program_id(1) == 0)
        def _():
            st_r[...] = jnp.zeros_like(st_r)
            st_i[...] = jnp.zeros_like(st_i)
            pw_r[...] = jnp.ones_like(pw_r)
            pw_i[...] = jnp.zeros_like(pw_i)

        u = u_ref[...].astype(bf16)
        sr_ref[...] = jnp.dot(u, wr_ref[...], preferred_element_type=f32)
        si_ref[...] = jnp.dot(u, wi_ref[...], preferred_element_type=f32)
        ar = jnp.broadcast_to(ar_ref[...], (NSEG, sb))
        ai = jnp.broadcast_to(ai_ref[...], (NSEG, sb))

        def step(k, c):
            cr, ci, qr, qi = c
            kk = (nt - 1 - k) if reverse else k
            rows = pl.ds(pl.multiple_of(kk * NSEG, NSEG), NSEG)
            nr = ar * cr - ai * ci + sr_ref[rows, :]
            ni = ar * ci + ai * cr + si_ref[rows, :]
            sr_ref[rows, :] = nr
            si_ref[rows, :] = ni
            return nr, ni, ar * qr - ai * qi, ar * qi + ai * qr

        cr, ci, qr, qi = lax.fori_loop(0, nt, step, (st_r[...], st_i[...], pw_r[...], pw_i[...]), unroll=8)
        st_r[...], st_i[...], pw_r[...], pw_i[...] = cr, ci, qr, qi
        pr_ref[...] = qr
        pi_ref[...] = qi
        _carry_finish(carry, c_in, c_out, c_sems, (pl.program_id(0) == nblk - 1) & (pl.program_id(1) == nT - 1))

    blk = pl.BlockSpec((tc, sb), lambda j, i: (tmap(i), j))
    wsp = pl.BlockSpec((None, cb, sb), lambda j, i: (j, 0, 0))
    asp = pl.BlockSpec((1, sb), lambda j, i: (0, j))
    psp = pl.BlockSpec((NSEG, sb), lambda j, i: (0, j))
    res = pl.pallas_call(
        body, name=name, grid=(nblk, nT),
        in_specs=[pl.BlockSpec((tc, cb), lambda j, i: (tmap(i), j)), wsp, wsp, asp, asp] + [_ANY] * len(carry.ins),
        out_specs=[blk, blk, psp, psp] + [_ANY] * len(carry.out_shapes),
        out_shape=[_S((T, NS), f32)] * 2 + [_S((NSEG, NS), f32)] * 2 + list(carry.out_shapes),
        scratch_shapes=[pltpu.VMEM((NSEG, sb), f32)] * 4 + list(carry.sems),
        compiler_params=_cparams(("arbitrary", "arbitrary") if carry.ins else ("parallel", "arbitrary")),
    )(src, w_r, w_i, a_r, a_i, *carry.ins)
    return (*res[:4], list(res[4:])) if carry.ins else res


def _s5_carry(e_r, e_i, p_r, p_i, *, reverse, name):
    NS = e_r.shape[1]

    def body(er_ref, ei_ref, pr_ref, pi_ref, cr_ref, ci_ref):
        ar, ai = pr_ref[0:1, :], pi_ref[0:1, :]
        cr = jnp.zeros((1, NS), f32)
        ci = jnp.zeros((1, NS), f32)
        order = list(range(NSEG - 1, -1, -1)) if reverse else list(range(NSEG))
        cr_ref[order[0]:order[0] + 1, :] = cr
        ci_ref[order[0]:order[0] + 1, :] = ci
        for prev, q in zip(order[:-1], order[1:]):
            er, ei = er_ref[prev:prev + 1, :], ei_ref[prev:prev + 1, :]
            cr, ci = er + ar * cr - ai * ci, ei + ar * ci + ai * cr
            cr_ref[q:q + 1, :] = cr
            ci_ref[q:q + 1, :] = ci

    return pl.pallas_call(body, name=name, out_shape=[_S((NSEG, NS), f32)] * 2)(e_r, e_i, p_r, p_i)


def _s5_fix_out(sl_r, sl_i, a_r, a_i, c_r, c_i, wc_r, wc_i, *, name):
    T, NS = sl_r.shape
    nblk, sb, cb = wc_r.shape
    tc = min(S5_TC, T)
    nT, nt = T // tc, tc // NSEG

    def body(lr_ref, li_ref, ar_ref, ai_ref, cr_ref, ci_ref, wr_ref, wi_ref, sr_ref, si_ref, y_ref, pw_r, pw_i):
        @pl.when(pl.program_id(1) == 0)
        def _():
            pw_r[...] = jnp.ones_like(pw_r)
            pw_i[...] = jnp.zeros_like(pw_i)

        ar = jnp.broadcast_to(ar_ref[...], (NSEG, sb))
        ai = jnp.broadcast_to(ai_ref[...], (NSEG, sb))
        cr, ci = cr_ref[...], ci_ref[...]

        def step(k, c):
            qr, qi = c
            qr, qi = ar * qr - ai * qi, ar * qi + ai * qr
            rows = pl.ds(pl.multiple_of(k * NSEG, NSEG), NSEG)
            sr_ref[rows, :] = lr_ref[rows, :] + qr * cr - qi * ci
            si_ref[rows, :] = li_ref[rows, :] + qr * ci + qi * cr
            return qr, qi

        qr, qi = lax.fori_loop(0, nt, step, (pw_r[...], pw_i[...]), unroll=8)
        pw_r[...], pw_i[...] = qr, qi
        y_ref[...] = (jnp.dot(sr_ref[...].astype(bf16), wr_ref[...], preferred_element_type=f32)
                      - jnp.dot(si_ref[...].astype(bf16), wi_ref[...], preferred_element_type=f32))

    blk = pl.BlockSpec((tc, sb), lambda j, i: (i, j))
    asp = pl.BlockSpec((1, sb), lambda j, i: (0, j))
    csp = pl.BlockSpec((NSEG, sb), lambda j, i: (0, j))
    wsp = pl.BlockSpec((None, sb, cb), lambda j, i: (j, 0, 0))
    return pl.pallas_call(
        body, name=name, grid=(nblk, nT), in_specs=[blk, blk, asp, asp, csp, csp, wsp, wsp],
        out_specs=[blk, blk, pl.BlockSpec((tc, cb), lambda j, i: (i, j))],
        out_shape=[_S((T, NS), f32)] * 2 + [_S((T, nblk * cb), f32)],
        scratch_shapes=[pltpu.VMEM((NSEG, sb), f32)] * 2, compiler_params=_cparams(("parallel", "arbitrary")),
    )(sl_r, sl_i, a_r, a_i, c_r, c_i, wc_r, wc_i)


def _s5_fix_bwd(ql_r, ql_i, ab_r, ab_i, c_r, c_i, s_r, s_i, sb_r, sb_i, u, dy, du_add, w_r, w_i, *, name):
    T, NS = ql_r.shape
    nblk, cb, sb = w_r.shape
    tc = min(S5_TC, T)
    nT, nt = T // tc, tc // NSEG
    tmap = lambda i: nT - 1 - i

    def body(lr_ref, li_ref, ar_ref, ai_ref, cr_ref, ci_ref, sr_ref, si_ref, br_ref, bi_ref, u_ref, dy_ref, dua_ref, wr_ref, wi_ref,
             du_ref, dwr_ref, dwi_ref, dcr_ref, dci_ref, dar_ref, dai_ref, pw_r, pw_i, ac_r, ac_i, q_r, q_i):
        first = pl.program_id(1) == 0

        @pl.when(first)
        def _():
            pw_r[...] = jnp.ones_like(pw_r)
            pw_i[...] = jnp.zeros_like(pw_i)
            ac_r[...] = jnp.zeros_like(ac_r)
            ac_i[...] = jnp.zeros_like(ac_i)
            dwr_ref[...] = jnp.zeros_like(dwr_ref)
            dwi_ref[...] = jnp.zeros_like(dwi_ref)
            dcr_ref[...] = jnp.zeros_like(dcr_ref)
            dci_ref[...] = jnp.zeros_like(dci_ref)

        ar = jnp.broadcast_to(ar_ref[...], (NSEG, sb))
        ai = jnp.broadcast_to(ai_ref[...], (NSEG, sb))
        cr, ci = cr_ref[...], ci_ref[...]

        def fix(rows, qr, qi, spr, spi, accr, acci):
            qr, qi = ar * qr - ai * qi, ar * qi + ai * qr
            xr = lr_ref[rows, :] + qr * cr - qi * ci
            xi = li_ref[rows, :] + qr * ci + qi * cr
            q_r[rows, :] = xr
            q_i[rows, :] = xi
            return qr, qi, accr + xr * spr + xi * spi, acci + xi * spr - xr * spi

        def step(k, c):
            qr, qi, accr, acci = c
            kk = nt - 1 - k
            rows = pl.ds(pl.multiple_of(kk * NSEG, NSEG), NSEG)
            prev = pl.ds(pl.multiple_of((kk - 1) * NSEG, NSEG), NSEG)
            return fix(rows, qr, qi, sr_ref[prev, :], si_ref[prev, :], accr, acci)

        c = lax.fori_loop(0, nt - 1, step, (pw_r[...], pw_i[...], ac_r[...], ac_i[...]), unroll=7)
        qr, qi, accr, acci = fix(pl.ds(0, NSEG), *c[:2], br_ref[...], bi_ref[...], *c[2:])
        pw_r[...], pw_i[...], ac_r[...], ac_i[...] = qr, qi, accr, acci

        qrb, qib = q_r[...].astype(bf16), q_i[...].astype(bf16)
        nt_dims = (((1,), (1,)), ((), ()))
        tn_dims = (((0,), (0,)), ((), ()))
        du_ref[...] = (dua_ref[...] + lax.dot_general(qrb, wr_ref[...], nt_dims, preferred_element_type=f32)
                       + lax.dot_general(qib, wi_ref[...], nt_dims, preferred_element_type=f32))
        ub = u_ref[...].astype(bf16)
        dwr_ref[...] += lax.dot_general(ub, qrb, tn_dims, preferred_element_type=f32)
        dwi_ref[...] += lax.dot_general(ub, qib, tn_dims, preferred_element_type=f32)
        dyb = dy_ref[...].astype(bf16)
        dcr_ref[...] += lax.dot_general(sr_ref[...].astype(bf16), dyb, tn_dims, preferred_element_type=f32)
        dci_ref[...] -= lax.dot_general(si_ref[...].astype(bf16), dyb, tn_dims, preferred_element_type=f32)

        @pl.when(pl.program_id(1) == nT - 1)
        def _():
            dar_ref[...] = jnp.sum(accr, axis=0, keepdims=True)
            dai_ref[...] = jnp.sum(acci, axis=0, keepdims=True)

    blk = pl.BlockSpec((tc, sb), lambda j, i: (tmap(i), j))
    asp = pl.BlockSpec((1, sb), lambda j, i: (0, j))
    csp = pl.BlockSpec((NSEG, sb), lambda j, i: (0, j))
    bsp = pl.BlockSpec((None, NSEG, sb), lambda j, i: (tmap(i), 0, j))
    chn = pl.BlockSpec((tc, cb), lambda j, i: (tmap(i), j))
    wsp = pl.BlockSpec((None, cb, sb), lambda j, i: (j, 0, 0))
    wcs = pl.BlockSpec((None, sb, cb), lambda j, i: (j, 0, 0))
    return pl.pallas_call(
        body, name=name, grid=(nblk, nT), in_specs=[blk, blk, asp, asp, csp, csp, blk, blk, bsp, bsp, chn, chn, chn, wsp, wsp],
        out_specs=[chn, wsp, wsp, wcs, wcs, asp, asp],
        out_shape=[_S((T, nblk * cb), f32), _S((nblk, cb, sb), f32), _S((nblk, cb, sb), f32), _S((nblk, sb, cb), f32),
                   _S((nblk, sb, cb), f32), _S((1, NS), f32), _S((1, NS), f32)],
        scratch_shapes=[pltpu.VMEM((NSEG, sb), f32)] * 4 + [pltpu.VMEM((tc, sb), f32)] * 2,
        compiler_params=_cparams(("parallel", "arbitrary")),
    )(ql_r, ql_i, ab_r, ab_i, c_r, c_i, s_r, s_i, sb_r, sb_i, u, dy, du_add, w_r, w_i)


SSD2_TB = 512
_NN = (((1,), (0,)), ((), ()))
_NT = (((1,), (1,)), ((), ()))
_TN = (((0,), (0,)), ((), ()))


def _dotf(a, b, dims):
    return lax.dot_general(a.astype(bf16), b.astype(bf16), dims, preferred_element_type=f32)


def _doth(a, b, dims=_NN, sel="b", parts=3):
    x, m = (a, b) if sel == "b" else (b, a)
    m = m.astype(bf16)
    out = None
    for _ in range(parts):
        piece = x.astype(bf16)
        x = x - piece.astype(f32)
        d = lax.dot_general(*((piece, m) if sel == "b" else (m, piece)), dims, preferred_element_type=f32)
        out = d if out is None else out + d
    return out


def _ssd_consts(hpg):
    W = hpg * CHUNK
    i = lax.broadcasted_iota(jnp.int32, (CHUNK, CHUNK), 0)
    j = lax.broadcasted_iota(jnp.int32, (CHUNK, CHUNK), 1)
    tril = (i >= j).astype(f32)
    r = lax.broadcasted_iota(jnp.int32, (W, W), 0)
    c = lax.broadcasted_iota(jnp.int32, (W, W), 1)
    bd = (r // CHUNK == c // CHUNK).astype(f32)
    triu_bd = bd * (r <= c).astype(f32)
    e_r = lax.broadcasted_iota(jnp.int32, (W, LANES), 0)
    e_c = lax.broadcasted_iota(jnp.int32, (W, LANES), 1)
    ered = (e_r // HEADDIM == e_c).astype(f32)
    return tril, jnp.tile(tril, (1, hpg)), bd, triu_bd, ered


def _ssd2_specs(G, hpg, tb, tmap, b_off, c_off):
    W = hpg * HEADDIM
    ncb = tb // CHUNK
    xsp = pl.BlockSpec((tb, W), lambda g, i: (tmap(i), g))
    bsp = pl.BlockSpec((tb, SSD_STATE), lambda g, i: (tmap(i), b_off + g))
    csp = pl.BlockSpec((tb, SSD_STATE), lambda g, i: (tmap(i), c_off + g))
    rsp = pl.BlockSpec((None, ncb, W), lambda g, i: (g, tmap(i), 0))
    dsp = pl.BlockSpec((1, W), lambda g, i: (0, g))
    hsp = pl.BlockSpec((None, ncb, SSD_STATE, W), lambda g, i: (g, tmap(i), 0, 0))
    const = lambda a: pl.BlockSpec(a.shape, lambda g, i: (0, 0))
    return xsp, bsp, csp, rsp, dsp, hsp, const


def _tile_rows(a, n):
    return jnp.concatenate([a] * n, axis=0)


def _ssd2_fwd(xc, dt4, a4, dtw, aw, d4, consts, *, d_inner, name, carry=None):
    carry = carry or _NO_CARRY
    T = xc.shape[0]
    G, nc, W = dtw.shape
    hpg = W // CHUNK
    tb = min(SSD2_TB, T)
    nb, ncb = T // tb, tb // CHUNK
    b_off = d_inner // SSD_STATE
    xsp, bsp, csp, rsp, dsp, hsp, const = _ssd2_specs(G, hpg, tb, lambda i: i, b_off, b_off + G)
    tril, mask4, bd, triu_bd, _ = consts

    def body(*refs):
        own, c_in, c_out, c_sems = _carry_split(carry, refs, 12, 2)
        x_ref, b_ref, c_ref, dt_ref, a_ref, dtw_ref, aw_ref, d_ref, tril_ref, mask_ref, bd_ref, tbd_ref, y_ref, hs_ref, h_scr = own
        _carry_start(carry, c_in, c_out, c_sems, (pl.program_id(0) == 0) & (pl.program_id(1) == 0))

        @pl.when(pl.program_id(1) == 0)
        def _():
            h_scr[...] = jnp.zeros_like(h_scr)

        acs_rows = _doth(aw_ref[...], tbd_ref[...])
        ht = h_scr[...]
        for c in range(ncb):
            rows = slice(c * CHUNK, (c + 1) * CHUNK)
            x, bm, cm = x_ref[rows, :], b_ref[rows, :], c_ref[rows, :]
            acs = _doth(tril_ref[...], a_ref[rows, :], sel="a")
            lmat = jnp.where(mask_ref[...] > 0, jnp.exp(jnp.minimum(acs - acs_rows[c:c + 1, :], 0.0)), 0.0)
            m4 = _dotf(cm, _tile_rows(bm, hpg), _NT) * lmat * dtw_ref[c:c + 1, :]
            xbd = _tile_rows(x, hpg) * bd_ref[...]
            hs_ref[c] = ht
            y_ref[rows, :] = _dotf(m4, xbd, _NN) + _dotf(cm, ht, _NN) * jnp.exp(acs) + d_ref[...] * x
            a_last = acs[CHUNK - 1:CHUNK, :]
            xw = x * (jnp.exp(a_last - acs) * dt_ref[rows, :])
            ht = ht * jnp.exp(a_last) + _dotf(bm, xw, _TN)
        h_scr[...] = ht
        _carry_finish(carry, c_in, c_out, c_sems, (pl.program_id(0) == G - 1) & (pl.program_id(1) == nb - 1))

    res = pl.pallas_call(
        body, name=name, grid=(G, nb),
        in_specs=[xsp, bsp, csp, xsp, xsp, rsp, rsp, dsp, const(tril), const(mask4), const(bd), const(triu_bd)] + [_ANY] * len(carry.ins),
        out_specs=[xsp, hsp] + [_ANY] * len(carry.out_shapes),
        out_shape=[_S((T, G * W), f32), _S((G, nc, SSD_STATE, W), f32)] + list(carry.out_shapes),
        scratch_shapes=[pltpu.VMEM((SSD_STATE, W), f32)] + list(carry.sems),
        compiler_params=_cparams(("arbitrary", "arbitrary") if carry.ins else ("parallel", "arbitrary")),
    )(xc, xc, xc, dt4, a4, dtw, aw, d4, tril, mask4, bd, triu_bd, *carry.ins)
    return res[0], res[1], list(res[2:])


def _ssd2_bwd(xc, dt4, a4, dtw, aw, d4, consts, hs, dy, *, d_inner, name):
    T = xc.shape[0]
    G, nc, W = dtw.shape
    hpg = W // CHUNK
    tb = min(SSD2_TB, T)
    nb, ncb = T // tb, tb // CHUNK
    b_off = d_inner // SSD_STATE
    tmap = lambda i: nb - 1 - i
    xsp, bsp, csp, rsp, dsp, hsp, const = _ssd2_specs(G, hpg, tb, tmap, b_off, b_off + G)
    gsp = pl.BlockSpec((tb, SSD_STATE), lambda g, i: (tmap(i), g))
    ddsp = pl.BlockSpec((None, 1, LANES), lambda g, i: (g, 0, 0))
    tril, mask4, bd, triu_bd, ered = consts

    def body(x_ref, b_ref, c_ref, dt_ref, a_ref, dtw_ref, aw_ref, d_ref, tril_ref, mask_ref, bd_ref, tbd_ref, er_ref, hs_ref, dy_ref,
             dx_ref, db_ref, dc_ref, ddtc_ref, dac_ref, ddtw_ref, daw_ref, dd_ref, g_scr, dd_scr, rw_scr, tl_scr):
        first = pl.program_id(1) == 0

        @pl.when(first)
        def _():
            g_scr[...] = jnp.zeros_like(g_scr)
            dd_scr[...] = jnp.zeros_like(dd_scr)

        mask = mask_ref[...] > 0
        lane_in_block = lax.broadcasted_iota(jnp.int32, mask.shape, 1) & (CHUNK - 1)
        maskt = lax.broadcasted_iota(jnp.int32, mask.shape, 0) <= lane_in_block
        acs_rows = _doth(aw_ref[...], tbd_ref[...])
        dht = g_scr[...]
        dd = dd_scr[...]
        for c in range(ncb - 1, -1, -1):
            rows = slice(c * CHUNK, (c + 1) * CHUNK)
            x, bm, cm, dyc = x_ref[rows, :], b_ref[rows, :], c_ref[rows, :], dy_ref[rows, :]
            dtc, dtr = dt_ref[rows, :], dtw_ref[c:c + 1, :]
            ht = hs_ref[c]
            acs = _doth(tril_ref[...], a_ref[rows, :], sel="a")
            seg = acs - acs_rows[c:c + 1, :]
            lmat = jnp.where(mask, jnp.exp(jnp.minimum(seg, 0.0)), 0.0)
            lmat_t = jnp.where(maskt, jnp.exp(jnp.minimum(-seg, 0.0)), 0.0)
            btile, ctile = _tile_rows(bm, hpg), _tile_rows(cm, hpg)
            g4 = _dotf(cm, btile, _NT)
            gt4 = _dotf(bm, ctile, _NT)
            m4 = g4 * lmat * dtr
            mt4 = gt4 * lmat_t * dtc
            xbd = _tile_rows(x, hpg) * bd_ref[...]
            dybd = _tile_rows(dyc, hpg) * bd_ref[...]
            dm4 = _dotf(dyc, xbd, _NT)
            dmt4 = _dotf(x, dybd, _NT)
            dx = d_ref[...] * dyc + _dotf(mt4, dybd, _NN)
            dd = dd + jnp.sum(dyc * x, axis=0, keepdims=True)
            e4 = dm4 * m4
            dc = _dotf(dm4 * lmat * dtr, btile, _NN)
            db = _dotf(dmt4 * lmat_t * dtc, ctile, _NN)
            decay = jnp.exp(acs)
            yoff = _dotf(cm, ht, _NN) * decay
            dz = dyc * decay
            dc = dc + _dotf(dz, ht, _NT)
            dht_prev = _dotf(cm, dz, _TN)
            a_last = acs[CHUNK - 1:CHUNK, :]
            ea_last = jnp.exp(a_last)
            erel = jnp.exp(a_last - acs)
            dte = erel * dtc
            dxw = _dotf(bm, dht, _NN)
            db = db + _dotf(x * dte, dht, _NT)
            dx = dx + dxw * dte
            q4 = dxw * x
            dacs = e4 + dyc * yoff - q4 * dte
            col = jnp.concatenate([q4 * erel, _doth(tril_ref[...], dacs, _TN, sel="a")], axis=0)
            col = _doth(col, er_ref[...], parts=2)
            ddtc_ref[rows, :] = col[:CHUNK]
            dac_ref[rows, :] = col[CHUNK:]
            ddtw_ref[c:c + 1, :] = jnp.sum(dm4 * g4 * lmat, axis=0, keepdims=True)
            rw_scr[c:c + 1, :] = -jnp.sum(e4, axis=0, keepdims=True)
            tl_scr[c:c + 1, :] = jnp.sum(q4 * dte, axis=0, keepdims=True) + ea_last * jnp.sum(dht * ht, axis=0, keepdims=True)
            dx_ref[rows, :] = dx
            db_ref[rows, :] = db
            dc_ref[rows, :] = dc
            dht = dht_prev + dht * ea_last
        daw_ref[...] = _doth(rw_scr[...], tbd_ref[...], _NT) + _doth(tl_scr[...], bd_ref[...])
        g_scr[...] = dht
        dd_scr[...] = dd

        @pl.when(pl.program_id(1) == nb - 1)
        def _():
            dd_ref[...] = _doth(dd, er_ref[...])

    return pl.pallas_call(
        body, name=name, grid=(G, nb),
        in_specs=[xsp, bsp, csp, xsp, xsp, rsp, rsp, dsp, const(tril), const(mask4), const(bd), const(triu_bd), const(ered), hsp, xsp],
        out_specs=[xsp, gsp, gsp, gsp, gsp, rsp, rsp, ddsp],
        out_shape=[_S((T, G * W), f32), _S((T, G * SSD_STATE), f32), _S((T, G * SSD_STATE), f32), _S((T, G * LANES), f32),
                   _S((T, G * LANES), f32), _S(dtw.shape, f32), _S(dtw.shape, f32), _S((G, 1, LANES), f32)],
        scratch_shapes=[pltpu.VMEM((SSD_STATE, W), f32), pltpu.VMEM((1, W), f32), pltpu.VMEM((ncb, W), f32), pltpu.VMEM((ncb, W), f32)],
        compiler_params=_cparams(("parallel", "arbitrary")),
    )(xc, xc, xc, dt4, a4, dtw, aw, d4, tril, mask4, bd, triu_bd, ered, hs, dy)


def _peers():
    x, y, c = lax.axis_index("x"), lax.axis_index("y"), lax.axis_index("c")
    return x, y, c


_ANY = pl.BlockSpec(memory_space=pl.ANY)
N_CHIP = N_DEV // 2


def _all_gather(shards, *, name):
    n = len(shards)
    carry = _carry_gather(shards)

    def body(*refs):
        x_refs, out_refs, sems = refs[:n], refs[n:2 * n], refs[2 * n:]
        _gather_start(x_refs, out_refs, sems)
        _gather_finish(x_refs, out_refs, sems)

    return pl.pallas_call(
        body, name=name, out_shape=list(carry.out_shapes), in_specs=[_ANY] * n, out_specs=[_ANY] * n, scratch_shapes=list(carry.sems),
    )(*shards)


def _gather_parts(x_refs, out_refs, sems):
    send_sems, recv_sems, local_sems = sems
    x, y, c = _peers()
    me, sibling = (x, y, c), (x, y, 1 - c)
    chips = [(1 - x, y), (x, 1 - y), (1 - x, 1 - y)]
    n = len(x_refs)

    def copy(a, r, block, to, src=None):
        px, py, pc = block
        slot = out_refs[a].at[4 * px + 2 * py + pc]
        return pltpu.make_async_remote_copy(
            src_ref=slot if src is None else src, dst_ref=slot, send_sem=send_sems.at[7 * a + r],
            recv_sem=recv_sems.at[7 * a + r], device_id=to, device_id_type=MESH)

    mine = [pltpu.make_async_copy(x_refs[a], out_refs[a].at[4 * x + 2 * y + c], local_sems.at[a]) for a in range(n)]
    first = []
    for a in range(n):
        first.append(copy(a, 0, me, sibling, src=x_refs[a]))
        first += [copy(a, 1 + j, me, (*chip, c), src=x_refs[a]) for j, chip in enumerate(chips)]
    return copy, mine, first, me, sibling, chips, c, n


def _gather_start(x_refs, out_refs, sems):
    _, mine, first, *_ = _gather_parts(x_refs, out_refs, sems)
    for cp in mine + first:
        cp.start()


def _gather_finish(x_refs, out_refs, sems):
    copy, mine, first, me, sibling, chips, c, n = _gather_parts(x_refs, out_refs, sems)
    passed = []
    for j, chip in enumerate(chips):
        for a in range(n):
            copy(a, 1 + j, (*chip, c), me).wait_recv()
            fwd = copy(a, 4 + j, (*chip, c), sibling)
            fwd.start()
            passed.append(fwd)
    for a in range(n):
        copy(a, 0, sibling, me).wait_recv()
    for j, chip in enumerate(chips):
        for a in range(n):
            copy(a, 4 + j, (*chip, 1 - c), me).wait_recv()
    for cp in first + passed:
        cp.wait_send()
    for cp in mine:
        cp.wait()


def _exchange_sibling(slots, *, name):
    n = len(slots)

    def body(*refs):
        x_refs, sib_refs = refs[:n], refs[n:2 * n]
        send_sems, recv_sems = refs[2 * n:]
        x, y, c = _peers()
        give = [pltpu.make_async_remote_copy(
            src_ref=x_refs[a].at[pl.ds(N_CHIP * (1 - c), N_CHIP)], dst_ref=sib_refs[a], send_sem=send_sems.at[a],
            recv_sem=recv_sems.at[a], device_id=(x, y, 1 - c), device_id_type=MESH) for a in range(n)]
        for cp in give:
            cp.start()
        for cp in give:
            cp.wait_recv()
        for cp in give:
            cp.wait_send()

    return list(pl.pallas_call(
        body, name=name, out_shape=[_S((N_CHIP,) + s.shape[1:], s.dtype) for s in slots], in_specs=[_ANY] * n, out_specs=[_ANY] * n,
        scratch_shapes=[pltpu.SemaphoreType.DMA((n,)), pltpu.SemaphoreType.DMA((n,))],
    )(*slots))


def _chip_sum(slots, sib, core, *, name):
    _, R, W = slots.shape
    tr = _pick(R, max(16, (1 << 20) // (4 * W)), align=16)

    def body(core_ref, x_ref, s_ref, o_ref):
        o_ref[...] = (x_ref[...].astype(f32) + s_ref[...].astype(f32)).astype(o_ref.dtype)

    blk = pl.BlockSpec((None, tr, W), lambda t, i, core_ref: (t, i, 0))
    return pl.pallas_call(
        body, name=name, out_shape=_S(sib.shape, slots.dtype),
        grid_spec=pltpu.PrefetchScalarGridSpec(
            num_scalar_prefetch=1, grid=(N_CHIP, R // tr),
            in_specs=[pl.BlockSpec((None, tr, W), lambda t, i, core_ref: (N_CHIP * core_ref[0] + t, i, 0)), blk], out_specs=blk),
        compiler_params=_cparams(("parallel", "parallel")),
    )(core, slots, sib)


def _exchange_chips(parts, *, name):
    n = len(parts)

    def body(*refs):
        copies = _chip_copies(refs[:n], refs[n:2 * n], *refs[2 * n:])
        _start_all(copies)
        _wait_all(copies)

    return list(pl.pallas_call(
        body, name=name, out_shape=_chip_out_shapes(parts), in_specs=[_ANY] * n, out_specs=[_ANY] * n,
        scratch_shapes=_chip_sems(n),
    )(*parts))


def _chip_out_shapes(parts):
    return [_S((N_CHIP - 1,) + p.shape[1:], p.dtype) for p in parts]


def _chip_sems(n):
    return [pltpu.SemaphoreType.DMA((3 * n,)), pltpu.SemaphoreType.DMA((3 * n,))]


def _chip_copies(p_refs, out_refs, send_sems, recv_sems):
    x, y, c = _peers()
    copies = []
    for j in range(1, N_CHIP):
        tx, ty = x ^ (j >> 1), y ^ (j & 1)
        for a in range(len(p_refs)):
            copies.append(pltpu.make_async_remote_copy(
                src_ref=p_refs[a].at[2 * tx + ty], dst_ref=out_refs[a].at[j - 1], send_sem=send_sems.at[3 * a + j - 1],
                recv_sem=recv_sems.at[3 * a + j - 1], device_id=(tx, ty, c), device_id_type=MESH))
    return copies


def _start_all(copies):
    for cp in copies:
        cp.start()


def _wait_all(copies):
    for cp in copies:
        cp.wait_recv()
    for cp in copies:
        cp.wait_send()


class _Carry(NamedTuple):
    ins: tuple = ()
    out_shapes: tuple = ()
    sems: tuple = ()
    start: Callable = None
    finish: Callable = None


_NO_CARRY = _Carry()


def _carry_chips(parts):
    return _Carry(tuple(parts), tuple(_chip_out_shapes(parts)), tuple(_chip_sems(len(parts))),
                  lambda i, o, s: _start_all(_chip_copies(i, o, *s)), lambda i, o, s: _wait_all(_chip_copies(i, o, *s)))


def _carry_gather(shards):
    n = len(shards)
    sems = (pltpu.SemaphoreType.DMA((7 * n,)), pltpu.SemaphoreType.DMA((7 * n,)), pltpu.SemaphoreType.DMA((n,)))
    return _Carry(tuple(shards), tuple(_S((N_DEV,) + s.shape, s.dtype) for s in shards), sems, _gather_start, _gather_finish)


def _carry_split(carry, refs, n_in, n_out):
    ci, co, cs = len(carry.ins), len(carry.out_shapes), len(carry.sems)
    refs = list(refs)
    own_in, c_in = refs[:n_in], refs[n_in:n_in + ci]
    own_out, c_out = refs[n_in + ci:n_in + ci + n_out], refs[n_in + ci + n_out:n_in + ci + n_out + co]
    rest = refs[n_in + ci + n_out + co:]
    own_scratch, c_sems = rest[:len(rest) - cs], rest[len(rest) - cs:]
    return own_in + own_out + own_scratch, c_in, c_out, c_sems


def _carry_start(carry, c_in, c_out, c_sems, first):
    if carry.ins:
        @pl.when(first)
        def _():
            carry.start(c_in, c_out, c_sems)


def _carry_finish(carry, c_in, c_out, c_sems, last):
    if carry.ins:
        @pl.when(last)
        def _():
            carry.finish(c_in, c_out, c_sems)


def _sum_slots(stack, *, name):
    n, R, W = stack.shape
    tr = _pick(R, 1024, align=8)

    def body(s_ref, o_ref):
        acc = s_ref[0]
        for k in range(1, n):
            acc = acc + s_ref[k]
        o_ref[...] = acc

    return pl.pallas_call(
        body, name=name, grid=(R // tr,), in_specs=[pl.BlockSpec((n, tr, W), lambda i: (0, i, 0))],
        out_specs=pl.BlockSpec((tr, W), lambda i: (i, 0)), out_shape=_S((R, W), f32), compiler_params=_cparams(("parallel",)),
    )(stack)


def _adamw_math(gv, wv, mv, vv):
    c1 = 1.0 / (1.0 - ADAM_B1 ** ADAM_STEP)
    c2 = 1.0 / (1.0 - ADAM_B2 ** ADAM_STEP)
    nm = ADAM_B1 * mv + (1.0 - ADAM_B1) * gv
    nv = ADAM_B2 * vv + (1.0 - ADAM_B2) * jnp.square(gv)
    return -ADAM_LR * ((nm * c1) / (jnp.sqrt(nv * c2) + ADAM_EPS) + ADAM_WD * wv), nm, nv


def _adamw(g, w, m, v, *, name):
    R, W = w.shape
    tr = _pick(R, max(8, (1 << 20) // (4 * W)), align=8)

    def body(g_ref, w_ref, m_ref, v_ref, d_ref, nm_ref, nv_ref):
        d_ref[...], nm_ref[...], nv_ref[...] = _adamw_math(g_ref[...], w_ref[...], m_ref[...], v_ref[...])

    sp = pl.BlockSpec((tr, W), lambda i: (i, 0))
    return pl.pallas_call(
        body, name=name, grid=(R // tr,), in_specs=[sp] * 4, out_specs=[sp] * 3, out_shape=[_S((R, W), f32)] * 3,
        compiler_params=_cparams(("parallel",)),
    )(g, w, m, v)


def _reduce_adamw(own, arrived, chip, w, m, v, *, name):
    n, R, W = arrived.shape
    tr = _pick(R, max(16, (1 << 20) // (4 * W)), align=16)

    def body(chip_ref, o_ref, p_ref, w_ref, m_ref, v_ref, g_ref, d_ref, nm_ref, nv_ref):
        gv = o_ref[...].astype(f32)
        for k in range(n):
            gv = gv + p_ref[k].astype(f32)
        g_ref[...] = gv
        d_ref[...], nm_ref[...], nv_ref[...] = _adamw_math(gv, w_ref[...], m_ref[...], v_ref[...])

    sp = pl.BlockSpec((tr, W), lambda i, chip_ref: (i, 0))
    return pl.pallas_call(
        body, name=name, out_shape=[_S((R, W), f32)] * 4,
        grid_spec=pltpu.PrefetchScalarGridSpec(
            num_scalar_prefetch=1, grid=(R // tr,),
            in_specs=[pl.BlockSpec((None, tr, W), lambda i, chip_ref: (chip_ref[0], i, 0)),
                      pl.BlockSpec((n, tr, W), lambda i, chip_ref: (0, i, 0))] + [sp] * 3, out_specs=[sp] * 4),
        compiler_params=_cparams(("parallel",)),
    )(chip, own, arrived, w, m, v)


def _pieces(seg_start, seg_len, shard_w):
    out, col = [], seg_start
    while col < seg_start + seg_len:
        k, a = divmod(col, shard_w)
        n = min(shard_w - a, seg_start + seg_len - col)
        out.append((k, a, col - seg_start, n))
        col += n
    return out


def _unshard_w_in(g, seg_lens, *, name):
    _, D, w = g.shape
    starts = [sum(seg_lens[:i]) for i in range(len(seg_lens))]
    widths = [max(n, LANES) for n in seg_lens]
    tm = _pick(D, 256, align=16)

    def body(g_ref, *o_refs):
        for o_ref, s0, n in zip(o_refs, starts, seg_lens):
            if n < o_ref.shape[1]:
                o_ref[...] = jnp.zeros_like(o_ref)
            for k, a, off, m in _pieces(s0, n, w):
                o_ref[:, off:off + m] = g_ref[k, :, a:a + m]

    return pl.pallas_call(
        body, name=name, grid=(D // tm,), in_specs=[pl.BlockSpec((N_DEV, tm, w), lambda i: (0, i, 0))],
        out_specs=[pl.BlockSpec((tm, wd), lambda i: (i, 0)) for wd in widths], out_shape=[_S((D, wd), g.dtype) for wd in widths],
        compiler_params=_cparams(("parallel",)),
    )(g)


def _unshard_pair(g1, g2, *, name):
    _, D, w = g1.shape
    tm = _pick(D, 256, align=16)

    def body(a_ref, b_ref, o_ref):
        for i, g_ref in enumerate((a_ref, b_ref)):
            for k in range(N_DEV):
                off = (i * N_DEV + k) * w
                o_ref[:, off:off + w] = g_ref[k]

    blk = pl.BlockSpec((N_DEV, tm, w), lambda i: (0, i, 0))
    return pl.pallas_call(
        body, name=name, grid=(D // tm,), in_specs=[blk, blk], out_specs=pl.BlockSpec((tm, 2 * N_DEV * w), lambda i: (i, 0)),
        out_shape=_S((D, 2 * N_DEV * w), g1.dtype), compiler_params=_cparams(("parallel",)),
    )(g1, g2)


def _reshard_pair(dw, *, name):
    D, w = dw.shape[0], dw.shape[1] // (2 * N_DEV)
    tm = _pick(D, 128, align=16)

    def body(g_ref, a_ref, b_ref):
        for i, o_ref in enumerate((a_ref, b_ref)):
            for k in range(N_DEV):
                off = (i * N_DEV + k) * w
                o_ref[_slot_of(k)] = g_ref[:, off:off + w].astype(o_ref.dtype)

    blk = pl.BlockSpec((N_DEV, tm, w), lambda i: (0, i, 0))
    return pl.pallas_call(
        body, name=name, grid=(D // tm,), in_specs=[pl.BlockSpec((tm, dw.shape[1]), lambda i: (i, 0))], out_specs=[blk, blk],
        out_shape=[_S((N_DEV, D, w), bf16)] * 2, compiler_params=_cparams(("parallel",)),
    )(dw)


def _reshard_w_in(grads, seg_lens, w, *, name):
    D = grads[0].shape[0]
    starts = [sum(seg_lens[:i]) for i in range(len(seg_lens))]
    tm = _pick(D, 128, align=16)

    def body(*refs):
        o_ref = refs[-1]
        for g_ref, s0, n in zip(refs[:-1], starts, seg_lens):
            for k, a, off, m in _pieces(s0, n, w):
                o_ref[_slot_of(k), :, a:a + m] = g_ref[:, off:off + m].astype(o_ref.dtype)

    return pl.pallas_call(
        body, name=name, grid=(D // tm,), in_specs=[pl.BlockSpec((tm, g.shape[1]), lambda i: (i, 0)) for g in grads],
        out_specs=pl.BlockSpec((N_DEV, tm, w), lambda i: (0, i, 0)), out_shape=_S((N_DEV, D, w), bf16),
        compiler_params=_cparams(("parallel",)),
    )(*grads)


def _pad_flat(a, mult):
    a = a.reshape(-1)
    n = -(-a.shape[0] // mult) * mult
    return a if n == a.shape[0] else jnp.pad(a, (0, n - a.shape[0]))


def _pad_cols(a, mult):
    n = -(-a.shape[1] // mult) * mult
    return a if n == a.shape[1] else jnp.pad(a, ((0, 0), (0, n - a.shape[1])))


def _block_diag(t):
    nblk, g, P, Q = t.shape
    eye = jnp.eye(g, dtype=t.dtype)
    return (t[:, :, :, None, :] * eye[None, :, None, :, None]).reshape(nblk, g * P, g * Q)


def _block_diag_t(w, P, Q):
    nblk = w.shape[0]
    g = w.shape[1] // P
    eye = jnp.eye(g, dtype=w.dtype)
    return (w.reshape(nblk, g, P, g, Q) * eye[None, :, None, :, None]).sum(axis=3)


_COLS = ("ffn1_w_gate", "ffn1_w_up", "ffn2_w_gate", "ffn2_w_up")
_ROWS = ("ffn1_w_down", "ffn2_w_down", "s5_w_glu", "w_proj_s5", "w_out", "w_proj_ssd")
_BIG = _COLS + _ROWS + ("w_in", "conv_w")
_SMALL = ("ffn1_norm", "mix_norm", "conv_b", "s5_A_re", "s5_A_im", "s5_log_dt", "s5_B_re", "s5_B_im", "s5_C_re", "s5_C_im",
          "s5_D", "s5_b_glu", "ssd_A_log", "ssd_dt_bias", "ssd_D", "ssd_norm", "b_gate", "ffn2_norm", "final_norm")
_WEIGHTS = ("ffn1_norm", "ffn1_w_gate", "ffn1_w_up", "ffn1_w_down", "mix_norm", "w_in", "conv_w", "conv_b", "s5_A_re", "s5_A_im",
            "s5_log_dt", "s5_B_re", "s5_B_im", "s5_C_re", "s5_C_im", "s5_D", "s5_w_glu", "s5_b_glu", "ssd_A_log", "ssd_dt_bias",
            "ssd_D", "ssd_norm", "w_proj_s5", "w_proj_ssd", "b_gate", "w_out", "ffn2_norm", "ffn2_w_gate", "ffn2_w_up",
            "ffn2_w_down", "final_norm")
def _with_carry(res, carry):
    return res if carry else (res, [])


def _ffn_fwd(x, n, wgu, wd_of, tag, carries=(None, None, None)):
    D = x.shape[1]
    h = _rows(_f_rmsnorm, [x], [n], [(D, bf16)], name=tag + "_norm")[0]
    ab, got0 = _with_carry(_mm(h, wgu, carry=carries[0], name=tag + "_gate_up"), carries[0])
    wd = wd_of(got0)
    F = wd.shape[0]
    (c,), got1 = _with_carry(_rows(_f_swiglu, [ab], [], [(F, bf16)], carry=carries[1], name=tag + "_act"), carries[1])
    y, got2 = _with_carry(_mm(c, wd, scale=0.5, add=x, carry=carries[2], name=tag + "_down"), carries[2])
    return y, (x, n, h, ab, c), (got0, got1, got2)


def _ffn_bwd(saved, wgu, wd, dy, tag, carry_after_dwd=None, carry_after_dwgu=None):
    x, n, h, ab, c = saved
    F = wd.shape[0]
    dc = _mm(dy, wd, tb=True, scale=0.5, name=tag + "_d_act")
    dwd = _mm(c, dy, ta=True, o_blk="m", o_slots=True, tm=F // 2, out_dtype=bf16, scale=0.5, name=tag + "_d_wdown")
    carry_w = carry_after_dwd(dwd) if carry_after_dwd else None
    (dab,), _ = _rows_bwd(_f_swiglu, [ab], [], [dc], name=tag + "_act_bwd", want_rows=[0], row_dtypes={0: bf16})
    dwgu, arr_w = _with_carry(_mm(h, dab, ta=True, carry=carry_w, name=tag + "_d_wgu"), carry_w)
    dwg, dwu = _reshard_pair(dwgu, name=tag + "_reshard_d_wgu")
    carry_h = carry_after_dwgu(dwg, dwu) if carry_after_dwgu else None
    dh, arr_h = _with_carry(_mm(dab, wgu, tb=True, carry=carry_h, name=tag + "_d_h"), carry_h)
    (dx,), (dn,) = _rows_bwd(_f_rmsnorm, [x], [n], [dh], name=tag + "_norm_bwd", want_rows=[0], adds={0: dy})
    return dx, dn, dwg, dwu, dwd, arr_w, arr_h


def kernel(x, ffn1_norm, ffn1_w_gate, ffn1_w_up, ffn1_w_down, mix_norm, w_in, conv_w, conv_b, s5_A_re, s5_A_im, s5_log_dt, s5_B_re, s5_B_im, s5_C_re, s5_C_im, s5_D, s5_w_glu, s5_b_glu, ssd_A_log, ssd_dt_bias, ssd_D, ssd_norm, w_proj_s5, w_proj_ssd, b_gate, w_out, ffn2_norm, ffn2_w_gate, ffn2_w_up, ffn2_w_down, final_norm, loss_target, m_ffn1_norm, m_ffn1_w_gate, m_ffn1_w_up, m_ffn1_w_down, m_mix_norm, m_w_in, m_conv_w, m_conv_b, m_s5_A_re, m_s5_A_im, m_s5_log_dt, m_s5_B_re, m_s5_B_im, m_s5_C_re, m_s5_C_im, m_s5_D, m_s5_w_glu, m_s5_b_glu, m_ssd_A_log, m_ssd_dt_bias, m_ssd_D, m_ssd_norm, m_w_proj_s5, m_w_proj_ssd, m_b_gate, m_w_out, m_ffn2_norm, m_ffn2_w_gate, m_ffn2_w_up, m_ffn2_w_down, m_final_norm, v_ffn1_norm, v_ffn1_w_gate, v_ffn1_w_up, v_ffn1_w_down, v_mix_norm, v_w_in, v_conv_w, v_conv_b, v_s5_A_re, v_s5_A_im, v_s5_log_dt, v_s5_B_re, v_s5_B_im, v_s5_C_re, v_s5_C_im, v_s5_D, v_s5_w_glu, v_s5_b_glu, v_ssd_A_log, v_ssd_dt_bias, v_ssd_D, v_ssd_norm, v_w_proj_s5, v_w_proj_ssd, v_b_gate, v_w_out, v_ffn2_norm, v_ffn2_w_gate, v_ffn2_w_up, v_ffn2_w_down, v_final_norm):
    P = dict(locals())
    T, D = x.shape[1], x.shape[2]
    x0, tgt = x[0], loss_target[0]
    sh = {k: P[k][0] for k in _BIG}

    send = {k: (sh[k] if k == "conv_w" else sh[k].astype(bf16)) for k in _BIG}
    W = {}

    def gather_in(keys):
        return _carry_gather([send[k] for k in keys])

    first_keys = ("ffn1_w_gate", "ffn1_w_up", "conv_w")
    W.update(zip(first_keys, _all_gather([send[k] for k in first_keys], name="gather_weights_first")))
    whole = lambda k: W[k].reshape(-1, D)
    conv_w_full = W["conv_w"].transpose(1, 0, 2).reshape(CONV_K, -1)

    d_inner = N_DEV * sh["w_proj_ssd"].shape[0]
    conv_dim = conv_w_full.shape[1]
    H = ssd_A_log.shape[1]
    G = (conv_dim - d_inner) // (2 * SSD_STATE)
    hpg = H // G
    nc = T // CHUNK
    Gs = D // S5_GROUP
    nblk = Gs // S5_GPB
    NS = Gs * S5_STATE
    seg_lens = (D, d_inner, conv_dim, H, 2 * D)

    cuts = [0, D // 3 // 16 * 16, D // 3 // 16 * 16 + 3 * D // 8 // 16 * 16, D]
    win_rows = [send["w_in"][a_:b_] for a_, b_ in zip(cuts[:-1], cuts[1:])]
    wgu1 = _unshard_pair(W["ffn1_w_gate"], W["ffn1_w_up"], name="unshard_ffn1_gate_up")

    def ffn1_down(got):
        W["ffn1_w_down"] = got[0]
        return whole("ffn1_w_down")

    x1, sv1, (got0, got1, got2) = _ffn_fwd(
        x0, ffn1_norm, wgu1, ffn1_down, "ffn1",
        carries=(_carry_gather([send["ffn1_w_down"], win_rows[0]]), _carry_gather([win_rows[1]]), _carry_gather([win_rows[2]])))
    ffn1_w = (wgu1, whole("ffn1_w_down"))
    W["w_in"] = jnp.concatenate([got0[1], got1[0], got2[0]], axis=1)
    w_u, w_z, w_xbc, w_dt, w_gl = _unshard_w_in(W["w_in"], seg_lens, name="unshard_w_in")
    h2 = _rows(_f_rmsnorm, [x1], [mix_norm], [(D, bf16)], name="mix_norm")[0]
    u_p = _mm(h2, w_u, o_seg=True, name="in_u")
    z = _mm(h2, w_z, name="in_z")
    xbc = _mm(h2, w_xbc, name="in_xbc")
    gl = _mm(h2, w_gl, name="in_gate")
    dtr = _mm(h2, w_dt, name="in_dt")

    rep = lambda a: jnp.repeat(a, S5_GROUP, axis=0)
    lr, li, ldt = s5_A_re[0], s5_A_im[0], s5_log_dt[0].reshape(Gs, 1)
    brt = s5_B_re[0].transpose(0, 2, 1).reshape(Gs * S5_GROUP, S5_STATE)
    bit = s5_B_im[0].transpose(0, 2, 1).reshape(Gs * S5_GROUP, S5_STATE)
    prep_args = (lr, li, ldt, rep(lr), rep(li), rep(ldt), brt, bit)
    ar, ai, bbrt, bbit = _s5_prep(prep_args, name="s5_prep")
    a_r, a_i = ar.reshape(1, NS), ai.reshape(1, NS)
    wb_r = _block_diag(bbrt.reshape(nblk, S5_GPB, S5_GROUP, S5_STATE)).astype(bf16)
    wb_i = _block_diag(bbit.reshape(nblk, S5_GPB, S5_GROUP, S5_STATE)).astype(bf16)
    c4r = s5_C_re[0].reshape(nblk, S5_GPB, S5_GROUP, S5_STATE).transpose(0, 1, 3, 2)
    c4i = s5_C_im[0].reshape(nblk, S5_GPB, S5_GROUP, S5_STATE).transpose(0, 1, 3, 2)
    wc_r, wc_i = _block_diag(c4r).astype(bf16), _block_diag(c4i).astype(bf16)
    mix_keys = ("s5_w_glu", "w_proj_s5", "w_proj_ssd", "w_out")
    sl_r, sl_i, p_r, p_i, got = _s5_local_scan(u_p, wb_r, wb_i, a_r, a_i, reverse=False, carry=gather_in(mix_keys), name="s5_scan")
    W.update(zip(mix_keys, got))
    w_glu, w_p5, w_pssd, w_o = whole("s5_w_glu"), whole("w_proj_s5"), whole("w_proj_ssd"), whole("w_out")
    c_r, c_i = _s5_carry(sl_r[T - NSEG:], sl_i[T - NSEG:], p_r, p_i, reverse=False, name="s5_carry")
    s_r, s_i, ylin = _s5_fix_out(sl_r, sl_i, a_r, a_i, c_r, c_i, wc_r, wc_i, name="s5_fix_out")
    g5 = _rows(_f_s5_post, [ylin, u_p], [s5_D], [(D, f32)], name="s5_gelu")[0]
    v5 = _mm(g5, w_glu, name="s5_glu_mm")
    o5 = _rows(_f_glu, [g5, v5], [s5_b_glu], [(D, bf16)], name="s5_glu")[0]
    p5 = _mm(o5, w_p5, a_seg=True, name="proj_s5")

    xc = _conv_fwd(xbc, conv_w_full, conv_b, name="conv")
    bias_p, alog_p = _pad_cols(ssd_dt_bias, LANES), _pad_cols(ssd_A_log, LANES)
    expand = (lax.broadcasted_iota(jnp.int32, (LANES, d_inner), 1) // HEADDIM
              == lax.broadcasted_iota(jnp.int32, (LANES, d_inner), 0)).astype(f32)
    dt_p, da_p, dt4, a4 = _rows(_f_dt_expand, [dtr], [bias_p, alog_p, expand],
                                [(LANES, f32), (LANES, f32), (d_inner, f32), (d_inner, f32)], name="ssd_dt")
    row_l = lambda a: a[:, :H].reshape(nc, CHUNK, G, hpg).transpose(2, 0, 3, 1).reshape(G, nc, hpg * CHUNK)
    ssd_in = (xc, dt4, a4, row_l(dt_p), row_l(da_p), jnp.repeat(ssd_D, HEADDIM, axis=1), _ssd_consts(hpg))
    ffn2_keys = ("ffn2_w_gate", "ffn2_w_up", "ffn2_w_down")
    y_ssd, hs, got = _ssd2_fwd(*ssd_in, d_inner=d_inner, carry=gather_in(ffn2_keys), name="ssd")
    W.update(zip(ffn2_keys, got))
    ffn2_w = (_unshard_pair(W["ffn2_w_gate"], W["ffn2_w_up"], name="unshard_ffn2_gate_up"), whole("ffn2_w_down"))
    yn = _rows(_f_gated_norm, [y_ssd, z], [ssd_norm], [(d_inner, bf16)], name="ssd_gated_norm")[0]
    pssd = _mm(yn, w_pssd, name="proj_ssd")

    merged = _rows(_f_merge, [gl, p5, pssd], [b_gate], [(D, bf16)], name="merge")[0]
    x2 = _mm(merged, w_o, add=x1, name="out_proj")
    x3, sv2, _ = _ffn_fwd(x2, ffn2_norm, ffn2_w[0], lambda _: ffn2_w[1], "ffn2")
    lossv, dx3, d_final = _loss_stage(x3, tgt, final_norm.reshape(1, D), name="loss")

    gw = {}
    gs = {"final_norm": d_final}
    slot_mm = lambda a_, b_, name, **kw: _mm(a_, b_, ta=True, o_blk="m", o_slots=True, out_dtype=bf16, name=name, **kw)
    core = lax.axis_index("c").astype(jnp.int32).reshape(1)
    chip = (2 * lax.axis_index("x") + lax.axis_index("y")).astype(jnp.int32).reshape(1)
    chip_sums, arrived = {}, {}

    def level1(keys, tag):
        sib = _exchange_sibling([gw[k] for k in keys], name="exchange_sibling_" + tag)
        for k, s_ in zip(keys, sib):
            chip_sums[k] = _chip_sum(gw[k], s_, core, name="chip_sum_" + k)
        return [chip_sums[k] for k in keys]

    dx2, gs["ffn2_norm"], gw["ffn2_w_gate"], gw["ffn2_w_up"], gw["ffn2_w_down"], _, _ = _ffn_bwd(sv2, *ffn2_w, dx3, "ffn2")

    dmerged = _mm(dx2, w_o, tb=True, name="d_merged")
    gw["w_out"] = slot_mm(merged, dx2, "d_w_out")
    (dgl, dp5, dpssd), (gs["b_gate"],) = _rows_bwd(_f_merge, [gl, p5, pssd], [b_gate], [dmerged], name="merge_bwd", want_rows=[0, 1, 2])

    dyn = _mm(dpssd, w_pssd, tb=True, name="d_yn")
    gw["w_proj_ssd"] = slot_mm(yn, dpssd, "d_w_proj_ssd")
    group_a = ("ffn2_w_gate", "ffn2_w_up", "ffn2_w_down", "w_out", "w_proj_ssd")
    parts_a = level1(group_a, "a")
    (dyssd, dz), (gs["ssd_norm"],) = _rows_bwd(_f_gated_norm, [y_ssd, z], [ssd_norm], [dyn], name="ssd_gated_norm_bwd", want_rows=[0, 1])
    dxs, dbm, dcm, ddtc, ddac, ddtw, ddaw, ddh = _ssd2_bwd(*ssd_in, hs, dyssd, d_inner=d_inner, name="ssd_bwd")

    def fold(col, row):
        col = col.reshape(T, G, LANES)[:, :, :hpg].reshape(T, H)
        row = row.reshape(G, nc, hpg, CHUNK).transpose(1, 3, 0, 2).reshape(T, H)
        return _pad_cols(col + row, LANES)

    (ddtr,), (dbias_p, dalog_p) = _rows_bwd(_f_dt, [dtr], [bias_p, alog_p], [fold(ddtc, ddtw), fold(ddac, ddaw)], name="ssd_dt_bwd", want_rows=[0])
    gs["ssd_dt_bias"], gs["ssd_A_log"], gs["ssd_D"] = dbias_p[:, :H], dalog_p[:, :H], ddh[:, 0, :hpg].reshape(1, H)
    dxbc, d_conv_w, gs["conv_b"], arr = _conv_bwd(
        xbc, conv_w_full, conv_b, [dxs, dbm, dcm], carry=_carry_chips(parts_a), name="conv_bwd")
    arrived.update(zip(group_a, arr))
    cwk = sh["conv_w"].shape[1]
    gw["conv_w"] = d_conv_w.reshape(CONV_K, N_CHIP, 2, cwk).transpose(2, 1, 0, 3).reshape(N_DEV, CONV_K, cwk)

    do5 = _mm(dp5, w_p5, tb=True, o_seg=True, name="d_o5")
    gw["w_proj_s5"] = slot_mm(o5, dp5, "d_w_proj_s5", a_seg=True)
    (dg5a, dv5), (gs["s5_b_glu"],) = _rows_bwd(_f_glu, [g5, v5], [s5_b_glu], [do5], name="s5_glu_bwd", want_rows=[0, 1])
    dg5 = _mm(dv5, w_glu, tb=True, add=dg5a, name="d_g5")
    gw["s5_w_glu"] = slot_mm(g5, dv5, "d_w_glu")
    (dylin, du_a), (gs["s5_D"],) = _rows_bwd(_f_s5_post, [ylin, u_p], [s5_D], [dg5], name="s5_gelu_bwd", want_rows=[0, 1])
    wct_r, wct_i = wc_r.transpose(0, 2, 1), -wc_i.transpose(0, 2, 1)
    ql_r, ql_i, pb_r, pb_i = _s5_local_scan(dylin, wct_r, wct_i, a_r, -a_i, reverse=True, name="s5_scan_bwd")
    cb_r, cb_i = _s5_carry(ql_r[:NSEG], ql_i[:NSEG], pb_r, pb_i, reverse=True, name="s5_carry_bwd")
    tc = min(S5_TC, T)

    def before_blocks(s):
        last = s.reshape(T // tc, tc, NS)[:, tc - NSEG:, :]
        wrap = jnp.concatenate([jnp.zeros((1, 1, NS), f32), last[-1:, : NSEG - 1, :]], axis=1)
        return jnp.concatenate([wrap, last[:-1]], axis=0)

    du_p, dwb_r, dwb_i, dwc_r, dwc_i, d_ar, d_ai = _s5_fix_bwd(
        ql_r, ql_i, a_r, -a_i, cb_r, cb_i, s_r, s_i, before_blocks(s_r), before_blocks(s_i), u_p, dylin, du_a, wb_r, wb_i, name="s5_fix_bwd")
    unblk = lambda w: _block_diag_t(w, S5_GROUP, S5_STATE).reshape(Gs * S5_GROUP, S5_STATE)
    rsum = jnp.repeat(jnp.eye(Gs, dtype=f32), S5_GROUP, axis=1)
    d_lr, d_li, d_ldt, d_brt, d_bit = _s5_prep_bwd(
        prep_args, (d_ar.reshape(Gs, S5_STATE), d_ai.reshape(Gs, S5_STATE), unblk(dwb_r), unblk(dwb_i)), rsum, name="s5_prep_bwd")
    gs["s5_A_re"], gs["s5_A_im"], gs["s5_log_dt"] = d_lr, d_li, d_ldt.reshape(1, Gs)
    gs["s5_B_re"] = d_brt.reshape(Gs, S5_GROUP, S5_STATE).transpose(0, 2, 1)
    gs["s5_B_im"] = d_bit.reshape(Gs, S5_GROUP, S5_STATE).transpose(0, 2, 1)
    gs["s5_C_re"] = _block_diag_t(dwc_r, S5_STATE, S5_GROUP).transpose(0, 1, 3, 2).reshape(Gs, S5_GROUP, S5_STATE)
    gs["s5_C_im"] = _block_diag_t(dwc_i, S5_STATE, S5_GROUP).transpose(0, 1, 3, 2).reshape(Gs, S5_GROUP, S5_STATE)

    d_w_in = [_mm(h2, du_p, ta=True, b_seg=True, name="d_w_u"), _mm(h2, dz, ta=True, name="d_w_z"), _mm(h2, dxbc, ta=True, name="d_w_xbc"),
              _mm(h2, ddtr, ta=True, name="d_w_dt"), _mm(h2, dgl, ta=True, name="d_w_gate")]
    gw["w_in"] = _reshard_w_in(d_w_in, seg_lens, sh["w_in"].shape[1], name="reshard_d_w_in")
    group_c = ("w_proj_s5", "s5_w_glu", "conv_w")
    parts_c = level1(group_c + ("w_in",), "c")
    c1 = int(D * 0.45) // 16 * 16
    c2 = c1 + D // 4 // 16 * 16
    win = [parts_c[3][:, :c1], parts_c[3][:, c1:c2], parts_c[3][:, c2:]]
    dh2 = _mm(du_p, w_u, tb=True, a_seg=True, name="d_h2_u")
    dh2, arr = _mm(dz, w_z, tb=True, add=dh2, carry=_carry_chips(parts_c[:3]), name="d_h2_z")
    arrived.update(zip(group_c, arr))
    dh2, (arr0,) = _mm(dxbc, w_xbc, tb=True, add=dh2, carry=_carry_chips([win[0]]), name="d_h2_xbc")
    dh2, (arr1,) = _mm(dgl, w_gl, tb=True, add=dh2, carry=_carry_chips([win[1]]), name="d_h2_gate")
    dh2 = _mm(ddtr, w_dt, tb=True, add=dh2, name="d_h2_dt")
    (dx1,), (gs["mix_norm"],) = _rows_bwd(_f_rmsnorm, [x1], [mix_norm], [dh2], name="mix_norm_bwd", want_rows=[0], adds={0: dx2})

    def carry_ffn1_down(dwd):
        gw["ffn1_w_down"] = dwd
        return _carry_chips(level1(("ffn1_w_down",), "d") + [win[2]])

    def carry_ffn1_gate_up(dwg, dwu):
        gw["ffn1_w_gate"], gw["ffn1_w_up"] = dwg, dwu
        return _carry_chips(level1(("ffn1_w_gate", "ffn1_w_up"), "e"))

    dx0, gs["ffn1_norm"], _, _, _, arr_w, arr_h = _ffn_bwd(
        sv1, *ffn1_w, dx1, "ffn1", carry_after_dwd=carry_ffn1_down, carry_after_dwgu=carry_ffn1_gate_up)
    arrived["ffn1_w_down"], arr2 = arr_w
    arrived["ffn1_w_gate"], arrived["ffn1_w_up"] = arr_h
    arrived["w_in"] = jnp.concatenate([arr0, arr1, arr2], axis=1)

    small_shapes = {k: (P[k][0].shape if P[k].ndim > 1 else P[k].shape) for k in _SMALL}
    pack = lambda d: jnp.concatenate([_pad_flat(d[k], TILE_ELEMS) for k in _SMALL]).reshape(-1, LANES)
    gsmall = _sum_slots(_all_gather([pack(gs)], name="gather_small_grads")[0], name="sum_small_grads")
    snum = {k: math.prod(small_shapes[k]) for k in _SMALL}
    ssz = {k: -(-snum[k] // TILE_ELEMS) * TILE_ELEMS for k in _SMALL}

    grads, delta, new_m, new_v = {}, {}, {}, {}
    for k in _BIG:
        grads[k], delta[k], new_m[k], new_v[k] = _reduce_adamw(
            chip_sums[k], arrived[k], chip, P[k][0], P["m_" + k][0], P["v_" + k][0], name="adamw_" + k)
    d_s, m_s, v_s = _adamw(gsmall, pack({k: P[k] for k in _SMALL}), pack({k: P["m_" + k] for k in _SMALL}),
                           pack({k: P["v_" + k] for k in _SMALL}), name="adamw_small")
    off = 0
    gflat, dflat, mflat, vflat = gsmall.reshape(-1), d_s.reshape(-1), m_s.reshape(-1), v_s.reshape(-1)
    for k in _SMALL:
        n = snum[k]
        grads[k], delta[k], new_m[k], new_v[k] = (a[off:off + n] for a in (gflat, dflat, mflat, vflat))
        off += ssz[k]

    loss = lax.psum(lossv[0, 0], ("x", "y", "c"))
    out = [loss, dx0.reshape(x.shape)]
    for d in (grads, delta, new_m, new_v):
        out += [d[k].reshape(P[k].shape) for k in _WEIGHTS]
    return tuple(out)
```

```python
import math
from typing import Callable, NamedTuple

import jax
import jax.numpy as jnp
from jax import lax
from jax.experimental import pallas as pl
from jax.experimental.pallas import tpu as pltpu

f32 = jnp.float32
bf16 = jnp.bfloat16
_S = jax.ShapeDtypeStruct

EPS = 1e-6
S5_GROUP = 16
S5_STATE = 64
HEADDIM = 64
SSD_STATE = 128
CHUNK = 64
CONV_K = 4
NSEG = 8
S5_GPB = 16
N_DEV = 8
LANES = 128
TILE_ELEMS = 8 * LANES

ADAM_LR = 0.001
ADAM_B1 = 0.9
ADAM_B2 = 0.999
ADAM_EPS = 1e-08
ADAM_WD = 0.01
ADAM_STEP = 10

VMEM_LIMIT = 56 * 1024 * 1024
MM_FULL_K = 3072
MM_MAX_TN = 3072
EPI_TM = 128
MESH = pl.DeviceIdType.MESH


def _cparams(sem=None):
    return pltpu.CompilerParams(dimension_semantics=sem, vmem_limit_bytes=VMEM_LIMIT)


def _pick(dim, pref, align=LANES):
    best = None
    t = align
    while t <= min(dim, pref):
        if dim % t == 0:
            best = t
        t += align
    return best or dim


def _slot_of(k):
    return (k & 1) * (N_DEV // 2) + (k >> 1)


def _mm(a, b, *, name, ta=False, tb=False, a_blk=None, b_blk=None, o_blk=None, o_slots=False, a_seg=False, b_seg=False,
        o_seg=False, tm=None, out_dtype=f32, scale=1.0, add=None, epi=None, carry=None):
    a2, b2 = a.shape[-2:], b.shape[-2:]
    Ma, Ka = (a2[1], a2[0]) if ta else a2
    Kb, Nb = (b2[1], b2[0]) if tb else b2
    M = Ma * (a.shape[0] if a_blk == "m" else 1)
    K = Ka * (a.shape[0] if a_blk == "k" else 1)
    N = Nb * (b.shape[0] if b_blk == "n" else 1)
    assert K == Kb * (b.shape[0] if b_blk == "k" else 1), (a.shape, b.shape, ta, tb, a_blk, b_blk)
    assert (a.ndim == 3) == (a_blk is not None) and (b.ndim == 3) == (b_blk is not None)
    tm = Ma if a_blk == "m" else (tm or _pick(M, 512))
    tn = Nb if b_blk == "n" else _pick(N, MM_MAX_TN)
    if a_blk == "k" or b_blk == "k":
        tk = Ka if a_blk == "k" else Kb
        assert tk == (Kb if b_blk == "k" else tk)
    else:
        tk = K if K <= MM_FULL_K else _pick(K, 1024 if ta else MM_FULL_K)
    if (a_seg and not ta) or o_seg:
        tm = M // NSEG
    if (a_seg and ta) or b_seg:
        tk = K // NSEG
    gm, gn, nk = M // tm, N // tn, K // tk
    assert not (add is not None and (o_seg or o_blk)) and not (o_blk and o_seg)

    if a_seg:
        assert a.ndim == 2
        a = a.reshape(a.shape[0] // NSEG, NSEG * a.shape[1])
        if ta:
            a_spec = pl.BlockSpec((tk, tm), lambda i, j, k: (0, k * (Ma // tm) + i))
        else:
            a_spec = pl.BlockSpec((tm, tk), lambda i, j, k: (0, i * (Ka // tk) + k))
    elif a.ndim == 3:
        lead = (lambda i, k: i) if a_blk == "m" else (lambda i, k: k)
        if ta:
            a_spec = pl.BlockSpec((None, tk, tm), lambda i, j, k: (lead(i, k), 0 if a_blk == "k" else k, 0 if a_blk == "m" else i))
        else:
            a_spec = pl.BlockSpec((None, tm, tk), lambda i, j, k: (lead(i, k), 0 if a_blk == "m" else i, 0 if a_blk == "k" else k))
    else:
        a_spec = pl.BlockSpec((tk, tm), lambda i, j, k: (k, i)) if ta else pl.BlockSpec((tm, tk), lambda i, j, k: (i, k))
    if b_seg:
        assert b.ndim == 2 and not tb
        b = b.reshape(b.shape[0] // NSEG, NSEG * b.shape[1])
        b_spec = pl.BlockSpec((tk, tn), lambda i, j, k: (0, k * (Nb // tn) + j))
    elif b.ndim == 3:
        lead = (lambda j, k: j) if b_blk == "n" else (lambda j, k: k)
        if tb:
            b_spec = pl.BlockSpec((None, tn, tk), lambda i, j, k: (lead(j, k), 0 if b_blk == "n" else j, 0 if b_blk == "k" else k))
        else:
            b_spec = pl.BlockSpec((None, tk, tn), lambda i, j, k: (lead(j, k), 0 if b_blk == "k" else k, 0 if b_blk == "n" else j))
    else:
        b_spec = pl.BlockSpec((tn, tk), lambda i, j, k: (j, k)) if tb else pl.BlockSpec((tk, tn), lambda i, j, k: (k, j))
    slot = _slot_of if o_slots else (lambda k: k)
    if o_blk == "n":
        assert gn == N_DEV or not o_slots
        o_shape, o_spec = (gn, M, tn), pl.BlockSpec((None, tm, tn), lambda i, j, k: (slot(j), i, 0))
    elif o_blk == "m" and o_slots and gm < N_DEV:
        rs = M // N_DEV
        per_tile = tm // rs
        assert per_tile % 2 == 0 and tm % rs == 0
        o_shape = (2, N_CHIP, rs, N)
        o_spec = pl.BlockSpec((2, per_tile // 2, rs, tn), lambda i, j, k: (0, i, 0, j))
    elif o_blk == "m":
        assert gm == N_DEV or not o_slots
        o_shape, o_spec = (gm, tm, N), pl.BlockSpec((None, tm, tn), lambda i, j, k: (slot(i), 0, j))
    elif o_seg:
        o_shape, o_spec = (tm, NSEG * N), pl.BlockSpec((tm, tn), lambda i, j, k: (0, i * (N // tn) + j))
    else:
        o_shape, o_spec = (M, N), pl.BlockSpec((tm, tn), lambda i, j, k: (i, j))
    dims = (((0 if ta else 1,), (1 if tb else 0,)), ((), ()))
    if epi is not None:
        assert gn == 1 and add is None and o_blk is None and not o_seg
        epi_fn, add, epi_w = epi
        o_shape, o_spec = (M, epi_w), pl.BlockSpec((tm, epi_w), lambda i, j, k: (i, 0))
    has_add = add is not None
    add_spec = pl.BlockSpec((tm, add.shape[1]), lambda i, j, k: (i, 0)) if epi is not None else o_spec

    carry = carry or _NO_CARRY
    n_in = 2 + has_add

    def body(*refs):
        own, c_in, c_out, c_sems = _carry_split(carry, refs, n_in, 1)
        a_ref, b_ref = own[0], own[1]
        add_ref = own[2] if has_add else None
        o_ref, acc_ref = own[-2], own[-1]
        i, j, k = pl.program_id(0), pl.program_id(1), pl.program_id(2)
        _carry_start(carry, c_in, c_out, c_sems, (i == 0) & (j == 0) & (k == 0))

        @pl.when(k == 0)
        def _():
            acc_ref[...] = jnp.zeros_like(acc_ref)

        acc_ref[...] += lax.dot_general(a_ref[...].astype(bf16), b_ref[...].astype(bf16), dims, preferred_element_type=f32)

        @pl.when(k == nk - 1)
        def _():
            r = acc_ref[...] * scale
            if epi is not None:
                r = epi_fn(r, add_ref[...].astype(f32))
            elif has_add:
                r = r + add_ref[...].astype(f32)
            if len(o_shape) == 4:
                rs = o_shape[2]
                for chip_l in range(o_ref.shape[1]):
                    for core in range(2):
                        dev = 2 * chip_l + core
                        o_ref[core, chip_l] = r[dev * rs:(dev + 1) * rs].astype(out_dtype)
            else:
                o_ref[...] = r.astype(out_dtype)

        _carry_finish(carry, c_in, c_out, c_sems, (i == gm - 1) & (j == gn - 1) & (k == nk - 1))

    ins = [a, b] + ([add] if has_add else []) + list(carry.ins)
    in_specs = [a_spec, b_spec] + ([add_spec] if has_add else []) + [_ANY] * len(carry.ins)
    res = pl.pallas_call(
        body, name=name, grid=(gm, gn, nk), in_specs=in_specs, out_specs=[o_spec] + [_ANY] * len(carry.out_shapes),
        out_shape=[_S(o_shape, out_dtype)] + list(carry.out_shapes),
        scratch_shapes=[pltpu.VMEM((tm, tn), f32)] + list(carry.sems),
        compiler_params=_cparams(("arbitrary",) * 3 if carry.ins else ("parallel", "parallel", "arbitrary")),
    )(*ins)
    out = res[0]
    if len(o_shape) == 4:
        out = out.reshape(N_DEV, o_shape[2], N)
    elif o_seg:
        out = out.reshape(M, N)
    return (out, list(res[1:])) if carry.ins else out


def _row_tile(T, widths):
    budget = 6 * 1024 * 1024
    tb = max(16, budget // (4 * sum(widths)))
    return _pick(T, tb, align=16)


def _rows(fn, rows, params, outs, *, name, carry=None):
    T = rows[0].shape[0]
    nr, npar = len(rows), len(params)
    tb = _row_tile(T, [r.shape[1] for r in rows] + [w for w, _ in outs])
    carry = carry or _NO_CARRY

    def body(*refs):
        own, c_in, c_out, c_sems = _carry_split(carry, refs, nr + npar, len(outs))
        _carry_start(carry, c_in, c_out, c_sems, pl.program_id(0) == 0)
        ins = [r[...].astype(f32) for r in own[: nr + npar]]
        res = fn(*ins)
        for o_ref, r in zip(own[nr + npar:], res):
            o_ref[...] = r.astype(o_ref.dtype)
        _carry_finish(carry, c_in, c_out, c_sems, pl.program_id(0) == T // tb - 1)

    in_specs = [pl.BlockSpec((tb, r.shape[1]), lambda i: (i, 0)) for r in rows]
    in_specs += [pl.BlockSpec(p.shape, lambda i: (0, 0)) for p in params]
    out_specs = [pl.BlockSpec((tb, w), lambda i: (i, 0)) for w, _ in outs]
    res = pl.pallas_call(
        body, name=name, grid=(T // tb,), in_specs=in_specs + [_ANY] * len(carry.ins),
        out_specs=out_specs + [_ANY] * len(carry.out_shapes), out_shape=[_S((T, w), d) for w, d in outs] + list(carry.out_shapes),
        scratch_shapes=list(carry.sems), compiler_params=_cparams(("arbitrary",) if carry.ins else ("parallel",)),
    )(*rows, *params, *carry.ins)
    return (tuple(res[:len(outs)]), list(res[len(outs):])) if carry.ins else tuple(res)


def _rows_bwd(fn, rows, params, cots, *, name, want_rows, row_dtypes=None, adds=None):
    T = rows[0].shape[0]
    nr, npar, nc = len(rows), len(params), len(cots)
    adds = adds or {}
    add_idx = sorted(adds)
    row_dtypes = row_dtypes or {}
    widths = [r.shape[1] for r in rows] + [c.shape[1] for c in cots] + [rows[i].shape[1] for i in want_rows]
    tb = _row_tile(T, widths)

    def body(*refs):
        ins = [r[...].astype(f32) for r in refs[: nr + npar]]
        cot = tuple(r[...].astype(f32) for r in refs[nr + npar: nr + npar + nc])
        add_refs = refs[nr + npar + nc: nr + npar + nc + len(add_idx)]
        out_refs = refs[nr + npar + nc + len(add_idx):]
        _, vjp = jax.vjp(lambda *a: tuple(fn(*a)), *ins)
        g = vjp(cot)
        for o_ref, i in zip(out_refs[: len(want_rows)], want_rows):
            r = g[i]
            if i in adds:
                r = r + add_refs[add_idx.index(i)][...].astype(f32)
            o_ref[...] = r.astype(o_ref.dtype)
        first = pl.program_id(0) == 0
        for o_ref, gp in zip(out_refs[len(want_rows):], g[nr:]):
            @pl.when(first)
            def _(o_ref=o_ref):
                o_ref[...] = jnp.zeros_like(o_ref)

            o_ref[...] += gp

    in_specs = [pl.BlockSpec((tb, r.shape[1]), lambda i: (i, 0)) for r in rows]
    in_specs += [pl.BlockSpec(p.shape, lambda i: (0, 0)) for p in params]
    in_specs += [pl.BlockSpec((tb, c.shape[1]), lambda i: (i, 0)) for c in cots]
    in_specs += [pl.BlockSpec((tb, adds[i].shape[1]), lambda i_: (i_, 0)) for i in add_idx]
    out_specs = [pl.BlockSpec((tb, rows[i].shape[1]), lambda i_: (i_, 0)) for i in want_rows]
    out_specs += [pl.BlockSpec(p.shape, lambda i: (0, 0)) for p in params]
    out_shape = [_S(rows[i].shape, row_dtypes.get(i, f32)) for i in want_rows] + [_S(p.shape, f32) for p in params]
    res = pl.pallas_call(
        body, name=name, grid=(T // tb,), in_specs=in_specs, out_specs=out_specs, out_shape=out_shape,
        compiler_params=_cparams(("arbitrary",)),
    )(*rows, *params, *cots, *[adds[i] for i in add_idx])
    return list(res[: len(want_rows)]), list(res[len(want_rows):])


def _f_rmsnorm(x, g):
    return (x * lax.rsqrt(jnp.mean(x * x, axis=-1, keepdims=True) + EPS) * g,)


def _f_swiglu(ab):
    F = ab.shape[1] // 2
    return (jax.nn.silu(ab[:, :F]) * ab[:, F:],)


def _f_s5_post(y, u, d):
    return (jax.nn.gelu(y + d * u),)


def _f_glu(g, v, b):
    return (g * jax.nn.sigmoid(v + b),)


def _f_gated_norm(y, z, w):
    return _f_rmsnorm(y * jax.nn.silu(z), w)


def _f_merge(gl, p5, pssd, b):
    D = p5.shape[1]
    gates = jax.nn.sigmoid(gl + b)
    return (gates[:, :D] * p5 + gates[:, D:] * pssd,)


def _f_dt(dtr, bias, a_log):
    dt = jax.nn.softplus(dtr + bias)
    return dt, dt * (-jnp.exp(a_log))


def _f_dt_expand(dtr, bias, a_log, e):
    dt, a = _f_dt(dtr, bias, a_log)
    return dt, a, _doth(dt, e), _doth(a, e)


def _loss_stage(x, tgt, g, *, name):
    T, D = x.shape
    tb = _row_tile(T, [D, D, D])

    def f(xb, gb, tb_):
        y = _f_rmsnorm(xb, gb)[0]
        return 0.5 * jnp.sum(jnp.mean(jnp.square(y - tb_), axis=-1, keepdims=True), axis=0, keepdims=True)

    def body(x_ref, t_ref, g_ref, l_ref, dx_ref, dg_ref):
        tv = t_ref[...]
        val, vjp = jax.vjp(lambda a, b: f(a, b, tv), x_ref[...], g_ref[...])
        dx, dg = vjp(jnp.ones((1, 1), f32))
        dx_ref[...] = dx

        @pl.when(pl.program_id(0) == 0)
        def _():
            l_ref[...] = jnp.zeros_like(l_ref)
            dg_ref[...] = jnp.zeros_like(dg_ref)

        l_ref[...] += jnp.broadcast_to(val, l_ref.shape)
        dg_ref[...] += dg

    row = pl.BlockSpec((tb, D), lambda i: (i, 0))
    par = pl.BlockSpec((1, D), lambda i: (0, 0))
    return pl.pallas_call(
        body, name=name, grid=(T // tb,), in_specs=[row, row, par],
        out_specs=[pl.BlockSpec((1, LANES), lambda i: (0, 0)), row, par],
        out_shape=[_S((1, LANES), f32), _S((T, D), f32), _S((1, D), f32)], compiler_params=_cparams(("arbitrary",)),
    )(x, tgt, g)


CONV_R = 64
HALO = 8


def _conv_shifts_down(ref, t, n_tiles):
    if isinstance(t, int) and t == 0:
        cur = ref[0:CONV_R, :]
        row = lax.broadcasted_iota(jnp.int32, cur.shape, 0)
        return [cur] + [jnp.where(row >= s, pltpu.roll(cur, s, axis=0), 0.0) for s in range(1, CONV_K)]
    win = ref[pl.ds(pl.multiple_of(t * CONV_R - HALO, HALO), CONV_R + HALO), :]
    return [win[HALO:]] + [pltpu.roll(win, s, axis=0)[HALO:] for s in range(1, CONV_K)]


def _conv_shifts_up(ref, t, n_tiles):
    if isinstance(t, int):
        cur = ref[t * CONV_R:(t + 1) * CONV_R, :]
        row = lax.broadcasted_iota(jnp.int32, cur.shape, 0)
        return [cur] + [jnp.where(row < CONV_R - s, pltpu.roll(cur, CONV_R - s, axis=0), 0.0) for s in range(1, CONV_K)]
    win = ref[pl.ds(pl.multiple_of(t * CONV_R, HALO), CONV_R + HALO), :]
    return [win[:CONV_R]] + [pltpu.roll(win, CONV_R + HALO - s, axis=0)[:CONV_R] for s in range(1, CONV_K)]


def _conv_pre(shifted, w, b):
    pre = b
    for k in range(CONV_K):
        pre = pre + w[k:k + 1, :] * shifted[CONV_K - 1 - k]
    return pre


def _conv_fwd(x, w, b, *, name):
    T, C = x.shape
    cb = _pick(C, 128)
    n_tiles = T // CONV_R

    def body(x_ref, w_ref, b_ref, o_ref):
        w, b_ = w_ref[...], b_ref[...]
        o_ref[0:CONV_R, :] = jax.nn.silu(_conv_pre(_conv_shifts_down(x_ref, 0, n_tiles), w, b_))

        def step(t, c):
            rows = pl.ds(pl.multiple_of(t * CONV_R, CONV_R), CONV_R)
            o_ref[rows, :] = jax.nn.silu(_conv_pre(_conv_shifts_down(x_ref, t, n_tiles), w, b_))
            return c

        lax.fori_loop(1, n_tiles, step, 0)

    col = pl.BlockSpec((T, cb), lambda j: (0, j))
    return pl.pallas_call(
        body, name=name, grid=(C // cb,), in_specs=[col, pl.BlockSpec((CONV_K, cb), lambda j: (0, j)), pl.BlockSpec((1, cb), lambda j: (0, j))],
        out_specs=col, out_shape=_S((T, C), f32), compiler_params=_cparams(("parallel",)),
    )(x, w, b)


def _conv_bwd(x, w, b, dy, *, name, carry=None):
    T, C = x.shape
    cb = _pick(C, 128)
    carry = carry or _NO_CARRY
    ends = []
    for d in dy:
        ends.append((ends[-1] if ends else 0) + d.shape[1] // cb)
    assert ends[-1] == C // cb and all(d.shape[1] % cb == 0 for d in dy)
    npc = len(dy)
    n_tiles = T // CONV_R

    def body(*refs):
        own, c_in, c_out, c_sems = _carry_split(carry, refs, 3 + npc, 3)
        x_ref, w_ref, b_ref = own[:3]
        dy_refs, (dx_ref, dw_ref, db_ref, dpre_ref) = own[3:3 + npc], own[3 + npc:]
        _carry_start(carry, c_in, c_out, c_sems, pl.program_id(0) == 0)
        j = pl.program_id(0)
        wv, bv = w_ref[...], b_ref[...]

        def fold8(v):
            return jnp.sum(v.reshape(CONV_R // 8, 8, cb), axis=0)

        def first_pass(t, acc):
            rows = slice(0, CONV_R) if isinstance(t, int) else pl.ds(pl.multiple_of(t * CONV_R, CONV_R), CONV_R)
            shifted = _conv_shifts_down(x_ref, t, n_tiles)
            pre = _conv_pre(shifted, wv, bv)
            dyv = dy_refs[-1][rows, :]
            for p in range(npc - 2, -1, -1):
                dyv = jnp.where(j < ends[p], dy_refs[p][rows, :], dyv)
            sg = jax.nn.sigmoid(pre)
            dpre = dyv * sg * (1.0 + pre * (1.0 - sg))
            dpre_ref[rows, :] = dpre
            return tuple(acc[k] + fold8(dpre * shifted[CONV_K - 1 - k]) for k in range(CONV_K)) + (acc[CONV_K] + fold8(dpre),)

        zero = jnp.zeros((8, cb), f32)
        acc = lax.fori_loop(1, n_tiles, first_pass, first_pass(0, (zero,) * (CONV_K + 1)))
        for k in range(CONV_K):
            dw_ref[k:k + 1, :] = jnp.sum(acc[k], axis=0, keepdims=True)
        db_ref[...] = jnp.sum(acc[CONV_K], axis=0, keepdims=True)

        def dx_of(t):
            up = _conv_shifts_up(dpre_ref, t, n_tiles)
            dx = wv[CONV_K - 1:CONV_K, :] * up[0]
            for k in range(CONV_K - 1):
                dx = dx + wv[k:k + 1, :] * up[CONV_K - 1 - k]
            return dx

        def second_pass(t, c):
            dx_ref[pl.ds(pl.multiple_of(t * CONV_R, CONV_R), CONV_R), :] = dx_of(t)
            return c

        lax.fori_loop(0, n_tiles - 1, second_pass, 0)
        dx_ref[(n_tiles - 1) * CONV_R:, :] = dx_of(n_tiles - 1)
        _carry_finish(carry, c_in, c_out, c_sems, pl.program_id(0) == C // cb - 1)

    col = pl.BlockSpec((T, cb), lambda j: (0, j))
    wsp = pl.BlockSpec((CONV_K, cb), lambda j: (0, j))
    bsp = pl.BlockSpec((1, cb), lambda j: (0, j))
    starts = [0] + ends[:-1]
    dy_specs = [pl.BlockSpec((T, cb), lambda j, s=s, e=e: (0, jnp.clip(j, s, e - 1) - s)) for s, e in zip(starts, ends)]
    res = pl.pallas_call(
        body, name=name, grid=(C // cb,), in_specs=[col, wsp, bsp] + dy_specs + [_ANY] * len(carry.ins),
        out_specs=[col, wsp, bsp] + [_ANY] * len(carry.out_shapes),
        out_shape=[_S((T, C), f32), _S((CONV_K, C), f32), _S((1, C), f32)] + list(carry.out_shapes),
        scratch_shapes=[pltpu.VMEM((T, cb), f32)] + list(carry.sems),
        compiler_params=_cparams(("arbitrary",) if carry.ins else ("parallel",)),
    )(x, w, b, *dy, *carry.ins)
    return (*res[:3], list(res[3:]))


def _f_s5_prep(lr, li, ldt, lrb, lib, ldtb, brt, bit):
    def disc(lr_, li_, ldt_):
        dt = jnp.exp(ldt_)
        mag = jnp.exp(lr_ * dt)
        ar, ai = mag * jnp.cos(li_ * dt), mag * jnp.sin(li_ * dt)
        den = lr_ * lr_ + li_ * li_
        cr = ((ar - 1.0) * lr_ + ai * li_) / den
        ci = (ai * lr_ - (ar - 1.0) * li_) / den
        return ar, ai, cr, ci

    ar, ai, _, _ = disc(lr, li, ldt)
    _, _, cr, ci = disc(lrb, lib, ldtb)
    return ar, ai, cr * brt - ci * bit, cr * bit + ci * brt


def _s5_prep(args, *, name):
    G, N = args[0].shape
    GM = args[3].shape[0]

    def body(*refs):
        res = _f_s5_prep(*[r[...] for r in refs[:8]])
        for o, r in zip(refs[8:], res):
            o[...] = r

    return pl.pallas_call(body, name=name, out_shape=[_S((G, N), f32)] * 2 + [_S((GM, N), f32)] * 2)(*args)


def _s5_prep_bwd(args, cots, rsum, *, name):
    G, N = args[0].shape
    GM = args[3].shape[0]

    def body(*refs):
        ins = [r[...] for r in refs[:8]]
        cot = tuple(r[...] for r in refs[8:12])
        rs = refs[12][...]
        _, vjp = jax.vjp(_f_s5_prep, *ins)
        g = vjp(cot)
        fold = lambda v: jnp.dot(rs, v, preferred_element_type=f32, precision=lax.Precision.HIGHEST)
        o = refs[13:]
        o[0][...] = g[0] + fold(g[3])
        o[1][...] = g[1] + fold(g[4])
        o[2][...] = g[2] + fold(jnp.broadcast_to(g[5], (GM, LANES)))[:, 0:1]
        o[3][...] = g[6]
        o[4][...] = g[7]

    return pl.pallas_call(
        body, name=name, out_shape=[_S((G, N), f32), _S((G, N), f32), _S((G, 1), f32), _S((GM, N), f32), _S((GM, N), f32)],
    )(*args, *cots, rsum)


S5_TC = 512


def _s5_local_scan(src, w_r, w_i, a_r, a_i, *, reverse, name, carry=None):
    T, C = src.shape
    nblk, cb, sb = w_r.shape
    NS = nblk * sb
    tc = min(S5_TC, T)
    nT, nt = T // tc, tc // NSEG
    tmap = (lambda i: nT - 1 - i) if reverse else (lambda i: i)

    carry = carry or _NO_CARRY

    def body(*refs):
        own, c_in, c_out, c_sems = _carry_split(carry, refs, 5, 4)
        u_ref, wr_ref, wi_ref, ar_ref, ai_ref, sr_ref, si_ref, pr_ref, pi_ref, st_r, st_i, pw_r, pw_i = own
        _carry_start(carry, c_in, c_out, c_sems, (pl.program_id(0) == 0) & (pl.program_id(1) == 0))

        @pl.when(pl.program_id(1) == 0)
        def _():
            st_r[...] = jnp.zeros_like(st_r)
            st_i[...] = jnp.zeros_like(st_i)
            pw_r[...] = jnp.ones_like(pw_r)
            pw_i[...] = jnp.zeros_like(pw_i)

        u = u_ref[...].astype(bf16)
        sr_ref[...] = jnp.dot(u, wr_ref[...], preferred_element_type=f32)
        si_ref[...] = jnp.dot(u, wi_ref[...], preferred_element_type=f32)
        ar = jnp.broadcast_to(ar_ref[...], (NSEG, sb))
        ai = jnp.broadcast_to(ai_ref[...], (NSEG, sb))

        def step(k, c):
            cr, ci, qr, qi = c
            kk = (nt - 1 - k) if reverse else k
            rows = pl.ds(pl.multiple_of(kk * NSEG, NSEG), NSEG)
            nr = ar * cr - ai * ci + sr_ref[rows, :]
            ni = ar * ci + ai * cr + si_ref[rows, :]
            sr_ref[rows, :] = nr
            si_ref[rows, :] = ni
            return nr, ni, ar * qr - ai * qi, ar * qi + ai * qr

        cr, ci, qr, qi = lax.fori_loop(0, nt, step, (st_r[...], st_i[...], pw_r[...], pw_i[...]), unroll=8)
        st_r[...], st_i[...], pw_r[...], pw_i[...] = cr, ci, qr, qi
        pr_ref[...] = qr
        pi_ref[...] = qi
        _carry_finish(carry, c_in, c_out, c_sems, (pl.program_id(0) == nblk - 1) & (pl.program_id(1) == nT - 1))

    blk = pl.BlockSpec((tc, sb), lambda j, i: (tmap(i), j))
    wsp = pl.BlockSpec((None, cb, sb), lambda j, i: (j, 0, 0))
    asp = pl.BlockSpec((1, sb), lambda j, i: (0, j))
    psp = pl.BlockSpec((NSEG, sb), lambda j, i: (0, j))
    res = pl.pallas_call(
        body, name=name, grid=(nblk, nT),
        in_specs=[pl.BlockSpec((tc, cb), lambda j, i: (tmap(i), j)), wsp, wsp, asp, asp] + [_ANY] * len(carry.ins),
        out_specs=[blk, blk, psp, psp] + [_ANY] * len(carry.out_shapes),
        out_shape=[_S((T, NS), f32)] * 2 + [_S((NSEG, NS), f32)] * 2 + list(carry.out_shapes),
        scratch_shapes=[pltpu.VMEM((NSEG, sb), f32)] * 4 + list(carry.sems),
        compiler_params=_cparams(("arbitrary", "arbitrary") if carry.ins else ("parallel", "arbitrary")),
    )(src, w_r, w_i, a_r, a_i, *carry.ins)
    return (*res[:4], list(res[4:])) if carry.ins else res


def _s5_carry(e_r, e_i, p_r, p_i, *, reverse, name):
    NS = e_r.shape[1]

    def body(er_ref, ei_ref, pr_ref, pi_ref, cr_ref, ci_ref):
        ar, ai = pr_ref[0:1, :], pi_ref[0:1, :]
        cr = jnp.zeros((1, NS), f32)
        ci = jnp.zeros((1, NS), f32)
        order = list(range(NSEG - 1, -1, -1)) if reverse else list(range(NSEG))
        cr_ref[order[0]:order[0] + 1, :] = cr
        ci_ref[order[0]:order[0] + 1, :] = ci
        for prev, q in zip(order[:-1], order[1:]):
            er, ei = er_ref[prev:prev + 1, :], ei_ref[prev:prev + 1, :]
            cr, ci = er + ar * cr - ai * ci, ei + ar * ci + ai * cr
            cr_ref[q:q + 1, :] = cr
            ci_ref[q:q + 1, :] = ci

    return pl.pallas_call(body, name=name, out_shape=[_S((NSEG, NS), f32)] * 2)(e_r, e_i, p_r, p_i)


def _s5_fix_out(sl_r, sl_i, a_r, a_i, c_r, c_i, wc_r, wc_i, *, name):
    T, NS = sl_r.shape
    nblk, sb, cb = wc_r.shape
    tc = min(S5_TC, T)
    nT, nt = T // tc, tc // NSEG

    def body(lr_ref, li_ref, ar_ref, ai_ref, cr_ref, ci_ref, wr_ref, wi_ref, sr_ref, si_ref, y_ref, pw_r, pw_i):
        @pl.when(pl.program_id(1) == 0)
        def _():
            pw_r[...] = jnp.ones_like(pw_r)
            pw_i[...] = jnp.zeros_like(pw_i)

        ar = jnp.broadcast_to(ar_ref[...], (NSEG, sb))
        ai = jnp.broadcast_to(ai_ref[...], (NSEG, sb))
        cr, ci = cr_ref[...], ci_ref[...]

        def step(k, c):
            qr, qi = c
            qr, qi = ar * qr - ai * qi, ar * qi + ai * qr
            rows = pl.ds(pl.multiple_of(k * NSEG, NSEG), NSEG)
            sr_ref[rows, :] = lr_ref[rows, :] + qr * cr - qi * ci
            si_ref[rows, :] = li_ref[rows, :] + qr * ci + qi * cr
            return qr, qi

        qr, qi = lax.fori_loop(0, nt, step, (pw_r[...], pw_i[...]), unroll=8)
        pw_r[...], pw_i[...] = qr, qi
        y_ref[...] = (jnp.dot(sr_ref[...].astype(bf16), wr_ref[...], preferred_element_type=f32)
                      - jnp.dot(si_ref[...].astype(bf16), wi_ref[...], preferred_element_type=f32))

    blk = pl.BlockSpec((tc, sb), lambda j, i: (i, j))
    asp = pl.BlockSpec((1, sb), lambda j, i: (0, j))
    csp = pl.BlockSpec((NSEG, sb), lambda j, i: (0, j))
    wsp = pl.BlockSpec((None, sb, cb), lambda j, i: (j, 0, 0))
    return pl.pallas_call(
        body, name=name, grid=(nblk, nT), in_specs=[blk, blk, asp, asp, csp, csp, wsp, wsp],
        out_specs=[blk, blk, pl.BlockSpec((tc, cb), lambda j, i: (i, j))],
        out_shape=[_S((T, NS), f32)] * 2 + [_S((T, nblk * cb), f32)],
        scratch_shapes=[pltpu.VMEM((NSEG, sb), f32)] * 2, compiler_params=_cparams(("parallel", "arbitrary")),
    )(sl_r, sl_i, a_r, a_i, c_r, c_i, wc_r, wc_i)


def _s5_fix_bwd(ql_r, ql_i, ab_r, ab_i, c_r, c_i, s_r, s_i, sb_r, sb_i, u, dy, du_add, w_r, w_i, *, name):
    T, NS = ql_r.shape
    nblk, cb, sb = w_r.shape
    tc = min(S5_TC, T)
    nT, nt = T // tc, tc // NSEG
    tmap = lambda i: nT - 1 - i

    def body(lr_ref, li_ref, ar_ref, ai_ref, cr_ref, ci_ref, sr_ref, si_ref, br_ref, bi_ref, u_ref, dy_ref, dua_ref, wr_ref, wi_ref,
             du_ref, dwr_ref, dwi_ref, dcr_ref, dci_ref, dar_ref, dai_ref, pw_r, pw_i, ac_r, ac_i, q_r, q_i):
        first = pl.program_id(1) == 0

        @pl.when(first)
        def _():
            pw_r[...] = jnp.ones_like(pw_r)
            pw_i[...] = jnp.zeros_like(pw_i)
            ac_r[...] = jnp.zeros_like(ac_r)
            ac_i[...] = jnp.zeros_like(ac_i)
            dwr_ref[...] = jnp.zeros_like(dwr_ref)
            dwi_ref[...] = jnp.zeros_like(dwi_ref)
            dcr_ref[...] = jnp.zeros_like(dcr_ref)
            dci_ref[...] = jnp.zeros_like(dci_ref)

        ar = jnp.broadcast_to(ar_ref[...], (NSEG, sb))
        ai = jnp.broadcast_to(ai_ref[...], (NSEG, sb))
        cr, ci = cr_ref[...], ci_ref[...]

        def fix(rows, qr, qi, spr, spi, accr, acci):
            qr, qi = ar * qr - ai * qi, ar * qi + ai * qr
            xr = lr_ref[rows, :] + qr * cr - qi * ci
            xi = li_ref[rows, :] + qr * ci + qi * cr
            q_r[rows, :] = xr
            q_i[rows, :] = xi
            return qr, qi, accr + xr * spr + xi * spi, acci + xi * spr - xr * spi

        def step(k, c):
            qr, qi, accr, acci = c
            kk = nt - 1 - k
            rows = pl.ds(pl.multiple_of(kk * NSEG, NSEG), NSEG)
            prev = pl.ds(pl.multiple_of((kk - 1) * NSEG, NSEG), NSEG)
            return fix(rows, qr, qi, sr_ref[prev, :], si_ref[prev, :], accr, acci)

        c = lax.fori_loop(0, nt - 1, step, (pw_r[...], pw_i[...], ac_r[...], ac_i[...]), unroll=7)
        qr, qi, accr, acci = fix(pl.ds(0, NSEG), *c[:2], br_ref[...], bi_ref[...], *c[2:])
        pw_r[...], pw_i[...], ac_r[...], ac_i[...] = qr, qi, accr, acci

        qrb, qib = q_r[...].astype(bf16), q_i[...].astype(bf16)
        nt_dims = (((1,), (1,)), ((), ()))
        tn_dims = (((0,), (0,)), ((), ()))
        du_ref[...] = (dua_ref[...] + lax.dot_general(qrb, wr_ref[...], nt_dims, preferred_element_type=f32)
                       + lax.dot_general(qib, wi_ref[...], nt_dims, preferred_element_type=f32))
        ub = u_ref[...].astype(bf16)
        dwr_ref[...] += lax.dot_general(ub, qrb, tn_dims, preferred_element_type=f32)
        dwi_ref[...] += lax.dot_general(ub, qib, tn_dims, preferred_element_type=f32)
        dyb = dy_ref[...].astype(bf16)
        dcr_ref[...] += lax.dot_general(sr_ref[...].astype(bf16), dyb, tn_dims, preferred_element_type=f32)
        dci_ref[...] -= lax.dot_general(si_ref[...].astype(bf16), dyb, tn_dims, preferred_element_type=f32)

        @pl.when(pl.program_id(1) == nT - 1)
        def _():
            dar_ref[...] = jnp.sum(accr, axis=0, keepdims=True)
            dai_ref[...] = jnp.sum(acci, axis=0, keepdims=True)

    blk = pl.BlockSpec((tc, sb), lambda j, i: (tmap(i), j))
    asp = pl.BlockSpec((1, sb), lambda j, i: (0, j))
    csp = pl.BlockSpec((NSEG, sb), lambda j, i: (0, j))
    bsp = pl.BlockSpec((None, NSEG, sb), lambda j, i: (tmap(i), 0, j))
    chn = pl.BlockSpec((tc, cb), lambda j, i: (tmap(i), j))
    wsp = pl.BlockSpec((None, cb, sb), lambda j, i: (j, 0, 0))
    wcs = pl.BlockSpec((None, sb, cb), lambda j, i: (j, 0, 0))
    return pl.pallas_call(
        body, name=name, grid=(nblk, nT), in_specs=[blk, blk, asp, asp, csp, csp, blk, blk, bsp, bsp, chn, chn, chn, wsp, wsp],
        out_specs=[chn, wsp, wsp, wcs, wcs, asp, asp],
        out_shape=[_S((T, nblk * cb), f32), _S((nblk, cb, sb), f32), _S((nblk, cb, sb), f32), _S((nblk, sb, cb), f32),
                   _S((nblk, sb, cb), f32), _S((1, NS), f32), _S((1, NS), f32)],
        scratch_shapes=[pltpu.VMEM((NSEG, sb), f32)] * 4 + [pltpu.VMEM((tc, sb), f32)] * 2,
        compiler_params=_cparams(("parallel", "arbitrary")),
    )(ql_r, ql_i, ab_r, ab_i, c_r, c_i, s_r, s_i, sb_r, sb_i, u, dy, du_add, w_r, w_i)


SSD2_TB = 512
_NN = (((1,), (0,)), ((), ()))
_NT = (((1,), (1,)), ((), ()))
_TN = (((0,), (0,)), ((), ()))


def _dotf(a, b, dims):
    return lax.dot_general(a.astype(bf16), b.astype(bf16), dims, preferred_element_type=f32)


def _doth(a, b, dims=_NN, sel="b", parts=3):
    x, m = (a, b) if sel == "b" else (b, a)
    m = m.astype(bf16)
    out = None
    for _ in range(parts):
        piece = x.astype(bf16)
        x = x - piece.astype(f32)
        d = lax.dot_general(*((piece, m) if sel == "b" else (m, piece)), dims, preferred_element_type=f32)
        out = d if out is None else out + d
    return out


def _ssd_consts(hpg):
    W = hpg * CHUNK
    i = lax.broadcasted_iota(jnp.int32, (CHUNK, CHUNK), 0)
    j = lax.broadcasted_iota(jnp.int32, (CHUNK, CHUNK), 1)
    tril = (i >= j).astype(f32)
    r = lax.broadcasted_iota(jnp.int32, (W, W), 0)
    c = lax.broadcasted_iota(jnp.int32, (W, W), 1)
    bd = (r // CHUNK == c // CHUNK).astype(f32)
    triu_bd = bd * (r <= c).astype(f32)
    e_r = lax.broadcasted_iota(jnp.int32, (W, LANES), 0)
    e_c = lax.broadcasted_iota(jnp.int32, (W, LANES), 1)
    ered = (e_r // HEADDIM == e_c).astype(f32)
    return tril, jnp.tile(tril, (1, hpg)), bd, triu_bd, ered


def _ssd2_specs(G, hpg, tb, tmap, b_off, c_off):
    W = hpg * HEADDIM
    ncb = tb // CHUNK
    xsp = pl.BlockSpec((tb, W), lambda g, i: (tmap(i), g))
    bsp = pl.BlockSpec((tb, SSD_STATE), lambda g, i: (tmap(i), b_off + g))
    csp = pl.BlockSpec((tb, SSD_STATE), lambda g, i: (tmap(i), c_off + g))
    rsp = pl.BlockSpec((None, ncb, W), lambda g, i: (g, tmap(i), 0))
    dsp = pl.BlockSpec((1, W), lambda g, i: (0, g))
    hsp = pl.BlockSpec((None, ncb, SSD_STATE, W), lambda g, i: (g, tmap(i), 0, 0))
    const = lambda a: pl.BlockSpec(a.shape, lambda g, i: (0, 0))
    return xsp, bsp, csp, rsp, dsp, hsp, const


def _tile_rows(a, n):
    return jnp.concatenate([a] * n, axis=0)


def _ssd2_fwd(xc, dt4, a4, dtw, aw, d4, consts, *, d_inner, name, carry=None):
    carry = carry or _NO_CARRY
    T = xc.shape[0]
    G, nc, W = dtw.shape
    hpg = W // CHUNK
    tb = min(SSD2_TB, T)
    nb, ncb = T // tb, tb // CHUNK
    b_off = d_inner // SSD_STATE
    xsp, bsp, csp, rsp, dsp, hsp, const = _ssd2_specs(G, hpg, tb, lambda i: i, b_off, b_off + G)
    tril, mask4, bd, triu_bd, _ = consts

    def body(*refs):
        own, c_in, c_out, c_sems = _carry_split(carry, refs, 12, 2)
        x_ref, b_ref, c_ref, dt_ref, a_ref, dtw_ref, aw_ref, d_ref, tril_ref, mask_ref, bd_ref, tbd_ref, y_ref, hs_ref, h_scr = own
        _carry_start(carry, c_in, c_out, c_sems, (pl.program_id(0) == 0) & (pl.program_id(1) == 0))

        @pl.when(pl.program_id(1) == 0)
        def _():
            h_scr[...] = jnp.zeros_like(h_scr)

        acs_rows = _doth(aw_ref[...], tbd_ref[...])
        ht = h_scr[...]
        for c in range(ncb):
            rows = slice(c * CHUNK, (c + 1) * CHUNK)
            x, bm, cm = x_ref[rows, :], b_ref[rows, :], c_ref[rows, :]
            acs = _doth(tril_ref[...], a_ref[rows, :], sel="a")
            lmat = jnp.where(mask_ref[...] > 0, jnp.exp(jnp.minimum(acs - acs_rows[c:c + 1, :], 0.0)), 0.0)
            m4 = _dotf(cm, _tile_rows(bm, hpg), _NT) * lmat * dtw_ref[c:c + 1, :]
            xbd = _tile_rows(x, hpg) * bd_ref[...]
            hs_ref[c] = ht
            y_ref[rows, :] = _dotf(m4, xbd, _NN) + _dotf(cm, ht, _NN) * jnp.exp(acs) + d_ref[...] * x
            a_last = acs[CHUNK - 1:CHUNK, :]
            xw = x * (jnp.exp(a_last - acs) * dt_ref[rows, :])
            ht = ht * jnp.exp(a_last) + _dotf(bm, xw, _TN)
        h_scr[...] = ht
        _carry_finish(carry, c_in, c_out, c_sems, (pl.program_id(0) == G - 1) & (pl.program_id(1) == nb - 1))

    res = pl.pallas_call(
        body, name=name, grid=(G, nb),
        in_specs=[xsp, bsp, csp, xsp, xsp, rsp, rsp, dsp, const(tril), const(mask4), const(bd), const(triu_bd)] + [_ANY] * len(carry.ins),
        out_specs=[xsp, hsp] + [_ANY] * len(carry.out_shapes),
        out_shape=[_S((T, G * W), f32), _S((G, nc, SSD_STATE, W), f32)] + list(carry.out_shapes),
        scratch_shapes=[pltpu.VMEM((SSD_STATE, W), f32)] + list(carry.sems),
        compiler_params=_cparams(("arbitrary", "arbitrary") if carry.ins else ("parallel", "arbitrary")),
    )(xc, xc, xc, dt4, a4, dtw, aw, d4, tril, mask4, bd, triu_bd, *carry.ins)
    return res[0], res[1], list(res[2:])


def _ssd2_bwd(xc, dt4, a4, dtw, aw, d4, consts, hs, dy, *, d_inner, name):
    T = xc.shape[0]
    G, nc, W = dtw.shape
    hpg = W // CHUNK
    tb = min(SSD2_TB, T)
    nb, ncb = T // tb, tb // CHUNK
    b_off = d_inner // SSD_STATE
    tmap = lambda i: nb - 1 - i
    xsp, bsp, csp, rsp, dsp, hsp, const = _ssd2_specs(G, hpg, tb, tmap, b_off, b_off + G)
    gsp = pl.BlockSpec((tb, SSD_STATE), lambda g, i: (tmap(i), g))
    ddsp = pl.BlockSpec((None, 1, LANES), lambda g, i: (g, 0, 0))
    tril, mask4, bd, triu_bd, ered = consts

    def body(x_ref, b_ref, c_ref, dt_ref, a_ref, dtw_ref, aw_ref, d_ref, tril_ref, mask_ref, bd_ref, tbd_ref, er_ref, hs_ref, dy_ref,
             dx_ref, db_ref, dc_ref, ddtc_ref, dac_ref, ddtw_ref, daw_ref, dd_ref, g_scr, dd_scr, rw_scr, tl_scr):
        first = pl.program_id(1) == 0

        @pl.when(first)
        def _():
            g_scr[...] = jnp.zeros_like(g_scr)
            dd_scr[...] = jnp.zeros_like(dd_scr)

        mask = mask_ref[...] > 0
        lane_in_block = lax.broadcasted_iota(jnp.int32, mask.shape, 1) & (CHUNK - 1)
        maskt = lax.broadcasted_iota(jnp.int32, mask.shape, 0) <= lane_in_block
        acs_rows = _doth(aw_ref[...], tbd_ref[...])
        dht = g_scr[...]
        dd = dd_scr[...]
        for c in range(ncb - 1, -1, -1):
            rows = slice(c * CHUNK, (c + 1) * CHUNK)
            x, bm, cm, dyc = x_ref[rows, :], b_ref[rows, :], c_ref[rows, :], dy_ref[rows, :]
            dtc, dtr = dt_ref[rows, :], dtw_ref[c:c + 1, :]
            ht = hs_ref[c]
            acs = _doth(tril_ref[...], a_ref[rows, :], sel="a")
            seg = acs - acs_rows[c:c + 1, :]
            lmat = jnp.where(mask, jnp.exp(jnp.minimum(seg, 0.0)), 0.0)
            lmat_t = jnp.where(maskt, jnp.exp(jnp.minimum(-seg, 0.0)), 0.0)
            btile, ctile = _tile_rows(bm, hpg), _tile_rows(cm, hpg)
            g4 = _dotf(cm, btile, _NT)
            gt4 = _dotf(bm, ctile, _NT)
            m4 = g4 * lmat * dtr
            mt4 = gt4 * lmat_t * dtc
            xbd = _tile_rows(x, hpg) * bd_ref[...]
            dybd = _tile_rows(dyc, hpg) * bd_ref[...]
            dm4 = _dotf(dyc, xbd, _NT)
            dmt4 = _dotf(x, dybd, _NT)
            dx = d_ref[...] * dyc + _dotf(mt4, dybd, _NN)
            dd = dd + jnp.sum(dyc * x, axis=0, keepdims=True)
            e4 = dm4 * m4
            dc = _dotf(dm4 * lmat * dtr, btile, _NN)
            db = _dotf(dmt4 * lmat_t * dtc, ctile, _NN)
            decay = jnp.exp(acs)
            yoff = _dotf(cm, ht, _NN) * decay
            dz = dyc * decay
            dc = dc + _dotf(dz, ht, _NT)
            dht_prev = _dotf(cm, dz, _TN)
            a_last = acs[CHUNK - 1:CHUNK, :]
            ea_last = jnp.exp(a_last)
            erel = jnp.exp(a_last - acs)
            dte = erel * dtc
            dxw = _dotf(bm, dht, _NN)
            db = db + _dotf(x * dte, dht, _NT)
            dx = dx + dxw * dte
            q4 = dxw * x
            dacs = e4 + dyc * yoff - q4 * dte
            col = jnp.concatenate([q4 * erel, _doth(tril_ref[...], dacs, _TN, sel="a")], axis=0)
            col = _doth(col, er_ref[...], parts=2)
            ddtc_ref[rows, :] = col[:CHUNK]
            dac_ref[rows, :] = col[CHUNK:]
            ddtw_ref[c:c + 1, :] = jnp.sum(dm4 * g4 * lmat, axis=0, keepdims=True)
            rw_scr[c:c + 1, :] = -jnp.sum(e4, axis=0, keepdims=True)
            tl_scr[c:c + 1, :] = jnp.sum(q4 * dte, axis=0, keepdims=True) + ea_last * jnp.sum(dht * ht, axis=0, keepdims=True)
            dx_ref[rows, :] = dx
            db_ref[rows, :] = db
            dc_ref[rows, :] = dc
            dht = dht_prev + dht * ea_last
        daw_ref[...] = _doth(rw_scr[...], tbd_ref[...], _NT) + _doth(tl_scr[...], bd_ref[...])
        g_scr[...] = dht
        dd_scr[...] = dd

        @pl.when(pl.program_id(1) == nb - 1)
        def _():
            dd_ref[...] = _doth(dd, er_ref[...])

    return pl.pallas_call(
        body, name=name, grid=(G, nb),
        in_specs=[xsp, bsp, csp, xsp, xsp, rsp, rsp, dsp, const(tril), const(mask4), const(bd), const(triu_bd), const(ered), hsp, xsp],
        out_specs=[xsp, gsp, gsp, gsp, gsp, rsp, rsp, ddsp],
        out_shape=[_S((T, G * W), f32), _S((T, G * SSD_STATE), f32), _S((T, G * SSD_STATE), f32), _S((T, G * LANES), f32),
                   _S((T, G * LANES), f32), _S(dtw.shape, f32), _S(dtw.shape, f32), _S((G, 1, LANES), f32)],
        scratch_shapes=[pltpu.VMEM((SSD_STATE, W), f32), pltpu.VMEM((1, W), f32), pltpu.VMEM((ncb, W), f32), pltpu.VMEM((ncb, W), f32)],
        compiler_params=_cparams(("parallel", "arbitrary")),
    )(xc, xc, xc, dt4, a4, dtw, aw, d4, tril, mask4, bd, triu_bd, ered, hs, dy)


def _peers():
    x, y, c = lax.axis_index("x"), lax.axis_index("y"), lax.axis_index("c")
    return x, y, c


_ANY = pl.BlockSpec(memory_space=pl.ANY)
N_CHIP = N_DEV // 2


def _all_gather(shards, *, name):
    n = len(shards)
    carry = _carry_gather(shards)

    def body(*refs):
        x_refs, out_refs, sems = refs[:n], refs[n:2 * n], refs[2 * n:]
        _gather_start(x_refs, out_refs, sems)
        _gather_finish(x_refs, out_refs, sems)

    return pl.pallas_call(
        body, name=name, out_shape=list(carry.out_shapes), in_specs=[_ANY] * n, out_specs=[_ANY] * n, scratch_shapes=list(carry.sems),
    )(*shards)


def _gather_parts(x_refs, out_refs, sems):
    send_sems, recv_sems, local_sems = sems
    x, y, c = _peers()
    me, sibling = (x, y, c), (x, y, 1 - c)
    chips = [(1 - x, y), (x, 1 - y), (1 - x, 1 - y)]
    n = len(x_refs)

    def copy(a, r, block, to, src=None):
        px, py, pc = block
        slot = out_refs[a].at[4 * px + 2 * py + pc]
        return pltpu.make_async_remote_copy(
            src_ref=slot if src is None else src, dst_ref=slot, send_sem=send_sems.at[7 * a + r],
            recv_sem=recv_sems.at[7 * a + r], device_id=to, device_id_type=MESH)

    mine = [pltpu.make_async_copy(x_refs[a], out_refs[a].at[4 * x + 2 * y + c], local_sems.at[a]) for a in range(n)]
    first = []
    for a in range(n):
        first.append(copy(a, 0, me, sibling, src=x_refs[a]))
        first += [copy(a, 1 + j, me, (*chip, c), src=x_refs[a]) for j, chip in enumerate(chips)]
    return copy, mine, first, me, sibling, chips, c, n


def _gather_start(x_refs, out_refs, sems):
    _, mine, first, *_ = _gather_parts(x_refs, out_refs, sems)
    for cp in mine + first:
        cp.start()


def _gather_finish(x_refs, out_refs, sems):
    copy, mine, first, me, sibling, chips, c, n = _gather_parts(x_refs, out_refs, sems)
    passed = []
    for j, chip in enumerate(chips):
        for a in range(n):
            copy(a, 1 + j, (*chip, c), me).wait_recv()
            fwd = copy(a, 4 + j, (*chip, c), sibling)
            fwd.start()
            passed.append(fwd)
    for a in range(n):
        copy(a, 0, sibling, me).wait_recv()
    for j, chip in enumerate(chips):
        for a in range(n):
            copy(a, 4 + j, (*chip, 1 - c), me).wait_recv()
    for cp in first + passed:
        cp.wait_send()
    for cp in mine:
        cp.wait()


def _exchange_sibling(slots, *, name):
    n = len(slots)

    def body(*refs):
        x_refs, sib_refs = refs[:n], refs[n:2 * n]
        send_sems, recv_sems = refs[2 * n:]
        x, y, c = _peers()
        give = [pltpu.make_async_remote_copy(
            src_ref=x_refs[a].at[pl.ds(N_CHIP * (1 - c), N_CHIP)], dst_ref=sib_refs[a], send_sem=send_sems.at[a],
            recv_sem=recv_sems.at[a], device_id=(x, y, 1 - c), device_id_type=MESH) for a in range(n)]
        for cp in give:
            cp.start()
        for cp in give:
            cp.wait_recv()
        for cp in give:
            cp.wait_send()

    return list(pl.pallas_call(
        body, name=name, out_shape=[_S((N_CHIP,) + s.shape[1:], s.dtype) for s in slots], in_specs=[_ANY] * n, out_specs=[_ANY] * n,
        scratch_shapes=[pltpu.SemaphoreType.DMA((n,)), pltpu.SemaphoreType.DMA((n,))],
    )(*slots))


def _chip_sum(slots, sib, core, *, name):
    _, R, W = slots.shape
    tr = _pick(R, max(16, (1 << 20) // (4 * W)), align=16)

    def body(core_ref, x_ref, s_ref, o_ref):
        o_ref[...] = (x_ref[...].astype(f32) + s_ref[...].astype(f32)).astype(o_ref.dtype)

    blk = pl.BlockSpec((None, tr, W), lambda t, i, core_ref: (t, i, 0))
    return pl.pallas_call(
        body, name=name, out_shape=_S(sib.shape, slots.dtype),
        grid_spec=pltpu.PrefetchScalarGridSpec(
            num_scalar_prefetch=1, grid=(N_CHIP, R // tr),
            in_specs=[pl.BlockSpec((None, tr, W), lambda t, i, core_ref: (N_CHIP * core_ref[0] + t, i, 0)), blk], out_specs=blk),
        compiler_params=_cparams(("parallel", "parallel")),
    )(core, slots, sib)


def _exchange_chips(parts, *, name):
    n = len(parts)

    def body(*refs):
        copies = _chip_copies(refs[:n], refs[n:2 * n], *refs[2 * n:])
        _start_all(copies)
        _wait_all(copies)

    return list(pl.pallas_call(
        body, name=name, out_shape=_chip_out_shapes(parts), in_specs=[_ANY] * n, out_specs=[_ANY] * n,
        scratch_shapes=_chip_sems(n),
    )(*parts))


def _chip_out_shapes(parts):
    return [_S((N_CHIP - 1,) + p.shape[1:], p.dtype) for p in parts]


def _chip_sems(n):
    return [pltpu.SemaphoreType.DMA((3 * n,)), pltpu.SemaphoreType.DMA((3 * n,))]


def _chip_copies(p_refs, out_refs, send_sems, recv_sems):
    x, y, c = _peers()
    copies = []
    for j in range(1, N_CHIP):
        tx, ty = x ^ (j >> 1), y ^ (j & 1)
        for a in range(len(p_refs)):
            copies.append(pltpu.make_async_remote_copy(
                src_ref=p_refs[a].at[2 * tx + ty], dst_ref=out_refs[a].at[j - 1], send_sem=send_sems.at[3 * a + j - 1],
                recv_sem=recv_sems.at[3 * a + j - 1], device_id=(tx, ty, c), device_id_type=MESH))
    return copies


def _start_all(copies):
    for cp in copies:
        cp.start()


def _wait_all(copies):
    for cp in copies:
        cp.wait_recv()
    for cp in copies:
        cp.wait_send()


class _Carry(NamedTuple):
    ins: tuple = ()
    out_shapes: tuple = ()
    sems: tuple = ()
    start: Callable = None
    finish: Callable = None


_NO_CARRY = _Carry()


def _carry_chips(parts):
    return _Carry(tuple(parts), tuple(_chip_out_shapes(parts)), tuple(_chip_sems(len(parts))),
                  lambda i, o, s: _start_all(_chip_copies(i, o, *s)), lambda i, o, s: _wait_all(_chip_copies(i, o, *s)))


def _carry_gather(shards):
    n = len(shards)
    sems = (pltpu.SemaphoreType.DMA((7 * n,)), pltpu.SemaphoreType.DMA((7 * n,)), pltpu.SemaphoreType.DMA((n,)))
    return _Carry(tuple(shards), tuple(_S((N_DEV,) + s.shape, s.dtype) for s in shards), sems, _gather_start, _gather_finish)


def _carry_split(carry, refs, n_in, n_out):
    ci, co, cs = len(carry.ins), len(carry.out_shapes), len(carry.sems)
    refs = list(refs)
    own_in, c_in = refs[:n_in], refs[n_in:n_in + ci]
    own_out, c_out = refs[n_in + ci:n_in + ci + n_out], refs[n_in + ci + n_out:n_in + ci + n_out + co]
    rest = refs[n_in + ci + n_out + co:]
    own_scratch, c_sems = rest[:len(rest) - cs], rest[len(rest) - cs:]
    return own_in + own_out + own_scratch, c_in, c_out, c_sems


def _carry_start(carry, c_in, c_out, c_sems, first):
    if carry.ins:
        @pl.when(first)
        def _():
            carry.start(c_in, c_out, c_sems)


def _carry_finish(carry, c_in, c_out, c_sems, last):
    if carry.ins:
        @pl.when(last)
        def _():
            carry.finish(c_in, c_out, c_sems)


def _sum_slots(stack, *, name):
    n, R, W = stack.shape
    tr = _pick(R, 1024, align=8)

    def body(s_ref, o_ref):
        acc = s_ref[0]
        for k in range(1, n):
            acc = acc + s_ref[k]
        o_ref[...] = acc

    return pl.pallas_call(
        body, name=name, grid=(R // tr,), in_specs=[pl.BlockSpec((n, tr, W), lambda i: (0, i, 0))],
        out_specs=pl.BlockSpec((tr, W), lambda i: (i, 0)), out_shape=_S((R, W), f32), compiler_params=_cparams(("parallel",)),
    )(stack)


def _adamw_math(gv, wv, mv, vv):
    c1 = 1.0 / (1.0 - ADAM_B1 ** ADAM_STEP)
    c2 = 1.0 / (1.0 - ADAM_B2 ** ADAM_STEP)
    nm = ADAM_B1 * mv + (1.0 - ADAM_B1) * gv
    nv = ADAM_B2 * vv + (1.0 - ADAM_B2) * jnp.square(gv)
    return -ADAM_LR * ((nm * c1) / (jnp.sqrt(nv * c2) + ADAM_EPS) + ADAM_WD * wv), nm, nv


def _adamw(g, w, m, v, *, name):
    R, W = w.shape
    tr = _pick(R, max(8, (1 << 20) // (4 * W)), align=8)

    def body(g_ref, w_ref, m_ref, v_ref, d_ref, nm_ref, nv_ref):
        d_ref[...], nm_ref[...], nv_ref[...] = _adamw_math(g_ref[...], w_ref[...], m_ref[...], v_ref[...])

    sp = pl.BlockSpec((tr, W), lambda i: (i, 0))
    return pl.pallas_call(
        body, name=name, grid=(R // tr,), in_specs=[sp] * 4, out_specs=[sp] * 3, out_shape=[_S((R, W), f32)] * 3,
        compiler_params=_cparams(("parallel",)),
    )(g, w, m, v)


def _reduce_adamw(own, arrived, chip, w, m, v, *, name):
    n, R, W = arrived.shape
    tr = _pick(R, max(16, (1 << 20) // (4 * W)), align=16)

    def body(chip_ref, o_ref, p_ref, w_ref, m_ref, v_ref, g_ref, d_ref, nm_ref, nv_ref):
        gv = o_ref[...].astype(f32)
        for k in range(n):
            gv = gv + p_ref[k].astype(f32)
        g_ref[...] = gv
        d_ref[...], nm_ref[...], nv_ref[...] = _adamw_math(gv, w_ref[...], m_ref[...], v_ref[...])

    sp = pl.BlockSpec((tr, W), lambda i, chip_ref: (i, 0))
    return pl.pallas_call(
        body, name=name, out_shape=[_S((R, W), f32)] * 4,
        grid_spec=pltpu.PrefetchScalarGridSpec(
            num_scalar_prefetch=1, grid=(R // tr,),
            in_specs=[pl.BlockSpec((None, tr, W), lambda i, chip_ref: (chip_ref[0], i, 0)),
                      pl.BlockSpec((n, tr, W), lambda i, chip_ref: (0, i, 0))] + [sp] * 3, out_specs=[sp] * 4),
        compiler_params=_cparams(("parallel",)),
    )(chip, own, arrived, w, m, v)


def _pieces(seg_start, seg_len, shard_w):
    out, col = [], seg_start
    while col < seg_start + seg_len:
        k, a = divmod(col, shard_w)
        n = min(shard_w - a, seg_start + seg_len - col)
        out.append((k, a, col - seg_start, n))
        col += n
    return out


def _unshard_w_in(g, seg_lens, *, name):
    _, D, w = g.shape
    starts = [sum(seg_lens[:i]) for i in range(len(seg_lens))]
    widths = [max(n, LANES) for n in seg_lens]
    tm = _pick(D, 256, align=16)

    def body(g_ref, *o_refs):
        for o_ref, s0, n in zip(o_refs, starts, seg_lens):
            if n < o_ref.shape[1]:
                o_ref[...] = jnp.zeros_like(o_ref)
            for k, a, off, m in _pieces(s0, n, w):
                o_ref[:, off:off + m] = g_ref[k, :, a:a + m]

    return pl.pallas_call(
        body, name=name, grid=(D // tm,), in_specs=[pl.BlockSpec((N_DEV, tm, w), lambda i: (0, i, 0))],
        out_specs=[pl.BlockSpec((tm, wd), lambda i: (i, 0)) for wd in widths], out_shape=[_S((D, wd), g.dtype) for wd in widths],
        compiler_params=_cparams(("parallel",)),
    )(g)


def _unshard_pair(g1, g2, *, name):
    _, D, w = g1.shape
    tm = _pick(D, 256, align=16)

    def body(a_ref, b_ref, o_ref):
        for i, g_ref in enumerate((a_ref, b_ref)):
            for k in range(N_DEV):
                off = (i * N_DEV + k) * w
                o_ref[:, off:off + w] = g_ref[k]

    blk = pl.BlockSpec((N_DEV, tm, w), lambda i: (0, i, 0))
    return pl.pallas_call(
        body, name=name, grid=(D // tm,), in_specs=[blk, blk], out_specs=pl.BlockSpec((tm, 2 * N_DEV * w), lambda i: (i, 0)),
        out_shape=_S((D, 2 * N_DEV * w), g1.dtype), compiler_params=_cparams(("parallel",)),
    )(g1, g2)


def _reshard_pair(dw, *, name):
    D, w = dw.shape[0], dw.shape[1] // (2 * N_DEV)
    tm = _pick(D, 128, align=16)

    def body(g_ref, a_ref, b_ref):
        for i, o_ref in enumerate((a_ref, b_ref)):
            for k in range(N_DEV):
                off = (i * N_DEV + k) * w
                o_ref[_slot_of(k)] = g_ref[:, off:off + w].astype(o_ref.dtype)

    blk = pl.BlockSpec((N_DEV, tm, w), lambda i: (0, i, 0))
    return pl.pallas_call(
        body, name=name, grid=(D // tm,), in_specs=[pl.BlockSpec((tm, dw.shape[1]), lambda i: (i, 0))], out_specs=[blk, blk],
        out_shape=[_S((N_DEV, D, w), bf16)] * 2, compiler_params=_cparams(("parallel",)),
    )(dw)


def _reshard_w_in(grads, seg_lens, w, *, name):
    D = grads[0].shape[0]
    starts = [sum(seg_lens[:i]) for i in range(len(seg_lens))]
    tm = _pick(D, 128, align=16)

    def body(*refs):
        o_ref = refs[-1]
        for g_ref, s0, n in zip(refs[:-1], starts, seg_lens):
            for k, a, off, m in _pieces(s0, n, w):
                o_ref[_slot_of(k), :, a:a + m] = g_ref[:, off:off + m].astype(o_ref.dtype)

    return pl.pallas_call(
        body, name=name, grid=(D // tm,), in_specs=[pl.BlockSpec((tm, g.shape[1]), lambda i: (i, 0)) for g in grads],
        out_specs=pl.BlockSpec((N_DEV, tm, w), lambda i: (0, i, 0)), out_shape=_S((N_DEV, D, w), bf16),
        compiler_params=_cparams(("parallel",)),
    )(*grads)


def _pad_flat(a, mult):
    a = a.reshape(-1)
    n = -(-a.shape[0] // mult) * mult
    return a if n == a.shape[0] else jnp.pad(a, (0, n - a.shape[0]))


def _pad_cols(a, mult):
    n = -(-a.shape[1] // mult) * mult
    return a if n == a.shape[1] else jnp.pad(a, ((0, 0), (0, n - a.shape[1])))


def _block_diag(t):
    nblk, g, P, Q = t.shape
    eye = jnp.eye(g, dtype=t.dtype)
    return (t[:, :, :, None, :] * eye[None, :, None, :, None]).reshape(nblk, g * P, g * Q)


def _block_diag_t(w, P, Q):
    nblk = w.shape[0]
    g = w.shape[1] // P
    eye = jnp.eye(g, dtype=w.dtype)
    return (w.reshape(nblk, g, P, g, Q) * eye[None, :, None, :, None]).sum(axis=3)


_COLS = ("ffn1_w_gate", "ffn1_w_up", "ffn2_w_gate", "ffn2_w_up")
_ROWS = ("ffn1_w_down", "ffn2_w_down", "s5_w_glu", "w_proj_s5", "w_out", "w_proj_ssd")
_BIG = _COLS + _ROWS + ("w_in", "conv_w")
_SMALL = ("ffn1_norm", "mix_norm", "conv_b", "s5_A_re", "s5_A_im", "s5_log_dt", "s5_B_re", "s5_B_im", "s5_C_re", "s5_C_im",
          "s5_D", "s5_b_glu", "ssd_A_log", "ssd_dt_bias", "ssd_D", "ssd_norm", "b_gate", "ffn2_norm", "final_norm")
_WEIGHTS = ("ffn1_norm", "ffn1_w_gate", "ffn1_w_up", "ffn1_w_down", "mix_norm", "w_in", "conv_w", "conv_b", "s5_A_re", "s5_A_im",
            "s5_log_dt", "s5_B_re", "s5_B_im", "s5_C_re", "s5_C_im", "s5_D", "s5_w_glu", "s5_b_glu", "ssd_A_log", "ssd_dt_bias",
            "ssd_D", "ssd_norm", "w_proj_s5", "w_proj_ssd", "b_gate", "w_out", "ffn2_norm", "ffn2_w_gate", "ffn2_w_up",
            "ffn2_w_down", "final_norm")
def _with_carry(res, carry):
    return res if carry else (res, [])


def _ffn_fwd(x, n, wgu, wd_of, tag, carries=(None, None, None)):
    D = x.shape[1]
    h = _rows(_f_rmsnorm, [x], [n], [(D, bf16)], name=tag + "_norm")[0]
    ab, got0 = _with_carry(_mm(h, wgu, carry=carries[0], name=tag + "_gate_up"), carries[0])
    wd = wd_of(got0)
    F = wd.shape[0]
    (c,), got1 = _with_carry(_rows(_f_swiglu, [ab], [], [(F, bf16)], carry=carries[1], name=tag + "_act"), carries[1])
    y, got2 = _with_carry(_mm(c, wd, scale=0.5, add=x, carry=carries[2], name=tag + "_down"), carries[2])
    return y, (x, n, h, ab, c), (got0, got1, got2)


def _ffn_bwd(saved, wgu, wd, dy, tag, carry_after_dwd=None, carry_after_dwgu=None):
    x, n, h, ab, c = saved
    F = wd.shape[0]
    def act_bwd(dc, ab_):
        return jax.vjp(lambda t: _f_swiglu(t)[0], ab_)[1](dc)[0]

    dab = _mm(dy, wd, tb=True, scale=0.5, tm=EPI_TM, epi=(act_bwd, ab, 2 * F), out_dtype=bf16, name=tag + "_d_act")
    dwd = _mm(c, dy, ta=True, o_blk="m", o_slots=True, tm=F // 2, out_dtype=bf16, scale=0.5, name=tag + "_d_wdown")
    carry_w = carry_after_dwd(dwd) if carry_after_dwd else None
    dwgu, arr_w = _with_carry(_mm(h, dab, ta=True, carry=carry_w, name=tag + "_d_wgu"), carry_w)
    dwg, dwu = _reshard_pair(dwgu, name=tag + "_reshard_d_wgu")
    carry_h = carry_after_dwgu(dwg, dwu) if carry_after_dwgu else None
    dh, arr_h = _with_carry(_mm(dab, wgu, tb=True, carry=carry_h, name=tag + "_d_h"), carry_h)
    (dx,), (dn,) = _rows_bwd(_f_rmsnorm, [x], [n], [dh], name=tag + "_norm_bwd", want_rows=[0], adds={0: dy})
    return dx, dn, dwg, dwu, dwd, arr_w, arr_h


def kernel(x, ffn1_norm, ffn1_w_gate, ffn1_w_up, ffn1_w_down, mix_norm, w_in, conv_w, conv_b, s5_A_re, s5_A_im, s5_log_dt, s5_B_re, s5_B_im, s5_C_re, s5_C_im, s5_D, s5_w_glu, s5_b_glu, ssd_A_log, ssd_dt_bias, ssd_D, ssd_norm, w_proj_s5, w_proj_ssd, b_gate, w_out, ffn2_norm, ffn2_w_gate, ffn2_w_up, ffn2_w_down, final_norm, loss_target, m_ffn1_norm, m_ffn1_w_gate, m_ffn1_w_up, m_ffn1_w_down, m_mix_norm, m_w_in, m_conv_w, m_conv_b, m_s5_A_re, m_s5_A_im, m_s5_log_dt, m_s5_B_re, m_s5_B_im, m_s5_C_re, m_s5_C_im, m_s5_D, m_s5_w_glu, m_s5_b_glu, m_ssd_A_log, m_ssd_dt_bias, m_ssd_D, m_ssd_norm, m_w_proj_s5, m_w_proj_ssd, m_b_gate, m_w_out, m_ffn2_norm, m_ffn2_w_gate, m_ffn2_w_up, m_ffn2_w_down, m_final_norm, v_ffn1_norm, v_ffn1_w_gate, v_ffn1_w_up, v_ffn1_w_down, v_mix_norm, v_w_in, v_conv_w, v_conv_b, v_s5_A_re, v_s5_A_im, v_s5_log_dt, v_s5_B_re, v_s5_B_im, v_s5_C_re, v_s5_C_im, v_s5_D, v_s5_w_glu, v_s5_b_glu, v_ssd_A_log, v_ssd_dt_bias, v_ssd_D, v_ssd_norm, v_w_proj_s5, v_w_proj_ssd, v_b_gate, v_w_out, v_ffn2_norm, v_ffn2_w_gate, v_ffn2_w_up, v_ffn2_w_down, v_final_norm):
    P = dict(locals())
    T, D = x.shape[1], x.shape[2]
    x0, tgt = x[0], loss_target[0]
    sh = {k: P[k][0] for k in _BIG}

    send = {k: (sh[k] if k == "conv_w" else sh[k].astype(bf16)) for k in _BIG}
    W = {}

    def gather_in(keys):
        return _carry_gather([send[k] for k in keys])

    first_keys = ("ffn1_w_gate", "ffn1_w_up", "conv_w")
    W.update(zip(first_keys, _all_gather([send[k] for k in first_keys], name="gather_weights_first")))
    whole = lambda k: W[k].reshape(-1, D)
    conv_w_full = W["conv_w"].transpose(1, 0, 2).reshape(CONV_K, -1)

    d_inner = N_DEV * sh["w_proj_ssd"].shape[0]
    conv_dim = conv_w_full.shape[1]
    H = ssd_A_log.shape[1]
    G = (conv_dim - d_inner) // (2 * SSD_STATE)
    hpg = H // G
    nc = T // CHUNK
    Gs = D // S5_GROUP
    nblk = Gs // S5_GPB
    NS = Gs * S5_STATE
    seg_lens = (D, d_inner, conv_dim, H, 2 * D)

    cuts = [0, D // 3 // 16 * 16, D // 3 // 16 * 16 + 3 * D // 8 // 16 * 16, D]
    win_rows = [send["w_in"][a_:b_] for a_, b_ in zip(cuts[:-1], cuts[1:])]
    wgu1 = _unshard_pair(W["ffn1_w_gate"], W["ffn1_w_up"], name="unshard_ffn1_gate_up")

    def ffn1_down(got):
        W["ffn1_w_down"] = got[0]
        return whole("ffn1_w_down")

    x1, sv1, (got0, got1, got2) = _ffn_fwd(
        x0, ffn1_norm, wgu1, ffn1_down, "ffn1",
        carries=(_carry_gather([send["ffn1_w_down"], win_rows[0]]), _carry_gather([win_rows[1]]), _carry_gather([win_rows[2]])))
    ffn1_w = (wgu1, whole("ffn1_w_down"))
    W["w_in"] = jnp.concatenate([got0[1], got1[0], got2[0]], axis=1)
    w_u, w_z, w_xbc, w_dt, w_gl = _unshard_w_in(W["w_in"], seg_lens, name="unshard_w_in")
    h2 = _rows(_f_rmsnorm, [x1], [mix_norm], [(D, bf16)], name="mix_norm")[0]
    u_p = _mm(h2, w_u, o_seg=True, name="in_u")
    z = _mm(h2, w_z, name="in_z")
    xbc = _mm(h2, w_xbc, name="in_xbc")
    gl = _mm(h2, w_gl, name="in_gate")
    dtr = _mm(h2, w_dt, name="in_dt")

    rep = lambda a: jnp.repeat(a, S5_GROUP, axis=0)
    lr, li, ldt = s5_A_re[0], s5_A_im[0], s5_log_dt[0].reshape(Gs, 1)
    brt = s5_B_re[0].transpose(0, 2, 1).reshape(Gs * S5_GROUP, S5_STATE)
    bit = s5_B_im[0].transpose(0, 2, 1).reshape(Gs * S5_GROUP, S5_STATE)
    prep_args = (lr, li, ldt, rep(lr), rep(li), rep(ldt), brt, bit)
    ar, ai, bbrt, bbit = _s5_prep(prep_args, name="s5_prep")
    a_r, a_i = ar.reshape(1, NS), ai.reshape(1, NS)
    wb_r = _block_diag(bbrt.reshape(nblk, S5_GPB, S5_GROUP, S5_STATE)).astype(bf16)
    wb_i = _block_diag(bbit.reshape(nblk, S5_GPB, S5_GROUP, S5_STATE)).astype(bf16)
    c4r = s5_C_re[0].reshape(nblk, S5_GPB, S5_GROUP, S5_STATE).transpose(0, 1, 3, 2)
    c4i = s5_C_im[0].reshape(nblk, S5_GPB, S5_GROUP, S5_STATE).transpose(0, 1, 3, 2)
    wc_r, wc_i = _block_diag(c4r).astype(bf16), _block_diag(c4i).astype(bf16)
    mix_keys = ("s5_w_glu", "w_proj_s5", "w_proj_ssd", "w_out")
    sl_r, sl_i, p_r, p_i, got = _s5_local_scan(u_p, wb_r, wb_i, a_r, a_i, reverse=False, carry=gather_in(mix_keys), name="s5_scan")
    W.update(zip(mix_keys, got))
    w_glu, w_p5, w_pssd, w_o = whole("s5_w_glu"), whole("w_proj_s5"), whole("w_proj_ssd"), whole("w_out")
    c_r, c_i = _s5_carry(sl_r[T - NSEG:], sl_i[T - NSEG:], p_r, p_i, reverse=False, name="s5_carry")
    s_r, s_i, ylin = _s5_fix_out(sl_r, sl_i, a_r, a_i, c_r, c_i, wc_r, wc_i, name="s5_fix_out")
    g5 = _rows(_f_s5_post, [ylin, u_p], [s5_D], [(D, f32)], name="s5_gelu")[0]
    v5 = _mm(g5, w_glu, name="s5_glu_mm")
    o5 = _rows(_f_glu, [g5, v5], [s5_b_glu], [(D, bf16)], name="s5_glu")[0]
    p5 = _mm(o5, w_p5, a_seg=True, name="proj_s5")

    xc = _conv_fwd(xbc, conv_w_full, conv_b, name="conv")
    bias_p, alog_p = _pad_cols(ssd_dt_bias, LANES), _pad_cols(ssd_A_log, LANES)
    expand = (lax.broadcasted_iota(jnp.int32, (LANES, d_inner), 1) // HEADDIM
              == lax.broadcasted_iota(jnp.int32, (LANES, d_inner), 0)).astype(f32)
    dt_p, da_p, dt4, a4 = _rows(_f_dt_expand, [dtr], [bias_p, alog_p, expand],
                                [(LANES, f32), (LANES, f32), (d_inner, f32), (d_inner, f32)], name="ssd_dt")
    row_l = lambda a: a[:, :H].reshape(nc, CHUNK, G, hpg).transpose(2, 0, 3, 1).reshape(G, nc, hpg * CHUNK)
    ssd_in = (xc, dt4, a4, row_l(dt_p), row_l(da_p), jnp.repeat(ssd_D, HEADDIM, axis=1), _ssd_consts(hpg))
    ffn2_keys = ("ffn2_w_gate", "ffn2_w_up", "ffn2_w_down")
    y_ssd, hs, got = _ssd2_fwd(*ssd_in, d_inner=d_inner, carry=gather_in(ffn2_keys), name="ssd")
    W.update(zip(ffn2_keys, got))
    ffn2_w = (_unshard_pair(W["ffn2_w_gate"], W["ffn2_w_up"], name="unshard_ffn2_gate_up"), whole("ffn2_w_down"))
    yn = _rows(_f_gated_norm, [y_ssd, z], [ssd_norm], [(d_inner, bf16)], name="ssd_gated_norm")[0]
    pssd = _mm(yn, w_pssd, name="proj_ssd")

    merged = _rows(_f_merge, [gl, p5, pssd], [b_gate], [(D, bf16)], name="merge")[0]
    x2 = _mm(merged, w_o, add=x1, name="out_proj")
    x3, sv2, _ = _ffn_fwd(x2, ffn2_norm, ffn2_w[0], lambda _: ffn2_w[1], "ffn2")
    lossv, dx3, d_final = _loss_stage(x3, tgt, final_norm.reshape(1, D), name="loss")

    gw = {}
    gs = {"final_norm": d_final}
    slot_mm = lambda a_, b_, name, **kw: _mm(a_, b_, ta=True, o_blk="m", o_slots=True, out_dtype=bf16, name=name, **kw)
    core = lax.axis_index("c").astype(jnp.int32).reshape(1)
    chip = (2 * lax.axis_index("x") + lax.axis_index("y")).astype(jnp.int32).reshape(1)
    chip_sums, arrived = {}, {}

    def level1(keys, tag):
        sib = _exchange_sibling([gw[k] for k in keys], name="exchange_sibling_" + tag)
        for k, s_ in zip(keys, sib):
            chip_sums[k] = _chip_sum(gw[k], s_, core, name="chip_sum_" + k)
        return [chip_sums[k] for k in keys]

    dx2, gs["ffn2_norm"], gw["ffn2_w_gate"], gw["ffn2_w_up"], gw["ffn2_w_down"], _, _ = _ffn_bwd(sv2, *ffn2_w, dx3, "ffn2")

    dmerged = _mm(dx2, w_o, tb=True, name="d_merged")
    gw["w_out"] = slot_mm(merged, dx2, "d_w_out")
    (dgl, dp5, dpssd), (gs["b_gate"],) = _rows_bwd(_f_merge, [gl, p5, pssd], [b_gate], [dmerged], name="merge_bwd", want_rows=[0, 1, 2])

    dyn = _mm(dpssd, w_pssd, tb=True, name="d_yn")
    gw["w_proj_ssd"] = slot_mm(yn, dpssd, "d_w_proj_ssd")
    group_a = ("ffn2_w_gate", "ffn2_w_up", "ffn2_w_down", "w_out", "w_proj_ssd")
    parts_a = level1(group_a, "a")
    (dyssd, dz), (gs["ssd_norm"],) = _rows_bwd(_f_gated_norm, [y_ssd, z], [ssd_norm], [dyn], name="ssd_gated_norm_bwd", want_rows=[0, 1])
    dxs, dbm, dcm, ddtc, ddac, ddtw, ddaw, ddh = _ssd2_bwd(*ssd_in, hs, dyssd, d_inner=d_inner, name="ssd_bwd")

    def fold(col, row):
        col = col.reshape(T, G, LANES)[:, :, :hpg].reshape(T, H)
        row = row.reshape(G, nc, hpg, CHUNK).transpose(1, 3, 0, 2).reshape(T, H)
        return _pad_cols(col + row, LANES)

    (ddtr,), (dbias_p, dalog_p) = _rows_bwd(_f_dt, [dtr], [bias_p, alog_p], [fold(ddtc, ddtw), fold(ddac, ddaw)], name="ssd_dt_bwd", want_rows=[0])
    gs["ssd_dt_bias"], gs["ssd_A_log"], gs["ssd_D"] = dbias_p[:, :H], dalog_p[:, :H], ddh[:, 0, :hpg].reshape(1, H)
    dxbc, d_conv_w, gs["conv_b"], arr = _conv_bwd(
        xbc, conv_w_full, conv_b, [dxs, dbm, dcm], carry=_carry_chips(parts_a), name="conv_bwd")
    arrived.update(zip(group_a, arr))
    cwk = sh["conv_w"].shape[1]
    gw["conv_w"] = d_conv_w.reshape(CONV_K, N_CHIP, 2, cwk).transpose(2, 1, 0, 3).reshape(N_DEV, CONV_K, cwk)

    do5 = _mm(dp5, w_p5, tb=True, o_seg=True, name="d_o5")
    gw["w_proj_s5"] = slot_mm(o5, dp5, "d_w_proj_s5", a_seg=True)
    (dg5a, dv5), (gs["s5_b_glu"],) = _rows_bwd(_f_glu, [g5, v5], [s5_b_glu], [do5], name="s5_glu_bwd", want_rows=[0, 1])
    dg5 = _mm(dv5, w_glu, tb=True, add=dg5a, name="d_g5")
    gw["s5_w_glu"] = slot_mm(g5, dv5, "d_w_glu")
    (dylin, du_a), (gs["s5_D"],) = _rows_bwd(_f_s5_post, [ylin, u_p], [s5_D], [dg5], name="s5_gelu_bwd", want_rows=[0, 1])
    wct_r, wct_i = wc_r.transpose(0, 2, 1), -wc_i.transpose(0, 2, 1)
    ql_r, ql_i, pb_r, pb_i = _s5_local_scan(dylin, wct_r, wct_i, a_r, -a_i, reverse=True, name="s5_scan_bwd")
    cb_r, cb_i = _s5_carry(ql_r[:NSEG], ql_i[:NSEG], pb_r, pb_i, reverse=True, name="s5_carry_bwd")
    tc = min(S5_TC, T)

    def before_blocks(s):
        last = s.reshape(T // tc, tc, NS)[:, tc - NSEG:, :]
        wrap = jnp.concatenate([jnp.zeros((1, 1, NS), f32), last[-1:, : NSEG - 1, :]], axis=1)
        return jnp.concatenate([wrap, last[:-1]], axis=0)

    du_p, dwb_r, dwb_i, dwc_r, dwc_i, d_ar, d_ai = _s5_fix_bwd(
        ql_r, ql_i, a_r, -a_i, cb_r, cb_i, s_r, s_i, before_blocks(s_r), before_blocks(s_i), u_p, dylin, du_a, wb_r, wb_i, name="s5_fix_bwd")
    unblk = lambda w: _block_diag_t(w, S5_GROUP, S5_STATE).reshape(Gs * S5_GROUP, S5_STATE)
    rsum = jnp.repeat(jnp.eye(Gs, dtype=f32), S5_GROUP, axis=1)
    d_lr, d_li, d_ldt, d_brt, d_bit = _s5_prep_bwd(
        prep_args, (d_ar.reshape(Gs, S5_STATE), d_ai.reshape(Gs, S5_STATE), unblk(dwb_r), unblk(dwb_i)), rsum, name="s5_prep_bwd")
    gs["s5_A_re"], gs["s5_A_im"], gs["s5_log_dt"] = d_lr, d_li, d_ldt.reshape(1, Gs)
    gs["s5_B_re"] = d_brt.reshape(Gs, S5_GROUP, S5_STATE).transpose(0, 2, 1)
    gs["s5_B_im"] = d_bit.reshape(Gs, S5_GROUP, S5_STATE).transpose(0, 2, 1)
    gs["s5_C_re"] = _block_diag_t(dwc_r, S5_STATE, S5_GROUP).transpose(0, 1, 3, 2).reshape(Gs, S5_GROUP, S5_STATE)
    gs["s5_C_im"] = _block_diag_t(dwc_i, S5_STATE, S5_GROUP).transpose(0, 1, 3, 2).reshape(Gs, S5_GROUP, S5_STATE)

    d_w_in = [_mm(h2, du_p, ta=True, b_seg=True, name="d_w_u"), _mm(h2, dz, ta=True, name="d_w_z"), _mm(h2, dxbc, ta=True, name="d_w_xbc"),
              _mm(h2, ddtr, ta=True, name="d_w_dt"), _mm(h2, dgl, ta=True, name="d_w_gate")]
    gw["w_in"] = _reshard_w_in(d_w_in, seg_lens, sh["w_in"].shape[1], name="reshard_d_w_in")
    group_c = ("w_proj_s5", "s5_w_glu", "conv_w")
    parts_c = level1(group_c + ("w_in",), "c")
    c1 = int(D * 0.45) // 16 * 16
    c2 = c1 + D // 4 // 16 * 16
    win = [parts_c[3][:, :c1], parts_c[3][:, c1:c2], parts_c[3][:, c2:]]
    dh2 = _mm(du_p, w_u, tb=True, a_seg=True, name="d_h2_u")
    dh2, arr = _mm(dz, w_z, tb=True, add=dh2, carry=_carry_chips(parts_c[:3]), name="d_h2_z")
    arrived.update(zip(group_c, arr))
    dh2, (arr0,) = _mm(dxbc, w_xbc, tb=True, add=dh2, carry=_carry_chips([win[0]]), name="d_h2_xbc")
    dh2, (arr1,) = _mm(dgl, w_gl, tb=True, add=dh2, carry=_carry_chips([win[1]]), name="d_h2_gate")
    dh2 = _mm(ddtr, w_dt, tb=True, add=dh2, name="d_h2_dt")
    (dx1,), (gs["mix_norm"],) = _rows_bwd(_f_rmsnorm, [x1], [mix_norm], [dh2], name="mix_norm_bwd", want_rows=[0], adds={0: dx2})

    def carry_ffn1_down(dwd):
        gw["ffn1_w_down"] = dwd
        return _carry_chips(level1(("ffn1_w_down",), "d") + [win[2]])

    def carry_ffn1_gate_up(dwg, dwu):
        gw["ffn1_w_gate"], gw["ffn1_w_up"] = dwg, dwu
        return _carry_chips(level1(("ffn1_w_gate", "ffn1_w_up"), "e"))

    dx0, gs["ffn1_norm"], _, _, _, arr_w, arr_h = _ffn_bwd(
        sv1, *ffn1_w, dx1, "ffn1", carry_after_dwd=carry_ffn1_down, carry_after_dwgu=carry_ffn1_gate_up)
    arrived["ffn1_w_down"], arr2 = arr_w
    arrived["ffn1_w_gate"], arrived["ffn1_w_up"] = arr_h
    arrived["w_in"] = jnp.concatenate([arr0, arr1, arr2], axis=1)

    small_shapes = {k: (P[k][0].shape if P[k].ndim > 1 else P[k].shape) for k in _SMALL}
    pack = lambda d: jnp.concatenate([_pad_flat(d[k], TILE_ELEMS) for k in _SMALL]).reshape(-1, LANES)
    gsmall = _sum_slots(_all_gather([pack(gs)], name="gather_small_grads")[0], name="sum_small_grads")
    snum = {k: math.prod(small_shapes[k]) for k in _SMALL}
    ssz = {k: -(-snum[k] // TILE_ELEMS) * TILE_ELEMS for k in _SMALL}

    grads, delta, new_m, new_v = {}, {}, {}, {}
    for k in _BIG:
        grads[k], delta[k], new_m[k], new_v[k] = _reduce_adamw(
            chip_sums[k], arrived[k], chip, P[k][0], P["m_" + k][0], P["v_" + k][0], name="adamw_" + k)
    d_s, m_s, v_s = _adamw(gsmall, pack({k: P[k] for k in _SMALL}), pack({k: P["m_" + k] for k in _SMALL}),
                           pack({k: P["v_" + k] for k in _SMALL}), name="adamw_small")
    off = 0
    gflat, dflat, mflat, vflat = gsmall.reshape(-1), d_s.reshape(-1), m_s.reshape(-1), v_s.reshape(-1)
    for k in _SMALL:
        n = snum[k]
        grads[k], delta[k], new_m[k], new_v[k] = (a[off:off + n] for a in (gflat, dflat, mflat, vflat))
        off += ssz[k]

    loss = lax.psum(lossv[0, 0], ("x", "y", "c"))
    out = [loss, dx0.reshape(x.shape)]
    for d in (grads, delta, new_m, new_v):
        out += [d[k].reshape(P[k].shape) for k in _WEIGHTS]
    return tuple(out)
```

```python
import math
from typing import Callable, NamedTuple

import jax
import jax.numpy as jnp
from jax import lax
from jax.experimental import pallas as pl
from jax.experimental.pallas import tpu as pltpu

f32 = jnp.float32
bf16 = jnp.bfloat16
_S = jax.ShapeDtypeStruct

EPS = 1e-6
S5_GROUP = 16
S5_STATE = 64
HEADDIM = 64
SSD_STATE = 128
CHUNK = 64
CONV_K = 4
NSEG = 8
S5_GPB = 16
N_DEV = 8
LANES = 128
TILE_ELEMS = 8 * LANES

ADAM_LR = 0.001
ADAM_B1 = 0.9
ADAM_B2 = 0.999
ADAM_EPS = 1e-08
ADAM_WD = 0.01
ADAM_STEP = 10

VMEM_LIMIT = 56 * 1024 * 1024
MM_FULL_K = 3072
MM_MAX_TN = 3072
EPI_TM = 128
MESH = pl.DeviceIdType.MESH


def _cparams(sem=None):
    return pltpu.CompilerParams(dimension_semantics=sem, vmem_limit_bytes=VMEM_LIMIT)


def _pick(dim, pref, align=LANES):
    best = None
    t = align
    while t <= min(dim, pref):
        if dim % t == 0:
            best = t
        t += align
    return best or dim


def _slot_of(k):
    return (k & 1) * (N_DEV // 2) + (k >> 1)


def _mm(a, b, *, name, ta=False, tb=False, a_blk=None, b_blk=None, o_blk=None, o_slots=False, a_seg=False, b_seg=False,
        o_seg=False, tm=None, out_dtype=f32, scale=1.0, add=None, epi=None, carry=None):
    a2, b2 = a.shape[-2:], b.shape[-2:]
    Ma, Ka = (a2[1], a2[0]) if ta else a2
    Kb, Nb = (b2[1], b2[0]) if tb else b2
    M = Ma * (a.shape[0] if a_blk == "m" else 1)
    K = Ka * (a.shape[0] if a_blk == "k" else 1)
    N = Nb * (b.shape[0] if b_blk == "n" else 1)
    assert K == Kb * (b.shape[0] if b_blk == "k" else 1), (a.shape, b.shape, ta, tb, a_blk, b_blk)
    assert (a.ndim == 3) == (a_blk is not None) and (b.ndim == 3) == (b_blk is not None)
    tm = Ma if a_blk == "m" else (tm or _pick(M, 512))
    tn = Nb if b_blk == "n" else _pick(N, MM_MAX_TN)
    if a_blk == "k" or b_blk == "k":
        tk = Ka if a_blk == "k" else Kb
        assert tk == (Kb if b_blk == "k" else tk)
    else:
        tk = K if K <= MM_FULL_K else _pick(K, 1024 if ta else MM_FULL_K)
    if (a_seg and not ta) or o_seg:
        tm = M // NSEG
    if (a_seg and ta) or b_seg:
        tk = K // NSEG
    gm, gn, nk = M // tm, N // tn, K // tk
    assert not (add is not None and (o_seg or o_blk)) and not (o_blk and o_seg)

    if a_seg:
        assert a.ndim == 2
        a = a.reshape(a.shape[0] // NSEG, NSEG * a.shape[1])
        if ta:
            a_spec = pl.BlockSpec((tk, tm), lambda i, j, k: (0, k * (Ma // tm) + i))
        else:
            a_spec = pl.BlockSpec((tm, tk), lambda i, j, k: (0, i * (Ka // tk) + k))
    elif a.ndim == 3:
        lead = (lambda i, k: i) if a_blk == "m" else (lambda i, k: k)
        if ta:
            a_spec = pl.BlockSpec((None, tk, tm), lambda i, j, k: (lead(i, k), 0 if a_blk == "k" else k, 0 if a_blk == "m" else i))
        else:
            a_spec = pl.BlockSpec((None, tm, tk), lambda i, j, k: (lead(i, k), 0 if a_blk == "m" else i, 0 if a_blk == "k" else k))
    else:
        a_spec = pl.BlockSpec((tk, tm), lambda i, j, k: (k, i)) if ta else pl.BlockSpec((tm, tk), lambda i, j, k: (i, k))
    if b_seg:
        assert b.ndim == 2 and not tb
        b = b.reshape(b.shape[0] // NSEG, NSEG * b.shape[1])
        b_spec = pl.BlockSpec((tk, tn), lambda i, j, k: (0, k * (Nb // tn) + j))
    elif b.ndim == 3:
        lead = (lambda j, k: j) if b_blk == "n" else (lambda j, k: k)
        if tb:
            b_spec = pl.BlockSpec((None, tn, tk), lambda i, j, k: (lead(j, k), 0 if b_blk == "n" else j, 0 if b_blk == "k" else k))
        else:
            b_spec = pl.BlockSpec((None, tk, tn), lambda i, j, k: (lead(j, k), 0 if b_blk == "k" else k, 0 if b_blk == "n" else j))
    else:
        b_spec = pl.BlockSpec((tn, tk), lambda i, j, k: (j, k)) if tb else pl.BlockSpec((tk, tn), lambda i, j, k: (k, j))
    slot = _slot_of if o_slots else (lambda k: k)
    if o_blk == "n":
        assert gn == N_DEV or not o_slots
        o_shape, o_spec = (gn, M, tn), pl.BlockSpec((None, tm, tn), lambda i, j, k: (slot(j), i, 0))
    elif o_blk == "m" and o_slots and gm < N_DEV:
        rs = M // N_DEV
        per_tile = tm // rs
        assert per_tile % 2 == 0 and tm % rs == 0
        o_shape = (2, N_CHIP, rs, N)
        o_spec = pl.BlockSpec((2, per_tile // 2, rs, tn), lambda i, j, k: (0, i, 0, j))
    elif o_blk == "m":
        assert gm == N_DEV or not o_slots
        o_shape, o_spec = (gm, tm, N), pl.BlockSpec((None, tm, tn), lambda i, j, k: (slot(i), 0, j))
    elif o_seg:
        o_shape, o_spec = (tm, NSEG * N), pl.BlockSpec((tm, tn), lambda i, j, k: (0, i * (N // tn) + j))
    else:
        o_shape, o_spec = (M, N), pl.BlockSpec((tm, tn), lambda i, j, k: (i, j))
    dims = (((0 if ta else 1,), (1 if tb else 0,)), ((), ()))
    if epi is not None:
        assert gn == 1 and add is None and o_blk is None and not o_seg
        epi_fn, add, epi_w = epi
        o_shape, o_spec = (M, epi_w), pl.BlockSpec((tm, epi_w), lambda i, j, k: (i, 0))
    has_add = add is not None
    add_spec = pl.BlockSpec((tm, add.shape[1]), lambda i, j, k: (i, 0)) if epi is not None else o_spec

    carry = carry or _NO_CARRY
    n_in = 2 + has_add

    def body(*refs):
        own, c_in, c_out, c_sems = _carry_split(carry, refs, n_in, 1)
        a_ref, b_ref = own[0], own[1]
        add_ref = own[2] if has_add else None
        o_ref, acc_ref = own[-2], own[-1]
        i, j, k = pl.program_id(0), pl.program_id(1), pl.program_id(2)
        _carry_start(carry, c_in, c_out, c_sems, (i == 0) & (j == 0) & (k == 0))

        @pl.when(k == 0)
        def _():
            acc_ref[...] = jnp.zeros_like(acc_ref)

        acc_ref[...] += lax.dot_general(a_ref[...].astype(bf16), b_ref[...].astype(bf16), dims, preferred_element_type=f32)

        @pl.when(k == nk - 1)
        def _():
            r = acc_ref[...] * scale
            if epi is not None:
                r = epi_fn(r, add_ref[...].astype(f32))
            elif has_add:
                r = r + add_ref[...].astype(f32)
            if len(o_shape) == 4:
                rs = o_shape[2]
                for chip_l in range(o_ref.shape[1]):
                    for core in range(2):
                        dev = 2 * chip_l + core
                        o_ref[core, chip_l] = r[dev * rs:(dev + 1) * rs].astype(out_dtype)
            else:
                o_ref[...] = r.astype(out_dtype)

        _carry_finish(carry, c_in, c_out, c_sems, (i == gm - 1) & (j == gn - 1) & (k == nk - 1))

    ins = [a, b] + ([add] if has_add else []) + list(carry.ins)
    in_specs = [a_spec, b_spec] + ([add_spec] if has_add else []) + [_ANY] * len(carry.ins)
    res = pl.pallas_call(
        body, name=name, grid=(gm, gn, nk), in_specs=in_specs, out_specs=[o_spec] + [_ANY] * len(carry.out_shapes),
        out_shape=[_S(o_shape, out_dtype)] + list(carry.out_shapes),
        scratch_shapes=[pltpu.VMEM((tm, tn), f32)] + list(carry.sems),
        compiler_params=_cparams(("arbitrary",) * 3 if carry.ins else ("parallel", "parallel", "arbitrary")),
    )(*ins)
    out = res[0]
    if len(o_shape) == 4:
        out = out.reshape(N_DEV, o_shape[2], N)
    elif o_seg:
        out = out.reshape(M, N)
    return (out, list(res[1:])) if carry.ins else out


def _row_tile(T, widths):
    budget = 6 * 1024 * 1024
    tb = max(16, budget // (4 * sum(widths)))
    return _pick(T, tb, align=16)


def _rows(fn, rows, params, outs, *, name, carry=None):
    T = rows[0].shape[0]
    nr, npar = len(rows), len(params)
    tb = _row_tile(T, [r.shape[1] for r in rows] + [w for w, _ in outs])
    carry = carry or _NO_CARRY

    def body(*refs):
        own, c_in, c_out, c_sems = _carry_split(carry, refs, nr + npar, len(outs))
        _carry_start(carry, c_in, c_out, c_sems, pl.program_id(0) == 0)
        ins = [r[...].astype(f32) for r in own[: nr + npar]]
        res = fn(*ins)
        for o_ref, r in zip(own[nr + npar:], res):
            o_ref[...] = r.astype(o_ref.dtype)
        _carry_finish(carry, c_in, c_out, c_sems, pl.program_id(0) == T // tb - 1)

    in_specs = [pl.BlockSpec((tb, r.shape[1]), lambda i: (i, 0)) for r in rows]
    in_specs += [pl.BlockSpec(p.shape, lambda i: (0, 0)) for p in params]
    out_specs = [pl.BlockSpec((tb, w), lambda i: (i, 0)) for w, _ in outs]
    res = pl.pallas_call(
        body, name=name, grid=(T // tb,), in_specs=in_specs + [_ANY] * len(carry.ins),
        out_specs=out_specs + [_ANY] * len(carry.out_shapes), out_shape=[_S((T, w), d) for w, d in outs] + list(carry.out_shapes),
        scratch_shapes=list(carry.sems), compiler_params=_cparams(("arbitrary",) if carry.ins else ("parallel",)),
    )(*rows, *params, *carry.ins)
    return (tuple(res[:len(outs)]), list(res[len(outs):])) if carry.ins else tuple(res)


def _rows_bwd(fn, rows, params, cots, *, name, want_rows, row_dtypes=None, adds=None):
    T = rows[0].shape[0]
    nr, npar, nc = len(rows), len(params), len(cots)
    adds = adds or {}
    add_idx = sorted(adds)
    row_dtypes = row_dtypes or {}
    widths = [r.shape[1] for r in rows] + [c.shape[1] for c in cots] + [rows[i].shape[1] for i in want_rows]
    tb = _row_tile(T, widths)

    def body(*refs):
        ins = [r[...].astype(f32) for r in refs[: nr + npar]]
        cot = tuple(r[...].astype(f32) for r in refs[nr + npar: nr + npar + nc])
        add_refs = refs[nr + npar + nc: nr + npar + nc + len(add_idx)]
        out_refs = refs[nr + npar + nc + len(add_idx):]
        _, vjp = jax.vjp(lambda *a: tuple(fn(*a)), *ins)
        g = vjp(cot)
        for o_ref, i in zip(out_refs[: len(want_rows)], want_rows):
            r = g[i]
            if i in adds:
                r = r + add_refs[add_idx.index(i)][...].astype(f32)
            o_ref[...] = r.astype(o_ref.dtype)
        first = pl.program_id(0) == 0
        for o_ref, gp in zip(out_refs[len(want_rows):], g[nr:]):
            @pl.when(first)
            def _(o_ref=o_ref):
                o_ref[...] = jnp.zeros_like(o_ref)

            o_ref[...] += gp

    in_specs = [pl.BlockSpec((tb, r.shape[1]), lambda i: (i, 0)) for r in rows]
    in_specs += [pl.BlockSpec(p.shape, lambda i: (0, 0)) for p in params]
    in_specs += [pl.BlockSpec((tb, c.shape[1]), lambda i: (i, 0)) for c in cots]
    in_specs += [pl.BlockSpec((tb, adds[i].shape[1]), lambda i_: (i_, 0)) for i in add_idx]
    out_specs = [pl.BlockSpec((tb, rows[i].shape[1]), lambda i_: (i_, 0)) for i in want_rows]
    out_specs += [pl.BlockSpec(p.shape, lambda i: (0, 0)) for p in params]
    out_shape = [_S(rows[i].shape, row_dtypes.get(i, f32)) for i in want_rows] + [_S(p.shape, f32) for p in params]
    res = pl.pallas_call(
        body, name=name, grid=(T // tb,), in_specs=in_specs, out_specs=out_specs, out_shape=out_shape,
        compiler_params=_cparams(("arbitrary",)),
    )(*rows, *params, *cots, *[adds[i] for i in add_idx])
    return list(res[: len(want_rows)]), list(res[len(want_rows):])


def _f_rmsnorm(x, g):
    return (x * lax.rsqrt(jnp.mean(x * x, axis=-1, keepdims=True) + EPS) * g,)


def _f_swiglu(ab):
    F = ab.shape[1] // 2
    return (jax.nn.silu(ab[:, :F]) * ab[:, F:],)


def _f_s5_post(y, u, d):
    return (jax.nn.gelu(y + d * u),)


def _f_glu(g, v, b):
    return (g * jax.nn.sigmoid(v + b),)


def _f_gated_norm(y, z, w):
    return _f_rmsnorm(y * jax.nn.silu(z), w)


def _f_merge(gl, p5, pssd, b):
    D = p5.shape[1]
    gates = jax.nn.sigmoid(gl + b)
    return (gates[:, :D] * p5 + gates[:, D:] * pssd,)


def _f_dt(dtr, bias, a_log):
    dt = jax.nn.softplus(dtr + bias)
    return dt, dt * (-jnp.exp(a_log))


def _f_dt_expand(dtr, bias, a_log, e):
    dt, a = _f_dt(dtr, bias, a_log)
    return dt, a, _doth(dt, e), _doth(a, e)


def _loss_stage(x, tgt, g, *, name):
    T, D = x.shape
    tb = _row_tile(T, [D, D, D])

    def f(xb, gb, tb_):
        y = _f_rmsnorm(xb, gb)[0]
        return 0.5 * jnp.sum(jnp.mean(jnp.square(y - tb_), axis=-1, keepdims=True), axis=0, keepdims=True)

    def body(x_ref, t_ref, g_ref, l_ref, dx_ref, dg_ref):
        tv = t_ref[...]
        val, vjp = jax.vjp(lambda a, b: f(a, b, tv), x_ref[...], g_ref[...])
        dx, dg = vjp(jnp.ones((1, 1), f32))
        dx_ref[...] = dx

        @pl.when(pl.program_id(0) == 0)
        def _():
            l_ref[...] = jnp.zeros_like(l_ref)
            dg_ref[...] = jnp.zeros_like(dg_ref)

        l_ref[...] += jnp.broadcast_to(val, l_ref.shape)
        dg_ref[...] += dg

    row = pl.BlockSpec((tb, D), lambda i: (i, 0))
    par = pl.BlockSpec((1, D), lambda i: (0, 0))
    return pl.pallas_call(
        body, name=name, grid=(T // tb,), in_specs=[row, row, par],
        out_specs=[pl.BlockSpec((1, LANES), lambda i: (0, 0)), row, par],
        out_shape=[_S((1, LANES), f32), _S((T, D), f32), _S((1, D), f32)], compiler_params=_cparams(("arbitrary",)),
    )(x, tgt, g)


CONV_R = 64
HALO = 8


def _conv_shifts_down(ref, t, n_tiles):
    if isinstance(t, int) and t == 0:
        cur = ref[0:CONV_R, :]
        row = lax.broadcasted_iota(jnp.int32, cur.shape, 0)
        return [cur] + [jnp.where(row >= s, pltpu.roll(cur, s, axis=0), 0.0) for s in range(1, CONV_K)]
    win = ref[pl.ds(pl.multiple_of(t * CONV_R - HALO, HALO), CONV_R + HALO), :]
    return [win[HALO:]] + [pltpu.roll(win, s, axis=0)[HALO:] for s in range(1, CONV_K)]


def _conv_shifts_up(ref, t, n_tiles):
    if isinstance(t, int):
        cur = ref[t * CONV_R:(t + 1) * CONV_R, :]
        row = lax.broadcasted_iota(jnp.int32, cur.shape, 0)
        return [cur] + [jnp.where(row < CONV_R - s, pltpu.roll(cur, CONV_R - s, axis=0), 0.0) for s in range(1, CONV_K)]
    win = ref[pl.ds(pl.multiple_of(t * CONV_R, HALO), CONV_R + HALO), :]
    return [win[:CONV_R]] + [pltpu.roll(win, CONV_R + HALO - s, axis=0)[:CONV_R] for s in range(1, CONV_K)]


def _conv_pre(shifted, w, b):
    pre = b
    for k in range(CONV_K):
        pre = pre + w[k:k + 1, :] * shifted[CONV_K - 1 - k]
    return pre


def _conv_fwd(x, w, b, *, name):
    T, C = x.shape
    cb = _pick(C, 256)

    def body(x_ref, w_ref, b_ref, o_ref):
        xv = x_ref[...]
        row = lax.broadcasted_iota(jnp.int32, xv.shape, 0)
        shifted = [xv] + [jnp.where(row >= s, pltpu.roll(xv, s, axis=0), 0.0) for s in range(1, CONV_K)]
        o_ref[...] = jax.nn.silu(_conv_pre(shifted, w_ref[...], b_ref[...]))

    col = pl.BlockSpec((T, cb), lambda j: (0, j))
    return pl.pallas_call(
        body, name=name, grid=(C // cb,), in_specs=[col, pl.BlockSpec((CONV_K, cb), lambda j: (0, j)), pl.BlockSpec((1, cb), lambda j: (0, j))],
        out_specs=col, out_shape=_S((T, C), f32), compiler_params=_cparams(("parallel",)),
    )(x, w, b)


def _conv_bwd(x, w, b, dy, *, name, carry=None):
    T, C = x.shape
    cb = _pick(C, 128)
    carry = carry or _NO_CARRY
    ends = []
    for d in dy:
        ends.append((ends[-1] if ends else 0) + d.shape[1] // cb)
    assert ends[-1] == C // cb and all(d.shape[1] % cb == 0 for d in dy)
    npc = len(dy)
    n_tiles = T // CONV_R

    def body(*refs):
        own, c_in, c_out, c_sems = _carry_split(carry, refs, 3 + npc, 3)
        x_ref, w_ref, b_ref = own[:3]
        dy_refs, (dx_ref, dw_ref, db_ref, dpre_ref) = own[3:3 + npc], own[3 + npc:]
        _carry_start(carry, c_in, c_out, c_sems, pl.program_id(0) == 0)
        j = pl.program_id(0)
        wv, bv = w_ref[...], b_ref[...]

        def fold8(v):
            return jnp.sum(v.reshape(CONV_R // 8, 8, cb), axis=0)

        def first_pass(t, acc):
            rows = slice(0, CONV_R) if isinstance(t, int) else pl.ds(pl.multiple_of(t * CONV_R, CONV_R), CONV_R)
            shifted = _conv_shifts_down(x_ref, t, n_tiles)
            pre = _conv_pre(shifted, wv, bv)
            dyv = dy_refs[-1][rows, :]
            for p in range(npc - 2, -1, -1):
                dyv = jnp.where(j < ends[p], dy_refs[p][rows, :], dyv)
            sg = jax.nn.sigmoid(pre)
            dpre = dyv * sg * (1.0 + pre * (1.0 - sg))
            dpre_ref[rows, :] = dpre
            return tuple(acc[k] + fold8(dpre * shifted[CONV_K - 1 - k]) for k in range(CONV_K)) + (acc[CONV_K] + fold8(dpre),)

        zero = jnp.zeros((8, cb), f32)
        acc = lax.fori_loop(1, n_tiles, first_pass, first_pass(0, (zero,) * (CONV_K + 1)), unroll=3)
        for k in range(CONV_K):
            dw_ref[k:k + 1, :] = jnp.sum(acc[k], axis=0, keepdims=True)
        db_ref[...] = jnp.sum(acc[CONV_K], axis=0, keepdims=True)

        def dx_of(t):
            up = _conv_shifts_up(dpre_ref, t, n_tiles)
            dx = wv[CONV_K - 1:CONV_K, :] * up[0]
            for k in range(CONV_K - 1):
                dx = dx + wv[k:k + 1, :] * up[CONV_K - 1 - k]
            return dx

        def second_pass(t, c):
            dx_ref[pl.ds(pl.multiple_of(t * CONV_R, CONV_R), CONV_R), :] = dx_of(t)
            return c

        lax.fori_loop(0, n_tiles - 1, second_pass, 0, unroll=3)
        dx_ref[(n_tiles - 1) * CONV_R:, :] = dx_of(n_tiles - 1)
        _carry_finish(carry, c_in, c_out, c_sems, pl.program_id(0) == C // cb - 1)

    col = pl.BlockSpec((T, cb), lambda j: (0, j))
    wsp = pl.BlockSpec((CONV_K, cb), lambda j: (0, j))
    bsp = pl.BlockSpec((1, cb), lambda j: (0, j))
    starts = [0] + ends[:-1]
    dy_specs = [pl.BlockSpec((T, cb), lambda j, s=s, e=e: (0, jnp.clip(j, s, e - 1) - s)) for s, e in zip(starts, ends)]
    res = pl.pallas_call(
        body, name=name, grid=(C // cb,), in_specs=[col, wsp, bsp] + dy_specs + [_ANY] * len(carry.ins),
        out_specs=[col, wsp, bsp] + [_ANY] * len(carry.out_shapes),
        out_shape=[_S((T, C), f32), _S((CONV_K, C), f32), _S((1, C), f32)] + list(carry.out_shapes),
        scratch_shapes=[pltpu.VMEM((T, cb), f32)] + list(carry.sems),
        compiler_params=_cparams(("arbitrary",) if carry.ins else ("parallel",)),
    )(x, w, b, *dy, *carry.ins)
    return (*res[:3], list(res[3:]))


def _f_s5_prep(lr, li, ldt, lrb, lib, ldtb, brt, bit):
    def disc(lr_, li_, ldt_):
        dt = jnp.exp(ldt_)
        mag = jnp.exp(lr_ * dt)
        ar, ai = mag * jnp.cos(li_ * dt), mag * jnp.sin(li_ * dt)
        den = lr_ * lr_ + li_ * li_
        cr = ((ar - 1.0) * lr_ + ai * li_) / den
        ci = (ai * lr_ - (ar - 1.0) * li_) / den
        return ar, ai, cr, ci

    ar, ai, _, _ = disc(lr, li, ldt)
    _, _, cr, ci = disc(lrb, lib, ldtb)
    return ar, ai, cr * brt - ci * bit, cr * bit + ci * brt


def _s5_prep(args, *, name):
    G, N = args[0].shape
    GM = args[3].shape[0]

    def body(*refs):
        res = _f_s5_prep(*[r[...] for r in refs[:8]])
        for o, r in zip(refs[8:], res):
            o[...] = r

    return pl.pallas_call(body, name=name, out_shape=[_S((G, N), f32)] * 2 + [_S((GM, N), f32)] * 2)(*args)


def _s5_prep_bwd(args, cots, rsum, *, name):
    G, N = args[0].shape
    GM = args[3].shape[0]

    def body(*refs):
        ins = [r[...] for r in refs[:8]]
        cot = tuple(r[...] for r in refs[8:12])
        rs = refs[12][...]
        _, vjp = jax.vjp(_f_s5_prep, *ins)
        g = vjp(cot)
        fold = lambda v: jnp.dot(rs, v, preferred_element_type=f32, precision=lax.Precision.HIGHEST)
        o = refs[13:]
        o[0][...] = g[0] + fold(g[3])
        o[1][...] = g[1] + fold(g[4])
        o[2][...] = g[2] + fold(jnp.broadcast_to(g[5], (GM, LANES)))[:, 0:1]
        o[3][...] = g[6]
        o[4][...] = g[7]

    return pl.pallas_call(
        body, name=name, out_shape=[_S((G, N), f32), _S((G, N), f32), _S((G, 1), f32), _S((GM, N), f32), _S((GM, N), f32)],
    )(*args, *cots, rsum)


S5_TC = 512


def _s5_local_scan(src, w_r, w_i, a_r, a_i, *, reverse, name, carry=None, powers=True):
    T, C = src.shape
    nblk, cb, sb = w_r.shape
    NS = nblk * sb
    tc = min(S5_TC, T)
    nT, nt = T // tc, tc // NSEG
    tmap = (lambda i: nT - 1 - i) if reverse else (lambda i: i)

    carry = carry or _NO_CARRY

    def body(*refs):
        own, c_in, c_out, c_sems = _carry_split(carry, refs, 5, 4)
        u_ref, wr_ref, wi_ref, ar_ref, ai_ref, sr_ref, si_ref, pr_ref, pi_ref, st_r, st_i, pw_r, pw_i = own
        _carry_start(carry, c_in, c_out, c_sems, (pl.program_id(0) == 0) & (pl.program_id(1) == 0))

        @pl.when(pl.program_id(1) == 0)
        def _():
            st_r[...] = jnp.zeros_like(st_r)
            st_i[...] = jnp.zeros_like(st_i)
            pw_r[...] = jnp.ones_like(pw_r)
            pw_i[...] = jnp.zeros_like(pw_i)

        u = u_ref[...].astype(bf16)
        sr_ref[...] = jnp.dot(u, wr_ref[...], preferred_element_type=f32)
        si_ref[...] = jnp.dot(u, wi_ref[...], preferred_element_type=f32)
        ar = jnp.broadcast_to(ar_ref[...], (NSEG, sb))
        ai = jnp.broadcast_to(ai_ref[...], (NSEG, sb))

        def step(k, c):
            cr, ci, qr, qi = c
            kk = (nt - 1 - k) if reverse else k
            rows = pl.ds(pl.multiple_of(kk * NSEG, NSEG), NSEG)
            nr = ar * cr - ai * ci + sr_ref[rows, :]
            ni = ar * ci + ai * cr + si_ref[rows, :]
            sr_ref[rows, :] = nr
            si_ref[rows, :] = ni
            return (nr, ni, ar * qr - ai * qi, ar * qi + ai * qr) if powers else (nr, ni, qr, qi)

        cr, ci, qr, qi = lax.fori_loop(0, nt, step, (st_r[...], st_i[...], pw_r[...], pw_i[...]), unroll=8)
        st_r[...], st_i[...], pw_r[...], pw_i[...] = cr, ci, qr, qi
        pr_ref[...] = qr
        pi_ref[...] = qi
        _carry_finish(carry, c_in, c_out, c_sems, (pl.program_id(0) == nblk - 1) & (pl.program_id(1) == nT - 1))

    blk = pl.BlockSpec((tc, sb), lambda j, i: (tmap(i), j))
    wsp = pl.BlockSpec((None, cb, sb), lambda j, i: (j, 0, 0))
    asp = pl.BlockSpec((1, sb), lambda j, i: (0, j))
    psp = pl.BlockSpec((NSEG, sb), lambda j, i: (0, j))
    res = pl.pallas_call(
        body, name=name, grid=(nblk, nT),
        in_specs=[pl.BlockSpec((tc, cb), lambda j, i: (tmap(i), j)), wsp, wsp, asp, asp] + [_ANY] * len(carry.ins),
        out_specs=[blk, blk, psp, psp] + [_ANY] * len(carry.out_shapes),
        out_shape=[_S((T, NS), f32)] * 2 + [_S((NSEG, NS), f32)] * 2 + list(carry.out_shapes),
        scratch_shapes=[pltpu.VMEM((NSEG, sb), f32)] * 4 + list(carry.sems),
        compiler_params=_cparams(("arbitrary", "arbitrary") if carry.ins else ("parallel", "arbitrary")),
    )(src, w_r, w_i, a_r, a_i, *carry.ins)
    return (*res[:4], list(res[4:])) if carry.ins else res


def _s5_carry(e_r, e_i, p_r, p_i, *, reverse, name):
    NS = e_r.shape[1]

    def body(er_ref, ei_ref, pr_ref, pi_ref, cr_ref, ci_ref):
        ar, ai = pr_ref[0:1, :], pi_ref[0:1, :]
        cr = jnp.zeros((1, NS), f32)
        ci = jnp.zeros((1, NS), f32)
        order = list(range(NSEG - 1, -1, -1)) if reverse else list(range(NSEG))
        cr_ref[order[0]:order[0] + 1, :] = cr
        ci_ref[order[0]:order[0] + 1, :] = ci
        for prev, q in zip(order[:-1], order[1:]):
            er, ei = er_ref[prev:prev + 1, :], ei_ref[prev:prev + 1, :]
            cr, ci = er + ar * cr - ai * ci, ei + ar * ci + ai * cr
            cr_ref[q:q + 1, :] = cr
            ci_ref[q:q + 1, :] = ci

    return pl.pallas_call(body, name=name, out_shape=[_S((NSEG, NS), f32)] * 2)(e_r, e_i, p_r, p_i)


def _s5_fix_out(sl_r, sl_i, a_r, a_i, c_r, c_i, wc_r, wc_i, *, name):
    T, NS = sl_r.shape
    nblk, sb, cb = wc_r.shape
    tc = min(S5_TC, T)
    nT, nt = T // tc, tc // NSEG

    def body(lr_ref, li_ref, ar_ref, ai_ref, cr_ref, ci_ref, wr_ref, wi_ref, sr_ref, si_ref, y_ref, pw_r, pw_i):
        @pl.when(pl.program_id(1) == 0)
        def _():
            pw_r[...] = jnp.ones_like(pw_r)
            pw_i[...] = jnp.zeros_like(pw_i)

        ar = jnp.broadcast_to(ar_ref[...], (NSEG, sb))
        ai = jnp.broadcast_to(ai_ref[...], (NSEG, sb))
        cr, ci = cr_ref[...], ci_ref[...]

        def step(k, c):
            qr, qi = c
            qr, qi = ar * qr - ai * qi, ar * qi + ai * qr
            rows = pl.ds(pl.multiple_of(k * NSEG, NSEG), NSEG)
            sr_ref[rows, :] = lr_ref[rows, :] + qr * cr - qi * ci
            si_ref[rows, :] = li_ref[rows, :] + qr * ci + qi * cr
            return qr, qi

        qr, qi = lax.fori_loop(0, nt, step, (pw_r[...], pw_i[...]), unroll=8)
        pw_r[...], pw_i[...] = qr, qi
        y_ref[...] = (jnp.dot(sr_ref[...].astype(bf16), wr_ref[...], preferred_element_type=f32)
                      - jnp.dot(si_ref[...].astype(bf16), wi_ref[...], preferred_element_type=f32))

    blk = pl.BlockSpec((tc, sb), lambda j, i: (i, j))
    asp = pl.BlockSpec((1, sb), lambda j, i: (0, j))
    csp = pl.BlockSpec((NSEG, sb), lambda j, i: (0, j))
    wsp = pl.BlockSpec((None, sb, cb), lambda j, i: (j, 0, 0))
    return pl.pallas_call(
        body, name=name, grid=(nblk, nT), in_specs=[blk, blk, asp, asp, csp, csp, wsp, wsp],
        out_specs=[blk, blk, pl.BlockSpec((tc, cb), lambda j, i: (i, j))],
        out_shape=[_S((T, NS), f32)] * 2 + [_S((T, nblk * cb), f32)],
        scratch_shapes=[pltpu.VMEM((NSEG, sb), f32)] * 2, compiler_params=_cparams(("parallel", "arbitrary")),
    )(sl_r, sl_i, a_r, a_i, c_r, c_i, wc_r, wc_i)


def _s5_fix_bwd(ql_r, ql_i, ab_r, ab_i, c_r, c_i, s_r, s_i, sb_r, sb_i, u, dy, du_add, w_r, w_i, *, name):
    T, NS = ql_r.shape
    nblk, cb, sb = w_r.shape
    tc = min(S5_TC, T)
    nT, nt = T // tc, tc // NSEG
    tmap = lambda i: nT - 1 - i

    def body(lr_ref, li_ref, ar_ref, ai_ref, cr_ref, ci_ref, sr_ref, si_ref, br_ref, bi_ref, u_ref, dy_ref, dua_ref, wr_ref, wi_ref,
             du_ref, dwr_ref, dwi_ref, dcr_ref, dci_ref, dar_ref, dai_ref, pw_r, pw_i, ac_r, ac_i, q_r, q_i):
        first = pl.program_id(1) == 0

        @pl.when(first)
        def _():
            pw_r[...] = jnp.ones_like(pw_r)
            pw_i[...] = jnp.zeros_like(pw_i)
            ac_r[...] = jnp.zeros_like(ac_r)
            ac_i[...] = jnp.zeros_like(ac_i)
            dwr_ref[...] = jnp.zeros_like(dwr_ref)
            dwi_ref[...] = jnp.zeros_like(dwi_ref)
            dcr_ref[...] = jnp.zeros_like(dcr_ref)
            dci_ref[...] = jnp.zeros_like(dci_ref)

        ar = jnp.broadcast_to(ar_ref[...], (NSEG, sb))
        ai = jnp.broadcast_to(ai_ref[...], (NSEG, sb))
        cr, ci = cr_ref[...], ci_ref[...]

        def fix(rows, qr, qi, spr, spi, accr, acci):
            qr, qi = ar * qr - ai * qi, ar * qi + ai * qr
            xr = lr_ref[rows, :] + qr * cr - qi * ci
            xi = li_ref[rows, :] + qr * ci + qi * cr
            q_r[rows, :] = xr
            q_i[rows, :] = xi
            return qr, qi, accr + xr * spr + xi * spi, acci + xi * spr - xr * spi

        def step(k, c):
            qr, qi, accr, acci = c
            kk = nt - 1 - k
            rows = pl.ds(pl.multiple_of(kk * NSEG, NSEG), NSEG)
            prev = pl.ds(pl.multiple_of((kk - 1) * NSEG, NSEG), NSEG)
            return fix(rows, qr, qi, sr_ref[prev, :], si_ref[prev, :], accr, acci)

        c = lax.fori_loop(0, nt - 1, step, (pw_r[...], pw_i[...], ac_r[...], ac_i[...]), unroll=7)
        qr, qi, accr, acci = fix(pl.ds(0, NSEG), *c[:2], br_ref[...], bi_ref[...], *c[2:])
        pw_r[...], pw_i[...], ac_r[...], ac_i[...] = qr, qi, accr, acci

        qrb, qib = q_r[...].astype(bf16), q_i[...].astype(bf16)
        nt_dims = (((1,), (1,)), ((), ()))
        tn_dims = (((0,), (0,)), ((), ()))
        du_ref[...] = (dua_ref[...] + lax.dot_general(qrb, wr_ref[...], nt_dims, preferred_element_type=f32)
                       + lax.dot_general(qib, wi_ref[...], nt_dims, preferred_element_type=f32))
        ub = u_ref[...].astype(bf16)
        dwr_ref[...] += lax.dot_general(ub, qrb, tn_dims, preferred_element_type=f32)
        dwi_ref[...] += lax.dot_general(ub, qib, tn_dims, preferred_element_type=f32)
        dyb = dy_ref[...].astype(bf16)
        dcr_ref[...] += lax.dot_general(sr_ref[...].astype(bf16), dyb, tn_dims, preferred_element_type=f32)
        dci_ref[...] -= lax.dot_general(si_ref[...].astype(bf16), dyb, tn_dims, preferred_element_type=f32)

        @pl.when(pl.program_id(1) == nT - 1)
        def _():
            dar_ref[...] = jnp.sum(accr, axis=0, keepdims=True)
            dai_ref[...] = jnp.sum(acci, axis=0, keepdims=True)

    blk = pl.BlockSpec((tc, sb), lambda j, i: (tmap(i), j))
    asp = pl.BlockSpec((1, sb), lambda j, i: (0, j))
    csp = pl.BlockSpec((NSEG, sb), lambda j, i: (0, j))
    bsp = pl.BlockSpec((None, NSEG, sb), lambda j, i: (tmap(i), 0, j))
    chn = pl.BlockSpec((tc, cb), lambda j, i: (tmap(i), j))
    wsp = pl.BlockSpec((None, cb, sb), lambda j, i: (j, 0, 0))
    wcs = pl.BlockSpec((None, sb, cb), lambda j, i: (j, 0, 0))
    return pl.pallas_call(
        body, name=name, grid=(nblk, nT), in_specs=[blk, blk, asp, asp, csp, csp, blk, blk, bsp, bsp, chn, chn, chn, wsp, wsp],
        out_specs=[chn, wsp, wsp, wcs, wcs, asp, asp],
        out_shape=[_S((T, nblk * cb), f32), _S((nblk, cb, sb), f32), _S((nblk, cb, sb), f32), _S((nblk, sb, cb), f32),
                   _S((nblk, sb, cb), f32), _S((1, NS), f32), _S((1, NS), f32)],
        scratch_shapes=[pltpu.VMEM((NSEG, sb), f32)] * 4 + [pltpu.VMEM((tc, sb), f32)] * 2,
        compiler_params=_cparams(("parallel", "arbitrary")),
    )(ql_r, ql_i, ab_r, ab_i, c_r, c_i, s_r, s_i, sb_r, sb_i, u, dy, du_add, w_r, w_i)


SSD2_TB = 512
_NN = (((1,), (0,)), ((), ()))
_NT = (((1,), (1,)), ((), ()))
_TN = (((0,), (0,)), ((), ()))


def _dotf(a, b, dims):
    return lax.dot_general(a.astype(bf16), b.astype(bf16), dims, preferred_element_type=f32)


def _doth(a, b, dims=_NN, sel="b", parts=3):
    x, m = (a, b) if sel == "b" else (b, a)
    m = m.astype(bf16)
    out = None
    for _ in range(parts):
        piece = x.astype(bf16)
        x = x - piece.astype(f32)
        d = lax.dot_general(*((piece, m) if sel == "b" else (m, piece)), dims, preferred_element_type=f32)
        out = d if out is None else out + d
    return out


def _ssd_consts(hpg):
    W = hpg * CHUNK
    i = lax.broadcasted_iota(jnp.int32, (CHUNK, CHUNK), 0)
    j = lax.broadcasted_iota(jnp.int32, (CHUNK, CHUNK), 1)
    tril = (i >= j).astype(f32)
    r = lax.broadcasted_iota(jnp.int32, (W, W), 0)
    c = lax.broadcasted_iota(jnp.int32, (W, W), 1)
    bd = (r // CHUNK == c // CHUNK).astype(f32)
    triu_bd = bd * (r <= c).astype(f32)
    e_r = lax.broadcasted_iota(jnp.int32, (W, LANES), 0)
    e_c = lax.broadcasted_iota(jnp.int32, (W, LANES), 1)
    ered = (e_r // HEADDIM == e_c).astype(f32)
    return tril, jnp.tile(tril, (1, hpg)), bd, triu_bd, ered


def _ssd2_specs(G, hpg, tb, tmap, b_off, c_off):
    W = hpg * HEADDIM
    ncb = tb // CHUNK
    xsp = pl.BlockSpec((tb, W), lambda g, i: (tmap(i), g))
    bsp = pl.BlockSpec((tb, SSD_STATE), lambda g, i: (tmap(i), b_off + g))
    csp = pl.BlockSpec((tb, SSD_STATE), lambda g, i: (tmap(i), c_off + g))
    rsp = pl.BlockSpec((None, ncb, W), lambda g, i: (g, tmap(i), 0))
    dsp = pl.BlockSpec((1, W), lambda g, i: (0, g))
    hsp = pl.BlockSpec((None, ncb, SSD_STATE, W), lambda g, i: (g, tmap(i), 0, 0))
    const = lambda a: pl.BlockSpec(a.shape, lambda g, i: (0, 0))
    return xsp, bsp, csp, rsp, dsp, hsp, const


def _tile_rows(a, n):
    return jnp.concatenate([a] * n, axis=0)


def _ssd2_fwd(xc, dt4, a4, dtw, aw, d4, consts, *, d_inner, name, carry=None):
    carry = carry or _NO_CARRY
    T = xc.shape[0]
    G, nc, W = dtw.shape
    hpg = W // CHUNK
    tb = min(SSD2_TB, T)
    nb, ncb = T // tb, tb // CHUNK
    b_off = d_inner // SSD_STATE
    xsp, bsp, csp, rsp, dsp, hsp, const = _ssd2_specs(G, hpg, tb, lambda i: i, b_off, b_off + G)
    tril, mask4, bd, triu_bd, _ = consts

    def body(*refs):
        own, c_in, c_out, c_sems = _carry_split(carry, refs, 12, 2)
        x_ref, b_ref, c_ref, dt_ref, a_ref, dtw_ref, aw_ref, d_ref, tril_ref, mask_ref, bd_ref, tbd_ref, y_ref, hs_ref, h_scr = own
        _carry_start(carry, c_in, c_out, c_sems, (pl.program_id(0) == 0) & (pl.program_id(1) == 0))

        @pl.when(pl.program_id(1) == 0)
        def _():
            h_scr[...] = jnp.zeros_like(h_scr)

        acs_rows = _doth(aw_ref[...], tbd_ref[...])
        ht = h_scr[...]
        for c in range(ncb):
            rows = slice(c * CHUNK, (c + 1) * CHUNK)
            x, bm, cm = x_ref[rows, :], b_ref[rows, :], c_ref[rows, :]
            acs = _doth(tril_ref[...], a_ref[rows, :], sel="a")
            lmat = jnp.where(mask_ref[...] > 0, jnp.exp(jnp.minimum(acs - acs_rows[c:c + 1, :], 0.0)), 0.0)
            m4 = _dotf(cm, _tile_rows(bm, hpg), _NT) * lmat * dtw_ref[c:c + 1, :]
            xbd = _tile_rows(x, hpg) * bd_ref[...]
            hs_ref[c] = ht
            y_ref[rows, :] = _dotf(m4, xbd, _NN) + _dotf(cm, ht, _NN) * jnp.exp(acs) + d_ref[...] * x
            a_last = acs[CHUNK - 1:CHUNK, :]
            xw = x * (jnp.exp(a_last - acs) * dt_ref[rows, :])
            ht = ht * jnp.exp(a_last) + _dotf(bm, xw, _TN)
        h_scr[...] = ht
        _carry_finish(carry, c_in, c_out, c_sems, (pl.program_id(0) == G - 1) & (pl.program_id(1) == nb - 1))

    res = pl.pallas_call(
        body, name=name, grid=(G, nb),
        in_specs=[xsp, bsp, csp, xsp, xsp, rsp, rsp, dsp, const(tril), const(mask4), const(bd), const(triu_bd)] + [_ANY] * len(carry.ins),
        out_specs=[xsp, hsp] + [_ANY] * len(carry.out_shapes),
        out_shape=[_S((T, G * W), f32), _S((G, nc, SSD_STATE, W), f32)] + list(carry.out_shapes),
        scratch_shapes=[pltpu.VMEM((SSD_STATE, W), f32)] + list(carry.sems),
        compiler_params=_cparams(("arbitrary", "arbitrary") if carry.ins else ("parallel", "arbitrary")),
    )(xc, xc, xc, dt4, a4, dtw, aw, d4, tril, mask4, bd, triu_bd, *carry.ins)
    return res[0], res[1], list(res[2:])


def _ssd2_bwd(xc, dt4, a4, dtw, aw, d4, consts, hs, dy, *, d_inner, name):
    T = xc.shape[0]
    G, nc, W = dtw.shape
    hpg = W // CHUNK
    tb = min(SSD2_TB, T)
    nb, ncb = T // tb, tb // CHUNK
    b_off = d_inner // SSD_STATE
    tmap = lambda i: nb - 1 - i
    xsp, bsp, csp, rsp, dsp, hsp, const = _ssd2_specs(G, hpg, tb, tmap, b_off, b_off + G)
    gsp = pl.BlockSpec((tb, SSD_STATE), lambda g, i: (tmap(i), g))
    ddsp = pl.BlockSpec((None, 1, LANES), lambda g, i: (g, 0, 0))
    tril, mask4, bd, triu_bd, ered = consts

    def body(x_ref, b_ref, c_ref, dt_ref, a_ref, dtw_ref, aw_ref, d_ref, tril_ref, mask_ref, bd_ref, tbd_ref, er_ref, hs_ref, dy_ref,
             dx_ref, db_ref, dc_ref, ddtc_ref, dac_ref, ddtw_ref, daw_ref, dd_ref, g_scr, dd_scr, rw_scr, tl_scr):
        first = pl.program_id(1) == 0

        @pl.when(first)
        def _():
            g_scr[...] = jnp.zeros_like(g_scr)
            dd_scr[...] = jnp.zeros_like(dd_scr)

        mask = mask_ref[...] > 0
        lane_in_block = lax.broadcasted_iota(jnp.int32, mask.shape, 1) & (CHUNK - 1)
        maskt = lax.broadcasted_iota(jnp.int32, mask.shape, 0) <= lane_in_block
        acs_rows = _doth(aw_ref[...], tbd_ref[...])
        dht = g_scr[...]
        dd = dd_scr[...]
        for c in range(ncb - 1, -1, -1):
            rows = slice(c * CHUNK, (c + 1) * CHUNK)
            x, bm, cm, dyc = x_ref[rows, :], b_ref[rows, :], c_ref[rows, :], dy_ref[rows, :]
            dtc, dtr = dt_ref[rows, :], dtw_ref[c:c + 1, :]
            ht = hs_ref[c]
            acs = _doth(tril_ref[...], a_ref[rows, :], sel="a")
            seg = acs - acs_rows[c:c + 1, :]
            lmat = jnp.where(mask, jnp.exp(jnp.minimum(seg, 0.0)), 0.0)
            lmat_t = jnp.where(maskt, jnp.exp(jnp.minimum(-seg, 0.0)), 0.0)
            btile, ctile = _tile_rows(bm, hpg), _tile_rows(cm, hpg)
            g4 = _dotf(cm, btile, _NT)
            gt4 = _dotf(bm, ctile, _NT)
            m4 = g4 * lmat * dtr
            mt4 = gt4 * lmat_t * dtc
            xbd = _tile_rows(x, hpg) * bd_ref[...]
            dybd = _tile_rows(dyc, hpg) * bd_ref[...]
            dm4 = _dotf(dyc, xbd, _NT)
            dmt4 = _dotf(x, dybd, _NT)
            dx = d_ref[...] * dyc + _dotf(mt4, dybd, _NN)
            dd = dd + jnp.sum(dyc * x, axis=0, keepdims=True)
            e4 = dm4 * m4
            dc = _dotf(dm4 * lmat * dtr, btile, _NN)
            db = _dotf(dmt4 * lmat_t * dtc, ctile, _NN)
            decay = jnp.exp(acs)
            yoff = _dotf(cm, ht, _NN) * decay
            dz = dyc * decay
            dc = dc + _dotf(dz, ht, _NT)
            dht_prev = _dotf(cm, dz, _TN)
            a_last = acs[CHUNK - 1:CHUNK, :]
            ea_last = jnp.exp(a_last)
            erel = jnp.exp(a_last - acs)
            dte = erel * dtc
            dxw = _dotf(bm, dht, _NN)
            db = db + _dotf(x * dte, dht, _NT)
            dx = dx + dxw * dte
            q4 = dxw * x
            dacs = e4 + dyc * yoff - q4 * dte
            col = jnp.concatenate([q4 * erel, _doth(tril_ref[...], dacs, _TN, sel="a")], axis=0)
            col = _doth(col, er_ref[...], parts=2)
            ddtc_ref[rows, :] = col[:CHUNK]
            dac_ref[rows, :] = col[CHUNK:]
            ddtw_ref[c:c + 1, :] = jnp.sum(dm4 * g4 * lmat, axis=0, keepdims=True)
            rw_scr[c:c + 1, :] = -jnp.sum(e4, axis=0, keepdims=True)
            tl_scr[c:c + 1, :] = jnp.sum(q4 * dte, axis=0, keepdims=True) + ea_last * jnp.sum(dht * ht, axis=0, keepdims=True)
            dx_ref[rows, :] = dx
            db_ref[rows, :] = db
            dc_ref[rows, :] = dc
            dht = dht_prev + dht * ea_last
        daw_ref[...] = _doth(rw_scr[...], tbd_ref[...], _NT) + _doth(tl_scr[...], bd_ref[...])
        g_scr[...] = dht
        dd_scr[...] = dd

        @pl.when(pl.program_id(1) == nb - 1)
        def _():
            dd_ref[...] = _doth(dd, er_ref[...])

    return pl.pallas_call(
        body, name=name, grid=(G, nb),
        in_specs=[xsp, bsp, csp, xsp, xsp, rsp, rsp, dsp, const(tril), const(mask4), const(bd), const(triu_bd), const(ered), hsp, xsp],
        out_specs=[xsp, gsp, gsp, gsp, gsp, rsp, rsp, ddsp],
        out_shape=[_S((T, G * W), f32), _S((T, G * SSD_STATE), f32), _S((T, G * SSD_STATE), f32), _S((T, G * LANES), f32),
                   _S((T, G * LANES), f32), _S(dtw.shape, f32), _S(dtw.shape, f32), _S((G, 1, LANES), f32)],
        scratch_shapes=[pltpu.VMEM((SSD_STATE, W), f32), pltpu.VMEM((1, W), f32), pltpu.VMEM((ncb, W), f32), pltpu.VMEM((ncb, W), f32)],
        compiler_params=_cparams(("parallel", "arbitrary")),
    )(xc, xc, xc, dt4, a4, dtw, aw, d4, tril, mask4, bd, triu_bd, ered, hs, dy)


def _peers():
    x, y, c = lax.axis_index("x"), lax.axis_index("y"), lax.axis_index("c")
    return x, y, c


_ANY = pl.BlockSpec(memory_space=pl.ANY)
N_CHIP = N_DEV // 2


def _all_gather(shards, *, name):
    n = len(shards)
    carry = _carry_gather(shards)

    def body(*refs):
        x_refs, out_refs, sems = refs[:n], refs[n:2 * n], refs[2 * n:]
        _gather_start(x_refs, out_refs, sems)
        _gather_finish(x_refs, out_refs, sems)

    return pl.pallas_call(
        body, name=name, out_shape=list(carry.out_shapes), in_specs=[_ANY] * n, out_specs=[_ANY] * n, scratch_shapes=list(carry.sems),
    )(*shards)


def _gather_parts(x_refs, out_refs, sems):
    send_sems, recv_sems, local_sems = sems
    x, y, c = _peers()
    me, sibling = (x, y, c), (x, y, 1 - c)
    chips = [(1 - x, y), (x, 1 - y), (1 - x, 1 - y)]
    n = len(x_refs)

    def copy(a, r, block, to, src=None):
        px, py, pc = block
        slot = out_refs[a].at[4 * px + 2 * py + pc]
        return pltpu.make_async_remote_copy(
            src_ref=slot if src is None else src, dst_ref=slot, send_sem=send_sems.at[7 * a + r],
            recv_sem=recv_sems.at[7 * a + r], device_id=to, device_id_type=MESH)

    mine = [pltpu.make_async_copy(x_refs[a], out_refs[a].at[4 * x + 2 * y + c], local_sems.at[a]) for a in range(n)]
    first = []
    for a in range(n):
        first.append(copy(a, 0, me, sibling, src=x_refs[a]))
        first += [copy(a, 1 + j, me, (*chip, c), src=x_refs[a]) for j, chip in enumerate(chips)]
    return copy, mine, first, me, sibling, chips, c, n


def _gather_start(x_refs, out_refs, sems):
    _, mine, first, *_ = _gather_parts(x_refs, out_refs, sems)
    for cp in mine + first:
        cp.start()


def _gather_finish(x_refs, out_refs, sems):
    copy, mine, first, me, sibling, chips, c, n = _gather_parts(x_refs, out_refs, sems)
    passed = []
    for j, chip in enumerate(chips):
        for a in range(n):
            copy(a, 1 + j, (*chip, c), me).wait_recv()
            fwd = copy(a, 4 + j, (*chip, c), sibling)
            fwd.start()
            passed.append(fwd)
    for a in range(n):
        copy(a, 0, sibling, me).wait_recv()
    for j, chip in enumerate(chips):
        for a in range(n):
            copy(a, 4 + j, (*chip, 1 - c), me).wait_recv()
    for cp in first + passed:
        cp.wait_send()
    for cp in mine:
        cp.wait()


def _exchange_sibling(slots, *, name):
    n = len(slots)

    def body(*refs):
        x_refs, sib_refs = refs[:n], refs[n:2 * n]
        send_sems, recv_sems = refs[2 * n:]
        x, y, c = _peers()
        give = [pltpu.make_async_remote_copy(
            src_ref=x_refs[a].at[pl.ds(N_CHIP * (1 - c), N_CHIP)], dst_ref=sib_refs[a], send_sem=send_sems.at[a],
            recv_sem=recv_sems.at[a], device_id=(x, y, 1 - c), device_id_type=MESH) for a in range(n)]
        for cp in give:
            cp.start()
        for cp in give:
            cp.wait_recv()
        for cp in give:
            cp.wait_send()

    return list(pl.pallas_call(
        body, name=name, out_shape=[_S((N_CHIP,) + s.shape[1:], s.dtype) for s in slots], in_specs=[_ANY] * n, out_specs=[_ANY] * n,
        scratch_shapes=[pltpu.SemaphoreType.DMA((n,)), pltpu.SemaphoreType.DMA((n,))],
    )(*slots))


def _chip_sum(slots, sib, core, *, name):
    _, R, W = slots.shape
    tr = _pick(R, max(16, (1 << 20) // (4 * W)), align=16)

    def body(core_ref, x_ref, s_ref, o_ref):
        o_ref[...] = (x_ref[...].astype(f32) + s_ref[...].astype(f32)).astype(o_ref.dtype)

    blk = pl.BlockSpec((None, tr, W), lambda t, i, core_ref: (t, i, 0))
    return pl.pallas_call(
        body, name=name, out_shape=_S(sib.shape, slots.dtype),
        grid_spec=pltpu.PrefetchScalarGridSpec(
            num_scalar_prefetch=1, grid=(N_CHIP, R // tr),
            in_specs=[pl.BlockSpec((None, tr, W), lambda t, i, core_ref: (N_CHIP * core_ref[0] + t, i, 0)), blk], out_specs=blk),
        compiler_params=_cparams(("parallel", "parallel")),
    )(core, slots, sib)


def _exchange_chips(parts, *, name):
    n = len(parts)

    def body(*refs):
        copies = _chip_copies(refs[:n], refs[n:2 * n], *refs[2 * n:])
        _start_all(copies)
        _wait_all(copies)

    return list(pl.pallas_call(
        body, name=name, out_shape=_chip_out_shapes(parts), in_specs=[_ANY] * n, out_specs=[_ANY] * n,
        scratch_shapes=_chip_sems(n),
    )(*parts))


def _chip_out_shapes(parts):
    return [_S((N_CHIP - 1,) + p.shape[1:], p.dtype) for p in parts]


def _chip_sems(n):
    return [pltpu.SemaphoreType.DMA((3 * n,)), pltpu.SemaphoreType.DMA((3 * n,))]


def _chip_copies(p_refs, out_refs, send_sems, recv_sems):
    x, y, c = _peers()
    copies = []
    for j in range(1, N_CHIP):
        tx, ty = x ^ (j >> 1), y ^ (j & 1)
        for a in range(len(p_refs)):
            copies.append(pltpu.make_async_remote_copy(
                src_ref=p_refs[a].at[2 * tx + ty], dst_ref=out_refs[a].at[j - 1], send_sem=send_sems.at[3 * a + j - 1],
                recv_sem=recv_sems.at[3 * a + j - 1], device_id=(tx, ty, c), device_id_type=MESH))
    return copies


def _start_all(copies):
    for cp in copies:
        cp.start()


def _wait_all(copies):
    for cp in copies:
        cp.wait_recv()
    for cp in copies:
        cp.wait_send()


class _Carry(NamedTuple):
    ins: tuple = ()
    out_shapes: tuple = ()
    sems: tuple = ()
    start: Callable = None
    finish: Callable = None


_NO_CARRY = _Carry()


def _carry_chips(parts):
    return _Carry(tuple(parts), tuple(_chip_out_shapes(parts)), tuple(_chip_sems(len(parts))),
                  lambda i, o, s: _start_all(_chip_copies(i, o, *s)), lambda i, o, s: _wait_all(_chip_copies(i, o, *s)))


def _carry_gather(shards):
    n = len(shards)
    sems = (pltpu.SemaphoreType.DMA((7 * n,)), pltpu.SemaphoreType.DMA((7 * n,)), pltpu.SemaphoreType.DMA((n,)))
    return _Carry(tuple(shards), tuple(_S((N_DEV,) + s.shape, s.dtype) for s in shards), sems, _gather_start, _gather_finish)


def _carry_split(carry, refs, n_in, n_out):
    ci, co, cs = len(carry.ins), len(carry.out_shapes), len(carry.sems)
    refs = list(refs)
    own_in, c_in = refs[:n_in], refs[n_in:n_in + ci]
    own_out, c_out = refs[n_in + ci:n_in + ci + n_out], refs[n_in + ci + n_out:n_in + ci + n_out + co]
    rest = refs[n_in + ci + n_out + co:]
    own_scratch, c_sems = rest[:len(rest) - cs], rest[len(rest) - cs:]
    return own_in + own_out + own_scratch, c_in, c_out, c_sems


def _carry_start(carry, c_in, c_out, c_sems, first):
    if carry.ins:
        @pl.when(first)
        def _():
            carry.start(c_in, c_out, c_sems)


def _carry_finish(carry, c_in, c_out, c_sems, last):
    if carry.ins:
        @pl.when(last)
        def _():
            carry.finish(c_in, c_out, c_sems)


def _sum_slots(stack, *, name):
    n, R, W = stack.shape
    tr = _pick(R, 1024, align=8)

    def body(s_ref, o_ref):
        acc = s_ref[0]
        for k in range(1, n):
            acc = acc + s_ref[k]
        o_ref[...] = acc

    return pl.pallas_call(
        body, name=name, grid=(R // tr,), in_specs=[pl.BlockSpec((n, tr, W), lambda i: (0, i, 0))],
        out_specs=pl.BlockSpec((tr, W), lambda i: (i, 0)), out_shape=_S((R, W), f32), compiler_params=_cparams(("parallel",)),
    )(stack)


def _adamw_math(gv, wv, mv, vv):
    c1 = 1.0 / (1.0 - ADAM_B1 ** ADAM_STEP)
    c2 = 1.0 / (1.0 - ADAM_B2 ** ADAM_STEP)
    nm = ADAM_B1 * mv + (1.0 - ADAM_B1) * gv
    nv = ADAM_B2 * vv + (1.0 - ADAM_B2) * jnp.square(gv)
    return -ADAM_LR * ((nm * c1) / (jnp.sqrt(nv * c2) + ADAM_EPS) + ADAM_WD * wv), nm, nv


def _adamw(g, w, m, v, *, name):
    R, W = w.shape
    tr = _pick(R, max(8, (1 << 20) // (4 * W)), align=8)

    def body(g_ref, w_ref, m_ref, v_ref, d_ref, nm_ref, nv_ref):
        d_ref[...], nm_ref[...], nv_ref[...] = _adamw_math(g_ref[...], w_ref[...], m_ref[...], v_ref[...])

    sp = pl.BlockSpec((tr, W), lambda i: (i, 0))
    return pl.pallas_call(
        body, name=name, grid=(R // tr,), in_specs=[sp] * 4, out_specs=[sp] * 3, out_shape=[_S((R, W), f32)] * 3,
        compiler_params=_cparams(("parallel",)),
    )(g, w, m, v)


def _reduce_adamw(own, arrived, chip, w, m, v, *, name):
    n, R, W = arrived.shape
    tr = _pick(R, max(16, (1 << 20) // (4 * W)), align=16)

    def body(chip_ref, o_ref, p_ref, w_ref, m_ref, v_ref, g_ref, d_ref, nm_ref, nv_ref):
        gv = o_ref[...].astype(f32)
        for k in range(n):
            gv = gv + p_ref[k].astype(f32)
        g_ref[...] = gv
        d_ref[...], nm_ref[...], nv_ref[...] = _adamw_math(gv, w_ref[...], m_ref[...], v_ref[...])

    sp = pl.BlockSpec((tr, W), lambda i, chip_ref: (i, 0))
    return pl.pallas_call(
        body, name=name, out_shape=[_S((R, W), f32)] * 4,
        grid_spec=pltpu.PrefetchScalarGridSpec(
            num_scalar_prefetch=1, grid=(R // tr,),
            in_specs=[pl.BlockSpec((None, tr, W), lambda i, chip_ref: (chip_ref[0], i, 0)),
                      pl.BlockSpec((n, tr, W), lambda i, chip_ref: (0, i, 0))] + [sp] * 3, out_specs=[sp] * 4),
        compiler_params=_cparams(("parallel",)),
    )(chip, own, arrived, w, m, v)


def _pieces(seg_start, seg_len, shard_w):
    out, col = [], seg_start
    while col < seg_start + seg_len:
        k, a = divmod(col, shard_w)
        n = min(shard_w - a, seg_start + seg_len - col)
        out.append((k, a, col - seg_start, n))
        col += n
    return out


def _unshard_w_in(g, seg_lens, *, name):
    _, D, w = g.shape
    starts = [sum(seg_lens[:i]) for i in range(len(seg_lens))]
    widths = [max(n, LANES) for n in seg_lens]
    tm = _pick(D, 256, align=16)

    def body(g_ref, *o_refs):
        for o_ref, s0, n in zip(o_refs, starts, seg_lens):
            if n < o_ref.shape[1]:
                o_ref[...] = jnp.zeros_like(o_ref)
            for k, a, off, m in _pieces(s0, n, w):
                o_ref[:, off:off + m] = g_ref[k, :, a:a + m]

    return pl.pallas_call(
        body, name=name, grid=(D // tm,), in_specs=[pl.BlockSpec((N_DEV, tm, w), lambda i: (0, i, 0))],
        out_specs=[pl.BlockSpec((tm, wd), lambda i: (i, 0)) for wd in widths], out_shape=[_S((D, wd), g.dtype) for wd in widths],
        compiler_params=_cparams(("parallel",)),
    )(g)


def _unshard_pair(g1, g2, *, name):
    _, D, w = g1.shape
    tm = _pick(D, 256, align=16)

    def body(a_ref, b_ref, o_ref):
        for i, g_ref in enumerate((a_ref, b_ref)):
            for k in range(N_DEV):
                off = (i * N_DEV + k) * w
                o_ref[:, off:off + w] = g_ref[k]

    blk = pl.BlockSpec((N_DEV, tm, w), lambda i: (0, i, 0))
    return pl.pallas_call(
        body, name=name, grid=(D // tm,), in_specs=[blk, blk], out_specs=pl.BlockSpec((tm, 2 * N_DEV * w), lambda i: (i, 0)),
        out_shape=_S((D, 2 * N_DEV * w), g1.dtype), compiler_params=_cparams(("parallel",)),
    )(g1, g2)


def _reshard_pair(dw, *, name):
    D, w = dw.shape[0], dw.shape[1] // (2 * N_DEV)
    tm = _pick(D, 128, align=16)

    def body(g_ref, a_ref, b_ref):
        for i, o_ref in enumerate((a_ref, b_ref)):
            for k in range(N_DEV):
                off = (i * N_DEV + k) * w
                o_ref[_slot_of(k)] = g_ref[:, off:off + w].astype(o_ref.dtype)

    blk = pl.BlockSpec((N_DEV, tm, w), lambda i: (0, i, 0))
    return pl.pallas_call(
        body, name=name, grid=(D // tm,), in_specs=[pl.BlockSpec((tm, dw.shape[1]), lambda i: (i, 0))], out_specs=[blk, blk],
        out_shape=[_S((N_DEV, D, w), bf16)] * 2, compiler_params=_cparams(("parallel",)),
    )(dw)


def _reshard_w_in(grads, seg_lens, w, *, name):
    D = grads[0].shape[0]
    starts = [sum(seg_lens[:i]) for i in range(len(seg_lens))]
    tm = _pick(D, 128, align=16)

    def body(*refs):
        o_ref = refs[-1]
        for g_ref, s0, n in zip(refs[:-1], starts, seg_lens):
            for k, a, off, m in _pieces(s0, n, w):
                o_ref[_slot_of(k), :, a:a + m] = g_ref[:, off:off + m].astype(o_ref.dtype)

    return pl.pallas_call(
        body, name=name, grid=(D // tm,), in_specs=[pl.BlockSpec((tm, g.shape[1]), lambda i: (i, 0)) for g in grads],
        out_specs=pl.BlockSpec((N_DEV, tm, w), lambda i: (0, i, 0)), out_shape=_S((N_DEV, D, w), bf16),
        compiler_params=_cparams(("parallel",)),
    )(*grads)


def _pad_flat(a, mult):
    a = a.reshape(-1)
    n = -(-a.shape[0] // mult) * mult
    return a if n == a.shape[0] else jnp.pad(a, (0, n - a.shape[0]))


def _pad_cols(a, mult):
    n = -(-a.shape[1] // mult) * mult
    return a if n == a.shape[1] else jnp.pad(a, ((0, 0), (0, n - a.shape[1])))


def _block_diag(t):
    nblk, g, P, Q = t.shape
    eye = jnp.eye(g, dtype=t.dtype)
    return (t[:, :, :, None, :] * eye[None, :, None, :, None]).reshape(nblk, g * P, g * Q)


def _block_diag_t(w, P, Q):
    nblk = w.shape[0]
    g = w.shape[1] // P
    eye = jnp.eye(g, dtype=w.dtype)
    return (w.reshape(nblk, g, P, g, Q) * eye[None, :, None, :, None]).sum(axis=3)


_COLS = ("ffn1_w_gate", "ffn1_w_up", "ffn2_w_gate", "ffn2_w_up")
_ROWS = ("ffn1_w_down", "ffn2_w_down", "s5_w_glu", "w_proj_s5", "w_out", "w_proj_ssd")
_BIG = _COLS + _ROWS + ("w_in", "conv_w")
_SMALL = ("ffn1_norm", "mix_norm", "conv_b", "s5_A_re", "s5_A_im", "s5_log_dt", "s5_B_re", "s5_B_im", "s5_C_re", "s5_C_im",
          "s5_D", "s5_b_glu", "ssd_A_log", "ssd_dt_bias", "ssd_D", "ssd_norm", "b_gate", "ffn2_norm", "final_norm")
_WEIGHTS = ("ffn1_norm", "ffn1_w_gate", "ffn1_w_up", "ffn1_w_down", "mix_norm", "w_in", "conv_w", "conv_b", "s5_A_re", "s5_A_im",
            "s5_log_dt", "s5_B_re", "s5_B_im", "s5_C_re", "s5_C_im", "s5_D", "s5_w_glu", "s5_b_glu", "ssd_A_log", "ssd_dt_bias",
            "ssd_D", "ssd_norm", "w_proj_s5", "w_proj_ssd", "b_gate", "w_out", "ffn2_norm", "ffn2_w_gate", "ffn2_w_up",
            "ffn2_w_down", "final_norm")
def _with_carry(res, carry):
    return res if carry else (res, [])


def _ffn_fwd(x, n, wgu, wd_of, tag, carries=(None, None, None)):
    D = x.shape[1]
    h = _rows(_f_rmsnorm, [x], [n], [(D, bf16)], name=tag + "_norm")[0]
    ab, got0 = _with_carry(_mm(h, wgu, carry=carries[0], name=tag + "_gate_up"), carries[0])
    wd = wd_of(got0)
    F = wd.shape[0]
    (c,), got1 = _with_carry(_rows(_f_swiglu, [ab], [], [(F, bf16)], carry=carries[1], name=tag + "_act"), carries[1])
    y, got2 = _with_carry(_mm(c, wd, scale=0.5, add=x, carry=carries[2], name=tag + "_down"), carries[2])
    return y, (x, n, h, ab, c), (got0, got1, got2)


def _ffn_bwd(saved, wgu, wd, dy, tag, carry_after_dwd=None, carry_after_dwgu=None):
    x, n, h, ab, c = saved
    F = wd.shape[0]
    def act_bwd(dc, ab_):
        return jax.vjp(lambda t: _f_swiglu(t)[0], ab_)[1](dc)[0]

    dab = _mm(dy, wd, tb=True, scale=0.5, tm=EPI_TM, epi=(act_bwd, ab, 2 * F), out_dtype=bf16, name=tag + "_d_act")
    dwd = _mm(c, dy, ta=True, o_blk="m", o_slots=True, tm=F // 2, out_dtype=bf16, scale=0.5, name=tag + "_d_wdown")
    carry_w = carry_after_dwd(dwd) if carry_after_dwd else None
    dwgu, arr_w = _with_carry(_mm(h, dab, ta=True, carry=carry_w, name=tag + "_d_wgu"), carry_w)
    dwg, dwu = _reshard_pair(dwgu, name=tag + "_reshard_d_wgu")
    carry_h = carry_after_dwgu(dwg, dwu) if carry_after_dwgu else None
    dh, arr_h = _with_carry(_mm(dab, wgu, tb=True, carry=carry_h, name=tag + "_d_h"), carry_h)
    (dx,), (dn,) = _rows_bwd(_f_rmsnorm, [x], [n], [dh], name=tag + "_norm_bwd", want_rows=[0], adds={0: dy})
    return dx, dn, dwg, dwu, dwd, arr_w, arr_h


def kernel(x, ffn1_norm, ffn1_w_gate, ffn1_w_up, ffn1_w_down, mix_norm, w_in, conv_w, conv_b, s5_A_re, s5_A_im, s5_log_dt, s5_B_re, s5_B_im, s5_C_re, s5_C_im, s5_D, s5_w_glu, s5_b_glu, ssd_A_log, ssd_dt_bias, ssd_D, ssd_norm, w_proj_s5, w_proj_ssd, b_gate, w_out, ffn2_norm, ffn2_w_gate, ffn2_w_up, ffn2_w_down, final_norm, loss_target, m_ffn1_norm, m_ffn1_w_gate, m_ffn1_w_up, m_ffn1_w_down, m_mix_norm, m_w_in, m_conv_w, m_conv_b, m_s5_A_re, m_s5_A_im, m_s5_log_dt, m_s5_B_re, m_s5_B_im, m_s5_C_re, m_s5_C_im, m_s5_D, m_s5_w_glu, m_s5_b_glu, m_ssd_A_log, m_ssd_dt_bias, m_ssd_D, m_ssd_norm, m_w_proj_s5, m_w_proj_ssd, m_b_gate, m_w_out, m_ffn2_norm, m_ffn2_w_gate, m_ffn2_w_up, m_ffn2_w_down, m_final_norm, v_ffn1_norm, v_ffn1_w_gate, v_ffn1_w_up, v_ffn1_w_down, v_mix_norm, v_w_in, v_conv_w, v_conv_b, v_s5_A_re, v_s5_A_im, v_s5_log_dt, v_s5_B_re, v_s5_B_im, v_s5_C_re, v_s5_C_im, v_s5_D, v_s5_w_glu, v_s5_b_glu, v_ssd_A_log, v_ssd_dt_bias, v_ssd_D, v_ssd_norm, v_w_proj_s5, v_w_proj_ssd, v_b_gate, v_w_out, v_ffn2_norm, v_ffn2_w_gate, v_ffn2_w_up, v_ffn2_w_down, v_final_norm):
    P = dict(locals())
    T, D = x.shape[1], x.shape[2]
    x0, tgt = x[0], loss_target[0]
    sh = {k: P[k][0] for k in _BIG}

    send = {k: (sh[k] if k == "conv_w" else sh[k].astype(bf16)) for k in _BIG}
    W = {}

    def gather_in(keys):
        return _carry_gather([send[k] for k in keys])

    first_keys = ("ffn1_w_gate", "ffn1_w_up", "conv_w")
    W.update(zip(first_keys, _all_gather([send[k] for k in first_keys], name="gather_weights_first")))
    whole = lambda k: W[k].reshape(-1, D)
    conv_w_full = W["conv_w"].transpose(1, 0, 2).reshape(CONV_K, -1)

    d_inner = N_DEV * sh["w_proj_ssd"].shape[0]
    conv_dim = conv_w_full.shape[1]
    H = ssd_A_log.shape[1]
    G = (conv_dim - d_inner) // (2 * SSD_STATE)
    hpg = H // G
    nc = T // CHUNK
    Gs = D // S5_GROUP
    nblk = Gs // S5_GPB
    NS = Gs * S5_STATE
    seg_lens = (D, d_inner, conv_dim, H, 2 * D)

    cuts = [0, D // 3 // 16 * 16, D // 3 // 16 * 16 + 3 * D // 8 // 16 * 16, D]
    win_rows = [send["w_in"][a_:b_] for a_, b_ in zip(cuts[:-1], cuts[1:])]
    wgu1 = _unshard_pair(W["ffn1_w_gate"], W["ffn1_w_up"], name="unshard_ffn1_gate_up")

    def ffn1_down(got):
        W["ffn1_w_down"] = got[0]
        return whole("ffn1_w_down")

    x1, sv1, (got0, got1, got2) = _ffn_fwd(
        x0, ffn1_norm, wgu1, ffn1_down, "ffn1",
        carries=(_carry_gather([send["ffn1_w_down"], win_rows[0]]), _carry_gather([win_rows[1]]), _carry_gather([win_rows[2]])))
    ffn1_w = (wgu1, whole("ffn1_w_down"))
    W["w_in"] = jnp.concatenate([got0[1], got1[0], got2[0]], axis=1)
    w_u, w_z, w_xbc, w_dt, w_gl = _unshard_w_in(W["w_in"], seg_lens, name="unshard_w_in")
    h2 = _rows(_f_rmsnorm, [x1], [mix_norm], [(D, bf16)], name="mix_norm")[0]
    u_p = _mm(h2, w_u, o_seg=True, name="in_u")
    z = _mm(h2, w_z, name="in_z")
    xbc = _mm(h2, w_xbc, name="in_xbc")
    gl = _mm(h2, w_gl, name="in_gate")
    dtr = _mm(h2, w_dt, name="in_dt")

    rep = lambda a: jnp.repeat(a, S5_GROUP, axis=0)
    lr, li, ldt = s5_A_re[0], s5_A_im[0], s5_log_dt[0].reshape(Gs, 1)
    brt = s5_B_re[0].transpose(0, 2, 1).reshape(Gs * S5_GROUP, S5_STATE)
    bit = s5_B_im[0].transpose(0, 2, 1).reshape(Gs * S5_GROUP, S5_STATE)
    prep_args = (lr, li, ldt, rep(lr), rep(li), rep(ldt), brt, bit)
    ar, ai, bbrt, bbit = _s5_prep(prep_args, name="s5_prep")
    a_r, a_i = ar.reshape(1, NS), ai.reshape(1, NS)
    wb_r = _block_diag(bbrt.reshape(nblk, S5_GPB, S5_GROUP, S5_STATE)).astype(bf16)
    wb_i = _block_diag(bbit.reshape(nblk, S5_GPB, S5_GROUP, S5_STATE)).astype(bf16)
    c4r = s5_C_re[0].reshape(nblk, S5_GPB, S5_GROUP, S5_STATE).transpose(0, 1, 3, 2)
    c4i = s5_C_im[0].reshape(nblk, S5_GPB, S5_GROUP, S5_STATE).transpose(0, 1, 3, 2)
    wc_r, wc_i = _block_diag(c4r).astype(bf16), _block_diag(c4i).astype(bf16)
    mix_keys = ("s5_w_glu", "w_proj_s5", "w_proj_ssd", "w_out")
    sl_r, sl_i, p_r, p_i, got = _s5_local_scan(u_p, wb_r, wb_i, a_r, a_i, reverse=False, carry=gather_in(mix_keys), name="s5_scan")
    W.update(zip(mix_keys, got))
    w_glu, w_p5, w_pssd, w_o = whole("s5_w_glu"), whole("w_proj_s5"), whole("w_proj_ssd"), whole("w_out")
    c_r, c_i = _s5_carry(sl_r[T - NSEG:], sl_i[T - NSEG:], p_r, p_i, reverse=False, name="s5_carry")
    s_r, s_i, ylin = _s5_fix_out(sl_r, sl_i, a_r, a_i, c_r, c_i, wc_r, wc_i, name="s5_fix_out")
    g5 = _rows(_f_s5_post, [ylin, u_p], [s5_D], [(D, f32)], name="s5_gelu")[0]
    v5 = _mm(g5, w_glu, name="s5_glu_mm")
    o5 = _rows(_f_glu, [g5, v5], [s5_b_glu], [(D, bf16)], name="s5_glu")[0]
    p5 = _mm(o5, w_p5, a_seg=True, name="proj_s5")

    xc = _conv_fwd(xbc, conv_w_full, conv_b, name="conv")
    bias_p, alog_p = _pad_cols(ssd_dt_bias, LANES), _pad_cols(ssd_A_log, LANES)
    expand = (lax.broadcasted_iota(jnp.int32, (LANES, d_inner), 1) // HEADDIM
              == lax.broadcasted_iota(jnp.int32, (LANES, d_inner), 0)).astype(f32)
    dt_p, da_p, dt4, a4 = _rows(_f_dt_expand, [dtr], [bias_p, alog_p, expand],
                                [(LANES, f32), (LANES, f32), (d_inner, f32), (d_inner, f32)], name="ssd_dt")
    row_l = lambda a: a[:, :H].reshape(nc, CHUNK, G, hpg).transpose(2, 0, 3, 1).reshape(G, nc, hpg * CHUNK)
    ssd_in = (xc, dt4, a4, row_l(dt_p), row_l(da_p), jnp.repeat(ssd_D, HEADDIM, axis=1), _ssd_consts(hpg))
    ffn2_keys = ("ffn2_w_gate", "ffn2_w_up", "ffn2_w_down")
    y_ssd, hs, got = _ssd2_fwd(*ssd_in, d_inner=d_inner, carry=gather_in(ffn2_keys), name="ssd")
    W.update(zip(ffn2_keys, got))
    ffn2_w = (_unshard_pair(W["ffn2_w_gate"], W["ffn2_w_up"], name="unshard_ffn2_gate_up"), whole("ffn2_w_down"))
    yn = _rows(_f_gated_norm, [y_ssd, z], [ssd_norm], [(d_inner, bf16)], name="ssd_gated_norm")[0]
    pssd = _mm(yn, w_pssd, name="proj_ssd")

    merged = _rows(_f_merge, [gl, p5, pssd], [b_gate], [(D, bf16)], name="merge")[0]
    x2 = _mm(merged, w_o, add=x1, name="out_proj")
    x3, sv2, _ = _ffn_fwd(x2, ffn2_norm, ffn2_w[0], lambda _: ffn2_w[1], "ffn2")
    lossv, dx3, d_final = _loss_stage(x3, tgt, final_norm.reshape(1, D), name="loss")

    gw = {}
    gs = {"final_norm": d_final}
    slot_mm = lambda a_, b_, name, **kw: _mm(a_, b_, ta=True, o_blk="m", o_slots=True, out_dtype=bf16, name=name, **kw)
    core = lax.axis_index("c").astype(jnp.int32).reshape(1)
    chip = (2 * lax.axis_index("x") + lax.axis_index("y")).astype(jnp.int32).reshape(1)
    chip_sums, arrived = {}, {}

    def level1(keys, tag):
        sib = _exchange_sibling([gw[k] for k in keys], name="exchange_sibling_" + tag)
        for k, s_ in zip(keys, sib):
            chip_sums[k] = _chip_sum(gw[k], s_, core, name="chip_sum_" + k)
        return [chip_sums[k] for k in keys]

    dx2, gs["ffn2_norm"], gw["ffn2_w_gate"], gw["ffn2_w_up"], gw["ffn2_w_down"], _, _ = _ffn_bwd(sv2, *ffn2_w, dx3, "ffn2")

    dmerged = _mm(dx2, w_o, tb=True, name="d_merged")
    gw["w_out"] = slot_mm(merged, dx2, "d_w_out")
    (dgl, dp5, dpssd), (gs["b_gate"],) = _rows_bwd(_f_merge, [gl, p5, pssd], [b_gate], [dmerged], name="merge_bwd", want_rows=[0, 1, 2])

    dyn = _mm(dpssd, w_pssd, tb=True, name="d_yn")
    gw["w_proj_ssd"] = slot_mm(yn, dpssd, "d_w_proj_ssd")
    group_a = ("ffn2_w_gate", "ffn2_w_up", "ffn2_w_down", "w_out", "w_proj_ssd")
    parts_a = level1(group_a, "a")
    (dyssd, dz), (gs["ssd_norm"],) = _rows_bwd(_f_gated_norm, [y_ssd, z], [ssd_norm], [dyn], name="ssd_gated_norm_bwd", want_rows=[0, 1])
    dxs, dbm, dcm, ddtc, ddac, ddtw, ddaw, ddh = _ssd2_bwd(*ssd_in, hs, dyssd, d_inner=d_inner, name="ssd_bwd")

    def fold(col, row):
        col = col.reshape(T, G, LANES)[:, :, :hpg].reshape(T, H)
        row = row.reshape(G, nc, hpg, CHUNK).transpose(1, 3, 0, 2).reshape(T, H)
        return _pad_cols(col + row, LANES)

    (ddtr,), (dbias_p, dalog_p) = _rows_bwd(_f_dt, [dtr], [bias_p, alog_p], [fold(ddtc, ddtw), fold(ddac, ddaw)], name="ssd_dt_bwd", want_rows=[0])
    gs["ssd_dt_bias"], gs["ssd_A_log"], gs["ssd_D"] = dbias_p[:, :H], dalog_p[:, :H], ddh[:, 0, :hpg].reshape(1, H)
    dxbc, d_conv_w, gs["conv_b"], arr = _conv_bwd(
        xbc, conv_w_full, conv_b, [dxs, dbm, dcm], carry=_carry_chips(parts_a), name="conv_bwd")
    arrived.update(zip(group_a, arr))
    cwk = sh["conv_w"].shape[1]
    gw["conv_w"] = d_conv_w.reshape(CONV_K, N_CHIP, 2, cwk).transpose(2, 1, 0, 3).reshape(N_DEV, CONV_K, cwk)

    do5 = _mm(dp5, w_p5, tb=True, o_seg=True, name="d_o5")
    gw["w_proj_s5"] = slot_mm(o5, dp5, "d_w_proj_s5", a_seg=True)
    (dg5a, dv5), (gs["s5_b_glu"],) = _rows_bwd(_f_glu, [g5, v5], [s5_b_glu], [do5], name="s5_glu_bwd", want_rows=[0, 1])
    dg5 = _mm(dv5, w_glu, tb=True, add=dg5a, name="d_g5")
    gw["s5_w_glu"] = slot_mm(g5, dv5, "d_w_glu")
    (dylin, du_a), (gs["s5_D"],) = _rows_bwd(_f_s5_post, [ylin, u_p], [s5_D], [dg5], name="s5_gelu_bwd", want_rows=[0, 1])
    wct_r, wct_i = wc_r.transpose(0, 2, 1), -wc_i.transpose(0, 2, 1)
    ql_r, ql_i, _, _ = _s5_local_scan(dylin, wct_r, wct_i, a_r, -a_i, reverse=True, powers=False, name="s5_scan_bwd")
    cb_r, cb_i = _s5_carry(ql_r[:NSEG], ql_i[:NSEG], p_r, -p_i, reverse=True, name="s5_carry_bwd")
    tc = min(S5_TC, T)

    def before_blocks(s):
        last = s.reshape(T // tc, tc, NS)[:, tc - NSEG:, :]
        wrap = jnp.concatenate([jnp.zeros((1, 1, NS), f32), last[-1:, : NSEG - 1, :]], axis=1)
        return jnp.concatenate([wrap, last[:-1]], axis=0)

    du_p, dwb_r, dwb_i, dwc_r, dwc_i, d_ar, d_ai = _s5_fix_bwd(
        ql_r, ql_i, a_r, -a_i, cb_r, cb_i, s_r, s_i, before_blocks(s_r), before_blocks(s_i), u_p, dylin, du_a, wb_r, wb_i, name="s5_fix_bwd")
    unblk = lambda w: _block_diag_t(w, S5_GROUP, S5_STATE).reshape(Gs * S5_GROUP, S5_STATE)
    rsum = jnp.repeat(jnp.eye(Gs, dtype=f32), S5_GROUP, axis=1)
    d_lr, d_li, d_ldt, d_brt, d_bit = _s5_prep_bwd(
        prep_args, (d_ar.reshape(Gs, S5_STATE), d_ai.reshape(Gs, S5_STATE), unblk(dwb_r), unblk(dwb_i)), rsum, name="s5_prep_bwd")
    gs["s5_A_re"], gs["s5_A_im"], gs["s5_log_dt"] = d_lr, d_li, d_ldt.reshape(1, Gs)
    gs["s5_B_re"] = d_brt.reshape(Gs, S5_GROUP, S5_STATE).transpose(0, 2, 1)
    gs["s5_B_im"] = d_bit.reshape(Gs, S5_GROUP, S5_STATE).transpose(0, 2, 1)
    gs["s5_C_re"] = _block_diag_t(dwc_r, S5_STATE, S5_GROUP).transpose(0, 1, 3, 2).reshape(Gs, S5_GROUP, S5_STATE)
    gs["s5_C_im"] = _block_diag_t(dwc_i, S5_STATE, S5_GROUP).transpose(0, 1, 3, 2).reshape(Gs, S5_GROUP, S5_STATE)

    d_w_in = [_mm(h2, du_p, ta=True, b_seg=True, name="d_w_u"), _mm(h2, dz, ta=True, name="d_w_z"), _mm(h2, dxbc, ta=True, name="d_w_xbc"),
              _mm(h2, ddtr, ta=True, name="d_w_dt"), _mm(h2, dgl, ta=True, name="d_w_gate")]
    gw["w_in"] = _reshard_w_in(d_w_in, seg_lens, sh["w_in"].shape[1], name="reshard_d_w_in")
    group_c = ("w_proj_s5", "s5_w_glu", "conv_w")
    parts_c = level1(group_c + ("w_in",), "c")
    c1 = int(D * 0.45) // 16 * 16
    c2 = c1 + D // 4 // 16 * 16
    win = [parts_c[3][:, :c1], parts_c[3][:, c1:c2], parts_c[3][:, c2:]]
    dh2 = _mm(du_p, w_u, tb=True, a_seg=True, name="d_h2_u")
    dh2, arr = _mm(dz, w_z, tb=True, add=dh2, carry=_carry_chips(parts_c[:3]), name="d_h2_z")
    arrived.update(zip(group_c, arr))
    dh2, (arr0,) = _mm(dxbc, w_xbc, tb=True, add=dh2, carry=_carry_chips([win[0]]), name="d_h2_xbc")
    dh2, (arr1,) = _mm(dgl, w_gl, tb=True, add=dh2, carry=_carry_chips([win[1]]), name="d_h2_gate")
    dh2 = _mm(ddtr, w_dt, tb=True, add=dh2, name="d_h2_dt")
    (dx1,), (gs["mix_norm"],) = _rows_bwd(_f_rmsnorm, [x1], [mix_norm], [dh2], name="mix_norm_bwd", want_rows=[0], adds={0: dx2})

    def carry_ffn1_down(dwd):
        gw["ffn1_w_down"] = dwd
        return _carry_chips(level1(("ffn1_w_down",), "d") + [win[2]])

    def carry_ffn1_gate_up(dwg, dwu):
        gw["ffn1_w_gate"], gw["ffn1_w_up"] = dwg, dwu
        return _carry_chips(level1(("ffn1_w_gate", "ffn1_w_up"), "e"))

    dx0, gs["ffn1_norm"], _, _, _, arr_w, arr_h = _ffn_bwd(
        sv1, *ffn1_w, dx1, "ffn1", carry_after_dwd=carry_ffn1_down, carry_after_dwgu=carry_ffn1_gate_up)
    arrived["ffn1_w_down"], arr2 = arr_w
    arrived["ffn1_w_gate"], arrived["ffn1_w_up"] = arr_h
    arrived["w_in"] = jnp.concatenate([arr0, arr1, arr2], axis=1)

    small_shapes = {k: (P[k][0].shape if P[k].ndim > 1 else P[k].shape) for k in _SMALL}
    pack = lambda d: jnp.concatenate([_pad_flat(d[k], TILE_ELEMS) for k in _SMALL]).reshape(-1, LANES)
    gsmall = _sum_slots(_all_gather([pack(gs)], name="gather_small_grads")[0], name="sum_small_grads")
    snum = {k: math.prod(small_shapes[k]) for k in _SMALL}
    ssz = {k: -(-snum[k] // TILE_ELEMS) * TILE_ELEMS for k in _SMALL}

    grads, delta, new_m, new_v = {}, {}, {}, {}
    for k in _BIG:
        grads[k], delta[k], new_m[k], new_v[k] = _reduce_adamw(
            chip_sums[k], arrived[k], chip, P[k][0], P["m_" + k][0], P["v_" + k][0], name="adamw_" + k)
    d_s, m_s, v_s = _adamw(gsmall, pack({k: P[k] for k in _SMALL}), pack({k: P["m_" + k] for k in _SMALL}),
                           pack({k: P["v_" + k] for k in _SMALL}), name="adamw_small")
    off = 0
    gflat, dflat, mflat, vflat = gsmall.reshape(-1), d_s.reshape(-1), m_s.reshape(-1), v_s.reshape(-1)
    for k in _SMALL:
        n = snum[k]
        grads[k], delta[k], new_m[k], new_v[k] = (a[off:off + n] for a in (gflat, dflat, mflat, vflat))
        off += ssz[k]

    loss = lax.psum(lossv[0, 0], ("x", "y", "c"))
    out = [loss, dx0.reshape(x.shape)]
    for d in (grads, delta, new_m, new_v):
        out += [d[k].reshape(P[k].shape) for k in _WEIGHTS]
    return tuple(out)
```

```python
import math
from typing import Callable, NamedTuple

import jax
import jax.numpy as jnp
from jax import lax
from jax.experimental import pallas as pl
from jax.experimental.pallas import tpu as pltpu

f32 = jnp.float32
bf16 = jnp.bfloat16
_S = jax.ShapeDtypeStruct

EPS = 1e-6
S5_GROUP = 16
S5_STATE = 64
HEADDIM = 64
SSD_STATE = 128
CHUNK = 64
CONV_K = 4
NSEG = 8
S5_GPB = 16
N_DEV = 8
LANES = 128
TILE_ELEMS = 8 * LANES

ADAM_LR = 0.001
ADAM_B1 = 0.9
ADAM_B2 = 0.999
ADAM_EPS = 1e-08
ADAM_WD = 0.01
ADAM_STEP = 10

VMEM_LIMIT = 56 * 1024 * 1024
MM_FULL_K = 3072
MM_MAX_TN = 3072
EPI_TM = 128
MESH = pl.DeviceIdType.MESH


def _cparams(sem=None):
    return pltpu.CompilerParams(dimension_semantics=sem, vmem_limit_bytes=VMEM_LIMIT)


def _pick(dim, pref, align=LANES):
    best = None
    t = align
    while t <= min(dim, pref):
        if dim % t == 0:
            best = t
        t += align
    return best or dim


def _slot_of(k):
    return (k & 1) * (N_DEV // 2) + (k >> 1)


def _mm(a, b, *, name, ta=False, tb=False, a_blk=None, b_blk=None, o_blk=None, o_slots=False, a_seg=False, b_seg=False,
        o_seg=False, tm=None, out_dtype=f32, scale=1.0, add=None, epi=None, carry=None):
    a2, b2 = a.shape[-2:], b.shape[-2:]
    Ma, Ka = (a2[1], a2[0]) if ta else a2
    Kb, Nb = (b2[1], b2[0]) if tb else b2
    M = Ma * (a.shape[0] if a_blk == "m" else 1)
    K = Ka * (a.shape[0] if a_blk == "k" else 1)
    N = Nb * (b.shape[0] if b_blk == "n" else 1)
    assert K == Kb * (b.shape[0] if b_blk == "k" else 1), (a.shape, b.shape, ta, tb, a_blk, b_blk)
    assert (a.ndim == 3) == (a_blk is not None) and (b.ndim == 3) == (b_blk is not None)
    tm = Ma if a_blk == "m" else (tm or _pick(M, 512))
    tn = Nb if b_blk == "n" else _pick(N, MM_MAX_TN)
    if a_blk == "k" or b_blk == "k":
        tk = Ka if a_blk == "k" else Kb
        assert tk == (Kb if b_blk == "k" else tk)
    else:
        tk = K if K <= MM_FULL_K else _pick(K, 1024 if ta else MM_FULL_K)
    if (a_seg and not ta) or o_seg:
        tm = M // NSEG
    if (a_seg and ta) or b_seg:
        tk = K // NSEG
    gm, gn, nk = M // tm, N // tn, K // tk
    assert not (add is not None and (o_seg or o_blk)) and not (o_blk and o_seg)

    if a_seg:
        assert a.ndim == 2
        a = a.reshape(a.shape[0] // NSEG, NSEG * a.shape[1])
        if ta:
            a_spec = pl.BlockSpec((tk, tm), lambda i, j, k: (0, k * (Ma // tm) + i))
        else:
            a_spec = pl.BlockSpec((tm, tk), lambda i, j, k: (0, i * (Ka // tk) + k))
    elif a.ndim == 3:
        lead = (lambda i, k: i) if a_blk == "m" else (lambda i, k: k)
        if ta:
            a_spec = pl.BlockSpec((None, tk, tm), lambda i, j, k: (lead(i, k), 0 if a_blk == "k" else k, 0 if a_blk == "m" else i))
        else:
            a_spec = pl.BlockSpec((None, tm, tk), lambda i, j, k: (lead(i, k), 0 if a_blk == "m" else i, 0 if a_blk == "k" else k))
    else:
        a_spec = pl.BlockSpec((tk, tm), lambda i, j, k: (k, i)) if ta else pl.BlockSpec((tm, tk), lambda i, j, k: (i, k))
    if b_seg:
        assert b.ndim == 2 and not tb
        b = b.reshape(b.shape[0] // NSEG, NSEG * b.shape[1])
        b_spec = pl.BlockSpec((tk, tn), lambda i, j, k: (0, k * (Nb // tn) + j))
    elif b.ndim == 3:
        lead = (lambda j, k: j) if b_blk == "n" else (lambda j, k: k)
        if tb:
            b_spec = pl.BlockSpec((None, tn, tk), lambda i, j, k: (lead(j, k), 0 if b_blk == "n" else j, 0 if b_blk == "k" else k))
        else:
            b_spec = pl.BlockSpec((None, tk, tn), lambda i, j, k: (lead(j, k), 0 if b_blk == "k" else k, 0 if b_blk == "n" else j))
    else:
        b_spec = pl.BlockSpec((tn, tk), lambda i, j, k: (j, k)) if tb else pl.BlockSpec((tk, tn), lambda i, j, k: (k, j))
    slot = _slot_of if o_slots else (lambda k: k)
    if o_blk == "n":
        assert gn == N_DEV or not o_slots
        o_shape, o_spec = (gn, M, tn), pl.BlockSpec((None, tm, tn), lambda i, j, k: (slot(j), i, 0))
    elif o_blk == "m" and o_slots and gm < N_DEV:
        rs = M // N_DEV
        per_tile = tm // rs
        assert per_tile % 2 == 0 and tm % rs == 0
        o_shape = (2, N_CHIP, rs, N)
        o_spec = pl.BlockSpec((2, per_tile // 2, rs, tn), lambda i, j, k: (0, i, 0, j))
    elif o_blk == "m":
        assert gm == N_DEV or not o_slots
        o_shape, o_spec = (gm, tm, N), pl.BlockSpec((None, tm, tn), lambda i, j, k: (slot(i), 0, j))
    elif o_seg:
        o_shape, o_spec = (tm, NSEG * N), pl.BlockSpec((tm, tn), lambda i, j, k: (0, i * (N // tn) + j))
    else:
        o_shape, o_spec = (M, N), pl.BlockSpec((tm, tn), lambda i, j, k: (i, j))
    dims = (((0 if ta else 1,), (1 if tb else 0,)), ((), ()))
    if epi is not None:
        assert gn == 1 and add is None and o_blk is None and not o_seg
        epi_fn, add, epi_w = epi
        o_shape, o_spec = (M, epi_w), pl.BlockSpec((tm, epi_w), lambda i, j, k: (i, 0))
    has_add = add is not None
    add_spec = pl.BlockSpec((tm, add.shape[1]), lambda i, j, k: (i, 0)) if epi is not None else o_spec

    carry = carry or _NO_CARRY
    n_in = 2 + has_add

    def body(*refs):
        own, c_in, c_out, c_sems = _carry_split(carry, refs, n_in, 1)
        a_ref, b_ref = own[0], own[1]
        add_ref = own[2] if has_add else None
        o_ref, acc_ref = own[-2], own[-1]
        i, j, k = pl.program_id(0), pl.program_id(1), pl.program_id(2)
        _carry_start(carry, c_in, c_out, c_sems, (i == 0) & (j == 0) & (k == 0))

        @pl.when(k == 0)
        def _():
            acc_ref[...] = jnp.zeros_like(acc_ref)

        acc_ref[...] += lax.dot_general(a_ref[...].astype(bf16), b_ref[...].astype(bf16), dims, preferred_element_type=f32)

        @pl.when(k == nk - 1)
        def _():
            r = acc_ref[...] * scale
            if epi is not None:
                r = epi_fn(r, add_ref[...].astype(f32))
            elif has_add:
                r = r + add_ref[...].astype(f32)
            if len(o_shape) == 4:
                rs = o_shape[2]
                for chip_l in range(o_ref.shape[1]):
                    for core in range(2):
                        dev = 2 * chip_l + core
                        o_ref[core, chip_l] = r[dev * rs:(dev + 1) * rs].astype(out_dtype)
            else:
                o_ref[...] = r.astype(out_dtype)

        _carry_finish(carry, c_in, c_out, c_sems, (i == gm - 1) & (j == gn - 1) & (k == nk - 1))

    ins = [a, b] + ([add] if has_add else []) + list(carry.ins)
    in_specs = [a_spec, b_spec] + ([add_spec] if has_add else []) + [_ANY] * len(carry.ins)
    res = pl.pallas_call(
        body, name=name, grid=(gm, gn, nk), in_specs=in_specs, out_specs=[o_spec] + [_ANY] * len(carry.out_shapes),
        out_shape=[_S(o_shape, out_dtype)] + list(carry.out_shapes),
        scratch_shapes=[pltpu.VMEM((tm, tn), f32)] + list(carry.sems),
        compiler_params=_cparams(("arbitrary",) * 3 if carry.ins else ("parallel", "parallel", "arbitrary")),
    )(*ins)
    out = res[0]
    if len(o_shape) == 4:
        out = out.reshape(N_DEV, o_shape[2], N)
    elif o_seg:
        out = out.reshape(M, N)
    return (out, list(res[1:])) if carry.ins else out


def _row_tile(T, widths):
    budget = 6 * 1024 * 1024
    tb = max(16, budget // (4 * sum(widths)))
    return _pick(T, tb, align=16)


def _rows(fn, rows, params, outs, *, name, carry=None):
    T = rows[0].shape[0]
    nr, npar = len(rows), len(params)
    tb = _row_tile(T, [r.shape[1] for r in rows] + [w for w, _ in outs])
    carry = carry or _NO_CARRY

    def body(*refs):
        own, c_in, c_out, c_sems = _carry_split(carry, refs, nr + npar, len(outs))
        _carry_start(carry, c_in, c_out, c_sems, pl.program_id(0) == 0)
        ins = [r[...].astype(f32) for r in own[: nr + npar]]
        res = fn(*ins)
        for o_ref, r in zip(own[nr + npar:], res):
            o_ref[...] = r.astype(o_ref.dtype)
        _carry_finish(carry, c_in, c_out, c_sems, pl.program_id(0) == T // tb - 1)

    in_specs = [pl.BlockSpec((tb, r.shape[1]), lambda i: (i, 0)) for r in rows]
    in_specs += [pl.BlockSpec(p.shape, lambda i: (0, 0)) for p in params]
    out_specs = [pl.BlockSpec((tb, w), lambda i: (i, 0)) for w, _ in outs]
    res = pl.pallas_call(
        body, name=name, grid=(T // tb,), in_specs=in_specs + [_ANY] * len(carry.ins),
        out_specs=out_specs + [_ANY] * len(carry.out_shapes), out_shape=[_S((T, w), d) for w, d in outs] + list(carry.out_shapes),
        scratch_shapes=list(carry.sems), compiler_params=_cparams(("arbitrary",) if carry.ins else ("parallel",)),
    )(*rows, *params, *carry.ins)
    return (tuple(res[:len(outs)]), list(res[len(outs):])) if carry.ins else tuple(res)


def _rows_bwd(fn, rows, params, cots, *, name, want_rows, row_dtypes=None, adds=None):
    T = rows[0].shape[0]
    nr, npar, nc = len(rows), len(params), len(cots)
    adds = adds or {}
    add_idx = sorted(adds)
    row_dtypes = row_dtypes or {}
    widths = [r.shape[1] for r in rows] + [c.shape[1] for c in cots] + [rows[i].shape[1] for i in want_rows]
    tb = _row_tile(T, widths)

    def body(*refs):
        ins = [r[...].astype(f32) for r in refs[: nr + npar]]
        cot = tuple(r[...].astype(f32) for r in refs[nr + npar: nr + npar + nc])
        add_refs = refs[nr + npar + nc: nr + npar + nc + len(add_idx)]
        out_refs = refs[nr + npar + nc + len(add_idx):]
        _, vjp = jax.vjp(lambda *a: tuple(fn(*a)), *ins)
        g = vjp(cot)
        for o_ref, i in zip(out_refs[: len(want_rows)], want_rows):
            r = g[i]
            if i in adds:
                r = r + add_refs[add_idx.index(i)][...].astype(f32)
            o_ref[...] = r.astype(o_ref.dtype)
        first = pl.program_id(0) == 0
        for o_ref, gp in zip(out_refs[len(want_rows):], g[nr:]):
            @pl.when(first)
            def _(o_ref=o_ref):
                o_ref[...] = jnp.zeros_like(o_ref)

            o_ref[...] += gp

    in_specs = [pl.BlockSpec((tb, r.shape[1]), lambda i: (i, 0)) for r in rows]
    in_specs += [pl.BlockSpec(p.shape, lambda i: (0, 0)) for p in params]
    in_specs += [pl.BlockSpec((tb, c.shape[1]), lambda i: (i, 0)) for c in cots]
    in_specs += [pl.BlockSpec((tb, adds[i].shape[1]), lambda i_: (i_, 0)) for i in add_idx]
    out_specs = [pl.BlockSpec((tb, rows[i].shape[1]), lambda i_: (i_, 0)) for i in want_rows]
    out_specs += [pl.BlockSpec(p.shape, lambda i: (0, 0)) for p in params]
    out_shape = [_S(rows[i].shape, row_dtypes.get(i, f32)) for i in want_rows] + [_S(p.shape, f32) for p in params]
    res = pl.pallas_call(
        body, name=name, grid=(T // tb,), in_specs=in_specs, out_specs=out_specs, out_shape=out_shape,
        compiler_params=_cparams(("arbitrary",)),
    )(*rows, *params, *cots, *[adds[i] for i in add_idx])
    return list(res[: len(want_rows)]), list(res[len(want_rows):])


def _f_rmsnorm(x, g):
    return (x * lax.rsqrt(jnp.mean(x * x, axis=-1, keepdims=True) + EPS) * g,)


def _f_swiglu(ab):
    F = ab.shape[1] // 2
    return (jax.nn.silu(ab[:, :F]) * ab[:, F:],)


def _f_s5_post(y, u, d):
    return (jax.nn.gelu(y + d * u),)


def _f_glu(g, v, b):
    return (g * jax.nn.sigmoid(v + b),)


def _f_gated_norm(y, z, w):
    return _f_rmsnorm(y * jax.nn.silu(z), w)


def _f_merge(gl, p5, pssd, b):
    D = p5.shape[1]
    gates = jax.nn.sigmoid(gl + b)
    return (gates[:, :D] * p5 + gates[:, D:] * pssd,)


def _f_dt(dtr, bias, a_log):
    dt = jax.nn.softplus(dtr + bias)
    return dt, dt * (-jnp.exp(a_log))


def _f_dt_expand(dtr, bias, a_log, e):
    dt, a = _f_dt(dtr, bias, a_log)
    return dt, a, _doth(dt, e), _doth(a, e)


def _loss_stage(x, tgt, g, *, name):
    T, D = x.shape
    tb = _row_tile(T, [D, D, D])

    def f(xb, gb, tb_):
        y = _f_rmsnorm(xb, gb)[0]
        return 0.5 * jnp.sum(jnp.mean(jnp.square(y - tb_), axis=-1, keepdims=True), axis=0, keepdims=True)

    def body(x_ref, t_ref, g_ref, l_ref, dx_ref, dg_ref):
        tv = t_ref[...]
        val, vjp = jax.vjp(lambda a, b: f(a, b, tv), x_ref[...], g_ref[...])
        dx, dg = vjp(jnp.ones((1, 1), f32))
        dx_ref[...] = dx

        @pl.when(pl.program_id(0) == 0)
        def _():
            l_ref[...] = jnp.zeros_like(l_ref)
            dg_ref[...] = jnp.zeros_like(dg_ref)

        l_ref[...] += jnp.broadcast_to(val, l_ref.shape)
        dg_ref[...] += dg

    row = pl.BlockSpec((tb, D), lambda i: (i, 0))
    par = pl.BlockSpec((1, D), lambda i: (0, 0))
    return pl.pallas_call(
        body, name=name, grid=(T // tb,), in_specs=[row, row, par],
        out_specs=[pl.BlockSpec((1, LANES), lambda i: (0, 0)), row, par],
        out_shape=[_S((1, LANES), f32), _S((T, D), f32), _S((1, D), f32)], compiler_params=_cparams(("arbitrary",)),
    )(x, tgt, g)


CONV_R = 64
HALO = 8


def _conv_shifts_down(ref, t):
    if isinstance(t, int) and t == 0:
        cur = ref[0:CONV_R, :]
        row = lax.broadcasted_iota(jnp.int32, cur.shape, 0)
        return [cur] + [jnp.where(row >= s, pltpu.roll(cur, s, axis=0), 0.0) for s in range(1, CONV_K)]
    win = ref[pl.ds(pl.multiple_of(t * CONV_R - HALO, HALO), CONV_R + HALO), :]
    return [win[HALO:]] + [pltpu.roll(win, s, axis=0)[HALO:] for s in range(1, CONV_K)]


def _conv_shifts_up(ref, t):
    if isinstance(t, int):
        cur = ref[t * CONV_R:(t + 1) * CONV_R, :]
        row = lax.broadcasted_iota(jnp.int32, cur.shape, 0)
        return [cur] + [jnp.where(row < CONV_R - s, pltpu.roll(cur, CONV_R - s, axis=0), 0.0) for s in range(1, CONV_K)]
    win = ref[pl.ds(pl.multiple_of(t * CONV_R, HALO), CONV_R + HALO), :]
    return [win[:CONV_R]] + [pltpu.roll(win, CONV_R + HALO - s, axis=0)[:CONV_R] for s in range(1, CONV_K)]


def _conv_pre(shifted, w, b):
    pre = b
    for k in range(CONV_K):
        pre = pre + w[k:k + 1, :] * shifted[CONV_K - 1 - k]
    return pre


def _conv_fwd(x, w, b, *, name):
    T, C = x.shape
    cb = _pick(C, 256)

    def body(x_ref, w_ref, b_ref, o_ref):
        xv = x_ref[...]
        row = lax.broadcasted_iota(jnp.int32, xv.shape, 0)
        shifted = [xv] + [jnp.where(row >= s, pltpu.roll(xv, s, axis=0), 0.0) for s in range(1, CONV_K)]
        o_ref[...] = jax.nn.silu(_conv_pre(shifted, w_ref[...], b_ref[...]))

    col = pl.BlockSpec((T, cb), lambda j: (0, j))
    return pl.pallas_call(
        body, name=name, grid=(C // cb,), in_specs=[col, pl.BlockSpec((CONV_K, cb), lambda j: (0, j)), pl.BlockSpec((1, cb), lambda j: (0, j))],
        out_specs=col, out_shape=_S((T, C), f32), compiler_params=_cparams(("parallel",)),
    )(x, w, b)


def _conv_bwd(x, w, b, dy, *, name, carry=None):
    T, C = x.shape
    cb = _pick(C, 128)
    carry = carry or _NO_CARRY
    ends = []
    for d in dy:
        ends.append((ends[-1] if ends else 0) + d.shape[1] // cb)
    assert ends[-1] == C // cb and all(d.shape[1] % cb == 0 for d in dy)
    npc = len(dy)
    n_tiles = T // CONV_R

    def body(*refs):
        own, c_in, c_out, c_sems = _carry_split(carry, refs, 3 + npc, 3)
        x_ref, w_ref, b_ref = own[:3]
        dy_refs, (dx_ref, dw_ref, db_ref, dpre_ref) = own[3:3 + npc], own[3 + npc:]
        _carry_start(carry, c_in, c_out, c_sems, pl.program_id(0) == 0)
        j = pl.program_id(0)
        wv, bv = w_ref[...], b_ref[...]

        def fold8(v):
            return jnp.sum(v.reshape(CONV_R // 8, 8, cb), axis=0)

        def first_pass(t, acc):
            rows = slice(0, CONV_R) if isinstance(t, int) else pl.ds(pl.multiple_of(t * CONV_R, CONV_R), CONV_R)
            shifted = _conv_shifts_down(x_ref, t)
            pre = _conv_pre(shifted, wv, bv)
            dyv = dy_refs[-1][rows, :]
            for p in range(npc - 2, -1, -1):
                dyv = jnp.where(j < ends[p], dy_refs[p][rows, :], dyv)
            sg = jax.nn.sigmoid(pre)
            dpre = dyv * sg * (1.0 + pre * (1.0 - sg))
            dpre_ref[rows, :] = dpre
            return tuple(acc[k] + fold8(dpre * shifted[CONV_K - 1 - k]) for k in range(CONV_K)) + (acc[CONV_K] + fold8(dpre),)

        zero = jnp.zeros((8, cb), f32)
        acc = lax.fori_loop(1, n_tiles, first_pass, first_pass(0, (zero,) * (CONV_K + 1)), unroll=3)
        for k in range(CONV_K):
            dw_ref[k:k + 1, :] = jnp.sum(acc[k], axis=0, keepdims=True)
        db_ref[...] = jnp.sum(acc[CONV_K], axis=0, keepdims=True)

        def dx_of(t):
            up = _conv_shifts_up(dpre_ref, t)
            dx = wv[CONV_K - 1:CONV_K, :] * up[0]
            for k in range(CONV_K - 1):
                dx = dx + wv[k:k + 1, :] * up[CONV_K - 1 - k]
            return dx

        def second_pass(t, c):
            dx_ref[pl.ds(pl.multiple_of(t * CONV_R, CONV_R), CONV_R), :] = dx_of(t)
            return c

        lax.fori_loop(0, n_tiles - 1, second_pass, 0, unroll=3)
        dx_ref[(n_tiles - 1) * CONV_R:, :] = dx_of(n_tiles - 1)
        _carry_finish(carry, c_in, c_out, c_sems, pl.program_id(0) == C // cb - 1)

    col = pl.BlockSpec((T, cb), lambda j: (0, j))
    wsp = pl.BlockSpec((CONV_K, cb), lambda j: (0, j))
    bsp = pl.BlockSpec((1, cb), lambda j: (0, j))
    starts = [0] + ends[:-1]
    dy_specs = [pl.BlockSpec((T, cb), lambda j, s=s, e=e: (0, jnp.clip(j, s, e - 1) - s)) for s, e in zip(starts, ends)]
    res = pl.pallas_call(
        body, name=name, grid=(C // cb,), in_specs=[col, wsp, bsp] + dy_specs + [_ANY] * len(carry.ins),
        out_specs=[col, wsp, bsp] + [_ANY] * len(carry.out_shapes),
        out_shape=[_S((T, C), f32), _S((CONV_K, C), f32), _S((1, C), f32)] + list(carry.out_shapes),
        scratch_shapes=[pltpu.VMEM((T, cb), f32)] + list(carry.sems),
        compiler_params=_cparams(("arbitrary",) if carry.ins else ("parallel",)),
    )(x, w, b, *dy, *carry.ins)
    return (*res[:3], list(res[3:]))


def _f_s5_prep(lr, li, ldt, lrb, lib, ldtb, brt, bit):
    def disc(lr_, li_, ldt_):
        dt = jnp.exp(ldt_)
        mag = jnp.exp(lr_ * dt)
        ar, ai = mag * jnp.cos(li_ * dt), mag * jnp.sin(li_ * dt)
        den = lr_ * lr_ + li_ * li_
        cr = ((ar - 1.0) * lr_ + ai * li_) / den
        ci = (ai * lr_ - (ar - 1.0) * li_) / den
        return ar, ai, cr, ci

    ar, ai, _, _ = disc(lr, li, ldt)
    _, _, cr, ci = disc(lrb, lib, ldtb)
    return ar, ai, cr * brt - ci * bit, cr * bit + ci * brt


def _s5_prep(args, *, name):
    G, N = args[0].shape
    GM = args[3].shape[0]

    def body(*refs):
        res = _f_s5_prep(*[r[...] for r in refs[:8]])
        for o, r in zip(refs[8:], res):
            o[...] = r

    return pl.pallas_call(body, name=name, out_shape=[_S((G, N), f32)] * 2 + [_S((GM, N), f32)] * 2)(*args)


def _s5_prep_bwd(args, cots, rsum, *, name):
    G, N = args[0].shape
    GM = args[3].shape[0]

    def body(*refs):
        ins = [r[...] for r in refs[:8]]
        cot = tuple(r[...] for r in refs[8:12])
        rs = refs[12][...]
        _, vjp = jax.vjp(_f_s5_prep, *ins)
        g = vjp(cot)
        fold = lambda v: jnp.dot(rs, v, preferred_element_type=f32, precision=lax.Precision.HIGHEST)
        o = refs[13:]
        o[0][...] = g[0] + fold(g[3])
        o[1][...] = g[1] + fold(g[4])
        o[2][...] = g[2] + fold(jnp.broadcast_to(g[5], (GM, LANES)))[:, 0:1]
        o[3][...] = g[6]
        o[4][...] = g[7]

    return pl.pallas_call(
        body, name=name, out_shape=[_S((G, N), f32), _S((G, N), f32), _S((G, 1), f32), _S((GM, N), f32), _S((GM, N), f32)],
    )(*args, *cots, rsum)


S5_TC = 512


def _s5_local_scan(src, w_r, w_i, a_r, a_i, *, reverse, name, carry=None, powers=True):
    T, C = src.shape
    nblk, cb, sb = w_r.shape
    NS = nblk * sb
    tc = min(S5_TC, T)
    nT, nt = T // tc, tc // NSEG
    tmap = (lambda i: nT - 1 - i) if reverse else (lambda i: i)

    carry = carry or _NO_CARRY

    def body(*refs):
        own, c_in, c_out, c_sems = _carry_split(carry, refs, 5, 4)
        u_ref, wr_ref, wi_ref, ar_ref, ai_ref, sr_ref, si_ref, pr_ref, pi_ref, st_r, st_i, pw_r, pw_i = own
        _carry_start(carry, c_in, c_out, c_sems, (pl.program_id(0) == 0) & (pl.program_id(1) == 0))

        @pl.when(pl.program_id(1) == 0)
        def _():
            st_r[...] = jnp.zeros_like(st_r)
            st_i[...] = jnp.zeros_like(st_i)
            pw_r[...] = jnp.ones_like(pw_r)
            pw_i[...] = jnp.zeros_like(pw_i)

        u = u_ref[...].astype(bf16)
        sr_ref[...] = jnp.dot(u, wr_ref[...], preferred_element_type=f32)
        si_ref[...] = jnp.dot(u, wi_ref[...], preferred_element_type=f32)
        ar = jnp.broadcast_to(ar_ref[...], (NSEG, sb))
        ai = jnp.broadcast_to(ai_ref[...], (NSEG, sb))

        def step(k, c):
            cr, ci, qr, qi = c
            kk = (nt - 1 - k) if reverse else k
            rows = pl.ds(pl.multiple_of(kk * NSEG, NSEG), NSEG)
            nr = ar * cr - ai * ci + sr_ref[rows, :]
            ni = ar * ci + ai * cr + si_ref[rows, :]
            sr_ref[rows, :] = nr
            si_ref[rows, :] = ni
            return (nr, ni, ar * qr - ai * qi, ar * qi + ai * qr) if powers else (nr, ni, qr, qi)

        cr, ci, qr, qi = lax.fori_loop(0, nt, step, (st_r[...], st_i[...], pw_r[...], pw_i[...]), unroll=8)
        st_r[...], st_i[...], pw_r[...], pw_i[...] = cr, ci, qr, qi
        pr_ref[...] = qr
        pi_ref[...] = qi
        _carry_finish(carry, c_in, c_out, c_sems, (pl.program_id(0) == nblk - 1) & (pl.program_id(1) == nT - 1))

    blk = pl.BlockSpec((tc, sb), lambda j, i: (tmap(i), j))
    wsp = pl.BlockSpec((None, cb, sb), lambda j, i: (j, 0, 0))
    asp = pl.BlockSpec((1, sb), lambda j, i: (0, j))
    psp = pl.BlockSpec((NSEG, sb), lambda j, i: (0, j))
    res = pl.pallas_call(
        body, name=name, grid=(nblk, nT),
        in_specs=[pl.BlockSpec((tc, cb), lambda j, i: (tmap(i), j)), wsp, wsp, asp, asp] + [_ANY] * len(carry.ins),
        out_specs=[blk, blk, psp, psp] + [_ANY] * len(carry.out_shapes),
        out_shape=[_S((T, NS), f32)] * 2 + [_S((NSEG, NS), f32)] * 2 + list(carry.out_shapes),
        scratch_shapes=[pltpu.VMEM((NSEG, sb), f32)] * 4 + list(carry.sems),
        compiler_params=_cparams(("arbitrary", "arbitrary") if carry.ins else ("parallel", "arbitrary")),
    )(src, w_r, w_i, a_r, a_i, *carry.ins)
    return (*res[:4], list(res[4:])) if carry.ins else res


def _s5_carry(e_r, e_i, p_r, p_i, *, reverse, name):
    NS = e_r.shape[1]

    def body(er_ref, ei_ref, pr_ref, pi_ref, cr_ref, ci_ref):
        ar, ai = pr_ref[0:1, :], pi_ref[0:1, :]
        cr = jnp.zeros((1, NS), f32)
        ci = jnp.zeros((1, NS), f32)
        order = list(range(NSEG - 1, -1, -1)) if reverse else list(range(NSEG))
        cr_ref[order[0]:order[0] + 1, :] = cr
        ci_ref[order[0]:order[0] + 1, :] = ci
        for prev, q in zip(order[:-1], order[1:]):
            er, ei = er_ref[prev:prev + 1, :], ei_ref[prev:prev + 1, :]
            cr, ci = er + ar * cr - ai * ci, ei + ar * ci + ai * cr
            cr_ref[q:q + 1, :] = cr
            ci_ref[q:q + 1, :] = ci

    return pl.pallas_call(body, name=name, out_shape=[_S((NSEG, NS), f32)] * 2)(e_r, e_i, p_r, p_i)


def _s5_fix_out(sl_r, sl_i, a_r, a_i, c_r, c_i, wc_r, wc_i, *, name):
    T, NS = sl_r.shape
    nblk, sb, cb = wc_r.shape
    tc = min(S5_TC, T)
    nT, nt = T // tc, tc // NSEG

    def body(lr_ref, li_ref, ar_ref, ai_ref, cr_ref, ci_ref, wr_ref, wi_ref, sr_ref, si_ref, y_ref, pw_r, pw_i):
        @pl.when(pl.program_id(1) == 0)
        def _():
            pw_r[...] = jnp.ones_like(pw_r)
            pw_i[...] = jnp.zeros_like(pw_i)

        ar = jnp.broadcast_to(ar_ref[...], (NSEG, sb))
        ai = jnp.broadcast_to(ai_ref[...], (NSEG, sb))
        cr, ci = cr_ref[...], ci_ref[...]

        def step(k, c):
            qr, qi = c
            qr, qi = ar * qr - ai * qi, ar * qi + ai * qr
            rows = pl.ds(pl.multiple_of(k * NSEG, NSEG), NSEG)
            sr_ref[rows, :] = lr_ref[rows, :] + qr * cr - qi * ci
            si_ref[rows, :] = li_ref[rows, :] + qr * ci + qi * cr
            return qr, qi

        qr, qi = lax.fori_loop(0, nt, step, (pw_r[...], pw_i[...]), unroll=8)
        pw_r[...], pw_i[...] = qr, qi
        y_ref[...] = (jnp.dot(sr_ref[...].astype(bf16), wr_ref[...], preferred_element_type=f32)
                      - jnp.dot(si_ref[...].astype(bf16), wi_ref[...], preferred_element_type=f32))

    blk = pl.BlockSpec((tc, sb), lambda j, i: (i, j))
    asp = pl.BlockSpec((1, sb), lambda j, i: (0, j))
    csp = pl.BlockSpec((NSEG, sb), lambda j, i: (0, j))
    wsp = pl.BlockSpec((None, sb, cb), lambda j, i: (j, 0, 0))
    return pl.pallas_call(
        body, name=name, grid=(nblk, nT), in_specs=[blk, blk, asp, asp, csp, csp, wsp, wsp],
        out_specs=[blk, blk, pl.BlockSpec((tc, cb), lambda j, i: (i, j))],
        out_shape=[_S((T, NS), f32)] * 2 + [_S((T, nblk * cb), f32)],
        scratch_shapes=[pltpu.VMEM((NSEG, sb), f32)] * 2, compiler_params=_cparams(("parallel", "arbitrary")),
    )(sl_r, sl_i, a_r, a_i, c_r, c_i, wc_r, wc_i)


def _s5_fix_bwd(ql_r, ql_i, ab_r, ab_i, c_r, c_i, s_r, s_i, sb_r, sb_i, u, dy, du_add, w_r, w_i, *, name):
    T, NS = ql_r.shape
    nblk, cb, sb = w_r.shape
    tc = min(S5_TC, T)
    nT, nt = T // tc, tc // NSEG
    tmap = lambda i: nT - 1 - i

    def body(lr_ref, li_ref, ar_ref, ai_ref, cr_ref, ci_ref, sr_ref, si_ref, br_ref, bi_ref, u_ref, dy_ref, dua_ref, wr_ref, wi_ref,
             du_ref, dwr_ref, dwi_ref, dcr_ref, dci_ref, dar_ref, dai_ref, pw_r, pw_i, ac_r, ac_i, q_r, q_i):
        first = pl.program_id(1) == 0

        @pl.when(first)
        def _():
            pw_r[...] = jnp.ones_like(pw_r)
            pw_i[...] = jnp.zeros_like(pw_i)
            ac_r[...] = jnp.zeros_like(ac_r)
            ac_i[...] = jnp.zeros_like(ac_i)
            dwr_ref[...] = jnp.zeros_like(dwr_ref)
            dwi_ref[...] = jnp.zeros_like(dwi_ref)
            dcr_ref[...] = jnp.zeros_like(dcr_ref)
            dci_ref[...] = jnp.zeros_like(dci_ref)

        ar = jnp.broadcast_to(ar_ref[...], (NSEG, sb))
        ai = jnp.broadcast_to(ai_ref[...], (NSEG, sb))
        cr, ci = cr_ref[...], ci_ref[...]

        def fix(rows, qr, qi, spr, spi, accr, acci):
            qr, qi = ar * qr - ai * qi, ar * qi + ai * qr
            xr = lr_ref[rows, :] + qr * cr - qi * ci
            xi = li_ref[rows, :] + qr * ci + qi * cr
            q_r[rows, :] = xr
            q_i[rows, :] = xi
            return qr, qi, accr + xr * spr + xi * spi, acci + xi * spr - xr * spi

        def step(k, c):
            qr, qi, accr, acci = c
            kk = nt - 1 - k
            rows = pl.ds(pl.multiple_of(kk * NSEG, NSEG), NSEG)
            prev = pl.ds(pl.multiple_of((kk - 1) * NSEG, NSEG), NSEG)
            return fix(rows, qr, qi, sr_ref[prev, :], si_ref[prev, :], accr, acci)

        c = lax.fori_loop(0, nt - 1, step, (pw_r[...], pw_i[...], ac_r[...], ac_i[...]), unroll=7)
        qr, qi, accr, acci = fix(pl.ds(0, NSEG), *c[:2], br_ref[...], bi_ref[...], *c[2:])
        pw_r[...], pw_i[...], ac_r[...], ac_i[...] = qr, qi, accr, acci

        qrb, qib = q_r[...].astype(bf16), q_i[...].astype(bf16)
        nt_dims = (((1,), (1,)), ((), ()))
        tn_dims = (((0,), (0,)), ((), ()))
        du_ref[...] = (dua_ref[...] + lax.dot_general(qrb, wr_ref[...], nt_dims, preferred_element_type=f32)
                       + lax.dot_general(qib, wi_ref[...], nt_dims, preferred_element_type=f32))
        ub = u_ref[...].astype(bf16)
        dwr_ref[...] += lax.dot_general(ub, qrb, tn_dims, preferred_element_type=f32)
        dwi_ref[...] += lax.dot_general(ub, qib, tn_dims, preferred_element_type=f32)
        dyb = dy_ref[...].astype(bf16)
        dcr_ref[...] += lax.dot_general(sr_ref[...].astype(bf16), dyb, tn_dims, preferred_element_type=f32)
        dci_ref[...] -= lax.dot_general(si_ref[...].astype(bf16), dyb, tn_dims, preferred_element_type=f32)

        @pl.when(pl.program_id(1) == nT - 1)
        def _():
            dar_ref[...] = jnp.sum(accr, axis=0, keepdims=True)
            dai_ref[...] = jnp.sum(acci, axis=0, keepdims=True)

    blk = pl.BlockSpec((tc, sb), lambda j, i: (tmap(i), j))
    asp = pl.BlockSpec((1, sb), lambda j, i: (0, j))
    csp = pl.BlockSpec((NSEG, sb), lambda j, i: (0, j))
    bsp = pl.BlockSpec((None, NSEG, sb), lambda j, i: (tmap(i), 0, j))
    chn = pl.BlockSpec((tc, cb), lambda j, i: (tmap(i), j))
    wsp = pl.BlockSpec((None, cb, sb), lambda j, i: (j, 0, 0))
    wcs = pl.BlockSpec((None, sb, cb), lambda j, i: (j, 0, 0))
    return pl.pallas_call(
        body, name=name, grid=(nblk, nT), in_specs=[blk, blk, asp, asp, csp, csp, blk, blk, bsp, bsp, chn, chn, chn, wsp, wsp],
        out_specs=[chn, wsp, wsp, wcs, wcs, asp, asp],
        out_shape=[_S((T, nblk * cb), f32), _S((nblk, cb, sb), f32), _S((nblk, cb, sb), f32), _S((nblk, sb, cb), f32),
                   _S((nblk, sb, cb), f32), _S((1, NS), f32), _S((1, NS), f32)],
        scratch_shapes=[pltpu.VMEM((NSEG, sb), f32)] * 4 + [pltpu.VMEM((tc, sb), f32)] * 2,
        compiler_params=_cparams(("parallel", "arbitrary")),
    )(ql_r, ql_i, ab_r, ab_i, c_r, c_i, s_r, s_i, sb_r, sb_i, u, dy, du_add, w_r, w_i)


SSD2_TB = 512
_NN = (((1,), (0,)), ((), ()))
_NT = (((1,), (1,)), ((), ()))
_TN = (((0,), (0,)), ((), ()))


def _dotf(a, b, dims):
    return lax.dot_general(a.astype(bf16), b.astype(bf16), dims, preferred_element_type=f32)


def _doth(a, b, dims=_NN, sel="b", parts=3):
    x, m = (a, b) if sel == "b" else (b, a)
    m = m.astype(bf16)
    out = None
    for _ in range(parts):
        piece = x.astype(bf16)
        x = x - piece.astype(f32)
        d = lax.dot_general(*((piece, m) if sel == "b" else (m, piece)), dims, preferred_element_type=f32)
        out = d if out is None else out + d
    return out


def _ssd_consts(hpg):
    W = hpg * CHUNK
    i = lax.broadcasted_iota(jnp.int32, (CHUNK, CHUNK), 0)
    j = lax.broadcasted_iota(jnp.int32, (CHUNK, CHUNK), 1)
    tril = (i >= j).astype(f32)
    r = lax.broadcasted_iota(jnp.int32, (W, W), 0)
    c = lax.broadcasted_iota(jnp.int32, (W, W), 1)
    bd = (r // CHUNK == c // CHUNK).astype(f32)
    triu_bd = bd * (r <= c).astype(f32)
    e_r = lax.broadcasted_iota(jnp.int32, (W, LANES), 0)
    e_c = lax.broadcasted_iota(jnp.int32, (W, LANES), 1)
    ered = (e_r // HEADDIM == e_c).astype(f32)
    return tril, jnp.tile(tril, (1, hpg)), bd, triu_bd, ered


def _ssd2_specs(G, hpg, tb, tmap, b_off, c_off):
    W = hpg * HEADDIM
    ncb = tb // CHUNK
    xsp = pl.BlockSpec((tb, W), lambda g, i: (tmap(i), g))
    bsp = pl.BlockSpec((tb, SSD_STATE), lambda g, i: (tmap(i), b_off + g))
    csp = pl.BlockSpec((tb, SSD_STATE), lambda g, i: (tmap(i), c_off + g))
    rsp = pl.BlockSpec((None, ncb, W), lambda g, i: (g, tmap(i), 0))
    dsp = pl.BlockSpec((1, W), lambda g, i: (0, g))
    hsp = pl.BlockSpec((None, ncb, SSD_STATE, W), lambda g, i: (g, tmap(i), 0, 0))
    const = lambda a: pl.BlockSpec(a.shape, lambda g, i: (0, 0))
    return xsp, bsp, csp, rsp, dsp, hsp, const


def _tile_rows(a, n):
    return jnp.concatenate([a] * n, axis=0)


def _ssd2_fwd(xc, dt4, a4, dtw, aw, d4, consts, *, d_inner, name, carry=None):
    carry = carry or _NO_CARRY
    T = xc.shape[0]
    G, nc, W = dtw.shape
    hpg = W // CHUNK
    tb = min(SSD2_TB, T)
    nb, ncb = T // tb, tb // CHUNK
    b_off = d_inner // SSD_STATE
    xsp, bsp, csp, rsp, dsp, hsp, const = _ssd2_specs(G, hpg, tb, lambda i: i, b_off, b_off + G)
    tril, mask4, bd, triu_bd, _ = consts

    def body(*refs):
        own, c_in, c_out, c_sems = _carry_split(carry, refs, 12, 2)
        x_ref, b_ref, c_ref, dt_ref, a_ref, dtw_ref, aw_ref, d_ref, tril_ref, mask_ref, bd_ref, tbd_ref, y_ref, hs_ref, h_scr = own
        _carry_start(carry, c_in, c_out, c_sems, (pl.program_id(0) == 0) & (pl.program_id(1) == 0))

        @pl.when(pl.program_id(1) == 0)
        def _():
            h_scr[...] = jnp.zeros_like(h_scr)

        acs_rows = _doth(aw_ref[...], tbd_ref[...])
        ht = h_scr[...]
        for c in range(ncb):
            rows = slice(c * CHUNK, (c + 1) * CHUNK)
            x, bm, cm = x_ref[rows, :], b_ref[rows, :], c_ref[rows, :]
            acs = _doth(tril_ref[...], a_ref[rows, :], sel="a")
            lmat = jnp.where(mask_ref[...] > 0, jnp.exp(jnp.minimum(acs - acs_rows[c:c + 1, :], 0.0)), 0.0)
            m4 = _dotf(cm, _tile_rows(bm, hpg), _NT) * lmat * dtw_ref[c:c + 1, :]
            xbd = _tile_rows(x, hpg) * bd_ref[...]
            hs_ref[c] = ht
            y_ref[rows, :] = _dotf(m4, xbd, _NN) + _dotf(cm, ht, _NN) * jnp.exp(acs) + d_ref[...] * x
            a_last = acs[CHUNK - 1:CHUNK, :]
            xw = x * (jnp.exp(a_last - acs) * dt_ref[rows, :])
            ht = ht * jnp.exp(a_last) + _dotf(bm, xw, _TN)
        h_scr[...] = ht
        _carry_finish(carry, c_in, c_out, c_sems, (pl.program_id(0) == G - 1) & (pl.program_id(1) == nb - 1))

    res = pl.pallas_call(
        body, name=name, grid=(G, nb),
        in_specs=[xsp, bsp, csp, xsp, xsp, rsp, rsp, dsp, const(tril), const(mask4), const(bd), const(triu_bd)] + [_ANY] * len(carry.ins),
        out_specs=[xsp, hsp] + [_ANY] * len(carry.out_shapes),
        out_shape=[_S((T, G * W), f32), _S((G, nc, SSD_STATE, W), f32)] + list(carry.out_shapes),
        scratch_shapes=[pltpu.VMEM((SSD_STATE, W), f32)] + list(carry.sems),
        compiler_params=_cparams(("arbitrary", "arbitrary") if carry.ins else ("parallel", "arbitrary")),
    )(xc, xc, xc, dt4, a4, dtw, aw, d4, tril, mask4, bd, triu_bd, *carry.ins)
    return res[0], res[1], list(res[2:])


def _ssd2_bwd(xc, dt4, a4, dtw, aw, d4, consts, hs, dy, *, d_inner, name):
    T = xc.shape[0]
    G, nc, W = dtw.shape
    hpg = W // CHUNK
    tb = min(SSD2_TB, T)
    nb, ncb = T // tb, tb // CHUNK
    b_off = d_inner // SSD_STATE
    tmap = lambda i: nb - 1 - i
    xsp, bsp, csp, rsp, dsp, hsp, const = _ssd2_specs(G, hpg, tb, tmap, b_off, b_off + G)
    gsp = pl.BlockSpec((tb, SSD_STATE), lambda g, i: (tmap(i), g))
    ddsp = pl.BlockSpec((None, 1, LANES), lambda g, i: (g, 0, 0))
    tril, mask4, bd, triu_bd, ered = consts

    def body(x_ref, b_ref, c_ref, dt_ref, a_ref, dtw_ref, aw_ref, d_ref, tril_ref, mask_ref, bd_ref, tbd_ref, er_ref, hs_ref, dy_ref,
             dx_ref, db_ref, dc_ref, ddtc_ref, dac_ref, ddtw_ref, daw_ref, dd_ref, g_scr, dd_scr, rw_scr, tl_scr):
        first = pl.program_id(1) == 0

        @pl.when(first)
        def _():
            g_scr[...] = jnp.zeros_like(g_scr)
            dd_scr[...] = jnp.zeros_like(dd_scr)

        mask = mask_ref[...] > 0
        lane_in_block = lax.broadcasted_iota(jnp.int32, mask.shape, 1) & (CHUNK - 1)
        maskt = lax.broadcasted_iota(jnp.int32, mask.shape, 0) <= lane_in_block
        acs_rows = _doth(aw_ref[...], tbd_ref[...])
        dht = g_scr[...]
        dd = dd_scr[...]
        for c in range(ncb - 1, -1, -1):
            rows = slice(c * CHUNK, (c + 1) * CHUNK)
            x, bm, cm, dyc = x_ref[rows, :], b_ref[rows, :], c_ref[rows, :], dy_ref[rows, :]
            dtc, dtr = dt_ref[rows, :], dtw_ref[c:c + 1, :]
            ht = hs_ref[c]
            acs = _doth(tril_ref[...], a_ref[rows, :], sel="a")
            seg = acs - acs_rows[c:c + 1, :]
            lmat = jnp.where(mask, jnp.exp(jnp.minimum(seg, 0.0)), 0.0)
            lmat_t = jnp.where(maskt, jnp.exp(jnp.minimum(-seg, 0.0)), 0.0)
            btile, ctile = _tile_rows(bm, hpg), _tile_rows(cm, hpg)
            g4 = _dotf(cm, btile, _NT)
            gt4 = _dotf(bm, ctile, _NT)
            m4 = g4 * lmat * dtr
            mt4 = gt4 * lmat_t * dtc
            xbd = _tile_rows(x, hpg) * bd_ref[...]
            dybd = _tile_rows(dyc, hpg) * bd_ref[...]
            dm4 = _dotf(dyc, xbd, _NT)
            dmt4 = _dotf(x, dybd, _NT)
            dx = d_ref[...] * dyc + _dotf(mt4, dybd, _NN)
            dd = dd + jnp.sum(dyc * x, axis=0, keepdims=True)
            e4 = dm4 * m4
            dc = _dotf(dm4 * lmat * dtr, btile, _NN)
            db = _dotf(dmt4 * lmat_t * dtc, ctile, _NN)
            decay = jnp.exp(acs)
            yoff = _dotf(cm, ht, _NN) * decay
            dz = dyc * decay
            dc = dc + _dotf(dz, ht, _NT)
            dht_prev = _dotf(cm, dz, _TN)
            a_last = acs[CHUNK - 1:CHUNK, :]
            ea_last = jnp.exp(a_last)
            erel = jnp.exp(a_last - acs)
            dte = erel * dtc
            dxw = _dotf(bm, dht, _NN)
            db = db + _dotf(x * dte, dht, _NT)
            dx = dx + dxw * dte
            q4 = dxw * x
            dacs = e4 + dyc * yoff - q4 * dte
            col = jnp.concatenate([q4 * erel, _doth(tril_ref[...], dacs, _TN, sel="a")], axis=0)
            col = _doth(col, er_ref[...], parts=2)
            ddtc_ref[rows, :] = col[:CHUNK]
            dac_ref[rows, :] = col[CHUNK:]
            ddtw_ref[c:c + 1, :] = jnp.sum(dm4 * g4 * lmat, axis=0, keepdims=True)
            rw_scr[c:c + 1, :] = -jnp.sum(e4, axis=0, keepdims=True)
            tl_scr[c:c + 1, :] = jnp.sum(q4 * dte, axis=0, keepdims=True) + ea_last * jnp.sum(dht * ht, axis=0, keepdims=True)
            dx_ref[rows, :] = dx
            db_ref[rows, :] = db
            dc_ref[rows, :] = dc
            dht = dht_prev + dht * ea_last
        daw_ref[...] = _doth(rw_scr[...], tbd_ref[...], _NT) + _doth(tl_scr[...], bd_ref[...])
        g_scr[...] = dht
        dd_scr[...] = dd

        @pl.when(pl.program_id(1) == nb - 1)
        def _():
            dd_ref[...] = _doth(dd, er_ref[...])

    return pl.pallas_call(
        body, name=name, grid=(G, nb),
        in_specs=[xsp, bsp, csp, xsp, xsp, rsp, rsp, dsp, const(tril), const(mask4), const(bd), const(triu_bd), const(ered), hsp, xsp],
        out_specs=[xsp, gsp, gsp, gsp, gsp, rsp, rsp, ddsp],
        out_shape=[_S((T, G * W), f32), _S((T, G * SSD_STATE), f32), _S((T, G * SSD_STATE), f32), _S((T, G * LANES), f32),
                   _S((T, G * LANES), f32), _S(dtw.shape, f32), _S(dtw.shape, f32), _S((G, 1, LANES), f32)],
        scratch_shapes=[pltpu.VMEM((SSD_STATE, W), f32), pltpu.VMEM((1, W), f32), pltpu.VMEM((ncb, W), f32), pltpu.VMEM((ncb, W), f32)],
        compiler_params=_cparams(("parallel", "arbitrary")),
    )(xc, xc, xc, dt4, a4, dtw, aw, d4, tril, mask4, bd, triu_bd, ered, hs, dy)


def _peers():
    x, y, c = lax.axis_index("x"), lax.axis_index("y"), lax.axis_index("c")
    return x, y, c


_ANY = pl.BlockSpec(memory_space=pl.ANY)
N_CHIP = N_DEV // 2


def _all_gather(shards, *, name):
    n = len(shards)
    carry = _carry_gather(shards)

    def body(*refs):
        x_refs, out_refs, sems = refs[:n], refs[n:2 * n], refs[2 * n:]
        _gather_start(x_refs, out_refs, sems)
        _gather_finish(x_refs, out_refs, sems)

    return pl.pallas_call(
        body, name=name, out_shape=list(carry.out_shapes), in_specs=[_ANY] * n, out_specs=[_ANY] * n, scratch_shapes=list(carry.sems),
    )(*shards)


def _gather_parts(x_refs, out_refs, sems):
    send_sems, recv_sems, local_sems = sems
    x, y, c = _peers()
    me, sibling = (x, y, c), (x, y, 1 - c)
    chips = [(1 - x, y), (x, 1 - y), (1 - x, 1 - y)]
    n = len(x_refs)

    def copy(a, r, block, to, src=None):
        px, py, pc = block
        slot = out_refs[a].at[4 * px + 2 * py + pc]
        return pltpu.make_async_remote_copy(
            src_ref=slot if src is None else src, dst_ref=slot, send_sem=send_sems.at[7 * a + r],
            recv_sem=recv_sems.at[7 * a + r], device_id=to, device_id_type=MESH)

    mine = [pltpu.make_async_copy(x_refs[a], out_refs[a].at[4 * x + 2 * y + c], local_sems.at[a]) for a in range(n)]
    first = []
    for a in range(n):
        first.append(copy(a, 0, me, sibling, src=x_refs[a]))
        first += [copy(a, 1 + j, me, (*chip, c), src=x_refs[a]) for j, chip in enumerate(chips)]
    return copy, mine, first, me, sibling, chips, c, n


def _gather_start(x_refs, out_refs, sems):
    _, mine, first, *_ = _gather_parts(x_refs, out_refs, sems)
    for cp in mine + first:
        cp.start()


def _gather_finish(x_refs, out_refs, sems):
    copy, mine, first, me, sibling, chips, c, n = _gather_parts(x_refs, out_refs, sems)
    passed = []
    for j, chip in enumerate(chips):
        for a in range(n):
            copy(a, 1 + j, (*chip, c), me).wait_recv()
            fwd = copy(a, 4 + j, (*chip, c), sibling)
            fwd.start()
            passed.append(fwd)
    for a in range(n):
        copy(a, 0, sibling, me).wait_recv()
    for j, chip in enumerate(chips):
        for a in range(n):
            copy(a, 4 + j, (*chip, 1 - c), me).wait_recv()
    for cp in first + passed:
        cp.wait_send()
    for cp in mine:
        cp.wait()


def _exchange_sibling(slots, *, name):
    n = len(slots)

    def body(*refs):
        x_refs, sib_refs = refs[:n], refs[n:2 * n]
        send_sems, recv_sems = refs[2 * n:]
        x, y, c = _peers()
        give = [pltpu.make_async_remote_copy(
            src_ref=x_refs[a].at[pl.ds(N_CHIP * (1 - c), N_CHIP)], dst_ref=sib_refs[a], send_sem=send_sems.at[a],
            recv_sem=recv_sems.at[a], device_id=(x, y, 1 - c), device_id_type=MESH) for a in range(n)]
        for cp in give:
            cp.start()
        for cp in give:
            cp.wait_recv()
        for cp in give:
            cp.wait_send()

    return list(pl.pallas_call(
        body, name=name, out_shape=[_S((N_CHIP,) + s.shape[1:], s.dtype) for s in slots], in_specs=[_ANY] * n, out_specs=[_ANY] * n,
        scratch_shapes=[pltpu.SemaphoreType.DMA((n,)), pltpu.SemaphoreType.DMA((n,))],
    )(*slots))


def _chip_sum(slots, sib, core, *, name):
    _, R, W = slots.shape
    tr = _pick(R, max(16, (1 << 20) // (4 * W)), align=16)

    def body(core_ref, x_ref, s_ref, o_ref):
        o_ref[...] = (x_ref[...].astype(f32) + s_ref[...].astype(f32)).astype(o_ref.dtype)

    blk = pl.BlockSpec((None, tr, W), lambda t, i, core_ref: (t, i, 0))
    return pl.pallas_call(
        body, name=name, out_shape=_S(sib.shape, slots.dtype),
        grid_spec=pltpu.PrefetchScalarGridSpec(
            num_scalar_prefetch=1, grid=(N_CHIP, R // tr),
            in_specs=[pl.BlockSpec((None, tr, W), lambda t, i, core_ref: (N_CHIP * core_ref[0] + t, i, 0)), blk], out_specs=blk),
        compiler_params=_cparams(("parallel", "parallel")),
    )(core, slots, sib)


def _chip_out_shapes(parts):
    return [_S((N_CHIP - 1,) + p.shape[1:], p.dtype) for p in parts]


def _chip_sems(n):
    return [pltpu.SemaphoreType.DMA((3 * n,)), pltpu.SemaphoreType.DMA((3 * n,))]


def _chip_copies(p_refs, out_refs, send_sems, recv_sems):
    x, y, c = _peers()
    copies = []
    for j in range(1, N_CHIP):
        tx, ty = x ^ (j >> 1), y ^ (j & 1)
        for a in range(len(p_refs)):
            copies.append(pltpu.make_async_remote_copy(
                src_ref=p_refs[a].at[2 * tx + ty], dst_ref=out_refs[a].at[j - 1], send_sem=send_sems.at[3 * a + j - 1],
                recv_sem=recv_sems.at[3 * a + j - 1], device_id=(tx, ty, c), device_id_type=MESH))
    return copies


def _start_all(copies):
    for cp in copies:
        cp.start()


def _wait_all(copies):
    for cp in copies:
        cp.wait_recv()
    for cp in copies:
        cp.wait_send()


class _Carry(NamedTuple):
    ins: tuple = ()
    out_shapes: tuple = ()
    sems: tuple = ()
    start: Callable = None
    finish: Callable = None


_NO_CARRY = _Carry()


def _carry_chips(parts):
    return _Carry(tuple(parts), tuple(_chip_out_shapes(parts)), tuple(_chip_sems(len(parts))),
                  lambda i, o, s: _start_all(_chip_copies(i, o, *s)), lambda i, o, s: _wait_all(_chip_copies(i, o, *s)))


def _carry_gather(shards):
    n = len(shards)
    sems = (pltpu.SemaphoreType.DMA((7 * n,)), pltpu.SemaphoreType.DMA((7 * n,)), pltpu.SemaphoreType.DMA((n,)))
    return _Carry(tuple(shards), tuple(_S((N_DEV,) + s.shape, s.dtype) for s in shards), sems, _gather_start, _gather_finish)


def _carry_split(carry, refs, n_in, n_out):
    ci, co, cs = len(carry.ins), len(carry.out_shapes), len(carry.sems)
    refs = list(refs)
    own_in, c_in = refs[:n_in], refs[n_in:n_in + ci]
    own_out, c_out = refs[n_in + ci:n_in + ci + n_out], refs[n_in + ci + n_out:n_in + ci + n_out + co]
    rest = refs[n_in + ci + n_out + co:]
    own_scratch, c_sems = rest[:len(rest) - cs], rest[len(rest) - cs:]
    return own_in + own_out + own_scratch, c_in, c_out, c_sems


def _carry_start(carry, c_in, c_out, c_sems, first):
    if carry.ins:
        @pl.when(first)
        def _():
            carry.start(c_in, c_out, c_sems)


def _carry_finish(carry, c_in, c_out, c_sems, last):
    if carry.ins:
        @pl.when(last)
        def _():
            carry.finish(c_in, c_out, c_sems)


def _sum_slots(stack, *, name):
    n, R, W = stack.shape
    tr = _pick(R, 1024, align=8)

    def body(s_ref, o_ref):
        acc = s_ref[0]
        for k in range(1, n):
            acc = acc + s_ref[k]
        o_ref[...] = acc

    return pl.pallas_call(
        body, name=name, grid=(R // tr,), in_specs=[pl.BlockSpec((n, tr, W), lambda i: (0, i, 0))],
        out_specs=pl.BlockSpec((tr, W), lambda i: (i, 0)), out_shape=_S((R, W), f32), compiler_params=_cparams(("parallel",)),
    )(stack)


def _adamw_math(gv, wv, mv, vv):
    c1 = 1.0 / (1.0 - ADAM_B1 ** ADAM_STEP)
    c2 = 1.0 / (1.0 - ADAM_B2 ** ADAM_STEP)
    nm = ADAM_B1 * mv + (1.0 - ADAM_B1) * gv
    nv = ADAM_B2 * vv + (1.0 - ADAM_B2) * jnp.square(gv)
    return -ADAM_LR * ((nm * c1) / (jnp.sqrt(nv * c2) + ADAM_EPS) + ADAM_WD * wv), nm, nv


def _adamw(g, w, m, v, *, name):
    R, W = w.shape
    tr = _pick(R, max(8, (1 << 20) // (4 * W)), align=8)

    def body(g_ref, w_ref, m_ref, v_ref, d_ref, nm_ref, nv_ref):
        d_ref[...], nm_ref[...], nv_ref[...] = _adamw_math(g_ref[...], w_ref[...], m_ref[...], v_ref[...])

    sp = pl.BlockSpec((tr, W), lambda i: (i, 0))
    return pl.pallas_call(
        body, name=name, grid=(R // tr,), in_specs=[sp] * 4, out_specs=[sp] * 3, out_shape=[_S((R, W), f32)] * 3,
        compiler_params=_cparams(("parallel",)),
    )(g, w, m, v)


def _reduce_adamw(own, arrived, chip, w, m, v, *, name):
    n, R, W = arrived.shape
    tr = _pick(R, max(16, (1 << 20) // (4 * W)), align=16)

    def body(chip_ref, o_ref, p_ref, w_ref, m_ref, v_ref, g_ref, d_ref, nm_ref, nv_ref):
        gv = o_ref[...].astype(f32)
        for k in range(n):
            gv = gv + p_ref[k].astype(f32)
        g_ref[...] = gv
        d_ref[...], nm_ref[...], nv_ref[...] = _adamw_math(gv, w_ref[...], m_ref[...], v_ref[...])

    sp = pl.BlockSpec((tr, W), lambda i, chip_ref: (i, 0))
    return pl.pallas_call(
        body, name=name, out_shape=[_S((R, W), f32)] * 4,
        grid_spec=pltpu.PrefetchScalarGridSpec(
            num_scalar_prefetch=1, grid=(R // tr,),
            in_specs=[pl.BlockSpec((None, tr, W), lambda i, chip_ref: (chip_ref[0], i, 0)),
                      pl.BlockSpec((n, tr, W), lambda i, chip_ref: (0, i, 0))] + [sp] * 3, out_specs=[sp] * 4),
        compiler_params=_cparams(("parallel",)),
    )(chip, own, arrived, w, m, v)


def _pieces(seg_start, seg_len, shard_w):
    out, col = [], seg_start
    while col < seg_start + seg_len:
        k, a = divmod(col, shard_w)
        n = min(shard_w - a, seg_start + seg_len - col)
        out.append((k, a, col - seg_start, n))
        col += n
    return out


def _unshard_w_in(g, seg_lens, *, name):
    _, D, w = g.shape
    starts = [sum(seg_lens[:i]) for i in range(len(seg_lens))]
    widths = [max(n, LANES) for n in seg_lens]
    tm = _pick(D, 256, align=16)

    def body(g_ref, *o_refs):
        for o_ref, s0, n in zip(o_refs, starts, seg_lens):
            if n < o_ref.shape[1]:
                o_ref[...] = jnp.zeros_like(o_ref)
            for k, a, off, m in _pieces(s0, n, w):
                o_ref[:, off:off + m] = g_ref[k, :, a:a + m]

    return pl.pallas_call(
        body, name=name, grid=(D // tm,), in_specs=[pl.BlockSpec((N_DEV, tm, w), lambda i: (0, i, 0))],
        out_specs=[pl.BlockSpec((tm, wd), lambda i: (i, 0)) for wd in widths], out_shape=[_S((D, wd), g.dtype) for wd in widths],
        compiler_params=_cparams(("parallel",)),
    )(g)


def _unshard_pair(g1, g2, *, name):
    _, D, w = g1.shape
    tm = _pick(D, 256, align=16)

    def body(a_ref, b_ref, o_ref):
        for i, g_ref in enumerate((a_ref, b_ref)):
            for k in range(N_DEV):
                off = (i * N_DEV + k) * w
                o_ref[:, off:off + w] = g_ref[k]

    blk = pl.BlockSpec((N_DEV, tm, w), lambda i: (0, i, 0))
    return pl.pallas_call(
        body, name=name, grid=(D // tm,), in_specs=[blk, blk], out_specs=pl.BlockSpec((tm, 2 * N_DEV * w), lambda i: (i, 0)),
        out_shape=_S((D, 2 * N_DEV * w), g1.dtype), compiler_params=_cparams(("parallel",)),
    )(g1, g2)


def _reshard_pair(dw, *, name):
    D, w = dw.shape[0], dw.shape[1] // (2 * N_DEV)
    tm = _pick(D, 128, align=16)

    def body(g_ref, a_ref, b_ref):
        for i, o_ref in enumerate((a_ref, b_ref)):
            for k in range(N_DEV):
                off = (i * N_DEV + k) * w
                o_ref[_slot_of(k)] = g_ref[:, off:off + w].astype(o_ref.dtype)

    blk = pl.BlockSpec((N_DEV, tm, w), lambda i: (0, i, 0))
    return pl.pallas_call(
        body, name=name, grid=(D // tm,), in_specs=[pl.BlockSpec((tm, dw.shape[1]), lambda i: (i, 0))], out_specs=[blk, blk],
        out_shape=[_S((N_DEV, D, w), bf16)] * 2, compiler_params=_cparams(("parallel",)),
    )(dw)


def _reshard_w_in(grads, seg_lens, w, *, name):
    D = grads[0].shape[0]
    starts = [sum(seg_lens[:i]) for i in range(len(seg_lens))]
    tm = _pick(D, 128, align=16)

    def body(*refs):
        o_ref = refs[-1]
        for g_ref, s0, n in zip(refs[:-1], starts, seg_lens):
            for k, a, off, m in _pieces(s0, n, w):
                o_ref[_slot_of(k), :, a:a + m] = g_ref[:, off:off + m].astype(o_ref.dtype)

    return pl.pallas_call(
        body, name=name, grid=(D // tm,), in_specs=[pl.BlockSpec((tm, g.shape[1]), lambda i: (i, 0)) for g in grads],
        out_specs=pl.BlockSpec((N_DEV, tm, w), lambda i: (0, i, 0)), out_shape=_S((N_DEV, D, w), bf16),
        compiler_params=_cparams(("parallel",)),
    )(*grads)


def _pad_flat(a, mult):
    a = a.reshape(-1)
    n = -(-a.shape[0] // mult) * mult
    return a if n == a.shape[0] else jnp.pad(a, (0, n - a.shape[0]))


def _pad_cols(a, mult):
    n = -(-a.shape[1] // mult) * mult
    return a if n == a.shape[1] else jnp.pad(a, ((0, 0), (0, n - a.shape[1])))


def _block_diag(t):
    nblk, g, P, Q = t.shape
    eye = jnp.eye(g, dtype=t.dtype)
    return (t[:, :, :, None, :] * eye[None, :, None, :, None]).reshape(nblk, g * P, g * Q)


def _block_diag_t(w, P, Q):
    nblk = w.shape[0]
    g = w.shape[1] // P
    eye = jnp.eye(g, dtype=w.dtype)
    return (w.reshape(nblk, g, P, g, Q) * eye[None, :, None, :, None]).sum(axis=3)


_COLS = ("ffn1_w_gate", "ffn1_w_up", "ffn2_w_gate", "ffn2_w_up")
_ROWS = ("ffn1_w_down", "ffn2_w_down", "s5_w_glu", "w_proj_s5", "w_out", "w_proj_ssd")
_BIG = _COLS + _ROWS + ("w_in", "conv_w")
_SMALL = ("ffn1_norm", "mix_norm", "conv_b", "s5_A_re", "s5_A_im", "s5_log_dt", "s5_B_re", "s5_B_im", "s5_C_re", "s5_C_im",
          "s5_D", "s5_b_glu", "ssd_A_log", "ssd_dt_bias", "ssd_D", "ssd_norm", "b_gate", "ffn2_norm", "final_norm")
_WEIGHTS = ("ffn1_norm", "ffn1_w_gate", "ffn1_w_up", "ffn1_w_down", "mix_norm", "w_in", "conv_w", "conv_b", "s5_A_re", "s5_A_im",
            "s5_log_dt", "s5_B_re", "s5_B_im", "s5_C_re", "s5_C_im", "s5_D", "s5_w_glu", "s5_b_glu", "ssd_A_log", "ssd_dt_bias",
            "ssd_D", "ssd_norm", "w_proj_s5", "w_proj_ssd", "b_gate", "w_out", "ffn2_norm", "ffn2_w_gate", "ffn2_w_up",
            "ffn2_w_down", "final_norm")
def _with_carry(res, carry):
    return res if carry else (res, [])


def _ffn_fwd(x, n, wgu, wd_of, tag, carries=(None, None, None)):
    D = x.shape[1]
    h = _rows(_f_rmsnorm, [x], [n], [(D, bf16)], name=tag + "_norm")[0]
    ab, got0 = _with_carry(_mm(h, wgu, carry=carries[0], out_dtype=bf16, name=tag + "_gate_up"), carries[0])
    wd = wd_of(got0)
    F = wd.shape[0]
    (c,), got1 = _with_carry(_rows(_f_swiglu, [ab], [], [(F, bf16)], carry=carries[1], name=tag + "_act"), carries[1])
    y, got2 = _with_carry(_mm(c, wd, scale=0.5, add=x, carry=carries[2], name=tag + "_down"), carries[2])
    return y, (x, n, h, ab, c), (got0, got1, got2)


def _ffn_bwd(saved, wgu, wd, dy, tag, carry_dact=None, carry_after_dwd=None, carry_after_dwgu=None):
    x, n, h, ab, c = saved
    F = wd.shape[0]
    def act_bwd(dc, ab_):
        return jax.vjp(lambda t: _f_swiglu(t)[0], ab_)[1](dc)[0]

    dab, arr_a = _with_carry(_mm(dy, wd, tb=True, scale=0.5, tm=EPI_TM, epi=(act_bwd, ab, 2 * F), out_dtype=bf16, carry=carry_dact,
                                 name=tag + "_d_act"), carry_dact)
    dwd = _mm(c, dy, ta=True, o_blk="m", o_slots=True, tm=F // 2, out_dtype=bf16, scale=0.5, name=tag + "_d_wdown")
    carry_w = carry_after_dwd(dwd) if carry_after_dwd else None
    dwgu, arr_w = _with_carry(_mm(h, dab, ta=True, carry=carry_w, name=tag + "_d_wgu"), carry_w)
    dwg, dwu = _reshard_pair(dwgu, name=tag + "_reshard_d_wgu")
    carry_h = carry_after_dwgu(dwg, dwu) if carry_after_dwgu else None
    dh, arr_h = _with_carry(_mm(dab, wgu, tb=True, carry=carry_h, name=tag + "_d_h"), carry_h)
    (dx,), (dn,) = _rows_bwd(_f_rmsnorm, [x], [n], [dh], name=tag + "_norm_bwd", want_rows=[0], adds={0: dy})
    return dx, dn, dwg, dwu, dwd, arr_a, arr_w, arr_h


def kernel(x, ffn1_norm, ffn1_w_gate, ffn1_w_up, ffn1_w_down, mix_norm, w_in, conv_w, conv_b, s5_A_re, s5_A_im, s5_log_dt, s5_B_re, s5_B_im, s5_C_re, s5_C_im, s5_D, s5_w_glu, s5_b_glu, ssd_A_log, ssd_dt_bias, ssd_D, ssd_norm, w_proj_s5, w_proj_ssd, b_gate, w_out, ffn2_norm, ffn2_w_gate, ffn2_w_up, ffn2_w_down, final_norm, loss_target, m_ffn1_norm, m_ffn1_w_gate, m_ffn1_w_up, m_ffn1_w_down, m_mix_norm, m_w_in, m_conv_w, m_conv_b, m_s5_A_re, m_s5_A_im, m_s5_log_dt, m_s5_B_re, m_s5_B_im, m_s5_C_re, m_s5_C_im, m_s5_D, m_s5_w_glu, m_s5_b_glu, m_ssd_A_log, m_ssd_dt_bias, m_ssd_D, m_ssd_norm, m_w_proj_s5, m_w_proj_ssd, m_b_gate, m_w_out, m_ffn2_norm, m_ffn2_w_gate, m_ffn2_w_up, m_ffn2_w_down, m_final_norm, v_ffn1_norm, v_ffn1_w_gate, v_ffn1_w_up, v_ffn1_w_down, v_mix_norm, v_w_in, v_conv_w, v_conv_b, v_s5_A_re, v_s5_A_im, v_s5_log_dt, v_s5_B_re, v_s5_B_im, v_s5_C_re, v_s5_C_im, v_s5_D, v_s5_w_glu, v_s5_b_glu, v_ssd_A_log, v_ssd_dt_bias, v_ssd_D, v_ssd_norm, v_w_proj_s5, v_w_proj_ssd, v_b_gate, v_w_out, v_ffn2_norm, v_ffn2_w_gate, v_ffn2_w_up, v_ffn2_w_down, v_final_norm):
    P = dict(locals())
    T, D = x.shape[1], x.shape[2]
    x0, tgt = x[0], loss_target[0]
    sh = {k: P[k][0] for k in _BIG}

    send = {k: (sh[k] if k == "conv_w" else sh[k].astype(bf16)) for k in _BIG}
    W = {}

    def gather_in(keys):
        return _carry_gather([send[k] for k in keys])

    first_keys = ("ffn1_w_gate", "ffn1_w_up", "conv_w")
    W.update(zip(first_keys, _all_gather([send[k] for k in first_keys], name="gather_weights_first")))
    whole = lambda k: W[k].reshape(-1, D)
    conv_w_full = W["conv_w"].transpose(1, 0, 2).reshape(CONV_K, -1)

    d_inner = N_DEV * sh["w_proj_ssd"].shape[0]
    conv_dim = conv_w_full.shape[1]
    H = ssd_A_log.shape[1]
    G = (conv_dim - d_inner) // (2 * SSD_STATE)
    hpg = H // G
    nc = T // CHUNK
    Gs = D // S5_GROUP
    nblk = Gs // S5_GPB
    NS = Gs * S5_STATE
    seg_lens = (D, d_inner, conv_dim, H, 2 * D)

    cuts = [0, D // 3 // 16 * 16, D // 3 // 16 * 16 + 3 * D // 8 // 16 * 16, D]
    win_rows = [send["w_in"][a_:b_] for a_, b_ in zip(cuts[:-1], cuts[1:])]
    wgu1 = _unshard_pair(W["ffn1_w_gate"], W["ffn1_w_up"], name="unshard_ffn1_gate_up")

    def ffn1_down(got):
        W["ffn1_w_down"] = got[0]
        return whole("ffn1_w_down")

    x1, sv1, (got0, got1, got2) = _ffn_fwd(
        x0, ffn1_norm, wgu1, ffn1_down, "ffn1",
        carries=(_carry_gather([send["ffn1_w_down"], win_rows[0]]), _carry_gather([win_rows[1]]), _carry_gather([win_rows[2]])))
    ffn1_w = (wgu1, whole("ffn1_w_down"))
    W["w_in"] = jnp.concatenate([got0[1], got1[0], got2[0]], axis=1)
    w_u, w_z, w_xbc, w_dt, w_gl = _unshard_w_in(W["w_in"], seg_lens, name="unshard_w_in")
    h2 = _rows(_f_rmsnorm, [x1], [mix_norm], [(D, bf16)], name="mix_norm")[0]
    u_p = _mm(h2, w_u, o_seg=True, name="in_u")
    z = _mm(h2, w_z, name="in_z")
    xbc = _mm(h2, w_xbc, name="in_xbc")
    gl = _mm(h2, w_gl, name="in_gate")
    dtr = _mm(h2, w_dt, name="in_dt")

    rep = lambda a: jnp.repeat(a, S5_GROUP, axis=0)
    lr, li, ldt = s5_A_re[0], s5_A_im[0], s5_log_dt[0].reshape(Gs, 1)
    brt = s5_B_re[0].transpose(0, 2, 1).reshape(Gs * S5_GROUP, S5_STATE)
    bit = s5_B_im[0].transpose(0, 2, 1).reshape(Gs * S5_GROUP, S5_STATE)
    prep_args = (lr, li, ldt, rep(lr), rep(li), rep(ldt), brt, bit)
    ar, ai, bbrt, bbit = _s5_prep(prep_args, name="s5_prep")
    a_r, a_i = ar.reshape(1, NS), ai.reshape(1, NS)
    wb_r = _block_diag(bbrt.reshape(nblk, S5_GPB, S5_GROUP, S5_STATE)).astype(bf16)
    wb_i = _block_diag(bbit.reshape(nblk, S5_GPB, S5_GROUP, S5_STATE)).astype(bf16)
    c4r = s5_C_re[0].reshape(nblk, S5_GPB, S5_GROUP, S5_STATE).transpose(0, 1, 3, 2)
    c4i = s5_C_im[0].reshape(nblk, S5_GPB, S5_GROUP, S5_STATE).transpose(0, 1, 3, 2)
    wc_r, wc_i = _block_diag(c4r).astype(bf16), _block_diag(c4i).astype(bf16)
    mix_keys = ("s5_w_glu", "w_proj_s5", "w_proj_ssd", "w_out")
    sl_r, sl_i, p_r, p_i, got = _s5_local_scan(u_p, wb_r, wb_i, a_r, a_i, reverse=False, carry=gather_in(mix_keys), name="s5_scan")
    W.update(zip(mix_keys, got))
    w_glu, w_p5, w_pssd, w_o = whole("s5_w_glu"), whole("w_proj_s5"), whole("w_proj_ssd"), whole("w_out")
    c_r, c_i = _s5_carry(sl_r[T - NSEG:], sl_i[T - NSEG:], p_r, p_i, reverse=False, name="s5_carry")
    s_r, s_i, ylin = _s5_fix_out(sl_r, sl_i, a_r, a_i, c_r, c_i, wc_r, wc_i, name="s5_fix_out")
    g5 = _rows(_f_s5_post, [ylin, u_p], [s5_D], [(D, f32)], name="s5_gelu")[0]
    v5 = _mm(g5, w_glu, name="s5_glu_mm")
    o5 = _rows(_f_glu, [g5, v5], [s5_b_glu], [(D, bf16)], name="s5_glu")[0]
    p5 = _mm(o5, w_p5, a_seg=True, name="proj_s5")

    xc = _conv_fwd(xbc, conv_w_full, conv_b, name="conv")
    bias_p, alog_p = _pad_cols(ssd_dt_bias, LANES), _pad_cols(ssd_A_log, LANES)
    expand = (lax.broadcasted_iota(jnp.int32, (LANES, d_inner), 1) // HEADDIM
              == lax.broadcasted_iota(jnp.int32, (LANES, d_inner), 0)).astype(f32)
    dt_p, da_p, dt4, a4 = _rows(_f_dt_expand, [dtr], [bias_p, alog_p, expand],
                                [(LANES, f32), (LANES, f32), (d_inner, f32), (d_inner, f32)], name="ssd_dt")
    row_l = lambda a: a[:, :H].reshape(nc, CHUNK, G, hpg).transpose(2, 0, 3, 1).reshape(G, nc, hpg * CHUNK)
    ssd_in = (xc, dt4, a4, row_l(dt_p), row_l(da_p), jnp.repeat(ssd_D, HEADDIM, axis=1), _ssd_consts(hpg))
    ffn2_keys = ("ffn2_w_gate", "ffn2_w_up", "ffn2_w_down")
    y_ssd, hs, got = _ssd2_fwd(*ssd_in, d_inner=d_inner, carry=gather_in(ffn2_keys), name="ssd")
    W.update(zip(ffn2_keys, got))
    ffn2_w = (_unshard_pair(W["ffn2_w_gate"], W["ffn2_w_up"], name="unshard_ffn2_gate_up"), whole("ffn2_w_down"))
    yn = _rows(_f_gated_norm, [y_ssd, z], [ssd_norm], [(d_inner, bf16)], name="ssd_gated_norm")[0]
    pssd = _mm(yn, w_pssd, name="proj_ssd")

    merged = _rows(_f_merge, [gl, p5, pssd], [b_gate], [(D, bf16)], name="merge")[0]
    x2 = _mm(merged, w_o, add=x1, name="out_proj")
    x3, sv2, _ = _ffn_fwd(x2, ffn2_norm, ffn2_w[0], lambda _: ffn2_w[1], "ffn2")
    lossv, dx3, d_final = _loss_stage(x3, tgt, final_norm.reshape(1, D), name="loss")

    gw = {}
    gs = {"final_norm": d_final}
    slot_mm = lambda a_, b_, name, **kw: _mm(a_, b_, ta=True, o_blk="m", o_slots=True, out_dtype=bf16, name=name, **kw)
    core = lax.axis_index("c").astype(jnp.int32).reshape(1)
    chip = (2 * lax.axis_index("x") + lax.axis_index("y")).astype(jnp.int32).reshape(1)
    chip_sums, arrived = {}, {}

    def level1(keys, tag):
        sib = _exchange_sibling([gw[k] for k in keys], name="exchange_sibling_" + tag)
        for k, s_ in zip(keys, sib):
            chip_sums[k] = _chip_sum(gw[k], s_, core, name="chip_sum_" + k)
        return [chip_sums[k] for k in keys]

    dx2, gs["ffn2_norm"], gw["ffn2_w_gate"], gw["ffn2_w_up"], gw["ffn2_w_down"], _, _, _ = _ffn_bwd(sv2, *ffn2_w, dx3, "ffn2")

    dmerged = _mm(dx2, w_o, tb=True, name="d_merged")
    gw["w_out"] = slot_mm(merged, dx2, "d_w_out")
    (dgl, dp5, dpssd), (gs["b_gate"],) = _rows_bwd(_f_merge, [gl, p5, pssd], [b_gate], [dmerged], name="merge_bwd", want_rows=[0, 1, 2])

    dyn = _mm(dpssd, w_pssd, tb=True, name="d_yn")
    gw["w_proj_ssd"] = slot_mm(yn, dpssd, "d_w_proj_ssd")
    group_a = ("ffn2_w_gate", "ffn2_w_up", "ffn2_w_down", "w_out", "w_proj_ssd")
    parts_a = level1(group_a, "a")
    (dyssd, dz), (gs["ssd_norm"],) = _rows_bwd(_f_gated_norm, [y_ssd, z], [ssd_norm], [dyn], name="ssd_gated_norm_bwd", want_rows=[0, 1])
    dxs, dbm, dcm, ddtc, ddac, ddtw, ddaw, ddh = _ssd2_bwd(*ssd_in, hs, dyssd, d_inner=d_inner, name="ssd_bwd")

    def fold(col, row):
        col = col.reshape(T, G, LANES)[:, :, :hpg].reshape(T, H)
        row = row.reshape(G, nc, hpg, CHUNK).transpose(1, 3, 0, 2).reshape(T, H)
        return _pad_cols(col + row, LANES)

    (ddtr,), (dbias_p, dalog_p) = _rows_bwd(_f_dt, [dtr], [bias_p, alog_p], [fold(ddtc, ddtw), fold(ddac, ddaw)], name="ssd_dt_bwd", want_rows=[0])
    gs["ssd_dt_bias"], gs["ssd_A_log"], gs["ssd_D"] = dbias_p[:, :H], dalog_p[:, :H], ddh[:, 0, :hpg].reshape(1, H)
    dxbc, d_conv_w, gs["conv_b"], arr = _conv_bwd(
        xbc, conv_w_full, conv_b, [dxs, dbm, dcm], carry=_carry_chips(parts_a), name="conv_bwd")
    arrived.update(zip(group_a, arr))
    cwk = sh["conv_w"].shape[1]
    gw["conv_w"] = d_conv_w.reshape(CONV_K, N_CHIP, 2, cwk).transpose(2, 1, 0, 3).reshape(N_DEV, CONV_K, cwk)

    do5 = _mm(dp5, w_p5, tb=True, o_seg=True, name="d_o5")
    gw["w_proj_s5"] = slot_mm(o5, dp5, "d_w_proj_s5", a_seg=True)
    (dg5a, dv5), (gs["s5_b_glu"],) = _rows_bwd(_f_glu, [g5, v5], [s5_b_glu], [do5], name="s5_glu_bwd", want_rows=[0, 1])
    dg5 = _mm(dv5, w_glu, tb=True, add=dg5a, name="d_g5")
    gw["s5_w_glu"] = slot_mm(g5, dv5, "d_w_glu")
    (dylin, du_a), (gs["s5_D"],) = _rows_bwd(_f_s5_post, [ylin, u_p], [s5_D], [dg5], name="s5_gelu_bwd", want_rows=[0, 1])
    wct_r, wct_i = wc_r.transpose(0, 2, 1), -wc_i.transpose(0, 2, 1)
    ql_r, ql_i, _, _ = _s5_local_scan(dylin, wct_r, wct_i, a_r, -a_i, reverse=True, powers=False, name="s5_scan_bwd")
    cb_r, cb_i = _s5_carry(ql_r[:NSEG], ql_i[:NSEG], p_r, -p_i, reverse=True, name="s5_carry_bwd")
    tc = min(S5_TC, T)

    def before_blocks(s):
        last = s.reshape(T // tc, tc, NS)[:, tc - NSEG:, :]
        wrap = jnp.concatenate([jnp.zeros((1, 1, NS), f32), last[-1:, : NSEG - 1, :]], axis=1)
        return jnp.concatenate([wrap, last[:-1]], axis=0)

    du_p, dwb_r, dwb_i, dwc_r, dwc_i, d_ar, d_ai = _s5_fix_bwd(
        ql_r, ql_i, a_r, -a_i, cb_r, cb_i, s_r, s_i, before_blocks(s_r), before_blocks(s_i), u_p, dylin, du_a, wb_r, wb_i, name="s5_fix_bwd")
    unblk = lambda w: _block_diag_t(w, S5_GROUP, S5_STATE).reshape(Gs * S5_GROUP, S5_STATE)
    rsum = jnp.repeat(jnp.eye(Gs, dtype=f32), S5_GROUP, axis=1)
    d_lr, d_li, d_ldt, d_brt, d_bit = _s5_prep_bwd(
        prep_args, (d_ar.reshape(Gs, S5_STATE), d_ai.reshape(Gs, S5_STATE), unblk(dwb_r), unblk(dwb_i)), rsum, name="s5_prep_bwd")
    gs["s5_A_re"], gs["s5_A_im"], gs["s5_log_dt"] = d_lr, d_li, d_ldt.reshape(1, Gs)
    gs["s5_B_re"] = d_brt.reshape(Gs, S5_GROUP, S5_STATE).transpose(0, 2, 1)
    gs["s5_B_im"] = d_bit.reshape(Gs, S5_GROUP, S5_STATE).transpose(0, 2, 1)
    gs["s5_C_re"] = _block_diag_t(dwc_r, S5_STATE, S5_GROUP).transpose(0, 1, 3, 2).reshape(Gs, S5_GROUP, S5_STATE)
    gs["s5_C_im"] = _block_diag_t(dwc_i, S5_STATE, S5_GROUP).transpose(0, 1, 3, 2).reshape(Gs, S5_GROUP, S5_STATE)

    d_w_in = [_mm(h2, du_p, ta=True, b_seg=True, name="d_w_u"), _mm(h2, dz, ta=True, name="d_w_z"), _mm(h2, dxbc, ta=True, name="d_w_xbc"),
              _mm(h2, ddtr, ta=True, name="d_w_dt"), _mm(h2, dgl, ta=True, name="d_w_gate")]
    gw["w_in"] = _reshard_w_in(d_w_in, seg_lens, sh["w_in"].shape[1], name="reshard_d_w_in")
    group_c = ("w_proj_s5", "s5_w_glu", "conv_w")
    parts_c = level1(group_c + ("w_in",), "c")
    c1 = int(D * 0.45) // 16 * 16
    c2 = c1 + D // 4 // 16 * 16
    win = [parts_c[3][:, :c1], parts_c[3][:, c1:c2], parts_c[3][:, c2:]]
    dh2 = _mm(du_p, w_u, tb=True, a_seg=True, name="d_h2_u")
    dh2, arr = _mm(dz, w_z, tb=True, add=dh2, carry=_carry_chips(parts_c[:3]), name="d_h2_z")
    arrived.update(zip(group_c, arr))
    dh2, (arr0,) = _mm(dxbc, w_xbc, tb=True, add=dh2, carry=_carry_chips([win[0]]), name="d_h2_xbc")
    dh2, (arr1,) = _mm(dgl, w_gl, tb=True, add=dh2, carry=_carry_chips([win[1]]), name="d_h2_gate")
    dh2 = _mm(ddtr, w_dt, tb=True, add=dh2, name="d_h2_dt")
    (dx1,), (gs["mix_norm"],) = _rows_bwd(_f_rmsnorm, [x1], [mix_norm], [dh2], name="mix_norm_bwd", want_rows=[0], adds={0: dx2})

    def carry_ffn1_down(dwd):
        gw["ffn1_w_down"] = dwd
        return _carry_chips(level1(("ffn1_w_down",), "d") + [win[2]])

    def carry_ffn1_gate_up(dwg, dwu):
        gw["ffn1_w_gate"], gw["ffn1_w_up"] = dwg, dwu
        return _carry_chips(level1(("ffn1_w_gate", "ffn1_w_up"), "e"))

    small_shapes = {k: (P[k][0].shape if P[k].ndim > 1 else P[k].shape) for k in _SMALL}
    pack = lambda d: jnp.concatenate([_pad_flat(d[k], TILE_ELEMS) for k in _SMALL]).reshape(-1, LANES)
    early = _carry_gather([pack({**gs, "ffn1_norm": jnp.zeros((1, D), f32)})])
    dx0, gs["ffn1_norm"], _, _, _, (small_rest,), arr_w, arr_h = _ffn_bwd(
        sv1, *ffn1_w, dx1, "ffn1", carry_dact=early, carry_after_dwd=carry_ffn1_down, carry_after_dwgu=carry_ffn1_gate_up)
    arrived["ffn1_w_down"], arr2 = arr_w
    arrived["ffn1_w_gate"], arrived["ffn1_w_up"] = arr_h
    arrived["w_in"] = jnp.concatenate([arr0, arr1, arr2], axis=1)
    head = _pad_flat(gs["ffn1_norm"], TILE_ELEMS).reshape(-1, LANES)
    gsmall = jnp.concatenate([
        _sum_slots(_all_gather([head], name="gather_ffn1_norm_grad")[0], name="sum_ffn1_norm_grad"),
        _sum_slots(small_rest, name="sum_small_grads")[head.shape[0]:]], axis=0)
    snum = {k: math.prod(small_shapes[k]) for k in _SMALL}
    ssz = {k: -(-snum[k] // TILE_ELEMS) * TILE_ELEMS for k in _SMALL}

    grads, delta, new_m, new_v = {}, {}, {}, {}
    for k in _BIG:
        grads[k], delta[k], new_m[k], new_v[k] = _reduce_adamw(
            chip_sums[k], arrived[k], chip, P[k][0], P["m_" + k][0], P["v_" + k][0], name="adamw_" + k)
    d_s, m_s, v_s = _adamw(gsmall, pack({k: P[k] for k in _SMALL}), pack({k: P["m_" + k] for k in _SMALL}),
                           pack({k: P["v_" + k] for k in _SMALL}), name="adamw_small")
    off = 0
    gflat, dflat, mflat, vflat = gsmall.reshape(-1), d_s.reshape(-1), m_s.reshape(-1), v_s.reshape(-1)
    for k in _SMALL:
        n = snum[k]
        grads[k], delta[k], new_m[k], new_v[k] = (a[off:off + n] for a in (gflat, dflat, mflat, vflat))
        off += ssz[k]

    loss = lax.psum(lossv[0, 0], ("x", "y", "c"))
    out = [loss, dx0.reshape(x.shape)]
    for d in (grads, delta, new_m, new_v):
        out += [d[k].reshape(P[k].shape) for k in _WEIGHTS]
    return tuple(out)
```

```python
import math
from typing import Callable, NamedTuple

import jax
import jax.numpy as jnp
from jax import lax
from jax.experimental import pallas as pl
from jax.experimental.pallas import tpu as pltpu

f32 = jnp.float32
bf16 = jnp.bfloat16
_S = jax.ShapeDtypeStruct

EPS = 1e-6
S5_GROUP = 16
S5_STATE = 64
HEADDIM = 64
SSD_STATE = 128
CHUNK = 64
CONV_K = 4
NSEG = 8
S5_GPB = 16
N_DEV = 8
LANES = 128
TILE_ELEMS = 8 * LANES

ADAM_LR = 0.001
ADAM_B1 = 0.9
ADAM_B2 = 0.999
ADAM_EPS = 1e-08
ADAM_WD = 0.01
ADAM_STEP = 10

VMEM_LIMIT = 56 * 1024 * 1024
MM_FULL_K = 3072
MM_MAX_TN = 3072
EPI_TM = 128
MESH = pl.DeviceIdType.MESH


def _cparams(sem=None):
    return pltpu.CompilerParams(dimension_semantics=sem, vmem_limit_bytes=VMEM_LIMIT)


def _pick(dim, pref, align=LANES):
    best = None
    t = align
    while t <= min(dim, pref):
        if dim % t == 0:
            best = t
        t += align
    return best or dim


def _slot_of(k):
    return (k & 1) * (N_DEV // 2) + (k >> 1)


def _mm(a, b, *, name, ta=False, tb=False, a_blk=None, b_blk=None, o_blk=None, o_slots=False, a_seg=False, b_seg=False,
        o_seg=False, tm=None, out_dtype=f32, scale=1.0, add=None, epi=None, carry=None):
    a2, b2 = a.shape[-2:], b.shape[-2:]
    Ma, Ka = (a2[1], a2[0]) if ta else a2
    Kb, Nb = (b2[1], b2[0]) if tb else b2
    M = Ma * (a.shape[0] if a_blk == "m" else 1)
    K = Ka * (a.shape[0] if a_blk == "k" else 1)
    N = Nb * (b.shape[0] if b_blk == "n" else 1)
    assert K == Kb * (b.shape[0] if b_blk == "k" else 1), (a.shape, b.shape, ta, tb, a_blk, b_blk)
    assert (a.ndim == 3) == (a_blk is not None) and (b.ndim == 3) == (b_blk is not None)
    tm = Ma if a_blk == "m" else (tm or _pick(M, 512))
    tn = Nb if b_blk == "n" else _pick(N, MM_MAX_TN)
    if a_blk == "k" or b_blk == "k":
        tk = Ka if a_blk == "k" else Kb
        assert tk == (Kb if b_blk == "k" else tk)
    else:
        tk = K if K <= MM_FULL_K else _pick(K, 1024 if ta else MM_FULL_K)
    if (a_seg and not ta) or o_seg:
        tm = M // NSEG
    if (a_seg and ta) or b_seg:
        tk = K // NSEG
    gm, gn, nk = M // tm, N // tn, K // tk
    assert not (add is not None and (o_seg or o_blk)) and not (o_blk and o_seg)

    if a_seg:
        assert a.ndim == 2
        a = a.reshape(a.shape[0] // NSEG, NSEG * a.shape[1])
        if ta:
            a_spec = pl.BlockSpec((tk, tm), lambda i, j, k: (0, k * (Ma // tm) + i))
        else:
            a_spec = pl.BlockSpec((tm, tk), lambda i, j, k: (0, i * (Ka // tk) + k))
    elif a.ndim == 3:
        lead = (lambda i, k: i) if a_blk == "m" else (lambda i, k: k)
        if ta:
            a_spec = pl.BlockSpec((None, tk, tm), lambda i, j, k: (lead(i, k), 0 if a_blk == "k" else k, 0 if a_blk == "m" else i))
        else:
            a_spec = pl.BlockSpec((None, tm, tk), lambda i, j, k: (lead(i, k), 0 if a_blk == "m" else i, 0 if a_blk == "k" else k))
    else:
        a_spec = pl.BlockSpec((tk, tm), lambda i, j, k: (k, i)) if ta else pl.BlockSpec((tm, tk), lambda i, j, k: (i, k))
    if b_seg:
        assert b.ndim == 2 and not tb
        b = b.reshape(b.shape[0] // NSEG, NSEG * b.shape[1])
        b_spec = pl.BlockSpec((tk, tn), lambda i, j, k: (0, k * (Nb // tn) + j))
    elif b.ndim == 3:
        lead = (lambda j, k: j) if b_blk == "n" else (lambda j, k: k)
        if tb:
            b_spec = pl.BlockSpec((None, tn, tk), lambda i, j, k: (lead(j, k), 0 if b_blk == "n" else j, 0 if b_blk == "k" else k))
        else:
            b_spec = pl.BlockSpec((None, tk, tn), lambda i, j, k: (lead(j, k), 0 if b_blk == "k" else k, 0 if b_blk == "n" else j))
    else:
        b_spec = pl.BlockSpec((tn, tk), lambda i, j, k: (j, k)) if tb else pl.BlockSpec((tk, tn), lambda i, j, k: (k, j))
    slot = _slot_of if o_slots else (lambda k: k)
    if o_blk == "n":
        assert gn == N_DEV or not o_slots
        o_shape, o_spec = (gn, M, tn), pl.BlockSpec((None, tm, tn), lambda i, j, k: (slot(j), i, 0))
    elif o_blk == "m" and o_slots and gm < N_DEV:
        rs = M // N_DEV
        per_tile = tm // rs
        assert per_tile % 2 == 0 and tm % rs == 0
        o_shape = (2, N_CHIP, rs, N)
        o_spec = pl.BlockSpec((2, per_tile // 2, rs, tn), lambda i, j, k: (0, i, 0, j))
    elif o_blk == "m":
        assert gm == N_DEV or not o_slots
        o_shape, o_spec = (gm, tm, N), pl.BlockSpec((None, tm, tn), lambda i, j, k: (slot(i), 0, j))
    elif o_seg:
        o_shape, o_spec = (tm, NSEG * N), pl.BlockSpec((tm, tn), lambda i, j, k: (0, i * (N // tn) + j))
    else:
        o_shape, o_spec = (M, N), pl.BlockSpec((tm, tn), lambda i, j, k: (i, j))
    dims = (((0 if ta else 1,), (1 if tb else 0,)), ((), ()))
    if epi is not None:
        assert gn == 1 and add is None and o_blk is None and not o_seg
        epi_fn, add, epi_w = epi
        o_shape, o_spec = (M, epi_w), pl.BlockSpec((tm, epi_w), lambda i, j, k: (i, 0))
    has_add = add is not None
    add_spec = pl.BlockSpec((tm, add.shape[1]), lambda i, j, k: (i, 0)) if epi is not None else o_spec

    carry = carry or _NO_CARRY
    n_in = 2 + has_add

    def body(*refs):
        own, c_in, c_out, c_sems = _carry_split(carry, refs, n_in, 1)
        a_ref, b_ref = own[0], own[1]
        add_ref = own[2] if has_add else None
        o_ref, acc_ref = own[-2], own[-1]
        i, j, k = pl.program_id(0), pl.program_id(1), pl.program_id(2)
        _carry_start(carry, c_in, c_out, c_sems, (i == 0) & (j == 0) & (k == 0))

        @pl.when(k == 0)
        def _():
            acc_ref[...] = jnp.zeros_like(acc_ref)

        acc_ref[...] += lax.dot_general(a_ref[...].astype(bf16), b_ref[...].astype(bf16), dims, preferred_element_type=f32)

        @pl.when(k == nk - 1)
        def _():
            r = acc_ref[...] * scale
            if epi is not None:
                r = epi_fn(r, add_ref[...].astype(f32))
            elif has_add:
                r = r + add_ref[...].astype(f32)
            if len(o_shape) == 4:
                rs = o_shape[2]
                for chip_l in range(o_ref.shape[1]):
                    for core in range(2):
                        dev = 2 * chip_l + core
                        o_ref[core, chip_l] = r[dev * rs:(dev + 1) * rs].astype(out_dtype)
            else:
                o_ref[...] = r.astype(out_dtype)

        _carry_finish(carry, c_in, c_out, c_sems, (i == gm - 1) & (j == gn - 1) & (k == nk - 1))

    ins = [a, b] + ([add] if has_add else []) + list(carry.ins)
    in_specs = [a_spec, b_spec] + ([add_spec] if has_add else []) + [_ANY] * len(carry.ins)
    res = pl.pallas_call(
        body, name=name, grid=(gm, gn, nk), in_specs=in_specs, out_specs=[o_spec] + [_ANY] * len(carry.out_shapes),
        out_shape=[_S(o_shape, out_dtype)] + list(carry.out_shapes),
        scratch_shapes=[pltpu.VMEM((tm, tn), f32)] + list(carry.sems),
        compiler_params=_cparams(("arbitrary",) * 3 if carry.ins else ("parallel", "parallel", "arbitrary")),
    )(*ins)
    out = res[0]
    if len(o_shape) == 4:
        out = out.reshape(N_DEV, o_shape[2], N)
    elif o_seg:
        out = out.reshape(M, N)
    return (out, list(res[1:])) if carry.ins else out


def _row_tile(T, widths):
    budget = 6 * 1024 * 1024
    tb = max(16, budget // (4 * sum(widths)))
    return _pick(T, tb, align=16)


def _rows(fn, rows, params, outs, *, name, carry=None):
    T = rows[0].shape[0]
    nr, npar = len(rows), len(params)
    tb = _row_tile(T, [r.shape[1] for r in rows] + [w for w, _ in outs])
    carry = carry or _NO_CARRY

    def body(*refs):
        own, c_in, c_out, c_sems = _carry_split(carry, refs, nr + npar, len(outs))
        _carry_start(carry, c_in, c_out, c_sems, pl.program_id(0) == 0)
        ins = [r[...].astype(f32) for r in own[: nr + npar]]
        res = fn(*ins)
        for o_ref, r in zip(own[nr + npar:], res):
            o_ref[...] = r.astype(o_ref.dtype)
        _carry_finish(carry, c_in, c_out, c_sems, pl.program_id(0) == T // tb - 1)

    in_specs = [pl.BlockSpec((tb, r.shape[1]), lambda i: (i, 0)) for r in rows]
    in_specs += [pl.BlockSpec(p.shape, lambda i: (0, 0)) for p in params]
    out_specs = [pl.BlockSpec((tb, w), lambda i: (i, 0)) for w, _ in outs]
    res = pl.pallas_call(
        body, name=name, grid=(T // tb,), in_specs=in_specs + [_ANY] * len(carry.ins),
        out_specs=out_specs + [_ANY] * len(carry.out_shapes), out_shape=[_S((T, w), d) for w, d in outs] + list(carry.out_shapes),
        scratch_shapes=list(carry.sems), compiler_params=_cparams(("arbitrary",) if carry.ins else ("parallel",)),
    )(*rows, *params, *carry.ins)
    return (tuple(res[:len(outs)]), list(res[len(outs):])) if carry.ins else tuple(res)


def _rows_bwd(fn, rows, params, cots, *, name, want_rows, row_dtypes=None, adds=None):
    T = rows[0].shape[0]
    nr, npar, nc = len(rows), len(params), len(cots)
    adds = adds or {}
    add_idx = sorted(adds)
    row_dtypes = row_dtypes or {}
    widths = [r.shape[1] for r in rows] + [c.shape[1] for c in cots] + [rows[i].shape[1] for i in want_rows]
    tb = _row_tile(T, widths)

    def body(*refs):
        ins = [r[...].astype(f32) for r in refs[: nr + npar]]
        cot = tuple(r[...].astype(f32) for r in refs[nr + npar: nr + npar + nc])
        add_refs = refs[nr + npar + nc: nr + npar + nc + len(add_idx)]
        out_refs = refs[nr + npar + nc + len(add_idx):]
        _, vjp = jax.vjp(lambda *a: tuple(fn(*a)), *ins)
        g = vjp(cot)
        for o_ref, i in zip(out_refs[: len(want_rows)], want_rows):
            r = g[i]
            if i in adds:
                r = r + add_refs[add_idx.index(i)][...].astype(f32)
            o_ref[...] = r.astype(o_ref.dtype)
        first = pl.program_id(0) == 0
        for o_ref, gp in zip(out_refs[len(want_rows):], g[nr:]):
            @pl.when(first)
            def _(o_ref=o_ref):
                o_ref[...] = jnp.zeros_like(o_ref)

            o_ref[...] += gp

    in_specs = [pl.BlockSpec((tb, r.shape[1]), lambda i: (i, 0)) for r in rows]
    in_specs += [pl.BlockSpec(p.shape, lambda i: (0, 0)) for p in params]
    in_specs += [pl.BlockSpec((tb, c.shape[1]), lambda i: (i, 0)) for c in cots]
    in_specs += [pl.BlockSpec((tb, adds[i].shape[1]), lambda i_: (i_, 0)) for i in add_idx]
    out_specs = [pl.BlockSpec((tb, rows[i].shape[1]), lambda i_: (i_, 0)) for i in want_rows]
    out_specs += [pl.BlockSpec(p.shape, lambda i: (0, 0)) for p in params]
    out_shape = [_S(rows[i].shape, row_dtypes.get(i, f32)) for i in want_rows] + [_S(p.shape, f32) for p in params]
    res = pl.pallas_call(
        body, name=name, grid=(T // tb,), in_specs=in_specs, out_specs=out_specs, out_shape=out_shape,
        compiler_params=_cparams(("arbitrary",)),
    )(*rows, *params, *cots, *[adds[i] for i in add_idx])
    return list(res[: len(want_rows)]), list(res[len(want_rows):])


def _f_rmsnorm(x, g):
    return (x * lax.rsqrt(jnp.mean(x * x, axis=-1, keepdims=True) + EPS) * g,)


def _f_swiglu(ab):
    F = ab.shape[1] // 2
    return (jax.nn.silu(ab[:, :F]) * ab[:, F:],)


def _f_s5_post(y, u, d):
    return (jax.nn.gelu(y + d * u),)


def _f_glu(g, v, b):
    return (g * jax.nn.sigmoid(v + b),)


def _f_gated_norm(y, z, w):
    return _f_rmsnorm(y * jax.nn.silu(z), w)


def _f_merge(gl, p5, pssd, b):
    D = p5.shape[1]
    gates = jax.nn.sigmoid(gl + b)
    return (gates[:, :D] * p5 + gates[:, D:] * pssd,)


def _f_dt(dtr, bias, a_log):
    dt = jax.nn.softplus(dtr + bias)
    return dt, dt * (-jnp.exp(a_log))


def _f_dt_expand(dtr, bias, a_log, e):
    dt, a = _f_dt(dtr, bias, a_log)
    return dt, a, _doth(dt, e), _doth(a, e)


def _loss_stage(x, tgt, g, *, name):
    T, D = x.shape
    tb = _row_tile(T, [D, D, D])

    def f(xb, gb, tb_):
        y = _f_rmsnorm(xb, gb)[0]
        return 0.5 * jnp.sum(jnp.mean(jnp.square(y - tb_), axis=-1, keepdims=True), axis=0, keepdims=True)

    def body(x_ref, t_ref, g_ref, l_ref, dx_ref, dg_ref):
        tv = t_ref[...]
        val, vjp = jax.vjp(lambda a, b: f(a, b, tv), x_ref[...], g_ref[...])
        dx, dg = vjp(jnp.ones((1, 1), f32))
        dx_ref[...] = dx

        @pl.when(pl.program_id(0) == 0)
        def _():
            l_ref[...] = jnp.zeros_like(l_ref)
            dg_ref[...] = jnp.zeros_like(dg_ref)

        l_ref[...] += jnp.broadcast_to(val, l_ref.shape)
        dg_ref[...] += dg

    row = pl.BlockSpec((tb, D), lambda i: (i, 0))
    par = pl.BlockSpec((1, D), lambda i: (0, 0))
    return pl.pallas_call(
        body, name=name, grid=(T // tb,), in_specs=[row, row, par],
        out_specs=[pl.BlockSpec((1, LANES), lambda i: (0, 0)), row, par],
        out_shape=[_S((1, LANES), f32), _S((T, D), f32), _S((1, D), f32)], compiler_params=_cparams(("arbitrary",)),
    )(x, tgt, g)


CONV_R = 64
HALO = 8


def _conv_shifts_down(ref, t):
    if isinstance(t, int) and t == 0:
        cur = ref[0:CONV_R, :]
        row = lax.broadcasted_iota(jnp.int32, cur.shape, 0)
        return [cur] + [jnp.where(row >= s, pltpu.roll(cur, s, axis=0), 0.0) for s in range(1, CONV_K)]
    win = ref[pl.ds(pl.multiple_of(t * CONV_R - HALO, HALO), CONV_R + HALO), :]
    return [win[HALO:]] + [pltpu.roll(win, s, axis=0)[HALO:] for s in range(1, CONV_K)]


def _conv_shifts_up(ref, t):
    if isinstance(t, int):
        cur = ref[t * CONV_R:(t + 1) * CONV_R, :]
        row = lax.broadcasted_iota(jnp.int32, cur.shape, 0)
        return [cur] + [jnp.where(row < CONV_R - s, pltpu.roll(cur, CONV_R - s, axis=0), 0.0) for s in range(1, CONV_K)]
    win = ref[pl.ds(pl.multiple_of(t * CONV_R, HALO), CONV_R + HALO), :]
    return [win[:CONV_R]] + [pltpu.roll(win, CONV_R + HALO - s, axis=0)[:CONV_R] for s in range(1, CONV_K)]


def _conv_pre(shifted, w, b):
    pre = b
    for k in range(CONV_K):
        pre = pre + w[k:k + 1, :] * shifted[CONV_K - 1 - k]
    return pre


def _conv_fwd(x, w, b, *, name):
    T, C = x.shape
    cb = _pick(C, 256)

    def body(x_ref, w_ref, b_ref, o_ref):
        xv = x_ref[...]
        row = lax.broadcasted_iota(jnp.int32, xv.shape, 0)
        shifted = [xv] + [jnp.where(row >= s, pltpu.roll(xv, s, axis=0), 0.0) for s in range(1, CONV_K)]
        o_ref[...] = jax.nn.silu(_conv_pre(shifted, w_ref[...], b_ref[...]))

    col = pl.BlockSpec((T, cb), lambda j: (0, j))
    return pl.pallas_call(
        body, name=name, grid=(C // cb,), in_specs=[col, pl.BlockSpec((CONV_K, cb), lambda j: (0, j)), pl.BlockSpec((1, cb), lambda j: (0, j))],
        out_specs=col, out_shape=_S((T, C), f32), compiler_params=_cparams(("parallel",)),
    )(x, w, b)


def _conv_bwd(x, w, b, dy, *, name, carry=None):
    T, C = x.shape
    cb = _pick(C, 128)
    carry = carry or _NO_CARRY
    ends = []
    for d in dy:
        ends.append((ends[-1] if ends else 0) + d.shape[1] // cb)
    assert ends[-1] == C // cb and all(d.shape[1] % cb == 0 for d in dy)
    npc = len(dy)
    n_tiles = T // CONV_R

    def body(*refs):
        own, c_in, c_out, c_sems = _carry_split(carry, refs, 3 + npc, 3)
        x_ref, w_ref, b_ref = own[:3]
        dy_refs, (dx_ref, dw_ref, db_ref, dpre_ref) = own[3:3 + npc], own[3 + npc:]
        _carry_start(carry, c_in, c_out, c_sems, pl.program_id(0) == 0)
        j = pl.program_id(0)
        wv, bv = w_ref[...], b_ref[...]

        def fold8(v):
            return jnp.sum(v.reshape(CONV_R // 8, 8, cb), axis=0)

        def first_pass(t, acc):
            rows = slice(0, CONV_R) if isinstance(t, int) else pl.ds(pl.multiple_of(t * CONV_R, CONV_R), CONV_R)
            shifted = _conv_shifts_down(x_ref, t)
            pre = _conv_pre(shifted, wv, bv)
            dyv = dy_refs[-1][rows, :]
            for p in range(npc - 2, -1, -1):
                dyv = jnp.where(j < ends[p], dy_refs[p][rows, :], dyv)
            sg = jax.nn.sigmoid(pre)
            dpre = dyv * sg * (1.0 + pre * (1.0 - sg))
            dpre_ref[rows, :] = dpre
            return tuple(acc[k] + fold8(dpre * shifted[CONV_K - 1 - k]) for k in range(CONV_K)) + (acc[CONV_K] + fold8(dpre),)

        zero = jnp.zeros((8, cb), f32)
        acc = lax.fori_loop(1, n_tiles, first_pass, first_pass(0, (zero,) * (CONV_K + 1)), unroll=3)
        for k in range(CONV_K):
            dw_ref[k:k + 1, :] = jnp.sum(acc[k], axis=0, keepdims=True)
        db_ref[...] = jnp.sum(acc[CONV_K], axis=0, keepdims=True)

        def dx_of(t):
            up = _conv_shifts_up(dpre_ref, t)
            dx = wv[CONV_K - 1:CONV_K, :] * up[0]
            for k in range(CONV_K - 1):
                dx = dx + wv[k:k + 1, :] * up[CONV_K - 1 - k]
            return dx

        def second_pass(t, c):
            dx_ref[pl.ds(pl.multiple_of(t * CONV_R, CONV_R), CONV_R), :] = dx_of(t)
            return c

        lax.fori_loop(0, n_tiles - 1, second_pass, 0, unroll=3)
        dx_ref[(n_tiles - 1) * CONV_R:, :] = dx_of(n_tiles - 1)
        _carry_finish(carry, c_in, c_out, c_sems, pl.program_id(0) == C // cb - 1)

    col = pl.BlockSpec((T, cb), lambda j: (0, j))
    wsp = pl.BlockSpec((CONV_K, cb), lambda j: (0, j))
    bsp = pl.BlockSpec((1, cb), lambda j: (0, j))
    starts = [0] + ends[:-1]
    dy_specs = [pl.BlockSpec((T, cb), lambda j, s=s, e=e: (0, jnp.clip(j, s, e - 1) - s)) for s, e in zip(starts, ends)]
    res = pl.pallas_call(
        body, name=name, grid=(C // cb,), in_specs=[col, wsp, bsp] + dy_specs + [_ANY] * len(carry.ins),
        out_specs=[col, wsp, bsp] + [_ANY] * len(carry.out_shapes),
        out_shape=[_S((T, C), f32), _S((CONV_K, C), f32), _S((1, C), f32)] + list(carry.out_shapes),
        scratch_shapes=[pltpu.VMEM((T, cb), f32)] + list(carry.sems),
        compiler_params=_cparams(("arbitrary",) if carry.ins else ("parallel",)),
    )(x, w, b, *dy, *carry.ins)
    return (*res[:3], list(res[3:]))


def _f_s5_prep(lr, li, ldt, lrb, lib, ldtb, brt, bit):
    def disc(lr_, li_, ldt_):
        dt = jnp.exp(ldt_)
        mag = jnp.exp(lr_ * dt)
        ar, ai = mag * jnp.cos(li_ * dt), mag * jnp.sin(li_ * dt)
        den = lr_ * lr_ + li_ * li_
        cr = ((ar - 1.0) * lr_ + ai * li_) / den
        ci = (ai * lr_ - (ar - 1.0) * li_) / den
        return ar, ai, cr, ci

    ar, ai, _, _ = disc(lr, li, ldt)
    _, _, cr, ci = disc(lrb, lib, ldtb)
    return ar, ai, cr * brt - ci * bit, cr * bit + ci * brt


def _s5_prep(args, *, name):
    G, N = args[0].shape
    GM = args[3].shape[0]

    def body(*refs):
        res = _f_s5_prep(*[r[...] for r in refs[:8]])
        for o, r in zip(refs[8:], res):
            o[...] = r

    return pl.pallas_call(body, name=name, out_shape=[_S((G, N), f32)] * 2 + [_S((GM, N), f32)] * 2)(*args)


def _s5_prep_bwd(args, cots, rsum, *, name):
    G, N = args[0].shape
    GM = args[3].shape[0]

    def body(*refs):
        ins = [r[...] for r in refs[:8]]
        cot = tuple(r[...] for r in refs[8:12])
        rs = refs[12][...]
        _, vjp = jax.vjp(_f_s5_prep, *ins)
        g = vjp(cot)
        fold = lambda v: jnp.dot(rs, v, preferred_element_type=f32, precision=lax.Precision.HIGHEST)
        o = refs[13:]
        o[0][...] = g[0] + fold(g[3])
        o[1][...] = g[1] + fold(g[4])
        o[2][...] = g[2] + fold(jnp.broadcast_to(g[5], (GM, LANES)))[:, 0:1]
        o[3][...] = g[6]
        o[4][...] = g[7]

    return pl.pallas_call(
        body, name=name, out_shape=[_S((G, N), f32), _S((G, N), f32), _S((G, 1), f32), _S((GM, N), f32), _S((GM, N), f32)],
    )(*args, *cots, rsum)


S5_TC = 512


def _s5_local_scan(src, w_r, w_i, a_r, a_i, *, reverse, name, carry=None, powers=True):
    T, C = src.shape
    nblk, cb, sb = w_r.shape
    NS = nblk * sb
    tc = min(S5_TC, T)
    nT, nt = T // tc, tc // NSEG
    tmap = (lambda i: nT - 1 - i) if reverse else (lambda i: i)

    carry = carry or _NO_CARRY

    def body(*refs):
        own, c_in, c_out, c_sems = _carry_split(carry, refs, 5, 4)
        u_ref, wr_ref, wi_ref, ar_ref, ai_ref, sr_ref, si_ref, pr_ref, pi_ref, st_r, st_i, pw_r, pw_i = own
        _carry_start(carry, c_in, c_out, c_sems, (pl.program_id(0) == 0) & (pl.program_id(1) == 0))

        @pl.when(pl.program_id(1) == 0)
        def _():
            st_r[...] = jnp.zeros_like(st_r)
            st_i[...] = jnp.zeros_like(st_i)
            pw_r[...] = jnp.ones_like(pw_r)
            pw_i[...] = jnp.zeros_like(pw_i)

        u = u_ref[...].astype(bf16)
        sr_ref[...] = jnp.dot(u, wr_ref[...], preferred_element_type=f32)
        si_ref[...] = jnp.dot(u, wi_ref[...], preferred_element_type=f32)
        ar = jnp.broadcast_to(ar_ref[...], (NSEG, sb))
        ai = jnp.broadcast_to(ai_ref[...], (NSEG, sb))

        def step(k, c):
            cr, ci, qr, qi = c
            kk = (nt - 1 - k) if reverse else k
            rows = pl.ds(pl.multiple_of(kk * NSEG, NSEG), NSEG)
            nr = ar * cr - ai * ci + sr_ref[rows, :]
            ni = ar * ci + ai * cr + si_ref[rows, :]
            sr_ref[rows, :] = nr
            si_ref[rows, :] = ni
            return (nr, ni, ar * qr - ai * qi, ar * qi + ai * qr) if powers else (nr, ni, qr, qi)

        cr, ci, qr, qi = lax.fori_loop(0, nt, step, (st_r[...], st_i[...], pw_r[...], pw_i[...]), unroll=8)
        st_r[...], st_i[...], pw_r[...], pw_i[...] = cr, ci, qr, qi
        pr_ref[...] = qr
        pi_ref[...] = qi
        _carry_finish(carry, c_in, c_out, c_sems, (pl.program_id(0) == nblk - 1) & (pl.program_id(1) == nT - 1))

    blk = pl.BlockSpec((tc, sb), lambda j, i: (tmap(i), j))
    wsp = pl.BlockSpec((None, cb, sb), lambda j, i: (j, 0, 0))
    asp = pl.BlockSpec((1, sb), lambda j, i: (0, j))
    psp = pl.BlockSpec((NSEG, sb), lambda j, i: (0, j))
    res = pl.pallas_call(
        body, name=name, grid=(nblk, nT),
        in_specs=[pl.BlockSpec((tc, cb), lambda j, i: (tmap(i), j)), wsp, wsp, asp, asp] + [_ANY] * len(carry.ins),
        out_specs=[blk, blk, psp, psp] + [_ANY] * len(carry.out_shapes),
        out_shape=[_S((T, NS), f32)] * 2 + [_S((NSEG, NS), f32)] * 2 + list(carry.out_shapes),
        scratch_shapes=[pltpu.VMEM((NSEG, sb), f32)] * 4 + list(carry.sems),
        compiler_params=_cparams(("arbitrary", "arbitrary") if carry.ins else ("parallel", "arbitrary")),
    )(src, w_r, w_i, a_r, a_i, *carry.ins)
    return (*res[:4], list(res[4:])) if carry.ins else res


def _s5_carry(e_r, e_i, p_r, p_i, *, reverse, name):
    NS = e_r.shape[1]

    def body(er_ref, ei_ref, pr_ref, pi_ref, cr_ref, ci_ref):
        ar, ai = pr_ref[0:1, :], pi_ref[0:1, :]
        cr = jnp.zeros((1, NS), f32)
        ci = jnp.zeros((1, NS), f32)
        order = list(range(NSEG - 1, -1, -1)) if reverse else list(range(NSEG))
        cr_ref[order[0]:order[0] + 1, :] = cr
        ci_ref[order[0]:order[0] + 1, :] = ci
        for prev, q in zip(order[:-1], order[1:]):
            er, ei = er_ref[prev:prev + 1, :], ei_ref[prev:prev + 1, :]
            cr, ci = er + ar * cr - ai * ci, ei + ar * ci + ai * cr
            cr_ref[q:q + 1, :] = cr
            ci_ref[q:q + 1, :] = ci

    return pl.pallas_call(body, name=name, out_shape=[_S((NSEG, NS), f32)] * 2)(e_r, e_i, p_r, p_i)


def _s5_fix_out(sl_r, sl_i, a_r, a_i, c_r, c_i, wc_r, wc_i, *, name):
    T, NS = sl_r.shape
    nblk, sb, cb = wc_r.shape
    tc = min(S5_TC, T)
    nT, nt = T // tc, tc // NSEG

    def body(lr_ref, li_ref, ar_ref, ai_ref, cr_ref, ci_ref, wr_ref, wi_ref, sr_ref, si_ref, y_ref, pw_r, pw_i):
        @pl.when(pl.program_id(1) == 0)
        def _():
            pw_r[...] = jnp.ones_like(pw_r)
            pw_i[...] = jnp.zeros_like(pw_i)

        ar = jnp.broadcast_to(ar_ref[...], (NSEG, sb))
        ai = jnp.broadcast_to(ai_ref[...], (NSEG, sb))
        cr, ci = cr_ref[...], ci_ref[...]

        def step(k, c):
            qr, qi = c
            qr, qi = ar * qr - ai * qi, ar * qi + ai * qr
            rows = pl.ds(pl.multiple_of(k * NSEG, NSEG), NSEG)
            sr_ref[rows, :] = lr_ref[rows, :] + qr * cr - qi * ci
            si_ref[rows, :] = li_ref[rows, :] + qr * ci + qi * cr
            return qr, qi

        qr, qi = lax.fori_loop(0, nt, step, (pw_r[...], pw_i[...]), unroll=8)
        pw_r[...], pw_i[...] = qr, qi
        y_ref[...] = (jnp.dot(sr_ref[...].astype(bf16), wr_ref[...], preferred_element_type=f32)
                      - jnp.dot(si_ref[...].astype(bf16), wi_ref[...], preferred_element_type=f32))

    blk = pl.BlockSpec((tc, sb), lambda j, i: (i, j))
    asp = pl.BlockSpec((1, sb), lambda j, i: (0, j))
    csp = pl.BlockSpec((NSEG, sb), lambda j, i: (0, j))
    wsp = pl.BlockSpec((None, sb, cb), lambda j, i: (j, 0, 0))
    return pl.pallas_call(
        body, name=name, grid=(nblk, nT), in_specs=[blk, blk, asp, asp, csp, csp, wsp, wsp],
        out_specs=[blk, blk, pl.BlockSpec((tc, cb), lambda j, i: (i, j))],
        out_shape=[_S((T, NS), f32)] * 2 + [_S((T, nblk * cb), f32)],
        scratch_shapes=[pltpu.VMEM((NSEG, sb), f32)] * 2, compiler_params=_cparams(("parallel", "arbitrary")),
    )(sl_r, sl_i, a_r, a_i, c_r, c_i, wc_r, wc_i)


def _s5_fix_bwd(ql_r, ql_i, ab_r, ab_i, c_r, c_i, s_r, s_i, sb_r, sb_i, u, dy, du_add, w_r, w_i, *, name):
    T, NS = ql_r.shape
    nblk, cb, sb = w_r.shape
    tc = min(S5_TC, T)
    nT, nt = T // tc, tc // NSEG
    tmap = lambda i: nT - 1 - i

    def body(lr_ref, li_ref, ar_ref, ai_ref, cr_ref, ci_ref, sr_ref, si_ref, br_ref, bi_ref, u_ref, dy_ref, dua_ref, wr_ref, wi_ref,
             du_ref, dwr_ref, dwi_ref, dcr_ref, dci_ref, dar_ref, dai_ref, pw_r, pw_i, ac_r, ac_i, q_r, q_i):
        first = pl.program_id(1) == 0

        @pl.when(first)
        def _():
            pw_r[...] = jnp.ones_like(pw_r)
            pw_i[...] = jnp.zeros_like(pw_i)
            ac_r[...] = jnp.zeros_like(ac_r)
            ac_i[...] = jnp.zeros_like(ac_i)
            dwr_ref[...] = jnp.zeros_like(dwr_ref)
            dwi_ref[...] = jnp.zeros_like(dwi_ref)
            dcr_ref[...] = jnp.zeros_like(dcr_ref)
            dci_ref[...] = jnp.zeros_like(dci_ref)

        ar = jnp.broadcast_to(ar_ref[...], (NSEG, sb))
        ai = jnp.broadcast_to(ai_ref[...], (NSEG, sb))
        cr, ci = cr_ref[...], ci_ref[...]

        def fix(rows, qr, qi, spr, spi, accr, acci):
            qr, qi = ar * qr - ai * qi, ar * qi + ai * qr
            xr = lr_ref[rows, :] + qr * cr - qi * ci
            xi = li_ref[rows, :] + qr * ci + qi * cr
            q_r[rows, :] = xr
            q_i[rows, :] = xi
            return qr, qi, accr + xr * spr + xi * spi, acci + xi * spr - xr * spi

        def step(k, c):
            qr, qi, accr, acci = c
            kk = nt - 1 - k
            rows = pl.ds(pl.multiple_of(kk * NSEG, NSEG), NSEG)
            prev = pl.ds(pl.multiple_of((kk - 1) * NSEG, NSEG), NSEG)
            return fix(rows, qr, qi, sr_ref[prev, :], si_ref[prev, :], accr, acci)

        c = lax.fori_loop(0, nt - 1, step, (pw_r[...], pw_i[...], ac_r[...], ac_i[...]), unroll=7)
        qr, qi, accr, acci = fix(pl.ds(0, NSEG), *c[:2], br_ref[...], bi_ref[...], *c[2:])
        pw_r[...], pw_i[...], ac_r[...], ac_i[...] = qr, qi, accr, acci

        qrb, qib = q_r[...].astype(bf16), q_i[...].astype(bf16)
        nt_dims = (((1,), (1,)), ((), ()))
        tn_dims = (((0,), (0,)), ((), ()))
        du_ref[...] = (dua_ref[...] + lax.dot_general(qrb, wr_ref[...], nt_dims, preferred_element_type=f32)
                       + lax.dot_general(qib, wi_ref[...], nt_dims, preferred_element_type=f32))
        ub = u_ref[...].astype(bf16)
        dwr_ref[...] += lax.dot_general(ub, qrb, tn_dims, preferred_element_type=f32)
        dwi_ref[...] += lax.dot_general(ub, qib, tn_dims, preferred_element_type=f32)
        dyb = dy_ref[...].astype(bf16)
        dcr_ref[...] += lax.dot_general(sr_ref[...].astype(bf16), dyb, tn_dims, preferred_element_type=f32)
        dci_ref[...] -= lax.dot_general(si_ref[...].astype(bf16), dyb, tn_dims, preferred_element_type=f32)

        @pl.when(pl.program_id(1) == nT - 1)
        def _():
            dar_ref[...] = jnp.sum(accr, axis=0, keepdims=True)
            dai_ref[...] = jnp.sum(acci, axis=0, keepdims=True)

    blk = pl.BlockSpec((tc, sb), lambda j, i: (tmap(i), j))
    asp = pl.BlockSpec((1, sb), lambda j, i: (0, j))
    csp = pl.BlockSpec((NSEG, sb), lambda j, i: (0, j))
    bsp = pl.BlockSpec((None, NSEG, sb), lambda j, i: (tmap(i), 0, j))
    chn = pl.BlockSpec((tc, cb), lambda j, i: (tmap(i), j))
    wsp = pl.BlockSpec((None, cb, sb), lambda j, i: (j, 0, 0))
    wcs = pl.BlockSpec((None, sb, cb), lambda j, i: (j, 0, 0))
    return pl.pallas_call(
        body, name=name, grid=(nblk, nT), in_specs=[blk, blk, asp, asp, csp, csp, blk, blk, bsp, bsp, chn, chn, chn, wsp, wsp],
        out_specs=[chn, wsp, wsp, wcs, wcs, asp, asp],
        out_shape=[_S((T, nblk * cb), f32), _S((nblk, cb, sb), f32), _S((nblk, cb, sb), f32), _S((nblk, sb, cb), f32),
                   _S((nblk, sb, cb), f32), _S((1, NS), f32), _S((1, NS), f32)],
        scratch_shapes=[pltpu.VMEM((NSEG, sb), f32)] * 4 + [pltpu.VMEM((tc, sb), f32)] * 2,
        compiler_params=_cparams(("parallel", "arbitrary")),
    )(ql_r, ql_i, ab_r, ab_i, c_r, c_i, s_r, s_i, sb_r, sb_i, u, dy, du_add, w_r, w_i)


SSD2_TB = 512
_NN = (((1,), (0,)), ((), ()))
_NT = (((1,), (1,)), ((), ()))
_TN = (((0,), (0,)), ((), ()))


def _dotf(a, b, dims):
    return lax.dot_general(a.astype(bf16), b.astype(bf16), dims, preferred_element_type=f32)


def _doth(a, b, dims=_NN, sel="b", parts=3):
    x, m = (a, b) if sel == "b" else (b, a)
    m = m.astype(bf16)
    out = None
    for _ in range(parts):
        piece = x.astype(bf16)
        x = x - piece.astype(f32)
        d = lax.dot_general(*((piece, m) if sel == "b" else (m, piece)), dims, preferred_element_type=f32)
        out = d if out is None else out + d
    return out


def _ssd_consts(hpg):
    W = hpg * CHUNK
    i = lax.broadcasted_iota(jnp.int32, (CHUNK, CHUNK), 0)
    j = lax.broadcasted_iota(jnp.int32, (CHUNK, CHUNK), 1)
    tril = (i >= j).astype(f32)
    r = lax.broadcasted_iota(jnp.int32, (W, W), 0)
    c = lax.broadcasted_iota(jnp.int32, (W, W), 1)
    bd = (r // CHUNK == c // CHUNK).astype(f32)
    triu_bd = bd * (r <= c).astype(f32)
    e_r = lax.broadcasted_iota(jnp.int32, (W, LANES), 0)
    e_c = lax.broadcasted_iota(jnp.int32, (W, LANES), 1)
    ered = (e_r // HEADDIM == e_c).astype(f32)
    return tril, jnp.tile(tril, (1, hpg)), bd, triu_bd, ered


def _ssd2_specs(G, hpg, tb, tmap, b_off, c_off):
    W = hpg * HEADDIM
    ncb = tb // CHUNK
    xsp = pl.BlockSpec((tb, W), lambda g, i: (tmap(i), g))
    bsp = pl.BlockSpec((tb, SSD_STATE), lambda g, i: (tmap(i), b_off + g))
    csp = pl.BlockSpec((tb, SSD_STATE), lambda g, i: (tmap(i), c_off + g))
    rsp = pl.BlockSpec((None, ncb, W), lambda g, i: (g, tmap(i), 0))
    dsp = pl.BlockSpec((1, W), lambda g, i: (0, g))
    hsp = pl.BlockSpec((None, ncb, SSD_STATE, W), lambda g, i: (g, tmap(i), 0, 0))
    const = lambda a: pl.BlockSpec(a.shape, lambda g, i: (0, 0))
    return xsp, bsp, csp, rsp, dsp, hsp, const


def _tile_rows(a, n):
    return jnp.concatenate([a] * n, axis=0)


def _ssd2_fwd(xc, dt4, a4, dtw, aw, d4, consts, *, d_inner, name, carry=None):
    carry = carry or _NO_CARRY
    T = xc.shape[0]
    G, nc, W = dtw.shape
    hpg = W // CHUNK
    tb = min(SSD2_TB, T)
    nb, ncb = T // tb, tb // CHUNK
    b_off = d_inner // SSD_STATE
    xsp, bsp, csp, rsp, dsp, hsp, const = _ssd2_specs(G, hpg, tb, lambda i: i, b_off, b_off + G)
    tril, mask4, bd, triu_bd, _ = consts

    def body(*refs):
        own, c_in, c_out, c_sems = _carry_split(carry, refs, 12, 2)
        x_ref, b_ref, c_ref, dt_ref, a_ref, dtw_ref, aw_ref, d_ref, tril_ref, mask_ref, bd_ref, tbd_ref, y_ref, hs_ref, h_scr = own
        _carry_start(carry, c_in, c_out, c_sems, (pl.program_id(0) == 0) & (pl.program_id(1) == 0))

        @pl.when(pl.program_id(1) == 0)
        def _():
            h_scr[...] = jnp.zeros_like(h_scr)

        acs_rows = _doth(aw_ref[...], tbd_ref[...])
        ht = h_scr[...]
        for c in range(ncb):
            rows = slice(c * CHUNK, (c + 1) * CHUNK)
            x, bm, cm = x_ref[rows, :], b_ref[rows, :], c_ref[rows, :]
            acs = _doth(tril_ref[...], a_ref[rows, :], sel="a")
            lmat = jnp.where(mask_ref[...] > 0, jnp.exp(jnp.minimum(acs - acs_rows[c:c + 1, :], 0.0)), 0.0)
            m4 = _dotf(cm, _tile_rows(bm, hpg), _NT) * lmat * dtw_ref[c:c + 1, :]
            xbd = _tile_rows(x, hpg) * bd_ref[...]
            hs_ref[c] = ht
            y_ref[rows, :] = _dotf(m4, xbd, _NN) + _dotf(cm, ht, _NN) * jnp.exp(acs) + d_ref[...] * x
            a_last = acs[CHUNK - 1:CHUNK, :]
            xw = x * (jnp.exp(a_last - acs) * dt_ref[rows, :])
            ht = ht * jnp.exp(a_last) + _dotf(bm, xw, _TN)
        h_scr[...] = ht
        _carry_finish(carry, c_in, c_out, c_sems, (pl.program_id(0) == G - 1) & (pl.program_id(1) == nb - 1))

    res = pl.pallas_call(
        body, name=name, grid=(G, nb),
        in_specs=[xsp, bsp, csp, xsp, xsp, rsp, rsp, dsp, const(tril), const(mask4), const(bd), const(triu_bd)] + [_ANY] * len(carry.ins),
        out_specs=[xsp, hsp] + [_ANY] * len(carry.out_shapes),
        out_shape=[_S((T, G * W), f32), _S((G, nc, SSD_STATE, W), f32)] + list(carry.out_shapes),
        scratch_shapes=[pltpu.VMEM((SSD_STATE, W), f32)] + list(carry.sems),
        compiler_params=_cparams(("arbitrary", "arbitrary") if carry.ins else ("parallel", "arbitrary")),
    )(xc, xc, xc, dt4, a4, dtw, aw, d4, tril, mask4, bd, triu_bd, *carry.ins)
    return res[0], res[1], list(res[2:])


def _ssd2_bwd(xc, dt4, a4, dtw, aw, d4, consts, hs, dy, *, d_inner, name):
    T = xc.shape[0]
    G, nc, W = dtw.shape
    hpg = W // CHUNK
    tb = min(SSD2_TB, T)
    nb, ncb = T // tb, tb // CHUNK
    b_off = d_inner // SSD_STATE
    tmap = lambda i: nb - 1 - i
    xsp, bsp, csp, rsp, dsp, hsp, const = _ssd2_specs(G, hpg, tb, tmap, b_off, b_off + G)
    gsp = pl.BlockSpec((tb, SSD_STATE), lambda g, i: (tmap(i), g))
    ddsp = pl.BlockSpec((None, 1, LANES), lambda g, i: (g, 0, 0))
    tril, mask4, bd, triu_bd, ered = consts

    def body(x_ref, b_ref, c_ref, dt_ref, a_ref, dtw_ref, aw_ref, d_ref, tril_ref, mask_ref, bd_ref, tbd_ref, er_ref, hs_ref, dy_ref,
             dx_ref, db_ref, dc_ref, ddtc_ref, dac_ref, ddtw_ref, daw_ref, dd_ref, g_scr, dd_scr, rw_scr, tl_scr):
        first = pl.program_id(1) == 0

        @pl.when(first)
        def _():
            g_scr[...] = jnp.zeros_like(g_scr)
            dd_scr[...] = jnp.zeros_like(dd_scr)

        mask = mask_ref[...] > 0
        lane_in_block = lax.broadcasted_iota(jnp.int32, mask.shape, 1) & (CHUNK - 1)
        maskt = lax.broadcasted_iota(jnp.int32, mask.shape, 0) <= lane_in_block
        acs_rows = _doth(aw_ref[...], tbd_ref[...])
        dht = g_scr[...]
        dd = dd_scr[...]
        for c in range(ncb - 1, -1, -1):
            rows = slice(c * CHUNK, (c + 1) * CHUNK)
            x, bm, cm, dyc = x_ref[rows, :], b_ref[rows, :], c_ref[rows, :], dy_ref[rows, :]
            dtc, dtr = dt_ref[rows, :], dtw_ref[c:c + 1, :]
            ht = hs_ref[c]
            acs = _doth(tril_ref[...], a_ref[rows, :], sel="a")
            seg = acs - acs_rows[c:c + 1, :]
            lmat = jnp.where(mask, jnp.exp(jnp.minimum(seg, 0.0)), 0.0)
            lmat_t = jnp.where(maskt, jnp.exp(jnp.minimum(-seg, 0.0)), 0.0)
            btile, ctile = _tile_rows(bm, hpg), _tile_rows(cm, hpg)
            g4 = _dotf(cm, btile, _NT)
            gt4 = _dotf(bm, ctile, _NT)
            m4 = g4 * lmat * dtr
            mt4 = gt4 * lmat_t * dtc
            xbd = _tile_rows(x, hpg) * bd_ref[...]
            dybd = _tile_rows(dyc, hpg) * bd_ref[...]
            dm4 = _dotf(dyc, xbd, _NT)
            dmt4 = _dotf(x, dybd, _NT)
            dx = d_ref[...] * dyc + _dotf(mt4, dybd, _NN)
            dd = dd + jnp.sum(dyc * x, axis=0, keepdims=True)
            e4 = dm4 * m4
            dc = _dotf(dm4 * lmat * dtr, btile, _NN)
            db = _dotf(dmt4 * lmat_t * dtc, ctile, _NN)
            decay = jnp.exp(acs)
            yoff = _dotf(cm, ht, _NN) * decay
            dz = dyc * decay
            dc = dc + _dotf(dz, ht, _NT)
            dht_prev = _dotf(cm, dz, _TN)
            a_last = acs[CHUNK - 1:CHUNK, :]
            ea_last = jnp.exp(a_last)
            erel = jnp.exp(a_last - acs)
            dte = erel * dtc
            dxw = _dotf(bm, dht, _NN)
            db = db + _dotf(x * dte, dht, _NT)
            dx = dx + dxw * dte
            q4 = dxw * x
            dacs = e4 + dyc * yoff - q4 * dte
            col = jnp.concatenate([q4 * erel, _doth(tril_ref[...], dacs, _TN, sel="a")], axis=0)
            col = _doth(col, er_ref[...], parts=2)
            ddtc_ref[rows, :] = col[:CHUNK]
            dac_ref[rows, :] = col[CHUNK:]
            ddtw_ref[c:c + 1, :] = jnp.sum(dm4 * g4 * lmat, axis=0, keepdims=True)
            rw_scr[c:c + 1, :] = -jnp.sum(e4, axis=0, keepdims=True)
            tl_scr[c:c + 1, :] = jnp.sum(q4 * dte, axis=0, keepdims=True) + ea_last * jnp.sum(dht * ht, axis=0, keepdims=True)
            dx_ref[rows, :] = dx
            db_ref[rows, :] = db
            dc_ref[rows, :] = dc
            dht = dht_prev + dht * ea_last
        daw_ref[...] = _doth(rw_scr[...], tbd_ref[...], _NT) + _doth(tl_scr[...], bd_ref[...])
        g_scr[...] = dht
        dd_scr[...] = dd

        @pl.when(pl.program_id(1) == nb - 1)
        def _():
            dd_ref[...] = _doth(dd, er_ref[...])

    return pl.pallas_call(
        body, name=name, grid=(G, nb),
        in_specs=[xsp, bsp, csp, xsp, xsp, rsp, rsp, dsp, const(tril), const(mask4), const(bd), const(triu_bd), const(ered), hsp, xsp],
        out_specs=[xsp, gsp, gsp, gsp, gsp, rsp, rsp, ddsp],
        out_shape=[_S((T, G * W), f32), _S((T, G * SSD_STATE), f32), _S((T, G * SSD_STATE), f32), _S((T, G * LANES), f32),
                   _S((T, G * LANES), f32), _S(dtw.shape, f32), _S(dtw.shape, f32), _S((G, 1, LANES), f32)],
        scratch_shapes=[pltpu.VMEM((SSD_STATE, W), f32), pltpu.VMEM((1, W), f32), pltpu.VMEM((ncb, W), f32), pltpu.VMEM((ncb, W), f32)],
        compiler_params=_cparams(("parallel", "arbitrary")),
    )(xc, xc, xc, dt4, a4, dtw, aw, d4, tril, mask4, bd, triu_bd, ered, hs, dy)


def _peers():
    x, y, c = lax.axis_index("x"), lax.axis_index("y"), lax.axis_index("c")
    return x, y, c


_ANY = pl.BlockSpec(memory_space=pl.ANY)
N_CHIP = N_DEV // 2


def _all_gather(shards, *, name):
    n = len(shards)
    carry = _carry_gather(shards)

    def body(*refs):
        x_refs, out_refs, sems = refs[:n], refs[n:2 * n], refs[2 * n:]
        _gather_start(x_refs, out_refs, sems)
        _gather_finish(x_refs, out_refs, sems)

    return pl.pallas_call(
        body, name=name, out_shape=list(carry.out_shapes), in_specs=[_ANY] * n, out_specs=[_ANY] * n, scratch_shapes=list(carry.sems),
    )(*shards)


def _gather_parts(x_refs, out_refs, sems):
    send_sems, recv_sems, local_sems = sems
    x, y, c = _peers()
    me, sibling = (x, y, c), (x, y, 1 - c)
    chips = [(1 - x, y), (x, 1 - y), (1 - x, 1 - y)]
    n = len(x_refs)

    def copy(a, r, block, to, src=None):
        px, py, pc = block
        slot = out_refs[a].at[4 * px + 2 * py + pc]
        return pltpu.make_async_remote_copy(
            src_ref=slot if src is None else src, dst_ref=slot, send_sem=send_sems.at[7 * a + r],
            recv_sem=recv_sems.at[7 * a + r], device_id=to, device_id_type=MESH)

    mine = [pltpu.make_async_copy(x_refs[a], out_refs[a].at[4 * x + 2 * y + c], local_sems.at[a]) for a in range(n)]
    first = []
    for a in range(n):
        first.append(copy(a, 0, me, sibling, src=x_refs[a]))
        first += [copy(a, 1 + j, me, (*chip, c), src=x_refs[a]) for j, chip in enumerate(chips)]
    return copy, mine, first, me, sibling, chips, c, n


def _gather_start(x_refs, out_refs, sems):
    _, mine, first, *_ = _gather_parts(x_refs, out_refs, sems)
    for cp in mine + first:
        cp.start()


def _gather_finish(x_refs, out_refs, sems):
    copy, mine, first, me, sibling, chips, c, n = _gather_parts(x_refs, out_refs, sems)
    passed = []
    for j, chip in enumerate(chips):
        for a in range(n):
            copy(a, 1 + j, (*chip, c), me).wait_recv()
            fwd = copy(a, 4 + j, (*chip, c), sibling)
            fwd.start()
            passed.append(fwd)
    for a in range(n):
        copy(a, 0, sibling, me).wait_recv()
    for j, chip in enumerate(chips):
        for a in range(n):
            copy(a, 4 + j, (*chip, 1 - c), me).wait_recv()
    for cp in first + passed:
        cp.wait_send()
    for cp in mine:
        cp.wait()


def _exchange_sibling(slots, *, name):
    n = len(slots)

    def body(*refs):
        x_refs, sib_refs = refs[:n], refs[n:2 * n]
        send_sems, recv_sems = refs[2 * n:]
        x, y, c = _peers()
        give = [pltpu.make_async_remote_copy(
            src_ref=x_refs[a].at[pl.ds(N_CHIP * (1 - c), N_CHIP)], dst_ref=sib_refs[a], send_sem=send_sems.at[a],
            recv_sem=recv_sems.at[a], device_id=(x, y, 1 - c), device_id_type=MESH) for a in range(n)]
        for cp in give:
            cp.start()
        for cp in give:
            cp.wait_recv()
        for cp in give:
            cp.wait_send()

    return list(pl.pallas_call(
        body, name=name, out_shape=[_S((N_CHIP,) + s.shape[1:], s.dtype) for s in slots], in_specs=[_ANY] * n, out_specs=[_ANY] * n,
        scratch_shapes=[pltpu.SemaphoreType.DMA((n,)), pltpu.SemaphoreType.DMA((n,))],
    )(*slots))


def _chip_sum(slots, sib, core, *, name):
    _, R, W = slots.shape
    tr = _pick(R, max(16, (1 << 20) // (4 * W)), align=16)

    def body(core_ref, x_ref, s_ref, o_ref):
        o_ref[...] = (x_ref[...].astype(f32) + s_ref[...].astype(f32)).astype(o_ref.dtype)

    blk = pl.BlockSpec((None, tr, W), lambda t, i, core_ref: (t, i, 0))
    return pl.pallas_call(
        body, name=name, out_shape=_S(sib.shape, slots.dtype),
        grid_spec=pltpu.PrefetchScalarGridSpec(
            num_scalar_prefetch=1, grid=(N_CHIP, R // tr),
            in_specs=[pl.BlockSpec((None, tr, W), lambda t, i, core_ref: (N_CHIP * core_ref[0] + t, i, 0)), blk], out_specs=blk),
        compiler_params=_cparams(("parallel", "parallel")),
    )(core, slots, sib)


def _chip_out_shapes(parts):
    return [_S((N_CHIP - 1,) + p.shape[1:], p.dtype) for p in parts]


def _chip_sems(n):
    return [pltpu.SemaphoreType.DMA((3 * n,)), pltpu.SemaphoreType.DMA((3 * n,))]


def _chip_copies(p_refs, out_refs, send_sems, recv_sems):
    x, y, c = _peers()
    copies = []
    for j in range(1, N_CHIP):
        tx, ty = x ^ (j >> 1), y ^ (j & 1)
        for a in range(len(p_refs)):
            copies.append(pltpu.make_async_remote_copy(
                src_ref=p_refs[a].at[2 * tx + ty], dst_ref=out_refs[a].at[j - 1], send_sem=send_sems.at[3 * a + j - 1],
                recv_sem=recv_sems.at[3 * a + j - 1], device_id=(tx, ty, c), device_id_type=MESH))
    return copies


def _start_all(copies):
    for cp in copies:
        cp.start()


def _wait_all(copies):
    for cp in copies:
        cp.wait_recv()
    for cp in copies:
        cp.wait_send()


class _Carry(NamedTuple):
    ins: tuple = ()
    out_shapes: tuple = ()
    sems: tuple = ()
    start: Callable = None
    finish: Callable = None


_NO_CARRY = _Carry()


def _carry_chips(parts):
    return _Carry(tuple(parts), tuple(_chip_out_shapes(parts)), tuple(_chip_sems(len(parts))),
                  lambda i, o, s: _start_all(_chip_copies(i, o, *s)), lambda i, o, s: _wait_all(_chip_copies(i, o, *s)))


def _carry_gather(shards):
    n = len(shards)
    sems = (pltpu.SemaphoreType.DMA((7 * n,)), pltpu.SemaphoreType.DMA((7 * n,)), pltpu.SemaphoreType.DMA((n,)))
    return _Carry(tuple(shards), tuple(_S((N_DEV,) + s.shape, s.dtype) for s in shards), sems, _gather_start, _gather_finish)


def _carry_split(carry, refs, n_in, n_out):
    ci, co, cs = len(carry.ins), len(carry.out_shapes), len(carry.sems)
    refs = list(refs)
    own_in, c_in = refs[:n_in], refs[n_in:n_in + ci]
    own_out, c_out = refs[n_in + ci:n_in + ci + n_out], refs[n_in + ci + n_out:n_in + ci + n_out + co]
    rest = refs[n_in + ci + n_out + co:]
    own_scratch, c_sems = rest[:len(rest) - cs], rest[len(rest) - cs:]
    return own_in + own_out + own_scratch, c_in, c_out, c_sems


def _carry_start(carry, c_in, c_out, c_sems, first):
    if carry.ins:
        @pl.when(first)
        def _():
            carry.start(c_in, c_out, c_sems)


def _carry_finish(carry, c_in, c_out, c_sems, last):
    if carry.ins:
        @pl.when(last)
        def _():
            carry.finish(c_in, c_out, c_sems)


def _sum_slots(stack, *, name):
    n, R, W = stack.shape
    tr = _pick(R, 1024, align=8)

    def body(s_ref, o_ref):
        acc = s_ref[0]
        for k in range(1, n):
            acc = acc + s_ref[k]
        o_ref[...] = acc

    return pl.pallas_call(
        body, name=name, grid=(R // tr,), in_specs=[pl.BlockSpec((n, tr, W), lambda i: (0, i, 0))],
        out_specs=pl.BlockSpec((tr, W), lambda i: (i, 0)), out_shape=_S((R, W), f32), compiler_params=_cparams(("parallel",)),
    )(stack)


def _adamw_math(gv, wv, mv, vv):
    c1 = 1.0 / (1.0 - ADAM_B1 ** ADAM_STEP)
    c2 = 1.0 / (1.0 - ADAM_B2 ** ADAM_STEP)
    nm = ADAM_B1 * mv + (1.0 - ADAM_B1) * gv
    nv = ADAM_B2 * vv + (1.0 - ADAM_B2) * jnp.square(gv)
    return -ADAM_LR * ((nm * c1) / (jnp.sqrt(nv * c2) + ADAM_EPS) + ADAM_WD * wv), nm, nv


def _adamw(g, w, m, v, *, name):
    R, W = w.shape
    tr = _pick(R, max(8, (1 << 20) // (4 * W)), align=8)

    def body(g_ref, w_ref, m_ref, v_ref, d_ref, nm_ref, nv_ref):
        d_ref[...], nm_ref[...], nv_ref[...] = _adamw_math(g_ref[...], w_ref[...], m_ref[...], v_ref[...])

    sp = pl.BlockSpec((tr, W), lambda i: (i, 0))
    return pl.pallas_call(
        body, name=name, grid=(R // tr,), in_specs=[sp] * 4, out_specs=[sp] * 3, out_shape=[_S((R, W), f32)] * 3,
        compiler_params=_cparams(("parallel",)),
    )(g, w, m, v)


def _reduce_adamw(own, arrived, chip, w, m, v, *, name):
    n, R, W = arrived.shape
    tr = _pick(R, max(16, (1 << 20) // (4 * W)), align=16)

    def body(chip_ref, o_ref, p_ref, w_ref, m_ref, v_ref, g_ref, d_ref, nm_ref, nv_ref):
        gv = o_ref[...].astype(f32)
        for k in range(n):
            gv = gv + p_ref[k].astype(f32)
        g_ref[...] = gv
        d_ref[...], nm_ref[...], nv_ref[...] = _adamw_math(gv, w_ref[...], m_ref[...], v_ref[...])

    sp = pl.BlockSpec((tr, W), lambda i, chip_ref: (i, 0))
    return pl.pallas_call(
        body, name=name, out_shape=[_S((R, W), f32)] * 4,
        grid_spec=pltpu.PrefetchScalarGridSpec(
            num_scalar_prefetch=1, grid=(R // tr,),
            in_specs=[pl.BlockSpec((None, tr, W), lambda i, chip_ref: (chip_ref[0], i, 0)),
                      pl.BlockSpec((n, tr, W), lambda i, chip_ref: (0, i, 0))] + [sp] * 3, out_specs=[sp] * 4),
        compiler_params=_cparams(("parallel",)),
    )(chip, own, arrived, w, m, v)


def _pieces(seg_start, seg_len, shard_w):
    out, col = [], seg_start
    while col < seg_start + seg_len:
        k, a = divmod(col, shard_w)
        n = min(shard_w - a, seg_start + seg_len - col)
        out.append((k, a, col - seg_start, n))
        col += n
    return out


def _unshard_w_in(g, seg_lens, *, name):
    _, D, w = g.shape
    starts = [sum(seg_lens[:i]) for i in range(len(seg_lens))]
    widths = [max(n, LANES) for n in seg_lens]
    tm = _pick(D, 256, align=16)

    def body(g_ref, *o_refs):
        for o_ref, s0, n in zip(o_refs, starts, seg_lens):
            if n < o_ref.shape[1]:
                o_ref[...] = jnp.zeros_like(o_ref)
            for k, a, off, m in _pieces(s0, n, w):
                o_ref[:, off:off + m] = g_ref[k, :, a:a + m]

    return pl.pallas_call(
        body, name=name, grid=(D // tm,), in_specs=[pl.BlockSpec((N_DEV, tm, w), lambda i: (0, i, 0))],
        out_specs=[pl.BlockSpec((tm, wd), lambda i: (i, 0)) for wd in widths], out_shape=[_S((D, wd), g.dtype) for wd in widths],
        compiler_params=_cparams(("parallel",)),
    )(g)


def _unshard_pair(g1, g2, *, name):
    _, D, w = g1.shape
    tm = _pick(D, 256, align=16)

    def body(a_ref, b_ref, o_ref):
        for i, g_ref in enumerate((a_ref, b_ref)):
            for k in range(N_DEV):
                off = (i * N_DEV + k) * w
                o_ref[:, off:off + w] = g_ref[k]

    blk = pl.BlockSpec((N_DEV, tm, w), lambda i: (0, i, 0))
    return pl.pallas_call(
        body, name=name, grid=(D // tm,), in_specs=[blk, blk], out_specs=pl.BlockSpec((tm, 2 * N_DEV * w), lambda i: (i, 0)),
        out_shape=_S((D, 2 * N_DEV * w), g1.dtype), compiler_params=_cparams(("parallel",)),
    )(g1, g2)


def _reshard_pair(dw, *, name):
    D, w = dw.shape[0], dw.shape[1] // (2 * N_DEV)
    tm = _pick(D, 128, align=16)

    def body(g_ref, a_ref, b_ref):
        for i, o_ref in enumerate((a_ref, b_ref)):
            for k in range(N_DEV):
                off = (i * N_DEV + k) * w
                o_ref[_slot_of(k)] = g_ref[:, off:off + w].astype(o_ref.dtype)

    blk = pl.BlockSpec((N_DEV, tm, w), lambda i: (0, i, 0))
    return pl.pallas_call(
        body, name=name, grid=(D // tm,), in_specs=[pl.BlockSpec((tm, dw.shape[1]), lambda i: (i, 0))], out_specs=[blk, blk],
        out_shape=[_S((N_DEV, D, w), bf16)] * 2, compiler_params=_cparams(("parallel",)),
    )(dw)


def _reshard_w_in(grads, seg_lens, w, *, name):
    D = grads[0].shape[0]
    starts = [sum(seg_lens[:i]) for i in range(len(seg_lens))]
    tm = _pick(D, 128, align=16)

    def body(*refs):
        o_ref = refs[-1]
        for g_ref, s0, n in zip(refs[:-1], starts, seg_lens):
            for k, a, off, m in _pieces(s0, n, w):
                o_ref[_slot_of(k), :, a:a + m] = g_ref[:, off:off + m].astype(o_ref.dtype)

    return pl.pallas_call(
        body, name=name, grid=(D // tm,), in_specs=[pl.BlockSpec((tm, g.shape[1]), lambda i: (i, 0)) for g in grads],
        out_specs=pl.BlockSpec((N_DEV, tm, w), lambda i: (0, i, 0)), out_shape=_S((N_DEV, D, w), bf16),
        compiler_params=_cparams(("parallel",)),
    )(*grads)


def _pad_flat(a, mult):
    a = a.reshape(-1)
    n = -(-a.shape[0] // mult) * mult
    return a if n == a.shape[0] else jnp.pad(a, (0, n - a.shape[0]))


def _pad_cols(a, mult):
    n = -(-a.shape[1] // mult) * mult
    return a if n == a.shape[1] else jnp.pad(a, ((0, 0), (0, n - a.shape[1])))


def _block_diag(t):
    nblk, g, P, Q = t.shape
    eye = jnp.eye(g, dtype=t.dtype)
    return (t[:, :, :, None, :] * eye[None, :, None, :, None]).reshape(nblk, g * P, g * Q)


def _block_diag_t(w, P, Q):
    nblk = w.shape[0]
    g = w.shape[1] // P
    eye = jnp.eye(g, dtype=w.dtype)
    return (w.reshape(nblk, g, P, g, Q) * eye[None, :, None, :, None]).sum(axis=3)


_COLS = ("ffn1_w_gate", "ffn1_w_up", "ffn2_w_gate", "ffn2_w_up")
_ROWS = ("ffn1_w_down", "ffn2_w_down", "s5_w_glu", "w_proj_s5", "w_out", "w_proj_ssd")
_BIG = _COLS + _ROWS + ("w_in", "conv_w")
_SMALL = ("ffn1_norm", "mix_norm", "conv_b", "s5_A_re", "s5_A_im", "s5_log_dt", "s5_B_re", "s5_B_im", "s5_C_re", "s5_C_im",
          "s5_D", "s5_b_glu", "ssd_A_log", "ssd_dt_bias", "ssd_D", "ssd_norm", "b_gate", "ffn2_norm", "final_norm")
_WEIGHTS = ("ffn1_norm", "ffn1_w_gate", "ffn1_w_up", "ffn1_w_down", "mix_norm", "w_in", "conv_w", "conv_b", "s5_A_re", "s5_A_im",
            "s5_log_dt", "s5_B_re", "s5_B_im", "s5_C_re", "s5_C_im", "s5_D", "s5_w_glu", "s5_b_glu", "ssd_A_log", "ssd_dt_bias",
            "ssd_D", "ssd_norm", "w_proj_s5", "w_proj_ssd", "b_gate", "w_out", "ffn2_norm", "ffn2_w_gate", "ffn2_w_up",
            "ffn2_w_down", "final_norm")
def _with_carry(res, carry):
    return res if carry else (res, [])


def _ffn_fwd(x, n, wgu, wd_of, tag, carries=(None, None, None)):
    D = x.shape[1]
    h = _rows(_f_rmsnorm, [x], [n], [(D, bf16)], name=tag + "_norm")[0]
    ab, got0 = _with_carry(_mm(h, wgu, carry=carries[0], out_dtype=bf16, name=tag + "_gate_up"), carries[0])
    wd = wd_of(got0)
    F = wd.shape[0]
    (c,), got1 = _with_carry(_rows(_f_swiglu, [ab], [], [(F, bf16)], carry=carries[1], name=tag + "_act"), carries[1])
    y, got2 = _with_carry(_mm(c, wd, scale=0.5, add=x, carry=carries[2], name=tag + "_down"), carries[2])
    return y, (x, n, h, ab, c), (got0, got1, got2)


def _ffn_bwd(saved, wgu, wd, dy, tag, carry_after_dwd=None, carry_after_dwgu=None):
    x, n, h, ab, c = saved
    F = wd.shape[0]
    def act_bwd(dc, ab_):
        return jax.vjp(lambda t: _f_swiglu(t)[0], ab_)[1](dc)[0]

    dab = _mm(dy, wd, tb=True, scale=0.5, tm=EPI_TM, epi=(act_bwd, ab, 2 * F), out_dtype=bf16, name=tag + "_d_act")
    dwd = _mm(c, dy, ta=True, o_blk="m", o_slots=True, tm=F // 2, out_dtype=bf16, scale=0.5, name=tag + "_d_wdown")
    carry_w = carry_after_dwd(dwd) if carry_after_dwd else None
    dwgu, arr_w = _with_carry(_mm(h, dab, ta=True, carry=carry_w, name=tag + "_d_wgu"), carry_w)
    dwg, dwu = _reshard_pair(dwgu, name=tag + "_reshard_d_wgu")
    carry_h = carry_after_dwgu(dwg, dwu) if carry_after_dwgu else None
    dh, arr_h = _with_carry(_mm(dab, wgu, tb=True, carry=carry_h, name=tag + "_d_h"), carry_h)
    (dx,), (dn,) = _rows_bwd(_f_rmsnorm, [x], [n], [dh], name=tag + "_norm_bwd", want_rows=[0], adds={0: dy})
    return dx, dn, dwg, dwu, dwd, arr_w, arr_h


def kernel(x, ffn1_norm, ffn1_w_gate, ffn1_w_up, ffn1_w_down, mix_norm, w_in, conv_w, conv_b, s5_A_re, s5_A_im, s5_log_dt, s5_B_re, s5_B_im, s5_C_re, s5_C_im, s5_D, s5_w_glu, s5_b_glu, ssd_A_log, ssd_dt_bias, ssd_D, ssd_norm, w_proj_s5, w_proj_ssd, b_gate, w_out, ffn2_norm, ffn2_w_gate, ffn2_w_up, ffn2_w_down, final_norm, loss_target, m_ffn1_norm, m_ffn1_w_gate, m_ffn1_w_up, m_ffn1_w_down, m_mix_norm, m_w_in, m_conv_w, m_conv_b, m_s5_A_re, m_s5_A_im, m_s5_log_dt, m_s5_B_re, m_s5_B_im, m_s5_C_re, m_s5_C_im, m_s5_D, m_s5_w_glu, m_s5_b_glu, m_ssd_A_log, m_ssd_dt_bias, m_ssd_D, m_ssd_norm, m_w_proj_s5, m_w_proj_ssd, m_b_gate, m_w_out, m_ffn2_norm, m_ffn2_w_gate, m_ffn2_w_up, m_ffn2_w_down, m_final_norm, v_ffn1_norm, v_ffn1_w_gate, v_ffn1_w_up, v_ffn1_w_down, v_mix_norm, v_w_in, v_conv_w, v_conv_b, v_s5_A_re, v_s5_A_im, v_s5_log_dt, v_s5_B_re, v_s5_B_im, v_s5_C_re, v_s5_C_im, v_s5_D, v_s5_w_glu, v_s5_b_glu, v_ssd_A_log, v_ssd_dt_bias, v_ssd_D, v_ssd_norm, v_w_proj_s5, v_w_proj_ssd, v_b_gate, v_w_out, v_ffn2_norm, v_ffn2_w_gate, v_ffn2_w_up, v_ffn2_w_down, v_final_norm):
    P = dict(locals())
    T, D = x.shape[1], x.shape[2]
    x0, tgt = x[0], loss_target[0]
    sh = {k: P[k][0] for k in _BIG}

    send = {k: (sh[k] if k == "conv_w" else sh[k].astype(bf16)) for k in _BIG}
    W = {}

    def gather_in(keys):
        return _carry_gather([send[k] for k in keys])

    first_keys = ("ffn1_w_gate", "ffn1_w_up", "conv_w")
    W.update(zip(first_keys, _all_gather([send[k] for k in first_keys], name="gather_weights_first")))
    whole = lambda k: W[k].reshape(-1, D)
    conv_w_full = W["conv_w"].transpose(1, 0, 2).reshape(CONV_K, -1)

    d_inner = N_DEV * sh["w_proj_ssd"].shape[0]
    conv_dim = conv_w_full.shape[1]
    H = ssd_A_log.shape[1]
    G = (conv_dim - d_inner) // (2 * SSD_STATE)
    hpg = H // G
    nc = T // CHUNK
    Gs = D // S5_GROUP
    nblk = Gs // S5_GPB
    NS = Gs * S5_STATE
    seg_lens = (D, d_inner, conv_dim, H, 2 * D)

    cuts = [0, D // 3 // 16 * 16, D // 3 // 16 * 16 + 3 * D // 8 // 16 * 16, D]
    win_rows = [send["w_in"][a_:b_] for a_, b_ in zip(cuts[:-1], cuts[1:])]
    wgu1 = _unshard_pair(W["ffn1_w_gate"], W["ffn1_w_up"], name="unshard_ffn1_gate_up")

    def ffn1_down(got):
        W["ffn1_w_down"] = got[0]
        return whole("ffn1_w_down")

    x1, sv1, (got0, got1, got2) = _ffn_fwd(
        x0, ffn1_norm, wgu1, ffn1_down, "ffn1",
        carries=(_carry_gather([send["ffn1_w_down"], win_rows[0]]), _carry_gather([win_rows[1]]), _carry_gather([win_rows[2]])))
    ffn1_w = (wgu1, whole("ffn1_w_down"))
    W["w_in"] = jnp.concatenate([got0[1], got1[0], got2[0]], axis=1)
    w_u, w_z, w_xbc, w_dt, w_gl = _unshard_w_in(W["w_in"], seg_lens, name="unshard_w_in")
    h2 = _rows(_f_rmsnorm, [x1], [mix_norm], [(D, bf16)], name="mix_norm")[0]
    u_p = _mm(h2, w_u, o_seg=True, name="in_u")
    z = _mm(h2, w_z, name="in_z")
    xbc = _mm(h2, w_xbc, name="in_xbc")
    gl = _mm(h2, w_gl, name="in_gate")
    dtr = _mm(h2, w_dt, name="in_dt")

    rep = lambda a: jnp.repeat(a, S5_GROUP, axis=0)
    lr, li, ldt = s5_A_re[0], s5_A_im[0], s5_log_dt[0].reshape(Gs, 1)
    brt = s5_B_re[0].transpose(0, 2, 1).reshape(Gs * S5_GROUP, S5_STATE)
    bit = s5_B_im[0].transpose(0, 2, 1).reshape(Gs * S5_GROUP, S5_STATE)
    prep_args = (lr, li, ldt, rep(lr), rep(li), rep(ldt), brt, bit)
    ar, ai, bbrt, bbit = _s5_prep(prep_args, name="s5_prep")
    a_r, a_i = ar.reshape(1, NS), ai.reshape(1, NS)
    wb_r = _block_diag(bbrt.reshape(nblk, S5_GPB, S5_GROUP, S5_STATE)).astype(bf16)
    wb_i = _block_diag(bbit.reshape(nblk, S5_GPB, S5_GROUP, S5_STATE)).astype(bf16)
    c4r = s5_C_re[0].reshape(nblk, S5_GPB, S5_GROUP, S5_STATE).transpose(0, 1, 3, 2)
    c4i = s5_C_im[0].reshape(nblk, S5_GPB, S5_GROUP, S5_STATE).transpose(0, 1, 3, 2)
    wc_r, wc_i = _block_diag(c4r).astype(bf16), _block_diag(c4i).astype(bf16)
    mix_keys = ("s5_w_glu", "w_proj_s5", "w_proj_ssd", "w_out")
    sl_r, sl_i, p_r, p_i, got = _s5_local_scan(u_p, wb_r, wb_i, a_r, a_i, reverse=False, carry=gather_in(mix_keys), name="s5_scan")
    W.update(zip(mix_keys, got))
    w_glu, w_p5, w_pssd, w_o = whole("s5_w_glu"), whole("w_proj_s5"), whole("w_proj_ssd"), whole("w_out")
    c_r, c_i = _s5_carry(sl_r[T - NSEG:], sl_i[T - NSEG:], p_r, p_i, reverse=False, name="s5_carry")
    s_r, s_i, ylin = _s5_fix_out(sl_r, sl_i, a_r, a_i, c_r, c_i, wc_r, wc_i, name="s5_fix_out")
    g5 = _rows(_f_s5_post, [ylin, u_p], [s5_D], [(D, f32)], name="s5_gelu")[0]
    v5 = _mm(g5, w_glu, name="s5_glu_mm")
    o5 = _rows(_f_glu, [g5, v5], [s5_b_glu], [(D, bf16)], name="s5_glu")[0]
    p5 = _mm(o5, w_p5, a_seg=True, name="proj_s5")

    xc = _conv_fwd(xbc, conv_w_full, conv_b, name="conv")
    bias_p, alog_p = _pad_cols(ssd_dt_bias, LANES), _pad_cols(ssd_A_log, LANES)
    expand = (lax.broadcasted_iota(jnp.int32, (LANES, d_inner), 1) // HEADDIM
              == lax.broadcasted_iota(jnp.int32, (LANES, d_inner), 0)).astype(f32)
    dt_p, da_p, dt4, a4 = _rows(_f_dt_expand, [dtr], [bias_p, alog_p, expand],
                                [(LANES, f32), (LANES, f32), (d_inner, f32), (d_inner, f32)], name="ssd_dt")
    row_l = lambda a: a[:, :H].reshape(nc, CHUNK, G, hpg).transpose(2, 0, 3, 1).reshape(G, nc, hpg * CHUNK)
    ssd_in = (xc, dt4, a4, row_l(dt_p), row_l(da_p), jnp.repeat(ssd_D, HEADDIM, axis=1), _ssd_consts(hpg))
    ffn2_keys = ("ffn2_w_gate", "ffn2_w_up", "ffn2_w_down")
    y_ssd, hs, got = _ssd2_fwd(*ssd_in, d_inner=d_inner, carry=gather_in(ffn2_keys), name="ssd")
    W.update(zip(ffn2_keys, got))
    ffn2_w = (_unshard_pair(W["ffn2_w_gate"], W["ffn2_w_up"], name="unshard_ffn2_gate_up"), whole("ffn2_w_down"))
    yn = _rows(_f_gated_norm, [y_ssd, z], [ssd_norm], [(d_inner, bf16)], name="ssd_gated_norm")[0]
    pssd = _mm(yn, w_pssd, name="proj_ssd")

    merged = _rows(_f_merge, [gl, p5, pssd], [b_gate], [(D, bf16)], name="merge")[0]
    x2 = _mm(merged, w_o, add=x1, name="out_proj")
    x3, sv2, _ = _ffn_fwd(x2, ffn2_norm, ffn2_w[0], lambda _: ffn2_w[1], "ffn2")
    lossv, dx3, d_final = _loss_stage(x3, tgt, final_norm.reshape(1, D), name="loss")

    gw = {}
    gs = {"final_norm": d_final}
    slot_mm = lambda a_, b_, name, **kw: _mm(a_, b_, ta=True, o_blk="m", o_slots=True, out_dtype=bf16, name=name, **kw)
    core = lax.axis_index("c").astype(jnp.int32).reshape(1)
    chip = (2 * lax.axis_index("x") + lax.axis_index("y")).astype(jnp.int32).reshape(1)
    chip_sums, arrived = {}, {}

    def level1(keys, tag):
        sib = _exchange_sibling([gw[k] for k in keys], name="exchange_sibling_" + tag)
        for k, s_ in zip(keys, sib):
            chip_sums[k] = _chip_sum(gw[k], s_, core, name="chip_sum_" + k)
        return [chip_sums[k] for k in keys]

    dx2, gs["ffn2_norm"], gw["ffn2_w_gate"], gw["ffn2_w_up"], gw["ffn2_w_down"], _, _ = _ffn_bwd(sv2, *ffn2_w, dx3, "ffn2")

    dmerged = _mm(dx2, w_o, tb=True, name="d_merged")
    gw["w_out"] = slot_mm(merged, dx2, "d_w_out")
    (dgl, dp5, dpssd), (gs["b_gate"],) = _rows_bwd(_f_merge, [gl, p5, pssd], [b_gate], [dmerged], name="merge_bwd", want_rows=[0, 1, 2])

    dyn = _mm(dpssd, w_pssd, tb=True, name="d_yn")
    gw["w_proj_ssd"] = slot_mm(yn, dpssd, "d_w_proj_ssd")
    group_a = ("ffn2_w_gate", "ffn2_w_up", "ffn2_w_down", "w_out", "w_proj_ssd")
    parts_a = level1(group_a, "a")
    (dyssd, dz), (gs["ssd_norm"],) = _rows_bwd(_f_gated_norm, [y_ssd, z], [ssd_norm], [dyn], name="ssd_gated_norm_bwd", want_rows=[0, 1])
    dxs, dbm, dcm, ddtc, ddac, ddtw, ddaw, ddh = _ssd2_bwd(*ssd_in, hs, dyssd, d_inner=d_inner, name="ssd_bwd")

    def fold(col, row):
        col = col.reshape(T, G, LANES)[:, :, :hpg].reshape(T, H)
        row = row.reshape(G, nc, hpg, CHUNK).transpose(1, 3, 0, 2).reshape(T, H)
        return _pad_cols(col + row, LANES)

    (ddtr,), (dbias_p, dalog_p) = _rows_bwd(_f_dt, [dtr], [bias_p, alog_p], [fold(ddtc, ddtw), fold(ddac, ddaw)], name="ssd_dt_bwd", want_rows=[0])
    gs["ssd_dt_bias"], gs["ssd_A_log"], gs["ssd_D"] = dbias_p[:, :H], dalog_p[:, :H], ddh[:, 0, :hpg].reshape(1, H)
    dxbc, d_conv_w, gs["conv_b"], arr = _conv_bwd(
        xbc, conv_w_full, conv_b, [dxs, dbm, dcm], carry=_carry_chips(parts_a), name="conv_bwd")
    arrived.update(zip(group_a, arr))
    cwk = sh["conv_w"].shape[1]
    gw["conv_w"] = d_conv_w.reshape(CONV_K, N_CHIP, 2, cwk).transpose(2, 1, 0, 3).reshape(N_DEV, CONV_K, cwk)

    do5 = _mm(dp5, w_p5, tb=True, o_seg=True, name="d_o5")
    gw["w_proj_s5"] = slot_mm(o5, dp5, "d_w_proj_s5", a_seg=True)
    (dg5a, dv5), (gs["s5_b_glu"],) = _rows_bwd(_f_glu, [g5, v5], [s5_b_glu], [do5], name="s5_glu_bwd", want_rows=[0, 1])
    dg5 = _mm(dv5, w_glu, tb=True, add=dg5a, name="d_g5")
    gw["s5_w_glu"] = slot_mm(g5, dv5, "d_w_glu")
    (dylin, du_a), (gs["s5_D"],) = _rows_bwd(_f_s5_post, [ylin, u_p], [s5_D], [dg5], name="s5_gelu_bwd", want_rows=[0, 1])
    wct_r, wct_i = wc_r.transpose(0, 2, 1), -wc_i.transpose(0, 2, 1)
    ql_r, ql_i, _, _ = _s5_local_scan(dylin, wct_r, wct_i, a_r, -a_i, reverse=True, powers=False, name="s5_scan_bwd")
    cb_r, cb_i = _s5_carry(ql_r[:NSEG], ql_i[:NSEG], p_r, -p_i, reverse=True, name="s5_carry_bwd")
    tc = min(S5_TC, T)

    def before_blocks(s):
        last = s.reshape(T // tc, tc, NS)[:, tc - NSEG:, :]
        wrap = jnp.concatenate([jnp.zeros((1, 1, NS), f32), last[-1:, : NSEG - 1, :]], axis=1)
        return jnp.concatenate([wrap, last[:-1]], axis=0)

    du_p, dwb_r, dwb_i, dwc_r, dwc_i, d_ar, d_ai = _s5_fix_bwd(
        ql_r, ql_i, a_r, -a_i, cb_r, cb_i, s_r, s_i, before_blocks(s_r), before_blocks(s_i), u_p, dylin, du_a, wb_r, wb_i, name="s5_fix_bwd")
    unblk = lambda w: _block_diag_t(w, S5_GROUP, S5_STATE).reshape(Gs * S5_GROUP, S5_STATE)
    rsum = jnp.repeat(jnp.eye(Gs, dtype=f32), S5_GROUP, axis=1)
    d_lr, d_li, d_ldt, d_brt, d_bit = _s5_prep_bwd(
        prep_args, (d_ar.reshape(Gs, S5_STATE), d_ai.reshape(Gs, S5_STATE), unblk(dwb_r), unblk(dwb_i)), rsum, name="s5_prep_bwd")
    gs["s5_A_re"], gs["s5_A_im"], gs["s5_log_dt"] = d_lr, d_li, d_ldt.reshape(1, Gs)
    gs["s5_B_re"] = d_brt.reshape(Gs, S5_GROUP, S5_STATE).transpose(0, 2, 1)
    gs["s5_B_im"] = d_bit.reshape(Gs, S5_GROUP, S5_STATE).transpose(0, 2, 1)
    gs["s5_C_re"] = _block_diag_t(dwc_r, S5_STATE, S5_GROUP).transpose(0, 1, 3, 2).reshape(Gs, S5_GROUP, S5_STATE)
    gs["s5_C_im"] = _block_diag_t(dwc_i, S5_STATE, S5_GROUP).transpose(0, 1, 3, 2).reshape(Gs, S5_GROUP, S5_STATE)

    d_w_in = [_mm(h2, du_p, ta=True, b_seg=True, name="d_w_u"), _mm(h2, dz, ta=True, name="d_w_z"), _mm(h2, dxbc, ta=True, name="d_w_xbc"),
              _mm(h2, ddtr, ta=True, name="d_w_dt"), _mm(h2, dgl, ta=True, name="d_w_gate")]
    gw["w_in"] = _reshard_w_in(d_w_in, seg_lens, sh["w_in"].shape[1], name="reshard_d_w_in")
    group_c = ("w_proj_s5", "s5_w_glu", "conv_w")
    parts_c = level1(group_c + ("w_in",), "c")
    c1 = int(D * 0.45) // 16 * 16
    c2 = c1 + D // 4 // 16 * 16
    win = [parts_c[3][:, :c1], parts_c[3][:, c1:c2], parts_c[3][:, c2:]]
    dh2 = _mm(du_p, w_u, tb=True, a_seg=True, name="d_h2_u")
    dh2, arr = _mm(dz, w_z, tb=True, add=dh2, carry=_carry_chips(parts_c[:3]), name="d_h2_z")
    arrived.update(zip(group_c, arr))
    dh2, (arr0,) = _mm(dxbc, w_xbc, tb=True, add=dh2, carry=_carry_chips([win[0]]), name="d_h2_xbc")
    dh2, (arr1,) = _mm(dgl, w_gl, tb=True, add=dh2, carry=_carry_chips([win[1]]), name="d_h2_gate")
    dh2 = _mm(ddtr, w_dt, tb=True, add=dh2, name="d_h2_dt")
    (dx1,), (gs["mix_norm"],) = _rows_bwd(_f_rmsnorm, [x1], [mix_norm], [dh2], name="mix_norm_bwd", want_rows=[0], adds={0: dx2})

    def carry_ffn1_down(dwd):
        gw["ffn1_w_down"] = dwd
        return _carry_chips(level1(("ffn1_w_down",), "d") + [win[2]])

    def carry_ffn1_gate_up(dwg, dwu):
        gw["ffn1_w_gate"], gw["ffn1_w_up"] = dwg, dwu
        return _carry_chips(level1(("ffn1_w_gate", "ffn1_w_up"), "e"))

    dx0, gs["ffn1_norm"], _, _, _, arr_w, arr_h = _ffn_bwd(
        sv1, *ffn1_w, dx1, "ffn1", carry_after_dwd=carry_ffn1_down, carry_after_dwgu=carry_ffn1_gate_up)
    arrived["ffn1_w_down"], arr2 = arr_w
    arrived["ffn1_w_gate"], arrived["ffn1_w_up"] = arr_h
    arrived["w_in"] = jnp.concatenate([arr0, arr1, arr2], axis=1)

    small_shapes = {k: (P[k][0].shape if P[k].ndim > 1 else P[k].shape) for k in _SMALL}
    pack = lambda d: jnp.concatenate([_pad_flat(d[k], TILE_ELEMS) for k in _SMALL]).reshape(-1, LANES)
    gsmall = _sum_slots(_all_gather([pack(gs)], name="gather_small_grads")[0], name="sum_small_grads")
    snum = {k: math.prod(small_shapes[k]) for k in _SMALL}
    ssz = {k: -(-snum[k] // TILE_ELEMS) * TILE_ELEMS for k in _SMALL}

    grads, delta, new_m, new_v = {}, {}, {}, {}
    for k in _BIG:
        grads[k], delta[k], new_m[k], new_v[k] = _reduce_adamw(
            chip_sums[k], arrived[k], chip, P[k][0], P["m_" + k][0], P["v_" + k][0], name="adamw_" + k)
    d_s, m_s, v_s = _adamw(gsmall, pack({k: P[k] for k in _SMALL}), pack({k: P["m_" + k] for k in _SMALL}),
                           pack({k: P["v_" + k] for k in _SMALL}), name="adamw_small")
    off = 0
    gflat, dflat, mflat, vflat = gsmall.reshape(-1), d_s.reshape(-1), m_s.reshape(-1), v_s.reshape(-1)
    for k in _SMALL:
        n = snum[k]
        grads[k], delta[k], new_m[k], new_v[k] = (a[off:off + n] for a in (gflat, dflat, mflat, vflat))
        off += ssz[k]

    loss = lax.psum(lossv[0, 0], ("x", "y", "c"))
    out = [loss, dx0.reshape(x.shape)]
    for d in (grads, delta, new_m, new_v):
        out += [d[k].reshape(P[k].shape) for k in _WEIGHTS]
    return tuple(out)
```

```python
import math
from typing import Callable, NamedTuple

import jax
import jax.numpy as jnp
from jax import lax
from jax.experimental import pallas as pl
from jax.experimental.pallas import tpu as pltpu

f32 = jnp.float32
bf16 = jnp.bfloat16
_S = jax.ShapeDtypeStruct

EPS = 1e-6
S5_GROUP = 16
S5_STATE = 64
HEADDIM = 64
SSD_STATE = 128
CHUNK = 64
CONV_K = 4
NSEG = 8
S5_GPB = 16
N_DEV = 8
LANES = 128
TILE_ELEMS = 8 * LANES

ADAM_LR = 0.001
ADAM_B1 = 0.9
ADAM_B2 = 0.999
ADAM_EPS = 1e-08
ADAM_WD = 0.01
ADAM_STEP = 10

VMEM_LIMIT = 56 * 1024 * 1024
MM_FULL_K = 3072
MM_MAX_TN = 3072
EPI_TM = 128
EW_BLOCK_ELEMS = 1 << 20
MESH = pl.DeviceIdType.MESH


def _cparams(sem=None):
    return pltpu.CompilerParams(dimension_semantics=sem, vmem_limit_bytes=VMEM_LIMIT)


def _pick(dim, pref, align=LANES):
    best = None
    t = align
    while t <= min(dim, pref):
        if dim % t == 0:
            best = t
        t += align
    return best or dim


def _slot_of(k):
    return (k & 1) * (N_DEV // 2) + (k >> 1)


def _mm(a, b, *, name, ta=False, tb=False, a_blk=None, b_blk=None, o_blk=None, o_slots=False, a_seg=False, b_seg=False,
        o_seg=False, tm=None, out_dtype=f32, scale=1.0, add=None, epi=None, carry=None):
    a2, b2 = a.shape[-2:], b.shape[-2:]
    Ma, Ka = (a2[1], a2[0]) if ta else a2
    Kb, Nb = (b2[1], b2[0]) if tb else b2
    M = Ma * (a.shape[0] if a_blk == "m" else 1)
    K = Ka * (a.shape[0] if a_blk == "k" else 1)
    N = Nb * (b.shape[0] if b_blk == "n" else 1)
    assert K == Kb * (b.shape[0] if b_blk == "k" else 1), (a.shape, b.shape, ta, tb, a_blk, b_blk)
    assert (a.ndim == 3) == (a_blk is not None) and (b.ndim == 3) == (b_blk is not None)
    tm = Ma if a_blk == "m" else (tm or _pick(M, 512))
    tn = Nb if b_blk == "n" else _pick(N, MM_MAX_TN)
    if a_blk == "k" or b_blk == "k":
        tk = Ka if a_blk == "k" else Kb
        assert tk == (Kb if b_blk == "k" else tk)
    else:
        tk = K if K <= MM_FULL_K else _pick(K, 1024 if ta else MM_FULL_K)
    if (a_seg and not ta) or o_seg:
        tm = M // NSEG
    if (a_seg and ta) or b_seg:
        tk = K // NSEG
    gm, gn, nk = M // tm, N // tn, K // tk
    assert not (add is not None and (o_seg or o_blk)) and not (o_blk and o_seg)

    if a_seg:
        assert a.ndim == 2
        a = a.reshape(a.shape[0] // NSEG, NSEG * a.shape[1])
        if ta:
            a_spec = pl.BlockSpec((tk, tm), lambda i, j, k: (0, k * (Ma // tm) + i))
        else:
            a_spec = pl.BlockSpec((tm, tk), lambda i, j, k: (0, i * (Ka // tk) + k))
    elif a.ndim == 3:
        lead = (lambda i, k: i) if a_blk == "m" else (lambda i, k: k)
        if ta:
            a_spec = pl.BlockSpec((None, tk, tm), lambda i, j, k: (lead(i, k), 0 if a_blk == "k" else k, 0 if a_blk == "m" else i))
        else:
            a_spec = pl.BlockSpec((None, tm, tk), lambda i, j, k: (lead(i, k), 0 if a_blk == "m" else i, 0 if a_blk == "k" else k))
    else:
        a_spec = pl.BlockSpec((tk, tm), lambda i, j, k: (k, i)) if ta else pl.BlockSpec((tm, tk), lambda i, j, k: (i, k))
    if b_seg:
        assert b.ndim == 2 and not tb
        b = b.reshape(b.shape[0] // NSEG, NSEG * b.shape[1])
        b_spec = pl.BlockSpec((tk, tn), lambda i, j, k: (0, k * (Nb // tn) + j))
    elif b.ndim == 3:
        lead = (lambda j, k: j) if b_blk == "n" else (lambda j, k: k)
        if tb:
            b_spec = pl.BlockSpec((None, tn, tk), lambda i, j, k: (lead(j, k), 0 if b_blk == "n" else j, 0 if b_blk == "k" else k))
        else:
            b_spec = pl.BlockSpec((None, tk, tn), lambda i, j, k: (lead(j, k), 0 if b_blk == "k" else k, 0 if b_blk == "n" else j))
    else:
        b_spec = pl.BlockSpec((tn, tk), lambda i, j, k: (j, k)) if tb else pl.BlockSpec((tk, tn), lambda i, j, k: (k, j))
    slot = _slot_of if o_slots else (lambda k: k)
    if o_blk == "n":
        assert gn == N_DEV or not o_slots
        o_shape, o_spec = (gn, M, tn), pl.BlockSpec((None, tm, tn), lambda i, j, k: (slot(j), i, 0))
    elif o_blk == "m" and o_slots and gm < N_DEV:
        rs = M // N_DEV
        per_tile = tm // rs
        assert per_tile % 2 == 0 and tm % rs == 0
        o_shape = (2, N_CHIP, rs, N)
        o_spec = pl.BlockSpec((2, per_tile // 2, rs, tn), lambda i, j, k: (0, i, 0, j))
    elif o_blk == "m":
        assert gm == N_DEV or not o_slots
        o_shape, o_spec = (gm, tm, N), pl.BlockSpec((None, tm, tn), lambda i, j, k: (slot(i), 0, j))
    elif o_seg:
        o_shape, o_spec = (tm, NSEG * N), pl.BlockSpec((tm, tn), lambda i, j, k: (0, i * (N // tn) + j))
    else:
        o_shape, o_spec = (M, N), pl.BlockSpec((tm, tn), lambda i, j, k: (i, j))
    dims = (((0 if ta else 1,), (1 if tb else 0,)), ((), ()))
    if epi is not None:
        assert gn == 1 and add is None and o_blk is None and not o_seg
        epi_fn, add, epi_w = epi
        o_shape, o_spec = (M, epi_w), pl.BlockSpec((tm, epi_w), lambda i, j, k: (i, 0))
    has_add = add is not None
    add_spec = pl.BlockSpec((tm, add.shape[1]), lambda i, j, k: (i, 0)) if epi is not None else o_spec

    carry = carry or _NO_CARRY
    n_in = 2 + has_add

    def body(*refs):
        own, c_in, c_out, c_sems = _carry_split(carry, refs, n_in, 1)
        a_ref, b_ref = own[0], own[1]
        add_ref = own[2] if has_add else None
        o_ref, acc_ref = own[-2], own[-1]
        i, j, k = pl.program_id(0), pl.program_id(1), pl.program_id(2)
        _carry_start(carry, c_in, c_out, c_sems, (i == 0) & (j == 0) & (k == 0))

        @pl.when(k == 0)
        def _():
            acc_ref[...] = jnp.zeros_like(acc_ref)

        acc_ref[...] += lax.dot_general(a_ref[...].astype(bf16), b_ref[...].astype(bf16), dims, preferred_element_type=f32)

        @pl.when(k == nk - 1)
        def _():
            r = acc_ref[...] * scale
            if epi is not None:
                r = epi_fn(r, add_ref[...].astype(f32))
            elif has_add:
                r = r + add_ref[...].astype(f32)
            if len(o_shape) == 4:
                rs = o_shape[2]
                for chip_l in range(o_ref.shape[1]):
                    for core in range(2):
                        dev = 2 * chip_l + core
                        o_ref[core, chip_l] = r[dev * rs:(dev + 1) * rs].astype(out_dtype)
            else:
                o_ref[...] = r.astype(out_dtype)

        _carry_finish(carry, c_in, c_out, c_sems, (i == gm - 1) & (j == gn - 1) & (k == nk - 1))

    ins = [a, b] + ([add] if has_add else []) + list(carry.ins)
    in_specs = [a_spec, b_spec] + ([add_spec] if has_add else []) + [_ANY] * len(carry.ins)
    res = pl.pallas_call(
        body, name=name, grid=(gm, gn, nk), in_specs=in_specs, out_specs=[o_spec] + [_ANY] * len(carry.out_shapes),
        out_shape=[_S(o_shape, out_dtype)] + list(carry.out_shapes),
        scratch_shapes=[pltpu.VMEM((tm, tn), f32)] + list(carry.sems),
        compiler_params=_cparams(("arbitrary",) * 3 if carry.ins else ("parallel", "parallel", "arbitrary")),
    )(*ins)
    out = res[0]
    if len(o_shape) == 4:
        out = out.reshape(N_DEV, o_shape[2], N)
    elif o_seg:
        out = out.reshape(M, N)
    return (out, list(res[1:])) if carry.ins else out


def _row_tile(T, widths):
    budget = 6 * 1024 * 1024
    tb = max(16, budget // (4 * sum(widths)))
    return _pick(T, tb, align=16)


def _rows(fn, rows, params, outs, *, name, carry=None):
    T = rows[0].shape[0]
    nr, npar = len(rows), len(params)
    tb = _row_tile(T, [r.shape[1] for r in rows] + [w for w, _ in outs])
    carry = carry or _NO_CARRY

    def body(*refs):
        own, c_in, c_out, c_sems = _carry_split(carry, refs, nr + npar, len(outs))
        _carry_start(carry, c_in, c_out, c_sems, pl.program_id(0) == 0)
        ins = [r[...].astype(f32) for r in own[: nr + npar]]
        res = fn(*ins)
        for o_ref, r in zip(own[nr + npar:], res):
            o_ref[...] = r.astype(o_ref.dtype)
        _carry_finish(carry, c_in, c_out, c_sems, pl.program_id(0) == T // tb - 1)

    in_specs = [pl.BlockSpec((tb, r.shape[1]), lambda i: (i, 0)) for r in rows]
    in_specs += [pl.BlockSpec(p.shape, lambda i: (0, 0)) for p in params]
    out_specs = [pl.BlockSpec((tb, w), lambda i: (i, 0)) for w, _ in outs]
    res = pl.pallas_call(
        body, name=name, grid=(T // tb,), in_specs=in_specs + [_ANY] * len(carry.ins),
        out_specs=out_specs + [_ANY] * len(carry.out_shapes), out_shape=[_S((T, w), d) for w, d in outs] + list(carry.out_shapes),
        scratch_shapes=list(carry.sems), compiler_params=_cparams(("arbitrary",) if carry.ins else ("parallel",)),
    )(*rows, *params, *carry.ins)
    return (tuple(res[:len(outs)]), list(res[len(outs):])) if carry.ins else tuple(res)


def _rows_bwd(fn, rows, params, cots, *, name, want_rows, row_dtypes=None, adds=None):
    T = rows[0].shape[0]
    nr, npar, nc = len(rows), len(params), len(cots)
    adds = adds or {}
    add_idx = sorted(adds)
    row_dtypes = row_dtypes or {}
    widths = [r.shape[1] for r in rows] + [c.shape[1] for c in cots] + [rows[i].shape[1] for i in want_rows]
    tb = _row_tile(T, widths)

    def body(*refs):
        ins = [r[...].astype(f32) for r in refs[: nr + npar]]
        cot = tuple(r[...].astype(f32) for r in refs[nr + npar: nr + npar + nc])
        add_refs = refs[nr + npar + nc: nr + npar + nc + len(add_idx)]
        out_refs = refs[nr + npar + nc + len(add_idx):]
        _, vjp = jax.vjp(lambda *a: tuple(fn(*a)), *ins)
        g = vjp(cot)
        for o_ref, i in zip(out_refs[: len(want_rows)], want_rows):
            r = g[i]
            if i in adds:
                r = r + add_refs[add_idx.index(i)][...].astype(f32)
            o_ref[...] = r.astype(o_ref.dtype)
        first = pl.program_id(0) == 0
        for o_ref, gp in zip(out_refs[len(want_rows):], g[nr:]):
            @pl.when(first)
            def _(o_ref=o_ref):
                o_ref[...] = jnp.zeros_like(o_ref)

            o_ref[...] += gp

    in_specs = [pl.BlockSpec((tb, r.shape[1]), lambda i: (i, 0)) for r in rows]
    in_specs += [pl.BlockSpec(p.shape, lambda i: (0, 0)) for p in params]
    in_specs += [pl.BlockSpec((tb, c.shape[1]), lambda i: (i, 0)) for c in cots]
    in_specs += [pl.BlockSpec((tb, adds[i].shape[1]), lambda i_: (i_, 0)) for i in add_idx]
    out_specs = [pl.BlockSpec((tb, rows[i].shape[1]), lambda i_: (i_, 0)) for i in want_rows]
    out_specs += [pl.BlockSpec(p.shape, lambda i: (0, 0)) for p in params]
    out_shape = [_S(rows[i].shape, row_dtypes.get(i, f32)) for i in want_rows] + [_S(p.shape, f32) for p in params]
    res = pl.pallas_call(
        body, name=name, grid=(T // tb,), in_specs=in_specs, out_specs=out_specs, out_shape=out_shape,
        compiler_params=_cparams(("arbitrary",)),
    )(*rows, *params, *cots, *[adds[i] for i in add_idx])
    return list(res[: len(want_rows)]), list(res[len(want_rows):])


def _f_rmsnorm(x, g):
    return (x * lax.rsqrt(jnp.mean(x * x, axis=-1, keepdims=True) + EPS) * g,)


def _f_swiglu(ab):
    F = ab.shape[1] // 2
    return (jax.nn.silu(ab[:, :F]) * ab[:, F:],)


def _f_s5_post(y, u, d):
    return (jax.nn.gelu(y + d * u),)


def _f_glu(g, v, b):
    return (g * jax.nn.sigmoid(v + b),)


def _f_gated_norm(y, z, w):
    return _f_rmsnorm(y * jax.nn.silu(z), w)


def _f_merge(gl, p5, pssd, b):
    D = p5.shape[1]
    gates = jax.nn.sigmoid(gl + b)
    return (gates[:, :D] * p5 + gates[:, D:] * pssd,)


def _f_dt(dtr, bias, a_log):
    dt = jax.nn.softplus(dtr + bias)
    return dt, dt * (-jnp.exp(a_log))


def _f_dt_expand(dtr, bias, a_log, e):
    dt, a = _f_dt(dtr, bias, a_log)
    return dt, a, _doth(dt, e), _doth(a, e)


def _loss_stage(x, tgt, g, *, name):
    T, D = x.shape
    tb = _row_tile(T, [D, D, D])

    def f(xb, gb, tb_):
        y = _f_rmsnorm(xb, gb)[0]
        return 0.5 * jnp.sum(jnp.mean(jnp.square(y - tb_), axis=-1, keepdims=True), axis=0, keepdims=True)

    def body(x_ref, t_ref, g_ref, l_ref, dx_ref, dg_ref):
        tv = t_ref[...]
        val, vjp = jax.vjp(lambda a, b: f(a, b, tv), x_ref[...], g_ref[...])
        dx, dg = vjp(jnp.ones((1, 1), f32))
        dx_ref[...] = dx

        @pl.when(pl.program_id(0) == 0)
        def _():
            l_ref[...] = jnp.zeros_like(l_ref)
            dg_ref[...] = jnp.zeros_like(dg_ref)

        l_ref[...] += jnp.broadcast_to(val, l_ref.shape)
        dg_ref[...] += dg

    row = pl.BlockSpec((tb, D), lambda i: (i, 0))
    par = pl.BlockSpec((1, D), lambda i: (0, 0))
    return pl.pallas_call(
        body, name=name, grid=(T // tb,), in_specs=[row, row, par],
        out_specs=[pl.BlockSpec((1, LANES), lambda i: (0, 0)), row, par],
        out_shape=[_S((1, LANES), f32), _S((T, D), f32), _S((1, D), f32)], compiler_params=_cparams(("arbitrary",)),
    )(x, tgt, g)


CONV_R = 64
HALO = 8


def _conv_shifts_down(ref, t):
    if isinstance(t, int) and t == 0:
        cur = ref[0:CONV_R, :]
        row = lax.broadcasted_iota(jnp.int32, cur.shape, 0)
        return [cur] + [jnp.where(row >= s, pltpu.roll(cur, s, axis=0), 0.0) for s in range(1, CONV_K)]
    win = ref[pl.ds(pl.multiple_of(t * CONV_R - HALO, HALO), CONV_R + HALO), :]
    return [win[HALO:]] + [pltpu.roll(win, s, axis=0)[HALO:] for s in range(1, CONV_K)]


def _conv_shifts_up(ref, t):
    if isinstance(t, int):
        cur = ref[t * CONV_R:(t + 1) * CONV_R, :]
        row = lax.broadcasted_iota(jnp.int32, cur.shape, 0)
        return [cur] + [jnp.where(row < CONV_R - s, pltpu.roll(cur, CONV_R - s, axis=0), 0.0) for s in range(1, CONV_K)]
    win = ref[pl.ds(pl.multiple_of(t * CONV_R, HALO), CONV_R + HALO), :]
    return [win[:CONV_R]] + [pltpu.roll(win, CONV_R + HALO - s, axis=0)[:CONV_R] for s in range(1, CONV_K)]


def _conv_pre(shifted, w, b):
    pre = b
    for k in range(CONV_K):
        pre = pre + w[k:k + 1, :] * shifted[CONV_K - 1 - k]
    return pre


def _conv_fwd(x, w, b, *, name):
    T, C = x.shape
    cb = _pick(C, 256)

    def body(x_ref, w_ref, b_ref, o_ref):
        xv = x_ref[...]
        row = lax.broadcasted_iota(jnp.int32, xv.shape, 0)
        shifted = [xv] + [jnp.where(row >= s, pltpu.roll(xv, s, axis=0), 0.0) for s in range(1, CONV_K)]
        o_ref[...] = jax.nn.silu(_conv_pre(shifted, w_ref[...], b_ref[...]))

    col = pl.BlockSpec((T, cb), lambda j: (0, j))
    return pl.pallas_call(
        body, name=name, grid=(C // cb,), in_specs=[col, pl.BlockSpec((CONV_K, cb), lambda j: (0, j)), pl.BlockSpec((1, cb), lambda j: (0, j))],
        out_specs=col, out_shape=_S((T, C), f32), compiler_params=_cparams(("parallel",)),
    )(x, w, b)


def _conv_bwd(x, w, b, dy, *, name, carry=None):
    T, C = x.shape
    cb = _pick(C, 128)
    carry = carry or _NO_CARRY
    ends = []
    for d in dy:
        ends.append((ends[-1] if ends else 0) + d.shape[1] // cb)
    assert ends[-1] == C // cb and all(d.shape[1] % cb == 0 for d in dy)
    npc = len(dy)
    n_tiles = T // CONV_R

    def body(*refs):
        own, c_in, c_out, c_sems = _carry_split(carry, refs, 3 + npc, 3)
        x_ref, w_ref, b_ref = own[:3]
        dy_refs, (dx_ref, dw_ref, db_ref, dpre_ref) = own[3:3 + npc], own[3 + npc:]
        _carry_start(carry, c_in, c_out, c_sems, pl.program_id(0) == 0)
        j = pl.program_id(0)
        wv, bv = w_ref[...], b_ref[...]

        def fold8(v):
            return jnp.sum(v.reshape(CONV_R // 8, 8, cb), axis=0)

        def first_pass(t, acc):
            rows = slice(0, CONV_R) if isinstance(t, int) else pl.ds(pl.multiple_of(t * CONV_R, CONV_R), CONV_R)
            shifted = _conv_shifts_down(x_ref, t)
            pre = _conv_pre(shifted, wv, bv)
            dyv = dy_refs[-1][rows, :]
            for p in range(npc - 2, -1, -1):
                dyv = jnp.where(j < ends[p], dy_refs[p][rows, :], dyv)
            sg = jax.nn.sigmoid(pre)
            dpre = dyv * sg * (1.0 + pre * (1.0 - sg))
            dpre_ref[rows, :] = dpre
            return tuple(acc[k] + fold8(dpre * shifted[CONV_K - 1 - k]) for k in range(CONV_K)) + (acc[CONV_K] + fold8(dpre),)

        zero = jnp.zeros((8, cb), f32)
        acc = lax.fori_loop(1, n_tiles, first_pass, first_pass(0, (zero,) * (CONV_K + 1)), unroll=3)
        for k in range(CONV_K):
            dw_ref[k:k + 1, :] = jnp.sum(acc[k], axis=0, keepdims=True)
        db_ref[...] = jnp.sum(acc[CONV_K], axis=0, keepdims=True)

        def dx_of(t):
            up = _conv_shifts_up(dpre_ref, t)
            dx = wv[CONV_K - 1:CONV_K, :] * up[0]
            for k in range(CONV_K - 1):
                dx = dx + wv[k:k + 1, :] * up[CONV_K - 1 - k]
            return dx

        def second_pass(t, c):
            dx_ref[pl.ds(pl.multiple_of(t * CONV_R, CONV_R), CONV_R), :] = dx_of(t)
            return c

        lax.fori_loop(0, n_tiles - 1, second_pass, 0, unroll=3)
        dx_ref[(n_tiles - 1) * CONV_R:, :] = dx_of(n_tiles - 1)
        _carry_finish(carry, c_in, c_out, c_sems, pl.program_id(0) == C // cb - 1)

    col = pl.BlockSpec((T, cb), lambda j: (0, j))
    wsp = pl.BlockSpec((CONV_K, cb), lambda j: (0, j))
    bsp = pl.BlockSpec((1, cb), lambda j: (0, j))
    starts = [0] + ends[:-1]
    dy_specs = [pl.BlockSpec((T, cb), lambda j, s=s, e=e: (0, jnp.clip(j, s, e - 1) - s)) for s, e in zip(starts, ends)]
    res = pl.pallas_call(
        body, name=name, grid=(C // cb,), in_specs=[col, wsp, bsp] + dy_specs + [_ANY] * len(carry.ins),
        out_specs=[col, wsp, bsp] + [_ANY] * len(carry.out_shapes),
        out_shape=[_S((T, C), f32), _S((CONV_K, C), f32), _S((1, C), f32)] + list(carry.out_shapes),
        scratch_shapes=[pltpu.VMEM((T, cb), f32)] + list(carry.sems),
        compiler_params=_cparams(("arbitrary",) if carry.ins else ("parallel",)),
    )(x, w, b, *dy, *carry.ins)
    return (*res[:3], list(res[3:]))


def _f_s5_prep(lr, li, ldt, lrb, lib, ldtb, brt, bit):
    def disc(lr_, li_, ldt_):
        dt = jnp.exp(ldt_)
        mag = jnp.exp(lr_ * dt)
        ar, ai = mag * jnp.cos(li_ * dt), mag * jnp.sin(li_ * dt)
        den = lr_ * lr_ + li_ * li_
        cr = ((ar - 1.0) * lr_ + ai * li_) / den
        ci = (ai * lr_ - (ar - 1.0) * li_) / den
        return ar, ai, cr, ci

    ar, ai, _, _ = disc(lr, li, ldt)
    _, _, cr, ci = disc(lrb, lib, ldtb)
    return ar, ai, cr * brt - ci * bit, cr * bit + ci * brt


def _s5_prep(args, *, name):
    G, N = args[0].shape
    GM = args[3].shape[0]

    def body(*refs):
        res = _f_s5_prep(*[r[...] for r in refs[:8]])
        for o, r in zip(refs[8:], res):
            o[...] = r

    return pl.pallas_call(body, name=name, out_shape=[_S((G, N), f32)] * 2 + [_S((GM, N), f32)] * 2)(*args)


def _s5_prep_bwd(args, cots, rsum, *, name):
    G, N = args[0].shape
    GM = args[3].shape[0]

    def body(*refs):
        ins = [r[...] for r in refs[:8]]
        cot = tuple(r[...] for r in refs[8:12])
        rs = refs[12][...]
        _, vjp = jax.vjp(_f_s5_prep, *ins)
        g = vjp(cot)
        fold = lambda v: jnp.dot(rs, v, preferred_element_type=f32, precision=lax.Precision.HIGHEST)
        o = refs[13:]
        o[0][...] = g[0] + fold(g[3])
        o[1][...] = g[1] + fold(g[4])
        o[2][...] = g[2] + fold(jnp.broadcast_to(g[5], (GM, LANES)))[:, 0:1]
        o[3][...] = g[6]
        o[4][...] = g[7]

    return pl.pallas_call(
        body, name=name, out_shape=[_S((G, N), f32), _S((G, N), f32), _S((G, 1), f32), _S((GM, N), f32), _S((GM, N), f32)],
    )(*args, *cots, rsum)


S5_TC = 512


def _s5_local_scan(src, w_r, w_i, a_r, a_i, *, reverse, name, carry=None, powers=True):
    T, C = src.shape
    nblk, cb, sb = w_r.shape
    NS = nblk * sb
    tc = min(S5_TC, T)
    nT, nt = T // tc, tc // NSEG
    tmap = (lambda i: nT - 1 - i) if reverse else (lambda i: i)

    carry = carry or _NO_CARRY

    def body(*refs):
        own, c_in, c_out, c_sems = _carry_split(carry, refs, 5, 4)
        u_ref, wr_ref, wi_ref, ar_ref, ai_ref, sr_ref, si_ref, pr_ref, pi_ref, st_r, st_i, pw_r, pw_i = own
        _carry_start(carry, c_in, c_out, c_sems, (pl.program_id(0) == 0) & (pl.program_id(1) == 0))

        @pl.when(pl.program_id(1) == 0)
        def _():
            st_r[...] = jnp.zeros_like(st_r)
            st_i[...] = jnp.zeros_like(st_i)
            pw_r[...] = jnp.ones_like(pw_r)
            pw_i[...] = jnp.zeros_like(pw_i)

        u = u_ref[...].astype(bf16)
        sr_ref[...] = jnp.dot(u, wr_ref[...], preferred_element_type=f32)
        si_ref[...] = jnp.dot(u, wi_ref[...], preferred_element_type=f32)
        ar = jnp.broadcast_to(ar_ref[...], (NSEG, sb))
        ai = jnp.broadcast_to(ai_ref[...], (NSEG, sb))

        def step(k, c):
            cr, ci, qr, qi = c
            kk = (nt - 1 - k) if reverse else k
            rows = pl.ds(pl.multiple_of(kk * NSEG, NSEG), NSEG)
            nr = ar * cr - ai * ci + sr_ref[rows, :]
            ni = ar * ci + ai * cr + si_ref[rows, :]
            sr_ref[rows, :] = nr
            si_ref[rows, :] = ni
            return (nr, ni, ar * qr - ai * qi, ar * qi + ai * qr) if powers else (nr, ni, qr, qi)

        cr, ci, qr, qi = lax.fori_loop(0, nt, step, (st_r[...], st_i[...], pw_r[...], pw_i[...]), unroll=8)
        st_r[...], st_i[...], pw_r[...], pw_i[...] = cr, ci, qr, qi
        pr_ref[...] = qr
        pi_ref[...] = qi
        _carry_finish(carry, c_in, c_out, c_sems, (pl.program_id(0) == nblk - 1) & (pl.program_id(1) == nT - 1))

    blk = pl.BlockSpec((tc, sb), lambda j, i: (tmap(i), j))
    wsp = pl.BlockSpec((None, cb, sb), lambda j, i: (j, 0, 0))
    asp = pl.BlockSpec((1, sb), lambda j, i: (0, j))
    psp = pl.BlockSpec((NSEG, sb), lambda j, i: (0, j))
    res = pl.pallas_call(
        body, name=name, grid=(nblk, nT),
        in_specs=[pl.BlockSpec((tc, cb), lambda j, i: (tmap(i), j)), wsp, wsp, asp, asp] + [_ANY] * len(carry.ins),
        out_specs=[blk, blk, psp, psp] + [_ANY] * len(carry.out_shapes),
        out_shape=[_S((T, NS), f32)] * 2 + [_S((NSEG, NS), f32)] * 2 + list(carry.out_shapes),
        scratch_shapes=[pltpu.VMEM((NSEG, sb), f32)] * 4 + list(carry.sems),
        compiler_params=_cparams(("arbitrary", "arbitrary") if carry.ins else ("parallel", "arbitrary")),
    )(src, w_r, w_i, a_r, a_i, *carry.ins)
    return (*res[:4], list(res[4:])) if carry.ins else res


def _s5_carry(e_r, e_i, p_r, p_i, *, reverse, name):
    NS = e_r.shape[1]

    def body(er_ref, ei_ref, pr_ref, pi_ref, cr_ref, ci_ref):
        ar, ai = pr_ref[0:1, :], pi_ref[0:1, :]
        cr = jnp.zeros((1, NS), f32)
        ci = jnp.zeros((1, NS), f32)
        order = list(range(NSEG - 1, -1, -1)) if reverse else list(range(NSEG))
        cr_ref[order[0]:order[0] + 1, :] = cr
        ci_ref[order[0]:order[0] + 1, :] = ci
        for prev, q in zip(order[:-1], order[1:]):
            er, ei = er_ref[prev:prev + 1, :], ei_ref[prev:prev + 1, :]
            cr, ci = er + ar * cr - ai * ci, ei + ar * ci + ai * cr
            cr_ref[q:q + 1, :] = cr
            ci_ref[q:q + 1, :] = ci

    return pl.pallas_call(body, name=name, out_shape=[_S((NSEG, NS), f32)] * 2)(e_r, e_i, p_r, p_i)


def _s5_fix_out(sl_r, sl_i, a_r, a_i, c_r, c_i, wc_r, wc_i, *, name):
    T, NS = sl_r.shape
    nblk, sb, cb = wc_r.shape
    tc = min(S5_TC, T)
    nT, nt = T // tc, tc // NSEG

    def body(lr_ref, li_ref, ar_ref, ai_ref, cr_ref, ci_ref, wr_ref, wi_ref, sr_ref, si_ref, y_ref, pw_r, pw_i):
        @pl.when(pl.program_id(1) == 0)
        def _():
            pw_r[...] = jnp.ones_like(pw_r)
            pw_i[...] = jnp.zeros_like(pw_i)

        ar = jnp.broadcast_to(ar_ref[...], (NSEG, sb))
        ai = jnp.broadcast_to(ai_ref[...], (NSEG, sb))
        cr, ci = cr_ref[...], ci_ref[...]

        def step(k, c):
            qr, qi = c
            qr, qi = ar * qr - ai * qi, ar * qi + ai * qr
            rows = pl.ds(pl.multiple_of(k * NSEG, NSEG), NSEG)
            sr_ref[rows, :] = lr_ref[rows, :] + qr * cr - qi * ci
            si_ref[rows, :] = li_ref[rows, :] + qr * ci + qi * cr
            return qr, qi

        qr, qi = lax.fori_loop(0, nt, step, (pw_r[...], pw_i[...]), unroll=8)
        pw_r[...], pw_i[...] = qr, qi
        y_ref[...] = (jnp.dot(sr_ref[...].astype(bf16), wr_ref[...], preferred_element_type=f32)
                      - jnp.dot(si_ref[...].astype(bf16), wi_ref[...], preferred_element_type=f32))

    blk = pl.BlockSpec((tc, sb), lambda j, i: (i, j))
    asp = pl.BlockSpec((1, sb), lambda j, i: (0, j))
    csp = pl.BlockSpec((NSEG, sb), lambda j, i: (0, j))
    wsp = pl.BlockSpec((None, sb, cb), lambda j, i: (j, 0, 0))
    return pl.pallas_call(
        body, name=name, grid=(nblk, nT), in_specs=[blk, blk, asp, asp, csp, csp, wsp, wsp],
        out_specs=[blk, blk, pl.BlockSpec((tc, cb), lambda j, i: (i, j))],
        out_shape=[_S((T, NS), f32)] * 2 + [_S((T, nblk * cb), f32)],
        scratch_shapes=[pltpu.VMEM((NSEG, sb), f32)] * 2, compiler_params=_cparams(("parallel", "arbitrary")),
    )(sl_r, sl_i, a_r, a_i, c_r, c_i, wc_r, wc_i)


def _s5_fix_bwd(ql_r, ql_i, ab_r, ab_i, c_r, c_i, s_r, s_i, sb_r, sb_i, u, dy, du_add, w_r, w_i, *, name):
    T, NS = ql_r.shape
    nblk, cb, sb = w_r.shape
    tc = min(S5_TC, T)
    nT, nt = T // tc, tc // NSEG
    tmap = lambda i: nT - 1 - i

    def body(lr_ref, li_ref, ar_ref, ai_ref, cr_ref, ci_ref, sr_ref, si_ref, br_ref, bi_ref, u_ref, dy_ref, dua_ref, wr_ref, wi_ref,
             du_ref, dwr_ref, dwi_ref, dcr_ref, dci_ref, dar_ref, dai_ref, pw_r, pw_i, ac_r, ac_i, q_r, q_i):
        first = pl.program_id(1) == 0

        @pl.when(first)
        def _():
            pw_r[...] = jnp.ones_like(pw_r)
            pw_i[...] = jnp.zeros_like(pw_i)
            ac_r[...] = jnp.zeros_like(ac_r)
            ac_i[...] = jnp.zeros_like(ac_i)
            dwr_ref[...] = jnp.zeros_like(dwr_ref)
            dwi_ref[...] = jnp.zeros_like(dwi_ref)
            dcr_ref[...] = jnp.zeros_like(dcr_ref)
            dci_ref[...] = jnp.zeros_like(dci_ref)

        ar = jnp.broadcast_to(ar_ref[...], (NSEG, sb))
        ai = jnp.broadcast_to(ai_ref[...], (NSEG, sb))
        cr, ci = cr_ref[...], ci_ref[...]

        def fix(rows, qr, qi, spr, spi, accr, acci):
            qr, qi = ar * qr - ai * qi, ar * qi + ai * qr
            xr = lr_ref[rows, :] + qr * cr - qi * ci
            xi = li_ref[rows, :] + qr * ci + qi * cr
            q_r[rows, :] = xr
            q_i[rows, :] = xi
            return qr, qi, accr + xr * spr + xi * spi, acci + xi * spr - xr * spi

        def step(k, c):
            qr, qi, accr, acci = c
            kk = nt - 1 - k
            rows = pl.ds(pl.multiple_of(kk * NSEG, NSEG), NSEG)
            prev = pl.ds(pl.multiple_of((kk - 1) * NSEG, NSEG), NSEG)
            return fix(rows, qr, qi, sr_ref[prev, :], si_ref[prev, :], accr, acci)

        c = lax.fori_loop(0, nt - 1, step, (pw_r[...], pw_i[...], ac_r[...], ac_i[...]), unroll=7)
        qr, qi, accr, acci = fix(pl.ds(0, NSEG), *c[:2], br_ref[...], bi_ref[...], *c[2:])
        pw_r[...], pw_i[...], ac_r[...], ac_i[...] = qr, qi, accr, acci

        qrb, qib = q_r[...].astype(bf16), q_i[...].astype(bf16)
        nt_dims = (((1,), (1,)), ((), ()))
        tn_dims = (((0,), (0,)), ((), ()))
        du_ref[...] = (dua_ref[...] + lax.dot_general(qrb, wr_ref[...], nt_dims, preferred_element_type=f32)
                       + lax.dot_general(qib, wi_ref[...], nt_dims, preferred_element_type=f32))
        ub = u_ref[...].astype(bf16)
        dwr_ref[...] += lax.dot_general(ub, qrb, tn_dims, preferred_element_type=f32)
        dwi_ref[...] += lax.dot_general(ub, qib, tn_dims, preferred_element_type=f32)
        dyb = dy_ref[...].astype(bf16)
        dcr_ref[...] += lax.dot_general(sr_ref[...].astype(bf16), dyb, tn_dims, preferred_element_type=f32)
        dci_ref[...] -= lax.dot_general(si_ref[...].astype(bf16), dyb, tn_dims, preferred_element_type=f32)

        @pl.when(pl.program_id(1) == nT - 1)
        def _():
            dar_ref[...] = jnp.sum(accr, axis=0, keepdims=True)
            dai_ref[...] = jnp.sum(acci, axis=0, keepdims=True)

    blk = pl.BlockSpec((tc, sb), lambda j, i: (tmap(i), j))
    asp = pl.BlockSpec((1, sb), lambda j, i: (0, j))
    csp = pl.BlockSpec((NSEG, sb), lambda j, i: (0, j))
    bsp = pl.BlockSpec((None, NSEG, sb), lambda j, i: (tmap(i), 0, j))
    chn = pl.BlockSpec((tc, cb), lambda j, i: (tmap(i), j))
    wsp = pl.BlockSpec((None, cb, sb), lambda j, i: (j, 0, 0))
    wcs = pl.BlockSpec((None, sb, cb), lambda j, i: (j, 0, 0))
    return pl.pallas_call(
        body, name=name, grid=(nblk, nT), in_specs=[blk, blk, asp, asp, csp, csp, blk, blk, bsp, bsp, chn, chn, chn, wsp, wsp],
        out_specs=[chn, wsp, wsp, wcs, wcs, asp, asp],
        out_shape=[_S((T, nblk * cb), f32), _S((nblk, cb, sb), f32), _S((nblk, cb, sb), f32), _S((nblk, sb, cb), f32),
                   _S((nblk, sb, cb), f32), _S((1, NS), f32), _S((1, NS), f32)],
        scratch_shapes=[pltpu.VMEM((NSEG, sb), f32)] * 4 + [pltpu.VMEM((tc, sb), f32)] * 2,
        compiler_params=_cparams(("parallel", "arbitrary")),
    )(ql_r, ql_i, ab_r, ab_i, c_r, c_i, s_r, s_i, sb_r, sb_i, u, dy, du_add, w_r, w_i)


SSD2_TB = 512
_NN = (((1,), (0,)), ((), ()))
_NT = (((1,), (1,)), ((), ()))
_TN = (((0,), (0,)), ((), ()))


def _dotf(a, b, dims):
    return lax.dot_general(a.astype(bf16), b.astype(bf16), dims, preferred_element_type=f32)


def _doth(a, b, dims=_NN, sel="b", parts=3):
    x, m = (a, b) if sel == "b" else (b, a)
    m = m.astype(bf16)
    out = None
    for _ in range(parts):
        piece = x.astype(bf16)
        x = x - piece.astype(f32)
        d = lax.dot_general(*((piece, m) if sel == "b" else (m, piece)), dims, preferred_element_type=f32)
        out = d if out is None else out + d
    return out


def _ssd_consts(hpg):
    W = hpg * CHUNK
    i = lax.broadcasted_iota(jnp.int32, (CHUNK, CHUNK), 0)
    j = lax.broadcasted_iota(jnp.int32, (CHUNK, CHUNK), 1)
    tril = (i >= j).astype(f32)
    r = lax.broadcasted_iota(jnp.int32, (W, W), 0)
    c = lax.broadcasted_iota(jnp.int32, (W, W), 1)
    bd = (r // CHUNK == c // CHUNK).astype(f32)
    triu_bd = bd * (r <= c).astype(f32)
    e_r = lax.broadcasted_iota(jnp.int32, (W, LANES), 0)
    e_c = lax.broadcasted_iota(jnp.int32, (W, LANES), 1)
    ered = (e_r // HEADDIM == e_c).astype(f32)
    return tril, jnp.tile(tril, (1, hpg)), bd, triu_bd, ered


def _ssd2_specs(G, hpg, tb, tmap, b_off, c_off):
    W = hpg * HEADDIM
    ncb = tb // CHUNK
    xsp = pl.BlockSpec((tb, W), lambda g, i: (tmap(i), g))
    bsp = pl.BlockSpec((tb, SSD_STATE), lambda g, i: (tmap(i), b_off + g))
    csp = pl.BlockSpec((tb, SSD_STATE), lambda g, i: (tmap(i), c_off + g))
    rsp = pl.BlockSpec((None, ncb, W), lambda g, i: (g, tmap(i), 0))
    dsp = pl.BlockSpec((1, W), lambda g, i: (0, g))
    hsp = pl.BlockSpec((None, ncb, SSD_STATE, W), lambda g, i: (g, tmap(i), 0, 0))
    const = lambda a: pl.BlockSpec(a.shape, lambda g, i: (0, 0))
    return xsp, bsp, csp, rsp, dsp, hsp, const


def _tile_rows(a, n):
    return jnp.concatenate([a] * n, axis=0)


def _ssd2_fwd(xc, dt4, a4, dtw, aw, d4, consts, *, d_inner, name, carry=None):
    carry = carry or _NO_CARRY
    T = xc.shape[0]
    G, nc, W = dtw.shape
    hpg = W // CHUNK
    tb = min(SSD2_TB, T)
    nb, ncb = T // tb, tb // CHUNK
    b_off = d_inner // SSD_STATE
    xsp, bsp, csp, rsp, dsp, hsp, const = _ssd2_specs(G, hpg, tb, lambda i: i, b_off, b_off + G)
    tril, mask4, bd, triu_bd, _ = consts

    def body(*refs):
        own, c_in, c_out, c_sems = _carry_split(carry, refs, 12, 2)
        x_ref, b_ref, c_ref, dt_ref, a_ref, dtw_ref, aw_ref, d_ref, tril_ref, mask_ref, bd_ref, tbd_ref, y_ref, hs_ref, h_scr = own
        _carry_start(carry, c_in, c_out, c_sems, (pl.program_id(0) == 0) & (pl.program_id(1) == 0))

        @pl.when(pl.program_id(1) == 0)
        def _():
            h_scr[...] = jnp.zeros_like(h_scr)

        acs_rows = _doth(aw_ref[...], tbd_ref[...])
        ht = h_scr[...]
        for c in range(ncb):
            rows = slice(c * CHUNK, (c + 1) * CHUNK)
            x, bm, cm = x_ref[rows, :], b_ref[rows, :], c_ref[rows, :]
            acs = _doth(tril_ref[...], a_ref[rows, :], sel="a")
            lmat = jnp.where(mask_ref[...] > 0, jnp.exp(jnp.minimum(acs - acs_rows[c:c + 1, :], 0.0)), 0.0)
            m4 = _dotf(cm, _tile_rows(bm, hpg), _NT) * lmat * dtw_ref[c:c + 1, :]
            xbd = _tile_rows(x, hpg) * bd_ref[...]
            hs_ref[c] = ht
            y_ref[rows, :] = _dotf(m4, xbd, _NN) + _dotf(cm, ht, _NN) * jnp.exp(acs) + d_ref[...] * x
            a_last = acs[CHUNK - 1:CHUNK, :]
            xw = x * (jnp.exp(a_last - acs) * dt_ref[rows, :])
            ht = ht * jnp.exp(a_last) + _dotf(bm, xw, _TN)
        h_scr[...] = ht
        _carry_finish(carry, c_in, c_out, c_sems, (pl.program_id(0) == G - 1) & (pl.program_id(1) == nb - 1))

    res = pl.pallas_call(
        body, name=name, grid=(G, nb),
        in_specs=[xsp, bsp, csp, xsp, xsp, rsp, rsp, dsp, const(tril), const(mask4), const(bd), const(triu_bd)] + [_ANY] * len(carry.ins),
        out_specs=[xsp, hsp] + [_ANY] * len(carry.out_shapes),
        out_shape=[_S((T, G * W), f32), _S((G, nc, SSD_STATE, W), f32)] + list(carry.out_shapes),
        scratch_shapes=[pltpu.VMEM((SSD_STATE, W), f32)] + list(carry.sems),
        compiler_params=_cparams(("arbitrary", "arbitrary") if carry.ins else ("parallel", "arbitrary")),
    )(xc, xc, xc, dt4, a4, dtw, aw, d4, tril, mask4, bd, triu_bd, *carry.ins)
    return res[0], res[1], list(res[2:])


def _ssd2_bwd(xc, dt4, a4, dtw, aw, d4, consts, hs, dy, *, d_inner, name):
    T = xc.shape[0]
    G, nc, W = dtw.shape
    hpg = W // CHUNK
    tb = min(SSD2_TB, T)
    nb, ncb = T // tb, tb // CHUNK
    b_off = d_inner // SSD_STATE
    tmap = lambda i: nb - 1 - i
    xsp, bsp, csp, rsp, dsp, hsp, const = _ssd2_specs(G, hpg, tb, tmap, b_off, b_off + G)
    gsp = pl.BlockSpec((tb, SSD_STATE), lambda g, i: (tmap(i), g))
    ddsp = pl.BlockSpec((None, 1, LANES), lambda g, i: (g, 0, 0))
    tril, mask4, bd, triu_bd, ered = consts

    def body(x_ref, b_ref, c_ref, dt_ref, a_ref, dtw_ref, aw_ref, d_ref, tril_ref, mask_ref, bd_ref, tbd_ref, er_ref, hs_ref, dy_ref,
             dx_ref, db_ref, dc_ref, ddtc_ref, dac_ref, ddtw_ref, daw_ref, dd_ref, g_scr, dd_scr, rw_scr, tl_scr):
        first = pl.program_id(1) == 0

        @pl.when(first)
        def _():
            g_scr[...] = jnp.zeros_like(g_scr)
            dd_scr[...] = jnp.zeros_like(dd_scr)

        mask = mask_ref[...] > 0
        lane_in_block = lax.broadcasted_iota(jnp.int32, mask.shape, 1) & (CHUNK - 1)
        maskt = lax.broadcasted_iota(jnp.int32, mask.shape, 0) <= lane_in_block
        acs_rows = _doth(aw_ref[...], tbd_ref[...])
        dht = g_scr[...]
        dd = dd_scr[...]
        for c in range(ncb - 1, -1, -1):
            rows = slice(c * CHUNK, (c + 1) * CHUNK)
            x, bm, cm, dyc = x_ref[rows, :], b_ref[rows, :], c_ref[rows, :], dy_ref[rows, :]
            dtc, dtr = dt_ref[rows, :], dtw_ref[c:c + 1, :]
            ht = hs_ref[c]
            acs = _doth(tril_ref[...], a_ref[rows, :], sel="a")
            seg = acs - acs_rows[c:c + 1, :]
            lmat = jnp.where(mask, jnp.exp(jnp.minimum(seg, 0.0)), 0.0)
            lmat_t = jnp.where(maskt, jnp.exp(jnp.minimum(-seg, 0.0)), 0.0)
            btile, ctile = _tile_rows(bm, hpg), _tile_rows(cm, hpg)
            g4 = _dotf(cm, btile, _NT)
            gt4 = _dotf(bm, ctile, _NT)
            m4 = g4 * lmat * dtr
            mt4 = gt4 * lmat_t * dtc
            xbd = _tile_rows(x, hpg) * bd_ref[...]
            dybd = _tile_rows(dyc, hpg) * bd_ref[...]
            dm4 = _dotf(dyc, xbd, _NT)
            dmt4 = _dotf(x, dybd, _NT)
            dx = d_ref[...] * dyc + _dotf(mt4, dybd, _NN)
            dd = dd + jnp.sum(dyc * x, axis=0, keepdims=True)
            e4 = dm4 * m4
            dc = _dotf(dm4 * lmat * dtr, btile, _NN)
            db = _dotf(dmt4 * lmat_t * dtc, ctile, _NN)
            decay = jnp.exp(acs)
            yoff = _dotf(cm, ht, _NN) * decay
            dz = dyc * decay
            dc = dc + _dotf(dz, ht, _NT)
            dht_prev = _dotf(cm, dz, _TN)
            a_last = acs[CHUNK - 1:CHUNK, :]
            ea_last = jnp.exp(a_last)
            erel = jnp.exp(a_last - acs)
            dte = erel * dtc
            dxw = _dotf(bm, dht, _NN)
            db = db + _dotf(x * dte, dht, _NT)
            dx = dx + dxw * dte
            q4 = dxw * x
            dacs = e4 + dyc * yoff - q4 * dte
            col = jnp.concatenate([q4 * erel, _doth(tril_ref[...], dacs, _TN, sel="a")], axis=0)
            col = _doth(col, er_ref[...], parts=2)
            ddtc_ref[rows, :] = col[:CHUNK]
            dac_ref[rows, :] = col[CHUNK:]
            ddtw_ref[c:c + 1, :] = jnp.sum(dm4 * g4 * lmat, axis=0, keepdims=True)
            rw_scr[c:c + 1, :] = -jnp.sum(e4, axis=0, keepdims=True)
            tl_scr[c:c + 1, :] = jnp.sum(q4 * dte, axis=0, keepdims=True) + ea_last * jnp.sum(dht * ht, axis=0, keepdims=True)
            dx_ref[rows, :] = dx
            db_ref[rows, :] = db
            dc_ref[rows, :] = dc
            dht = dht_prev + dht * ea_last
        daw_ref[...] = _doth(rw_scr[...], tbd_ref[...], _NT) + _doth(tl_scr[...], bd_ref[...])
        g_scr[...] = dht
        dd_scr[...] = dd

        @pl.when(pl.program_id(1) == nb - 1)
        def _():
            dd_ref[...] = _doth(dd, er_ref[...])

    return pl.pallas_call(
        body, name=name, grid=(G, nb),
        in_specs=[xsp, bsp, csp, xsp, xsp, rsp, rsp, dsp, const(tril), const(mask4), const(bd), const(triu_bd), const(ered), hsp, xsp],
        out_specs=[xsp, gsp, gsp, gsp, gsp, rsp, rsp, ddsp],
        out_shape=[_S((T, G * W), f32), _S((T, G * SSD_STATE), f32), _S((T, G * SSD_STATE), f32), _S((T, G * LANES), f32),
                   _S((T, G * LANES), f32), _S(dtw.shape, f32), _S(dtw.shape, f32), _S((G, 1, LANES), f32)],
        scratch_shapes=[pltpu.VMEM((SSD_STATE, W), f32), pltpu.VMEM((1, W), f32), pltpu.VMEM((ncb, W), f32), pltpu.VMEM((ncb, W), f32)],
        compiler_params=_cparams(("parallel", "arbitrary")),
    )(xc, xc, xc, dt4, a4, dtw, aw, d4, tril, mask4, bd, triu_bd, ered, hs, dy)


def _peers():
    x, y, c = lax.axis_index("x"), lax.axis_index("y"), lax.axis_index("c")
    return x, y, c


_ANY = pl.BlockSpec(memory_space=pl.ANY)
N_CHIP = N_DEV // 2


def _all_gather(shards, *, name):
    n = len(shards)
    carry = _carry_gather(shards)

    def body(*refs):
        x_refs, out_refs, sems = refs[:n], refs[n:2 * n], refs[2 * n:]
        _gather_start(x_refs, out_refs, sems)
        _gather_finish(x_refs, out_refs, sems)

    return pl.pallas_call(
        body, name=name, out_shape=list(carry.out_shapes), in_specs=[_ANY] * n, out_specs=[_ANY] * n, scratch_shapes=list(carry.sems),
    )(*shards)


def _gather_parts(x_refs, out_refs, sems):
    send_sems, recv_sems, local_sems = sems
    x, y, c = _peers()
    me, sibling = (x, y, c), (x, y, 1 - c)
    chips = [(1 - x, y), (x, 1 - y), (1 - x, 1 - y)]
    n = len(x_refs)

    def copy(a, r, block, to, src=None):
        px, py, pc = block
        slot = out_refs[a].at[4 * px + 2 * py + pc]
        return pltpu.make_async_remote_copy(
            src_ref=slot if src is None else src, dst_ref=slot, send_sem=send_sems.at[7 * a + r],
            recv_sem=recv_sems.at[7 * a + r], device_id=to, device_id_type=MESH)

    mine = [pltpu.make_async_copy(x_refs[a], out_refs[a].at[4 * x + 2 * y + c], local_sems.at[a]) for a in range(n)]
    first = []
    for a in range(n):
        first.append(copy(a, 0, me, sibling, src=x_refs[a]))
        first += [copy(a, 1 + j, me, (*chip, c), src=x_refs[a]) for j, chip in enumerate(chips)]
    return copy, mine, first, me, sibling, chips, c, n


def _gather_start(x_refs, out_refs, sems):
    _, mine, first, *_ = _gather_parts(x_refs, out_refs, sems)
    for cp in mine + first:
        cp.start()


def _gather_finish(x_refs, out_refs, sems):
    copy, mine, first, me, sibling, chips, c, n = _gather_parts(x_refs, out_refs, sems)
    passed = []
    for j, chip in enumerate(chips):
        for a in range(n):
            copy(a, 1 + j, (*chip, c), me).wait_recv()
            fwd = copy(a, 4 + j, (*chip, c), sibling)
            fwd.start()
            passed.append(fwd)
    for a in range(n):
        copy(a, 0, sibling, me).wait_recv()
    for j, chip in enumerate(chips):
        for a in range(n):
            copy(a, 4 + j, (*chip, 1 - c), me).wait_recv()
    for cp in first + passed:
        cp.wait_send()
    for cp in mine:
        cp.wait()


def _exchange_sibling(slots, *, name):
    n = len(slots)

    def body(*refs):
        x_refs, sib_refs = refs[:n], refs[n:2 * n]
        send_sems, recv_sems = refs[2 * n:]
        x, y, c = _peers()
        give = [pltpu.make_async_remote_copy(
            src_ref=x_refs[a].at[pl.ds(N_CHIP * (1 - c), N_CHIP)], dst_ref=sib_refs[a], send_sem=send_sems.at[a],
            recv_sem=recv_sems.at[a], device_id=(x, y, 1 - c), device_id_type=MESH) for a in range(n)]
        for cp in give:
            cp.start()
        for cp in give:
            cp.wait_recv()
        for cp in give:
            cp.wait_send()

    return list(pl.pallas_call(
        body, name=name, out_shape=[_S((N_CHIP,) + s.shape[1:], s.dtype) for s in slots], in_specs=[_ANY] * n, out_specs=[_ANY] * n,
        scratch_shapes=[pltpu.SemaphoreType.DMA((n,)), pltpu.SemaphoreType.DMA((n,))],
    )(*slots))


def _chip_sum(slots, sib, core, *, name):
    _, R, W = slots.shape
    tr = _pick(R, max(16, EW_BLOCK_ELEMS // W), align=16)

    def body(core_ref, x_ref, s_ref, o_ref):
        o_ref[...] = (x_ref[...].astype(f32) + s_ref[...].astype(f32)).astype(o_ref.dtype)

    blk = pl.BlockSpec((None, tr, W), lambda t, i, core_ref: (t, i, 0))
    return pl.pallas_call(
        body, name=name, out_shape=_S(sib.shape, slots.dtype),
        grid_spec=pltpu.PrefetchScalarGridSpec(
            num_scalar_prefetch=1, grid=(N_CHIP, R // tr),
            in_specs=[pl.BlockSpec((None, tr, W), lambda t, i, core_ref: (N_CHIP * core_ref[0] + t, i, 0)), blk], out_specs=blk),
        compiler_params=_cparams(("parallel", "parallel")),
    )(core, slots, sib)


def _chip_out_shapes(parts):
    return [_S((N_CHIP - 1,) + p.shape[1:], p.dtype) for p in parts]


def _chip_sems(n):
    return [pltpu.SemaphoreType.DMA((3 * n,)), pltpu.SemaphoreType.DMA((3 * n,))]


def _chip_copies(p_refs, out_refs, send_sems, recv_sems):
    x, y, c = _peers()
    copies = []
    for j in range(1, N_CHIP):
        tx, ty = x ^ (j >> 1), y ^ (j & 1)
        for a in range(len(p_refs)):
            copies.append(pltpu.make_async_remote_copy(
                src_ref=p_refs[a].at[2 * tx + ty], dst_ref=out_refs[a].at[j - 1], send_sem=send_sems.at[3 * a + j - 1],
                recv_sem=recv_sems.at[3 * a + j - 1], device_id=(tx, ty, c), device_id_type=MESH))
    return copies


def _start_all(copies):
    for cp in copies:
        cp.start()


def _wait_all(copies):
    for cp in copies:
        cp.wait_recv()
    for cp in copies:
        cp.wait_send()


class _Carry(NamedTuple):
    ins: tuple = ()
    out_shapes: tuple = ()
    sems: tuple = ()
    start: Callable = None
    finish: Callable = None


_NO_CARRY = _Carry()


def _carry_chips(parts):
    return _Carry(tuple(parts), tuple(_chip_out_shapes(parts)), tuple(_chip_sems(len(parts))),
                  lambda i, o, s: _start_all(_chip_copies(i, o, *s)), lambda i, o, s: _wait_all(_chip_copies(i, o, *s)))


def _carry_gather(shards):
    n = len(shards)
    sems = (pltpu.SemaphoreType.DMA((7 * n,)), pltpu.SemaphoreType.DMA((7 * n,)), pltpu.SemaphoreType.DMA((n,)))
    return _Carry(tuple(shards), tuple(_S((N_DEV,) + s.shape, s.dtype) for s in shards), sems, _gather_start, _gather_finish)


def _carry_split(carry, refs, n_in, n_out):
    ci, co, cs = len(carry.ins), len(carry.out_shapes), len(carry.sems)
    refs = list(refs)
    own_in, c_in = refs[:n_in], refs[n_in:n_in + ci]
    own_out, c_out = refs[n_in + ci:n_in + ci + n_out], refs[n_in + ci + n_out:n_in + ci + n_out + co]
    rest = refs[n_in + ci + n_out + co:]
    own_scratch, c_sems = rest[:len(rest) - cs], rest[len(rest) - cs:]
    return own_in + own_out + own_scratch, c_in, c_out, c_sems


def _carry_start(carry, c_in, c_out, c_sems, first):
    if carry.ins:
        @pl.when(first)
        def _():
            carry.start(c_in, c_out, c_sems)


def _carry_finish(carry, c_in, c_out, c_sems, last):
    if carry.ins:
        @pl.when(last)
        def _():
            carry.finish(c_in, c_out, c_sems)


def _sum_slots(stack, *, name):
    n, R, W = stack.shape
    tr = _pick(R, 1024, align=8)

    def body(s_ref, o_ref):
        acc = s_ref[0]
        for k in range(1, n):
            acc = acc + s_ref[k]
        o_ref[...] = acc

    return pl.pallas_call(
        body, name=name, grid=(R // tr,), in_specs=[pl.BlockSpec((n, tr, W), lambda i: (0, i, 0))],
        out_specs=pl.BlockSpec((tr, W), lambda i: (i, 0)), out_shape=_S((R, W), f32), compiler_params=_cparams(("parallel",)),
    )(stack)


def _adamw_math(gv, wv, mv, vv):
    c1 = 1.0 / (1.0 - ADAM_B1 ** ADAM_STEP)
    c2 = 1.0 / (1.0 - ADAM_B2 ** ADAM_STEP)
    nm = ADAM_B1 * mv + (1.0 - ADAM_B1) * gv
    nv = ADAM_B2 * vv + (1.0 - ADAM_B2) * jnp.square(gv)
    return -ADAM_LR * ((nm * c1) / (jnp.sqrt(nv * c2) + ADAM_EPS) + ADAM_WD * wv), nm, nv


def _adamw(g, w, m, v, *, name):
    R, W = w.shape
    tr = _pick(R, max(8, (1 << 20) // (4 * W)), align=8)

    def body(g_ref, w_ref, m_ref, v_ref, d_ref, nm_ref, nv_ref):
        d_ref[...], nm_ref[...], nv_ref[...] = _adamw_math(g_ref[...], w_ref[...], m_ref[...], v_ref[...])

    sp = pl.BlockSpec((tr, W), lambda i: (i, 0))
    return pl.pallas_call(
        body, name=name, grid=(R // tr,), in_specs=[sp] * 4, out_specs=[sp] * 3, out_shape=[_S((R, W), f32)] * 3,
        compiler_params=_cparams(("parallel",)),
    )(g, w, m, v)


def _reduce_adamw(own, arrived, chip, w, m, v, *, name):
    n, R, W = arrived.shape
    tr = _pick(R, max(16, EW_BLOCK_ELEMS // (2 * W)), align=16)

    def body(chip_ref, o_ref, p_ref, w_ref, m_ref, v_ref, g_ref, d_ref, nm_ref, nv_ref):
        gv = o_ref[...].astype(f32)
        for k in range(n):
            gv = gv + p_ref[k].astype(f32)
        g_ref[...] = gv
        d_ref[...], nm_ref[...], nv_ref[...] = _adamw_math(gv, w_ref[...], m_ref[...], v_ref[...])

    sp = pl.BlockSpec((tr, W), lambda i, chip_ref: (i, 0))
    return pl.pallas_call(
        body, name=name, out_shape=[_S((R, W), f32)] * 4,
        grid_spec=pltpu.PrefetchScalarGridSpec(
            num_scalar_prefetch=1, grid=(R // tr,),
            in_specs=[pl.BlockSpec((None, tr, W), lambda i, chip_ref: (chip_ref[0], i, 0)),
                      pl.BlockSpec((n, tr, W), lambda i, chip_ref: (0, i, 0))] + [sp] * 3, out_specs=[sp] * 4),
        compiler_params=_cparams(("parallel",)),
    )(chip, own, arrived, w, m, v)


def _pieces(seg_start, seg_len, shard_w):
    out, col = [], seg_start
    while col < seg_start + seg_len:
        k, a = divmod(col, shard_w)
        n = min(shard_w - a, seg_start + seg_len - col)
        out.append((k, a, col - seg_start, n))
        col += n
    return out


def _unshard_w_in(g, seg_lens, *, name):
    _, D, w = g.shape
    starts = [sum(seg_lens[:i]) for i in range(len(seg_lens))]
    widths = [max(n, LANES) for n in seg_lens]
    tm = _pick(D, 256, align=16)

    def body(g_ref, *o_refs):
        for o_ref, s0, n in zip(o_refs, starts, seg_lens):
            if n < o_ref.shape[1]:
                o_ref[...] = jnp.zeros_like(o_ref)
            for k, a, off, m in _pieces(s0, n, w):
                o_ref[:, off:off + m] = g_ref[k, :, a:a + m]

    return pl.pallas_call(
        body, name=name, grid=(D // tm,), in_specs=[pl.BlockSpec((N_DEV, tm, w), lambda i: (0, i, 0))],
        out_specs=[pl.BlockSpec((tm, wd), lambda i: (i, 0)) for wd in widths], out_shape=[_S((D, wd), g.dtype) for wd in widths],
        compiler_params=_cparams(("parallel",)),
    )(g)


def _unshard_pair(g1, g2, *, name):
    _, D, w = g1.shape
    tm = _pick(D, 256, align=16)

    def body(a_ref, b_ref, o_ref):
        for i, g_ref in enumerate((a_ref, b_ref)):
            for k in range(N_DEV):
                off = (i * N_DEV + k) * w
                o_ref[:, off:off + w] = g_ref[k]

    blk = pl.BlockSpec((N_DEV, tm, w), lambda i: (0, i, 0))
    return pl.pallas_call(
        body, name=name, grid=(D // tm,), in_specs=[blk, blk], out_specs=pl.BlockSpec((tm, 2 * N_DEV * w), lambda i: (i, 0)),
        out_shape=_S((D, 2 * N_DEV * w), g1.dtype), compiler_params=_cparams(("parallel",)),
    )(g1, g2)


def _reshard_pair(dw, *, name):
    D, w = dw.shape[0], dw.shape[1] // (2 * N_DEV)
    tm = _pick(D, 128, align=16)

    def body(g_ref, a_ref, b_ref):
        for i, o_ref in enumerate((a_ref, b_ref)):
            for k in range(N_DEV):
                off = (i * N_DEV + k) * w
                o_ref[_slot_of(k)] = g_ref[:, off:off + w].astype(o_ref.dtype)

    blk = pl.BlockSpec((N_DEV, tm, w), lambda i: (0, i, 0))
    return pl.pallas_call(
        body, name=name, grid=(D // tm,), in_specs=[pl.BlockSpec((tm, dw.shape[1]), lambda i: (i, 0))], out_specs=[blk, blk],
        out_shape=[_S((N_DEV, D, w), bf16)] * 2, compiler_params=_cparams(("parallel",)),
    )(dw)


def _reshard_w_in(grads, seg_lens, w, *, name):
    D = grads[0].shape[0]
    starts = [sum(seg_lens[:i]) for i in range(len(seg_lens))]
    tm = _pick(D, 128, align=16)

    def body(*refs):
        o_ref = refs[-1]
        for g_ref, s0, n in zip(refs[:-1], starts, seg_lens):
            for k, a, off, m in _pieces(s0, n, w):
                o_ref[_slot_of(k), :, a:a + m] = g_ref[:, off:off + m].astype(o_ref.dtype)

    return pl.pallas_call(
        body, name=name, grid=(D // tm,), in_specs=[pl.BlockSpec((tm, g.shape[1]), lambda i: (i, 0)) for g in grads],
        out_specs=pl.BlockSpec((N_DEV, tm, w), lambda i: (0, i, 0)), out_shape=_S((N_DEV, D, w), bf16),
        compiler_params=_cparams(("parallel",)),
    )(*grads)


def _pad_flat(a, mult):
    a = a.reshape(-1)
    n = -(-a.shape[0] // mult) * mult
    return a if n == a.shape[0] else jnp.pad(a, (0, n - a.shape[0]))


def _pad_cols(a, mult):
    n = -(-a.shape[1] // mult) * mult
    return a if n == a.shape[1] else jnp.pad(a, ((0, 0), (0, n - a.shape[1])))


def _block_diag(t):
    nblk, g, P, Q = t.shape
    eye = jnp.eye(g, dtype=t.dtype)
    return (t[:, :, :, None, :] * eye[None, :, None, :, None]).reshape(nblk, g * P, g * Q)


def _block_diag_t(w, P, Q):
    nblk = w.shape[0]
    g = w.shape[1] // P
    eye = jnp.eye(g, dtype=w.dtype)
    return (w.reshape(nblk, g, P, g, Q) * eye[None, :, None, :, None]).sum(axis=3)


_COLS = ("ffn1_w_gate", "ffn1_w_up", "ffn2_w_gate", "ffn2_w_up")
_ROWS = ("ffn1_w_down", "ffn2_w_down", "s5_w_glu", "w_proj_s5", "w_out", "w_proj_ssd")
_BIG = _COLS + _ROWS + ("w_in", "conv_w")
_SMALL = ("ffn1_norm", "mix_norm", "conv_b", "s5_A_re", "s5_A_im", "s5_log_dt", "s5_B_re", "s5_B_im", "s5_C_re", "s5_C_im",
          "s5_D", "s5_b_glu", "ssd_A_log", "ssd_dt_bias", "ssd_D", "ssd_norm", "b_gate", "ffn2_norm", "final_norm")
_WEIGHTS = ("ffn1_norm", "ffn1_w_gate", "ffn1_w_up", "ffn1_w_down", "mix_norm", "w_in", "conv_w", "conv_b", "s5_A_re", "s5_A_im",
            "s5_log_dt", "s5_B_re", "s5_B_im", "s5_C_re", "s5_C_im", "s5_D", "s5_w_glu", "s5_b_glu", "ssd_A_log", "ssd_dt_bias",
            "ssd_D", "ssd_norm", "w_proj_s5", "w_proj_ssd", "b_gate", "w_out", "ffn2_norm", "ffn2_w_gate", "ffn2_w_up",
            "ffn2_w_down", "final_norm")
def _with_carry(res, carry):
    return res if carry else (res, [])


def _ffn_fwd(x, n, wgu, wd_of, tag, carries=(None, None, None)):
    D = x.shape[1]
    h = _rows(_f_rmsnorm, [x], [n], [(D, bf16)], name=tag + "_norm")[0]
    ab, got0 = _with_carry(_mm(h, wgu, carry=carries[0], name=tag + "_gate_up"), carries[0])
    wd = wd_of(got0)
    F = wd.shape[0]
    (c,), got1 = _with_carry(_rows(_f_swiglu, [ab], [], [(F, bf16)], carry=carries[1], name=tag + "_act"), carries[1])
    y, got2 = _with_carry(_mm(c, wd, scale=0.5, add=x, carry=carries[2], name=tag + "_down"), carries[2])
    return y, (x, n, h, ab, c), (got0, got1, got2)


def _ffn_bwd(saved, wgu, wd, dy, tag, carry_after_dwd=None, carry_after_dwgu=None):
    x, n, h, ab, c = saved
    F = wd.shape[0]
    def act_bwd(dc, ab_):
        return jax.vjp(lambda t: _f_swiglu(t)[0], ab_)[1](dc)[0]

    dab = _mm(dy, wd, tb=True, scale=0.5, tm=EPI_TM, epi=(act_bwd, ab, 2 * F), out_dtype=bf16, name=tag + "_d_act")
    dwd = _mm(c, dy, ta=True, o_blk="m", o_slots=True, tm=F // 2, out_dtype=bf16, scale=0.5, name=tag + "_d_wdown")
    carry_w = carry_after_dwd(dwd) if carry_after_dwd else None
    dwgu, arr_w = _with_carry(_mm(h, dab, ta=True, carry=carry_w, name=tag + "_d_wgu"), carry_w)
    dwg, dwu = _reshard_pair(dwgu, name=tag + "_reshard_d_wgu")
    carry_h = carry_after_dwgu(dwg, dwu) if carry_after_dwgu else None
    dh, arr_h = _with_carry(_mm(dab, wgu, tb=True, carry=carry_h, name=tag + "_d_h"), carry_h)
    (dx,), (dn,) = _rows_bwd(_f_rmsnorm, [x], [n], [dh], name=tag + "_norm_bwd", want_rows=[0], adds={0: dy})
    return dx, dn, dwg, dwu, dwd, arr_w, arr_h


def kernel(x, ffn1_norm, ffn1_w_gate, ffn1_w_up, ffn1_w_down, mix_norm, w_in, conv_w, conv_b, s5_A_re, s5_A_im, s5_log_dt, s5_B_re, s5_B_im, s5_C_re, s5_C_im, s5_D, s5_w_glu, s5_b_glu, ssd_A_log, ssd_dt_bias, ssd_D, ssd_norm, w_proj_s5, w_proj_ssd, b_gate, w_out, ffn2_norm, ffn2_w_gate, ffn2_w_up, ffn2_w_down, final_norm, loss_target, m_ffn1_norm, m_ffn1_w_gate, m_ffn1_w_up, m_ffn1_w_down, m_mix_norm, m_w_in, m_conv_w, m_conv_b, m_s5_A_re, m_s5_A_im, m_s5_log_dt, m_s5_B_re, m_s5_B_im, m_s5_C_re, m_s5_C_im, m_s5_D, m_s5_w_glu, m_s5_b_glu, m_ssd_A_log, m_ssd_dt_bias, m_ssd_D, m_ssd_norm, m_w_proj_s5, m_w_proj_ssd, m_b_gate, m_w_out, m_ffn2_norm, m_ffn2_w_gate, m_ffn2_w_up, m_ffn2_w_down, m_final_norm, v_ffn1_norm, v_ffn1_w_gate, v_ffn1_w_up, v_ffn1_w_down, v_mix_norm, v_w_in, v_conv_w, v_conv_b, v_s5_A_re, v_s5_A_im, v_s5_log_dt, v_s5_B_re, v_s5_B_im, v_s5_C_re, v_s5_C_im, v_s5_D, v_s5_w_glu, v_s5_b_glu, v_ssd_A_log, v_ssd_dt_bias, v_ssd_D, v_ssd_norm, v_w_proj_s5, v_w_proj_ssd, v_b_gate, v_w_out, v_ffn2_norm, v_ffn2_w_gate, v_ffn2_w_up, v_ffn2_w_down, v_final_norm):
    P = dict(locals())
    T, D = x.shape[1], x.shape[2]
    x0, tgt = x[0], loss_target[0]
    sh = {k: P[k][0] for k in _BIG}

    send = {k: (sh[k] if k == "conv_w" else sh[k].astype(bf16)) for k in _BIG}
    W = {}

    def gather_in(keys):
        return _carry_gather([send[k] for k in keys])

    first_keys = ("ffn1_w_gate", "ffn1_w_up", "conv_w")
    W.update(zip(first_keys, _all_gather([send[k] for k in first_keys], name="gather_weights_first")))
    whole = lambda k: W[k].reshape(-1, D)
    conv_w_full = W["conv_w"].transpose(1, 0, 2).reshape(CONV_K, -1)

    d_inner = N_DEV * sh["w_proj_ssd"].shape[0]
    conv_dim = conv_w_full.shape[1]
    H = ssd_A_log.shape[1]
    G = (conv_dim - d_inner) // (2 * SSD_STATE)
    hpg = H // G
    nc = T // CHUNK
    Gs = D // S5_GROUP
    nblk = Gs // S5_GPB
    NS = Gs * S5_STATE
    seg_lens = (D, d_inner, conv_dim, H, 2 * D)

    cuts = [0, D // 3 // 16 * 16, D // 3 // 16 * 16 + 3 * D // 8 // 16 * 16, D]
    win_rows = [send["w_in"][a_:b_] for a_, b_ in zip(cuts[:-1], cuts[1:])]
    wgu1 = _unshard_pair(W["ffn1_w_gate"], W["ffn1_w_up"], name="unshard_ffn1_gate_up")

    def ffn1_down(got):
        W["ffn1_w_down"] = got[0]
        return whole("ffn1_w_down")

    x1, sv1, (got0, got1, got2) = _ffn_fwd(
        x0, ffn1_norm, wgu1, ffn1_down, "ffn1",
        carries=(_carry_gather([send["ffn1_w_down"], win_rows[0]]), _carry_gather([win_rows[1]]), _carry_gather([win_rows[2]])))
    ffn1_w = (wgu1, whole("ffn1_w_down"))
    W["w_in"] = jnp.concatenate([got0[1], got1[0], got2[0]], axis=1)
    w_u, w_z, w_xbc, w_dt, w_gl = _unshard_w_in(W["w_in"], seg_lens, name="unshard_w_in")
    h2 = _rows(_f_rmsnorm, [x1], [mix_norm], [(D, bf16)], name="mix_norm")[0]
    u_p = _mm(h2, w_u, o_seg=True, name="in_u")
    z = _mm(h2, w_z, name="in_z")
    xbc = _mm(h2, w_xbc, name="in_xbc")
    gl = _mm(h2, w_gl, name="in_gate")
    dtr = _mm(h2, w_dt, name="in_dt")

    rep = lambda a: jnp.repeat(a, S5_GROUP, axis=0)
    lr, li, ldt = s5_A_re[0], s5_A_im[0], s5_log_dt[0].reshape(Gs, 1)
    brt = s5_B_re[0].transpose(0, 2, 1).reshape(Gs * S5_GROUP, S5_STATE)
    bit = s5_B_im[0].transpose(0, 2, 1).reshape(Gs * S5_GROUP, S5_STATE)
    prep_args = (lr, li, ldt, rep(lr), rep(li), rep(ldt), brt, bit)
    ar, ai, bbrt, bbit = _s5_prep(prep_args, name="s5_prep")
    a_r, a_i = ar.reshape(1, NS), ai.reshape(1, NS)
    wb_r = _block_diag(bbrt.reshape(nblk, S5_GPB, S5_GROUP, S5_STATE)).astype(bf16)
    wb_i = _block_diag(bbit.reshape(nblk, S5_GPB, S5_GROUP, S5_STATE)).astype(bf16)
    c4r = s5_C_re[0].reshape(nblk, S5_GPB, S5_GROUP, S5_STATE).transpose(0, 1, 3, 2)
    c4i = s5_C_im[0].reshape(nblk, S5_GPB, S5_GROUP, S5_STATE).transpose(0, 1, 3, 2)
    wc_r, wc_i = _block_diag(c4r).astype(bf16), _block_diag(c4i).astype(bf16)
    mix_keys = ("s5_w_glu", "w_proj_s5", "w_proj_ssd", "w_out")
    sl_r, sl_i, p_r, p_i, got = _s5_local_scan(u_p, wb_r, wb_i, a_r, a_i, reverse=False, carry=gather_in(mix_keys), name="s5_scan")
    W.update(zip(mix_keys, got))
    w_glu, w_p5, w_pssd, w_o = whole("s5_w_glu"), whole("w_proj_s5"), whole("w_proj_ssd"), whole("w_out")
    c_r, c_i = _s5_carry(sl_r[T - NSEG:], sl_i[T - NSEG:], p_r, p_i, reverse=False, name="s5_carry")
    s_r, s_i, ylin = _s5_fix_out(sl_r, sl_i, a_r, a_i, c_r, c_i, wc_r, wc_i, name="s5_fix_out")
    g5 = _rows(_f_s5_post, [ylin, u_p], [s5_D], [(D, f32)], name="s5_gelu")[0]
    v5 = _mm(g5, w_glu, name="s5_glu_mm")
    o5 = _rows(_f_glu, [g5, v5], [s5_b_glu], [(D, bf16)], name="s5_glu")[0]
    p5 = _mm(o5, w_p5, a_seg=True, name="proj_s5")

    xc = _conv_fwd(xbc, conv_w_full, conv_b, name="conv")
    bias_p, alog_p = _pad_cols(ssd_dt_bias, LANES), _pad_cols(ssd_A_log, LANES)
    expand = (lax.broadcasted_iota(jnp.int32, (LANES, d_inner), 1) // HEADDIM
              == lax.broadcasted_iota(jnp.int32, (LANES, d_inner), 0)).astype(f32)
    dt_p, da_p, dt4, a4 = _rows(_f_dt_expand, [dtr], [bias_p, alog_p, expand],
                                [(LANES, f32), (LANES, f32), (d_inner, f32), (d_inner, f32)], name="ssd_dt")
    row_l = lambda a: a[:, :H].reshape(nc, CHUNK, G, hpg).transpose(2, 0, 3, 1).reshape(G, nc, hpg * CHUNK)
    ssd_in = (xc, dt4, a4, row_l(dt_p), row_l(da_p), jnp.repeat(ssd_D, HEADDIM, axis=1), _ssd_consts(hpg))
    ffn2_keys = ("ffn2_w_gate", "ffn2_w_up", "ffn2_w_down")
    y_ssd, hs, got = _ssd2_fwd(*ssd_in, d_inner=d_inner, carry=gather_in(ffn2_keys), name="ssd")
    W.update(zip(ffn2_keys, got))
    ffn2_w = (_unshard_pair(W["ffn2_w_gate"], W["ffn2_w_up"], name="unshard_ffn2_gate_up"), whole("ffn2_w_down"))
    yn = _rows(_f_gated_norm, [y_ssd, z], [ssd_norm], [(d_inner, bf16)], name="ssd_gated_norm")[0]
    pssd = _mm(yn, w_pssd, name="proj_ssd")

    merged = _rows(_f_merge, [gl, p5, pssd], [b_gate], [(D, bf16)], name="merge")[0]
    x2 = _mm(merged, w_o, add=x1, name="out_proj")
    x3, sv2, _ = _ffn_fwd(x2, ffn2_norm, ffn2_w[0], lambda _: ffn2_w[1], "ffn2")
    lossv, dx3, d_final = _loss_stage(x3, tgt, final_norm.reshape(1, D), name="loss")

    gw = {}
    gs = {"final_norm": d_final}
    slot_mm = lambda a_, b_, name, **kw: _mm(a_, b_, ta=True, o_blk="m", o_slots=True, out_dtype=bf16, name=name, **kw)
    core = lax.axis_index("c").astype(jnp.int32).reshape(1)
    chip = (2 * lax.axis_index("x") + lax.axis_index("y")).astype(jnp.int32).reshape(1)
    chip_sums, arrived = {}, {}

    def level1(keys, tag):
        sib = _exchange_sibling([gw[k] for k in keys], name="exchange_sibling_" + tag)
        for k, s_ in zip(keys, sib):
            chip_sums[k] = _chip_sum(gw[k], s_, core, name="chip_sum_" + k)
        return [chip_sums[k] for k in keys]

    dx2, gs["ffn2_norm"], gw["ffn2_w_gate"], gw["ffn2_w_up"], gw["ffn2_w_down"], _, _ = _ffn_bwd(sv2, *ffn2_w, dx3, "ffn2")

    dmerged = _mm(dx2, w_o, tb=True, name="d_merged")
    gw["w_out"] = slot_mm(merged, dx2, "d_w_out")
    (dgl, dp5, dpssd), (gs["b_gate"],) = _rows_bwd(_f_merge, [gl, p5, pssd], [b_gate], [dmerged], name="merge_bwd", want_rows=[0, 1, 2])

    dyn = _mm(dpssd, w_pssd, tb=True, name="d_yn")
    gw["w_proj_ssd"] = slot_mm(yn, dpssd, "d_w_proj_ssd")
    group_a = ("ffn2_w_gate", "ffn2_w_up", "ffn2_w_down", "w_out", "w_proj_ssd")
    parts_a = level1(group_a, "a")
    (dyssd, dz), (gs["ssd_norm"],) = _rows_bwd(_f_gated_norm, [y_ssd, z], [ssd_norm], [dyn], name="ssd_gated_norm_bwd", want_rows=[0, 1])
    dxs, dbm, dcm, ddtc, ddac, ddtw, ddaw, ddh = _ssd2_bwd(*ssd_in, hs, dyssd, d_inner=d_inner, name="ssd_bwd")

    def fold(col, row):
        col = col.reshape(T, G, LANES)[:, :, :hpg].reshape(T, H)
        row = row.reshape(G, nc, hpg, CHUNK).transpose(1, 3, 0, 2).reshape(T, H)
        return _pad_cols(col + row, LANES)

    (ddtr,), (dbias_p, dalog_p) = _rows_bwd(_f_dt, [dtr], [bias_p, alog_p], [fold(ddtc, ddtw), fold(ddac, ddaw)], name="ssd_dt_bwd", want_rows=[0])
    gs["ssd_dt_bias"], gs["ssd_A_log"], gs["ssd_D"] = dbias_p[:, :H], dalog_p[:, :H], ddh[:, 0, :hpg].reshape(1, H)
    dxbc, d_conv_w, gs["conv_b"], arr = _conv_bwd(
        xbc, conv_w_full, conv_b, [dxs, dbm, dcm], carry=_carry_chips(parts_a), name="conv_bwd")
    arrived.update(zip(group_a, arr))
    cwk = sh["conv_w"].shape[1]
    gw["conv_w"] = d_conv_w.reshape(CONV_K, N_CHIP, 2, cwk).transpose(2, 1, 0, 3).reshape(N_DEV, CONV_K, cwk)

    do5 = _mm(dp5, w_p5, tb=True, o_seg=True, name="d_o5")
    gw["w_proj_s5"] = slot_mm(o5, dp5, "d_w_proj_s5", a_seg=True)
    (dg5a, dv5), (gs["s5_b_glu"],) = _rows_bwd(_f_glu, [g5, v5], [s5_b_glu], [do5], name="s5_glu_bwd", want_rows=[0, 1])
    dg5 = _mm(dv5, w_glu, tb=True, add=dg5a, name="d_g5")
    gw["s5_w_glu"] = slot_mm(g5, dv5, "d_w_glu")
    (dylin, du_a), (gs["s5_D"],) = _rows_bwd(_f_s5_post, [ylin, u_p], [s5_D], [dg5], name="s5_gelu_bwd", want_rows=[0, 1])
    wct_r, wct_i = wc_r.transpose(0, 2, 1), -wc_i.transpose(0, 2, 1)
    ql_r, ql_i, _, _ = _s5_local_scan(dylin, wct_r, wct_i, a_r, -a_i, reverse=True, powers=False, name="s5_scan_bwd")
    cb_r, cb_i = _s5_carry(ql_r[:NSEG], ql_i[:NSEG], p_r, -p_i, reverse=True, name="s5_carry_bwd")
    tc = min(S5_TC, T)

    def before_blocks(s):
        last = s.reshape(T // tc, tc, NS)[:, tc - NSEG:, :]
        wrap = jnp.concatenate([jnp.zeros((1, 1, NS), f32), last[-1:, : NSEG - 1, :]], axis=1)
        return jnp.concatenate([wrap, last[:-1]], axis=0)

    du_p, dwb_r, dwb_i, dwc_r, dwc_i, d_ar, d_ai = _s5_fix_bwd(
        ql_r, ql_i, a_r, -a_i, cb_r, cb_i, s_r, s_i, before_blocks(s_r), before_blocks(s_i), u_p, dylin, du_a, wb_r, wb_i, name="s5_fix_bwd")
    unblk = lambda w: _block_diag_t(w, S5_GROUP, S5_STATE).reshape(Gs * S5_GROUP, S5_STATE)
    rsum = jnp.repeat(jnp.eye(Gs, dtype=f32), S5_GROUP, axis=1)
    d_lr, d_li, d_ldt, d_brt, d_bit = _s5_prep_bwd(
        prep_args, (d_ar.reshape(Gs, S5_STATE), d_ai.reshape(Gs, S5_STATE), unblk(dwb_r), unblk(dwb_i)), rsum, name="s5_prep_bwd")
    gs["s5_A_re"], gs["s5_A_im"], gs["s5_log_dt"] = d_lr, d_li, d_ldt.reshape(1, Gs)
    gs["s5_B_re"] = d_brt.reshape(Gs, S5_GROUP, S5_STATE).transpose(0, 2, 1)
    gs["s5_B_im"] = d_bit.reshape(Gs, S5_GROUP, S5_STATE).transpose(0, 2, 1)
    gs["s5_C_re"] = _block_diag_t(dwc_r, S5_STATE, S5_GROUP).transpose(0, 1, 3, 2).reshape(Gs, S5_GROUP, S5_STATE)
    gs["s5_C_im"] = _block_diag_t(dwc_i, S5_STATE, S5_GROUP).transpose(0, 1, 3, 2).reshape(Gs, S5_GROUP, S5_STATE)

    d_w_in = [_mm(h2, du_p, ta=True, b_seg=True, name="d_w_u"), _mm(h2, dz, ta=True, name="d_w_z"), _mm(h2, dxbc, ta=True, name="d_w_xbc"),
              _mm(h2, ddtr, ta=True, name="d_w_dt"), _mm(h2, dgl, ta=True, name="d_w_gate")]
    gw["w_in"] = _reshard_w_in(d_w_in, seg_lens, sh["w_in"].shape[1], name="reshard_d_w_in")
    group_c = ("w_proj_s5", "s5_w_glu", "conv_w")
    parts_c = level1(group_c + ("w_in",), "c")
    c1 = int(D * 0.45) // 16 * 16
    c2 = c1 + D // 4 // 16 * 16
    win = [parts_c[3][:, :c1], parts_c[3][:, c1:c2], parts_c[3][:, c2:]]
    dh2 = _mm(du_p, w_u, tb=True, a_seg=True, name="d_h2_u")
    dh2, arr = _mm(dz, w_z, tb=True, add=dh2, carry=_carry_chips(parts_c[:3]), name="d_h2_z")
    arrived.update(zip(group_c, arr))
    dh2, (arr0,) = _mm(dxbc, w_xbc, tb=True, add=dh2, carry=_carry_chips([win[0]]), name="d_h2_xbc")
    dh2, (arr1,) = _mm(dgl, w_gl, tb=True, add=dh2, carry=_carry_chips([win[1]]), name="d_h2_gate")
    dh2 = _mm(ddtr, w_dt, tb=True, add=dh2, name="d_h2_dt")
    (dx1,), (gs["mix_norm"],) = _rows_bwd(_f_rmsnorm, [x1], [mix_norm], [dh2], name="mix_norm_bwd", want_rows=[0], adds={0: dx2})

    def carry_ffn1_down(dwd):
        gw["ffn1_w_down"] = dwd
        return _carry_chips(level1(("ffn1_w_down",), "d") + [win[2]])

    def carry_ffn1_gate_up(dwg, dwu):
        gw["ffn1_w_gate"], gw["ffn1_w_up"] = dwg, dwu
        return _carry_chips(level1(("ffn1_w_gate", "ffn1_w_up"), "e"))

    dx0, gs["ffn1_norm"], _, _, _, arr_w, arr_h = _ffn_bwd(
        sv1, *ffn1_w, dx1, "ffn1", carry_after_dwd=carry_ffn1_down, carry_after_dwgu=carry_ffn1_gate_up)
    arrived["ffn1_w_down"], arr2 = arr_w
    arrived["ffn1_w_gate"], arrived["ffn1_w_up"] = arr_h
    arrived["w_in"] = jnp.concatenate([arr0, arr1, arr2], axis=1)

    small_shapes = {k: (P[k][0].shape if P[k].ndim > 1 else P[k].shape) for k in _SMALL}
    pack = lambda d: jnp.concatenate([_pad_flat(d[k], TILE_ELEMS) for k in _SMALL]).reshape(-1, LANES)
    gsmall = _sum_slots(_all_gather([pack(gs)], name="gather_small_grads")[0], name="sum_small_grads")
    snum = {k: math.prod(small_shapes[k]) for k in _SMALL}
    ssz = {k: -(-snum[k] // TILE_ELEMS) * TILE_ELEMS for k in _SMALL}

    grads, delta, new_m, new_v = {}, {}, {}, {}
    for k in _BIG:
        grads[k], delta[k], new_m[k], new_v[k] = _reduce_adamw(
            chip_sums[k], arrived[k], chip, P[k][0], P["m_" + k][0], P["v_" + k][0], name="adamw_" + k)
    d_s, m_s, v_s = _adamw(gsmall, pack({k: P[k] for k in _SMALL}), pack({k: P["m_" + k] for k in _SMALL}),
                           pack({k: P["v_" + k] for k in _SMALL}), name="adamw_small")
    off = 0
    gflat, dflat, mflat, vflat = gsmall.reshape(-1), d_s.reshape(-1), m_s.reshape(-1), v_s.reshape(-1)
    for k in _SMALL:
        n = snum[k]
        grads[k], delta[k], new_m[k], new_v[k] = (a[off:off + n] for a in (gflat, dflat, mflat, vflat))
        off += ssz[k]

    loss = lax.psum(lossv[0, 0], ("x", "y", "c"))
    out = [loss, dx0.reshape(x.shape)]
    for d in (grads, delta, new_m, new_v):
        out += [d[k].reshape(P[k].shape) for k in _WEIGHTS]
    return tuple(out)
```

```python
import math
from typing import Callable, NamedTuple

import jax
import jax.numpy as jnp
from jax import lax
from jax.experimental import pallas as pl
from jax.experimental.pallas import tpu as pltpu

f32 = jnp.float32
bf16 = jnp.bfloat16
_S = jax.ShapeDtypeStruct

EPS = 1e-6
S5_GROUP = 16
S5_STATE = 64
HEADDIM = 64
SSD_STATE = 128
CHUNK = 64
CONV_K = 4
NSEG = 8
S5_GPB = 16
N_DEV = 8
LANES = 128
TILE_ELEMS = 8 * LANES

ADAM_LR = 0.001
ADAM_B1 = 0.9
ADAM_B2 = 0.999
ADAM_EPS = 1e-08
ADAM_WD = 0.01
ADAM_STEP = 10

VMEM_LIMIT = 56 * 1024 * 1024
MM_FULL_K = 3072
MM_MAX_TN = 3072
EPI_TM = 128
EW_BLOCK_ELEMS = 1 << 20
MESH = pl.DeviceIdType.MESH


def _cparams(sem=None):
    return pltpu.CompilerParams(dimension_semantics=sem, vmem_limit_bytes=VMEM_LIMIT)


def _pick(dim, pref, align=LANES):
    best = None
    t = align
    while t <= min(dim, pref):
        if dim % t == 0:
            best = t
        t += align
    return best or dim


def _slot_of(k):
    return (k & 1) * (N_DEV // 2) + (k >> 1)


def _mm(a, b, *, name, ta=False, tb=False, a_blk=None, b_blk=None, o_blk=None, o_slots=False, a_seg=False, b_seg=False,
        o_seg=False, tm=None, out_dtype=f32, scale=1.0, add=None, epi=None, carry=None):
    a2, b2 = a.shape[-2:], b.shape[-2:]
    Ma, Ka = (a2[1], a2[0]) if ta else a2
    Kb, Nb = (b2[1], b2[0]) if tb else b2
    M = Ma * (a.shape[0] if a_blk == "m" else 1)
    K = Ka * (a.shape[0] if a_blk == "k" else 1)
    N = Nb * (b.shape[0] if b_blk == "n" else 1)
    assert K == Kb * (b.shape[0] if b_blk == "k" else 1), (a.shape, b.shape, ta, tb, a_blk, b_blk)
    assert (a.ndim == 3) == (a_blk is not None) and (b.ndim == 3) == (b_blk is not None)
    tm = Ma if a_blk == "m" else (tm or _pick(M, 512))
    tn = Nb if b_blk == "n" else _pick(N, MM_MAX_TN)
    if a_blk == "k" or b_blk == "k":
        tk = Ka if a_blk == "k" else Kb
        assert tk == (Kb if b_blk == "k" else tk)
    else:
        tk = K if K <= MM_FULL_K else _pick(K, 1024 if ta else MM_FULL_K)
    if (a_seg and not ta) or o_seg:
        tm = M // NSEG
    if (a_seg and ta) or b_seg:
        tk = K // NSEG
    gm, gn, nk = M // tm, N // tn, K // tk
    assert not (add is not None and (o_seg or o_blk)) and not (o_blk and o_seg)

    if a_seg:
        assert a.ndim == 2
        a = a.reshape(a.shape[0] // NSEG, NSEG * a.shape[1])
        if ta:
            a_spec = pl.BlockSpec((tk, tm), lambda i, j, k: (0, k * (Ma // tm) + i))
        else:
            a_spec = pl.BlockSpec((tm, tk), lambda i, j, k: (0, i * (Ka // tk) + k))
    elif a.ndim == 3:
        lead = (lambda i, k: i) if a_blk == "m" else (lambda i, k: k)
        if ta:
            a_spec = pl.BlockSpec((None, tk, tm), lambda i, j, k: (lead(i, k), 0 if a_blk == "k" else k, 0 if a_blk == "m" else i))
        else:
            a_spec = pl.BlockSpec((None, tm, tk), lambda i, j, k: (lead(i, k), 0 if a_blk == "m" else i, 0 if a_blk == "k" else k))
    else:
        a_spec = pl.BlockSpec((tk, tm), lambda i, j, k: (k, i)) if ta else pl.BlockSpec((tm, tk), lambda i, j, k: (i, k))
    if b_seg:
        assert b.ndim == 2 and not tb
        b = b.reshape(b.shape[0] // NSEG, NSEG * b.shape[1])
        b_spec = pl.BlockSpec((tk, tn), lambda i, j, k: (0, k * (Nb // tn) + j))
    elif b.ndim == 3:
        lead = (lambda j, k: j) if b_blk == "n" else (lambda j, k: k)
        if tb:
            b_spec = pl.BlockSpec((None, tn, tk), lambda i, j, k: (lead(j, k), 0 if b_blk == "n" else j, 0 if b_blk == "k" else k))
        else:
            b_spec = pl.BlockSpec((None, tk, tn), lambda i, j, k: (lead(j, k), 0 if b_blk == "k" else k, 0 if b_blk == "n" else j))
    else:
        b_spec = pl.BlockSpec((tn, tk), lambda i, j, k: (j, k)) if tb else pl.BlockSpec((tk, tn), lambda i, j, k: (k, j))
    slot = _slot_of if o_slots else (lambda k: k)
    if o_blk == "n":
        assert gn == N_DEV or not o_slots
        o_shape, o_spec = (gn, M, tn), pl.BlockSpec((None, tm, tn), lambda i, j, k: (slot(j), i, 0))
    elif o_blk == "m" and o_slots and gm < N_DEV:
        rs = M // N_DEV
        per_tile = tm // rs
        assert per_tile % 2 == 0 and tm % rs == 0
        o_shape = (2, N_CHIP, rs, N)
        o_spec = pl.BlockSpec((2, per_tile // 2, rs, tn), lambda i, j, k: (0, i, 0, j))
    elif o_blk == "m":
        assert gm == N_DEV or not o_slots
        o_shape, o_spec = (gm, tm, N), pl.BlockSpec((None, tm, tn), lambda i, j, k: (slot(i), 0, j))
    elif o_seg:
        o_shape, o_spec = (tm, NSEG * N), pl.BlockSpec((tm, tn), lambda i, j, k: (0, i * (N // tn) + j))
    else:
        o_shape, o_spec = (M, N), pl.BlockSpec((tm, tn), lambda i, j, k: (i, j))
    dims = (((0 if ta else 1,), (1 if tb else 0,)), ((), ()))
    if epi is not None:
        assert gn == 1 and add is None and o_blk is None and not o_seg
        epi_fn, add, epi_w = epi
        o_shape, o_spec = (M, epi_w), pl.BlockSpec((tm, epi_w), lambda i, j, k: (i, 0))
    has_add = add is not None
    add_spec = pl.BlockSpec((tm, add.shape[1]), lambda i, j, k: (i, 0)) if epi is not None else o_spec

    carry = carry or _NO_CARRY
    n_in = 2 + has_add

    def body(*refs):
        own, c_in, c_out, c_sems = _carry_split(carry, refs, n_in, 1)
        a_ref, b_ref = own[0], own[1]
        add_ref = own[2] if has_add else None
        o_ref, acc_ref = own[-2], own[-1]
        i, j, k = pl.program_id(0), pl.program_id(1), pl.program_id(2)
        _carry_start(carry, c_in, c_out, c_sems, (i == 0) & (j == 0) & (k == 0))

        @pl.when(k == 0)
        def _():
            acc_ref[...] = jnp.zeros_like(acc_ref)

        acc_ref[...] += lax.dot_general(a_ref[...].astype(bf16), b_ref[...].astype(bf16), dims, preferred_element_type=f32)

        @pl.when(k == nk - 1)
        def _():
            r = acc_ref[...] * scale
            if epi is not None:
                r = epi_fn(r, add_ref[...].astype(f32))
            elif has_add:
                r = r + add_ref[...].astype(f32)
            if len(o_shape) == 4:
                rs = o_shape[2]
                for chip_l in range(o_ref.shape[1]):
                    for core in range(2):
                        dev = 2 * chip_l + core
                        o_ref[core, chip_l] = r[dev * rs:(dev + 1) * rs].astype(out_dtype)
            else:
                o_ref[...] = r.astype(out_dtype)

        _carry_finish(carry, c_in, c_out, c_sems, (i == gm - 1) & (j == gn - 1) & (k == nk - 1))

    ins = [a, b] + ([add] if has_add else []) + list(carry.ins)
    in_specs = [a_spec, b_spec] + ([add_spec] if has_add else []) + [_ANY] * len(carry.ins)
    res = pl.pallas_call(
        body, name=name, grid=(gm, gn, nk), in_specs=in_specs, out_specs=[o_spec] + [_ANY] * len(carry.out_shapes),
        out_shape=[_S(o_shape, out_dtype)] + list(carry.out_shapes),
        scratch_shapes=[pltpu.VMEM((tm, tn), f32)] + list(carry.sems),
        compiler_params=_cparams(("arbitrary",) * 3 if carry.ins else ("parallel", "parallel", "arbitrary")),
    )(*ins)
    out = res[0]
    if len(o_shape) == 4:
        out = out.reshape(N_DEV, o_shape[2], N)
    elif o_seg:
        out = out.reshape(M, N)
    return (out, list(res[1:])) if carry.ins else out


def _row_tile(T, widths):
    budget = 6 * 1024 * 1024
    tb = max(16, budget // (4 * sum(widths)))
    return _pick(T, tb, align=16)


def _rows(fn, rows, params, outs, *, name, carry=None):
    T = rows[0].shape[0]
    nr, npar = len(rows), len(params)
    tb = _row_tile(T, [r.shape[1] for r in rows] + [w for w, _ in outs])
    carry = carry or _NO_CARRY

    def body(*refs):
        own, c_in, c_out, c_sems = _carry_split(carry, refs, nr + npar, len(outs))
        _carry_start(carry, c_in, c_out, c_sems, pl.program_id(0) == 0)
        ins = [r[...].astype(f32) for r in own[: nr + npar]]
        res = fn(*ins)
        for o_ref, r in zip(own[nr + npar:], res):
            o_ref[...] = r.astype(o_ref.dtype)
        _carry_finish(carry, c_in, c_out, c_sems, pl.program_id(0) == T // tb - 1)

    in_specs = [pl.BlockSpec((tb, r.shape[1]), lambda i: (i, 0)) for r in rows]
    in_specs += [pl.BlockSpec(p.shape, lambda i: (0, 0)) for p in params]
    out_specs = [pl.BlockSpec((tb, w), lambda i: (i, 0)) for w, _ in outs]
    res = pl.pallas_call(
        body, name=name, grid=(T // tb,), in_specs=in_specs + [_ANY] * len(carry.ins),
        out_specs=out_specs + [_ANY] * len(carry.out_shapes), out_shape=[_S((T, w), d) for w, d in outs] + list(carry.out_shapes),
        scratch_shapes=list(carry.sems), compiler_params=_cparams(("arbitrary",) if carry.ins else ("parallel",)),
    )(*rows, *params, *carry.ins)
    return (tuple(res[:len(outs)]), list(res[len(outs):])) if carry.ins else tuple(res)


def _rows_bwd(fn, rows, params, cots, *, name, want_rows, row_dtypes=None, adds=None):
    T = rows[0].shape[0]
    nr, npar, nc = len(rows), len(params), len(cots)
    adds = adds or {}
    add_idx = sorted(adds)
    row_dtypes = row_dtypes or {}
    widths = [r.shape[1] for r in rows] + [c.shape[1] for c in cots] + [rows[i].shape[1] for i in want_rows]
    tb = _row_tile(T, widths)

    def body(*refs):
        ins = [r[...].astype(f32) for r in refs[: nr + npar]]
        cot = tuple(r[...].astype(f32) for r in refs[nr + npar: nr + npar + nc])
        add_refs = refs[nr + npar + nc: nr + npar + nc + len(add_idx)]
        out_refs = refs[nr + npar + nc + len(add_idx):]
        _, vjp = jax.vjp(lambda *a: tuple(fn(*a)), *ins)
        g = vjp(cot)
        for o_ref, i in zip(out_refs[: len(want_rows)], want_rows):
            r = g[i]
            if i in adds:
                r = r + add_refs[add_idx.index(i)][...].astype(f32)
            o_ref[...] = r.astype(o_ref.dtype)
        first = pl.program_id(0) == 0
        for o_ref, gp in zip(out_refs[len(want_rows):], g[nr:]):
            @pl.when(first)
            def _(o_ref=o_ref):
                o_ref[...] = jnp.zeros_like(o_ref)

            o_ref[...] += gp

    in_specs = [pl.BlockSpec((tb, r.shape[1]), lambda i: (i, 0)) for r in rows]
    in_specs += [pl.BlockSpec(p.shape, lambda i: (0, 0)) for p in params]
    in_specs += [pl.BlockSpec((tb, c.shape[1]), lambda i: (i, 0)) for c in cots]
    in_specs += [pl.BlockSpec((tb, adds[i].shape[1]), lambda i_: (i_, 0)) for i in add_idx]
    out_specs = [pl.BlockSpec((tb, rows[i].shape[1]), lambda i_: (i_, 0)) for i in want_rows]
    out_specs += [pl.BlockSpec(p.shape, lambda i: (0, 0)) for p in params]
    out_shape = [_S(rows[i].shape, row_dtypes.get(i, f32)) for i in want_rows] + [_S(p.shape, f32) for p in params]
    res = pl.pallas_call(
        body, name=name, grid=(T // tb,), in_specs=in_specs, out_specs=out_specs, out_shape=out_shape,
        compiler_params=_cparams(("arbitrary",)),
    )(*rows, *params, *cots, *[adds[i] for i in add_idx])
    return list(res[: len(want_rows)]), list(res[len(want_rows):])


def _f_rmsnorm(x, g):
    return (x * lax.rsqrt(jnp.mean(x * x, axis=-1, keepdims=True) + EPS) * g,)


def _f_swiglu(ab):
    F = ab.shape[1] // 2
    return (jax.nn.silu(ab[:, :F]) * ab[:, F:],)


def _f_s5_post(y, u, d):
    return (jax.nn.gelu(y + d * u),)


def _f_glu(g, v, b):
    return (g * jax.nn.sigmoid(v + b),)


def _f_gated_norm(y, z, w):
    return _f_rmsnorm(y * jax.nn.silu(z), w)


def _f_merge(gl, p5, pssd, b):
    D = p5.shape[1]
    gates = jax.nn.sigmoid(gl + b)
    return (gates[:, :D] * p5 + gates[:, D:] * pssd,)


def _f_dt(dtr, bias, a_log):
    dt = jax.nn.softplus(dtr + bias)
    return dt, dt * (-jnp.exp(a_log))


def _f_dt_expand(dtr, bias, a_log, e):
    dt, a = _f_dt(dtr, bias, a_log)
    return dt, a, _doth(dt, e), _doth(a, e)


def _loss_stage(x, tgt, g, *, name):
    T, D = x.shape
    tb = _row_tile(T, [D, D, D])

    def f(xb, gb, tb_):
        y = _f_rmsnorm(xb, gb)[0]
        return 0.5 * jnp.sum(jnp.mean(jnp.square(y - tb_), axis=-1, keepdims=True), axis=0, keepdims=True)

    def body(x_ref, t_ref, g_ref, l_ref, dx_ref, dg_ref):
        tv = t_ref[...]
        val, vjp = jax.vjp(lambda a, b: f(a, b, tv), x_ref[...], g_ref[...])
        dx, dg = vjp(jnp.ones((1, 1), f32))
        dx_ref[...] = dx

        @pl.when(pl.program_id(0) == 0)
        def _():
            l_ref[...] = jnp.zeros_like(l_ref)
            dg_ref[...] = jnp.zeros_like(dg_ref)

        l_ref[...] += jnp.broadcast_to(val, l_ref.shape)
        dg_ref[...] += dg

    row = pl.BlockSpec((tb, D), lambda i: (i, 0))
    par = pl.BlockSpec((1, D), lambda i: (0, 0))
    return pl.pallas_call(
        body, name=name, grid=(T // tb,), in_specs=[row, row, par],
        out_specs=[pl.BlockSpec((1, LANES), lambda i: (0, 0)), row, par],
        out_shape=[_S((1, LANES), f32), _S((T, D), f32), _S((1, D), f32)], compiler_params=_cparams(("arbitrary",)),
    )(x, tgt, g)


CONV_R = 64
HALO = 8


def _conv_shifts_down(ref, t):
    if isinstance(t, int) and t == 0:
        cur = ref[0:CONV_R, :]
        row = lax.broadcasted_iota(jnp.int32, cur.shape, 0)
        return [cur] + [jnp.where(row >= s, pltpu.roll(cur, s, axis=0), 0.0) for s in range(1, CONV_K)]
    win = ref[pl.ds(pl.multiple_of(t * CONV_R - HALO, HALO), CONV_R + HALO), :]
    return [win[HALO:]] + [pltpu.roll(win, s, axis=0)[HALO:] for s in range(1, CONV_K)]


def _conv_shifts_up(ref, t):
    if isinstance(t, int):
        cur = ref[t * CONV_R:(t + 1) * CONV_R, :]
        row = lax.broadcasted_iota(jnp.int32, cur.shape, 0)
        return [cur] + [jnp.where(row < CONV_R - s, pltpu.roll(cur, CONV_R - s, axis=0), 0.0) for s in range(1, CONV_K)]
    win = ref[pl.ds(pl.multiple_of(t * CONV_R, HALO), CONV_R + HALO), :]
    return [win[:CONV_R]] + [pltpu.roll(win, CONV_R + HALO - s, axis=0)[:CONV_R] for s in range(1, CONV_K)]


def _conv_pre(shifted, w, b):
    pre = b
    for k in range(CONV_K):
        pre = pre + w[k:k + 1, :] * shifted[CONV_K - 1 - k]
    return pre


def _conv_fwd(x, w, b, *, name):
    T, C = x.shape
    cb = _pick(C, 256)

    def body(x_ref, w_ref, b_ref, o_ref):
        xv = x_ref[...]
        row = lax.broadcasted_iota(jnp.int32, xv.shape, 0)
        shifted = [xv] + [jnp.where(row >= s, pltpu.roll(xv, s, axis=0), 0.0) for s in range(1, CONV_K)]
        o_ref[...] = jax.nn.silu(_conv_pre(shifted, w_ref[...], b_ref[...]))

    col = pl.BlockSpec((T, cb), lambda j: (0, j))
    return pl.pallas_call(
        body, name=name, grid=(C // cb,), in_specs=[col, pl.BlockSpec((CONV_K, cb), lambda j: (0, j)), pl.BlockSpec((1, cb), lambda j: (0, j))],
        out_specs=col, out_shape=_S((T, C), f32), compiler_params=_cparams(("parallel",)),
    )(x, w, b)


def _conv_bwd(x, w, b, dy, *, name, carry=None):
    T, C = x.shape
    cb = _pick(C, 128)
    carry = carry or _NO_CARRY
    ends = []
    for d in dy:
        ends.append((ends[-1] if ends else 0) + d.shape[1] // cb)
    assert ends[-1] == C // cb and all(d.shape[1] % cb == 0 for d in dy)
    npc = len(dy)
    n_tiles = T // CONV_R

    def body(*refs):
        own, c_in, c_out, c_sems = _carry_split(carry, refs, 3 + npc, 3)
        x_ref, w_ref, b_ref = own[:3]
        dy_refs, (dx_ref, dw_ref, db_ref, dpre_ref) = own[3:3 + npc], own[3 + npc:]
        _carry_start(carry, c_in, c_out, c_sems, pl.program_id(0) == 0)
        j = pl.program_id(0)
        wv, bv = w_ref[...], b_ref[...]

        def fold8(v):
            return jnp.sum(v.reshape(CONV_R // 8, 8, cb), axis=0)

        def first_pass(t, acc):
            rows = slice(0, CONV_R) if isinstance(t, int) else pl.ds(pl.multiple_of(t * CONV_R, CONV_R), CONV_R)
            shifted = _conv_shifts_down(x_ref, t)
            pre = _conv_pre(shifted, wv, bv)
            dyv = dy_refs[-1][rows, :]
            for p in range(npc - 2, -1, -1):
                dyv = jnp.where(j < ends[p], dy_refs[p][rows, :], dyv)
            sg = jax.nn.sigmoid(pre)
            dpre = dyv * sg * (1.0 + pre * (1.0 - sg))
            dpre_ref[rows, :] = dpre
            return tuple(acc[k] + fold8(dpre * shifted[CONV_K - 1 - k]) for k in range(CONV_K)) + (acc[CONV_K] + fold8(dpre),)

        zero = jnp.zeros((8, cb), f32)
        acc = lax.fori_loop(1, n_tiles, first_pass, first_pass(0, (zero,) * (CONV_K + 1)), unroll=3)
        for k in range(CONV_K):
            dw_ref[k:k + 1, :] = jnp.sum(acc[k], axis=0, keepdims=True)
        db_ref[...] = jnp.sum(acc[CONV_K], axis=0, keepdims=True)

        def dx_of(t):
            up = _conv_shifts_up(dpre_ref, t)
            dx = wv[CONV_K - 1:CONV_K, :] * up[0]
            for k in range(CONV_K - 1):
                dx = dx + wv[k:k + 1, :] * up[CONV_K - 1 - k]
            return dx

        def second_pass(t, c):
            dx_ref[pl.ds(pl.multiple_of(t * CONV_R, CONV_R), CONV_R), :] = dx_of(t)
            return c

        lax.fori_loop(0, n_tiles - 1, second_pass, 0, unroll=3)
        dx_ref[(n_tiles - 1) * CONV_R:, :] = dx_of(n_tiles - 1)
        _carry_finish(carry, c_in, c_out, c_sems, pl.program_id(0) == C // cb - 1)

    col = pl.BlockSpec((T, cb), lambda j: (0, j))
    wsp = pl.BlockSpec((CONV_K, cb), lambda j: (0, j))
    bsp = pl.BlockSpec((1, cb), lambda j: (0, j))
    starts = [0] + ends[:-1]
    dy_specs = [pl.BlockSpec((T, cb), lambda j, s=s, e=e: (0, jnp.clip(j, s, e - 1) - s)) for s, e in zip(starts, ends)]
    res = pl.pallas_call(
        body, name=name, grid=(C // cb,), in_specs=[col, wsp, bsp] + dy_specs + [_ANY] * len(carry.ins),
        out_specs=[col, wsp, bsp] + [_ANY] * len(carry.out_shapes),
        out_shape=[_S((T, C), f32), _S((CONV_K, C), f32), _S((1, C), f32)] + list(carry.out_shapes),
        scratch_shapes=[pltpu.VMEM((T, cb), f32)] + list(carry.sems),
        compiler_params=_cparams(("arbitrary",) if carry.ins else ("parallel",)),
    )(x, w, b, *dy, *carry.ins)
    return (*res[:3], list(res[3:]))


def _f_s5_prep(lr, li, ldt, lrb, lib, ldtb, brt, bit):
    def disc(lr_, li_, ldt_):
        dt = jnp.exp(ldt_)
        mag = jnp.exp(lr_ * dt)
        ar, ai = mag * jnp.cos(li_ * dt), mag * jnp.sin(li_ * dt)
        den = lr_ * lr_ + li_ * li_
        cr = ((ar - 1.0) * lr_ + ai * li_) / den
        ci = (ai * lr_ - (ar - 1.0) * li_) / den
        return ar, ai, cr, ci

    ar, ai, _, _ = disc(lr, li, ldt)
    _, _, cr, ci = disc(lrb, lib, ldtb)
    return ar, ai, cr * brt - ci * bit, cr * bit + ci * brt


def _s5_prep(args, *, name):
    G, N = args[0].shape
    GM = args[3].shape[0]

    def body(*refs):
        res = _f_s5_prep(*[r[...] for r in refs[:8]])
        for o, r in zip(refs[8:], res):
            o[...] = r

    return pl.pallas_call(body, name=name, out_shape=[_S((G, N), f32)] * 2 + [_S((GM, N), f32)] * 2)(*args)


def _s5_prep_bwd(args, cots, rsum, *, name):
    G, N = args[0].shape
    GM = args[3].shape[0]

    def body(*refs):
        ins = [r[...] for r in refs[:8]]
        cot = tuple(r[...] for r in refs[8:12])
        rs = refs[12][...]
        _, vjp = jax.vjp(_f_s5_prep, *ins)
        g = vjp(cot)
        fold = lambda v: jnp.dot(rs, v, preferred_element_type=f32, precision=lax.Precision.HIGHEST)
        o = refs[13:]
        o[0][...] = g[0] + fold(g[3])
        o[1][...] = g[1] + fold(g[4])
        o[2][...] = g[2] + fold(jnp.broadcast_to(g[5], (GM, LANES)))[:, 0:1]
        o[3][...] = g[6]
        o[4][...] = g[7]

    return pl.pallas_call(
        body, name=name, out_shape=[_S((G, N), f32), _S((G, N), f32), _S((G, 1), f32), _S((GM, N), f32), _S((GM, N), f32)],
    )(*args, *cots, rsum)


S5_TC = 512


def _s5_local_scan(src, w_r, w_i, a_r, a_i, *, reverse, name, carry=None, powers=True):
    T, C = src.shape
    nblk, cb, sb = w_r.shape
    NS = nblk * sb
    tc = min(S5_TC, T)
    nT, nt = T // tc, tc // NSEG
    tmap = (lambda i: nT - 1 - i) if reverse else (lambda i: i)

    carry = carry or _NO_CARRY

    def body(*refs):
        own, c_in, c_out, c_sems = _carry_split(carry, refs, 5, 4)
        u_ref, wr_ref, wi_ref, ar_ref, ai_ref, sr_ref, si_ref, pr_ref, pi_ref, st_r, st_i, pw_r, pw_i = own
        _carry_start(carry, c_in, c_out, c_sems, (pl.program_id(0) == 0) & (pl.program_id(1) == 0))

        @pl.when(pl.program_id(1) == 0)
        def _():
            st_r[...] = jnp.zeros_like(st_r)
            st_i[...] = jnp.zeros_like(st_i)
            pw_r[...] = jnp.ones_like(pw_r)
            pw_i[...] = jnp.zeros_like(pw_i)

        u = u_ref[...].astype(bf16)
        sr_ref[...] = jnp.dot(u, wr_ref[...], preferred_element_type=f32)
        si_ref[...] = jnp.dot(u, wi_ref[...], preferred_element_type=f32)
        ar = jnp.broadcast_to(ar_ref[...], (NSEG, sb))
        ai = jnp.broadcast_to(ai_ref[...], (NSEG, sb))

        def step(k, c):
            cr, ci, qr, qi = c
            kk = (nt - 1 - k) if reverse else k
            rows = pl.ds(pl.multiple_of(kk * NSEG, NSEG), NSEG)
            nr = ar * cr - ai * ci + sr_ref[rows, :]
            ni = ar * ci + ai * cr + si_ref[rows, :]
            sr_ref[rows, :] = nr
            si_ref[rows, :] = ni
            return (nr, ni, ar * qr - ai * qi, ar * qi + ai * qr) if powers else (nr, ni, qr, qi)

        cr, ci, qr, qi = lax.fori_loop(0, nt, step, (st_r[...], st_i[...], pw_r[...], pw_i[...]), unroll=8)
        st_r[...], st_i[...], pw_r[...], pw_i[...] = cr, ci, qr, qi
        pr_ref[...] = qr
        pi_ref[...] = qi
        _carry_finish(carry, c_in, c_out, c_sems, (pl.program_id(0) == nblk - 1) & (pl.program_id(1) == nT - 1))

    blk = pl.BlockSpec((tc, sb), lambda j, i: (tmap(i), j))
    wsp = pl.BlockSpec((None, cb, sb), lambda j, i: (j, 0, 0))
    asp = pl.BlockSpec((1, sb), lambda j, i: (0, j))
    psp = pl.BlockSpec((NSEG, sb), lambda j, i: (0, j))
    res = pl.pallas_call(
        body, name=name, grid=(nblk, nT),
        in_specs=[pl.BlockSpec((tc, cb), lambda j, i: (tmap(i), j)), wsp, wsp, asp, asp] + [_ANY] * len(carry.ins),
        out_specs=[blk, blk, psp, psp] + [_ANY] * len(carry.out_shapes),
        out_shape=[_S((T, NS), f32)] * 2 + [_S((NSEG, NS), f32)] * 2 + list(carry.out_shapes),
        scratch_shapes=[pltpu.VMEM((NSEG, sb), f32)] * 4 + list(carry.sems),
        compiler_params=_cparams(("arbitrary", "arbitrary") if carry.ins else ("parallel", "arbitrary")),
    )(src, w_r, w_i, a_r, a_i, *carry.ins)
    return (*res[:4], list(res[4:])) if carry.ins else res


def _s5_carry(e_r, e_i, p_r, p_i, *, reverse, name):
    NS = e_r.shape[1]

    def body(er_ref, ei_ref, pr_ref, pi_ref, cr_ref, ci_ref):
        ar, ai = pr_ref[0:1, :], pi_ref[0:1, :]
        cr = jnp.zeros((1, NS), f32)
        ci = jnp.zeros((1, NS), f32)
        order = list(range(NSEG - 1, -1, -1)) if reverse else list(range(NSEG))
        cr_ref[order[0]:order[0] + 1, :] = cr
        ci_ref[order[0]:order[0] + 1, :] = ci
        for prev, q in zip(order[:-1], order[1:]):
            er, ei = er_ref[prev:prev + 1, :], ei_ref[prev:prev + 1, :]
            cr, ci = er + ar * cr - ai * ci, ei + ar * ci + ai * cr
            cr_ref[q:q + 1, :] = cr
            ci_ref[q:q + 1, :] = ci

    return pl.pallas_call(body, name=name, out_shape=[_S((NSEG, NS), f32)] * 2)(e_r, e_i, p_r, p_i)


def _s5_fix_out(sl_r, sl_i, a_r, a_i, c_r, c_i, wc_r, wc_i, *, name):
    T, NS = sl_r.shape
    nblk, sb, cb = wc_r.shape
    tc = min(S5_TC, T)
    nT, nt = T // tc, tc // NSEG

    def body(lr_ref, li_ref, ar_ref, ai_ref, cr_ref, ci_ref, wr_ref, wi_ref, sr_ref, si_ref, y_ref, pw_r, pw_i):
        @pl.when(pl.program_id(1) == 0)
        def _():
            pw_r[...] = jnp.ones_like(pw_r)
            pw_i[...] = jnp.zeros_like(pw_i)

        ar = jnp.broadcast_to(ar_ref[...], (NSEG, sb))
        ai = jnp.broadcast_to(ai_ref[...], (NSEG, sb))
        cr, ci = cr_ref[...], ci_ref[...]

        def step(k, c):
            qr, qi = c
            qr, qi = ar * qr - ai * qi, ar * qi + ai * qr
            rows = pl.ds(pl.multiple_of(k * NSEG, NSEG), NSEG)
            sr_ref[rows, :] = lr_ref[rows, :] + qr * cr - qi * ci
            si_ref[rows, :] = li_ref[rows, :] + qr * ci + qi * cr
            return qr, qi

        qr, qi = lax.fori_loop(0, nt, step, (pw_r[...], pw_i[...]), unroll=8)
        pw_r[...], pw_i[...] = qr, qi
        y_ref[...] = (jnp.dot(sr_ref[...].astype(bf16), wr_ref[...], preferred_element_type=f32)
                      - jnp.dot(si_ref[...].astype(bf16), wi_ref[...], preferred_element_type=f32))

    blk = pl.BlockSpec((tc, sb), lambda j, i: (i, j))
    asp = pl.BlockSpec((1, sb), lambda j, i: (0, j))
    csp = pl.BlockSpec((NSEG, sb), lambda j, i: (0, j))
    wsp = pl.BlockSpec((None, sb, cb), lambda j, i: (j, 0, 0))
    return pl.pallas_call(
        body, name=name, grid=(nblk, nT), in_specs=[blk, blk, asp, asp, csp, csp, wsp, wsp],
        out_specs=[blk, blk, pl.BlockSpec((tc, cb), lambda j, i: (i, j))],
        out_shape=[_S((T, NS), f32)] * 2 + [_S((T, nblk * cb), f32)],
        scratch_shapes=[pltpu.VMEM((NSEG, sb), f32)] * 2, compiler_params=_cparams(("parallel", "arbitrary")),
    )(sl_r, sl_i, a_r, a_i, c_r, c_i, wc_r, wc_i)


def _s5_fix_bwd(ql_r, ql_i, ab_r, ab_i, c_r, c_i, s_r, s_i, sb_r, sb_i, u, dy, du_add, w_r, w_i, *, name):
    T, NS = ql_r.shape
    nblk, cb, sb = w_r.shape
    tc = min(S5_TC, T)
    nT, nt = T // tc, tc // NSEG
    tmap = lambda i: nT - 1 - i

    def body(lr_ref, li_ref, ar_ref, ai_ref, cr_ref, ci_ref, sr_ref, si_ref, br_ref, bi_ref, u_ref, dy_ref, dua_ref, wr_ref, wi_ref,
             du_ref, dwr_ref, dwi_ref, dcr_ref, dci_ref, dar_ref, dai_ref, pw_r, pw_i, ac_r, ac_i, q_r, q_i):
        first = pl.program_id(1) == 0

        @pl.when(first)
        def _():
            pw_r[...] = jnp.ones_like(pw_r)
            pw_i[...] = jnp.zeros_like(pw_i)
            ac_r[...] = jnp.zeros_like(ac_r)
            ac_i[...] = jnp.zeros_like(ac_i)
            dwr_ref[...] = jnp.zeros_like(dwr_ref)
            dwi_ref[...] = jnp.zeros_like(dwi_ref)
            dcr_ref[...] = jnp.zeros_like(dcr_ref)
            dci_ref[...] = jnp.zeros_like(dci_ref)

        ar = jnp.broadcast_to(ar_ref[...], (NSEG, sb))
        ai = jnp.broadcast_to(ai_ref[...], (NSEG, sb))
        cr, ci = cr_ref[...], ci_ref[...]

        def fix(rows, qr, qi, spr, spi, accr, acci):
            qr, qi = ar * qr - ai * qi, ar * qi + ai * qr
            xr = lr_ref[rows, :] + qr * cr - qi * ci
            xi = li_ref[rows, :] + qr * ci + qi * cr
            q_r[rows, :] = xr
            q_i[rows, :] = xi
            return qr, qi, accr + xr * spr + xi * spi, acci + xi * spr - xr * spi

        def step(k, c):
            qr, qi, accr, acci = c
            kk = nt - 1 - k
            rows = pl.ds(pl.multiple_of(kk * NSEG, NSEG), NSEG)
            prev = pl.ds(pl.multiple_of((kk - 1) * NSEG, NSEG), NSEG)
            return fix(rows, qr, qi, sr_ref[prev, :], si_ref[prev, :], accr, acci)

        c = lax.fori_loop(0, nt - 1, step, (pw_r[...], pw_i[...], ac_r[...], ac_i[...]), unroll=7)
        qr, qi, accr, acci = fix(pl.ds(0, NSEG), *c[:2], br_ref[...], bi_ref[...], *c[2:])
        pw_r[...], pw_i[...], ac_r[...], ac_i[...] = qr, qi, accr, acci

        qrb, qib = q_r[...].astype(bf16), q_i[...].astype(bf16)
        nt_dims = (((1,), (1,)), ((), ()))
        tn_dims = (((0,), (0,)), ((), ()))
        du_ref[...] = (dua_ref[...] + lax.dot_general(qrb, wr_ref[...], nt_dims, preferred_element_type=f32)
                       + lax.dot_general(qib, wi_ref[...], nt_dims, preferred_element_type=f32))
        ub = u_ref[...].astype(bf16)
        dwr_ref[...] += lax.dot_general(ub, qrb, tn_dims, preferred_element_type=f32)
        dwi_ref[...] += lax.dot_general(ub, qib, tn_dims, preferred_element_type=f32)
        dyb = dy_ref[...].astype(bf16)
        dcr_ref[...] += lax.dot_general(sr_ref[...].astype(bf16), dyb, tn_dims, preferred_element_type=f32)
        dci_ref[...] -= lax.dot_general(si_ref[...].astype(bf16), dyb, tn_dims, preferred_element_type=f32)

        @pl.when(pl.program_id(1) == nT - 1)
        def _():
            dar_ref[...] = jnp.sum(accr, axis=0, keepdims=True)
            dai_ref[...] = jnp.sum(acci, axis=0, keepdims=True)

    blk = pl.BlockSpec((tc, sb), lambda j, i: (tmap(i), j))
    asp = pl.BlockSpec((1, sb), lambda j, i: (0, j))
    csp = pl.BlockSpec((NSEG, sb), lambda j, i: (0, j))
    bsp = pl.BlockSpec((None, NSEG, sb), lambda j, i: (tmap(i), 0, j))
    chn = pl.BlockSpec((tc, cb), lambda j, i: (tmap(i), j))
    wsp = pl.BlockSpec((None, cb, sb), lambda j, i: (j, 0, 0))
    wcs = pl.BlockSpec((None, sb, cb), lambda j, i: (j, 0, 0))
    return pl.pallas_call(
        body, name=name, grid=(nblk, nT), in_specs=[blk, blk, asp, asp, csp, csp, blk, blk, bsp, bsp, chn, chn, chn, wsp, wsp],
        out_specs=[chn, wsp, wsp, wcs, wcs, asp, asp],
        out_shape=[_S((T, nblk * cb), f32), _S((nblk, cb, sb), f32), _S((nblk, cb, sb), f32), _S((nblk, sb, cb), f32),
                   _S((nblk, sb, cb), f32), _S((1, NS), f32), _S((1, NS), f32)],
        scratch_shapes=[pltpu.VMEM((NSEG, sb), f32)] * 4 + [pltpu.VMEM((tc, sb), f32)] * 2,
        compiler_params=_cparams(("parallel", "arbitrary")),
    )(ql_r, ql_i, ab_r, ab_i, c_r, c_i, s_r, s_i, sb_r, sb_i, u, dy, du_add, w_r, w_i)


SSD2_TB = 512
_NN = (((1,), (0,)), ((), ()))
_NT = (((1,), (1,)), ((), ()))
_TN = (((0,), (0,)), ((), ()))


def _dotf(a, b, dims):
    return lax.dot_general(a.astype(bf16), b.astype(bf16), dims, preferred_element_type=f32)


def _doth(a, b, dims=_NN, sel="b", parts=3):
    x, m = (a, b) if sel == "b" else (b, a)
    m = m.astype(bf16)
    out = None
    for _ in range(parts):
        piece = x.astype(bf16)
        x = x - piece.astype(f32)
        d = lax.dot_general(*((piece, m) if sel == "b" else (m, piece)), dims, preferred_element_type=f32)
        out = d if out is None else out + d
    return out


def _ssd_consts(hpg):
    W = hpg * CHUNK
    i = lax.broadcasted_iota(jnp.int32, (CHUNK, CHUNK), 0)
    j = lax.broadcasted_iota(jnp.int32, (CHUNK, CHUNK), 1)
    tril = (i >= j).astype(f32)
    r = lax.broadcasted_iota(jnp.int32, (W, W), 0)
    c = lax.broadcasted_iota(jnp.int32, (W, W), 1)
    bd = (r // CHUNK == c // CHUNK).astype(f32)
    triu_bd = bd * (r <= c).astype(f32)
    e_r = lax.broadcasted_iota(jnp.int32, (W, LANES), 0)
    e_c = lax.broadcasted_iota(jnp.int32, (W, LANES), 1)
    ered = (e_r // HEADDIM == e_c).astype(f32)
    return tril, jnp.tile(tril, (1, hpg)), bd, triu_bd, ered


def _ssd2_specs(G, hpg, tb, tmap, b_off, c_off):
    W = hpg * HEADDIM
    ncb = tb // CHUNK
    xsp = pl.BlockSpec((tb, W), lambda g, i: (tmap(i), g))
    bsp = pl.BlockSpec((tb, SSD_STATE), lambda g, i: (tmap(i), b_off + g))
    csp = pl.BlockSpec((tb, SSD_STATE), lambda g, i: (tmap(i), c_off + g))
    rsp = pl.BlockSpec((None, ncb, W), lambda g, i: (g, tmap(i), 0))
    dsp = pl.BlockSpec((1, W), lambda g, i: (0, g))
    hsp = pl.BlockSpec((None, ncb, SSD_STATE, W), lambda g, i: (g, tmap(i), 0, 0))
    const = lambda a: pl.BlockSpec(a.shape, lambda g, i: (0, 0))
    return xsp, bsp, csp, rsp, dsp, hsp, const


def _tile_rows(a, n):
    return jnp.concatenate([a] * n, axis=0)


def _ssd2_fwd(xc, dt4, a4, dtw, aw, d4, consts, *, d_inner, name, carry=None):
    carry = carry or _NO_CARRY
    T = xc.shape[0]
    G, nc, W = dtw.shape
    hpg = W // CHUNK
    tb = min(SSD2_TB, T)
    nb, ncb = T // tb, tb // CHUNK
    b_off = d_inner // SSD_STATE
    xsp, bsp, csp, rsp, dsp, hsp, const = _ssd2_specs(G, hpg, tb, lambda i: i, b_off, b_off + G)
    tril, mask4, bd, triu_bd, _ = consts

    def body(*refs):
        own, c_in, c_out, c_sems = _carry_split(carry, refs, 12, 2)
        x_ref, b_ref, c_ref, dt_ref, a_ref, dtw_ref, aw_ref, d_ref, tril_ref, mask_ref, bd_ref, tbd_ref, y_ref, hs_ref, h_scr = own
        _carry_start(carry, c_in, c_out, c_sems, (pl.program_id(0) == 0) & (pl.program_id(1) == 0))

        @pl.when(pl.program_id(1) == 0)
        def _():
            h_scr[...] = jnp.zeros_like(h_scr)

        acs_rows = _doth(aw_ref[...], tbd_ref[...])
        ht = h_scr[...]
        for c in range(ncb):
            rows = slice(c * CHUNK, (c + 1) * CHUNK)
            x, bm, cm = x_ref[rows, :], b_ref[rows, :], c_ref[rows, :]
            acs = _doth(tril_ref[...], a_ref[rows, :], sel="a")
            lmat = jnp.where(mask_ref[...] > 0, jnp.exp(jnp.minimum(acs - acs_rows[c:c + 1, :], 0.0)), 0.0)
            m4 = _dotf(cm, _tile_rows(bm, hpg), _NT) * lmat * dtw_ref[c:c + 1, :]
            xbd = _tile_rows(x, hpg) * bd_ref[...]
            hs_ref[c] = ht
            y_ref[rows, :] = _dotf(m4, xbd, _NN) + _dotf(cm, ht, _NN) * jnp.exp(acs) + d_ref[...] * x
            a_last = acs[CHUNK - 1:CHUNK, :]
            xw = x * (jnp.exp(a_last - acs) * dt_ref[rows, :])
            ht = ht * jnp.exp(a_last) + _dotf(bm, xw, _TN)
        h_scr[...] = ht
        _carry_finish(carry, c_in, c_out, c_sems, (pl.program_id(0) == G - 1) & (pl.program_id(1) == nb - 1))

    res = pl.pallas_call(
        body, name=name, grid=(G, nb),
        in_specs=[xsp, bsp, csp, xsp, xsp, rsp, rsp, dsp, const(tril), const(mask4), const(bd), const(triu_bd)] + [_ANY] * len(carry.ins),
        out_specs=[xsp, hsp] + [_ANY] * len(carry.out_shapes),
        out_shape=[_S((T, G * W), f32), _S((G, nc, SSD_STATE, W), f32)] + list(carry.out_shapes),
        scratch_shapes=[pltpu.VMEM((SSD_STATE, W), f32)] + list(carry.sems),
        compiler_params=_cparams(("arbitrary", "arbitrary") if carry.ins else ("parallel", "arbitrary")),
    )(xc, xc, xc, dt4, a4, dtw, aw, d4, tril, mask4, bd, triu_bd, *carry.ins)
    return res[0], res[1], list(res[2:])


def _ssd2_bwd(xc, dt4, a4, dtw, aw, d4, consts, hs, dy, *, d_inner, name):
    T = xc.shape[0]
    G, nc, W = dtw.shape
    hpg = W // CHUNK
    tb = min(SSD2_TB, T)
    nb, ncb = T // tb, tb // CHUNK
    b_off = d_inner // SSD_STATE
    tmap = lambda i: nb - 1 - i
    xsp, bsp, csp, rsp, dsp, hsp, const = _ssd2_specs(G, hpg, tb, tmap, b_off, b_off + G)
    gsp = pl.BlockSpec((tb, SSD_STATE), lambda g, i: (tmap(i), g))
    ddsp = pl.BlockSpec((None, 1, LANES), lambda g, i: (g, 0, 0))
    tril, mask4, bd, triu_bd, ered = consts

    def body(x_ref, b_ref, c_ref, dt_ref, a_ref, dtw_ref, aw_ref, d_ref, tril_ref, mask_ref, bd_ref, tbd_ref, er_ref, hs_ref, dy_ref,
             dx_ref, db_ref, dc_ref, ddtc_ref, dac_ref, ddtw_ref, daw_ref, dd_ref, g_scr, dd_scr, rw_scr, tl_scr):
        first = pl.program_id(1) == 0

        @pl.when(first)
        def _():
            g_scr[...] = jnp.zeros_like(g_scr)
            dd_scr[...] = jnp.zeros_like(dd_scr)

        mask = mask_ref[...] > 0
        lane_in_block = lax.broadcasted_iota(jnp.int32, mask.shape, 1) & (CHUNK - 1)
        maskt = lax.broadcasted_iota(jnp.int32, mask.shape, 0) <= lane_in_block
        acs_rows = _doth(aw_ref[...], tbd_ref[...])
        dht = g_scr[...]
        dd = dd_scr[...]
        for c in range(ncb - 1, -1, -1):
            rows = slice(c * CHUNK, (c + 1) * CHUNK)
            x, bm, cm, dyc = x_ref[rows, :], b_ref[rows, :], c_ref[rows, :], dy_ref[rows, :]
            dtc, dtr = dt_ref[rows, :], dtw_ref[c:c + 1, :]
            ht = hs_ref[c]
            acs = _doth(tril_ref[...], a_ref[rows, :], sel="a")
            seg = acs - acs_rows[c:c + 1, :]
            lmat = jnp.where(mask, jnp.exp(jnp.minimum(seg, 0.0)), 0.0)
            lmat_t = jnp.where(maskt, jnp.exp(jnp.minimum(-seg, 0.0)), 0.0)
            btile, ctile = _tile_rows(bm, hpg), _tile_rows(cm, hpg)
            g4 = _dotf(cm, btile, _NT)
            gt4 = _dotf(bm, ctile, _NT)
            m4 = g4 * lmat * dtr
            mt4 = gt4 * lmat_t * dtc
            xbd = _tile_rows(x, hpg) * bd_ref[...]
            dybd = _tile_rows(dyc, hpg) * bd_ref[...]
            dm4 = _dotf(dyc, xbd, _NT)
            dmt4 = _dotf(x, dybd, _NT)
            dx = d_ref[...] * dyc + _dotf(mt4, dybd, _NN)
            dd = dd + jnp.sum(dyc * x, axis=0, keepdims=True)
            e4 = dm4 * m4
            dc = _dotf(dm4 * lmat * dtr, btile, _NN)
            db = _dotf(dmt4 * lmat_t * dtc, ctile, _NN)
            decay = jnp.exp(acs)
            yoff = _dotf(cm, ht, _NN) * decay
            dz = dyc * decay
            dc = dc + _dotf(dz, ht, _NT)
            dht_prev = _dotf(cm, dz, _TN)
            a_last = acs[CHUNK - 1:CHUNK, :]
            ea_last = jnp.exp(a_last)
            erel = jnp.exp(a_last - acs)
            dte = erel * dtc
            dxw = _dotf(bm, dht, _NN)
            db = db + _dotf(x * dte, dht, _NT)
            dx = dx + dxw * dte
            q4 = dxw * x
            dacs = e4 + dyc * yoff - q4 * dte
            col = jnp.concatenate([q4 * erel, _doth(tril_ref[...], dacs, _TN, sel="a")], axis=0)
            col = _doth(col, er_ref[...], parts=2)
            ddtc_ref[rows, :] = col[:CHUNK]
            dac_ref[rows, :] = col[CHUNK:]
            ddtw_ref[c:c + 1, :] = jnp.sum(dm4 * g4 * lmat, axis=0, keepdims=True)
            rw_scr[c:c + 1, :] = -jnp.sum(e4, axis=0, keepdims=True)
            tl_scr[c:c + 1, :] = jnp.sum(q4 * dte, axis=0, keepdims=True) + ea_last * jnp.sum(dht * ht, axis=0, keepdims=True)
            dx_ref[rows, :] = dx
            db_ref[rows, :] = db
            dc_ref[rows, :] = dc
            dht = dht_prev + dht * ea_last
        daw_ref[...] = _doth(rw_scr[...], tbd_ref[...], _NT) + _doth(tl_scr[...], bd_ref[...])
        g_scr[...] = dht
        dd_scr[...] = dd

        @pl.when(pl.program_id(1) == nb - 1)
        def _():
            dd_ref[...] = _doth(dd, er_ref[...])

    return pl.pallas_call(
        body, name=name, grid=(G, nb),
        in_specs=[xsp, bsp, csp, xsp, xsp, rsp, rsp, dsp, const(tril), const(mask4), const(bd), const(triu_bd), const(ered), hsp, xsp],
        out_specs=[xsp, gsp, gsp, gsp, gsp, rsp, rsp, ddsp],
        out_shape=[_S((T, G * W), f32), _S((T, G * SSD_STATE), f32), _S((T, G * SSD_STATE), f32), _S((T, G * LANES), f32),
                   _S((T, G * LANES), f32), _S(dtw.shape, f32), _S(dtw.shape, f32), _S((G, 1, LANES), f32)],
        scratch_shapes=[pltpu.VMEM((SSD_STATE, W), f32), pltpu.VMEM((1, W), f32), pltpu.VMEM((ncb, W), f32), pltpu.VMEM((ncb, W), f32)],
        compiler_params=_cparams(("parallel", "arbitrary")),
    )(xc, xc, xc, dt4, a4, dtw, aw, d4, tril, mask4, bd, triu_bd, ered, hs, dy)


def _peers():
    x, y, c = lax.axis_index("x"), lax.axis_index("y"), lax.axis_index("c")
    return x, y, c


_ANY = pl.BlockSpec(memory_space=pl.ANY)
N_CHIP = N_DEV // 2


def _all_gather(shards, *, name):
    n = len(shards)
    carry = _carry_gather(shards)

    def body(*refs):
        x_refs, out_refs, sems = refs[:n], refs[n:2 * n], refs[2 * n:]
        _gather_start(x_refs, out_refs, sems)
        _gather_finish(x_refs, out_refs, sems)

    return pl.pallas_call(
        body, name=name, out_shape=list(carry.out_shapes), in_specs=[_ANY] * n, out_specs=[_ANY] * n, scratch_shapes=list(carry.sems),
    )(*shards)


def _gather_parts(x_refs, out_refs, sems):
    send_sems, recv_sems, local_sems = sems
    x, y, c = _peers()
    me, sibling = (x, y, c), (x, y, 1 - c)
    chips = [(1 - x, y), (x, 1 - y), (1 - x, 1 - y)]
    n = len(x_refs)

    def copy(a, r, block, to, src=None):
        px, py, pc = block
        slot = out_refs[a].at[4 * px + 2 * py + pc]
        return pltpu.make_async_remote_copy(
            src_ref=slot if src is None else src, dst_ref=slot, send_sem=send_sems.at[7 * a + r],
            recv_sem=recv_sems.at[7 * a + r], device_id=to, device_id_type=MESH)

    mine = [pltpu.make_async_copy(x_refs[a], out_refs[a].at[4 * x + 2 * y + c], local_sems.at[a]) for a in range(n)]
    first = []
    for a in range(n):
        first.append(copy(a, 0, me, sibling, src=x_refs[a]))
        first += [copy(a, 1 + j, me, (*chip, c), src=x_refs[a]) for j, chip in enumerate(chips)]
    return copy, mine, first, me, sibling, chips, c, n


def _gather_start(x_refs, out_refs, sems):
    _, mine, first, *_ = _gather_parts(x_refs, out_refs, sems)
    for cp in mine + first:
        cp.start()


def _gather_finish(x_refs, out_refs, sems):
    copy, mine, first, me, sibling, chips, c, n = _gather_parts(x_refs, out_refs, sems)
    passed = []
    for j, chip in enumerate(chips):
        for a in range(n):
            copy(a, 1 + j, (*chip, c), me).wait_recv()
            fwd = copy(a, 4 + j, (*chip, c), sibling)
            fwd.start()
            passed.append(fwd)
    for a in range(n):
        copy(a, 0, sibling, me).wait_recv()
    for j, chip in enumerate(chips):
        for a in range(n):
            copy(a, 4 + j, (*chip, 1 - c), me).wait_recv()
    for cp in first + passed:
        cp.wait_send()
    for cp in mine:
        cp.wait()


def _exchange_sibling(slots, *, name):
    n = len(slots)

    def body(*refs):
        x_refs, sib_refs = refs[:n], refs[n:2 * n]
        send_sems, recv_sems = refs[2 * n:]
        x, y, c = _peers()
        give = [pltpu.make_async_remote_copy(
            src_ref=x_refs[a].at[pl.ds(N_CHIP * (1 - c), N_CHIP)], dst_ref=sib_refs[a], send_sem=send_sems.at[a],
            recv_sem=recv_sems.at[a], device_id=(x, y, 1 - c), device_id_type=MESH) for a in range(n)]
        for cp in give:
            cp.start()
        for cp in give:
            cp.wait_recv()
        for cp in give:
            cp.wait_send()

    return list(pl.pallas_call(
        body, name=name, out_shape=[_S((N_CHIP,) + s.shape[1:], s.dtype) for s in slots], in_specs=[_ANY] * n, out_specs=[_ANY] * n,
        scratch_shapes=[pltpu.SemaphoreType.DMA((n,)), pltpu.SemaphoreType.DMA((n,))],
    )(*slots))


def _chip_sum(slots, sib, core, *, name):
    _, R, W = slots.shape
    tr = _pick(R, max(16, EW_BLOCK_ELEMS // W), align=16)

    def body(core_ref, x_ref, s_ref, o_ref):
        o_ref[...] = (x_ref[...].astype(f32) + s_ref[...].astype(f32)).astype(o_ref.dtype)

    blk = pl.BlockSpec((None, tr, W), lambda t, i, core_ref: (t, i, 0))
    return pl.pallas_call(
        body, name=name, out_shape=_S(sib.shape, slots.dtype),
        grid_spec=pltpu.PrefetchScalarGridSpec(
            num_scalar_prefetch=1, grid=(N_CHIP, R // tr),
            in_specs=[pl.BlockSpec((None, tr, W), lambda t, i, core_ref: (N_CHIP * core_ref[0] + t, i, 0)), blk], out_specs=blk),
        compiler_params=_cparams(("parallel", "parallel")),
    )(core, slots, sib)


def _chip_out_shapes(parts):
    return [_S((N_CHIP - 1,) + p.shape[1:], p.dtype) for p in parts]


def _chip_sems(n):
    return [pltpu.SemaphoreType.DMA((3 * n,)), pltpu.SemaphoreType.DMA((3 * n,))]


def _chip_copies(p_refs, out_refs, send_sems, recv_sems):
    x, y, c = _peers()
    copies = []
    for j in range(1, N_CHIP):
        tx, ty = x ^ (j >> 1), y ^ (j & 1)
        for a in range(len(p_refs)):
            copies.append(pltpu.make_async_remote_copy(
                src_ref=p_refs[a].at[2 * tx + ty], dst_ref=out_refs[a].at[j - 1], send_sem=send_sems.at[3 * a + j - 1],
                recv_sem=recv_sems.at[3 * a + j - 1], device_id=(tx, ty, c), device_id_type=MESH))
    return copies


def _start_all(copies):
    for cp in copies:
        cp.start()


def _wait_all(copies):
    for cp in copies:
        cp.wait_recv()
    for cp in copies:
        cp.wait_send()


class _Carry(NamedTuple):
    ins: tuple = ()
    out_shapes: tuple = ()
    sems: tuple = ()
    start: Callable = None
    finish: Callable = None


_NO_CARRY = _Carry()


def _carry_chips(parts):
    return _Carry(tuple(parts), tuple(_chip_out_shapes(parts)), tuple(_chip_sems(len(parts))),
                  lambda i, o, s: _start_all(_chip_copies(i, o, *s)), lambda i, o, s: _wait_all(_chip_copies(i, o, *s)))


def _carry_gather(shards):
    n = len(shards)
    sems = (pltpu.SemaphoreType.DMA((7 * n,)), pltpu.SemaphoreType.DMA((7 * n,)), pltpu.SemaphoreType.DMA((n,)))
    return _Carry(tuple(shards), tuple(_S((N_DEV,) + s.shape, s.dtype) for s in shards), sems, _gather_start, _gather_finish)


def _carry_split(carry, refs, n_in, n_out):
    ci, co, cs = len(carry.ins), len(carry.out_shapes), len(carry.sems)
    refs = list(refs)
    own_in, c_in = refs[:n_in], refs[n_in:n_in + ci]
    own_out, c_out = refs[n_in + ci:n_in + ci + n_out], refs[n_in + ci + n_out:n_in + ci + n_out + co]
    rest = refs[n_in + ci + n_out + co:]
    own_scratch, c_sems = rest[:len(rest) - cs], rest[len(rest) - cs:]
    return own_in + own_out + own_scratch, c_in, c_out, c_sems


def _carry_start(carry, c_in, c_out, c_sems, first):
    if carry.ins:
        @pl.when(first)
        def _():
            carry.start(c_in, c_out, c_sems)


def _carry_finish(carry, c_in, c_out, c_sems, last):
    if carry.ins:
        @pl.when(last)
        def _():
            carry.finish(c_in, c_out, c_sems)


def _sum_slots(stack, *, name):
    n, R, W = stack.shape
    tr = _pick(R, 1024, align=8)

    def body(s_ref, o_ref):
        acc = s_ref[0]
        for k in range(1, n):
            acc = acc + s_ref[k]
        o_ref[...] = acc

    return pl.pallas_call(
        body, name=name, grid=(R // tr,), in_specs=[pl.BlockSpec((n, tr, W), lambda i: (0, i, 0))],
        out_specs=pl.BlockSpec((tr, W), lambda i: (i, 0)), out_shape=_S((R, W), f32), compiler_params=_cparams(("parallel",)),
    )(stack)


def _adamw_math(gv, wv, mv, vv):
    c1 = 1.0 / (1.0 - ADAM_B1 ** ADAM_STEP)
    c2 = 1.0 / (1.0 - ADAM_B2 ** ADAM_STEP)
    nm = ADAM_B1 * mv + (1.0 - ADAM_B1) * gv
    nv = ADAM_B2 * vv + (1.0 - ADAM_B2) * jnp.square(gv)
    return -ADAM_LR * ((nm * c1) / (jnp.sqrt(nv * c2) + ADAM_EPS) + ADAM_WD * wv), nm, nv


def _adamw(g, w, m, v, *, name):
    R, W = w.shape
    tr = _pick(R, max(8, (1 << 20) // (4 * W)), align=8)

    def body(g_ref, w_ref, m_ref, v_ref, d_ref, nm_ref, nv_ref):
        d_ref[...], nm_ref[...], nv_ref[...] = _adamw_math(g_ref[...], w_ref[...], m_ref[...], v_ref[...])

    sp = pl.BlockSpec((tr, W), lambda i: (i, 0))
    return pl.pallas_call(
        body, name=name, grid=(R // tr,), in_specs=[sp] * 4, out_specs=[sp] * 3, out_shape=[_S((R, W), f32)] * 3,
        compiler_params=_cparams(("parallel",)),
    )(g, w, m, v)


def _reduce_adamw(own, arrived, chip, w, m, v, *, name):
    n, R, W = arrived.shape
    tr = _pick(R, max(16, EW_BLOCK_ELEMS // (4 * W)), align=16)

    def body(chip_ref, o_ref, p_ref, w_ref, m_ref, v_ref, g_ref, d_ref, nm_ref, nv_ref):
        gv = o_ref[...].astype(f32)
        for k in range(n):
            gv = gv + p_ref[k].astype(f32)
        g_ref[...] = gv
        d_ref[...], nm_ref[...], nv_ref[...] = _adamw_math(gv, w_ref[...], m_ref[...], v_ref[...])

    sp = pl.BlockSpec((tr, W), lambda i, chip_ref: (i, 0))
    return pl.pallas_call(
        body, name=name, out_shape=[_S((R, W), f32)] * 4,
        grid_spec=pltpu.PrefetchScalarGridSpec(
            num_scalar_prefetch=1, grid=(R // tr,),
            in_specs=[pl.BlockSpec((None, tr, W), lambda i, chip_ref: (chip_ref[0], i, 0)),
                      pl.BlockSpec((n, tr, W), lambda i, chip_ref: (0, i, 0))] + [sp] * 3, out_specs=[sp] * 4),
        compiler_params=_cparams(("parallel",)),
    )(chip, own, arrived, w, m, v)


def _pieces(seg_start, seg_len, shard_w):
    out, col = [], seg_start
    while col < seg_start + seg_len:
        k, a = divmod(col, shard_w)
        n = min(shard_w - a, seg_start + seg_len - col)
        out.append((k, a, col - seg_start, n))
        col += n
    return out


def _unshard_w_in(g, seg_lens, *, name):
    _, D, w = g.shape
    starts = [sum(seg_lens[:i]) for i in range(len(seg_lens))]
    widths = [max(n, LANES) for n in seg_lens]
    tm = _pick(D, 256, align=16)

    def body(g_ref, *o_refs):
        for o_ref, s0, n in zip(o_refs, starts, seg_lens):
            if n < o_ref.shape[1]:
                o_ref[...] = jnp.zeros_like(o_ref)
            for k, a, off, m in _pieces(s0, n, w):
                o_ref[:, off:off + m] = g_ref[k, :, a:a + m]

    return pl.pallas_call(
        body, name=name, grid=(D // tm,), in_specs=[pl.BlockSpec((N_DEV, tm, w), lambda i: (0, i, 0))],
        out_specs=[pl.BlockSpec((tm, wd), lambda i: (i, 0)) for wd in widths], out_shape=[_S((D, wd), g.dtype) for wd in widths],
        compiler_params=_cparams(("parallel",)),
    )(g)


def _unshard_pair(g1, g2, *, name):
    _, D, w = g1.shape
    tm = _pick(D, 256, align=16)

    def body(a_ref, b_ref, o_ref):
        for i, g_ref in enumerate((a_ref, b_ref)):
            for k in range(N_DEV):
                off = (i * N_DEV + k) * w
                o_ref[:, off:off + w] = g_ref[k]

    blk = pl.BlockSpec((N_DEV, tm, w), lambda i: (0, i, 0))
    return pl.pallas_call(
        body, name=name, grid=(D // tm,), in_specs=[blk, blk], out_specs=pl.BlockSpec((tm, 2 * N_DEV * w), lambda i: (i, 0)),
        out_shape=_S((D, 2 * N_DEV * w), g1.dtype), compiler_params=_cparams(("parallel",)),
    )(g1, g2)


def _reshard_pair(dw, *, name):
    D, w = dw.shape[0], dw.shape[1] // (2 * N_DEV)
    tm = _pick(D, 128, align=16)

    def body(g_ref, a_ref, b_ref):
        for i, o_ref in enumerate((a_ref, b_ref)):
            for k in range(N_DEV):
                off = (i * N_DEV + k) * w
                o_ref[_slot_of(k)] = g_ref[:, off:off + w].astype(o_ref.dtype)

    blk = pl.BlockSpec((N_DEV, tm, w), lambda i: (0, i, 0))
    return pl.pallas_call(
        body, name=name, grid=(D // tm,), in_specs=[pl.BlockSpec((tm, dw.shape[1]), lambda i: (i, 0))], out_specs=[blk, blk],
        out_shape=[_S((N_DEV, D, w), bf16)] * 2, compiler_params=_cparams(("parallel",)),
    )(dw)


def _reshard_w_in(grads, seg_lens, w, *, name):
    D = grads[0].shape[0]
    starts = [sum(seg_lens[:i]) for i in range(len(seg_lens))]
    tm = _pick(D, 128, align=16)

    def body(*refs):
        o_ref = refs[-1]
        for g_ref, s0, n in zip(refs[:-1], starts, seg_lens):
            for k, a, off, m in _pieces(s0, n, w):
                o_ref[_slot_of(k), :, a:a + m] = g_ref[:, off:off + m].astype(o_ref.dtype)

    return pl.pallas_call(
        body, name=name, grid=(D // tm,), in_specs=[pl.BlockSpec((tm, g.shape[1]), lambda i: (i, 0)) for g in grads],
        out_specs=pl.BlockSpec((N_DEV, tm, w), lambda i: (0, i, 0)), out_shape=_S((N_DEV, D, w), bf16),
        compiler_params=_cparams(("parallel",)),
    )(*grads)


def _pad_flat(a, mult):
    a = a.reshape(-1)
    n = -(-a.shape[0] // mult) * mult
    return a if n == a.shape[0] else jnp.pad(a, (0, n - a.shape[0]))


def _pad_cols(a, mult):
    n = -(-a.shape[1] // mult) * mult
    return a if n == a.shape[1] else jnp.pad(a, ((0, 0), (0, n - a.shape[1])))


def _block_diag(t):
    nblk, g, P, Q = t.shape
    eye = jnp.eye(g, dtype=t.dtype)
    return (t[:, :, :, None, :] * eye[None, :, None, :, None]).reshape(nblk, g * P, g * Q)


def _block_diag_t(w, P, Q):
    nblk = w.shape[0]
    g = w.shape[1] // P
    eye = jnp.eye(g, dtype=w.dtype)
    return (w.reshape(nblk, g, P, g, Q) * eye[None, :, None, :, None]).sum(axis=3)


_COLS = ("ffn1_w_gate", "ffn1_w_up", "ffn2_w_gate", "ffn2_w_up")
_ROWS = ("ffn1_w_down", "ffn2_w_down", "s5_w_glu", "w_proj_s5", "w_out", "w_proj_ssd")
_BIG = _COLS + _ROWS + ("w_in", "conv_w")
_SMALL = ("ffn1_norm", "mix_norm", "conv_b", "s5_A_re", "s5_A_im", "s5_log_dt", "s5_B_re", "s5_B_im", "s5_C_re", "s5_C_im",
          "s5_D", "s5_b_glu", "ssd_A_log", "ssd_dt_bias", "ssd_D", "ssd_norm", "b_gate", "ffn2_norm", "final_norm")
_WEIGHTS = ("ffn1_norm", "ffn1_w_gate", "ffn1_w_up", "ffn1_w_down", "mix_norm", "w_in", "conv_w", "conv_b", "s5_A_re", "s5_A_im",
            "s5_log_dt", "s5_B_re", "s5_B_im", "s5_C_re", "s5_C_im", "s5_D", "s5_w_glu", "s5_b_glu", "ssd_A_log", "ssd_dt_bias",
            "ssd_D", "ssd_norm", "w_proj_s5", "w_proj_ssd", "b_gate", "w_out", "ffn2_norm", "ffn2_w_gate", "ffn2_w_up",
            "ffn2_w_down", "final_norm")
def _with_carry(res, carry):
    return res if carry else (res, [])


def _ffn_fwd(x, n, wgu, wd_of, tag, carries=(None, None, None)):
    D = x.shape[1]
    h = _rows(_f_rmsnorm, [x], [n], [(D, bf16)], name=tag + "_norm")[0]
    ab, got0 = _with_carry(_mm(h, wgu, carry=carries[0], name=tag + "_gate_up"), carries[0])
    wd = wd_of(got0)
    F = wd.shape[0]
    (c,), got1 = _with_carry(_rows(_f_swiglu, [ab], [], [(F, bf16)], carry=carries[1], name=tag + "_act"), carries[1])
    y, got2 = _with_carry(_mm(c, wd, scale=0.5, add=x, carry=carries[2], name=tag + "_down"), carries[2])
    return y, (x, n, h, ab, c), (got0, got1, got2)


def _ffn_bwd(saved, wgu, wd, dy, tag, carry_after_dwd=None, carry_after_dwgu=None):
    x, n, h, ab, c = saved
    F = wd.shape[0]
    def act_bwd(dc, ab_):
        return jax.vjp(lambda t: _f_swiglu(t)[0], ab_)[1](dc)[0]

    dab = _mm(dy, wd, tb=True, scale=0.5, tm=EPI_TM, epi=(act_bwd, ab, 2 * F), out_dtype=bf16, name=tag + "_d_act")
    dwd = _mm(c, dy, ta=True, o_blk="m", o_slots=True, tm=F // 2, out_dtype=bf16, scale=0.5, name=tag + "_d_wdown")
    carry_w = carry_after_dwd(dwd) if carry_after_dwd else None
    dwgu, arr_w = _with_carry(_mm(h, dab, ta=True, carry=carry_w, name=tag + "_d_wgu"), carry_w)
    dwg, dwu = _reshard_pair(dwgu, name=tag + "_reshard_d_wgu")
    carry_h = carry_after_dwgu(dwg, dwu) if carry_after_dwgu else None
    dh, arr_h = _with_carry(_mm(dab, wgu, tb=True, carry=carry_h, name=tag + "_d_h"), carry_h)
    (dx,), (dn,) = _rows_bwd(_f_rmsnorm, [x], [n], [dh], name=tag + "_norm_bwd", want_rows=[0], adds={0: dy})
    return dx, dn, dwg, dwu, dwd, arr_w, arr_h


def kernel(x, ffn1_norm, ffn1_w_gate, ffn1_w_up, ffn1_w_down, mix_norm, w_in, conv_w, conv_b, s5_A_re, s5_A_im, s5_log_dt, s5_B_re, s5_B_im, s5_C_re, s5_C_im, s5_D, s5_w_glu, s5_b_glu, ssd_A_log, ssd_dt_bias, ssd_D, ssd_norm, w_proj_s5, w_proj_ssd, b_gate, w_out, ffn2_norm, ffn2_w_gate, ffn2_w_up, ffn2_w_down, final_norm, loss_target, m_ffn1_norm, m_ffn1_w_gate, m_ffn1_w_up, m_ffn1_w_down, m_mix_norm, m_w_in, m_conv_w, m_conv_b, m_s5_A_re, m_s5_A_im, m_s5_log_dt, m_s5_B_re, m_s5_B_im, m_s5_C_re, m_s5_C_im, m_s5_D, m_s5_w_glu, m_s5_b_glu, m_ssd_A_log, m_ssd_dt_bias, m_ssd_D, m_ssd_norm, m_w_proj_s5, m_w_proj_ssd, m_b_gate, m_w_out, m_ffn2_norm, m_ffn2_w_gate, m_ffn2_w_up, m_ffn2_w_down, m_final_norm, v_ffn1_norm, v_ffn1_w_gate, v_ffn1_w_up, v_ffn1_w_down, v_mix_norm, v_w_in, v_conv_w, v_conv_b, v_s5_A_re, v_s5_A_im, v_s5_log_dt, v_s5_B_re, v_s5_B_im, v_s5_C_re, v_s5_C_im, v_s5_D, v_s5_w_glu, v_s5_b_glu, v_ssd_A_log, v_ssd_dt_bias, v_ssd_D, v_ssd_norm, v_w_proj_s5, v_w_proj_ssd, v_b_gate, v_w_out, v_ffn2_norm, v_ffn2_w_gate, v_ffn2_w_up, v_ffn2_w_down, v_final_norm):
    P = dict(locals())
    T, D = x.shape[1], x.shape[2]
    x0, tgt = x[0], loss_target[0]
    sh = {k: P[k][0] for k in _BIG}

    send = {k: (sh[k] if k == "conv_w" else sh[k].astype(bf16)) for k in _BIG}
    W = {}

    def gather_in(keys):
        return _carry_gather([send[k] for k in keys])

    first_keys = ("ffn1_w_gate", "ffn1_w_up", "conv_w")
    W.update(zip(first_keys, _all_gather([send[k] for k in first_keys], name="gather_weights_first")))
    whole = lambda k: W[k].reshape(-1, D)
    conv_w_full = W["conv_w"].transpose(1, 0, 2).reshape(CONV_K, -1)

    d_inner = N_DEV * sh["w_proj_ssd"].shape[0]
    conv_dim = conv_w_full.shape[1]
    H = ssd_A_log.shape[1]
    G = (conv_dim - d_inner) // (2 * SSD_STATE)
    hpg = H // G
    nc = T // CHUNK
    Gs = D // S5_GROUP
    nblk = Gs // S5_GPB
    NS = Gs * S5_STATE
    seg_lens = (D, d_inner, conv_dim, H, 2 * D)

    cuts = [0, D // 3 // 16 * 16, D // 3 // 16 * 16 + 3 * D // 8 // 16 * 16, D]
    win_rows = [send["w_in"][a_:b_] for a_, b_ in zip(cuts[:-1], cuts[1:])]
    wgu1 = _unshard_pair(W["ffn1_w_gate"], W["ffn1_w_up"], name="unshard_ffn1_gate_up")

    def ffn1_down(got):
        W["ffn1_w_down"] = got[0]
        return whole("ffn1_w_down")

    x1, sv1, (got0, got1, got2) = _ffn_fwd(
        x0, ffn1_norm, wgu1, ffn1_down, "ffn1",
        carries=(_carry_gather([send["ffn1_w_down"], win_rows[0]]), _carry_gather([win_rows[1]]), _carry_gather([win_rows[2]])))
    ffn1_w = (wgu1, whole("ffn1_w_down"))
    W["w_in"] = jnp.concatenate([got0[1], got1[0], got2[0]], axis=1)
    w_u, w_z, w_xbc, w_dt, w_gl = _unshard_w_in(W["w_in"], seg_lens, name="unshard_w_in")
    h2 = _rows(_f_rmsnorm, [x1], [mix_norm], [(D, bf16)], name="mix_norm")[0]
    u_p = _mm(h2, w_u, o_seg=True, name="in_u")
    z = _mm(h2, w_z, name="in_z")
    xbc = _mm(h2, w_xbc, name="in_xbc")
    gl = _mm(h2, w_gl, name="in_gate")
    dtr = _mm(h2, w_dt, name="in_dt")

    rep = lambda a: jnp.repeat(a, S5_GROUP, axis=0)
    lr, li, ldt = s5_A_re[0], s5_A_im[0], s5_log_dt[0].reshape(Gs, 1)
    brt = s5_B_re[0].transpose(0, 2, 1).reshape(Gs * S5_GROUP, S5_STATE)
    bit = s5_B_im[0].transpose(0, 2, 1).reshape(Gs * S5_GROUP, S5_STATE)
    prep_args = (lr, li, ldt, rep(lr), rep(li), rep(ldt), brt, bit)
    ar, ai, bbrt, bbit = _s5_prep(prep_args, name="s5_prep")
    a_r, a_i = ar.reshape(1, NS), ai.reshape(1, NS)
    wb_r = _block_diag(bbrt.reshape(nblk, S5_GPB, S5_GROUP, S5_STATE)).astype(bf16)
    wb_i = _block_diag(bbit.reshape(nblk, S5_GPB, S5_GROUP, S5_STATE)).astype(bf16)
    c4r = s5_C_re[0].reshape(nblk, S5_GPB, S5_GROUP, S5_STATE).transpose(0, 1, 3, 2)
    c4i = s5_C_im[0].reshape(nblk, S5_GPB, S5_GROUP, S5_STATE).transpose(0, 1, 3, 2)
    wc_r, wc_i = _block_diag(c4r).astype(bf16), _block_diag(c4i).astype(bf16)
    mix_keys = ("s5_w_glu", "w_proj_s5", "w_proj_ssd", "w_out")
    sl_r, sl_i, p_r, p_i, got = _s5_local_scan(u_p, wb_r, wb_i, a_r, a_i, reverse=False, carry=gather_in(mix_keys), name="s5_scan")
    W.update(zip(mix_keys, got))
    w_glu, w_p5, w_pssd, w_o = whole("s5_w_glu"), whole("w_proj_s5"), whole("w_proj_ssd"), whole("w_out")
    c_r, c_i = _s5_carry(sl_r[T - NSEG:], sl_i[T - NSEG:], p_r, p_i, reverse=False, name="s5_carry")
    s_r, s_i, ylin = _s5_fix_out(sl_r, sl_i, a_r, a_i, c_r, c_i, wc_r, wc_i, name="s5_fix_out")
    g5 = _rows(_f_s5_post, [ylin, u_p], [s5_D], [(D, f32)], name="s5_gelu")[0]
    v5 = _mm(g5, w_glu, name="s5_glu_mm")
    o5 = _rows(_f_glu, [g5, v5], [s5_b_glu], [(D, bf16)], name="s5_glu")[0]
    p5 = _mm(o5, w_p5, a_seg=True, name="proj_s5")

    xc = _conv_fwd(xbc, conv_w_full, conv_b, name="conv")
    bias_p, alog_p = _pad_cols(ssd_dt_bias, LANES), _pad_cols(ssd_A_log, LANES)
    expand = (lax.broadcasted_iota(jnp.int32, (LANES, d_inner), 1) // HEADDIM
              == lax.broadcasted_iota(jnp.int32, (LANES, d_inner), 0)).astype(f32)
    dt_p, da_p, dt4, a4 = _rows(_f_dt_expand, [dtr], [bias_p, alog_p, expand],
                                [(LANES, f32), (LANES, f32), (d_inner, f32), (d_inner, f32)], name="ssd_dt")
    row_l = lambda a: a[:, :H].reshape(nc, CHUNK, G, hpg).transpose(2, 0, 3, 1).reshape(G, nc, hpg * CHUNK)
    ssd_in = (xc, dt4, a4, row_l(dt_p), row_l(da_p), jnp.repeat(ssd_D, HEADDIM, axis=1), _ssd_consts(hpg))
    ffn2_keys = ("ffn2_w_gate", "ffn2_w_up", "ffn2_w_down")
    y_ssd, hs, got = _ssd2_fwd(*ssd_in, d_inner=d_inner, carry=gather_in(ffn2_keys), name="ssd")
    W.update(zip(ffn2_keys, got))
    ffn2_w = (_unshard_pair(W["ffn2_w_gate"], W["ffn2_w_up"], name="unshard_ffn2_gate_up"), whole("ffn2_w_down"))
    yn = _rows(_f_gated_norm, [y_ssd, z], [ssd_norm], [(d_inner, bf16)], name="ssd_gated_norm")[0]
    pssd = _mm(yn, w_pssd, name="proj_ssd")

    merged = _rows(_f_merge, [gl, p5, pssd], [b_gate], [(D, bf16)], name="merge")[0]
    x2 = _mm(merged, w_o, add=x1, name="out_proj")
    x3, sv2, _ = _ffn_fwd(x2, ffn2_norm, ffn2_w[0], lambda _: ffn2_w[1], "ffn2")
    lossv, dx3, d_final = _loss_stage(x3, tgt, final_norm.reshape(1, D), name="loss")

    gw = {}
    gs = {"final_norm": d_final}
    slot_mm = lambda a_, b_, name, **kw: _mm(a_, b_, ta=True, o_blk="m", o_slots=True, out_dtype=bf16, name=name, **kw)
    core = lax.axis_index("c").astype(jnp.int32).reshape(1)
    chip = (2 * lax.axis_index("x") + lax.axis_index("y")).astype(jnp.int32).reshape(1)
    chip_sums, arrived = {}, {}

    def level1(keys, tag):
        sib = _exchange_sibling([gw[k] for k in keys], name="exchange_sibling_" + tag)
        for k, s_ in zip(keys, sib):
            chip_sums[k] = _chip_sum(gw[k], s_, core, name="chip_sum_" + k)
        return [chip_sums[k] for k in keys]

    dx2, gs["ffn2_norm"], gw["ffn2_w_gate"], gw["ffn2_w_up"], gw["ffn2_w_down"], _, _ = _ffn_bwd(sv2, *ffn2_w, dx3, "ffn2")

    dmerged = _mm(dx2, w_o, tb=True, name="d_merged")
    gw["w_out"] = slot_mm(merged, dx2, "d_w_out")
    (dgl, dp5, dpssd), (gs["b_gate"],) = _rows_bwd(_f_merge, [gl, p5, pssd], [b_gate], [dmerged], name="merge_bwd", want_rows=[0, 1, 2])

    dyn = _mm(dpssd, w_pssd, tb=True, name="d_yn")
    gw["w_proj_ssd"] = slot_mm(yn, dpssd, "d_w_proj_ssd")
    group_a = ("ffn2_w_gate", "ffn2_w_up", "ffn2_w_down", "w_out", "w_proj_ssd")
    parts_a = level1(group_a, "a")
    (dyssd, dz), (gs["ssd_norm"],) = _rows_bwd(_f_gated_norm, [y_ssd, z], [ssd_norm], [dyn], name="ssd_gated_norm_bwd", want_rows=[0, 1])
    dxs, dbm, dcm, ddtc, ddac, ddtw, ddaw, ddh = _ssd2_bwd(*ssd_in, hs, dyssd, d_inner=d_inner, name="ssd_bwd")

    def fold(col, row):
        col = col.reshape(T, G, LANES)[:, :, :hpg].reshape(T, H)
        row = row.reshape(G, nc, hpg, CHUNK).transpose(1, 3, 0, 2).reshape(T, H)
        return _pad_cols(col + row, LANES)

    (ddtr,), (dbias_p, dalog_p) = _rows_bwd(_f_dt, [dtr], [bias_p, alog_p], [fold(ddtc, ddtw), fold(ddac, ddaw)], name="ssd_dt_bwd", want_rows=[0])
    gs["ssd_dt_bias"], gs["ssd_A_log"], gs["ssd_D"] = dbias_p[:, :H], dalog_p[:, :H], ddh[:, 0, :hpg].reshape(1, H)
    dxbc, d_conv_w, gs["conv_b"], arr = _conv_bwd(
        xbc, conv_w_full, conv_b, [dxs, dbm, dcm], carry=_carry_chips(parts_a), name="conv_bwd")
    arrived.update(zip(group_a, arr))
    cwk = sh["conv_w"].shape[1]
    gw["conv_w"] = d_conv_w.reshape(CONV_K, N_CHIP, 2, cwk).transpose(2, 1, 0, 3).reshape(N_DEV, CONV_K, cwk)

    do5 = _mm(dp5, w_p5, tb=True, o_seg=True, name="d_o5")
    gw["w_proj_s5"] = slot_mm(o5, dp5, "d_w_proj_s5", a_seg=True)
    (dg5a, dv5), (gs["s5_b_glu"],) = _rows_bwd(_f_glu, [g5, v5], [s5_b_glu], [do5], name="s5_glu_bwd", want_rows=[0, 1])
    dg5 = _mm(dv5, w_glu, tb=True, add=dg5a, name="d_g5")
    gw["s5_w_glu"] = slot_mm(g5, dv5, "d_w_glu")
    (dylin, du_a), (gs["s5_D"],) = _rows_bwd(_f_s5_post, [ylin, u_p], [s5_D], [dg5], name="s5_gelu_bwd", want_rows=[0, 1])
    wct_r, wct_i = wc_r.transpose(0, 2, 1), -wc_i.transpose(0, 2, 1)
    ql_r, ql_i, _, _ = _s5_local_scan(dylin, wct_r, wct_i, a_r, -a_i, reverse=True, powers=False, name="s5_scan_bwd")
    cb_r, cb_i = _s5_carry(ql_r[:NSEG], ql_i[:NSEG], p_r, -p_i, reverse=True, name="s5_carry_bwd")
    tc = min(S5_TC, T)

    def before_blocks(s):
        last = s.reshape(T // tc, tc, NS)[:, tc - NSEG:, :]
        wrap = jnp.concatenate([jnp.zeros((1, 1, NS), f32), last[-1:, : NSEG - 1, :]], axis=1)
        return jnp.concatenate([wrap, last[:-1]], axis=0)

    du_p, dwb_r, dwb_i, dwc_r, dwc_i, d_ar, d_ai = _s5_fix_bwd(
        ql_r, ql_i, a_r, -a_i, cb_r, cb_i, s_r, s_i, before_blocks(s_r), before_blocks(s_i), u_p, dylin, du_a, wb_r, wb_i, name="s5_fix_bwd")
    unblk = lambda w: _block_diag_t(w, S5_GROUP, S5_STATE).reshape(Gs * S5_GROUP, S5_STATE)
    rsum = jnp.repeat(jnp.eye(Gs, dtype=f32), S5_GROUP, axis=1)
    d_lr, d_li, d_ldt, d_brt, d_bit = _s5_prep_bwd(
        prep_args, (d_ar.reshape(Gs, S5_STATE), d_ai.reshape(Gs, S5_STATE), unblk(dwb_r), unblk(dwb_i)), rsum, name="s5_prep_bwd")
    gs["s5_A_re"], gs["s5_A_im"], gs["s5_log_dt"] = d_lr, d_li, d_ldt.reshape(1, Gs)
    gs["s5_B_re"] = d_brt.reshape(Gs, S5_GROUP, S5_STATE).transpose(0, 2, 1)
    gs["s5_B_im"] = d_bit.reshape(Gs, S5_GROUP, S5_STATE).transpose(0, 2, 1)
    gs["s5_C_re"] = _block_diag_t(dwc_r, S5_STATE, S5_GROUP).transpose(0, 1, 3, 2).reshape(Gs, S5_GROUP, S5_STATE)
    gs["s5_C_im"] = _block_diag_t(dwc_i, S5_STATE, S5_GROUP).transpose(0, 1, 3, 2).reshape(Gs, S5_GROUP, S5_STATE)

    d_w_in = [_mm(h2, du_p, ta=True, b_seg=True, name="d_w_u"), _mm(h2, dz, ta=True, name="d_w_z"), _mm(h2, dxbc, ta=True, name="d_w_xbc"),
              _mm(h2, ddtr, ta=True, name="d_w_dt"), _mm(h2, dgl, ta=True, name="d_w_gate")]
    gw["w_in"] = _reshard_w_in(d_w_in, seg_lens, sh["w_in"].shape[1], name="reshard_d_w_in")
    group_c = ("w_proj_s5", "s5_w_glu", "conv_w")
    parts_c = level1(group_c + ("w_in",), "c")
    c1 = int(D * 0.45) // 16 * 16
    c2 = c1 + D // 4 // 16 * 16
    win = [parts_c[3][:, :c1], parts_c[3][:, c1:c2], parts_c[3][:, c2:]]
    dh2 = _mm(du_p, w_u, tb=True, a_seg=True, name="d_h2_u")
    dh2, arr = _mm(dz, w_z, tb=True, add=dh2, carry=_carry_chips(parts_c[:3]), name="d_h2_z")
    arrived.update(zip(group_c, arr))
    dh2, (arr0,) = _mm(dxbc, w_xbc, tb=True, add=dh2, carry=_carry_chips([win[0]]), name="d_h2_xbc")
    dh2, (arr1,) = _mm(dgl, w_gl, tb=True, add=dh2, carry=_carry_chips([win[1]]), name="d_h2_gate")
    dh2 = _mm(ddtr, w_dt, tb=True, add=dh2, name="d_h2_dt")
    (dx1,), (gs["mix_norm"],) = _rows_bwd(_f_rmsnorm, [x1], [mix_norm], [dh2], name="mix_norm_bwd", want_rows=[0], adds={0: dx2})

    def carry_ffn1_down(dwd):
        gw["ffn1_w_down"] = dwd
        return _carry_chips(level1(("ffn1_w_down",), "d") + [win[2]])

    def carry_ffn1_gate_up(dwg, dwu):
        gw["ffn1_w_gate"], gw["ffn1_w_up"] = dwg, dwu
        return _carry_chips(level1(("ffn1_w_gate", "ffn1_w_up"), "e"))

    dx0, gs["ffn1_norm"], _, _, _, arr_w, arr_h = _ffn_bwd(
        sv1, *ffn1_w, dx1, "ffn1", carry_after_dwd=carry_ffn1_down, carry_after_dwgu=carry_ffn1_gate_up)
    arrived["ffn1_w_down"], arr2 = arr_w
    arrived["ffn1_w_gate"], arrived["ffn1_w_up"] = arr_h
    arrived["w_in"] = jnp.concatenate([arr0, arr1, arr2], axis=1)

    small_shapes = {k: (P[k][0].shape if P[k].ndim > 1 else P[k].shape) for k in _SMALL}
    pack = lambda d: jnp.concatenate([_pad_flat(d[k], TILE_ELEMS) for k in _SMALL]).reshape(-1, LANES)
    gsmall = _sum_slots(_all_gather([pack(gs)], name="gather_small_grads")[0], name="sum_small_grads")
    snum = {k: math.prod(small_shapes[k]) for k in _SMALL}
    ssz = {k: -(-snum[k] // TILE_ELEMS) * TILE_ELEMS for k in _SMALL}

    grads, delta, new_m, new_v = {}, {}, {}, {}
    for k in _BIG:
        grads[k], delta[k], new_m[k], new_v[k] = _reduce_adamw(
            chip_sums[k], arrived[k], chip, P[k][0], P["m_" + k][0], P["v_" + k][0], name="adamw_" + k)
    d_s, m_s, v_s = _adamw(gsmall, pack({k: P[k] for k in _SMALL}), pack({k: P["m_" + k] for k in _SMALL}),
                           pack({k: P["v_" + k] for k in _SMALL}), name="adamw_small")
    off = 0
    gflat, dflat, mflat, vflat = gsmall.reshape(-1), d_s.reshape(-1), m_s.reshape(-1), v_s.reshape(-1)
    for k in _SMALL:
        n = snum[k]
        grads[k], delta[k], new_m[k], new_v[k] = (a[off:off + n] for a in (gflat, dflat, mflat, vflat))
        off += ssz[k]

    loss = lax.psum(lossv[0, 0], ("x", "y", "c"))
    out = [loss, dx0.reshape(x.shape)]
    for d in (grads, delta, new_m, new_v):
        out += [d[k].reshape(P[k].shape) for k in _WEIGHTS]
    return tuple(out)
```

```python
import math
from typing import Callable, NamedTuple

import jax
import jax.numpy as jnp
from jax import lax
from jax.experimental import pallas as pl
from jax.experimental.pallas import tpu as pltpu

f32 = jnp.float32
bf16 = jnp.bfloat16
_S = jax.ShapeDtypeStruct

EPS = 1e-6
S5_GROUP = 16
S5_STATE = 64
HEADDIM = 64
SSD_STATE = 128
CHUNK = 64
CONV_K = 4
NSEG = 8
S5_GPB = 16
N_DEV = 8
LANES = 128
TILE_ELEMS = 8 * LANES

ADAM_LR = 0.001
ADAM_B1 = 0.9
ADAM_B2 = 0.999
ADAM_EPS = 1e-08
ADAM_WD = 0.01
ADAM_STEP = 10

VMEM_LIMIT = 56 * 1024 * 1024
MM_FULL_K = 3072
MM_MAX_TN = 3072
EPI_TM = 128
ROW_BLOCK_BYTES = 12 * 1024 * 1024
EW_BLOCK_ELEMS = 1 << 20
MESH = pl.DeviceIdType.MESH


def _cparams(sem=None):
    return pltpu.CompilerParams(dimension_semantics=sem, vmem_limit_bytes=VMEM_LIMIT)


def _pick(dim, pref, align=LANES):
    best = None
    t = align
    while t <= min(dim, pref):
        if dim % t == 0:
            best = t
        t += align
    return best or dim


def _slot_of(k):
    return (k & 1) * (N_DEV // 2) + (k >> 1)


def _mm(a, b, *, name, ta=False, tb=False, a_blk=None, b_blk=None, o_blk=None, o_slots=False, a_seg=False, b_seg=False,
        o_seg=False, tm=None, out_dtype=f32, scale=1.0, add=None, epi=None, carry=None):
    a2, b2 = a.shape[-2:], b.shape[-2:]
    Ma, Ka = (a2[1], a2[0]) if ta else a2
    Kb, Nb = (b2[1], b2[0]) if tb else b2
    M = Ma * (a.shape[0] if a_blk == "m" else 1)
    K = Ka * (a.shape[0] if a_blk == "k" else 1)
    N = Nb * (b.shape[0] if b_blk == "n" else 1)
    assert K == Kb * (b.shape[0] if b_blk == "k" else 1), (a.shape, b.shape, ta, tb, a_blk, b_blk)
    assert (a.ndim == 3) == (a_blk is not None) and (b.ndim == 3) == (b_blk is not None)
    tm = Ma if a_blk == "m" else (tm or _pick(M, 512))
    tn = Nb if b_blk == "n" else _pick(N, MM_MAX_TN)
    if a_blk == "k" or b_blk == "k":
        tk = Ka if a_blk == "k" else Kb
        assert tk == (Kb if b_blk == "k" else tk)
    else:
        tk = K if K <= MM_FULL_K else _pick(K, 1024 if ta else MM_FULL_K)
    if (a_seg and not ta) or o_seg:
        tm = M // NSEG
    if (a_seg and ta) or b_seg:
        tk = K // NSEG
    gm, gn, nk = M // tm, N // tn, K // tk
    assert not (add is not None and (o_seg or o_blk)) and not (o_blk and o_seg)

    if a_seg:
        assert a.ndim == 2
        a = a.reshape(a.shape[0] // NSEG, NSEG * a.shape[1])
        if ta:
            a_spec = pl.BlockSpec((tk, tm), lambda i, j, k: (0, k * (Ma // tm) + i))
        else:
            a_spec = pl.BlockSpec((tm, tk), lambda i, j, k: (0, i * (Ka // tk) + k))
    elif a.ndim == 3:
        lead = (lambda i, k: i) if a_blk == "m" else (lambda i, k: k)
        if ta:
            a_spec = pl.BlockSpec((None, tk, tm), lambda i, j, k: (lead(i, k), 0 if a_blk == "k" else k, 0 if a_blk == "m" else i))
        else:
            a_spec = pl.BlockSpec((None, tm, tk), lambda i, j, k: (lead(i, k), 0 if a_blk == "m" else i, 0 if a_blk == "k" else k))
    else:
        a_spec = pl.BlockSpec((tk, tm), lambda i, j, k: (k, i)) if ta else pl.BlockSpec((tm, tk), lambda i, j, k: (i, k))
    if b_seg:
        assert b.ndim == 2 and not tb
        b = b.reshape(b.shape[0] // NSEG, NSEG * b.shape[1])
        b_spec = pl.BlockSpec((tk, tn), lambda i, j, k: (0, k * (Nb // tn) + j))
    elif b.ndim == 3:
        lead = (lambda j, k: j) if b_blk == "n" else (lambda j, k: k)
        if tb:
            b_spec = pl.BlockSpec((None, tn, tk), lambda i, j, k: (lead(j, k), 0 if b_blk == "n" else j, 0 if b_blk == "k" else k))
        else:
            b_spec = pl.BlockSpec((None, tk, tn), lambda i, j, k: (lead(j, k), 0 if b_blk == "k" else k, 0 if b_blk == "n" else j))
    else:
        b_spec = pl.BlockSpec((tn, tk), lambda i, j, k: (j, k)) if tb else pl.BlockSpec((tk, tn), lambda i, j, k: (k, j))
    slot = _slot_of if o_slots else (lambda k: k)
    if o_blk == "n":
        assert gn == N_DEV or not o_slots
        o_shape, o_spec = (gn, M, tn), pl.BlockSpec((None, tm, tn), lambda i, j, k: (slot(j), i, 0))
    elif o_blk == "m" and o_slots and gm < N_DEV:
        rs = M // N_DEV
        per_tile = tm // rs
        assert per_tile % 2 == 0 and tm % rs == 0
        o_shape = (2, N_CHIP, rs, N)
        o_spec = pl.BlockSpec((2, per_tile // 2, rs, tn), lambda i, j, k: (0, i, 0, j))
    elif o_blk == "m":
        assert gm == N_DEV or not o_slots
        o_shape, o_spec = (gm, tm, N), pl.BlockSpec((None, tm, tn), lambda i, j, k: (slot(i), 0, j))
    elif o_seg:
        o_shape, o_spec = (tm, NSEG * N), pl.BlockSpec((tm, tn), lambda i, j, k: (0, i * (N // tn) + j))
    else:
        o_shape, o_spec = (M, N), pl.BlockSpec((tm, tn), lambda i, j, k: (i, j))
    dims = (((0 if ta else 1,), (1 if tb else 0,)), ((), ()))
    if epi is not None:
        assert gn == 1 and add is None and o_blk is None and not o_seg
        epi_fn, add, epi_w = epi
        o_shape, o_spec = (M, epi_w), pl.BlockSpec((tm, epi_w), lambda i, j, k: (i, 0))
    has_add = add is not None
    add_spec = pl.BlockSpec((tm, add.shape[1]), lambda i, j, k: (i, 0)) if epi is not None else o_spec

    carry = carry or _NO_CARRY
    n_in = 2 + has_add

    def body(*refs):
        own, c_in, c_out, c_sems = _carry_split(carry, refs, n_in, 1)
        a_ref, b_ref = own[0], own[1]
        add_ref = own[2] if has_add else None
        o_ref, acc_ref = own[-2], own[-1]
        i, j, k = pl.program_id(0), pl.program_id(1), pl.program_id(2)
        _carry_start(carry, c_in, c_out, c_sems, (i == 0) & (j == 0) & (k == 0))

        @pl.when(k == 0)
        def _():
            acc_ref[...] = jnp.zeros_like(acc_ref)

        acc_ref[...] += lax.dot_general(a_ref[...].astype(bf16), b_ref[...].astype(bf16), dims, preferred_element_type=f32)

        @pl.when(k == nk - 1)
        def _():
            r = acc_ref[...] * scale
            if epi is not None:
                r = epi_fn(r, add_ref[...].astype(f32))
            elif has_add:
                r = r + add_ref[...].astype(f32)
            if len(o_shape) == 4:
                rs = o_shape[2]
                for chip_l in range(o_ref.shape[1]):
                    for core in range(2):
                        dev = 2 * chip_l + core
                        o_ref[core, chip_l] = r[dev * rs:(dev + 1) * rs].astype(out_dtype)
            else:
                o_ref[...] = r.astype(out_dtype)

        _carry_finish(carry, c_in, c_out, c_sems, (i == gm - 1) & (j == gn - 1) & (k == nk - 1))

    ins = [a, b] + ([add] if has_add else []) + list(carry.ins)
    in_specs = [a_spec, b_spec] + ([add_spec] if has_add else []) + [_ANY] * len(carry.ins)
    res = pl.pallas_call(
        body, name=name, grid=(gm, gn, nk), in_specs=in_specs, out_specs=[o_spec] + [_ANY] * len(carry.out_shapes),
        out_shape=[_S(o_shape, out_dtype)] + list(carry.out_shapes),
        scratch_shapes=[pltpu.VMEM((tm, tn), f32)] + list(carry.sems),
        compiler_params=_cparams(("arbitrary",) * 3 if carry.ins else ("parallel", "parallel", "arbitrary")),
    )(*ins)
    out = res[0]
    if len(o_shape) == 4:
        out = out.reshape(N_DEV, o_shape[2], N)
    elif o_seg:
        out = out.reshape(M, N)
    return (out, list(res[1:])) if carry.ins else out


def _row_tile(T, widths):
    budget = ROW_BLOCK_BYTES
    tb = max(16, budget // (4 * sum(widths)))
    return _pick(T, tb, align=16)


def _rows(fn, rows, params, outs, *, name, carry=None):
    T = rows[0].shape[0]
    nr, npar = len(rows), len(params)
    tb = _row_tile(T, [r.shape[1] for r in rows] + [w for w, _ in outs])
    carry = carry or _NO_CARRY

    def body(*refs):
        own, c_in, c_out, c_sems = _carry_split(carry, refs, nr + npar, len(outs))
        _carry_start(carry, c_in, c_out, c_sems, pl.program_id(0) == 0)
        ins = [r[...].astype(f32) for r in own[: nr + npar]]
        res = fn(*ins)
        for o_ref, r in zip(own[nr + npar:], res):
            o_ref[...] = r.astype(o_ref.dtype)
        _carry_finish(carry, c_in, c_out, c_sems, pl.program_id(0) == T // tb - 1)

    in_specs = [pl.BlockSpec((tb, r.shape[1]), lambda i: (i, 0)) for r in rows]
    in_specs += [pl.BlockSpec(p.shape, lambda i: (0, 0)) for p in params]
    out_specs = [pl.BlockSpec((tb, w), lambda i: (i, 0)) for w, _ in outs]
    res = pl.pallas_call(
        body, name=name, grid=(T // tb,), in_specs=in_specs + [_ANY] * len(carry.ins),
        out_specs=out_specs + [_ANY] * len(carry.out_shapes), out_shape=[_S((T, w), d) for w, d in outs] + list(carry.out_shapes),
        scratch_shapes=list(carry.sems), compiler_params=_cparams(("arbitrary",) if carry.ins else ("parallel",)),
    )(*rows, *params, *carry.ins)
    return (tuple(res[:len(outs)]), list(res[len(outs):])) if carry.ins else tuple(res)


def _rows_bwd(fn, rows, params, cots, *, name, want_rows, row_dtypes=None, adds=None):
    T = rows[0].shape[0]
    nr, npar, nc = len(rows), len(params), len(cots)
    adds = adds or {}
    add_idx = sorted(adds)
    row_dtypes = row_dtypes or {}
    widths = [r.shape[1] for r in rows] + [c.shape[1] for c in cots] + [rows[i].shape[1] for i in want_rows]
    tb = _row_tile(T, widths)

    def body(*refs):
        ins = [r[...].astype(f32) for r in refs[: nr + npar]]
        cot = tuple(r[...].astype(f32) for r in refs[nr + npar: nr + npar + nc])
        add_refs = refs[nr + npar + nc: nr + npar + nc + len(add_idx)]
        out_refs = refs[nr + npar + nc + len(add_idx):]
        _, vjp = jax.vjp(lambda *a: tuple(fn(*a)), *ins)
        g = vjp(cot)
        for o_ref, i in zip(out_refs[: len(want_rows)], want_rows):
            r = g[i]
            if i in adds:
                r = r + add_refs[add_idx.index(i)][...].astype(f32)
            o_ref[...] = r.astype(o_ref.dtype)
        first = pl.program_id(0) == 0
        for o_ref, gp in zip(out_refs[len(want_rows):], g[nr:]):
            @pl.when(first)
            def _(o_ref=o_ref):
                o_ref[...] = jnp.zeros_like(o_ref)

            o_ref[...] += gp

    in_specs = [pl.BlockSpec((tb, r.shape[1]), lambda i: (i, 0)) for r in rows]
    in_specs += [pl.BlockSpec(p.shape, lambda i: (0, 0)) for p in params]
    in_specs += [pl.BlockSpec((tb, c.shape[1]), lambda i: (i, 0)) for c in cots]
    in_specs += [pl.BlockSpec((tb, adds[i].shape[1]), lambda i_: (i_, 0)) for i in add_idx]
    out_specs = [pl.BlockSpec((tb, rows[i].shape[1]), lambda i_: (i_, 0)) for i in want_rows]
    out_specs += [pl.BlockSpec(p.shape, lambda i: (0, 0)) for p in params]
    out_shape = [_S(rows[i].shape, row_dtypes.get(i, f32)) for i in want_rows] + [_S(p.shape, f32) for p in params]
    res = pl.pallas_call(
        body, name=name, grid=(T // tb,), in_specs=in_specs, out_specs=out_specs, out_shape=out_shape,
        compiler_params=_cparams(("arbitrary",)),
    )(*rows, *params, *cots, *[adds[i] for i in add_idx])
    return list(res[: len(want_rows)]), list(res[len(want_rows):])


def _f_rmsnorm(x, g):
    return (x * lax.rsqrt(jnp.mean(x * x, axis=-1, keepdims=True) + EPS) * g,)


def _f_swiglu(ab):
    F = ab.shape[1] // 2
    return (jax.nn.silu(ab[:, :F]) * ab[:, F:],)


def _f_s5_post(y, u, d):
    return (jax.nn.gelu(y + d * u),)


def _f_glu(g, v, b):
    return (g * jax.nn.sigmoid(v + b),)


def _f_gated_norm(y, z, w):
    return _f_rmsnorm(y * jax.nn.silu(z), w)


def _f_merge(gl, p5, pssd, b):
    D = p5.shape[1]
    gates = jax.nn.sigmoid(gl + b)
    return (gates[:, :D] * p5 + gates[:, D:] * pssd,)


def _f_dt(dtr, bias, a_log):
    dt = jax.nn.softplus(dtr + bias)
    return dt, dt * (-jnp.exp(a_log))


def _f_dt_expand(dtr, bias, a_log, e):
    dt, a = _f_dt(dtr, bias, a_log)
    return dt, a, _doth(dt, e), _doth(a, e)


def _loss_stage(x, tgt, g, *, name):
    T, D = x.shape
    tb = _row_tile(T, [D, D, D])

    def f(xb, gb, tb_):
        y = _f_rmsnorm(xb, gb)[0]
        return 0.5 * jnp.sum(jnp.mean(jnp.square(y - tb_), axis=-1, keepdims=True), axis=0, keepdims=True)

    def body(x_ref, t_ref, g_ref, l_ref, dx_ref, dg_ref):
        tv = t_ref[...]
        val, vjp = jax.vjp(lambda a, b: f(a, b, tv), x_ref[...], g_ref[...])
        dx, dg = vjp(jnp.ones((1, 1), f32))
        dx_ref[...] = dx

        @pl.when(pl.program_id(0) == 0)
        def _():
            l_ref[...] = jnp.zeros_like(l_ref)
            dg_ref[...] = jnp.zeros_like(dg_ref)

        l_ref[...] += jnp.broadcast_to(val, l_ref.shape)
        dg_ref[...] += dg

    row = pl.BlockSpec((tb, D), lambda i: (i, 0))
    par = pl.BlockSpec((1, D), lambda i: (0, 0))
    return pl.pallas_call(
        body, name=name, grid=(T // tb,), in_specs=[row, row, par],
        out_specs=[pl.BlockSpec((1, LANES), lambda i: (0, 0)), row, par],
        out_shape=[_S((1, LANES), f32), _S((T, D), f32), _S((1, D), f32)], compiler_params=_cparams(("arbitrary",)),
    )(x, tgt, g)


CONV_R = 64
HALO = 8


def _conv_shifts_down(ref, t):
    if isinstance(t, int) and t == 0:
        cur = ref[0:CONV_R, :]
        row = lax.broadcasted_iota(jnp.int32, cur.shape, 0)
        return [cur] + [jnp.where(row >= s, pltpu.roll(cur, s, axis=0), 0.0) for s in range(1, CONV_K)]
    win = ref[pl.ds(pl.multiple_of(t * CONV_R - HALO, HALO), CONV_R + HALO), :]
    return [win[HALO:]] + [pltpu.roll(win, s, axis=0)[HALO:] for s in range(1, CONV_K)]


def _conv_shifts_up(ref, t):
    if isinstance(t, int):
        cur = ref[t * CONV_R:(t + 1) * CONV_R, :]
        row = lax.broadcasted_iota(jnp.int32, cur.shape, 0)
        return [cur] + [jnp.where(row < CONV_R - s, pltpu.roll(cur, CONV_R - s, axis=0), 0.0) for s in range(1, CONV_K)]
    win = ref[pl.ds(pl.multiple_of(t * CONV_R, HALO), CONV_R + HALO), :]
    return [win[:CONV_R]] + [pltpu.roll(win, CONV_R + HALO - s, axis=0)[:CONV_R] for s in range(1, CONV_K)]


def _conv_pre(shifted, w, b):
    pre = b
    for k in range(CONV_K):
        pre = pre + w[k:k + 1, :] * shifted[CONV_K - 1 - k]
    return pre


def _conv_fwd(x, w, b, *, name):
    T, C = x.shape
    cb = _pick(C, 256)

    def body(x_ref, w_ref, b_ref, o_ref):
        xv = x_ref[...]
        row = lax.broadcasted_iota(jnp.int32, xv.shape, 0)
        shifted = [xv] + [jnp.where(row >= s, pltpu.roll(xv, s, axis=0), 0.0) for s in range(1, CONV_K)]
        o_ref[...] = jax.nn.silu(_conv_pre(shifted, w_ref[...], b_ref[...]))

    col = pl.BlockSpec((T, cb), lambda j: (0, j))
    return pl.pallas_call(
        body, name=name, grid=(C // cb,), in_specs=[col, pl.BlockSpec((CONV_K, cb), lambda j: (0, j)), pl.BlockSpec((1, cb), lambda j: (0, j))],
        out_specs=col, out_shape=_S((T, C), f32), compiler_params=_cparams(("parallel",)),
    )(x, w, b)


def _conv_bwd(x, w, b, dy, *, name, carry=None):
    T, C = x.shape
    cb = _pick(C, 128)
    carry = carry or _NO_CARRY
    ends = []
    for d in dy:
        ends.append((ends[-1] if ends else 0) + d.shape[1] // cb)
    assert ends[-1] == C // cb and all(d.shape[1] % cb == 0 for d in dy)
    npc = len(dy)
    n_tiles = T // CONV_R

    def body(*refs):
        own, c_in, c_out, c_sems = _carry_split(carry, refs, 3 + npc, 3)
        x_ref, w_ref, b_ref = own[:3]
        dy_refs, (dx_ref, dw_ref, db_ref, dpre_ref) = own[3:3 + npc], own[3 + npc:]
        _carry_start(carry, c_in, c_out, c_sems, pl.program_id(0) == 0)
        j = pl.program_id(0)
        wv, bv = w_ref[...], b_ref[...]

        def fold8(v):
            return jnp.sum(v.reshape(CONV_R // 8, 8, cb), axis=0)

        def first_pass(t, acc):
            rows = slice(0, CONV_R) if isinstance(t, int) else pl.ds(pl.multiple_of(t * CONV_R, CONV_R), CONV_R)
            shifted = _conv_shifts_down(x_ref, t)
            pre = _conv_pre(shifted, wv, bv)
            dyv = dy_refs[-1][rows, :]
            for p in range(npc - 2, -1, -1):
                dyv = jnp.where(j < ends[p], dy_refs[p][rows, :], dyv)
            sg = jax.nn.sigmoid(pre)
            dpre = dyv * sg * (1.0 + pre * (1.0 - sg))
            dpre_ref[rows, :] = dpre
            return tuple(acc[k] + fold8(dpre * shifted[CONV_K - 1 - k]) for k in range(CONV_K)) + (acc[CONV_K] + fold8(dpre),)

        zero = jnp.zeros((8, cb), f32)
        acc = lax.fori_loop(1, n_tiles, first_pass, first_pass(0, (zero,) * (CONV_K + 1)), unroll=3)
        for k in range(CONV_K):
            dw_ref[k:k + 1, :] = jnp.sum(acc[k], axis=0, keepdims=True)
        db_ref[...] = jnp.sum(acc[CONV_K], axis=0, keepdims=True)

        def dx_of(t):
            up = _conv_shifts_up(dpre_ref, t)
            dx = wv[CONV_K - 1:CONV_K, :] * up[0]
            for k in range(CONV_K - 1):
                dx = dx + wv[k:k + 1, :] * up[CONV_K - 1 - k]
            return dx

        def second_pass(t, c):
            dx_ref[pl.ds(pl.multiple_of(t * CONV_R, CONV_R), CONV_R), :] = dx_of(t)
            return c

        lax.fori_loop(0, n_tiles - 1, second_pass, 0, unroll=3)
        dx_ref[(n_tiles - 1) * CONV_R:, :] = dx_of(n_tiles - 1)
        _carry_finish(carry, c_in, c_out, c_sems, pl.program_id(0) == C // cb - 1)

    col = pl.BlockSpec((T, cb), lambda j: (0, j))
    wsp = pl.BlockSpec((CONV_K, cb), lambda j: (0, j))
    bsp = pl.BlockSpec((1, cb), lambda j: (0, j))
    starts = [0] + ends[:-1]
    dy_specs = [pl.BlockSpec((T, cb), lambda j, s=s, e=e: (0, jnp.clip(j, s, e - 1) - s)) for s, e in zip(starts, ends)]
    res = pl.pallas_call(
        body, name=name, grid=(C // cb,), in_specs=[col, wsp, bsp] + dy_specs + [_ANY] * len(carry.ins),
        out_specs=[col, wsp, bsp] + [_ANY] * len(carry.out_shapes),
        out_shape=[_S((T, C), f32), _S((CONV_K, C), f32), _S((1, C), f32)] + list(carry.out_shapes),
        scratch_shapes=[pltpu.VMEM((T, cb), f32)] + list(carry.sems),
        compiler_params=_cparams(("arbitrary",) if carry.ins else ("parallel",)),
    )(x, w, b, *dy, *carry.ins)
    return (*res[:3], list(res[3:]))


def _f_s5_prep(lr, li, ldt, lrb, lib, ldtb, brt, bit):
    def disc(lr_, li_, ldt_):
        dt = jnp.exp(ldt_)
        mag = jnp.exp(lr_ * dt)
        ar, ai = mag * jnp.cos(li_ * dt), mag * jnp.sin(li_ * dt)
        den = lr_ * lr_ + li_ * li_
        cr = ((ar - 1.0) * lr_ + ai * li_) / den
        ci = (ai * lr_ - (ar - 1.0) * li_) / den
        return ar, ai, cr, ci

    ar, ai, _, _ = disc(lr, li, ldt)
    _, _, cr, ci = disc(lrb, lib, ldtb)
    return ar, ai, cr * brt - ci * bit, cr * bit + ci * brt


def _s5_prep(args, *, name):
    G, N = args[0].shape
    GM = args[3].shape[0]

    def body(*refs):
        res = _f_s5_prep(*[r[...] for r in refs[:8]])
        for o, r in zip(refs[8:], res):
            o[...] = r

    return pl.pallas_call(body, name=name, out_shape=[_S((G, N), f32)] * 2 + [_S((GM, N), f32)] * 2)(*args)


def _s5_prep_bwd(args, cots, rsum, *, name):
    G, N = args[0].shape
    GM = args[3].shape[0]

    def body(*refs):
        ins = [r[...] for r in refs[:8]]
        cot = tuple(r[...] for r in refs[8:12])
        rs = refs[12][...]
        _, vjp = jax.vjp(_f_s5_prep, *ins)
        g = vjp(cot)
        fold = lambda v: jnp.dot(rs, v, preferred_element_type=f32, precision=lax.Precision.HIGHEST)
        o = refs[13:]
        o[0][...] = g[0] + fold(g[3])
        o[1][...] = g[1] + fold(g[4])
        o[2][...] = g[2] + fold(jnp.broadcast_to(g[5], (GM, LANES)))[:, 0:1]
        o[3][...] = g[6]
        o[4][...] = g[7]

    return pl.pallas_call(
        body, name=name, out_shape=[_S((G, N), f32), _S((G, N), f32), _S((G, 1), f32), _S((GM, N), f32), _S((GM, N), f32)],
    )(*args, *cots, rsum)


S5_TC = 512


def _s5_local_scan(src, w_r, w_i, a_r, a_i, *, reverse, name, carry=None, powers=True):
    T, C = src.shape
    nblk, cb, sb = w_r.shape
    NS = nblk * sb
    tc = min(S5_TC, T)
    nT, nt = T // tc, tc // NSEG
    tmap = (lambda i: nT - 1 - i) if reverse else (lambda i: i)

    carry = carry or _NO_CARRY

    def body(*refs):
        own, c_in, c_out, c_sems = _carry_split(carry, refs, 5, 4)
        u_ref, wr_ref, wi_ref, ar_ref, ai_ref, sr_ref, si_ref, pr_ref, pi_ref, st_r, st_i, pw_r, pw_i = own
        _carry_start(carry, c_in, c_out, c_sems, (pl.program_id(0) == 0) & (pl.program_id(1) == 0))

        @pl.when(pl.program_id(1) == 0)
        def _():
            st_r[...] = jnp.zeros_like(st_r)
            st_i[...] = jnp.zeros_like(st_i)
            pw_r[...] = jnp.ones_like(pw_r)
            pw_i[...] = jnp.zeros_like(pw_i)

        u = u_ref[...].astype(bf16)
        sr_ref[...] = jnp.dot(u, wr_ref[...], preferred_element_type=f32)
        si_ref[...] = jnp.dot(u, wi_ref[...], preferred_element_type=f32)
        ar = jnp.broadcast_to(ar_ref[...], (NSEG, sb))
        ai = jnp.broadcast_to(ai_ref[...], (NSEG, sb))

        def step(k, c):
            cr, ci, qr, qi = c
            kk = (nt - 1 - k) if reverse else k
            rows = pl.ds(pl.multiple_of(kk * NSEG, NSEG), NSEG)
            nr = ar * cr - ai * ci + sr_ref[rows, :]
            ni = ar * ci + ai * cr + si_ref[rows, :]
            sr_ref[rows, :] = nr
            si_ref[rows, :] = ni
            return (nr, ni, ar * qr - ai * qi, ar * qi + ai * qr) if powers else (nr, ni, qr, qi)

        cr, ci, qr, qi = lax.fori_loop(0, nt, step, (st_r[...], st_i[...], pw_r[...], pw_i[...]), unroll=8)
        st_r[...], st_i[...], pw_r[...], pw_i[...] = cr, ci, qr, qi
        pr_ref[...] = qr
        pi_ref[...] = qi
        _carry_finish(carry, c_in, c_out, c_sems, (pl.program_id(0) == nblk - 1) & (pl.program_id(1) == nT - 1))

    blk = pl.BlockSpec((tc, sb), lambda j, i: (tmap(i), j))
    wsp = pl.BlockSpec((None, cb, sb), lambda j, i: (j, 0, 0))
    asp = pl.BlockSpec((1, sb), lambda j, i: (0, j))
    psp = pl.BlockSpec((NSEG, sb), lambda j, i: (0, j))
    res = pl.pallas_call(
        body, name=name, grid=(nblk, nT),
        in_specs=[pl.BlockSpec((tc, cb), lambda j, i: (tmap(i), j)), wsp, wsp, asp, asp] + [_ANY] * len(carry.ins),
        out_specs=[blk, blk, psp, psp] + [_ANY] * len(carry.out_shapes),
        out_shape=[_S((T, NS), f32)] * 2 + [_S((NSEG, NS), f32)] * 2 + list(carry.out_shapes),
        scratch_shapes=[pltpu.VMEM((NSEG, sb), f32)] * 4 + list(carry.sems),
        compiler_params=_cparams(("arbitrary", "arbitrary") if carry.ins else ("parallel", "arbitrary")),
    )(src, w_r, w_i, a_r, a_i, *carry.ins)
    return (*res[:4], list(res[4:])) if carry.ins else res


def _s5_carry(e_r, e_i, p_r, p_i, *, reverse, name):
    NS = e_r.shape[1]

    def body(er_ref, ei_ref, pr_ref, pi_ref, cr_ref, ci_ref):
        ar, ai = pr_ref[0:1, :], pi_ref[0:1, :]
        cr = jnp.zeros((1, NS), f32)
        ci = jnp.zeros((1, NS), f32)
        order = list(range(NSEG - 1, -1, -1)) if reverse else list(range(NSEG))
        cr_ref[order[0]:order[0] + 1, :] = cr
        ci_ref[order[0]:order[0] + 1, :] = ci
        for prev, q in zip(order[:-1], order[1:]):
            er, ei = er_ref[prev:prev + 1, :], ei_ref[prev:prev + 1, :]
            cr, ci = er + ar * cr - ai * ci, ei + ar * ci + ai * cr
            cr_ref[q:q + 1, :] = cr
            ci_ref[q:q + 1, :] = ci

    return pl.pallas_call(body, name=name, out_shape=[_S((NSEG, NS), f32)] * 2)(e_r, e_i, p_r, p_i)


def _s5_fix_out(sl_r, sl_i, a_r, a_i, c_r, c_i, wc_r, wc_i, *, name):
    T, NS = sl_r.shape
    nblk, sb, cb = wc_r.shape
    tc = min(S5_TC, T)
    nT, nt = T // tc, tc // NSEG

    def body(lr_ref, li_ref, ar_ref, ai_ref, cr_ref, ci_ref, wr_ref, wi_ref, sr_ref, si_ref, y_ref, pw_r, pw_i):
        @pl.when(pl.program_id(1) == 0)
        def _():
            pw_r[...] = jnp.ones_like(pw_r)
            pw_i[...] = jnp.zeros_like(pw_i)

        ar = jnp.broadcast_to(ar_ref[...], (NSEG, sb))
        ai = jnp.broadcast_to(ai_ref[...], (NSEG, sb))
        cr, ci = cr_ref[...], ci_ref[...]

        def step(k, c):
            qr, qi = c
            qr, qi = ar * qr - ai * qi, ar * qi + ai * qr
            rows = pl.ds(pl.multiple_of(k * NSEG, NSEG), NSEG)
            sr_ref[rows, :] = lr_ref[rows, :] + qr * cr - qi * ci
            si_ref[rows, :] = li_ref[rows, :] + qr * ci + qi * cr
            return qr, qi

        qr, qi = lax.fori_loop(0, nt, step, (pw_r[...], pw_i[...]), unroll=8)
        pw_r[...], pw_i[...] = qr, qi
        y_ref[...] = (jnp.dot(sr_ref[...].astype(bf16), wr_ref[...], preferred_element_type=f32)
                      - jnp.dot(si_ref[...].astype(bf16), wi_ref[...], preferred_element_type=f32))

    blk = pl.BlockSpec((tc, sb), lambda j, i: (i, j))
    asp = pl.BlockSpec((1, sb), lambda j, i: (0, j))
    csp = pl.BlockSpec((NSEG, sb), lambda j, i: (0, j))
    wsp = pl.BlockSpec((None, sb, cb), lambda j, i: (j, 0, 0))
    return pl.pallas_call(
        body, name=name, grid=(nblk, nT), in_specs=[blk, blk, asp, asp, csp, csp, wsp, wsp],
        out_specs=[blk, blk, pl.BlockSpec((tc, cb), lambda j, i: (i, j))],
        out_shape=[_S((T, NS), f32)] * 2 + [_S((T, nblk * cb), f32)],
        scratch_shapes=[pltpu.VMEM((NSEG, sb), f32)] * 2, compiler_params=_cparams(("parallel", "arbitrary")),
    )(sl_r, sl_i, a_r, a_i, c_r, c_i, wc_r, wc_i)


def _s5_fix_bwd(ql_r, ql_i, ab_r, ab_i, c_r, c_i, s_r, s_i, sb_r, sb_i, u, dy, du_add, w_r, w_i, *, name):
    T, NS = ql_r.shape
    nblk, cb, sb = w_r.shape
    tc = min(S5_TC, T)
    nT, nt = T // tc, tc // NSEG
    tmap = lambda i: nT - 1 - i

    def body(lr_ref, li_ref, ar_ref, ai_ref, cr_ref, ci_ref, sr_ref, si_ref, br_ref, bi_ref, u_ref, dy_ref, dua_ref, wr_ref, wi_ref,
             du_ref, dwr_ref, dwi_ref, dcr_ref, dci_ref, dar_ref, dai_ref, pw_r, pw_i, ac_r, ac_i, q_r, q_i):
        first = pl.program_id(1) == 0

        @pl.when(first)
        def _():
            pw_r[...] = jnp.ones_like(pw_r)
            pw_i[...] = jnp.zeros_like(pw_i)
            ac_r[...] = jnp.zeros_like(ac_r)
            ac_i[...] = jnp.zeros_like(ac_i)
            dwr_ref[...] = jnp.zeros_like(dwr_ref)
            dwi_ref[...] = jnp.zeros_like(dwi_ref)
            dcr_ref[...] = jnp.zeros_like(dcr_ref)
            dci_ref[...] = jnp.zeros_like(dci_ref)

        ar = jnp.broadcast_to(ar_ref[...], (NSEG, sb))
        ai = jnp.broadcast_to(ai_ref[...], (NSEG, sb))
        cr, ci = cr_ref[...], ci_ref[...]

        def fix(rows, qr, qi, spr, spi, accr, acci):
            qr, qi = ar * qr - ai * qi, ar * qi + ai * qr
            xr = lr_ref[rows, :] + qr * cr - qi * ci
            xi = li_ref[rows, :] + qr * ci + qi * cr
            q_r[rows, :] = xr
            q_i[rows, :] = xi
            return qr, qi, accr + xr * spr + xi * spi, acci + xi * spr - xr * spi

        def step(k, c):
            qr, qi, accr, acci = c
            kk = nt - 1 - k
            rows = pl.ds(pl.multiple_of(kk * NSEG, NSEG), NSEG)
            prev = pl.ds(pl.multiple_of((kk - 1) * NSEG, NSEG), NSEG)
            return fix(rows, qr, qi, sr_ref[prev, :], si_ref[prev, :], accr, acci)

        c = lax.fori_loop(0, nt - 1, step, (pw_r[...], pw_i[...], ac_r[...], ac_i[...]), unroll=7)
        qr, qi, accr, acci = fix(pl.ds(0, NSEG), *c[:2], br_ref[...], bi_ref[...], *c[2:])
        pw_r[...], pw_i[...], ac_r[...], ac_i[...] = qr, qi, accr, acci

        qrb, qib = q_r[...].astype(bf16), q_i[...].astype(bf16)
        nt_dims = (((1,), (1,)), ((), ()))
        tn_dims = (((0,), (0,)), ((), ()))
        du_ref[...] = (dua_ref[...] + lax.dot_general(qrb, wr_ref[...], nt_dims, preferred_element_type=f32)
                       + lax.dot_general(qib, wi_ref[...], nt_dims, preferred_element_type=f32))
        ub = u_ref[...].astype(bf16)
        dwr_ref[...] += lax.dot_general(ub, qrb, tn_dims, preferred_element_type=f32)
        dwi_ref[...] += lax.dot_general(ub, qib, tn_dims, preferred_element_type=f32)
        dyb = dy_ref[...].astype(bf16)
        dcr_ref[...] += lax.dot_general(sr_ref[...].astype(bf16), dyb, tn_dims, preferred_element_type=f32)
        dci_ref[...] -= lax.dot_general(si_ref[...].astype(bf16), dyb, tn_dims, preferred_element_type=f32)

        @pl.when(pl.program_id(1) == nT - 1)
        def _():
            dar_ref[...] = jnp.sum(accr, axis=0, keepdims=True)
            dai_ref[...] = jnp.sum(acci, axis=0, keepdims=True)

    blk = pl.BlockSpec((tc, sb), lambda j, i: (tmap(i), j))
    asp = pl.BlockSpec((1, sb), lambda j, i: (0, j))
    csp = pl.BlockSpec((NSEG, sb), lambda j, i: (0, j))
    bsp = pl.BlockSpec((None, NSEG, sb), lambda j, i: (tmap(i), 0, j))
    chn = pl.BlockSpec((tc, cb), lambda j, i: (tmap(i), j))
    wsp = pl.BlockSpec((None, cb, sb), lambda j, i: (j, 0, 0))
    wcs = pl.BlockSpec((None, sb, cb), lambda j, i: (j, 0, 0))
    return pl.pallas_call(
        body, name=name, grid=(nblk, nT), in_specs=[blk, blk, asp, asp, csp, csp, blk, blk, bsp, bsp, chn, chn, chn, wsp, wsp],
        out_specs=[chn, wsp, wsp, wcs, wcs, asp, asp],
        out_shape=[_S((T, nblk * cb), f32), _S((nblk, cb, sb), f32), _S((nblk, cb, sb), f32), _S((nblk, sb, cb), f32),
                   _S((nblk, sb, cb), f32), _S((1, NS), f32), _S((1, NS), f32)],
        scratch_shapes=[pltpu.VMEM((NSEG, sb), f32)] * 4 + [pltpu.VMEM((tc, sb), f32)] * 2,
        compiler_params=_cparams(("parallel", "arbitrary")),
    )(ql_r, ql_i, ab_r, ab_i, c_r, c_i, s_r, s_i, sb_r, sb_i, u, dy, du_add, w_r, w_i)


SSD2_TB = 512
_NN = (((1,), (0,)), ((), ()))
_NT = (((1,), (1,)), ((), ()))
_TN = (((0,), (0,)), ((), ()))


def _dotf(a, b, dims):
    return lax.dot_general(a.astype(bf16), b.astype(bf16), dims, preferred_element_type=f32)


def _doth(a, b, dims=_NN, sel="b", parts=3):
    x, m = (a, b) if sel == "b" else (b, a)
    m = m.astype(bf16)
    out = None
    for _ in range(parts):
        piece = x.astype(bf16)
        x = x - piece.astype(f32)
        d = lax.dot_general(*((piece, m) if sel == "b" else (m, piece)), dims, preferred_element_type=f32)
        out = d if out is None else out + d
    return out


def _ssd_consts(hpg):
    W = hpg * CHUNK
    i = lax.broadcasted_iota(jnp.int32, (CHUNK, CHUNK), 0)
    j = lax.broadcasted_iota(jnp.int32, (CHUNK, CHUNK), 1)
    tril = (i >= j).astype(f32)
    r = lax.broadcasted_iota(jnp.int32, (W, W), 0)
    c = lax.broadcasted_iota(jnp.int32, (W, W), 1)
    bd = (r // CHUNK == c // CHUNK).astype(f32)
    triu_bd = bd * (r <= c).astype(f32)
    e_r = lax.broadcasted_iota(jnp.int32, (W, LANES), 0)
    e_c = lax.broadcasted_iota(jnp.int32, (W, LANES), 1)
    ered = (e_r // HEADDIM == e_c).astype(f32)
    return tril, jnp.tile(tril, (1, hpg)), bd, triu_bd, ered


def _ssd2_specs(G, hpg, tb, tmap, b_off, c_off):
    W = hpg * HEADDIM
    ncb = tb // CHUNK
    xsp = pl.BlockSpec((tb, W), lambda g, i: (tmap(i), g))
    bsp = pl.BlockSpec((tb, SSD_STATE), lambda g, i: (tmap(i), b_off + g))
    csp = pl.BlockSpec((tb, SSD_STATE), lambda g, i: (tmap(i), c_off + g))
    rsp = pl.BlockSpec((None, ncb, W), lambda g, i: (g, tmap(i), 0))
    dsp = pl.BlockSpec((1, W), lambda g, i: (0, g))
    hsp = pl.BlockSpec((None, ncb, SSD_STATE, W), lambda g, i: (g, tmap(i), 0, 0))
    const = lambda a: pl.BlockSpec(a.shape, lambda g, i: (0, 0))
    return xsp, bsp, csp, rsp, dsp, hsp, const


def _tile_rows(a, n):
    return jnp.concatenate([a] * n, axis=0)


def _ssd2_fwd(xc, dt4, a4, dtw, aw, d4, consts, *, d_inner, name, carry=None):
    carry = carry or _NO_CARRY
    T = xc.shape[0]
    G, nc, W = dtw.shape
    hpg = W // CHUNK
    tb = min(SSD2_TB, T)
    nb, ncb = T // tb, tb // CHUNK
    b_off = d_inner // SSD_STATE
    xsp, bsp, csp, rsp, dsp, hsp, const = _ssd2_specs(G, hpg, tb, lambda i: i, b_off, b_off + G)
    tril, mask4, bd, triu_bd, _ = consts

    def body(*refs):
        own, c_in, c_out, c_sems = _carry_split(carry, refs, 12, 2)
        x_ref, b_ref, c_ref, dt_ref, a_ref, dtw_ref, aw_ref, d_ref, tril_ref, mask_ref, bd_ref, tbd_ref, y_ref, hs_ref, h_scr = own
        _carry_start(carry, c_in, c_out, c_sems, (pl.program_id(0) == 0) & (pl.program_id(1) == 0))

        @pl.when(pl.program_id(1) == 0)
        def _():
            h_scr[...] = jnp.zeros_like(h_scr)

        acs_rows = _doth(aw_ref[...], tbd_ref[...])
        ht = h_scr[...]
        for c in range(ncb):
            rows = slice(c * CHUNK, (c + 1) * CHUNK)
            x, bm, cm = x_ref[rows, :], b_ref[rows, :], c_ref[rows, :]
            acs = _doth(tril_ref[...], a_ref[rows, :], sel="a")
            lmat = jnp.where(mask_ref[...] > 0, jnp.exp(jnp.minimum(acs - acs_rows[c:c + 1, :], 0.0)), 0.0)
            m4 = _dotf(cm, _tile_rows(bm, hpg), _NT) * lmat * dtw_ref[c:c + 1, :]
            xbd = _tile_rows(x, hpg) * bd_ref[...]
            hs_ref[c] = ht
            y_ref[rows, :] = _dotf(m4, xbd, _NN) + _dotf(cm, ht, _NN) * jnp.exp(acs) + d_ref[...] * x
            a_last = acs[CHUNK - 1:CHUNK, :]
            xw = x * (jnp.exp(a_last - acs) * dt_ref[rows, :])
            ht = ht * jnp.exp(a_last) + _dotf(bm, xw, _TN)
        h_scr[...] = ht
        _carry_finish(carry, c_in, c_out, c_sems, (pl.program_id(0) == G - 1) & (pl.program_id(1) == nb - 1))

    res = pl.pallas_call(
        body, name=name, grid=(G, nb),
        in_specs=[xsp, bsp, csp, xsp, xsp, rsp, rsp, dsp, const(tril), const(mask4), const(bd), const(triu_bd)] + [_ANY] * len(carry.ins),
        out_specs=[xsp, hsp] + [_ANY] * len(carry.out_shapes),
        out_shape=[_S((T, G * W), f32), _S((G, nc, SSD_STATE, W), f32)] + list(carry.out_shapes),
        scratch_shapes=[pltpu.VMEM((SSD_STATE, W), f32)] + list(carry.sems),
        compiler_params=_cparams(("arbitrary", "arbitrary") if carry.ins else ("parallel", "arbitrary")),
    )(xc, xc, xc, dt4, a4, dtw, aw, d4, tril, mask4, bd, triu_bd, *carry.ins)
    return res[0], res[1], list(res[2:])


def _ssd2_bwd(xc, dt4, a4, dtw, aw, d4, consts, hs, dy, *, d_inner, name):
    T = xc.shape[0]
    G, nc, W = dtw.shape
    hpg = W // CHUNK
    tb = min(SSD2_TB, T)
    nb, ncb = T // tb, tb // CHUNK
    b_off = d_inner // SSD_STATE
    tmap = lambda i: nb - 1 - i
    xsp, bsp, csp, rsp, dsp, hsp, const = _ssd2_specs(G, hpg, tb, tmap, b_off, b_off + G)
    gsp = pl.BlockSpec((tb, SSD_STATE), lambda g, i: (tmap(i), g))
    ddsp = pl.BlockSpec((None, 1, LANES), lambda g, i: (g, 0, 0))
    tril, mask4, bd, triu_bd, ered = consts

    def body(x_ref, b_ref, c_ref, dt_ref, a_ref, dtw_ref, aw_ref, d_ref, tril_ref, mask_ref, bd_ref, tbd_ref, er_ref, hs_ref, dy_ref,
             dx_ref, db_ref, dc_ref, ddtc_ref, dac_ref, ddtw_ref, daw_ref, dd_ref, g_scr, dd_scr, rw_scr, tl_scr):
        first = pl.program_id(1) == 0

        @pl.when(first)
        def _():
            g_scr[...] = jnp.zeros_like(g_scr)
            dd_scr[...] = jnp.zeros_like(dd_scr)

        mask = mask_ref[...] > 0
        lane_in_block = lax.broadcasted_iota(jnp.int32, mask.shape, 1) & (CHUNK - 1)
        maskt = lax.broadcasted_iota(jnp.int32, mask.shape, 0) <= lane_in_block
        acs_rows = _doth(aw_ref[...], tbd_ref[...])
        dht = g_scr[...]
        dd = dd_scr[...]
        for c in range(ncb - 1, -1, -1):
            rows = slice(c * CHUNK, (c + 1) * CHUNK)
            x, bm, cm, dyc = x_ref[rows, :], b_ref[rows, :], c_ref[rows, :], dy_ref[rows, :]
            dtc, dtr = dt_ref[rows, :], dtw_ref[c:c + 1, :]
            ht = hs_ref[c]
            acs = _doth(tril_ref[...], a_ref[rows, :], sel="a")
            seg = acs - acs_rows[c:c + 1, :]
            lmat = jnp.where(mask, jnp.exp(jnp.minimum(seg, 0.0)), 0.0)
            lmat_t = jnp.where(maskt, jnp.exp(jnp.minimum(-seg, 0.0)), 0.0)
            btile, ctile = _tile_rows(bm, hpg), _tile_rows(cm, hpg)
            g4 = _dotf(cm, btile, _NT)
            gt4 = _dotf(bm, ctile, _NT)
            m4 = g4 * lmat * dtr
            mt4 = gt4 * lmat_t * dtc
            xbd = _tile_rows(x, hpg) * bd_ref[...]
            dybd = _tile_rows(dyc, hpg) * bd_ref[...]
            dm4 = _dotf(dyc, xbd, _NT)
            dmt4 = _dotf(x, dybd, _NT)
            dx = d_ref[...] * dyc + _dotf(mt4, dybd, _NN)
            dd = dd + jnp.sum(dyc * x, axis=0, keepdims=True)
            e4 = dm4 * m4
            dc = _dotf(dm4 * lmat * dtr, btile, _NN)
            db = _dotf(dmt4 * lmat_t * dtc, ctile, _NN)
            decay = jnp.exp(acs)
            yoff = _dotf(cm, ht, _NN) * decay
            dz = dyc * decay
            dc = dc + _dotf(dz, ht, _NT)
            dht_prev = _dotf(cm, dz, _TN)
            a_last = acs[CHUNK - 1:CHUNK, :]
            ea_last = jnp.exp(a_last)
            erel = jnp.exp(a_last - acs)
            dte = erel * dtc
            dxw = _dotf(bm, dht, _NN)
            db = db + _dotf(x * dte, dht, _NT)
            dx = dx + dxw * dte
            q4 = dxw * x
            dacs = e4 + dyc * yoff - q4 * dte
            col = jnp.concatenate([q4 * erel, _doth(tril_ref[...], dacs, _TN, sel="a")], axis=0)
            col = _doth(col, er_ref[...], parts=2)
            ddtc_ref[rows, :] = col[:CHUNK]
            dac_ref[rows, :] = col[CHUNK:]
            ddtw_ref[c:c + 1, :] = jnp.sum(dm4 * g4 * lmat, axis=0, keepdims=True)
            rw_scr[c:c + 1, :] = -jnp.sum(e4, axis=0, keepdims=True)
            tl_scr[c:c + 1, :] = jnp.sum(q4 * dte, axis=0, keepdims=True) + ea_last * jnp.sum(dht * ht, axis=0, keepdims=True)
            dx_ref[rows, :] = dx
            db_ref[rows, :] = db
            dc_ref[rows, :] = dc
            dht = dht_prev + dht * ea_last
        daw_ref[...] = _doth(rw_scr[...], tbd_ref[...], _NT) + _doth(tl_scr[...], bd_ref[...])
        g_scr[...] = dht
        dd_scr[...] = dd

        @pl.when(pl.program_id(1) == nb - 1)
        def _():
            dd_ref[...] = _doth(dd, er_ref[...])

    return pl.pallas_call(
        body, name=name, grid=(G, nb),
        in_specs=[xsp, bsp, csp, xsp, xsp, rsp, rsp, dsp, const(tril), const(mask4), const(bd), const(triu_bd), const(ered), hsp, xsp],
        out_specs=[xsp, gsp, gsp, gsp, gsp, rsp, rsp, ddsp],
        out_shape=[_S((T, G * W), f32), _S((T, G * SSD_STATE), f32), _S((T, G * SSD_STATE), f32), _S((T, G * LANES), f32),
                   _S((T, G * LANES), f32), _S(dtw.shape, f32), _S(dtw.shape, f32), _S((G, 1, LANES), f32)],
        scratch_shapes=[pltpu.VMEM((SSD_STATE, W), f32), pltpu.VMEM((1, W), f32), pltpu.VMEM((ncb, W), f32), pltpu.VMEM((ncb, W), f32)],
        compiler_params=_cparams(("parallel", "arbitrary")),
    )(xc, xc, xc, dt4, a4, dtw, aw, d4, tril, mask4, bd, triu_bd, ered, hs, dy)


def _peers():
    x, y, c = lax.axis_index("x"), lax.axis_index("y"), lax.axis_index("c")
    return x, y, c


_ANY = pl.BlockSpec(memory_space=pl.ANY)
N_CHIP = N_DEV // 2


def _all_gather(shards, *, name):
    n = len(shards)
    carry = _carry_gather(shards)

    def body(*refs):
        x_refs, out_refs, sems = refs[:n], refs[n:2 * n], refs[2 * n:]
        _gather_start(x_refs, out_refs, sems)
        _gather_finish(x_refs, out_refs, sems)

    return pl.pallas_call(
        body, name=name, out_shape=list(carry.out_shapes), in_specs=[_ANY] * n, out_specs=[_ANY] * n, scratch_shapes=list(carry.sems),
    )(*shards)


def _gather_parts(x_refs, out_refs, sems):
    send_sems, recv_sems, local_sems = sems
    x, y, c = _peers()
    me, sibling = (x, y, c), (x, y, 1 - c)
    chips = [(1 - x, y), (x, 1 - y), (1 - x, 1 - y)]
    n = len(x_refs)

    def copy(a, r, block, to, src=None):
        px, py, pc = block
        slot = out_refs[a].at[4 * px + 2 * py + pc]
        return pltpu.make_async_remote_copy(
            src_ref=slot if src is None else src, dst_ref=slot, send_sem=send_sems.at[7 * a + r],
            recv_sem=recv_sems.at[7 * a + r], device_id=to, device_id_type=MESH)

    mine = [pltpu.make_async_copy(x_refs[a], out_refs[a].at[4 * x + 2 * y + c], local_sems.at[a]) for a in range(n)]
    first = []
    for a in range(n):
        first.append(copy(a, 0, me, sibling, src=x_refs[a]))
        first += [copy(a, 1 + j, me, (*chip, c), src=x_refs[a]) for j, chip in enumerate(chips)]
    return copy, mine, first, me, sibling, chips, c, n


def _gather_start(x_refs, out_refs, sems):
    _, mine, first, *_ = _gather_parts(x_refs, out_refs, sems)
    for cp in mine + first:
        cp.start()


def _gather_finish(x_refs, out_refs, sems):
    copy, mine, first, me, sibling, chips, c, n = _gather_parts(x_refs, out_refs, sems)
    passed = []
    for j, chip in enumerate(chips):
        for a in range(n):
            copy(a, 1 + j, (*chip, c), me).wait_recv()
            fwd = copy(a, 4 + j, (*chip, c), sibling)
            fwd.start()
            passed.append(fwd)
    for a in range(n):
        copy(a, 0, sibling, me).wait_recv()
    for j, chip in enumerate(chips):
        for a in range(n):
            copy(a, 4 + j, (*chip, 1 - c), me).wait_recv()
    for cp in first + passed:
        cp.wait_send()
    for cp in mine:
        cp.wait()


def _exchange_sibling(slots, *, name):
    n = len(slots)

    def body(*refs):
        x_refs, sib_refs = refs[:n], refs[n:2 * n]
        send_sems, recv_sems = refs[2 * n:]
        x, y, c = _peers()
        give = [pltpu.make_async_remote_copy(
            src_ref=x_refs[a].at[pl.ds(N_CHIP * (1 - c), N_CHIP)], dst_ref=sib_refs[a], send_sem=send_sems.at[a],
            recv_sem=recv_sems.at[a], device_id=(x, y, 1 - c), device_id_type=MESH) for a in range(n)]
        for cp in give:
            cp.start()
        for cp in give:
            cp.wait_recv()
        for cp in give:
            cp.wait_send()

    return list(pl.pallas_call(
        body, name=name, out_shape=[_S((N_CHIP,) + s.shape[1:], s.dtype) for s in slots], in_specs=[_ANY] * n, out_specs=[_ANY] * n,
        scratch_shapes=[pltpu.SemaphoreType.DMA((n,)), pltpu.SemaphoreType.DMA((n,))],
    )(*slots))


def _chip_sum(slots, sib, core, *, name):
    _, R, W = slots.shape
    tr = _pick(R, max(16, EW_BLOCK_ELEMS // W), align=16)

    def body(core_ref, x_ref, s_ref, o_ref):
        o_ref[...] = (x_ref[...].astype(f32) + s_ref[...].astype(f32)).astype(o_ref.dtype)

    blk = pl.BlockSpec((None, tr, W), lambda t, i, core_ref: (t, i, 0))
    return pl.pallas_call(
        body, name=name, out_shape=_S(sib.shape, slots.dtype),
        grid_spec=pltpu.PrefetchScalarGridSpec(
            num_scalar_prefetch=1, grid=(N_CHIP, R // tr),
            in_specs=[pl.BlockSpec((None, tr, W), lambda t, i, core_ref: (N_CHIP * core_ref[0] + t, i, 0)), blk], out_specs=blk),
        compiler_params=_cparams(("parallel", "parallel")),
    )(core, slots, sib)


def _chip_out_shapes(parts):
    return [_S((N_CHIP - 1,) + p.shape[1:], p.dtype) for p in parts]


def _chip_sems(n):
    return [pltpu.SemaphoreType.DMA((3 * n,)), pltpu.SemaphoreType.DMA((3 * n,))]


def _chip_copies(p_refs, out_refs, send_sems, recv_sems):
    x, y, c = _peers()
    copies = []
    for j in range(1, N_CHIP):
        tx, ty = x ^ (j >> 1), y ^ (j & 1)
        for a in range(len(p_refs)):
            copies.append(pltpu.make_async_remote_copy(
                src_ref=p_refs[a].at[2 * tx + ty], dst_ref=out_refs[a].at[j - 1], send_sem=send_sems.at[3 * a + j - 1],
                recv_sem=recv_sems.at[3 * a + j - 1], device_id=(tx, ty, c), device_id_type=MESH))
    return copies


def _start_all(copies):
    for cp in copies:
        cp.start()


def _wait_all(copies):
    for cp in copies:
        cp.wait_recv()
    for cp in copies:
        cp.wait_send()


class _Carry(NamedTuple):
    ins: tuple = ()
    out_shapes: tuple = ()
    sems: tuple = ()
    start: Callable = None
    finish: Callable = None


_NO_CARRY = _Carry()


def _carry_chips(parts):
    return _Carry(tuple(parts), tuple(_chip_out_shapes(parts)), tuple(_chip_sems(len(parts))),
                  lambda i, o, s: _start_all(_chip_copies(i, o, *s)), lambda i, o, s: _wait_all(_chip_copies(i, o, *s)))


def _carry_gather(shards):
    n = len(shards)
    sems = (pltpu.SemaphoreType.DMA((7 * n,)), pltpu.SemaphoreType.DMA((7 * n,)), pltpu.SemaphoreType.DMA((n,)))
    return _Carry(tuple(shards), tuple(_S((N_DEV,) + s.shape, s.dtype) for s in shards), sems, _gather_start, _gather_finish)


def _carry_split(carry, refs, n_in, n_out):
    ci, co, cs = len(carry.ins), len(carry.out_shapes), len(carry.sems)
    refs = list(refs)
    own_in, c_in = refs[:n_in], refs[n_in:n_in + ci]
    own_out, c_out = refs[n_in + ci:n_in + ci + n_out], refs[n_in + ci + n_out:n_in + ci + n_out + co]
    rest = refs[n_in + ci + n_out + co:]
    own_scratch, c_sems = rest[:len(rest) - cs], rest[len(rest) - cs:]
    return own_in + own_out + own_scratch, c_in, c_out, c_sems


def _carry_start(carry, c_in, c_out, c_sems, first):
    if carry.ins:
        @pl.when(first)
        def _():
            carry.start(c_in, c_out, c_sems)


def _carry_finish(carry, c_in, c_out, c_sems, last):
    if carry.ins:
        @pl.when(last)
        def _():
            carry.finish(c_in, c_out, c_sems)


def _sum_slots(stack, *, name):
    n, R, W = stack.shape
    tr = _pick(R, 1024, align=8)

    def body(s_ref, o_ref):
        acc = s_ref[0]
        for k in range(1, n):
            acc = acc + s_ref[k]
        o_ref[...] = acc

    return pl.pallas_call(
        body, name=name, grid=(R // tr,), in_specs=[pl.BlockSpec((n, tr, W), lambda i: (0, i, 0))],
        out_specs=pl.BlockSpec((tr, W), lambda i: (i, 0)), out_shape=_S((R, W), f32), compiler_params=_cparams(("parallel",)),
    )(stack)


def _adamw_math(gv, wv, mv, vv):
    c1 = 1.0 / (1.0 - ADAM_B1 ** ADAM_STEP)
    c2 = 1.0 / (1.0 - ADAM_B2 ** ADAM_STEP)
    nm = ADAM_B1 * mv + (1.0 - ADAM_B1) * gv
    nv = ADAM_B2 * vv + (1.0 - ADAM_B2) * jnp.square(gv)
    return -ADAM_LR * ((nm * c1) / (jnp.sqrt(nv * c2) + ADAM_EPS) + ADAM_WD * wv), nm, nv


def _adamw(g, w, m, v, *, name):
    R, W = w.shape
    tr = _pick(R, max(8, (1 << 20) // (4 * W)), align=8)

    def body(g_ref, w_ref, m_ref, v_ref, d_ref, nm_ref, nv_ref):
        d_ref[...], nm_ref[...], nv_ref[...] = _adamw_math(g_ref[...], w_ref[...], m_ref[...], v_ref[...])

    sp = pl.BlockSpec((tr, W), lambda i: (i, 0))
    return pl.pallas_call(
        body, name=name, grid=(R // tr,), in_specs=[sp] * 4, out_specs=[sp] * 3, out_shape=[_S((R, W), f32)] * 3,
        compiler_params=_cparams(("parallel",)),
    )(g, w, m, v)


def _reduce_adamw(own, arrived, chip, w, m, v, *, name):
    n, R, W = arrived.shape
    tr = _pick(R, max(16, EW_BLOCK_ELEMS // (4 * W)), align=16)

    def body(chip_ref, o_ref, p_ref, w_ref, m_ref, v_ref, g_ref, d_ref, nm_ref, nv_ref):
        gv = o_ref[...].astype(f32)
        for k in range(n):
            gv = gv + p_ref[k].astype(f32)
        g_ref[...] = gv
        d_ref[...], nm_ref[...], nv_ref[...] = _adamw_math(gv, w_ref[...], m_ref[...], v_ref[...])

    sp = pl.BlockSpec((tr, W), lambda i, chip_ref: (i, 0))
    return pl.pallas_call(
        body, name=name, out_shape=[_S((R, W), f32)] * 4,
        grid_spec=pltpu.PrefetchScalarGridSpec(
            num_scalar_prefetch=1, grid=(R // tr,),
            in_specs=[pl.BlockSpec((None, tr, W), lambda i, chip_ref: (chip_ref[0], i, 0)),
                      pl.BlockSpec((n, tr, W), lambda i, chip_ref: (0, i, 0))] + [sp] * 3, out_specs=[sp] * 4),
        compiler_params=_cparams(("parallel",)),
    )(chip, own, arrived, w, m, v)


def _pieces(seg_start, seg_len, shard_w):
    out, col = [], seg_start
    while col < seg_start + seg_len:
        k, a = divmod(col, shard_w)
        n = min(shard_w - a, seg_start + seg_len - col)
        out.append((k, a, col - seg_start, n))
        col += n
    return out


def _unshard_w_in(g, seg_lens, *, name):
    _, D, w = g.shape
    starts = [sum(seg_lens[:i]) for i in range(len(seg_lens))]
    widths = [max(n, LANES) for n in seg_lens]
    tm = _pick(D, 256, align=16)

    def body(g_ref, *o_refs):
        for o_ref, s0, n in zip(o_refs, starts, seg_lens):
            if n < o_ref.shape[1]:
                o_ref[...] = jnp.zeros_like(o_ref)
            for k, a, off, m in _pieces(s0, n, w):
                o_ref[:, off:off + m] = g_ref[k, :, a:a + m]

    return pl.pallas_call(
        body, name=name, grid=(D // tm,), in_specs=[pl.BlockSpec((N_DEV, tm, w), lambda i: (0, i, 0))],
        out_specs=[pl.BlockSpec((tm, wd), lambda i: (i, 0)) for wd in widths], out_shape=[_S((D, wd), g.dtype) for wd in widths],
        compiler_params=_cparams(("parallel",)),
    )(g)


def _unshard_pair(g1, g2, *, name):
    _, D, w = g1.shape
    tm = _pick(D, 256, align=16)

    def body(a_ref, b_ref, o_ref):
        for i, g_ref in enumerate((a_ref, b_ref)):
            for k in range(N_DEV):
                off = (i * N_DEV + k) * w
                o_ref[:, off:off + w] = g_ref[k]

    blk = pl.BlockSpec((N_DEV, tm, w), lambda i: (0, i, 0))
    return pl.pallas_call(
        body, name=name, grid=(D // tm,), in_specs=[blk, blk], out_specs=pl.BlockSpec((tm, 2 * N_DEV * w), lambda i: (i, 0)),
        out_shape=_S((D, 2 * N_DEV * w), g1.dtype), compiler_params=_cparams(("parallel",)),
    )(g1, g2)


def _reshard_pair(dw, *, name):
    D, w = dw.shape[0], dw.shape[1] // (2 * N_DEV)
    tm = _pick(D, 128, align=16)

    def body(g_ref, a_ref, b_ref):
        for i, o_ref in enumerate((a_ref, b_ref)):
            for k in range(N_DEV):
                off = (i * N_DEV + k) * w
                o_ref[_slot_of(k)] = g_ref[:, off:off + w].astype(o_ref.dtype)

    blk = pl.BlockSpec((N_DEV, tm, w), lambda i: (0, i, 0))
    return pl.pallas_call(
        body, name=name, grid=(D // tm,), in_specs=[pl.BlockSpec((tm, dw.shape[1]), lambda i: (i, 0))], out_specs=[blk, blk],
        out_shape=[_S((N_DEV, D, w), bf16)] * 2, compiler_params=_cparams(("parallel",)),
    )(dw)


def _reshard_w_in(grads, seg_lens, w, *, name):
    D = grads[0].shape[0]
    starts = [sum(seg_lens[:i]) for i in range(len(seg_lens))]
    tm = _pick(D, 128, align=16)

    def body(*refs):
        o_ref = refs[-1]
        for g_ref, s0, n in zip(refs[:-1], starts, seg_lens):
            for k, a, off, m in _pieces(s0, n, w):
                o_ref[_slot_of(k), :, a:a + m] = g_ref[:, off:off + m].astype(o_ref.dtype)

    return pl.pallas_call(
        body, name=name, grid=(D // tm,), in_specs=[pl.BlockSpec((tm, g.shape[1]), lambda i: (i, 0)) for g in grads],
        out_specs=pl.BlockSpec((N_DEV, tm, w), lambda i: (0, i, 0)), out_shape=_S((N_DEV, D, w), bf16),
        compiler_params=_cparams(("parallel",)),
    )(*grads)


def _pad_flat(a, mult):
    a = a.reshape(-1)
    n = -(-a.shape[0] // mult) * mult
    return a if n == a.shape[0] else jnp.pad(a, (0, n - a.shape[0]))


def _pad_cols(a, mult):
    n = -(-a.shape[1] // mult) * mult
    return a if n == a.shape[1] else jnp.pad(a, ((0, 0), (0, n - a.shape[1])))


def _block_diag(t):
    nblk, g, P, Q = t.shape
    eye = jnp.eye(g, dtype=t.dtype)
    return (t[:, :, :, None, :] * eye[None, :, None, :, None]).reshape(nblk, g * P, g * Q)


def _block_diag_t(w, P, Q):
    nblk = w.shape[0]
    g = w.shape[1] // P
    eye = jnp.eye(g, dtype=w.dtype)
    return (w.reshape(nblk, g, P, g, Q) * eye[None, :, None, :, None]).sum(axis=3)


_COLS = ("ffn1_w_gate", "ffn1_w_up", "ffn2_w_gate", "ffn2_w_up")
_ROWS = ("ffn1_w_down", "ffn2_w_down", "s5_w_glu", "w_proj_s5", "w_out", "w_proj_ssd")
_BIG = _COLS + _ROWS + ("w_in", "conv_w")
_SMALL = ("ffn1_norm", "mix_norm", "conv_b", "s5_A_re", "s5_A_im", "s5_log_dt", "s5_B_re", "s5_B_im", "s5_C_re", "s5_C_im",
          "s5_D", "s5_b_glu", "ssd_A_log", "ssd_dt_bias", "ssd_D", "ssd_norm", "b_gate", "ffn2_norm", "final_norm")
_WEIGHTS = ("ffn1_norm", "ffn1_w_gate", "ffn1_w_up", "ffn1_w_down", "mix_norm", "w_in", "conv_w", "conv_b", "s5_A_re", "s5_A_im",
            "s5_log_dt", "s5_B_re", "s5_B_im", "s5_C_re", "s5_C_im", "s5_D", "s5_w_glu", "s5_b_glu", "ssd_A_log", "ssd_dt_bias",
            "ssd_D", "ssd_norm", "w_proj_s5", "w_proj_ssd", "b_gate", "w_out", "ffn2_norm", "ffn2_w_gate", "ffn2_w_up",
            "ffn2_w_down", "final_norm")
def _with_carry(res, carry):
    return res if carry else (res, [])


def _ffn_fwd(x, n, wgu, wd_of, tag, carries=(None, None, None)):
    D = x.shape[1]
    h = _rows(_f_rmsnorm, [x], [n], [(D, bf16)], name=tag + "_norm")[0]
    ab, got0 = _with_carry(_mm(h, wgu, carry=carries[0], name=tag + "_gate_up"), carries[0])
    wd = wd_of(got0)
    F = wd.shape[0]
    (c,), got1 = _with_carry(_rows(_f_swiglu, [ab], [], [(F, bf16)], carry=carries[1], name=tag + "_act"), carries[1])
    y, got2 = _with_carry(_mm(c, wd, scale=0.5, add=x, carry=carries[2], name=tag + "_down"), carries[2])
    return y, (x, n, h, ab, c), (got0, got1, got2)


def _ffn_bwd(saved, wgu, wd, dy, tag, carry_after_dwd=None, carry_after_dwgu=None):
    x, n, h, ab, c = saved
    F = wd.shape[0]
    def act_bwd(dc, ab_):
        return jax.vjp(lambda t: _f_swiglu(t)[0], ab_)[1](dc)[0]

    dab = _mm(dy, wd, tb=True, scale=0.5, tm=EPI_TM, epi=(act_bwd, ab, 2 * F), out_dtype=bf16, name=tag + "_d_act")
    dwd = _mm(c, dy, ta=True, o_blk="m", o_slots=True, tm=F // 2, out_dtype=bf16, scale=0.5, name=tag + "_d_wdown")
    carry_w = carry_after_dwd(dwd) if carry_after_dwd else None
    dwgu, arr_w = _with_carry(_mm(h, dab, ta=True, carry=carry_w, name=tag + "_d_wgu"), carry_w)
    dwg, dwu = _reshard_pair(dwgu, name=tag + "_reshard_d_wgu")
    carry_h = carry_after_dwgu(dwg, dwu) if carry_after_dwgu else None
    dh, arr_h = _with_carry(_mm(dab, wgu, tb=True, carry=carry_h, name=tag + "_d_h"), carry_h)
    (dx,), (dn,) = _rows_bwd(_f_rmsnorm, [x], [n], [dh], name=tag + "_norm_bwd", want_rows=[0], adds={0: dy})
    return dx, dn, dwg, dwu, dwd, arr_w, arr_h


def kernel(x, ffn1_norm, ffn1_w_gate, ffn1_w_up, ffn1_w_down, mix_norm, w_in, conv_w, conv_b, s5_A_re, s5_A_im, s5_log_dt, s5_B_re, s5_B_im, s5_C_re, s5_C_im, s5_D, s5_w_glu, s5_b_glu, ssd_A_log, ssd_dt_bias, ssd_D, ssd_norm, w_proj_s5, w_proj_ssd, b_gate, w_out, ffn2_norm, ffn2_w_gate, ffn2_w_up, ffn2_w_down, final_norm, loss_target, m_ffn1_norm, m_ffn1_w_gate, m_ffn1_w_up, m_ffn1_w_down, m_mix_norm, m_w_in, m_conv_w, m_conv_b, m_s5_A_re, m_s5_A_im, m_s5_log_dt, m_s5_B_re, m_s5_B_im, m_s5_C_re, m_s5_C_im, m_s5_D, m_s5_w_glu, m_s5_b_glu, m_ssd_A_log, m_ssd_dt_bias, m_ssd_D, m_ssd_norm, m_w_proj_s5, m_w_proj_ssd, m_b_gate, m_w_out, m_ffn2_norm, m_ffn2_w_gate, m_ffn2_w_up, m_ffn2_w_down, m_final_norm, v_ffn1_norm, v_ffn1_w_gate, v_ffn1_w_up, v_ffn1_w_down, v_mix_norm, v_w_in, v_conv_w, v_conv_b, v_s5_A_re, v_s5_A_im, v_s5_log_dt, v_s5_B_re, v_s5_B_im, v_s5_C_re, v_s5_C_im, v_s5_D, v_s5_w_glu, v_s5_b_glu, v_ssd_A_log, v_ssd_dt_bias, v_ssd_D, v_ssd_norm, v_w_proj_s5, v_w_proj_ssd, v_b_gate, v_w_out, v_ffn2_norm, v_ffn2_w_gate, v_ffn2_w_up, v_ffn2_w_down, v_final_norm):
    P = dict(locals())
    T, D = x.shape[1], x.shape[2]
    x0, tgt = x[0], loss_target[0]
    sh = {k: P[k][0] for k in _BIG}

    send = {k: (sh[k] if k == "conv_w" else sh[k].astype(bf16)) for k in _BIG}
    W = {}

    def gather_in(keys):
        return _carry_gather([send[k] for k in keys])

    first_keys = ("ffn1_w_gate", "ffn1_w_up", "conv_w")
    W.update(zip(first_keys, _all_gather([send[k] for k in first_keys], name="gather_weights_first")))
    whole = lambda k: W[k].reshape(-1, D)
    conv_w_full = W["conv_w"].transpose(1, 0, 2).reshape(CONV_K, -1)

    d_inner = N_DEV * sh["w_proj_ssd"].shape[0]
    conv_dim = conv_w_full.shape[1]
    H = ssd_A_log.shape[1]
    G = (conv_dim - d_inner) // (2 * SSD_STATE)
    hpg = H // G
    nc = T // CHUNK
    Gs = D // S5_GROUP
    nblk = Gs // S5_GPB
    NS = Gs * S5_STATE
    seg_lens = (D, d_inner, conv_dim, H, 2 * D)

    cuts = [0, D // 3 // 16 * 16, D // 3 // 16 * 16 + 3 * D // 8 // 16 * 16, D]
    win_rows = [send["w_in"][a_:b_] for a_, b_ in zip(cuts[:-1], cuts[1:])]
    wgu1 = _unshard_pair(W["ffn1_w_gate"], W["ffn1_w_up"], name="unshard_ffn1_gate_up")

    def ffn1_down(got):
        W["ffn1_w_down"] = got[0]
        return whole("ffn1_w_down")

    x1, sv1, (got0, got1, got2) = _ffn_fwd(
        x0, ffn1_norm, wgu1, ffn1_down, "ffn1",
        carries=(_carry_gather([send["ffn1_w_down"], win_rows[0]]), _carry_gather([win_rows[1]]), _carry_gather([win_rows[2]])))
    ffn1_w = (wgu1, whole("ffn1_w_down"))
    W["w_in"] = jnp.concatenate([got0[1], got1[0], got2[0]], axis=1)
    w_u, w_z, w_xbc, w_dt, w_gl = _unshard_w_in(W["w_in"], seg_lens, name="unshard_w_in")
    h2 = _rows(_f_rmsnorm, [x1], [mix_norm], [(D, bf16)], name="mix_norm")[0]
    u_p = _mm(h2, w_u, o_seg=True, name="in_u")
    z = _mm(h2, w_z, name="in_z")
    xbc = _mm(h2, w_xbc, name="in_xbc")
    gl = _mm(h2, w_gl, name="in_gate")
    dtr = _mm(h2, w_dt, name="in_dt")

    rep = lambda a: jnp.repeat(a, S5_GROUP, axis=0)
    lr, li, ldt = s5_A_re[0], s5_A_im[0], s5_log_dt[0].reshape(Gs, 1)
    brt = s5_B_re[0].transpose(0, 2, 1).reshape(Gs * S5_GROUP, S5_STATE)
    bit = s5_B_im[0].transpose(0, 2, 1).reshape(Gs * S5_GROUP, S5_STATE)
    prep_args = (lr, li, ldt, rep(lr), rep(li), rep(ldt), brt, bit)
    ar, ai, bbrt, bbit = _s5_prep(prep_args, name="s5_prep")
    a_r, a_i = ar.reshape(1, NS), ai.reshape(1, NS)
    wb_r = _block_diag(bbrt.reshape(nblk, S5_GPB, S5_GROUP, S5_STATE)).astype(bf16)
    wb_i = _block_diag(bbit.reshape(nblk, S5_GPB, S5_GROUP, S5_STATE)).astype(bf16)
    c4r = s5_C_re[0].reshape(nblk, S5_GPB, S5_GROUP, S5_STATE).transpose(0, 1, 3, 2)
    c4i = s5_C_im[0].reshape(nblk, S5_GPB, S5_GROUP, S5_STATE).transpose(0, 1, 3, 2)
    wc_r, wc_i = _block_diag(c4r).astype(bf16), _block_diag(c4i).astype(bf16)
    mix_keys = ("s5_w_glu", "w_proj_s5", "w_proj_ssd", "w_out")
    sl_r, sl_i, p_r, p_i, got = _s5_local_scan(u_p, wb_r, wb_i, a_r, a_i, reverse=False, carry=gather_in(mix_keys), name="s5_scan")
    W.update(zip(mix_keys, got))
    w_glu, w_p5, w_pssd, w_o = whole("s5_w_glu"), whole("w_proj_s5"), whole("w_proj_ssd"), whole("w_out")
    c_r, c_i = _s5_carry(sl_r[T - NSEG:], sl_i[T - NSEG:], p_r, p_i, reverse=False, name="s5_carry")
    s_r, s_i, ylin = _s5_fix_out(sl_r, sl_i, a_r, a_i, c_r, c_i, wc_r, wc_i, name="s5_fix_out")
    g5 = _rows(_f_s5_post, [ylin, u_p], [s5_D], [(D, f32)], name="s5_gelu")[0]
    v5 = _mm(g5, w_glu, name="s5_glu_mm")
    o5 = _rows(_f_glu, [g5, v5], [s5_b_glu], [(D, bf16)], name="s5_glu")[0]
    p5 = _mm(o5, w_p5, a_seg=True, name="proj_s5")

    xc = _conv_fwd(xbc, conv_w_full, conv_b, name="conv")
    bias_p, alog_p = _pad_cols(ssd_dt_bias, LANES), _pad_cols(ssd_A_log, LANES)
    expand = (lax.broadcasted_iota(jnp.int32, (LANES, d_inner), 1) // HEADDIM
              == lax.broadcasted_iota(jnp.int32, (LANES, d_inner), 0)).astype(f32)
    dt_p, da_p, dt4, a4 = _rows(_f_dt_expand, [dtr], [bias_p, alog_p, expand],
                                [(LANES, f32), (LANES, f32), (d_inner, f32), (d_inner, f32)], name="ssd_dt")
    row_l = lambda a: a[:, :H].reshape(nc, CHUNK, G, hpg).transpose(2, 0, 3, 1).reshape(G, nc, hpg * CHUNK)
    ssd_in = (xc, dt4, a4, row_l(dt_p), row_l(da_p), jnp.repeat(ssd_D, HEADDIM, axis=1), _ssd_consts(hpg))
    ffn2_keys = ("ffn2_w_gate", "ffn2_w_up", "ffn2_w_down")
    y_ssd, hs, got = _ssd2_fwd(*ssd_in, d_inner=d_inner, carry=gather_in(ffn2_keys), name="ssd")
    W.update(zip(ffn2_keys, got))
    ffn2_w = (_unshard_pair(W["ffn2_w_gate"], W["ffn2_w_up"], name="unshard_ffn2_gate_up"), whole("ffn2_w_down"))
    yn = _rows(_f_gated_norm, [y_ssd, z], [ssd_norm], [(d_inner, bf16)], name="ssd_gated_norm")[0]
    pssd = _mm(yn, w_pssd, name="proj_ssd")

    merged = _rows(_f_merge, [gl, p5, pssd], [b_gate], [(D, bf16)], name="merge")[0]
    x2 = _mm(merged, w_o, add=x1, name="out_proj")
    x3, sv2, _ = _ffn_fwd(x2, ffn2_norm, ffn2_w[0], lambda _: ffn2_w[1], "ffn2")
    lossv, dx3, d_final = _loss_stage(x3, tgt, final_norm.reshape(1, D), name="loss")

    gw = {}
    gs = {"final_norm": d_final}
    slot_mm = lambda a_, b_, name, **kw: _mm(a_, b_, ta=True, o_blk="m", o_slots=True, out_dtype=bf16, name=name, **kw)
    core = lax.axis_index("c").astype(jnp.int32).reshape(1)
    chip = (2 * lax.axis_index("x") + lax.axis_index("y")).astype(jnp.int32).reshape(1)
    chip_sums, arrived = {}, {}

    def level1(keys, tag):
        sib = _exchange_sibling([gw[k] for k in keys], name="exchange_sibling_" + tag)
        for k, s_ in zip(keys, sib):
            chip_sums[k] = _chip_sum(gw[k], s_, core, name="chip_sum_" + k)
        return [chip_sums[k] for k in keys]

    dx2, gs["ffn2_norm"], gw["ffn2_w_gate"], gw["ffn2_w_up"], gw["ffn2_w_down"], _, _ = _ffn_bwd(sv2, *ffn2_w, dx3, "ffn2")

    dmerged = _mm(dx2, w_o, tb=True, name="d_merged")
    gw["w_out"] = slot_mm(merged, dx2, "d_w_out")
    (dgl, dp5, dpssd), (gs["b_gate"],) = _rows_bwd(_f_merge, [gl, p5, pssd], [b_gate], [dmerged], name="merge_bwd", want_rows=[0, 1, 2])

    dyn = _mm(dpssd, w_pssd, tb=True, name="d_yn")
    gw["w_proj_ssd"] = slot_mm(yn, dpssd, "d_w_proj_ssd")
    group_a = ("ffn2_w_gate", "ffn2_w_up", "ffn2_w_down", "w_out", "w_proj_ssd")
    parts_a = level1(group_a, "a")
    (dyssd, dz), (gs["ssd_norm"],) = _rows_bwd(_f_gated_norm, [y_ssd, z], [ssd_norm], [dyn], name="ssd_gated_norm_bwd", want_rows=[0, 1])
    dxs, dbm, dcm, ddtc, ddac, ddtw, ddaw, ddh = _ssd2_bwd(*ssd_in, hs, dyssd, d_inner=d_inner, name="ssd_bwd")

    def fold(col, row):
        col = col.reshape(T, G, LANES)[:, :, :hpg].reshape(T, H)
        row = row.reshape(G, nc, hpg, CHUNK).transpose(1, 3, 0, 2).reshape(T, H)
        return _pad_cols(col + row, LANES)

    (ddtr,), (dbias_p, dalog_p) = _rows_bwd(_f_dt, [dtr], [bias_p, alog_p], [fold(ddtc, ddtw), fold(ddac, ddaw)], name="ssd_dt_bwd", want_rows=[0])
    gs["ssd_dt_bias"], gs["ssd_A_log"], gs["ssd_D"] = dbias_p[:, :H], dalog_p[:, :H], ddh[:, 0, :hpg].reshape(1, H)
    dxbc, d_conv_w, gs["conv_b"], arr = _conv_bwd(
        xbc, conv_w_full, conv_b, [dxs, dbm, dcm], carry=_carry_chips(parts_a), name="conv_bwd")
    arrived.update(zip(group_a, arr))
    cwk = sh["conv_w"].shape[1]
    gw["conv_w"] = d_conv_w.reshape(CONV_K, N_CHIP, 2, cwk).transpose(2, 1, 0, 3).reshape(N_DEV, CONV_K, cwk)

    do5 = _mm(dp5, w_p5, tb=True, o_seg=True, name="d_o5")
    gw["w_proj_s5"] = slot_mm(o5, dp5, "d_w_proj_s5", a_seg=True)
    (dg5a, dv5), (gs["s5_b_glu"],) = _rows_bwd(_f_glu, [g5, v5], [s5_b_glu], [do5], name="s5_glu_bwd", want_rows=[0, 1])
    dg5 = _mm(dv5, w_glu, tb=True, add=dg5a, name="d_g5")
    gw["s5_w_glu"] = slot_mm(g5, dv5, "d_w_glu")
    (dylin, du_a), (gs["s5_D"],) = _rows_bwd(_f_s5_post, [ylin, u_p], [s5_D], [dg5], name="s5_gelu_bwd", want_rows=[0, 1])
    wct_r, wct_i = wc_r.transpose(0, 2, 1), -wc_i.transpose(0, 2, 1)
    ql_r, ql_i, _, _ = _s5_local_scan(dylin, wct_r, wct_i, a_r, -a_i, reverse=True, powers=False, name="s5_scan_bwd")
    cb_r, cb_i = _s5_carry(ql_r[:NSEG], ql_i[:NSEG], p_r, -p_i, reverse=True, name="s5_carry_bwd")
    tc = min(S5_TC, T)

    def before_blocks(s):
        last = s.reshape(T // tc, tc, NS)[:, tc - NSEG:, :]
        wrap = jnp.concatenate([jnp.zeros((1, 1, NS), f32), last[-1:, : NSEG - 1, :]], axis=1)
        return jnp.concatenate([wrap, last[:-1]], axis=0)

    du_p, dwb_r, dwb_i, dwc_r, dwc_i, d_ar, d_ai = _s5_fix_bwd(
        ql_r, ql_i, a_r, -a_i, cb_r, cb_i, s_r, s_i, before_blocks(s_r), before_blocks(s_i), u_p, dylin, du_a, wb_r, wb_i, name="s5_fix_bwd")
    unblk = lambda w: _block_diag_t(w, S5_GROUP, S5_STATE).reshape(Gs * S5_GROUP, S5_STATE)
    rsum = jnp.repeat(jnp.eye(Gs, dtype=f32), S5_GROUP, axis=1)
    d_lr, d_li, d_ldt, d_brt, d_bit = _s5_prep_bwd(
        prep_args, (d_ar.reshape(Gs, S5_STATE), d_ai.reshape(Gs, S5_STATE), unblk(dwb_r), unblk(dwb_i)), rsum, name="s5_prep_bwd")
    gs["s5_A_re"], gs["s5_A_im"], gs["s5_log_dt"] = d_lr, d_li, d_ldt.reshape(1, Gs)
    gs["s5_B_re"] = d_brt.reshape(Gs, S5_GROUP, S5_STATE).transpose(0, 2, 1)
    gs["s5_B_im"] = d_bit.reshape(Gs, S5_GROUP, S5_STATE).transpose(0, 2, 1)
    gs["s5_C_re"] = _block_diag_t(dwc_r, S5_STATE, S5_GROUP).transpose(0, 1, 3, 2).reshape(Gs, S5_GROUP, S5_STATE)
    gs["s5_C_im"] = _block_diag_t(dwc_i, S5_STATE, S5_GROUP).transpose(0, 1, 3, 2).reshape(Gs, S5_GROUP, S5_STATE)

    d_w_in = [_mm(h2, du_p, ta=True, b_seg=True, name="d_w_u"), _mm(h2, dz, ta=True, name="d_w_z"), _mm(h2, dxbc, ta=True, name="d_w_xbc"),
              _mm(h2, ddtr, ta=True, name="d_w_dt"), _mm(h2, dgl, ta=True, name="d_w_gate")]
    gw["w_in"] = _reshard_w_in(d_w_in, seg_lens, sh["w_in"].shape[1], name="reshard_d_w_in")
    group_c = ("w_proj_s5", "s5_w_glu", "conv_w")
    parts_c = level1(group_c + ("w_in",), "c")
    c1 = int(D * 0.45) // 16 * 16
    c2 = c1 + D // 4 // 16 * 16
    win = [parts_c[3][:, :c1], parts_c[3][:, c1:c2], parts_c[3][:, c2:]]
    dh2 = _mm(du_p, w_u, tb=True, a_seg=True, name="d_h2_u")
    dh2, arr = _mm(dz, w_z, tb=True, add=dh2, carry=_carry_chips(parts_c[:3]), name="d_h2_z")
    arrived.update(zip(group_c, arr))
    dh2, (arr0,) = _mm(dxbc, w_xbc, tb=True, add=dh2, carry=_carry_chips([win[0]]), name="d_h2_xbc")
    dh2, (arr1,) = _mm(dgl, w_gl, tb=True, add=dh2, carry=_carry_chips([win[1]]), name="d_h2_gate")
    dh2 = _mm(ddtr, w_dt, tb=True, add=dh2, name="d_h2_dt")
    (dx1,), (gs["mix_norm"],) = _rows_bwd(_f_rmsnorm, [x1], [mix_norm], [dh2], name="mix_norm_bwd", want_rows=[0], adds={0: dx2})

    def carry_ffn1_down(dwd):
        gw["ffn1_w_down"] = dwd
        return _carry_chips(level1(("ffn1_w_down",), "d") + [win[2]])

    def carry_ffn1_gate_up(dwg, dwu):
        gw["ffn1_w_gate"], gw["ffn1_w_up"] = dwg, dwu
        return _carry_chips(level1(("ffn1_w_gate", "ffn1_w_up"), "e"))

    dx0, gs["ffn1_norm"], _, _, _, arr_w, arr_h = _ffn_bwd(
        sv1, *ffn1_w, dx1, "ffn1", carry_after_dwd=carry_ffn1_down, carry_after_dwgu=carry_ffn1_gate_up)
    arrived["ffn1_w_down"], arr2 = arr_w
    arrived["ffn1_w_gate"], arrived["ffn1_w_up"] = arr_h
    arrived["w_in"] = jnp.concatenate([arr0, arr1, arr2], axis=1)

    small_shapes = {k: (P[k][0].shape if P[k].ndim > 1 else P[k].shape) for k in _SMALL}
    pack = lambda d: jnp.concatenate([_pad_flat(d[k], TILE_ELEMS) for k in _SMALL]).reshape(-1, LANES)
    gsmall = _sum_slots(_all_gather([pack(gs)], name="gather_small_grads")[0], name="sum_small_grads")
    snum = {k: math.prod(small_shapes[k]) for k in _SMALL}
    ssz = {k: -(-snum[k] // TILE_ELEMS) * TILE_ELEMS for k in _SMALL}

    grads, delta, new_m, new_v = {}, {}, {}, {}
    for k in _BIG:
        grads[k], delta[k], new_m[k], new_v[k] = _reduce_adamw(
            chip_sums[k], arrived[k], chip, P[k][0], P["m_" + k][0], P["v_" + k][0], name="adamw_" + k)
    d_s, m_s, v_s = _adamw(gsmall, pack({k: P[k] for k in _SMALL}), pack({k: P["m_" + k] for k in _SMALL}),
                           pack({k: P["v_" + k] for k in _SMALL}), name="adamw_small")
    off = 0
    gflat, dflat, mflat, vflat = gsmall.reshape(-1), d_s.reshape(-1), m_s.reshape(-1), v_s.reshape(-1)
    for k in _SMALL:
        n = snum[k]
        grads[k], delta[k], new_m[k], new_v[k] = (a[off:off + n] for a in (gflat, dflat, mflat, vflat))
        off += ssz[k]

    loss = lax.psum(lossv[0, 0], ("x", "y", "c"))
    out = [loss, dx0.reshape(x.shape)]
    for d in (grads, delta, new_m, new_v):
        out += [d[k].reshape(P[k].shape) for k in _WEIGHTS]
    return tuple(out)
```

```python
import math
from typing import Callable, NamedTuple

import jax
import jax.numpy as jnp
from jax import lax
from jax.experimental import pallas as pl
from jax.experimental.pallas import tpu as pltpu

f32 = jnp.float32
bf16 = jnp.bfloat16
_S = jax.ShapeDtypeStruct

EPS = 1e-6
S5_GROUP = 16
S5_STATE = 64
HEADDIM = 64
SSD_STATE = 128
CHUNK = 64
CONV_K = 4
NSEG = 8
S5_GPB = 16
N_DEV = 8
LANES = 128
TILE_ELEMS = 8 * LANES

ADAM_LR = 0.001
ADAM_B1 = 0.9
ADAM_B2 = 0.999
ADAM_EPS = 1e-08
ADAM_WD = 0.01
ADAM_STEP = 10

VMEM_LIMIT = 56 * 1024 * 1024
MM_FULL_K = 3072
MM_MAX_TN = 3072
EPI_TM = 128
ROW_BLOCK_BYTES = 12 * 1024 * 1024
EW_BLOCK_ELEMS = 1 << 20
MESH = pl.DeviceIdType.MESH


def _cparams(sem=None):
    return pltpu.CompilerParams(dimension_semantics=sem, vmem_limit_bytes=VMEM_LIMIT)


def _pick(dim, pref, align=LANES):
    best = None
    t = align
    while t <= min(dim, pref):
        if dim % t == 0:
            best = t
        t += align
    return best or dim


def _slot_of(k):
    return (k & 1) * (N_DEV // 2) + (k >> 1)


def _mm(a, b, *, name, ta=False, tb=False, a_blk=None, b_blk=None, o_blk=None, o_slots=False, a_seg=False, b_seg=False,
        o_seg=False, tm=None, out_dtype=f32, scale=1.0, add=None, epi=None, carry=None):
    a2, b2 = a.shape[-2:], b.shape[-2:]
    Ma, Ka = (a2[1], a2[0]) if ta else a2
    Kb, Nb = (b2[1], b2[0]) if tb else b2
    M = Ma * (a.shape[0] if a_blk == "m" else 1)
    K = Ka * (a.shape[0] if a_blk == "k" else 1)
    N = Nb * (b.shape[0] if b_blk == "n" else 1)
    assert K == Kb * (b.shape[0] if b_blk == "k" else 1), (a.shape, b.shape, ta, tb, a_blk, b_blk)
    assert (a.ndim == 3) == (a_blk is not None) and (b.ndim == 3) == (b_blk is not None)
    tm = Ma if a_blk == "m" else (tm or _pick(M, 512))
    tn = Nb if b_blk == "n" else _pick(N, MM_MAX_TN)
    if a_blk == "k" or b_blk == "k":
        tk = Ka if a_blk == "k" else Kb
        assert tk == (Kb if b_blk == "k" else tk)
    else:
        tk = K if K <= MM_FULL_K else _pick(K, 1024 if ta else MM_FULL_K)
    if (a_seg and not ta) or o_seg:
        tm = M // NSEG
    if (a_seg and ta) or b_seg:
        tk = K // NSEG
    gm, gn, nk = M // tm, N // tn, K // tk
    assert not (add is not None and (o_seg or o_blk)) and not (o_blk and o_seg)

    if a_seg:
        assert a.ndim == 2
        a = a.reshape(a.shape[0] // NSEG, NSEG * a.shape[1])
        if ta:
            a_spec = pl.BlockSpec((tk, tm), lambda i, j, k: (0, k * (Ma // tm) + i))
        else:
            a_spec = pl.BlockSpec((tm, tk), lambda i, j, k: (0, i * (Ka // tk) + k))
    elif a.ndim == 3:
        lead = (lambda i, k: i) if a_blk == "m" else (lambda i, k: k)
        if ta:
            a_spec = pl.BlockSpec((None, tk, tm), lambda i, j, k: (lead(i, k), 0 if a_blk == "k" else k, 0 if a_blk == "m" else i))
        else:
            a_spec = pl.BlockSpec((None, tm, tk), lambda i, j, k: (lead(i, k), 0 if a_blk == "m" else i, 0 if a_blk == "k" else k))
    else:
        a_spec = pl.BlockSpec((tk, tm), lambda i, j, k: (k, i)) if ta else pl.BlockSpec((tm, tk), lambda i, j, k: (i, k))
    if b_seg:
        assert b.ndim == 2 and not tb
        b = b.reshape(b.shape[0] // NSEG, NSEG * b.shape[1])
        b_spec = pl.BlockSpec((tk, tn), lambda i, j, k: (0, k * (Nb // tn) + j))
    elif b.ndim == 3:
        lead = (lambda j, k: j) if b_blk == "n" else (lambda j, k: k)
        if tb:
            b_spec = pl.BlockSpec((None, tn, tk), lambda i, j, k: (lead(j, k), 0 if b_blk == "n" else j, 0 if b_blk == "k" else k))
        else:
            b_spec = pl.BlockSpec((None, tk, tn), lambda i, j, k: (lead(j, k), 0 if b_blk == "k" else k, 0 if b_blk == "n" else j))
    else:
        b_spec = pl.BlockSpec((tn, tk), lambda i, j, k: (j, k)) if tb else pl.BlockSpec((tk, tn), lambda i, j, k: (k, j))
    slot = _slot_of if o_slots else (lambda k: k)
    if o_blk == "n":
        assert gn == N_DEV or not o_slots
        o_shape, o_spec = (gn, M, tn), pl.BlockSpec((None, tm, tn), lambda i, j, k: (slot(j), i, 0))
    elif o_blk == "m" and o_slots and gm < N_DEV:
        rs = M // N_DEV
        per_tile = tm // rs
        assert per_tile % 2 == 0 and tm % rs == 0
        o_shape = (2, N_CHIP, rs, N)
        o_spec = pl.BlockSpec((2, per_tile // 2, rs, tn), lambda i, j, k: (0, i, 0, j))
    elif o_blk == "m":
        assert gm == N_DEV or not o_slots
        o_shape, o_spec = (gm, tm, N), pl.BlockSpec((None, tm, tn), lambda i, j, k: (slot(i), 0, j))
    elif o_seg:
        o_shape, o_spec = (tm, NSEG * N), pl.BlockSpec((tm, tn), lambda i, j, k: (0, i * (N // tn) + j))
    else:
        o_shape, o_spec = (M, N), pl.BlockSpec((tm, tn), lambda i, j, k: (i, j))
    dims = (((0 if ta else 1,), (1 if tb else 0,)), ((), ()))
    if epi is not None:
        assert gn == 1 and add is None and o_blk is None and not o_seg
        epi_fn, add, epi_w = epi
        o_shape, o_spec = (M, epi_w), pl.BlockSpec((tm, epi_w), lambda i, j, k: (i, 0))
    has_add = add is not None
    add_spec = pl.BlockSpec((tm, add.shape[1]), lambda i, j, k: (i, 0)) if epi is not None else o_spec

    carry = carry or _NO_CARRY
    n_in = 2 + has_add

    def body(*refs):
        own, c_in, c_out, c_sems = _carry_split(carry, refs, n_in, 1)
        a_ref, b_ref = own[0], own[1]
        add_ref = own[2] if has_add else None
        o_ref, acc_ref = own[-2], own[-1]
        i, j, k = pl.program_id(0), pl.program_id(1), pl.program_id(2)
        _carry_start(carry, c_in, c_out, c_sems, (i == 0) & (j == 0) & (k == 0))

        @pl.when(k == 0)
        def _():
            acc_ref[...] = jnp.zeros_like(acc_ref)

        acc_ref[...] += lax.dot_general(a_ref[...].astype(bf16), b_ref[...].astype(bf16), dims, preferred_element_type=f32)

        @pl.when(k == nk - 1)
        def _():
            r = acc_ref[...] * scale
            if epi is not None:
                r = epi_fn(r, add_ref[...].astype(f32))
            elif has_add:
                r = r + add_ref[...].astype(f32)
            if len(o_shape) == 4:
                rs = o_shape[2]
                for chip_l in range(o_ref.shape[1]):
                    for core in range(2):
                        dev = 2 * chip_l + core
                        o_ref[core, chip_l] = r[dev * rs:(dev + 1) * rs].astype(out_dtype)
            else:
                o_ref[...] = r.astype(out_dtype)

        _carry_finish(carry, c_in, c_out, c_sems, (i == gm - 1) & (j == gn - 1) & (k == nk - 1))

    ins = [a, b] + ([add] if has_add else []) + list(carry.ins)
    in_specs = [a_spec, b_spec] + ([add_spec] if has_add else []) + [_ANY] * len(carry.ins)
    res = pl.pallas_call(
        body, name=name, grid=(gm, gn, nk), in_specs=in_specs, out_specs=[o_spec] + [_ANY] * len(carry.out_shapes),
        out_shape=[_S(o_shape, out_dtype)] + list(carry.out_shapes),
        scratch_shapes=[pltpu.VMEM((tm, tn), f32)] + list(carry.sems),
        compiler_params=_cparams(("arbitrary",) * 3 if carry.ins else ("parallel", "parallel", "arbitrary")),
    )(*ins)
    out = res[0]
    if len(o_shape) == 4:
        out = out.reshape(N_DEV, o_shape[2], N)
    elif o_seg:
        out = out.reshape(M, N)
    return (out, list(res[1:])) if carry.ins else out


def _row_tile(T, widths):
    budget = ROW_BLOCK_BYTES
    tb = max(16, budget // (4 * sum(widths)))
    return _pick(T, tb, align=16)


def _rows(fn, rows, params, outs, *, name, carry=None):
    T = rows[0].shape[0]
    nr, npar = len(rows), len(params)
    tb = _row_tile(T, [r.shape[1] for r in rows] + [w for w, _ in outs])
    carry = carry or _NO_CARRY

    def body(*refs):
        own, c_in, c_out, c_sems = _carry_split(carry, refs, nr + npar, len(outs))
        _carry_start(carry, c_in, c_out, c_sems, pl.program_id(0) == 0)
        ins = [r[...].astype(f32) for r in own[: nr + npar]]
        res = fn(*ins)
        for o_ref, r in zip(own[nr + npar:], res):
            o_ref[...] = r.astype(o_ref.dtype)
        _carry_finish(carry, c_in, c_out, c_sems, pl.program_id(0) == T // tb - 1)

    in_specs = [pl.BlockSpec((tb, r.shape[1]), lambda i: (i, 0)) for r in rows]
    in_specs += [pl.BlockSpec(p.shape, lambda i: (0, 0)) for p in params]
    out_specs = [pl.BlockSpec((tb, w), lambda i: (i, 0)) for w, _ in outs]
    res = pl.pallas_call(
        body, name=name, grid=(T // tb,), in_specs=in_specs + [_ANY] * len(carry.ins),
        out_specs=out_specs + [_ANY] * len(carry.out_shapes), out_shape=[_S((T, w), d) for w, d in outs] + list(carry.out_shapes),
        scratch_shapes=list(carry.sems), compiler_params=_cparams(("arbitrary",) if carry.ins else ("parallel",)),
    )(*rows, *params, *carry.ins)
    return (tuple(res[:len(outs)]), list(res[len(outs):])) if carry.ins else tuple(res)


def _rows_bwd(fn, rows, params, cots, *, name, want_rows, row_dtypes=None, adds=None):
    T = rows[0].shape[0]
    nr, npar, nc = len(rows), len(params), len(cots)
    adds = adds or {}
    add_idx = sorted(adds)
    row_dtypes = row_dtypes or {}
    widths = [r.shape[1] for r in rows] + [c.shape[1] for c in cots] + [rows[i].shape[1] for i in want_rows]
    tb = _row_tile(T, widths)

    def body(*refs):
        ins = [r[...].astype(f32) for r in refs[: nr + npar]]
        cot = tuple(r[...].astype(f32) for r in refs[nr + npar: nr + npar + nc])
        add_refs = refs[nr + npar + nc: nr + npar + nc + len(add_idx)]
        out_refs = refs[nr + npar + nc + len(add_idx):]
        _, vjp = jax.vjp(lambda *a: tuple(fn(*a)), *ins)
        g = vjp(cot)
        for o_ref, i in zip(out_refs[: len(want_rows)], want_rows):
            r = g[i]
            if i in adds:
                r = r + add_refs[add_idx.index(i)][...].astype(f32)
            o_ref[...] = r.astype(o_ref.dtype)
        first = pl.program_id(0) == 0
        for o_ref, gp in zip(out_refs[len(want_rows):], g[nr:]):
            @pl.when(first)
            def _(o_ref=o_ref):
                o_ref[...] = jnp.zeros_like(o_ref)

            o_ref[...] += gp

    in_specs = [pl.BlockSpec((tb, r.shape[1]), lambda i: (i, 0)) for r in rows]
    in_specs += [pl.BlockSpec(p.shape, lambda i: (0, 0)) for p in params]
    in_specs += [pl.BlockSpec((tb, c.shape[1]), lambda i: (i, 0)) for c in cots]
    in_specs += [pl.BlockSpec((tb, adds[i].shape[1]), lambda i_: (i_, 0)) for i in add_idx]
    out_specs = [pl.BlockSpec((tb, rows[i].shape[1]), lambda i_: (i_, 0)) for i in want_rows]
    out_specs += [pl.BlockSpec(p.shape, lambda i: (0, 0)) for p in params]
    out_shape = [_S(rows[i].shape, row_dtypes.get(i, f32)) for i in want_rows] + [_S(p.shape, f32) for p in params]
    res = pl.pallas_call(
        body, name=name, grid=(T // tb,), in_specs=in_specs, out_specs=out_specs, out_shape=out_shape,
        compiler_params=_cparams(("arbitrary",)),
    )(*rows, *params, *cots, *[adds[i] for i in add_idx])
    return list(res[: len(want_rows)]), list(res[len(want_rows):])


def _f_rmsnorm(x, g):
    return (x * lax.rsqrt(jnp.mean(x * x, axis=-1, keepdims=True) + EPS) * g,)


def _f_swiglu(ab):
    F = ab.shape[1] // 2
    return (jax.nn.silu(ab[:, :F]) * ab[:, F:],)


def _f_s5_post(y, u, d):
    return (jax.nn.gelu(y + d * u),)


def _f_glu(g, v, b):
    return (g * jax.nn.sigmoid(v + b),)


def _f_gated_norm(y, z, w):
    return _f_rmsnorm(y * jax.nn.silu(z), w)


def _f_merge(gl, p5, pssd, b):
    D = p5.shape[1]
    gates = jax.nn.sigmoid(gl + b)
    return (gates[:, :D] * p5 + gates[:, D:] * pssd,)


def _f_dt(dtr, bias, a_log):
    dt = jax.nn.softplus(dtr + bias)
    return dt, dt * (-jnp.exp(a_log))


def _f_dt_expand(dtr, bias, a_log, e):
    dt, a = _f_dt(dtr, bias, a_log)
    return dt, a, _doth(dt, e), _doth(a, e)


def _loss_stage(x, tgt, g, *, name):
    T, D = x.shape
    tb = _row_tile(T, [D, D, D])

    def f(xb, gb, tb_):
        y = _f_rmsnorm(xb, gb)[0]
        return 0.5 * jnp.sum(jnp.mean(jnp.square(y - tb_), axis=-1, keepdims=True), axis=0, keepdims=True)

    def body(x_ref, t_ref, g_ref, l_ref, dx_ref, dg_ref):
        tv = t_ref[...]
        val, vjp = jax.vjp(lambda a, b: f(a, b, tv), x_ref[...], g_ref[...])
        dx, dg = vjp(jnp.ones((1, 1), f32))
        dx_ref[...] = dx

        @pl.when(pl.program_id(0) == 0)
        def _():
            l_ref[...] = jnp.zeros_like(l_ref)
            dg_ref[...] = jnp.zeros_like(dg_ref)

        l_ref[...] += jnp.broadcast_to(val, l_ref.shape)
        dg_ref[...] += dg

    row = pl.BlockSpec((tb, D), lambda i: (i, 0))
    par = pl.BlockSpec((1, D), lambda i: (0, 0))
    return pl.pallas_call(
        body, name=name, grid=(T // tb,), in_specs=[row, row, par],
        out_specs=[pl.BlockSpec((1, LANES), lambda i: (0, 0)), row, par],
        out_shape=[_S((1, LANES), f32), _S((T, D), f32), _S((1, D), f32)], compiler_params=_cparams(("arbitrary",)),
    )(x, tgt, g)


CONV_R = 64
HALO = 8


def _conv_shifts_down(ref, t):
    if isinstance(t, int) and t == 0:
        cur = ref[0:CONV_R, :]
        row = lax.broadcasted_iota(jnp.int32, cur.shape, 0)
        return [cur] + [jnp.where(row >= s, pltpu.roll(cur, s, axis=0), 0.0) for s in range(1, CONV_K)]
    win = ref[pl.ds(pl.multiple_of(t * CONV_R - HALO, HALO), CONV_R + HALO), :]
    return [win[HALO:]] + [pltpu.roll(win, s, axis=0)[HALO:] for s in range(1, CONV_K)]


def _conv_shifts_up(ref, t):
    if isinstance(t, int):
        cur = ref[t * CONV_R:(t + 1) * CONV_R, :]
        row = lax.broadcasted_iota(jnp.int32, cur.shape, 0)
        return [cur] + [jnp.where(row < CONV_R - s, pltpu.roll(cur, CONV_R - s, axis=0), 0.0) for s in range(1, CONV_K)]
    win = ref[pl.ds(pl.multiple_of(t * CONV_R, HALO), CONV_R + HALO), :]
    return [win[:CONV_R]] + [pltpu.roll(win, CONV_R + HALO - s, axis=0)[:CONV_R] for s in range(1, CONV_K)]


def _conv_pre(shifted, w, b):
    pre = b
    for k in range(CONV_K):
        pre = pre + w[k:k + 1, :] * shifted[CONV_K - 1 - k]
    return pre


def _conv_fwd(x, w, b, *, name):
    T, C = x.shape
    cb = _pick(C, 256)

    def body(x_ref, w_ref, b_ref, o_ref):
        xv = x_ref[...]
        row = lax.broadcasted_iota(jnp.int32, xv.shape, 0)
        shifted = [xv] + [jnp.where(row >= s, pltpu.roll(xv, s, axis=0), 0.0) for s in range(1, CONV_K)]
        o_ref[...] = jax.nn.silu(_conv_pre(shifted, w_ref[...], b_ref[...]))

    col = pl.BlockSpec((T, cb), lambda j: (0, j))
    return pl.pallas_call(
        body, name=name, grid=(C // cb,), in_specs=[col, pl.BlockSpec((CONV_K, cb), lambda j: (0, j)), pl.BlockSpec((1, cb), lambda j: (0, j))],
        out_specs=col, out_shape=_S((T, C), f32), compiler_params=_cparams(("parallel",)),
    )(x, w, b)


def _conv_bwd(x, w, b, dy, *, name, carry=None):
    T, C = x.shape
    cb = _pick(C, 128)
    carry = carry or _NO_CARRY
    ends = []
    for d in dy:
        ends.append((ends[-1] if ends else 0) + d.shape[1] // cb)
    assert ends[-1] == C // cb and all(d.shape[1] % cb == 0 for d in dy)
    npc = len(dy)
    n_tiles = T // CONV_R

    def body(*refs):
        own, c_in, c_out, c_sems = _carry_split(carry, refs, 3 + npc, 3)
        x_ref, w_ref, b_ref = own[:3]
        dy_refs, (dx_ref, dw_ref, db_ref, dpre_ref) = own[3:3 + npc], own[3 + npc:]
        _carry_start(carry, c_in, c_out, c_sems, pl.program_id(0) == 0)
        j = pl.program_id(0)
        wv, bv = w_ref[...], b_ref[...]

        def fold8(v):
            return jnp.sum(v.reshape(CONV_R // 8, 8, cb), axis=0)

        def first_pass(t, acc):
            rows = slice(0, CONV_R) if isinstance(t, int) else pl.ds(pl.multiple_of(t * CONV_R, CONV_R), CONV_R)
            shifted = _conv_shifts_down(x_ref, t)
            pre = _conv_pre(shifted, wv, bv)
            dyv = dy_refs[-1][rows, :]
            for p in range(npc - 2, -1, -1):
                dyv = jnp.where(j < ends[p], dy_refs[p][rows, :], dyv)
            sg = jax.nn.sigmoid(pre)
            dpre = dyv * sg * (1.0 + pre * (1.0 - sg))
            dpre_ref[rows, :] = dpre
            return tuple(acc[k] + fold8(dpre * shifted[CONV_K - 1 - k]) for k in range(CONV_K)) + (acc[CONV_K] + fold8(dpre),)

        zero = jnp.zeros((8, cb), f32)
        acc = lax.fori_loop(1, n_tiles, first_pass, first_pass(0, (zero,) * (CONV_K + 1)), unroll=3)
        for k in range(CONV_K):
            dw_ref[k:k + 1, :] = jnp.sum(acc[k], axis=0, keepdims=True)
        db_ref[...] = jnp.sum(acc[CONV_K], axis=0, keepdims=True)

        def dx_of(t):
            up = _conv_shifts_up(dpre_ref, t)
            dx = wv[CONV_K - 1:CONV_K, :] * up[0]
            for k in range(CONV_K - 1):
                dx = dx + wv[k:k + 1, :] * up[CONV_K - 1 - k]
            return dx

        def second_pass(t, c):
            dx_ref[pl.ds(pl.multiple_of(t * CONV_R, CONV_R), CONV_R), :] = dx_of(t)
            return c

        lax.fori_loop(0, n_tiles - 1, second_pass, 0, unroll=3)
        dx_ref[(n_tiles - 1) * CONV_R:, :] = dx_of(n_tiles - 1)
        _carry_finish(carry, c_in, c_out, c_sems, pl.program_id(0) == C // cb - 1)

    col = pl.BlockSpec((T, cb), lambda j: (0, j))
    wsp = pl.BlockSpec((CONV_K, cb), lambda j: (0, j))
    bsp = pl.BlockSpec((1, cb), lambda j: (0, j))
    starts = [0] + ends[:-1]
    dy_specs = [pl.BlockSpec((T, cb), lambda j, s=s, e=e: (0, jnp.clip(j, s, e - 1) - s)) for s, e in zip(starts, ends)]
    res = pl.pallas_call(
        body, name=name, grid=(C // cb,), in_specs=[col, wsp, bsp] + dy_specs + [_ANY] * len(carry.ins),
        out_specs=[col, wsp, bsp] + [_ANY] * len(carry.out_shapes),
        out_shape=[_S((T, C), f32), _S((CONV_K, C), f32), _S((1, C), f32)] + list(carry.out_shapes),
        scratch_shapes=[pltpu.VMEM((T, cb), f32)] + list(carry.sems),
        compiler_params=_cparams(("arbitrary",) if carry.ins else ("parallel",)),
    )(x, w, b, *dy, *carry.ins)
    return (*res[:3], list(res[3:]))


def _f_s5_prep(lr, li, ldt, lrb, lib, ldtb, brt, bit):
    def disc(lr_, li_, ldt_):
        dt = jnp.exp(ldt_)
        mag = jnp.exp(lr_ * dt)
        ar, ai = mag * jnp.cos(li_ * dt), mag * jnp.sin(li_ * dt)
        den = lr_ * lr_ + li_ * li_
        cr = ((ar - 1.0) * lr_ + ai * li_) / den
        ci = (ai * lr_ - (ar - 1.0) * li_) / den
        return ar, ai, cr, ci

    ar, ai, _, _ = disc(lr, li, ldt)
    _, _, cr, ci = disc(lrb, lib, ldtb)
    return ar, ai, cr * brt - ci * bit, cr * bit + ci * brt


def _s5_prep(args, *, name):
    G, N = args[0].shape
    GM = args[3].shape[0]

    def body(*refs):
        res = _f_s5_prep(*[r[...] for r in refs[:8]])
        for o, r in zip(refs[8:], res):
            o[...] = r

    return pl.pallas_call(body, name=name, out_shape=[_S((G, N), f32)] * 2 + [_S((GM, N), f32)] * 2)(*args)


def _s5_prep_bwd(args, cots, rsum, *, name):
    G, N = args[0].shape
    GM = args[3].shape[0]

    def body(*refs):
        ins = [r[...] for r in refs[:8]]
        cot = tuple(r[...] for r in refs[8:12])
        rs = refs[12][...]
        _, vjp = jax.vjp(_f_s5_prep, *ins)
        g = vjp(cot)
        fold = lambda v: jnp.dot(rs, v, preferred_element_type=f32, precision=lax.Precision.HIGHEST)
        o = refs[13:]
        o[0][...] = g[0] + fold(g[3])
        o[1][...] = g[1] + fold(g[4])
        o[2][...] = g[2] + fold(jnp.broadcast_to(g[5], (GM, LANES)))[:, 0:1]
        o[3][...] = g[6]
        o[4][...] = g[7]

    return pl.pallas_call(
        body, name=name, out_shape=[_S((G, N), f32), _S((G, N), f32), _S((G, 1), f32), _S((GM, N), f32), _S((GM, N), f32)],
    )(*args, *cots, rsum)


S5_TC = 512


def _s5_local_scan(src, w_r, w_i, a_r, a_i, *, reverse, name, carry=None, powers=True):
    T, C = src.shape
    nblk, cb, sb = w_r.shape
    NS = nblk * sb
    tc = min(S5_TC, T)
    nT, nt = T // tc, tc // NSEG
    tmap = (lambda i: nT - 1 - i) if reverse else (lambda i: i)

    carry = carry or _NO_CARRY

    def body(*refs):
        own, c_in, c_out, c_sems = _carry_split(carry, refs, 5, 4)
        u_ref, wr_ref, wi_ref, ar_ref, ai_ref, sr_ref, si_ref, pr_ref, pi_ref, st_r, st_i, pw_r, pw_i = own
        _carry_start(carry, c_in, c_out, c_sems, (pl.program_id(0) == 0) & (pl.program_id(1) == 0))

        @pl.when(pl.program_id(1) == 0)
        def _():
            st_r[...] = jnp.zeros_like(st_r)
            st_i[...] = jnp.zeros_like(st_i)
            pw_r[...] = jnp.ones_like(pw_r)
            pw_i[...] = jnp.zeros_like(pw_i)

        u = u_ref[...].astype(bf16)
        sr_ref[...] = jnp.dot(u, wr_ref[...], preferred_element_type=f32)
        si_ref[...] = jnp.dot(u, wi_ref[...], preferred_element_type=f32)
        ar = jnp.broadcast_to(ar_ref[...], (NSEG, sb))
        ai = jnp.broadcast_to(ai_ref[...], (NSEG, sb))

        def step(k, c):
            cr, ci, qr, qi = c
            kk = (nt - 1 - k) if reverse else k
            rows = pl.ds(pl.multiple_of(kk * NSEG, NSEG), NSEG)
            nr = ar * cr - ai * ci + sr_ref[rows, :]
            ni = ar * ci + ai * cr + si_ref[rows, :]
            sr_ref[rows, :] = nr
            si_ref[rows, :] = ni
            return (nr, ni, ar * qr - ai * qi, ar * qi + ai * qr) if powers else (nr, ni, qr, qi)

        cr, ci, qr, qi = lax.fori_loop(0, nt, step, (st_r[...], st_i[...], pw_r[...], pw_i[...]), unroll=8)
        st_r[...], st_i[...], pw_r[...], pw_i[...] = cr, ci, qr, qi
        pr_ref[...] = qr
        pi_ref[...] = qi
        _carry_finish(carry, c_in, c_out, c_sems, (pl.program_id(0) == nblk - 1) & (pl.program_id(1) == nT - 1))

    blk = pl.BlockSpec((tc, sb), lambda j, i: (tmap(i), j))
    wsp = pl.BlockSpec((None, cb, sb), lambda j, i: (j, 0, 0))
    asp = pl.BlockSpec((1, sb), lambda j, i: (0, j))
    psp = pl.BlockSpec((NSEG, sb), lambda j, i: (0, j))
    res = pl.pallas_call(
        body, name=name, grid=(nblk, nT),
        in_specs=[pl.BlockSpec((tc, cb), lambda j, i: (tmap(i), j)), wsp, wsp, asp, asp] + [_ANY] * len(carry.ins),
        out_specs=[blk, blk, psp, psp] + [_ANY] * len(carry.out_shapes),
        out_shape=[_S((T, NS), f32)] * 2 + [_S((NSEG, NS), f32)] * 2 + list(carry.out_shapes),
        scratch_shapes=[pltpu.VMEM((NSEG, sb), f32)] * 4 + list(carry.sems),
        compiler_params=_cparams(("arbitrary", "arbitrary") if carry.ins else ("parallel", "arbitrary")),
    )(src, w_r, w_i, a_r, a_i, *carry.ins)
    return (*res[:4], list(res[4:])) if carry.ins else res


def _s5_carry(e_r, e_i, p_r, p_i, *, reverse, name):
    NS = e_r.shape[1]

    def body(er_ref, ei_ref, pr_ref, pi_ref, cr_ref, ci_ref):
        ar, ai = pr_ref[0:1, :], pi_ref[0:1, :]
        cr = jnp.zeros((1, NS), f32)
        ci = jnp.zeros((1, NS), f32)
        order = list(range(NSEG - 1, -1, -1)) if reverse else list(range(NSEG))
        cr_ref[order[0]:order[0] + 1, :] = cr
        ci_ref[order[0]:order[0] + 1, :] = ci
        for prev, q in zip(order[:-1], order[1:]):
            er, ei = er_ref[prev:prev + 1, :], ei_ref[prev:prev + 1, :]
            cr, ci = er + ar * cr - ai * ci, ei + ar * ci + ai * cr
            cr_ref[q:q + 1, :] = cr
            ci_ref[q:q + 1, :] = ci

    return pl.pallas_call(body, name=name, out_shape=[_S((NSEG, NS), f32)] * 2)(e_r, e_i, p_r, p_i)


def _s5_fix_out(sl_r, sl_i, a_r, a_i, c_r, c_i, wc_r, wc_i, *, name):
    T, NS = sl_r.shape
    nblk, sb, cb = wc_r.shape
    tc = min(S5_TC, T)
    nT, nt = T // tc, tc // NSEG

    def body(lr_ref, li_ref, ar_ref, ai_ref, cr_ref, ci_ref, wr_ref, wi_ref, sr_ref, si_ref, y_ref, pw_r, pw_i):
        @pl.when(pl.program_id(1) == 0)
        def _():
            pw_r[...] = jnp.ones_like(pw_r)
            pw_i[...] = jnp.zeros_like(pw_i)

        ar = jnp.broadcast_to(ar_ref[...], (NSEG, sb))
        ai = jnp.broadcast_to(ai_ref[...], (NSEG, sb))
        cr, ci = cr_ref[...], ci_ref[...]

        def step(k, c):
            qr, qi = c
            qr, qi = ar * qr - ai * qi, ar * qi + ai * qr
            rows = pl.ds(pl.multiple_of(k * NSEG, NSEG), NSEG)
            sr_ref[rows, :] = lr_ref[rows, :] + qr * cr - qi * ci
            si_ref[rows, :] = li_ref[rows, :] + qr * ci + qi * cr
            return qr, qi

        qr, qi = lax.fori_loop(0, nt, step, (pw_r[...], pw_i[...]), unroll=8)
        pw_r[...], pw_i[...] = qr, qi
        y_ref[...] = (jnp.dot(sr_ref[...].astype(bf16), wr_ref[...], preferred_element_type=f32)
                      - jnp.dot(si_ref[...].astype(bf16), wi_ref[...], preferred_element_type=f32))

    blk = pl.BlockSpec((tc, sb), lambda j, i: (i, j))
    asp = pl.BlockSpec((1, sb), lambda j, i: (0, j))
    csp = pl.BlockSpec((NSEG, sb), lambda j, i: (0, j))
    wsp = pl.BlockSpec((None, sb, cb), lambda j, i: (j, 0, 0))
    return pl.pallas_call(
        body, name=name, grid=(nblk, nT), in_specs=[blk, blk, asp, asp, csp, csp, wsp, wsp],
        out_specs=[blk, blk, pl.BlockSpec((tc, cb), lambda j, i: (i, j))],
        out_shape=[_S((T, NS), f32)] * 2 + [_S((T, nblk * cb), f32)],
        scratch_shapes=[pltpu.VMEM((NSEG, sb), f32)] * 2, compiler_params=_cparams(("parallel", "arbitrary")),
    )(sl_r, sl_i, a_r, a_i, c_r, c_i, wc_r, wc_i)


def _s5_fix_bwd(ql_r, ql_i, ab_r, ab_i, c_r, c_i, s_r, s_i, sb_r, sb_i, u, dy, du_add, w_r, w_i, *, name):
    T, NS = ql_r.shape
    nblk, cb, sb = w_r.shape
    tc = min(S5_TC, T)
    nT, nt = T // tc, tc // NSEG
    tmap = lambda i: nT - 1 - i

    def body(lr_ref, li_ref, ar_ref, ai_ref, cr_ref, ci_ref, sr_ref, si_ref, br_ref, bi_ref, u_ref, dy_ref, dua_ref, wr_ref, wi_ref,
             du_ref, dwr_ref, dwi_ref, dcr_ref, dci_ref, dar_ref, dai_ref, pw_r, pw_i, ac_r, ac_i, q_r, q_i):
        first = pl.program_id(1) == 0

        @pl.when(first)
        def _():
            pw_r[...] = jnp.ones_like(pw_r)
            pw_i[...] = jnp.zeros_like(pw_i)
            ac_r[...] = jnp.zeros_like(ac_r)
            ac_i[...] = jnp.zeros_like(ac_i)
            dwr_ref[...] = jnp.zeros_like(dwr_ref)
            dwi_ref[...] = jnp.zeros_like(dwi_ref)
            dcr_ref[...] = jnp.zeros_like(dcr_ref)
            dci_ref[...] = jnp.zeros_like(dci_ref)

        ar = jnp.broadcast_to(ar_ref[...], (NSEG, sb))
        ai = jnp.broadcast_to(ai_ref[...], (NSEG, sb))
        cr, ci = cr_ref[...], ci_ref[...]

        def fix(rows, qr, qi, spr, spi, accr, acci):
            qr, qi = ar * qr - ai * qi, ar * qi + ai * qr
            xr = lr_ref[rows, :] + qr * cr - qi * ci
            xi = li_ref[rows, :] + qr * ci + qi * cr
            q_r[rows, :] = xr
            q_i[rows, :] = xi
            return qr, qi, accr + xr * spr + xi * spi, acci + xi * spr - xr * spi

        def step(k, c):
            qr, qi, accr, acci = c
            kk = nt - 1 - k
            rows = pl.ds(pl.multiple_of(kk * NSEG, NSEG), NSEG)
            prev = pl.ds(pl.multiple_of((kk - 1) * NSEG, NSEG), NSEG)
            return fix(rows, qr, qi, sr_ref[prev, :], si_ref[prev, :], accr, acci)

        c = lax.fori_loop(0, nt - 1, step, (pw_r[...], pw_i[...], ac_r[...], ac_i[...]), unroll=7)
        qr, qi, accr, acci = fix(pl.ds(0, NSEG), *c[:2], br_ref[...], bi_ref[...], *c[2:])
        pw_r[...], pw_i[...], ac_r[...], ac_i[...] = qr, qi, accr, acci

        qrb, qib = q_r[...].astype(bf16), q_i[...].astype(bf16)
        nt_dims = (((1,), (1,)), ((), ()))
        tn_dims = (((0,), (0,)), ((), ()))
        du_ref[...] = (dua_ref[...] + lax.dot_general(qrb, wr_ref[...], nt_dims, preferred_element_type=f32)
                       + lax.dot_general(qib, wi_ref[...], nt_dims, preferred_element_type=f32))
        ub = u_ref[...].astype(bf16)
        dwr_ref[...] += lax.dot_general(ub, qrb, tn_dims, preferred_element_type=f32)
        dwi_ref[...] += lax.dot_general(ub, qib, tn_dims, preferred_element_type=f32)
        dyb = dy_ref[...].astype(bf16)
        dcr_ref[...] += lax.dot_general(sr_ref[...].astype(bf16), dyb, tn_dims, preferred_element_type=f32)
        dci_ref[...] -= lax.dot_general(si_ref[...].astype(bf16), dyb, tn_dims, preferred_element_type=f32)

        @pl.when(pl.program_id(1) == nT - 1)
        def _():
            dar_ref[...] = jnp.sum(accr, axis=0, keepdims=True)
            dai_ref[...] = jnp.sum(acci, axis=0, keepdims=True)

    blk = pl.BlockSpec((tc, sb), lambda j, i: (tmap(i), j))
    asp = pl.BlockSpec((1, sb), lambda j, i: (0, j))
    csp = pl.BlockSpec((NSEG, sb), lambda j, i: (0, j))
    bsp = pl.BlockSpec((None, NSEG, sb), lambda j, i: (tmap(i), 0, j))
    chn = pl.BlockSpec((tc, cb), lambda j, i: (tmap(i), j))
    wsp = pl.BlockSpec((None, cb, sb), lambda j, i: (j, 0, 0))
    wcs = pl.BlockSpec((None, sb, cb), lambda j, i: (j, 0, 0))
    return pl.pallas_call(
        body, name=name, grid=(nblk, nT), in_specs=[blk, blk, asp, asp, csp, csp, blk, blk, bsp, bsp, chn, chn, chn, wsp, wsp],
        out_specs=[chn, wsp, wsp, wcs, wcs, asp, asp],
        out_shape=[_S((T, nblk * cb), f32), _S((nblk, cb, sb), f32), _S((nblk, cb, sb), f32), _S((nblk, sb, cb), f32),
                   _S((nblk, sb, cb), f32), _S((1, NS), f32), _S((1, NS), f32)],
        scratch_shapes=[pltpu.VMEM((NSEG, sb), f32)] * 4 + [pltpu.VMEM((tc, sb), f32)] * 2,
        compiler_params=_cparams(("parallel", "arbitrary")),
    )(ql_r, ql_i, ab_r, ab_i, c_r, c_i, s_r, s_i, sb_r, sb_i, u, dy, du_add, w_r, w_i)


SSD2_TB = 512
_NN = (((1,), (0,)), ((), ()))
_NT = (((1,), (1,)), ((), ()))
_TN = (((0,), (0,)), ((), ()))


def _dotf(a, b, dims):
    return lax.dot_general(a.astype(bf16), b.astype(bf16), dims, preferred_element_type=f32)


def _doth(a, b, dims=_NN, sel="b", parts=3):
    x, m = (a, b) if sel == "b" else (b, a)
    m = m.astype(bf16)
    out = None
    for _ in range(parts):
        piece = x.astype(bf16)
        x = x - piece.astype(f32)
        d = lax.dot_general(*((piece, m) if sel == "b" else (m, piece)), dims, preferred_element_type=f32)
        out = d if out is None else out + d
    return out


def _ssd_consts(hpg):
    W = hpg * CHUNK
    i = lax.broadcasted_iota(jnp.int32, (CHUNK, CHUNK), 0)
    j = lax.broadcasted_iota(jnp.int32, (CHUNK, CHUNK), 1)
    tril = (i >= j).astype(f32)
    r = lax.broadcasted_iota(jnp.int32, (W, W), 0)
    c = lax.broadcasted_iota(jnp.int32, (W, W), 1)
    bd = (r // CHUNK == c // CHUNK).astype(f32)
    triu_bd = bd * (r <= c).astype(f32)
    e_r = lax.broadcasted_iota(jnp.int32, (W, LANES), 0)
    e_c = lax.broadcasted_iota(jnp.int32, (W, LANES), 1)
    ered = (e_r // HEADDIM == e_c).astype(f32)
    return tril, jnp.tile(tril, (1, hpg)), bd, triu_bd, ered


def _ssd2_specs(G, hpg, tb, tmap, b_off, c_off):
    W = hpg * HEADDIM
    ncb = tb // CHUNK
    xsp = pl.BlockSpec((tb, W), lambda g, i: (tmap(i), g))
    bsp = pl.BlockSpec((tb, SSD_STATE), lambda g, i: (tmap(i), b_off + g))
    csp = pl.BlockSpec((tb, SSD_STATE), lambda g, i: (tmap(i), c_off + g))
    rsp = pl.BlockSpec((None, ncb, W), lambda g, i: (g, tmap(i), 0))
    dsp = pl.BlockSpec((1, W), lambda g, i: (0, g))
    hsp = pl.BlockSpec((None, ncb, SSD_STATE, W), lambda g, i: (g, tmap(i), 0, 0))
    const = lambda a: pl.BlockSpec(a.shape, lambda g, i: (0, 0))
    return xsp, bsp, csp, rsp, dsp, hsp, const


def _tile_rows(a, n):
    return jnp.concatenate([a] * n, axis=0)


def _ssd2_fwd(xc, dt4, a4, dtw, aw, d4, consts, *, d_inner, name, carry=None):
    carry = carry or _NO_CARRY
    T = xc.shape[0]
    G, nc, W = dtw.shape
    hpg = W // CHUNK
    tb = min(SSD2_TB, T)
    nb, ncb = T // tb, tb // CHUNK
    b_off = d_inner // SSD_STATE
    xsp, bsp, csp, rsp, dsp, hsp, const = _ssd2_specs(G, hpg, tb, lambda i: i, b_off, b_off + G)
    tril, mask4, bd, triu_bd, _ = consts

    def body(*refs):
        own, c_in, c_out, c_sems = _carry_split(carry, refs, 12, 2)
        x_ref, b_ref, c_ref, dt_ref, a_ref, dtw_ref, aw_ref, d_ref, tril_ref, mask_ref, bd_ref, tbd_ref, y_ref, hs_ref, h_scr = own
        _carry_start(carry, c_in, c_out, c_sems, (pl.program_id(0) == 0) & (pl.program_id(1) == 0))

        @pl.when(pl.program_id(1) == 0)
        def _():
            h_scr[...] = jnp.zeros_like(h_scr)

        acs_rows = _doth(aw_ref[...], tbd_ref[...])
        ht = h_scr[...]
        for c in range(ncb):
            rows = slice(c * CHUNK, (c + 1) * CHUNK)
            x, bm, cm = x_ref[rows, :], b_ref[rows, :], c_ref[rows, :]
            acs = _doth(tril_ref[...], a_ref[rows, :], sel="a")
            lmat = jnp.where(mask_ref[...] > 0, jnp.exp(jnp.minimum(acs - acs_rows[c:c + 1, :], 0.0)), 0.0)
            m4 = _dotf(cm, _tile_rows(bm, hpg), _NT) * lmat * dtw_ref[c:c + 1, :]
            xbd = _tile_rows(x, hpg) * bd_ref[...]
            hs_ref[c] = ht
            y_ref[rows, :] = _dotf(m4, xbd, _NN) + _dotf(cm, ht, _NN) * jnp.exp(acs) + d_ref[...] * x
            a_last = acs[CHUNK - 1:CHUNK, :]
            xw = x * (jnp.exp(a_last - acs) * dt_ref[rows, :])
            ht = ht * jnp.exp(a_last) + _dotf(bm, xw, _TN)
        h_scr[...] = ht
        _carry_finish(carry, c_in, c_out, c_sems, (pl.program_id(0) == G - 1) & (pl.program_id(1) == nb - 1))

    res = pl.pallas_call(
        body, name=name, grid=(G, nb),
        in_specs=[xsp, bsp, csp, xsp, xsp, rsp, rsp, dsp, const(tril), const(mask4), const(bd), const(triu_bd)] + [_ANY] * len(carry.ins),
        out_specs=[xsp, hsp] + [_ANY] * len(carry.out_shapes),
        out_shape=[_S((T, G * W), f32), _S((G, nc, SSD_STATE, W), f32)] + list(carry.out_shapes),
        scratch_shapes=[pltpu.VMEM((SSD_STATE, W), f32)] + list(carry.sems),
        compiler_params=_cparams(("arbitrary", "arbitrary") if carry.ins else ("parallel", "arbitrary")),
    )(xc, xc, xc, dt4, a4, dtw, aw, d4, tril, mask4, bd, triu_bd, *carry.ins)
    return res[0], res[1], list(res[2:])


def _ssd2_bwd(xc, dt4, a4, dtw, aw, d4, consts, hs, dy, *, d_inner, name):
    T = xc.shape[0]
    G, nc, W = dtw.shape
    hpg = W // CHUNK
    tb = min(SSD2_TB, T)
    nb, ncb = T // tb, tb // CHUNK
    b_off = d_inner // SSD_STATE
    tmap = lambda i: nb - 1 - i
    xsp, bsp, csp, rsp, dsp, hsp, const = _ssd2_specs(G, hpg, tb, tmap, b_off, b_off + G)
    gsp = pl.BlockSpec((tb, SSD_STATE), lambda g, i: (tmap(i), g))
    ddsp = pl.BlockSpec((None, 1, LANES), lambda g, i: (g, 0, 0))
    tril, mask4, bd, triu_bd, ered = consts

    def body(x_ref, b_ref, c_ref, dt_ref, a_ref, dtw_ref, aw_ref, d_ref, tril_ref, mask_ref, bd_ref, tbd_ref, er_ref, hs_ref, dy_ref,
             dx_ref, db_ref, dc_ref, ddtc_ref, dac_ref, ddtw_ref, daw_ref, dd_ref, g_scr, dd_scr, rw_scr, tl_scr):
        first = pl.program_id(1) == 0

        @pl.when(first)
        def _():
            g_scr[...] = jnp.zeros_like(g_scr)
            dd_scr[...] = jnp.zeros_like(dd_scr)

        mask = mask_ref[...] > 0
        lane_in_block = lax.broadcasted_iota(jnp.int32, mask.shape, 1) & (CHUNK - 1)
        maskt = lax.broadcasted_iota(jnp.int32, mask.shape, 0) <= lane_in_block
        acs_rows = _doth(aw_ref[...], tbd_ref[...])
        dht = g_scr[...]
        dd = dd_scr[...]
        for c in range(ncb - 1, -1, -1):
            rows = slice(c * CHUNK, (c + 1) * CHUNK)
            x, bm, cm, dyc = x_ref[rows, :], b_ref[rows, :], c_ref[rows, :], dy_ref[rows, :]
            dtc, dtr = dt_ref[rows, :], dtw_ref[c:c + 1, :]
            ht = hs_ref[c]
            acs = _doth(tril_ref[...], a_ref[rows, :], sel="a")
            seg = acs - acs_rows[c:c + 1, :]
            lmat = jnp.where(mask, jnp.exp(jnp.minimum(seg, 0.0)), 0.0)
            lmat_t = jnp.where(maskt, jnp.exp(jnp.minimum(-seg, 0.0)), 0.0)
            btile, ctile = _tile_rows(bm, hpg), _tile_rows(cm, hpg)
            g4 = _dotf(cm, btile, _NT)
            gt4 = _dotf(bm, ctile, _NT)
            m4 = g4 * lmat * dtr
            mt4 = gt4 * lmat_t * dtc
            xbd = _tile_rows(x, hpg) * bd_ref[...]
            dybd = _tile_rows(dyc, hpg) * bd_ref[...]
            dm4 = _dotf(dyc, xbd, _NT)
            dmt4 = _dotf(x, dybd, _NT)
            dx = d_ref[...] * dyc + _dotf(mt4, dybd, _NN)
            dd = dd + jnp.sum(dyc * x, axis=0, keepdims=True)
            e4 = dm4 * m4
            dc = _dotf(dm4 * lmat * dtr, btile, _NN)
            db = _dotf(dmt4 * lmat_t * dtc, ctile, _NN)
            decay = jnp.exp(acs)
            yoff = _dotf(cm, ht, _NN) * decay
            dz = dyc * decay
            dc = dc + _dotf(dz, ht, _NT)
            dht_prev = _dotf(cm, dz, _TN)
            a_last = acs[CHUNK - 1:CHUNK, :]
            ea_last = jnp.exp(a_last)
            erel = jnp.exp(a_last - acs)
            dte = erel * dtc
            dxw = _dotf(bm, dht, _NN)
            db = db + _dotf(x * dte, dht, _NT)
            dx = dx + dxw * dte
            q4 = dxw * x
            dacs = e4 + dyc * yoff - q4 * dte
            col = jnp.concatenate([q4 * erel, _doth(tril_ref[...], dacs, _TN, sel="a")], axis=0)
            col = _doth(col, er_ref[...], parts=2)
            ddtc_ref[rows, :] = col[:CHUNK]
            dac_ref[rows, :] = col[CHUNK:]
            ddtw_ref[c:c + 1, :] = jnp.sum(dm4 * g4 * lmat, axis=0, keepdims=True)
            rw_scr[c:c + 1, :] = -jnp.sum(e4, axis=0, keepdims=True)
            tl_scr[c:c + 1, :] = jnp.sum(q4 * dte, axis=0, keepdims=True) + ea_last * jnp.sum(dht * ht, axis=0, keepdims=True)
            dx_ref[rows, :] = dx
            db_ref[rows, :] = db
            dc_ref[rows, :] = dc
            dht = dht_prev + dht * ea_last
        daw_ref[...] = _doth(rw_scr[...], tbd_ref[...], _NT) + _doth(tl_scr[...], bd_ref[...])
        g_scr[...] = dht
        dd_scr[...] = dd

        @pl.when(pl.program_id(1) == nb - 1)
        def _():
            dd_ref[...] = _doth(dd, er_ref[...])

    return pl.pallas_call(
        body, name=name, grid=(G, nb),
        in_specs=[xsp, bsp, csp, xsp, xsp, rsp, rsp, dsp, const(tril), const(mask4), const(bd), const(triu_bd), const(ered), hsp, xsp],
        out_specs=[xsp, gsp, gsp, gsp, gsp, rsp, rsp, ddsp],
        out_shape=[_S((T, G * W), f32), _S((T, G * SSD_STATE), f32), _S((T, G * SSD_STATE), f32), _S((T, G * LANES), f32),
                   _S((T, G * LANES), f32), _S(dtw.shape, f32), _S(dtw.shape, f32), _S((G, 1, LANES), f32)],
        scratch_shapes=[pltpu.VMEM((SSD_STATE, W), f32), pltpu.VMEM((1, W), f32), pltpu.VMEM((ncb, W), f32), pltpu.VMEM((ncb, W), f32)],
        compiler_params=_cparams(("parallel", "arbitrary")),
    )(xc, xc, xc, dt4, a4, dtw, aw, d4, tril, mask4, bd, triu_bd, ered, hs, dy)


def _peers():
    x, y, c = lax.axis_index("x"), lax.axis_index("y"), lax.axis_index("c")
    return x, y, c


_ANY = pl.BlockSpec(memory_space=pl.ANY)
N_CHIP = N_DEV // 2


def _all_gather(shards, *, name):
    n = len(shards)
    carry = _carry_gather(shards)

    def body(*refs):
        x_refs, out_refs, sems = refs[:n], refs[n:2 * n], refs[2 * n:]
        _gather_start(x_refs, out_refs, sems)
        _gather_finish(x_refs, out_refs, sems)

    return pl.pallas_call(
        body, name=name, out_shape=list(carry.out_shapes), in_specs=[_ANY] * n, out_specs=[_ANY] * n, scratch_shapes=list(carry.sems),
    )(*shards)


def _gather_parts(x_refs, out_refs, sems):
    send_sems, recv_sems, local_sems = sems
    x, y, c = _peers()
    me, sibling = (x, y, c), (x, y, 1 - c)
    chips = [(1 - x, y), (x, 1 - y), (1 - x, 1 - y)]
    n = len(x_refs)

    def copy(a, r, block, to, src=None):
        px, py, pc = block
        slot = out_refs[a].at[4 * px + 2 * py + pc]
        return pltpu.make_async_remote_copy(
            src_ref=slot if src is None else src, dst_ref=slot, send_sem=send_sems.at[7 * a + r],
            recv_sem=recv_sems.at[7 * a + r], device_id=to, device_id_type=MESH)

    mine = [pltpu.make_async_copy(x_refs[a], out_refs[a].at[4 * x + 2 * y + c], local_sems.at[a]) for a in range(n)]
    first = []
    for a in range(n):
        first.append(copy(a, 0, me, sibling, src=x_refs[a]))
        first += [copy(a, 1 + j, me, (*chip, c), src=x_refs[a]) for j, chip in enumerate(chips)]
    return copy, mine, first, me, sibling, chips, c, n


def _gather_start(x_refs, out_refs, sems):
    _, mine, first, *_ = _gather_parts(x_refs, out_refs, sems)
    for cp in mine + first:
        cp.start()


def _gather_finish(x_refs, out_refs, sems):
    copy, mine, first, me, sibling, chips, c, n = _gather_parts(x_refs, out_refs, sems)
    passed = []
    for j, chip in enumerate(chips):
        for a in range(n):
            copy(a, 1 + j, (*chip, c), me).wait_recv()
            fwd = copy(a, 4 + j, (*chip, c), sibling)
            fwd.start()
            passed.append(fwd)
    for a in range(n):
        copy(a, 0, sibling, me).wait_recv()
    for j, chip in enumerate(chips):
        for a in range(n):
            copy(a, 4 + j, (*chip, 1 - c), me).wait_recv()
    for cp in first + passed:
        cp.wait_send()
    for cp in mine:
        cp.wait()


def _exchange_sibling(slots, *, name):
    n = len(slots)

    def body(*refs):
        x_refs, sib_refs = refs[:n], refs[n:2 * n]
        send_sems, recv_sems = refs[2 * n:]
        x, y, c = _peers()
        give = [pltpu.make_async_remote_copy(
            src_ref=x_refs[a].at[pl.ds(N_CHIP * (1 - c), N_CHIP)], dst_ref=sib_refs[a], send_sem=send_sems.at[a],
            recv_sem=recv_sems.at[a], device_id=(x, y, 1 - c), device_id_type=MESH) for a in range(n)]
        for cp in give:
            cp.start()
        for cp in give:
            cp.wait_recv()
        for cp in give:
            cp.wait_send()

    return list(pl.pallas_call(
        body, name=name, out_shape=[_S((N_CHIP,) + s.shape[1:], s.dtype) for s in slots], in_specs=[_ANY] * n, out_specs=[_ANY] * n,
        scratch_shapes=[pltpu.SemaphoreType.DMA((n,)), pltpu.SemaphoreType.DMA((n,))],
    )(*slots))


def _chip_sum(slots, sib, core, *, name):
    _, R, W = slots.shape
    tr = _pick(R, max(16, EW_BLOCK_ELEMS // W), align=16)

    def body(core_ref, x_ref, s_ref, o_ref):
        o_ref[...] = (x_ref[...].astype(f32) + s_ref[...].astype(f32)).astype(o_ref.dtype)

    blk = pl.BlockSpec((None, tr, W), lambda t, i, core_ref: (t, i, 0))
    return pl.pallas_call(
        body, name=name, out_shape=_S(sib.shape, slots.dtype),
        grid_spec=pltpu.PrefetchScalarGridSpec(
            num_scalar_prefetch=1, grid=(N_CHIP, R // tr),
            in_specs=[pl.BlockSpec((None, tr, W), lambda t, i, core_ref: (N_CHIP * core_ref[0] + t, i, 0)), blk], out_specs=blk),
        compiler_params=_cparams(("parallel", "parallel")),
    )(core, slots, sib)


def _chip_out_shapes(parts):
    return [_S((N_CHIP - 1,) + p.shape[1:], p.dtype) for p in parts]


def _chip_sems(n):
    return [pltpu.SemaphoreType.DMA((3 * n,)), pltpu.SemaphoreType.DMA((3 * n,))]


def _chip_copies(p_refs, out_refs, send_sems, recv_sems):
    x, y, c = _peers()
    copies = []
    for j in range(1, N_CHIP):
        tx, ty = x ^ (j >> 1), y ^ (j & 1)
        for a in range(len(p_refs)):
            copies.append(pltpu.make_async_remote_copy(
                src_ref=p_refs[a].at[2 * tx + ty], dst_ref=out_refs[a].at[j - 1], send_sem=send_sems.at[3 * a + j - 1],
                recv_sem=recv_sems.at[3 * a + j - 1], device_id=(tx, ty, c), device_id_type=MESH))
    return copies


def _start_all(copies):
    for cp in copies:
        cp.start()


def _wait_all(copies):
    for cp in copies:
        cp.wait_recv()
    for cp in copies:
        cp.wait_send()


class _Carry(NamedTuple):
    ins: tuple = ()
    out_shapes: tuple = ()
    sems: tuple = ()
    start: Callable = None
    finish: Callable = None


_NO_CARRY = _Carry()


def _carry_chips(parts):
    return _Carry(tuple(parts), tuple(_chip_out_shapes(parts)), tuple(_chip_sems(len(parts))),
                  lambda i, o, s: _start_all(_chip_copies(i, o, *s)), lambda i, o, s: _wait_all(_chip_copies(i, o, *s)))


def _carry_gather(shards):
    n = len(shards)
    sems = (pltpu.SemaphoreType.DMA((7 * n,)), pltpu.SemaphoreType.DMA((7 * n,)), pltpu.SemaphoreType.DMA((n,)))
    return _Carry(tuple(shards), tuple(_S((N_DEV,) + s.shape, s.dtype) for s in shards), sems, _gather_start, _gather_finish)


def _carry_split(carry, refs, n_in, n_out):
    ci, co, cs = len(carry.ins), len(carry.out_shapes), len(carry.sems)
    refs = list(refs)
    own_in, c_in = refs[:n_in], refs[n_in:n_in + ci]
    own_out, c_out = refs[n_in + ci:n_in + ci + n_out], refs[n_in + ci + n_out:n_in + ci + n_out + co]
    rest = refs[n_in + ci + n_out + co:]
    own_scratch, c_sems = rest[:len(rest) - cs], rest[len(rest) - cs:]
    return own_in + own_out + own_scratch, c_in, c_out, c_sems


def _carry_start(carry, c_in, c_out, c_sems, first):
    if carry.ins:
        @pl.when(first)
        def _():
            carry.start(c_in, c_out, c_sems)


def _carry_finish(carry, c_in, c_out, c_sems, last):
    if carry.ins:
        @pl.when(last)
        def _():
            carry.finish(c_in, c_out, c_sems)


def _sum_slots(stack, *, name):
    n, R, W = stack.shape
    tr = _pick(R, 1024, align=8)

    def body(s_ref, o_ref):
        acc = s_ref[0]
        for k in range(1, n):
            acc = acc + s_ref[k]
        o_ref[...] = acc

    return pl.pallas_call(
        body, name=name, grid=(R // tr,), in_specs=[pl.BlockSpec((n, tr, W), lambda i: (0, i, 0))],
        out_specs=pl.BlockSpec((tr, W), lambda i: (i, 0)), out_shape=_S((R, W), f32), compiler_params=_cparams(("parallel",)),
    )(stack)


def _adamw_math(gv, wv, mv, vv):
    c1 = 1.0 / (1.0 - ADAM_B1 ** ADAM_STEP)
    c2 = 1.0 / (1.0 - ADAM_B2 ** ADAM_STEP)
    nm = ADAM_B1 * mv + (1.0 - ADAM_B1) * gv
    nv = ADAM_B2 * vv + (1.0 - ADAM_B2) * jnp.square(gv)
    return -ADAM_LR * ((nm * c1) / (jnp.sqrt(nv * c2) + ADAM_EPS) + ADAM_WD * wv), nm, nv


def _adamw(g, w, m, v, *, name):
    R, W = w.shape
    tr = _pick(R, max(8, (1 << 20) // (4 * W)), align=8)

    def body(g_ref, w_ref, m_ref, v_ref, d_ref, nm_ref, nv_ref):
        d_ref[...], nm_ref[...], nv_ref[...] = _adamw_math(g_ref[...], w_ref[...], m_ref[...], v_ref[...])

    sp = pl.BlockSpec((tr, W), lambda i: (i, 0))
    return pl.pallas_call(
        body, name=name, grid=(R // tr,), in_specs=[sp] * 4, out_specs=[sp] * 3, out_shape=[_S((R, W), f32)] * 3,
        compiler_params=_cparams(("parallel",)),
    )(g, w, m, v)


def _reduce_adamw(own, arrived, chip, w, m, v, *, name):
    n, R, W = arrived.shape
    tr = _pick(R, max(16, EW_BLOCK_ELEMS // (4 * W)), align=16)

    def body(chip_ref, o_ref, p_ref, w_ref, m_ref, v_ref, g_ref, d_ref, nm_ref, nv_ref):
        gv = o_ref[...].astype(f32)
        for k in range(n):
            gv = gv + p_ref[k].astype(f32)
        g_ref[...] = gv
        d_ref[...], nm_ref[...], nv_ref[...] = _adamw_math(gv, w_ref[...], m_ref[...], v_ref[...])

    sp = pl.BlockSpec((tr, W), lambda i, chip_ref: (i, 0))
    return pl.pallas_call(
        body, name=name, out_shape=[_S((R, W), f32)] * 4,
        grid_spec=pltpu.PrefetchScalarGridSpec(
            num_scalar_prefetch=1, grid=(R // tr,),
            in_specs=[pl.BlockSpec((None, tr, W), lambda i, chip_ref: (chip_ref[0], i, 0)),
                      pl.BlockSpec((n, tr, W), lambda i, chip_ref: (0, i, 0))] + [sp] * 3, out_specs=[sp] * 4),
        compiler_params=_cparams(("parallel",)),
    )(chip, own, arrived, w, m, v)


def _pieces(seg_start, seg_len, shard_w):
    out, col = [], seg_start
    while col < seg_start + seg_len:
        k, a = divmod(col, shard_w)
        n = min(shard_w - a, seg_start + seg_len - col)
        out.append((k, a, col - seg_start, n))
        col += n
    return out


def _unshard_w_in(g, seg_lens, *, name):
    _, D, w = g.shape
    starts = [sum(seg_lens[:i]) for i in range(len(seg_lens))]
    widths = [max(n, LANES) for n in seg_lens]
    tm = _pick(D, 256, align=16)

    def body(g_ref, *o_refs):
        for o_ref, s0, n in zip(o_refs, starts, seg_lens):
            if n < o_ref.shape[1]:
                o_ref[...] = jnp.zeros_like(o_ref)
            for k, a, off, m in _pieces(s0, n, w):
                o_ref[:, off:off + m] = g_ref[k, :, a:a + m]

    return pl.pallas_call(
        body, name=name, grid=(D // tm,), in_specs=[pl.BlockSpec((N_DEV, tm, w), lambda i: (0, i, 0))],
        out_specs=[pl.BlockSpec((tm, wd), lambda i: (i, 0)) for wd in widths], out_shape=[_S((D, wd), g.dtype) for wd in widths],
        compiler_params=_cparams(("parallel",)),
    )(g)


def _unshard_pair(g1, g2, *, name):
    _, D, w = g1.shape
    tm = _pick(D, 256, align=16)

    def body(a_ref, b_ref, o_ref):
        for i, g_ref in enumerate((a_ref, b_ref)):
            for k in range(N_DEV):
                off = (i * N_DEV + k) * w
                o_ref[:, off:off + w] = g_ref[k]

    blk = pl.BlockSpec((N_DEV, tm, w), lambda i: (0, i, 0))
    return pl.pallas_call(
        body, name=name, grid=(D // tm,), in_specs=[blk, blk], out_specs=pl.BlockSpec((tm, 2 * N_DEV * w), lambda i: (i, 0)),
        out_shape=_S((D, 2 * N_DEV * w), g1.dtype), compiler_params=_cparams(("parallel",)),
    )(g1, g2)


def _reshard_pair(dw, *, name):
    D, w = dw.shape[0], dw.shape[1] // (2 * N_DEV)
    tm = _pick(D, 128, align=16)

    def body(g_ref, a_ref, b_ref):
        for i, o_ref in enumerate((a_ref, b_ref)):
            for k in range(N_DEV):
                off = (i * N_DEV + k) * w
                o_ref[_slot_of(k)] = g_ref[:, off:off + w].astype(o_ref.dtype)

    blk = pl.BlockSpec((N_DEV, tm, w), lambda i: (0, i, 0))
    return pl.pallas_call(
        body, name=name, grid=(D // tm,), in_specs=[pl.BlockSpec((tm, dw.shape[1]), lambda i: (i, 0))], out_specs=[blk, blk],
        out_shape=[_S((N_DEV, D, w), bf16)] * 2, compiler_params=_cparams(("parallel",)),
    )(dw)


def _reshard_w_in(grads, seg_lens, w, *, name):
    D = grads[0].shape[0]
    starts = [sum(seg_lens[:i]) for i in range(len(seg_lens))]
    tm = _pick(D, 128, align=16)

    def body(*refs):
        o_ref = refs[-1]
        for g_ref, s0, n in zip(refs[:-1], starts, seg_lens):
            for k, a, off, m in _pieces(s0, n, w):
                o_ref[_slot_of(k), :, a:a + m] = g_ref[:, off:off + m].astype(o_ref.dtype)

    return pl.pallas_call(
        body, name=name, grid=(D // tm,), in_specs=[pl.BlockSpec((tm, g.shape[1]), lambda i: (i, 0)) for g in grads],
        out_specs=pl.BlockSpec((N_DEV, tm, w), lambda i: (0, i, 0)), out_shape=_S((N_DEV, D, w), bf16),
        compiler_params=_cparams(("parallel",)),
    )(*grads)


def _pad_flat(a, mult):
    a = a.reshape(-1)
    n = -(-a.shape[0] // mult) * mult
    return a if n == a.shape[0] else jnp.pad(a, (0, n - a.shape[0]))


def _pad_cols(a, mult):
    n = -(-a.shape[1] // mult) * mult
    return a if n == a.shape[1] else jnp.pad(a, ((0, 0), (0, n - a.shape[1])))


def _block_diag(t):
    nblk, g, P, Q = t.shape
    eye = jnp.eye(g, dtype=t.dtype)
    return (t[:, :, :, None, :] * eye[None, :, None, :, None]).reshape(nblk, g * P, g * Q)


def _block_diag_t(w, P, Q):
    nblk = w.shape[0]
    g = w.shape[1] // P
    eye = jnp.eye(g, dtype=w.dtype)
    return (w.reshape(nblk, g, P, g, Q) * eye[None, :, None, :, None]).sum(axis=3)


_COLS = ("ffn1_w_gate", "ffn1_w_up", "ffn2_w_gate", "ffn2_w_up")
_ROWS = ("ffn1_w_down", "ffn2_w_down", "s5_w_glu", "w_proj_s5", "w_out", "w_proj_ssd")
_BIG = _COLS + _ROWS + ("w_in", "conv_w")
_SMALL = ("ffn1_norm", "mix_norm", "conv_b", "s5_A_re", "s5_A_im", "s5_log_dt", "s5_B_re", "s5_B_im", "s5_C_re", "s5_C_im",
          "s5_D", "s5_b_glu", "ssd_A_log", "ssd_dt_bias", "ssd_D", "ssd_norm", "b_gate", "ffn2_norm", "final_norm")
_WEIGHTS = ("ffn1_norm", "ffn1_w_gate", "ffn1_w_up", "ffn1_w_down", "mix_norm", "w_in", "conv_w", "conv_b", "s5_A_re", "s5_A_im",
            "s5_log_dt", "s5_B_re", "s5_B_im", "s5_C_re", "s5_C_im", "s5_D", "s5_w_glu", "s5_b_glu", "ssd_A_log", "ssd_dt_bias",
            "ssd_D", "ssd_norm", "w_proj_s5", "w_proj_ssd", "b_gate", "w_out", "ffn2_norm", "ffn2_w_gate", "ffn2_w_up",
            "ffn2_w_down", "final_norm")
def _with_carry(res, carry):
    return res if carry else (res, [])


def _ffn_fwd(x, n, wgu, wd_of, tag, carries=(None, None, None), h=None):
    D = x.shape[1]
    if h is None:
        h = _rows(_f_rmsnorm, [x], [n], [(D, bf16)], name=tag + "_norm")[0]
    ab, got0 = _with_carry(_mm(h, wgu, carry=carries[0], name=tag + "_gate_up"), carries[0])
    wd = wd_of(got0)
    F = wd.shape[0]
    (c,), got1 = _with_carry(_rows(_f_swiglu, [ab], [], [(F, bf16)], carry=carries[1], name=tag + "_act"), carries[1])
    y, got2 = _with_carry(_mm(c, wd, scale=0.5, add=x, carry=carries[2], name=tag + "_down"), carries[2])
    return y, (x, n, h, ab, c), (got0, got1, got2)


def _ffn_bwd(saved, wgu, wd, dy, tag, carry_after_dwd=None, carry_after_dwgu=None):
    x, n, h, ab, c = saved
    F = wd.shape[0]
    def act_bwd(dc, ab_):
        return jax.vjp(lambda t: _f_swiglu(t)[0], ab_)[1](dc)[0]

    dab = _mm(dy, wd, tb=True, scale=0.5, tm=EPI_TM, epi=(act_bwd, ab, 2 * F), out_dtype=bf16, name=tag + "_d_act")
    dwd = _mm(c, dy, ta=True, o_blk="m", o_slots=True, tm=F // 2, out_dtype=bf16, scale=0.5, name=tag + "_d_wdown")
    carry_w = carry_after_dwd(dwd) if carry_after_dwd else None
    dwgu, arr_w = _with_carry(_mm(h, dab, ta=True, carry=carry_w, name=tag + "_d_wgu"), carry_w)
    dwg, dwu = _reshard_pair(dwgu, name=tag + "_reshard_d_wgu")
    carry_h = carry_after_dwgu(dwg, dwu) if carry_after_dwgu else None
    dh, arr_h = _with_carry(_mm(dab, wgu, tb=True, carry=carry_h, name=tag + "_d_h"), carry_h)
    (dx,), (dn,) = _rows_bwd(_f_rmsnorm, [x], [n], [dh], name=tag + "_norm_bwd", want_rows=[0], adds={0: dy})
    return dx, dn, dwg, dwu, dwd, arr_w, arr_h


def kernel(x, ffn1_norm, ffn1_w_gate, ffn1_w_up, ffn1_w_down, mix_norm, w_in, conv_w, conv_b, s5_A_re, s5_A_im, s5_log_dt, s5_B_re, s5_B_im, s5_C_re, s5_C_im, s5_D, s5_w_glu, s5_b_glu, ssd_A_log, ssd_dt_bias, ssd_D, ssd_norm, w_proj_s5, w_proj_ssd, b_gate, w_out, ffn2_norm, ffn2_w_gate, ffn2_w_up, ffn2_w_down, final_norm, loss_target, m_ffn1_norm, m_ffn1_w_gate, m_ffn1_w_up, m_ffn1_w_down, m_mix_norm, m_w_in, m_conv_w, m_conv_b, m_s5_A_re, m_s5_A_im, m_s5_log_dt, m_s5_B_re, m_s5_B_im, m_s5_C_re, m_s5_C_im, m_s5_D, m_s5_w_glu, m_s5_b_glu, m_ssd_A_log, m_ssd_dt_bias, m_ssd_D, m_ssd_norm, m_w_proj_s5, m_w_proj_ssd, m_b_gate, m_w_out, m_ffn2_norm, m_ffn2_w_gate, m_ffn2_w_up, m_ffn2_w_down, m_final_norm, v_ffn1_norm, v_ffn1_w_gate, v_ffn1_w_up, v_ffn1_w_down, v_mix_norm, v_w_in, v_conv_w, v_conv_b, v_s5_A_re, v_s5_A_im, v_s5_log_dt, v_s5_B_re, v_s5_B_im, v_s5_C_re, v_s5_C_im, v_s5_D, v_s5_w_glu, v_s5_b_glu, v_ssd_A_log, v_ssd_dt_bias, v_ssd_D, v_ssd_norm, v_w_proj_s5, v_w_proj_ssd, v_b_gate, v_w_out, v_ffn2_norm, v_ffn2_w_gate, v_ffn2_w_up, v_ffn2_w_down, v_final_norm):
    P = dict(locals())
    T, D = x.shape[1], x.shape[2]
    x0, tgt = x[0], loss_target[0]
    sh = {k: P[k][0] for k in _BIG}

    send = {k: (sh[k] if k == "conv_w" else sh[k].astype(bf16)) for k in _BIG}
    W = {}

    def gather_in(keys):
        return _carry_gather([send[k] for k in keys])

    first_keys = ("ffn1_w_gate", "ffn1_w_up", "conv_w")
    (h1,), got = _rows(_f_rmsnorm, [x0], [ffn1_norm], [(D, bf16)], carry=gather_in(first_keys), name="ffn1_norm")
    W.update(zip(first_keys, got))
    whole = lambda k: W[k].reshape(-1, D)
    conv_w_full = W["conv_w"].transpose(1, 0, 2).reshape(CONV_K, -1)

    d_inner = N_DEV * sh["w_proj_ssd"].shape[0]
    conv_dim = conv_w_full.shape[1]
    H = ssd_A_log.shape[1]
    G = (conv_dim - d_inner) // (2 * SSD_STATE)
    hpg = H // G
    nc = T // CHUNK
    Gs = D // S5_GROUP
    nblk = Gs // S5_GPB
    NS = Gs * S5_STATE
    seg_lens = (D, d_inner, conv_dim, H, 2 * D)

    cuts = [0, D // 3 // 16 * 16, D // 3 // 16 * 16 + 3 * D // 8 // 16 * 16, D]
    win_rows = [send["w_in"][a_:b_] for a_, b_ in zip(cuts[:-1], cuts[1:])]
    wgu1 = _unshard_pair(W["ffn1_w_gate"], W["ffn1_w_up"], name="unshard_ffn1_gate_up")

    def ffn1_down(got):
        W["ffn1_w_down"] = got[0]
        return whole("ffn1_w_down")

    x1, sv1, (got0, got1, got2) = _ffn_fwd(
        x0, ffn1_norm, wgu1, ffn1_down, "ffn1", h=h1,
        carries=(_carry_gather([send["ffn1_w_down"], win_rows[0]]), _carry_gather([win_rows[1]]), _carry_gather([win_rows[2]])))
    ffn1_w = (wgu1, whole("ffn1_w_down"))
    W["w_in"] = jnp.concatenate([got0[1], got1[0], got2[0]], axis=1)
    w_u, w_z, w_xbc, w_dt, w_gl = _unshard_w_in(W["w_in"], seg_lens, name="unshard_w_in")
    h2 = _rows(_f_rmsnorm, [x1], [mix_norm], [(D, bf16)], name="mix_norm")[0]
    u_p = _mm(h2, w_u, o_seg=True, name="in_u")
    z = _mm(h2, w_z, name="in_z")
    xbc = _mm(h2, w_xbc, name="in_xbc")
    gl = _mm(h2, w_gl, name="in_gate")
    dtr = _mm(h2, w_dt, name="in_dt")

    rep = lambda a: jnp.repeat(a, S5_GROUP, axis=0)
    lr, li, ldt = s5_A_re[0], s5_A_im[0], s5_log_dt[0].reshape(Gs, 1)
    brt = s5_B_re[0].transpose(0, 2, 1).reshape(Gs * S5_GROUP, S5_STATE)
    bit = s5_B_im[0].transpose(0, 2, 1).reshape(Gs * S5_GROUP, S5_STATE)
    prep_args = (lr, li, ldt, rep(lr), rep(li), rep(ldt), brt, bit)
    ar, ai, bbrt, bbit = _s5_prep(prep_args, name="s5_prep")
    a_r, a_i = ar.reshape(1, NS), ai.reshape(1, NS)
    wb_r = _block_diag(bbrt.reshape(nblk, S5_GPB, S5_GROUP, S5_STATE)).astype(bf16)
    wb_i = _block_diag(bbit.reshape(nblk, S5_GPB, S5_GROUP, S5_STATE)).astype(bf16)
    c4r = s5_C_re[0].reshape(nblk, S5_GPB, S5_GROUP, S5_STATE).transpose(0, 1, 3, 2)
    c4i = s5_C_im[0].reshape(nblk, S5_GPB, S5_GROUP, S5_STATE).transpose(0, 1, 3, 2)
    wc_r, wc_i = _block_diag(c4r).astype(bf16), _block_diag(c4i).astype(bf16)
    mix_keys = ("s5_w_glu", "w_proj_s5", "w_proj_ssd", "w_out")
    sl_r, sl_i, p_r, p_i, got = _s5_local_scan(u_p, wb_r, wb_i, a_r, a_i, reverse=False, carry=gather_in(mix_keys), name="s5_scan")
    W.update(zip(mix_keys, got))
    w_glu, w_p5, w_pssd, w_o = whole("s5_w_glu"), whole("w_proj_s5"), whole("w_proj_ssd"), whole("w_out")
    c_r, c_i = _s5_carry(sl_r[T - NSEG:], sl_i[T - NSEG:], p_r, p_i, reverse=False, name="s5_carry")
    s_r, s_i, ylin = _s5_fix_out(sl_r, sl_i, a_r, a_i, c_r, c_i, wc_r, wc_i, name="s5_fix_out")
    g5 = _rows(_f_s5_post, [ylin, u_p], [s5_D], [(D, f32)], name="s5_gelu")[0]
    v5 = _mm(g5, w_glu, name="s5_glu_mm")
    o5 = _rows(_f_glu, [g5, v5], [s5_b_glu], [(D, bf16)], name="s5_glu")[0]
    p5 = _mm(o5, w_p5, a_seg=True, name="proj_s5")

    xc = _conv_fwd(xbc, conv_w_full, conv_b, name="conv")
    bias_p, alog_p = _pad_cols(ssd_dt_bias, LANES), _pad_cols(ssd_A_log, LANES)
    expand = (lax.broadcasted_iota(jnp.int32, (LANES, d_inner), 1) // HEADDIM
              == lax.broadcasted_iota(jnp.int32, (LANES, d_inner), 0)).astype(f32)
    dt_p, da_p, dt4, a4 = _rows(_f_dt_expand, [dtr], [bias_p, alog_p, expand],
                                [(LANES, f32), (LANES, f32), (d_inner, f32), (d_inner, f32)], name="ssd_dt")
    row_l = lambda a: a[:, :H].reshape(nc, CHUNK, G, hpg).transpose(2, 0, 3, 1).reshape(G, nc, hpg * CHUNK)
    ssd_in = (xc, dt4, a4, row_l(dt_p), row_l(da_p), jnp.repeat(ssd_D, HEADDIM, axis=1), _ssd_consts(hpg))
    ffn2_keys = ("ffn2_w_gate", "ffn2_w_up", "ffn2_w_down")
    y_ssd, hs, got = _ssd2_fwd(*ssd_in, d_inner=d_inner, carry=gather_in(ffn2_keys), name="ssd")
    W.update(zip(ffn2_keys, got))
    ffn2_w = (_unshard_pair(W["ffn2_w_gate"], W["ffn2_w_up"], name="unshard_ffn2_gate_up"), whole("ffn2_w_down"))
    yn = _rows(_f_gated_norm, [y_ssd, z], [ssd_norm], [(d_inner, bf16)], name="ssd_gated_norm")[0]
    pssd = _mm(yn, w_pssd, name="proj_ssd")

    merged = _rows(_f_merge, [gl, p5, pssd], [b_gate], [(D, bf16)], name="merge")[0]
    x2 = _mm(merged, w_o, add=x1, name="out_proj")
    x3, sv2, _ = _ffn_fwd(x2, ffn2_norm, ffn2_w[0], lambda _: ffn2_w[1], "ffn2")
    lossv, dx3, d_final = _loss_stage(x3, tgt, final_norm.reshape(1, D), name="loss")

    gw = {}
    gs = {"final_norm": d_final}
    slot_mm = lambda a_, b_, name, **kw: _mm(a_, b_, ta=True, o_blk="m", o_slots=True, out_dtype=bf16, name=name, **kw)
    core = lax.axis_index("c").astype(jnp.int32).reshape(1)
    chip = (2 * lax.axis_index("x") + lax.axis_index("y")).astype(jnp.int32).reshape(1)
    chip_sums, arrived = {}, {}

    def level1(keys, tag):
        sib = _exchange_sibling([gw[k] for k in keys], name="exchange_sibling_" + tag)
        for k, s_ in zip(keys, sib):
            chip_sums[k] = _chip_sum(gw[k], s_, core, name="chip_sum_" + k)
        return [chip_sums[k] for k in keys]

    dx2, gs["ffn2_norm"], gw["ffn2_w_gate"], gw["ffn2_w_up"], gw["ffn2_w_down"], _, _ = _ffn_bwd(sv2, *ffn2_w, dx3, "ffn2")

    dmerged = _mm(dx2, w_o, tb=True, name="d_merged")
    gw["w_out"] = slot_mm(merged, dx2, "d_w_out")
    (dgl, dp5, dpssd), (gs["b_gate"],) = _rows_bwd(_f_merge, [gl, p5, pssd], [b_gate], [dmerged], name="merge_bwd", want_rows=[0, 1, 2])

    dyn = _mm(dpssd, w_pssd, tb=True, name="d_yn")
    gw["w_proj_ssd"] = slot_mm(yn, dpssd, "d_w_proj_ssd")
    group_a = ("ffn2_w_gate", "ffn2_w_up", "ffn2_w_down", "w_out", "w_proj_ssd")
    parts_a = level1(group_a, "a")
    (dyssd, dz), (gs["ssd_norm"],) = _rows_bwd(_f_gated_norm, [y_ssd, z], [ssd_norm], [dyn], name="ssd_gated_norm_bwd", want_rows=[0, 1])
    dxs, dbm, dcm, ddtc, ddac, ddtw, ddaw, ddh = _ssd2_bwd(*ssd_in, hs, dyssd, d_inner=d_inner, name="ssd_bwd")

    def fold(col, row):
        col = col.reshape(T, G, LANES)[:, :, :hpg].reshape(T, H)
        row = row.reshape(G, nc, hpg, CHUNK).transpose(1, 3, 0, 2).reshape(T, H)
        return _pad_cols(col + row, LANES)

    (ddtr,), (dbias_p, dalog_p) = _rows_bwd(_f_dt, [dtr], [bias_p, alog_p], [fold(ddtc, ddtw), fold(ddac, ddaw)], name="ssd_dt_bwd", want_rows=[0])
    gs["ssd_dt_bias"], gs["ssd_A_log"], gs["ssd_D"] = dbias_p[:, :H], dalog_p[:, :H], ddh[:, 0, :hpg].reshape(1, H)
    dxbc, d_conv_w, gs["conv_b"], arr = _conv_bwd(
        xbc, conv_w_full, conv_b, [dxs, dbm, dcm], carry=_carry_chips(parts_a), name="conv_bwd")
    arrived.update(zip(group_a, arr))
    cwk = sh["conv_w"].shape[1]
    gw["conv_w"] = d_conv_w.reshape(CONV_K, N_CHIP, 2, cwk).transpose(2, 1, 0, 3).reshape(N_DEV, CONV_K, cwk)

    do5 = _mm(dp5, w_p5, tb=True, o_seg=True, name="d_o5")
    gw["w_proj_s5"] = slot_mm(o5, dp5, "d_w_proj_s5", a_seg=True)
    (dg5a, dv5), (gs["s5_b_glu"],) = _rows_bwd(_f_glu, [g5, v5], [s5_b_glu], [do5], name="s5_glu_bwd", want_rows=[0, 1])
    dg5 = _mm(dv5, w_glu, tb=True, add=dg5a, name="d_g5")
    gw["s5_w_glu"] = slot_mm(g5, dv5, "d_w_glu")
    (dylin, du_a), (gs["s5_D"],) = _rows_bwd(_f_s5_post, [ylin, u_p], [s5_D], [dg5], name="s5_gelu_bwd", want_rows=[0, 1])
    wct_r, wct_i = wc_r.transpose(0, 2, 1), -wc_i.transpose(0, 2, 1)
    ql_r, ql_i, _, _ = _s5_local_scan(dylin, wct_r, wct_i, a_r, -a_i, reverse=True, powers=False, name="s5_scan_bwd")
    cb_r, cb_i = _s5_carry(ql_r[:NSEG], ql_i[:NSEG], p_r, -p_i, reverse=True, name="s5_carry_bwd")
    tc = min(S5_TC, T)

    def before_blocks(s):
        last = s.reshape(T // tc, tc, NS)[:, tc - NSEG:, :]
        wrap = jnp.concatenate([jnp.zeros((1, 1, NS), f32), last[-1:, : NSEG - 1, :]], axis=1)
        return jnp.concatenate([wrap, last[:-1]], axis=0)

    du_p, dwb_r, dwb_i, dwc_r, dwc_i, d_ar, d_ai = _s5_fix_bwd(
        ql_r, ql_i, a_r, -a_i, cb_r, cb_i, s_r, s_i, before_blocks(s_r), before_blocks(s_i), u_p, dylin, du_a, wb_r, wb_i, name="s5_fix_bwd")
    unblk = lambda w: _block_diag_t(w, S5_GROUP, S5_STATE).reshape(Gs * S5_GROUP, S5_STATE)
    rsum = jnp.repeat(jnp.eye(Gs, dtype=f32), S5_GROUP, axis=1)
    d_lr, d_li, d_ldt, d_brt, d_bit = _s5_prep_bwd(
        prep_args, (d_ar.reshape(Gs, S5_STATE), d_ai.reshape(Gs, S5_STATE), unblk(dwb_r), unblk(dwb_i)), rsum, name="s5_prep_bwd")
    gs["s5_A_re"], gs["s5_A_im"], gs["s5_log_dt"] = d_lr, d_li, d_ldt.reshape(1, Gs)
    gs["s5_B_re"] = d_brt.reshape(Gs, S5_GROUP, S5_STATE).transpose(0, 2, 1)
    gs["s5_B_im"] = d_bit.reshape(Gs, S5_GROUP, S5_STATE).transpose(0, 2, 1)
    gs["s5_C_re"] = _block_diag_t(dwc_r, S5_STATE, S5_GROUP).transpose(0, 1, 3, 2).reshape(Gs, S5_GROUP, S5_STATE)
    gs["s5_C_im"] = _block_diag_t(dwc_i, S5_STATE, S5_GROUP).transpose(0, 1, 3, 2).reshape(Gs, S5_GROUP, S5_STATE)

    d_w_in = [_mm(h2, du_p, ta=True, b_seg=True, name="d_w_u"), _mm(h2, dz, ta=True, name="d_w_z"), _mm(h2, dxbc, ta=True, name="d_w_xbc"),
              _mm(h2, ddtr, ta=True, name="d_w_dt"), _mm(h2, dgl, ta=True, name="d_w_gate")]
    gw["w_in"] = _reshard_w_in(d_w_in, seg_lens, sh["w_in"].shape[1], name="reshard_d_w_in")
    group_c = ("w_proj_s5", "s5_w_glu", "conv_w")
    parts_c = level1(group_c + ("w_in",), "c")
    c1 = int(D * 0.45) // 16 * 16
    c2 = c1 + D // 4 // 16 * 16
    win = [parts_c[3][:, :c1], parts_c[3][:, c1:c2], parts_c[3][:, c2:]]
    dh2 = _mm(du_p, w_u, tb=True, a_seg=True, name="d_h2_u")
    dh2, arr = _mm(dz, w_z, tb=True, add=dh2, carry=_carry_chips(parts_c[:3]), name="d_h2_z")
    arrived.update(zip(group_c, arr))
    dh2, (arr0,) = _mm(dxbc, w_xbc, tb=True, add=dh2, carry=_carry_chips([win[0]]), name="d_h2_xbc")
    dh2, (arr1,) = _mm(dgl, w_gl, tb=True, add=dh2, carry=_carry_chips([win[1]]), name="d_h2_gate")
    dh2 = _mm(ddtr, w_dt, tb=True, add=dh2, name="d_h2_dt")
    (dx1,), (gs["mix_norm"],) = _rows_bwd(_f_rmsnorm, [x1], [mix_norm], [dh2], name="mix_norm_bwd", want_rows=[0], adds={0: dx2})

    def carry_ffn1_down(dwd):
        gw["ffn1_w_down"] = dwd
        return _carry_chips(level1(("ffn1_w_down",), "d") + [win[2]])

    def carry_ffn1_gate_up(dwg, dwu):
        gw["ffn1_w_gate"], gw["ffn1_w_up"] = dwg, dwu
        return _carry_chips(level1(("ffn1_w_gate", "ffn1_w_up"), "e"))

    dx0, gs["ffn1_norm"], _, _, _, arr_w, arr_h = _ffn_bwd(
        sv1, *ffn1_w, dx1, "ffn1", carry_after_dwd=carry_ffn1_down, carry_after_dwgu=carry_ffn1_gate_up)
    arrived["ffn1_w_down"], arr2 = arr_w
    arrived["ffn1_w_gate"], arrived["ffn1_w_up"] = arr_h
    arrived["w_in"] = jnp.concatenate([arr0, arr1, arr2], axis=1)

    small_shapes = {k: (P[k][0].shape if P[k].ndim > 1 else P[k].shape) for k in _SMALL}
    pack = lambda d: jnp.concatenate([_pad_flat(d[k], TILE_ELEMS) for k in _SMALL]).reshape(-1, LANES)
    gsmall = _sum_slots(_all_gather([pack(gs)], name="gather_small_grads")[0], name="sum_small_grads")
    snum = {k: math.prod(small_shapes[k]) for k in _SMALL}
    ssz = {k: -(-snum[k] // TILE_ELEMS) * TILE_ELEMS for k in _SMALL}

    grads, delta, new_m, new_v = {}, {}, {}, {}
    for k in _BIG:
        grads[k], delta[k], new_m[k], new_v[k] = _reduce_adamw(
            chip_sums[k], arrived[k], chip, P[k][0], P["m_" + k][0], P["v_" + k][0], name="adamw_" + k)
    d_s, m_s, v_s = _adamw(gsmall, pack({k: P[k] for k in _SMALL}), pack({k: P["m_" + k] for k in _SMALL}),
                           pack({k: P["v_" + k] for k in _SMALL}), name="adamw_small")
    off = 0
    gflat, dflat, mflat, vflat = gsmall.reshape(-1), d_s.reshape(-1), m_s.reshape(-1), v_s.reshape(-1)
    for k in _SMALL:
        n = snum[k]
        grads[k], delta[k], new_m[k], new_v[k] = (a[off:off + n] for a in (gflat, dflat, mflat, vflat))
        off += ssz[k]

    loss = lax.psum(lossv[0, 0], ("x", "y", "c"))
    out = [loss, dx0.reshape(x.shape)]
    for d in (grads, delta, new_m, new_v):
        out += [d[k].reshape(P[k].shape) for k in _WEIGHTS]
    return tuple(out)
```

```python
import math
from typing import Callable, NamedTuple

import jax
import jax.numpy as jnp
from jax import lax
from jax.experimental import pallas as pl
from jax.experimental.pallas import tpu as pltpu

f32 = jnp.float32
bf16 = jnp.bfloat16
_S = jax.ShapeDtypeStruct

EPS = 1e-6
S5_GROUP = 16
S5_STATE = 64
HEADDIM = 64
SSD_STATE = 128
CHUNK = 64
CONV_K = 4
NSEG = 8
S5_GPB = 16
N_DEV = 8
LANES = 128
TILE_ELEMS = 8 * LANES

ADAM_LR = 0.001
ADAM_B1 = 0.9
ADAM_B2 = 0.999
ADAM_EPS = 1e-08
ADAM_WD = 0.01
ADAM_STEP = 10

VMEM_LIMIT = 56 * 1024 * 1024
MM_FULL_K = 3072
MM_MAX_TN = 3072
EPI_TM = 128
ROW_BLOCK_BYTES = 12 * 1024 * 1024
EW_BLOCK_ELEMS = 1 << 20
MESH = pl.DeviceIdType.MESH


def _cparams(sem=None):
    return pltpu.CompilerParams(dimension_semantics=sem, vmem_limit_bytes=VMEM_LIMIT)


def _pick(dim, pref, align=LANES):
    best = None
    t = align
    while t <= min(dim, pref):
        if dim % t == 0:
            best = t
        t += align
    return best or dim


def _slot_of(k):
    return (k & 1) * (N_DEV // 2) + (k >> 1)


def _mm(a, b, *, name, ta=False, tb=False, a_blk=None, b_blk=None, o_blk=None, o_slots=False, a_seg=False, b_seg=False,
        o_seg=False, tm=None, out_dtype=f32, scale=1.0, add=None, epi=None, carry=None):
    a2, b2 = a.shape[-2:], b.shape[-2:]
    Ma, Ka = (a2[1], a2[0]) if ta else a2
    Kb, Nb = (b2[1], b2[0]) if tb else b2
    M = Ma * (a.shape[0] if a_blk == "m" else 1)
    K = Ka * (a.shape[0] if a_blk == "k" else 1)
    N = Nb * (b.shape[0] if b_blk == "n" else 1)
    assert K == Kb * (b.shape[0] if b_blk == "k" else 1), (a.shape, b.shape, ta, tb, a_blk, b_blk)
    assert (a.ndim == 3) == (a_blk is not None) and (b.ndim == 3) == (b_blk is not None)
    tm = Ma if a_blk == "m" else (tm or _pick(M, 512))
    tn = Nb if b_blk == "n" else _pick(N, MM_MAX_TN)
    if a_blk == "k" or b_blk == "k":
        tk = Ka if a_blk == "k" else Kb
        assert tk == (Kb if b_blk == "k" else tk)
    else:
        tk = K if K <= MM_FULL_K else _pick(K, 1024 if ta else MM_FULL_K)
    if (a_seg and not ta) or o_seg:
        tm = M // NSEG
    if (a_seg and ta) or b_seg:
        tk = K // NSEG
    gm, gn, nk = M // tm, N // tn, K // tk
    assert not (add is not None and (o_seg or o_blk)) and not (o_blk and o_seg)

    if a_seg:
        assert a.ndim == 2
        a = a.reshape(a.shape[0] // NSEG, NSEG * a.shape[1])
        if ta:
            a_spec = pl.BlockSpec((tk, tm), lambda i, j, k: (0, k * (Ma // tm) + i))
        else:
            a_spec = pl.BlockSpec((tm, tk), lambda i, j, k: (0, i * (Ka // tk) + k))
    elif a.ndim == 3:
        lead = (lambda i, k: i) if a_blk == "m" else (lambda i, k: k)
        if ta:
            a_spec = pl.BlockSpec((None, tk, tm), lambda i, j, k: (lead(i, k), 0 if a_blk == "k" else k, 0 if a_blk == "m" else i))
        else:
            a_spec = pl.BlockSpec((None, tm, tk), lambda i, j, k: (lead(i, k), 0 if a_blk == "m" else i, 0 if a_blk == "k" else k))
    else:
        a_spec = pl.BlockSpec((tk, tm), lambda i, j, k: (k, i)) if ta else pl.BlockSpec((tm, tk), lambda i, j, k: (i, k))
    if b_seg:
        assert b.ndim == 2 and not tb
        b = b.reshape(b.shape[0] // NSEG, NSEG * b.shape[1])
        b_spec = pl.BlockSpec((tk, tn), lambda i, j, k: (0, k * (Nb // tn) + j))
    elif b.ndim == 3:
        lead = (lambda j, k: j) if b_blk == "n" else (lambda j, k: k)
        if tb:
            b_spec = pl.BlockSpec((None, tn, tk), lambda i, j, k: (lead(j, k), 0 if b_blk == "n" else j, 0 if b_blk == "k" else k))
        else:
            b_spec = pl.BlockSpec((None, tk, tn), lambda i, j, k: (lead(j, k), 0 if b_blk == "k" else k, 0 if b_blk == "n" else j))
    else:
        b_spec = pl.BlockSpec((tn, tk), lambda i, j, k: (j, k)) if tb else pl.BlockSpec((tk, tn), lambda i, j, k: (k, j))
    slot = _slot_of if o_slots else (lambda k: k)
    if o_blk == "n":
        assert gn == N_DEV or not o_slots
        o_shape, o_spec = (gn, M, tn), pl.BlockSpec((None, tm, tn), lambda i, j, k: (slot(j), i, 0))
    elif o_blk == "m" and o_slots and gm < N_DEV:
        rs = M // N_DEV
        per_tile = tm // rs
        assert per_tile % 2 == 0 and tm % rs == 0
        o_shape = (2, N_CHIP, rs, N)
        o_spec = pl.BlockSpec((2, per_tile // 2, rs, tn), lambda i, j, k: (0, i, 0, j))
    elif o_blk == "m":
        assert gm == N_DEV or not o_slots
        o_shape, o_spec = (gm, tm, N), pl.BlockSpec((None, tm, tn), lambda i, j, k: (slot(i), 0, j))
    elif o_seg:
        o_shape, o_spec = (tm, NSEG * N), pl.BlockSpec((tm, tn), lambda i, j, k: (0, i * (N // tn) + j))
    else:
        o_shape, o_spec = (M, N), pl.BlockSpec((tm, tn), lambda i, j, k: (i, j))
    dims = (((0 if ta else 1,), (1 if tb else 0,)), ((), ()))
    if epi is not None:
        assert gn == 1 and add is None and o_blk is None and not o_seg
        epi_fn, add, epi_w = epi
        o_shape, o_spec = (M, epi_w), pl.BlockSpec((tm, epi_w), lambda i, j, k: (i, 0))
    has_add = add is not None
    add_spec = pl.BlockSpec((tm, add.shape[1]), lambda i, j, k: (i, 0)) if epi is not None else o_spec

    carry = carry or _NO_CARRY
    n_in = 2 + has_add

    def body(*refs):
        own, c_in, c_out, c_sems = _carry_split(carry, refs, n_in, 1)
        a_ref, b_ref = own[0], own[1]
        add_ref = own[2] if has_add else None
        o_ref, acc_ref = own[-2], own[-1]
        i, j, k = pl.program_id(0), pl.program_id(1), pl.program_id(2)
        _carry_start(carry, c_in, c_out, c_sems, (i == 0) & (j == 0) & (k == 0))

        @pl.when(k == 0)
        def _():
            acc_ref[...] = jnp.zeros_like(acc_ref)

        acc_ref[...] += lax.dot_general(a_ref[...].astype(bf16), b_ref[...].astype(bf16), dims, preferred_element_type=f32)

        @pl.when(k == nk - 1)
        def _():
            r = acc_ref[...] * scale
            if epi is not None:
                r = epi_fn(r, add_ref[...].astype(f32))
            elif has_add:
                r = r + add_ref[...].astype(f32)
            if len(o_shape) == 4:
                rs = o_shape[2]
                for chip_l in range(o_ref.shape[1]):
                    for core in range(2):
                        dev = 2 * chip_l + core
                        o_ref[core, chip_l] = r[dev * rs:(dev + 1) * rs].astype(out_dtype)
            else:
                o_ref[...] = r.astype(out_dtype)

        _carry_finish(carry, c_in, c_out, c_sems, (i == gm - 1) & (j == gn - 1) & (k == nk - 1))

    ins = [a, b] + ([add] if has_add else []) + list(carry.ins)
    in_specs = [a_spec, b_spec] + ([add_spec] if has_add else []) + [_ANY] * len(carry.ins)
    res = pl.pallas_call(
        body, name=name, grid=(gm, gn, nk), in_specs=in_specs, out_specs=[o_spec] + [_ANY] * len(carry.out_shapes),
        out_shape=[_S(o_shape, out_dtype)] + list(carry.out_shapes),
        scratch_shapes=[pltpu.VMEM((tm, tn), f32)] + list(carry.sems),
        compiler_params=_cparams(("arbitrary",) * 3 if carry.ins else ("parallel", "parallel", "arbitrary")),
    )(*ins)
    out = res[0]
    if len(o_shape) == 4:
        out = out.reshape(N_DEV, o_shape[2], N)
    elif o_seg:
        out = out.reshape(M, N)
    return (out, list(res[1:])) if carry.ins else out


def _row_tile(T, widths):
    budget = ROW_BLOCK_BYTES
    tb = max(16, budget // (4 * sum(widths)))
    return _pick(T, tb, align=16)


def _rows(fn, rows, params, outs, *, name, carry=None):
    T = rows[0].shape[0]
    nr, npar = len(rows), len(params)
    tb = _row_tile(T, [r.shape[1] for r in rows] + [w for w, _ in outs])
    carry = carry or _NO_CARRY

    def body(*refs):
        own, c_in, c_out, c_sems = _carry_split(carry, refs, nr + npar, len(outs))
        _carry_start(carry, c_in, c_out, c_sems, pl.program_id(0) == 0)
        ins = [r[...].astype(f32) for r in own[: nr + npar]]
        res = fn(*ins)
        for o_ref, r in zip(own[nr + npar:], res):
            o_ref[...] = r.astype(o_ref.dtype)
        _carry_finish(carry, c_in, c_out, c_sems, pl.program_id(0) == T // tb - 1)

    in_specs = [pl.BlockSpec((tb, r.shape[1]), lambda i: (i, 0)) for r in rows]
    in_specs += [pl.BlockSpec(p.shape, lambda i: (0, 0)) for p in params]
    out_specs = [pl.BlockSpec((tb, w), lambda i: (i, 0)) for w, _ in outs]
    res = pl.pallas_call(
        body, name=name, grid=(T // tb,), in_specs=in_specs + [_ANY] * len(carry.ins),
        out_specs=out_specs + [_ANY] * len(carry.out_shapes), out_shape=[_S((T, w), d) for w, d in outs] + list(carry.out_shapes),
        scratch_shapes=list(carry.sems), compiler_params=_cparams(("arbitrary",) if carry.ins else ("parallel",)),
    )(*rows, *params, *carry.ins)
    return (tuple(res[:len(outs)]), list(res[len(outs):])) if carry.ins else tuple(res)


def _rows_bwd(fn, rows, params, cots, *, name, want_rows, row_dtypes=None, adds=None):
    T = rows[0].shape[0]
    nr, npar, nc = len(rows), len(params), len(cots)
    adds = adds or {}
    add_idx = sorted(adds)
    row_dtypes = row_dtypes or {}
    widths = [r.shape[1] for r in rows] + [c.shape[1] for c in cots] + [rows[i].shape[1] for i in want_rows]
    tb = _row_tile(T, widths)

    def body(*refs):
        ins = [r[...].astype(f32) for r in refs[: nr + npar]]
        cot = tuple(r[...].astype(f32) for r in refs[nr + npar: nr + npar + nc])
        add_refs = refs[nr + npar + nc: nr + npar + nc + len(add_idx)]
        out_refs = refs[nr + npar + nc + len(add_idx):]
        _, vjp = jax.vjp(lambda *a: tuple(fn(*a)), *ins)
        g = vjp(cot)
        for o_ref, i in zip(out_refs[: len(want_rows)], want_rows):
            r = g[i]
            if i in adds:
                r = r + add_refs[add_idx.index(i)][...].astype(f32)
            o_ref[...] = r.astype(o_ref.dtype)
        first = pl.program_id(0) == 0
        for o_ref, gp in zip(out_refs[len(want_rows):], g[nr:]):
            @pl.when(first)
            def _(o_ref=o_ref):
                o_ref[...] = jnp.zeros_like(o_ref)

            o_ref[...] += gp

    in_specs = [pl.BlockSpec((tb, r.shape[1]), lambda i: (i, 0)) for r in rows]
    in_specs += [pl.BlockSpec(p.shape, lambda i: (0, 0)) for p in params]
    in_specs += [pl.BlockSpec((tb, c.shape[1]), lambda i: (i, 0)) for c in cots]
    in_specs += [pl.BlockSpec((tb, adds[i].shape[1]), lambda i_: (i_, 0)) for i in add_idx]
    out_specs = [pl.BlockSpec((tb, rows[i].shape[1]), lambda i_: (i_, 0)) for i in want_rows]
    out_specs += [pl.BlockSpec(p.shape, lambda i: (0, 0)) for p in params]
    out_shape = [_S(rows[i].shape, row_dtypes.get(i, f32)) for i in want_rows] + [_S(p.shape, f32) for p in params]
    res = pl.pallas_call(
        body, name=name, grid=(T // tb,), in_specs=in_specs, out_specs=out_specs, out_shape=out_shape,
        compiler_params=_cparams(("arbitrary",)),
    )(*rows, *params, *cots, *[adds[i] for i in add_idx])
    return list(res[: len(want_rows)]), list(res[len(want_rows):])


def _f_rmsnorm(x, g):
    return (x * lax.rsqrt(jnp.mean(x * x, axis=-1, keepdims=True) + EPS) * g,)


def _f_swiglu(ab):
    F = ab.shape[1] // 2
    return (jax.nn.silu(ab[:, :F]) * ab[:, F:],)


def _f_s5_post(y, u, d):
    return (jax.nn.gelu(y + d * u),)


def _f_glu(g, v, b):
    return (g * jax.nn.sigmoid(v + b),)


def _f_gated_norm(y, z, w):
    return _f_rmsnorm(y * jax.nn.silu(z), w)


def _f_merge(gl, p5, pssd, b):
    D = p5.shape[1]
    gates = jax.nn.sigmoid(gl + b)
    return (gates[:, :D] * p5 + gates[:, D:] * pssd,)


def _f_dt(dtr, bias, a_log):
    dt = jax.nn.softplus(dtr + bias)
    return dt, dt * (-jnp.exp(a_log))


def _f_dt_expand(dtr, bias, a_log, e):
    dt, a = _f_dt(dtr, bias, a_log)
    return dt, a, _doth(dt, e), _doth(a, e)


def _loss_stage(x, tgt, g, *, name):
    T, D = x.shape
    tb = _row_tile(T, [D, D, D])

    def f(xb, gb, tb_):
        y = _f_rmsnorm(xb, gb)[0]
        return 0.5 * jnp.sum(jnp.mean(jnp.square(y - tb_), axis=-1, keepdims=True), axis=0, keepdims=True)

    def body(x_ref, t_ref, g_ref, l_ref, dx_ref, dg_ref):
        tv = t_ref[...]
        val, vjp = jax.vjp(lambda a, b: f(a, b, tv), x_ref[...], g_ref[...])
        dx, dg = vjp(jnp.ones((1, 1), f32))
        dx_ref[...] = dx

        @pl.when(pl.program_id(0) == 0)
        def _():
            l_ref[...] = jnp.zeros_like(l_ref)
            dg_ref[...] = jnp.zeros_like(dg_ref)

        l_ref[...] += jnp.broadcast_to(val, l_ref.shape)
        dg_ref[...] += dg

    row = pl.BlockSpec((tb, D), lambda i: (i, 0))
    par = pl.BlockSpec((1, D), lambda i: (0, 0))
    return pl.pallas_call(
        body, name=name, grid=(T // tb,), in_specs=[row, row, par],
        out_specs=[pl.BlockSpec((1, LANES), lambda i: (0, 0)), row, par],
        out_shape=[_S((1, LANES), f32), _S((T, D), f32), _S((1, D), f32)], compiler_params=_cparams(("arbitrary",)),
    )(x, tgt, g)


CONV_R = 64
HALO = 8


def _conv_shifts_down(ref, t):
    if isinstance(t, int) and t == 0:
        cur = ref[0:CONV_R, :]
        row = lax.broadcasted_iota(jnp.int32, cur.shape, 0)
        return [cur] + [jnp.where(row >= s, pltpu.roll(cur, s, axis=0), 0.0) for s in range(1, CONV_K)]
    win = ref[pl.ds(pl.multiple_of(t * CONV_R - HALO, HALO), CONV_R + HALO), :]
    return [win[HALO:]] + [pltpu.roll(win, s, axis=0)[HALO:] for s in range(1, CONV_K)]


def _conv_shifts_up(ref, t):
    if isinstance(t, int):
        cur = ref[t * CONV_R:(t + 1) * CONV_R, :]
        row = lax.broadcasted_iota(jnp.int32, cur.shape, 0)
        return [cur] + [jnp.where(row < CONV_R - s, pltpu.roll(cur, CONV_R - s, axis=0), 0.0) for s in range(1, CONV_K)]
    win = ref[pl.ds(pl.multiple_of(t * CONV_R, HALO), CONV_R + HALO), :]
    return [win[:CONV_R]] + [pltpu.roll(win, CONV_R + HALO - s, axis=0)[:CONV_R] for s in range(1, CONV_K)]


def _conv_pre(shifted, w, b):
    pre = b
    for k in range(CONV_K):
        pre = pre + w[k:k + 1, :] * shifted[CONV_K - 1 - k]
    return pre


def _conv_fwd(x, w, b, *, name):
    T, C = x.shape
    cb = _pick(C, 256)

    def body(x_ref, w_ref, b_ref, o_ref):
        xv = x_ref[...]
        row = lax.broadcasted_iota(jnp.int32, xv.shape, 0)
        shifted = [xv] + [jnp.where(row >= s, pltpu.roll(xv, s, axis=0), 0.0) for s in range(1, CONV_K)]
        o_ref[...] = jax.nn.silu(_conv_pre(shifted, w_ref[...], b_ref[...]))

    col = pl.BlockSpec((T, cb), lambda j: (0, j))
    return pl.pallas_call(
        body, name=name, grid=(C // cb,), in_specs=[col, pl.BlockSpec((CONV_K, cb), lambda j: (0, j)), pl.BlockSpec((1, cb), lambda j: (0, j))],
        out_specs=col, out_shape=_S((T, C), f32), compiler_params=_cparams(("parallel",)),
    )(x, w, b)


def _conv_bwd(x, w, b, dy, *, name, carry=None):
    T, C = x.shape
    cb = _pick(C, 128)
    carry = carry or _NO_CARRY
    ends = []
    for d in dy:
        ends.append((ends[-1] if ends else 0) + d.shape[1] // cb)
    assert ends[-1] == C // cb and all(d.shape[1] % cb == 0 for d in dy)
    npc = len(dy)
    n_tiles = T // CONV_R

    def body(*refs):
        own, c_in, c_out, c_sems = _carry_split(carry, refs, 3 + npc, 3)
        x_ref, w_ref, b_ref = own[:3]
        dy_refs, (dx_ref, dw_ref, db_ref, dpre_ref) = own[3:3 + npc], own[3 + npc:]
        _carry_start(carry, c_in, c_out, c_sems, pl.program_id(0) == 0)
        j = pl.program_id(0)
        wv, bv = w_ref[...], b_ref[...]

        def fold8(v):
            return jnp.sum(v.reshape(CONV_R // 8, 8, cb), axis=0)

        def first_pass(t, acc):
            rows = slice(0, CONV_R) if isinstance(t, int) else pl.ds(pl.multiple_of(t * CONV_R, CONV_R), CONV_R)
            shifted = _conv_shifts_down(x_ref, t)
            pre = _conv_pre(shifted, wv, bv)
            dyv = dy_refs[-1][rows, :]
            for p in range(npc - 2, -1, -1):
                dyv = jnp.where(j < ends[p], dy_refs[p][rows, :], dyv)
            sg = jax.nn.sigmoid(pre)
            dpre = dyv * sg * (1.0 + pre * (1.0 - sg))
            dpre_ref[rows, :] = dpre
            return tuple(acc[k] + fold8(dpre * shifted[CONV_K - 1 - k]) for k in range(CONV_K)) + (acc[CONV_K] + fold8(dpre),)

        zero = jnp.zeros((8, cb), f32)
        acc = lax.fori_loop(1, n_tiles, first_pass, first_pass(0, (zero,) * (CONV_K + 1)), unroll=3)
        for k in range(CONV_K):
            dw_ref[k:k + 1, :] = jnp.sum(acc[k], axis=0, keepdims=True)
        db_ref[...] = jnp.sum(acc[CONV_K], axis=0, keepdims=True)

        def dx_of(t):
            up = _conv_shifts_up(dpre_ref, t)
            dx = wv[CONV_K - 1:CONV_K, :] * up[0]
            for k in range(CONV_K - 1):
                dx = dx + wv[k:k + 1, :] * up[CONV_K - 1 - k]
            return dx

        def second_pass(t, c):
            dx_ref[pl.ds(pl.multiple_of(t * CONV_R, CONV_R), CONV_R), :] = dx_of(t)
            return c

        lax.fori_loop(0, n_tiles - 1, second_pass, 0, unroll=3)
        dx_ref[(n_tiles - 1) * CONV_R:, :] = dx_of(n_tiles - 1)
        _carry_finish(carry, c_in, c_out, c_sems, pl.program_id(0) == C // cb - 1)

    col = pl.BlockSpec((T, cb), lambda j: (0, j))
    wsp = pl.BlockSpec((CONV_K, cb), lambda j: (0, j))
    bsp = pl.BlockSpec((1, cb), lambda j: (0, j))
    starts = [0] + ends[:-1]
    dy_specs = [pl.BlockSpec((T, cb), lambda j, s=s, e=e: (0, jnp.clip(j, s, e - 1) - s)) for s, e in zip(starts, ends)]
    res = pl.pallas_call(
        body, name=name, grid=(C // cb,), in_specs=[col, wsp, bsp] + dy_specs + [_ANY] * len(carry.ins),
        out_specs=[col, wsp, bsp] + [_ANY] * len(carry.out_shapes),
        out_shape=[_S((T, C), f32), _S((CONV_K, C), f32), _S((1, C), f32)] + list(carry.out_shapes),
        scratch_shapes=[pltpu.VMEM((T, cb), f32)] + list(carry.sems),
        compiler_params=_cparams(("arbitrary",) if carry.ins else ("parallel",)),
    )(x, w, b, *dy, *carry.ins)
    return (*res[:3], list(res[3:]))


def _f_s5_prep(lr, li, ldt, lrb, lib, ldtb, brt, bit):
    def disc(lr_, li_, ldt_):
        dt = jnp.exp(ldt_)
        mag = jnp.exp(lr_ * dt)
        ar, ai = mag * jnp.cos(li_ * dt), mag * jnp.sin(li_ * dt)
        den = lr_ * lr_ + li_ * li_
        cr = ((ar - 1.0) * lr_ + ai * li_) / den
        ci = (ai * lr_ - (ar - 1.0) * li_) / den
        return ar, ai, cr, ci

    ar, ai, _, _ = disc(lr, li, ldt)
    _, _, cr, ci = disc(lrb, lib, ldtb)
    return ar, ai, cr * brt - ci * bit, cr * bit + ci * brt


def _s5_prep(args, *, name):
    G, N = args[0].shape
    GM = args[3].shape[0]

    def body(*refs):
        res = _f_s5_prep(*[r[...] for r in refs[:8]])
        for o, r in zip(refs[8:], res):
            o[...] = r

    return pl.pallas_call(body, name=name, out_shape=[_S((G, N), f32)] * 2 + [_S((GM, N), f32)] * 2)(*args)


def _s5_prep_bwd(args, cots, rsum, *, name):
    G, N = args[0].shape
    GM = args[3].shape[0]

    def body(*refs):
        ins = [r[...] for r in refs[:8]]
        cot = tuple(r[...] for r in refs[8:12])
        rs = refs[12][...]
        _, vjp = jax.vjp(_f_s5_prep, *ins)
        g = vjp(cot)
        fold = lambda v: jnp.dot(rs, v, preferred_element_type=f32, precision=lax.Precision.HIGHEST)
        o = refs[13:]
        o[0][...] = g[0] + fold(g[3])
        o[1][...] = g[1] + fold(g[4])
        o[2][...] = g[2] + fold(jnp.broadcast_to(g[5], (GM, LANES)))[:, 0:1]
        o[3][...] = g[6]
        o[4][...] = g[7]

    return pl.pallas_call(
        body, name=name, out_shape=[_S((G, N), f32), _S((G, N), f32), _S((G, 1), f32), _S((GM, N), f32), _S((GM, N), f32)],
    )(*args, *cots, rsum)


S5_TC = 512


def _s5_local_scan(src, w_r, w_i, a_r, a_i, *, reverse, name, carry=None, powers=True):
    T, C = src.shape
    nblk, cb, sb = w_r.shape
    NS = nblk * sb
    tc = min(S5_TC, T)
    nT, nt = T // tc, tc // NSEG
    tmap = (lambda i: nT - 1 - i) if reverse else (lambda i: i)

    carry = carry or _NO_CARRY

    def body(*refs):
        own, c_in, c_out, c_sems = _carry_split(carry, refs, 5, 4)
        u_ref, wr_ref, wi_ref, ar_ref, ai_ref, sr_ref, si_ref, pr_ref, pi_ref, st_r, st_i, pw_r, pw_i = own
        _carry_start(carry, c_in, c_out, c_sems, (pl.program_id(0) == 0) & (pl.program_id(1) == 0))

        @pl.when(pl.program_id(1) == 0)
        def _():
            st_r[...] = jnp.zeros_like(st_r)
            st_i[...] = jnp.zeros_like(st_i)
            pw_r[...] = jnp.ones_like(pw_r)
            pw_i[...] = jnp.zeros_like(pw_i)

        u = u_ref[...].astype(bf16)
        sr_ref[...] = jnp.dot(u, wr_ref[...], preferred_element_type=f32)
        si_ref[...] = jnp.dot(u, wi_ref[...], preferred_element_type=f32)
        ar = jnp.broadcast_to(ar_ref[...], (NSEG, sb))
        ai = jnp.broadcast_to(ai_ref[...], (NSEG, sb))

        def step(k, c):
            cr, ci, qr, qi = c
            kk = (nt - 1 - k) if reverse else k
            rows = pl.ds(pl.multiple_of(kk * NSEG, NSEG), NSEG)
            nr = ar * cr - ai * ci + sr_ref[rows, :]
            ni = ar * ci + ai * cr + si_ref[rows, :]
            sr_ref[rows, :] = nr
            si_ref[rows, :] = ni
            return (nr, ni, ar * qr - ai * qi, ar * qi + ai * qr) if powers else (nr, ni, qr, qi)

        cr, ci, qr, qi = lax.fori_loop(0, nt, step, (st_r[...], st_i[...], pw_r[...], pw_i[...]), unroll=8)
        st_r[...], st_i[...], pw_r[...], pw_i[...] = cr, ci, qr, qi
        pr_ref[...] = qr
        pi_ref[...] = qi
        _carry_finish(carry, c_in, c_out, c_sems, (pl.program_id(0) == nblk - 1) & (pl.program_id(1) == nT - 1))

    blk = pl.BlockSpec((tc, sb), lambda j, i: (tmap(i), j))
    wsp = pl.BlockSpec((None, cb, sb), lambda j, i: (j, 0, 0))
    asp = pl.BlockSpec((1, sb), lambda j, i: (0, j))
    psp = pl.BlockSpec((NSEG, sb), lambda j, i: (0, j))
    res = pl.pallas_call(
        body, name=name, grid=(nblk, nT),
        in_specs=[pl.BlockSpec((tc, cb), lambda j, i: (tmap(i), j)), wsp, wsp, asp, asp] + [_ANY] * len(carry.ins),
        out_specs=[blk, blk, psp, psp] + [_ANY] * len(carry.out_shapes),
        out_shape=[_S((T, NS), f32)] * 2 + [_S((NSEG, NS), f32)] * 2 + list(carry.out_shapes),
        scratch_shapes=[pltpu.VMEM((NSEG, sb), f32)] * 4 + list(carry.sems),
        compiler_params=_cparams(("arbitrary", "arbitrary") if carry.ins else ("parallel", "arbitrary")),
    )(src, w_r, w_i, a_r, a_i, *carry.ins)
    return (*res[:4], list(res[4:])) if carry.ins else res


def _s5_carry(e_r, e_i, p_r, p_i, *, reverse, name):
    NS = e_r.shape[1]

    def body(er_ref, ei_ref, pr_ref, pi_ref, cr_ref, ci_ref):
        ar, ai = pr_ref[0:1, :], pi_ref[0:1, :]
        cr = jnp.zeros((1, NS), f32)
        ci = jnp.zeros((1, NS), f32)
        order = list(range(NSEG - 1, -1, -1)) if reverse else list(range(NSEG))
        cr_ref[order[0]:order[0] + 1, :] = cr
        ci_ref[order[0]:order[0] + 1, :] = ci
        for prev, q in zip(order[:-1], order[1:]):
            er, ei = er_ref[prev:prev + 1, :], ei_ref[prev:prev + 1, :]
            cr, ci = er + ar * cr - ai * ci, ei + ar * ci + ai * cr
            cr_ref[q:q + 1, :] = cr
            ci_ref[q:q + 1, :] = ci

    return pl.pallas_call(body, name=name, out_shape=[_S((NSEG, NS), f32)] * 2)(e_r, e_i, p_r, p_i)


def _s5_fix_out(sl_r, sl_i, a_r, a_i, c_r, c_i, wc_r, wc_i, *, name):
    T, NS = sl_r.shape
    nblk, sb, cb = wc_r.shape
    tc = min(S5_TC, T)
    nT, nt = T // tc, tc // NSEG

    def body(lr_ref, li_ref, ar_ref, ai_ref, cr_ref, ci_ref, wr_ref, wi_ref, sr_ref, si_ref, y_ref, pw_r, pw_i):
        @pl.when(pl.program_id(1) == 0)
        def _():
            pw_r[...] = jnp.ones_like(pw_r)
            pw_i[...] = jnp.zeros_like(pw_i)

        ar = jnp.broadcast_to(ar_ref[...], (NSEG, sb))
        ai = jnp.broadcast_to(ai_ref[...], (NSEG, sb))
        cr, ci = cr_ref[...], ci_ref[...]

        def step(k, c):
            qr, qi = c
            qr, qi = ar * qr - ai * qi, ar * qi + ai * qr
            rows = pl.ds(pl.multiple_of(k * NSEG, NSEG), NSEG)
            sr_ref[rows, :] = lr_ref[rows, :] + qr * cr - qi * ci
            si_ref[rows, :] = li_ref[rows, :] + qr * ci + qi * cr
            return qr, qi

        qr, qi = lax.fori_loop(0, nt, step, (pw_r[...], pw_i[...]), unroll=8)
        pw_r[...], pw_i[...] = qr, qi
        y_ref[...] = (jnp.dot(sr_ref[...].astype(bf16), wr_ref[...], preferred_element_type=f32)
                      - jnp.dot(si_ref[...].astype(bf16), wi_ref[...], preferred_element_type=f32))

    blk = pl.BlockSpec((tc, sb), lambda j, i: (i, j))
    asp = pl.BlockSpec((1, sb), lambda j, i: (0, j))
    csp = pl.BlockSpec((NSEG, sb), lambda j, i: (0, j))
    wsp = pl.BlockSpec((None, sb, cb), lambda j, i: (j, 0, 0))
    return pl.pallas_call(
        body, name=name, grid=(nblk, nT), in_specs=[blk, blk, asp, asp, csp, csp, wsp, wsp],
        out_specs=[blk, blk, pl.BlockSpec((tc, cb), lambda j, i: (i, j))],
        out_shape=[_S((T, NS), f32)] * 2 + [_S((T, nblk * cb), f32)],
        scratch_shapes=[pltpu.VMEM((NSEG, sb), f32)] * 2, compiler_params=_cparams(("parallel", "arbitrary")),
    )(sl_r, sl_i, a_r, a_i, c_r, c_i, wc_r, wc_i)


def _s5_fix_bwd(ql_r, ql_i, ab_r, ab_i, c_r, c_i, s_r, s_i, sb_r, sb_i, u, dy, du_add, w_r, w_i, *, name):
    T, NS = ql_r.shape
    nblk, cb, sb = w_r.shape
    tc = min(S5_TC, T)
    nT, nt = T // tc, tc // NSEG
    tmap = lambda i: nT - 1 - i

    def body(lr_ref, li_ref, ar_ref, ai_ref, cr_ref, ci_ref, sr_ref, si_ref, br_ref, bi_ref, u_ref, dy_ref, dua_ref, wr_ref, wi_ref,
             du_ref, dwr_ref, dwi_ref, dcr_ref, dci_ref, dar_ref, dai_ref, pw_r, pw_i, ac_r, ac_i, q_r, q_i):
        first = pl.program_id(1) == 0

        @pl.when(first)
        def _():
            pw_r[...] = jnp.ones_like(pw_r)
            pw_i[...] = jnp.zeros_like(pw_i)
            ac_r[...] = jnp.zeros_like(ac_r)
            ac_i[...] = jnp.zeros_like(ac_i)
            dwr_ref[...] = jnp.zeros_like(dwr_ref)
            dwi_ref[...] = jnp.zeros_like(dwi_ref)
            dcr_ref[...] = jnp.zeros_like(dcr_ref)
            dci_ref[...] = jnp.zeros_like(dci_ref)

        ar = jnp.broadcast_to(ar_ref[...], (NSEG, sb))
        ai = jnp.broadcast_to(ai_ref[...], (NSEG, sb))
        cr, ci = cr_ref[...], ci_ref[...]

        def fix(rows, qr, qi, spr, spi, accr, acci):
            qr, qi = ar * qr - ai * qi, ar * qi + ai * qr
            xr = lr_ref[rows, :] + qr * cr - qi * ci
            xi = li_ref[rows, :] + qr * ci + qi * cr
            q_r[rows, :] = xr
            q_i[rows, :] = xi
            return qr, qi, accr + xr * spr + xi * spi, acci + xi * spr - xr * spi

        def step(k, c):
            qr, qi, accr, acci = c
            kk = nt - 1 - k
            rows = pl.ds(pl.multiple_of(kk * NSEG, NSEG), NSEG)
            prev = pl.ds(pl.multiple_of((kk - 1) * NSEG, NSEG), NSEG)
            return fix(rows, qr, qi, sr_ref[prev, :], si_ref[prev, :], accr, acci)

        c = lax.fori_loop(0, nt - 1, step, (pw_r[...], pw_i[...], ac_r[...], ac_i[...]), unroll=7)
        qr, qi, accr, acci = fix(pl.ds(0, NSEG), *c[:2], br_ref[...], bi_ref[...], *c[2:])
        pw_r[...], pw_i[...], ac_r[...], ac_i[...] = qr, qi, accr, acci

        qrb, qib = q_r[...].astype(bf16), q_i[...].astype(bf16)
        nt_dims = (((1,), (1,)), ((), ()))
        tn_dims = (((0,), (0,)), ((), ()))
        du_ref[...] = (dua_ref[...] + lax.dot_general(qrb, wr_ref[...], nt_dims, preferred_element_type=f32)
                       + lax.dot_general(qib, wi_ref[...], nt_dims, preferred_element_type=f32))
        ub = u_ref[...].astype(bf16)
        dwr_ref[...] += lax.dot_general(ub, qrb, tn_dims, preferred_element_type=f32)
        dwi_ref[...] += lax.dot_general(ub, qib, tn_dims, preferred_element_type=f32)
        dyb = dy_ref[...].astype(bf16)
        dcr_ref[...] += lax.dot_general(sr_ref[...].astype(bf16), dyb, tn_dims, preferred_element_type=f32)
        dci_ref[...] -= lax.dot_general(si_ref[...].astype(bf16), dyb, tn_dims, preferred_element_type=f32)

        @pl.when(pl.program_id(1) == nT - 1)
        def _():
            dar_ref[...] = jnp.sum(accr, axis=0, keepdims=True)
            dai_ref[...] = jnp.sum(acci, axis=0, keepdims=True)

    blk = pl.BlockSpec((tc, sb), lambda j, i: (tmap(i), j))
    asp = pl.BlockSpec((1, sb), lambda j, i: (0, j))
    csp = pl.BlockSpec((NSEG, sb), lambda j, i: (0, j))
    bsp = pl.BlockSpec((None, NSEG, sb), lambda j, i: (tmap(i), 0, j))
    chn = pl.BlockSpec((tc, cb), lambda j, i: (tmap(i), j))
    wsp = pl.BlockSpec((None, cb, sb), lambda j, i: (j, 0, 0))
    wcs = pl.BlockSpec((None, sb, cb), lambda j, i: (j, 0, 0))
    return pl.pallas_call(
        body, name=name, grid=(nblk, nT), in_specs=[blk, blk, asp, asp, csp, csp, blk, blk, bsp, bsp, chn, chn, chn, wsp, wsp],
        out_specs=[chn, wsp, wsp, wcs, wcs, asp, asp],
        out_shape=[_S((T, nblk * cb), f32), _S((nblk, cb, sb), f32), _S((nblk, cb, sb), f32), _S((nblk, sb, cb), f32),
                   _S((nblk, sb, cb), f32), _S((1, NS), f32), _S((1, NS), f32)],
        scratch_shapes=[pltpu.VMEM((NSEG, sb), f32)] * 4 + [pltpu.VMEM((tc, sb), f32)] * 2,
        compiler_params=_cparams(("parallel", "arbitrary")),
    )(ql_r, ql_i, ab_r, ab_i, c_r, c_i, s_r, s_i, sb_r, sb_i, u, dy, du_add, w_r, w_i)


SSD2_TB = 512
_NN = (((1,), (0,)), ((), ()))
_NT = (((1,), (1,)), ((), ()))
_TN = (((0,), (0,)), ((), ()))


def _dotf(a, b, dims):
    return lax.dot_general(a.astype(bf16), b.astype(bf16), dims, preferred_element_type=f32)


def _doth(a, b, dims=_NN, sel="b", parts=3):
    x, m = (a, b) if sel == "b" else (b, a)
    m = m.astype(bf16)
    out = None
    for _ in range(parts):
        piece = x.astype(bf16)
        x = x - piece.astype(f32)
        d = lax.dot_general(*((piece, m) if sel == "b" else (m, piece)), dims, preferred_element_type=f32)
        out = d if out is None else out + d
    return out


def _ssd_consts(hpg):
    W = hpg * CHUNK
    i = lax.broadcasted_iota(jnp.int32, (CHUNK, CHUNK), 0)
    j = lax.broadcasted_iota(jnp.int32, (CHUNK, CHUNK), 1)
    tril = (i >= j).astype(f32)
    r = lax.broadcasted_iota(jnp.int32, (W, W), 0)
    c = lax.broadcasted_iota(jnp.int32, (W, W), 1)
    bd = (r // CHUNK == c // CHUNK).astype(f32)
    triu_bd = bd * (r <= c).astype(f32)
    e_r = lax.broadcasted_iota(jnp.int32, (W, LANES), 0)
    e_c = lax.broadcasted_iota(jnp.int32, (W, LANES), 1)
    ered = (e_r // HEADDIM == e_c).astype(f32)
    return tril, jnp.tile(tril, (1, hpg)), bd, triu_bd, ered


def _ssd2_specs(G, hpg, tb, tmap, b_off, c_off):
    W = hpg * HEADDIM
    ncb = tb // CHUNK
    xsp = pl.BlockSpec((tb, W), lambda g, i: (tmap(i), g))
    bsp = pl.BlockSpec((tb, SSD_STATE), lambda g, i: (tmap(i), b_off + g))
    csp = pl.BlockSpec((tb, SSD_STATE), lambda g, i: (tmap(i), c_off + g))
    rsp = pl.BlockSpec((None, ncb, W), lambda g, i: (g, tmap(i), 0))
    dsp = pl.BlockSpec((1, W), lambda g, i: (0, g))
    hsp = pl.BlockSpec((None, ncb, SSD_STATE, W), lambda g, i: (g, tmap(i), 0, 0))
    const = lambda a: pl.BlockSpec(a.shape, lambda g, i: (0, 0))
    return xsp, bsp, csp, rsp, dsp, hsp, const


def _tile_rows(a, n):
    return jnp.concatenate([a] * n, axis=0)


def _ssd2_fwd(xc, dt4, a4, dtw, aw, d4, consts, *, d_inner, name, carry=None):
    carry = carry or _NO_CARRY
    T = xc.shape[0]
    G, nc, W = dtw.shape
    hpg = W // CHUNK
    tb = min(SSD2_TB, T)
    nb, ncb = T // tb, tb // CHUNK
    b_off = d_inner // SSD_STATE
    xsp, bsp, csp, rsp, dsp, hsp, const = _ssd2_specs(G, hpg, tb, lambda i: i, b_off, b_off + G)
    tril, mask4, bd, triu_bd, _ = consts

    def body(*refs):
        own, c_in, c_out, c_sems = _carry_split(carry, refs, 12, 2)
        x_ref, b_ref, c_ref, dt_ref, a_ref, dtw_ref, aw_ref, d_ref, tril_ref, mask_ref, bd_ref, tbd_ref, y_ref, hs_ref, h_scr = own
        _carry_start(carry, c_in, c_out, c_sems, (pl.program_id(0) == 0) & (pl.program_id(1) == 0))

        @pl.when(pl.program_id(1) == 0)
        def _():
            h_scr[...] = jnp.zeros_like(h_scr)

        acs_rows = _doth(aw_ref[...], tbd_ref[...])
        ht = h_scr[...]
        for c in range(ncb):
            rows = slice(c * CHUNK, (c + 1) * CHUNK)
            x, bm, cm = x_ref[rows, :], b_ref[rows, :], c_ref[rows, :]
            acs = _doth(tril_ref[...], a_ref[rows, :], sel="a")
            lmat = jnp.where(mask_ref[...] > 0, jnp.exp(jnp.minimum(acs - acs_rows[c:c + 1, :], 0.0)), 0.0)
            m4 = _dotf(cm, _tile_rows(bm, hpg), _NT) * lmat * dtw_ref[c:c + 1, :]
            xbd = _tile_rows(x, hpg) * bd_ref[...]
            hs_ref[c] = ht
            y_ref[rows, :] = _dotf(m4, xbd, _NN) + _dotf(cm, ht, _NN) * jnp.exp(acs) + d_ref[...] * x
            a_last = acs[CHUNK - 1:CHUNK, :]
            xw = x * (jnp.exp(a_last - acs) * dt_ref[rows, :])
            ht = ht * jnp.exp(a_last) + _dotf(bm, xw, _TN)
        h_scr[...] = ht
        _carry_finish(carry, c_in, c_out, c_sems, (pl.program_id(0) == G - 1) & (pl.program_id(1) == nb - 1))

    res = pl.pallas_call(
        body, name=name, grid=(G, nb),
        in_specs=[xsp, bsp, csp, xsp, xsp, rsp, rsp, dsp, const(tril), const(mask4), const(bd), const(triu_bd)] + [_ANY] * len(carry.ins),
        out_specs=[xsp, hsp] + [_ANY] * len(carry.out_shapes),
        out_shape=[_S((T, G * W), f32), _S((G, nc, SSD_STATE, W), f32)] + list(carry.out_shapes),
        scratch_shapes=[pltpu.VMEM((SSD_STATE, W), f32)] + list(carry.sems),
        compiler_params=_cparams(("arbitrary", "arbitrary") if carry.ins else ("parallel", "arbitrary")),
    )(xc, xc, xc, dt4, a4, dtw, aw, d4, tril, mask4, bd, triu_bd, *carry.ins)
    return res[0], res[1], list(res[2:])


def _ssd2_bwd(xc, dt4, a4, dtw, aw, d4, consts, hs, dy, *, d_inner, name, carry=None):
    T = xc.shape[0]
    G, nc, W = dtw.shape
    hpg = W // CHUNK
    tb = min(SSD2_TB, T)
    nb, ncb = T // tb, tb // CHUNK
    b_off = d_inner // SSD_STATE
    tmap = lambda i: nb - 1 - i
    xsp, bsp, csp, rsp, dsp, hsp, const = _ssd2_specs(G, hpg, tb, tmap, b_off, b_off + G)
    gsp = pl.BlockSpec((tb, SSD_STATE), lambda g, i: (tmap(i), g))
    ddsp = pl.BlockSpec((None, 1, LANES), lambda g, i: (g, 0, 0))
    tril, mask4, bd, triu_bd, ered = consts

    carry = carry or _NO_CARRY

    def body(*refs):
        own, c_in, c_out, c_sems = _carry_split(carry, refs, 15, 8)
        (x_ref, b_ref, c_ref, dt_ref, a_ref, dtw_ref, aw_ref, d_ref, tril_ref, mask_ref, bd_ref, tbd_ref, er_ref, hs_ref, dy_ref,
         dx_ref, db_ref, dc_ref, ddtc_ref, dac_ref, ddtw_ref, daw_ref, dd_ref, g_scr, dd_scr, rw_scr, tl_scr) = own
        _carry_start(carry, c_in, c_out, c_sems, (pl.program_id(0) == 0) & (pl.program_id(1) == 0))
        first = pl.program_id(1) == 0

        @pl.when(first)
        def _():
            g_scr[...] = jnp.zeros_like(g_scr)
            dd_scr[...] = jnp.zeros_like(dd_scr)

        mask = mask_ref[...] > 0
        lane_in_block = lax.broadcasted_iota(jnp.int32, mask.shape, 1) & (CHUNK - 1)
        maskt = lax.broadcasted_iota(jnp.int32, mask.shape, 0) <= lane_in_block
        acs_rows = _doth(aw_ref[...], tbd_ref[...])
        dht = g_scr[...]
        dd = dd_scr[...]
        for c in range(ncb - 1, -1, -1):
            rows = slice(c * CHUNK, (c + 1) * CHUNK)
            x, bm, cm, dyc = x_ref[rows, :], b_ref[rows, :], c_ref[rows, :], dy_ref[rows, :]
            dtc, dtr = dt_ref[rows, :], dtw_ref[c:c + 1, :]
            ht = hs_ref[c]
            acs = _doth(tril_ref[...], a_ref[rows, :], sel="a")
            seg = acs - acs_rows[c:c + 1, :]
            lmat = jnp.where(mask, jnp.exp(jnp.minimum(seg, 0.0)), 0.0)
            lmat_t = jnp.where(maskt, jnp.exp(jnp.minimum(-seg, 0.0)), 0.0)
            btile, ctile = _tile_rows(bm, hpg), _tile_rows(cm, hpg)
            g4 = _dotf(cm, btile, _NT)
            gt4 = _dotf(bm, ctile, _NT)
            m4 = g4 * lmat * dtr
            mt4 = gt4 * lmat_t * dtc
            xbd = _tile_rows(x, hpg) * bd_ref[...]
            dybd = _tile_rows(dyc, hpg) * bd_ref[...]
            dm4 = _dotf(dyc, xbd, _NT)
            dmt4 = _dotf(x, dybd, _NT)
            dx = d_ref[...] * dyc + _dotf(mt4, dybd, _NN)
            dd = dd + jnp.sum(dyc * x, axis=0, keepdims=True)
            e4 = dm4 * m4
            dc = _dotf(dm4 * lmat * dtr, btile, _NN)
            db = _dotf(dmt4 * lmat_t * dtc, ctile, _NN)
            decay = jnp.exp(acs)
            yoff = _dotf(cm, ht, _NN) * decay
            dz = dyc * decay
            dc = dc + _dotf(dz, ht, _NT)
            dht_prev = _dotf(cm, dz, _TN)
            a_last = acs[CHUNK - 1:CHUNK, :]
            ea_last = jnp.exp(a_last)
            erel = jnp.exp(a_last - acs)
            dte = erel * dtc
            dxw = _dotf(bm, dht, _NN)
            db = db + _dotf(x * dte, dht, _NT)
            dx = dx + dxw * dte
            q4 = dxw * x
            dacs = e4 + dyc * yoff - q4 * dte
            col = jnp.concatenate([q4 * erel, _doth(tril_ref[...], dacs, _TN, sel="a")], axis=0)
            col = _doth(col, er_ref[...], parts=2)
            ddtc_ref[rows, :] = col[:CHUNK]
            dac_ref[rows, :] = col[CHUNK:]
            ddtw_ref[c:c + 1, :] = jnp.sum(dm4 * g4 * lmat, axis=0, keepdims=True)
            rw_scr[c:c + 1, :] = -jnp.sum(e4, axis=0, keepdims=True)
            tl_scr[c:c + 1, :] = jnp.sum(q4 * dte, axis=0, keepdims=True) + ea_last * jnp.sum(dht * ht, axis=0, keepdims=True)
            dx_ref[rows, :] = dx
            db_ref[rows, :] = db
            dc_ref[rows, :] = dc
            dht = dht_prev + dht * ea_last
        daw_ref[...] = _doth(rw_scr[...], tbd_ref[...], _NT) + _doth(tl_scr[...], bd_ref[...])
        g_scr[...] = dht
        dd_scr[...] = dd

        @pl.when(pl.program_id(1) == nb - 1)
        def _():
            dd_ref[...] = _doth(dd, er_ref[...])

        _carry_finish(carry, c_in, c_out, c_sems, (pl.program_id(0) == G - 1) & (pl.program_id(1) == nb - 1))

    res = pl.pallas_call(
        body, name=name, grid=(G, nb),
        in_specs=[xsp, bsp, csp, xsp, xsp, rsp, rsp, dsp, const(tril), const(mask4), const(bd), const(triu_bd), const(ered), hsp, xsp]
        + [_ANY] * len(carry.ins),
        out_specs=[xsp, gsp, gsp, gsp, gsp, rsp, rsp, ddsp] + [_ANY] * len(carry.out_shapes),
        out_shape=[_S((T, G * W), f32), _S((T, G * SSD_STATE), f32), _S((T, G * SSD_STATE), f32), _S((T, G * LANES), f32),
                   _S((T, G * LANES), f32), _S(dtw.shape, f32), _S(dtw.shape, f32), _S((G, 1, LANES), f32)] + list(carry.out_shapes),
        scratch_shapes=[pltpu.VMEM((SSD_STATE, W), f32), pltpu.VMEM((1, W), f32), pltpu.VMEM((ncb, W), f32), pltpu.VMEM((ncb, W), f32)]
        + list(carry.sems),
        compiler_params=_cparams(("arbitrary", "arbitrary") if carry.ins else ("parallel", "arbitrary")),
    )(xc, xc, xc, dt4, a4, dtw, aw, d4, tril, mask4, bd, triu_bd, ered, hs, dy, *carry.ins)
    return (*res[:8], list(res[8:]))


def _peers():
    x, y, c = lax.axis_index("x"), lax.axis_index("y"), lax.axis_index("c")
    return x, y, c


_ANY = pl.BlockSpec(memory_space=pl.ANY)
N_CHIP = N_DEV // 2


def _all_gather(shards, *, name):
    n = len(shards)
    carry = _carry_gather(shards)

    def body(*refs):
        x_refs, out_refs, sems = refs[:n], refs[n:2 * n], refs[2 * n:]
        _gather_start(x_refs, out_refs, sems)
        _gather_finish(x_refs, out_refs, sems)

    return pl.pallas_call(
        body, name=name, out_shape=list(carry.out_shapes), in_specs=[_ANY] * n, out_specs=[_ANY] * n, scratch_shapes=list(carry.sems),
    )(*shards)


def _gather_parts(x_refs, out_refs, sems):
    send_sems, recv_sems, local_sems = sems
    x, y, c = _peers()
    me, sibling = (x, y, c), (x, y, 1 - c)
    chips = [(1 - x, y), (x, 1 - y), (1 - x, 1 - y)]
    n = len(x_refs)

    def copy(a, r, block, to, src=None):
        px, py, pc = block
        slot = out_refs[a].at[4 * px + 2 * py + pc]
        return pltpu.make_async_remote_copy(
            src_ref=slot if src is None else src, dst_ref=slot, send_sem=send_sems.at[7 * a + r],
            recv_sem=recv_sems.at[7 * a + r], device_id=to, device_id_type=MESH)

    mine = [pltpu.make_async_copy(x_refs[a], out_refs[a].at[4 * x + 2 * y + c], local_sems.at[a]) for a in range(n)]
    first = []
    for a in range(n):
        first.append(copy(a, 0, me, sibling, src=x_refs[a]))
        first += [copy(a, 1 + j, me, (*chip, c), src=x_refs[a]) for j, chip in enumerate(chips)]
    return copy, mine, first, me, sibling, chips, c, n


def _gather_start(x_refs, out_refs, sems):
    _, mine, first, *_ = _gather_parts(x_refs, out_refs, sems)
    for cp in mine + first:
        cp.start()


def _gather_finish(x_refs, out_refs, sems):
    copy, mine, first, me, sibling, chips, c, n = _gather_parts(x_refs, out_refs, sems)
    passed = []
    for j, chip in enumerate(chips):
        for a in range(n):
            copy(a, 1 + j, (*chip, c), me).wait_recv()
            fwd = copy(a, 4 + j, (*chip, c), sibling)
            fwd.start()
            passed.append(fwd)
    for a in range(n):
        copy(a, 0, sibling, me).wait_recv()
    for j, chip in enumerate(chips):
        for a in range(n):
            copy(a, 4 + j, (*chip, 1 - c), me).wait_recv()
    for cp in first + passed:
        cp.wait_send()
    for cp in mine:
        cp.wait()


def _exchange_sibling(slots, *, name):
    n = len(slots)
    carry = _carry_sibling(slots)

    def body(*refs):
        carry.start(refs[:n], refs[n:2 * n], refs[2 * n:])
        carry.finish(refs[:n], refs[n:2 * n], refs[2 * n:])

    return list(pl.pallas_call(
        body, name=name, out_shape=list(carry.out_shapes), in_specs=[_ANY] * n, out_specs=[_ANY] * n, scratch_shapes=list(carry.sems),
    )(*slots))


def _sibling_copies(x_refs, sib_refs, send_sems, recv_sems):
    x, y, c = _peers()
    return [pltpu.make_async_remote_copy(
        src_ref=x_refs[a].at[pl.ds(N_CHIP * (1 - c), N_CHIP)], dst_ref=sib_refs[a], send_sem=send_sems.at[a],
        recv_sem=recv_sems.at[a], device_id=(x, y, 1 - c), device_id_type=MESH) for a in range(len(x_refs))]


def _carry_sibling(slots):
    n = len(slots)
    return _Carry(tuple(slots), tuple(_S((N_CHIP,) + s.shape[1:], s.dtype) for s in slots),
                  (pltpu.SemaphoreType.DMA((n,)), pltpu.SemaphoreType.DMA((n,))),
                  lambda i, o, s: _start_all(_sibling_copies(i, o, *s)), lambda i, o, s: _wait_all(_sibling_copies(i, o, *s)))


def _chip_sum(slots, sib, core, *, name):
    _, R, W = slots.shape
    tr = _pick(R, max(16, EW_BLOCK_ELEMS // W), align=16)

    def body(core_ref, x_ref, s_ref, o_ref):
        o_ref[...] = (x_ref[...].astype(f32) + s_ref[...].astype(f32)).astype(o_ref.dtype)

    blk = pl.BlockSpec((None, tr, W), lambda t, i, core_ref: (t, i, 0))
    return pl.pallas_call(
        body, name=name, out_shape=_S(sib.shape, slots.dtype),
        grid_spec=pltpu.PrefetchScalarGridSpec(
            num_scalar_prefetch=1, grid=(N_CHIP, R // tr),
            in_specs=[pl.BlockSpec((None, tr, W), lambda t, i, core_ref: (N_CHIP * core_ref[0] + t, i, 0)), blk], out_specs=blk),
        compiler_params=_cparams(("parallel", "parallel")),
    )(core, slots, sib)


def _chip_out_shapes(parts):
    return [_S((N_CHIP - 1,) + p.shape[1:], p.dtype) for p in parts]


def _chip_sems(n):
    return [pltpu.SemaphoreType.DMA((3 * n,)), pltpu.SemaphoreType.DMA((3 * n,))]


def _chip_copies(p_refs, out_refs, send_sems, recv_sems):
    x, y, c = _peers()
    copies = []
    for j in range(1, N_CHIP):
        tx, ty = x ^ (j >> 1), y ^ (j & 1)
        for a in range(len(p_refs)):
            copies.append(pltpu.make_async_remote_copy(
                src_ref=p_refs[a].at[2 * tx + ty], dst_ref=out_refs[a].at[j - 1], send_sem=send_sems.at[3 * a + j - 1],
                recv_sem=recv_sems.at[3 * a + j - 1], device_id=(tx, ty, c), device_id_type=MESH))
    return copies


def _start_all(copies):
    for cp in copies:
        cp.start()


def _wait_all(copies):
    for cp in copies:
        cp.wait_recv()
    for cp in copies:
        cp.wait_send()


class _Carry(NamedTuple):
    ins: tuple = ()
    out_shapes: tuple = ()
    sems: tuple = ()
    start: Callable = None
    finish: Callable = None


_NO_CARRY = _Carry()


def _carry_chips(parts):
    return _Carry(tuple(parts), tuple(_chip_out_shapes(parts)), tuple(_chip_sems(len(parts))),
                  lambda i, o, s: _start_all(_chip_copies(i, o, *s)), lambda i, o, s: _wait_all(_chip_copies(i, o, *s)))


def _carry_gather(shards):
    n = len(shards)
    sems = (pltpu.SemaphoreType.DMA((7 * n,)), pltpu.SemaphoreType.DMA((7 * n,)), pltpu.SemaphoreType.DMA((n,)))
    return _Carry(tuple(shards), tuple(_S((N_DEV,) + s.shape, s.dtype) for s in shards), sems, _gather_start, _gather_finish)


def _carry_split(carry, refs, n_in, n_out):
    ci, co, cs = len(carry.ins), len(carry.out_shapes), len(carry.sems)
    refs = list(refs)
    own_in, c_in = refs[:n_in], refs[n_in:n_in + ci]
    own_out, c_out = refs[n_in + ci:n_in + ci + n_out], refs[n_in + ci + n_out:n_in + ci + n_out + co]
    rest = refs[n_in + ci + n_out + co:]
    own_scratch, c_sems = rest[:len(rest) - cs], rest[len(rest) - cs:]
    return own_in + own_out + own_scratch, c_in, c_out, c_sems


def _carry_start(carry, c_in, c_out, c_sems, first):
    if carry.ins:
        @pl.when(first)
        def _():
            carry.start(c_in, c_out, c_sems)


def _carry_finish(carry, c_in, c_out, c_sems, last):
    if carry.ins:
        @pl.when(last)
        def _():
            carry.finish(c_in, c_out, c_sems)


def _sum_slots(stack, *, name):
    n, R, W = stack.shape
    tr = _pick(R, 1024, align=8)

    def body(s_ref, o_ref):
        acc = s_ref[0]
        for k in range(1, n):
            acc = acc + s_ref[k]
        o_ref[...] = acc

    return pl.pallas_call(
        body, name=name, grid=(R // tr,), in_specs=[pl.BlockSpec((n, tr, W), lambda i: (0, i, 0))],
        out_specs=pl.BlockSpec((tr, W), lambda i: (i, 0)), out_shape=_S((R, W), f32), compiler_params=_cparams(("parallel",)),
    )(stack)


def _adamw_math(gv, wv, mv, vv):
    c1 = 1.0 / (1.0 - ADAM_B1 ** ADAM_STEP)
    c2 = 1.0 / (1.0 - ADAM_B2 ** ADAM_STEP)
    nm = ADAM_B1 * mv + (1.0 - ADAM_B1) * gv
    nv = ADAM_B2 * vv + (1.0 - ADAM_B2) * jnp.square(gv)
    return -ADAM_LR * ((nm * c1) / (jnp.sqrt(nv * c2) + ADAM_EPS) + ADAM_WD * wv), nm, nv


def _adamw(g, w, m, v, *, name):
    R, W = w.shape
    tr = _pick(R, max(8, (1 << 20) // (4 * W)), align=8)

    def body(g_ref, w_ref, m_ref, v_ref, d_ref, nm_ref, nv_ref):
        d_ref[...], nm_ref[...], nv_ref[...] = _adamw_math(g_ref[...], w_ref[...], m_ref[...], v_ref[...])

    sp = pl.BlockSpec((tr, W), lambda i: (i, 0))
    return pl.pallas_call(
        body, name=name, grid=(R // tr,), in_specs=[sp] * 4, out_specs=[sp] * 3, out_shape=[_S((R, W), f32)] * 3,
        compiler_params=_cparams(("parallel",)),
    )(g, w, m, v)


def _reduce_adamw(own, arrived, chip, w, m, v, *, name):
    n, R, W = arrived.shape
    tr = _pick(R, max(16, EW_BLOCK_ELEMS // (4 * W)), align=16)

    def body(chip_ref, o_ref, p_ref, w_ref, m_ref, v_ref, g_ref, d_ref, nm_ref, nv_ref):
        gv = o_ref[...].astype(f32)
        for k in range(n):
            gv = gv + p_ref[k].astype(f32)
        g_ref[...] = gv
        d_ref[...], nm_ref[...], nv_ref[...] = _adamw_math(gv, w_ref[...], m_ref[...], v_ref[...])

    sp = pl.BlockSpec((tr, W), lambda i, chip_ref: (i, 0))
    return pl.pallas_call(
        body, name=name, out_shape=[_S((R, W), f32)] * 4,
        grid_spec=pltpu.PrefetchScalarGridSpec(
            num_scalar_prefetch=1, grid=(R // tr,),
            in_specs=[pl.BlockSpec((None, tr, W), lambda i, chip_ref: (chip_ref[0], i, 0)),
                      pl.BlockSpec((n, tr, W), lambda i, chip_ref: (0, i, 0))] + [sp] * 3, out_specs=[sp] * 4),
        compiler_params=_cparams(("parallel",)),
    )(chip, own, arrived, w, m, v)


def _pieces(seg_start, seg_len, shard_w):
    out, col = [], seg_start
    while col < seg_start + seg_len:
        k, a = divmod(col, shard_w)
        n = min(shard_w - a, seg_start + seg_len - col)
        out.append((k, a, col - seg_start, n))
        col += n
    return out


def _unshard_w_in(g, seg_lens, *, name):
    _, D, w = g.shape
    starts = [sum(seg_lens[:i]) for i in range(len(seg_lens))]
    widths = [max(n, LANES) for n in seg_lens]
    tm = _pick(D, 256, align=16)

    def body(g_ref, *o_refs):
        for o_ref, s0, n in zip(o_refs, starts, seg_lens):
            if n < o_ref.shape[1]:
                o_ref[...] = jnp.zeros_like(o_ref)
            for k, a, off, m in _pieces(s0, n, w):
                o_ref[:, off:off + m] = g_ref[k, :, a:a + m]

    return pl.pallas_call(
        body, name=name, grid=(D // tm,), in_specs=[pl.BlockSpec((N_DEV, tm, w), lambda i: (0, i, 0))],
        out_specs=[pl.BlockSpec((tm, wd), lambda i: (i, 0)) for wd in widths], out_shape=[_S((D, wd), g.dtype) for wd in widths],
        compiler_params=_cparams(("parallel",)),
    )(g)


def _unshard_pair(g1, g2, *, name):
    _, D, w = g1.shape
    tm = _pick(D, 256, align=16)

    def body(a_ref, b_ref, o_ref):
        for i, g_ref in enumerate((a_ref, b_ref)):
            for k in range(N_DEV):
                off = (i * N_DEV + k) * w
                o_ref[:, off:off + w] = g_ref[k]

    blk = pl.BlockSpec((N_DEV, tm, w), lambda i: (0, i, 0))
    return pl.pallas_call(
        body, name=name, grid=(D // tm,), in_specs=[blk, blk], out_specs=pl.BlockSpec((tm, 2 * N_DEV * w), lambda i: (i, 0)),
        out_shape=_S((D, 2 * N_DEV * w), g1.dtype), compiler_params=_cparams(("parallel",)),
    )(g1, g2)


def _reshard_pair(dw, *, name):
    D, w = dw.shape[0], dw.shape[1] // (2 * N_DEV)
    tm = _pick(D, 128, align=16)

    def body(g_ref, a_ref, b_ref):
        for i, o_ref in enumerate((a_ref, b_ref)):
            for k in range(N_DEV):
                off = (i * N_DEV + k) * w
                o_ref[_slot_of(k)] = g_ref[:, off:off + w].astype(o_ref.dtype)

    blk = pl.BlockSpec((N_DEV, tm, w), lambda i: (0, i, 0))
    return pl.pallas_call(
        body, name=name, grid=(D // tm,), in_specs=[pl.BlockSpec((tm, dw.shape[1]), lambda i: (i, 0))], out_specs=[blk, blk],
        out_shape=[_S((N_DEV, D, w), bf16)] * 2, compiler_params=_cparams(("parallel",)),
    )(dw)


def _reshard_w_in(grads, seg_lens, w, *, name):
    D = grads[0].shape[0]
    starts = [sum(seg_lens[:i]) for i in range(len(seg_lens))]
    tm = _pick(D, 128, align=16)

    def body(*refs):
        o_ref = refs[-1]
        for g_ref, s0, n in zip(refs[:-1], starts, seg_lens):
            for k, a, off, m in _pieces(s0, n, w):
                o_ref[_slot_of(k), :, a:a + m] = g_ref[:, off:off + m].astype(o_ref.dtype)

    return pl.pallas_call(
        body, name=name, grid=(D // tm,), in_specs=[pl.BlockSpec((tm, g.shape[1]), lambda i: (i, 0)) for g in grads],
        out_specs=pl.BlockSpec((N_DEV, tm, w), lambda i: (0, i, 0)), out_shape=_S((N_DEV, D, w), bf16),
        compiler_params=_cparams(("parallel",)),
    )(*grads)


def _pad_flat(a, mult):
    a = a.reshape(-1)
    n = -(-a.shape[0] // mult) * mult
    return a if n == a.shape[0] else jnp.pad(a, (0, n - a.shape[0]))


def _pad_cols(a, mult):
    n = -(-a.shape[1] // mult) * mult
    return a if n == a.shape[1] else jnp.pad(a, ((0, 0), (0, n - a.shape[1])))


def _block_diag(t):
    nblk, g, P, Q = t.shape
    eye = jnp.eye(g, dtype=t.dtype)
    return (t[:, :, :, None, :] * eye[None, :, None, :, None]).reshape(nblk, g * P, g * Q)


def _block_diag_t(w, P, Q):
    nblk = w.shape[0]
    g = w.shape[1] // P
    eye = jnp.eye(g, dtype=w.dtype)
    return (w.reshape(nblk, g, P, g, Q) * eye[None, :, None, :, None]).sum(axis=3)


_COLS = ("ffn1_w_gate", "ffn1_w_up", "ffn2_w_gate", "ffn2_w_up")
_ROWS = ("ffn1_w_down", "ffn2_w_down", "s5_w_glu", "w_proj_s5", "w_out", "w_proj_ssd")
_BIG = _COLS + _ROWS + ("w_in", "conv_w")
_SMALL = ("ffn1_norm", "mix_norm", "conv_b", "s5_A_re", "s5_A_im", "s5_log_dt", "s5_B_re", "s5_B_im", "s5_C_re", "s5_C_im",
          "s5_D", "s5_b_glu", "ssd_A_log", "ssd_dt_bias", "ssd_D", "ssd_norm", "b_gate", "ffn2_norm", "final_norm")
_WEIGHTS = ("ffn1_norm", "ffn1_w_gate", "ffn1_w_up", "ffn1_w_down", "mix_norm", "w_in", "conv_w", "conv_b", "s5_A_re", "s5_A_im",
            "s5_log_dt", "s5_B_re", "s5_B_im", "s5_C_re", "s5_C_im", "s5_D", "s5_w_glu", "s5_b_glu", "ssd_A_log", "ssd_dt_bias",
            "ssd_D", "ssd_norm", "w_proj_s5", "w_proj_ssd", "b_gate", "w_out", "ffn2_norm", "ffn2_w_gate", "ffn2_w_up",
            "ffn2_w_down", "final_norm")
def _with_carry(res, carry):
    return res if carry else (res, [])


def _ffn_fwd(x, n, wgu, wd_of, tag, carries=(None, None, None), h=None):
    D = x.shape[1]
    if h is None:
        h = _rows(_f_rmsnorm, [x], [n], [(D, bf16)], name=tag + "_norm")[0]
    ab, got0 = _with_carry(_mm(h, wgu, carry=carries[0], name=tag + "_gate_up"), carries[0])
    wd = wd_of(got0)
    F = wd.shape[0]
    (c,), got1 = _with_carry(_rows(_f_swiglu, [ab], [], [(F, bf16)], carry=carries[1], name=tag + "_act"), carries[1])
    y, got2 = _with_carry(_mm(c, wd, scale=0.5, add=x, carry=carries[2], name=tag + "_down"), carries[2])
    return y, (x, n, h, ab, c), (got0, got1, got2)


def _ffn_bwd(saved, wgu, wd, dy, tag, carry_after_dwd=None, carry_after_dwgu=None):
    x, n, h, ab, c = saved
    F = wd.shape[0]
    def act_bwd(dc, ab_):
        return jax.vjp(lambda t: _f_swiglu(t)[0], ab_)[1](dc)[0]

    dab = _mm(dy, wd, tb=True, scale=0.5, tm=EPI_TM, epi=(act_bwd, ab, 2 * F), out_dtype=bf16, name=tag + "_d_act")
    dwd = _mm(c, dy, ta=True, o_blk="m", o_slots=True, tm=F // 2, out_dtype=bf16, scale=0.5, name=tag + "_d_wdown")
    carry_w = carry_after_dwd(dwd) if carry_after_dwd else None
    dwgu, arr_w = _with_carry(_mm(h, dab, ta=True, carry=carry_w, name=tag + "_d_wgu"), carry_w)
    dwg, dwu = _reshard_pair(dwgu, name=tag + "_reshard_d_wgu")
    carry_h = carry_after_dwgu(dwg, dwu) if carry_after_dwgu else None
    dh, arr_h = _with_carry(_mm(dab, wgu, tb=True, carry=carry_h, name=tag + "_d_h"), carry_h)
    (dx,), (dn,) = _rows_bwd(_f_rmsnorm, [x], [n], [dh], name=tag + "_norm_bwd", want_rows=[0], adds={0: dy})
    return dx, dn, dwg, dwu, dwd, arr_w, arr_h


def kernel(x, ffn1_norm, ffn1_w_gate, ffn1_w_up, ffn1_w_down, mix_norm, w_in, conv_w, conv_b, s5_A_re, s5_A_im, s5_log_dt, s5_B_re, s5_B_im, s5_C_re, s5_C_im, s5_D, s5_w_glu, s5_b_glu, ssd_A_log, ssd_dt_bias, ssd_D, ssd_norm, w_proj_s5, w_proj_ssd, b_gate, w_out, ffn2_norm, ffn2_w_gate, ffn2_w_up, ffn2_w_down, final_norm, loss_target, m_ffn1_norm, m_ffn1_w_gate, m_ffn1_w_up, m_ffn1_w_down, m_mix_norm, m_w_in, m_conv_w, m_conv_b, m_s5_A_re, m_s5_A_im, m_s5_log_dt, m_s5_B_re, m_s5_B_im, m_s5_C_re, m_s5_C_im, m_s5_D, m_s5_w_glu, m_s5_b_glu, m_ssd_A_log, m_ssd_dt_bias, m_ssd_D, m_ssd_norm, m_w_proj_s5, m_w_proj_ssd, m_b_gate, m_w_out, m_ffn2_norm, m_ffn2_w_gate, m_ffn2_w_up, m_ffn2_w_down, m_final_norm, v_ffn1_norm, v_ffn1_w_gate, v_ffn1_w_up, v_ffn1_w_down, v_mix_norm, v_w_in, v_conv_w, v_conv_b, v_s5_A_re, v_s5_A_im, v_s5_log_dt, v_s5_B_re, v_s5_B_im, v_s5_C_re, v_s5_C_im, v_s5_D, v_s5_w_glu, v_s5_b_glu, v_ssd_A_log, v_ssd_dt_bias, v_ssd_D, v_ssd_norm, v_w_proj_s5, v_w_proj_ssd, v_b_gate, v_w_out, v_ffn2_norm, v_ffn2_w_gate, v_ffn2_w_up, v_ffn2_w_down, v_final_norm):
    P = dict(locals())
    T, D = x.shape[1], x.shape[2]
    x0, tgt = x[0], loss_target[0]
    sh = {k: P[k][0] for k in _BIG}

    send = {k: (sh[k] if k == "conv_w" else sh[k].astype(bf16)) for k in _BIG}
    W = {}

    def gather_in(keys):
        return _carry_gather([send[k] for k in keys])

    first_keys = ("ffn1_w_gate", "ffn1_w_up", "conv_w")
    (h1,), got = _rows(_f_rmsnorm, [x0], [ffn1_norm], [(D, bf16)], carry=gather_in(first_keys), name="ffn1_norm")
    W.update(zip(first_keys, got))
    whole = lambda k: W[k].reshape(-1, D)
    conv_w_full = W["conv_w"].transpose(1, 0, 2).reshape(CONV_K, -1)

    d_inner = N_DEV * sh["w_proj_ssd"].shape[0]
    conv_dim = conv_w_full.shape[1]
    H = ssd_A_log.shape[1]
    G = (conv_dim - d_inner) // (2 * SSD_STATE)
    hpg = H // G
    nc = T // CHUNK
    Gs = D // S5_GROUP
    nblk = Gs // S5_GPB
    NS = Gs * S5_STATE
    seg_lens = (D, d_inner, conv_dim, H, 2 * D)

    cuts = [0, D // 3 // 16 * 16, D // 3 // 16 * 16 + 3 * D // 8 // 16 * 16, D]
    win_rows = [send["w_in"][a_:b_] for a_, b_ in zip(cuts[:-1], cuts[1:])]
    wgu1 = _unshard_pair(W["ffn1_w_gate"], W["ffn1_w_up"], name="unshard_ffn1_gate_up")

    def ffn1_down(got):
        W["ffn1_w_down"] = got[0]
        return whole("ffn1_w_down")

    x1, sv1, (got0, got1, got2) = _ffn_fwd(
        x0, ffn1_norm, wgu1, ffn1_down, "ffn1", h=h1,
        carries=(_carry_gather([send["ffn1_w_down"], win_rows[0]]), _carry_gather([win_rows[1]]), _carry_gather([win_rows[2]])))
    ffn1_w = (wgu1, whole("ffn1_w_down"))
    W["w_in"] = jnp.concatenate([got0[1], got1[0], got2[0]], axis=1)
    w_u, w_z, w_xbc, w_dt, w_gl = _unshard_w_in(W["w_in"], seg_lens, name="unshard_w_in")
    h2 = _rows(_f_rmsnorm, [x1], [mix_norm], [(D, bf16)], name="mix_norm")[0]
    u_p = _mm(h2, w_u, o_seg=True, name="in_u")
    z = _mm(h2, w_z, name="in_z")
    xbc = _mm(h2, w_xbc, name="in_xbc")
    gl = _mm(h2, w_gl, name="in_gate")
    dtr = _mm(h2, w_dt, name="in_dt")

    rep = lambda a: jnp.repeat(a, S5_GROUP, axis=0)
    lr, li, ldt = s5_A_re[0], s5_A_im[0], s5_log_dt[0].reshape(Gs, 1)
    brt = s5_B_re[0].transpose(0, 2, 1).reshape(Gs * S5_GROUP, S5_STATE)
    bit = s5_B_im[0].transpose(0, 2, 1).reshape(Gs * S5_GROUP, S5_STATE)
    prep_args = (lr, li, ldt, rep(lr), rep(li), rep(ldt), brt, bit)
    ar, ai, bbrt, bbit = _s5_prep(prep_args, name="s5_prep")
    a_r, a_i = ar.reshape(1, NS), ai.reshape(1, NS)
    wb_r = _block_diag(bbrt.reshape(nblk, S5_GPB, S5_GROUP, S5_STATE)).astype(bf16)
    wb_i = _block_diag(bbit.reshape(nblk, S5_GPB, S5_GROUP, S5_STATE)).astype(bf16)
    c4r = s5_C_re[0].reshape(nblk, S5_GPB, S5_GROUP, S5_STATE).transpose(0, 1, 3, 2)
    c4i = s5_C_im[0].reshape(nblk, S5_GPB, S5_GROUP, S5_STATE).transpose(0, 1, 3, 2)
    wc_r, wc_i = _block_diag(c4r).astype(bf16), _block_diag(c4i).astype(bf16)
    mix_keys = ("s5_w_glu", "w_proj_s5", "w_proj_ssd", "w_out")
    sl_r, sl_i, p_r, p_i, got = _s5_local_scan(u_p, wb_r, wb_i, a_r, a_i, reverse=False, carry=gather_in(mix_keys), name="s5_scan")
    W.update(zip(mix_keys, got))
    w_glu, w_p5, w_pssd, w_o = whole("s5_w_glu"), whole("w_proj_s5"), whole("w_proj_ssd"), whole("w_out")
    c_r, c_i = _s5_carry(sl_r[T - NSEG:], sl_i[T - NSEG:], p_r, p_i, reverse=False, name="s5_carry")
    s_r, s_i, ylin = _s5_fix_out(sl_r, sl_i, a_r, a_i, c_r, c_i, wc_r, wc_i, name="s5_fix_out")
    g5 = _rows(_f_s5_post, [ylin, u_p], [s5_D], [(D, f32)], name="s5_gelu")[0]
    v5 = _mm(g5, w_glu, name="s5_glu_mm")
    o5 = _rows(_f_glu, [g5, v5], [s5_b_glu], [(D, bf16)], name="s5_glu")[0]
    p5 = _mm(o5, w_p5, a_seg=True, name="proj_s5")

    xc = _conv_fwd(xbc, conv_w_full, conv_b, name="conv")
    bias_p, alog_p = _pad_cols(ssd_dt_bias, LANES), _pad_cols(ssd_A_log, LANES)
    expand = (lax.broadcasted_iota(jnp.int32, (LANES, d_inner), 1) // HEADDIM
              == lax.broadcasted_iota(jnp.int32, (LANES, d_inner), 0)).astype(f32)
    dt_p, da_p, dt4, a4 = _rows(_f_dt_expand, [dtr], [bias_p, alog_p, expand],
                                [(LANES, f32), (LANES, f32), (d_inner, f32), (d_inner, f32)], name="ssd_dt")
    row_l = lambda a: a[:, :H].reshape(nc, CHUNK, G, hpg).transpose(2, 0, 3, 1).reshape(G, nc, hpg * CHUNK)
    ssd_in = (xc, dt4, a4, row_l(dt_p), row_l(da_p), jnp.repeat(ssd_D, HEADDIM, axis=1), _ssd_consts(hpg))
    ffn2_keys = ("ffn2_w_gate", "ffn2_w_up", "ffn2_w_down")
    y_ssd, hs, got = _ssd2_fwd(*ssd_in, d_inner=d_inner, carry=gather_in(ffn2_keys), name="ssd")
    W.update(zip(ffn2_keys, got))
    ffn2_w = (_unshard_pair(W["ffn2_w_gate"], W["ffn2_w_up"], name="unshard_ffn2_gate_up"), whole("ffn2_w_down"))
    yn = _rows(_f_gated_norm, [y_ssd, z], [ssd_norm], [(d_inner, bf16)], name="ssd_gated_norm")[0]
    pssd = _mm(yn, w_pssd, name="proj_ssd")

    merged = _rows(_f_merge, [gl, p5, pssd], [b_gate], [(D, bf16)], name="merge")[0]
    x2 = _mm(merged, w_o, add=x1, name="out_proj")
    x3, sv2, _ = _ffn_fwd(x2, ffn2_norm, ffn2_w[0], lambda _: ffn2_w[1], "ffn2")
    lossv, dx3, d_final = _loss_stage(x3, tgt, final_norm.reshape(1, D), name="loss")

    gw = {}
    gs = {"final_norm": d_final}
    slot_mm = lambda a_, b_, name, **kw: _mm(a_, b_, ta=True, o_blk="m", o_slots=True, out_dtype=bf16, name=name, **kw)
    core = lax.axis_index("c").astype(jnp.int32).reshape(1)
    chip = (2 * lax.axis_index("x") + lax.axis_index("y")).astype(jnp.int32).reshape(1)
    chip_sums, arrived = {}, {}

    def sums_of(keys, sib):
        for k, s_ in zip(keys, sib):
            chip_sums[k] = _chip_sum(gw[k], s_, core, name="chip_sum_" + k)
        return [chip_sums[k] for k in keys]

    def level1(keys, tag):
        return sums_of(keys, _exchange_sibling([gw[k] for k in keys], name="exchange_sibling_" + tag))

    dx2, gs["ffn2_norm"], gw["ffn2_w_gate"], gw["ffn2_w_up"], gw["ffn2_w_down"], _, _ = _ffn_bwd(sv2, *ffn2_w, dx3, "ffn2")

    dmerged = _mm(dx2, w_o, tb=True, name="d_merged")
    gw["w_out"] = slot_mm(merged, dx2, "d_w_out")
    (dgl, dp5, dpssd), (gs["b_gate"],) = _rows_bwd(_f_merge, [gl, p5, pssd], [b_gate], [dmerged], name="merge_bwd", want_rows=[0, 1, 2])

    dyn = _mm(dpssd, w_pssd, tb=True, name="d_yn")
    gw["w_proj_ssd"] = slot_mm(yn, dpssd, "d_w_proj_ssd")
    group_a = ("ffn2_w_gate", "ffn2_w_up", "ffn2_w_down", "w_out", "w_proj_ssd")
    (dyssd, dz), (gs["ssd_norm"],) = _rows_bwd(_f_gated_norm, [y_ssd, z], [ssd_norm], [dyn], name="ssd_gated_norm_bwd", want_rows=[0, 1])
    dxs, dbm, dcm, ddtc, ddac, ddtw, ddaw, ddh, sib_a = _ssd2_bwd(
        *ssd_in, hs, dyssd, d_inner=d_inner, carry=_carry_sibling([gw[k] for k in group_a]), name="ssd_bwd")
    parts_a = sums_of(group_a, sib_a)

    def fold(col, row):
        col = col.reshape(T, G, LANES)[:, :, :hpg].reshape(T, H)
        row = row.reshape(G, nc, hpg, CHUNK).transpose(1, 3, 0, 2).reshape(T, H)
        return _pad_cols(col + row, LANES)

    (ddtr,), (dbias_p, dalog_p) = _rows_bwd(_f_dt, [dtr], [bias_p, alog_p], [fold(ddtc, ddtw), fold(ddac, ddaw)], name="ssd_dt_bwd", want_rows=[0])
    gs["ssd_dt_bias"], gs["ssd_A_log"], gs["ssd_D"] = dbias_p[:, :H], dalog_p[:, :H], ddh[:, 0, :hpg].reshape(1, H)
    dxbc, d_conv_w, gs["conv_b"], arr = _conv_bwd(
        xbc, conv_w_full, conv_b, [dxs, dbm, dcm], carry=_carry_chips(parts_a), name="conv_bwd")
    arrived.update(zip(group_a, arr))
    cwk = sh["conv_w"].shape[1]
    gw["conv_w"] = d_conv_w.reshape(CONV_K, N_CHIP, 2, cwk).transpose(2, 1, 0, 3).reshape(N_DEV, CONV_K, cwk)

    do5 = _mm(dp5, w_p5, tb=True, o_seg=True, name="d_o5")
    gw["w_proj_s5"] = slot_mm(o5, dp5, "d_w_proj_s5", a_seg=True)
    (dg5a, dv5), (gs["s5_b_glu"],) = _rows_bwd(_f_glu, [g5, v5], [s5_b_glu], [do5], name="s5_glu_bwd", want_rows=[0, 1])
    dg5 = _mm(dv5, w_glu, tb=True, add=dg5a, name="d_g5")
    gw["s5_w_glu"] = slot_mm(g5, dv5, "d_w_glu")
    (dylin, du_a), (gs["s5_D"],) = _rows_bwd(_f_s5_post, [ylin, u_p], [s5_D], [dg5], name="s5_gelu_bwd", want_rows=[0, 1])
    wct_r, wct_i = wc_r.transpose(0, 2, 1), -wc_i.transpose(0, 2, 1)
    ql_r, ql_i, _, _ = _s5_local_scan(dylin, wct_r, wct_i, a_r, -a_i, reverse=True, powers=False, name="s5_scan_bwd")
    cb_r, cb_i = _s5_carry(ql_r[:NSEG], ql_i[:NSEG], p_r, -p_i, reverse=True, name="s5_carry_bwd")
    tc = min(S5_TC, T)

    def before_blocks(s):
        last = s.reshape(T // tc, tc, NS)[:, tc - NSEG:, :]
        wrap = jnp.concatenate([jnp.zeros((1, 1, NS), f32), last[-1:, : NSEG - 1, :]], axis=1)
        return jnp.concatenate([wrap, last[:-1]], axis=0)

    du_p, dwb_r, dwb_i, dwc_r, dwc_i, d_ar, d_ai = _s5_fix_bwd(
        ql_r, ql_i, a_r, -a_i, cb_r, cb_i, s_r, s_i, before_blocks(s_r), before_blocks(s_i), u_p, dylin, du_a, wb_r, wb_i, name="s5_fix_bwd")
    unblk = lambda w: _block_diag_t(w, S5_GROUP, S5_STATE).reshape(Gs * S5_GROUP, S5_STATE)
    rsum = jnp.repeat(jnp.eye(Gs, dtype=f32), S5_GROUP, axis=1)
    d_lr, d_li, d_ldt, d_brt, d_bit = _s5_prep_bwd(
        prep_args, (d_ar.reshape(Gs, S5_STATE), d_ai.reshape(Gs, S5_STATE), unblk(dwb_r), unblk(dwb_i)), rsum, name="s5_prep_bwd")
    gs["s5_A_re"], gs["s5_A_im"], gs["s5_log_dt"] = d_lr, d_li, d_ldt.reshape(1, Gs)
    gs["s5_B_re"] = d_brt.reshape(Gs, S5_GROUP, S5_STATE).transpose(0, 2, 1)
    gs["s5_B_im"] = d_bit.reshape(Gs, S5_GROUP, S5_STATE).transpose(0, 2, 1)
    gs["s5_C_re"] = _block_diag_t(dwc_r, S5_STATE, S5_GROUP).transpose(0, 1, 3, 2).reshape(Gs, S5_GROUP, S5_STATE)
    gs["s5_C_im"] = _block_diag_t(dwc_i, S5_STATE, S5_GROUP).transpose(0, 1, 3, 2).reshape(Gs, S5_GROUP, S5_STATE)

    d_w_in = [_mm(h2, du_p, ta=True, b_seg=True, name="d_w_u"), _mm(h2, dz, ta=True, name="d_w_z"), _mm(h2, dxbc, ta=True, name="d_w_xbc"),
              _mm(h2, ddtr, ta=True, name="d_w_dt"), _mm(h2, dgl, ta=True, name="d_w_gate")]
    gw["w_in"] = _reshard_w_in(d_w_in, seg_lens, sh["w_in"].shape[1], name="reshard_d_w_in")
    group_c = ("w_proj_s5", "s5_w_glu", "conv_w")
    keys_c = group_c + ("w_in",)
    dh2, sib_c = _mm(du_p, w_u, tb=True, a_seg=True, carry=_carry_sibling([gw[k] for k in keys_c]), name="d_h2_u")
    parts_c = sums_of(keys_c, sib_c)
    c1 = int(D * 0.45) // 16 * 16
    c2 = c1 + D // 4 // 16 * 16
    win = [parts_c[3][:, :c1], parts_c[3][:, c1:c2], parts_c[3][:, c2:]]
    dh2, arr = _mm(dz, w_z, tb=True, add=dh2, carry=_carry_chips(parts_c[:3]), name="d_h2_z")
    arrived.update(zip(group_c, arr))
    dh2, (arr0,) = _mm(dxbc, w_xbc, tb=True, add=dh2, carry=_carry_chips([win[0]]), name="d_h2_xbc")
    dh2, (arr1,) = _mm(dgl, w_gl, tb=True, add=dh2, carry=_carry_chips([win[1]]), name="d_h2_gate")
    dh2 = _mm(ddtr, w_dt, tb=True, add=dh2, name="d_h2_dt")
    (dx1,), (gs["mix_norm"],) = _rows_bwd(_f_rmsnorm, [x1], [mix_norm], [dh2], name="mix_norm_bwd", want_rows=[0], adds={0: dx2})

    def carry_ffn1_down(dwd):
        gw["ffn1_w_down"] = dwd
        return _carry_chips(level1(("ffn1_w_down",), "d") + [win[2]])

    def carry_ffn1_gate_up(dwg, dwu):
        gw["ffn1_w_gate"], gw["ffn1_w_up"] = dwg, dwu
        return _carry_chips(level1(("ffn1_w_gate", "ffn1_w_up"), "e"))

    dx0, gs["ffn1_norm"], _, _, _, arr_w, arr_h = _ffn_bwd(
        sv1, *ffn1_w, dx1, "ffn1", carry_after_dwd=carry_ffn1_down, carry_after_dwgu=carry_ffn1_gate_up)
    arrived["ffn1_w_down"], arr2 = arr_w
    arrived["ffn1_w_gate"], arrived["ffn1_w_up"] = arr_h
    arrived["w_in"] = jnp.concatenate([arr0, arr1, arr2], axis=1)

    small_shapes = {k: (P[k][0].shape if P[k].ndim > 1 else P[k].shape) for k in _SMALL}
    pack = lambda d: jnp.concatenate([_pad_flat(d[k], TILE_ELEMS) for k in _SMALL]).reshape(-1, LANES)
    gsmall = _sum_slots(_all_gather([pack(gs)], name="gather_small_grads")[0], name="sum_small_grads")
    snum = {k: math.prod(small_shapes[k]) for k in _SMALL}
    ssz = {k: -(-snum[k] // TILE_ELEMS) * TILE_ELEMS for k in _SMALL}

    grads, delta, new_m, new_v = {}, {}, {}, {}
    for k in _BIG:
        grads[k], delta[k], new_m[k], new_v[k] = _reduce_adamw(
            chip_sums[k], arrived[k], chip, P[k][0], P["m_" + k][0], P["v_" + k][0], name="adamw_" + k)
    d_s, m_s, v_s = _adamw(gsmall, pack({k: P[k] for k in _SMALL}), pack({k: P["m_" + k] for k in _SMALL}),
                           pack({k: P["v_" + k] for k in _SMALL}), name="adamw_small")
    off = 0
    gflat, dflat, mflat, vflat = gsmall.reshape(-1), d_s.reshape(-1), m_s.reshape(-1), v_s.reshape(-1)
    for k in _SMALL:
        n = snum[k]
        grads[k], delta[k], new_m[k], new_v[k] = (a[off:off + n] for a in (gflat, dflat, mflat, vflat))
        off += ssz[k]

    loss = lax.psum(lossv[0, 0], ("x", "y", "c"))
    out = [loss, dx0.reshape(x.shape)]
    for d in (grads, delta, new_m, new_v):
        out += [d[k].reshape(P[k].shape) for k in _WEIGHTS]
    return tuple(out)
```

```python
import math
from typing import Callable, NamedTuple

import jax
import jax.numpy as jnp
from jax import lax
from jax.experimental import pallas as pl
from jax.experimental.pallas import tpu as pltpu

f32 = jnp.float32
bf16 = jnp.bfloat16
_S = jax.ShapeDtypeStruct

EPS = 1e-6
S5_GROUP = 16
S5_STATE = 64
HEADDIM = 64
SSD_STATE = 128
CHUNK = 64
CONV_K = 4
NSEG = 8
S5_GPB = 16
N_DEV = 8
LANES = 128
TILE_ELEMS = 8 * LANES

ADAM_LR = 0.001
ADAM_B1 = 0.9
ADAM_B2 = 0.999
ADAM_EPS = 1e-08
ADAM_WD = 0.01
ADAM_STEP = 10

VMEM_LIMIT = 56 * 1024 * 1024
MM_FULL_K = 3072
MM_MAX_TN = 3072
EPI_TM = 256
ROW_BLOCK_BYTES = 12 * 1024 * 1024
EW_BLOCK_ELEMS = 1 << 20
MESH = pl.DeviceIdType.MESH


def _cparams(sem=None):
    return pltpu.CompilerParams(dimension_semantics=sem, vmem_limit_bytes=VMEM_LIMIT)


def _pick(dim, pref, align=LANES):
    best = None
    t = align
    while t <= min(dim, pref):
        if dim % t == 0:
            best = t
        t += align
    return best or dim


def _slot_of(k):
    return (k & 1) * (N_DEV // 2) + (k >> 1)


def _mm(a, b, *, name, ta=False, tb=False, a_blk=None, b_blk=None, o_blk=None, o_slots=False, a_seg=False, b_seg=False,
        o_seg=False, tm=None, out_dtype=f32, scale=1.0, add=None, epi=None, carry=None):
    a2, b2 = a.shape[-2:], b.shape[-2:]
    Ma, Ka = (a2[1], a2[0]) if ta else a2
    Kb, Nb = (b2[1], b2[0]) if tb else b2
    M = Ma * (a.shape[0] if a_blk == "m" else 1)
    K = Ka * (a.shape[0] if a_blk == "k" else 1)
    N = Nb * (b.shape[0] if b_blk == "n" else 1)
    assert K == Kb * (b.shape[0] if b_blk == "k" else 1), (a.shape, b.shape, ta, tb, a_blk, b_blk)
    assert (a.ndim == 3) == (a_blk is not None) and (b.ndim == 3) == (b_blk is not None)
    tm = Ma if a_blk == "m" else (tm or _pick(M, 512))
    tn = Nb if b_blk == "n" else _pick(N, MM_MAX_TN)
    if a_blk == "k" or b_blk == "k":
        tk = Ka if a_blk == "k" else Kb
        assert tk == (Kb if b_blk == "k" else tk)
    else:
        tk = K if K <= MM_FULL_K else _pick(K, 1024 if ta else MM_FULL_K)
    if (a_seg and not ta) or o_seg:
        tm = M // NSEG
    if (a_seg and ta) or b_seg:
        tk = K // NSEG
    gm, gn, nk = M // tm, N // tn, K // tk
    assert not (add is not None and (o_seg or o_blk)) and not (o_blk and o_seg)

    if a_seg:
        assert a.ndim == 2
        a = a.reshape(a.shape[0] // NSEG, NSEG * a.shape[1])
        if ta:
            a_spec = pl.BlockSpec((tk, tm), lambda i, j, k: (0, k * (Ma // tm) + i))
        else:
            a_spec = pl.BlockSpec((tm, tk), lambda i, j, k: (0, i * (Ka // tk) + k))
    elif a.ndim == 3:
        lead = (lambda i, k: i) if a_blk == "m" else (lambda i, k: k)
        if ta:
            a_spec = pl.BlockSpec((None, tk, tm), lambda i, j, k: (lead(i, k), 0 if a_blk == "k" else k, 0 if a_blk == "m" else i))
        else:
            a_spec = pl.BlockSpec((None, tm, tk), lambda i, j, k: (lead(i, k), 0 if a_blk == "m" else i, 0 if a_blk == "k" else k))
    else:
        a_spec = pl.BlockSpec((tk, tm), lambda i, j, k: (k, i)) if ta else pl.BlockSpec((tm, tk), lambda i, j, k: (i, k))
    if b_seg:
        assert b.ndim == 2 and not tb
        b = b.reshape(b.shape[0] // NSEG, NSEG * b.shape[1])
        b_spec = pl.BlockSpec((tk, tn), lambda i, j, k: (0, k * (Nb // tn) + j))
    elif b.ndim == 3:
        lead = (lambda j, k: j) if b_blk == "n" else (lambda j, k: k)
        if tb:
            b_spec = pl.BlockSpec((None, tn, tk), lambda i, j, k: (lead(j, k), 0 if b_blk == "n" else j, 0 if b_blk == "k" else k))
        else:
            b_spec = pl.BlockSpec((None, tk, tn), lambda i, j, k: (lead(j, k), 0 if b_blk == "k" else k, 0 if b_blk == "n" else j))
    else:
        b_spec = pl.BlockSpec((tn, tk), lambda i, j, k: (j, k)) if tb else pl.BlockSpec((tk, tn), lambda i, j, k: (k, j))
    slot = _slot_of if o_slots else (lambda k: k)
    if o_blk == "n":
        assert gn == N_DEV or not o_slots
        o_shape, o_spec = (gn, M, tn), pl.BlockSpec((None, tm, tn), lambda i, j, k: (slot(j), i, 0))
    elif o_blk == "m" and o_slots and gm < N_DEV:
        rs = M // N_DEV
        per_tile = tm // rs
        assert per_tile % 2 == 0 and tm % rs == 0
        o_shape = (2, N_CHIP, rs, N)
        o_spec = pl.BlockSpec((2, per_tile // 2, rs, tn), lambda i, j, k: (0, i, 0, j))
    elif o_blk == "m":
        assert gm == N_DEV or not o_slots
        o_shape, o_spec = (gm, tm, N), pl.BlockSpec((None, tm, tn), lambda i, j, k: (slot(i), 0, j))
    elif o_seg:
        o_shape, o_spec = (tm, NSEG * N), pl.BlockSpec((tm, tn), lambda i, j, k: (0, i * (N // tn) + j))
    else:
        o_shape, o_spec = (M, N), pl.BlockSpec((tm, tn), lambda i, j, k: (i, j))
    dims = (((0 if ta else 1,), (1 if tb else 0,)), ((), ()))
    if epi is not None:
        assert gn == 1 and add is None and o_blk is None and not o_seg
        epi_fn, add, epi_w = epi
        o_shape, o_spec = (M, epi_w), pl.BlockSpec((tm, epi_w), lambda i, j, k: (i, 0))
    has_add = add is not None
    add_spec = pl.BlockSpec((tm, add.shape[1]), lambda i, j, k: (i, 0)) if epi is not None else o_spec

    carry = carry or _NO_CARRY
    n_in = 2 + has_add

    def body(*refs):
        own, c_in, c_out, c_sems = _carry_split(carry, refs, n_in, 1)
        a_ref, b_ref = own[0], own[1]
        add_ref = own[2] if has_add else None
        o_ref, acc_ref = own[-2], own[-1]
        i, j, k = pl.program_id(0), pl.program_id(1), pl.program_id(2)
        _carry_start(carry, c_in, c_out, c_sems, (i == 0) & (j == 0) & (k == 0))

        @pl.when(k == 0)
        def _():
            acc_ref[...] = jnp.zeros_like(acc_ref)

        acc_ref[...] += lax.dot_general(a_ref[...].astype(bf16), b_ref[...].astype(bf16), dims, preferred_element_type=f32)

        @pl.when(k == nk - 1)
        def _():
            r = acc_ref[...] * scale
            if epi is not None:
                r = epi_fn(r, add_ref[...].astype(f32))
            elif has_add:
                r = r + add_ref[...].astype(f32)
            if len(o_shape) == 4:
                rs = o_shape[2]
                for chip_l in range(o_ref.shape[1]):
                    for core in range(2):
                        dev = 2 * chip_l + core
                        o_ref[core, chip_l] = r[dev * rs:(dev + 1) * rs].astype(out_dtype)
            else:
                o_ref[...] = r.astype(out_dtype)

        _carry_finish(carry, c_in, c_out, c_sems, (i == gm - 1) & (j == gn - 1) & (k == nk - 1))

    ins = [a, b] + ([add] if has_add else []) + list(carry.ins)
    in_specs = [a_spec, b_spec] + ([add_spec] if has_add else []) + [_ANY] * len(carry.ins)
    res = pl.pallas_call(
        body, name=name, grid=(gm, gn, nk), in_specs=in_specs, out_specs=[o_spec] + [_ANY] * len(carry.out_shapes),
        out_shape=[_S(o_shape, out_dtype)] + list(carry.out_shapes),
        scratch_shapes=[pltpu.VMEM((tm, tn), f32)] + list(carry.sems),
        compiler_params=_cparams(("arbitrary",) * 3 if carry.ins else ("parallel", "parallel", "arbitrary")),
    )(*ins)
    out = res[0]
    if len(o_shape) == 4:
        out = out.reshape(N_DEV, o_shape[2], N)
    elif o_seg:
        out = out.reshape(M, N)
    return (out, list(res[1:])) if carry.ins else out


def _row_tile(T, widths):
    budget = ROW_BLOCK_BYTES
    tb = max(16, budget // (4 * sum(widths)))
    return _pick(T, tb, align=16)


def _rows(fn, rows, params, outs, *, name, carry=None):
    T = rows[0].shape[0]
    nr, npar = len(rows), len(params)
    tb = _row_tile(T, [r.shape[1] for r in rows] + [w for w, _ in outs])
    carry = carry or _NO_CARRY

    def body(*refs):
        own, c_in, c_out, c_sems = _carry_split(carry, refs, nr + npar, len(outs))
        _carry_start(carry, c_in, c_out, c_sems, pl.program_id(0) == 0)
        ins = [r[...].astype(f32) for r in own[: nr + npar]]
        res = fn(*ins)
        for o_ref, r in zip(own[nr + npar:], res):
            o_ref[...] = r.astype(o_ref.dtype)
        _carry_finish(carry, c_in, c_out, c_sems, pl.program_id(0) == T // tb - 1)

    in_specs = [pl.BlockSpec((tb, r.shape[1]), lambda i: (i, 0)) for r in rows]
    in_specs += [pl.BlockSpec(p.shape, lambda i: (0, 0)) for p in params]
    out_specs = [pl.BlockSpec((tb, w), lambda i: (i, 0)) for w, _ in outs]
    res = pl.pallas_call(
        body, name=name, grid=(T // tb,), in_specs=in_specs + [_ANY] * len(carry.ins),
        out_specs=out_specs + [_ANY] * len(carry.out_shapes), out_shape=[_S((T, w), d) for w, d in outs] + list(carry.out_shapes),
        scratch_shapes=list(carry.sems), compiler_params=_cparams(("arbitrary",) if carry.ins else ("parallel",)),
    )(*rows, *params, *carry.ins)
    return (tuple(res[:len(outs)]), list(res[len(outs):])) if carry.ins else tuple(res)


def _rows_bwd(fn, rows, params, cots, *, name, want_rows, row_dtypes=None, adds=None):
    T = rows[0].shape[0]
    nr, npar, nc = len(rows), len(params), len(cots)
    adds = adds or {}
    add_idx = sorted(adds)
    row_dtypes = row_dtypes or {}
    widths = [r.shape[1] for r in rows] + [c.shape[1] for c in cots] + [rows[i].shape[1] for i in want_rows]
    tb = _row_tile(T, widths)

    def body(*refs):
        ins = [r[...].astype(f32) for r in refs[: nr + npar]]
        cot = tuple(r[...].astype(f32) for r in refs[nr + npar: nr + npar + nc])
        add_refs = refs[nr + npar + nc: nr + npar + nc + len(add_idx)]
        out_refs = refs[nr + npar + nc + len(add_idx):]
        _, vjp = jax.vjp(lambda *a: tuple(fn(*a)), *ins)
        g = vjp(cot)
        for o_ref, i in zip(out_refs[: len(want_rows)], want_rows):
            r = g[i]
            if i in adds:
                r = r + add_refs[add_idx.index(i)][...].astype(f32)
            o_ref[...] = r.astype(o_ref.dtype)
        first = pl.program_id(0) == 0
        for o_ref, gp in zip(out_refs[len(want_rows):], g[nr:]):
            @pl.when(first)
            def _(o_ref=o_ref):
                o_ref[...] = jnp.zeros_like(o_ref)

            o_ref[...] += gp

    in_specs = [pl.BlockSpec((tb, r.shape[1]), lambda i: (i, 0)) for r in rows]
    in_specs += [pl.BlockSpec(p.shape, lambda i: (0, 0)) for p in params]
    in_specs += [pl.BlockSpec((tb, c.shape[1]), lambda i: (i, 0)) for c in cots]
    in_specs += [pl.BlockSpec((tb, adds[i].shape[1]), lambda i_: (i_, 0)) for i in add_idx]
    out_specs = [pl.BlockSpec((tb, rows[i].shape[1]), lambda i_: (i_, 0)) for i in want_rows]
    out_specs += [pl.BlockSpec(p.shape, lambda i: (0, 0)) for p in params]
    out_shape = [_S(rows[i].shape, row_dtypes.get(i, f32)) for i in want_rows] + [_S(p.shape, f32) for p in params]
    res = pl.pallas_call(
        body, name=name, grid=(T // tb,), in_specs=in_specs, out_specs=out_specs, out_shape=out_shape,
        compiler_params=_cparams(("arbitrary",)),
    )(*rows, *params, *cots, *[adds[i] for i in add_idx])
    return list(res[: len(want_rows)]), list(res[len(want_rows):])


def _f_rmsnorm(x, g):
    return (x * lax.rsqrt(jnp.mean(x * x, axis=-1, keepdims=True) + EPS) * g,)


def _f_swiglu(ab):
    F = ab.shape[1] // 2
    return (jax.nn.silu(ab[:, :F]) * ab[:, F:],)


def _f_s5_post(y, u, d):
    return (jax.nn.gelu(y + d * u),)


def _f_glu(g, v, b):
    return (g * jax.nn.sigmoid(v + b),)


def _f_gated_norm(y, z, w):
    return _f_rmsnorm(y * jax.nn.silu(z), w)


def _f_merge(gl, p5, pssd, b):
    D = p5.shape[1]
    gates = jax.nn.sigmoid(gl + b)
    return (gates[:, :D] * p5 + gates[:, D:] * pssd,)


def _f_dt(dtr, bias, a_log):
    dt = jax.nn.softplus(dtr + bias)
    return dt, dt * (-jnp.exp(a_log))


def _f_dt_expand(dtr, bias, a_log, e):
    dt, a = _f_dt(dtr, bias, a_log)
    return dt, a, _doth(dt, e), _doth(a, e)


def _loss_stage(x, tgt, g, *, name):
    T, D = x.shape
    tb = _row_tile(T, [D, D, D])

    def f(xb, gb, tb_):
        y = _f_rmsnorm(xb, gb)[0]
        return 0.5 * jnp.sum(jnp.mean(jnp.square(y - tb_), axis=-1, keepdims=True), axis=0, keepdims=True)

    def body(x_ref, t_ref, g_ref, l_ref, dx_ref, dg_ref):
        tv = t_ref[...]
        val, vjp = jax.vjp(lambda a, b: f(a, b, tv), x_ref[...], g_ref[...])
        dx, dg = vjp(jnp.ones((1, 1), f32))
        dx_ref[...] = dx

        @pl.when(pl.program_id(0) == 0)
        def _():
            l_ref[...] = jnp.zeros_like(l_ref)
            dg_ref[...] = jnp.zeros_like(dg_ref)

        l_ref[...] += jnp.broadcast_to(val, l_ref.shape)
        dg_ref[...] += dg

    row = pl.BlockSpec((tb, D), lambda i: (i, 0))
    par = pl.BlockSpec((1, D), lambda i: (0, 0))
    return pl.pallas_call(
        body, name=name, grid=(T // tb,), in_specs=[row, row, par],
        out_specs=[pl.BlockSpec((1, LANES), lambda i: (0, 0)), row, par],
        out_shape=[_S((1, LANES), f32), _S((T, D), f32), _S((1, D), f32)], compiler_params=_cparams(("arbitrary",)),
    )(x, tgt, g)


CONV_R = 64
HALO = 8


def _conv_shifts_down(ref, t):
    if isinstance(t, int) and t == 0:
        cur = ref[0:CONV_R, :]
        row = lax.broadcasted_iota(jnp.int32, cur.shape, 0)
        return [cur] + [jnp.where(row >= s, pltpu.roll(cur, s, axis=0), 0.0) for s in range(1, CONV_K)]
    win = ref[pl.ds(pl.multiple_of(t * CONV_R - HALO, HALO), CONV_R + HALO), :]
    return [win[HALO:]] + [pltpu.roll(win, s, axis=0)[HALO:] for s in range(1, CONV_K)]


def _conv_shifts_up(ref, t):
    if isinstance(t, int):
        cur = ref[t * CONV_R:(t + 1) * CONV_R, :]
        row = lax.broadcasted_iota(jnp.int32, cur.shape, 0)
        return [cur] + [jnp.where(row < CONV_R - s, pltpu.roll(cur, CONV_R - s, axis=0), 0.0) for s in range(1, CONV_K)]
    win = ref[pl.ds(pl.multiple_of(t * CONV_R, HALO), CONV_R + HALO), :]
    return [win[:CONV_R]] + [pltpu.roll(win, CONV_R + HALO - s, axis=0)[:CONV_R] for s in range(1, CONV_K)]


def _conv_pre(shifted, w, b):
    pre = b
    for k in range(CONV_K):
        pre = pre + w[k:k + 1, :] * shifted[CONV_K - 1 - k]
    return pre


def _conv_fwd(x, w, b, *, name):
    T, C = x.shape
    cb = _pick(C, 256)

    def body(x_ref, w_ref, b_ref, o_ref):
        xv = x_ref[...]
        row = lax.broadcasted_iota(jnp.int32, xv.shape, 0)
        shifted = [xv] + [jnp.where(row >= s, pltpu.roll(xv, s, axis=0), 0.0) for s in range(1, CONV_K)]
        o_ref[...] = jax.nn.silu(_conv_pre(shifted, w_ref[...], b_ref[...]))

    col = pl.BlockSpec((T, cb), lambda j: (0, j))
    return pl.pallas_call(
        body, name=name, grid=(C // cb,), in_specs=[col, pl.BlockSpec((CONV_K, cb), lambda j: (0, j)), pl.BlockSpec((1, cb), lambda j: (0, j))],
        out_specs=col, out_shape=_S((T, C), f32), compiler_params=_cparams(("parallel",)),
    )(x, w, b)


def _conv_bwd(x, w, b, dy, *, name, carry=None):
    T, C = x.shape
    cb = _pick(C, 128)
    carry = carry or _NO_CARRY
    ends = []
    for d in dy:
        ends.append((ends[-1] if ends else 0) + d.shape[1] // cb)
    assert ends[-1] == C // cb and all(d.shape[1] % cb == 0 for d in dy)
    npc = len(dy)
    n_tiles = T // CONV_R

    def body(*refs):
        own, c_in, c_out, c_sems = _carry_split(carry, refs, 3 + npc, 3)
        x_ref, w_ref, b_ref = own[:3]
        dy_refs, (dx_ref, dw_ref, db_ref, dpre_ref) = own[3:3 + npc], own[3 + npc:]
        _carry_start(carry, c_in, c_out, c_sems, pl.program_id(0) == 0)
        j = pl.program_id(0)
        wv, bv = w_ref[...], b_ref[...]

        def fold8(v):
            return jnp.sum(v.reshape(CONV_R // 8, 8, cb), axis=0)

        def first_pass(t, acc):
            rows = slice(0, CONV_R) if isinstance(t, int) else pl.ds(pl.multiple_of(t * CONV_R, CONV_R), CONV_R)
            shifted = _conv_shifts_down(x_ref, t)
            pre = _conv_pre(shifted, wv, bv)
            dyv = dy_refs[-1][rows, :]
            for p in range(npc - 2, -1, -1):
                dyv = jnp.where(j < ends[p], dy_refs[p][rows, :], dyv)
            sg = jax.nn.sigmoid(pre)
            dpre = dyv * sg * (1.0 + pre * (1.0 - sg))
            dpre_ref[rows, :] = dpre
            return tuple(acc[k] + fold8(dpre * shifted[CONV_K - 1 - k]) for k in range(CONV_K)) + (acc[CONV_K] + fold8(dpre),)

        zero = jnp.zeros((8, cb), f32)
        acc = lax.fori_loop(1, n_tiles, first_pass, first_pass(0, (zero,) * (CONV_K + 1)), unroll=3)
        for k in range(CONV_K):
            dw_ref[k:k + 1, :] = jnp.sum(acc[k], axis=0, keepdims=True)
        db_ref[...] = jnp.sum(acc[CONV_K], axis=0, keepdims=True)

        def dx_of(t):
            up = _conv_shifts_up(dpre_ref, t)
            dx = wv[CONV_K - 1:CONV_K, :] * up[0]
            for k in range(CONV_K - 1):
                dx = dx + wv[k:k + 1, :] * up[CONV_K - 1 - k]
            return dx

        def second_pass(t, c):
            dx_ref[pl.ds(pl.multiple_of(t * CONV_R, CONV_R), CONV_R), :] = dx_of(t)
            return c

        lax.fori_loop(0, n_tiles - 1, second_pass, 0, unroll=3)
        dx_ref[(n_tiles - 1) * CONV_R:, :] = dx_of(n_tiles - 1)
        _carry_finish(carry, c_in, c_out, c_sems, pl.program_id(0) == C // cb - 1)

    col = pl.BlockSpec((T, cb), lambda j: (0, j))
    wsp = pl.BlockSpec((CONV_K, cb), lambda j: (0, j))
    bsp = pl.BlockSpec((1, cb), lambda j: (0, j))
    starts = [0] + ends[:-1]
    dy_specs = [pl.BlockSpec((T, cb), lambda j, s=s, e=e: (0, jnp.clip(j, s, e - 1) - s)) for s, e in zip(starts, ends)]
    res = pl.pallas_call(
        body, name=name, grid=(C // cb,), in_specs=[col, wsp, bsp] + dy_specs + [_ANY] * len(carry.ins),
        out_specs=[col, wsp, bsp] + [_ANY] * len(carry.out_shapes),
        out_shape=[_S((T, C), f32), _S((CONV_K, C), f32), _S((1, C), f32)] + list(carry.out_shapes),
        scratch_shapes=[pltpu.VMEM((T, cb), f32)] + list(carry.sems),
        compiler_params=_cparams(("arbitrary",) if carry.ins else ("parallel",)),
    )(x, w, b, *dy, *carry.ins)
    return (*res[:3], list(res[3:]))


def _f_s5_prep(lr, li, ldt, lrb, lib, ldtb, brt, bit):
    def disc(lr_, li_, ldt_):
        dt = jnp.exp(ldt_)
        mag = jnp.exp(lr_ * dt)
        ar, ai = mag * jnp.cos(li_ * dt), mag * jnp.sin(li_ * dt)
        den = lr_ * lr_ + li_ * li_
        cr = ((ar - 1.0) * lr_ + ai * li_) / den
        ci = (ai * lr_ - (ar - 1.0) * li_) / den
        return ar, ai, cr, ci

    ar, ai, _, _ = disc(lr, li, ldt)
    _, _, cr, ci = disc(lrb, lib, ldtb)
    return ar, ai, cr * brt - ci * bit, cr * bit + ci * brt


def _s5_prep(args, *, name):
    G, N = args[0].shape
    GM = args[3].shape[0]

    def body(*refs):
        res = _f_s5_prep(*[r[...] for r in refs[:8]])
        for o, r in zip(refs[8:], res):
            o[...] = r

    return pl.pallas_call(body, name=name, out_shape=[_S((G, N), f32)] * 2 + [_S((GM, N), f32)] * 2)(*args)


def _s5_prep_bwd(args, cots, rsum, *, name):
    G, N = args[0].shape
    GM = args[3].shape[0]

    def body(*refs):
        ins = [r[...] for r in refs[:8]]
        cot = tuple(r[...] for r in refs[8:12])
        rs = refs[12][...]
        _, vjp = jax.vjp(_f_s5_prep, *ins)
        g = vjp(cot)
        fold = lambda v: jnp.dot(rs, v, preferred_element_type=f32, precision=lax.Precision.HIGHEST)
        o = refs[13:]
        o[0][...] = g[0] + fold(g[3])
        o[1][...] = g[1] + fold(g[4])
        o[2][...] = g[2] + fold(jnp.broadcast_to(g[5], (GM, LANES)))[:, 0:1]
        o[3][...] = g[6]
        o[4][...] = g[7]

    return pl.pallas_call(
        body, name=name, out_shape=[_S((G, N), f32), _S((G, N), f32), _S((G, 1), f32), _S((GM, N), f32), _S((GM, N), f32)],
    )(*args, *cots, rsum)


S5_TC = 512


def _s5_local_scan(src, w_r, w_i, a_r, a_i, *, reverse, name, carry=None, powers=True):
    T, C = src.shape
    nblk, cb, sb = w_r.shape
    NS = nblk * sb
    tc = min(S5_TC, T)
    nT, nt = T // tc, tc // NSEG
    tmap = (lambda i: nT - 1 - i) if reverse else (lambda i: i)

    carry = carry or _NO_CARRY

    def body(*refs):
        own, c_in, c_out, c_sems = _carry_split(carry, refs, 5, 4)
        u_ref, wr_ref, wi_ref, ar_ref, ai_ref, sr_ref, si_ref, pr_ref, pi_ref, st_r, st_i, pw_r, pw_i = own
        _carry_start(carry, c_in, c_out, c_sems, (pl.program_id(0) == 0) & (pl.program_id(1) == 0))

        @pl.when(pl.program_id(1) == 0)
        def _():
            st_r[...] = jnp.zeros_like(st_r)
            st_i[...] = jnp.zeros_like(st_i)
            pw_r[...] = jnp.ones_like(pw_r)
            pw_i[...] = jnp.zeros_like(pw_i)

        u = u_ref[...].astype(bf16)
        sr_ref[...] = jnp.dot(u, wr_ref[...], preferred_element_type=f32)
        si_ref[...] = jnp.dot(u, wi_ref[...], preferred_element_type=f32)
        ar = jnp.broadcast_to(ar_ref[...], (NSEG, sb))
        ai = jnp.broadcast_to(ai_ref[...], (NSEG, sb))

        def step(k, c):
            cr, ci, qr, qi = c
            kk = (nt - 1 - k) if reverse else k
            rows = pl.ds(pl.multiple_of(kk * NSEG, NSEG), NSEG)
            nr = ar * cr - ai * ci + sr_ref[rows, :]
            ni = ar * ci + ai * cr + si_ref[rows, :]
            sr_ref[rows, :] = nr
            si_ref[rows, :] = ni
            return (nr, ni, ar * qr - ai * qi, ar * qi + ai * qr) if powers else (nr, ni, qr, qi)

        cr, ci, qr, qi = lax.fori_loop(0, nt, step, (st_r[...], st_i[...], pw_r[...], pw_i[...]), unroll=8)
        st_r[...], st_i[...], pw_r[...], pw_i[...] = cr, ci, qr, qi
        pr_ref[...] = qr
        pi_ref[...] = qi
        _carry_finish(carry, c_in, c_out, c_sems, (pl.program_id(0) == nblk - 1) & (pl.program_id(1) == nT - 1))

    blk = pl.BlockSpec((tc, sb), lambda j, i: (tmap(i), j))
    wsp = pl.BlockSpec((None, cb, sb), lambda j, i: (j, 0, 0))
    asp = pl.BlockSpec((1, sb), lambda j, i: (0, j))
    psp = pl.BlockSpec((NSEG, sb), lambda j, i: (0, j))
    res = pl.pallas_call(
        body, name=name, grid=(nblk, nT),
        in_specs=[pl.BlockSpec((tc, cb), lambda j, i: (tmap(i), j)), wsp, wsp, asp, asp] + [_ANY] * len(carry.ins),
        out_specs=[blk, blk, psp, psp] + [_ANY] * len(carry.out_shapes),
        out_shape=[_S((T, NS), f32)] * 2 + [_S((NSEG, NS), f32)] * 2 + list(carry.out_shapes),
        scratch_shapes=[pltpu.VMEM((NSEG, sb), f32)] * 4 + list(carry.sems),
        compiler_params=_cparams(("arbitrary", "arbitrary") if carry.ins else ("parallel", "arbitrary")),
    )(src, w_r, w_i, a_r, a_i, *carry.ins)
    return (*res[:4], list(res[4:])) if carry.ins else res


def _s5_carry(e_r, e_i, p_r, p_i, *, reverse, name):
    NS = e_r.shape[1]

    def body(er_ref, ei_ref, pr_ref, pi_ref, cr_ref, ci_ref):
        ar, ai = pr_ref[0:1, :], pi_ref[0:1, :]
        cr = jnp.zeros((1, NS), f32)
        ci = jnp.zeros((1, NS), f32)
        order = list(range(NSEG - 1, -1, -1)) if reverse else list(range(NSEG))
        cr_ref[order[0]:order[0] + 1, :] = cr
        ci_ref[order[0]:order[0] + 1, :] = ci
        for prev, q in zip(order[:-1], order[1:]):
            er, ei = er_ref[prev:prev + 1, :], ei_ref[prev:prev + 1, :]
            cr, ci = er + ar * cr - ai * ci, ei + ar * ci + ai * cr
            cr_ref[q:q + 1, :] = cr
            ci_ref[q:q + 1, :] = ci

    return pl.pallas_call(body, name=name, out_shape=[_S((NSEG, NS), f32)] * 2)(e_r, e_i, p_r, p_i)


def _s5_fix_out(sl_r, sl_i, a_r, a_i, c_r, c_i, wc_r, wc_i, *, name):
    T, NS = sl_r.shape
    nblk, sb, cb = wc_r.shape
    tc = min(S5_TC, T)
    nT, nt = T // tc, tc // NSEG

    def body(lr_ref, li_ref, ar_ref, ai_ref, cr_ref, ci_ref, wr_ref, wi_ref, sr_ref, si_ref, y_ref, pw_r, pw_i):
        @pl.when(pl.program_id(1) == 0)
        def _():
            pw_r[...] = jnp.ones_like(pw_r)
            pw_i[...] = jnp.zeros_like(pw_i)

        ar = jnp.broadcast_to(ar_ref[...], (NSEG, sb))
        ai = jnp.broadcast_to(ai_ref[...], (NSEG, sb))
        cr, ci = cr_ref[...], ci_ref[...]

        def step(k, c):
            qr, qi = c
            qr, qi = ar * qr - ai * qi, ar * qi + ai * qr
            rows = pl.ds(pl.multiple_of(k * NSEG, NSEG), NSEG)
            sr_ref[rows, :] = lr_ref[rows, :] + qr * cr - qi * ci
            si_ref[rows, :] = li_ref[rows, :] + qr * ci + qi * cr
            return qr, qi

        qr, qi = lax.fori_loop(0, nt, step, (pw_r[...], pw_i[...]), unroll=8)
        pw_r[...], pw_i[...] = qr, qi
        y_ref[...] = (jnp.dot(sr_ref[...].astype(bf16), wr_ref[...], preferred_element_type=f32)
                      - jnp.dot(si_ref[...].astype(bf16), wi_ref[...], preferred_element_type=f32))

    blk = pl.BlockSpec((tc, sb), lambda j, i: (i, j))
    asp = pl.BlockSpec((1, sb), lambda j, i: (0, j))
    csp = pl.BlockSpec((NSEG, sb), lambda j, i: (0, j))
    wsp = pl.BlockSpec((None, sb, cb), lambda j, i: (j, 0, 0))
    return pl.pallas_call(
        body, name=name, grid=(nblk, nT), in_specs=[blk, blk, asp, asp, csp, csp, wsp, wsp],
        out_specs=[blk, blk, pl.BlockSpec((tc, cb), lambda j, i: (i, j))],
        out_shape=[_S((T, NS), f32)] * 2 + [_S((T, nblk * cb), f32)],
        scratch_shapes=[pltpu.VMEM((NSEG, sb), f32)] * 2, compiler_params=_cparams(("parallel", "arbitrary")),
    )(sl_r, sl_i, a_r, a_i, c_r, c_i, wc_r, wc_i)


def _s5_fix_bwd(ql_r, ql_i, ab_r, ab_i, c_r, c_i, s_r, s_i, sb_r, sb_i, u, dy, du_add, w_r, w_i, *, name):
    T, NS = ql_r.shape
    nblk, cb, sb = w_r.shape
    tc = min(S5_TC, T)
    nT, nt = T // tc, tc // NSEG
    tmap = lambda i: nT - 1 - i

    def body(lr_ref, li_ref, ar_ref, ai_ref, cr_ref, ci_ref, sr_ref, si_ref, br_ref, bi_ref, u_ref, dy_ref, dua_ref, wr_ref, wi_ref,
             du_ref, dwr_ref, dwi_ref, dcr_ref, dci_ref, dar_ref, dai_ref, pw_r, pw_i, ac_r, ac_i, q_r, q_i):
        first = pl.program_id(1) == 0

        @pl.when(first)
        def _():
            pw_r[...] = jnp.ones_like(pw_r)
            pw_i[...] = jnp.zeros_like(pw_i)
            ac_r[...] = jnp.zeros_like(ac_r)
            ac_i[...] = jnp.zeros_like(ac_i)
            dwr_ref[...] = jnp.zeros_like(dwr_ref)
            dwi_ref[...] = jnp.zeros_like(dwi_ref)
            dcr_ref[...] = jnp.zeros_like(dcr_ref)
            dci_ref[...] = jnp.zeros_like(dci_ref)

        ar = jnp.broadcast_to(ar_ref[...], (NSEG, sb))
        ai = jnp.broadcast_to(ai_ref[...], (NSEG, sb))
        cr, ci = cr_ref[...], ci_ref[...]

        def fix(rows, qr, qi, spr, spi, accr, acci):
            qr, qi = ar * qr - ai * qi, ar * qi + ai * qr
            xr = lr_ref[rows, :] + qr * cr - qi * ci
            xi = li_ref[rows, :] + qr * ci + qi * cr
            q_r[rows, :] = xr
            q_i[rows, :] = xi
            return qr, qi, accr + xr * spr + xi * spi, acci + xi * spr - xr * spi

        def step(k, c):
            qr, qi, accr, acci = c
            kk = nt - 1 - k
            rows = pl.ds(pl.multiple_of(kk * NSEG, NSEG), NSEG)
            prev = pl.ds(pl.multiple_of((kk - 1) * NSEG, NSEG), NSEG)
            return fix(rows, qr, qi, sr_ref[prev, :], si_ref[prev, :], accr, acci)

        c = lax.fori_loop(0, nt - 1, step, (pw_r[...], pw_i[...], ac_r[...], ac_i[...]), unroll=7)
        qr, qi, accr, acci = fix(pl.ds(0, NSEG), *c[:2], br_ref[...], bi_ref[...], *c[2:])
        pw_r[...], pw_i[...], ac_r[...], ac_i[...] = qr, qi, accr, acci

        qrb, qib = q_r[...].astype(bf16), q_i[...].astype(bf16)
        nt_dims = (((1,), (1,)), ((), ()))
        tn_dims = (((0,), (0,)), ((), ()))
        du_ref[...] = (dua_ref[...] + lax.dot_general(qrb, wr_ref[...], nt_dims, preferred_element_type=f32)
                       + lax.dot_general(qib, wi_ref[...], nt_dims, preferred_element_type=f32))
        ub = u_ref[...].astype(bf16)
        dwr_ref[...] += lax.dot_general(ub, qrb, tn_dims, preferred_element_type=f32)
        dwi_ref[...] += lax.dot_general(ub, qib, tn_dims, preferred_element_type=f32)
        dyb = dy_ref[...].astype(bf16)
        dcr_ref[...] += lax.dot_general(sr_ref[...].astype(bf16), dyb, tn_dims, preferred_element_type=f32)
        dci_ref[...] -= lax.dot_general(si_ref[...].astype(bf16), dyb, tn_dims, preferred_element_type=f32)

        @pl.when(pl.program_id(1) == nT - 1)
        def _():
            dar_ref[...] = jnp.sum(accr, axis=0, keepdims=True)
            dai_ref[...] = jnp.sum(acci, axis=0, keepdims=True)

    blk = pl.BlockSpec((tc, sb), lambda j, i: (tmap(i), j))
    asp = pl.BlockSpec((1, sb), lambda j, i: (0, j))
    csp = pl.BlockSpec((NSEG, sb), lambda j, i: (0, j))
    bsp = pl.BlockSpec((None, NSEG, sb), lambda j, i: (tmap(i), 0, j))
    chn = pl.BlockSpec((tc, cb), lambda j, i: (tmap(i), j))
    wsp = pl.BlockSpec((None, cb, sb), lambda j, i: (j, 0, 0))
    wcs = pl.BlockSpec((None, sb, cb), lambda j, i: (j, 0, 0))
    return pl.pallas_call(
        body, name=name, grid=(nblk, nT), in_specs=[blk, blk, asp, asp, csp, csp, blk, blk, bsp, bsp, chn, chn, chn, wsp, wsp],
        out_specs=[chn, wsp, wsp, wcs, wcs, asp, asp],
        out_shape=[_S((T, nblk * cb), f32), _S((nblk, cb, sb), f32), _S((nblk, cb, sb), f32), _S((nblk, sb, cb), f32),
                   _S((nblk, sb, cb), f32), _S((1, NS), f32), _S((1, NS), f32)],
        scratch_shapes=[pltpu.VMEM((NSEG, sb), f32)] * 4 + [pltpu.VMEM((tc, sb), f32)] * 2,
        compiler_params=_cparams(("parallel", "arbitrary")),
    )(ql_r, ql_i, ab_r, ab_i, c_r, c_i, s_r, s_i, sb_r, sb_i, u, dy, du_add, w_r, w_i)


SSD2_TB = 512
_NN = (((1,), (0,)), ((), ()))
_NT = (((1,), (1,)), ((), ()))
_TN = (((0,), (0,)), ((), ()))


def _dotf(a, b, dims):
    return lax.dot_general(a.astype(bf16), b.astype(bf16), dims, preferred_element_type=f32)


def _doth(a, b, dims=_NN, sel="b", parts=3):
    x, m = (a, b) if sel == "b" else (b, a)
    m = m.astype(bf16)
    out = None
    for _ in range(parts):
        piece = x.astype(bf16)
        x = x - piece.astype(f32)
        d = lax.dot_general(*((piece, m) if sel == "b" else (m, piece)), dims, preferred_element_type=f32)
        out = d if out is None else out + d
    return out


def _ssd_consts(hpg):
    W = hpg * CHUNK
    i = lax.broadcasted_iota(jnp.int32, (CHUNK, CHUNK), 0)
    j = lax.broadcasted_iota(jnp.int32, (CHUNK, CHUNK), 1)
    tril = (i >= j).astype(f32)
    r = lax.broadcasted_iota(jnp.int32, (W, W), 0)
    c = lax.broadcasted_iota(jnp.int32, (W, W), 1)
    bd = (r // CHUNK == c // CHUNK).astype(f32)
    triu_bd = bd * (r <= c).astype(f32)
    e_r = lax.broadcasted_iota(jnp.int32, (W, LANES), 0)
    e_c = lax.broadcasted_iota(jnp.int32, (W, LANES), 1)
    ered = (e_r // HEADDIM == e_c).astype(f32)
    return tril, jnp.tile(tril, (1, hpg)), bd, triu_bd, ered


def _ssd2_specs(G, hpg, tb, tmap, b_off, c_off):
    W = hpg * HEADDIM
    ncb = tb // CHUNK
    xsp = pl.BlockSpec((tb, W), lambda g, i: (tmap(i), g))
    bsp = pl.BlockSpec((tb, SSD_STATE), lambda g, i: (tmap(i), b_off + g))
    csp = pl.BlockSpec((tb, SSD_STATE), lambda g, i: (tmap(i), c_off + g))
    rsp = pl.BlockSpec((None, ncb, W), lambda g, i: (g, tmap(i), 0))
    dsp = pl.BlockSpec((1, W), lambda g, i: (0, g))
    hsp = pl.BlockSpec((None, ncb, SSD_STATE, W), lambda g, i: (g, tmap(i), 0, 0))
    const = lambda a: pl.BlockSpec(a.shape, lambda g, i: (0, 0))
    return xsp, bsp, csp, rsp, dsp, hsp, const


def _tile_rows(a, n):
    return jnp.concatenate([a] * n, axis=0)


def _ssd2_fwd(xc, dt4, a4, dtw, aw, d4, consts, *, d_inner, name, carry=None):
    carry = carry or _NO_CARRY
    T = xc.shape[0]
    G, nc, W = dtw.shape
    hpg = W // CHUNK
    tb = min(SSD2_TB, T)
    nb, ncb = T // tb, tb // CHUNK
    b_off = d_inner // SSD_STATE
    xsp, bsp, csp, rsp, dsp, hsp, const = _ssd2_specs(G, hpg, tb, lambda i: i, b_off, b_off + G)
    tril, mask4, bd, triu_bd, _ = consts

    def body(*refs):
        own, c_in, c_out, c_sems = _carry_split(carry, refs, 12, 2)
        x_ref, b_ref, c_ref, dt_ref, a_ref, dtw_ref, aw_ref, d_ref, tril_ref, mask_ref, bd_ref, tbd_ref, y_ref, hs_ref, h_scr = own
        _carry_start(carry, c_in, c_out, c_sems, (pl.program_id(0) == 0) & (pl.program_id(1) == 0))

        @pl.when(pl.program_id(1) == 0)
        def _():
            h_scr[...] = jnp.zeros_like(h_scr)

        acs_rows = _doth(aw_ref[...], tbd_ref[...])
        ht = h_scr[...]
        for c in range(ncb):
            rows = slice(c * CHUNK, (c + 1) * CHUNK)
            x, bm, cm = x_ref[rows, :], b_ref[rows, :], c_ref[rows, :]
            acs = _doth(tril_ref[...], a_ref[rows, :], sel="a")
            lmat = jnp.where(mask_ref[...] > 0, jnp.exp(jnp.minimum(acs - acs_rows[c:c + 1, :], 0.0)), 0.0)
            m4 = _dotf(cm, _tile_rows(bm, hpg), _NT) * lmat * dtw_ref[c:c + 1, :]
            xbd = _tile_rows(x, hpg) * bd_ref[...]
            hs_ref[c] = ht
            y_ref[rows, :] = _dotf(m4, xbd, _NN) + _dotf(cm, ht, _NN) * jnp.exp(acs) + d_ref[...] * x
            a_last = acs[CHUNK - 1:CHUNK, :]
            xw = x * (jnp.exp(a_last - acs) * dt_ref[rows, :])
            ht = ht * jnp.exp(a_last) + _dotf(bm, xw, _TN)
        h_scr[...] = ht
        _carry_finish(carry, c_in, c_out, c_sems, (pl.program_id(0) == G - 1) & (pl.program_id(1) == nb - 1))

    res = pl.pallas_call(
        body, name=name, grid=(G, nb),
        in_specs=[xsp, bsp, csp, xsp, xsp, rsp, rsp, dsp, const(tril), const(mask4), const(bd), const(triu_bd)] + [_ANY] * len(carry.ins),
        out_specs=[xsp, hsp] + [_ANY] * len(carry.out_shapes),
        out_shape=[_S((T, G * W), f32), _S((G, nc, SSD_STATE, W), f32)] + list(carry.out_shapes),
        scratch_shapes=[pltpu.VMEM((SSD_STATE, W), f32)] + list(carry.sems),
        compiler_params=_cparams(("arbitrary", "arbitrary") if carry.ins else ("parallel", "arbitrary")),
    )(xc, xc, xc, dt4, a4, dtw, aw, d4, tril, mask4, bd, triu_bd, *carry.ins)
    return res[0], res[1], list(res[2:])


def _ssd2_bwd(xc, dt4, a4, dtw, aw, d4, consts, hs, dy, *, d_inner, name, carry=None):
    T = xc.shape[0]
    G, nc, W = dtw.shape
    hpg = W // CHUNK
    tb = min(SSD2_TB, T)
    nb, ncb = T // tb, tb // CHUNK
    b_off = d_inner // SSD_STATE
    tmap = lambda i: nb - 1 - i
    xsp, bsp, csp, rsp, dsp, hsp, const = _ssd2_specs(G, hpg, tb, tmap, b_off, b_off + G)
    gsp = pl.BlockSpec((tb, SSD_STATE), lambda g, i: (tmap(i), g))
    ddsp = pl.BlockSpec((None, 1, LANES), lambda g, i: (g, 0, 0))
    tril, mask4, bd, triu_bd, ered = consts

    carry = carry or _NO_CARRY

    def body(*refs):
        own, c_in, c_out, c_sems = _carry_split(carry, refs, 15, 8)
        (x_ref, b_ref, c_ref, dt_ref, a_ref, dtw_ref, aw_ref, d_ref, tril_ref, mask_ref, bd_ref, tbd_ref, er_ref, hs_ref, dy_ref,
         dx_ref, db_ref, dc_ref, ddtc_ref, dac_ref, ddtw_ref, daw_ref, dd_ref, g_scr, dd_scr, rw_scr, tl_scr) = own
        _carry_start(carry, c_in, c_out, c_sems, (pl.program_id(0) == 0) & (pl.program_id(1) == 0))
        first = pl.program_id(1) == 0

        @pl.when(first)
        def _():
            g_scr[...] = jnp.zeros_like(g_scr)
            dd_scr[...] = jnp.zeros_like(dd_scr)

        mask = mask_ref[...] > 0
        lane_in_block = lax.broadcasted_iota(jnp.int32, mask.shape, 1) & (CHUNK - 1)
        maskt = lax.broadcasted_iota(jnp.int32, mask.shape, 0) <= lane_in_block
        acs_rows = _doth(aw_ref[...], tbd_ref[...])
        dht = g_scr[...]
        dd = dd_scr[...]
        for c in range(ncb - 1, -1, -1):
            rows = slice(c * CHUNK, (c + 1) * CHUNK)
            x, bm, cm, dyc = x_ref[rows, :], b_ref[rows, :], c_ref[rows, :], dy_ref[rows, :]
            dtc, dtr = dt_ref[rows, :], dtw_ref[c:c + 1, :]
            ht = hs_ref[c]
            acs = _doth(tril_ref[...], a_ref[rows, :], sel="a")
            seg = acs - acs_rows[c:c + 1, :]
            lmat = jnp.where(mask, jnp.exp(jnp.minimum(seg, 0.0)), 0.0)
            lmat_t = jnp.where(maskt, jnp.exp(jnp.minimum(-seg, 0.0)), 0.0)
            btile, ctile = _tile_rows(bm, hpg), _tile_rows(cm, hpg)
            g4 = _dotf(cm, btile, _NT)
            gt4 = _dotf(bm, ctile, _NT)
            m4 = g4 * lmat * dtr
            mt4 = gt4 * lmat_t * dtc
            xbd = _tile_rows(x, hpg) * bd_ref[...]
            dybd = _tile_rows(dyc, hpg) * bd_ref[...]
            dm4 = _dotf(dyc, xbd, _NT)
            dmt4 = _dotf(x, dybd, _NT)
            dx = d_ref[...] * dyc + _dotf(mt4, dybd, _NN)
            dd = dd + jnp.sum(dyc * x, axis=0, keepdims=True)
            e4 = dm4 * m4
            dc = _dotf(dm4 * lmat * dtr, btile, _NN)
            db = _dotf(dmt4 * lmat_t * dtc, ctile, _NN)
            decay = jnp.exp(acs)
            yoff = _dotf(cm, ht, _NN) * decay
            dz = dyc * decay
            dc = dc + _dotf(dz, ht, _NT)
            dht_prev = _dotf(cm, dz, _TN)
            a_last = acs[CHUNK - 1:CHUNK, :]
            ea_last = jnp.exp(a_last)
            erel = jnp.exp(a_last - acs)
            dte = erel * dtc
            dxw = _dotf(bm, dht, _NN)
            db = db + _dotf(x * dte, dht, _NT)
            dx = dx + dxw * dte
            q4 = dxw * x
            dacs = e4 + dyc * yoff - q4 * dte
            col = jnp.concatenate([q4 * erel, _doth(tril_ref[...], dacs, _TN, sel="a")], axis=0)
            col = _doth(col, er_ref[...], parts=2)
            ddtc_ref[rows, :] = col[:CHUNK]
            dac_ref[rows, :] = col[CHUNK:]
            ddtw_ref[c:c + 1, :] = jnp.sum(dm4 * g4 * lmat, axis=0, keepdims=True)
            rw_scr[c:c + 1, :] = -jnp.sum(e4, axis=0, keepdims=True)
            tl_scr[c:c + 1, :] = jnp.sum(q4 * dte, axis=0, keepdims=True) + ea_last * jnp.sum(dht * ht, axis=0, keepdims=True)
            dx_ref[rows, :] = dx
            db_ref[rows, :] = db
            dc_ref[rows, :] = dc
            dht = dht_prev + dht * ea_last
        daw_ref[...] = _doth(rw_scr[...], tbd_ref[...], _NT) + _doth(tl_scr[...], bd_ref[...])
        g_scr[...] = dht
        dd_scr[...] = dd

        @pl.when(pl.program_id(1) == nb - 1)
        def _():
            dd_ref[...] = _doth(dd, er_ref[...])

        _carry_finish(carry, c_in, c_out, c_sems, (pl.program_id(0) == G - 1) & (pl.program_id(1) == nb - 1))

    res = pl.pallas_call(
        body, name=name, grid=(G, nb),
        in_specs=[xsp, bsp, csp, xsp, xsp, rsp, rsp, dsp, const(tril), const(mask4), const(bd), const(triu_bd), const(ered), hsp, xsp]
        + [_ANY] * len(carry.ins),
        out_specs=[xsp, gsp, gsp, gsp, gsp, rsp, rsp, ddsp] + [_ANY] * len(carry.out_shapes),
        out_shape=[_S((T, G * W), f32), _S((T, G * SSD_STATE), f32), _S((T, G * SSD_STATE), f32), _S((T, G * LANES), f32),
                   _S((T, G * LANES), f32), _S(dtw.shape, f32), _S(dtw.shape, f32), _S((G, 1, LANES), f32)] + list(carry.out_shapes),
        scratch_shapes=[pltpu.VMEM((SSD_STATE, W), f32), pltpu.VMEM((1, W), f32), pltpu.VMEM((ncb, W), f32), pltpu.VMEM((ncb, W), f32)]
        + list(carry.sems),
        compiler_params=_cparams(("arbitrary", "arbitrary") if carry.ins else ("parallel", "arbitrary")),
    )(xc, xc, xc, dt4, a4, dtw, aw, d4, tril, mask4, bd, triu_bd, ered, hs, dy, *carry.ins)
    return (*res[:8], list(res[8:]))


def _peers():
    x, y, c = lax.axis_index("x"), lax.axis_index("y"), lax.axis_index("c")
    return x, y, c


_ANY = pl.BlockSpec(memory_space=pl.ANY)
N_CHIP = N_DEV // 2


def _all_gather(shards, *, name):
    n = len(shards)
    carry = _carry_gather(shards)

    def body(*refs):
        x_refs, out_refs, sems = refs[:n], refs[n:2 * n], refs[2 * n:]
        _gather_start(x_refs, out_refs, sems)
        _gather_finish(x_refs, out_refs, sems)

    return pl.pallas_call(
        body, name=name, out_shape=list(carry.out_shapes), in_specs=[_ANY] * n, out_specs=[_ANY] * n, scratch_shapes=list(carry.sems),
    )(*shards)


def _gather_parts(x_refs, out_refs, sems):
    send_sems, recv_sems, local_sems = sems
    x, y, c = _peers()
    me, sibling = (x, y, c), (x, y, 1 - c)
    chips = [(1 - x, y), (x, 1 - y), (1 - x, 1 - y)]
    n = len(x_refs)

    def copy(a, r, block, to, src=None):
        px, py, pc = block
        slot = out_refs[a].at[4 * px + 2 * py + pc]
        return pltpu.make_async_remote_copy(
            src_ref=slot if src is None else src, dst_ref=slot, send_sem=send_sems.at[7 * a + r],
            recv_sem=recv_sems.at[7 * a + r], device_id=to, device_id_type=MESH)

    mine = [pltpu.make_async_copy(x_refs[a], out_refs[a].at[4 * x + 2 * y + c], local_sems.at[a]) for a in range(n)]
    first = []
    for a in range(n):
        first.append(copy(a, 0, me, sibling, src=x_refs[a]))
        first += [copy(a, 1 + j, me, (*chip, c), src=x_refs[a]) for j, chip in enumerate(chips)]
    return copy, mine, first, me, sibling, chips, c, n


def _gather_start(x_refs, out_refs, sems):
    _, mine, first, *_ = _gather_parts(x_refs, out_refs, sems)
    for cp in mine + first:
        cp.start()


def _gather_finish(x_refs, out_refs, sems):
    copy, mine, first, me, sibling, chips, c, n = _gather_parts(x_refs, out_refs, sems)
    passed = []
    for j, chip in enumerate(chips):
        for a in range(n):
            copy(a, 1 + j, (*chip, c), me).wait_recv()
            fwd = copy(a, 4 + j, (*chip, c), sibling)
            fwd.start()
            passed.append(fwd)
    for a in range(n):
        copy(a, 0, sibling, me).wait_recv()
    for j, chip in enumerate(chips):
        for a in range(n):
            copy(a, 4 + j, (*chip, 1 - c), me).wait_recv()
    for cp in first + passed:
        cp.wait_send()
    for cp in mine:
        cp.wait()


def _exchange_sibling(slots, *, name):
    n = len(slots)
    carry = _carry_sibling(slots)

    def body(*refs):
        carry.start(refs[:n], refs[n:2 * n], refs[2 * n:])
        carry.finish(refs[:n], refs[n:2 * n], refs[2 * n:])

    return list(pl.pallas_call(
        body, name=name, out_shape=list(carry.out_shapes), in_specs=[_ANY] * n, out_specs=[_ANY] * n, scratch_shapes=list(carry.sems),
    )(*slots))


def _sibling_copies(x_refs, sib_refs, send_sems, recv_sems):
    x, y, c = _peers()
    return [pltpu.make_async_remote_copy(
        src_ref=x_refs[a].at[pl.ds(N_CHIP * (1 - c), N_CHIP)], dst_ref=sib_refs[a], send_sem=send_sems.at[a],
        recv_sem=recv_sems.at[a], device_id=(x, y, 1 - c), device_id_type=MESH) for a in range(len(x_refs))]


def _carry_sibling(slots):
    n = len(slots)
    return _Carry(tuple(slots), tuple(_S((N_CHIP,) + s.shape[1:], s.dtype) for s in slots),
                  (pltpu.SemaphoreType.DMA((n,)), pltpu.SemaphoreType.DMA((n,))),
                  lambda i, o, s: _start_all(_sibling_copies(i, o, *s)), lambda i, o, s: _wait_all(_sibling_copies(i, o, *s)))


def _chip_sum(slots, sib, core, *, name):
    _, R, W = slots.shape
    tr = _pick(R, max(16, EW_BLOCK_ELEMS // W), align=16)

    def body(core_ref, x_ref, s_ref, o_ref):
        o_ref[...] = (x_ref[...].astype(f32) + s_ref[...].astype(f32)).astype(o_ref.dtype)

    blk = pl.BlockSpec((None, tr, W), lambda t, i, core_ref: (t, i, 0))
    return pl.pallas_call(
        body, name=name, out_shape=_S(sib.shape, slots.dtype),
        grid_spec=pltpu.PrefetchScalarGridSpec(
            num_scalar_prefetch=1, grid=(N_CHIP, R // tr),
            in_specs=[pl.BlockSpec((None, tr, W), lambda t, i, core_ref: (N_CHIP * core_ref[0] + t, i, 0)), blk], out_specs=blk),
        compiler_params=_cparams(("parallel", "parallel")),
    )(core, slots, sib)


def _chip_out_shapes(parts):
    return [_S((N_CHIP - 1,) + p.shape[1:], p.dtype) for p in parts]


def _chip_sems(n):
    return [pltpu.SemaphoreType.DMA((3 * n,)), pltpu.SemaphoreType.DMA((3 * n,))]


def _chip_copies(p_refs, out_refs, send_sems, recv_sems):
    x, y, c = _peers()
    copies = []
    for j in range(1, N_CHIP):
        tx, ty = x ^ (j >> 1), y ^ (j & 1)
        for a in range(len(p_refs)):
            copies.append(pltpu.make_async_remote_copy(
                src_ref=p_refs[a].at[2 * tx + ty], dst_ref=out_refs[a].at[j - 1], send_sem=send_sems.at[3 * a + j - 1],
                recv_sem=recv_sems.at[3 * a + j - 1], device_id=(tx, ty, c), device_id_type=MESH))
    return copies


def _start_all(copies):
    for cp in copies:
        cp.start()


def _wait_all(copies):
    for cp in copies:
        cp.wait_recv()
    for cp in copies:
        cp.wait_send()


class _Carry(NamedTuple):
    ins: tuple = ()
    out_shapes: tuple = ()
    sems: tuple = ()
    start: Callable = None
    finish: Callable = None


_NO_CARRY = _Carry()


def _carry_chips(parts):
    return _Carry(tuple(parts), tuple(_chip_out_shapes(parts)), tuple(_chip_sems(len(parts))),
                  lambda i, o, s: _start_all(_chip_copies(i, o, *s)), lambda i, o, s: _wait_all(_chip_copies(i, o, *s)))


def _carry_gather(shards):
    n = len(shards)
    sems = (pltpu.SemaphoreType.DMA((7 * n,)), pltpu.SemaphoreType.DMA((7 * n,)), pltpu.SemaphoreType.DMA((n,)))
    return _Carry(tuple(shards), tuple(_S((N_DEV,) + s.shape, s.dtype) for s in shards), sems, _gather_start, _gather_finish)


def _carry_split(carry, refs, n_in, n_out):
    ci, co, cs = len(carry.ins), len(carry.out_shapes), len(carry.sems)
    refs = list(refs)
    own_in, c_in = refs[:n_in], refs[n_in:n_in + ci]
    own_out, c_out = refs[n_in + ci:n_in + ci + n_out], refs[n_in + ci + n_out:n_in + ci + n_out + co]
    rest = refs[n_in + ci + n_out + co:]
    own_scratch, c_sems = rest[:len(rest) - cs], rest[len(rest) - cs:]
    return own_in + own_out + own_scratch, c_in, c_out, c_sems


def _carry_start(carry, c_in, c_out, c_sems, first):
    if carry.ins:
        @pl.when(first)
        def _():
            carry.start(c_in, c_out, c_sems)


def _carry_finish(carry, c_in, c_out, c_sems, last):
    if carry.ins:
        @pl.when(last)
        def _():
            carry.finish(c_in, c_out, c_sems)


def _sum_slots(stack, *, name):
    n, R, W = stack.shape
    tr = _pick(R, 1024, align=8)

    def body(s_ref, o_ref):
        acc = s_ref[0]
        for k in range(1, n):
            acc = acc + s_ref[k]
        o_ref[...] = acc

    return pl.pallas_call(
        body, name=name, grid=(R // tr,), in_specs=[pl.BlockSpec((n, tr, W), lambda i: (0, i, 0))],
        out_specs=pl.BlockSpec((tr, W), lambda i: (i, 0)), out_shape=_S((R, W), f32), compiler_params=_cparams(("parallel",)),
    )(stack)


def _adamw_math(gv, wv, mv, vv):
    c1 = 1.0 / (1.0 - ADAM_B1 ** ADAM_STEP)
    c2 = 1.0 / (1.0 - ADAM_B2 ** ADAM_STEP)
    nm = ADAM_B1 * mv + (1.0 - ADAM_B1) * gv
    nv = ADAM_B2 * vv + (1.0 - ADAM_B2) * jnp.square(gv)
    return -ADAM_LR * ((nm * c1) / (jnp.sqrt(nv * c2) + ADAM_EPS) + ADAM_WD * wv), nm, nv


def _adamw(g, w, m, v, *, name):
    R, W = w.shape
    tr = _pick(R, max(8, (1 << 20) // (4 * W)), align=8)

    def body(g_ref, w_ref, m_ref, v_ref, d_ref, nm_ref, nv_ref):
        d_ref[...], nm_ref[...], nv_ref[...] = _adamw_math(g_ref[...], w_ref[...], m_ref[...], v_ref[...])

    sp = pl.BlockSpec((tr, W), lambda i: (i, 0))
    return pl.pallas_call(
        body, name=name, grid=(R // tr,), in_specs=[sp] * 4, out_specs=[sp] * 3, out_shape=[_S((R, W), f32)] * 3,
        compiler_params=_cparams(("parallel",)),
    )(g, w, m, v)


def _reduce_adamw(own, arrived, chip, w, m, v, *, name):
    n, R, W = arrived.shape
    tr = _pick(R, max(16, EW_BLOCK_ELEMS // (4 * W)), align=16)

    def body(chip_ref, o_ref, p_ref, w_ref, m_ref, v_ref, g_ref, d_ref, nm_ref, nv_ref):
        gv = o_ref[...].astype(f32)
        for k in range(n):
            gv = gv + p_ref[k].astype(f32)
        g_ref[...] = gv
        d_ref[...], nm_ref[...], nv_ref[...] = _adamw_math(gv, w_ref[...], m_ref[...], v_ref[...])

    sp = pl.BlockSpec((tr, W), lambda i, chip_ref: (i, 0))
    return pl.pallas_call(
        body, name=name, out_shape=[_S((R, W), f32)] * 4,
        grid_spec=pltpu.PrefetchScalarGridSpec(
            num_scalar_prefetch=1, grid=(R // tr,),
            in_specs=[pl.BlockSpec((None, tr, W), lambda i, chip_ref: (chip_ref[0], i, 0)),
                      pl.BlockSpec((n, tr, W), lambda i, chip_ref: (0, i, 0))] + [sp] * 3, out_specs=[sp] * 4),
        compiler_params=_cparams(("parallel",)),
    )(chip, own, arrived, w, m, v)


def _pieces(seg_start, seg_len, shard_w):
    out, col = [], seg_start
    while col < seg_start + seg_len:
        k, a = divmod(col, shard_w)
        n = min(shard_w - a, seg_start + seg_len - col)
        out.append((k, a, col - seg_start, n))
        col += n
    return out


def _unshard_w_in(g, seg_lens, *, name):
    _, D, w = g.shape
    starts = [sum(seg_lens[:i]) for i in range(len(seg_lens))]
    widths = [max(n, LANES) for n in seg_lens]
    tm = _pick(D, 256, align=16)

    def body(g_ref, *o_refs):
        for o_ref, s0, n in zip(o_refs, starts, seg_lens):
            if n < o_ref.shape[1]:
                o_ref[...] = jnp.zeros_like(o_ref)
            for k, a, off, m in _pieces(s0, n, w):
                o_ref[:, off:off + m] = g_ref[k, :, a:a + m]

    return pl.pallas_call(
        body, name=name, grid=(D // tm,), in_specs=[pl.BlockSpec((N_DEV, tm, w), lambda i: (0, i, 0))],
        out_specs=[pl.BlockSpec((tm, wd), lambda i: (i, 0)) for wd in widths], out_shape=[_S((D, wd), g.dtype) for wd in widths],
        compiler_params=_cparams(("parallel",)),
    )(g)


def _unshard_pair(g1, g2, *, name):
    _, D, w = g1.shape
    tm = _pick(D, 256, align=16)

    def body(a_ref, b_ref, o_ref):
        for i, g_ref in enumerate((a_ref, b_ref)):
            for k in range(N_DEV):
                off = (i * N_DEV + k) * w
                o_ref[:, off:off + w] = g_ref[k]

    blk = pl.BlockSpec((N_DEV, tm, w), lambda i: (0, i, 0))
    return pl.pallas_call(
        body, name=name, grid=(D // tm,), in_specs=[blk, blk], out_specs=pl.BlockSpec((tm, 2 * N_DEV * w), lambda i: (i, 0)),
        out_shape=_S((D, 2 * N_DEV * w), g1.dtype), compiler_params=_cparams(("parallel",)),
    )(g1, g2)


def _reshard_pair(dw, *, name):
    D, w = dw.shape[0], dw.shape[1] // (2 * N_DEV)
    tm = _pick(D, 128, align=16)

    def body(g_ref, a_ref, b_ref):
        for i, o_ref in enumerate((a_ref, b_ref)):
            for k in range(N_DEV):
                off = (i * N_DEV + k) * w
                o_ref[_slot_of(k)] = g_ref[:, off:off + w].astype(o_ref.dtype)

    blk = pl.BlockSpec((N_DEV, tm, w), lambda i: (0, i, 0))
    return pl.pallas_call(
        body, name=name, grid=(D // tm,), in_specs=[pl.BlockSpec((tm, dw.shape[1]), lambda i: (i, 0))], out_specs=[blk, blk],
        out_shape=[_S((N_DEV, D, w), bf16)] * 2, compiler_params=_cparams(("parallel",)),
    )(dw)


def _reshard_w_in(grads, seg_lens, w, *, name):
    D = grads[0].shape[0]
    starts = [sum(seg_lens[:i]) for i in range(len(seg_lens))]
    tm = _pick(D, 128, align=16)

    def body(*refs):
        o_ref = refs[-1]
        for g_ref, s0, n in zip(refs[:-1], starts, seg_lens):
            for k, a, off, m in _pieces(s0, n, w):
                o_ref[_slot_of(k), :, a:a + m] = g_ref[:, off:off + m].astype(o_ref.dtype)

    return pl.pallas_call(
        body, name=name, grid=(D // tm,), in_specs=[pl.BlockSpec((tm, g.shape[1]), lambda i: (i, 0)) for g in grads],
        out_specs=pl.BlockSpec((N_DEV, tm, w), lambda i: (0, i, 0)), out_shape=_S((N_DEV, D, w), bf16),
        compiler_params=_cparams(("parallel",)),
    )(*grads)


def _pad_flat(a, mult):
    a = a.reshape(-1)
    n = -(-a.shape[0] // mult) * mult
    return a if n == a.shape[0] else jnp.pad(a, (0, n - a.shape[0]))


def _pad_cols(a, mult):
    n = -(-a.shape[1] // mult) * mult
    return a if n == a.shape[1] else jnp.pad(a, ((0, 0), (0, n - a.shape[1])))


def _block_diag(t):
    nblk, g, P, Q = t.shape
    eye = jnp.eye(g, dtype=t.dtype)
    return (t[:, :, :, None, :] * eye[None, :, None, :, None]).reshape(nblk, g * P, g * Q)


def _block_diag_t(w, P, Q):
    nblk = w.shape[0]
    g = w.shape[1] // P
    eye = jnp.eye(g, dtype=w.dtype)
    return (w.reshape(nblk, g, P, g, Q) * eye[None, :, None, :, None]).sum(axis=3)


_COLS = ("ffn1_w_gate", "ffn1_w_up", "ffn2_w_gate", "ffn2_w_up")
_ROWS = ("ffn1_w_down", "ffn2_w_down", "s5_w_glu", "w_proj_s5", "w_out", "w_proj_ssd")
_BIG = _COLS + _ROWS + ("w_in", "conv_w")
_SMALL = ("ffn1_norm", "mix_norm", "conv_b", "s5_A_re", "s5_A_im", "s5_log_dt", "s5_B_re", "s5_B_im", "s5_C_re", "s5_C_im",
          "s5_D", "s5_b_glu", "ssd_A_log", "ssd_dt_bias", "ssd_D", "ssd_norm", "b_gate", "ffn2_norm", "final_norm")
_WEIGHTS = ("ffn1_norm", "ffn1_w_gate", "ffn1_w_up", "ffn1_w_down", "mix_norm", "w_in", "conv_w", "conv_b", "s5_A_re", "s5_A_im",
            "s5_log_dt", "s5_B_re", "s5_B_im", "s5_C_re", "s5_C_im", "s5_D", "s5_w_glu", "s5_b_glu", "ssd_A_log", "ssd_dt_bias",
            "ssd_D", "ssd_norm", "w_proj_s5", "w_proj_ssd", "b_gate", "w_out", "ffn2_norm", "ffn2_w_gate", "ffn2_w_up",
            "ffn2_w_down", "final_norm")
def _with_carry(res, carry):
    return res if carry else (res, [])


def _ffn_fwd(x, n, wgu, wd_of, tag, carries=(None, None, None), h=None):
    D = x.shape[1]
    if h is None:
        h = _rows(_f_rmsnorm, [x], [n], [(D, bf16)], name=tag + "_norm")[0]
    ab, got0 = _with_carry(_mm(h, wgu, carry=carries[0], name=tag + "_gate_up"), carries[0])
    wd = wd_of(got0)
    F = wd.shape[0]
    (c,), got1 = _with_carry(_rows(_f_swiglu, [ab], [], [(F, bf16)], carry=carries[1], name=tag + "_act"), carries[1])
    y, got2 = _with_carry(_mm(c, wd, scale=0.5, add=x, carry=carries[2], name=tag + "_down"), carries[2])
    return y, (x, n, h, ab, c), (got0, got1, got2)


def _ffn_bwd(saved, wgu, wd, dy, tag, carry_after_dwd=None, carry_after_dwgu=None):
    x, n, h, ab, c = saved
    F = wd.shape[0]
    def act_bwd(dc, ab_):
        return jax.vjp(lambda t: _f_swiglu(t)[0], ab_)[1](dc)[0]

    dab = _mm(dy, wd, tb=True, scale=0.5, tm=EPI_TM, epi=(act_bwd, ab, 2 * F), out_dtype=bf16, name=tag + "_d_act")
    dwd = _mm(c, dy, ta=True, o_blk="m", o_slots=True, tm=F // 2, out_dtype=bf16, scale=0.5, name=tag + "_d_wdown")
    carry_w = carry_after_dwd(dwd) if carry_after_dwd else None
    dwgu, arr_w = _with_carry(_mm(h, dab, ta=True, carry=carry_w, name=tag + "_d_wgu"), carry_w)
    dwg, dwu = _reshard_pair(dwgu, name=tag + "_reshard_d_wgu")
    carry_h = carry_after_dwgu(dwg, dwu) if carry_after_dwgu else None
    dh, arr_h = _with_carry(_mm(dab, wgu, tb=True, carry=carry_h, name=tag + "_d_h"), carry_h)
    (dx,), (dn,) = _rows_bwd(_f_rmsnorm, [x], [n], [dh], name=tag + "_norm_bwd", want_rows=[0], adds={0: dy})
    return dx, dn, dwg, dwu, dwd, arr_w, arr_h


def kernel(x, ffn1_norm, ffn1_w_gate, ffn1_w_up, ffn1_w_down, mix_norm, w_in, conv_w, conv_b, s5_A_re, s5_A_im, s5_log_dt, s5_B_re, s5_B_im, s5_C_re, s5_C_im, s5_D, s5_w_glu, s5_b_glu, ssd_A_log, ssd_dt_bias, ssd_D, ssd_norm, w_proj_s5, w_proj_ssd, b_gate, w_out, ffn2_norm, ffn2_w_gate, ffn2_w_up, ffn2_w_down, final_norm, loss_target, m_ffn1_norm, m_ffn1_w_gate, m_ffn1_w_up, m_ffn1_w_down, m_mix_norm, m_w_in, m_conv_w, m_conv_b, m_s5_A_re, m_s5_A_im, m_s5_log_dt, m_s5_B_re, m_s5_B_im, m_s5_C_re, m_s5_C_im, m_s5_D, m_s5_w_glu, m_s5_b_glu, m_ssd_A_log, m_ssd_dt_bias, m_ssd_D, m_ssd_norm, m_w_proj_s5, m_w_proj_ssd, m_b_gate, m_w_out, m_ffn2_norm, m_ffn2_w_gate, m_ffn2_w_up, m_ffn2_w_down, m_final_norm, v_ffn1_norm, v_ffn1_w_gate, v_ffn1_w_up, v_ffn1_w_down, v_mix_norm, v_w_in, v_conv_w, v_conv_b, v_s5_A_re, v_s5_A_im, v_s5_log_dt, v_s5_B_re, v_s5_B_im, v_s5_C_re, v_s5_C_im, v_s5_D, v_s5_w_glu, v_s5_b_glu, v_ssd_A_log, v_ssd_dt_bias, v_ssd_D, v_ssd_norm, v_w_proj_s5, v_w_proj_ssd, v_b_gate, v_w_out, v_ffn2_norm, v_ffn2_w_gate, v_ffn2_w_up, v_ffn2_w_down, v_final_norm):
    P = dict(locals())
    T, D = x.shape[1], x.shape[2]
    x0, tgt = x[0], loss_target[0]
    sh = {k: P[k][0] for k in _BIG}

    send = {k: (sh[k] if k == "conv_w" else sh[k].astype(bf16)) for k in _BIG}
    W = {}

    def gather_in(keys):
        return _carry_gather([send[k] for k in keys])

    first_keys = ("ffn1_w_gate", "ffn1_w_up", "conv_w")
    (h1,), got = _rows(_f_rmsnorm, [x0], [ffn1_norm], [(D, bf16)], carry=gather_in(first_keys), name="ffn1_norm")
    W.update(zip(first_keys, got))
    whole = lambda k: W[k].reshape(-1, D)
    conv_w_full = W["conv_w"].transpose(1, 0, 2).reshape(CONV_K, -1)

    d_inner = N_DEV * sh["w_proj_ssd"].shape[0]
    conv_dim = conv_w_full.shape[1]
    H = ssd_A_log.shape[1]
    G = (conv_dim - d_inner) // (2 * SSD_STATE)
    hpg = H // G
    nc = T // CHUNK
    Gs = D // S5_GROUP
    nblk = Gs // S5_GPB
    NS = Gs * S5_STATE
    seg_lens = (D, d_inner, conv_dim, H, 2 * D)

    cuts = [0, D // 3 // 16 * 16, D // 3 // 16 * 16 + 3 * D // 8 // 16 * 16, D]
    win_rows = [send["w_in"][a_:b_] for a_, b_ in zip(cuts[:-1], cuts[1:])]
    wgu1 = _unshard_pair(W["ffn1_w_gate"], W["ffn1_w_up"], name="unshard_ffn1_gate_up")

    def ffn1_down(got):
        W["ffn1_w_down"] = got[0]
        return whole("ffn1_w_down")

    x1, sv1, (got0, got1, got2) = _ffn_fwd(
        x0, ffn1_norm, wgu1, ffn1_down, "ffn1", h=h1,
        carries=(_carry_gather([send["ffn1_w_down"], win_rows[0]]), _carry_gather([win_rows[1]]), _carry_gather([win_rows[2]])))
    ffn1_w = (wgu1, whole("ffn1_w_down"))
    W["w_in"] = jnp.concatenate([got0[1], got1[0], got2[0]], axis=1)
    w_u, w_z, w_xbc, w_dt, w_gl = _unshard_w_in(W["w_in"], seg_lens, name="unshard_w_in")
    h2 = _rows(_f_rmsnorm, [x1], [mix_norm], [(D, bf16)], name="mix_norm")[0]
    u_p = _mm(h2, w_u, o_seg=True, name="in_u")
    z = _mm(h2, w_z, name="in_z")
    xbc = _mm(h2, w_xbc, name="in_xbc")
    gl = _mm(h2, w_gl, name="in_gate")
    dtr = _mm(h2, w_dt, name="in_dt")

    rep = lambda a: jnp.repeat(a, S5_GROUP, axis=0)
    lr, li, ldt = s5_A_re[0], s5_A_im[0], s5_log_dt[0].reshape(Gs, 1)
    brt = s5_B_re[0].transpose(0, 2, 1).reshape(Gs * S5_GROUP, S5_STATE)
    bit = s5_B_im[0].transpose(0, 2, 1).reshape(Gs * S5_GROUP, S5_STATE)
    prep_args = (lr, li, ldt, rep(lr), rep(li), rep(ldt), brt, bit)
    ar, ai, bbrt, bbit = _s5_prep(prep_args, name="s5_prep")
    a_r, a_i = ar.reshape(1, NS), ai.reshape(1, NS)
    wb_r = _block_diag(bbrt.reshape(nblk, S5_GPB, S5_GROUP, S5_STATE)).astype(bf16)
    wb_i = _block_diag(bbit.reshape(nblk, S5_GPB, S5_GROUP, S5_STATE)).astype(bf16)
    c4r = s5_C_re[0].reshape(nblk, S5_GPB, S5_GROUP, S5_STATE).transpose(0, 1, 3, 2)
    c4i = s5_C_im[0].reshape(nblk, S5_GPB, S5_GROUP, S5_STATE).transpose(0, 1, 3, 2)
    wc_r, wc_i = _block_diag(c4r).astype(bf16), _block_diag(c4i).astype(bf16)
    mix_keys = ("s5_w_glu", "w_proj_s5", "w_proj_ssd", "w_out")
    sl_r, sl_i, p_r, p_i, got = _s5_local_scan(u_p, wb_r, wb_i, a_r, a_i, reverse=False, carry=gather_in(mix_keys), name="s5_scan")
    W.update(zip(mix_keys, got))
    w_glu, w_p5, w_pssd, w_o = whole("s5_w_glu"), whole("w_proj_s5"), whole("w_proj_ssd"), whole("w_out")
    c_r, c_i = _s5_carry(sl_r[T - NSEG:], sl_i[T - NSEG:], p_r, p_i, reverse=False, name="s5_carry")
    s_r, s_i, ylin = _s5_fix_out(sl_r, sl_i, a_r, a_i, c_r, c_i, wc_r, wc_i, name="s5_fix_out")
    g5 = _rows(_f_s5_post, [ylin, u_p], [s5_D], [(D, f32)], name="s5_gelu")[0]
    v5 = _mm(g5, w_glu, name="s5_glu_mm")
    o5 = _rows(_f_glu, [g5, v5], [s5_b_glu], [(D, bf16)], name="s5_glu")[0]
    p5 = _mm(o5, w_p5, a_seg=True, name="proj_s5")

    xc = _conv_fwd(xbc, conv_w_full, conv_b, name="conv")
    bias_p, alog_p = _pad_cols(ssd_dt_bias, LANES), _pad_cols(ssd_A_log, LANES)
    expand = (lax.broadcasted_iota(jnp.int32, (LANES, d_inner), 1) // HEADDIM
              == lax.broadcasted_iota(jnp.int32, (LANES, d_inner), 0)).astype(f32)
    dt_p, da_p, dt4, a4 = _rows(_f_dt_expand, [dtr], [bias_p, alog_p, expand],
                                [(LANES, f32), (LANES, f32), (d_inner, f32), (d_inner, f32)], name="ssd_dt")
    row_l = lambda a: a[:, :H].reshape(nc, CHUNK, G, hpg).transpose(2, 0, 3, 1).reshape(G, nc, hpg * CHUNK)
    ssd_in = (xc, dt4, a4, row_l(dt_p), row_l(da_p), jnp.repeat(ssd_D, HEADDIM, axis=1), _ssd_consts(hpg))
    ffn2_keys = ("ffn2_w_gate", "ffn2_w_up", "ffn2_w_down")
    y_ssd, hs, got = _ssd2_fwd(*ssd_in, d_inner=d_inner, carry=gather_in(ffn2_keys), name="ssd")
    W.update(zip(ffn2_keys, got))
    ffn2_w = (_unshard_pair(W["ffn2_w_gate"], W["ffn2_w_up"], name="unshard_ffn2_gate_up"), whole("ffn2_w_down"))
    yn = _rows(_f_gated_norm, [y_ssd, z], [ssd_norm], [(d_inner, bf16)], name="ssd_gated_norm")[0]
    pssd = _mm(yn, w_pssd, name="proj_ssd")

    merged = _rows(_f_merge, [gl, p5, pssd], [b_gate], [(D, bf16)], name="merge")[0]
    x2 = _mm(merged, w_o, add=x1, name="out_proj")
    x3, sv2, _ = _ffn_fwd(x2, ffn2_norm, ffn2_w[0], lambda _: ffn2_w[1], "ffn2")
    lossv, dx3, d_final = _loss_stage(x3, tgt, final_norm.reshape(1, D), name="loss")

    gw = {}
    gs = {"final_norm": d_final}
    slot_mm = lambda a_, b_, name, **kw: _mm(a_, b_, ta=True, o_blk="m", o_slots=True, out_dtype=bf16, name=name, **kw)
    core = lax.axis_index("c").astype(jnp.int32).reshape(1)
    chip = (2 * lax.axis_index("x") + lax.axis_index("y")).astype(jnp.int32).reshape(1)
    chip_sums, arrived = {}, {}

    def sums_of(keys, sib):
        for k, s_ in zip(keys, sib):
            chip_sums[k] = _chip_sum(gw[k], s_, core, name="chip_sum_" + k)
        return [chip_sums[k] for k in keys]

    def level1(keys, tag):
        return sums_of(keys, _exchange_sibling([gw[k] for k in keys], name="exchange_sibling_" + tag))

    dx2, gs["ffn2_norm"], gw["ffn2_w_gate"], gw["ffn2_w_up"], gw["ffn2_w_down"], _, _ = _ffn_bwd(sv2, *ffn2_w, dx3, "ffn2")

    dmerged = _mm(dx2, w_o, tb=True, name="d_merged")
    gw["w_out"] = slot_mm(merged, dx2, "d_w_out")
    (dgl, dp5, dpssd), (gs["b_gate"],) = _rows_bwd(_f_merge, [gl, p5, pssd], [b_gate], [dmerged], name="merge_bwd", want_rows=[0, 1, 2])

    dyn = _mm(dpssd, w_pssd, tb=True, name="d_yn")
    gw["w_proj_ssd"] = slot_mm(yn, dpssd, "d_w_proj_ssd")
    group_a = ("ffn2_w_gate", "ffn2_w_up", "ffn2_w_down", "w_out", "w_proj_ssd")
    (dyssd, dz), (gs["ssd_norm"],) = _rows_bwd(_f_gated_norm, [y_ssd, z], [ssd_norm], [dyn], name="ssd_gated_norm_bwd", want_rows=[0, 1])
    dxs, dbm, dcm, ddtc, ddac, ddtw, ddaw, ddh, sib_a = _ssd2_bwd(
        *ssd_in, hs, dyssd, d_inner=d_inner, carry=_carry_sibling([gw[k] for k in group_a]), name="ssd_bwd")
    parts_a = sums_of(group_a, sib_a)

    def fold(col, row):
        col = col.reshape(T, G, LANES)[:, :, :hpg].reshape(T, H)
        row = row.reshape(G, nc, hpg, CHUNK).transpose(1, 3, 0, 2).reshape(T, H)
        return _pad_cols(col + row, LANES)

    (ddtr,), (dbias_p, dalog_p) = _rows_bwd(_f_dt, [dtr], [bias_p, alog_p], [fold(ddtc, ddtw), fold(ddac, ddaw)], name="ssd_dt_bwd", want_rows=[0])
    gs["ssd_dt_bias"], gs["ssd_A_log"], gs["ssd_D"] = dbias_p[:, :H], dalog_p[:, :H], ddh[:, 0, :hpg].reshape(1, H)
    dxbc, d_conv_w, gs["conv_b"], arr = _conv_bwd(
        xbc, conv_w_full, conv_b, [dxs, dbm, dcm], carry=_carry_chips(parts_a), name="conv_bwd")
    arrived.update(zip(group_a, arr))
    cwk = sh["conv_w"].shape[1]
    gw["conv_w"] = d_conv_w.reshape(CONV_K, N_CHIP, 2, cwk).transpose(2, 1, 0, 3).reshape(N_DEV, CONV_K, cwk)

    do5 = _mm(dp5, w_p5, tb=True, o_seg=True, name="d_o5")
    gw["w_proj_s5"] = slot_mm(o5, dp5, "d_w_proj_s5", a_seg=True)
    (dg5a, dv5), (gs["s5_b_glu"],) = _rows_bwd(_f_glu, [g5, v5], [s5_b_glu], [do5], name="s5_glu_bwd", want_rows=[0, 1])
    dg5 = _mm(dv5, w_glu, tb=True, add=dg5a, name="d_g5")
    gw["s5_w_glu"] = slot_mm(g5, dv5, "d_w_glu")
    (dylin, du_a), (gs["s5_D"],) = _rows_bwd(_f_s5_post, [ylin, u_p], [s5_D], [dg5], name="s5_gelu_bwd", want_rows=[0, 1])
    wct_r, wct_i = wc_r.transpose(0, 2, 1), -wc_i.transpose(0, 2, 1)
    ql_r, ql_i, _, _ = _s5_local_scan(dylin, wct_r, wct_i, a_r, -a_i, reverse=True, powers=False, name="s5_scan_bwd")
    cb_r, cb_i = _s5_carry(ql_r[:NSEG], ql_i[:NSEG], p_r, -p_i, reverse=True, name="s5_carry_bwd")
    tc = min(S5_TC, T)

    def before_blocks(s):
        last = s.reshape(T // tc, tc, NS)[:, tc - NSEG:, :]
        wrap = jnp.concatenate([jnp.zeros((1, 1, NS), f32), last[-1:, : NSEG - 1, :]], axis=1)
        return jnp.concatenate([wrap, last[:-1]], axis=0)

    du_p, dwb_r, dwb_i, dwc_r, dwc_i, d_ar, d_ai = _s5_fix_bwd(
        ql_r, ql_i, a_r, -a_i, cb_r, cb_i, s_r, s_i, before_blocks(s_r), before_blocks(s_i), u_p, dylin, du_a, wb_r, wb_i, name="s5_fix_bwd")
    unblk = lambda w: _block_diag_t(w, S5_GROUP, S5_STATE).reshape(Gs * S5_GROUP, S5_STATE)
    rsum = jnp.repeat(jnp.eye(Gs, dtype=f32), S5_GROUP, axis=1)
    d_lr, d_li, d_ldt, d_brt, d_bit = _s5_prep_bwd(
        prep_args, (d_ar.reshape(Gs, S5_STATE), d_ai.reshape(Gs, S5_STATE), unblk(dwb_r), unblk(dwb_i)), rsum, name="s5_prep_bwd")
    gs["s5_A_re"], gs["s5_A_im"], gs["s5_log_dt"] = d_lr, d_li, d_ldt.reshape(1, Gs)
    gs["s5_B_re"] = d_brt.reshape(Gs, S5_GROUP, S5_STATE).transpose(0, 2, 1)
    gs["s5_B_im"] = d_bit.reshape(Gs, S5_GROUP, S5_STATE).transpose(0, 2, 1)
    gs["s5_C_re"] = _block_diag_t(dwc_r, S5_STATE, S5_GROUP).transpose(0, 1, 3, 2).reshape(Gs, S5_GROUP, S5_STATE)
    gs["s5_C_im"] = _block_diag_t(dwc_i, S5_STATE, S5_GROUP).transpose(0, 1, 3, 2).reshape(Gs, S5_GROUP, S5_STATE)

    d_w_in = [_mm(h2, du_p, ta=True, b_seg=True, name="d_w_u"), _mm(h2, dz, ta=True, name="d_w_z"), _mm(h2, dxbc, ta=True, name="d_w_xbc"),
              _mm(h2, ddtr, ta=True, name="d_w_dt"), _mm(h2, dgl, ta=True, name="d_w_gate")]
    gw["w_in"] = _reshard_w_in(d_w_in, seg_lens, sh["w_in"].shape[1], name="reshard_d_w_in")
    group_c = ("w_proj_s5", "s5_w_glu", "conv_w")
    keys_c = group_c + ("w_in",)
    dh2, sib_c = _mm(du_p, w_u, tb=True, a_seg=True, carry=_carry_sibling([gw[k] for k in keys_c]), name="d_h2_u")
    parts_c = sums_of(keys_c, sib_c)
    c1 = int(D * 0.45) // 16 * 16
    c2 = c1 + D // 4 // 16 * 16
    win = [parts_c[3][:, :c1], parts_c[3][:, c1:c2], parts_c[3][:, c2:]]
    dh2, arr = _mm(dz, w_z, tb=True, add=dh2, carry=_carry_chips(parts_c[:3]), name="d_h2_z")
    arrived.update(zip(group_c, arr))
    dh2, (arr0,) = _mm(dxbc, w_xbc, tb=True, add=dh2, carry=_carry_chips([win[0]]), name="d_h2_xbc")
    dh2, (arr1,) = _mm(dgl, w_gl, tb=True, add=dh2, carry=_carry_chips([win[1]]), name="d_h2_gate")
    dh2 = _mm(ddtr, w_dt, tb=True, add=dh2, name="d_h2_dt")
    (dx1,), (gs["mix_norm"],) = _rows_bwd(_f_rmsnorm, [x1], [mix_norm], [dh2], name="mix_norm_bwd", want_rows=[0], adds={0: dx2})

    def carry_ffn1_down(dwd):
        gw["ffn1_w_down"] = dwd
        return _carry_chips(level1(("ffn1_w_down",), "d") + [win[2]])

    def carry_ffn1_gate_up(dwg, dwu):
        gw["ffn1_w_gate"], gw["ffn1_w_up"] = dwg, dwu
        return _carry_chips(level1(("ffn1_w_gate", "ffn1_w_up"), "e"))

    dx0, gs["ffn1_norm"], _, _, _, arr_w, arr_h = _ffn_bwd(
        sv1, *ffn1_w, dx1, "ffn1", carry_after_dwd=carry_ffn1_down, carry_after_dwgu=carry_ffn1_gate_up)
    arrived["ffn1_w_down"], arr2 = arr_w
    arrived["ffn1_w_gate"], arrived["ffn1_w_up"] = arr_h
    arrived["w_in"] = jnp.concatenate([arr0, arr1, arr2], axis=1)

    small_shapes = {k: (P[k][0].shape if P[k].ndim > 1 else P[k].shape) for k in _SMALL}
    pack = lambda d: jnp.concatenate([_pad_flat(d[k], TILE_ELEMS) for k in _SMALL]).reshape(-1, LANES)
    gsmall = _sum_slots(_all_gather([pack(gs)], name="gather_small_grads")[0], name="sum_small_grads")
    snum = {k: math.prod(small_shapes[k]) for k in _SMALL}
    ssz = {k: -(-snum[k] // TILE_ELEMS) * TILE_ELEMS for k in _SMALL}

    grads, delta, new_m, new_v = {}, {}, {}, {}
    for k in _BIG:
        grads[k], delta[k], new_m[k], new_v[k] = _reduce_adamw(
            chip_sums[k], arrived[k], chip, P[k][0], P["m_" + k][0], P["v_" + k][0], name="adamw_" + k)
    d_s, m_s, v_s = _adamw(gsmall, pack({k: P[k] for k in _SMALL}), pack({k: P["m_" + k] for k in _SMALL}),
                           pack({k: P["v_" + k] for k in _SMALL}), name="adamw_small")
    off = 0
    gflat, dflat, mflat, vflat = gsmall.reshape(-1), d_s.reshape(-1), m_s.reshape(-1), v_s.reshape(-1)
    for k in _SMALL:
        n = snum[k]
        grads[k], delta[k], new_m[k], new_v[k] = (a[off:off + n] for a in (gflat, dflat, mflat, vflat))
        off += ssz[k]

    loss = lax.psum(lossv[0, 0], ("x", "y", "c"))
    out = [loss, dx0.reshape(x.shape)]
    for d in (grads, delta, new_m, new_v):
        out += [d[k].reshape(P[k].shape) for k in _WEIGHTS]
    return tuple(out)
```

```python
import math
from typing import Callable, NamedTuple

import jax
import jax.numpy as jnp
from jax import lax
from jax.experimental import pallas as pl
from jax.experimental.pallas import tpu as pltpu

f32 = jnp.float32
bf16 = jnp.bfloat16
_S = jax.ShapeDtypeStruct

EPS = 1e-6
S5_GROUP = 16
S5_STATE = 64
HEADDIM = 64
SSD_STATE = 128
CHUNK = 64
CONV_K = 4
NSEG = 8
S5_GPB = 16
N_DEV = 8
LANES = 128
TILE_ELEMS = 8 * LANES

ADAM_LR = 0.001
ADAM_B1 = 0.9
ADAM_B2 = 0.999
ADAM_EPS = 1e-08
ADAM_WD = 0.01
ADAM_STEP = 10

VMEM_LIMIT = 56 * 1024 * 1024
MM_FULL_K = 3072
MM_MAX_TN = 3072
EPI_TM = 256
ROW_BLOCK_BYTES = 12 * 1024 * 1024
EW_BLOCK_ELEMS = 1 << 20
MESH = pl.DeviceIdType.MESH


def _cparams(sem=None):
    return pltpu.CompilerParams(dimension_semantics=sem, vmem_limit_bytes=VMEM_LIMIT)


def _pick(dim, pref, align=LANES):
    best = None
    t = align
    while t <= min(dim, pref):
        if dim % t == 0:
            best = t
        t += align
    return best or dim


def _slot_of(k):
    return (k & 1) * (N_DEV // 2) + (k >> 1)


def _mm(a, b, *, name, ta=False, tb=False, a_blk=None, b_blk=None, o_blk=None, o_slots=False, a_seg=False, b_seg=False,
        o_seg=False, tm=None, out_dtype=f32, scale=1.0, add=None, epi=None, carry=None):
    a2, b2 = a.shape[-2:], b.shape[-2:]
    Ma, Ka = (a2[1], a2[0]) if ta else a2
    Kb, Nb = (b2[1], b2[0]) if tb else b2
    M = Ma * (a.shape[0] if a_blk == "m" else 1)
    K = Ka * (a.shape[0] if a_blk == "k" else 1)
    N = Nb * (b.shape[0] if b_blk == "n" else 1)
    assert K == Kb * (b.shape[0] if b_blk == "k" else 1), (a.shape, b.shape, ta, tb, a_blk, b_blk)
    assert (a.ndim == 3) == (a_blk is not None) and (b.ndim == 3) == (b_blk is not None)
    tm = Ma if a_blk == "m" else (tm or _pick(M, 512))
    tn = Nb if b_blk == "n" else _pick(N, MM_MAX_TN)
    if a_blk == "k" or b_blk == "k":
        tk = Ka if a_blk == "k" else Kb
        assert tk == (Kb if b_blk == "k" else tk)
    else:
        tk = K if K <= MM_FULL_K else _pick(K, 1024 if ta else MM_FULL_K)
    if (a_seg and not ta) or o_seg:
        tm = M // NSEG
    if (a_seg and ta) or b_seg:
        tk = K // NSEG
    gm, gn, nk = M // tm, N // tn, K // tk
    assert not (add is not None and (o_seg or o_blk)) and not (o_blk and o_seg)

    if a_seg:
        assert a.ndim == 2
        a = a.reshape(a.shape[0] // NSEG, NSEG * a.shape[1])
        if ta:
            a_spec = pl.BlockSpec((tk, tm), lambda i, j, k: (0, k * (Ma // tm) + i))
        else:
            a_spec = pl.BlockSpec((tm, tk), lambda i, j, k: (0, i * (Ka // tk) + k))
    elif a.ndim == 3:
        lead = (lambda i, k: i) if a_blk == "m" else (lambda i, k: k)
        if ta:
            a_spec = pl.BlockSpec((None, tk, tm), lambda i, j, k: (lead(i, k), 0 if a_blk == "k" else k, 0 if a_blk == "m" else i))
        else:
            a_spec = pl.BlockSpec((None, tm, tk), lambda i, j, k: (lead(i, k), 0 if a_blk == "m" else i, 0 if a_blk == "k" else k))
    else:
        a_spec = pl.BlockSpec((tk, tm), lambda i, j, k: (k, i)) if ta else pl.BlockSpec((tm, tk), lambda i, j, k: (i, k))
    if b_seg:
        assert b.ndim == 2 and not tb
        b = b.reshape(b.shape[0] // NSEG, NSEG * b.shape[1])
        b_spec = pl.BlockSpec((tk, tn), lambda i, j, k: (0, k * (Nb // tn) + j))
    elif b.ndim == 3:
        lead = (lambda j, k: j) if b_blk == "n" else (lambda j, k: k)
        if tb:
            b_spec = pl.BlockSpec((None, tn, tk), lambda i, j, k: (lead(j, k), 0 if b_blk == "n" else j, 0 if b_blk == "k" else k))
        else:
            b_spec = pl.BlockSpec((None, tk, tn), lambda i, j, k: (lead(j, k), 0 if b_blk == "k" else k, 0 if b_blk == "n" else j))
    else:
        b_spec = pl.BlockSpec((tn, tk), lambda i, j, k: (j, k)) if tb else pl.BlockSpec((tk, tn), lambda i, j, k: (k, j))
    slot = _slot_of if o_slots else (lambda k: k)
    if o_blk == "n":
        assert gn == N_DEV or not o_slots
        o_shape, o_spec = (gn, M, tn), pl.BlockSpec((None, tm, tn), lambda i, j, k: (slot(j), i, 0))
    elif o_blk == "m" and o_slots and gm < N_DEV:
        rs = M // N_DEV
        per_tile = tm // rs
        assert per_tile % 2 == 0 and tm % rs == 0
        o_shape = (2, N_CHIP, rs, N)
        o_spec = pl.BlockSpec((2, per_tile // 2, rs, tn), lambda i, j, k: (0, i, 0, j))
    elif o_blk == "m":
        assert gm == N_DEV or not o_slots
        o_shape, o_spec = (gm, tm, N), pl.BlockSpec((None, tm, tn), lambda i, j, k: (slot(i), 0, j))
    elif o_seg:
        o_shape, o_spec = (tm, NSEG * N), pl.BlockSpec((tm, tn), lambda i, j, k: (0, i * (N // tn) + j))
    else:
        o_shape, o_spec = (M, N), pl.BlockSpec((tm, tn), lambda i, j, k: (i, j))
    dims = (((0 if ta else 1,), (1 if tb else 0,)), ((), ()))
    if epi is not None:
        assert gn == 1 and add is None and o_blk is None and not o_seg
        epi_fn, add, epi_w = epi
        o_shape, o_spec = (M, epi_w), pl.BlockSpec((tm, epi_w), lambda i, j, k: (i, 0))
    has_add = add is not None
    add_spec = pl.BlockSpec((tm, add.shape[1]), lambda i, j, k: (i, 0)) if epi is not None else o_spec

    carry = carry or _NO_CARRY
    n_in = 2 + has_add

    def body(*refs):
        own, c_in, c_out, c_sems = _carry_split(carry, refs, n_in, 1)
        a_ref, b_ref = own[0], own[1]
        add_ref = own[2] if has_add else None
        o_ref, acc_ref = own[-2], own[-1]
        i, j, k = pl.program_id(0), pl.program_id(1), pl.program_id(2)
        _carry_start(carry, c_in, c_out, c_sems, (i == 0) & (j == 0) & (k == 0))

        @pl.when(k == 0)
        def _():
            acc_ref[...] = jnp.zeros_like(acc_ref)

        acc_ref[...] += lax.dot_general(a_ref[...].astype(bf16), b_ref[...].astype(bf16), dims, preferred_element_type=f32)

        @pl.when(k == nk - 1)
        def _():
            r = acc_ref[...] * scale
            if epi is not None:
                r = epi_fn(r, add_ref[...].astype(f32))
            elif has_add:
                r = r + add_ref[...].astype(f32)
            if len(o_shape) == 4:
                rs = o_shape[2]
                for chip_l in range(o_ref.shape[1]):
                    for core in range(2):
                        dev = 2 * chip_l + core
                        o_ref[core, chip_l] = r[dev * rs:(dev + 1) * rs].astype(out_dtype)
            else:
                o_ref[...] = r.astype(out_dtype)

        _carry_finish(carry, c_in, c_out, c_sems, (i == gm - 1) & (j == gn - 1) & (k == nk - 1))

    ins = [a, b] + ([add] if has_add else []) + list(carry.ins)
    in_specs = [a_spec, b_spec] + ([add_spec] if has_add else []) + [_ANY] * len(carry.ins)
    res = pl.pallas_call(
        body, name=name, grid=(gm, gn, nk), in_specs=in_specs, out_specs=[o_spec] + [_ANY] * len(carry.out_shapes),
        out_shape=[_S(o_shape, out_dtype)] + list(carry.out_shapes),
        scratch_shapes=[pltpu.VMEM((tm, tn), f32)] + list(carry.sems),
        compiler_params=_cparams(("arbitrary",) * 3 if carry.ins else ("parallel", "parallel", "arbitrary")),
    )(*ins)
    out = res[0]
    if len(o_shape) == 4:
        out = out.reshape(N_DEV, o_shape[2], N)
    elif o_seg:
        out = out.reshape(M, N)
    return (out, list(res[1:])) if carry.ins else out


def _row_tile(T, widths):
    budget = ROW_BLOCK_BYTES
    tb = max(16, budget // (4 * sum(widths)))
    return _pick(T, tb, align=16)


def _rows(fn, rows, params, outs, *, name, carry=None):
    T = rows[0].shape[0]
    nr, npar = len(rows), len(params)
    tb = _row_tile(T, [r.shape[1] for r in rows] + [w for w, _ in outs])
    carry = carry or _NO_CARRY

    def body(*refs):
        own, c_in, c_out, c_sems = _carry_split(carry, refs, nr + npar, len(outs))
        _carry_start(carry, c_in, c_out, c_sems, pl.program_id(0) == 0)
        ins = [r[...].astype(f32) for r in own[: nr + npar]]
        res = fn(*ins)
        for o_ref, r in zip(own[nr + npar:], res):
            o_ref[...] = r.astype(o_ref.dtype)
        _carry_finish(carry, c_in, c_out, c_sems, pl.program_id(0) == T // tb - 1)

    in_specs = [pl.BlockSpec((tb, r.shape[1]), lambda i: (i, 0)) for r in rows]
    in_specs += [pl.BlockSpec(p.shape, lambda i: (0, 0)) for p in params]
    out_specs = [pl.BlockSpec((tb, w), lambda i: (i, 0)) for w, _ in outs]
    res = pl.pallas_call(
        body, name=name, grid=(T // tb,), in_specs=in_specs + [_ANY] * len(carry.ins),
        out_specs=out_specs + [_ANY] * len(carry.out_shapes), out_shape=[_S((T, w), d) for w, d in outs] + list(carry.out_shapes),
        scratch_shapes=list(carry.sems), compiler_params=_cparams(("arbitrary",) if carry.ins else ("parallel",)),
    )(*rows, *params, *carry.ins)
    return (tuple(res[:len(outs)]), list(res[len(outs):])) if carry.ins else tuple(res)


def _rows_bwd(fn, rows, params, cots, *, name, want_rows, row_dtypes=None, adds=None):
    T = rows[0].shape[0]
    nr, npar, nc = len(rows), len(params), len(cots)
    adds = adds or {}
    add_idx = sorted(adds)
    row_dtypes = row_dtypes or {}
    widths = [r.shape[1] for r in rows] + [c.shape[1] for c in cots] + [rows[i].shape[1] for i in want_rows]
    tb = _row_tile(T, widths)

    def body(*refs):
        ins = [r[...].astype(f32) for r in refs[: nr + npar]]
        cot = tuple(r[...].astype(f32) for r in refs[nr + npar: nr + npar + nc])
        add_refs = refs[nr + npar + nc: nr + npar + nc + len(add_idx)]
        out_refs = refs[nr + npar + nc + len(add_idx):]
        _, vjp = jax.vjp(lambda *a: tuple(fn(*a)), *ins)
        g = vjp(cot)
        for o_ref, i in zip(out_refs[: len(want_rows)], want_rows):
            r = g[i]
            if i in adds:
                r = r + add_refs[add_idx.index(i)][...].astype(f32)
            o_ref[...] = r.astype(o_ref.dtype)
        first = pl.program_id(0) == 0
        for o_ref, gp in zip(out_refs[len(want_rows):], g[nr:]):
            @pl.when(first)
            def _(o_ref=o_ref):
                o_ref[...] = jnp.zeros_like(o_ref)

            o_ref[...] += gp

    in_specs = [pl.BlockSpec((tb, r.shape[1]), lambda i: (i, 0)) for r in rows]
    in_specs += [pl.BlockSpec(p.shape, lambda i: (0, 0)) for p in params]
    in_specs += [pl.BlockSpec((tb, c.shape[1]), lambda i: (i, 0)) for c in cots]
    in_specs += [pl.BlockSpec((tb, adds[i].shape[1]), lambda i_: (i_, 0)) for i in add_idx]
    out_specs = [pl.BlockSpec((tb, rows[i].shape[1]), lambda i_: (i_, 0)) for i in want_rows]
    out_specs += [pl.BlockSpec(p.shape, lambda i: (0, 0)) for p in params]
    out_shape = [_S(rows[i].shape, row_dtypes.get(i, f32)) for i in want_rows] + [_S(p.shape, f32) for p in params]
    res = pl.pallas_call(
        body, name=name, grid=(T // tb,), in_specs=in_specs, out_specs=out_specs, out_shape=out_shape,
        compiler_params=_cparams(("arbitrary",)),
    )(*rows, *params, *cots, *[adds[i] for i in add_idx])
    return list(res[: len(want_rows)]), list(res[len(want_rows):])


def _f_rmsnorm(x, g):
    return (x * lax.rsqrt(jnp.mean(x * x, axis=-1, keepdims=True) + EPS) * g,)


def _f_swiglu(ab):
    F = ab.shape[1] // 2
    return (jax.nn.silu(ab[:, :F]) * ab[:, F:],)


def _f_s5_post(y, u, d):
    return (jax.nn.gelu(y + d * u),)


def _f_glu(g, v, b):
    return (g * jax.nn.sigmoid(v + b),)


def _f_gated_norm(y, z, w):
    return _f_rmsnorm(y * jax.nn.silu(z), w)


def _f_merge(gl, p5, pssd, b):
    D = p5.shape[1]
    gates = jax.nn.sigmoid(gl + b)
    return (gates[:, :D] * p5 + gates[:, D:] * pssd,)


def _f_dt(dtr, bias, a_log):
    dt = jax.nn.softplus(dtr + bias)
    return dt, dt * (-jnp.exp(a_log))


def _f_dt_expand(dtr, bias, a_log, e):
    dt, a = _f_dt(dtr, bias, a_log)
    return dt, a, _doth(dt, e), _doth(a, e)


def _loss_stage(x, tgt, g, *, name):
    T, D = x.shape
    tb = _row_tile(T, [D, D, D])

    def f(xb, gb, tb_):
        y = _f_rmsnorm(xb, gb)[0]
        return 0.5 * jnp.sum(jnp.mean(jnp.square(y - tb_), axis=-1, keepdims=True), axis=0, keepdims=True)

    def body(x_ref, t_ref, g_ref, l_ref, dx_ref, dg_ref):
        tv = t_ref[...]
        val, vjp = jax.vjp(lambda a, b: f(a, b, tv), x_ref[...], g_ref[...])
        dx, dg = vjp(jnp.ones((1, 1), f32))
        dx_ref[...] = dx

        @pl.when(pl.program_id(0) == 0)
        def _():
            l_ref[...] = jnp.zeros_like(l_ref)
            dg_ref[...] = jnp.zeros_like(dg_ref)

        l_ref[...] += jnp.broadcast_to(val, l_ref.shape)
        dg_ref[...] += dg

    row = pl.BlockSpec((tb, D), lambda i: (i, 0))
    par = pl.BlockSpec((1, D), lambda i: (0, 0))
    return pl.pallas_call(
        body, name=name, grid=(T // tb,), in_specs=[row, row, par],
        out_specs=[pl.BlockSpec((1, LANES), lambda i: (0, 0)), row, par],
        out_shape=[_S((1, LANES), f32), _S((T, D), f32), _S((1, D), f32)], compiler_params=_cparams(("arbitrary",)),
    )(x, tgt, g)


CONV_R = 64
HALO = 8


def _conv_shifts_down(ref, t):
    if isinstance(t, int) and t == 0:
        cur = ref[0:CONV_R, :]
        row = lax.broadcasted_iota(jnp.int32, cur.shape, 0)
        return [cur] + [jnp.where(row >= s, pltpu.roll(cur, s, axis=0), 0.0) for s in range(1, CONV_K)]
    win = ref[pl.ds(pl.multiple_of(t * CONV_R - HALO, HALO), CONV_R + HALO), :]
    return [win[HALO:]] + [pltpu.roll(win, s, axis=0)[HALO:] for s in range(1, CONV_K)]


def _conv_shifts_up(ref, t):
    if isinstance(t, int):
        cur = ref[t * CONV_R:(t + 1) * CONV_R, :]
        row = lax.broadcasted_iota(jnp.int32, cur.shape, 0)
        return [cur] + [jnp.where(row < CONV_R - s, pltpu.roll(cur, CONV_R - s, axis=0), 0.0) for s in range(1, CONV_K)]
    win = ref[pl.ds(pl.multiple_of(t * CONV_R, HALO), CONV_R + HALO), :]
    return [win[:CONV_R]] + [pltpu.roll(win, CONV_R + HALO - s, axis=0)[:CONV_R] for s in range(1, CONV_K)]


def _conv_pre(shifted, w, b):
    pre = b
    for k in range(CONV_K):
        pre = pre + w[k:k + 1, :] * shifted[CONV_K - 1 - k]
    return pre


def _conv_fwd(x, w, b, *, name):
    T, C = x.shape
    cb = _pick(C, 256)

    def body(x_ref, w_ref, b_ref, o_ref):
        xv = x_ref[...]
        row = lax.broadcasted_iota(jnp.int32, xv.shape, 0)
        shifted = [xv] + [jnp.where(row >= s, pltpu.roll(xv, s, axis=0), 0.0) for s in range(1, CONV_K)]
        o_ref[...] = jax.nn.silu(_conv_pre(shifted, w_ref[...], b_ref[...]))

    col = pl.BlockSpec((T, cb), lambda j: (0, j))
    return pl.pallas_call(
        body, name=name, grid=(C // cb,), in_specs=[col, pl.BlockSpec((CONV_K, cb), lambda j: (0, j)), pl.BlockSpec((1, cb), lambda j: (0, j))],
        out_specs=col, out_shape=_S((T, C), f32), compiler_params=_cparams(("parallel",)),
    )(x, w, b)


def _conv_bwd(x, w, b, dy, *, name, carry=None):
    T, C = x.shape
    cb = _pick(C, 128)
    carry = carry or _NO_CARRY
    ends = []
    for d in dy:
        ends.append((ends[-1] if ends else 0) + d.shape[1] // cb)
    assert ends[-1] == C // cb and all(d.shape[1] % cb == 0 for d in dy)
    npc = len(dy)
    n_tiles = T // CONV_R

    def body(*refs):
        own, c_in, c_out, c_sems = _carry_split(carry, refs, 3 + npc, 3)
        x_ref, w_ref, b_ref = own[:3]
        dy_refs, (dx_ref, dw_ref, db_ref, dpre_ref) = own[3:3 + npc], own[3 + npc:]
        _carry_start(carry, c_in, c_out, c_sems, pl.program_id(0) == 0)
        j = pl.program_id(0)
        wv, bv = w_ref[...], b_ref[...]

        def fold8(v):
            return jnp.sum(v.reshape(CONV_R // 8, 8, cb), axis=0)

        def first_pass(t, acc):
            rows = slice(0, CONV_R) if isinstance(t, int) else pl.ds(pl.multiple_of(t * CONV_R, CONV_R), CONV_R)
            shifted = _conv_shifts_down(x_ref, t)
            pre = _conv_pre(shifted, wv, bv)
            dyv = dy_refs[-1][rows, :]
            for p in range(npc - 2, -1, -1):
                dyv = jnp.where(j < ends[p], dy_refs[p][rows, :], dyv)
            sg = jax.nn.sigmoid(pre)
            dpre = dyv * sg * (1.0 + pre * (1.0 - sg))
            dpre_ref[rows, :] = dpre
            return tuple(acc[k] + fold8(dpre * shifted[CONV_K - 1 - k]) for k in range(CONV_K)) + (acc[CONV_K] + fold8(dpre),)

        zero = jnp.zeros((8, cb), f32)
        acc = lax.fori_loop(1, n_tiles, first_pass, first_pass(0, (zero,) * (CONV_K + 1)), unroll=3)
        for k in range(CONV_K):
            dw_ref[k:k + 1, :] = jnp.sum(acc[k], axis=0, keepdims=True)
        db_ref[...] = jnp.sum(acc[CONV_K], axis=0, keepdims=True)

        def dx_of(t):
            up = _conv_shifts_up(dpre_ref, t)
            dx = wv[CONV_K - 1:CONV_K, :] * up[0]
            for k in range(CONV_K - 1):
                dx = dx + wv[k:k + 1, :] * up[CONV_K - 1 - k]
            return dx

        def second_pass(t, c):
            dx_ref[pl.ds(pl.multiple_of(t * CONV_R, CONV_R), CONV_R), :] = dx_of(t)
            return c

        lax.fori_loop(0, n_tiles - 1, second_pass, 0, unroll=3)
        dx_ref[(n_tiles - 1) * CONV_R:, :] = dx_of(n_tiles - 1)
        _carry_finish(carry, c_in, c_out, c_sems, pl.program_id(0) == C // cb - 1)

    col = pl.BlockSpec((T, cb), lambda j: (0, j))
    wsp = pl.BlockSpec((CONV_K, cb), lambda j: (0, j))
    bsp = pl.BlockSpec((1, cb), lambda j: (0, j))
    starts = [0] + ends[:-1]
    dy_specs = [pl.BlockSpec((T, cb), lambda j, s=s, e=e: (0, jnp.clip(j, s, e - 1) - s)) for s, e in zip(starts, ends)]
    res = pl.pallas_call(
        body, name=name, grid=(C // cb,), in_specs=[col, wsp, bsp] + dy_specs + [_ANY] * len(carry.ins),
        out_specs=[col, wsp, bsp] + [_ANY] * len(carry.out_shapes),
        out_shape=[_S((T, C), f32), _S((CONV_K, C), f32), _S((1, C), f32)] + list(carry.out_shapes),
        scratch_shapes=[pltpu.VMEM((T, cb), f32)] + list(carry.sems),
        compiler_params=_cparams(("arbitrary",) if carry.ins else ("parallel",)),
    )(x, w, b, *dy, *carry.ins)
    return (*res[:3], list(res[3:]))


def _f_s5_prep(lr, li, ldt, lrb, lib, ldtb, brt, bit):
    def disc(lr_, li_, ldt_):
        dt = jnp.exp(ldt_)
        mag = jnp.exp(lr_ * dt)
        ar, ai = mag * jnp.cos(li_ * dt), mag * jnp.sin(li_ * dt)
        den = lr_ * lr_ + li_ * li_
        cr = ((ar - 1.0) * lr_ + ai * li_) / den
        ci = (ai * lr_ - (ar - 1.0) * li_) / den
        return ar, ai, cr, ci

    ar, ai, _, _ = disc(lr, li, ldt)
    _, _, cr, ci = disc(lrb, lib, ldtb)
    return ar, ai, cr * brt - ci * bit, cr * bit + ci * brt


def _s5_prep(args, *, name):
    G, N = args[0].shape
    GM = args[3].shape[0]

    def body(*refs):
        res = _f_s5_prep(*[r[...] for r in refs[:8]])
        for o, r in zip(refs[8:], res):
            o[...] = r

    return pl.pallas_call(body, name=name, out_shape=[_S((G, N), f32)] * 2 + [_S((GM, N), f32)] * 2)(*args)


def _s5_prep_bwd(args, cots, rsum, *, name):
    G, N = args[0].shape
    GM = args[3].shape[0]

    def body(*refs):
        ins = [r[...] for r in refs[:8]]
        cot = tuple(r[...] for r in refs[8:12])
        rs = refs[12][...]
        _, vjp = jax.vjp(_f_s5_prep, *ins)
        g = vjp(cot)
        fold = lambda v: jnp.dot(rs, v, preferred_element_type=f32, precision=lax.Precision.HIGHEST)
        o = refs[13:]
        o[0][...] = g[0] + fold(g[3])
        o[1][...] = g[1] + fold(g[4])
        o[2][...] = g[2] + fold(jnp.broadcast_to(g[5], (GM, LANES)))[:, 0:1]
        o[3][...] = g[6]
        o[4][...] = g[7]

    return pl.pallas_call(
        body, name=name, out_shape=[_S((G, N), f32), _S((G, N), f32), _S((G, 1), f32), _S((GM, N), f32), _S((GM, N), f32)],
    )(*args, *cots, rsum)


S5_TC = 512


def _s5_local_scan(src, w_r, w_i, a_r, a_i, *, reverse, name, carry=None, powers=True):
    T, C = src.shape
    nblk, cb, sb = w_r.shape
    NS = nblk * sb
    tc = min(S5_TC, T)
    nT, nt = T // tc, tc // NSEG
    tmap = (lambda i: nT - 1 - i) if reverse else (lambda i: i)

    carry = carry or _NO_CARRY

    def body(*refs):
        own, c_in, c_out, c_sems = _carry_split(carry, refs, 5, 4)
        u_ref, wr_ref, wi_ref, ar_ref, ai_ref, sr_ref, si_ref, pr_ref, pi_ref, st_r, st_i, pw_r, pw_i = own
        _carry_start(carry, c_in, c_out, c_sems, (pl.program_id(0) == 0) & (pl.program_id(1) == 0))

        @pl.when(pl.program_id(1) == 0)
        def _():
            st_r[...] = jnp.zeros_like(st_r)
            st_i[...] = jnp.zeros_like(st_i)
            pw_r[...] = jnp.ones_like(pw_r)
            pw_i[...] = jnp.zeros_like(pw_i)

        u = u_ref[...].astype(bf16)
        sr_ref[...] = jnp.dot(u, wr_ref[...], preferred_element_type=f32)
        si_ref[...] = jnp.dot(u, wi_ref[...], preferred_element_type=f32)
        ar = jnp.broadcast_to(ar_ref[...], (NSEG, sb))
        ai = jnp.broadcast_to(ai_ref[...], (NSEG, sb))

        def step(k, c):
            cr, ci, qr, qi = c
            kk = (nt - 1 - k) if reverse else k
            rows = pl.ds(pl.multiple_of(kk * NSEG, NSEG), NSEG)
            nr = ar * cr - ai * ci + sr_ref[rows, :]
            ni = ar * ci + ai * cr + si_ref[rows, :]
            sr_ref[rows, :] = nr
            si_ref[rows, :] = ni
            return (nr, ni, ar * qr - ai * qi, ar * qi + ai * qr) if powers else (nr, ni, qr, qi)

        cr, ci, qr, qi = lax.fori_loop(0, nt, step, (st_r[...], st_i[...], pw_r[...], pw_i[...]), unroll=8)
        st_r[...], st_i[...], pw_r[...], pw_i[...] = cr, ci, qr, qi
        pr_ref[...] = qr
        pi_ref[...] = qi
        _carry_finish(carry, c_in, c_out, c_sems, (pl.program_id(0) == nblk - 1) & (pl.program_id(1) == nT - 1))

    blk = pl.BlockSpec((tc, sb), lambda j, i: (tmap(i), j))
    wsp = pl.BlockSpec((None, cb, sb), lambda j, i: (j, 0, 0))
    asp = pl.BlockSpec((1, sb), lambda j, i: (0, j))
    psp = pl.BlockSpec((NSEG, sb), lambda j, i: (0, j))
    res = pl.pallas_call(
        body, name=name, grid=(nblk, nT),
        in_specs=[pl.BlockSpec((tc, cb), lambda j, i: (tmap(i), j)), wsp, wsp, asp, asp] + [_ANY] * len(carry.ins),
        out_specs=[blk, blk, psp, psp] + [_ANY] * len(carry.out_shapes),
        out_shape=[_S((T, NS), f32)] * 2 + [_S((NSEG, NS), f32)] * 2 + list(carry.out_shapes),
        scratch_shapes=[pltpu.VMEM((NSEG, sb), f32)] * 4 + list(carry.sems),
        compiler_params=_cparams(("arbitrary", "arbitrary") if carry.ins else ("parallel", "arbitrary")),
    )(src, w_r, w_i, a_r, a_i, *carry.ins)
    return (*res[:4], list(res[4:])) if carry.ins else res


def _s5_carry(e_r, e_i, p_r, p_i, *, reverse, name):
    NS = e_r.shape[1]

    def body(er_ref, ei_ref, pr_ref, pi_ref, cr_ref, ci_ref):
        ar, ai = pr_ref[0:1, :], pi_ref[0:1, :]
        cr = jnp.zeros((1, NS), f32)
        ci = jnp.zeros((1, NS), f32)
        order = list(range(NSEG - 1, -1, -1)) if reverse else list(range(NSEG))
        cr_ref[order[0]:order[0] + 1, :] = cr
        ci_ref[order[0]:order[0] + 1, :] = ci
        for prev, q in zip(order[:-1], order[1:]):
            er, ei = er_ref[prev:prev + 1, :], ei_ref[prev:prev + 1, :]
            cr, ci = er + ar * cr - ai * ci, ei + ar * ci + ai * cr
            cr_ref[q:q + 1, :] = cr
            ci_ref[q:q + 1, :] = ci

    return pl.pallas_call(body, name=name, out_shape=[_S((NSEG, NS), f32)] * 2)(e_r, e_i, p_r, p_i)


def _s5_fix_out(sl_r, sl_i, a_r, a_i, c_r, c_i, wc_r, wc_i, *, name):
    T, NS = sl_r.shape
    nblk, sb, cb = wc_r.shape
    tc = min(S5_TC, T)
    nT, nt = T // tc, tc // NSEG

    def body(lr_ref, li_ref, ar_ref, ai_ref, cr_ref, ci_ref, wr_ref, wi_ref, sr_ref, si_ref, y_ref, pw_r, pw_i):
        @pl.when(pl.program_id(1) == 0)
        def _():
            pw_r[...] = jnp.ones_like(pw_r)
            pw_i[...] = jnp.zeros_like(pw_i)

        ar = jnp.broadcast_to(ar_ref[...], (NSEG, sb))
        ai = jnp.broadcast_to(ai_ref[...], (NSEG, sb))
        cr, ci = cr_ref[...], ci_ref[...]

        def step(k, c):
            qr, qi = c
            qr, qi = ar * qr - ai * qi, ar * qi + ai * qr
            rows = pl.ds(pl.multiple_of(k * NSEG, NSEG), NSEG)
            sr_ref[rows, :] = lr_ref[rows, :] + qr * cr - qi * ci
            si_ref[rows, :] = li_ref[rows, :] + qr * ci + qi * cr
            return qr, qi

        qr, qi = lax.fori_loop(0, nt, step, (pw_r[...], pw_i[...]), unroll=8)
        pw_r[...], pw_i[...] = qr, qi
        y_ref[...] = (jnp.dot(sr_ref[...].astype(bf16), wr_ref[...], preferred_element_type=f32)
                      - jnp.dot(si_ref[...].astype(bf16), wi_ref[...], preferred_element_type=f32))

    blk = pl.BlockSpec((tc, sb), lambda j, i: (i, j))
    asp = pl.BlockSpec((1, sb), lambda j, i: (0, j))
    csp = pl.BlockSpec((NSEG, sb), lambda j, i: (0, j))
    wsp = pl.BlockSpec((None, sb, cb), lambda j, i: (j, 0, 0))
    return pl.pallas_call(
        body, name=name, grid=(nblk, nT), in_specs=[blk, blk, asp, asp, csp, csp, wsp, wsp],
        out_specs=[blk, blk, pl.BlockSpec((tc, cb), lambda j, i: (i, j))],
        out_shape=[_S((T, NS), f32)] * 2 + [_S((T, nblk * cb), f32)],
        scratch_shapes=[pltpu.VMEM((NSEG, sb), f32)] * 2, compiler_params=_cparams(("parallel", "arbitrary")),
    )(sl_r, sl_i, a_r, a_i, c_r, c_i, wc_r, wc_i)


def _s5_fix_bwd(ql_r, ql_i, ab_r, ab_i, c_r, c_i, s_r, s_i, sb_r, sb_i, u, dy, du_add, w_r, w_i, *, name):
    T, NS = ql_r.shape
    nblk, cb, sb = w_r.shape
    tc = min(S5_TC, T)
    nT, nt = T // tc, tc // NSEG
    tmap = lambda i: nT - 1 - i

    def body(lr_ref, li_ref, ar_ref, ai_ref, cr_ref, ci_ref, sr_ref, si_ref, br_ref, bi_ref, u_ref, dy_ref, dua_ref, wr_ref, wi_ref,
             du_ref, dwr_ref, dwi_ref, dcr_ref, dci_ref, dar_ref, dai_ref, pw_r, pw_i, ac_r, ac_i, q_r, q_i):
        first = pl.program_id(1) == 0

        @pl.when(first)
        def _():
            pw_r[...] = jnp.ones_like(pw_r)
            pw_i[...] = jnp.zeros_like(pw_i)
            ac_r[...] = jnp.zeros_like(ac_r)
            ac_i[...] = jnp.zeros_like(ac_i)
            dwr_ref[...] = jnp.zeros_like(dwr_ref)
            dwi_ref[...] = jnp.zeros_like(dwi_ref)
            dcr_ref[...] = jnp.zeros_like(dcr_ref)
            dci_ref[...] = jnp.zeros_like(dci_ref)

        ar = jnp.broadcast_to(ar_ref[...], (NSEG, sb))
        ai = jnp.broadcast_to(ai_ref[...], (NSEG, sb))
        cr, ci = cr_ref[...], ci_ref[...]

        def fix(rows, qr, qi, spr, spi, accr, acci):
            qr, qi = ar * qr - ai * qi, ar * qi + ai * qr
            xr = lr_ref[rows, :] + qr * cr - qi * ci
            xi = li_ref[rows, :] + qr * ci + qi * cr
            q_r[rows, :] = xr
            q_i[rows, :] = xi
            return qr, qi, accr + xr * spr + xi * spi, acci + xi * spr - xr * spi

        def step(k, c):
            qr, qi, accr, acci = c
            kk = nt - 1 - k
            rows = pl.ds(pl.multiple_of(kk * NSEG, NSEG), NSEG)
            prev = pl.ds(pl.multiple_of((kk - 1) * NSEG, NSEG), NSEG)
            return fix(rows, qr, qi, sr_ref[prev, :], si_ref[prev, :], accr, acci)

        c = lax.fori_loop(0, nt - 1, step, (pw_r[...], pw_i[...], ac_r[...], ac_i[...]), unroll=7)
        qr, qi, accr, acci = fix(pl.ds(0, NSEG), *c[:2], br_ref[...], bi_ref[...], *c[2:])
        pw_r[...], pw_i[...], ac_r[...], ac_i[...] = qr, qi, accr, acci

        qrb, qib = q_r[...].astype(bf16), q_i[...].astype(bf16)
        nt_dims = (((1,), (1,)), ((), ()))
        tn_dims = (((0,), (0,)), ((), ()))
        du_ref[...] = (dua_ref[...] + lax.dot_general(qrb, wr_ref[...], nt_dims, preferred_element_type=f32)
                       + lax.dot_general(qib, wi_ref[...], nt_dims, preferred_element_type=f32))
        ub = u_ref[...].astype(bf16)
        dwr_ref[...] += lax.dot_general(ub, qrb, tn_dims, preferred_element_type=f32)
        dwi_ref[...] += lax.dot_general(ub, qib, tn_dims, preferred_element_type=f32)
        dyb = dy_ref[...].astype(bf16)
        dcr_ref[...] += lax.dot_general(sr_ref[...].astype(bf16), dyb, tn_dims, preferred_element_type=f32)
        dci_ref[...] -= lax.dot_general(si_ref[...].astype(bf16), dyb, tn_dims, preferred_element_type=f32)

        @pl.when(pl.program_id(1) == nT - 1)
        def _():
            dar_ref[...] = jnp.sum(accr, axis=0, keepdims=True)
            dai_ref[...] = jnp.sum(acci, axis=0, keepdims=True)

    blk = pl.BlockSpec((tc, sb), lambda j, i: (tmap(i), j))
    asp = pl.BlockSpec((1, sb), lambda j, i: (0, j))
    csp = pl.BlockSpec((NSEG, sb), lambda j, i: (0, j))
    bsp = pl.BlockSpec((None, NSEG, sb), lambda j, i: (tmap(i), 0, j))
    chn = pl.BlockSpec((tc, cb), lambda j, i: (tmap(i), j))
    wsp = pl.BlockSpec((None, cb, sb), lambda j, i: (j, 0, 0))
    wcs = pl.BlockSpec((None, sb, cb), lambda j, i: (j, 0, 0))
    return pl.pallas_call(
        body, name=name, grid=(nblk, nT), in_specs=[blk, blk, asp, asp, csp, csp, blk, blk, bsp, bsp, chn, chn, chn, wsp, wsp],
        out_specs=[chn, wsp, wsp, wcs, wcs, asp, asp],
        out_shape=[_S((T, nblk * cb), f32), _S((nblk, cb, sb), f32), _S((nblk, cb, sb), f32), _S((nblk, sb, cb), f32),
                   _S((nblk, sb, cb), f32), _S((1, NS), f32), _S((1, NS), f32)],
        scratch_shapes=[pltpu.VMEM((NSEG, sb), f32)] * 4 + [pltpu.VMEM((tc, sb), f32)] * 2,
        compiler_params=_cparams(("parallel", "arbitrary")),
    )(ql_r, ql_i, ab_r, ab_i, c_r, c_i, s_r, s_i, sb_r, sb_i, u, dy, du_add, w_r, w_i)


SSD2_TB = 512
_NN = (((1,), (0,)), ((), ()))
_NT = (((1,), (1,)), ((), ()))
_TN = (((0,), (0,)), ((), ()))


def _dotf(a, b, dims):
    return lax.dot_general(a.astype(bf16), b.astype(bf16), dims, preferred_element_type=f32)


def _doth(a, b, dims=_NN, sel="b", parts=3):
    x, m = (a, b) if sel == "b" else (b, a)
    m = m.astype(bf16)
    out = None
    for _ in range(parts):
        piece = x.astype(bf16)
        x = x - piece.astype(f32)
        d = lax.dot_general(*((piece, m) if sel == "b" else (m, piece)), dims, preferred_element_type=f32)
        out = d if out is None else out + d
    return out


def _ssd_consts(hpg):
    W = hpg * CHUNK
    i = lax.broadcasted_iota(jnp.int32, (CHUNK, CHUNK), 0)
    j = lax.broadcasted_iota(jnp.int32, (CHUNK, CHUNK), 1)
    tril = (i >= j).astype(f32)
    r = lax.broadcasted_iota(jnp.int32, (W, W), 0)
    c = lax.broadcasted_iota(jnp.int32, (W, W), 1)
    bd = (r // CHUNK == c // CHUNK).astype(f32)
    triu_bd = bd * (r <= c).astype(f32)
    e_r = lax.broadcasted_iota(jnp.int32, (W, LANES), 0)
    e_c = lax.broadcasted_iota(jnp.int32, (W, LANES), 1)
    ered = (e_r // HEADDIM == e_c).astype(f32)
    return tril, jnp.tile(tril, (1, hpg)), bd, triu_bd, ered


def _ssd2_specs(G, hpg, tb, tmap, b_off, c_off):
    W = hpg * HEADDIM
    ncb = tb // CHUNK
    xsp = pl.BlockSpec((tb, W), lambda g, i: (tmap(i), g))
    bsp = pl.BlockSpec((tb, SSD_STATE), lambda g, i: (tmap(i), b_off + g))
    csp = pl.BlockSpec((tb, SSD_STATE), lambda g, i: (tmap(i), c_off + g))
    rsp = pl.BlockSpec((None, ncb, W), lambda g, i: (g, tmap(i), 0))
    dsp = pl.BlockSpec((1, W), lambda g, i: (0, g))
    hsp = pl.BlockSpec((None, ncb, SSD_STATE, W), lambda g, i: (g, tmap(i), 0, 0))
    const = lambda a: pl.BlockSpec(a.shape, lambda g, i: (0, 0))
    return xsp, bsp, csp, rsp, dsp, hsp, const


def _tile_rows(a, n):
    return jnp.concatenate([a] * n, axis=0)


def _ssd2_fwd(xc, dt4, a4, dtw, aw, d4, consts, *, d_inner, name, carry=None):
    carry = carry or _NO_CARRY
    T = xc.shape[0]
    G, nc, W = dtw.shape
    hpg = W // CHUNK
    tb = min(SSD2_TB, T)
    nb, ncb = T // tb, tb // CHUNK
    b_off = d_inner // SSD_STATE
    xsp, bsp, csp, rsp, dsp, hsp, const = _ssd2_specs(G, hpg, tb, lambda i: i, b_off, b_off + G)
    tril, mask4, bd, triu_bd, _ = consts

    def body(*refs):
        own, c_in, c_out, c_sems = _carry_split(carry, refs, 12, 2)
        x_ref, b_ref, c_ref, dt_ref, a_ref, dtw_ref, aw_ref, d_ref, tril_ref, mask_ref, bd_ref, tbd_ref, y_ref, hs_ref, h_scr = own
        _carry_start(carry, c_in, c_out, c_sems, (pl.program_id(0) == 0) & (pl.program_id(1) == 0))

        @pl.when(pl.program_id(1) == 0)
        def _():
            h_scr[...] = jnp.zeros_like(h_scr)

        acs_rows = _doth(aw_ref[...], tbd_ref[...])
        ht = h_scr[...]
        for c in range(ncb):
            rows = slice(c * CHUNK, (c + 1) * CHUNK)
            x, bm, cm = x_ref[rows, :], b_ref[rows, :], c_ref[rows, :]
            acs = _doth(tril_ref[...], a_ref[rows, :], sel="a")
            lmat = jnp.where(mask_ref[...] > 0, jnp.exp(jnp.minimum(acs - acs_rows[c:c + 1, :], 0.0)), 0.0)
            m4 = _dotf(cm, _tile_rows(bm, hpg), _NT) * lmat * dtw_ref[c:c + 1, :]
            xbd = _tile_rows(x, hpg) * bd_ref[...]
            hs_ref[c] = ht
            y_ref[rows, :] = _dotf(m4, xbd, _NN) + _dotf(cm, ht, _NN) * jnp.exp(acs) + d_ref[...] * x
            a_last = acs[CHUNK - 1:CHUNK, :]
            xw = x * (jnp.exp(a_last - acs) * dt_ref[rows, :])
            ht = ht * jnp.exp(a_last) + _dotf(bm, xw, _TN)
        h_scr[...] = ht
        _carry_finish(carry, c_in, c_out, c_sems, (pl.program_id(0) == G - 1) & (pl.program_id(1) == nb - 1))

    res = pl.pallas_call(
        body, name=name, grid=(G, nb),
        in_specs=[xsp, bsp, csp, xsp, xsp, rsp, rsp, dsp, const(tril), const(mask4), const(bd), const(triu_bd)] + [_ANY] * len(carry.ins),
        out_specs=[xsp, hsp] + [_ANY] * len(carry.out_shapes),
        out_shape=[_S((T, G * W), f32), _S((G, nc, SSD_STATE, W), f32)] + list(carry.out_shapes),
        scratch_shapes=[pltpu.VMEM((SSD_STATE, W), f32)] + list(carry.sems),
        compiler_params=_cparams(("arbitrary", "arbitrary") if carry.ins else ("parallel", "arbitrary")),
    )(xc, xc, xc, dt4, a4, dtw, aw, d4, tril, mask4, bd, triu_bd, *carry.ins)
    return res[0], res[1], list(res[2:])


def _ssd2_bwd(xc, dt4, a4, dtw, aw, d4, consts, hs, dy, *, d_inner, name, carry=None):
    T = xc.shape[0]
    G, nc, W = dtw.shape
    hpg = W // CHUNK
    tb = min(SSD2_TB, T)
    nb, ncb = T // tb, tb // CHUNK
    b_off = d_inner // SSD_STATE
    tmap = lambda i: nb - 1 - i
    xsp, bsp, csp, rsp, dsp, hsp, const = _ssd2_specs(G, hpg, tb, tmap, b_off, b_off + G)
    gsp = pl.BlockSpec((tb, SSD_STATE), lambda g, i: (tmap(i), g))
    ddsp = pl.BlockSpec((None, 1, LANES), lambda g, i: (g, 0, 0))
    tril, mask4, bd, triu_bd, ered = consts

    carry = carry or _NO_CARRY

    def body(*refs):
        own, c_in, c_out, c_sems = _carry_split(carry, refs, 15, 8)
        (x_ref, b_ref, c_ref, dt_ref, a_ref, dtw_ref, aw_ref, d_ref, tril_ref, mask_ref, bd_ref, tbd_ref, er_ref, hs_ref, dy_ref,
         dx_ref, db_ref, dc_ref, ddtc_ref, dac_ref, ddtw_ref, daw_ref, dd_ref, g_scr, dd_scr, rw_scr, tl_scr) = own
        _carry_start(carry, c_in, c_out, c_sems, (pl.program_id(0) == 0) & (pl.program_id(1) == 0))
        first = pl.program_id(1) == 0

        @pl.when(first)
        def _():
            g_scr[...] = jnp.zeros_like(g_scr)
            dd_scr[...] = jnp.zeros_like(dd_scr)

        mask = mask_ref[...] > 0
        lane_in_block = lax.broadcasted_iota(jnp.int32, mask.shape, 1) & (CHUNK - 1)
        maskt = lax.broadcasted_iota(jnp.int32, mask.shape, 0) <= lane_in_block
        acs_rows = _doth(aw_ref[...], tbd_ref[...])
        dht = g_scr[...]
        dd = dd_scr[...]
        for c in range(ncb - 1, -1, -1):
            rows = slice(c * CHUNK, (c + 1) * CHUNK)
            x, bm, cm, dyc = x_ref[rows, :], b_ref[rows, :], c_ref[rows, :], dy_ref[rows, :]
            dtc, dtr = dt_ref[rows, :], dtw_ref[c:c + 1, :]
            ht = hs_ref[c]
            acs = _doth(tril_ref[...], a_ref[rows, :], sel="a")
            seg = acs - acs_rows[c:c + 1, :]
            lmat = jnp.where(mask, jnp.exp(jnp.minimum(seg, 0.0)), 0.0)
            lmat_t = jnp.where(maskt, jnp.exp(jnp.minimum(-seg, 0.0)), 0.0)
            btile, ctile = _tile_rows(bm, hpg), _tile_rows(cm, hpg)
            g4 = _dotf(cm, btile, _NT)
            gt4 = _dotf(bm, ctile, _NT)
            m4 = g4 * lmat * dtr
            mt4 = gt4 * lmat_t * dtc
            xbd = _tile_rows(x, hpg) * bd_ref[...]
            dybd = _tile_rows(dyc, hpg) * bd_ref[...]
            dm4 = _dotf(dyc, xbd, _NT)
            dmt4 = _dotf(x, dybd, _NT)
            dx = d_ref[...] * dyc + _dotf(mt4, dybd, _NN)
            dd = dd + jnp.sum(dyc * x, axis=0, keepdims=True)
            e4 = dm4 * m4
            dc = _dotf(dm4 * lmat * dtr, btile, _NN)
            db = _dotf(dmt4 * lmat_t * dtc, ctile, _NN)
            decay = jnp.exp(acs)
            yoff = _dotf(cm, ht, _NN) * decay
            dz = dyc * decay
            dc = dc + _dotf(dz, ht, _NT)
            dht_prev = _dotf(cm, dz, _TN)
            a_last = acs[CHUNK - 1:CHUNK, :]
            ea_last = jnp.exp(a_last)
            erel = jnp.exp(a_last - acs)
            dte = erel * dtc
            dxw = _dotf(bm, dht, _NN)
            db = db + _dotf(x * dte, dht, _NT)
            dx = dx + dxw * dte
            q4 = dxw * x
            dacs = e4 + dyc * yoff - q4 * dte
            col = jnp.concatenate([q4 * erel, _doth(tril_ref[...], dacs, _TN, sel="a")], axis=0)
            col = _doth(col, er_ref[...], parts=2)
            ddtc_ref[rows, :] = col[:CHUNK]
            dac_ref[rows, :] = col[CHUNK:]
            ddtw_ref[c:c + 1, :] = jnp.sum(dm4 * g4 * lmat, axis=0, keepdims=True)
            rw_scr[c:c + 1, :] = -jnp.sum(e4, axis=0, keepdims=True)
            tl_scr[c:c + 1, :] = jnp.sum(q4 * dte, axis=0, keepdims=True) + ea_last * jnp.sum(dht * ht, axis=0, keepdims=True)
            dx_ref[rows, :] = dx
            db_ref[rows, :] = db
            dc_ref[rows, :] = dc
            dht = dht_prev + dht * ea_last
        daw_ref[...] = _doth(rw_scr[...], tbd_ref[...], _NT) + _doth(tl_scr[...], bd_ref[...])
        g_scr[...] = dht
        dd_scr[...] = dd

        @pl.when(pl.program_id(1) == nb - 1)
        def _():
            dd_ref[...] = _doth(dd, er_ref[...])

        _carry_finish(carry, c_in, c_out, c_sems, (pl.program_id(0) == G - 1) & (pl.program_id(1) == nb - 1))

    res = pl.pallas_call(
        body, name=name, grid=(G, nb),
        in_specs=[xsp, bsp, csp, xsp, xsp, rsp, rsp, dsp, const(tril), const(mask4), const(bd), const(triu_bd), const(ered), hsp, xsp]
        + [_ANY] * len(carry.ins),
        out_specs=[xsp, gsp, gsp, gsp, gsp, rsp, rsp, ddsp] + [_ANY] * len(carry.out_shapes),
        out_shape=[_S((T, G * W), f32), _S((T, G * SSD_STATE), f32), _S((T, G * SSD_STATE), f32), _S((T, G * LANES), f32),
                   _S((T, G * LANES), f32), _S(dtw.shape, f32), _S(dtw.shape, f32), _S((G, 1, LANES), f32)] + list(carry.out_shapes),
        scratch_shapes=[pltpu.VMEM((SSD_STATE, W), f32), pltpu.VMEM((1, W), f32), pltpu.VMEM((ncb, W), f32), pltpu.VMEM((ncb, W), f32)]
        + list(carry.sems),
        compiler_params=_cparams(("arbitrary", "arbitrary") if carry.ins else ("parallel", "arbitrary")),
    )(xc, xc, xc, dt4, a4, dtw, aw, d4, tril, mask4, bd, triu_bd, ered, hs, dy, *carry.ins)
    return (*res[:8], list(res[8:]))


def _peers():
    x, y, c = lax.axis_index("x"), lax.axis_index("y"), lax.axis_index("c")
    return x, y, c


_ANY = pl.BlockSpec(memory_space=pl.ANY)
N_CHIP = N_DEV // 2


def _all_gather(shards, *, name):
    n = len(shards)
    carry = _carry_gather(shards)

    def body(*refs):
        x_refs, out_refs, sems = refs[:n], refs[n:2 * n], refs[2 * n:]
        _gather_start(x_refs, out_refs, sems)
        _gather_finish(x_refs, out_refs, sems)

    return pl.pallas_call(
        body, name=name, out_shape=list(carry.out_shapes), in_specs=[_ANY] * n, out_specs=[_ANY] * n, scratch_shapes=list(carry.sems),
    )(*shards)


def _gather_parts(x_refs, out_refs, sems):
    send_sems, recv_sems, local_sems = sems
    x, y, c = _peers()
    me, sibling = (x, y, c), (x, y, 1 - c)
    chips = [(1 - x, y), (x, 1 - y), (1 - x, 1 - y)]
    n = len(x_refs)

    def copy(a, r, block, to, src=None):
        px, py, pc = block
        slot = out_refs[a].at[4 * px + 2 * py + pc]
        return pltpu.make_async_remote_copy(
            src_ref=slot if src is None else src, dst_ref=slot, send_sem=send_sems.at[7 * a + r],
            recv_sem=recv_sems.at[7 * a + r], device_id=to, device_id_type=MESH)

    mine = [pltpu.make_async_copy(x_refs[a], out_refs[a].at[4 * x + 2 * y + c], local_sems.at[a]) for a in range(n)]
    first = []
    for a in range(n):
        first.append(copy(a, 0, me, sibling, src=x_refs[a]))
        first += [copy(a, 1 + j, me, (*chip, c), src=x_refs[a]) for j, chip in enumerate(chips)]
    return copy, mine, first, me, sibling, chips, c, n


def _gather_start(x_refs, out_refs, sems):
    _, mine, first, *_ = _gather_parts(x_refs, out_refs, sems)
    for cp in mine + first:
        cp.start()


def _gather_finish(x_refs, out_refs, sems):
    copy, mine, first, me, sibling, chips, c, n = _gather_parts(x_refs, out_refs, sems)
    passed = []
    for j, chip in enumerate(chips):
        for a in range(n):
            copy(a, 1 + j, (*chip, c), me).wait_recv()
            fwd = copy(a, 4 + j, (*chip, c), sibling)
            fwd.start()
            passed.append(fwd)
    for a in range(n):
        copy(a, 0, sibling, me).wait_recv()
    for j, chip in enumerate(chips):
        for a in range(n):
            copy(a, 4 + j, (*chip, 1 - c), me).wait_recv()
    for cp in first + passed:
        cp.wait_send()
    for cp in mine:
        cp.wait()


def _exchange_sibling(slots, *, name):
    n = len(slots)
    carry = _carry_sibling(slots)

    def body(*refs):
        carry.start(refs[:n], refs[n:2 * n], refs[2 * n:])
        carry.finish(refs[:n], refs[n:2 * n], refs[2 * n:])

    return list(pl.pallas_call(
        body, name=name, out_shape=list(carry.out_shapes), in_specs=[_ANY] * n, out_specs=[_ANY] * n, scratch_shapes=list(carry.sems),
    )(*slots))


def _sibling_copies(x_refs, sib_refs, send_sems, recv_sems):
    x, y, c = _peers()
    return [pltpu.make_async_remote_copy(
        src_ref=x_refs[a].at[pl.ds(N_CHIP * (1 - c), N_CHIP)], dst_ref=sib_refs[a], send_sem=send_sems.at[a],
        recv_sem=recv_sems.at[a], device_id=(x, y, 1 - c), device_id_type=MESH) for a in range(len(x_refs))]


def _carry_sibling(slots):
    n = len(slots)
    return _Carry(tuple(slots), tuple(_S((N_CHIP,) + s.shape[1:], s.dtype) for s in slots),
                  (pltpu.SemaphoreType.DMA((n,)), pltpu.SemaphoreType.DMA((n,))),
                  lambda i, o, s: _start_all(_sibling_copies(i, o, *s)), lambda i, o, s: _wait_all(_sibling_copies(i, o, *s)))


def _chip_sum(slots, sib, core, *, name):
    _, R, W = slots.shape
    tr = _pick(R, max(16, EW_BLOCK_ELEMS // W), align=16)

    def body(core_ref, x_ref, s_ref, o_ref):
        o_ref[...] = (x_ref[...].astype(f32) + s_ref[...].astype(f32)).astype(o_ref.dtype)

    blk = pl.BlockSpec((None, tr, W), lambda t, i, core_ref: (t, i, 0))
    return pl.pallas_call(
        body, name=name, out_shape=_S(sib.shape, slots.dtype),
        grid_spec=pltpu.PrefetchScalarGridSpec(
            num_scalar_prefetch=1, grid=(N_CHIP, R // tr),
            in_specs=[pl.BlockSpec((None, tr, W), lambda t, i, core_ref: (N_CHIP * core_ref[0] + t, i, 0)), blk], out_specs=blk),
        compiler_params=_cparams(("parallel", "parallel")),
    )(core, slots, sib)


def _chip_out_shapes(parts):
    return [_S((N_CHIP - 1,) + p.shape[1:], p.dtype) for p in parts]


def _chip_sems(n):
    return [pltpu.SemaphoreType.DMA((3 * n,)), pltpu.SemaphoreType.DMA((3 * n,))]


def _chip_copies(p_refs, out_refs, send_sems, recv_sems):
    x, y, c = _peers()
    copies = []
    for j in range(1, N_CHIP):
        tx, ty = x ^ (j >> 1), y ^ (j & 1)
        for a in range(len(p_refs)):
            copies.append(pltpu.make_async_remote_copy(
                src_ref=p_refs[a].at[2 * tx + ty], dst_ref=out_refs[a].at[j - 1], send_sem=send_sems.at[3 * a + j - 1],
                recv_sem=recv_sems.at[3 * a + j - 1], device_id=(tx, ty, c), device_id_type=MESH))
    return copies


def _start_all(copies):
    for cp in copies:
        cp.start()


def _wait_all(copies):
    for cp in copies:
        cp.wait_recv()
    for cp in copies:
        cp.wait_send()


class _Carry(NamedTuple):
    ins: tuple = ()
    out_shapes: tuple = ()
    sems: tuple = ()
    start: Callable = None
    finish: Callable = None


_NO_CARRY = _Carry()


def _carry_chips(parts):
    return _Carry(tuple(parts), tuple(_chip_out_shapes(parts)), tuple(_chip_sems(len(parts))),
                  lambda i, o, s: _start_all(_chip_copies(i, o, *s)), lambda i, o, s: _wait_all(_chip_copies(i, o, *s)))


def _carry_gather(shards):
    n = len(shards)
    sems = (pltpu.SemaphoreType.DMA((7 * n,)), pltpu.SemaphoreType.DMA((7 * n,)), pltpu.SemaphoreType.DMA((n,)))
    return _Carry(tuple(shards), tuple(_S((N_DEV,) + s.shape, s.dtype) for s in shards), sems, _gather_start, _gather_finish)


def _carry_split(carry, refs, n_in, n_out):
    ci, co, cs = len(carry.ins), len(carry.out_shapes), len(carry.sems)
    refs = list(refs)
    own_in, c_in = refs[:n_in], refs[n_in:n_in + ci]
    own_out, c_out = refs[n_in + ci:n_in + ci + n_out], refs[n_in + ci + n_out:n_in + ci + n_out + co]
    rest = refs[n_in + ci + n_out + co:]
    own_scratch, c_sems = rest[:len(rest) - cs], rest[len(rest) - cs:]
    return own_in + own_out + own_scratch, c_in, c_out, c_sems


def _carry_start(carry, c_in, c_out, c_sems, first):
    if carry.ins:
        @pl.when(first)
        def _():
            carry.start(c_in, c_out, c_sems)


def _carry_finish(carry, c_in, c_out, c_sems, last):
    if carry.ins:
        @pl.when(last)
        def _():
            carry.finish(c_in, c_out, c_sems)


def _sum_slots(stack, *, name):
    n, R, W = stack.shape
    tr = _pick(R, 1024, align=8)

    def body(s_ref, o_ref):
        acc = s_ref[0]
        for k in range(1, n):
            acc = acc + s_ref[k]
        o_ref[...] = acc

    return pl.pallas_call(
        body, name=name, grid=(R // tr,), in_specs=[pl.BlockSpec((n, tr, W), lambda i: (0, i, 0))],
        out_specs=pl.BlockSpec((tr, W), lambda i: (i, 0)), out_shape=_S((R, W), f32), compiler_params=_cparams(("parallel",)),
    )(stack)


def _adamw_math(gv, wv, mv, vv):
    c1 = 1.0 / (1.0 - ADAM_B1 ** ADAM_STEP)
    c2 = 1.0 / (1.0 - ADAM_B2 ** ADAM_STEP)
    nm = ADAM_B1 * mv + (1.0 - ADAM_B1) * gv
    nv = ADAM_B2 * vv + (1.0 - ADAM_B2) * jnp.square(gv)
    return -ADAM_LR * ((nm * c1) / (jnp.sqrt(nv * c2) + ADAM_EPS) + ADAM_WD * wv), nm, nv


def _adamw(g, w, m, v, *, name):
    R, W = w.shape
    tr = _pick(R, max(8, (1 << 20) // (4 * W)), align=8)

    def body(g_ref, w_ref, m_ref, v_ref, d_ref, nm_ref, nv_ref):
        d_ref[...], nm_ref[...], nv_ref[...] = _adamw_math(g_ref[...], w_ref[...], m_ref[...], v_ref[...])

    sp = pl.BlockSpec((tr, W), lambda i: (i, 0))
    return pl.pallas_call(
        body, name=name, grid=(R // tr,), in_specs=[sp] * 4, out_specs=[sp] * 3, out_shape=[_S((R, W), f32)] * 3,
        compiler_params=_cparams(("parallel",)),
    )(g, w, m, v)


def _reduce_adamw(own, arrived, chip, w, m, v, *, name):
    n, R, W = arrived.shape
    tr = _pick(R, max(16, EW_BLOCK_ELEMS // (4 * W)), align=16)

    def body(chip_ref, o_ref, p_ref, w_ref, m_ref, v_ref, g_ref, d_ref, nm_ref, nv_ref):
        gv = o_ref[...].astype(f32)
        for k in range(n):
            gv = gv + p_ref[k].astype(f32)
        g_ref[...] = gv
        d_ref[...], nm_ref[...], nv_ref[...] = _adamw_math(gv, w_ref[...], m_ref[...], v_ref[...])

    sp = pl.BlockSpec((tr, W), lambda i, chip_ref: (i, 0))
    return pl.pallas_call(
        body, name=name, out_shape=[_S((R, W), f32)] * 4,
        grid_spec=pltpu.PrefetchScalarGridSpec(
            num_scalar_prefetch=1, grid=(R // tr,),
            in_specs=[pl.BlockSpec((None, tr, W), lambda i, chip_ref: (chip_ref[0], i, 0)),
                      pl.BlockSpec((n, tr, W), lambda i, chip_ref: (0, i, 0))] + [sp] * 3, out_specs=[sp] * 4),
        compiler_params=_cparams(("parallel",)),
    )(chip, own, arrived, w, m, v)


def _pieces(seg_start, seg_len, shard_w):
    out, col = [], seg_start
    while col < seg_start + seg_len:
        k, a = divmod(col, shard_w)
        n = min(shard_w - a, seg_start + seg_len - col)
        out.append((k, a, col - seg_start, n))
        col += n
    return out


def _unshard_w_in(g, seg_lens, *, name):
    _, D, w = g.shape
    starts = [sum(seg_lens[:i]) for i in range(len(seg_lens))]
    widths = [max(n, LANES) for n in seg_lens]
    tm = _pick(D, 512, align=16)

    def body(g_ref, *o_refs):
        for o_ref, s0, n in zip(o_refs, starts, seg_lens):
            if n < o_ref.shape[1]:
                o_ref[...] = jnp.zeros_like(o_ref)
            for k, a, off, m in _pieces(s0, n, w):
                o_ref[:, off:off + m] = g_ref[k, :, a:a + m]

    return pl.pallas_call(
        body, name=name, grid=(D // tm,), in_specs=[pl.BlockSpec((N_DEV, tm, w), lambda i: (0, i, 0))],
        out_specs=[pl.BlockSpec((tm, wd), lambda i: (i, 0)) for wd in widths], out_shape=[_S((D, wd), g.dtype) for wd in widths],
        compiler_params=_cparams(("parallel",)),
    )(g)


def _unshard_pair(g1, g2, *, name):
    _, D, w = g1.shape
    tm = _pick(D, 512, align=16)

    def body(a_ref, b_ref, o_ref):
        for i, g_ref in enumerate((a_ref, b_ref)):
            for k in range(N_DEV):
                off = (i * N_DEV + k) * w
                o_ref[:, off:off + w] = g_ref[k]

    blk = pl.BlockSpec((N_DEV, tm, w), lambda i: (0, i, 0))
    return pl.pallas_call(
        body, name=name, grid=(D // tm,), in_specs=[blk, blk], out_specs=pl.BlockSpec((tm, 2 * N_DEV * w), lambda i: (i, 0)),
        out_shape=_S((D, 2 * N_DEV * w), g1.dtype), compiler_params=_cparams(("parallel",)),
    )(g1, g2)


def _reshard_pair(dw, *, name):
    D, w = dw.shape[0], dw.shape[1] // (2 * N_DEV)
    tm = _pick(D, 256, align=16)

    def body(g_ref, a_ref, b_ref):
        for i, o_ref in enumerate((a_ref, b_ref)):
            for k in range(N_DEV):
                off = (i * N_DEV + k) * w
                o_ref[_slot_of(k)] = g_ref[:, off:off + w].astype(o_ref.dtype)

    blk = pl.BlockSpec((N_DEV, tm, w), lambda i: (0, i, 0))
    return pl.pallas_call(
        body, name=name, grid=(D // tm,), in_specs=[pl.BlockSpec((tm, dw.shape[1]), lambda i: (i, 0))], out_specs=[blk, blk],
        out_shape=[_S((N_DEV, D, w), bf16)] * 2, compiler_params=_cparams(("parallel",)),
    )(dw)


def _reshard_w_in(grads, seg_lens, w, *, name):
    D = grads[0].shape[0]
    starts = [sum(seg_lens[:i]) for i in range(len(seg_lens))]
    tm = _pick(D, 256, align=16)

    def body(*refs):
        o_ref = refs[-1]
        for g_ref, s0, n in zip(refs[:-1], starts, seg_lens):
            for k, a, off, m in _pieces(s0, n, w):
                o_ref[_slot_of(k), :, a:a + m] = g_ref[:, off:off + m].astype(o_ref.dtype)

    return pl.pallas_call(
        body, name=name, grid=(D // tm,), in_specs=[pl.BlockSpec((tm, g.shape[1]), lambda i: (i, 0)) for g in grads],
        out_specs=pl.BlockSpec((N_DEV, tm, w), lambda i: (0, i, 0)), out_shape=_S((N_DEV, D, w), bf16),
        compiler_params=_cparams(("parallel",)),
    )(*grads)


def _pad_flat(a, mult):
    a = a.reshape(-1)
    n = -(-a.shape[0] // mult) * mult
    return a if n == a.shape[0] else jnp.pad(a, (0, n - a.shape[0]))


def _pad_cols(a, mult):
    n = -(-a.shape[1] // mult) * mult
    return a if n == a.shape[1] else jnp.pad(a, ((0, 0), (0, n - a.shape[1])))


def _block_diag(t):
    nblk, g, P, Q = t.shape
    eye = jnp.eye(g, dtype=t.dtype)
    return (t[:, :, :, None, :] * eye[None, :, None, :, None]).reshape(nblk, g * P, g * Q)


def _block_diag_t(w, P, Q):
    nblk = w.shape[0]
    g = w.shape[1] // P
    eye = jnp.eye(g, dtype=w.dtype)
    return (w.reshape(nblk, g, P, g, Q) * eye[None, :, None, :, None]).sum(axis=3)


_COLS = ("ffn1_w_gate", "ffn1_w_up", "ffn2_w_gate", "ffn2_w_up")
_ROWS = ("ffn1_w_down", "ffn2_w_down", "s5_w_glu", "w_proj_s5", "w_out", "w_proj_ssd")
_BIG = _COLS + _ROWS + ("w_in", "conv_w")
_SMALL = ("ffn1_norm", "mix_norm", "conv_b", "s5_A_re", "s5_A_im", "s5_log_dt", "s5_B_re", "s5_B_im", "s5_C_re", "s5_C_im",
          "s5_D", "s5_b_glu", "ssd_A_log", "ssd_dt_bias", "ssd_D", "ssd_norm", "b_gate", "ffn2_norm", "final_norm")
_WEIGHTS = ("ffn1_norm", "ffn1_w_gate", "ffn1_w_up", "ffn1_w_down", "mix_norm", "w_in", "conv_w", "conv_b", "s5_A_re", "s5_A_im",
            "s5_log_dt", "s5_B_re", "s5_B_im", "s5_C_re", "s5_C_im", "s5_D", "s5_w_glu", "s5_b_glu", "ssd_A_log", "ssd_dt_bias",
            "ssd_D", "ssd_norm", "w_proj_s5", "w_proj_ssd", "b_gate", "w_out", "ffn2_norm", "ffn2_w_gate", "ffn2_w_up",
            "ffn2_w_down", "final_norm")
def _with_carry(res, carry):
    return res if carry else (res, [])


def _ffn_fwd(x, n, wgu, wd_of, tag, carries=(None, None, None), h=None):
    D = x.shape[1]
    if h is None:
        h = _rows(_f_rmsnorm, [x], [n], [(D, bf16)], name=tag + "_norm")[0]
    ab, got0 = _with_carry(_mm(h, wgu, carry=carries[0], name=tag + "_gate_up"), carries[0])
    wd = wd_of(got0)
    F = wd.shape[0]
    (c,), got1 = _with_carry(_rows(_f_swiglu, [ab], [], [(F, bf16)], carry=carries[1], name=tag + "_act"), carries[1])
    y, got2 = _with_carry(_mm(c, wd, scale=0.5, add=x, carry=carries[2], name=tag + "_down"), carries[2])
    return y, (x, n, h, ab, c), (got0, got1, got2)


def _ffn_bwd(saved, wgu, wd, dy, tag, carry_after_dwd=None, carry_after_dwgu=None):
    x, n, h, ab, c = saved
    F = wd.shape[0]
    def act_bwd(dc, ab_):
        return jax.vjp(lambda t: _f_swiglu(t)[0], ab_)[1](dc)[0]

    dab = _mm(dy, wd, tb=True, scale=0.5, tm=EPI_TM, epi=(act_bwd, ab, 2 * F), out_dtype=bf16, name=tag + "_d_act")
    dwd = _mm(c, dy, ta=True, o_blk="m", o_slots=True, tm=F // 2, out_dtype=bf16, scale=0.5, name=tag + "_d_wdown")
    carry_w = carry_after_dwd(dwd) if carry_after_dwd else None
    dwgu, arr_w = _with_carry(_mm(h, dab, ta=True, carry=carry_w, name=tag + "_d_wgu"), carry_w)
    dwg, dwu = _reshard_pair(dwgu, name=tag + "_reshard_d_wgu")
    carry_h = carry_after_dwgu(dwg, dwu) if carry_after_dwgu else None
    dh, arr_h = _with_carry(_mm(dab, wgu, tb=True, carry=carry_h, name=tag + "_d_h"), carry_h)
    (dx,), (dn,) = _rows_bwd(_f_rmsnorm, [x], [n], [dh], name=tag + "_norm_bwd", want_rows=[0], adds={0: dy})
    return dx, dn, dwg, dwu, dwd, arr_w, arr_h


def kernel(x, ffn1_norm, ffn1_w_gate, ffn1_w_up, ffn1_w_down, mix_norm, w_in, conv_w, conv_b, s5_A_re, s5_A_im, s5_log_dt, s5_B_re, s5_B_im, s5_C_re, s5_C_im, s5_D, s5_w_glu, s5_b_glu, ssd_A_log, ssd_dt_bias, ssd_D, ssd_norm, w_proj_s5, w_proj_ssd, b_gate, w_out, ffn2_norm, ffn2_w_gate, ffn2_w_up, ffn2_w_down, final_norm, loss_target, m_ffn1_norm, m_ffn1_w_gate, m_ffn1_w_up, m_ffn1_w_down, m_mix_norm, m_w_in, m_conv_w, m_conv_b, m_s5_A_re, m_s5_A_im, m_s5_log_dt, m_s5_B_re, m_s5_B_im, m_s5_C_re, m_s5_C_im, m_s5_D, m_s5_w_glu, m_s5_b_glu, m_ssd_A_log, m_ssd_dt_bias, m_ssd_D, m_ssd_norm, m_w_proj_s5, m_w_proj_ssd, m_b_gate, m_w_out, m_ffn2_norm, m_ffn2_w_gate, m_ffn2_w_up, m_ffn2_w_down, m_final_norm, v_ffn1_norm, v_ffn1_w_gate, v_ffn1_w_up, v_ffn1_w_down, v_mix_norm, v_w_in, v_conv_w, v_conv_b, v_s5_A_re, v_s5_A_im, v_s5_log_dt, v_s5_B_re, v_s5_B_im, v_s5_C_re, v_s5_C_im, v_s5_D, v_s5_w_glu, v_s5_b_glu, v_ssd_A_log, v_ssd_dt_bias, v_ssd_D, v_ssd_norm, v_w_proj_s5, v_w_proj_ssd, v_b_gate, v_w_out, v_ffn2_norm, v_ffn2_w_gate, v_ffn2_w_up, v_ffn2_w_down, v_final_norm):
    P = dict(locals())
    T, D = x.shape[1], x.shape[2]
    x0, tgt = x[0], loss_target[0]
    sh = {k: P[k][0] for k in _BIG}

    send = {k: (sh[k] if k == "conv_w" else sh[k].astype(bf16)) for k in _BIG}
    W = {}

    def gather_in(keys):
        return _carry_gather([send[k] for k in keys])

    first_keys = ("ffn1_w_gate", "ffn1_w_up", "conv_w")
    (h1,), got = _rows(_f_rmsnorm, [x0], [ffn1_norm], [(D, bf16)], carry=gather_in(first_keys), name="ffn1_norm")
    W.update(zip(first_keys, got))
    whole = lambda k: W[k].reshape(-1, D)
    conv_w_full = W["conv_w"].transpose(1, 0, 2).reshape(CONV_K, -1)

    d_inner = N_DEV * sh["w_proj_ssd"].shape[0]
    conv_dim = conv_w_full.shape[1]
    H = ssd_A_log.shape[1]
    G = (conv_dim - d_inner) // (2 * SSD_STATE)
    hpg = H // G
    nc = T // CHUNK
    Gs = D // S5_GROUP
    nblk = Gs // S5_GPB
    NS = Gs * S5_STATE
    seg_lens = (D, d_inner, conv_dim, H, 2 * D)

    cuts = [0, D // 3 // 16 * 16, D // 3 // 16 * 16 + 3 * D // 8 // 16 * 16, D]
    win_rows = [send["w_in"][a_:b_] for a_, b_ in zip(cuts[:-1], cuts[1:])]
    wgu1 = _unshard_pair(W["ffn1_w_gate"], W["ffn1_w_up"], name="unshard_ffn1_gate_up")

    def ffn1_down(got):
        W["ffn1_w_down"] = got[0]
        return whole("ffn1_w_down")

    x1, sv1, (got0, got1, got2) = _ffn_fwd(
        x0, ffn1_norm, wgu1, ffn1_down, "ffn1", h=h1,
        carries=(_carry_gather([send["ffn1_w_down"], win_rows[0]]), _carry_gather([win_rows[1]]), _carry_gather([win_rows[2]])))
    ffn1_w = (wgu1, whole("ffn1_w_down"))
    W["w_in"] = jnp.concatenate([got0[1], got1[0], got2[0]], axis=1)
    w_u, w_z, w_xbc, w_dt, w_gl = _unshard_w_in(W["w_in"], seg_lens, name="unshard_w_in")
    h2 = _rows(_f_rmsnorm, [x1], [mix_norm], [(D, bf16)], name="mix_norm")[0]
    u_p = _mm(h2, w_u, o_seg=True, name="in_u")
    z = _mm(h2, w_z, name="in_z")
    xbc = _mm(h2, w_xbc, name="in_xbc")
    gl = _mm(h2, w_gl, name="in_gate")
    dtr = _mm(h2, w_dt, name="in_dt")

    rep = lambda a: jnp.repeat(a, S5_GROUP, axis=0)
    lr, li, ldt = s5_A_re[0], s5_A_im[0], s5_log_dt[0].reshape(Gs, 1)
    brt = s5_B_re[0].transpose(0, 2, 1).reshape(Gs * S5_GROUP, S5_STATE)
    bit = s5_B_im[0].transpose(0, 2, 1).reshape(Gs * S5_GROUP, S5_STATE)
    prep_args = (lr, li, ldt, rep(lr), rep(li), rep(ldt), brt, bit)
    ar, ai, bbrt, bbit = _s5_prep(prep_args, name="s5_prep")
    a_r, a_i = ar.reshape(1, NS), ai.reshape(1, NS)
    wb_r = _block_diag(bbrt.reshape(nblk, S5_GPB, S5_GROUP, S5_STATE)).astype(bf16)
    wb_i = _block_diag(bbit.reshape(nblk, S5_GPB, S5_GROUP, S5_STATE)).astype(bf16)
    c4r = s5_C_re[0].reshape(nblk, S5_GPB, S5_GROUP, S5_STATE).transpose(0, 1, 3, 2)
    c4i = s5_C_im[0].reshape(nblk, S5_GPB, S5_GROUP, S5_STATE).transpose(0, 1, 3, 2)
    wc_r, wc_i = _block_diag(c4r).astype(bf16), _block_diag(c4i).astype(bf16)
    mix_keys = ("s5_w_glu", "w_proj_s5", "w_proj_ssd", "w_out")
    sl_r, sl_i, p_r, p_i, got = _s5_local_scan(u_p, wb_r, wb_i, a_r, a_i, reverse=False, carry=gather_in(mix_keys), name="s5_scan")
    W.update(zip(mix_keys, got))
    w_glu, w_p5, w_pssd, w_o = whole("s5_w_glu"), whole("w_proj_s5"), whole("w_proj_ssd"), whole("w_out")
    c_r, c_i = _s5_carry(sl_r[T - NSEG:], sl_i[T - NSEG:], p_r, p_i, reverse=False, name="s5_carry")
    s_r, s_i, ylin = _s5_fix_out(sl_r, sl_i, a_r, a_i, c_r, c_i, wc_r, wc_i, name="s5_fix_out")
    g5 = _rows(_f_s5_post, [ylin, u_p], [s5_D], [(D, f32)], name="s5_gelu")[0]
    v5 = _mm(g5, w_glu, name="s5_glu_mm")
    o5 = _rows(_f_glu, [g5, v5], [s5_b_glu], [(D, bf16)], name="s5_glu")[0]
    p5 = _mm(o5, w_p5, a_seg=True, name="proj_s5")

    xc = _conv_fwd(xbc, conv_w_full, conv_b, name="conv")
    bias_p, alog_p = _pad_cols(ssd_dt_bias, LANES), _pad_cols(ssd_A_log, LANES)
    expand = (lax.broadcasted_iota(jnp.int32, (LANES, d_inner), 1) // HEADDIM
              == lax.broadcasted_iota(jnp.int32, (LANES, d_inner), 0)).astype(f32)
    dt_p, da_p, dt4, a4 = _rows(_f_dt_expand, [dtr], [bias_p, alog_p, expand],
                                [(LANES, f32), (LANES, f32), (d_inner, f32), (d_inner, f32)], name="ssd_dt")
    row_l = lambda a: a[:, :H].reshape(nc, CHUNK, G, hpg).transpose(2, 0, 3, 1).reshape(G, nc, hpg * CHUNK)
    ssd_in = (xc, dt4, a4, row_l(dt_p), row_l(da_p), jnp.repeat(ssd_D, HEADDIM, axis=1), _ssd_consts(hpg))
    ffn2_keys = ("ffn2_w_gate", "ffn2_w_up", "ffn2_w_down")
    y_ssd, hs, got = _ssd2_fwd(*ssd_in, d_inner=d_inner, carry=gather_in(ffn2_keys), name="ssd")
    W.update(zip(ffn2_keys, got))
    ffn2_w = (_unshard_pair(W["ffn2_w_gate"], W["ffn2_w_up"], name="unshard_ffn2_gate_up"), whole("ffn2_w_down"))
    yn = _rows(_f_gated_norm, [y_ssd, z], [ssd_norm], [(d_inner, bf16)], name="ssd_gated_norm")[0]
    pssd = _mm(yn, w_pssd, name="proj_ssd")

    merged = _rows(_f_merge, [gl, p5, pssd], [b_gate], [(D, bf16)], name="merge")[0]
    x2 = _mm(merged, w_o, add=x1, name="out_proj")
    x3, sv2, _ = _ffn_fwd(x2, ffn2_norm, ffn2_w[0], lambda _: ffn2_w[1], "ffn2")
    lossv, dx3, d_final = _loss_stage(x3, tgt, final_norm.reshape(1, D), name="loss")

    gw = {}
    gs = {"final_norm": d_final}
    slot_mm = lambda a_, b_, name, **kw: _mm(a_, b_, ta=True, o_blk="m", o_slots=True, out_dtype=bf16, name=name, **kw)
    core = lax.axis_index("c").astype(jnp.int32).reshape(1)
    chip = (2 * lax.axis_index("x") + lax.axis_index("y")).astype(jnp.int32).reshape(1)
    chip_sums, arrived = {}, {}

    def sums_of(keys, sib):
        for k, s_ in zip(keys, sib):
            chip_sums[k] = _chip_sum(gw[k], s_, core, name="chip_sum_" + k)
        return [chip_sums[k] for k in keys]

    def level1(keys, tag):
        return sums_of(keys, _exchange_sibling([gw[k] for k in keys], name="exchange_sibling_" + tag))

    dx2, gs["ffn2_norm"], gw["ffn2_w_gate"], gw["ffn2_w_up"], gw["ffn2_w_down"], _, _ = _ffn_bwd(sv2, *ffn2_w, dx3, "ffn2")

    dmerged = _mm(dx2, w_o, tb=True, name="d_merged")
    gw["w_out"] = slot_mm(merged, dx2, "d_w_out")
    (dgl, dp5, dpssd), (gs["b_gate"],) = _rows_bwd(_f_merge, [gl, p5, pssd], [b_gate], [dmerged], name="merge_bwd", want_rows=[0, 1, 2])

    dyn = _mm(dpssd, w_pssd, tb=True, name="d_yn")
    gw["w_proj_ssd"] = slot_mm(yn, dpssd, "d_w_proj_ssd")
    group_a = ("ffn2_w_gate", "ffn2_w_up", "ffn2_w_down", "w_out", "w_proj_ssd")
    (dyssd, dz), (gs["ssd_norm"],) = _rows_bwd(_f_gated_norm, [y_ssd, z], [ssd_norm], [dyn], name="ssd_gated_norm_bwd", want_rows=[0, 1])
    dxs, dbm, dcm, ddtc, ddac, ddtw, ddaw, ddh, sib_a = _ssd2_bwd(
        *ssd_in, hs, dyssd, d_inner=d_inner, carry=_carry_sibling([gw[k] for k in group_a]), name="ssd_bwd")
    parts_a = sums_of(group_a, sib_a)

    def fold(col, row):
        col = col.reshape(T, G, LANES)[:, :, :hpg].reshape(T, H)
        row = row.reshape(G, nc, hpg, CHUNK).transpose(1, 3, 0, 2).reshape(T, H)
        return _pad_cols(col + row, LANES)

    (ddtr,), (dbias_p, dalog_p) = _rows_bwd(_f_dt, [dtr], [bias_p, alog_p], [fold(ddtc, ddtw), fold(ddac, ddaw)], name="ssd_dt_bwd", want_rows=[0])
    gs["ssd_dt_bias"], gs["ssd_A_log"], gs["ssd_D"] = dbias_p[:, :H], dalog_p[:, :H], ddh[:, 0, :hpg].reshape(1, H)
    dxbc, d_conv_w, gs["conv_b"], arr = _conv_bwd(
        xbc, conv_w_full, conv_b, [dxs, dbm, dcm], carry=_carry_chips(parts_a), name="conv_bwd")
    arrived.update(zip(group_a, arr))
    cwk = sh["conv_w"].shape[1]
    gw["conv_w"] = d_conv_w.reshape(CONV_K, N_CHIP, 2, cwk).transpose(2, 1, 0, 3).reshape(N_DEV, CONV_K, cwk)

    do5 = _mm(dp5, w_p5, tb=True, o_seg=True, name="d_o5")
    gw["w_proj_s5"] = slot_mm(o5, dp5, "d_w_proj_s5", a_seg=True)
    (dg5a, dv5), (gs["s5_b_glu"],) = _rows_bwd(_f_glu, [g5, v5], [s5_b_glu], [do5], name="s5_glu_bwd", want_rows=[0, 1])
    dg5 = _mm(dv5, w_glu, tb=True, add=dg5a, name="d_g5")
    gw["s5_w_glu"] = slot_mm(g5, dv5, "d_w_glu")
    (dylin, du_a), (gs["s5_D"],) = _rows_bwd(_f_s5_post, [ylin, u_p], [s5_D], [dg5], name="s5_gelu_bwd", want_rows=[0, 1])
    wct_r, wct_i = wc_r.transpose(0, 2, 1), -wc_i.transpose(0, 2, 1)
    ql_r, ql_i, _, _ = _s5_local_scan(dylin, wct_r, wct_i, a_r, -a_i, reverse=True, powers=False, name="s5_scan_bwd")
    cb_r, cb_i = _s5_carry(ql_r[:NSEG], ql_i[:NSEG], p_r, -p_i, reverse=True, name="s5_carry_bwd")
    tc = min(S5_TC, T)

    def before_blocks(s):
        last = s.reshape(T // tc, tc, NS)[:, tc - NSEG:, :]
        wrap = jnp.concatenate([jnp.zeros((1, 1, NS), f32), last[-1:, : NSEG - 1, :]], axis=1)
        return jnp.concatenate([wrap, last[:-1]], axis=0)

    du_p, dwb_r, dwb_i, dwc_r, dwc_i, d_ar, d_ai = _s5_fix_bwd(
        ql_r, ql_i, a_r, -a_i, cb_r, cb_i, s_r, s_i, before_blocks(s_r), before_blocks(s_i), u_p, dylin, du_a, wb_r, wb_i, name="s5_fix_bwd")
    unblk = lambda w: _block_diag_t(w, S5_GROUP, S5_STATE).reshape(Gs * S5_GROUP, S5_STATE)
    rsum = jnp.repeat(jnp.eye(Gs, dtype=f32), S5_GROUP, axis=1)
    d_lr, d_li, d_ldt, d_brt, d_bit = _s5_prep_bwd(
        prep_args, (d_ar.reshape(Gs, S5_STATE), d_ai.reshape(Gs, S5_STATE), unblk(dwb_r), unblk(dwb_i)), rsum, name="s5_prep_bwd")
    gs["s5_A_re"], gs["s5_A_im"], gs["s5_log_dt"] = d_lr, d_li, d_ldt.reshape(1, Gs)
    gs["s5_B_re"] = d_brt.reshape(Gs, S5_GROUP, S5_STATE).transpose(0, 2, 1)
    gs["s5_B_im"] = d_bit.reshape(Gs, S5_GROUP, S5_STATE).transpose(0, 2, 1)
    gs["s5_C_re"] = _block_diag_t(dwc_r, S5_STATE, S5_GROUP).transpose(0, 1, 3, 2).reshape(Gs, S5_GROUP, S5_STATE)
    gs["s5_C_im"] = _block_diag_t(dwc_i, S5_STATE, S5_GROUP).transpose(0, 1, 3, 2).reshape(Gs, S5_GROUP, S5_STATE)

    d_w_in = [_mm(h2, du_p, ta=True, b_seg=True, name="d_w_u"), _mm(h2, dz, ta=True, name="d_w_z"), _mm(h2, dxbc, ta=True, name="d_w_xbc"),
              _mm(h2, ddtr, ta=True, name="d_w_dt"), _mm(h2, dgl, ta=True, name="d_w_gate")]
    gw["w_in"] = _reshard_w_in(d_w_in, seg_lens, sh["w_in"].shape[1], name="reshard_d_w_in")
    group_c = ("w_proj_s5", "s5_w_glu", "conv_w")
    keys_c = group_c + ("w_in",)
    dh2, sib_c = _mm(du_p, w_u, tb=True, a_seg=True, carry=_carry_sibling([gw[k] for k in keys_c]), name="d_h2_u")
    parts_c = sums_of(keys_c, sib_c)
    c1 = int(D * 0.45) // 16 * 16
    c2 = c1 + D // 4 // 16 * 16
    win = [parts_c[3][:, :c1], parts_c[3][:, c1:c2], parts_c[3][:, c2:]]
    dh2, arr = _mm(dz, w_z, tb=True, add=dh2, carry=_carry_chips(parts_c[:3]), name="d_h2_z")
    arrived.update(zip(group_c, arr))
    dh2, (arr0,) = _mm(dxbc, w_xbc, tb=True, add=dh2, carry=_carry_chips([win[0]]), name="d_h2_xbc")
    dh2, (arr1,) = _mm(dgl, w_gl, tb=True, add=dh2, carry=_carry_chips([win[1]]), name="d_h2_gate")
    dh2 = _mm(ddtr, w_dt, tb=True, add=dh2, name="d_h2_dt")
    (dx1,), (gs["mix_norm"],) = _rows_bwd(_f_rmsnorm, [x1], [mix_norm], [dh2], name="mix_norm_bwd", want_rows=[0], adds={0: dx2})

    def carry_ffn1_down(dwd):
        gw["ffn1_w_down"] = dwd
        return _carry_chips(level1(("ffn1_w_down",), "d") + [win[2]])

    def carry_ffn1_gate_up(dwg, dwu):
        gw["ffn1_w_gate"], gw["ffn1_w_up"] = dwg, dwu
        return _carry_chips(level1(("ffn1_w_gate", "ffn1_w_up"), "e"))

    dx0, gs["ffn1_norm"], _, _, _, arr_w, arr_h = _ffn_bwd(
        sv1, *ffn1_w, dx1, "ffn1", carry_after_dwd=carry_ffn1_down, carry_after_dwgu=carry_ffn1_gate_up)
    arrived["ffn1_w_down"], arr2 = arr_w
    arrived["ffn1_w_gate"], arrived["ffn1_w_up"] = arr_h
    arrived["w_in"] = jnp.concatenate([arr0, arr1, arr2], axis=1)

    small_shapes = {k: (P[k][0].shape if P[k].ndim > 1 else P[k].shape) for k in _SMALL}
    pack = lambda d: jnp.concatenate([_pad_flat(d[k], TILE_ELEMS) for k in _SMALL]).reshape(-1, LANES)
    gsmall = _sum_slots(_all_gather([pack(gs)], name="gather_small_grads")[0], name="sum_small_grads")
    snum = {k: math.prod(small_shapes[k]) for k in _SMALL}
    ssz = {k: -(-snum[k] // TILE_ELEMS) * TILE_ELEMS for k in _SMALL}

    grads, delta, new_m, new_v = {}, {}, {}, {}
    for k in _BIG:
        grads[k], delta[k], new_m[k], new_v[k] = _reduce_adamw(
            chip_sums[k], arrived[k], chip, P[k][0], P["m_" + k][0], P["v_" + k][0], name="adamw_" + k)
    d_s, m_s, v_s = _adamw(gsmall, pack({k: P[k] for k in _SMALL}), pack({k: P["m_" + k] for k in _SMALL}),
                           pack({k: P["v_" + k] for k in _SMALL}), name="adamw_small")
    off = 0
    gflat, dflat, mflat, vflat = gsmall.reshape(-1), d_s.reshape(-1), m_s.reshape(-1), v_s.reshape(-1)
    for k in _SMALL:
        n = snum[k]
        grads[k], delta[k], new_m[k], new_v[k] = (a[off:off + n] for a in (gflat, dflat, mflat, vflat))
        off += ssz[k]

    loss = lax.psum(lossv[0, 0], ("x", "y", "c"))
    out = [loss, dx0.reshape(x.shape)]
    for d in (grads, delta, new_m, new_v):
        out += [d[k].reshape(P[k].shape) for k in _WEIGHTS]
    return tuple(out)
```

```python
import math
from typing import Callable, NamedTuple

import jax
import jax.numpy as jnp
from jax import lax
from jax.experimental import pallas as pl
from jax.experimental.pallas import tpu as pltpu

f32 = jnp.float32
bf16 = jnp.bfloat16
_S = jax.ShapeDtypeStruct

EPS = 1e-6
S5_GROUP = 16
S5_STATE = 64
HEADDIM = 64
SSD_STATE = 128
CHUNK = 64
CONV_K = 4
NSEG = 8
S5_GPB = 16
N_DEV = 8
LANES = 128
TILE_ELEMS = 8 * LANES

ADAM_LR = 0.001
ADAM_B1 = 0.9
ADAM_B2 = 0.999
ADAM_EPS = 1e-08
ADAM_WD = 0.01
ADAM_STEP = 10

VMEM_LIMIT = 56 * 1024 * 1024
MM_FULL_K = 3072
MM_MAX_TN = 3072
EPI_TM = 256
ROW_BLOCK_BYTES = 12 * 1024 * 1024
EW_BLOCK_ELEMS = 1 << 20
MESH = pl.DeviceIdType.MESH


def _cparams(sem=None):
    return pltpu.CompilerParams(dimension_semantics=sem, vmem_limit_bytes=VMEM_LIMIT)


def _pick(dim, pref, align=LANES):
    best = None
    t = align
    while t <= min(dim, pref):
        if dim % t == 0:
            best = t
        t += align
    return best or dim


def _slot_of(k):
    return (k & 1) * (N_DEV // 2) + (k >> 1)


def _mm(a, b, *, name, ta=False, tb=False, a_blk=None, b_blk=None, o_blk=None, o_slots=False, a_seg=False, b_seg=False,
        o_seg=False, tm=None, out_dtype=f32, scale=1.0, add=None, epi=None, carry=None):
    a2, b2 = a.shape[-2:], b.shape[-2:]
    Ma, Ka = (a2[1], a2[0]) if ta else a2
    Kb, Nb = (b2[1], b2[0]) if tb else b2
    M = Ma * (a.shape[0] if a_blk == "m" else 1)
    K = Ka * (a.shape[0] if a_blk == "k" else 1)
    N = Nb * (b.shape[0] if b_blk == "n" else 1)
    assert K == Kb * (b.shape[0] if b_blk == "k" else 1), (a.shape, b.shape, ta, tb, a_blk, b_blk)
    assert (a.ndim == 3) == (a_blk is not None) and (b.ndim == 3) == (b_blk is not None)
    tm = Ma if a_blk == "m" else (tm or _pick(M, 512))
    tn = Nb if b_blk == "n" else _pick(N, MM_MAX_TN)
    if a_blk == "k" or b_blk == "k":
        tk = Ka if a_blk == "k" else Kb
        assert tk == (Kb if b_blk == "k" else tk)
    else:
        tk = K if K <= MM_FULL_K else _pick(K, 1024 if ta else MM_FULL_K)
    if (a_seg and not ta) or o_seg:
        tm = M // NSEG
    if (a_seg and ta) or b_seg:
        tk = K // NSEG
    gm, gn, nk = M // tm, N // tn, K // tk
    assert not (add is not None and (o_seg or o_blk)) and not (o_blk and o_seg)

    if a_seg:
        assert a.ndim == 2
        a = a.reshape(a.shape[0] // NSEG, NSEG * a.shape[1])
        if ta:
            a_spec = pl.BlockSpec((tk, tm), lambda i, j, k: (0, k * (Ma // tm) + i))
        else:
            a_spec = pl.BlockSpec((tm, tk), lambda i, j, k: (0, i * (Ka // tk) + k))
    elif a.ndim == 3:
        lead = (lambda i, k: i) if a_blk == "m" else (lambda i, k: k)
        if ta:
            a_spec = pl.BlockSpec((None, tk, tm), lambda i, j, k: (lead(i, k), 0 if a_blk == "k" else k, 0 if a_blk == "m" else i))
        else:
            a_spec = pl.BlockSpec((None, tm, tk), lambda i, j, k: (lead(i, k), 0 if a_blk == "m" else i, 0 if a_blk == "k" else k))
    else:
        a_spec = pl.BlockSpec((tk, tm), lambda i, j, k: (k, i)) if ta else pl.BlockSpec((tm, tk), lambda i, j, k: (i, k))
    if b_seg:
        assert b.ndim == 2 and not tb
        b = b.reshape(b.shape[0] // NSEG, NSEG * b.shape[1])
        b_spec = pl.BlockSpec((tk, tn), lambda i, j, k: (0, k * (Nb // tn) + j))
    elif b.ndim == 3:
        lead = (lambda j, k: j) if b_blk == "n" else (lambda j, k: k)
        if tb:
            b_spec = pl.BlockSpec((None, tn, tk), lambda i, j, k: (lead(j, k), 0 if b_blk == "n" else j, 0 if b_blk == "k" else k))
        else:
            b_spec = pl.BlockSpec((None, tk, tn), lambda i, j, k: (lead(j, k), 0 if b_blk == "k" else k, 0 if b_blk == "n" else j))
    else:
        b_spec = pl.BlockSpec((tn, tk), lambda i, j, k: (j, k)) if tb else pl.BlockSpec((tk, tn), lambda i, j, k: (k, j))
    slot = _slot_of if o_slots else (lambda k: k)
    if o_blk == "n":
        assert gn == N_DEV or not o_slots
        o_shape, o_spec = (gn, M, tn), pl.BlockSpec((None, tm, tn), lambda i, j, k: (slot(j), i, 0))
    elif o_blk == "m" and o_slots and gm < N_DEV:
        rs = M // N_DEV
        per_tile = tm // rs
        assert per_tile % 2 == 0 and tm % rs == 0
        o_shape = (2, N_CHIP, rs, N)
        o_spec = pl.BlockSpec((2, per_tile // 2, rs, tn), lambda i, j, k: (0, i, 0, j))
    elif o_blk == "m":
        assert gm == N_DEV or not o_slots
        o_shape, o_spec = (gm, tm, N), pl.BlockSpec((None, tm, tn), lambda i, j, k: (slot(i), 0, j))
    elif o_seg:
        o_shape, o_spec = (tm, NSEG * N), pl.BlockSpec((tm, tn), lambda i, j, k: (0, i * (N // tn) + j))
    else:
        o_shape, o_spec = (M, N), pl.BlockSpec((tm, tn), lambda i, j, k: (i, j))
    dims = (((0 if ta else 1,), (1 if tb else 0,)), ((), ()))
    if epi is not None:
        assert gn == 1 and add is None and o_blk is None and not o_seg
        epi_fn, add, epi_w = epi
        o_shape, o_spec = (M, epi_w), pl.BlockSpec((tm, epi_w), lambda i, j, k: (i, 0))
    has_add = add is not None
    add_spec = pl.BlockSpec((tm, add.shape[1]), lambda i, j, k: (i, 0)) if epi is not None else o_spec

    carry = carry or _NO_CARRY
    n_in = 2 + has_add

    def body(*refs):
        own, c_in, c_out, c_sems = _carry_split(carry, refs, n_in, 1)
        a_ref, b_ref = own[0], own[1]
        add_ref = own[2] if has_add else None
        o_ref, acc_ref = own[-2], own[-1]
        i, j, k = pl.program_id(0), pl.program_id(1), pl.program_id(2)
        _carry_start(carry, c_in, c_out, c_sems, (i == 0) & (j == 0) & (k == 0))

        @pl.when(k == 0)
        def _():
            acc_ref[...] = jnp.zeros_like(acc_ref)

        acc_ref[...] += lax.dot_general(a_ref[...].astype(bf16), b_ref[...].astype(bf16), dims, preferred_element_type=f32)

        @pl.when(k == nk - 1)
        def _():
            r = acc_ref[...] * scale
            if epi is not None:
                r = epi_fn(r, add_ref[...].astype(f32))
            elif has_add:
                r = r + add_ref[...].astype(f32)
            if len(o_shape) == 4:
                rs = o_shape[2]
                for chip_l in range(o_ref.shape[1]):
                    for core in range(2):
                        dev = 2 * chip_l + core
                        o_ref[core, chip_l] = r[dev * rs:(dev + 1) * rs].astype(out_dtype)
            else:
                o_ref[...] = r.astype(out_dtype)

        _carry_finish(carry, c_in, c_out, c_sems, (i == gm - 1) & (j == gn - 1) & (k == nk - 1))

    ins = [a, b] + ([add] if has_add else []) + list(carry.ins)
    in_specs = [a_spec, b_spec] + ([add_spec] if has_add else []) + [_ANY] * len(carry.ins)
    res = pl.pallas_call(
        body, name=name, grid=(gm, gn, nk), in_specs=in_specs, out_specs=[o_spec] + [_ANY] * len(carry.out_shapes),
        out_shape=[_S(o_shape, out_dtype)] + list(carry.out_shapes),
        scratch_shapes=[pltpu.VMEM((tm, tn), f32)] + list(carry.sems),
        compiler_params=_cparams(("arbitrary",) * 3 if carry.ins else ("parallel", "parallel", "arbitrary")),
    )(*ins)
    out = res[0]
    if len(o_shape) == 4:
        out = out.reshape(N_DEV, o_shape[2], N)
    elif o_seg:
        out = out.reshape(M, N)
    return (out, list(res[1:])) if carry.ins else out


def _row_tile(T, widths):
    budget = ROW_BLOCK_BYTES
    tb = max(16, budget // (4 * sum(widths)))
    return _pick(T, tb, align=16)


def _rows(fn, rows, params, outs, *, name, carry=None):
    T = rows[0].shape[0]
    nr, npar = len(rows), len(params)
    tb = _row_tile(T, [r.shape[1] for r in rows] + [w for w, _ in outs])
    carry = carry or _NO_CARRY

    def body(*refs):
        own, c_in, c_out, c_sems = _carry_split(carry, refs, nr + npar, len(outs))
        _carry_start(carry, c_in, c_out, c_sems, pl.program_id(0) == 0)
        ins = [r[...].astype(f32) for r in own[: nr + npar]]
        res = fn(*ins)
        for o_ref, r in zip(own[nr + npar:], res):
            o_ref[...] = r.astype(o_ref.dtype)
        _carry_finish(carry, c_in, c_out, c_sems, pl.program_id(0) == T // tb - 1)

    in_specs = [pl.BlockSpec((tb, r.shape[1]), lambda i: (i, 0)) for r in rows]
    in_specs += [pl.BlockSpec(p.shape, lambda i: (0, 0)) for p in params]
    out_specs = [pl.BlockSpec((tb, w), lambda i: (i, 0)) for w, _ in outs]
    res = pl.pallas_call(
        body, name=name, grid=(T // tb,), in_specs=in_specs + [_ANY] * len(carry.ins),
        out_specs=out_specs + [_ANY] * len(carry.out_shapes), out_shape=[_S((T, w), d) for w, d in outs] + list(carry.out_shapes),
        scratch_shapes=list(carry.sems), compiler_params=_cparams(("arbitrary",) if carry.ins else ("parallel",)),
    )(*rows, *params, *carry.ins)
    return (tuple(res[:len(outs)]), list(res[len(outs):])) if carry.ins else tuple(res)


def _rows_bwd(fn, rows, params, cots, *, name, want_rows, row_dtypes=None, adds=None):
    T = rows[0].shape[0]
    nr, npar, nc = len(rows), len(params), len(cots)
    adds = adds or {}
    add_idx = sorted(adds)
    row_dtypes = row_dtypes or {}
    widths = [r.shape[1] for r in rows] + [c.shape[1] for c in cots] + [rows[i].shape[1] for i in want_rows]
    tb = _row_tile(T, widths)

    def body(*refs):
        ins = [r[...].astype(f32) for r in refs[: nr + npar]]
        cot = tuple(r[...].astype(f32) for r in refs[nr + npar: nr + npar + nc])
        add_refs = refs[nr + npar + nc: nr + npar + nc + len(add_idx)]
        out_refs = refs[nr + npar + nc + len(add_idx):]
        _, vjp = jax.vjp(lambda *a: tuple(fn(*a)), *ins)
        g = vjp(cot)
        for o_ref, i in zip(out_refs[: len(want_rows)], want_rows):
            r = g[i]
            if i in adds:
                r = r + add_refs[add_idx.index(i)][...].astype(f32)
            o_ref[...] = r.astype(o_ref.dtype)
        first = pl.program_id(0) == 0
        for o_ref, gp in zip(out_refs[len(want_rows):], g[nr:]):
            @pl.when(first)
            def _(o_ref=o_ref):
                o_ref[...] = jnp.zeros_like(o_ref)

            o_ref[...] += gp

    in_specs = [pl.BlockSpec((tb, r.shape[1]), lambda i: (i, 0)) for r in rows]
    in_specs += [pl.BlockSpec(p.shape, lambda i: (0, 0)) for p in params]
    in_specs += [pl.BlockSpec((tb, c.shape[1]), lambda i: (i, 0)) for c in cots]
    in_specs += [pl.BlockSpec((tb, adds[i].shape[1]), lambda i_: (i_, 0)) for i in add_idx]
    out_specs = [pl.BlockSpec((tb, rows[i].shape[1]), lambda i_: (i_, 0)) for i in want_rows]
    out_specs += [pl.BlockSpec(p.shape, lambda i: (0, 0)) for p in params]
    out_shape = [_S(rows[i].shape, row_dtypes.get(i, f32)) for i in want_rows] + [_S(p.shape, f32) for p in params]
    res = pl.pallas_call(
        body, name=name, grid=(T // tb,), in_specs=in_specs, out_specs=out_specs, out_shape=out_shape,
        compiler_params=_cparams(("arbitrary",)),
    )(*rows, *params, *cots, *[adds[i] for i in add_idx])
    return list(res[: len(want_rows)]), list(res[len(want_rows):])


def _f_rmsnorm(x, g):
    return (x * lax.rsqrt(jnp.mean(x * x, axis=-1, keepdims=True) + EPS) * g,)


def _f_swiglu(ab):
    F = ab.shape[1] // 2
    return (jax.nn.silu(ab[:, :F]) * ab[:, F:],)


def _f_s5_post(y, u, d):
    return (jax.nn.gelu(y + d * u),)


def _f_glu(g, v, b):
    return (g * jax.nn.sigmoid(v + b),)


def _f_gated_norm(y, z, w):
    return _f_rmsnorm(y * jax.nn.silu(z), w)


def _f_merge(gl, p5, pssd, b):
    D = p5.shape[1]
    gates = jax.nn.sigmoid(gl + b)
    return (gates[:, :D] * p5 + gates[:, D:] * pssd,)


def _f_dt(dtr, bias, a_log):
    dt = jax.nn.softplus(dtr + bias)
    return dt, dt * (-jnp.exp(a_log))


def _f_dt_expand(dtr, bias, a_log, e):
    dt, a = _f_dt(dtr, bias, a_log)
    return dt, a, _doth(dt, e), _doth(a, e)


def _loss_stage(x, tgt, g, *, name):
    T, D = x.shape
    tb = _row_tile(T, [D, D, D])

    def f(xb, gb, tb_):
        y = _f_rmsnorm(xb, gb)[0]
        return 0.5 * jnp.sum(jnp.mean(jnp.square(y - tb_), axis=-1, keepdims=True), axis=0, keepdims=True)

    def body(x_ref, t_ref, g_ref, l_ref, dx_ref, dg_ref):
        tv = t_ref[...]
        val, vjp = jax.vjp(lambda a, b: f(a, b, tv), x_ref[...], g_ref[...])
        dx, dg = vjp(jnp.ones((1, 1), f32))
        dx_ref[...] = dx

        @pl.when(pl.program_id(0) == 0)
        def _():
            l_ref[...] = jnp.zeros_like(l_ref)
            dg_ref[...] = jnp.zeros_like(dg_ref)

        l_ref[...] += jnp.broadcast_to(val, l_ref.shape)
        dg_ref[...] += dg

    row = pl.BlockSpec((tb, D), lambda i: (i, 0))
    par = pl.BlockSpec((1, D), lambda i: (0, 0))
    return pl.pallas_call(
        body, name=name, grid=(T // tb,), in_specs=[row, row, par],
        out_specs=[pl.BlockSpec((1, LANES), lambda i: (0, 0)), row, par],
        out_shape=[_S((1, LANES), f32), _S((T, D), f32), _S((1, D), f32)], compiler_params=_cparams(("arbitrary",)),
    )(x, tgt, g)


CONV_R = 64
HALO = 8


def _conv_shifts_down(ref, t):
    if isinstance(t, int) and t == 0:
        cur = ref[0:CONV_R, :]
        row = lax.broadcasted_iota(jnp.int32, cur.shape, 0)
        return [cur] + [jnp.where(row >= s, pltpu.roll(cur, s, axis=0), 0.0) for s in range(1, CONV_K)]
    win = ref[pl.ds(pl.multiple_of(t * CONV_R - HALO, HALO), CONV_R + HALO), :]
    return [win[HALO:]] + [pltpu.roll(win, s, axis=0)[HALO:] for s in range(1, CONV_K)]


def _conv_shifts_up(ref, t):
    if isinstance(t, int):
        cur = ref[t * CONV_R:(t + 1) * CONV_R, :]
        row = lax.broadcasted_iota(jnp.int32, cur.shape, 0)
        return [cur] + [jnp.where(row < CONV_R - s, pltpu.roll(cur, CONV_R - s, axis=0), 0.0) for s in range(1, CONV_K)]
    win = ref[pl.ds(pl.multiple_of(t * CONV_R, HALO), CONV_R + HALO), :]
    return [win[:CONV_R]] + [pltpu.roll(win, CONV_R + HALO - s, axis=0)[:CONV_R] for s in range(1, CONV_K)]


def _conv_pre(shifted, w, b):
    pre = b
    for k in range(CONV_K):
        pre = pre + w[k:k + 1, :] * shifted[CONV_K - 1 - k]
    return pre


def _conv_fwd(x, w, b, *, name):
    T, C = x.shape
    cb = _pick(C, 256)

    def body(x_ref, w_ref, b_ref, o_ref):
        xv = x_ref[...]
        row = lax.broadcasted_iota(jnp.int32, xv.shape, 0)
        shifted = [xv] + [jnp.where(row >= s, pltpu.roll(xv, s, axis=0), 0.0) for s in range(1, CONV_K)]
        o_ref[...] = jax.nn.silu(_conv_pre(shifted, w_ref[...], b_ref[...]))

    col = pl.BlockSpec((T, cb), lambda j: (0, j))
    return pl.pallas_call(
        body, name=name, grid=(C // cb,), in_specs=[col, pl.BlockSpec((CONV_K, cb), lambda j: (0, j)), pl.BlockSpec((1, cb), lambda j: (0, j))],
        out_specs=col, out_shape=_S((T, C), f32), compiler_params=_cparams(("parallel",)),
    )(x, w, b)


def _conv_bwd(x, w, b, dy, *, name, carry=None):
    T, C = x.shape
    cb = _pick(C, 128)
    carry = carry or _NO_CARRY
    ends = []
    for d in dy:
        ends.append((ends[-1] if ends else 0) + d.shape[1] // cb)
    assert ends[-1] == C // cb and all(d.shape[1] % cb == 0 for d in dy)
    npc = len(dy)
    n_tiles = T // CONV_R

    def body(*refs):
        own, c_in, c_out, c_sems = _carry_split(carry, refs, 3 + npc, 3)
        x_ref, w_ref, b_ref = own[:3]
        dy_refs, (dx_ref, dw_ref, db_ref, dpre_ref) = own[3:3 + npc], own[3 + npc:]
        _carry_start(carry, c_in, c_out, c_sems, pl.program_id(0) == 0)
        j = pl.program_id(0)
        wv, bv = w_ref[...], b_ref[...]

        def fold8(v):
            return jnp.sum(v.reshape(CONV_R // 8, 8, cb), axis=0)

        def first_pass(t, acc):
            rows = slice(0, CONV_R) if isinstance(t, int) else pl.ds(pl.multiple_of(t * CONV_R, CONV_R), CONV_R)
            shifted = _conv_shifts_down(x_ref, t)
            pre = _conv_pre(shifted, wv, bv)
            dyv = dy_refs[-1][rows, :]
            for p in range(npc - 2, -1, -1):
                dyv = jnp.where(j < ends[p], dy_refs[p][rows, :], dyv)
            sg = jax.nn.sigmoid(pre)
            dpre = dyv * sg * (1.0 + pre * (1.0 - sg))
            dpre_ref[rows, :] = dpre
            return tuple(acc[k] + fold8(dpre * shifted[CONV_K - 1 - k]) for k in range(CONV_K)) + (acc[CONV_K] + fold8(dpre),)

        zero = jnp.zeros((8, cb), f32)
        acc = lax.fori_loop(1, n_tiles, first_pass, first_pass(0, (zero,) * (CONV_K + 1)), unroll=3)
        for k in range(CONV_K):
            dw_ref[k:k + 1, :] = jnp.sum(acc[k], axis=0, keepdims=True)
        db_ref[...] = jnp.sum(acc[CONV_K], axis=0, keepdims=True)

        def dx_of(t):
            up = _conv_shifts_up(dpre_ref, t)
            dx = wv[CONV_K - 1:CONV_K, :] * up[0]
            for k in range(CONV_K - 1):
                dx = dx + wv[k:k + 1, :] * up[CONV_K - 1 - k]
            return dx

        def second_pass(t, c):
            dx_ref[pl.ds(pl.multiple_of(t * CONV_R, CONV_R), CONV_R), :] = dx_of(t)
            return c

        lax.fori_loop(0, n_tiles - 1, second_pass, 0, unroll=3)
        dx_ref[(n_tiles - 1) * CONV_R:, :] = dx_of(n_tiles - 1)
        _carry_finish(carry, c_in, c_out, c_sems, pl.program_id(0) == C // cb - 1)

    col = pl.BlockSpec((T, cb), lambda j: (0, j))
    wsp = pl.BlockSpec((CONV_K, cb), lambda j: (0, j))
    bsp = pl.BlockSpec((1, cb), lambda j: (0, j))
    starts = [0] + ends[:-1]
    dy_specs = [pl.BlockSpec((T, cb), lambda j, s=s, e=e: (0, jnp.clip(j, s, e - 1) - s)) for s, e in zip(starts, ends)]
    res = pl.pallas_call(
        body, name=name, grid=(C // cb,), in_specs=[col, wsp, bsp] + dy_specs + [_ANY] * len(carry.ins),
        out_specs=[col, wsp, bsp] + [_ANY] * len(carry.out_shapes),
        out_shape=[_S((T, C), f32), _S((CONV_K, C), f32), _S((1, C), f32)] + list(carry.out_shapes),
        scratch_shapes=[pltpu.VMEM((T, cb), f32)] + list(carry.sems),
        compiler_params=_cparams(("arbitrary",) if carry.ins else ("parallel",)),
    )(x, w, b, *dy, *carry.ins)
    return (*res[:3], list(res[3:]))


def _f_s5_prep(lr, li, ldt, lrb, lib, ldtb, brt, bit):
    def disc(lr_, li_, ldt_):
        dt = jnp.exp(ldt_)
        mag = jnp.exp(lr_ * dt)
        ar, ai = mag * jnp.cos(li_ * dt), mag * jnp.sin(li_ * dt)
        den = lr_ * lr_ + li_ * li_
        cr = ((ar - 1.0) * lr_ + ai * li_) / den
        ci = (ai * lr_ - (ar - 1.0) * li_) / den
        return ar, ai, cr, ci

    ar, ai, _, _ = disc(lr, li, ldt)
    _, _, cr, ci = disc(lrb, lib, ldtb)
    return ar, ai, cr * brt - ci * bit, cr * bit + ci * brt


def _s5_prep(args, *, name):
    G, N = args[0].shape
    GM = args[3].shape[0]

    def body(*refs):
        res = _f_s5_prep(*[r[...] for r in refs[:8]])
        for o, r in zip(refs[8:], res):
            o[...] = r

    return pl.pallas_call(body, name=name, out_shape=[_S((G, N), f32)] * 2 + [_S((GM, N), f32)] * 2)(*args)


def _s5_prep_bwd(args, cots, rsum, *, name):
    G, N = args[0].shape
    GM = args[3].shape[0]

    def body(*refs):
        ins = [r[...] for r in refs[:8]]
        cot = tuple(r[...] for r in refs[8:12])
        rs = refs[12][...]
        _, vjp = jax.vjp(_f_s5_prep, *ins)
        g = vjp(cot)
        fold = lambda v: jnp.dot(rs, v, preferred_element_type=f32, precision=lax.Precision.HIGHEST)
        o = refs[13:]
        o[0][...] = g[0] + fold(g[3])
        o[1][...] = g[1] + fold(g[4])
        o[2][...] = g[2] + fold(jnp.broadcast_to(g[5], (GM, LANES)))[:, 0:1]
        o[3][...] = g[6]
        o[4][...] = g[7]

    return pl.pallas_call(
        body, name=name, out_shape=[_S((G, N), f32), _S((G, N), f32), _S((G, 1), f32), _S((GM, N), f32), _S((GM, N), f32)],
    )(*args, *cots, rsum)


S5_TC = 512


def _s5_local_scan(src, w_r, w_i, a_r, a_i, *, reverse, name, carry=None, powers=True):
    T, C = src.shape
    nblk, cb, sb = w_r.shape
    NS = nblk * sb
    tc = min(S5_TC, T)
    nT, nt = T // tc, tc // NSEG
    tmap = (lambda i: nT - 1 - i) if reverse else (lambda i: i)

    carry = carry or _NO_CARRY

    def body(*refs):
        own, c_in, c_out, c_sems = _carry_split(carry, refs, 5, 4)
        u_ref, wr_ref, wi_ref, ar_ref, ai_ref, sr_ref, si_ref, pr_ref, pi_ref, st_r, st_i, pw_r, pw_i = own
        _carry_start(carry, c_in, c_out, c_sems, (pl.program_id(0) == 0) & (pl.program_id(1) == 0))

        @pl.when(pl.program_id(1) == 0)
        def _():
            st_r[...] = jnp.zeros_like(st_r)
            st_i[...] = jnp.zeros_like(st_i)
            pw_r[...] = jnp.ones_like(pw_r)
            pw_i[...] = jnp.zeros_like(pw_i)

        u = u_ref[...].astype(bf16)
        sr_ref[...] = jnp.dot(u, wr_ref[...], preferred_element_type=f32)
        si_ref[...] = jnp.dot(u, wi_ref[...], preferred_element_type=f32)
        ar = jnp.broadcast_to(ar_ref[...], (NSEG, sb))
        ai = jnp.broadcast_to(ai_ref[...], (NSEG, sb))

        def step(k, c):
            cr, ci, qr, qi = c
            kk = (nt - 1 - k) if reverse else k
            rows = pl.ds(pl.multiple_of(kk * NSEG, NSEG), NSEG)
            nr = ar * cr - ai * ci + sr_ref[rows, :]
            ni = ar * ci + ai * cr + si_ref[rows, :]
            sr_ref[rows, :] = nr
            si_ref[rows, :] = ni
            return (nr, ni, ar * qr - ai * qi, ar * qi + ai * qr) if powers else (nr, ni, qr, qi)

        cr, ci, qr, qi = lax.fori_loop(0, nt, step, (st_r[...], st_i[...], pw_r[...], pw_i[...]), unroll=8)
        st_r[...], st_i[...], pw_r[...], pw_i[...] = cr, ci, qr, qi
        pr_ref[...] = qr
        pi_ref[...] = qi
        _carry_finish(carry, c_in, c_out, c_sems, (pl.program_id(0) == nblk - 1) & (pl.program_id(1) == nT - 1))

    blk = pl.BlockSpec((tc, sb), lambda j, i: (tmap(i), j))
    wsp = pl.BlockSpec((None, cb, sb), lambda j, i: (j, 0, 0))
    asp = pl.BlockSpec((1, sb), lambda j, i: (0, j))
    psp = pl.BlockSpec((NSEG, sb), lambda j, i: (0, j))
    res = pl.pallas_call(
        body, name=name, grid=(nblk, nT),
        in_specs=[pl.BlockSpec((tc, cb), lambda j, i: (tmap(i), j)), wsp, wsp, asp, asp] + [_ANY] * len(carry.ins),
        out_specs=[blk, blk, psp, psp] + [_ANY] * len(carry.out_shapes),
        out_shape=[_S((T, NS), f32)] * 2 + [_S((NSEG, NS), f32)] * 2 + list(carry.out_shapes),
        scratch_shapes=[pltpu.VMEM((NSEG, sb), f32)] * 4 + list(carry.sems),
        compiler_params=_cparams(("arbitrary", "arbitrary") if carry.ins else ("parallel", "arbitrary")),
    )(src, w_r, w_i, a_r, a_i, *carry.ins)
    return (*res[:4], list(res[4:])) if carry.ins else res


def _s5_carry(e_r, e_i, p_r, p_i, *, reverse, name):
    NS = e_r.shape[1]

    def body(er_ref, ei_ref, pr_ref, pi_ref, cr_ref, ci_ref):
        ar, ai = pr_ref[0:1, :], pi_ref[0:1, :]
        cr = jnp.zeros((1, NS), f32)
        ci = jnp.zeros((1, NS), f32)
        order = list(range(NSEG - 1, -1, -1)) if reverse else list(range(NSEG))
        cr_ref[order[0]:order[0] + 1, :] = cr
        ci_ref[order[0]:order[0] + 1, :] = ci
        for prev, q in zip(order[:-1], order[1:]):
            er, ei = er_ref[prev:prev + 1, :], ei_ref[prev:prev + 1, :]
            cr, ci = er + ar * cr - ai * ci, ei + ar * ci + ai * cr
            cr_ref[q:q + 1, :] = cr
            ci_ref[q:q + 1, :] = ci

    return pl.pallas_call(body, name=name, out_shape=[_S((NSEG, NS), f32)] * 2)(e_r, e_i, p_r, p_i)


def _s5_fix_out(sl_r, sl_i, a_r, a_i, c_r, c_i, wc_r, wc_i, *, name):
    T, NS = sl_r.shape
    nblk, sb, cb = wc_r.shape
    tc = min(S5_TC, T)
    nT, nt = T // tc, tc // NSEG

    def body(lr_ref, li_ref, ar_ref, ai_ref, cr_ref, ci_ref, wr_ref, wi_ref, sr_ref, si_ref, y_ref, pw_r, pw_i, pv_r, pv_i):
        ar = jnp.broadcast_to(ar_ref[...], (NSEG, sb))
        ai = jnp.broadcast_to(ai_ref[...], (NSEG, sb))
        a2r, a2i = ar * ar - ai * ai, 2.0 * ar * ai

        @pl.when(pl.program_id(1) == 0)
        def _():
            pw_r[...], pw_i[...] = ar, ai
            pv_r[...], pv_i[...] = a2r, a2i

        cr, ci = cr_ref[...], ci_ref[...]

        def fix(k, qr, qi):
            rows = pl.ds(pl.multiple_of(k * NSEG, NSEG), NSEG)
            sr_ref[rows, :] = lr_ref[rows, :] + qr * cr - qi * ci
            si_ref[rows, :] = li_ref[rows, :] + qr * ci + qi * cr

        def step(k, c):
            qr, qi, vr, vi = c
            fix(2 * k, qr, qi)
            fix(2 * k + 1, vr, vi)
            return a2r * qr - a2i * qi, a2r * qi + a2i * qr, a2r * vr - a2i * vi, a2r * vi + a2i * vr

        pw_r[...], pw_i[...], pv_r[...], pv_i[...] = lax.fori_loop(
            0, nt // 2, step, (pw_r[...], pw_i[...], pv_r[...], pv_i[...]), unroll=4)
        y_ref[...] = (jnp.dot(sr_ref[...].astype(bf16), wr_ref[...], preferred_element_type=f32)
                      - jnp.dot(si_ref[...].astype(bf16), wi_ref[...], preferred_element_type=f32))

    blk = pl.BlockSpec((tc, sb), lambda j, i: (i, j))
    asp = pl.BlockSpec((1, sb), lambda j, i: (0, j))
    csp = pl.BlockSpec((NSEG, sb), lambda j, i: (0, j))
    wsp = pl.BlockSpec((None, sb, cb), lambda j, i: (j, 0, 0))
    return pl.pallas_call(
        body, name=name, grid=(nblk, nT), in_specs=[blk, blk, asp, asp, csp, csp, wsp, wsp],
        out_specs=[blk, blk, pl.BlockSpec((tc, cb), lambda j, i: (i, j))],
        out_shape=[_S((T, NS), f32)] * 2 + [_S((T, nblk * cb), f32)],
        scratch_shapes=[pltpu.VMEM((NSEG, sb), f32)] * 4, compiler_params=_cparams(("parallel", "arbitrary")),
    )(sl_r, sl_i, a_r, a_i, c_r, c_i, wc_r, wc_i)


def _s5_fix_bwd(ql_r, ql_i, ab_r, ab_i, c_r, c_i, s_r, s_i, sb_r, sb_i, u, dy, du_add, w_r, w_i, *, name):
    T, NS = ql_r.shape
    nblk, cb, sb = w_r.shape
    tc = min(S5_TC, T)
    nT, nt = T // tc, tc // NSEG
    tmap = lambda i: nT - 1 - i

    def body(lr_ref, li_ref, ar_ref, ai_ref, cr_ref, ci_ref, sr_ref, si_ref, br_ref, bi_ref, u_ref, dy_ref, dua_ref, wr_ref, wi_ref,
             du_ref, dwr_ref, dwi_ref, dcr_ref, dci_ref, dar_ref, dai_ref, pw_r, pw_i, ac_r, ac_i, q_r, q_i):
        first = pl.program_id(1) == 0

        @pl.when(first)
        def _():
            pw_r[...] = jnp.ones_like(pw_r)
            pw_i[...] = jnp.zeros_like(pw_i)
            ac_r[...] = jnp.zeros_like(ac_r)
            ac_i[...] = jnp.zeros_like(ac_i)
            dwr_ref[...] = jnp.zeros_like(dwr_ref)
            dwi_ref[...] = jnp.zeros_like(dwi_ref)
            dcr_ref[...] = jnp.zeros_like(dcr_ref)
            dci_ref[...] = jnp.zeros_like(dci_ref)

        ar = jnp.broadcast_to(ar_ref[...], (NSEG, sb))
        ai = jnp.broadcast_to(ai_ref[...], (NSEG, sb))
        cr, ci = cr_ref[...], ci_ref[...]

        def fix(rows, qr, qi, spr, spi, accr, acci):
            qr, qi = ar * qr - ai * qi, ar * qi + ai * qr
            xr = lr_ref[rows, :] + qr * cr - qi * ci
            xi = li_ref[rows, :] + qr * ci + qi * cr
            q_r[rows, :] = xr
            q_i[rows, :] = xi
            return qr, qi, accr + xr * spr + xi * spi, acci + xi * spr - xr * spi

        def step(k, c):
            qr, qi, accr, acci = c
            kk = nt - 1 - k
            rows = pl.ds(pl.multiple_of(kk * NSEG, NSEG), NSEG)
            prev = pl.ds(pl.multiple_of((kk - 1) * NSEG, NSEG), NSEG)
            return fix(rows, qr, qi, sr_ref[prev, :], si_ref[prev, :], accr, acci)

        c = lax.fori_loop(0, nt - 1, step, (pw_r[...], pw_i[...], ac_r[...], ac_i[...]), unroll=7)
        qr, qi, accr, acci = fix(pl.ds(0, NSEG), *c[:2], br_ref[...], bi_ref[...], *c[2:])
        pw_r[...], pw_i[...], ac_r[...], ac_i[...] = qr, qi, accr, acci

        qrb, qib = q_r[...].astype(bf16), q_i[...].astype(bf16)
        nt_dims = (((1,), (1,)), ((), ()))
        tn_dims = (((0,), (0,)), ((), ()))
        du_ref[...] = (dua_ref[...] + lax.dot_general(qrb, wr_ref[...], nt_dims, preferred_element_type=f32)
                       + lax.dot_general(qib, wi_ref[...], nt_dims, preferred_element_type=f32))
        ub = u_ref[...].astype(bf16)
        dwr_ref[...] += lax.dot_general(ub, qrb, tn_dims, preferred_element_type=f32)
        dwi_ref[...] += lax.dot_general(ub, qib, tn_dims, preferred_element_type=f32)
        dyb = dy_ref[...].astype(bf16)
        dcr_ref[...] += lax.dot_general(sr_ref[...].astype(bf16), dyb, tn_dims, preferred_element_type=f32)
        dci_ref[...] -= lax.dot_general(si_ref[...].astype(bf16), dyb, tn_dims, preferred_element_type=f32)

        @pl.when(pl.program_id(1) == nT - 1)
        def _():
            dar_ref[...] = jnp.sum(accr, axis=0, keepdims=True)
            dai_ref[...] = jnp.sum(acci, axis=0, keepdims=True)

    blk = pl.BlockSpec((tc, sb), lambda j, i: (tmap(i), j))
    asp = pl.BlockSpec((1, sb), lambda j, i: (0, j))
    csp = pl.BlockSpec((NSEG, sb), lambda j, i: (0, j))
    bsp = pl.BlockSpec((None, NSEG, sb), lambda j, i: (tmap(i), 0, j))
    chn = pl.BlockSpec((tc, cb), lambda j, i: (tmap(i), j))
    wsp = pl.BlockSpec((None, cb, sb), lambda j, i: (j, 0, 0))
    wcs = pl.BlockSpec((None, sb, cb), lambda j, i: (j, 0, 0))
    return pl.pallas_call(
        body, name=name, grid=(nblk, nT), in_specs=[blk, blk, asp, asp, csp, csp, blk, blk, bsp, bsp, chn, chn, chn, wsp, wsp],
        out_specs=[chn, wsp, wsp, wcs, wcs, asp, asp],
        out_shape=[_S((T, nblk * cb), f32), _S((nblk, cb, sb), f32), _S((nblk, cb, sb), f32), _S((nblk, sb, cb), f32),
                   _S((nblk, sb, cb), f32), _S((1, NS), f32), _S((1, NS), f32)],
        scratch_shapes=[pltpu.VMEM((NSEG, sb), f32)] * 4 + [pltpu.VMEM((tc, sb), f32)] * 2,
        compiler_params=_cparams(("parallel", "arbitrary")),
    )(ql_r, ql_i, ab_r, ab_i, c_r, c_i, s_r, s_i, sb_r, sb_i, u, dy, du_add, w_r, w_i)


SSD2_TB = 512
_NN = (((1,), (0,)), ((), ()))
_NT = (((1,), (1,)), ((), ()))
_TN = (((0,), (0,)), ((), ()))


def _dotf(a, b, dims):
    return lax.dot_general(a.astype(bf16), b.astype(bf16), dims, preferred_element_type=f32)


def _doth(a, b, dims=_NN, sel="b", parts=3):
    x, m = (a, b) if sel == "b" else (b, a)
    m = m.astype(bf16)
    out = None
    for _ in range(parts):
        piece = x.astype(bf16)
        x = x - piece.astype(f32)
        d = lax.dot_general(*((piece, m) if sel == "b" else (m, piece)), dims, preferred_element_type=f32)
        out = d if out is None else out + d
    return out


def _ssd_consts(hpg):
    W = hpg * CHUNK
    i = lax.broadcasted_iota(jnp.int32, (CHUNK, CHUNK), 0)
    j = lax.broadcasted_iota(jnp.int32, (CHUNK, CHUNK), 1)
    tril = (i >= j).astype(f32)
    r = lax.broadcasted_iota(jnp.int32, (W, W), 0)
    c = lax.broadcasted_iota(jnp.int32, (W, W), 1)
    bd = (r // CHUNK == c // CHUNK).astype(f32)
    triu_bd = bd * (r <= c).astype(f32)
    e_r = lax.broadcasted_iota(jnp.int32, (W, LANES), 0)
    e_c = lax.broadcasted_iota(jnp.int32, (W, LANES), 1)
    ered = (e_r // HEADDIM == e_c).astype(f32)
    return tril, jnp.tile(tril, (1, hpg)), bd, triu_bd, ered


def _ssd2_specs(G, hpg, tb, tmap, b_off, c_off):
    W = hpg * HEADDIM
    ncb = tb // CHUNK
    xsp = pl.BlockSpec((tb, W), lambda g, i: (tmap(i), g))
    bsp = pl.BlockSpec((tb, SSD_STATE), lambda g, i: (tmap(i), b_off + g))
    csp = pl.BlockSpec((tb, SSD_STATE), lambda g, i: (tmap(i), c_off + g))
    rsp = pl.BlockSpec((None, ncb, W), lambda g, i: (g, tmap(i), 0))
    dsp = pl.BlockSpec((1, W), lambda g, i: (0, g))
    hsp = pl.BlockSpec((None, ncb, SSD_STATE, W), lambda g, i: (g, tmap(i), 0, 0))
    const = lambda a: pl.BlockSpec(a.shape, lambda g, i: (0, 0))
    return xsp, bsp, csp, rsp, dsp, hsp, const


def _tile_rows(a, n):
    return jnp.concatenate([a] * n, axis=0)


def _ssd2_fwd(xc, dt4, a4, dtw, aw, d4, consts, *, d_inner, name, carry=None):
    carry = carry or _NO_CARRY
    T = xc.shape[0]
    G, nc, W = dtw.shape
    hpg = W // CHUNK
    tb = min(SSD2_TB, T)
    nb, ncb = T // tb, tb // CHUNK
    b_off = d_inner // SSD_STATE
    xsp, bsp, csp, rsp, dsp, hsp, const = _ssd2_specs(G, hpg, tb, lambda i: i, b_off, b_off + G)
    tril, mask4, bd, triu_bd, _ = consts

    def body(*refs):
        own, c_in, c_out, c_sems = _carry_split(carry, refs, 12, 2)
        x_ref, b_ref, c_ref, dt_ref, a_ref, dtw_ref, aw_ref, d_ref, tril_ref, mask_ref, bd_ref, tbd_ref, y_ref, hs_ref, h_scr = own
        _carry_start(carry, c_in, c_out, c_sems, (pl.program_id(0) == 0) & (pl.program_id(1) == 0))

        @pl.when(pl.program_id(1) == 0)
        def _():
            h_scr[...] = jnp.zeros_like(h_scr)

        acs_rows = _doth(aw_ref[...], tbd_ref[...])
        ht = h_scr[...]
        for c in range(ncb):
            rows = slice(c * CHUNK, (c + 1) * CHUNK)
            x, bm, cm = x_ref[rows, :], b_ref[rows, :], c_ref[rows, :]
            acs = _doth(tril_ref[...], a_ref[rows, :], sel="a")
            lmat = jnp.where(mask_ref[...] > 0, jnp.exp(jnp.minimum(acs - acs_rows[c:c + 1, :], 0.0)), 0.0)
            m4 = _dotf(cm, _tile_rows(bm, hpg), _NT) * lmat * dtw_ref[c:c + 1, :]
            xbd = _tile_rows(x, hpg) * bd_ref[...]
            hs_ref[c] = ht
            y_ref[rows, :] = _dotf(m4, xbd, _NN) + _dotf(cm, ht, _NN) * jnp.exp(acs) + d_ref[...] * x
            a_last = acs[CHUNK - 1:CHUNK, :]
            xw = x * (jnp.exp(a_last - acs) * dt_ref[rows, :])
            ht = ht * jnp.exp(a_last) + _dotf(bm, xw, _TN)
        h_scr[...] = ht
        _carry_finish(carry, c_in, c_out, c_sems, (pl.program_id(0) == G - 1) & (pl.program_id(1) == nb - 1))

    res = pl.pallas_call(
        body, name=name, grid=(G, nb),
        in_specs=[xsp, bsp, csp, xsp, xsp, rsp, rsp, dsp, const(tril), const(mask4), const(bd), const(triu_bd)] + [_ANY] * len(carry.ins),
        out_specs=[xsp, hsp] + [_ANY] * len(carry.out_shapes),
        out_shape=[_S((T, G * W), f32), _S((G, nc, SSD_STATE, W), f32)] + list(carry.out_shapes),
        scratch_shapes=[pltpu.VMEM((SSD_STATE, W), f32)] + list(carry.sems),
        compiler_params=_cparams(("arbitrary", "arbitrary") if carry.ins else ("parallel", "arbitrary")),
    )(xc, xc, xc, dt4, a4, dtw, aw, d4, tril, mask4, bd, triu_bd, *carry.ins)
    return res[0], res[1], list(res[2:])


def _ssd2_bwd(xc, dt4, a4, dtw, aw, d4, consts, hs, dy, *, d_inner, name, carry=None):
    T = xc.shape[0]
    G, nc, W = dtw.shape
    hpg = W // CHUNK
    tb = min(SSD2_TB, T)
    nb, ncb = T // tb, tb // CHUNK
    b_off = d_inner // SSD_STATE
    tmap = lambda i: nb - 1 - i
    xsp, bsp, csp, rsp, dsp, hsp, const = _ssd2_specs(G, hpg, tb, tmap, b_off, b_off + G)
    gsp = pl.BlockSpec((tb, SSD_STATE), lambda g, i: (tmap(i), g))
    ddsp = pl.BlockSpec((None, 1, LANES), lambda g, i: (g, 0, 0))
    tril, mask4, bd, triu_bd, ered = consts

    carry = carry or _NO_CARRY

    def body(*refs):
        own, c_in, c_out, c_sems = _carry_split(carry, refs, 15, 8)
        (x_ref, b_ref, c_ref, dt_ref, a_ref, dtw_ref, aw_ref, d_ref, tril_ref, mask_ref, bd_ref, tbd_ref, er_ref, hs_ref, dy_ref,
         dx_ref, db_ref, dc_ref, ddtc_ref, dac_ref, ddtw_ref, daw_ref, dd_ref, g_scr, dd_scr, rw_scr, tl_scr) = own
        _carry_start(carry, c_in, c_out, c_sems, (pl.program_id(0) == 0) & (pl.program_id(1) == 0))
        first = pl.program_id(1) == 0

        @pl.when(first)
        def _():
            g_scr[...] = jnp.zeros_like(g_scr)
            dd_scr[...] = jnp.zeros_like(dd_scr)

        mask = mask_ref[...] > 0
        lane_in_block = lax.broadcasted_iota(jnp.int32, mask.shape, 1) & (CHUNK - 1)
        maskt = lax.broadcasted_iota(jnp.int32, mask.shape, 0) <= lane_in_block
        acs_rows = _doth(aw_ref[...], tbd_ref[...])
        dht = g_scr[...]
        dd = dd_scr[...]
        for c in range(ncb - 1, -1, -1):
            rows = slice(c * CHUNK, (c + 1) * CHUNK)
            x, bm, cm, dyc = x_ref[rows, :], b_ref[rows, :], c_ref[rows, :], dy_ref[rows, :]
            dtc, dtr = dt_ref[rows, :], dtw_ref[c:c + 1, :]
            ht = hs_ref[c]
            acs = _doth(tril_ref[...], a_ref[rows, :], sel="a")
            seg = acs - acs_rows[c:c + 1, :]
            lmat = jnp.where(mask, jnp.exp(jnp.minimum(seg, 0.0)), 0.0)
            lmat_t = jnp.where(maskt, jnp.exp(jnp.minimum(-seg, 0.0)), 0.0)
            btile, ctile = _tile_rows(bm, hpg), _tile_rows(cm, hpg)
            g4 = _dotf(cm, btile, _NT)
            gt4 = _dotf(bm, ctile, _NT)
            m4 = g4 * lmat * dtr
            mt4 = gt4 * lmat_t * dtc
            xbd = _tile_rows(x, hpg) * bd_ref[...]
            dybd = _tile_rows(dyc, hpg) * bd_ref[...]
            dm4 = _dotf(dyc, xbd, _NT)
            dmt4 = _dotf(x, dybd, _NT)
            dx = d_ref[...] * dyc + _dotf(mt4, dybd, _NN)
            dd = dd + jnp.sum(dyc * x, axis=0, keepdims=True)
            e4 = dm4 * m4
            dc = _dotf(dm4 * lmat * dtr, btile, _NN)
            db = _dotf(dmt4 * lmat_t * dtc, ctile, _NN)
            decay = jnp.exp(acs)
            yoff = _dotf(cm, ht, _NN) * decay
            dz = dyc * decay
            dc = dc + _dotf(dz, ht, _NT)
            dht_prev = _dotf(cm, dz, _TN)
            a_last = acs[CHUNK - 1:CHUNK, :]
            ea_last = jnp.exp(a_last)
            erel = jnp.exp(a_last - acs)
            dte = erel * dtc
            dxw = _dotf(bm, dht, _NN)
            db = db + _dotf(x * dte, dht, _NT)
            dx = dx + dxw * dte
            q4 = dxw * x
            dacs = e4 + dyc * yoff - q4 * dte
            col = jnp.concatenate([q4 * erel, _doth(tril_ref[...], dacs, _TN, sel="a")], axis=0)
            col = _doth(col, er_ref[...], parts=2)
            ddtc_ref[rows, :] = col[:CHUNK]
            dac_ref[rows, :] = col[CHUNK:]
            ddtw_ref[c:c + 1, :] = jnp.sum(dm4 * g4 * lmat, axis=0, keepdims=True)
            rw_scr[c:c + 1, :] = -jnp.sum(e4, axis=0, keepdims=True)
            tl_scr[c:c + 1, :] = jnp.sum(q4 * dte, axis=0, keepdims=True) + ea_last * jnp.sum(dht * ht, axis=0, keepdims=True)
            dx_ref[rows, :] = dx
            db_ref[rows, :] = db
            dc_ref[rows, :] = dc
            dht = dht_prev + dht * ea_last
        daw_ref[...] = _doth(rw_scr[...], tbd_ref[...], _NT) + _doth(tl_scr[...], bd_ref[...])
        g_scr[...] = dht
        dd_scr[...] = dd

        @pl.when(pl.program_id(1) == nb - 1)
        def _():
            dd_ref[...] = _doth(dd, er_ref[...])

        _carry_finish(carry, c_in, c_out, c_sems, (pl.program_id(0) == G - 1) & (pl.program_id(1) == nb - 1))

    res = pl.pallas_call(
        body, name=name, grid=(G, nb),
        in_specs=[xsp, bsp, csp, xsp, xsp, rsp, rsp, dsp, const(tril), const(mask4), const(bd), const(triu_bd), const(ered), hsp, xsp]
        + [_ANY] * len(carry.ins),
        out_specs=[xsp, gsp, gsp, gsp, gsp, rsp, rsp, ddsp] + [_ANY] * len(carry.out_shapes),
        out_shape=[_S((T, G * W), f32), _S((T, G * SSD_STATE), f32), _S((T, G * SSD_STATE), f32), _S((T, G * LANES), f32),
                   _S((T, G * LANES), f32), _S(dtw.shape, f32), _S(dtw.shape, f32), _S((G, 1, LANES), f32)] + list(carry.out_shapes),
        scratch_shapes=[pltpu.VMEM((SSD_STATE, W), f32), pltpu.VMEM((1, W), f32), pltpu.VMEM((ncb, W), f32), pltpu.VMEM((ncb, W), f32)]
        + list(carry.sems),
        compiler_params=_cparams(("arbitrary", "arbitrary") if carry.ins else ("parallel", "arbitrary")),
    )(xc, xc, xc, dt4, a4, dtw, aw, d4, tril, mask4, bd, triu_bd, ered, hs, dy, *carry.ins)
    return (*res[:8], list(res[8:]))


def _peers():
    x, y, c = lax.axis_index("x"), lax.axis_index("y"), lax.axis_index("c")
    return x, y, c


_ANY = pl.BlockSpec(memory_space=pl.ANY)
N_CHIP = N_DEV // 2


def _all_gather(shards, *, name):
    n = len(shards)
    carry = _carry_gather(shards)

    def body(*refs):
        x_refs, out_refs, sems = refs[:n], refs[n:2 * n], refs[2 * n:]
        _gather_start(x_refs, out_refs, sems)
        _gather_finish(x_refs, out_refs, sems)

    return pl.pallas_call(
        body, name=name, out_shape=list(carry.out_shapes), in_specs=[_ANY] * n, out_specs=[_ANY] * n, scratch_shapes=list(carry.sems),
    )(*shards)


def _gather_parts(x_refs, out_refs, sems):
    send_sems, recv_sems, local_sems = sems
    x, y, c = _peers()
    me, sibling = (x, y, c), (x, y, 1 - c)
    chips = [(1 - x, y), (x, 1 - y), (1 - x, 1 - y)]
    n = len(x_refs)

    def copy(a, r, block, to, src=None):
        px, py, pc = block
        slot = out_refs[a].at[4 * px + 2 * py + pc]
        return pltpu.make_async_remote_copy(
            src_ref=slot if src is None else src, dst_ref=slot, send_sem=send_sems.at[7 * a + r],
            recv_sem=recv_sems.at[7 * a + r], device_id=to, device_id_type=MESH)

    mine = [pltpu.make_async_copy(x_refs[a], out_refs[a].at[4 * x + 2 * y + c], local_sems.at[a]) for a in range(n)]
    first = []
    for a in range(n):
        first.append(copy(a, 0, me, sibling, src=x_refs[a]))
        first += [copy(a, 1 + j, me, (*chip, c), src=x_refs[a]) for j, chip in enumerate(chips)]
    return copy, mine, first, me, sibling, chips, c, n


def _gather_start(x_refs, out_refs, sems):
    _, mine, first, *_ = _gather_parts(x_refs, out_refs, sems)
    for cp in mine + first:
        cp.start()


def _gather_finish(x_refs, out_refs, sems):
    copy, mine, first, me, sibling, chips, c, n = _gather_parts(x_refs, out_refs, sems)
    passed = []
    for j, chip in enumerate(chips):
        for a in range(n):
            copy(a, 1 + j, (*chip, c), me).wait_recv()
            fwd = copy(a, 4 + j, (*chip, c), sibling)
            fwd.start()
            passed.append(fwd)
    for a in range(n):
        copy(a, 0, sibling, me).wait_recv()
    for j, chip in enumerate(chips):
        for a in range(n):
            copy(a, 4 + j, (*chip, 1 - c), me).wait_recv()
    for cp in first + passed:
        cp.wait_send()
    for cp in mine:
        cp.wait()


def _exchange_sibling(slots, *, name):
    n = len(slots)
    carry = _carry_sibling(slots)

    def body(*refs):
        carry.start(refs[:n], refs[n:2 * n], refs[2 * n:])
        carry.finish(refs[:n], refs[n:2 * n], refs[2 * n:])

    return list(pl.pallas_call(
        body, name=name, out_shape=list(carry.out_shapes), in_specs=[_ANY] * n, out_specs=[_ANY] * n, scratch_shapes=list(carry.sems),
    )(*slots))


def _sibling_copies(x_refs, sib_refs, send_sems, recv_sems):
    x, y, c = _peers()
    return [pltpu.make_async_remote_copy(
        src_ref=x_refs[a].at[pl.ds(N_CHIP * (1 - c), N_CHIP)], dst_ref=sib_refs[a], send_sem=send_sems.at[a],
        recv_sem=recv_sems.at[a], device_id=(x, y, 1 - c), device_id_type=MESH) for a in range(len(x_refs))]


def _carry_sibling(slots):
    n = len(slots)
    return _Carry(tuple(slots), tuple(_S((N_CHIP,) + s.shape[1:], s.dtype) for s in slots),
                  (pltpu.SemaphoreType.DMA((n,)), pltpu.SemaphoreType.DMA((n,))),
                  lambda i, o, s: _start_all(_sibling_copies(i, o, *s)), lambda i, o, s: _wait_all(_sibling_copies(i, o, *s)))


def _chip_sum(slots, sib, core, *, name):
    _, R, W = slots.shape
    tr = _pick(R, max(16, EW_BLOCK_ELEMS // W), align=16)

    def body(core_ref, x_ref, s_ref, o_ref):
        o_ref[...] = (x_ref[...].astype(f32) + s_ref[...].astype(f32)).astype(o_ref.dtype)

    blk = pl.BlockSpec((None, tr, W), lambda t, i, core_ref: (t, i, 0))
    return pl.pallas_call(
        body, name=name, out_shape=_S(sib.shape, slots.dtype),
        grid_spec=pltpu.PrefetchScalarGridSpec(
            num_scalar_prefetch=1, grid=(N_CHIP, R // tr),
            in_specs=[pl.BlockSpec((None, tr, W), lambda t, i, core_ref: (N_CHIP * core_ref[0] + t, i, 0)), blk], out_specs=blk),
        compiler_params=_cparams(("parallel", "parallel")),
    )(core, slots, sib)


def _chip_out_shapes(parts):
    return [_S((N_CHIP - 1,) + p.shape[1:], p.dtype) for p in parts]


def _chip_sems(n):
    return [pltpu.SemaphoreType.DMA((3 * n,)), pltpu.SemaphoreType.DMA((3 * n,))]


def _chip_copies(p_refs, out_refs, send_sems, recv_sems):
    x, y, c = _peers()
    copies = []
    for j in range(1, N_CHIP):
        tx, ty = x ^ (j >> 1), y ^ (j & 1)
        for a in range(len(p_refs)):
            copies.append(pltpu.make_async_remote_copy(
                src_ref=p_refs[a].at[2 * tx + ty], dst_ref=out_refs[a].at[j - 1], send_sem=send_sems.at[3 * a + j - 1],
                recv_sem=recv_sems.at[3 * a + j - 1], device_id=(tx, ty, c), device_id_type=MESH))
    return copies


def _start_all(copies):
    for cp in copies:
        cp.start()


def _wait_all(copies):
    for cp in copies:
        cp.wait_recv()
    for cp in copies:
        cp.wait_send()


class _Carry(NamedTuple):
    ins: tuple = ()
    out_shapes: tuple = ()
    sems: tuple = ()
    start: Callable = None
    finish: Callable = None


_NO_CARRY = _Carry()


def _carry_chips(parts):
    return _Carry(tuple(parts), tuple(_chip_out_shapes(parts)), tuple(_chip_sems(len(parts))),
                  lambda i, o, s: _start_all(_chip_copies(i, o, *s)), lambda i, o, s: _wait_all(_chip_copies(i, o, *s)))


def _carry_gather(shards):
    n = len(shards)
    sems = (pltpu.SemaphoreType.DMA((7 * n,)), pltpu.SemaphoreType.DMA((7 * n,)), pltpu.SemaphoreType.DMA((n,)))
    return _Carry(tuple(shards), tuple(_S((N_DEV,) + s.shape, s.dtype) for s in shards), sems, _gather_start, _gather_finish)


def _carry_split(carry, refs, n_in, n_out):
    ci, co, cs = len(carry.ins), len(carry.out_shapes), len(carry.sems)
    refs = list(refs)
    own_in, c_in = refs[:n_in], refs[n_in:n_in + ci]
    own_out, c_out = refs[n_in + ci:n_in + ci + n_out], refs[n_in + ci + n_out:n_in + ci + n_out + co]
    rest = refs[n_in + ci + n_out + co:]
    own_scratch, c_sems = rest[:len(rest) - cs], rest[len(rest) - cs:]
    return own_in + own_out + own_scratch, c_in, c_out, c_sems


def _carry_start(carry, c_in, c_out, c_sems, first):
    if carry.ins:
        @pl.when(first)
        def _():
            carry.start(c_in, c_out, c_sems)


def _carry_finish(carry, c_in, c_out, c_sems, last):
    if carry.ins:
        @pl.when(last)
        def _():
            carry.finish(c_in, c_out, c_sems)


def _sum_slots(stack, *, name):
    n, R, W = stack.shape
    tr = _pick(R, 1024, align=8)

    def body(s_ref, o_ref):
        acc = s_ref[0]
        for k in range(1, n):
            acc = acc + s_ref[k]
        o_ref[...] = acc

    return pl.pallas_call(
        body, name=name, grid=(R // tr,), in_specs=[pl.BlockSpec((n, tr, W), lambda i: (0, i, 0))],
        out_specs=pl.BlockSpec((tr, W), lambda i: (i, 0)), out_shape=_S((R, W), f32), compiler_params=_cparams(("parallel",)),
    )(stack)


def _adamw_math(gv, wv, mv, vv):
    c1 = 1.0 / (1.0 - ADAM_B1 ** ADAM_STEP)
    c2 = 1.0 / (1.0 - ADAM_B2 ** ADAM_STEP)
    nm = ADAM_B1 * mv + (1.0 - ADAM_B1) * gv
    nv = ADAM_B2 * vv + (1.0 - ADAM_B2) * jnp.square(gv)
    return -ADAM_LR * ((nm * c1) / (jnp.sqrt(nv * c2) + ADAM_EPS) + ADAM_WD * wv), nm, nv


def _adamw(g, w, m, v, *, name):
    R, W = w.shape
    tr = _pick(R, max(8, (1 << 20) // (4 * W)), align=8)

    def body(g_ref, w_ref, m_ref, v_ref, d_ref, nm_ref, nv_ref):
        d_ref[...], nm_ref[...], nv_ref[...] = _adamw_math(g_ref[...], w_ref[...], m_ref[...], v_ref[...])

    sp = pl.BlockSpec((tr, W), lambda i: (i, 0))
    return pl.pallas_call(
        body, name=name, grid=(R // tr,), in_specs=[sp] * 4, out_specs=[sp] * 3, out_shape=[_S((R, W), f32)] * 3,
        compiler_params=_cparams(("parallel",)),
    )(g, w, m, v)


def _reduce_adamw(own, arrived, chip, w, m, v, *, name):
    n, R, W = arrived.shape
    tr = _pick(R, max(16, EW_BLOCK_ELEMS // (4 * W)), align=16)

    def body(chip_ref, o_ref, p_ref, w_ref, m_ref, v_ref, g_ref, d_ref, nm_ref, nv_ref):
        gv = o_ref[...].astype(f32)
        for k in range(n):
            gv = gv + p_ref[k].astype(f32)
        g_ref[...] = gv
        d_ref[...], nm_ref[...], nv_ref[...] = _adamw_math(gv, w_ref[...], m_ref[...], v_ref[...])

    sp = pl.BlockSpec((tr, W), lambda i, chip_ref: (i, 0))
    return pl.pallas_call(
        body, name=name, out_shape=[_S((R, W), f32)] * 4,
        grid_spec=pltpu.PrefetchScalarGridSpec(
            num_scalar_prefetch=1, grid=(R // tr,),
            in_specs=[pl.BlockSpec((None, tr, W), lambda i, chip_ref: (chip_ref[0], i, 0)),
                      pl.BlockSpec((n, tr, W), lambda i, chip_ref: (0, i, 0))] + [sp] * 3, out_specs=[sp] * 4),
        compiler_params=_cparams(("parallel",)),
    )(chip, own, arrived, w, m, v)


def _pieces(seg_start, seg_len, shard_w):
    out, col = [], seg_start
    while col < seg_start + seg_len:
        k, a = divmod(col, shard_w)
        n = min(shard_w - a, seg_start + seg_len - col)
        out.append((k, a, col - seg_start, n))
        col += n
    return out


def _unshard_w_in(g, seg_lens, *, name):
    _, D, w = g.shape
    starts = [sum(seg_lens[:i]) for i in range(len(seg_lens))]
    widths = [max(n, LANES) for n in seg_lens]
    tm = _pick(D, 512, align=16)

    def body(g_ref, *o_refs):
        for o_ref, s0, n in zip(o_refs, starts, seg_lens):
            if n < o_ref.shape[1]:
                o_ref[...] = jnp.zeros_like(o_ref)
            for k, a, off, m in _pieces(s0, n, w):
                o_ref[:, off:off + m] = g_ref[k, :, a:a + m]

    return pl.pallas_call(
        body, name=name, grid=(D // tm,), in_specs=[pl.BlockSpec((N_DEV, tm, w), lambda i: (0, i, 0))],
        out_specs=[pl.BlockSpec((tm, wd), lambda i: (i, 0)) for wd in widths], out_shape=[_S((D, wd), g.dtype) for wd in widths],
        compiler_params=_cparams(("parallel",)),
    )(g)


def _unshard_pair(g1, g2, *, name):
    _, D, w = g1.shape
    tm = _pick(D, 512, align=16)

    def body(a_ref, b_ref, o_ref):
        for i, g_ref in enumerate((a_ref, b_ref)):
            for k in range(N_DEV):
                off = (i * N_DEV + k) * w
                o_ref[:, off:off + w] = g_ref[k]

    blk = pl.BlockSpec((N_DEV, tm, w), lambda i: (0, i, 0))
    return pl.pallas_call(
        body, name=name, grid=(D // tm,), in_specs=[blk, blk], out_specs=pl.BlockSpec((tm, 2 * N_DEV * w), lambda i: (i, 0)),
        out_shape=_S((D, 2 * N_DEV * w), g1.dtype), compiler_params=_cparams(("parallel",)),
    )(g1, g2)


def _reshard_pair(dw, *, name):
    D, w = dw.shape[0], dw.shape[1] // (2 * N_DEV)
    tm = _pick(D, 256, align=16)

    def body(g_ref, a_ref, b_ref):
        for i, o_ref in enumerate((a_ref, b_ref)):
            for k in range(N_DEV):
                off = (i * N_DEV + k) * w
                o_ref[_slot_of(k)] = g_ref[:, off:off + w].astype(o_ref.dtype)

    blk = pl.BlockSpec((N_DEV, tm, w), lambda i: (0, i, 0))
    return pl.pallas_call(
        body, name=name, grid=(D // tm,), in_specs=[pl.BlockSpec((tm, dw.shape[1]), lambda i: (i, 0))], out_specs=[blk, blk],
        out_shape=[_S((N_DEV, D, w), bf16)] * 2, compiler_params=_cparams(("parallel",)),
    )(dw)


def _reshard_w_in(grads, seg_lens, w, *, name):
    D = grads[0].shape[0]
    starts = [sum(seg_lens[:i]) for i in range(len(seg_lens))]
    tm = _pick(D, 256, align=16)

    def body(*refs):
        o_ref = refs[-1]
        for g_ref, s0, n in zip(refs[:-1], starts, seg_lens):
            for k, a, off, m in _pieces(s0, n, w):
                o_ref[_slot_of(k), :, a:a + m] = g_ref[:, off:off + m].astype(o_ref.dtype)

    return pl.pallas_call(
        body, name=name, grid=(D // tm,), in_specs=[pl.BlockSpec((tm, g.shape[1]), lambda i: (i, 0)) for g in grads],
        out_specs=pl.BlockSpec((N_DEV, tm, w), lambda i: (0, i, 0)), out_shape=_S((N_DEV, D, w), bf16),
        compiler_params=_cparams(("parallel",)),
    )(*grads)


def _pad_flat(a, mult):
    a = a.reshape(-1)
    n = -(-a.shape[0] // mult) * mult
    return a if n == a.shape[0] else jnp.pad(a, (0, n - a.shape[0]))


def _pad_cols(a, mult):
    n = -(-a.shape[1] // mult) * mult
    return a if n == a.shape[1] else jnp.pad(a, ((0, 0), (0, n - a.shape[1])))


def _block_diag(t):
    nblk, g, P, Q = t.shape
    eye = jnp.eye(g, dtype=t.dtype)
    return (t[:, :, :, None, :] * eye[None, :, None, :, None]).reshape(nblk, g * P, g * Q)


def _block_diag_t(w, P, Q):
    nblk = w.shape[0]
    g = w.shape[1] // P
    eye = jnp.eye(g, dtype=w.dtype)
    return (w.reshape(nblk, g, P, g, Q) * eye[None, :, None, :, None]).sum(axis=3)


_COLS = ("ffn1_w_gate", "ffn1_w_up", "ffn2_w_gate", "ffn2_w_up")
_ROWS = ("ffn1_w_down", "ffn2_w_down", "s5_w_glu", "w_proj_s5", "w_out", "w_proj_ssd")
_BIG = _COLS + _ROWS + ("w_in", "conv_w")
_SMALL = ("ffn1_norm", "mix_norm", "conv_b", "s5_A_re", "s5_A_im", "s5_log_dt", "s5_B_re", "s5_B_im", "s5_C_re", "s5_C_im",
          "s5_D", "s5_b_glu", "ssd_A_log", "ssd_dt_bias", "ssd_D", "ssd_norm", "b_gate", "ffn2_norm", "final_norm")
_WEIGHTS = ("ffn1_norm", "ffn1_w_gate", "ffn1_w_up", "ffn1_w_down", "mix_norm", "w_in", "conv_w", "conv_b", "s5_A_re", "s5_A_im",
            "s5_log_dt", "s5_B_re", "s5_B_im", "s5_C_re", "s5_C_im", "s5_D", "s5_w_glu", "s5_b_glu", "ssd_A_log", "ssd_dt_bias",
            "ssd_D", "ssd_norm", "w_proj_s5", "w_proj_ssd", "b_gate", "w_out", "ffn2_norm", "ffn2_w_gate", "ffn2_w_up",
            "ffn2_w_down", "final_norm")
def _with_carry(res, carry):
    return res if carry else (res, [])


def _ffn_fwd(x, n, wgu, wd_of, tag, carries=(None, None, None), h=None):
    D = x.shape[1]
    if h is None:
        h = _rows(_f_rmsnorm, [x], [n], [(D, bf16)], name=tag + "_norm")[0]
    ab, got0 = _with_carry(_mm(h, wgu, carry=carries[0], name=tag + "_gate_up"), carries[0])
    wd = wd_of(got0)
    F = wd.shape[0]
    (c,), got1 = _with_carry(_rows(_f_swiglu, [ab], [], [(F, bf16)], carry=carries[1], name=tag + "_act"), carries[1])
    y, got2 = _with_carry(_mm(c, wd, scale=0.5, add=x, carry=carries[2], name=tag + "_down"), carries[2])
    return y, (x, n, h, ab, c), (got0, got1, got2)


def _ffn_bwd(saved, wgu, wd, dy, tag, carry_after_dwd=None, carry_after_dwgu=None):
    x, n, h, ab, c = saved
    F = wd.shape[0]
    def act_bwd(dc, ab_):
        return jax.vjp(lambda t: _f_swiglu(t)[0], ab_)[1](dc)[0]

    dab = _mm(dy, wd, tb=True, scale=0.5, tm=EPI_TM, epi=(act_bwd, ab, 2 * F), out_dtype=bf16, name=tag + "_d_act")
    dwd = _mm(c, dy, ta=True, o_blk="m", o_slots=True, tm=F // 2, out_dtype=bf16, scale=0.5, name=tag + "_d_wdown")
    carry_w = carry_after_dwd(dwd) if carry_after_dwd else None
    dwgu, arr_w = _with_carry(_mm(h, dab, ta=True, carry=carry_w, name=tag + "_d_wgu"), carry_w)
    dwg, dwu = _reshard_pair(dwgu, name=tag + "_reshard_d_wgu")
    carry_h = carry_after_dwgu(dwg, dwu) if carry_after_dwgu else None
    dh, arr_h = _with_carry(_mm(dab, wgu, tb=True, carry=carry_h, name=tag + "_d_h"), carry_h)
    (dx,), (dn,) = _rows_bwd(_f_rmsnorm, [x], [n], [dh], name=tag + "_norm_bwd", want_rows=[0], adds={0: dy})
    return dx, dn, dwg, dwu, dwd, arr_w, arr_h


def kernel(x, ffn1_norm, ffn1_w_gate, ffn1_w_up, ffn1_w_down, mix_norm, w_in, conv_w, conv_b, s5_A_re, s5_A_im, s5_log_dt, s5_B_re, s5_B_im, s5_C_re, s5_C_im, s5_D, s5_w_glu, s5_b_glu, ssd_A_log, ssd_dt_bias, ssd_D, ssd_norm, w_proj_s5, w_proj_ssd, b_gate, w_out, ffn2_norm, ffn2_w_gate, ffn2_w_up, ffn2_w_down, final_norm, loss_target, m_ffn1_norm, m_ffn1_w_gate, m_ffn1_w_up, m_ffn1_w_down, m_mix_norm, m_w_in, m_conv_w, m_conv_b, m_s5_A_re, m_s5_A_im, m_s5_log_dt, m_s5_B_re, m_s5_B_im, m_s5_C_re, m_s5_C_im, m_s5_D, m_s5_w_glu, m_s5_b_glu, m_ssd_A_log, m_ssd_dt_bias, m_ssd_D, m_ssd_norm, m_w_proj_s5, m_w_proj_ssd, m_b_gate, m_w_out, m_ffn2_norm, m_ffn2_w_gate, m_ffn2_w_up, m_ffn2_w_down, m_final_norm, v_ffn1_norm, v_ffn1_w_gate, v_ffn1_w_up, v_ffn1_w_down, v_mix_norm, v_w_in, v_conv_w, v_conv_b, v_s5_A_re, v_s5_A_im, v_s5_log_dt, v_s5_B_re, v_s5_B_im, v_s5_C_re, v_s5_C_im, v_s5_D, v_s5_w_glu, v_s5_b_glu, v_ssd_A_log, v_ssd_dt_bias, v_ssd_D, v_ssd_norm, v_w_proj_s5, v_w_proj_ssd, v_b_gate, v_w_out, v_ffn2_norm, v_ffn2_w_gate, v_ffn2_w_up, v_ffn2_w_down, v_final_norm):
    P = dict(locals())
    T, D = x.shape[1], x.shape[2]
    x0, tgt = x[0], loss_target[0]
    sh = {k: P[k][0] for k in _BIG}

    send = {k: (sh[k] if k == "conv_w" else sh[k].astype(bf16)) for k in _BIG}
    W = {}

    def gather_in(keys):
        return _carry_gather([send[k] for k in keys])

    first_keys = ("ffn1_w_gate", "ffn1_w_up", "conv_w")
    (h1,), got = _rows(_f_rmsnorm, [x0], [ffn1_norm], [(D, bf16)], carry=gather_in(first_keys), name="ffn1_norm")
    W.update(zip(first_keys, got))
    whole = lambda k: W[k].reshape(-1, D)
    conv_w_full = W["conv_w"].transpose(1, 0, 2).reshape(CONV_K, -1)

    d_inner = N_DEV * sh["w_proj_ssd"].shape[0]
    conv_dim = conv_w_full.shape[1]
    H = ssd_A_log.shape[1]
    G = (conv_dim - d_inner) // (2 * SSD_STATE)
    hpg = H // G
    nc = T // CHUNK
    Gs = D // S5_GROUP
    nblk = Gs // S5_GPB
    NS = Gs * S5_STATE
    seg_lens = (D, d_inner, conv_dim, H, 2 * D)

    cuts = [0, D // 3 // 16 * 16, D // 3 // 16 * 16 + 3 * D // 8 // 16 * 16, D]
    win_rows = [send["w_in"][a_:b_] for a_, b_ in zip(cuts[:-1], cuts[1:])]
    wgu1 = _unshard_pair(W["ffn1_w_gate"], W["ffn1_w_up"], name="unshard_ffn1_gate_up")

    def ffn1_down(got):
        W["ffn1_w_down"] = got[0]
        return whole("ffn1_w_down")

    x1, sv1, (got0, got1, got2) = _ffn_fwd(
        x0, ffn1_norm, wgu1, ffn1_down, "ffn1", h=h1,
        carries=(_carry_gather([send["ffn1_w_down"], win_rows[0]]), _carry_gather([win_rows[1]]), _carry_gather([win_rows[2]])))
    ffn1_w = (wgu1, whole("ffn1_w_down"))
    W["w_in"] = jnp.concatenate([got0[1], got1[0], got2[0]], axis=1)
    w_u, w_z, w_xbc, w_dt, w_gl = _unshard_w_in(W["w_in"], seg_lens, name="unshard_w_in")
    h2 = _rows(_f_rmsnorm, [x1], [mix_norm], [(D, bf16)], name="mix_norm")[0]
    u_p = _mm(h2, w_u, o_seg=True, name="in_u")
    z = _mm(h2, w_z, name="in_z")
    xbc = _mm(h2, w_xbc, name="in_xbc")
    gl = _mm(h2, w_gl, name="in_gate")
    dtr = _mm(h2, w_dt, name="in_dt")

    rep = lambda a: jnp.repeat(a, S5_GROUP, axis=0)
    lr, li, ldt = s5_A_re[0], s5_A_im[0], s5_log_dt[0].reshape(Gs, 1)
    brt = s5_B_re[0].transpose(0, 2, 1).reshape(Gs * S5_GROUP, S5_STATE)
    bit = s5_B_im[0].transpose(0, 2, 1).reshape(Gs * S5_GROUP, S5_STATE)
    prep_args = (lr, li, ldt, rep(lr), rep(li), rep(ldt), brt, bit)
    ar, ai, bbrt, bbit = _s5_prep(prep_args, name="s5_prep")
    a_r, a_i = ar.reshape(1, NS), ai.reshape(1, NS)
    wb_r = _block_diag(bbrt.reshape(nblk, S5_GPB, S5_GROUP, S5_STATE)).astype(bf16)
    wb_i = _block_diag(bbit.reshape(nblk, S5_GPB, S5_GROUP, S5_STATE)).astype(bf16)
    c4r = s5_C_re[0].reshape(nblk, S5_GPB, S5_GROUP, S5_STATE).transpose(0, 1, 3, 2)
    c4i = s5_C_im[0].reshape(nblk, S5_GPB, S5_GROUP, S5_STATE).transpose(0, 1, 3, 2)
    wc_r, wc_i = _block_diag(c4r).astype(bf16), _block_diag(c4i).astype(bf16)
    mix_keys = ("s5_w_glu", "w_proj_s5", "w_proj_ssd", "w_out")
    sl_r, sl_i, p_r, p_i, got = _s5_local_scan(u_p, wb_r, wb_i, a_r, a_i, reverse=False, carry=gather_in(mix_keys), name="s5_scan")
    W.update(zip(mix_keys, got))
    w_glu, w_p5, w_pssd, w_o = whole("s5_w_glu"), whole("w_proj_s5"), whole("w_proj_ssd"), whole("w_out")
    c_r, c_i = _s5_carry(sl_r[T - NSEG:], sl_i[T - NSEG:], p_r, p_i, reverse=False, name="s5_carry")
    s_r, s_i, ylin = _s5_fix_out(sl_r, sl_i, a_r, a_i, c_r, c_i, wc_r, wc_i, name="s5_fix_out")
    g5 = _rows(_f_s5_post, [ylin, u_p], [s5_D], [(D, f32)], name="s5_gelu")[0]
    v5 = _mm(g5, w_glu, name="s5_glu_mm")
    o5 = _rows(_f_glu, [g5, v5], [s5_b_glu], [(D, bf16)], name="s5_glu")[0]
    p5 = _mm(o5, w_p5, a_seg=True, name="proj_s5")

    xc = _conv_fwd(xbc, conv_w_full, conv_b, name="conv")
    bias_p, alog_p = _pad_cols(ssd_dt_bias, LANES), _pad_cols(ssd_A_log, LANES)
    expand = (lax.broadcasted_iota(jnp.int32, (LANES, d_inner), 1) // HEADDIM
              == lax.broadcasted_iota(jnp.int32, (LANES, d_inner), 0)).astype(f32)
    dt_p, da_p, dt4, a4 = _rows(_f_dt_expand, [dtr], [bias_p, alog_p, expand],
                                [(LANES, f32), (LANES, f32), (d_inner, f32), (d_inner, f32)], name="ssd_dt")
    row_l = lambda a: a[:, :H].reshape(nc, CHUNK, G, hpg).transpose(2, 0, 3, 1).reshape(G, nc, hpg * CHUNK)
    ssd_in = (xc, dt4, a4, row_l(dt_p), row_l(da_p), jnp.repeat(ssd_D, HEADDIM, axis=1), _ssd_consts(hpg))
    ffn2_keys = ("ffn2_w_gate", "ffn2_w_up", "ffn2_w_down")
    y_ssd, hs, got = _ssd2_fwd(*ssd_in, d_inner=d_inner, carry=gather_in(ffn2_keys), name="ssd")
    W.update(zip(ffn2_keys, got))
    ffn2_w = (_unshard_pair(W["ffn2_w_gate"], W["ffn2_w_up"], name="unshard_ffn2_gate_up"), whole("ffn2_w_down"))
    yn = _rows(_f_gated_norm, [y_ssd, z], [ssd_norm], [(d_inner, bf16)], name="ssd_gated_norm")[0]
    pssd = _mm(yn, w_pssd, name="proj_ssd")

    merged = _rows(_f_merge, [gl, p5, pssd], [b_gate], [(D, bf16)], name="merge")[0]
    x2 = _mm(merged, w_o, add=x1, name="out_proj")
    x3, sv2, _ = _ffn_fwd(x2, ffn2_norm, ffn2_w[0], lambda _: ffn2_w[1], "ffn2")
    lossv, dx3, d_final = _loss_stage(x3, tgt, final_norm.reshape(1, D), name="loss")

    gw = {}
    gs = {"final_norm": d_final}
    slot_mm = lambda a_, b_, name, **kw: _mm(a_, b_, ta=True, o_blk="m", o_slots=True, out_dtype=bf16, name=name, **kw)
    core = lax.axis_index("c").astype(jnp.int32).reshape(1)
    chip = (2 * lax.axis_index("x") + lax.axis_index("y")).astype(jnp.int32).reshape(1)
    chip_sums, arrived = {}, {}

    def sums_of(keys, sib):
        for k, s_ in zip(keys, sib):
            chip_sums[k] = _chip_sum(gw[k], s_, core, name="chip_sum_" + k)
        return [chip_sums[k] for k in keys]

    def level1(keys, tag):
        return sums_of(keys, _exchange_sibling([gw[k] for k in keys], name="exchange_sibling_" + tag))

    dx2, gs["ffn2_norm"], gw["ffn2_w_gate"], gw["ffn2_w_up"], gw["ffn2_w_down"], _, _ = _ffn_bwd(sv2, *ffn2_w, dx3, "ffn2")

    dmerged = _mm(dx2, w_o, tb=True, name="d_merged")
    gw["w_out"] = slot_mm(merged, dx2, "d_w_out")
    (dgl, dp5, dpssd), (gs["b_gate"],) = _rows_bwd(_f_merge, [gl, p5, pssd], [b_gate], [dmerged], name="merge_bwd", want_rows=[0, 1, 2])

    dyn = _mm(dpssd, w_pssd, tb=True, name="d_yn")
    gw["w_proj_ssd"] = slot_mm(yn, dpssd, "d_w_proj_ssd")
    group_a = ("ffn2_w_gate", "ffn2_w_up", "ffn2_w_down", "w_out", "w_proj_ssd")
    (dyssd, dz), (gs["ssd_norm"],) = _rows_bwd(_f_gated_norm, [y_ssd, z], [ssd_norm], [dyn], name="ssd_gated_norm_bwd", want_rows=[0, 1])
    dxs, dbm, dcm, ddtc, ddac, ddtw, ddaw, ddh, sib_a = _ssd2_bwd(
        *ssd_in, hs, dyssd, d_inner=d_inner, carry=_carry_sibling([gw[k] for k in group_a]), name="ssd_bwd")
    parts_a = sums_of(group_a, sib_a)

    def fold(col, row):
        col = col.reshape(T, G, LANES)[:, :, :hpg].reshape(T, H)
        row = row.reshape(G, nc, hpg, CHUNK).transpose(1, 3, 0, 2).reshape(T, H)
        return _pad_cols(col + row, LANES)

    (ddtr,), (dbias_p, dalog_p) = _rows_bwd(_f_dt, [dtr], [bias_p, alog_p], [fold(ddtc, ddtw), fold(ddac, ddaw)], name="ssd_dt_bwd", want_rows=[0])
    gs["ssd_dt_bias"], gs["ssd_A_log"], gs["ssd_D"] = dbias_p[:, :H], dalog_p[:, :H], ddh[:, 0, :hpg].reshape(1, H)
    dxbc, d_conv_w, gs["conv_b"], arr = _conv_bwd(
        xbc, conv_w_full, conv_b, [dxs, dbm, dcm], carry=_carry_chips(parts_a), name="conv_bwd")
    arrived.update(zip(group_a, arr))
    cwk = sh["conv_w"].shape[1]
    gw["conv_w"] = d_conv_w.reshape(CONV_K, N_CHIP, 2, cwk).transpose(2, 1, 0, 3).reshape(N_DEV, CONV_K, cwk)

    do5 = _mm(dp5, w_p5, tb=True, o_seg=True, name="d_o5")
    gw["w_proj_s5"] = slot_mm(o5, dp5, "d_w_proj_s5", a_seg=True)
    (dg5a, dv5), (gs["s5_b_glu"],) = _rows_bwd(_f_glu, [g5, v5], [s5_b_glu], [do5], name="s5_glu_bwd", want_rows=[0, 1])
    dg5 = _mm(dv5, w_glu, tb=True, add=dg5a, name="d_g5")
    gw["s5_w_glu"] = slot_mm(g5, dv5, "d_w_glu")
    (dylin, du_a), (gs["s5_D"],) = _rows_bwd(_f_s5_post, [ylin, u_p], [s5_D], [dg5], name="s5_gelu_bwd", want_rows=[0, 1])
    wct_r, wct_i = wc_r.transpose(0, 2, 1), -wc_i.transpose(0, 2, 1)
    ql_r, ql_i, _, _ = _s5_local_scan(dylin, wct_r, wct_i, a_r, -a_i, reverse=True, powers=False, name="s5_scan_bwd")
    cb_r, cb_i = _s5_carry(ql_r[:NSEG], ql_i[:NSEG], p_r, -p_i, reverse=True, name="s5_carry_bwd")
    tc = min(S5_TC, T)

    def before_blocks(s):
        last = s.reshape(T // tc, tc, NS)[:, tc - NSEG:, :]
        wrap = jnp.concatenate([jnp.zeros((1, 1, NS), f32), last[-1:, : NSEG - 1, :]], axis=1)
        return jnp.concatenate([wrap, last[:-1]], axis=0)

    du_p, dwb_r, dwb_i, dwc_r, dwc_i, d_ar, d_ai = _s5_fix_bwd(
        ql_r, ql_i, a_r, -a_i, cb_r, cb_i, s_r, s_i, before_blocks(s_r), before_blocks(s_i), u_p, dylin, du_a, wb_r, wb_i, name="s5_fix_bwd")
    unblk = lambda w: _block_diag_t(w, S5_GROUP, S5_STATE).reshape(Gs * S5_GROUP, S5_STATE)
    rsum = jnp.repeat(jnp.eye(Gs, dtype=f32), S5_GROUP, axis=1)
    d_lr, d_li, d_ldt, d_brt, d_bit = _s5_prep_bwd(
        prep_args, (d_ar.reshape(Gs, S5_STATE), d_ai.reshape(Gs, S5_STATE), unblk(dwb_r), unblk(dwb_i)), rsum, name="s5_prep_bwd")
    gs["s5_A_re"], gs["s5_A_im"], gs["s5_log_dt"] = d_lr, d_li, d_ldt.reshape(1, Gs)
    gs["s5_B_re"] = d_brt.reshape(Gs, S5_GROUP, S5_STATE).transpose(0, 2, 1)
    gs["s5_B_im"] = d_bit.reshape(Gs, S5_GROUP, S5_STATE).transpose(0, 2, 1)
    gs["s5_C_re"] = _block_diag_t(dwc_r, S5_STATE, S5_GROUP).transpose(0, 1, 3, 2).reshape(Gs, S5_GROUP, S5_STATE)
    gs["s5_C_im"] = _block_diag_t(dwc_i, S5_STATE, S5_GROUP).transpose(0, 1, 3, 2).reshape(Gs, S5_GROUP, S5_STATE)

    d_w_in = [_mm(h2, du_p, ta=True, b_seg=True, name="d_w_u"), _mm(h2, dz, ta=True, name="d_w_z"), _mm(h2, dxbc, ta=True, name="d_w_xbc"),
              _mm(h2, ddtr, ta=True, name="d_w_dt"), _mm(h2, dgl, ta=True, name="d_w_gate")]
    gw["w_in"] = _reshard_w_in(d_w_in, seg_lens, sh["w_in"].shape[1], name="reshard_d_w_in")
    group_c = ("w_proj_s5", "s5_w_glu", "conv_w")
    keys_c = group_c + ("w_in",)
    dh2, sib_c = _mm(du_p, w_u, tb=True, a_seg=True, carry=_carry_sibling([gw[k] for k in keys_c]), name="d_h2_u")
    parts_c = sums_of(keys_c, sib_c)
    c1 = int(D * 0.45) // 16 * 16
    c2 = c1 + D // 4 // 16 * 16
    win = [parts_c[3][:, :c1], parts_c[3][:, c1:c2], parts_c[3][:, c2:]]
    dh2, arr = _mm(dz, w_z, tb=True, add=dh2, carry=_carry_chips(parts_c[:3]), name="d_h2_z")
    arrived.update(zip(group_c, arr))
    dh2, (arr0,) = _mm(dxbc, w_xbc, tb=True, add=dh2, carry=_carry_chips([win[0]]), name="d_h2_xbc")
    dh2, (arr1,) = _mm(dgl, w_gl, tb=True, add=dh2, carry=_carry_chips([win[1]]), name="d_h2_gate")
    dh2 = _mm(ddtr, w_dt, tb=True, add=dh2, name="d_h2_dt")
    (dx1,), (gs["mix_norm"],) = _rows_bwd(_f_rmsnorm, [x1], [mix_norm], [dh2], name="mix_norm_bwd", want_rows=[0], adds={0: dx2})

    def carry_ffn1_down(dwd):
        gw["ffn1_w_down"] = dwd
        return _carry_chips(level1(("ffn1_w_down",), "d") + [win[2]])

    def carry_ffn1_gate_up(dwg, dwu):
        gw["ffn1_w_gate"], gw["ffn1_w_up"] = dwg, dwu
        return _carry_chips(level1(("ffn1_w_gate", "ffn1_w_up"), "e"))

    dx0, gs["ffn1_norm"], _, _, _, arr_w, arr_h = _ffn_bwd(
        sv1, *ffn1_w, dx1, "ffn1", carry_after_dwd=carry_ffn1_down, carry_after_dwgu=carry_ffn1_gate_up)
    arrived["ffn1_w_down"], arr2 = arr_w
    arrived["ffn1_w_gate"], arrived["ffn1_w_up"] = arr_h
    arrived["w_in"] = jnp.concatenate([arr0, arr1, arr2], axis=1)

    small_shapes = {k: (P[k][0].shape if P[k].ndim > 1 else P[k].shape) for k in _SMALL}
    pack = lambda d: jnp.concatenate([_pad_flat(d[k], TILE_ELEMS) for k in _SMALL]).reshape(-1, LANES)
    gsmall = _sum_slots(_all_gather([pack(gs)], name="gather_small_grads")[0], name="sum_small_grads")
    snum = {k: math.prod(small_shapes[k]) for k in _SMALL}
    ssz = {k: -(-snum[k] // TILE_ELEMS) * TILE_ELEMS for k in _SMALL}

    grads, delta, new_m, new_v = {}, {}, {}, {}
    for k in _BIG:
        grads[k], delta[k], new_m[k], new_v[k] = _reduce_adamw(
            chip_sums[k], arrived[k], chip, P[k][0], P["m_" + k][0], P["v_" + k][0], name="adamw_" + k)
    d_s, m_s, v_s = _adamw(gsmall, pack({k: P[k] for k in _SMALL}), pack({k: P["m_" + k] for k in _SMALL}),
                           pack({k: P["v_" + k] for k in _SMALL}), name="adamw_small")
    off = 0
    gflat, dflat, mflat, vflat = gsmall.reshape(-1), d_s.reshape(-1), m_s.reshape(-1), v_s.reshape(-1)
    for k in _SMALL:
        n = snum[k]
        grads[k], delta[k], new_m[k], new_v[k] = (a[off:off + n] for a in (gflat, dflat, mflat, vflat))
        off += ssz[k]

    loss = lax.psum(lossv[0, 0], ("x", "y", "c"))
    out = [loss, dx0.reshape(x.shape)]
    for d in (grads, delta, new_m, new_v):
        out += [d[k].reshape(P[k].shape) for k in _WEIGHTS]
    return tuple(out)
```
